```python
import math
import jax, jax.numpy as jnp
from jax import lax
import numpy as np

D_MODEL = 1024
BATCH = 8
SEQ = 2048
DEPTH = 1

MEM_LEN = 256
DN_HEADS = 8
DN_DK = 128
DN_DV = 128
DN_CHUNK = 64
CONV_K = 4
SB_HEADS = 8
SB_DH = 128
SB_BLOCK = 128
MEM_HEADS = 4
MEM_DH = 64
N_BRANCH = 3
NORM_EPS = 1e-6

DN_QK = DN_HEADS * DN_DK
DN_VW = DN_HEADS * DN_DV
DN_QKV_W = 2 * DN_QK + DN_VW
SB_W = SB_HEADS * SB_DH
MEM_W = MEM_HEADS * MEM_DH
IN_SIZES = (DN_QKV_W, DN_VW, DN_HEADS, DN_HEADS, 3 * SB_W, SB_W, MEM_W, MEM_W, N_BRANCH * D_MODEL)
IN_WIDTH = sum(IN_SIZES)

kernel_name = "hybrid_deltanet_stickbreak_memory_block"


def rmsnorm(x, g):
    xf = x.astype(jnp.float32)
    y = xf * lax.rsqrt(jnp.mean(xf * xf, axis=-1, keepdims=True) + NORM_EPS)
    return (y * g.astype(jnp.float32)).astype(x.dtype)


def l2norm(x):
    return x * lax.rsqrt(jnp.sum(x * x, axis=-1, keepdims=True) + NORM_EPS)


def to_heads(t, n_heads):
    b, s, _ = t.shape
    return t.reshape(b, s, n_heads, -1).transpose(0, 2, 1, 3)


def merge_heads(t):
    b, h, s, d = t.shape
    return t.transpose(0, 2, 1, 3).reshape(b, s, h * d)


def causal_dwconv(x, w):
    k = w.shape[0]
    t = x.shape[1]
    xp = jnp.pad(x, ((0, 0), (k - 1, 0), (0, 0)))
    return sum(xp[:, j:j + t] * w[j] for j in range(k))


def gated_delta_rule(q, k, v, beta, g):
    b, h, t, dk = q.shape
    dv = v.shape[-1]
    c = DN_CHUNK
    n = t // c
    q = q.reshape(b, h, n, c, dk)
    k = k.reshape(b, h, n, c, dk)
    v = v.reshape(b, h, n, c, dv)
    beta = beta.reshape(b, h, n, c)
    G = jnp.cumsum(g.reshape(b, h, n, c), axis=-1)
    idx = jnp.arange(c)
    incl = idx[:, None] >= idx[None, :]
    strict = idx[:, None] > idx[None, :]
    diff = G[..., :, None] - G[..., None, :]
    gam_incl = jnp.exp(jnp.where(incl, diff, -jnp.inf))
    gam_strict = jnp.where(strict, gam_incl, 0.0)
    kk = jnp.einsum('bhncd,bhnsd->bhncs', k, k)
    m = beta[..., :, None] * kk * gam_strict
    eye = jnp.eye(c, dtype=jnp.float32)
    t_inv = lax.linalg.triangular_solve(eye + m, jnp.broadcast_to(eye, m.shape),
                                        left_side=True, lower=True, unit_diagonal=True)
    u = jnp.einsum('bhncs,bhnsd->bhncd', t_inv, v * beta[..., None])
    w = jnp.einsum('bhncs,bhnsd->bhncd', t_inv, k * (beta * jnp.exp(G))[..., None])
    a_intra = jnp.einsum('bhncd,bhnsd->bhncs', q, k) * gam_incl
    q_dec = q * jnp.exp(G)[..., None]
    last = G[..., -1]
    k_dec = k * jnp.exp(last[..., None] - G)[..., None]

    def step(s, xs):
        q_n, w_n, u_n, k_n, a_n, last_n = xs
        v_new = u_n - jnp.einsum('bhcd,bhde->bhce', w_n, s)
        o = jnp.einsum('bhcd,bhde->bhce', q_n, s) + jnp.einsum('bhcs,bhse->bhce', a_n, v_new)
        s = s * jnp.exp(last_n)[..., None, None] + jnp.einsum('bhcd,bhce->bhde', k_n, v_new)
        return s, o

    xs = (jnp.moveaxis(q_dec, 2, 0), jnp.moveaxis(w, 2, 0), jnp.moveaxis(u, 2, 0),
          jnp.moveaxis(k_dec, 2, 0), jnp.moveaxis(a_intra, 2, 0), jnp.moveaxis(last, 2, 0))
    s0 = jnp.zeros((b, h, dk, dv), jnp.float32)
    _, o = lax.scan(step, s0, xs)
    return jnp.moveaxis(o, 0, 2).reshape(b, h, t, dv)


def stick_breaking_attention(q, k, v):
    _, _, t, d = q.shape
    scale = 1.0 / math.sqrt(d)
    outs = []
    for i in range(t // SB_BLOCK):
        t0 = i * SB_BLOCK
        kl = t0 + SB_BLOCK
        z = jnp.einsum('bhtd,bhsd->bhts', q[:, :, t0:kl], k[:, :, :kl]).astype(jnp.float32) * scale
        t_pos = t0 + jnp.arange(SB_BLOCK)
        s_pos = jnp.arange(kl)
        causal = s_pos[None, :] < t_pos[:, None]
        log_beta = jax.nn.log_sigmoid(z)
        log_fail = jnp.where(causal, jax.nn.log_sigmoid(-z), 0.0)
        surv = lax.cumsum(log_fail, axis=3, reverse=True) - log_fail
        att = jnp.where(causal, jnp.exp(log_beta + surv), 0.0)
        outs.append(jnp.einsum('bhts,bhsd->bhtd', att.astype(v.dtype), v[:, :, :kl]))
    return jnp.concatenate(outs, axis=2)


def memory_attention(q, mk, mv):
    s = jnp.einsum('bhtd,bhmd->bhtm', q, mk).astype(jnp.float32) * (1.0 / math.sqrt(q.shape[-1]))
    p = jax.nn.softmax(s, axis=-1)
    return jnp.einsum('bhtm,bhmd->bhtd', p.astype(mv.dtype), mv)


def hybrid_layer(x, mem, norm_g, mem_norm_g, w_in, conv_w, a_log, dt_bias, dn_norm_g,
                 w_mem_kv, w_br_dn, w_br_sb, w_br_mem, w_out):
    h = rmsnorm(x, norm_g)
    proj = h @ w_in
    splits = [int(s) for s in np.cumsum(IN_SIZES)[:-1]]
    dn_qkv, dn_z, dn_b, dn_a, sb_qkv, sb_z, m_q, m_z, gates = jnp.split(proj, splits, axis=-1)

    dn_qkv = jax.nn.silu(causal_dwconv(dn_qkv, conv_w))
    dq, dk, dv = jnp.split(dn_qkv, [DN_QK, 2 * DN_QK], axis=-1)
    dq = l2norm(to_heads(dq, DN_HEADS).astype(jnp.float32)) * (DN_DK ** -0.5)
    dk = l2norm(to_heads(dk, DN_HEADS).astype(jnp.float32))
    dv = to_heads(dv, DN_HEADS).astype(jnp.float32)
    beta = jax.nn.sigmoid(dn_b.astype(jnp.float32)).transpose(0, 2, 1)
    g = -(jnp.exp(a_log.astype(jnp.float32))
          * jax.nn.softplus(dn_a.astype(jnp.float32) + dt_bias.astype(jnp.float32))).transpose(0, 2, 1)
    o_dn = gated_delta_rule(dq, dk, dv, beta, g)
    o_dn = merge_heads(rmsnorm(o_dn, dn_norm_g)).astype(x.dtype) * jax.nn.silu(dn_z)

    sq, sk, sv = jnp.split(sb_qkv, 3, axis=-1)
    o_sb = stick_breaking_attention(to_heads(sq, SB_HEADS), to_heads(sk, SB_HEADS), to_heads(sv, SB_HEADS))
    o_sb = merge_heads(o_sb) * jax.nn.silu(sb_z)

    mkv = rmsnorm(mem, mem_norm_g) @ w_mem_kv
    mk, mv = jnp.split(mkv, 2, axis=-1)
    o_m = memory_attention(to_heads(m_q, MEM_HEADS), to_heads(mk, MEM_HEADS), to_heads(mv, MEM_HEADS))
    o_m = merge_heads(o_m) * jax.nn.silu(m_z)

    g_dn, g_sb, g_m = jnp.split(jax.nn.sigmoid(gates), N_BRANCH, axis=-1)
    merged = g_dn * (o_dn @ w_br_dn) + g_sb * (o_sb @ w_br_sb) + g_m * (o_m @ w_br_mem)
    return x + merged @ w_out


def _fwd_setup_inputs(seed: int = 0) -> dict:
    key = jax.random.key(seed)
    ks = jax.random.split(key, 16)
    f = jnp.float32

    def dense(k, shape, fan_in):
        return jax.random.normal(k, shape, f) * (fan_in ** -0.5)

    def gain(k, shape):
        return 1.0 + 0.02 * jax.random.normal(k, shape, f)

    x = jax.random.normal(ks[0], (BATCH, SEQ, D_MODEL), f)
    mem = jax.random.normal(ks[1], (BATCH, MEM_LEN, D_MODEL), f)
    norm_g = gain(ks[2], (DEPTH, D_MODEL))
    mem_norm_g = gain(ks[3], (DEPTH, D_MODEL))
    w_in = dense(ks[4], (DEPTH, D_MODEL, IN_WIDTH), D_MODEL)
    conv_w = dense(ks[5], (DEPTH, CONV_K, DN_QKV_W), CONV_K)
    a_log = jnp.log(jax.random.uniform(ks[6], (DEPTH, DN_HEADS), f, 1.0, 16.0))
    dt = jnp.exp(jax.random.uniform(ks[7], (DEPTH, DN_HEADS), f, math.log(1e-3), math.log(1e-1)))
    dt_bias = dt + jnp.log(-jnp.expm1(-dt))
    dn_norm_g = gain(ks[8], (DEPTH, DN_DV))
    w_mem_kv = dense(ks[9], (DEPTH, D_MODEL, 2 * MEM_W), D_MODEL)
    w_br_dn = dense(ks[10], (DEPTH, DN_VW, D_MODEL), DN_VW)
    w_br_sb = dense(ks[11], (DEPTH, SB_W, D_MODEL), SB_W)
    w_br_mem = dense(ks[12], (DEPTH, MEM_W, D_MODEL), MEM_W)
    w_out = dense(ks[13], (DEPTH, D_MODEL, D_MODEL), D_MODEL)
    final_g = gain(ks[14], (D_MODEL,))
    return {"x": x, "mem": mem, "norm_g": norm_g, "mem_norm_g": mem_norm_g, "w_in": w_in,
            "conv_w": conv_w, "a_log": a_log, "dt_bias": dt_bias, "dn_norm_g": dn_norm_g,
            "w_mem_kv": w_mem_kv, "w_br_dn": w_br_dn, "w_br_sb": w_br_sb, "w_br_mem": w_br_mem,
            "w_out": w_out, "final_g": final_g}


def _fwd_reference(x, mem, norm_g, mem_norm_g, w_in, conv_w, a_log, dt_bias, dn_norm_g,
              w_mem_kv, w_br_dn, w_br_sb, w_br_mem, w_out, final_g):
    for l in range(DEPTH):
        x = hybrid_layer(x, mem, norm_g[l], mem_norm_g[l], w_in[l], conv_w[l], a_log[l], dt_bias[l],
                         dn_norm_g[l], w_mem_kv[l], w_br_dn[l], w_br_sb[l], w_br_mem[l], w_out[l])
    return rmsnorm(x, final_g)


import jax as _jax
import jax.numpy as _jnp

TWIN_FORMAT = 'train_step'
FWD_PARAMS = ['x', 'mem', 'norm_g', 'mem_norm_g', 'w_in', 'conv_w', 'a_log', 'dt_bias', 'dn_norm_g', 'w_mem_kv', 'w_br_dn', 'w_br_sb', 'w_br_mem', 'w_out', 'final_g']
TWIN_WEIGHTS = ['norm_g', 'mem_norm_g', 'w_in', 'conv_w', 'a_log', 'dt_bias', 'dn_norm_g', 'w_mem_kv', 'w_br_dn', 'w_br_sb', 'w_br_mem', 'w_out', 'final_g']
TWIN_DIFF_INPUT = 'x'
TWIN_INPUTS = ['x', 'mem', 'norm_g', 'mem_norm_g', 'w_in', 'conv_w', 'a_log', 'dt_bias', 'dn_norm_g', 'w_mem_kv', 'w_br_dn', 'w_br_sb', 'w_br_mem', 'w_out', 'final_g', 'loss_target', 'm_norm_g', 'm_mem_norm_g', 'm_w_in', 'm_conv_w', 'm_a_log', 'm_dt_bias', 'm_dn_norm_g', 'm_w_mem_kv', 'm_w_br_dn', 'm_w_br_sb', 'm_w_br_mem', 'm_w_out', 'm_final_g', 'v_norm_g', 'v_mem_norm_g', 'v_w_in', 'v_conv_w', 'v_a_log', 'v_dt_bias', 'v_dn_norm_g', 'v_w_mem_kv', 'v_w_br_dn', 'v_w_br_sb', 'v_w_br_mem', 'v_w_out', 'v_final_g']
TWIN_OUTPUTS = ['loss', 'grad_x', 'grad_norm_g', 'grad_mem_norm_g', 'grad_w_in', 'grad_conv_w', 'grad_a_log', 'grad_dt_bias', 'grad_dn_norm_g', 'grad_w_mem_kv', 'grad_w_br_dn', 'grad_w_br_sb', 'grad_w_br_mem', 'grad_w_out', 'grad_final_g', 'delta_norm_g', 'delta_mem_norm_g', 'delta_w_in', 'delta_conv_w', 'delta_a_log', 'delta_dt_bias', 'delta_dn_norm_g', 'delta_w_mem_kv', 'delta_w_br_dn', 'delta_w_br_sb', 'delta_w_br_mem', 'delta_w_out', 'delta_final_g', 'new_m_norm_g', 'new_m_mem_norm_g', 'new_m_w_in', 'new_m_conv_w', 'new_m_a_log', 'new_m_dt_bias', 'new_m_dn_norm_g', 'new_m_w_mem_kv', 'new_m_w_br_dn', 'new_m_w_br_sb', 'new_m_w_br_mem', 'new_m_w_out', 'new_m_final_g', 'new_v_norm_g', 'new_v_mem_norm_g', 'new_v_w_in', 'new_v_conv_w', 'new_v_a_log', 'new_v_dt_bias', 'new_v_dn_norm_g', 'new_v_w_mem_kv', 'new_v_w_br_dn', 'new_v_w_br_sb', 'new_v_w_br_mem', 'new_v_w_out', 'new_v_final_g']
TWIN_LEAF_KINDS = {'loss': 'loss', 'grad_x': 'grad_x', 'grad_norm_g': 'grad_w', 'grad_mem_norm_g': 'grad_w', 'grad_w_in': 'grad_w', 'grad_conv_w': 'grad_w', 'grad_a_log': 'grad_w', 'grad_dt_bias': 'grad_w', 'grad_dn_norm_g': 'grad_w', 'grad_w_mem_kv': 'grad_w', 'grad_w_br_dn': 'grad_w', 'grad_w_br_sb': 'grad_w', 'grad_w_br_mem': 'grad_w', 'grad_w_out': 'grad_w', 'grad_final_g': 'grad_w', 'delta_norm_g': 'delta_w', 'delta_mem_norm_g': 'delta_w', 'delta_w_in': 'delta_w', 'delta_conv_w': 'delta_w', 'delta_a_log': 'delta_w', 'delta_dt_bias': 'delta_w', 'delta_dn_norm_g': 'delta_w', 'delta_w_mem_kv': 'delta_w', 'delta_w_br_dn': 'delta_w', 'delta_w_br_sb': 'delta_w', 'delta_w_br_mem': 'delta_w', 'delta_w_out': 'delta_w', 'delta_final_g': 'delta_w', 'new_m_norm_g': 'new_m', 'new_m_mem_norm_g': 'new_m', 'new_m_w_in': 'new_m', 'new_m_conv_w': 'new_m', 'new_m_a_log': 'new_m', 'new_m_dt_bias': 'new_m', 'new_m_dn_norm_g': 'new_m', 'new_m_w_mem_kv': 'new_m', 'new_m_w_br_dn': 'new_m', 'new_m_w_br_sb': 'new_m', 'new_m_w_br_mem': 'new_m', 'new_m_w_out': 'new_m', 'new_m_final_g': 'new_m', 'new_v_norm_g': 'new_v', 'new_v_mem_norm_g': 'new_v', 'new_v_w_in': 'new_v', 'new_v_conv_w': 'new_v', 'new_v_a_log': 'new_v', 'new_v_dt_bias': 'new_v', 'new_v_dn_norm_g': 'new_v', 'new_v_w_mem_kv': 'new_v', 'new_v_w_br_dn': 'new_v', 'new_v_w_br_sb': 'new_v', 'new_v_w_br_mem': 'new_v', 'new_v_w_out': 'new_v', 'new_v_final_g': 'new_v'}


def _forward(args):
    return _fwd_reference(*[args[k] for k in FWD_PARAMS])


def _output_shape():
    out = _jax.eval_shape(lambda: _forward(_fwd_setup_inputs(0)))
    return out.shape, out.dtype

N_MICROBATCH = 1
ADAM_LR = 0.001
ADAM_B1 = 0.9
ADAM_B2 = 0.999
ADAM_EPS = 1e-08
ADAM_WD = 0.01
ADAM_STEP = 10
PER_EXAMPLE_BATCH_AXIS = {'x': 0, 'mem': 0, 'loss_target': 0}
SHARED_INPUTS = []
_WEIGHT_DTYPES = {'norm_g': _jnp.float32, 'mem_norm_g': _jnp.float32, 'w_in': _jnp.float32, 'conv_w': _jnp.float32, 'a_log': _jnp.float32, 'dt_bias': _jnp.float32, 'dn_norm_g': _jnp.float32, 'w_mem_kv': _jnp.float32, 'w_br_dn': _jnp.float32, 'w_br_sb': _jnp.float32, 'w_br_mem': _jnp.float32, 'w_out': _jnp.float32, 'final_g': _jnp.float32}
MOMENT_SCALE = {'norm_g': 7.419583e-02, 'mem_norm_g': 6.159065e-03, 'w_in': 2.221072e-02, 'conv_w': 2.954516e-02, 'a_log': 1.291108e-01, 'dt_bias': 1.284108e-01, 'dn_norm_g': 1.126050e-01, 'w_mem_kv': 8.128106e-03, 'w_br_dn': 3.762739e-02, 'w_br_sb': 2.434285e-02, 'w_br_mem': 4.066493e-03, 'w_out': 4.491345e-02, 'final_g': 1.599961e+01}


def _to_microbatches(a, axis):
    t = _jnp.moveaxis(a, axis, 0)
    t = t.reshape((N_MICROBATCH, t.shape[0] // N_MICROBATCH) + t.shape[1:])
    return _jnp.moveaxis(t, 1, axis + 1)


def setup_inputs(seed: int = 0) -> dict:
    inp = _fwd_setup_inputs(seed)
    key = _jax.random.fold_in(_jax.random.key(seed), 7919)
    shape, _ = _output_shape()
    out = dict(inp)
    out["loss_target"] = _jax.random.normal(_jax.random.fold_in(key, 0), shape, _jnp.float32)
    for i, name in enumerate(TWIN_WEIGHTS):
        w = inp[name].astype(_jnp.float32)
        if MOMENT_SCALE is None:
            s = _jnp.sqrt(_jnp.mean(_jnp.square(w)) + 1e-30)
        else:
            s = MOMENT_SCALE[name]
        km, kv = _jax.random.split(_jax.random.fold_in(key, i + 1))
        out[name] = w
        out["m_" + name] = s * _jax.random.normal(km, w.shape, _jnp.float32)
        out["v_" + name] = (s * s) * _jax.random.uniform(kv, w.shape, _jnp.float32, 0.5, 1.5)
    if N_MICROBATCH > 1:
        for name, axis in PER_EXAMPLE_BATCH_AXIS.items():
            out[name] = _to_microbatches(out[name], axis)
    return {'x': out['x'], 'mem': out['mem'], 'norm_g': out['norm_g'], 'mem_norm_g': out['mem_norm_g'], 'w_in': out['w_in'], 'conv_w': out['conv_w'], 'a_log': out['a_log'], 'dt_bias': out['dt_bias'], 'dn_norm_g': out['dn_norm_g'], 'w_mem_kv': out['w_mem_kv'], 'w_br_dn': out['w_br_dn'], 'w_br_sb': out['w_br_sb'], 'w_br_mem': out['w_br_mem'], 'w_out': out['w_out'], 'final_g': out['final_g'], 'loss_target': out['loss_target'], 'm_norm_g': out['m_norm_g'], 'm_mem_norm_g': out['m_mem_norm_g'], 'm_w_in': out['m_w_in'], 'm_conv_w': out['m_conv_w'], 'm_a_log': out['m_a_log'], 'm_dt_bias': out['m_dt_bias'], 'm_dn_norm_g': out['m_dn_norm_g'], 'm_w_mem_kv': out['m_w_mem_kv'], 'm_w_br_dn': out['m_w_br_dn'], 'm_w_br_sb': out['m_w_br_sb'], 'm_w_br_mem': out['m_w_br_mem'], 'm_w_out': out['m_w_out'], 'm_final_g': out['m_final_g'], 'v_norm_g': out['v_norm_g'], 'v_mem_norm_g': out['v_mem_norm_g'], 'v_w_in': out['v_w_in'], 'v_conv_w': out['v_conv_w'], 'v_a_log': out['v_a_log'], 'v_dt_bias': out['v_dt_bias'], 'v_dn_norm_g': out['v_dn_norm_g'], 'v_w_mem_kv': out['v_w_mem_kv'], 'v_w_br_dn': out['v_w_br_dn'], 'v_w_br_sb': out['v_w_br_sb'], 'v_w_br_mem': out['v_w_br_mem'], 'v_w_out': out['v_w_out'], 'v_final_g': out['v_final_g']}


def _loss(weights, diff, rest, loss_target):
    with _jax.named_scope("forward"):
        args = {**rest, TWIN_DIFF_INPUT: diff, **{k: w.astype(_WEIGHT_DTYPES[k]) for k, w in weights.items()}}
        y = _forward(args)
    with _jax.named_scope("loss_head"):
        err = _jnp.square(y.astype(_jnp.float32) - loss_target)
        return 0.5 * _jnp.sum(_jnp.mean(err, axis=-1)) if err.ndim else 0.5 * err


def _adamw(w, g, m, v):
    m = ADAM_B1 * m + (1.0 - ADAM_B1) * g
    v = ADAM_B2 * v + (1.0 - ADAM_B2) * _jnp.square(g)
    m_hat = m / (1.0 - ADAM_B1 ** ADAM_STEP)
    v_hat = v / (1.0 - ADAM_B2 ** ADAM_STEP)
    delta = -ADAM_LR * (m_hat / (_jnp.sqrt(v_hat) + ADAM_EPS) + ADAM_WD * w)
    return delta, m, v


def reference(x, mem, norm_g, mem_norm_g, w_in, conv_w, a_log, dt_bias, dn_norm_g, w_mem_kv, w_br_dn, w_br_sb, w_br_mem, w_out, final_g, loss_target, m_norm_g, m_mem_norm_g, m_w_in, m_conv_w, m_a_log, m_dt_bias, m_dn_norm_g, m_w_mem_kv, m_w_br_dn, m_w_br_sb, m_w_br_mem, m_w_out, m_final_g, v_norm_g, v_mem_norm_g, v_w_in, v_conv_w, v_a_log, v_dt_bias, v_dn_norm_g, v_w_mem_kv, v_w_br_dn, v_w_br_sb, v_w_br_mem, v_w_out, v_final_g):
    given = dict(x=x, mem=mem, norm_g=norm_g, mem_norm_g=mem_norm_g, w_in=w_in, conv_w=conv_w, a_log=a_log, dt_bias=dt_bias, dn_norm_g=dn_norm_g, w_mem_kv=w_mem_kv, w_br_dn=w_br_dn, w_br_sb=w_br_sb, w_br_mem=w_br_mem, w_out=w_out, final_g=final_g, loss_target=loss_target, m_norm_g=m_norm_g, m_mem_norm_g=m_mem_norm_g, m_w_in=m_w_in, m_conv_w=m_conv_w, m_a_log=m_a_log, m_dt_bias=m_dt_bias, m_dn_norm_g=m_dn_norm_g, m_w_mem_kv=m_w_mem_kv, m_w_br_dn=m_w_br_dn, m_w_br_sb=m_w_br_sb, m_w_br_mem=m_w_br_mem, m_w_out=m_w_out, m_final_g=m_final_g, v_norm_g=v_norm_g, v_mem_norm_g=v_mem_norm_g, v_w_in=v_w_in, v_conv_w=v_conv_w, v_a_log=v_a_log, v_dt_bias=v_dt_bias, v_dn_norm_g=v_dn_norm_g, v_w_mem_kv=v_w_mem_kv, v_w_br_dn=v_w_br_dn, v_w_br_sb=v_w_br_sb, v_w_br_mem=v_w_br_mem, v_w_out=v_w_out, v_final_g=v_final_g)
    weights = {n: given[n] for n in TWIN_WEIGHTS}
    shared = {n: given[n] for n in SHARED_INPUTS}
    per_example = {n: given[n] for n in ['x', 'mem']}
    grad_fn = _jax.value_and_grad(_loss, argnums=(0, 1))

    def one_microbatch(ex, loss_target):
        ex = dict(ex)
        diff = ex.pop(TWIN_DIFF_INPUT)
        return grad_fn(weights, diff, {**shared, **ex}, loss_target)

    if N_MICROBATCH == 1:
        loss, (grad_w, grad_x) = one_microbatch(per_example, given["loss_target"])
    else:
        def body(carry, xs):
            loss_sum, grad_sum = carry
            l_k, (gw_k, gx_k) = one_microbatch(xs[0], xs[1])
            with _jax.named_scope("update"):
                return (loss_sum + l_k, _jax.tree.map(_jnp.add, grad_sum, gw_k)), gx_k

        init = (_jnp.zeros((), _jnp.float32), _jax.tree.map(_jnp.zeros_like, weights))
        (loss, grad_w), grad_x = _jax.lax.scan(body, init, (per_example, given["loss_target"]))
    with _jax.named_scope("update"):
        delta_w, new_m, new_v = {}, {}, {}
        for n in TWIN_WEIGHTS:
            delta_w[n], new_m[n], new_v[n] = _adamw(weights[n], grad_w[n], given["m_" + n], given["v_" + n])
    return (loss, grad_x, *[grad_w[n] for n in TWIN_WEIGHTS], *[delta_w[n] for n in TWIN_WEIGHTS],
            *[new_m[n] for n in TWIN_WEIGHTS], *[new_v[n] for n in TWIN_WEIGHTS])
```

```python
import functools
import math

import jax
import jax.numpy as jnp
from jax import lax
from jax.experimental import pallas as pl
from jax.experimental.pallas import tpu as pltpu

F32 = jnp.float32
BF16 = jnp.bfloat16

D_MODEL = 1024
N_DEV = 8
N_HEADS = 8
D_HEAD = 128
DN_CHUNK = 64
CONV_K = 4
MEM_LEN = 256
MEM_HEADS = 4
MEM_DH = 64
MEM_W = MEM_HEADS * MEM_DH
NORM_EPS = 1e-6
IN_WIDTH = 11792
SHARD_W = IN_WIDTH // N_DEV

LANE = 128
SUPER = 2 * DN_CHUNK

O_QKV_DN = 0
O_Z_DN = 3072
O_QKV_SB = 4096
O_Z_SB = 7168
O_MQ = 8192
O_MZ = 8448
O_GATES = 8704
O_BA = 11776
W_AL = 11904
ORIG_BA = 4096
N_BA = 16

WIN_TILES = 13
WIN_W = WIN_TILES * LANE


def _aligned_col(o):
    return o if o < ORIG_BA else o - N_BA


WIN_START = tuple(min(_aligned_col(SHARD_W * d) // LANE, (W_AL // LANE) - WIN_TILES) for d in range(N_DEV))
WIN_OFF = tuple(_aligned_col(SHARD_W * d) - LANE * WIN_START[d] if SHARD_W * d >= ORIG_BA + N_BA or SHARD_W * d < ORIG_BA
                else None for d in range(N_DEV))
BA_DEV = ORIG_BA // SHARD_W
BA_LOCAL = ORIG_BA - BA_DEV * SHARD_W

ADAM_LR = 0.001
ADAM_B1 = 0.9
ADAM_B2 = 0.999
ADAM_EPS = 1e-08
ADAM_WD = 0.01
ADAM_STEP = 10

NN = (((1,), (0,)), ((), ()))
NT = (((1,), (1,)), ((), ()))
TN = (((0,), (0,)), ((), ()))


def _dot(a, b, dims):
    return lax.dot_general(a.astype(BF16), b.astype(BF16), dims, preferred_element_type=F32)


def _split2(a):
    hi = a.astype(BF16)
    lo = (a - hi.astype(F32)).astype(BF16)
    return hi, lo


def _dot3(a, b, dims):
    ah, al = _split2(a)
    bh, bl = _split2(b)
    d = functools.partial(lax.dot_general, dimension_numbers=dims, preferred_element_type=F32)
    return d(ah, bh) + (d(ah, bl) + d(al, bh))


def _sel_dot_impl(sel01, x, dims):
    sel = sel01.astype(BF16)
    h1 = x.astype(BF16)
    r1 = x - h1.astype(F32)
    h2 = r1.astype(BF16)
    h3 = (r1 - h2.astype(F32)).astype(BF16)
    d = functools.partial(lax.dot_general, dimension_numbers=dims, preferred_element_type=F32)
    return d(sel, h1) + (d(sel, h2) + d(sel, h3))


@jax.custom_vjp
def _sel_dot(sel01, x):
    return _sel_dot_impl(sel01, x, NN)


_sel_dot.defvjp(lambda s, x: (_sel_dot(s, x), s),
                lambda s, g: (jnp.zeros_like(s), _sel_dot_impl(s, g, TN)))


def _make_mm(dotfn):
    @jax.custom_vjp
    def nn(a, b):
        return dotfn(a, b, NN)

    @jax.custom_vjp
    def nt(a, b):
        return dotfn(a, b, NT)

    @jax.custom_vjp
    def tn(a, b):
        return dotfn(a, b, TN)

    nn.defvjp(lambda a, b: (nn(a, b), (a, b)), lambda r, g: (nt(g, r[1]), tn(r[0], g)))
    nt.defvjp(lambda a, b: (nt(a, b), (a, b)), lambda r, g: (nn(g, r[1]), tn(g, r[0])))
    tn.defvjp(lambda a, b: (tn(a, b), (a, b)), lambda r, g: (nt(r[1], g), nn(r[0], g)))
    return nn, nt, tn


mm_nn, mm_nt, mm_tn = _make_mm(_dot)
mm3_nn, mm3_nt, mm3_tn = _make_mm(_dot3)


def _sigmoid(x):
    return jax.nn.sigmoid(x)


def _silu(x):
    return x * _sigmoid(x)


def _softplus_parts(x):
    sp = jnp.log1p(jnp.exp(-jnp.abs(x)))
    return jnp.maximum(x, 0.0) + sp, jnp.maximum(-x, 0.0) + sp


def _rmsnorm(x, g):
    return x * lax.rsqrt(jnp.mean(x * x, axis=-1, keepdims=True) + NORM_EPS) * g


def _iota2(shape, dim):
    return lax.broadcasted_iota(jnp.int32, shape, dim)


def _div64(i):
    return lax.shift_right_logical(i, jnp.full(i.shape, 6, jnp.int32))


@jax.custom_vjp
def _inv_unit_lower(m):
    n = m.shape[0]
    eye = (_iota2((n, n), 0) == _iota2((n, n), 1)).astype(F32)
    r = eye - m
    p = m
    for _ in range(5):
        p = mm3_nn(p, p)
        r = r + mm3_nn(r, p)
    return r


def _inv_fwd(m):
    r = _inv_unit_lower(m)
    return r, r


def _inv_bwd(r, g):
    return (-mm3_nt(mm3_tn(r, g), r),)


_inv_unit_lower.defvjp(_inv_fwd, _inv_bwd)


def _dn_block(cq, ck, cv, bcol, acol, zt, alog, dtb, gn, s0):
    n = SUPER
    row = _iota2((n, n), 0)
    col = _iota2((n, n), 1)
    same = _div64(row) == _div64(col)
    incl = jnp.logical_and(same, row >= col)
    strict = jnp.logical_and(same, row > col)

    qn = cq * lax.rsqrt(jnp.sum(cq * cq, axis=-1, keepdims=True) + NORM_EPS) * (D_HEAD ** -0.5)
    kn = ck * lax.rsqrt(jnp.sum(ck * ck, axis=-1, keepdims=True) + NORM_EPS)
    beta = _sigmoid(bcol)
    g = -(jnp.exp(alog) * _softplus_parts(acol + dtb)[0])
    gb = jnp.broadcast_to(g, (n, n))
    gcum = _sel_dot(incl.astype(F32), gb)
    diff = gcum - gcum.T
    gam_incl = jnp.where(incl, jnp.exp(jnp.where(incl, diff, 0.0)), 0.0)
    gam_strict = jnp.where(strict, gam_incl, 0.0)
    kk = mm_nt(kn, kn)
    t_inv = _inv_unit_lower(beta * kk * gam_strict)
    eg = jnp.exp(gcum)
    u = mm_nn(t_inv, cv * beta)
    w = mm_nn(t_inv, kn * (beta * eg))
    a_intra = mm_nt(qn, kn) * gam_incl
    q_dec = qn * eg
    h = DN_CHUNK
    last0 = gcum[h - 1:h, :]
    last1 = gcum[n - 1:n, :]
    lastb = jnp.concatenate([jnp.broadcast_to(last0, (h, n)), jnp.broadcast_to(last1, (h, n))], axis=0)
    k_dec = kn * jnp.exp(lastb - gcum)
    v0 = u[:h] - mm_nn(w[:h], s0)
    s1 = s0 * jnp.exp(last0) + mm_tn(k_dec[:h], v0)
    v1 = u[h:] - mm_nn(w[h:], s1)
    s2 = s1 * jnp.exp(last1) + mm_tn(k_dec[h:], v1)
    v_new = jnp.concatenate([v0, v1], axis=0)
    o = jnp.concatenate([mm_nn(q_dec[:h], s0), mm_nn(q_dec[h:], s1)], axis=0) + mm_nn(a_intra, v_new)
    return _rmsnorm(o, gn) * _silu(zt), s2


def _mem_fn(mq, mz, mkv):
    mk = mkv[:, :MEM_W]
    mv = mkv[:, MEM_W:]
    lane = _iota2((1, MEM_W), 1)
    out = jnp.zeros(mq.shape, F32)
    for hd in range(MEM_HEADS):
        hm = (_div64(lane) == hd).astype(F32)
        s = mm_nt(mq * hm, mk) * (1.0 / math.sqrt(MEM_DH))
        s = s - jnp.max(s, axis=-1, keepdims=True)
        e = jnp.exp(s)
        p = e / jnp.sum(e, axis=-1, keepdims=True)
        out = out + mm_nn(p, mv) * hm
    return out * _silu(mz)


def _merge_fn(gd, gs, gm, yd, ys, ym):
    return _sigmoid(gd) * yd + _sigmoid(gs) * ys + _sigmoid(gm) * ym


def _loss_fn(x, mo, fg, tgt):
    y = _rmsnorm(x + mo, fg)
    err = y - tgt
    return 0.5 * jnp.sum(jnp.mean(err * err, axis=-1, keepdims=True), axis=0, keepdims=True)


def _matmul(a, b, mode, out_dtype, tm, tn, tk, name, b_col0=0, n_cols=None):
    if mode == "nn":
        m, kdim = a.shape
        n = b.shape[1] if n_cols is None else n_cols
    elif mode == "nt":
        m, kdim = a.shape
        n = b.shape[0]
    else:
        kdim, m = a.shape
        n = b.shape[1] if n_cols is None else n_cols
    tm, tn, tk = min(tm, m), min(tn, n), min(tk, kdim)
    assert m % tm == 0 and n % tn == 0 and kdim % tk == 0 and b_col0 % tn == 0
    nk = kdim // tk
    jb = b_col0 // tn
    dims = {"nn": NN, "nt": NT, "tn": TN}[mode]

    def body(a_ref, b_ref, o_ref, acc_ref):
        k = pl.program_id(2)
        part = _dot(a_ref[...], b_ref[...], dims)

        @pl.when(k == 0)
        def _():
            acc_ref[...] = part

        @pl.when(k > 0)
        def _():
            acc_ref[...] += part

        @pl.when(k == nk - 1)
        def _():
            o_ref[...] = acc_ref[...].astype(o_ref.dtype)

    if mode == "nn":
        a_spec = pl.BlockSpec((tm, tk), lambda i, j, k: (i, k))
        b_spec = pl.BlockSpec((tk, tn), lambda i, j, k: (k, j + jb))
    elif mode == "nt":
        a_spec = pl.BlockSpec((tm, tk), lambda i, j, k: (i, k))
        b_spec = pl.BlockSpec((tn, tk), lambda i, j, k: (j, k))
    else:
        a_spec = pl.BlockSpec((tk, tm), lambda i, j, k: (k, i))
        b_spec = pl.BlockSpec((tk, tn), lambda i, j, k: (k, j + jb))
    return pl.pallas_call(
        body,
        name=name,
        grid=(m // tm, n // tn, nk),
        in_specs=[a_spec, b_spec],
        out_specs=pl.BlockSpec((tm, tn), lambda i, j, k: (i, j)),
        out_shape=jax.ShapeDtypeStruct((m, n), out_dtype),
        scratch_shapes=[pltpu.VMEM((tm, tn), F32)],
        compiler_params=pltpu.CompilerParams(dimension_semantics=("parallel", "parallel", "arbitrary")),
    )(a, b)


def _norm_in(x, g, tm=256):
    t = x.shape[0]

    def body(x_ref, g_ref, h_ref, ht_ref):
        h = _rmsnorm(x_ref[...], g_ref[...])
        h_ref[...] = h.astype(BF16)
        ht_ref[...] = h.T.astype(BF16)

    return pl.pallas_call(
        body,
        name="norm_in",
        grid=(t // tm,),
        in_specs=[pl.BlockSpec((tm, D_MODEL), lambda i: (i, 0)), pl.BlockSpec((1, D_MODEL), lambda i: (0, 0))],
        out_specs=[pl.BlockSpec((tm, D_MODEL), lambda i: (i, 0)), pl.BlockSpec((D_MODEL, tm), lambda i: (0, i))],
        out_shape=[jax.ShapeDtypeStruct((t, D_MODEL), BF16), jax.ShapeDtypeStruct((D_MODEL, t), BF16)],
    )(x, g)


def _norm_in_bwd(x, g, dh, dres, tm=256):
    t = x.shape[0]

    def body(x_ref, g_ref, dh_ref, dres_ref, dx_ref, dg_ref):
        _, vjp = jax.vjp(_rmsnorm, x_ref[...], g_ref[...])
        dx, dg = vjp(dh_ref[...])
        dx_ref[...] = dx + dres_ref[...]

        @pl.when(pl.program_id(0) == 0)
        def _():
            dg_ref[...] = jnp.zeros_like(dg_ref)

        dg_ref[...] += dg

    row = pl.BlockSpec((tm, D_MODEL), lambda i: (i, 0))
    vec = pl.BlockSpec((1, D_MODEL), lambda i: (0, 0))
    return pl.pallas_call(
        body,
        name="norm_in_bwd",
        grid=(t // tm,),
        in_specs=[row, vec, row, row],
        out_specs=[row, vec],
        out_shape=[jax.ShapeDtypeStruct((t, D_MODEL), F32), jax.ShapeDtypeStruct((1, D_MODEL), F32)],
    )(x, g, dh, dres)


def _merge(proj, yd, ys, ym, tm=256, tc=512):
    t = proj.shape[0]
    g0 = O_GATES // tc
    gstep = D_MODEL // tc

    def body(gd, gs, gm, yd_ref, ys_ref, ym_ref, o_ref):
        o_ref[...] = _merge_fn(gd[...], gs[...], gm[...], yd_ref[...], ys_ref[...], ym_ref[...]).astype(BF16)

    def gate(k):
        return pl.BlockSpec((tm, tc), lambda i, j: (i, g0 + k * gstep + j))

    blk = pl.BlockSpec((tm, tc), lambda i, j: (i, j))
    return pl.pallas_call(
        body,
        name="merge",
        grid=(t // tm, D_MODEL // tc),
        in_specs=[gate(0), gate(1), gate(2), blk, blk, blk],
        out_specs=blk,
        out_shape=jax.ShapeDtypeStruct((t, D_MODEL), BF16),
    )(proj, proj, proj, yd, ys, ym)


def _merge_bwd(proj, yd, ys, ym, dmerged, tm=256, tc=512):
    t = proj.shape[0]
    g0 = O_GATES // tc
    gstep = D_MODEL // tc

    def body(gd, gs, gm, yd_ref, ys_ref, ym_ref, dm_ref, dyd, dys, dym, dgd, dgs, dgm):
        _, vjp = jax.vjp(_merge_fn, gd[...], gs[...], gm[...], yd_ref[...], ys_ref[...], ym_ref[...])
        outs = vjp(dm_ref[...])
        for ref, val in zip((dgd, dgs, dgm, dyd, dys, dym), outs):
            ref[...] = val.astype(BF16)

    def gate(k):
        return pl.BlockSpec((tm, tc), lambda i, j: (i, g0 + k * gstep + j))

    blk = pl.BlockSpec((tm, tc), lambda i, j: (i, j))
    o = jax.ShapeDtypeStruct((t, D_MODEL), BF16)
    return pl.pallas_call(
        body,
        name="merge_bwd",
        grid=(t // tm, D_MODEL // tc),
        in_specs=[gate(0), gate(1), gate(2), blk, blk, blk, blk],
        out_specs=[blk] * 6,
        out_shape=[o] * 6,
    )(proj, proj, proj, yd, ys, ym, dmerged)


def _loss_head(x, mo, fg, tgt, tm=256):
    t = x.shape[0]

    def body(x_ref, mo_ref, fg_ref, t_ref, loss_ref, dout_ref, dfg_ref):
        loss, vjp = jax.vjp(_loss_fn, x_ref[...], mo_ref[...], fg_ref[...], t_ref[...])
        _, dmo, dfg, _ = vjp(jnp.ones((1, 1), F32))

        @pl.when(pl.program_id(0) == 0)
        def _():
            loss_ref[...] = jnp.zeros_like(loss_ref)
            dfg_ref[...] = jnp.zeros_like(dfg_ref)

        loss_ref[...] += jnp.broadcast_to(loss, loss_ref.shape)
        dfg_ref[...] += dfg
        dout_ref[...] = dmo

    row = pl.BlockSpec((tm, D_MODEL), lambda i: (i, 0))
    vec = pl.BlockSpec((1, D_MODEL), lambda i: (0, 0))
    return pl.pallas_call(
        body,
        name="loss_head",
        grid=(t // tm,),
        in_specs=[row, row, vec, row],
        out_specs=[pl.BlockSpec((1, LANE), lambda i: (0, 0)), row, vec],
        out_shape=[jax.ShapeDtypeStruct((1, LANE), F32), jax.ShapeDtypeStruct((t, D_MODEL), F32),
                   jax.ShapeDtypeStruct((1, D_MODEL), F32)],
    )(x, mo, fg, tgt)


def _shift_rows(x, s):
    t = x.shape[0]
    if s == 0:
        return x
    rolled = pltpu.roll(x, s % t, 0)
    row = _iota2(x.shape, 0)
    keep = row >= s if s > 0 else row < t + s
    return jnp.where(keep, rolled, 0.0)


def _conv_pre(x, w):
    return sum(_shift_rows(x, CONV_K - 1 - j) * w[j:j + 1, :] for j in range(CONV_K))


def _dn_conv(proj, conv_w):
    t = proj.shape[0]
    nb = 3 * D_MODEL // LANE

    def body(x_ref, w_ref, c_ref):
        c_ref[...] = _silu(_conv_pre(x_ref[...], w_ref[...]))

    return pl.pallas_call(
        body,
        name="dn_conv",
        grid=(nb,),
        in_specs=[pl.BlockSpec((t, LANE), lambda j: (0, j)), pl.BlockSpec((CONV_K, LANE), lambda j: (0, j))],
        out_specs=pl.BlockSpec((t, LANE), lambda j: (0, j)),
        out_shape=jax.ShapeDtypeStruct((t, 3 * D_MODEL), F32),
    )(proj, conv_w)


def _dn_conv_bwd(proj, conv_w, dc, part):
    t = proj.shape[0]
    nb = D_MODEL // LANE
    b0 = part * nb

    def body(x_ref, w_ref, dc_ref, dx_ref, dw_ref):
        x = x_ref[...]
        w = w_ref[...]
        pre = _conv_pre(x, w)
        sg = _sigmoid(pre)
        dpre = dc_ref[...] * (sg * (1.0 + pre * (1.0 - sg)))
        dx = sum(_shift_rows(dpre, -(CONV_K - 1 - j)) * w[j:j + 1, :] for j in range(CONV_K))
        dx_ref[...] = dx.astype(BF16)
        dw_ref[...] = jnp.concatenate(
            [jnp.sum(dpre * _shift_rows(x, CONV_K - 1 - j), axis=0, keepdims=True) for j in range(CONV_K)], axis=0)

    return pl.pallas_call(
        body,
        name=f"dn_conv_bwd{part}",
        grid=(nb,),
        in_specs=[pl.BlockSpec((t, LANE), lambda j: (0, b0 + j)), pl.BlockSpec((CONV_K, LANE), lambda j: (0, b0 + j)),
                  pl.BlockSpec((t, LANE), lambda j: (0, j))],
        out_specs=[pl.BlockSpec((t, LANE), lambda j: (0, j)), pl.BlockSpec((CONV_K, LANE), lambda j: (0, j))],
        out_shape=[jax.ShapeDtypeStruct((t, D_MODEL), BF16), jax.ShapeDtypeStruct((CONV_K, D_MODEL), F32)],
    )(proj, conv_w, dc)


def _ba_columns(ba, hd):
    lane = _iota2(ba.shape, 1)
    bcol = jnp.sum(jnp.where(lane == hd, ba, 0.0), axis=1, keepdims=True)
    acol = jnp.sum(jnp.where(lane == N_HEADS + hd, ba, 0.0), axis=1, keepdims=True)
    return bcol, acol


def _head_scalar(row, hd):
    lane = _iota2(row.shape, 1)
    return jnp.sum(jnp.where(lane == hd, row, 0.0), axis=1, keepdims=True)


def _dn_fwd(c, proj, alog_row, dtb_row, gn):
    t = c.shape[0]
    nblk = t // SUPER
    zb = O_Z_DN // LANE
    bab = O_BA // LANE

    def body(cq, ck, cv, ba_ref, z_ref, alog_ref, dtb_ref, gn_ref, o_ref, s_ref, state):
        hd = pl.program_id(1)

        @pl.when(pl.program_id(0) == 0)
        def _():
            state[hd] = jnp.zeros((D_HEAD, D_HEAD), F32)

        s0 = state[hd]
        s_ref[0, 0] = s0
        bcol, acol = _ba_columns(ba_ref[...], hd)
        o, s2 = _dn_block(cq[...], ck[...], cv[...], bcol, acol, z_ref[...], _head_scalar(alog_ref[...], hd),
                          _head_scalar(dtb_ref[...], hd), gn_ref[...], s0)
        o_ref[...] = o.astype(BF16)
        state[hd] = s2

    def colblk(b0):
        return pl.BlockSpec((SUPER, LANE), lambda i, h: (i, b0 + h))

    vec = pl.BlockSpec((1, LANE), lambda i, h: (0, 0))
    return pl.pallas_call(
        body,
        name="dn_fwd",
        grid=(nblk, N_HEADS),
        in_specs=[colblk(0), colblk(N_HEADS), colblk(2 * N_HEADS),
                  pl.BlockSpec((SUPER, LANE), lambda i, h: (i, bab)), colblk(zb), vec, vec, vec],
        out_specs=[pl.BlockSpec((SUPER, LANE), lambda i, h: (i, h)),
                   pl.BlockSpec((1, 1, D_HEAD, D_HEAD), lambda i, h: (i, h, 0, 0))],
        out_shape=[jax.ShapeDtypeStruct((t, D_MODEL), BF16),
                   jax.ShapeDtypeStruct((nblk, N_HEADS, D_HEAD, D_HEAD), F32)],
        scratch_shapes=[pltpu.VMEM((N_HEADS, D_HEAD, D_HEAD), F32)],
    )(c, c, c, proj, proj, alog_row, dtb_row, gn)


def _dn_bwd(c, proj, alog_row, dtb_row, gn, states, do):
    t = c.shape[0]
    nblk = t // SUPER
    zb = O_Z_DN // LANE
    bab = O_BA // LANE

    def body(cq, ck, cv, ba_ref, z_ref, alog_ref, dtb_ref, gn_ref, s_ref, do_ref,
             dq_ref, dk_ref, dv_ref, dz_ref, dba_ref, dsc_ref, dgn_ref, dstate):
        i = pl.program_id(0)
        hd = pl.program_id(1)

        @pl.when(i == 0)
        def _():
            dstate[hd] = jnp.zeros((D_HEAD, D_HEAD), F32)

        @pl.when(jnp.logical_and(i == 0, hd == 0))
        def _():
            dsc_ref[...] = jnp.zeros_like(dsc_ref)
            dgn_ref[...] = jnp.zeros_like(dgn_ref)

        @pl.when(hd == 0)
        def _():
            dba_ref[...] = jnp.zeros_like(dba_ref)

        bcol, acol = _ba_columns(ba_ref[...], hd)
        _, vjp = jax.vjp(_dn_block, cq[...], ck[...], cv[...], bcol, acol, z_ref[...],
                         _head_scalar(alog_ref[...], hd), _head_scalar(dtb_ref[...], hd), gn_ref[...], s_ref[0, 0])
        dq, dk, dv, dbc, dac, dz, dal, ddt, dgn, ds0 = vjp((do_ref[...].astype(F32), dstate[hd]))
        dq_ref[...] = dq
        dk_ref[...] = dk
        dv_ref[...] = dv
        dz_ref[...] = dz.astype(BF16)
        lane = _iota2((SUPER, LANE), 1)
        dba_ref[...] += jnp.where(lane == hd, dbc, 0.0) + jnp.where(lane == N_HEADS + hd, dac, 0.0)
        lane1 = _iota2((1, LANE), 1)
        dsc_ref[0:1, :] += jnp.where(lane1 == hd, dal, 0.0)
        dsc_ref[1:2, :] += jnp.where(lane1 == hd, ddt, 0.0)
        dgn_ref[...] += dgn
        dstate[hd] = ds0

    def colblk(b0):
        return pl.BlockSpec((SUPER, LANE), lambda i, h: (nblk - 1 - i, b0 + h))

    vec = pl.BlockSpec((1, LANE), lambda i, h: (0, 0))
    outs = pl.pallas_call(
        body,
        name="dn_bwd",
        grid=(nblk, N_HEADS),
        in_specs=[colblk(0), colblk(N_HEADS), colblk(2 * N_HEADS),
                  pl.BlockSpec((SUPER, LANE), lambda i, h: (nblk - 1 - i, bab)), colblk(zb), vec, vec, vec,
                  pl.BlockSpec((1, 1, D_HEAD, D_HEAD), lambda i, h: (nblk - 1 - i, h, 0, 0)), colblk(0)],
        out_specs=[colblk(0), colblk(0), colblk(0), colblk(0),
                   pl.BlockSpec((SUPER, LANE), lambda i, h: (nblk - 1 - i, 0)),
                   pl.BlockSpec((2, LANE), lambda i, h: (0, 0)), vec],
        out_shape=[jax.ShapeDtypeStruct((t, D_MODEL), F32)] * 3
        + [jax.ShapeDtypeStruct((t, D_MODEL), BF16), jax.ShapeDtypeStruct((t, LANE), F32),
           jax.ShapeDtypeStruct((2, LANE), F32), jax.ShapeDtypeStruct((1, LANE), F32)],
        scratch_shapes=[pltpu.VMEM((N_HEADS, D_HEAD, D_HEAD), F32)],
    )(c, c, c, proj, proj, alog_row, dtb_row, gn, states, do)
    return outs


SB_TQ = 256
SB_TK = 128


def _sb_scores(qs, k, t_pos, kb):
    z = _dot(qs, k, NT)
    s_pos = kb * SB_TK + _iota2(z.shape, 1)
    causal = s_pos < t_pos
    sp_pos, sp_neg = _softplus_parts(z)
    lb = -sp_neg
    lf_raw = -sp_pos
    lf = jnp.where(causal, lf_raw, 0.0)
    return causal, lb, lf_raw, lf


def _suffix_sums(x, sel):
    hi, lo = _split2(x)
    d = functools.partial(lax.dot_general, dimension_numbers=NN, preferred_element_type=F32)
    return d(hi, sel) + d(lo, sel)


def _sb_fwd(proj):
    t = proj.shape[0]
    tq = min(SB_TQ, t)
    qb, kb0, vb, zb = (O_QKV_SB // LANE, O_QKV_SB // LANE + N_HEADS, O_QKV_SB // LANE + 2 * N_HEADS, O_Z_SB // LANE)
    scale = 1.0 / math.sqrt(D_HEAD)

    def body(q_ref, k_ref, v_ref, z_ref, o_ref, oraw_ref):
        qi = pl.program_id(1)
        qs = (q_ref[...] * scale).astype(BF16)
        t_pos = qi * tq + _iota2((tq, SB_TK), 0)
        after = (_iota2((SB_TK, SB_TK), 0) > _iota2((SB_TK, SB_TK), 1)).astype(BF16)
        nkb = (qi + 1) * (tq // SB_TK)

        def step(i, carry):
            acc, c_lf = carry
            kb = nkb - 1 - i
            rows = pl.ds(pl.multiple_of(kb * SB_TK, SB_TK), SB_TK)
            causal, lb, _, lf = _sb_scores(qs, k_ref[rows, :], t_pos, kb)
            surv = _suffix_sums(lf, after) + c_lf
            att = jnp.where(causal, jnp.exp(lb + surv), 0.0)
            acc = acc + _dot(att, v_ref[rows, :], NN)
            return acc, c_lf + jnp.sum(lf, axis=1, keepdims=True)

        acc, _ = lax.fori_loop(0, nkb, step, (jnp.zeros((tq, D_HEAD), F32), jnp.zeros((tq, 1), F32)))
        oraw_ref[...] = acc
        o_ref[...] = (acc * _silu(z_ref[...])).astype(BF16)

    def full(b0):
        return pl.BlockSpec((t, LANE), lambda h, i: (0, b0 + h))

    out = pl.BlockSpec((tq, LANE), lambda h, i: (i, h))
    return pl.pallas_call(
        body,
        name="sb_fwd",
        grid=(N_HEADS, t // tq),
        in_specs=[pl.BlockSpec((tq, LANE), lambda h, i: (i, qb + h)), full(kb0), full(vb),
                  pl.BlockSpec((tq, LANE), lambda h, i: (i, zb + h))],
        out_specs=[out, out],
        out_shape=[jax.ShapeDtypeStruct((t, D_MODEL), BF16), jax.ShapeDtypeStruct((t, D_MODEL), F32)],
    )(proj, proj, proj, proj)


def _sb_bwd(proj, oraw, do):
    t = proj.shape[0]
    tq = min(SB_TQ, t)
    qb, kb0, vb, zb = (O_QKV_SB // LANE, O_QKV_SB // LANE + N_HEADS, O_QKV_SB // LANE + 2 * N_HEADS, O_Z_SB // LANE)
    scale = 1.0 / math.sqrt(D_HEAD)

    def body(q_ref, k_ref, v_ref, z_ref, oraw_ref, do_ref, dq_ref, dk_ref, dv_ref, dz_ref, dk_acc, dv_acc,
             p_scr, sig_scr, oms_scr):
        qi = pl.program_id(1)
        nq = pl.num_programs(1)

        @pl.when(qi == 0)
        def _():
            dk_acc[...] = jnp.zeros_like(dk_acc)
            dv_acc[...] = jnp.zeros_like(dv_acc)

        zg = z_ref[...]
        sg = _sigmoid(zg)
        dog = do_ref[...].astype(F32)
        dz_ref[...] = (dog * oraw_ref[...] * (sg * (1.0 + zg * (1.0 - sg)))).astype(BF16)
        d_o16 = (dog * (zg * sg)).astype(BF16)
        qs = (q_ref[...] * scale).astype(BF16)
        t_pos = qi * tq + _iota2((tq, SB_TK), 0)
        ri = _iota2((SB_TK, SB_TK), 0)
        ci = _iota2((SB_TK, SB_TK), 1)
        after = (ri > ci).astype(BF16)
        earlier = (ri < ci).astype(BF16)
        nkb = (qi + 1) * (tq // SB_TK)

        def step_down(i, c_lf):
            kb = nkb - 1 - i
            rows = pl.ds(pl.multiple_of(kb * SB_TK, SB_TK), SB_TK)
            causal, lb, lf_raw, lf = _sb_scores(qs, k_ref[rows, :], t_pos, kb)
            surv = _suffix_sums(lf, after) + c_lf
            att = jnp.where(causal, jnp.exp(lb + surv), 0.0)
            p_scr[kb] = att * _dot(d_o16, v_ref[rows, :], NT)
            sig_scr[kb] = jnp.where(causal, jnp.exp(lb), 0.0)
            oms_scr[kb] = jnp.exp(lf_raw)
            dv_acc[rows, :] += _dot(att, d_o16, TN)
            return c_lf + jnp.sum(lf, axis=1, keepdims=True)

        lax.fori_loop(0, nkb, step_down, jnp.zeros((tq, 1), F32))

        def step_up(kb, carry):
            dq, c_p = carry
            rows = pl.ds(pl.multiple_of(kb * SB_TK, SB_TK), SB_TK)
            p = p_scr[kb]
            before = _suffix_sums(p, earlier) + c_p
            dzz = p * oms_scr[kb] - sig_scr[kb] * before
            dk_acc[rows, :] += _dot(dzz, qs, TN)
            return dq + _dot(dzz, k_ref[rows, :], NN), c_p + jnp.sum(p, axis=1, keepdims=True)

        dq, _ = lax.fori_loop(0, nkb, step_up, (jnp.zeros((tq, D_HEAD), F32), jnp.zeros((tq, 1), F32)))
        dq_ref[...] = (dq * scale).astype(BF16)

        @pl.when(qi == nq - 1)
        def _():
            dk_ref[...] = dk_acc[...].astype(BF16)
            dv_ref[...] = dv_acc[...].astype(BF16)

    def full(b0):
        return pl.BlockSpec((t, LANE), lambda h, i: (0, b0 + h))

    blk = pl.BlockSpec((tq, LANE), lambda h, i: (i, h))
    o = jax.ShapeDtypeStruct((t, D_MODEL), BF16)
    return pl.pallas_call(
        body,
        name="sb_bwd",
        grid=(N_HEADS, t // tq),
        in_specs=[pl.BlockSpec((tq, LANE), lambda h, i: (i, qb + h)), full(kb0), full(vb),
                  pl.BlockSpec((tq, LANE), lambda h, i: (i, zb + h)), blk, blk],
        out_specs=[blk, full(0), full(0), blk],
        out_shape=[o, o, o, o],
        scratch_shapes=[pltpu.VMEM((t, D_HEAD), F32), pltpu.VMEM((t, D_HEAD), F32)]
        + [pltpu.VMEM((t // SB_TK, tq, SB_TK), F32)] * 3,
    )(proj, proj, proj, proj, oraw, do)


def _mem_kv_fn(mem, mg, w):
    return mm_nn(_rmsnorm(mem, mg), w)


def _mem_kv(mem, mg, w):
    def body(m_ref, g_ref, w_ref, o_ref):
        o_ref[...] = _mem_kv_fn(m_ref[...], g_ref[...], w_ref[...])

    return pl.pallas_call(body, name="mem_kv", out_shape=jax.ShapeDtypeStruct((MEM_LEN, 2 * MEM_W), F32))(mem, mg, w)


def _mem_kv_bwd(mem, mg, w, dmkv):
    def body(m_ref, g_ref, w_ref, d_ref, dg_ref, dw_ref):
        _, vjp = jax.vjp(_mem_kv_fn, m_ref[...], g_ref[...], w_ref[...].astype(F32))
        _, dg, dw = vjp(d_ref[...])
        dg_ref[...] = dg
        dw_ref[...] = dw

    return pl.pallas_call(
        body, name="mem_kv_bwd",
        out_shape=[jax.ShapeDtypeStruct((1, D_MODEL), F32), jax.ShapeDtypeStruct((D_MODEL, 2 * MEM_W), F32)],
    )(mem, mg, w, dmkv)


def _mem_attn(proj, mkv, tm=256):
    t = proj.shape[0]
    tm = min(tm, t)

    def body(q_ref, z_ref, kv_ref, o_ref):
        o_ref[...] = _mem_fn(q_ref[...], z_ref[...], kv_ref[...]).astype(BF16)

    return pl.pallas_call(
        body,
        name="mem_attn",
        grid=(t // tm,),
        in_specs=[pl.BlockSpec((tm, MEM_W), lambda i: (i, O_MQ // MEM_W)),
                  pl.BlockSpec((tm, MEM_W), lambda i: (i, O_MZ // MEM_W)),
                  pl.BlockSpec((MEM_LEN, 2 * MEM_W), lambda i: (0, 0))],
        out_specs=pl.BlockSpec((tm, MEM_W), lambda i: (i, 0)),
        out_shape=jax.ShapeDtypeStruct((t, MEM_W), BF16),
    )(proj, proj, mkv)


def _mem_attn_bwd(proj, mkv, do, tm=256):
    t = proj.shape[0]
    tm = min(tm, t)

    def body(q_ref, z_ref, kv_ref, do_ref, dq_ref, dz_ref, dkv_ref):
        _, vjp = jax.vjp(_mem_fn, q_ref[...], z_ref[...], kv_ref[...])
        dq, dz, dkv = vjp(do_ref[...].astype(F32))
        dq_ref[...] = dq.astype(BF16)
        dz_ref[...] = dz.astype(BF16)

        @pl.when(pl.program_id(0) == 0)
        def _():
            dkv_ref[...] = jnp.zeros_like(dkv_ref)

        dkv_ref[...] += dkv

    blk = pl.BlockSpec((tm, MEM_W), lambda i: (i, 0))
    kv = pl.BlockSpec((MEM_LEN, 2 * MEM_W), lambda i: (0, 0))
    return pl.pallas_call(
        body,
        name="mem_attn_bwd",
        grid=(t // tm,),
        in_specs=[pl.BlockSpec((tm, MEM_W), lambda i: (i, O_MQ // MEM_W)),
                  pl.BlockSpec((tm, MEM_W), lambda i: (i, O_MZ // MEM_W)), kv, blk],
        out_specs=[blk, blk, kv],
        out_shape=[jax.ShapeDtypeStruct((t, MEM_W), BF16), jax.ShapeDtypeStruct((t, MEM_W), BF16),
                   jax.ShapeDtypeStruct((MEM_LEN, 2 * MEM_W), F32)],
    )(proj, proj, mkv, do)


def _local_step(x, mem, tgt, norm_g, mem_norm_g, w_al, conv_w, alog_row, dtb_row, dn_norm_g, w_mem_kv, w_br_dn, w_br_sb,
                w_br_mem, w_out, final_g):
    h, h_t = _norm_in(x, norm_g)
    proj = _matmul(h, w_al, "nn", F32, 2048, 384, 1024, "proj")

    c = _dn_conv(proj, conv_w)
    o_dn, states = _dn_fwd(c, proj, alog_row, dtb_row, dn_norm_g)
    o_sb, o_sb_raw = _sb_fwd(proj)
    mkv = _mem_kv(mem, mem_norm_g, w_mem_kv)
    o_m = _mem_attn(proj, mkv)

    y_dn = _matmul(o_dn, w_br_dn, "nn", F32, 512, 1024, 1024, "y_dn")
    y_sb = _matmul(o_sb, w_br_sb, "nn", F32, 512, 1024, 1024, "y_sb")
    y_m = _matmul(o_m, w_br_mem, "nn", F32, 512, 1024, 1024, "y_m")
    merged = _merge(proj, y_dn, y_sb, y_m)
    mo = _matmul(merged, w_out, "nn", F32, 512, 1024, 1024, "mo")
    loss, dout, d_final_g = _loss_head(x, mo, final_g, tgt)

    dmerged = _matmul(dout, w_out, "nt", F32, 512, 1024, 1024, "dmerged")
    dw_out = _matmul(merged, dout, "tn", F32, 256, 1024, 2048, "dw_out")
    dy_dn, dy_sb, dy_m, dg_dn, dg_sb, dg_m = _merge_bwd(proj, y_dn, y_sb, y_m, dmerged)
    do_dn = _matmul(dy_dn, w_br_dn, "nt", BF16, 512, 1024, 1024, "do_dn")
    do_sb = _matmul(dy_sb, w_br_sb, "nt", BF16, 512, 1024, 1024, "do_sb")
    do_m = _matmul(dy_m, w_br_mem, "nt", BF16, 512, 256, 1024, "do_m")
    dw_br_dn = _matmul(o_dn, dy_dn, "tn", F32, 256, 1024, 2048, "dw_br_dn")
    dw_br_sb = _matmul(o_sb, dy_sb, "tn", F32, 256, 1024, 2048, "dw_br_sb")
    dw_br_mem = _matmul(o_m, dy_m, "tn", F32, 256, 1024, 2048, "dw_br_mem")

    dmq, dmz, dmkv = _mem_attn_bwd(proj, mkv, do_m)
    d_mem_norm_g, dw_mem_kv = _mem_kv_bwd(mem, mem_norm_g, w_mem_kv, dmkv)
    dq_sb, dk_sb, dv_sb, dz_sb = _sb_bwd(proj, o_sb_raw, do_sb)
    dcq, dck, dcv, dz_dn, dba, dscal, d_dn_norm_g = _dn_bwd(c, proj, alog_row, dtb_row, dn_norm_g, states, do_dn)
    dq_dn, dcw_q = _dn_conv_bwd(proj, conv_w, dcq, 0)
    dk_dn, dcw_k = _dn_conv_bwd(proj, conv_w, dck, 1)
    dv_dn, dcw_v = _dn_conv_bwd(proj, conv_w, dcv, 2)
    d_conv_w = jnp.concatenate([dcw_q, dcw_k, dcw_v], axis=1)

    dproj = jnp.concatenate([dq_dn, dk_dn, dv_dn, dz_dn, dq_sb, dk_sb, dv_sb, dz_sb, dmq, dmz, dg_dn, dg_sb, dg_m,
                             dba.astype(BF16)], axis=1)
    dh = _matmul(dproj, w_al, "nt", F32, 512, 1024, 3968, "dh")
    dw_al = _matmul(h_t, dproj, "nn", F32, 1024, 384, 2048, "dw_al")
    grad_x, d_norm_g = _norm_in_bwd(x, norm_g, dh, dout)
    return dict(loss=loss, grad_x=grad_x, norm_g=d_norm_g, mem_norm_g=d_mem_norm_g, w_al=dw_al, conv_w=d_conv_w,
                scal=dscal, dn_norm_g=d_dn_norm_g, w_mem_kv=dw_mem_kv, w_br_dn=dw_br_dn, w_br_sb=dw_br_sb,
                w_br_mem=dw_br_mem, w_out=dw_out, final_g=d_final_g)


MESH = pl.DeviceIdType.MESH
ANY = pl.BlockSpec(memory_space=pl.ANY)


def _position():
    return lax.axis_index("x"), lax.axis_index("y"), lax.axis_index("c")


def _all_gather(xs, name):
    n = len(xs)

    def body(*refs):
        x_refs, o_refs = refs[:n], refs[n:2 * n]
        send_sems, recv_sems, local_sems = refs[2 * n:]
        x, y, c = _position()
        me, sibling = (x, y, c), (x, y, 1 - c)
        chips = [(1 - x, y), (x, 1 - y), (1 - x, 1 - y)]

        def slot(p):
            return 4 * p[0] + 2 * p[1] + p[2]

        def copy(a, k, block, to, src=None):
            dst = o_refs[a].at[slot(block)]
            return pltpu.make_async_remote_copy(
                src_ref=dst if src is None else src, dst_ref=dst, send_sem=send_sems.at[7 * a + k],
                recv_sem=recv_sems.at[7 * a + k], device_id=to, device_id_type=MESH)

        mine = [pltpu.make_async_copy(x_refs[a], o_refs[a].at[slot(me)], local_sems.at[a]) for a in range(n)]
        for cp in mine:
            cp.start()
        first = []
        for a in range(n):
            first.append(copy(a, 0, me, sibling, src=x_refs[a]))
            first += [copy(a, 1 + j, me, (*chip, c), src=x_refs[a]) for j, chip in enumerate(chips)]
        for cp in first:
            cp.start()
        passed = []
        for j, chip in enumerate(chips):
            for a in range(n):
                copy(a, 1 + j, (*chip, c), me).wait_recv()
                cp = copy(a, 4 + j, (*chip, c), sibling)
                cp.start()
                passed.append(cp)
        for a in range(n):
            copy(a, 0, sibling, me).wait_recv()
            for j, chip in enumerate(chips):
                copy(a, 4 + j, (*chip, 1 - c), me).wait_recv()
        for cp in first + passed:
            cp.wait_send()
        for cp in mine:
            cp.wait()

    return pl.pallas_call(
        body,
        name=name,
        in_specs=[ANY] * n,
        out_specs=[ANY] * n,
        out_shape=[jax.ShapeDtypeStruct((N_DEV, *v.shape), v.dtype) for v in xs],
        scratch_shapes=[pltpu.SemaphoreType.DMA((7 * n,)), pltpu.SemaphoreType.DMA((7 * n,)),
                        pltpu.SemaphoreType.DMA((n,))],
    )(*xs)


def _window_view(ref, dest):
    return ref.at[:, pl.ds(LANE * WIN_START[dest], WIN_W)]


def _halving_stage(xs, axis, name, windowed=()):
    n_arr = len(xs)
    metas = []
    for k, v in enumerate(xs):
        if k in windowed:
            metas.append((N_DEV // 2, v.shape[0], WIN_W))
        else:
            assert v.shape[1] == 2
            metas.append((v.shape[0], v.shape[2], v.shape[3]))
    chunk = [min(r, 1 << int(math.log2((1 << 17) // c))) for (_, r, c) in metas]
    assert all(r % ch == 0 and ch % 8 == 0 for ch, (_, r, _) in zip(chunk, metas))
    offs = [sum(m[0] for m in metas[:k]) for k in range(n_arr)]
    n_sem = sum(m[0] for m in metas)

    def body(*refs):
        x_refs = refs[:n_arr]
        o_refs = refs[n_arr:2 * n_arr]
        land_refs = refs[2 * n_arr:3 * n_arr]
        rest = refs[3 * n_arr:]
        bufs = rest[:3 * n_arr]
        send_sems, recv_sems, in_sems, out_sems = rest[3 * n_arr:]
        pos = dict(zip("xyc", _position()))
        bit = pos[axis]
        peer = tuple(1 - pos[a] if a == axis else pos[a] for a in "xyc")

        def view(k, i, b):
            if k in windowed:
                return _window_view(x_refs[k], 2 * i + b)
            return x_refs[k].at[i, b]

        def add_blocks(k, a_view, b_view, o_view):
            _, rows, _ = metas[k]
            ch = chunk[k]
            nch = rows // ch
            va, vb, vo = bufs[3 * k:3 * k + 3]

            def rows_of(j):
                return pl.ds(pl.multiple_of(j * ch, 8), ch)

            def loads(j, s):
                return (pltpu.make_async_copy(a_view.at[rows_of(j), :], va.at[s], in_sems.at[0, s]),
                        pltpu.make_async_copy(b_view.at[rows_of(j), :], vb.at[s], in_sems.at[1, s]))

            def store(j, s):
                return pltpu.make_async_copy(vo.at[s], o_view.at[rows_of(j), :], out_sems.at[s])

            for cp in loads(0, 0):
                cp.start()

            def step(j, _):
                s = lax.rem(j, 2)

                @pl.when(j + 1 < nch)
                def _():
                    for cp in loads(j + 1, 1 - s):
                        cp.start()

                for cp in loads(j, s):
                    cp.wait()

                @pl.when(j >= 2)
                def _():
                    store(j - 2, s).wait()

                vo[s] = va[s] + vb[s]
                store(j, s).start()
                return 0

            lax.fori_loop(0, nch, step, 0)
            for j in range(max(0, nch - 2), nch):
                store(j, j % 2).wait()

        for b in (0, 1):
            @pl.when(bit == b)
            def _(b=b):
                sends = []
                for k in range(n_arr):
                    for i in range(metas[k][0]):
                        cp = pltpu.make_async_remote_copy(
                            src_ref=view(k, i, 1 - b), dst_ref=land_refs[k].at[i], send_sem=send_sems.at[offs[k] + i],
                            recv_sem=recv_sems.at[offs[k] + i], device_id=peer, device_id_type=MESH)
                        cp.start()
                        sends.append(cp)
                idx = 0
                for k in range(n_arr):
                    for i in range(metas[k][0]):
                        sends[idx].wait_recv()
                        add_blocks(k, view(k, i, b), land_refs[k].at[i], o_refs[k].at[i])
                        idx += 1
                for cp in sends:
                    cp.wait_send()

    out_shape = [jax.ShapeDtypeStruct(m, F32) for m in metas]
    scratch = []
    for k in range(n_arr):
        scratch += [pltpu.VMEM((2, chunk[k], metas[k][2]), F32)] * 3
    scratch += [pltpu.SemaphoreType.DMA((n_sem,)), pltpu.SemaphoreType.DMA((n_sem,)),
                pltpu.SemaphoreType.DMA((2, 2)), pltpu.SemaphoreType.DMA((2,))]
    outs = pl.pallas_call(
        body,
        name=name,
        in_specs=[ANY] * n_arr,
        out_specs=[ANY] * (2 * n_arr),
        out_shape=out_shape + out_shape,
        scratch_shapes=scratch,
    )(*xs)
    return outs[:n_arr]


def _reduce_scatter(dw_al, blocks):
    xs = [dw_al] + [b.reshape(N_DEV // 2, 2, *b.shape[1:]) for b in blocks]
    ys = _halving_stage(xs, "c", "rs_c", windowed=(0,))
    ys = _halving_stage([v.reshape(2, 2, *v.shape[1:]) for v in ys], "y", "rs_y")
    ys = _halving_stage([v.reshape(1, 2, *v.shape[1:]) for v in ys], "x", "rs_x")
    return [v[0] for v in ys]


def _sum_slots(g):
    def body(g_ref, o_ref):
        acc = g_ref[0]
        for d in range(1, N_DEV):
            acc = acc + g_ref[d]
        o_ref[...] = acc

    return pl.pallas_call(body, name="sum_slots", out_shape=jax.ShapeDtypeStruct(g.shape[1:], g.dtype))(g)


def _assemble_w_al(wins, bas):
    n_tiles = W_AL // LANE
    ba_tile = O_BA // LANE
    owners = [[d for d in range(N_DEV) if WIN_START[d] <= t < WIN_START[d] + WIN_TILES] for t in range(ba_tile)]
    runs = []
    shared = []
    for t in range(ba_tile):
        if len(owners[t]) == 1:
            d = owners[t][0]
            if runs and runs[-1][0] == d and runs[-1][1] + runs[-1][2] == t:
                runs[-1][2] += 1
            else:
                runs.append([d, t, 1])
        else:
            assert len(owners[t]) == 2
            shared.append(t)
    n_sh = len(shared)
    rows = wins.shape[1]

    def body(w_ref, ba_ref, o_ref, va, vb, sems):
        def tiles(ref, d, t, n):
            return ref.at[d, :, pl.ds(LANE * (t - WIN_START[d]), LANE * n)]

        copies = [pltpu.make_async_copy(tiles(w_ref, d, t, n), o_ref.at[:, pl.ds(LANE * t, LANE * n)], sems.at[i])
                  for i, (d, t, n) in enumerate(runs)]
        copies.append(pltpu.make_async_copy(ba_ref.at[BA_DEV], o_ref.at[:, pl.ds(LANE * ba_tile, LANE)],
                                            sems.at[len(runs)]))
        for cp in copies:
            cp.start()
        loads = []
        for i, t in enumerate(shared):
            d0, d1 = owners[t]
            loads.append(pltpu.make_async_copy(tiles(w_ref, d0, t, 1), va.at[i], sems.at[len(runs) + 1 + 2 * i]))
            loads.append(pltpu.make_async_copy(tiles(w_ref, d1, t, 1), vb.at[i], sems.at[len(runs) + 2 + 2 * i]))
        for cp in loads:
            cp.start()
        for cp in loads:
            cp.wait()
        va[...] = va[...] + vb[...]
        stores = [pltpu.make_async_copy(va.at[i], o_ref.at[:, pl.ds(LANE * t, LANE)], sems.at[len(runs) + 1 + 2 * i])
                  for i, t in enumerate(shared)]
        for cp in stores:
            cp.start()
        for cp in stores + copies:
            cp.wait()

    assert n_tiles == ba_tile + 1
    return pl.pallas_call(
        body,
        name="assemble_w_al",
        in_specs=[ANY, ANY],
        out_specs=ANY,
        out_shape=jax.ShapeDtypeStruct((rows, W_AL), wins.dtype),
        scratch_shapes=[pltpu.VMEM((n_sh, rows, LANE), wins.dtype), pltpu.VMEM((n_sh, rows, LANE), wins.dtype),
                        pltpu.SemaphoreType.DMA((len(runs) + 1 + 2 * n_sh,))],
    )(wins, bas)


def _adamw(w, g, m, v, name):
    r, c = w.shape
    tm = r if r <= 256 else 256
    assert r % tm == 0

    def body(w_ref, g_ref, m_ref, v_ref, d_ref, nm_ref, nv_ref):
        gg = g_ref[...]
        m_new = ADAM_B1 * m_ref[...] + (1.0 - ADAM_B1) * gg
        v_new = ADAM_B2 * v_ref[...] + (1.0 - ADAM_B2) * (gg * gg)
        m_hat = m_new / (1.0 - ADAM_B1 ** ADAM_STEP)
        v_hat = v_new / (1.0 - ADAM_B2 ** ADAM_STEP)
        d_ref[...] = -ADAM_LR * (m_hat / (jnp.sqrt(v_hat) + ADAM_EPS) + ADAM_WD * w_ref[...])
        nm_ref[...] = m_new
        nv_ref[...] = v_new

    blk = pl.BlockSpec((tm, c), lambda i: (i, 0))
    o = jax.ShapeDtypeStruct((r, c), F32)
    return pl.pallas_call(body, name=name, grid=(r // tm,), in_specs=[blk] * 4, out_specs=[blk] * 3,
                          out_shape=[o, o, o])(w, g, m, v)


def _select(me, table):
    return sum(jnp.where(me == d, jnp.int32(v), jnp.int32(0)) for d, v in enumerate(table))


WIN_SHIFT = tuple(SHARD_W * d - LANE * WIN_START[d] for d in range(N_DEV))
PAD_L = 256
PAD_R = 256


def _shard_to_window(shard, me):
    shift = _select(me, WIN_SHIFT)
    start = _select(me, WIN_START)
    padded = jnp.pad(shard, ((0, 0), (PAD_L, PAD_R)))
    rows = shard.shape[0]
    lo = lax.dynamic_slice(padded, (0, PAD_L - shift), (rows, WIN_W))
    hi = lax.dynamic_slice(padded, (0, PAD_L - shift + N_BA), (rows, WIN_W))
    aligned = LANE * start + lax.broadcasted_iota(jnp.int32, (1, WIN_W), 1)
    return jnp.where(aligned >= ORIG_BA, hi, lo)


def _window_to_shard(win, ba_grad, me):
    shift = _select(me, WIN_SHIFT)
    rows = win.shape[0]
    padded = jnp.pad(win, ((0, 0), (N_BA, PAD_R)))
    lo = lax.dynamic_slice(padded, (0, N_BA + shift), (rows, SHARD_W))
    hi = lax.dynamic_slice(padded, (0, shift), (rows, SHARD_W))
    orig = SHARD_W * me + lax.broadcasted_iota(jnp.int32, (1, SHARD_W), 1)
    ba_full = lax.dynamic_update_slice(jnp.zeros((rows, SHARD_W), win.dtype), ba_grad, (0, BA_LOCAL))
    return jnp.where(orig < ORIG_BA, lo, jnp.where(orig >= ORIG_BA + N_BA, hi, ba_full))


def _pad_row(v, width=D_MODEL):
    v = v.reshape(1, -1)
    return jnp.pad(v, ((0, 0), (0, width - v.shape[1])))


def _slab(v, rows=8):
    return jnp.pad(v, ((0, rows - v.shape[0]), (0, D_MODEL - v.shape[1])))


def kernel(x, mem, norm_g, mem_norm_g, w_in, conv_w, a_log, dt_bias, dn_norm_g, w_mem_kv, w_br_dn, w_br_sb, w_br_mem, w_out, final_g, loss_target, m_norm_g, m_mem_norm_g, m_w_in, m_conv_w, m_a_log, m_dt_bias, m_dn_norm_g, m_w_mem_kv, m_w_br_dn, m_w_br_sb, m_w_br_mem, m_w_out, m_final_g, v_norm_g, v_mem_norm_g, v_w_in, v_conv_w, v_a_log, v_dt_bias, v_dn_norm_g, v_w_mem_kv, v_w_br_dn, v_w_br_sb, v_w_br_mem, v_w_out, v_final_g):
    xi, yi, ci = _position()
    me = 4 * xi + 2 * yi + ci

    shard = w_in[0]
    win = _shard_to_window(shard, me).astype(BF16)
    ba = jnp.pad(shard[:, BA_LOCAL:BA_LOCAL + N_BA], ((0, 0), (0, LANE - N_BA))).astype(BF16)
    g_win, g_ba, g_kv, g_dn, g_sb, g_out, g_mem, g_conv = _all_gather(
        [win, ba, w_mem_kv[0].astype(BF16), w_br_dn[0].astype(BF16), w_br_sb[0].astype(BF16), w_out[0].astype(BF16),
         w_br_mem[0].astype(BF16), conv_w[0]], "gather_weights")
    w_al = _assemble_w_al(g_win, g_ba)
    w_mem_kv_f = g_kv.reshape(D_MODEL, 2 * MEM_W)
    w_br_dn_f = g_dn.reshape(D_MODEL, D_MODEL)
    w_br_sb_f = g_sb.reshape(D_MODEL, D_MODEL)
    w_out_f = g_out.reshape(D_MODEL, D_MODEL)
    w_br_mem_f = g_mem.transpose(1, 0, 2).reshape(MEM_W, D_MODEL)
    conv_w_f = g_conv.transpose(1, 0, 2).reshape(CONV_K, 3 * D_MODEL)

    r = _local_step(x[0], mem[0], loss_target[0], norm_g, mem_norm_g, w_al, conv_w_f, _pad_row(a_log, LANE),
                    _pad_row(dt_bias, LANE), dn_norm_g, w_mem_kv_f, w_br_dn_f, w_br_sb_f, w_br_mem_f, w_out_f,
                    final_g.reshape(1, D_MODEL))

    dw_al = r["w_al"]
    g_win, g_kv, g_dn, g_sb, g_out, g_mem = _reduce_scatter(dw_al, [
        r["w_mem_kv"].reshape(N_DEV, D_MODEL // N_DEV, 2 * MEM_W),
        r["w_br_dn"].reshape(N_DEV, D_MODEL // N_DEV, D_MODEL),
        r["w_br_sb"].reshape(N_DEV, D_MODEL // N_DEV, D_MODEL),
        r["w_out"].reshape(N_DEV, D_MODEL // N_DEV, D_MODEL),
        r["w_br_mem"].reshape(MEM_W, N_DEV, D_MODEL // N_DEV).transpose(1, 0, 2)])
    small = jnp.concatenate([
        _slab(r["norm_g"]), _slab(r["mem_norm_g"]), _slab(r["final_g"]), _slab(r["dn_norm_g"]), _slab(r["scal"]),
        _slab(r["loss"][:, :1]), _slab(r["conv_w"].reshape(CONV_K * 3, D_MODEL), 16),
        dw_al[:, O_BA:O_BA + N_BA].T], axis=0)
    (g_small,) = _all_gather([small], "gather_small")
    small = _sum_slots(g_small)
    g_norm_g, g_mem_norm_g, g_final_g = small[0:1], small[8:9], small[16]
    g_dn_norm_g = small[24:25, :LANE]
    g_a_log, g_dt_bias = small[32:33, :N_HEADS], small[33:34, :N_HEADS]
    loss = small[40, 0]
    g_conv_full = small[48:48 + CONV_K * 3].reshape(CONV_K, 3 * D_MODEL)
    cw = conv_w.shape[2]
    g_conv = lax.dynamic_slice(g_conv_full, (0, cw * me), (CONV_K, cw))
    g_w_in = _window_to_shard(g_win, small[64:64 + N_BA].T, me)

    def adam(name, w, g, m, v):
        shp = w.shape
        w2, g2, m2, v2 = (a.reshape(-1, shp[-1]) for a in (w, g, m, v))
        return tuple(o.reshape(shp) for o in _adamw(w2, g2, m2, v2, "adamw_" + name))

    def pack(vals):
        return jnp.concatenate([_slab(a.reshape(1, -1)) for a in vals], axis=0)

    smalls = (norm_g, mem_norm_g, final_g, dn_norm_g, a_log, dt_bias)
    p_d, p_m, p_v = _adamw(pack(smalls), pack((g_norm_g, g_mem_norm_g, g_final_g, g_dn_norm_g, g_a_log, g_dt_bias)),
                           pack((m_norm_g, m_mem_norm_g, m_final_g, m_dn_norm_g, m_a_log, m_dt_bias)),
                           pack((v_norm_g, v_mem_norm_g, v_final_g, v_dn_norm_g, v_a_log, v_dt_bias)), "adamw_small")

    def unpack(p):
        return [p[8 * i, :a.size].reshape(a.shape) for i, a in enumerate(smalls)]

    sd, sm, sv = unpack(p_d), unpack(p_m), unpack(p_v)
    grads = dict(norm_g=g_norm_g, mem_norm_g=g_mem_norm_g, w_in=g_w_in[None], conv_w=g_conv[None], a_log=g_a_log,
                 dt_bias=g_dt_bias, dn_norm_g=g_dn_norm_g, w_mem_kv=g_kv[None], w_br_dn=g_dn[None], w_br_sb=g_sb[None],
                 w_br_mem=g_mem[None], w_out=g_out[None], final_g=g_final_g)
    big = dict(w_in=(w_in, m_w_in, v_w_in), conv_w=(conv_w, m_conv_w, v_conv_w), w_mem_kv=(w_mem_kv, m_w_mem_kv, v_w_mem_kv),
               w_br_dn=(w_br_dn, m_w_br_dn, v_w_br_dn), w_br_sb=(w_br_sb, m_w_br_sb, v_w_br_sb),
               w_br_mem=(w_br_mem, m_w_br_mem, v_w_br_mem), w_out=(w_out, m_w_out, v_w_out))
    order = ["norm_g", "mem_norm_g", "w_in", "conv_w", "a_log", "dt_bias", "dn_norm_g", "w_mem_kv", "w_br_dn", "w_br_sb",
             "w_br_mem", "w_out", "final_g"]
    small_idx = {"norm_g": 0, "mem_norm_g": 1, "final_g": 2, "dn_norm_g": 3, "a_log": 4, "dt_bias": 5}
    deltas, new_m, new_v = {}, {}, {}
    for nm in order:
        if nm in big:
            w, m, v = big[nm]
            deltas[nm], new_m[nm], new_v[nm] = adam(nm, w, grads[nm], m, v)
        else:
            i = small_idx[nm]
            deltas[nm], new_m[nm], new_v[nm] = sd[i], sm[i], sv[i]
    return (loss, r["grad_x"][None], *[grads[nm] for nm in order], *[deltas[nm] for nm in order],
            *[new_m[nm] for nm in order], *[new_v[nm] for nm in order])
```

```python
import functools
import math

import jax
import jax.numpy as jnp
from jax import lax
from jax.experimental import pallas as pl
from jax.experimental.pallas import tpu as pltpu

F32 = jnp.float32
BF16 = jnp.bfloat16

D_MODEL = 1024
N_DEV = 8
N_HEADS = 8
D_HEAD = 128
DN_CHUNK = 64
CONV_K = 4
MEM_LEN = 256
MEM_HEADS = 4
MEM_DH = 64
MEM_W = MEM_HEADS * MEM_DH
NORM_EPS = 1e-6
IN_WIDTH = 11792
SHARD_W = IN_WIDTH // N_DEV

LANE = 128
SUPER = 2 * DN_CHUNK

O_QKV_DN = 0
O_Z_DN = 3072
O_QKV_SB = 4096
O_Z_SB = 7168
O_MQ = 8192
O_MZ = 8448
O_GATES = 8704
O_BA = 11776
W_AL = 11904
ORIG_BA = 4096
N_BA = 16

WIN_TILES = 13
WIN_W = WIN_TILES * LANE


def _aligned_col(o):
    return o if o < ORIG_BA else o - N_BA


WIN_START = tuple(min(_aligned_col(SHARD_W * d) // LANE, (W_AL // LANE) - WIN_TILES) for d in range(N_DEV))
WIN_OFF = tuple(_aligned_col(SHARD_W * d) - LANE * WIN_START[d] if SHARD_W * d >= ORIG_BA + N_BA or SHARD_W * d < ORIG_BA
                else None for d in range(N_DEV))
BA_DEV = ORIG_BA // SHARD_W
BA_LOCAL = ORIG_BA - BA_DEV * SHARD_W

ADAM_LR = 0.001
ADAM_B1 = 0.9
ADAM_B2 = 0.999
ADAM_EPS = 1e-08
ADAM_WD = 0.01
ADAM_STEP = 10

NN = (((1,), (0,)), ((), ()))
NT = (((1,), (1,)), ((), ()))
TN = (((0,), (0,)), ((), ()))


def _dot(a, b, dims):
    return lax.dot_general(a.astype(BF16), b.astype(BF16), dims, preferred_element_type=F32)


def _split2(a):
    hi = a.astype(BF16)
    lo = (a - hi.astype(F32)).astype(BF16)
    return hi, lo


def _dot3(a, b, dims):
    ah, al = _split2(a)
    bh, bl = _split2(b)
    d = functools.partial(lax.dot_general, dimension_numbers=dims, preferred_element_type=F32)
    return d(ah, bh) + (d(ah, bl) + d(al, bh))


def _sel_dot_impl(sel01, x, dims):
    sel = sel01.astype(BF16)
    h1 = x.astype(BF16)
    r1 = x - h1.astype(F32)
    h2 = r1.astype(BF16)
    h3 = (r1 - h2.astype(F32)).astype(BF16)
    d = functools.partial(lax.dot_general, dimension_numbers=dims, preferred_element_type=F32)
    return d(sel, h1) + (d(sel, h2) + d(sel, h3))


@jax.custom_vjp
def _sel_dot(sel01, x):
    return _sel_dot_impl(sel01, x, NN)


_sel_dot.defvjp(lambda s, x: (_sel_dot(s, x), s),
                lambda s, g: (jnp.zeros_like(s), _sel_dot_impl(s, g, TN)))


def _make_mm(dotfn):
    @jax.custom_vjp
    def nn(a, b):
        return dotfn(a, b, NN)

    @jax.custom_vjp
    def nt(a, b):
        return dotfn(a, b, NT)

    @jax.custom_vjp
    def tn(a, b):
        return dotfn(a, b, TN)

    nn.defvjp(lambda a, b: (nn(a, b), (a, b)), lambda r, g: (nt(g, r[1]), tn(r[0], g)))
    nt.defvjp(lambda a, b: (nt(a, b), (a, b)), lambda r, g: (nn(g, r[1]), tn(g, r[0])))
    tn.defvjp(lambda a, b: (tn(a, b), (a, b)), lambda r, g: (nt(r[1], g), nn(r[0], g)))
    return nn, nt, tn


mm_nn, mm_nt, mm_tn = _make_mm(_dot)
mm3_nn, mm3_nt, mm3_tn = _make_mm(_dot3)


def _sigmoid(x):
    return jax.nn.sigmoid(x)


def _silu(x):
    return x * _sigmoid(x)


def _softplus_parts(x):
    sp = jnp.log1p(jnp.exp(-jnp.abs(x)))
    return jnp.maximum(x, 0.0) + sp, jnp.maximum(-x, 0.0) + sp


def _rmsnorm(x, g):
    return x * lax.rsqrt(jnp.mean(x * x, axis=-1, keepdims=True) + NORM_EPS) * g


def _iota2(shape, dim):
    return lax.broadcasted_iota(jnp.int32, shape, dim)


def _div64(i):
    return lax.shift_right_logical(i, jnp.full(i.shape, 6, jnp.int32))


def _each(f, *lists):
    return [f(*a) for a in zip(*lists)]


@jax.custom_vjp
def _inv_unit_lower(ms):
    n = ms[0].shape[0]
    eye = (_iota2((n, n), 0) == _iota2((n, n), 1)).astype(F32)
    rs = [eye - m for m in ms]
    ps = ms
    for _ in range(5):
        ps = _each(mm3_nn, ps, ps)
        rs = _each(lambda r, p: r + mm3_nn(r, p), rs, ps)
    return rs


def _inv_fwd(ms):
    rs = _inv_unit_lower(ms)
    return rs, rs


def _inv_bwd(rs, gs):
    ts = _each(mm3_tn, rs, gs)
    return (_each(lambda t, r: -mm3_nt(t, r), ts, rs),)


_inv_unit_lower.defvjp(_inv_fwd, _inv_bwd)


def _dn_block(cq, ck, cv, bcol, acol, zt, alog, dtb, gn, s0):
    n = SUPER
    h = DN_CHUNK
    row = _iota2((n, n), 0)
    col = _iota2((n, n), 1)
    same = _div64(row) == _div64(col)
    incl = jnp.logical_and(same, row >= col)
    strict = jnp.logical_and(same, row > col)
    incl_f = incl.astype(F32)

    qn = _each(lambda x: x * lax.rsqrt(jnp.sum(x * x, axis=-1, keepdims=True) + NORM_EPS) * (D_HEAD ** -0.5), cq)
    kn = _each(lambda x: x * lax.rsqrt(jnp.sum(x * x, axis=-1, keepdims=True) + NORM_EPS), ck)
    beta = _each(_sigmoid, bcol)
    g = _each(lambda al, ac, dt: -(jnp.exp(al) * _softplus_parts(ac + dt)[0]), alog, acol, dtb)
    gcum = _each(lambda x: _sel_dot(incl_f, jnp.broadcast_to(x, (n, n))), g)
    gam_incl = _each(lambda x: jnp.where(incl, jnp.exp(jnp.where(incl, x - x.T, 0.0)), 0.0), gcum)
    kk = _each(mm_nt, kn, kn)
    t_inv = _inv_unit_lower(_each(lambda b, x, gm: b * x * jnp.where(strict, gm, 0.0), beta, kk, gam_incl))
    eg = _each(jnp.exp, gcum)
    u = _each(lambda t, v, b: mm_nn(t, v * b), t_inv, cv, beta)
    w = _each(lambda t, k, b, e: mm_nn(t, k * (b * e)), t_inv, kn, beta, eg)
    a_intra = _each(lambda q, k, gm: mm_nt(q, k) * gm, qn, kn, gam_incl)
    q_dec = _each(lambda q, e: q * e, qn, eg)
    last0 = _each(lambda x: x[h - 1:h, :], gcum)
    last1 = _each(lambda x: x[n - 1:n, :], gcum)
    k_dec = _each(lambda k, x, l0, l1: k * jnp.exp(jnp.concatenate(
        [jnp.broadcast_to(l0, (h, n)), jnp.broadcast_to(l1, (h, n))], axis=0) - x), kn, gcum, last0, last1)
    v0 = _each(lambda uu, ww, s: uu[:h] - mm_nn(ww[:h], s), u, w, s0)
    o0 = _each(lambda q, s: mm_nn(q[:h], s), q_dec, s0)
    s1 = _each(lambda s, l0, k, v: s * jnp.exp(l0) + mm_tn(k[:h], v), s0, last0, k_dec, v0)
    v1 = _each(lambda uu, ww, s: uu[h:] - mm_nn(ww[h:], s), u, w, s1)
    o1 = _each(lambda q, s: mm_nn(q[h:], s), q_dec, s1)
    s2 = _each(lambda s, l1, k, v: s * jnp.exp(l1) + mm_tn(k[h:], v), s1, last1, k_dec, v1)
    o = _each(lambda a, b, am, x, y: jnp.concatenate([a, b], axis=0) + mm_nn(am, jnp.concatenate([x, y], axis=0)),
              o0, o1, a_intra, v0, v1)
    out = _each(lambda x, z: _rmsnorm(x, gn) * _silu(z), o, zt)
    return out, s2


def _mem_fn(mq, mz, mkv):
    mk = mkv[:, :MEM_W]
    mv = mkv[:, MEM_W:]
    lane = _iota2((1, MEM_W), 1)
    out = jnp.zeros(mq.shape, F32)
    for hd in range(MEM_HEADS):
        hm = (_div64(lane) == hd).astype(F32)
        s = mm_nt(mq * hm, mk) * (1.0 / math.sqrt(MEM_DH))
        s = s - jnp.max(s, axis=-1, keepdims=True)
        e = jnp.exp(s)
        p = e / jnp.sum(e, axis=-1, keepdims=True)
        out = out + mm_nn(p, mv) * hm
    return out * _silu(mz)


def _merge_fn(gd, gs, gm, yd, ys, ym):
    return _sigmoid(gd) * yd + _sigmoid(gs) * ys + _sigmoid(gm) * ym


def _loss_fn(x, mo, fg, tgt):
    y = _rmsnorm(x + mo, fg)
    err = y - tgt
    return 0.5 * jnp.sum(jnp.mean(err * err, axis=-1, keepdims=True), axis=0, keepdims=True)


def _matmul(a, b, mode, out_dtype, tm, tn, tk, name, b_col0=0, n_cols=None):
    if mode == "nn":
        m, kdim = a.shape
        n = b.shape[1] if n_cols is None else n_cols
    elif mode == "nt":
        m, kdim = a.shape
        n = b.shape[0]
    else:
        kdim, m = a.shape
        n = b.shape[1] if n_cols is None else n_cols
    tm, tn, tk = min(tm, m), min(tn, n), min(tk, kdim)
    assert m % tm == 0 and n % tn == 0 and kdim % tk == 0 and b_col0 % tn == 0
    nk = kdim // tk
    jb = b_col0 // tn
    dims = {"nn": NN, "nt": NT, "tn": TN}[mode]

    def body(a_ref, b_ref, o_ref, acc_ref):
        k = pl.program_id(2)
        part = _dot(a_ref[...], b_ref[...], dims)

        @pl.when(k == 0)
        def _():
            acc_ref[...] = part

        @pl.when(k > 0)
        def _():
            acc_ref[...] += part

        @pl.when(k == nk - 1)
        def _():
            o_ref[...] = acc_ref[...].astype(o_ref.dtype)

    if mode == "nn":
        a_spec = pl.BlockSpec((tm, tk), lambda i, j, k: (i, k))
        b_spec = pl.BlockSpec((tk, tn), lambda i, j, k: (k, j + jb))
    elif mode == "nt":
        a_spec = pl.BlockSpec((tm, tk), lambda i, j, k: (i, k))
        b_spec = pl.BlockSpec((tn, tk), lambda i, j, k: (j, k))
    else:
        a_spec = pl.BlockSpec((tk, tm), lambda i, j, k: (k, i))
        b_spec = pl.BlockSpec((tk, tn), lambda i, j, k: (k, j + jb))
    return pl.pallas_call(
        body,
        name=name,
        grid=(m // tm, n // tn, nk),
        in_specs=[a_spec, b_spec],
        out_specs=pl.BlockSpec((tm, tn), lambda i, j, k: (i, j)),
        out_shape=jax.ShapeDtypeStruct((m, n), out_dtype),
        scratch_shapes=[pltpu.VMEM((tm, tn), F32)],
        compiler_params=pltpu.CompilerParams(dimension_semantics=("parallel", "parallel", "arbitrary")),
    )(a, b)


def _norm_in(x, g, tm=256):
    t = x.shape[0]

    def body(x_ref, g_ref, h_ref, ht_ref):
        h = _rmsnorm(x_ref[...], g_ref[...])
        h_ref[...] = h.astype(BF16)
        ht_ref[...] = h.T.astype(BF16)

    return pl.pallas_call(
        body,
        name="norm_in",
        grid=(t // tm,),
        in_specs=[pl.BlockSpec((tm, D_MODEL), lambda i: (i, 0)), pl.BlockSpec((1, D_MODEL), lambda i: (0, 0))],
        out_specs=[pl.BlockSpec((tm, D_MODEL), lambda i: (i, 0)), pl.BlockSpec((D_MODEL, tm), lambda i: (0, i))],
        out_shape=[jax.ShapeDtypeStruct((t, D_MODEL), BF16), jax.ShapeDtypeStruct((D_MODEL, t), BF16)],
    )(x, g)


def _norm_in_bwd(x, g, dh, dres, tm=256):
    t = x.shape[0]

    def body(x_ref, g_ref, dh_ref, dres_ref, dx_ref, dg_ref):
        _, vjp = jax.vjp(_rmsnorm, x_ref[...], g_ref[...])
        dx, dg = vjp(dh_ref[...])
        dx_ref[...] = dx + dres_ref[...]

        @pl.when(pl.program_id(0) == 0)
        def _():
            dg_ref[...] = jnp.zeros_like(dg_ref)

        dg_ref[...] += dg

    row = pl.BlockSpec((tm, D_MODEL), lambda i: (i, 0))
    vec = pl.BlockSpec((1, D_MODEL), lambda i: (0, 0))
    return pl.pallas_call(
        body,
        name="norm_in_bwd",
        grid=(t // tm,),
        in_specs=[row, vec, row, row],
        out_specs=[row, vec],
        out_shape=[jax.ShapeDtypeStruct((t, D_MODEL), F32), jax.ShapeDtypeStruct((1, D_MODEL), F32)],
    )(x, g, dh, dres)


def _merge(proj, yd, ys, ym, tm=256, tc=512):
    t = proj.shape[0]
    g0 = O_GATES // tc
    gstep = D_MODEL // tc

    def body(gd, gs, gm, yd_ref, ys_ref, ym_ref, o_ref):
        o_ref[...] = _merge_fn(gd[...], gs[...], gm[...], yd_ref[...], ys_ref[...], ym_ref[...]).astype(BF16)

    def gate(k):
        return pl.BlockSpec((tm, tc), lambda i, j: (i, g0 + k * gstep + j))

    blk = pl.BlockSpec((tm, tc), lambda i, j: (i, j))
    return pl.pallas_call(
        body,
        name="merge",
        grid=(t // tm, D_MODEL // tc),
        in_specs=[gate(0), gate(1), gate(2), blk, blk, blk],
        out_specs=blk,
        out_shape=jax.ShapeDtypeStruct((t, D_MODEL), BF16),
    )(proj, proj, proj, yd, ys, ym)


def _merge_bwd(proj, yd, ys, ym, dmerged, tm=256, tc=512):
    t = proj.shape[0]
    g0 = O_GATES // tc
    gstep = D_MODEL // tc

    def body(gd, gs, gm, yd_ref, ys_ref, ym_ref, dm_ref, dyd, dys, dym, dgd, dgs, dgm):
        _, vjp = jax.vjp(_merge_fn, gd[...], gs[...], gm[...], yd_ref[...], ys_ref[...], ym_ref[...])
        outs = vjp(dm_ref[...])
        for ref, val in zip((dgd, dgs, dgm, dyd, dys, dym), outs):
            ref[...] = val.astype(BF16)

    def gate(k):
        return pl.BlockSpec((tm, tc), lambda i, j: (i, g0 + k * gstep + j))

    blk = pl.BlockSpec((tm, tc), lambda i, j: (i, j))
    o = jax.ShapeDtypeStruct((t, D_MODEL), BF16)
    return pl.pallas_call(
        body,
        name="merge_bwd",
        grid=(t // tm, D_MODEL // tc),
        in_specs=[gate(0), gate(1), gate(2), blk, blk, blk, blk],
        out_specs=[blk] * 6,
        out_shape=[o] * 6,
    )(proj, proj, proj, yd, ys, ym, dmerged)


def _loss_head(x, mo, fg, tgt, tm=256):
    t = x.shape[0]

    def body(x_ref, mo_ref, fg_ref, t_ref, loss_ref, dout_ref, dfg_ref):
        loss, vjp = jax.vjp(_loss_fn, x_ref[...], mo_ref[...], fg_ref[...], t_ref[...])
        _, dmo, dfg, _ = vjp(jnp.ones((1, 1), F32))

        @pl.when(pl.program_id(0) == 0)
        def _():
            loss_ref[...] = jnp.zeros_like(loss_ref)
            dfg_ref[...] = jnp.zeros_like(dfg_ref)

        loss_ref[...] += jnp.broadcast_to(loss, loss_ref.shape)
        dfg_ref[...] += dfg
        dout_ref[...] = dmo

    row = pl.BlockSpec((tm, D_MODEL), lambda i: (i, 0))
    vec = pl.BlockSpec((1, D_MODEL), lambda i: (0, 0))
    return pl.pallas_call(
        body,
        name="loss_head",
        grid=(t // tm,),
        in_specs=[row, row, vec, row],
        out_specs=[pl.BlockSpec((1, LANE), lambda i: (0, 0)), row, vec],
        out_shape=[jax.ShapeDtypeStruct((1, LANE), F32), jax.ShapeDtypeStruct((t, D_MODEL), F32),
                   jax.ShapeDtypeStruct((1, D_MODEL), F32)],
    )(x, mo, fg, tgt)


def _shift_rows(x, s):
    t = x.shape[0]
    if s == 0:
        return x
    rolled = pltpu.roll(x, s % t, 0)
    row = _iota2(x.shape, 0)
    keep = row >= s if s > 0 else row < t + s
    return jnp.where(keep, rolled, 0.0)


def _conv_pre(x, w):
    return sum(_shift_rows(x, CONV_K - 1 - j) * w[j:j + 1, :] for j in range(CONV_K))


def _dn_conv(proj, conv_w):
    t = proj.shape[0]
    nb = 3 * D_MODEL // LANE

    def body(x_ref, w_ref, c_ref):
        c_ref[...] = _silu(_conv_pre(x_ref[...], w_ref[...]))

    return pl.pallas_call(
        body,
        name="dn_conv",
        grid=(nb,),
        in_specs=[pl.BlockSpec((t, LANE), lambda j: (0, j)), pl.BlockSpec((CONV_K, LANE), lambda j: (0, j))],
        out_specs=pl.BlockSpec((t, LANE), lambda j: (0, j)),
        out_shape=jax.ShapeDtypeStruct((t, 3 * D_MODEL), F32),
    )(proj, conv_w)


def _dn_conv_bwd(proj, conv_w, dc, part):
    t = proj.shape[0]
    nb = D_MODEL // LANE
    b0 = part * nb

    def body(x_ref, w_ref, dc_ref, dx_ref, dw_ref):
        x = x_ref[...]
        w = w_ref[...]
        pre = _conv_pre(x, w)
        sg = _sigmoid(pre)
        dpre = dc_ref[...] * (sg * (1.0 + pre * (1.0 - sg)))
        dx = sum(_shift_rows(dpre, -(CONV_K - 1 - j)) * w[j:j + 1, :] for j in range(CONV_K))
        dx_ref[...] = dx.astype(BF16)
        dw_ref[...] = jnp.concatenate(
            [jnp.sum(dpre * _shift_rows(x, CONV_K - 1 - j), axis=0, keepdims=True) for j in range(CONV_K)], axis=0)

    return pl.pallas_call(
        body,
        name=f"dn_conv_bwd{part}",
        grid=(nb,),
        in_specs=[pl.BlockSpec((t, LANE), lambda j: (0, b0 + j)), pl.BlockSpec((CONV_K, LANE), lambda j: (0, b0 + j)),
                  pl.BlockSpec((t, LANE), lambda j: (0, j))],
        out_specs=[pl.BlockSpec((t, LANE), lambda j: (0, j)), pl.BlockSpec((CONV_K, LANE), lambda j: (0, j))],
        out_shape=[jax.ShapeDtypeStruct((t, D_MODEL), BF16), jax.ShapeDtypeStruct((CONV_K, D_MODEL), F32)],
    )(proj, conv_w, dc)


def _ba_columns(ba, hd):
    lane = _iota2(ba.shape, 1)
    bcol = jnp.sum(jnp.where(lane == hd, ba, 0.0), axis=1, keepdims=True)
    acol = jnp.sum(jnp.where(lane == N_HEADS + hd, ba, 0.0), axis=1, keepdims=True)
    return bcol, acol


def _head_scalar(row, hd):
    lane = _iota2(row.shape, 1)
    return jnp.sum(jnp.where(lane == hd, row, 0.0), axis=1, keepdims=True)


DN_HP = 4


def _dn_inputs(cq, ck, cv, ba_ref, z_ref, alog_ref, dtb_ref, heads, lanes):
    ba = ba_ref[...]
    cols = [_ba_columns(ba, hd) for hd in heads]
    return ([cq[:, ln] for ln in lanes], [ck[:, ln] for ln in lanes], [cv[:, ln] for ln in lanes],
            [c[0] for c in cols], [c[1] for c in cols], [z_ref[:, ln] for ln in lanes],
            [_head_scalar(alog_ref[...], hd) for hd in heads], [_head_scalar(dtb_ref[...], hd) for hd in heads])


def _dn_specs(nblk, reverse):
    w = DN_HP * LANE
    nq = D_MODEL // w

    def row(i):
        return nblk - 1 - i if reverse else i

    def colblk(b0):
        return pl.BlockSpec((SUPER, w), lambda i, h: (row(i), b0 + h))

    ba = pl.BlockSpec((SUPER, LANE), lambda i, h: (row(i), O_BA // LANE))
    vec = pl.BlockSpec((1, LANE), lambda i, h: (0, 0))
    st = pl.BlockSpec((1, DN_HP, D_HEAD, D_HEAD), lambda i, h: (row(i), h, 0, 0))
    return colblk, nq, ba, vec, st


def _dn_fwd(c, proj, alog_row, dtb_row, gn):
    t = c.shape[0]
    nblk = t // SUPER
    colblk, nq, ba, vec, st = _dn_specs(nblk, False)

    def body(cq, ck, cv, ba_ref, z_ref, alog_ref, dtb_ref, gn_ref, o_ref, s_ref, state):
        @pl.when(jnp.logical_and(pl.program_id(0) == 0, pl.program_id(1) == 0))
        def _():
            state[...] = jnp.zeros_like(state)

        heads = [pl.program_id(1) * DN_HP + j for j in range(DN_HP)]
        lanes = [slice(j * LANE, (j + 1) * LANE) for j in range(DN_HP)]
        s0 = [state[hd] for hd in heads]
        outs, s2 = _dn_block(*_dn_inputs(cq, ck, cv, ba_ref, z_ref, alog_ref, dtb_ref, heads, lanes), gn_ref[...], s0)
        for j, (hd, ln) in enumerate(zip(heads, lanes)):
            s_ref[0, j] = s0[j]
            o_ref[:, ln] = outs[j].astype(BF16)
            state[hd] = s2[j]

    return pl.pallas_call(
        body,
        name="dn_fwd",
        grid=(nblk, N_HEADS // DN_HP),
        in_specs=[colblk(0), colblk(nq), colblk(2 * nq), ba, colblk(O_Z_DN // (DN_HP * LANE)), vec, vec, vec],
        out_specs=[colblk(0), st],
        out_shape=[jax.ShapeDtypeStruct((t, D_MODEL), BF16),
                   jax.ShapeDtypeStruct((nblk, N_HEADS, D_HEAD, D_HEAD), F32)],
        scratch_shapes=[pltpu.VMEM((N_HEADS, D_HEAD, D_HEAD), F32)],
    )(c, c, c, proj, proj, alog_row, dtb_row, gn)


def _dn_bwd(c, proj, alog_row, dtb_row, gn, states, do):
    t = c.shape[0]
    nblk = t // SUPER
    colblk, nq, ba, vec, st = _dn_specs(nblk, True)

    def body(cq, ck, cv, ba_ref, z_ref, alog_ref, dtb_ref, gn_ref, s_ref, do_ref,
             dq_ref, dk_ref, dv_ref, dz_ref, dba_ref, dsc_ref, dgn_ref, dstate):
        i = pl.program_id(0)
        hq = pl.program_id(1)

        @pl.when(jnp.logical_and(i == 0, hq == 0))
        def _():
            dstate[...] = jnp.zeros_like(dstate)
            dsc_ref[...] = jnp.zeros_like(dsc_ref)
            dgn_ref[...] = jnp.zeros_like(dgn_ref)

        @pl.when(hq == 0)
        def _():
            dba_ref[...] = jnp.zeros_like(dba_ref)

        lane = _iota2((SUPER, LANE), 1)
        lane1 = _iota2((1, LANE), 1)
        heads = [hq * DN_HP + j for j in range(DN_HP)]
        lanes = [slice(j * LANE, (j + 1) * LANE) for j in range(DN_HP)]
        ds_in = [dstate[hd] for hd in heads]
        s_in = [s_ref[0, j] for j in range(DN_HP)]
        _, vjp = jax.vjp(_dn_block, *_dn_inputs(cq, ck, cv, ba_ref, z_ref, alog_ref, dtb_ref, heads, lanes),
                         gn_ref[...], s_in)
        dq, dk, dv, dbc, dac, dz, dal, ddt, dgn, ds0 = vjp(([do_ref[:, ln].astype(F32) for ln in lanes], ds_in))
        dba = jnp.zeros((SUPER, LANE), F32)
        dal_row = jnp.zeros((1, LANE), F32)
        ddt_row = jnp.zeros((1, LANE), F32)
        for j, (hd, ln) in enumerate(zip(heads, lanes)):
            dq_ref[:, ln] = dq[j]
            dk_ref[:, ln] = dk[j]
            dv_ref[:, ln] = dv[j]
            dz_ref[:, ln] = dz[j].astype(BF16)
            dstate[hd] = ds0[j]
            dba = dba + jnp.where(lane == hd, dbc[j], 0.0) + jnp.where(lane == N_HEADS + hd, dac[j], 0.0)
            dal_row = dal_row + jnp.where(lane1 == hd, dal[j], 0.0)
            ddt_row = ddt_row + jnp.where(lane1 == hd, ddt[j], 0.0)
        dba_ref[...] += dba
        dsc_ref[0:1, :] += dal_row
        dsc_ref[1:2, :] += ddt_row
        dgn_ref[...] += dgn

    outs = pl.pallas_call(
        body,
        name="dn_bwd",
        grid=(nblk, N_HEADS // DN_HP),
        in_specs=[colblk(0), colblk(nq), colblk(2 * nq), ba, colblk(O_Z_DN // (DN_HP * LANE)), vec, vec, vec, st,
                  colblk(0)],
        out_specs=[colblk(0), colblk(0), colblk(0), colblk(0),
                   pl.BlockSpec((SUPER, LANE), lambda i, h: (nblk - 1 - i, 0)),
                   pl.BlockSpec((2, LANE), lambda i, h: (0, 0)), vec],
        out_shape=[jax.ShapeDtypeStruct((t, D_MODEL), F32)] * 3
        + [jax.ShapeDtypeStruct((t, D_MODEL), BF16), jax.ShapeDtypeStruct((t, LANE), F32),
           jax.ShapeDtypeStruct((2, LANE), F32), jax.ShapeDtypeStruct((1, LANE), F32)],
        scratch_shapes=[pltpu.VMEM((N_HEADS, D_HEAD, D_HEAD), F32)],
    )(c, c, c, proj, proj, alog_row, dtb_row, gn, states, do)
    return outs


SB_TQ = 256
SB_TK = 256
SB_HP = 2


def _sb_logits(z, mask):
    sp = jnp.log(1.0 + jnp.exp(-jnp.abs(z)))
    lf_raw = -(jnp.maximum(z, 0.0) + sp)
    lb = lf_raw + z
    lf = lf_raw if mask is None else jnp.where(mask, lf_raw, 0.0)
    return lb, lf_raw, lf


def _suffix_sums(x, sel):
    hi, lo = _split2(x)
    d = functools.partial(lax.dot_general, dimension_numbers=NN, preferred_element_type=F32)
    return d(hi, sel) + d(lo, sel)


def _sb_diag_mask(tq, r):
    return r * SB_TK + _iota2((tq, SB_TK), 1) < _iota2((tq, SB_TK), 0)


def _sb_specs(t, tq):
    w = SB_HP * LANE
    q0, k0, v0, z0 = (O_QKV_SB // w, (O_QKV_SB + D_MODEL) // w, (O_QKV_SB + 2 * D_MODEL) // w, O_Z_SB // w)

    def blk(b0):
        return pl.BlockSpec((tq, w), lambda h, i: (i, b0 + h))

    def full(b0):
        return pl.BlockSpec((t, w), lambda h, i: (0, b0 + h))

    return blk(q0), full(k0), full(v0), blk(z0), blk(0), full(0)


def _sb_fwd(proj):
    t = proj.shape[0]
    tq = min(SB_TQ, t)
    ndiag = tq // SB_TK
    scale = 1.0 / math.sqrt(D_HEAD)

    def body(q_ref, k_ref, v_ref, z_ref, o_ref, oraw_ref):
        qi = pl.program_id(1)
        lanes = [slice(hd * LANE, (hd + 1) * LANE) for hd in range(SB_HP)]
        qs = [(q_ref[:, ln] * scale).astype(BF16) for ln in lanes]
        after = (_iota2((SB_TK, SB_TK), 0) > _iota2((SB_TK, SB_TK), 1)).astype(BF16)
        oraw_ref[...] = jnp.zeros_like(oraw_ref)

        def block(kb, mask, c_lf):
            rows = pl.ds(pl.multiple_of(kb * SB_TK, SB_TK), SB_TK)
            z = _each(lambda q, ln: _dot(q, k_ref[rows, ln], NT), qs, lanes)
            lg = _each(lambda x: _sb_logits(x, mask), z)
            surv = _each(lambda x: _suffix_sums(x[2], after), lg)
            att = _each(lambda x, s, c: jnp.exp(x[0] + s + c), lg, surv, c_lf)
            if mask is not None:
                att = _each(lambda a: jnp.where(mask, a, 0.0), att)
            pv = _each(lambda a, ln: _dot(a, v_ref[rows, ln], NN), att, lanes)
            for p, ln in zip(pv, lanes):
                oraw_ref[:, ln] += p
            return tuple(_each(lambda c, x: c + jnp.sum(x[2], axis=1, keepdims=True), c_lf, lg))

        carry = tuple(jnp.zeros((tq, 1), F32) for _ in range(SB_HP))
        for r in reversed(range(ndiag)):
            carry = block(qi * ndiag + r, _sb_diag_mask(tq, r), carry)
        lax.fori_loop(0, qi * ndiag, lambda i, c: block(qi * ndiag - 1 - i, None, c), carry)
        o_ref[...] = (oraw_ref[...] * _silu(z_ref[...])).astype(BF16)

    q_spec, k_spec, v_spec, z_spec, out, _ = _sb_specs(t, tq)
    return pl.pallas_call(
        body,
        name="sb_fwd",
        grid=(N_HEADS // SB_HP, t // tq),
        in_specs=[q_spec, k_spec, v_spec, z_spec],
        out_specs=[out, out],
        out_shape=[jax.ShapeDtypeStruct((t, D_MODEL), BF16), jax.ShapeDtypeStruct((t, D_MODEL), F32)],
    )(proj, proj, proj, proj)


def _sb_bwd(proj, oraw, do):
    t = proj.shape[0]
    tq = min(SB_TQ, t)
    ndiag = tq // SB_TK
    scale = 1.0 / math.sqrt(D_HEAD)

    def body(q_ref, k_ref, v_ref, z_ref, oraw_ref, do_ref, dq_ref, dk_ref, dv_ref, dz_ref, dk_acc, dv_acc,
             p_scr, sig_scr, oms_scr):
        qi = pl.program_id(1)
        nq = pl.num_programs(1)

        @pl.when(qi == 0)
        def _():
            dk_acc[...] = jnp.zeros_like(dk_acc)
            dv_acc[...] = jnp.zeros_like(dv_acc)

        heads = range(SB_HP)
        lanes = [slice(hd * LANE, (hd + 1) * LANE) for hd in heads]
        zg = z_ref[...]
        sg = _sigmoid(zg)
        dog = do_ref[...].astype(F32)
        dz_ref[...] = (dog * oraw_ref[...] * (sg * (1.0 + zg * (1.0 - sg)))).astype(BF16)
        d_o = (dog * (zg * sg)).astype(BF16)
        d_o16 = [d_o[:, ln] for ln in lanes]
        qs = [(q_ref[:, ln] * scale).astype(BF16) for ln in lanes]
        ri = _iota2((SB_TK, SB_TK), 0)
        ci = _iota2((SB_TK, SB_TK), 1)
        after = (ri > ci).astype(BF16)
        earlier = (ri < ci).astype(BF16)

        def rows_of(kb):
            return pl.ds(pl.multiple_of(kb * SB_TK, SB_TK), SB_TK)

        def down(kb, mask, c_lf):
            rows = rows_of(kb)
            z = _each(lambda q, ln: _dot(q, k_ref[rows, ln], NT), qs, lanes)
            da = _each(lambda d, ln: _dot(d, v_ref[rows, ln], NT), d_o16, lanes)
            lg = _each(lambda x: _sb_logits(x, mask), z)
            surv = _each(lambda x: _suffix_sums(x[2], after), lg)
            att = _each(lambda x, s, c: jnp.exp(x[0] + s + c), lg, surv, c_lf)
            sig = _each(lambda x: jnp.exp(x[0]), lg)
            if mask is not None:
                att = _each(lambda a: jnp.where(mask, a, 0.0), att)
                sig = _each(lambda a: jnp.where(mask, a, 0.0), sig)
            dv = _each(lambda a, d: _dot(a, d, TN), att, d_o16)
            for hd in heads:
                p_scr[hd, kb] = att[hd] * da[hd]
                sig_scr[hd, kb] = sig[hd]
                oms_scr[hd, kb] = jnp.exp(lg[hd][1])
                dv_acc[rows, lanes[hd]] += dv[hd]
            return tuple(_each(lambda c, x: c + jnp.sum(x[2], axis=1, keepdims=True), c_lf, lg))

        c_lf = tuple(jnp.zeros((tq, 1), F32) for _ in heads)
        for r in reversed(range(ndiag)):
            c_lf = down(qi * ndiag + r, _sb_diag_mask(tq, r), c_lf)
        lax.fori_loop(0, qi * ndiag, lambda i, c: down(qi * ndiag - 1 - i, None, c), c_lf)

        def up(kb, carry):
            dq, c_p = carry
            rows = rows_of(kb)
            p = [p_scr[hd, kb] for hd in heads]
            before = _each(lambda x, c: _suffix_sums(x, earlier) + c, p, c_p)
            dzz = _each(lambda x, b, hd: x * oms_scr[hd, kb] - sig_scr[hd, kb] * b, p, before, list(heads))
            dk = _each(lambda x, q: _dot(x, q, TN), dzz, qs)
            dq = _each(lambda a, x, ln: a + _dot(x, k_ref[rows, ln], NN), dq, dzz, lanes)
            for hd in heads:
                dk_acc[rows, lanes[hd]] += dk[hd]
            return tuple(dq), tuple(_each(lambda c, x: c + jnp.sum(x, axis=1, keepdims=True), c_p, p))

        init = (tuple(jnp.zeros((tq, D_HEAD), F32) for _ in heads), tuple(jnp.zeros((tq, 1), F32) for _ in heads))
        dq, _ = lax.fori_loop(0, (qi + 1) * ndiag, up, init)
        for hd in heads:
            dq_ref[:, lanes[hd]] = (dq[hd] * scale).astype(BF16)

        @pl.when(qi == nq - 1)
        def _():
            dk_ref[...] = dk_acc[...].astype(BF16)
            dv_ref[...] = dv_acc[...].astype(BF16)

    q_spec, k_spec, v_spec, z_spec, blk, full = _sb_specs(t, tq)
    o = jax.ShapeDtypeStruct((t, D_MODEL), BF16)
    w = SB_HP * LANE
    return pl.pallas_call(
        body,
        name="sb_bwd",
        grid=(N_HEADS // SB_HP, t // tq),
        in_specs=[q_spec, k_spec, v_spec, z_spec, blk, blk],
        out_specs=[blk, full, full, blk],
        out_shape=[o, o, o, o],
        scratch_shapes=[pltpu.VMEM((t, w), F32), pltpu.VMEM((t, w), F32)]
        + [pltpu.VMEM((SB_HP, t // SB_TK, tq, SB_TK), F32)] * 3,
    )(proj, proj, proj, proj, oraw, do)


def _mem_kv_fn(mem, mg, w):
    return mm_nn(_rmsnorm(mem, mg), w)


def _mem_kv(mem, mg, w):
    def body(m_ref, g_ref, w_ref, o_ref):
        o_ref[...] = _mem_kv_fn(m_ref[...], g_ref[...], w_ref[...])

    return pl.pallas_call(body, name="mem_kv", out_shape=jax.ShapeDtypeStruct((MEM_LEN, 2 * MEM_W), F32))(mem, mg, w)


def _mem_kv_bwd(mem, mg, w, dmkv):
    def body(m_ref, g_ref, w_ref, d_ref, dg_ref, dw_ref):
        _, vjp = jax.vjp(_mem_kv_fn, m_ref[...], g_ref[...], w_ref[...].astype(F32))
        _, dg, dw = vjp(d_ref[...])
        dg_ref[...] = dg
        dw_ref[...] = dw

    return pl.pallas_call(
        body, name="mem_kv_bwd",
        out_shape=[jax.ShapeDtypeStruct((1, D_MODEL), F32), jax.ShapeDtypeStruct((D_MODEL, 2 * MEM_W), F32)],
    )(mem, mg, w, dmkv)


def _mem_attn(proj, mkv, tm=256):
    t = proj.shape[0]
    tm = min(tm, t)

    def body(q_ref, z_ref, kv_ref, o_ref):
        o_ref[...] = _mem_fn(q_ref[...], z_ref[...], kv_ref[...]).astype(BF16)

    return pl.pallas_call(
        body,
        name="mem_attn",
        grid=(t // tm,),
        in_specs=[pl.BlockSpec((tm, MEM_W), lambda i: (i, O_MQ // MEM_W)),
                  pl.BlockSpec((tm, MEM_W), lambda i: (i, O_MZ // MEM_W)),
                  pl.BlockSpec((MEM_LEN, 2 * MEM_W), lambda i: (0, 0))],
        out_specs=pl.BlockSpec((tm, MEM_W), lambda i: (i, 0)),
        out_shape=jax.ShapeDtypeStruct((t, MEM_W), BF16),
    )(proj, proj, mkv)


def _mem_attn_bwd(proj, mkv, do, tm=256):
    t = proj.shape[0]
    tm = min(tm, t)

    def body(q_ref, z_ref, kv_ref, do_ref, dq_ref, dz_ref, dkv_ref):
        _, vjp = jax.vjp(_mem_fn, q_ref[...], z_ref[...], kv_ref[...])
        dq, dz, dkv = vjp(do_ref[...].astype(F32))
        dq_ref[...] = dq.astype(BF16)
        dz_ref[...] = dz.astype(BF16)

        @pl.when(pl.program_id(0) == 0)
        def _():
            dkv_ref[...] = jnp.zeros_like(dkv_ref)

        dkv_ref[...] += dkv

    blk = pl.BlockSpec((tm, MEM_W), lambda i: (i, 0))
    kv = pl.BlockSpec((MEM_LEN, 2 * MEM_W), lambda i: (0, 0))
    return pl.pallas_call(
        body,
        name="mem_attn_bwd",
        grid=(t // tm,),
        in_specs=[pl.BlockSpec((tm, MEM_W), lambda i: (i, O_MQ // MEM_W)),
                  pl.BlockSpec((tm, MEM_W), lambda i: (i, O_MZ // MEM_W)), kv, blk],
        out_specs=[blk, blk, kv],
        out_shape=[jax.ShapeDtypeStruct((t, MEM_W), BF16), jax.ShapeDtypeStruct((t, MEM_W), BF16),
                   jax.ShapeDtypeStruct((MEM_LEN, 2 * MEM_W), F32)],
    )(proj, proj, mkv, do)


def _local_step(x, mem, tgt, norm_g, mem_norm_g, w_al, conv_w, alog_row, dtb_row, dn_norm_g, w_mem_kv, w_br_dn, w_br_sb,
                w_br_mem, w_out, final_g):
    h, h_t = _norm_in(x, norm_g)
    proj = _matmul(h, w_al, "nn", F32, 2048, 384, 1024, "proj")

    c = _dn_conv(proj, conv_w)
    o_dn, states = _dn_fwd(c, proj, alog_row, dtb_row, dn_norm_g)
    o_sb, o_sb_raw = _sb_fwd(proj)
    mkv = _mem_kv(mem, mem_norm_g, w_mem_kv)
    o_m = _mem_attn(proj, mkv)

    y_dn = _matmul(o_dn, w_br_dn, "nn", F32, 512, 1024, 1024, "y_dn")
    y_sb = _matmul(o_sb, w_br_sb, "nn", F32, 512, 1024, 1024, "y_sb")
    y_m = _matmul(o_m, w_br_mem, "nn", F32, 512, 1024, 1024, "y_m")
    merged = _merge(proj, y_dn, y_sb, y_m)
    mo = _matmul(merged, w_out, "nn", F32, 512, 1024, 1024, "mo")
    loss, dout, d_final_g = _loss_head(x, mo, final_g, tgt)

    dmerged = _matmul(dout, w_out, "nt", F32, 512, 1024, 1024, "dmerged")
    dw_out = _matmul(merged, dout, "tn", F32, 256, 1024, 2048, "dw_out")
    dy_dn, dy_sb, dy_m, dg_dn, dg_sb, dg_m = _merge_bwd(proj, y_dn, y_sb, y_m, dmerged)
    do_dn = _matmul(dy_dn, w_br_dn, "nt", BF16, 512, 1024, 1024, "do_dn")
    do_sb = _matmul(dy_sb, w_br_sb, "nt", BF16, 512, 1024, 1024, "do_sb")
    do_m = _matmul(dy_m, w_br_mem, "nt", BF16, 512, 256, 1024, "do_m")
    dw_br_dn = _matmul(o_dn, dy_dn, "tn", F32, 256, 1024, 2048, "dw_br_dn")
    dw_br_sb = _matmul(o_sb, dy_sb, "tn", F32, 256, 1024, 2048, "dw_br_sb")
    dw_br_mem = _matmul(o_m, dy_m, "tn", F32, 256, 1024, 2048, "dw_br_mem")

    dmq, dmz, dmkv = _mem_attn_bwd(proj, mkv, do_m)
    d_mem_norm_g, dw_mem_kv = _mem_kv_bwd(mem, mem_norm_g, w_mem_kv, dmkv)
    dq_sb, dk_sb, dv_sb, dz_sb = _sb_bwd(proj, o_sb_raw, do_sb)
    dcq, dck, dcv, dz_dn, dba, dscal, d_dn_norm_g = _dn_bwd(c, proj, alog_row, dtb_row, dn_norm_g, states, do_dn)
    dq_dn, dcw_q = _dn_conv_bwd(proj, conv_w, dcq, 0)
    dk_dn, dcw_k = _dn_conv_bwd(proj, conv_w, dck, 1)
    dv_dn, dcw_v = _dn_conv_bwd(proj, conv_w, dcv, 2)
    d_conv_w = jnp.concatenate([dcw_q, dcw_k, dcw_v], axis=1)

    dproj = jnp.concatenate([dq_dn, dk_dn, dv_dn, dz_dn, dq_sb, dk_sb, dv_sb, dz_sb, dmq, dmz, dg_dn, dg_sb, dg_m,
                             dba.astype(BF16)], axis=1)
    dh = _matmul(dproj, w_al, "nt", F32, 512, 1024, 3968, "dh")
    dw_al = _matmul(h_t, dproj, "nn", F32, 1024, 384, 2048, "dw_al")
    grad_x, d_norm_g = _norm_in_bwd(x, norm_g, dh, dout)
    return dict(loss=loss, grad_x=grad_x, norm_g=d_norm_g, mem_norm_g=d_mem_norm_g, w_al=dw_al, conv_w=d_conv_w,
                scal=dscal, dn_norm_g=d_dn_norm_g, w_mem_kv=dw_mem_kv, w_br_dn=dw_br_dn, w_br_sb=dw_br_sb,
                w_br_mem=dw_br_mem, w_out=dw_out, final_g=d_final_g)


MESH = pl.DeviceIdType.MESH
ANY = pl.BlockSpec(memory_space=pl.ANY)


def _position():
    return lax.axis_index("x"), lax.axis_index("y"), lax.axis_index("c")


def _all_gather(xs, name):
    n = len(xs)

    def body(*refs):
        x_refs, o_refs = refs[:n], refs[n:2 * n]
        send_sems, recv_sems, local_sems = refs[2 * n:]
        x, y, c = _position()
        me, sibling = (x, y, c), (x, y, 1 - c)
        chips = [(1 - x, y), (x, 1 - y), (1 - x, 1 - y)]

        def slot(p):
            return 4 * p[0] + 2 * p[1] + p[2]

        def copy(a, k, block, to, src=None):
            dst = o_refs[a].at[slot(block)]
            return pltpu.make_async_remote_copy(
                src_ref=dst if src is None else src, dst_ref=dst, send_sem=send_sems.at[7 * a + k],
                recv_sem=recv_sems.at[7 * a + k], device_id=to, device_id_type=MESH)

        mine = [pltpu.make_async_copy(x_refs[a], o_refs[a].at[slot(me)], local_sems.at[a]) for a in range(n)]
        for cp in mine:
            cp.start()
        first = []
        for a in range(n):
            first.append(copy(a, 0, me, sibling, src=x_refs[a]))
            first += [copy(a, 1 + j, me, (*chip, c), src=x_refs[a]) for j, chip in enumerate(chips)]
        for cp in first:
            cp.start()
        passed = []
        for j, chip in enumerate(chips):
            for a in range(n):
                copy(a, 1 + j, (*chip, c), me).wait_recv()
                cp = copy(a, 4 + j, (*chip, c), sibling)
                cp.start()
                passed.append(cp)
        for a in range(n):
            copy(a, 0, sibling, me).wait_recv()
            for j, chip in enumerate(chips):
                copy(a, 4 + j, (*chip, 1 - c), me).wait_recv()
        for cp in first + passed:
            cp.wait_send()
        for cp in mine:
            cp.wait()

    return pl.pallas_call(
        body,
        name=name,
        in_specs=[ANY] * n,
        out_specs=[ANY] * n,
        out_shape=[jax.ShapeDtypeStruct((N_DEV, *v.shape), v.dtype) for v in xs],
        scratch_shapes=[pltpu.SemaphoreType.DMA((7 * n,)), pltpu.SemaphoreType.DMA((7 * n,)),
                        pltpu.SemaphoreType.DMA((n,))],
    )(*xs)


def _window_view(ref, dest):
    return ref.at[:, pl.ds(LANE * WIN_START[dest], WIN_W)]


def _halving_stage(xs, axis, name, windowed=()):
    n_arr = len(xs)
    metas = []
    for k, v in enumerate(xs):
        if k in windowed:
            metas.append((N_DEV // 2, v.shape[0], WIN_W))
        else:
            assert v.shape[1] == 2
            metas.append((v.shape[0], v.shape[2], v.shape[3]))
    chunk = [min(r, 1 << int(math.log2((1 << 17) // c))) for (_, r, c) in metas]
    assert all(r % ch == 0 and ch % 8 == 0 for ch, (_, r, _) in zip(chunk, metas))
    offs = [sum(m[0] for m in metas[:k]) for k in range(n_arr)]
    n_sem = sum(m[0] for m in metas)

    def body(*refs):
        x_refs = refs[:n_arr]
        o_refs = refs[n_arr:2 * n_arr]
        land_refs = refs[2 * n_arr:3 * n_arr]
        rest = refs[3 * n_arr:]
        bufs = rest[:3 * n_arr]
        send_sems, recv_sems, in_sems, out_sems = rest[3 * n_arr:]
        pos = dict(zip("xyc", _position()))
        bit = pos[axis]
        peer = tuple(1 - pos[a] if a == axis else pos[a] for a in "xyc")

        def view(k, i, b):
            if k in windowed:
                return _window_view(x_refs[k], 2 * i + b)
            return x_refs[k].at[i, b]

        def add_blocks(k, a_view, b_view, o_view):
            _, rows, _ = metas[k]
            ch = chunk[k]
            nch = rows // ch
            va, vb, vo = bufs[3 * k:3 * k + 3]

            def rows_of(j):
                return pl.ds(pl.multiple_of(j * ch, 8), ch)

            def loads(j, s):
                return (pltpu.make_async_copy(a_view.at[rows_of(j), :], va.at[s], in_sems.at[0, s]),
                        pltpu.make_async_copy(b_view.at[rows_of(j), :], vb.at[s], in_sems.at[1, s]))

            def store(j, s):
                return pltpu.make_async_copy(vo.at[s], o_view.at[rows_of(j), :], out_sems.at[s])

            for cp in loads(0, 0):
                cp.start()

            def step(j, _):
                s = lax.rem(j, 2)

                @pl.when(j + 1 < nch)
                def _():
                    for cp in loads(j + 1, 1 - s):
                        cp.start()

                for cp in loads(j, s):
                    cp.wait()

                @pl.when(j >= 2)
                def _():
                    store(j - 2, s).wait()

                vo[s] = va[s] + vb[s]
                store(j, s).start()
                return 0

            lax.fori_loop(0, nch, step, 0)
            for j in range(max(0, nch - 2), nch):
                store(j, j % 2).wait()

        for b in (0, 1):
            @pl.when(bit == b)
            def _(b=b):
                sends = []
                for k in range(n_arr):
                    for i in range(metas[k][0]):
                        cp = pltpu.make_async_remote_copy(
                            src_ref=view(k, i, 1 - b), dst_ref=land_refs[k].at[i], send_sem=send_sems.at[offs[k] + i],
                            recv_sem=recv_sems.at[offs[k] + i], device_id=peer, device_id_type=MESH)
                        cp.start()
                        sends.append(cp)
                idx = 0
                for k in range(n_arr):
                    for i in range(metas[k][0]):
                        sends[idx].wait_recv()
                        add_blocks(k, view(k, i, b), land_refs[k].at[i], o_refs[k].at[i])
                        idx += 1
                for cp in sends:
                    cp.wait_send()

    out_shape = [jax.ShapeDtypeStruct(m, F32) for m in metas]
    scratch = []
    for k in range(n_arr):
        scratch += [pltpu.VMEM((2, chunk[k], metas[k][2]), F32)] * 3
    scratch += [pltpu.SemaphoreType.DMA((n_sem,)), pltpu.SemaphoreType.DMA((n_sem,)),
                pltpu.SemaphoreType.DMA((2, 2)), pltpu.SemaphoreType.DMA((2,))]
    outs = pl.pallas_call(
        body,
        name=name,
        in_specs=[ANY] * n_arr,
        out_specs=[ANY] * (2 * n_arr),
        out_shape=out_shape + out_shape,
        scratch_shapes=scratch,
    )(*xs)
    return outs[:n_arr]


def _reduce_scatter(dw_al, blocks):
    xs = [dw_al] + [b.reshape(N_DEV // 2, 2, *b.shape[1:]) for b in blocks]
    ys = _halving_stage(xs, "c", "rs_c", windowed=(0,))
    ys = _halving_stage([v.reshape(2, 2, *v.shape[1:]) for v in ys], "y", "rs_y")
    ys = _halving_stage([v.reshape(1, 2, *v.shape[1:]) for v in ys], "x", "rs_x")
    return [v[0] for v in ys]


def _sum_slots(g):
    def body(g_ref, o_ref):
        acc = g_ref[0]
        for d in range(1, N_DEV):
            acc = acc + g_ref[d]
        o_ref[...] = acc

    return pl.pallas_call(body, name="sum_slots", out_shape=jax.ShapeDtypeStruct(g.shape[1:], g.dtype))(g)


def _assemble_w_al(wins, bas):
    ba_tile = O_BA // LANE
    assert W_AL // LANE == ba_tile + 1

    def tiles(d, t, n):
        return wins[d, :, LANE * (t - WIN_START[d]):LANE * (t - WIN_START[d] + n)]

    pieces = []
    run = None
    for t in range(ba_tile + 1):
        owners = [d for d in range(N_DEV) if WIN_START[d] <= t < WIN_START[d] + WIN_TILES] if t < ba_tile else []
        if run is not None and (len(owners) != 1 or owners[0] != run[0]):
            pieces.append(tiles(*run))
            run = None
        if len(owners) == 1:
            run = (owners[0], t, 1) if run is None else (run[0], run[1], run[2] + 1)
        elif len(owners) == 2:
            pieces.append(tiles(owners[0], t, 1) + tiles(owners[1], t, 1))
    pieces.append(bas[BA_DEV])
    return jnp.concatenate(pieces, axis=1)


def _adamw(w, g, m, v, name):
    r, c = w.shape
    tm = r if r <= 256 else 256
    assert r % tm == 0

    def body(w_ref, g_ref, m_ref, v_ref, d_ref, nm_ref, nv_ref):
        gg = g_ref[...]
        m_new = ADAM_B1 * m_ref[...] + (1.0 - ADAM_B1) * gg
        v_new = ADAM_B2 * v_ref[...] + (1.0 - ADAM_B2) * (gg * gg)
        m_hat = m_new / (1.0 - ADAM_B1 ** ADAM_STEP)
        v_hat = v_new / (1.0 - ADAM_B2 ** ADAM_STEP)
        d_ref[...] = -ADAM_LR * (m_hat / (jnp.sqrt(v_hat) + ADAM_EPS) + ADAM_WD * w_ref[...])
        nm_ref[...] = m_new
        nv_ref[...] = v_new

    blk = pl.BlockSpec((tm, c), lambda i: (i, 0))
    o = jax.ShapeDtypeStruct((r, c), F32)
    return pl.pallas_call(body, name=name, grid=(r // tm,), in_specs=[blk] * 4, out_specs=[blk] * 3,
                          out_shape=[o, o, o])(w, g, m, v)


def _select(me, table):
    return sum(jnp.where(me == d, jnp.int32(v), jnp.int32(0)) for d, v in enumerate(table))


WIN_SHIFT = tuple(SHARD_W * d - LANE * WIN_START[d] for d in range(N_DEV))
PAD_L = 256
PAD_R = 256


def _shard_to_window(shard, me):
    shift = _select(me, WIN_SHIFT)
    start = _select(me, WIN_START)
    padded = jnp.pad(shard, ((0, 0), (PAD_L, PAD_R)))
    rows = shard.shape[0]
    lo = lax.dynamic_slice(padded, (0, PAD_L - shift), (rows, WIN_W))
    hi = lax.dynamic_slice(padded, (0, PAD_L - shift + N_BA), (rows, WIN_W))
    aligned = LANE * start + lax.broadcasted_iota(jnp.int32, (1, WIN_W), 1)
    return jnp.where(aligned >= ORIG_BA, hi, lo)


def _window_to_shard(win, ba_grad, me):
    shift = _select(me, WIN_SHIFT)
    rows = win.shape[0]
    padded = jnp.pad(win, ((0, 0), (N_BA, PAD_R)))
    lo = lax.dynamic_slice(padded, (0, N_BA + shift), (rows, SHARD_W))
    hi = lax.dynamic_slice(padded, (0, shift), (rows, SHARD_W))
    orig = SHARD_W * me + lax.broadcasted_iota(jnp.int32, (1, SHARD_W), 1)
    ba_full = lax.dynamic_update_slice(jnp.zeros((rows, SHARD_W), win.dtype), ba_grad, (0, BA_LOCAL))
    return jnp.where(orig < ORIG_BA, lo, jnp.where(orig >= ORIG_BA + N_BA, hi, ba_full))


def _pad_row(v, width=D_MODEL):
    v = v.reshape(1, -1)
    return jnp.pad(v, ((0, 0), (0, width - v.shape[1])))


def _slab(v, rows=8):
    return jnp.pad(v, ((0, rows - v.shape[0]), (0, D_MODEL - v.shape[1])))


def kernel(x, mem, norm_g, mem_norm_g, w_in, conv_w, a_log, dt_bias, dn_norm_g, w_mem_kv, w_br_dn, w_br_sb, w_br_mem, w_out, final_g, loss_target, m_norm_g, m_mem_norm_g, m_w_in, m_conv_w, m_a_log, m_dt_bias, m_dn_norm_g, m_w_mem_kv, m_w_br_dn, m_w_br_sb, m_w_br_mem, m_w_out, m_final_g, v_norm_g, v_mem_norm_g, v_w_in, v_conv_w, v_a_log, v_dt_bias, v_dn_norm_g, v_w_mem_kv, v_w_br_dn, v_w_br_sb, v_w_br_mem, v_w_out, v_final_g):
    xi, yi, ci = _position()
    me = 4 * xi + 2 * yi + ci

    shard = w_in[0]
    win = _shard_to_window(shard, me).astype(BF16)
    ba = jnp.pad(shard[:, BA_LOCAL:BA_LOCAL + N_BA], ((0, 0), (0, LANE - N_BA))).astype(BF16)
    g_win, g_ba, g_kv, g_dn, g_sb, g_out, g_mem, g_conv = _all_gather(
        [win, ba, w_mem_kv[0].astype(BF16), w_br_dn[0].astype(BF16), w_br_sb[0].astype(BF16), w_out[0].astype(BF16),
         w_br_mem[0].astype(BF16), conv_w[0]], "gather_weights")
    w_al = _assemble_w_al(g_win, g_ba)
    w_mem_kv_f = g_kv.reshape(D_MODEL, 2 * MEM_W)
    w_br_dn_f = g_dn.reshape(D_MODEL, D_MODEL)
    w_br_sb_f = g_sb.reshape(D_MODEL, D_MODEL)
    w_out_f = g_out.reshape(D_MODEL, D_MODEL)
    w_br_mem_f = g_mem.transpose(1, 0, 2).reshape(MEM_W, D_MODEL)
    conv_w_f = g_conv.transpose(1, 0, 2).reshape(CONV_K, 3 * D_MODEL)

    r = _local_step(x[0], mem[0], loss_target[0], norm_g, mem_norm_g, w_al, conv_w_f, _pad_row(a_log, LANE),
                    _pad_row(dt_bias, LANE), dn_norm_g, w_mem_kv_f, w_br_dn_f, w_br_sb_f, w_br_mem_f, w_out_f,
                    final_g.reshape(1, D_MODEL))

    dw_al = r["w_al"]
    g_win, g_kv, g_dn, g_sb, g_out, g_mem = _reduce_scatter(dw_al, [
        r["w_mem_kv"].reshape(N_DEV, D_MODEL // N_DEV, 2 * MEM_W),
        r["w_br_dn"].reshape(N_DEV, D_MODEL // N_DEV, D_MODEL),
        r["w_br_sb"].reshape(N_DEV, D_MODEL // N_DEV, D_MODEL),
        r["w_out"].reshape(N_DEV, D_MODEL // N_DEV, D_MODEL),
        r["w_br_mem"].reshape(MEM_W, N_DEV, D_MODEL // N_DEV).transpose(1, 0, 2)])
    small = jnp.concatenate([
        _slab(r["norm_g"]), _slab(r["mem_norm_g"]), _slab(r["final_g"]), _slab(r["dn_norm_g"]), _slab(r["scal"]),
        _slab(r["loss"][:, :1]), _slab(r["conv_w"].reshape(CONV_K * 3, D_MODEL), 16),
        dw_al[:, O_BA:O_BA + N_BA].T], axis=0)
    (g_small,) = _all_gather([small], "gather_small")
    small = _sum_slots(g_small)
    g_norm_g, g_mem_norm_g, g_final_g = small[0:1], small[8:9], small[16]
    g_dn_norm_g = small[24:25, :LANE]
    g_a_log, g_dt_bias = small[32:33, :N_HEADS], small[33:34, :N_HEADS]
    loss = small[40, 0]
    g_conv_full = small[48:48 + CONV_K * 3].reshape(CONV_K, 3 * D_MODEL)
    cw = conv_w.shape[2]
    g_conv = lax.dynamic_slice(g_conv_full, (0, cw * me), (CONV_K, cw))
    g_w_in = _window_to_shard(g_win, small[64:64 + N_BA].T, me)

    def adam(name, w, g, m, v):
        shp = w.shape
        w2, g2, m2, v2 = (a.reshape(-1, shp[-1]) for a in (w, g, m, v))
        return tuple(o.reshape(shp) for o in _adamw(w2, g2, m2, v2, "adamw_" + name))

    def pack(vals):
        return jnp.concatenate([_slab(a.reshape(1, -1)) for a in vals], axis=0)

    smalls = (norm_g, mem_norm_g, final_g, dn_norm_g, a_log, dt_bias)
    p_d, p_m, p_v = _adamw(pack(smalls), pack((g_norm_g, g_mem_norm_g, g_final_g, g_dn_norm_g, g_a_log, g_dt_bias)),
                           pack((m_norm_g, m_mem_norm_g, m_final_g, m_dn_norm_g, m_a_log, m_dt_bias)),
                           pack((v_norm_g, v_mem_norm_g, v_final_g, v_dn_norm_g, v_a_log, v_dt_bias)), "adamw_small")

    def unpack(p):
        return [p[8 * i, :a.size].reshape(a.shape) for i, a in enumerate(smalls)]

    sd, sm, sv = unpack(p_d), unpack(p_m), unpack(p_v)
    grads = dict(norm_g=g_norm_g, mem_norm_g=g_mem_norm_g, w_in=g_w_in[None], conv_w=g_conv[None], a_log=g_a_log,
                 dt_bias=g_dt_bias, dn_norm_g=g_dn_norm_g, w_mem_kv=g_kv[None], w_br_dn=g_dn[None], w_br_sb=g_sb[None],
                 w_br_mem=g_mem[None], w_out=g_out[None], final_g=g_final_g)
    big = dict(w_in=(w_in, m_w_in, v_w_in), conv_w=(conv_w, m_conv_w, v_conv_w), w_mem_kv=(w_mem_kv, m_w_mem_kv, v_w_mem_kv),
               w_br_dn=(w_br_dn, m_w_br_dn, v_w_br_dn), w_br_sb=(w_br_sb, m_w_br_sb, v_w_br_sb),
               w_br_mem=(w_br_mem, m_w_br_mem, v_w_br_mem), w_out=(w_out, m_w_out, v_w_out))
    order = ["norm_g", "mem_norm_g", "w_in", "conv_w", "a_log", "dt_bias", "dn_norm_g", "w_mem_kv", "w_br_dn", "w_br_sb",
             "w_br_mem", "w_out", "final_g"]
    small_idx = {"norm_g": 0, "mem_norm_g": 1, "final_g": 2, "dn_norm_g": 3, "a_log": 4, "dt_bias": 5}
    deltas, new_m, new_v = {}, {}, {}
    for nm in order:
        if nm in big:
            w, m, v = big[nm]
            deltas[nm], new_m[nm], new_v[nm] = adam(nm, w, grads[nm], m, v)
        else:
            i = small_idx[nm]
            deltas[nm], new_m[nm], new_v[nm] = sd[i], sm[i], sv[i]
    return (loss, r["grad_x"][None], *[grads[nm] for nm in order], *[deltas[nm] for nm in order],
            *[new_m[nm] for nm in order], *[new_v[nm] for nm in order])
```

```python
import functools
import math

import jax
import jax.numpy as jnp
from jax import lax
from jax.experimental import pallas as pl
from jax.experimental.pallas import tpu as pltpu

F32 = jnp.float32
BF16 = jnp.bfloat16

D_MODEL = 1024
N_DEV = 8
N_HEADS = 8
D_HEAD = 128
DN_CHUNK = 64
CONV_K = 4
MEM_LEN = 256
MEM_HEADS = 4
MEM_DH = 64
MEM_W = MEM_HEADS * MEM_DH
NORM_EPS = 1e-6
IN_WIDTH = 11792
SHARD_W = IN_WIDTH // N_DEV

LANE = 128
SUPER = 2 * DN_CHUNK

O_QKV_DN = 0
O_Z_DN = 3072
O_QKV_SB = 4096
O_Z_SB = 7168
O_MQ = 8192
O_MZ = 8448
O_GATES = 8704
O_BA = 11776
W_AL = 11904
ORIG_BA = 4096
N_BA = 16

WIN_TILES = 13
WIN_W = WIN_TILES * LANE


def _aligned_col(o):
    return o if o < ORIG_BA else o - N_BA


WIN_START = tuple(min(_aligned_col(SHARD_W * d) // LANE, (W_AL // LANE) - WIN_TILES) for d in range(N_DEV))
WIN_OFF = tuple(_aligned_col(SHARD_W * d) - LANE * WIN_START[d] if SHARD_W * d >= ORIG_BA + N_BA or SHARD_W * d < ORIG_BA
                else None for d in range(N_DEV))
BA_DEV = ORIG_BA // SHARD_W
BA_LOCAL = ORIG_BA - BA_DEV * SHARD_W

ADAM_LR = 0.001
ADAM_B1 = 0.9
ADAM_B2 = 0.999
ADAM_EPS = 1e-08
ADAM_WD = 0.01
ADAM_STEP = 10

NN = (((1,), (0,)), ((), ()))
NT = (((1,), (1,)), ((), ()))
TN = (((0,), (0,)), ((), ()))


def _dot(a, b, dims):
    return lax.dot_general(a.astype(BF16), b.astype(BF16), dims, preferred_element_type=F32)


def _split2(a):
    hi = a.astype(BF16)
    lo = (a - hi.astype(F32)).astype(BF16)
    return hi, lo


def _dot3(a, b, dims):
    ah, al = _split2(a)
    bh, bl = _split2(b)
    d = functools.partial(lax.dot_general, dimension_numbers=dims, preferred_element_type=F32)
    return d(ah, bh) + (d(ah, bl) + d(al, bh))


def _sel_dot_impl(sel01, x, dims):
    sel = sel01.astype(BF16)
    h1 = x.astype(BF16)
    r1 = x - h1.astype(F32)
    h2 = r1.astype(BF16)
    h3 = (r1 - h2.astype(F32)).astype(BF16)
    d = functools.partial(lax.dot_general, dimension_numbers=dims, preferred_element_type=F32)
    return d(sel, h1) + (d(sel, h2) + d(sel, h3))


@jax.custom_vjp
def _sel_dot(sel01, x):
    return _sel_dot_impl(sel01, x, NN)


_sel_dot.defvjp(lambda s, x: (_sel_dot(s, x), s),
                lambda s, g: (jnp.zeros_like(s), _sel_dot_impl(s, g, TN)))


def _make_mm(dotfn):
    @jax.custom_vjp
    def nn(a, b):
        return dotfn(a, b, NN)

    @jax.custom_vjp
    def nt(a, b):
        return dotfn(a, b, NT)

    @jax.custom_vjp
    def tn(a, b):
        return dotfn(a, b, TN)

    nn.defvjp(lambda a, b: (nn(a, b), (a, b)), lambda r, g: (nt(g, r[1]), tn(r[0], g)))
    nt.defvjp(lambda a, b: (nt(a, b), (a, b)), lambda r, g: (nn(g, r[1]), tn(g, r[0])))
    tn.defvjp(lambda a, b: (tn(a, b), (a, b)), lambda r, g: (nt(r[1], g), nn(r[0], g)))
    return nn, nt, tn


mm_nn, mm_nt, mm_tn = _make_mm(_dot)
mm3_nn, mm3_nt, mm3_tn = _make_mm(_dot3)


def _sigmoid(x):
    return jax.nn.sigmoid(x)


def _silu(x):
    return x * _sigmoid(x)


def _softplus_parts(x):
    sp = jnp.log1p(jnp.exp(-jnp.abs(x)))
    return jnp.maximum(x, 0.0) + sp, jnp.maximum(-x, 0.0) + sp


def _rmsnorm(x, g):
    return x * lax.rsqrt(jnp.mean(x * x, axis=-1, keepdims=True) + NORM_EPS) * g


def _iota2(shape, dim):
    return lax.broadcasted_iota(jnp.int32, shape, dim)


def _div64(i):
    return lax.shift_right_logical(i, jnp.full(i.shape, 6, jnp.int32))


def _each(f, *lists):
    return [f(*a) for a in zip(*lists)]


@jax.custom_vjp
def _inv_unit_lower(ms):
    n = ms[0].shape[0]
    eye = (_iota2((n, n), 0) == _iota2((n, n), 1)).astype(F32)
    rs = [eye - m for m in ms]
    ps = ms
    for _ in range(5):
        ps = _each(mm3_nn, ps, ps)
        rs = _each(lambda r, p: r + mm3_nn(r, p), rs, ps)
    return rs


def _inv_fwd(ms):
    rs = _inv_unit_lower(ms)
    return rs, rs


def _inv_bwd(rs, gs):
    ts = _each(mm3_tn, rs, gs)
    return (_each(lambda t, r: -mm3_nt(t, r), ts, rs),)


_inv_unit_lower.defvjp(_inv_fwd, _inv_bwd)


def _dn_block(cq, ck, cv, bcol, acol, zt, alog, dtb, gn, s0):
    n = SUPER
    h = DN_CHUNK
    row = _iota2((n, n), 0)
    col = _iota2((n, n), 1)
    same = _div64(row) == _div64(col)
    incl = jnp.logical_and(same, row >= col)
    strict = jnp.logical_and(same, row > col)
    incl_f = incl.astype(F32)

    qn = _each(lambda x: x * lax.rsqrt(jnp.sum(x * x, axis=-1, keepdims=True) + NORM_EPS) * (D_HEAD ** -0.5), cq)
    kn = _each(lambda x: x * lax.rsqrt(jnp.sum(x * x, axis=-1, keepdims=True) + NORM_EPS), ck)
    beta = _each(_sigmoid, bcol)
    g = _each(lambda al, ac, dt: -(jnp.exp(al) * _softplus_parts(ac + dt)[0]), alog, acol, dtb)
    gcum = _each(lambda x: _sel_dot(incl_f, jnp.broadcast_to(x, (n, n))), g)
    gam_incl = _each(lambda x: jnp.where(incl, jnp.exp(jnp.where(incl, x - x.T, 0.0)), 0.0), gcum)
    kk = _each(mm_nt, kn, kn)
    t_inv = _inv_unit_lower(_each(lambda b, x, gm: b * x * jnp.where(strict, gm, 0.0), beta, kk, gam_incl))
    eg = _each(jnp.exp, gcum)
    u = _each(lambda t, v, b: mm_nn(t, v * b), t_inv, cv, beta)
    w = _each(lambda t, k, b, e: mm_nn(t, k * (b * e)), t_inv, kn, beta, eg)
    a_intra = _each(lambda q, k, gm: mm_nt(q, k) * gm, qn, kn, gam_incl)
    q_dec = _each(lambda q, e: q * e, qn, eg)
    last0 = _each(lambda x: x[h - 1:h, :], gcum)
    last1 = _each(lambda x: x[n - 1:n, :], gcum)
    k_dec = _each(lambda k, x, l0, l1: k * jnp.exp(jnp.concatenate(
        [jnp.broadcast_to(l0, (h, n)), jnp.broadcast_to(l1, (h, n))], axis=0) - x), kn, gcum, last0, last1)
    v0 = _each(lambda uu, ww, s: uu[:h] - mm_nn(ww[:h], s), u, w, s0)
    o0 = _each(lambda q, s: mm_nn(q[:h], s), q_dec, s0)
    s1 = _each(lambda s, l0, k, v: s * jnp.exp(l0) + mm_tn(k[:h], v), s0, last0, k_dec, v0)
    v1 = _each(lambda uu, ww, s: uu[h:] - mm_nn(ww[h:], s), u, w, s1)
    o1 = _each(lambda q, s: mm_nn(q[h:], s), q_dec, s1)
    s2 = _each(lambda s, l1, k, v: s * jnp.exp(l1) + mm_tn(k[h:], v), s1, last1, k_dec, v1)
    o = _each(lambda a, b, am, x, y: jnp.concatenate([a, b], axis=0) + mm_nn(am, jnp.concatenate([x, y], axis=0)),
              o0, o1, a_intra, v0, v1)
    out = _each(lambda x, z: _rmsnorm(x, gn) * _silu(z), o, zt)
    return out, s2


def _mem_fn(mq, mz, mkv):
    mk = mkv[:, :MEM_W]
    mv = mkv[:, MEM_W:]
    lane = _iota2((1, MEM_W), 1)
    out = jnp.zeros(mq.shape, F32)
    for hd in range(MEM_HEADS):
        hm = (_div64(lane) == hd).astype(F32)
        s = mm_nt(mq * hm, mk) * (1.0 / math.sqrt(MEM_DH))
        s = s - jnp.max(s, axis=-1, keepdims=True)
        e = jnp.exp(s)
        p = e / jnp.sum(e, axis=-1, keepdims=True)
        out = out + mm_nn(p, mv) * hm
    return out * _silu(mz)


def _merge_fn(gd, gs, gm, yd, ys, ym):
    return _sigmoid(gd) * yd + _sigmoid(gs) * ys + _sigmoid(gm) * ym


def _loss_fn(x, mo, fg, tgt):
    y = _rmsnorm(x + mo, fg)
    err = y - tgt
    return 0.5 * jnp.sum(jnp.mean(err * err, axis=-1, keepdims=True), axis=0, keepdims=True)


def _matmul(a, b, mode, out_dtype, tm, tn, tk, name, b_col0=0, n_cols=None):
    if mode == "nn":
        m, kdim = a.shape
        n = b.shape[1] if n_cols is None else n_cols
    elif mode == "nt":
        m, kdim = a.shape
        n = b.shape[0]
    else:
        kdim, m = a.shape
        n = b.shape[1] if n_cols is None else n_cols
    tm, tn, tk = min(tm, m), min(tn, n), min(tk, kdim)
    assert m % tm == 0 and n % tn == 0 and kdim % tk == 0 and b_col0 % tn == 0
    nk = kdim // tk
    jb = b_col0 // tn
    dims = {"nn": NN, "nt": NT, "tn": TN}[mode]

    def body(a_ref, b_ref, o_ref, acc_ref):
        k = pl.program_id(2)
        part = _dot(a_ref[...], b_ref[...], dims)

        @pl.when(k == 0)
        def _():
            acc_ref[...] = part

        @pl.when(k > 0)
        def _():
            acc_ref[...] += part

        @pl.when(k == nk - 1)
        def _():
            o_ref[...] = acc_ref[...].astype(o_ref.dtype)

    if mode == "nn":
        a_spec = pl.BlockSpec((tm, tk), lambda i, j, k: (i, k))
        b_spec = pl.BlockSpec((tk, tn), lambda i, j, k: (k, j + jb))
    elif mode == "nt":
        a_spec = pl.BlockSpec((tm, tk), lambda i, j, k: (i, k))
        b_spec = pl.BlockSpec((tn, tk), lambda i, j, k: (j, k))
    else:
        a_spec = pl.BlockSpec((tk, tm), lambda i, j, k: (k, i))
        b_spec = pl.BlockSpec((tk, tn), lambda i, j, k: (k, j + jb))
    return pl.pallas_call(
        body,
        name=name,
        grid=(m // tm, n // tn, nk),
        in_specs=[a_spec, b_spec],
        out_specs=pl.BlockSpec((tm, tn), lambda i, j, k: (i, j)),
        out_shape=jax.ShapeDtypeStruct((m, n), out_dtype),
        scratch_shapes=[pltpu.VMEM((tm, tn), F32)],
        compiler_params=pltpu.CompilerParams(dimension_semantics=("parallel", "parallel", "arbitrary")),
    )(a, b)


def _norm_in(x, g, tm=256):
    t = x.shape[0]

    def body(x_ref, g_ref, h_ref, ht_ref):
        h = _rmsnorm(x_ref[...], g_ref[...])
        h_ref[...] = h.astype(BF16)
        ht_ref[...] = h.T.astype(BF16)

    return pl.pallas_call(
        body,
        name="norm_in",
        grid=(t // tm,),
        in_specs=[pl.BlockSpec((tm, D_MODEL), lambda i: (i, 0)), pl.BlockSpec((1, D_MODEL), lambda i: (0, 0))],
        out_specs=[pl.BlockSpec((tm, D_MODEL), lambda i: (i, 0)), pl.BlockSpec((D_MODEL, tm), lambda i: (0, i))],
        out_shape=[jax.ShapeDtypeStruct((t, D_MODEL), BF16), jax.ShapeDtypeStruct((D_MODEL, t), BF16)],
    )(x, g)


def _norm_in_bwd(x, g, dh, dres, tm=256):
    t = x.shape[0]

    def body(x_ref, g_ref, dh_ref, dres_ref, dx_ref, dg_ref):
        _, vjp = jax.vjp(_rmsnorm, x_ref[...], g_ref[...])
        dx, dg = vjp(dh_ref[...])
        dx_ref[...] = dx + dres_ref[...]

        @pl.when(pl.program_id(0) == 0)
        def _():
            dg_ref[...] = jnp.zeros_like(dg_ref)

        dg_ref[...] += dg

    row = pl.BlockSpec((tm, D_MODEL), lambda i: (i, 0))
    vec = pl.BlockSpec((1, D_MODEL), lambda i: (0, 0))
    return pl.pallas_call(
        body,
        name="norm_in_bwd",
        grid=(t // tm,),
        in_specs=[row, vec, row, row],
        out_specs=[row, vec],
        out_shape=[jax.ShapeDtypeStruct((t, D_MODEL), F32), jax.ShapeDtypeStruct((1, D_MODEL), F32)],
    )(x, g, dh, dres)


def _merge(proj, yd, ys, ym, tm=256, tc=512):
    t = proj.shape[0]
    g0 = O_GATES // tc
    gstep = D_MODEL // tc

    def body(gd, gs, gm, yd_ref, ys_ref, ym_ref, o_ref):
        o_ref[...] = _merge_fn(gd[...], gs[...], gm[...], yd_ref[...], ys_ref[...], ym_ref[...]).astype(BF16)

    def gate(k):
        return pl.BlockSpec((tm, tc), lambda i, j: (i, g0 + k * gstep + j))

    blk = pl.BlockSpec((tm, tc), lambda i, j: (i, j))
    return pl.pallas_call(
        body,
        name="merge",
        grid=(t // tm, D_MODEL // tc),
        in_specs=[gate(0), gate(1), gate(2), blk, blk, blk],
        out_specs=blk,
        out_shape=jax.ShapeDtypeStruct((t, D_MODEL), BF16),
    )(proj, proj, proj, yd, ys, ym)


def _merge_bwd(proj, yd, ys, ym, dmerged, tm=256, tc=512):
    t = proj.shape[0]
    g0 = O_GATES // tc
    gstep = D_MODEL // tc

    def body(gd, gs, gm, yd_ref, ys_ref, ym_ref, dm_ref, dyd, dys, dym, dgd, dgs, dgm):
        _, vjp = jax.vjp(_merge_fn, gd[...], gs[...], gm[...], yd_ref[...], ys_ref[...], ym_ref[...])
        outs = vjp(dm_ref[...])
        for ref, val in zip((dgd, dgs, dgm, dyd, dys, dym), outs):
            ref[...] = val.astype(BF16)

    def gate(k):
        return pl.BlockSpec((tm, tc), lambda i, j: (i, g0 + k * gstep + j))

    blk = pl.BlockSpec((tm, tc), lambda i, j: (i, j))
    o = jax.ShapeDtypeStruct((t, D_MODEL), BF16)
    return pl.pallas_call(
        body,
        name="merge_bwd",
        grid=(t // tm, D_MODEL // tc),
        in_specs=[gate(0), gate(1), gate(2), blk, blk, blk, blk],
        out_specs=[blk] * 6,
        out_shape=[o] * 6,
    )(proj, proj, proj, yd, ys, ym, dmerged)


def _loss_head(x, mo, fg, tgt, tm=256):
    t = x.shape[0]

    def body(x_ref, mo_ref, fg_ref, t_ref, loss_ref, dout_ref, dfg_ref):
        loss, vjp = jax.vjp(_loss_fn, x_ref[...], mo_ref[...], fg_ref[...], t_ref[...])
        _, dmo, dfg, _ = vjp(jnp.ones((1, 1), F32))

        @pl.when(pl.program_id(0) == 0)
        def _():
            loss_ref[...] = jnp.zeros_like(loss_ref)
            dfg_ref[...] = jnp.zeros_like(dfg_ref)

        loss_ref[...] += jnp.broadcast_to(loss, loss_ref.shape)
        dfg_ref[...] += dfg
        dout_ref[...] = dmo

    row = pl.BlockSpec((tm, D_MODEL), lambda i: (i, 0))
    vec = pl.BlockSpec((1, D_MODEL), lambda i: (0, 0))
    return pl.pallas_call(
        body,
        name="loss_head",
        grid=(t // tm,),
        in_specs=[row, row, vec, row],
        out_specs=[pl.BlockSpec((1, LANE), lambda i: (0, 0)), row, vec],
        out_shape=[jax.ShapeDtypeStruct((1, LANE), F32), jax.ShapeDtypeStruct((t, D_MODEL), F32),
                   jax.ShapeDtypeStruct((1, D_MODEL), F32)],
    )(x, mo, fg, tgt)


def _shift_rows(x, s):
    t = x.shape[0]
    if s == 0:
        return x
    rolled = pltpu.roll(x, s % t, 0)
    row = _iota2(x.shape, 0)
    keep = row >= s if s > 0 else row < t + s
    return jnp.where(keep, rolled, 0.0)


def _conv_pre(x, w):
    return sum(_shift_rows(x, CONV_K - 1 - j) * w[j:j + 1, :] for j in range(CONV_K))


def _dn_conv(proj, conv_w):
    t = proj.shape[0]
    nb = 3 * D_MODEL // LANE

    def body(x_ref, w_ref, c_ref):
        c_ref[...] = _silu(_conv_pre(x_ref[...], w_ref[...]))

    return pl.pallas_call(
        body,
        name="dn_conv",
        grid=(nb,),
        in_specs=[pl.BlockSpec((t, LANE), lambda j: (0, j)), pl.BlockSpec((CONV_K, LANE), lambda j: (0, j))],
        out_specs=pl.BlockSpec((t, LANE), lambda j: (0, j)),
        out_shape=jax.ShapeDtypeStruct((t, 3 * D_MODEL), F32),
    )(proj, conv_w)


def _dn_conv_bwd(proj, conv_w, dc, part):
    t = proj.shape[0]
    nb = D_MODEL // LANE
    b0 = part * nb

    def body(x_ref, w_ref, dc_ref, dx_ref, dw_ref):
        x = x_ref[...]
        w = w_ref[...]
        pre = _conv_pre(x, w)
        sg = _sigmoid(pre)
        dpre = dc_ref[...] * (sg * (1.0 + pre * (1.0 - sg)))
        dx = sum(_shift_rows(dpre, -(CONV_K - 1 - j)) * w[j:j + 1, :] for j in range(CONV_K))
        dx_ref[...] = dx.astype(BF16)
        dw_ref[...] = jnp.concatenate(
            [jnp.sum(dpre * _shift_rows(x, CONV_K - 1 - j), axis=0, keepdims=True) for j in range(CONV_K)], axis=0)

    return pl.pallas_call(
        body,
        name=f"dn_conv_bwd{part}",
        grid=(nb,),
        in_specs=[pl.BlockSpec((t, LANE), lambda j: (0, b0 + j)), pl.BlockSpec((CONV_K, LANE), lambda j: (0, b0 + j)),
                  pl.BlockSpec((t, LANE), lambda j: (0, j))],
        out_specs=[pl.BlockSpec((t, LANE), lambda j: (0, j)), pl.BlockSpec((CONV_K, LANE), lambda j: (0, j))],
        out_shape=[jax.ShapeDtypeStruct((t, D_MODEL), BF16), jax.ShapeDtypeStruct((CONV_K, D_MODEL), F32)],
    )(proj, conv_w, dc)


def _ba_columns(ba, hd):
    lane = _iota2(ba.shape, 1)
    bcol = jnp.sum(jnp.where(lane == hd, ba, 0.0), axis=1, keepdims=True)
    acol = jnp.sum(jnp.where(lane == N_HEADS + hd, ba, 0.0), axis=1, keepdims=True)
    return bcol, acol


def _head_scalar(row, hd):
    lane = _iota2(row.shape, 1)
    return jnp.sum(jnp.where(lane == hd, row, 0.0), axis=1, keepdims=True)


DN_HP = 4


def _dn_inputs(cq, ck, cv, ba_ref, z_ref, alog_ref, dtb_ref, heads, lanes):
    ba = ba_ref[...]
    cols = [_ba_columns(ba, hd) for hd in heads]
    return ([cq[:, ln] for ln in lanes], [ck[:, ln] for ln in lanes], [cv[:, ln] for ln in lanes],
            [c[0] for c in cols], [c[1] for c in cols], [z_ref[:, ln] for ln in lanes],
            [_head_scalar(alog_ref[...], hd) for hd in heads], [_head_scalar(dtb_ref[...], hd) for hd in heads])


def _dn_specs(nblk, reverse):
    w = DN_HP * LANE
    nq = D_MODEL // w

    def row(i):
        return nblk - 1 - i if reverse else i

    def colblk(b0):
        return pl.BlockSpec((SUPER, w), lambda i, h: (row(i), b0 + h))

    ba = pl.BlockSpec((SUPER, LANE), lambda i, h: (row(i), O_BA // LANE))
    vec = pl.BlockSpec((1, LANE), lambda i, h: (0, 0))
    st = pl.BlockSpec((1, DN_HP, D_HEAD, D_HEAD), lambda i, h: (row(i), h, 0, 0))
    return colblk, nq, ba, vec, st


def _dn_fwd(c, proj, alog_row, dtb_row, gn):
    t = c.shape[0]
    nblk = t // SUPER
    colblk, nq, ba, vec, st = _dn_specs(nblk, False)

    def body(cq, ck, cv, ba_ref, z_ref, alog_ref, dtb_ref, gn_ref, o_ref, s_ref, state):
        @pl.when(jnp.logical_and(pl.program_id(0) == 0, pl.program_id(1) == 0))
        def _():
            state[...] = jnp.zeros_like(state)

        heads = [pl.program_id(1) * DN_HP + j for j in range(DN_HP)]
        lanes = [slice(j * LANE, (j + 1) * LANE) for j in range(DN_HP)]
        s0 = [state[hd] for hd in heads]
        outs, s2 = _dn_block(*_dn_inputs(cq, ck, cv, ba_ref, z_ref, alog_ref, dtb_ref, heads, lanes), gn_ref[...], s0)
        for j, (hd, ln) in enumerate(zip(heads, lanes)):
            s_ref[0, j] = s0[j]
            o_ref[:, ln] = outs[j].astype(BF16)
            state[hd] = s2[j]

    return pl.pallas_call(
        body,
        name="dn_fwd",
        grid=(nblk, N_HEADS // DN_HP),
        in_specs=[colblk(0), colblk(nq), colblk(2 * nq), ba, colblk(O_Z_DN // (DN_HP * LANE)), vec, vec, vec],
        out_specs=[colblk(0), st],
        out_shape=[jax.ShapeDtypeStruct((t, D_MODEL), BF16),
                   jax.ShapeDtypeStruct((nblk, N_HEADS, D_HEAD, D_HEAD), F32)],
        scratch_shapes=[pltpu.VMEM((N_HEADS, D_HEAD, D_HEAD), F32)],
    )(c, c, c, proj, proj, alog_row, dtb_row, gn)


def _dn_bwd(c, proj, alog_row, dtb_row, gn, states, do):
    t = c.shape[0]
    nblk = t // SUPER
    colblk, nq, ba, vec, st = _dn_specs(nblk, True)

    def body(cq, ck, cv, ba_ref, z_ref, alog_ref, dtb_ref, gn_ref, s_ref, do_ref,
             dq_ref, dk_ref, dv_ref, dz_ref, dba_ref, dsc_ref, dgn_ref, dstate):
        i = pl.program_id(0)
        hq = pl.program_id(1)

        @pl.when(jnp.logical_and(i == 0, hq == 0))
        def _():
            dstate[...] = jnp.zeros_like(dstate)
            dsc_ref[...] = jnp.zeros_like(dsc_ref)
            dgn_ref[...] = jnp.zeros_like(dgn_ref)

        @pl.when(hq == 0)
        def _():
            dba_ref[...] = jnp.zeros_like(dba_ref)

        lane = _iota2((SUPER, LANE), 1)
        lane1 = _iota2((1, LANE), 1)
        heads = [hq * DN_HP + j for j in range(DN_HP)]
        lanes = [slice(j * LANE, (j + 1) * LANE) for j in range(DN_HP)]
        ds_in = [dstate[hd] for hd in heads]
        s_in = [s_ref[0, j] for j in range(DN_HP)]
        _, vjp = jax.vjp(_dn_block, *_dn_inputs(cq, ck, cv, ba_ref, z_ref, alog_ref, dtb_ref, heads, lanes),
                         gn_ref[...], s_in)
        dq, dk, dv, dbc, dac, dz, dal, ddt, dgn, ds0 = vjp(([do_ref[:, ln].astype(F32) for ln in lanes], ds_in))
        dba = jnp.zeros((SUPER, LANE), F32)
        dal_row = jnp.zeros((1, LANE), F32)
        ddt_row = jnp.zeros((1, LANE), F32)
        for j, (hd, ln) in enumerate(zip(heads, lanes)):
            dq_ref[:, ln] = dq[j]
            dk_ref[:, ln] = dk[j]
            dv_ref[:, ln] = dv[j]
            dz_ref[:, ln] = dz[j].astype(BF16)
            dstate[hd] = ds0[j]
            dba = dba + jnp.where(lane == hd, dbc[j], 0.0) + jnp.where(lane == N_HEADS + hd, dac[j], 0.0)
            dal_row = dal_row + jnp.where(lane1 == hd, dal[j], 0.0)
            ddt_row = ddt_row + jnp.where(lane1 == hd, ddt[j], 0.0)
        dba_ref[...] += dba
        dsc_ref[0:1, :] += dal_row
        dsc_ref[1:2, :] += ddt_row
        dgn_ref[...] += dgn

    outs = pl.pallas_call(
        body,
        name="dn_bwd",
        grid=(nblk, N_HEADS // DN_HP),
        in_specs=[colblk(0), colblk(nq), colblk(2 * nq), ba, colblk(O_Z_DN // (DN_HP * LANE)), vec, vec, vec, st,
                  colblk(0)],
        out_specs=[colblk(0), colblk(0), colblk(0), colblk(0),
                   pl.BlockSpec((SUPER, LANE), lambda i, h: (nblk - 1 - i, 0)),
                   pl.BlockSpec((2, LANE), lambda i, h: (0, 0)), vec],
        out_shape=[jax.ShapeDtypeStruct((t, D_MODEL), F32)] * 3
        + [jax.ShapeDtypeStruct((t, D_MODEL), BF16), jax.ShapeDtypeStruct((t, LANE), F32),
           jax.ShapeDtypeStruct((2, LANE), F32), jax.ShapeDtypeStruct((1, LANE), F32)],
        scratch_shapes=[pltpu.VMEM((N_HEADS, D_HEAD, D_HEAD), F32)],
    )(c, c, c, proj, proj, alog_row, dtb_row, gn, states, do)
    return outs


SB_TQ = 256
SB_TK = 256
SB_HP = 2


def _sb_logits(z, mask):
    sp = jnp.log(1.0 + jnp.exp(-jnp.abs(z)))
    lf_raw = -(jnp.maximum(z, 0.0) + sp)
    lb = lf_raw + z
    lf = lf_raw if mask is None else jnp.where(mask, lf_raw, 0.0)
    return lb, lf_raw, lf


def _suffix_sums(x, sel):
    hi, lo = _split2(x)
    d = functools.partial(lax.dot_general, dimension_numbers=NN, preferred_element_type=F32)
    return d(hi, sel) + d(lo, sel)


def _sb_diag_mask(tq, r):
    return r * SB_TK + _iota2((tq, SB_TK), 1) < _iota2((tq, SB_TK), 0)


def _sb_specs(t, tq):
    w = SB_HP * LANE
    q0, k0, v0, z0 = (O_QKV_SB // w, (O_QKV_SB + D_MODEL) // w, (O_QKV_SB + 2 * D_MODEL) // w, O_Z_SB // w)

    def blk(b0):
        return pl.BlockSpec((tq, w), lambda h, i: (i, b0 + h))

    def full(b0):
        return pl.BlockSpec((t, w), lambda h, i: (0, b0 + h))

    return blk(q0), full(k0), full(v0), blk(z0), blk(0), full(0)


def _sb_fwd(proj):
    t = proj.shape[0]
    tq = min(SB_TQ, t)
    ndiag = tq // SB_TK
    scale = 1.0 / math.sqrt(D_HEAD)

    def body(q_ref, k_ref, v_ref, z_ref, o_ref, oraw_ref):
        qi = pl.program_id(1)
        lanes = [slice(hd * LANE, (hd + 1) * LANE) for hd in range(SB_HP)]
        qs = [(q_ref[:, ln] * scale).astype(BF16) for ln in lanes]
        after = (_iota2((SB_TK, SB_TK), 0) > _iota2((SB_TK, SB_TK), 1)).astype(BF16)
        oraw_ref[...] = jnp.zeros_like(oraw_ref)

        def block(kb, mask, c_lf):
            rows = pl.ds(pl.multiple_of(kb * SB_TK, SB_TK), SB_TK)
            z = _each(lambda q, ln: _dot(q, k_ref[rows, ln], NT), qs, lanes)
            lg = _each(lambda x: _sb_logits(x, mask), z)
            surv = _each(lambda x: _suffix_sums(x[2], after), lg)
            att = _each(lambda x, s, c: jnp.exp(x[0] + s + c), lg, surv, c_lf)
            if mask is not None:
                att = _each(lambda a: jnp.where(mask, a, 0.0), att)
            pv = _each(lambda a, ln: _dot(a, v_ref[rows, ln], NN), att, lanes)
            for p, ln in zip(pv, lanes):
                oraw_ref[:, ln] += p
            return tuple(_each(lambda c, x: c + jnp.sum(x[2], axis=1, keepdims=True), c_lf, lg))

        carry = tuple(jnp.zeros((tq, 1), F32) for _ in range(SB_HP))
        for r in reversed(range(ndiag)):
            carry = block(qi * ndiag + r, _sb_diag_mask(tq, r), carry)
        lax.fori_loop(0, qi * ndiag, lambda i, c: block(qi * ndiag - 1 - i, None, c), carry)
        o_ref[...] = (oraw_ref[...] * _silu(z_ref[...])).astype(BF16)

    q_spec, k_spec, v_spec, z_spec, out, _ = _sb_specs(t, tq)
    return pl.pallas_call(
        body,
        name="sb_fwd",
        grid=(N_HEADS // SB_HP, t // tq),
        in_specs=[q_spec, k_spec, v_spec, z_spec],
        out_specs=[out, out],
        out_shape=[jax.ShapeDtypeStruct((t, D_MODEL), BF16), jax.ShapeDtypeStruct((t, D_MODEL), F32)],
    )(proj, proj, proj, proj)


def _sb_bwd(proj, oraw, do):
    t = proj.shape[0]
    tq = min(SB_TQ, t)
    ndiag = tq // SB_TK
    scale = 1.0 / math.sqrt(D_HEAD)

    def body(q_ref, k_ref, v_ref, z_ref, oraw_ref, do_ref, dq_ref, dk_ref, dv_ref, dz_ref, dk_acc, dv_acc,
             p_scr, sig_scr, oms_scr):
        qi = pl.program_id(1)
        nq = pl.num_programs(1)

        @pl.when(qi == 0)
        def _():
            dk_acc[...] = jnp.zeros_like(dk_acc)
            dv_acc[...] = jnp.zeros_like(dv_acc)

        heads = range(SB_HP)
        lanes = [slice(hd * LANE, (hd + 1) * LANE) for hd in heads]
        zg = z_ref[...]
        sg = _sigmoid(zg)
        dog = do_ref[...].astype(F32)
        dz_ref[...] = (dog * oraw_ref[...] * (sg * (1.0 + zg * (1.0 - sg)))).astype(BF16)
        d_o = (dog * (zg * sg)).astype(BF16)
        d_o16 = [d_o[:, ln] for ln in lanes]
        qs = [(q_ref[:, ln] * scale).astype(BF16) for ln in lanes]
        ri = _iota2((SB_TK, SB_TK), 0)
        ci = _iota2((SB_TK, SB_TK), 1)
        after = (ri > ci).astype(BF16)
        earlier = (ri < ci).astype(BF16)

        def rows_of(kb):
            return pl.ds(pl.multiple_of(kb * SB_TK, SB_TK), SB_TK)

        def down(kb, mask, c_lf):
            rows = rows_of(kb)
            z = _each(lambda q, ln: _dot(q, k_ref[rows, ln], NT), qs, lanes)
            da = _each(lambda d, ln: _dot(d, v_ref[rows, ln], NT), d_o16, lanes)
            lg = _each(lambda x: _sb_logits(x, mask), z)
            surv = _each(lambda x: _suffix_sums(x[2], after), lg)
            att = _each(lambda x, s, c: jnp.exp(x[0] + s + c), lg, surv, c_lf)
            sig = _each(lambda x: jnp.exp(x[0]), lg)
            if mask is not None:
                att = _each(lambda a: jnp.where(mask, a, 0.0), att)
                sig = _each(lambda a: jnp.where(mask, a, 0.0), sig)
            dv = _each(lambda a, d: _dot(a, d, TN), att, d_o16)
            for hd in heads:
                p_scr[hd, kb] = att[hd] * da[hd]
                sig_scr[hd, kb] = sig[hd]
                oms_scr[hd, kb] = jnp.exp(lg[hd][1])
                dv_acc[rows, lanes[hd]] += dv[hd]
            return tuple(_each(lambda c, x: c + jnp.sum(x[2], axis=1, keepdims=True), c_lf, lg))

        c_lf = tuple(jnp.zeros((tq, 1), F32) for _ in heads)
        for r in reversed(range(ndiag)):
            c_lf = down(qi * ndiag + r, _sb_diag_mask(tq, r), c_lf)
        lax.fori_loop(0, qi * ndiag, lambda i, c: down(qi * ndiag - 1 - i, None, c), c_lf)

        def up(kb, carry):
            dq, c_p = carry
            rows = rows_of(kb)
            p = [p_scr[hd, kb] for hd in heads]
            before = _each(lambda x, c: _suffix_sums(x, earlier) + c, p, c_p)
            dzz = _each(lambda x, b, hd: x * oms_scr[hd, kb] - sig_scr[hd, kb] * b, p, before, list(heads))
            dk = _each(lambda x, q: _dot(x, q, TN), dzz, qs)
            dq = _each(lambda a, x, ln: a + _dot(x, k_ref[rows, ln], NN), dq, dzz, lanes)
            for hd in heads:
                dk_acc[rows, lanes[hd]] += dk[hd]
            return tuple(dq), tuple(_each(lambda c, x: c + jnp.sum(x, axis=1, keepdims=True), c_p, p))

        init = (tuple(jnp.zeros((tq, D_HEAD), F32) for _ in heads), tuple(jnp.zeros((tq, 1), F32) for _ in heads))
        dq, _ = lax.fori_loop(0, (qi + 1) * ndiag, up, init)
        for hd in heads:
            dq_ref[:, lanes[hd]] = (dq[hd] * scale).astype(BF16)

        @pl.when(qi == nq - 1)
        def _():
            dk_ref[...] = dk_acc[...].astype(BF16)
            dv_ref[...] = dv_acc[...].astype(BF16)

    q_spec, k_spec, v_spec, z_spec, blk, full = _sb_specs(t, tq)
    o = jax.ShapeDtypeStruct((t, D_MODEL), BF16)
    w = SB_HP * LANE
    return pl.pallas_call(
        body,
        name="sb_bwd",
        grid=(N_HEADS // SB_HP, t // tq),
        in_specs=[q_spec, k_spec, v_spec, z_spec, blk, blk],
        out_specs=[blk, full, full, blk],
        out_shape=[o, o, o, o],
        scratch_shapes=[pltpu.VMEM((t, w), F32), pltpu.VMEM((t, w), F32)]
        + [pltpu.VMEM((SB_HP, t // SB_TK, tq, SB_TK), F32)] * 3,
    )(proj, proj, proj, proj, oraw, do)


def _mem_kv_fn(mem, mg, w):
    return mm_nn(_rmsnorm(mem, mg), w)


def _mem_kv(mem, mg, w):
    def body(m_ref, g_ref, w_ref, o_ref):
        o_ref[...] = _mem_kv_fn(m_ref[...], g_ref[...], w_ref[...])

    return pl.pallas_call(body, name="mem_kv", out_shape=jax.ShapeDtypeStruct((MEM_LEN, 2 * MEM_W), F32))(mem, mg, w)


def _mem_kv_bwd(mem, mg, w, dmkv):
    def body(m_ref, g_ref, w_ref, d_ref, dg_ref, dw_ref):
        _, vjp = jax.vjp(_mem_kv_fn, m_ref[...], g_ref[...], w_ref[...].astype(F32))
        _, dg, dw = vjp(d_ref[...])
        dg_ref[...] = dg
        dw_ref[...] = dw

    return pl.pallas_call(
        body, name="mem_kv_bwd",
        out_shape=[jax.ShapeDtypeStruct((1, D_MODEL), F32), jax.ShapeDtypeStruct((D_MODEL, 2 * MEM_W), F32)],
    )(mem, mg, w, dmkv)


def _mem_attn(proj, mkv, tm=256):
    t = proj.shape[0]
    tm = min(tm, t)

    def body(q_ref, z_ref, kv_ref, o_ref):
        o_ref[...] = _mem_fn(q_ref[...], z_ref[...], kv_ref[...]).astype(BF16)

    return pl.pallas_call(
        body,
        name="mem_attn",
        grid=(t // tm,),
        in_specs=[pl.BlockSpec((tm, MEM_W), lambda i: (i, O_MQ // MEM_W)),
                  pl.BlockSpec((tm, MEM_W), lambda i: (i, O_MZ // MEM_W)),
                  pl.BlockSpec((MEM_LEN, 2 * MEM_W), lambda i: (0, 0))],
        out_specs=pl.BlockSpec((tm, MEM_W), lambda i: (i, 0)),
        out_shape=jax.ShapeDtypeStruct((t, MEM_W), BF16),
    )(proj, proj, mkv)


def _mem_attn_bwd(proj, mkv, do, tm=256):
    t = proj.shape[0]
    tm = min(tm, t)

    def body(q_ref, z_ref, kv_ref, do_ref, dq_ref, dz_ref, dkv_ref):
        _, vjp = jax.vjp(_mem_fn, q_ref[...], z_ref[...], kv_ref[...])
        dq, dz, dkv = vjp(do_ref[...].astype(F32))
        dq_ref[...] = dq.astype(BF16)
        dz_ref[...] = dz.astype(BF16)

        @pl.when(pl.program_id(0) == 0)
        def _():
            dkv_ref[...] = jnp.zeros_like(dkv_ref)

        dkv_ref[...] += dkv

    blk = pl.BlockSpec((tm, MEM_W), lambda i: (i, 0))
    kv = pl.BlockSpec((MEM_LEN, 2 * MEM_W), lambda i: (0, 0))
    return pl.pallas_call(
        body,
        name="mem_attn_bwd",
        grid=(t // tm,),
        in_specs=[pl.BlockSpec((tm, MEM_W), lambda i: (i, O_MQ // MEM_W)),
                  pl.BlockSpec((tm, MEM_W), lambda i: (i, O_MZ // MEM_W)), kv, blk],
        out_specs=[blk, blk, kv],
        out_shape=[jax.ShapeDtypeStruct((t, MEM_W), BF16), jax.ShapeDtypeStruct((t, MEM_W), BF16),
                   jax.ShapeDtypeStruct((MEM_LEN, 2 * MEM_W), F32)],
    )(proj, proj, mkv, do)


def _local_step(x, mem, tgt, norm_g, mem_norm_g, w_al, conv_w, alog_row, dtb_row, dn_norm_g, w_mem_kv, w_br_dn, w_br_sb,
                w_br_mem, w_out, final_g):
    h, h_t = _norm_in(x, norm_g)
    proj = _matmul(h, w_al, "nn", F32, 2048, 384, 1024, "proj")

    c = _dn_conv(proj, conv_w)
    o_dn, states = _dn_fwd(c, proj, alog_row, dtb_row, dn_norm_g)
    o_sb, o_sb_raw = _sb_fwd(proj)
    mkv = _mem_kv(mem, mem_norm_g, w_mem_kv)
    o_m = _mem_attn(proj, mkv)

    y_dn = _matmul(o_dn, w_br_dn, "nn", F32, 512, 1024, 1024, "y_dn")
    y_sb = _matmul(o_sb, w_br_sb, "nn", F32, 512, 1024, 1024, "y_sb")
    y_m = _matmul(o_m, w_br_mem, "nn", F32, 512, 1024, 1024, "y_m")
    merged = _merge(proj, y_dn, y_sb, y_m)
    mo = _matmul(merged, w_out, "nn", F32, 512, 1024, 1024, "mo")
    loss, dout, d_final_g = _loss_head(x, mo, final_g, tgt)

    dmerged = _matmul(dout, w_out, "nt", F32, 512, 1024, 1024, "dmerged")
    dw_out = _matmul(merged, dout, "tn", F32, 256, 1024, 2048, "dw_out")
    dy_dn, dy_sb, dy_m, dg_dn, dg_sb, dg_m = _merge_bwd(proj, y_dn, y_sb, y_m, dmerged)
    do_dn = _matmul(dy_dn, w_br_dn, "nt", BF16, 512, 1024, 1024, "do_dn")
    do_sb = _matmul(dy_sb, w_br_sb, "nt", BF16, 512, 1024, 1024, "do_sb")
    do_m = _matmul(dy_m, w_br_mem, "nt", BF16, 512, 256, 1024, "do_m")
    dw_br_dn = _matmul(o_dn, dy_dn, "tn", F32, 256, 1024, 2048, "dw_br_dn")
    dw_br_sb = _matmul(o_sb, dy_sb, "tn", F32, 256, 1024, 2048, "dw_br_sb")
    dw_br_mem = _matmul(o_m, dy_m, "tn", F32, 256, 1024, 2048, "dw_br_mem")

    dmq, dmz, dmkv = _mem_attn_bwd(proj, mkv, do_m)
    d_mem_norm_g, dw_mem_kv = _mem_kv_bwd(mem, mem_norm_g, w_mem_kv, dmkv)
    dq_sb, dk_sb, dv_sb, dz_sb = _sb_bwd(proj, o_sb_raw, do_sb)
    dcq, dck, dcv, dz_dn, dba, dscal, d_dn_norm_g = _dn_bwd(c, proj, alog_row, dtb_row, dn_norm_g, states, do_dn)
    dq_dn, dcw_q = _dn_conv_bwd(proj, conv_w, dcq, 0)
    dk_dn, dcw_k = _dn_conv_bwd(proj, conv_w, dck, 1)
    dv_dn, dcw_v = _dn_conv_bwd(proj, conv_w, dcv, 2)
    d_conv_w = jnp.concatenate([dcw_q, dcw_k, dcw_v], axis=1)

    dproj = jnp.concatenate([dq_dn, dk_dn, dv_dn, dz_dn, dq_sb, dk_sb, dv_sb, dz_sb, dmq, dmz, dg_dn, dg_sb, dg_m,
                             dba.astype(BF16)], axis=1)
    dh = _matmul(dproj, w_al, "nt", F32, 512, 1024, 3968, "dh")
    dw_al = _matmul(h_t, dproj, "nn", F32, 1024, 384, 2048, "dw_al")
    grad_x, d_norm_g = _norm_in_bwd(x, norm_g, dh, dout)
    return dict(loss=loss, grad_x=grad_x, norm_g=d_norm_g, mem_norm_g=d_mem_norm_g, w_al=dw_al, conv_w=d_conv_w,
                scal=dscal, dn_norm_g=d_dn_norm_g, w_mem_kv=dw_mem_kv, w_br_dn=dw_br_dn, w_br_sb=dw_br_sb,
                w_br_mem=dw_br_mem, w_out=dw_out, final_g=d_final_g)


MESH = pl.DeviceIdType.MESH
ANY = pl.BlockSpec(memory_space=pl.ANY)


def _position():
    return lax.axis_index("x"), lax.axis_index("y"), lax.axis_index("c")


def _all_gather(xs, name):
    n = len(xs)

    def body(*refs):
        x_refs, o_refs = refs[:n], refs[n:2 * n]
        send_sems, recv_sems, local_sems = refs[2 * n:]
        x, y, c = _position()
        me, sibling = (x, y, c), (x, y, 1 - c)
        chips = [(1 - x, y), (x, 1 - y), (1 - x, 1 - y)]

        def slot(p):
            return 4 * p[0] + 2 * p[1] + p[2]

        def copy(a, k, block, to, src=None):
            dst = o_refs[a].at[slot(block)]
            return pltpu.make_async_remote_copy(
                src_ref=dst if src is None else src, dst_ref=dst, send_sem=send_sems.at[7 * a + k],
                recv_sem=recv_sems.at[7 * a + k], device_id=to, device_id_type=MESH)

        mine = [pltpu.make_async_copy(x_refs[a], o_refs[a].at[slot(me)], local_sems.at[a]) for a in range(n)]
        for cp in mine:
            cp.start()
        first = []
        for a in range(n):
            first.append(copy(a, 0, me, sibling, src=x_refs[a]))
            first += [copy(a, 1 + j, me, (*chip, c), src=x_refs[a]) for j, chip in enumerate(chips)]
        for cp in first:
            cp.start()
        passed = []
        for j, chip in enumerate(chips):
            for a in range(n):
                copy(a, 1 + j, (*chip, c), me).wait_recv()
                cp = copy(a, 4 + j, (*chip, c), sibling)
                cp.start()
                passed.append(cp)
        for a in range(n):
            copy(a, 0, sibling, me).wait_recv()
            for j, chip in enumerate(chips):
                copy(a, 4 + j, (*chip, 1 - c), me).wait_recv()
        for cp in first + passed:
            cp.wait_send()
        for cp in mine:
            cp.wait()

    return pl.pallas_call(
        body,
        name=name,
        in_specs=[ANY] * n,
        out_specs=[ANY] * n,
        out_shape=[jax.ShapeDtypeStruct((N_DEV, *v.shape), v.dtype) for v in xs],
        scratch_shapes=[pltpu.SemaphoreType.DMA((7 * n,)), pltpu.SemaphoreType.DMA((7 * n,)),
                        pltpu.SemaphoreType.DMA((n,))],
    )(*xs)


def _window_view(ref, dest):
    return ref.at[:, pl.ds(LANE * WIN_START[dest], WIN_W)]


def _halving_stage(xs, axis, name, out_dtype, windowed=()):
    n_arr = len(xs)
    metas = []
    for k, v in enumerate(xs):
        if k in windowed:
            metas.append((N_DEV // 2, v.shape[0], WIN_W))
        else:
            assert v.shape[1] == 2
            metas.append((v.shape[0], v.shape[2], v.shape[3]))
    chunk = [min(r, 1 << int(math.log2((1 << 17) // c))) for (_, r, c) in metas]
    assert all(r % ch == 0 and ch % 8 == 0 for ch, (_, r, _) in zip(chunk, metas))
    offs = [sum(m[0] for m in metas[:k]) for k in range(n_arr)]
    n_sem = sum(m[0] for m in metas)

    def body(*refs):
        x_refs = refs[:n_arr]
        o_refs = refs[n_arr:2 * n_arr]
        land_refs = refs[2 * n_arr:3 * n_arr]
        rest = refs[3 * n_arr:]
        bufs = rest[:3 * n_arr]
        send_sems, recv_sems, in_sems, out_sems = rest[3 * n_arr:]
        pos = dict(zip("xyc", _position()))
        bit = pos[axis]
        peer = tuple(1 - pos[a] if a == axis else pos[a] for a in "xyc")

        def view(k, i, b):
            if k in windowed:
                return _window_view(x_refs[k], 2 * i + b)
            return x_refs[k].at[i, b]

        def add_blocks(k, a_view, b_view, o_view):
            _, rows, _ = metas[k]
            ch = chunk[k]
            nch = rows // ch
            va, vb, vo = bufs[3 * k:3 * k + 3]

            def rows_of(j):
                return pl.ds(pl.multiple_of(j * ch, 8), ch)

            def loads(j, s):
                return (pltpu.make_async_copy(a_view.at[rows_of(j), :], va.at[s], in_sems.at[0, s]),
                        pltpu.make_async_copy(b_view.at[rows_of(j), :], vb.at[s], in_sems.at[1, s]))

            def store(j, s):
                return pltpu.make_async_copy(vo.at[s], o_view.at[rows_of(j), :], out_sems.at[s])

            for cp in loads(0, 0):
                cp.start()

            def step(j, _):
                s = lax.rem(j, 2)

                @pl.when(j + 1 < nch)
                def _():
                    for cp in loads(j + 1, 1 - s):
                        cp.start()

                for cp in loads(j, s):
                    cp.wait()

                @pl.when(j >= 2)
                def _():
                    store(j - 2, s).wait()

                vo[s] = (va[s] + vb[s]).astype(vo.dtype)
                store(j, s).start()
                return 0

            lax.fori_loop(0, nch, step, 0)
            for j in range(max(0, nch - 2), nch):
                store(j, j % 2).wait()

        for b in (0, 1):
            @pl.when(bit == b)
            def _(b=b):
                sends = []
                for k in range(n_arr):
                    for i in range(metas[k][0]):
                        cp = pltpu.make_async_remote_copy(
                            src_ref=view(k, i, 1 - b), dst_ref=land_refs[k].at[i], send_sem=send_sems.at[offs[k] + i],
                            recv_sem=recv_sems.at[offs[k] + i], device_id=peer, device_id_type=MESH)
                        cp.start()
                        sends.append(cp)
                idx = 0
                for k in range(n_arr):
                    for i in range(metas[k][0]):
                        sends[idx].wait_recv()
                        add_blocks(k, view(k, i, b), land_refs[k].at[i], o_refs[k].at[i])
                        idx += 1
                for cp in sends:
                    cp.wait_send()

    out_shape = [jax.ShapeDtypeStruct(m, out_dtype) for m in metas]
    land_shape = [jax.ShapeDtypeStruct(m, F32) for m in metas]
    scratch = []
    for k in range(n_arr):
        scratch += [pltpu.VMEM((2, chunk[k], metas[k][2]), F32)] * 2 + [pltpu.VMEM((2, chunk[k], metas[k][2]), out_dtype)]
    scratch += [pltpu.SemaphoreType.DMA((n_sem,)), pltpu.SemaphoreType.DMA((n_sem,)),
                pltpu.SemaphoreType.DMA((2, 2)), pltpu.SemaphoreType.DMA((2,))]
    outs = pl.pallas_call(
        body,
        name=name,
        in_specs=[ANY] * n_arr,
        out_specs=[ANY] * (2 * n_arr),
        out_shape=out_shape + land_shape,
        scratch_shapes=scratch,
    )(*xs)
    return outs[:n_arr]


def _hbm_add(a_view, b_view, o_view, bufs, in_sems, out_sems, ch):
    rows = a_view.shape[0]
    nch = rows // ch
    va, vb, vo = bufs

    def rows_of(j):
        return pl.ds(pl.multiple_of(j * ch, 16), ch)

    def loads(j, s):
        return (pltpu.make_async_copy(a_view.at[rows_of(j), :], va.at[s], in_sems.at[0, s]),
                pltpu.make_async_copy(b_view.at[rows_of(j), :], vb.at[s], in_sems.at[1, s]))

    def store(j, s):
        return pltpu.make_async_copy(vo.at[s], o_view.at[rows_of(j), :], out_sems.at[s])

    for cp in loads(0, 0):
        cp.start()

    def step(j, _):
        s = lax.rem(j, 2)

        @pl.when(j + 1 < nch)
        def _():
            for cp in loads(j + 1, 1 - s):
                cp.start()

        for cp in loads(j, s):
            cp.wait()

        @pl.when(j >= 2)
        def _():
            store(j - 2, s).wait()

        vo[s] = (va[s].astype(F32) + vb[s].astype(F32)).astype(vo.dtype)
        store(j, s).start()
        return 0

    lax.fori_loop(0, nch, step, 0)
    for j in range(max(0, nch - 2), nch):
        store(j, j % 2).wait()


def _xy_stage(xs, first, name):
    n_arr = len(xs)
    if first:
        shapes = [(v.shape[2] // 2, v.shape[3]) for v in xs]
        ins = list(xs)
    else:
        shapes = [(a.shape[1], a.shape[2]) for a, _ in xs]
        ins = [v for pair in xs for v in pair]
    n_blk = 2 if first else 1
    out_dtype = BF16 if first else F32
    chunk = [min(r, 1 << int(math.log2((1 << 17) // c))) for (r, c) in shapes]
    assert all(r % ch == 0 and ch % 16 == 0 for ch, (r, _) in zip(chunk, shapes))
    n_sem = 2 * n_blk * n_arr

    def body(*refs):
        n_in = len(ins)
        in_refs = refs[:n_in]
        n_out = 2 * n_arr if first else n_arr
        o_refs = refs[n_in:n_in + n_out]
        land = refs[n_in + n_out:n_in + n_out + 2 * n_arr]
        rest = refs[n_in + n_out + 2 * n_arr:]
        bufs = rest[:3 * n_arr]
        send_sems, recv_sems, in_sems, out_sems = rest[3 * n_arr:]
        x, y, c = _position()
        peers = {"x": (1 - x, y, c), "y": (x, 1 - y, c)}
        jobs = []
        for k in range(n_arr):
            r, _ = shapes[k]
            half_a, half_b = pl.ds(0, r), pl.ds(r, r)
            if first:
                src = in_refs[k]
                for i in range(2):
                    jobs.append((k, src.at[i, 1 - y, half_a, :], src.at[i, y, half_a, :], land[2 * k].at[i],
                                 o_refs[2 * k].at[i], "y"))
                    jobs.append((k, src.at[1 - x, i, half_b, :], src.at[x, i, half_b, :], land[2 * k + 1].at[i],
                                 o_refs[2 * k + 1].at[i], "x"))
            else:
                a1, b1 = in_refs[2 * k], in_refs[2 * k + 1]
                jobs.append((k, a1.at[1 - x], a1.at[x], land[2 * k], o_refs[k].at[half_a, :], "x"))
                jobs.append((k, b1.at[1 - y], b1.at[y], land[2 * k + 1], o_refs[k].at[half_b, :], "y"))
        sends = []
        for n, (k, send, _, landing, _, axis) in enumerate(jobs):
            cp = pltpu.make_async_remote_copy(src_ref=send, dst_ref=landing, send_sem=send_sems.at[n],
                                              recv_sem=recv_sems.at[n], device_id=peers[axis], device_id_type=MESH)
            cp.start()
            sends.append(cp)
        for cp, (k, _, kept, landing, out, _) in zip(sends, jobs):
            cp.wait_recv()
            _hbm_add(kept, landing, out, bufs[3 * k:3 * k + 3], in_sems, out_sems, chunk[k])
        for cp in sends:
            cp.wait_send()

    if first:
        out_shape = [jax.ShapeDtypeStruct((2, r, c), BF16) for (r, c) in shapes for _ in range(2)]
        land_shape = out_shape
    else:
        out_shape = [jax.ShapeDtypeStruct((2 * r, c), F32) for (r, c) in shapes]
        land_shape = [jax.ShapeDtypeStruct((r, c), BF16) for (r, c) in shapes for _ in range(2)]
    scratch = []
    for k in range(n_arr):
        scratch += [pltpu.VMEM((2, chunk[k], shapes[k][1]), BF16)] * 2 + [pltpu.VMEM((2, chunk[k], shapes[k][1]), out_dtype)]
    scratch += [pltpu.SemaphoreType.DMA((n_sem,)), pltpu.SemaphoreType.DMA((n_sem,)),
                pltpu.SemaphoreType.DMA((2, 2)), pltpu.SemaphoreType.DMA((2,))]
    outs = pl.pallas_call(
        body,
        name=name,
        in_specs=[ANY] * len(ins),
        out_specs=[ANY] * (len(out_shape) + len(land_shape)),
        out_shape=out_shape + land_shape,
        scratch_shapes=scratch,
    )(*ins)
    outs = outs[:len(out_shape)]
    return [(outs[2 * k], outs[2 * k + 1]) for k in range(n_arr)] if first else list(outs)


def _reduce_scatter(dw_al, blocks):
    xs = [dw_al] + [b.reshape(N_DEV // 2, 2, *b.shape[1:]) for b in blocks]
    ys = _halving_stage(xs, "c", "rs_c", BF16, windowed=(0,))
    pairs = _xy_stage([v.reshape(2, 2, *v.shape[1:]) for v in ys], True, "rs_xy1")
    return _xy_stage(pairs, False, "rs_xy2")


def _sum_slots(g):
    def body(g_ref, o_ref):
        acc = g_ref[0]
        for d in range(1, N_DEV):
            acc = acc + g_ref[d]
        o_ref[...] = acc

    return pl.pallas_call(body, name="sum_slots", out_shape=jax.ShapeDtypeStruct(g.shape[1:], g.dtype))(g)


def _assemble_w_al(wins, bas):
    ba_tile = O_BA // LANE
    assert W_AL // LANE == ba_tile + 1

    def tiles(d, t, n):
        return wins[d, :, LANE * (t - WIN_START[d]):LANE * (t - WIN_START[d] + n)]

    pieces = []
    run = None
    for t in range(ba_tile + 1):
        owners = [d for d in range(N_DEV) if WIN_START[d] <= t < WIN_START[d] + WIN_TILES] if t < ba_tile else []
        if run is not None and (len(owners) != 1 or owners[0] != run[0]):
            pieces.append(tiles(*run))
            run = None
        if len(owners) == 1:
            run = (owners[0], t, 1) if run is None else (run[0], run[1], run[2] + 1)
        elif len(owners) == 2:
            pieces.append(tiles(owners[0], t, 1) + tiles(owners[1], t, 1))
    pieces.append(bas[BA_DEV])
    return jnp.concatenate(pieces, axis=1)


def _adamw(w, g, m, v, name):
    r, c = w.shape
    tm = r if r <= 256 else 256
    assert r % tm == 0

    def body(w_ref, g_ref, m_ref, v_ref, d_ref, nm_ref, nv_ref):
        gg = g_ref[...]
        m_new = ADAM_B1 * m_ref[...] + (1.0 - ADAM_B1) * gg
        v_new = ADAM_B2 * v_ref[...] + (1.0 - ADAM_B2) * (gg * gg)
        m_hat = m_new / (1.0 - ADAM_B1 ** ADAM_STEP)
        v_hat = v_new / (1.0 - ADAM_B2 ** ADAM_STEP)
        d_ref[...] = -ADAM_LR * (m_hat / (jnp.sqrt(v_hat) + ADAM_EPS) + ADAM_WD * w_ref[...])
        nm_ref[...] = m_new
        nv_ref[...] = v_new

    blk = pl.BlockSpec((tm, c), lambda i: (i, 0))
    o = jax.ShapeDtypeStruct((r, c), F32)
    return pl.pallas_call(body, name=name, grid=(r // tm,), in_specs=[blk] * 4, out_specs=[blk] * 3,
                          out_shape=[o, o, o])(w, g, m, v)


def _select(me, table):
    return sum(jnp.where(me == d, jnp.int32(v), jnp.int32(0)) for d, v in enumerate(table))


WIN_SHIFT = tuple(SHARD_W * d - LANE * WIN_START[d] for d in range(N_DEV))
PAD_L = 256
PAD_R = 256


def _shard_to_window(shard, me):
    shift = _select(me, WIN_SHIFT)
    start = _select(me, WIN_START)
    padded = jnp.pad(shard, ((0, 0), (PAD_L, PAD_R)))
    rows = shard.shape[0]
    lo = lax.dynamic_slice(padded, (0, PAD_L - shift), (rows, WIN_W))
    hi = lax.dynamic_slice(padded, (0, PAD_L - shift + N_BA), (rows, WIN_W))
    aligned = LANE * start + lax.broadcasted_iota(jnp.int32, (1, WIN_W), 1)
    return jnp.where(aligned >= ORIG_BA, hi, lo)


def _window_to_shard(win, ba_grad, me):
    shift = _select(me, WIN_SHIFT)
    rows = win.shape[0]
    padded = jnp.pad(win, ((0, 0), (N_BA, PAD_R)))
    lo = lax.dynamic_slice(padded, (0, N_BA + shift), (rows, SHARD_W))
    hi = lax.dynamic_slice(padded, (0, shift), (rows, SHARD_W))
    orig = SHARD_W * me + lax.broadcasted_iota(jnp.int32, (1, SHARD_W), 1)
    ba_full = lax.dynamic_update_slice(jnp.zeros((rows, SHARD_W), win.dtype), ba_grad, (0, BA_LOCAL))
    return jnp.where(orig < ORIG_BA, lo, jnp.where(orig >= ORIG_BA + N_BA, hi, ba_full))


def _pad_row(v, width=D_MODEL):
    v = v.reshape(1, -1)
    return jnp.pad(v, ((0, 0), (0, width - v.shape[1])))


def _slab(v, rows=8):
    return jnp.pad(v, ((0, rows - v.shape[0]), (0, D_MODEL - v.shape[1])))


def kernel(x, mem, norm_g, mem_norm_g, w_in, conv_w, a_log, dt_bias, dn_norm_g, w_mem_kv, w_br_dn, w_br_sb, w_br_mem, w_out, final_g, loss_target, m_norm_g, m_mem_norm_g, m_w_in, m_conv_w, m_a_log, m_dt_bias, m_dn_norm_g, m_w_mem_kv, m_w_br_dn, m_w_br_sb, m_w_br_mem, m_w_out, m_final_g, v_norm_g, v_mem_norm_g, v_w_in, v_conv_w, v_a_log, v_dt_bias, v_dn_norm_g, v_w_mem_kv, v_w_br_dn, v_w_br_sb, v_w_br_mem, v_w_out, v_final_g):
    xi, yi, ci = _position()
    me = 4 * xi + 2 * yi + ci

    shard = w_in[0]
    win = _shard_to_window(shard, me).astype(BF16)
    ba = jnp.pad(shard[:, BA_LOCAL:BA_LOCAL + N_BA], ((0, 0), (0, LANE - N_BA))).astype(BF16)
    g_win, g_ba, g_kv, g_dn, g_sb, g_out, g_mem, g_conv = _all_gather(
        [win, ba, w_mem_kv[0].astype(BF16), w_br_dn[0].astype(BF16), w_br_sb[0].astype(BF16), w_out[0].astype(BF16),
         w_br_mem[0].astype(BF16), conv_w[0]], "gather_weights")
    w_al = _assemble_w_al(g_win, g_ba)
    w_mem_kv_f = g_kv.reshape(D_MODEL, 2 * MEM_W)
    w_br_dn_f = g_dn.reshape(D_MODEL, D_MODEL)
    w_br_sb_f = g_sb.reshape(D_MODEL, D_MODEL)
    w_out_f = g_out.reshape(D_MODEL, D_MODEL)
    w_br_mem_f = g_mem.transpose(1, 0, 2).reshape(MEM_W, D_MODEL)
    conv_w_f = g_conv.transpose(1, 0, 2).reshape(CONV_K, 3 * D_MODEL)

    r = _local_step(x[0], mem[0], loss_target[0], norm_g, mem_norm_g, w_al, conv_w_f, _pad_row(a_log, LANE),
                    _pad_row(dt_bias, LANE), dn_norm_g, w_mem_kv_f, w_br_dn_f, w_br_sb_f, w_br_mem_f, w_out_f,
                    final_g.reshape(1, D_MODEL))

    dw_al = r["w_al"]
    g_win, g_kv, g_dn, g_sb, g_out, g_mem = _reduce_scatter(dw_al, [
        r["w_mem_kv"].reshape(N_DEV, D_MODEL // N_DEV, 2 * MEM_W),
        r["w_br_dn"].reshape(N_DEV, D_MODEL // N_DEV, D_MODEL),
        r["w_br_sb"].reshape(N_DEV, D_MODEL // N_DEV, D_MODEL),
        r["w_out"].reshape(N_DEV, D_MODEL // N_DEV, D_MODEL),
        r["w_br_mem"].reshape(MEM_W, N_DEV, D_MODEL // N_DEV).transpose(1, 0, 2)])
    small = jnp.concatenate([
        _slab(r["norm_g"]), _slab(r["mem_norm_g"]), _slab(r["final_g"]), _slab(r["dn_norm_g"]), _slab(r["scal"]),
        _slab(r["loss"][:, :1]), _slab(r["conv_w"].reshape(CONV_K * 3, D_MODEL), 16),
        dw_al[:, O_BA:O_BA + N_BA].T], axis=0)
    (g_small,) = _all_gather([small], "gather_small")
    small = _sum_slots(g_small)
    g_norm_g, g_mem_norm_g, g_final_g = small[0:1], small[8:9], small[16]
    g_dn_norm_g = small[24:25, :LANE]
    g_a_log, g_dt_bias = small[32:33, :N_HEADS], small[33:34, :N_HEADS]
    loss = small[40, 0]
    g_conv_full = small[48:48 + CONV_K * 3].reshape(CONV_K, 3 * D_MODEL)
    cw = conv_w.shape[2]
    g_conv = lax.dynamic_slice(g_conv_full, (0, cw * me), (CONV_K, cw))
    g_w_in = _window_to_shard(g_win, small[64:64 + N_BA].T, me)

    def adam(name, w, g, m, v):
        shp = w.shape
        w2, g2, m2, v2 = (a.reshape(-1, shp[-1]) for a in (w, g, m, v))
        return tuple(o.reshape(shp) for o in _adamw(w2, g2, m2, v2, "adamw_" + name))

    def pack(vals):
        return jnp.concatenate([_slab(a.reshape(1, -1)) for a in vals], axis=0)

    smalls = (norm_g, mem_norm_g, final_g, dn_norm_g, a_log, dt_bias)
    p_d, p_m, p_v = _adamw(pack(smalls), pack((g_norm_g, g_mem_norm_g, g_final_g, g_dn_norm_g, g_a_log, g_dt_bias)),
                           pack((m_norm_g, m_mem_norm_g, m_final_g, m_dn_norm_g, m_a_log, m_dt_bias)),
                           pack((v_norm_g, v_mem_norm_g, v_final_g, v_dn_norm_g, v_a_log, v_dt_bias)), "adamw_small")

    def unpack(p):
        return [p[8 * i, :a.size].reshape(a.shape) for i, a in enumerate(smalls)]

    sd, sm, sv = unpack(p_d), unpack(p_m), unpack(p_v)
    grads = dict(norm_g=g_norm_g, mem_norm_g=g_mem_norm_g, w_in=g_w_in[None], conv_w=g_conv[None], a_log=g_a_log,
                 dt_bias=g_dt_bias, dn_norm_g=g_dn_norm_g, w_mem_kv=g_kv[None], w_br_dn=g_dn[None], w_br_sb=g_sb[None],
                 w_br_mem=g_mem[None], w_out=g_out[None], final_g=g_final_g)
    big = dict(w_in=(w_in, m_w_in, v_w_in), conv_w=(conv_w, m_conv_w, v_conv_w), w_mem_kv=(w_mem_kv, m_w_mem_kv, v_w_mem_kv),
               w_br_dn=(w_br_dn, m_w_br_dn, v_w_br_dn), w_br_sb=(w_br_sb, m_w_br_sb, v_w_br_sb),
               w_br_mem=(w_br_mem, m_w_br_mem, v_w_br_mem), w_out=(w_out, m_w_out, v_w_out))
    order = ["norm_g", "mem_norm_g", "w_in", "conv_w", "a_log", "dt_bias", "dn_norm_g", "w_mem_kv", "w_br_dn", "w_br_sb",
             "w_br_mem", "w_out", "final_g"]
    small_idx = {"norm_g": 0, "mem_norm_g": 1, "final_g": 2, "dn_norm_g": 3, "a_log": 4, "dt_bias": 5}
    deltas, new_m, new_v = {}, {}, {}
    for nm in order:
        if nm in big:
            w, m, v = big[nm]
            deltas[nm], new_m[nm], new_v[nm] = adam(nm, w, grads[nm], m, v)
        else:
            i = small_idx[nm]
            deltas[nm], new_m[nm], new_v[nm] = sd[i], sm[i], sv[i]
    return (loss, r["grad_x"][None], *[grads[nm] for nm in order], *[deltas[nm] for nm in order],
            *[new_m[nm] for nm in order], *[new_v[nm] for nm in order])
```

```python
import functools
import math

import jax
import jax.numpy as jnp
from jax import lax
from jax.experimental import pallas as pl
from jax.experimental.pallas import tpu as pltpu

F32 = jnp.float32
BF16 = jnp.bfloat16

D_MODEL = 1024
N_DEV = 8
N_HEADS = 8
D_HEAD = 128
DN_CHUNK = 64
CONV_K = 4
MEM_LEN = 256
MEM_HEADS = 4
MEM_DH = 64
MEM_W = MEM_HEADS * MEM_DH
NORM_EPS = 1e-6
IN_WIDTH = 11792
SHARD_W = IN_WIDTH // N_DEV

LANE = 128
SUPER = 2 * DN_CHUNK

O_QKV_DN = 0
O_Z_DN = 3072
O_QKV_SB = 4096
O_Z_SB = 7168
O_MQ = 8192
O_MZ = 8448
O_GATES = 8704
O_BA = 11776
W_AL = 11904
ORIG_BA = 4096
N_BA = 16

WIN_TILES = 13
WIN_W = WIN_TILES * LANE


def _aligned_col(o):
    return o if o < ORIG_BA else o - N_BA


WIN_START = tuple(min(_aligned_col(SHARD_W * d) // LANE, (W_AL // LANE) - WIN_TILES) for d in range(N_DEV))
WIN_OFF = tuple(_aligned_col(SHARD_W * d) - LANE * WIN_START[d] if SHARD_W * d >= ORIG_BA + N_BA or SHARD_W * d < ORIG_BA
                else None for d in range(N_DEV))
BA_DEV = ORIG_BA // SHARD_W
BA_LOCAL = ORIG_BA - BA_DEV * SHARD_W

ADAM_LR = 0.001
ADAM_B1 = 0.9
ADAM_B2 = 0.999
ADAM_EPS = 1e-08
ADAM_WD = 0.01
ADAM_STEP = 10

NN = (((1,), (0,)), ((), ()))
NT = (((1,), (1,)), ((), ()))
TN = (((0,), (0,)), ((), ()))


def _dot(a, b, dims):
    return lax.dot_general(a.astype(BF16), b.astype(BF16), dims, preferred_element_type=F32)


def _split2(a):
    hi = a.astype(BF16)
    lo = (a - hi.astype(F32)).astype(BF16)
    return hi, lo


def _dot3(a, b, dims):
    ah, al = _split2(a)
    bh, bl = _split2(b)
    d = functools.partial(lax.dot_general, dimension_numbers=dims, preferred_element_type=F32)
    return d(ah, bh) + (d(ah, bl) + d(al, bh))


def _sel_dot_impl(sel01, x, dims):
    sel = sel01.astype(BF16)
    h1 = x.astype(BF16)
    r1 = x - h1.astype(F32)
    h2 = r1.astype(BF16)
    h3 = (r1 - h2.astype(F32)).astype(BF16)
    d = functools.partial(lax.dot_general, dimension_numbers=dims, preferred_element_type=F32)
    return d(sel, h1) + (d(sel, h2) + d(sel, h3))


@jax.custom_vjp
def _sel_dot(sel01, x):
    return _sel_dot_impl(sel01, x, NN)


_sel_dot.defvjp(lambda s, x: (_sel_dot(s, x), s),
                lambda s, g: (jnp.zeros_like(s), _sel_dot_impl(s, g, TN)))


def _make_mm(dotfn):
    @jax.custom_vjp
    def nn(a, b):
        return dotfn(a, b, NN)

    @jax.custom_vjp
    def nt(a, b):
        return dotfn(a, b, NT)

    @jax.custom_vjp
    def tn(a, b):
        return dotfn(a, b, TN)

    nn.defvjp(lambda a, b: (nn(a, b), (a, b)), lambda r, g: (nt(g, r[1]), tn(r[0], g)))
    nt.defvjp(lambda a, b: (nt(a, b), (a, b)), lambda r, g: (nn(g, r[1]), tn(g, r[0])))
    tn.defvjp(lambda a, b: (tn(a, b), (a, b)), lambda r, g: (nt(r[1], g), nn(r[0], g)))
    return nn, nt, tn


mm_nn, mm_nt, mm_tn = _make_mm(_dot)
mm3_nn, mm3_nt, mm3_tn = _make_mm(_dot3)


def _sigmoid(x):
    return jax.nn.sigmoid(x)


def _silu(x):
    return x * _sigmoid(x)


def _softplus_parts(x):
    sp = jnp.log1p(jnp.exp(-jnp.abs(x)))
    return jnp.maximum(x, 0.0) + sp, jnp.maximum(-x, 0.0) + sp


def _rmsnorm(x, g):
    return x * lax.rsqrt(jnp.mean(x * x, axis=-1, keepdims=True) + NORM_EPS) * g


def _iota2(shape, dim):
    return lax.broadcasted_iota(jnp.int32, shape, dim)


def _div64(i):
    return lax.shift_right_logical(i, jnp.full(i.shape, 6, jnp.int32))


def _each(f, *lists):
    return [f(*a) for a in zip(*lists)]


@jax.custom_vjp
def _inv_unit_lower(ms):
    n = ms[0].shape[0]
    eye = (_iota2((n, n), 0) == _iota2((n, n), 1)).astype(F32)
    rs = [eye - m for m in ms]
    ps = ms
    for _ in range(5):
        ps = _each(mm3_nn, ps, ps)
        rs = _each(lambda r, p: r + mm3_nn(r, p), rs, ps)
    return rs


def _inv_fwd(ms):
    rs = _inv_unit_lower(ms)
    return rs, rs


def _inv_bwd(rs, gs):
    ts = _each(mm3_tn, rs, gs)
    return (_each(lambda t, r: -mm3_nt(t, r), ts, rs),)


_inv_unit_lower.defvjp(_inv_fwd, _inv_bwd)


def _dn_block(cq, ck, cv, bcol, acol, zt, alog, dtb, gn, s0):
    n = SUPER
    h = DN_CHUNK
    row = _iota2((n, n), 0)
    col = _iota2((n, n), 1)
    same = _div64(row) == _div64(col)
    incl = jnp.logical_and(same, row >= col)
    strict = jnp.logical_and(same, row > col)
    incl_f = incl.astype(F32)

    qn = _each(lambda x: x * lax.rsqrt(jnp.sum(x * x, axis=-1, keepdims=True) + NORM_EPS) * (D_HEAD ** -0.5), cq)
    kn = _each(lambda x: x * lax.rsqrt(jnp.sum(x * x, axis=-1, keepdims=True) + NORM_EPS), ck)
    beta = _each(_sigmoid, bcol)
    g = _each(lambda al, ac, dt: -(jnp.exp(al) * _softplus_parts(ac + dt)[0]), alog, acol, dtb)
    gcum = _each(lambda x: _sel_dot(incl_f, jnp.broadcast_to(x, (n, n))), g)
    gam_incl = _each(lambda x: jnp.where(incl, jnp.exp(jnp.where(incl, x - x.T, 0.0)), 0.0), gcum)
    kk = _each(mm_nt, kn, kn)
    t_inv = _inv_unit_lower(_each(lambda b, x, gm: b * x * jnp.where(strict, gm, 0.0), beta, kk, gam_incl))
    eg = _each(jnp.exp, gcum)
    u = _each(lambda t, v, b: mm_nn(t, v * b), t_inv, cv, beta)
    w = _each(lambda t, k, b, e: mm_nn(t, k * (b * e)), t_inv, kn, beta, eg)
    a_intra = _each(lambda q, k, gm: mm_nt(q, k) * gm, qn, kn, gam_incl)
    q_dec = _each(lambda q, e: q * e, qn, eg)
    last0 = _each(lambda x: x[h - 1:h, :], gcum)
    last1 = _each(lambda x: x[n - 1:n, :], gcum)
    k_dec = _each(lambda k, x, l0, l1: k * jnp.exp(jnp.concatenate(
        [jnp.broadcast_to(l0, (h, n)), jnp.broadcast_to(l1, (h, n))], axis=0) - x), kn, gcum, last0, last1)
    v0 = _each(lambda uu, ww, s: uu[:h] - mm_nn(ww[:h], s), u, w, s0)
    o0 = _each(lambda q, s: mm_nn(q[:h], s), q_dec, s0)
    s1 = _each(lambda s, l0, k, v: s * jnp.exp(l0) + mm_tn(k[:h], v), s0, last0, k_dec, v0)
    v1 = _each(lambda uu, ww, s: uu[h:] - mm_nn(ww[h:], s), u, w, s1)
    o1 = _each(lambda q, s: mm_nn(q[h:], s), q_dec, s1)
    s2 = _each(lambda s, l1, k, v: s * jnp.exp(l1) + mm_tn(k[h:], v), s1, last1, k_dec, v1)
    o = _each(lambda a, b, am, x, y: jnp.concatenate([a, b], axis=0) + mm_nn(am, jnp.concatenate([x, y], axis=0)),
              o0, o1, a_intra, v0, v1)
    out = _each(lambda x, z: _rmsnorm(x, gn) * _silu(z), o, zt)
    return out, s2


def _mem_fn(mq, mz, mkv):
    mk = mkv[:, :MEM_W]
    mv = mkv[:, MEM_W:]
    lane = _iota2((1, MEM_W), 1)
    out = jnp.zeros(mq.shape, F32)
    for hd in range(MEM_HEADS):
        hm = (_div64(lane) == hd).astype(F32)
        s = mm_nt(mq * hm, mk) * (1.0 / math.sqrt(MEM_DH))
        s = s - jnp.max(s, axis=-1, keepdims=True)
        e = jnp.exp(s)
        p = e / jnp.sum(e, axis=-1, keepdims=True)
        out = out + mm_nn(p, mv) * hm
    return out * _silu(mz)


def _merge_fn(gd, gs, gm, yd, ys, ym):
    return _sigmoid(gd) * yd + _sigmoid(gs) * ys + _sigmoid(gm) * ym


def _loss_fn(x, mo, fg, tgt):
    y = _rmsnorm(x + mo, fg)
    err = y - tgt
    return 0.5 * jnp.sum(jnp.mean(err * err, axis=-1, keepdims=True), axis=0, keepdims=True)


def _matmul(a, b, mode, out_dtype, tm, tn, tk, name, b_col0=0, n_cols=None):
    if mode == "nn":
        m, kdim = a.shape
        n = b.shape[1] if n_cols is None else n_cols
    elif mode == "nt":
        m, kdim = a.shape
        n = b.shape[0]
    else:
        kdim, m = a.shape
        n = b.shape[1] if n_cols is None else n_cols
    tm, tn, tk = min(tm, m), min(tn, n), min(tk, kdim)
    assert m % tm == 0 and n % tn == 0 and kdim % tk == 0 and b_col0 % tn == 0
    nk = kdim // tk
    jb = b_col0 // tn
    dims = {"nn": NN, "nt": NT, "tn": TN}[mode]

    def body(a_ref, b_ref, o_ref, acc_ref):
        k = pl.program_id(2)
        part = _dot(a_ref[...], b_ref[...], dims)

        @pl.when(k == 0)
        def _():
            acc_ref[...] = part

        @pl.when(k > 0)
        def _():
            acc_ref[...] += part

        @pl.when(k == nk - 1)
        def _():
            o_ref[...] = acc_ref[...].astype(o_ref.dtype)

    if mode == "nn":
        a_spec = pl.BlockSpec((tm, tk), lambda i, j, k: (i, k))
        b_spec = pl.BlockSpec((tk, tn), lambda i, j, k: (k, j + jb))
    elif mode == "nt":
        a_spec = pl.BlockSpec((tm, tk), lambda i, j, k: (i, k))
        b_spec = pl.BlockSpec((tn, tk), lambda i, j, k: (j, k))
    else:
        a_spec = pl.BlockSpec((tk, tm), lambda i, j, k: (k, i))
        b_spec = pl.BlockSpec((tk, tn), lambda i, j, k: (k, j + jb))
    return pl.pallas_call(
        body,
        name=name,
        grid=(m // tm, n // tn, nk),
        in_specs=[a_spec, b_spec],
        out_specs=pl.BlockSpec((tm, tn), lambda i, j, k: (i, j)),
        out_shape=jax.ShapeDtypeStruct((m, n), out_dtype),
        scratch_shapes=[pltpu.VMEM((tm, tn), F32)],
        compiler_params=pltpu.CompilerParams(dimension_semantics=("parallel", "parallel", "arbitrary")),
    )(a, b)


def _norm_in(x, g, tm=256):
    t = x.shape[0]

    def body(x_ref, g_ref, h_ref, ht_ref):
        h = _rmsnorm(x_ref[...], g_ref[...])
        h_ref[...] = h.astype(BF16)
        ht_ref[...] = h.T.astype(BF16)

    return pl.pallas_call(
        body,
        name="norm_in",
        grid=(t // tm,),
        in_specs=[pl.BlockSpec((tm, D_MODEL), lambda i: (i, 0)), pl.BlockSpec((1, D_MODEL), lambda i: (0, 0))],
        out_specs=[pl.BlockSpec((tm, D_MODEL), lambda i: (i, 0)), pl.BlockSpec((D_MODEL, tm), lambda i: (0, i))],
        out_shape=[jax.ShapeDtypeStruct((t, D_MODEL), BF16), jax.ShapeDtypeStruct((D_MODEL, t), BF16)],
    )(x, g)


def _norm_in_bwd(x, g, dh, dres, tm=256):
    t = x.shape[0]

    def body(x_ref, g_ref, dh_ref, dres_ref, dx_ref, dg_ref):
        _, vjp = jax.vjp(_rmsnorm, x_ref[...], g_ref[...])
        dx, dg = vjp(dh_ref[...])
        dx_ref[...] = dx + dres_ref[...]

        @pl.when(pl.program_id(0) == 0)
        def _():
            dg_ref[...] = jnp.zeros_like(dg_ref)

        dg_ref[...] += dg

    row = pl.BlockSpec((tm, D_MODEL), lambda i: (i, 0))
    vec = pl.BlockSpec((1, D_MODEL), lambda i: (0, 0))
    return pl.pallas_call(
        body,
        name="norm_in_bwd",
        grid=(t // tm,),
        in_specs=[row, vec, row, row],
        out_specs=[row, vec],
        out_shape=[jax.ShapeDtypeStruct((t, D_MODEL), F32), jax.ShapeDtypeStruct((1, D_MODEL), F32)],
    )(x, g, dh, dres)


def _merge(proj, yd, ys, ym, tm=256, tc=512):
    t = proj.shape[0]
    g0 = O_GATES // tc
    gstep = D_MODEL // tc

    def body(gd, gs, gm, yd_ref, ys_ref, ym_ref, o_ref):
        o_ref[...] = _merge_fn(gd[...], gs[...], gm[...], yd_ref[...], ys_ref[...], ym_ref[...]).astype(BF16)

    def gate(k):
        return pl.BlockSpec((tm, tc), lambda i, j: (i, g0 + k * gstep + j))

    blk = pl.BlockSpec((tm, tc), lambda i, j: (i, j))
    return pl.pallas_call(
        body,
        name="merge",
        grid=(t // tm, D_MODEL // tc),
        in_specs=[gate(0), gate(1), gate(2), blk, blk, blk],
        out_specs=blk,
        out_shape=jax.ShapeDtypeStruct((t, D_MODEL), BF16),
    )(proj, proj, proj, yd, ys, ym)


def _merge_bwd(proj, yd, ys, ym, dmerged, tm=256, tc=512):
    t = proj.shape[0]
    g0 = O_GATES // tc
    gstep = D_MODEL // tc

    def body(gd, gs, gm, yd_ref, ys_ref, ym_ref, dm_ref, dyd, dys, dym, dgd, dgs, dgm):
        _, vjp = jax.vjp(_merge_fn, gd[...], gs[...], gm[...], yd_ref[...], ys_ref[...], ym_ref[...])
        outs = vjp(dm_ref[...])
        for ref, val in zip((dgd, dgs, dgm, dyd, dys, dym), outs):
            ref[...] = val.astype(BF16)

    def gate(k):
        return pl.BlockSpec((tm, tc), lambda i, j: (i, g0 + k * gstep + j))

    blk = pl.BlockSpec((tm, tc), lambda i, j: (i, j))
    o = jax.ShapeDtypeStruct((t, D_MODEL), BF16)
    return pl.pallas_call(
        body,
        name="merge_bwd",
        grid=(t // tm, D_MODEL // tc),
        in_specs=[gate(0), gate(1), gate(2), blk, blk, blk, blk],
        out_specs=[blk] * 6,
        out_shape=[o] * 6,
    )(proj, proj, proj, yd, ys, ym, dmerged)


def _loss_head(x, mo, fg, tgt, tm=256):
    t = x.shape[0]

    def body(x_ref, mo_ref, fg_ref, t_ref, loss_ref, dout_ref, dfg_ref):
        loss, vjp = jax.vjp(_loss_fn, x_ref[...], mo_ref[...], fg_ref[...], t_ref[...])
        _, dmo, dfg, _ = vjp(jnp.ones((1, 1), F32))

        @pl.when(pl.program_id(0) == 0)
        def _():
            loss_ref[...] = jnp.zeros_like(loss_ref)
            dfg_ref[...] = jnp.zeros_like(dfg_ref)

        loss_ref[...] += jnp.broadcast_to(loss, loss_ref.shape)
        dfg_ref[...] += dfg
        dout_ref[...] = dmo

    row = pl.BlockSpec((tm, D_MODEL), lambda i: (i, 0))
    vec = pl.BlockSpec((1, D_MODEL), lambda i: (0, 0))
    return pl.pallas_call(
        body,
        name="loss_head",
        grid=(t // tm,),
        in_specs=[row, row, vec, row],
        out_specs=[pl.BlockSpec((1, LANE), lambda i: (0, 0)), row, vec],
        out_shape=[jax.ShapeDtypeStruct((1, LANE), F32), jax.ShapeDtypeStruct((t, D_MODEL), F32),
                   jax.ShapeDtypeStruct((1, D_MODEL), F32)],
    )(x, mo, fg, tgt)


def _shift_rows(x, s):
    t = x.shape[0]
    if s == 0:
        return x
    rolled = pltpu.roll(x, s % t, 0)
    row = _iota2(x.shape, 0)
    keep = row >= s if s > 0 else row < t + s
    return jnp.where(keep, rolled, 0.0)


def _conv_pre(x, w):
    return sum(_shift_rows(x, CONV_K - 1 - j) * w[j:j + 1, :] for j in range(CONV_K))


def _dn_conv(proj, conv_w):
    t = proj.shape[0]
    nb = 3 * D_MODEL // LANE

    def body(x_ref, w_ref, c_ref):
        c_ref[...] = _silu(_conv_pre(x_ref[...], w_ref[...]))

    return pl.pallas_call(
        body,
        name="dn_conv",
        grid=(nb,),
        in_specs=[pl.BlockSpec((t, LANE), lambda j: (0, j)), pl.BlockSpec((CONV_K, LANE), lambda j: (0, j))],
        out_specs=pl.BlockSpec((t, LANE), lambda j: (0, j)),
        out_shape=jax.ShapeDtypeStruct((t, 3 * D_MODEL), F32),
    )(proj, conv_w)


def _dn_conv_bwd(proj, conv_w, dc, part):
    t = proj.shape[0]
    nb = D_MODEL // LANE
    b0 = part * nb

    def body(x_ref, w_ref, dc_ref, dx_ref, dw_ref):
        x = x_ref[...]
        w = w_ref[...]
        pre = _conv_pre(x, w)
        sg = _sigmoid(pre)
        dpre = dc_ref[...] * (sg * (1.0 + pre * (1.0 - sg)))
        dx = sum(_shift_rows(dpre, -(CONV_K - 1 - j)) * w[j:j + 1, :] for j in range(CONV_K))
        dx_ref[...] = dx.astype(BF16)
        dw_ref[...] = jnp.concatenate(
            [jnp.sum(dpre * _shift_rows(x, CONV_K - 1 - j), axis=0, keepdims=True) for j in range(CONV_K)], axis=0)

    return pl.pallas_call(
        body,
        name=f"dn_conv_bwd{part}",
        grid=(nb,),
        in_specs=[pl.BlockSpec((t, LANE), lambda j: (0, b0 + j)), pl.BlockSpec((CONV_K, LANE), lambda j: (0, b0 + j)),
                  pl.BlockSpec((t, LANE), lambda j: (0, j))],
        out_specs=[pl.BlockSpec((t, LANE), lambda j: (0, j)), pl.BlockSpec((CONV_K, LANE), lambda j: (0, j))],
        out_shape=[jax.ShapeDtypeStruct((t, D_MODEL), BF16), jax.ShapeDtypeStruct((CONV_K, D_MODEL), F32)],
    )(proj, conv_w, dc)


def _ba_columns(ba, hd):
    lane = _iota2(ba.shape, 1)
    bcol = jnp.sum(jnp.where(lane == hd, ba, 0.0), axis=1, keepdims=True)
    acol = jnp.sum(jnp.where(lane == N_HEADS + hd, ba, 0.0), axis=1, keepdims=True)
    return bcol, acol


def _head_scalar(row, hd):
    lane = _iota2(row.shape, 1)
    return jnp.sum(jnp.where(lane == hd, row, 0.0), axis=1, keepdims=True)


DN_HP = 4


def _dn_inputs(cq, ck, cv, ba_ref, z_ref, alog_ref, dtb_ref, heads, lanes):
    ba = ba_ref[...]
    cols = [_ba_columns(ba, hd) for hd in heads]
    return ([cq[:, ln] for ln in lanes], [ck[:, ln] for ln in lanes], [cv[:, ln] for ln in lanes],
            [c[0] for c in cols], [c[1] for c in cols], [z_ref[:, ln] for ln in lanes],
            [_head_scalar(alog_ref[...], hd) for hd in heads], [_head_scalar(dtb_ref[...], hd) for hd in heads])


def _dn_specs(nblk, reverse):
    w = DN_HP * LANE
    nq = D_MODEL // w

    def row(i):
        return nblk - 1 - i if reverse else i

    def colblk(b0):
        return pl.BlockSpec((SUPER, w), lambda i, h: (row(i), b0 + h))

    ba = pl.BlockSpec((SUPER, LANE), lambda i, h: (row(i), O_BA // LANE))
    vec = pl.BlockSpec((1, LANE), lambda i, h: (0, 0))
    st = pl.BlockSpec((1, DN_HP, D_HEAD, D_HEAD), lambda i, h: (row(i), h, 0, 0))
    return colblk, nq, ba, vec, st


def _dn_fwd(c, proj, alog_row, dtb_row, gn):
    t = c.shape[0]
    nblk = t // SUPER
    colblk, nq, ba, vec, st = _dn_specs(nblk, False)

    def body(cq, ck, cv, ba_ref, z_ref, alog_ref, dtb_ref, gn_ref, o_ref, s_ref, state):
        @pl.when(jnp.logical_and(pl.program_id(0) == 0, pl.program_id(1) == 0))
        def _():
            state[...] = jnp.zeros_like(state)

        heads = [pl.program_id(1) * DN_HP + j for j in range(DN_HP)]
        lanes = [slice(j * LANE, (j + 1) * LANE) for j in range(DN_HP)]
        s0 = [state[hd] for hd in heads]
        outs, s2 = _dn_block(*_dn_inputs(cq, ck, cv, ba_ref, z_ref, alog_ref, dtb_ref, heads, lanes), gn_ref[...], s0)
        for j, (hd, ln) in enumerate(zip(heads, lanes)):
            s_ref[0, j] = s0[j]
            o_ref[:, ln] = outs[j].astype(BF16)
            state[hd] = s2[j]

    return pl.pallas_call(
        body,
        name="dn_fwd",
        grid=(nblk, N_HEADS // DN_HP),
        in_specs=[colblk(0), colblk(nq), colblk(2 * nq), ba, colblk(O_Z_DN // (DN_HP * LANE)), vec, vec, vec],
        out_specs=[colblk(0), st],
        out_shape=[jax.ShapeDtypeStruct((t, D_MODEL), BF16),
                   jax.ShapeDtypeStruct((nblk, N_HEADS, D_HEAD, D_HEAD), F32)],
        scratch_shapes=[pltpu.VMEM((N_HEADS, D_HEAD, D_HEAD), F32)],
    )(c, c, c, proj, proj, alog_row, dtb_row, gn)


def _dn_bwd(c, proj, alog_row, dtb_row, gn, states, do):
    t = c.shape[0]
    nblk = t // SUPER
    colblk, nq, ba, vec, st = _dn_specs(nblk, True)

    def body(cq, ck, cv, ba_ref, z_ref, alog_ref, dtb_ref, gn_ref, s_ref, do_ref,
             dq_ref, dk_ref, dv_ref, dz_ref, dba_ref, dsc_ref, dgn_ref, dstate):
        i = pl.program_id(0)
        hq = pl.program_id(1)

        @pl.when(jnp.logical_and(i == 0, hq == 0))
        def _():
            dstate[...] = jnp.zeros_like(dstate)
            dsc_ref[...] = jnp.zeros_like(dsc_ref)
            dgn_ref[...] = jnp.zeros_like(dgn_ref)

        @pl.when(hq == 0)
        def _():
            dba_ref[...] = jnp.zeros_like(dba_ref)

        lane = _iota2((SUPER, LANE), 1)
        lane1 = _iota2((1, LANE), 1)
        heads = [hq * DN_HP + j for j in range(DN_HP)]
        lanes = [slice(j * LANE, (j + 1) * LANE) for j in range(DN_HP)]
        ds_in = [dstate[hd] for hd in heads]
        s_in = [s_ref[0, j] for j in range(DN_HP)]
        _, vjp = jax.vjp(_dn_block, *_dn_inputs(cq, ck, cv, ba_ref, z_ref, alog_ref, dtb_ref, heads, lanes),
                         gn_ref[...], s_in)
        dq, dk, dv, dbc, dac, dz, dal, ddt, dgn, ds0 = vjp(([do_ref[:, ln].astype(F32) for ln in lanes], ds_in))
        dba = jnp.zeros((SUPER, LANE), F32)
        dal_row = jnp.zeros((1, LANE), F32)
        ddt_row = jnp.zeros((1, LANE), F32)
        for j, (hd, ln) in enumerate(zip(heads, lanes)):
            dq_ref[:, ln] = dq[j]
            dk_ref[:, ln] = dk[j]
            dv_ref[:, ln] = dv[j]
            dz_ref[:, ln] = dz[j].astype(BF16)
            dstate[hd] = ds0[j]
            dba = dba + jnp.where(lane == hd, dbc[j], 0.0) + jnp.where(lane == N_HEADS + hd, dac[j], 0.0)
            dal_row = dal_row + jnp.where(lane1 == hd, dal[j], 0.0)
            ddt_row = ddt_row + jnp.where(lane1 == hd, ddt[j], 0.0)
        dba_ref[...] += dba
        dsc_ref[0:1, :] += dal_row
        dsc_ref[1:2, :] += ddt_row
        dgn_ref[...] += dgn

    outs = pl.pallas_call(
        body,
        name="dn_bwd",
        grid=(nblk, N_HEADS // DN_HP),
        in_specs=[colblk(0), colblk(nq), colblk(2 * nq), ba, colblk(O_Z_DN // (DN_HP * LANE)), vec, vec, vec, st,
                  colblk(0)],
        out_specs=[colblk(0), colblk(0), colblk(0), colblk(0),
                   pl.BlockSpec((SUPER, LANE), lambda i, h: (nblk - 1 - i, 0)),
                   pl.BlockSpec((2, LANE), lambda i, h: (0, 0)), vec],
        out_shape=[jax.ShapeDtypeStruct((t, D_MODEL), F32)] * 3
        + [jax.ShapeDtypeStruct((t, D_MODEL), BF16), jax.ShapeDtypeStruct((t, LANE), F32),
           jax.ShapeDtypeStruct((2, LANE), F32), jax.ShapeDtypeStruct((1, LANE), F32)],
        scratch_shapes=[pltpu.VMEM((N_HEADS, D_HEAD, D_HEAD), F32)],
    )(c, c, c, proj, proj, alog_row, dtb_row, gn, states, do)
    return outs


SB_TQ = 256
SB_TK = 256
SB_HP = 2


def _sb_logits(z, mask):
    sp = jnp.log(1.0 + jnp.exp(-jnp.abs(z)))
    lf_raw = -(jnp.maximum(z, 0.0) + sp)
    lb = lf_raw + z
    lf = lf_raw if mask is None else jnp.where(mask, lf_raw, 0.0)
    return lb, lf_raw, lf


def _suffix_sums(x, sel):
    hi, lo = _split2(x)
    d = functools.partial(lax.dot_general, dimension_numbers=NN, preferred_element_type=F32)
    return d(hi, sel) + d(lo, sel)


def _sb_diag_mask(tq, r):
    return r * SB_TK + _iota2((tq, SB_TK), 1) < _iota2((tq, SB_TK), 0)


def _sb_specs(t, tq):
    w = SB_HP * LANE
    q0, k0, v0, z0 = (O_QKV_SB // w, (O_QKV_SB + D_MODEL) // w, (O_QKV_SB + 2 * D_MODEL) // w, O_Z_SB // w)

    def blk(b0):
        return pl.BlockSpec((tq, w), lambda h, i: (i, b0 + h))

    def full(b0):
        return pl.BlockSpec((t, w), lambda h, i: (0, b0 + h))

    return blk(q0), full(k0), full(v0), blk(z0), blk(0), full(0)


def _sb_fwd(proj):
    t = proj.shape[0]
    tq = min(SB_TQ, t)
    ndiag = tq // SB_TK
    scale = 1.0 / math.sqrt(D_HEAD)

    def body(q_ref, k_ref, v_ref, z_ref, o_ref, oraw_ref):
        qi = pl.program_id(1)
        lanes = [slice(hd * LANE, (hd + 1) * LANE) for hd in range(SB_HP)]
        qs = [(q_ref[:, ln] * scale).astype(BF16) for ln in lanes]
        after = (_iota2((SB_TK, SB_TK), 0) > _iota2((SB_TK, SB_TK), 1)).astype(BF16)
        oraw_ref[...] = jnp.zeros_like(oraw_ref)

        def block(kb, mask, c_lf):
            rows = pl.ds(pl.multiple_of(kb * SB_TK, SB_TK), SB_TK)
            z = _each(lambda q, ln: _dot(q, k_ref[rows, ln], NT), qs, lanes)
            lg = _each(lambda x: _sb_logits(x, mask), z)
            surv = _each(lambda x: _suffix_sums(x[2], after), lg)
            att = _each(lambda x, s, c: jnp.exp(x[0] + s + c), lg, surv, c_lf)
            if mask is not None:
                att = _each(lambda a: jnp.where(mask, a, 0.0), att)
            pv = _each(lambda a, ln: _dot(a, v_ref[rows, ln], NN), att, lanes)
            for p, ln in zip(pv, lanes):
                oraw_ref[:, ln] += p
            return tuple(_each(lambda c, x: c + jnp.sum(x[2], axis=1, keepdims=True), c_lf, lg))

        carry = tuple(jnp.zeros((tq, 1), F32) for _ in range(SB_HP))
        for r in reversed(range(ndiag)):
            carry = block(qi * ndiag + r, _sb_diag_mask(tq, r), carry)
        lax.fori_loop(0, qi * ndiag, lambda i, c: block(qi * ndiag - 1 - i, None, c), carry)
        o_ref[...] = (oraw_ref[...] * _silu(z_ref[...])).astype(BF16)

    q_spec, k_spec, v_spec, z_spec, out, _ = _sb_specs(t, tq)
    return pl.pallas_call(
        body,
        name="sb_fwd",
        grid=(N_HEADS // SB_HP, t // tq),
        in_specs=[q_spec, k_spec, v_spec, z_spec],
        out_specs=[out, out],
        out_shape=[jax.ShapeDtypeStruct((t, D_MODEL), BF16), jax.ShapeDtypeStruct((t, D_MODEL), F32)],
    )(proj, proj, proj, proj)


def _sb_bwd(proj, oraw, do):
    t = proj.shape[0]
    tq = min(SB_TQ, t)
    ndiag = tq // SB_TK
    scale = 1.0 / math.sqrt(D_HEAD)

    def body(q_ref, k_ref, v_ref, z_ref, oraw_ref, do_ref, dq_ref, dk_ref, dv_ref, dz_ref, dk_acc, dv_acc,
             p_scr, sig_scr, oms_scr):
        qi = pl.program_id(1)
        nq = pl.num_programs(1)

        @pl.when(qi == 0)
        def _():
            dk_acc[...] = jnp.zeros_like(dk_acc)
            dv_acc[...] = jnp.zeros_like(dv_acc)

        heads = range(SB_HP)
        lanes = [slice(hd * LANE, (hd + 1) * LANE) for hd in heads]
        zg = z_ref[...]
        sg = _sigmoid(zg)
        dog = do_ref[...].astype(F32)
        dz_ref[...] = (dog * oraw_ref[...] * (sg * (1.0 + zg * (1.0 - sg)))).astype(BF16)
        d_o = (dog * (zg * sg)).astype(BF16)
        d_o16 = [d_o[:, ln] for ln in lanes]
        qs = [(q_ref[:, ln] * scale).astype(BF16) for ln in lanes]
        ri = _iota2((SB_TK, SB_TK), 0)
        ci = _iota2((SB_TK, SB_TK), 1)
        after = (ri > ci).astype(BF16)
        earlier = (ri < ci).astype(BF16)

        def rows_of(kb):
            return pl.ds(pl.multiple_of(kb * SB_TK, SB_TK), SB_TK)

        def down(kb, mask, c_lf):
            rows = rows_of(kb)
            z = _each(lambda q, ln: _dot(q, k_ref[rows, ln], NT), qs, lanes)
            da = _each(lambda d, ln: _dot(d, v_ref[rows, ln], NT), d_o16, lanes)
            lg = _each(lambda x: _sb_logits(x, mask), z)
            surv = _each(lambda x: _suffix_sums(x[2], after), lg)
            att = _each(lambda x, s, c: jnp.exp(x[0] + s + c), lg, surv, c_lf)
            sig = _each(lambda x: jnp.exp(x[0]), lg)
            if mask is not None:
                att = _each(lambda a: jnp.where(mask, a, 0.0), att)
                sig = _each(lambda a: jnp.where(mask, a, 0.0), sig)
            dv = _each(lambda a, d: _dot(a, d, TN), att, d_o16)
            for hd in heads:
                p_scr[hd, kb] = att[hd] * da[hd]
                sig_scr[hd, kb] = sig[hd]
                oms_scr[hd, kb] = jnp.exp(lg[hd][1])
                dv_acc[rows, lanes[hd]] += dv[hd]
            return tuple(_each(lambda c, x: c + jnp.sum(x[2], axis=1, keepdims=True), c_lf, lg))

        c_lf = tuple(jnp.zeros((tq, 1), F32) for _ in heads)
        for r in reversed(range(ndiag)):
            c_lf = down(qi * ndiag + r, _sb_diag_mask(tq, r), c_lf)
        lax.fori_loop(0, qi * ndiag, lambda i, c: down(qi * ndiag - 1 - i, None, c), c_lf)

        def up(kb, carry):
            dq, c_p = carry
            rows = rows_of(kb)
            p = [p_scr[hd, kb] for hd in heads]
            before = _each(lambda x, c: _suffix_sums(x, earlier) + c, p, c_p)
            dzz = _each(lambda x, b, hd: x * oms_scr[hd, kb] - sig_scr[hd, kb] * b, p, before, list(heads))
            dk = _each(lambda x, q: _dot(x, q, TN), dzz, qs)
            dq = _each(lambda a, x, ln: a + _dot(x, k_ref[rows, ln], NN), dq, dzz, lanes)
            for hd in heads:
                dk_acc[rows, lanes[hd]] += dk[hd]
            return tuple(dq), tuple(_each(lambda c, x: c + jnp.sum(x, axis=1, keepdims=True), c_p, p))

        init = (tuple(jnp.zeros((tq, D_HEAD), F32) for _ in heads), tuple(jnp.zeros((tq, 1), F32) for _ in heads))
        dq, _ = lax.fori_loop(0, (qi + 1) * ndiag, up, init)
        for hd in heads:
            dq_ref[:, lanes[hd]] = (dq[hd] * scale).astype(BF16)

        @pl.when(qi == nq - 1)
        def _():
            dk_ref[...] = dk_acc[...].astype(BF16)
            dv_ref[...] = dv_acc[...].astype(BF16)

    q_spec, k_spec, v_spec, z_spec, blk, full = _sb_specs(t, tq)
    o = jax.ShapeDtypeStruct((t, D_MODEL), BF16)
    w = SB_HP * LANE
    return pl.pallas_call(
        body,
        name="sb_bwd",
        grid=(N_HEADS // SB_HP, t // tq),
        in_specs=[q_spec, k_spec, v_spec, z_spec, blk, blk],
        out_specs=[blk, full, full, blk],
        out_shape=[o, o, o, o],
        scratch_shapes=[pltpu.VMEM((t, w), F32), pltpu.VMEM((t, w), F32)]
        + [pltpu.VMEM((SB_HP, t // SB_TK, tq, SB_TK), F32)] * 3,
    )(proj, proj, proj, proj, oraw, do)


def _mem_kv_fn(mem, mg, w):
    return mm_nn(_rmsnorm(mem, mg), w)


def _mem_kv(mem, mg, w):
    def body(m_ref, g_ref, w_ref, o_ref):
        o_ref[...] = _mem_kv_fn(m_ref[...], g_ref[...], w_ref[...])

    return pl.pallas_call(body, name="mem_kv", out_shape=jax.ShapeDtypeStruct((MEM_LEN, 2 * MEM_W), F32))(mem, mg, w)


def _mem_kv_bwd(mem, mg, w, dmkv):
    def body(m_ref, g_ref, w_ref, d_ref, dg_ref, dw_ref):
        _, vjp = jax.vjp(_mem_kv_fn, m_ref[...], g_ref[...], w_ref[...].astype(F32))
        _, dg, dw = vjp(d_ref[...])
        dg_ref[...] = dg
        dw_ref[...] = dw

    return pl.pallas_call(
        body, name="mem_kv_bwd",
        out_shape=[jax.ShapeDtypeStruct((1, D_MODEL), F32), jax.ShapeDtypeStruct((D_MODEL, 2 * MEM_W), F32)],
    )(mem, mg, w, dmkv)


def _mem_attn(proj, mkv, tm=256):
    t = proj.shape[0]
    tm = min(tm, t)

    def body(q_ref, z_ref, kv_ref, o_ref):
        o_ref[...] = _mem_fn(q_ref[...], z_ref[...], kv_ref[...]).astype(BF16)

    return pl.pallas_call(
        body,
        name="mem_attn",
        grid=(t // tm,),
        in_specs=[pl.BlockSpec((tm, MEM_W), lambda i: (i, O_MQ // MEM_W)),
                  pl.BlockSpec((tm, MEM_W), lambda i: (i, O_MZ // MEM_W)),
                  pl.BlockSpec((MEM_LEN, 2 * MEM_W), lambda i: (0, 0))],
        out_specs=pl.BlockSpec((tm, MEM_W), lambda i: (i, 0)),
        out_shape=jax.ShapeDtypeStruct((t, MEM_W), BF16),
    )(proj, proj, mkv)


def _mem_attn_bwd(proj, mkv, do, tm=256):
    t = proj.shape[0]
    tm = min(tm, t)

    def body(q_ref, z_ref, kv_ref, do_ref, dq_ref, dz_ref, dkv_ref):
        _, vjp = jax.vjp(_mem_fn, q_ref[...], z_ref[...], kv_ref[...])
        dq, dz, dkv = vjp(do_ref[...].astype(F32))
        dq_ref[...] = dq.astype(BF16)
        dz_ref[...] = dz.astype(BF16)

        @pl.when(pl.program_id(0) == 0)
        def _():
            dkv_ref[...] = jnp.zeros_like(dkv_ref)

        dkv_ref[...] += dkv

    blk = pl.BlockSpec((tm, MEM_W), lambda i: (i, 0))
    kv = pl.BlockSpec((MEM_LEN, 2 * MEM_W), lambda i: (0, 0))
    return pl.pallas_call(
        body,
        name="mem_attn_bwd",
        grid=(t // tm,),
        in_specs=[pl.BlockSpec((tm, MEM_W), lambda i: (i, O_MQ // MEM_W)),
                  pl.BlockSpec((tm, MEM_W), lambda i: (i, O_MZ // MEM_W)), kv, blk],
        out_specs=[blk, blk, kv],
        out_shape=[jax.ShapeDtypeStruct((t, MEM_W), BF16), jax.ShapeDtypeStruct((t, MEM_W), BF16),
                   jax.ShapeDtypeStruct((MEM_LEN, 2 * MEM_W), F32)],
    )(proj, proj, mkv, do)


def _local_step(x, mem, tgt, norm_g, mem_norm_g, w_al, conv_w, alog_row, dtb_row, dn_norm_g, w_mem_kv, w_br_dn, w_br_sb,
                w_br_mem, w_out, final_g):
    h, h_t = _norm_in(x, norm_g)
    proj = _matmul(h, w_al, "nn", F32, 2048, 384, 1024, "proj")

    c = _dn_conv(proj, conv_w)
    o_dn, states = _dn_fwd(c, proj, alog_row, dtb_row, dn_norm_g)
    o_sb, o_sb_raw = _sb_fwd(proj)
    mkv = _mem_kv(mem, mem_norm_g, w_mem_kv)
    o_m = _mem_attn(proj, mkv)

    y_dn = _matmul(o_dn, w_br_dn, "nn", F32, 512, 1024, 1024, "y_dn")
    y_sb = _matmul(o_sb, w_br_sb, "nn", F32, 512, 1024, 1024, "y_sb")
    y_m = _matmul(o_m, w_br_mem, "nn", F32, 512, 1024, 1024, "y_m")
    merged = _merge(proj, y_dn, y_sb, y_m)
    mo = _matmul(merged, w_out, "nn", F32, 512, 1024, 1024, "mo")
    loss, dout, d_final_g = _loss_head(x, mo, final_g, tgt)

    dmerged = _matmul(dout, w_out, "nt", F32, 512, 1024, 1024, "dmerged")
    dw_out = _matmul(merged, dout, "tn", F32, 256, 1024, 2048, "dw_out")
    dy_dn, dy_sb, dy_m, dg_dn, dg_sb, dg_m = _merge_bwd(proj, y_dn, y_sb, y_m, dmerged)
    do_dn = _matmul(dy_dn, w_br_dn, "nt", BF16, 512, 1024, 1024, "do_dn")
    do_sb = _matmul(dy_sb, w_br_sb, "nt", BF16, 512, 1024, 1024, "do_sb")
    do_m = _matmul(dy_m, w_br_mem, "nt", BF16, 512, 256, 1024, "do_m")
    dw_br_dn = _matmul(o_dn, dy_dn, "tn", F32, 256, 1024, 2048, "dw_br_dn")
    dw_br_sb = _matmul(o_sb, dy_sb, "tn", F32, 256, 1024, 2048, "dw_br_sb")
    dw_br_mem = _matmul(o_m, dy_m, "tn", F32, 256, 1024, 2048, "dw_br_mem")

    dmq, dmz, dmkv = _mem_attn_bwd(proj, mkv, do_m)
    d_mem_norm_g, dw_mem_kv = _mem_kv_bwd(mem, mem_norm_g, w_mem_kv, dmkv)
    dq_sb, dk_sb, dv_sb, dz_sb = _sb_bwd(proj, o_sb_raw, do_sb)
    dcq, dck, dcv, dz_dn, dba, dscal, d_dn_norm_g = _dn_bwd(c, proj, alog_row, dtb_row, dn_norm_g, states, do_dn)
    dq_dn, dcw_q = _dn_conv_bwd(proj, conv_w, dcq, 0)
    dk_dn, dcw_k = _dn_conv_bwd(proj, conv_w, dck, 1)
    dv_dn, dcw_v = _dn_conv_bwd(proj, conv_w, dcv, 2)
    d_conv_w = jnp.concatenate([dcw_q, dcw_k, dcw_v], axis=1)

    dproj = jnp.concatenate([dq_dn, dk_dn, dv_dn, dz_dn, dq_sb, dk_sb, dv_sb, dz_sb, dmq, dmz, dg_dn, dg_sb, dg_m,
                             dba.astype(BF16)], axis=1)
    dh = _matmul(dproj, w_al, "nt", F32, 512, 1024, 3968, "dh")
    dw_al = _matmul(h_t, dproj, "nn", F32, 1024, 384, 2048, "dw_al")
    grad_x, d_norm_g = _norm_in_bwd(x, norm_g, dh, dout)
    return dict(loss=loss, grad_x=grad_x, norm_g=d_norm_g, mem_norm_g=d_mem_norm_g, w_al=dw_al, conv_w=d_conv_w,
                scal=dscal, dn_norm_g=d_dn_norm_g, w_mem_kv=dw_mem_kv, w_br_dn=dw_br_dn, w_br_sb=dw_br_sb,
                w_br_mem=dw_br_mem, w_out=dw_out, final_g=d_final_g)


MESH = pl.DeviceIdType.MESH
ANY = pl.BlockSpec(memory_space=pl.ANY)


def _position():
    return lax.axis_index("x"), lax.axis_index("y"), lax.axis_index("c")


def _all_gather(xs, name):
    n = len(xs)

    def body(*refs):
        x_refs, o_refs = refs[:n], refs[n:2 * n]
        send_sems, recv_sems, local_sems = refs[2 * n:]
        x, y, c = _position()
        me, sibling = (x, y, c), (x, y, 1 - c)
        chips = [(1 - x, y), (x, 1 - y), (1 - x, 1 - y)]

        def slot(p):
            return 4 * p[0] + 2 * p[1] + p[2]

        def copy(a, k, block, to, src=None):
            dst = o_refs[a].at[slot(block)]
            return pltpu.make_async_remote_copy(
                src_ref=dst if src is None else src, dst_ref=dst, send_sem=send_sems.at[7 * a + k],
                recv_sem=recv_sems.at[7 * a + k], device_id=to, device_id_type=MESH)

        mine = [pltpu.make_async_copy(x_refs[a], o_refs[a].at[slot(me)], local_sems.at[a]) for a in range(n)]
        for cp in mine:
            cp.start()
        first = []
        for a in range(n):
            first.append(copy(a, 0, me, sibling, src=x_refs[a]))
            first += [copy(a, 1 + j, me, (*chip, c), src=x_refs[a]) for j, chip in enumerate(chips)]
        for cp in first:
            cp.start()
        passed = []
        for j, chip in enumerate(chips):
            for a in range(n):
                copy(a, 1 + j, (*chip, c), me).wait_recv()
                cp = copy(a, 4 + j, (*chip, c), sibling)
                cp.start()
                passed.append(cp)
        for a in range(n):
            copy(a, 0, sibling, me).wait_recv()
            for j, chip in enumerate(chips):
                copy(a, 4 + j, (*chip, 1 - c), me).wait_recv()
        for cp in first + passed:
            cp.wait_send()
        for cp in mine:
            cp.wait()

    return pl.pallas_call(
        body,
        name=name,
        in_specs=[ANY] * n,
        out_specs=[ANY] * n,
        out_shape=[jax.ShapeDtypeStruct((N_DEV, *v.shape), v.dtype) for v in xs],
        scratch_shapes=[pltpu.SemaphoreType.DMA((7 * n,)), pltpu.SemaphoreType.DMA((7 * n,)),
                        pltpu.SemaphoreType.DMA((n,))],
    )(*xs)


def _window_view(ref, dest):
    return ref.at[:, pl.ds(LANE * WIN_START[dest], WIN_W)]


def _halving_stage(xs, axis, name, out_dtype, windowed=()):
    n_arr = len(xs)
    metas = []
    for k, v in enumerate(xs):
        if k in windowed:
            metas.append((N_DEV // 2, v.shape[0], WIN_W))
        else:
            assert v.shape[1] == 2
            metas.append((v.shape[0], v.shape[2], v.shape[3]))
    chunk = [min(r, 1 << int(math.log2((1 << 17) // c))) for (_, r, c) in metas]
    assert all(r % ch == 0 and ch % 8 == 0 for ch, (_, r, _) in zip(chunk, metas))
    offs = [sum(m[0] for m in metas[:k]) for k in range(n_arr)]
    n_sem = sum(m[0] for m in metas)

    def body(*refs):
        x_refs = refs[:n_arr]
        o_refs = refs[n_arr:2 * n_arr]
        land_refs = refs[2 * n_arr:3 * n_arr]
        rest = refs[3 * n_arr:]
        bufs = rest[:3 * n_arr]
        send_sems, recv_sems, in_sems, out_sems = rest[3 * n_arr:]
        pos = dict(zip("xyc", _position()))
        bit = pos[axis]
        peer = tuple(1 - pos[a] if a == axis else pos[a] for a in "xyc")

        def view(k, i, b):
            if k in windowed:
                return _window_view(x_refs[k], 2 * i + b)
            return x_refs[k].at[i, b]

        def add_blocks(k, a_view, b_view, o_view):
            _, rows, _ = metas[k]
            ch = chunk[k]
            nch = rows // ch
            va, vb, vo = bufs[3 * k:3 * k + 3]

            def rows_of(j):
                return pl.ds(pl.multiple_of(j * ch, 8), ch)

            def loads(j, s):
                return (pltpu.make_async_copy(a_view.at[rows_of(j), :], va.at[s], in_sems.at[0, s]),
                        pltpu.make_async_copy(b_view.at[rows_of(j), :], vb.at[s], in_sems.at[1, s]))

            def store(j, s):
                return pltpu.make_async_copy(vo.at[s], o_view.at[rows_of(j), :], out_sems.at[s])

            for cp in loads(0, 0):
                cp.start()

            def step(j, _):
                s = lax.rem(j, 2)

                @pl.when(j + 1 < nch)
                def _():
                    for cp in loads(j + 1, 1 - s):
                        cp.start()

                for cp in loads(j, s):
                    cp.wait()

                @pl.when(j >= 2)
                def _():
                    store(j - 2, s).wait()

                vo[s] = (va[s] + vb[s]).astype(vo.dtype)
                store(j, s).start()
                return 0

            lax.fori_loop(0, nch, step, 0)
            for j in range(max(0, nch - 2), nch):
                store(j, j % 2).wait()

        for b in (0, 1):
            @pl.when(bit == b)
            def _(b=b):
                sends = []
                for k in range(n_arr):
                    for i in range(metas[k][0]):
                        cp = pltpu.make_async_remote_copy(
                            src_ref=view(k, i, 1 - b), dst_ref=land_refs[k].at[i], send_sem=send_sems.at[offs[k] + i],
                            recv_sem=recv_sems.at[offs[k] + i], device_id=peer, device_id_type=MESH)
                        cp.start()
                        sends.append(cp)
                idx = 0
                for k in range(n_arr):
                    for i in range(metas[k][0]):
                        sends[idx].wait_recv()
                        add_blocks(k, view(k, i, b), land_refs[k].at[i], o_refs[k].at[i])
                        idx += 1
                for cp in sends:
                    cp.wait_send()

    out_shape = [jax.ShapeDtypeStruct(m, out_dtype) for m in metas]
    land_shape = [jax.ShapeDtypeStruct(m, F32) for m in metas]
    scratch = []
    for k in range(n_arr):
        scratch += [pltpu.VMEM((2, chunk[k], metas[k][2]), F32)] * 2 + [pltpu.VMEM((2, chunk[k], metas[k][2]), out_dtype)]
    scratch += [pltpu.SemaphoreType.DMA((n_sem,)), pltpu.SemaphoreType.DMA((n_sem,)),
                pltpu.SemaphoreType.DMA((2, 2)), pltpu.SemaphoreType.DMA((2,))]
    outs = pl.pallas_call(
        body,
        name=name,
        in_specs=[ANY] * n_arr,
        out_specs=[ANY] * (2 * n_arr),
        out_shape=out_shape + land_shape,
        scratch_shapes=scratch,
    )(*xs)
    return outs[:n_arr]


def _hbm_add(a_view, b_view, o_view, bufs, in_sems, out_sems, ch):
    rows = a_view.shape[0]
    nch = rows // ch
    va, vb, vo = bufs

    def rows_of(j):
        return pl.ds(pl.multiple_of(j * ch, 16), ch)

    def loads(j, s):
        return (pltpu.make_async_copy(a_view.at[rows_of(j), :], va.at[s], in_sems.at[0, s]),
                pltpu.make_async_copy(b_view.at[rows_of(j), :], vb.at[s], in_sems.at[1, s]))

    def store(j, s):
        return pltpu.make_async_copy(vo.at[s], o_view.at[rows_of(j), :], out_sems.at[s])

    for cp in loads(0, 0):
        cp.start()

    def step(j, _):
        s = lax.rem(j, 2)

        @pl.when(j + 1 < nch)
        def _():
            for cp in loads(j + 1, 1 - s):
                cp.start()

        for cp in loads(j, s):
            cp.wait()

        @pl.when(j >= 2)
        def _():
            store(j - 2, s).wait()

        vo[s] = (va[s].astype(F32) + vb[s].astype(F32)).astype(vo.dtype)
        store(j, s).start()
        return 0

    lax.fori_loop(0, nch, step, 0)
    for j in range(max(0, nch - 2), nch):
        store(j, j % 2).wait()


def _xy_stage(xs, first, name):
    n_arr = len(xs)
    if first:
        shapes = [(v.shape[2] // 2, v.shape[3]) for v in xs]
        ins = list(xs)
    else:
        shapes = [(a.shape[1], a.shape[2]) for a, _ in xs]
        ins = [v for pair in xs for v in pair]
    n_blk = 2 if first else 1
    out_dtype = BF16 if first else F32
    chunk = [min(r, 1 << int(math.log2((1 << 17) // c))) for (r, c) in shapes]
    assert all(r % ch == 0 and ch % 16 == 0 for ch, (r, _) in zip(chunk, shapes))
    n_sem = 2 * n_blk * n_arr

    def body(*refs):
        n_in = len(ins)
        in_refs = refs[:n_in]
        n_out = 2 * n_arr if first else n_arr
        o_refs = refs[n_in:n_in + n_out]
        land = refs[n_in + n_out:n_in + n_out + 2 * n_arr]
        rest = refs[n_in + n_out + 2 * n_arr:]
        bufs = rest[:3 * n_arr]
        send_sems, recv_sems, in_sems, out_sems = rest[3 * n_arr:]
        x, y, c = _position()
        peers = {"x": (1 - x, y, c), "y": (x, 1 - y, c)}
        jobs = []
        for k in range(n_arr):
            r, _ = shapes[k]
            half_a, half_b = pl.ds(0, r), pl.ds(r, r)
            if first:
                src = in_refs[k]
                for i in range(2):
                    jobs.append((k, src.at[i, 1 - y, half_a, :], src.at[i, y, half_a, :], land[2 * k].at[i],
                                 o_refs[2 * k].at[i], "y"))
                    jobs.append((k, src.at[1 - x, i, half_b, :], src.at[x, i, half_b, :], land[2 * k + 1].at[i],
                                 o_refs[2 * k + 1].at[i], "x"))
            else:
                a1, b1 = in_refs[2 * k], in_refs[2 * k + 1]
                jobs.append((k, a1.at[1 - x], a1.at[x], land[2 * k], o_refs[k].at[half_a, :], "x"))
                jobs.append((k, b1.at[1 - y], b1.at[y], land[2 * k + 1], o_refs[k].at[half_b, :], "y"))
        sends = []
        for n, (k, send, _, landing, _, axis) in enumerate(jobs):
            cp = pltpu.make_async_remote_copy(src_ref=send, dst_ref=landing, send_sem=send_sems.at[n],
                                              recv_sem=recv_sems.at[n], device_id=peers[axis], device_id_type=MESH)
            cp.start()
            sends.append(cp)
        for cp, (k, _, kept, landing, out, _) in zip(sends, jobs):
            cp.wait_recv()
            _hbm_add(kept, landing, out, bufs[3 * k:3 * k + 3], in_sems, out_sems, chunk[k])
        for cp in sends:
            cp.wait_send()

    if first:
        out_shape = [jax.ShapeDtypeStruct((2, r, c), BF16) for (r, c) in shapes for _ in range(2)]
        land_shape = out_shape
    else:
        out_shape = [jax.ShapeDtypeStruct((2 * r, c), F32) for (r, c) in shapes]
        land_shape = [jax.ShapeDtypeStruct((r, c), BF16) for (r, c) in shapes for _ in range(2)]
    scratch = []
    for k in range(n_arr):
        scratch += [pltpu.VMEM((2, chunk[k], shapes[k][1]), BF16)] * 2 + [pltpu.VMEM((2, chunk[k], shapes[k][1]), out_dtype)]
    scratch += [pltpu.SemaphoreType.DMA((n_sem,)), pltpu.SemaphoreType.DMA((n_sem,)),
                pltpu.SemaphoreType.DMA((2, 2)), pltpu.SemaphoreType.DMA((2,))]
    outs = pl.pallas_call(
        body,
        name=name,
        in_specs=[ANY] * len(ins),
        out_specs=[ANY] * (len(out_shape) + len(land_shape)),
        out_shape=out_shape + land_shape,
        scratch_shapes=scratch,
    )(*ins)
    outs = outs[:len(out_shape)]
    return [(outs[2 * k], outs[2 * k + 1]) for k in range(n_arr)] if first else list(outs)


def _reduce_scatter(dw_al, blocks):
    xs = [dw_al] + [b.reshape(N_DEV // 2, 2, *b.shape[1:]) for b in blocks]
    ys = _halving_stage(xs, "c", "rs_c", BF16, windowed=(0,))
    pairs = _xy_stage([v.reshape(2, 2, *v.shape[1:]) for v in ys], True, "rs_xy1")
    return _xy_stage(pairs, False, "rs_xy2")


def _sum_slots(gs):
    n = len(gs)

    def body(*refs):
        for g_ref, o_ref in zip(refs[:n], refs[n:]):
            acc = g_ref[0]
            for d in range(1, N_DEV):
                acc = acc + g_ref[d]
            o_ref[...] = acc

    return pl.pallas_call(body, name="sum_slots",
                          out_shape=[jax.ShapeDtypeStruct(g.shape[1:], g.dtype) for g in gs])(*gs)


def _assemble_w_al(wins, bas):
    ba_tile = O_BA // LANE
    assert W_AL // LANE == ba_tile + 1
    rows = wins.shape[1]
    n_buf = 3
    ends = [WIN_START[d + 1] if d + 1 < N_DEV else ba_tile for d in range(N_DEV)]

    def body(w_ref, ba_ref, o_ref, buf, ld_sems, st_sems, ba_sem):
        def load(d):
            return pltpu.make_async_copy(w_ref.at[d], buf.at[d % n_buf], ld_sems.at[d % n_buf])

        def store(d):
            n = LANE * (ends[d] - WIN_START[d])
            return pltpu.make_async_copy(buf.at[d % n_buf, :, pl.ds(0, n)],
                                         o_ref.at[:, pl.ds(LANE * WIN_START[d], n)], st_sems.at[d % n_buf])

        ba_copy = pltpu.make_async_copy(ba_ref.at[BA_DEV], o_ref.at[:, pl.ds(LANE * ba_tile, LANE)], ba_sem)
        ba_copy.start()
        load(0).start()
        for d in range(N_DEV):
            if d + 1 < N_DEV:
                if d + 1 >= n_buf:
                    store(d + 1 - n_buf).wait()
                load(d + 1).start()
            load(d).wait()
            if d > 0:
                ov = LANE * (WIN_START[d - 1] + WIN_TILES - WIN_START[d])
                buf[d % n_buf, :, :ov] = buf[d % n_buf, :, :ov] + buf[(d - 1) % n_buf, :, WIN_W - ov:]
            store(d).start()
        for d in range(N_DEV - n_buf, N_DEV):
            store(d).wait()
        ba_copy.wait()

    return pl.pallas_call(
        body,
        name="assemble_w_al",
        in_specs=[ANY, ANY],
        out_specs=ANY,
        out_shape=jax.ShapeDtypeStruct((rows, W_AL), wins.dtype),
        scratch_shapes=[pltpu.VMEM((n_buf, rows, WIN_W), wins.dtype), pltpu.SemaphoreType.DMA((n_buf,)),
                        pltpu.SemaphoreType.DMA((n_buf,)), pltpu.SemaphoreType.DMA],
    )(wins, bas)


def _adamw_math(w, g, m, v):
    m_new = ADAM_B1 * m + (1.0 - ADAM_B1) * g
    v_new = ADAM_B2 * v + (1.0 - ADAM_B2) * (g * g)
    m_hat = m_new / (1.0 - ADAM_B1 ** ADAM_STEP)
    v_hat = v_new / (1.0 - ADAM_B2 ** ADAM_STEP)
    return -ADAM_LR * (m_hat / (jnp.sqrt(v_hat) + ADAM_EPS) + ADAM_WD * w), m_new, v_new


def _adamw(w, g, m, v, name, tm=256):
    _, r, c = w.shape
    assert r % tm == 0

    def body(w_ref, g_ref, m_ref, v_ref, d_ref, nm_ref, nv_ref):
        d_ref[...], nm_ref[...], nv_ref[...] = _adamw_math(w_ref[...], g_ref[...], m_ref[...], v_ref[...])

    blk = pl.BlockSpec((1, tm, c), lambda i: (0, i, 0))
    o = jax.ShapeDtypeStruct(w.shape, F32)
    return pl.pallas_call(body, name=name, grid=(r // tm,), in_specs=[blk] * 4, out_specs=[blk] * 3,
                          out_shape=[o, o, o])(w, g, m, v)


def _adamw_many(ws, gs, ms, vs, name):
    n = len(ws)

    def body(*refs):
        for k in range(n):
            w_ref, g_ref, m_ref, v_ref = (refs[j * n + k] for j in range(4))
            d_ref, nm_ref, nv_ref = (refs[(4 + j) * n + k] for j in range(3))
            d_ref[...], nm_ref[...], nv_ref[...] = _adamw_math(w_ref[...], g_ref[...], m_ref[...], v_ref[...])

    shapes = [jax.ShapeDtypeStruct(w.shape, F32) for w in ws]
    outs = pl.pallas_call(body, name=name, out_shape=shapes * 3)(*ws, *gs, *ms, *vs)
    return outs[:n], outs[n:2 * n], outs[2 * n:]


def _select(me, table):
    return sum(jnp.where(me == d, jnp.int32(v), jnp.int32(0)) for d, v in enumerate(table))


WIN_SHIFT = tuple(SHARD_W * d - LANE * WIN_START[d] for d in range(N_DEV))
PAD_L = 256
PAD_R = 256


def _shard_to_window(shard, me):
    shift = _select(me, WIN_SHIFT)
    start = _select(me, WIN_START)
    padded = jnp.pad(shard, ((0, 0), (PAD_L, PAD_R)))
    rows = shard.shape[0]
    lo = lax.dynamic_slice(padded, (0, PAD_L - shift), (rows, WIN_W))
    hi = lax.dynamic_slice(padded, (0, PAD_L - shift + N_BA), (rows, WIN_W))
    aligned = LANE * start + lax.broadcasted_iota(jnp.int32, (1, WIN_W), 1)
    return jnp.where(aligned >= ORIG_BA, hi, lo)


def _window_to_shard(win, ba_grad, me):
    shift = _select(me, WIN_SHIFT)
    rows = win.shape[0]
    padded = jnp.pad(win, ((0, 0), (N_BA, PAD_R)))
    lo = lax.dynamic_slice(padded, (0, N_BA + shift), (rows, SHARD_W))
    hi = lax.dynamic_slice(padded, (0, shift), (rows, SHARD_W))
    orig = SHARD_W * me + lax.broadcasted_iota(jnp.int32, (1, SHARD_W), 1)
    ba_full = lax.dynamic_update_slice(jnp.zeros((rows, SHARD_W), win.dtype), ba_grad, (0, BA_LOCAL))
    return jnp.where(orig < ORIG_BA, lo, jnp.where(orig >= ORIG_BA + N_BA, hi, ba_full))


def _pad_row(v, width=D_MODEL):
    v = v.reshape(1, -1)
    return jnp.pad(v, ((0, 0), (0, width - v.shape[1])))


def _slab(v, rows=8):
    return jnp.pad(v, ((0, rows - v.shape[0]), (0, D_MODEL - v.shape[1])))


def kernel(x, mem, norm_g, mem_norm_g, w_in, conv_w, a_log, dt_bias, dn_norm_g, w_mem_kv, w_br_dn, w_br_sb, w_br_mem, w_out, final_g, loss_target, m_norm_g, m_mem_norm_g, m_w_in, m_conv_w, m_a_log, m_dt_bias, m_dn_norm_g, m_w_mem_kv, m_w_br_dn, m_w_br_sb, m_w_br_mem, m_w_out, m_final_g, v_norm_g, v_mem_norm_g, v_w_in, v_conv_w, v_a_log, v_dt_bias, v_dn_norm_g, v_w_mem_kv, v_w_br_dn, v_w_br_sb, v_w_br_mem, v_w_out, v_final_g):
    xi, yi, ci = _position()
    me = 4 * xi + 2 * yi + ci

    shard = w_in[0]
    win = _shard_to_window(shard, me).astype(BF16)
    ba = jnp.pad(shard[:, BA_LOCAL:BA_LOCAL + N_BA], ((0, 0), (0, LANE - N_BA))).astype(BF16)
    g_win, g_ba, g_kv, g_dn, g_sb, g_out, g_mem, g_conv = _all_gather(
        [win, ba, w_mem_kv[0].astype(BF16), w_br_dn[0].astype(BF16), w_br_sb[0].astype(BF16), w_out[0].astype(BF16),
         w_br_mem[0].astype(BF16), conv_w[0]], "gather_weights")
    w_al = _assemble_w_al(g_win, g_ba)
    w_mem_kv_f = g_kv.reshape(D_MODEL, 2 * MEM_W)
    w_br_dn_f = g_dn.reshape(D_MODEL, D_MODEL)
    w_br_sb_f = g_sb.reshape(D_MODEL, D_MODEL)
    w_out_f = g_out.reshape(D_MODEL, D_MODEL)
    w_br_mem_f = g_mem.transpose(1, 0, 2).reshape(MEM_W, D_MODEL)
    conv_w_f = g_conv.transpose(1, 0, 2).reshape(CONV_K, 3 * D_MODEL)

    r = _local_step(x[0], mem[0], loss_target[0], norm_g, mem_norm_g, w_al, conv_w_f, _pad_row(a_log, LANE),
                    _pad_row(dt_bias, LANE), dn_norm_g, w_mem_kv_f, w_br_dn_f, w_br_sb_f, w_br_mem_f, w_out_f,
                    final_g.reshape(1, D_MODEL))

    dw_al = r["w_al"]
    g_win, g_kv, g_dn, g_sb, g_out, g_mem = _reduce_scatter(dw_al, [
        r["w_mem_kv"].reshape(N_DEV, D_MODEL // N_DEV, 2 * MEM_W),
        r["w_br_dn"].reshape(N_DEV, D_MODEL // N_DEV, D_MODEL),
        r["w_br_sb"].reshape(N_DEV, D_MODEL // N_DEV, D_MODEL),
        r["w_out"].reshape(N_DEV, D_MODEL // N_DEV, D_MODEL),
        r["w_br_mem"].reshape(MEM_W, N_DEV, D_MODEL // N_DEV).transpose(1, 0, 2)])
    parts = [r["norm_g"], r["mem_norm_g"], r["final_g"], r["dn_norm_g"], r["scal"], r["loss"], r["conv_w"],
             dw_al[:, O_BA:O_BA + LANE]]
    s_norm_g, s_mem_norm_g, s_final_g, s_dn_norm_g, s_scal, s_loss, s_conv, s_ba = _sum_slots(
        _all_gather(parts, "gather_small"))
    loss = s_loss[0, 0]
    cw = conv_w.shape[2]
    g_conv = lax.dynamic_slice(s_conv, (0, cw * me), (CONV_K, cw))
    g_w_in = _window_to_shard(g_win, s_ba[:, :N_BA], me)
    grads = dict(norm_g=s_norm_g, mem_norm_g=s_mem_norm_g, w_in=g_w_in[None], conv_w=g_conv[None],
                 a_log=s_scal[0:1, :N_HEADS], dt_bias=s_scal[1:2, :N_HEADS], dn_norm_g=s_dn_norm_g, w_mem_kv=g_kv[None],
                 w_br_dn=g_dn[None], w_br_sb=g_sb[None], w_br_mem=g_mem[None], w_out=g_out[None],
                 final_g=s_final_g.reshape(D_MODEL))

    params = dict(norm_g=(norm_g, m_norm_g, v_norm_g), mem_norm_g=(mem_norm_g, m_mem_norm_g, v_mem_norm_g),
                  w_in=(w_in, m_w_in, v_w_in), conv_w=(conv_w, m_conv_w, v_conv_w), a_log=(a_log, m_a_log, v_a_log),
                  dt_bias=(dt_bias, m_dt_bias, v_dt_bias), dn_norm_g=(dn_norm_g, m_dn_norm_g, v_dn_norm_g),
                  w_mem_kv=(w_mem_kv, m_w_mem_kv, v_w_mem_kv), w_br_dn=(w_br_dn, m_w_br_dn, v_w_br_dn),
                  w_br_sb=(w_br_sb, m_w_br_sb, v_w_br_sb), w_br_mem=(w_br_mem, m_w_br_mem, v_w_br_mem),
                  w_out=(w_out, m_w_out, v_w_out), final_g=(final_g, m_final_g, v_final_g))
    order = list(params)
    deltas, new_m, new_v = {}, {}, {}
    deltas["w_in"], new_m["w_in"], new_v["w_in"] = _adamw(w_in, grads["w_in"], m_w_in, v_w_in, "adamw_w_in")
    rest = [nm for nm in order if nm != "w_in"]

    def two_d(a):
        return a.reshape(1, -1) if a.ndim == 1 else a

    d_l, m_l, v_l = _adamw_many([two_d(params[nm][0]) for nm in rest], [two_d(grads[nm]) for nm in rest],
                                [two_d(params[nm][1]) for nm in rest], [two_d(params[nm][2]) for nm in rest], "adamw_rest")
    for k, nm in enumerate(rest):
        shp = params[nm][0].shape
        deltas[nm], new_m[nm], new_v[nm] = d_l[k].reshape(shp), m_l[k].reshape(shp), v_l[k].reshape(shp)
    return (loss, r["grad_x"][None], *[grads[nm] for nm in order], *[deltas[nm] for nm in order],
            *[new_m[nm] for nm in order], *[new_v[nm] for nm in order])
```

```python
import functools
import math

import jax
import jax.numpy as jnp
from jax import lax
from jax.experimental import pallas as pl
from jax.experimental.pallas import tpu as pltpu

F32 = jnp.float32
BF16 = jnp.bfloat16

D_MODEL = 1024
N_DEV = 8
N_HEADS = 8
D_HEAD = 128
DN_CHUNK = 64
CONV_K = 4
MEM_LEN = 256
MEM_HEADS = 4
MEM_DH = 64
MEM_W = MEM_HEADS * MEM_DH
NORM_EPS = 1e-6
IN_WIDTH = 11792
SHARD_W = IN_WIDTH // N_DEV

LANE = 128
SUPER = 2 * DN_CHUNK

O_QKV_DN = 0
O_Z_DN = 3072
O_QKV_SB = 4096
O_Z_SB = 7168
O_MQ = 8192
O_MZ = 8448
O_GATES = 8704
O_BA = 11776
W_AL = 11904
ORIG_BA = 4096
N_BA = 16

WIN_TILES = 13
WIN_W = WIN_TILES * LANE


def _aligned_col(o):
    return o if o < ORIG_BA else o - N_BA


WIN_START = tuple(min(_aligned_col(SHARD_W * d) // LANE, (W_AL // LANE) - WIN_TILES) for d in range(N_DEV))
WIN_OFF = tuple(_aligned_col(SHARD_W * d) - LANE * WIN_START[d] if SHARD_W * d >= ORIG_BA + N_BA or SHARD_W * d < ORIG_BA
                else None for d in range(N_DEV))
BA_DEV = ORIG_BA // SHARD_W
BA_LOCAL = ORIG_BA - BA_DEV * SHARD_W

ADAM_LR = 0.001
ADAM_B1 = 0.9
ADAM_B2 = 0.999
ADAM_EPS = 1e-08
ADAM_WD = 0.01
ADAM_STEP = 10

NN = (((1,), (0,)), ((), ()))
NT = (((1,), (1,)), ((), ()))
TN = (((0,), (0,)), ((), ()))


def _dot(a, b, dims):
    return lax.dot_general(a.astype(BF16), b.astype(BF16), dims, preferred_element_type=F32)


def _split2(a):
    hi = a.astype(BF16)
    lo = (a - hi.astype(F32)).astype(BF16)
    return hi, lo


def _dot3(a, b, dims):
    ah, al = _split2(a)
    bh, bl = _split2(b)
    d = functools.partial(lax.dot_general, dimension_numbers=dims, preferred_element_type=F32)
    return d(ah, bh) + (d(ah, bl) + d(al, bh))


def _sel_dot_impl(sel01, x, dims):
    sel = sel01.astype(BF16)
    h1 = x.astype(BF16)
    r1 = x - h1.astype(F32)
    h2 = r1.astype(BF16)
    h3 = (r1 - h2.astype(F32)).astype(BF16)
    d = functools.partial(lax.dot_general, dimension_numbers=dims, preferred_element_type=F32)
    return d(sel, h1) + (d(sel, h2) + d(sel, h3))


@jax.custom_vjp
def _sel_dot(sel01, x):
    return _sel_dot_impl(sel01, x, NN)


_sel_dot.defvjp(lambda s, x: (_sel_dot(s, x), s),
                lambda s, g: (jnp.zeros_like(s), _sel_dot_impl(s, g, TN)))


def _make_mm(dotfn):
    @jax.custom_vjp
    def nn(a, b):
        return dotfn(a, b, NN)

    @jax.custom_vjp
    def nt(a, b):
        return dotfn(a, b, NT)

    @jax.custom_vjp
    def tn(a, b):
        return dotfn(a, b, TN)

    nn.defvjp(lambda a, b: (nn(a, b), (a, b)), lambda r, g: (nt(g, r[1]), tn(r[0], g)))
    nt.defvjp(lambda a, b: (nt(a, b), (a, b)), lambda r, g: (nn(g, r[1]), tn(g, r[0])))
    tn.defvjp(lambda a, b: (tn(a, b), (a, b)), lambda r, g: (nt(r[1], g), nn(r[0], g)))
    return nn, nt, tn


mm_nn, mm_nt, mm_tn = _make_mm(_dot)
mm3_nn, mm3_nt, mm3_tn = _make_mm(_dot3)


def _sigmoid(x):
    return jax.nn.sigmoid(x)


def _silu(x):
    return x * _sigmoid(x)


def _softplus_parts(x):
    sp = jnp.log1p(jnp.exp(-jnp.abs(x)))
    return jnp.maximum(x, 0.0) + sp, jnp.maximum(-x, 0.0) + sp


def _rmsnorm(x, g):
    return x * lax.rsqrt(jnp.mean(x * x, axis=-1, keepdims=True) + NORM_EPS) * g


def _iota2(shape, dim):
    return lax.broadcasted_iota(jnp.int32, shape, dim)


def _div64(i):
    return lax.shift_right_logical(i, jnp.full(i.shape, 6, jnp.int32))


def _each(f, *lists):
    return [f(*a) for a in zip(*lists)]


@jax.custom_vjp
def _inv_unit_lower(ms):
    n = ms[0].shape[0]
    eye = (_iota2((n, n), 0) == _iota2((n, n), 1)).astype(F32)
    rs = [eye - m for m in ms]
    ps = ms
    for _ in range(5):
        ps = _each(mm3_nn, ps, ps)
        rs = _each(lambda r, p: r + mm3_nn(r, p), rs, ps)
    return rs


def _inv_fwd(ms):
    rs = _inv_unit_lower(ms)
    return rs, rs


def _inv_bwd(rs, gs):
    ts = _each(mm3_tn, rs, gs)
    return (_each(lambda t, r: -mm3_nt(t, r), ts, rs),)


_inv_unit_lower.defvjp(_inv_fwd, _inv_bwd)


def _dn_block(cq, ck, cv, bcol, acol, zt, alog, dtb, gn, s0):
    n = SUPER
    h = DN_CHUNK
    row = _iota2((n, n), 0)
    col = _iota2((n, n), 1)
    same = _div64(row) == _div64(col)
    incl = jnp.logical_and(same, row >= col)
    strict = jnp.logical_and(same, row > col)
    incl_f = incl.astype(F32)

    qn = _each(lambda x: x * lax.rsqrt(jnp.sum(x * x, axis=-1, keepdims=True) + NORM_EPS) * (D_HEAD ** -0.5), cq)
    kn = _each(lambda x: x * lax.rsqrt(jnp.sum(x * x, axis=-1, keepdims=True) + NORM_EPS), ck)
    beta = _each(_sigmoid, bcol)
    g = _each(lambda al, ac, dt: -(jnp.exp(al) * _softplus_parts(ac + dt)[0]), alog, acol, dtb)
    gcum = _each(lambda x: _sel_dot(incl_f, jnp.broadcast_to(x, (n, n))), g)
    gam_incl = _each(lambda x: jnp.where(incl, jnp.exp(jnp.where(incl, x - x.T, 0.0)), 0.0), gcum)
    kk = _each(mm_nt, kn, kn)
    t_inv = _inv_unit_lower(_each(lambda b, x, gm: b * x * jnp.where(strict, gm, 0.0), beta, kk, gam_incl))
    eg = _each(jnp.exp, gcum)
    u = _each(lambda t, v, b: mm_nn(t, v * b), t_inv, cv, beta)
    w = _each(lambda t, k, b, e: mm_nn(t, k * (b * e)), t_inv, kn, beta, eg)
    a_intra = _each(lambda q, k, gm: mm_nt(q, k) * gm, qn, kn, gam_incl)
    q_dec = _each(lambda q, e: q * e, qn, eg)
    last0 = _each(lambda x: x[h - 1:h, :], gcum)
    last1 = _each(lambda x: x[n - 1:n, :], gcum)
    k_dec = _each(lambda k, x, l0, l1: k * jnp.exp(jnp.concatenate(
        [jnp.broadcast_to(l0, (h, n)), jnp.broadcast_to(l1, (h, n))], axis=0) - x), kn, gcum, last0, last1)
    v0 = _each(lambda uu, ww, s: uu[:h] - mm_nn(ww[:h], s), u, w, s0)
    o0 = _each(lambda q, s: mm_nn(q[:h], s), q_dec, s0)
    s1 = _each(lambda s, l0, k, v: s * jnp.exp(l0) + mm_tn(k[:h], v), s0, last0, k_dec, v0)
    v1 = _each(lambda uu, ww, s: uu[h:] - mm_nn(ww[h:], s), u, w, s1)
    o1 = _each(lambda q, s: mm_nn(q[h:], s), q_dec, s1)
    s2 = _each(lambda s, l1, k, v: s * jnp.exp(l1) + mm_tn(k[h:], v), s1, last1, k_dec, v1)
    o = _each(lambda a, b, am, x, y: jnp.concatenate([a, b], axis=0) + mm_nn(am, jnp.concatenate([x, y], axis=0)),
              o0, o1, a_intra, v0, v1)
    out = _each(lambda x, z: _rmsnorm(x, gn) * _silu(z), o, zt)
    return out, s2


def _mem_fn(mq, mz, mkv):
    mk = mkv[:, :MEM_W]
    mv = mkv[:, MEM_W:]
    lane = _iota2((1, MEM_W), 1)
    out = jnp.zeros(mq.shape, F32)
    for hd in range(MEM_HEADS):
        hm = (_div64(lane) == hd).astype(F32)
        s = mm_nt(mq * hm, mk) * (1.0 / math.sqrt(MEM_DH))
        s = s - jnp.max(s, axis=-1, keepdims=True)
        e = jnp.exp(s)
        p = e / jnp.sum(e, axis=-1, keepdims=True)
        out = out + mm_nn(p, mv) * hm
    return out * _silu(mz)


def _merge_fn(gd, gs, gm, yd, ys, ym):
    return _sigmoid(gd) * yd + _sigmoid(gs) * ys + _sigmoid(gm) * ym


def _loss_fn(x, mo, fg, tgt):
    y = _rmsnorm(x + mo, fg)
    err = y - tgt
    return 0.5 * jnp.sum(jnp.mean(err * err, axis=-1, keepdims=True), axis=0, keepdims=True)


def _matmul(a, b, mode, out_dtype, tm, tn, tk, name, b_col0=0, n_cols=None):
    if mode == "nn":
        m, kdim = a.shape
        n = b.shape[1] if n_cols is None else n_cols
    elif mode == "nt":
        m, kdim = a.shape
        n = b.shape[0]
    else:
        kdim, m = a.shape
        n = b.shape[1] if n_cols is None else n_cols
    tm, tn, tk = min(tm, m), min(tn, n), min(tk, kdim)
    assert m % tm == 0 and n % tn == 0 and kdim % tk == 0 and b_col0 % tn == 0
    nk = kdim // tk
    jb = b_col0 // tn
    dims = {"nn": NN, "nt": NT, "tn": TN}[mode]

    def body(a_ref, b_ref, o_ref, acc_ref):
        k = pl.program_id(2)
        part = _dot(a_ref[...], b_ref[...], dims)

        @pl.when(k == 0)
        def _():
            acc_ref[...] = part

        @pl.when(k > 0)
        def _():
            acc_ref[...] += part

        @pl.when(k == nk - 1)
        def _():
            o_ref[...] = acc_ref[...].astype(o_ref.dtype)

    if mode == "nn":
        a_spec = pl.BlockSpec((tm, tk), lambda i, j, k: (i, k))
        b_spec = pl.BlockSpec((tk, tn), lambda i, j, k: (k, j + jb))
    elif mode == "nt":
        a_spec = pl.BlockSpec((tm, tk), lambda i, j, k: (i, k))
        b_spec = pl.BlockSpec((tn, tk), lambda i, j, k: (j, k))
    else:
        a_spec = pl.BlockSpec((tk, tm), lambda i, j, k: (k, i))
        b_spec = pl.BlockSpec((tk, tn), lambda i, j, k: (k, j + jb))
    return pl.pallas_call(
        body,
        name=name,
        grid=(m // tm, n // tn, nk),
        in_specs=[a_spec, b_spec],
        out_specs=pl.BlockSpec((tm, tn), lambda i, j, k: (i, j)),
        out_shape=jax.ShapeDtypeStruct((m, n), out_dtype),
        scratch_shapes=[pltpu.VMEM((tm, tn), F32)],
        compiler_params=pltpu.CompilerParams(dimension_semantics=("parallel", "parallel", "arbitrary")),
    )(a, b)


def _norm_in(x, g, tm=256):
    t = x.shape[0]

    def body(x_ref, g_ref, h_ref):
        h_ref[...] = _rmsnorm(x_ref[...], g_ref[...]).astype(BF16)

    return pl.pallas_call(
        body,
        name="norm_in",
        grid=(t // tm,),
        in_specs=[pl.BlockSpec((tm, D_MODEL), lambda i: (i, 0)), pl.BlockSpec((1, D_MODEL), lambda i: (0, 0))],
        out_specs=pl.BlockSpec((tm, D_MODEL), lambda i: (i, 0)),
        out_shape=jax.ShapeDtypeStruct((t, D_MODEL), BF16),
    )(x, g)


def _norm_in_bwd(x, g, dh, dres, tm=256):
    t = x.shape[0]

    def body(x_ref, g_ref, dh_ref, dres_ref, dx_ref, dg_ref):
        _, vjp = jax.vjp(_rmsnorm, x_ref[...], g_ref[...])
        dx, dg = vjp(dh_ref[...])
        dx_ref[...] = dx + dres_ref[...]

        @pl.when(pl.program_id(0) == 0)
        def _():
            dg_ref[...] = jnp.zeros_like(dg_ref)

        dg_ref[...] += dg

    row = pl.BlockSpec((tm, D_MODEL), lambda i: (i, 0))
    vec = pl.BlockSpec((1, D_MODEL), lambda i: (0, 0))
    return pl.pallas_call(
        body,
        name="norm_in_bwd",
        grid=(t // tm,),
        in_specs=[row, vec, row, row],
        out_specs=[row, vec],
        out_shape=[jax.ShapeDtypeStruct((t, D_MODEL), F32), jax.ShapeDtypeStruct((1, D_MODEL), F32)],
    )(x, g, dh, dres)


def _merge(proj, yd, ys, ym, tm=256, tc=512):
    t = proj.shape[0]
    g0 = O_GATES // tc
    gstep = D_MODEL // tc

    def body(gd, gs, gm, yd_ref, ys_ref, ym_ref, o_ref):
        o_ref[...] = _merge_fn(gd[...], gs[...], gm[...], yd_ref[...], ys_ref[...], ym_ref[...]).astype(BF16)

    def gate(k):
        return pl.BlockSpec((tm, tc), lambda i, j: (i, g0 + k * gstep + j))

    blk = pl.BlockSpec((tm, tc), lambda i, j: (i, j))
    return pl.pallas_call(
        body,
        name="merge",
        grid=(t // tm, D_MODEL // tc),
        in_specs=[gate(0), gate(1), gate(2), blk, blk, blk],
        out_specs=blk,
        out_shape=jax.ShapeDtypeStruct((t, D_MODEL), BF16),
    )(proj, proj, proj, yd, ys, ym)


def _merge_bwd(proj, yd, ys, ym, dmerged, tm=256, tc=512):
    t = proj.shape[0]
    g0 = O_GATES // tc
    gstep = D_MODEL // tc

    def body(gd, gs, gm, yd_ref, ys_ref, ym_ref, dm_ref, dyd, dys, dym, dgd, dgs, dgm):
        _, vjp = jax.vjp(_merge_fn, gd[...], gs[...], gm[...], yd_ref[...], ys_ref[...], ym_ref[...])
        outs = vjp(dm_ref[...])
        for ref, val in zip((dgd, dgs, dgm, dyd, dys, dym), outs):
            ref[...] = val.astype(BF16)

    def gate(k):
        return pl.BlockSpec((tm, tc), lambda i, j: (i, g0 + k * gstep + j))

    blk = pl.BlockSpec((tm, tc), lambda i, j: (i, j))
    o = jax.ShapeDtypeStruct((t, D_MODEL), BF16)
    return pl.pallas_call(
        body,
        name="merge_bwd",
        grid=(t // tm, D_MODEL // tc),
        in_specs=[gate(0), gate(1), gate(2), blk, blk, blk, blk],
        out_specs=[blk] * 6,
        out_shape=[o] * 6,
    )(proj, proj, proj, yd, ys, ym, dmerged)


def _loss_head(x, mo, fg, tgt, tm=256):
    t = x.shape[0]

    def body(x_ref, mo_ref, fg_ref, t_ref, loss_ref, dout_ref, dfg_ref):
        loss, vjp = jax.vjp(_loss_fn, x_ref[...], mo_ref[...], fg_ref[...], t_ref[...])
        _, dmo, dfg, _ = vjp(jnp.ones((1, 1), F32))

        @pl.when(pl.program_id(0) == 0)
        def _():
            loss_ref[...] = jnp.zeros_like(loss_ref)
            dfg_ref[...] = jnp.zeros_like(dfg_ref)

        loss_ref[...] += jnp.broadcast_to(loss, loss_ref.shape)
        dfg_ref[...] += dfg
        dout_ref[...] = dmo

    row = pl.BlockSpec((tm, D_MODEL), lambda i: (i, 0))
    vec = pl.BlockSpec((1, D_MODEL), lambda i: (0, 0))
    return pl.pallas_call(
        body,
        name="loss_head",
        grid=(t // tm,),
        in_specs=[row, row, vec, row],
        out_specs=[pl.BlockSpec((1, LANE), lambda i: (0, 0)), row, vec],
        out_shape=[jax.ShapeDtypeStruct((1, LANE), F32), jax.ShapeDtypeStruct((t, D_MODEL), F32),
                   jax.ShapeDtypeStruct((1, D_MODEL), F32)],
    )(x, mo, fg, tgt)


def _shift_rows(x, s):
    t = x.shape[0]
    if s == 0:
        return x
    rolled = pltpu.roll(x, s % t, 0)
    row = _iota2(x.shape, 0)
    keep = row >= s if s > 0 else row < t + s
    return jnp.where(keep, rolled, 0.0)


def _conv_pre(x, w):
    return sum(_shift_rows(x, CONV_K - 1 - j) * w[j:j + 1, :] for j in range(CONV_K))


def _dn_conv(proj, conv_w):
    t = proj.shape[0]
    nb = 3 * D_MODEL // LANE

    def body(x_ref, w_ref, c_ref):
        c_ref[...] = _silu(_conv_pre(x_ref[...], w_ref[...]))

    return pl.pallas_call(
        body,
        name="dn_conv",
        grid=(nb,),
        in_specs=[pl.BlockSpec((t, LANE), lambda j: (0, j)), pl.BlockSpec((CONV_K, LANE), lambda j: (0, j))],
        out_specs=pl.BlockSpec((t, LANE), lambda j: (0, j)),
        out_shape=jax.ShapeDtypeStruct((t, 3 * D_MODEL), F32),
    )(proj, conv_w)


def _dn_conv_bwd(proj, conv_w, dc, part):
    t = proj.shape[0]
    nb = D_MODEL // LANE
    b0 = part * nb

    def body(x_ref, w_ref, dc_ref, dx_ref, dw_ref):
        x = x_ref[...]
        w = w_ref[...]
        pre = _conv_pre(x, w)
        sg = _sigmoid(pre)
        dpre = dc_ref[...] * (sg * (1.0 + pre * (1.0 - sg)))
        dx = sum(_shift_rows(dpre, -(CONV_K - 1 - j)) * w[j:j + 1, :] for j in range(CONV_K))
        dx_ref[...] = dx.astype(BF16)
        dw_ref[...] = jnp.concatenate(
            [jnp.sum(dpre * _shift_rows(x, CONV_K - 1 - j), axis=0, keepdims=True) for j in range(CONV_K)], axis=0)

    return pl.pallas_call(
        body,
        name=f"dn_conv_bwd{part}",
        grid=(nb,),
        in_specs=[pl.BlockSpec((t, LANE), lambda j: (0, b0 + j)), pl.BlockSpec((CONV_K, LANE), lambda j: (0, b0 + j)),
                  pl.BlockSpec((t, LANE), lambda j: (0, j))],
        out_specs=[pl.BlockSpec((t, LANE), lambda j: (0, j)), pl.BlockSpec((CONV_K, LANE), lambda j: (0, j))],
        out_shape=[jax.ShapeDtypeStruct((t, D_MODEL), BF16), jax.ShapeDtypeStruct((CONV_K, D_MODEL), F32)],
    )(proj, conv_w, dc)


def _ba_columns(ba, hd):
    lane = _iota2(ba.shape, 1)
    bcol = jnp.sum(jnp.where(lane == hd, ba, 0.0), axis=1, keepdims=True)
    acol = jnp.sum(jnp.where(lane == N_HEADS + hd, ba, 0.0), axis=1, keepdims=True)
    return bcol, acol


def _head_scalar(row, hd):
    lane = _iota2(row.shape, 1)
    return jnp.sum(jnp.where(lane == hd, row, 0.0), axis=1, keepdims=True)


DN_HP = 4


def _dn_inputs(cq, ck, cv, ba_ref, z_ref, alog_ref, dtb_ref, heads, lanes):
    ba = ba_ref[...]
    cols = [_ba_columns(ba, hd) for hd in heads]
    return ([cq[:, ln] for ln in lanes], [ck[:, ln] for ln in lanes], [cv[:, ln] for ln in lanes],
            [c[0] for c in cols], [c[1] for c in cols], [z_ref[:, ln] for ln in lanes],
            [_head_scalar(alog_ref[...], hd) for hd in heads], [_head_scalar(dtb_ref[...], hd) for hd in heads])


def _dn_specs(nblk, reverse):
    w = DN_HP * LANE
    nq = D_MODEL // w

    def row(i):
        return nblk - 1 - i if reverse else i

    def colblk(b0):
        return pl.BlockSpec((SUPER, w), lambda i, h: (row(i), b0 + h))

    ba = pl.BlockSpec((SUPER, LANE), lambda i, h: (row(i), O_BA // LANE))
    vec = pl.BlockSpec((1, LANE), lambda i, h: (0, 0))
    st = pl.BlockSpec((1, DN_HP, D_HEAD, D_HEAD), lambda i, h: (row(i), h, 0, 0))
    return colblk, nq, ba, vec, st


def _dn_fwd(c, proj, alog_row, dtb_row, gn):
    t = c.shape[0]
    nblk = t // SUPER
    colblk, nq, ba, vec, st = _dn_specs(nblk, False)

    def body(cq, ck, cv, ba_ref, z_ref, alog_ref, dtb_ref, gn_ref, o_ref, s_ref, state):
        @pl.when(jnp.logical_and(pl.program_id(0) == 0, pl.program_id(1) == 0))
        def _():
            state[...] = jnp.zeros_like(state)

        heads = [pl.program_id(1) * DN_HP + j for j in range(DN_HP)]
        lanes = [slice(j * LANE, (j + 1) * LANE) for j in range(DN_HP)]
        s0 = [state[hd] for hd in heads]
        outs, s2 = _dn_block(*_dn_inputs(cq, ck, cv, ba_ref, z_ref, alog_ref, dtb_ref, heads, lanes), gn_ref[...], s0)
        for j, (hd, ln) in enumerate(zip(heads, lanes)):
            s_ref[0, j] = s0[j]
            o_ref[:, ln] = outs[j].astype(BF16)
            state[hd] = s2[j]

    return pl.pallas_call(
        body,
        name="dn_fwd",
        grid=(nblk, N_HEADS // DN_HP),
        in_specs=[colblk(0), colblk(nq), colblk(2 * nq), ba, colblk(O_Z_DN // (DN_HP * LANE)), vec, vec, vec],
        out_specs=[colblk(0), st],
        out_shape=[jax.ShapeDtypeStruct((t, D_MODEL), BF16),
                   jax.ShapeDtypeStruct((nblk, N_HEADS, D_HEAD, D_HEAD), F32)],
        scratch_shapes=[pltpu.VMEM((N_HEADS, D_HEAD, D_HEAD), F32)],
    )(c, c, c, proj, proj, alog_row, dtb_row, gn)


def _dn_bwd(c, proj, alog_row, dtb_row, gn, states, do):
    t = c.shape[0]
    nblk = t // SUPER
    colblk, nq, ba, vec, st = _dn_specs(nblk, True)

    def body(cq, ck, cv, ba_ref, z_ref, alog_ref, dtb_ref, gn_ref, s_ref, do_ref,
             dq_ref, dk_ref, dv_ref, dz_ref, dba_ref, dsc_ref, dgn_ref, dstate):
        i = pl.program_id(0)
        hq = pl.program_id(1)

        @pl.when(jnp.logical_and(i == 0, hq == 0))
        def _():
            dstate[...] = jnp.zeros_like(dstate)
            dsc_ref[...] = jnp.zeros_like(dsc_ref)
            dgn_ref[...] = jnp.zeros_like(dgn_ref)

        @pl.when(hq == 0)
        def _():
            dba_ref[...] = jnp.zeros_like(dba_ref)

        lane = _iota2((SUPER, LANE), 1)
        lane1 = _iota2((1, LANE), 1)
        heads = [hq * DN_HP + j for j in range(DN_HP)]
        lanes = [slice(j * LANE, (j + 1) * LANE) for j in range(DN_HP)]
        ds_in = [dstate[hd] for hd in heads]
        s_in = [s_ref[0, j] for j in range(DN_HP)]
        _, vjp = jax.vjp(_dn_block, *_dn_inputs(cq, ck, cv, ba_ref, z_ref, alog_ref, dtb_ref, heads, lanes),
                         gn_ref[...], s_in)
        dq, dk, dv, dbc, dac, dz, dal, ddt, dgn, ds0 = vjp(([do_ref[:, ln].astype(F32) for ln in lanes], ds_in))
        dba = jnp.zeros((SUPER, LANE), F32)
        dal_row = jnp.zeros((1, LANE), F32)
        ddt_row = jnp.zeros((1, LANE), F32)
        for j, (hd, ln) in enumerate(zip(heads, lanes)):
            dq_ref[:, ln] = dq[j]
            dk_ref[:, ln] = dk[j]
            dv_ref[:, ln] = dv[j]
            dz_ref[:, ln] = dz[j].astype(BF16)
            dstate[hd] = ds0[j]
            dba = dba + jnp.where(lane == hd, dbc[j], 0.0) + jnp.where(lane == N_HEADS + hd, dac[j], 0.0)
            dal_row = dal_row + jnp.where(lane1 == hd, dal[j], 0.0)
            ddt_row = ddt_row + jnp.where(lane1 == hd, ddt[j], 0.0)
        dba_ref[...] += dba
        dsc_ref[0:1, :] += dal_row
        dsc_ref[1:2, :] += ddt_row
        dgn_ref[...] += dgn

    outs = pl.pallas_call(
        body,
        name="dn_bwd",
        grid=(nblk, N_HEADS // DN_HP),
        in_specs=[colblk(0), colblk(nq), colblk(2 * nq), ba, colblk(O_Z_DN // (DN_HP * LANE)), vec, vec, vec, st,
                  colblk(0)],
        out_specs=[colblk(0), colblk(0), colblk(0), colblk(0),
                   pl.BlockSpec((SUPER, LANE), lambda i, h: (nblk - 1 - i, 0)),
                   pl.BlockSpec((2, LANE), lambda i, h: (0, 0)), vec],
        out_shape=[jax.ShapeDtypeStruct((t, D_MODEL), F32)] * 3
        + [jax.ShapeDtypeStruct((t, D_MODEL), BF16), jax.ShapeDtypeStruct((t, LANE), F32),
           jax.ShapeDtypeStruct((2, LANE), F32), jax.ShapeDtypeStruct((1, LANE), F32)],
        scratch_shapes=[pltpu.VMEM((N_HEADS, D_HEAD, D_HEAD), F32)],
    )(c, c, c, proj, proj, alog_row, dtb_row, gn, states, do)
    return outs


SB_TQ = 256
SB_TK = 256
SB_HP = 2


def _sb_logits(z, mask):
    sp = jnp.log(1.0 + jnp.exp(-jnp.abs(z)))
    lf_raw = -(jnp.maximum(z, 0.0) + sp)
    lb = lf_raw + z
    lf = lf_raw if mask is None else jnp.where(mask, lf_raw, 0.0)
    return lb, lf_raw, lf


def _suffix_sums(x, sel):
    hi, lo = _split2(x)
    d = functools.partial(lax.dot_general, dimension_numbers=NN, preferred_element_type=F32)
    return d(hi, sel) + d(lo, sel)


def _sb_diag_mask(tq, r):
    return r * SB_TK + _iota2((tq, SB_TK), 1) < _iota2((tq, SB_TK), 0)


def _sb_specs(t, tq):
    w = SB_HP * LANE
    q0, k0, v0, z0 = (O_QKV_SB // w, (O_QKV_SB + D_MODEL) // w, (O_QKV_SB + 2 * D_MODEL) // w, O_Z_SB // w)

    def blk(b0):
        return pl.BlockSpec((tq, w), lambda h, i: (i, b0 + h))

    def full(b0):
        return pl.BlockSpec((t, w), lambda h, i: (0, b0 + h))

    return blk(q0), full(k0), full(v0), blk(z0), blk(0), full(0)


def _sb_fwd(proj):
    t = proj.shape[0]
    tq = min(SB_TQ, t)
    ndiag = tq // SB_TK
    scale = 1.0 / math.sqrt(D_HEAD)

    def body(q_ref, k_ref, v_ref, z_ref, o_ref, oraw_ref):
        qi = pl.program_id(1)
        lanes = [slice(hd * LANE, (hd + 1) * LANE) for hd in range(SB_HP)]
        qs = [(q_ref[:, ln] * scale).astype(BF16) for ln in lanes]
        after = (_iota2((SB_TK, SB_TK), 0) > _iota2((SB_TK, SB_TK), 1)).astype(BF16)
        oraw_ref[...] = jnp.zeros_like(oraw_ref)

        def block(kb, mask, c_lf):
            rows = pl.ds(pl.multiple_of(kb * SB_TK, SB_TK), SB_TK)
            z = _each(lambda q, ln: _dot(q, k_ref[rows, ln], NT), qs, lanes)
            lg = _each(lambda x: _sb_logits(x, mask), z)
            surv = _each(lambda x: _suffix_sums(x[2], after), lg)
            att = _each(lambda x, s, c: jnp.exp(x[0] + s + c), lg, surv, c_lf)
            if mask is not None:
                att = _each(lambda a: jnp.where(mask, a, 0.0), att)
            pv = _each(lambda a, ln: _dot(a, v_ref[rows, ln], NN), att, lanes)
            for p, ln in zip(pv, lanes):
                oraw_ref[:, ln] += p
            return tuple(_each(lambda c, x: c + jnp.sum(x[2], axis=1, keepdims=True), c_lf, lg))

        carry = tuple(jnp.zeros((tq, 1), F32) for _ in range(SB_HP))
        for r in reversed(range(ndiag)):
            carry = block(qi * ndiag + r, _sb_diag_mask(tq, r), carry)
        lax.fori_loop(0, qi * ndiag, lambda i, c: block(qi * ndiag - 1 - i, None, c), carry)
        o_ref[...] = (oraw_ref[...] * _silu(z_ref[...])).astype(BF16)

    q_spec, k_spec, v_spec, z_spec, out, _ = _sb_specs(t, tq)
    return pl.pallas_call(
        body,
        name="sb_fwd",
        grid=(N_HEADS // SB_HP, t // tq),
        in_specs=[q_spec, k_spec, v_spec, z_spec],
        out_specs=[out, out],
        out_shape=[jax.ShapeDtypeStruct((t, D_MODEL), BF16), jax.ShapeDtypeStruct((t, D_MODEL), F32)],
    )(proj, proj, proj, proj)


def _sb_bwd(proj, oraw, do):
    t = proj.shape[0]
    tq = min(SB_TQ, t)
    ndiag = tq // SB_TK
    scale = 1.0 / math.sqrt(D_HEAD)

    def body(q_ref, k_ref, v_ref, z_ref, oraw_ref, do_ref, dq_ref, dk_ref, dv_ref, dz_ref, dk_acc, dv_acc,
             p_scr, sig_scr, oms_scr):
        qi = pl.program_id(1)
        nq = pl.num_programs(1)

        @pl.when(qi == 0)
        def _():
            dk_acc[...] = jnp.zeros_like(dk_acc)
            dv_acc[...] = jnp.zeros_like(dv_acc)

        heads = range(SB_HP)
        lanes = [slice(hd * LANE, (hd + 1) * LANE) for hd in heads]
        zg = z_ref[...]
        sg = _sigmoid(zg)
        dog = do_ref[...].astype(F32)
        dz_ref[...] = (dog * oraw_ref[...] * (sg * (1.0 + zg * (1.0 - sg)))).astype(BF16)
        d_o = (dog * (zg * sg)).astype(BF16)
        d_o16 = [d_o[:, ln] for ln in lanes]
        qs = [(q_ref[:, ln] * scale).astype(BF16) for ln in lanes]
        ri = _iota2((SB_TK, SB_TK), 0)
        ci = _iota2((SB_TK, SB_TK), 1)
        after = (ri > ci).astype(BF16)
        earlier = (ri < ci).astype(BF16)

        def rows_of(kb):
            return pl.ds(pl.multiple_of(kb * SB_TK, SB_TK), SB_TK)

        def down(kb, mask, c_lf):
            rows = rows_of(kb)
            z = _each(lambda q, ln: _dot(q, k_ref[rows, ln], NT), qs, lanes)
            da = _each(lambda d, ln: _dot(d, v_ref[rows, ln], NT), d_o16, lanes)
            lg = _each(lambda x: _sb_logits(x, mask), z)
            surv = _each(lambda x: _suffix_sums(x[2], after), lg)
            att = _each(lambda x, s, c: jnp.exp(x[0] + s + c), lg, surv, c_lf)
            sig = _each(lambda x: jnp.exp(x[0]), lg)
            if mask is not None:
                att = _each(lambda a: jnp.where(mask, a, 0.0), att)
                sig = _each(lambda a: jnp.where(mask, a, 0.0), sig)
            dv = _each(lambda a, d: _dot(a, d, TN), att, d_o16)
            for hd in heads:
                p_scr[hd, kb] = att[hd] * da[hd]
                sig_scr[hd, kb] = sig[hd]
                oms_scr[hd, kb] = jnp.exp(lg[hd][1])
                dv_acc[rows, lanes[hd]] += dv[hd]
            return tuple(_each(lambda c, x: c + jnp.sum(x[2], axis=1, keepdims=True), c_lf, lg))

        c_lf = tuple(jnp.zeros((tq, 1), F32) for _ in heads)
        for r in reversed(range(ndiag)):
            c_lf = down(qi * ndiag + r, _sb_diag_mask(tq, r), c_lf)
        lax.fori_loop(0, qi * ndiag, lambda i, c: down(qi * ndiag - 1 - i, None, c), c_lf)

        def up(kb, carry):
            dq, c_p = carry
            rows = rows_of(kb)
            p = [p_scr[hd, kb] for hd in heads]
            before = _each(lambda x, c: _suffix_sums(x, earlier) + c, p, c_p)
            dzz = _each(lambda x, b, hd: x * oms_scr[hd, kb] - sig_scr[hd, kb] * b, p, before, list(heads))
            dk = _each(lambda x, q: _dot(x, q, TN), dzz, qs)
            dq = _each(lambda a, x, ln: a + _dot(x, k_ref[rows, ln], NN), dq, dzz, lanes)
            for hd in heads:
                dk_acc[rows, lanes[hd]] += dk[hd]
            return tuple(dq), tuple(_each(lambda c, x: c + jnp.sum(x, axis=1, keepdims=True), c_p, p))

        init = (tuple(jnp.zeros((tq, D_HEAD), F32) for _ in heads), tuple(jnp.zeros((tq, 1), F32) for _ in heads))
        dq, _ = lax.fori_loop(0, (qi + 1) * ndiag, up, init)
        for hd in heads:
            dq_ref[:, lanes[hd]] = (dq[hd] * scale).astype(BF16)

        @pl.when(qi == nq - 1)
        def _():
            dk_ref[...] = dk_acc[...].astype(BF16)
            dv_ref[...] = dv_acc[...].astype(BF16)

    q_spec, k_spec, v_spec, z_spec, blk, full = _sb_specs(t, tq)
    o = jax.ShapeDtypeStruct((t, D_MODEL), BF16)
    w = SB_HP * LANE
    return pl.pallas_call(
        body,
        name="sb_bwd",
        grid=(N_HEADS // SB_HP, t // tq),
        in_specs=[q_spec, k_spec, v_spec, z_spec, blk, blk],
        out_specs=[blk, full, full, blk],
        out_shape=[o, o, o, o],
        scratch_shapes=[pltpu.VMEM((t, w), F32), pltpu.VMEM((t, w), F32)]
        + [pltpu.VMEM((SB_HP, t // SB_TK, tq, SB_TK), F32)] * 3,
    )(proj, proj, proj, proj, oraw, do)


def _mem_kv_fn(mem, mg, w):
    return mm_nn(_rmsnorm(mem, mg), w)


def _mem_kv(mem, mg, w):
    def body(m_ref, g_ref, w_ref, o_ref):
        o_ref[...] = _mem_kv_fn(m_ref[...], g_ref[...], w_ref[...])

    return pl.pallas_call(body, name="mem_kv", out_shape=jax.ShapeDtypeStruct((MEM_LEN, 2 * MEM_W), F32))(mem, mg, w)


def _mem_kv_bwd(mem, mg, w, dmkv):
    def body(m_ref, g_ref, w_ref, d_ref, dg_ref, dw_ref):
        _, vjp = jax.vjp(_mem_kv_fn, m_ref[...], g_ref[...], w_ref[...].astype(F32))
        _, dg, dw = vjp(d_ref[...])
        dg_ref[...] = dg
        dw_ref[...] = dw

    return pl.pallas_call(
        body, name="mem_kv_bwd",
        out_shape=[jax.ShapeDtypeStruct((1, D_MODEL), F32), jax.ShapeDtypeStruct((D_MODEL, 2 * MEM_W), F32)],
    )(mem, mg, w, dmkv)


def _mem_attn(proj, mkv, tm=256):
    t = proj.shape[0]
    tm = min(tm, t)

    def body(q_ref, z_ref, kv_ref, o_ref):
        o_ref[...] = _mem_fn(q_ref[...], z_ref[...], kv_ref[...]).astype(BF16)

    return pl.pallas_call(
        body,
        name="mem_attn",
        grid=(t // tm,),
        in_specs=[pl.BlockSpec((tm, MEM_W), lambda i: (i, O_MQ // MEM_W)),
                  pl.BlockSpec((tm, MEM_W), lambda i: (i, O_MZ // MEM_W)),
                  pl.BlockSpec((MEM_LEN, 2 * MEM_W), lambda i: (0, 0))],
        out_specs=pl.BlockSpec((tm, MEM_W), lambda i: (i, 0)),
        out_shape=jax.ShapeDtypeStruct((t, MEM_W), BF16),
    )(proj, proj, mkv)


def _mem_attn_bwd(proj, mkv, do, tm=256):
    t = proj.shape[0]
    tm = min(tm, t)

    def body(q_ref, z_ref, kv_ref, do_ref, dq_ref, dz_ref, dkv_ref):
        _, vjp = jax.vjp(_mem_fn, q_ref[...], z_ref[...], kv_ref[...])
        dq, dz, dkv = vjp(do_ref[...].astype(F32))
        dq_ref[...] = dq.astype(BF16)
        dz_ref[...] = dz.astype(BF16)

        @pl.when(pl.program_id(0) == 0)
        def _():
            dkv_ref[...] = jnp.zeros_like(dkv_ref)

        dkv_ref[...] += dkv

    blk = pl.BlockSpec((tm, MEM_W), lambda i: (i, 0))
    kv = pl.BlockSpec((MEM_LEN, 2 * MEM_W), lambda i: (0, 0))
    return pl.pallas_call(
        body,
        name="mem_attn_bwd",
        grid=(t // tm,),
        in_specs=[pl.BlockSpec((tm, MEM_W), lambda i: (i, O_MQ // MEM_W)),
                  pl.BlockSpec((tm, MEM_W), lambda i: (i, O_MZ // MEM_W)), kv, blk],
        out_specs=[blk, blk, kv],
        out_shape=[jax.ShapeDtypeStruct((t, MEM_W), BF16), jax.ShapeDtypeStruct((t, MEM_W), BF16),
                   jax.ShapeDtypeStruct((MEM_LEN, 2 * MEM_W), F32)],
    )(proj, proj, mkv, do)


def _local_step(x, mem, tgt, norm_g, mem_norm_g, w_alt, conv_w, alog_row, dtb_row, dn_norm_g, w_mem_kv, w_br_dn, w_br_sb,
                w_br_mem, w_out, final_g):
    h = _norm_in(x, norm_g)
    proj = _matmul(h, w_alt, "nt", F32, 2048, 384, 1024, "proj")

    c = _dn_conv(proj, conv_w)
    o_dn, states = _dn_fwd(c, proj, alog_row, dtb_row, dn_norm_g)
    o_sb, o_sb_raw = _sb_fwd(proj)
    mkv = _mem_kv(mem, mem_norm_g, w_mem_kv)
    o_m = _mem_attn(proj, mkv)

    y_dn = _matmul(o_dn, w_br_dn, "nn", F32, 512, 1024, 1024, "y_dn")
    y_sb = _matmul(o_sb, w_br_sb, "nn", F32, 512, 1024, 1024, "y_sb")
    y_m = _matmul(o_m, w_br_mem, "nn", F32, 512, 1024, 1024, "y_m")
    merged = _merge(proj, y_dn, y_sb, y_m)
    mo = _matmul(merged, w_out, "nn", F32, 512, 1024, 1024, "mo")
    loss, dout, d_final_g = _loss_head(x, mo, final_g, tgt)

    dmerged = _matmul(dout, w_out, "nt", F32, 512, 1024, 1024, "dmerged")
    dw_out = _matmul(merged, dout, "tn", F32, 256, 1024, 2048, "dw_out")
    dy_dn, dy_sb, dy_m, dg_dn, dg_sb, dg_m = _merge_bwd(proj, y_dn, y_sb, y_m, dmerged)
    do_dn = _matmul(dy_dn, w_br_dn, "nt", BF16, 512, 1024, 1024, "do_dn")
    do_sb = _matmul(dy_sb, w_br_sb, "nt", BF16, 512, 1024, 1024, "do_sb")
    do_m = _matmul(dy_m, w_br_mem, "nt", BF16, 512, 256, 1024, "do_m")
    dw_br_dn = _matmul(o_dn, dy_dn, "tn", F32, 256, 1024, 2048, "dw_br_dn")
    dw_br_sb = _matmul(o_sb, dy_sb, "tn", F32, 256, 1024, 2048, "dw_br_sb")
    dw_br_mem = _matmul(o_m, dy_m, "tn", F32, 256, 1024, 2048, "dw_br_mem")

    dmq, dmz, dmkv = _mem_attn_bwd(proj, mkv, do_m)
    d_mem_norm_g, dw_mem_kv = _mem_kv_bwd(mem, mem_norm_g, w_mem_kv, dmkv)
    dq_sb, dk_sb, dv_sb, dz_sb = _sb_bwd(proj, o_sb_raw, do_sb)
    dcq, dck, dcv, dz_dn, dba, dscal, d_dn_norm_g = _dn_bwd(c, proj, alog_row, dtb_row, dn_norm_g, states, do_dn)
    dq_dn, dcw_q = _dn_conv_bwd(proj, conv_w, dcq, 0)
    dk_dn, dcw_k = _dn_conv_bwd(proj, conv_w, dck, 1)
    dv_dn, dcw_v = _dn_conv_bwd(proj, conv_w, dcv, 2)
    d_conv_w = jnp.concatenate([dcw_q, dcw_k, dcw_v], axis=1)

    dproj = jnp.concatenate([dq_dn, dk_dn, dv_dn, dz_dn, dq_sb, dk_sb, dv_sb, dz_sb, dmq, dmz, dg_dn, dg_sb, dg_m,
                             dba.astype(BF16)], axis=1)
    dh = _matmul(dproj, w_alt, "nn", F32, 512, 1024, 3968, "dh")
    dw_alt = _matmul(dproj, h, "tn", F32, 384, 1024, 2048, "dw_alt")
    grad_x, d_norm_g = _norm_in_bwd(x, norm_g, dh, dout)
    return dict(loss=loss, grad_x=grad_x, norm_g=d_norm_g, mem_norm_g=d_mem_norm_g, w_alt=dw_alt, conv_w=d_conv_w,
                scal=dscal, dn_norm_g=d_dn_norm_g, w_mem_kv=dw_mem_kv, w_br_dn=dw_br_dn, w_br_sb=dw_br_sb,
                w_br_mem=dw_br_mem, w_out=dw_out, final_g=d_final_g)


MESH = pl.DeviceIdType.MESH
ANY = pl.BlockSpec(memory_space=pl.ANY)


def _position():
    return lax.axis_index("x"), lax.axis_index("y"), lax.axis_index("c")


def _all_gather(xs, name):
    n = len(xs)

    def body(*refs):
        x_refs, o_refs = refs[:n], refs[n:2 * n]
        send_sems, recv_sems, local_sems = refs[2 * n:]
        x, y, c = _position()
        me, sibling = (x, y, c), (x, y, 1 - c)
        chips = [(1 - x, y), (x, 1 - y), (1 - x, 1 - y)]

        def slot(p):
            return 4 * p[0] + 2 * p[1] + p[2]

        def copy(a, k, block, to, src=None):
            dst = o_refs[a].at[slot(block)]
            return pltpu.make_async_remote_copy(
                src_ref=dst if src is None else src, dst_ref=dst, send_sem=send_sems.at[7 * a + k],
                recv_sem=recv_sems.at[7 * a + k], device_id=to, device_id_type=MESH)

        mine = [pltpu.make_async_copy(x_refs[a], o_refs[a].at[slot(me)], local_sems.at[a]) for a in range(n)]
        for cp in mine:
            cp.start()
        first = []
        for a in range(n):
            first.append(copy(a, 0, me, sibling, src=x_refs[a]))
            first += [copy(a, 1 + j, me, (*chip, c), src=x_refs[a]) for j, chip in enumerate(chips)]
        for cp in first:
            cp.start()
        passed = []
        for j, chip in enumerate(chips):
            for a in range(n):
                copy(a, 1 + j, (*chip, c), me).wait_recv()
                cp = copy(a, 4 + j, (*chip, c), sibling)
                cp.start()
                passed.append(cp)
        for a in range(n):
            copy(a, 0, sibling, me).wait_recv()
            for j, chip in enumerate(chips):
                copy(a, 4 + j, (*chip, 1 - c), me).wait_recv()
        for cp in first + passed:
            cp.wait_send()
        for cp in mine:
            cp.wait()

    return pl.pallas_call(
        body,
        name=name,
        in_specs=[ANY] * n,
        out_specs=[ANY] * n,
        out_shape=[jax.ShapeDtypeStruct((N_DEV, *v.shape), v.dtype) for v in xs],
        scratch_shapes=[pltpu.SemaphoreType.DMA((7 * n,)), pltpu.SemaphoreType.DMA((7 * n,)),
                        pltpu.SemaphoreType.DMA((n,))],
    )(*xs)


def _window_view(ref, dest):
    return ref.at[pl.ds(LANE * WIN_START[dest], WIN_W), :]


def _chunk_rows(rows, cols):
    best = max(ch for ch in range(16, rows + 1, 16) if rows % ch == 0 and ch * cols <= (1 << 17))
    return best


def _halving_stage(xs, axis, name, out_dtype, windowed=()):
    n_arr = len(xs)
    metas = []
    for k, v in enumerate(xs):
        if k in windowed:
            metas.append((N_DEV // 2, WIN_W, v.shape[1]))
        else:
            assert v.shape[1] == 2
            metas.append((v.shape[0], v.shape[2], v.shape[3]))
    chunk = [_chunk_rows(r, c) for (_, r, c) in metas]
    offs = [sum(m[0] for m in metas[:k]) for k in range(n_arr)]
    n_sem = sum(m[0] for m in metas)

    def body(*refs):
        x_refs = refs[:n_arr]
        o_refs = refs[n_arr:2 * n_arr]
        land_refs = refs[2 * n_arr:3 * n_arr]
        rest = refs[3 * n_arr:]
        bufs = rest[:3 * n_arr]
        send_sems, recv_sems, in_sems, out_sems = rest[3 * n_arr:]
        pos = dict(zip("xyc", _position()))
        bit = pos[axis]
        peer = tuple(1 - pos[a] if a == axis else pos[a] for a in "xyc")

        def view(k, i, b):
            if k in windowed:
                return _window_view(x_refs[k], 2 * i + b)
            return x_refs[k].at[i, b]

        def add_blocks(k, a_view, b_view, o_view):
            _, rows, _ = metas[k]
            ch = chunk[k]
            nch = rows // ch
            va, vb, vo = bufs[3 * k:3 * k + 3]

            def rows_of(j):
                return pl.ds(pl.multiple_of(j * ch, 8), ch)

            def loads(j, s):
                return (pltpu.make_async_copy(a_view.at[rows_of(j), :], va.at[s], in_sems.at[0, s]),
                        pltpu.make_async_copy(b_view.at[rows_of(j), :], vb.at[s], in_sems.at[1, s]))

            def store(j, s):
                return pltpu.make_async_copy(vo.at[s], o_view.at[rows_of(j), :], out_sems.at[s])

            for cp in loads(0, 0):
                cp.start()

            def step(j, _):
                s = lax.rem(j, 2)

                @pl.when(j + 1 < nch)
                def _():
                    for cp in loads(j + 1, 1 - s):
                        cp.start()

                for cp in loads(j, s):
                    cp.wait()

                @pl.when(j >= 2)
                def _():
                    store(j - 2, s).wait()

                vo[s] = (va[s] + vb[s]).astype(vo.dtype)
                store(j, s).start()
                return 0

            lax.fori_loop(0, nch, step, 0)
            for j in range(max(0, nch - 2), nch):
                store(j, j % 2).wait()

        for b in (0, 1):
            @pl.when(bit == b)
            def _(b=b):
                sends = []
                for k in range(n_arr):
                    for i in range(metas[k][0]):
                        cp = pltpu.make_async_remote_copy(
                            src_ref=view(k, i, 1 - b), dst_ref=land_refs[k].at[i], send_sem=send_sems.at[offs[k] + i],
                            recv_sem=recv_sems.at[offs[k] + i], device_id=peer, device_id_type=MESH)
                        cp.start()
                        sends.append(cp)
                idx = 0
                for k in range(n_arr):
                    for i in range(metas[k][0]):
                        sends[idx].wait_recv()
                        add_blocks(k, view(k, i, b), land_refs[k].at[i], o_refs[k].at[i])
                        idx += 1
                for cp in sends:
                    cp.wait_send()

    out_shape = [jax.ShapeDtypeStruct(m, out_dtype) for m in metas]
    land_shape = [jax.ShapeDtypeStruct(m, F32) for m in metas]
    scratch = []
    for k in range(n_arr):
        scratch += [pltpu.VMEM((2, chunk[k], metas[k][2]), F32)] * 2 + [pltpu.VMEM((2, chunk[k], metas[k][2]), out_dtype)]
    scratch += [pltpu.SemaphoreType.DMA((n_sem,)), pltpu.SemaphoreType.DMA((n_sem,)),
                pltpu.SemaphoreType.DMA((2, 2)), pltpu.SemaphoreType.DMA((2,))]
    outs = pl.pallas_call(
        body,
        name=name,
        in_specs=[ANY] * n_arr,
        out_specs=[ANY] * (2 * n_arr),
        out_shape=out_shape + land_shape,
        scratch_shapes=scratch,
    )(*xs)
    return outs[:n_arr]


def _hbm_add(a_view, b_view, o_view, bufs, in_sems, out_sems, ch):
    rows = a_view.shape[0]
    nch = rows // ch
    va, vb, vo = bufs

    def rows_of(j):
        return pl.ds(pl.multiple_of(j * ch, 16), ch)

    def loads(j, s):
        return (pltpu.make_async_copy(a_view.at[rows_of(j), :], va.at[s], in_sems.at[0, s]),
                pltpu.make_async_copy(b_view.at[rows_of(j), :], vb.at[s], in_sems.at[1, s]))

    def store(j, s):
        return pltpu.make_async_copy(vo.at[s], o_view.at[rows_of(j), :], out_sems.at[s])

    for cp in loads(0, 0):
        cp.start()

    def step(j, _):
        s = lax.rem(j, 2)

        @pl.when(j + 1 < nch)
        def _():
            for cp in loads(j + 1, 1 - s):
                cp.start()

        for cp in loads(j, s):
            cp.wait()

        @pl.when(j >= 2)
        def _():
            store(j - 2, s).wait()

        vo[s] = (va[s].astype(F32) + vb[s].astype(F32)).astype(vo.dtype)
        store(j, s).start()
        return 0

    lax.fori_loop(0, nch, step, 0)
    for j in range(max(0, nch - 2), nch):
        store(j, j % 2).wait()


def _xy_stage(xs, first, name):
    n_arr = len(xs)
    if first:
        shapes = [(v.shape[2] // 2, v.shape[3]) for v in xs]
        ins = list(xs)
    else:
        shapes = [(a.shape[1], a.shape[2]) for a, _ in xs]
        ins = [v for pair in xs for v in pair]
    n_blk = 2 if first else 1
    out_dtype = BF16 if first else F32
    chunk = [_chunk_rows(r, c) for (r, c) in shapes]
    n_sem = 2 * n_blk * n_arr

    def body(*refs):
        n_in = len(ins)
        in_refs = refs[:n_in]
        n_out = 2 * n_arr if first else n_arr
        o_refs = refs[n_in:n_in + n_out]
        land = refs[n_in + n_out:n_in + n_out + 2 * n_arr]
        rest = refs[n_in + n_out + 2 * n_arr:]
        bufs = rest[:3 * n_arr]
        send_sems, recv_sems, in_sems, out_sems = rest[3 * n_arr:]
        x, y, c = _position()
        peers = {"x": (1 - x, y, c), "y": (x, 1 - y, c)}
        jobs = []
        for k in range(n_arr):
            r, _ = shapes[k]
            half_a, half_b = pl.ds(0, r), pl.ds(r, r)
            if first:
                src = in_refs[k]
                for i in range(2):
                    jobs.append((k, src.at[i, 1 - y, half_a, :], src.at[i, y, half_a, :], land[2 * k].at[i],
                                 o_refs[2 * k].at[i], "y"))
                    jobs.append((k, src.at[1 - x, i, half_b, :], src.at[x, i, half_b, :], land[2 * k + 1].at[i],
                                 o_refs[2 * k + 1].at[i], "x"))
            else:
                a1, b1 = in_refs[2 * k], in_refs[2 * k + 1]
                jobs.append((k, a1.at[1 - x], a1.at[x], land[2 * k], o_refs[k].at[half_a, :], "x"))
                jobs.append((k, b1.at[1 - y], b1.at[y], land[2 * k + 1], o_refs[k].at[half_b, :], "y"))
        sends = []
        for n, (k, send, _, landing, _, axis) in enumerate(jobs):
            cp = pltpu.make_async_remote_copy(src_ref=send, dst_ref=landing, send_sem=send_sems.at[n],
                                              recv_sem=recv_sems.at[n], device_id=peers[axis], device_id_type=MESH)
            cp.start()
            sends.append(cp)
        for cp, (k, _, kept, landing, out, _) in zip(sends, jobs):
            cp.wait_recv()
            _hbm_add(kept, landing, out, bufs[3 * k:3 * k + 3], in_sems, out_sems, chunk[k])
        for cp in sends:
            cp.wait_send()

    if first:
        out_shape = [jax.ShapeDtypeStruct((2, r, c), BF16) for (r, c) in shapes for _ in range(2)]
        land_shape = out_shape
    else:
        out_shape = [jax.ShapeDtypeStruct((2 * r, c), F32) for (r, c) in shapes]
        land_shape = [jax.ShapeDtypeStruct((r, c), BF16) for (r, c) in shapes for _ in range(2)]
    scratch = []
    for k in range(n_arr):
        scratch += [pltpu.VMEM((2, chunk[k], shapes[k][1]), BF16)] * 2 + [pltpu.VMEM((2, chunk[k], shapes[k][1]), out_dtype)]
    scratch += [pltpu.SemaphoreType.DMA((n_sem,)), pltpu.SemaphoreType.DMA((n_sem,)),
                pltpu.SemaphoreType.DMA((2, 2)), pltpu.SemaphoreType.DMA((2,))]
    outs = pl.pallas_call(
        body,
        name=name,
        in_specs=[ANY] * len(ins),
        out_specs=[ANY] * (len(out_shape) + len(land_shape)),
        out_shape=out_shape + land_shape,
        scratch_shapes=scratch,
    )(*ins)
    outs = outs[:len(out_shape)]
    return [(outs[2 * k], outs[2 * k + 1]) for k in range(n_arr)] if first else list(outs)


def _reduce_scatter(dw_al, blocks):
    xs = [dw_al] + [b.reshape(N_DEV // 2, 2, *b.shape[1:]) for b in blocks]
    ys = _halving_stage(xs, "c", "rs_c", BF16, windowed=(0,))
    pairs = _xy_stage([v.reshape(2, 2, *v.shape[1:]) for v in ys], True, "rs_xy1")
    return _xy_stage(pairs, False, "rs_xy2")


def _sum_slots(gs):
    n = len(gs)

    def body(*refs):
        for g_ref, o_ref in zip(refs[:n], refs[n:]):
            acc = g_ref[0]
            for d in range(1, N_DEV):
                acc = acc + g_ref[d]
            o_ref[...] = acc

    return pl.pallas_call(body, name="sum_slots",
                          out_shape=[jax.ShapeDtypeStruct(g.shape[1:], g.dtype) for g in gs])(*gs)


def _assemble_w_al(wins, bas):
    ba_tile = O_BA // LANE
    assert W_AL // LANE == ba_tile + 1
    cols = wins.shape[2]
    n_buf = 3
    ends = [WIN_START[d + 1] if d + 1 < N_DEV else ba_tile for d in range(N_DEV)]

    def body(w_ref, ba_ref, o_ref, buf, ld_sems, st_sems, ba_sem):
        def load(d):
            return pltpu.make_async_copy(w_ref.at[d], buf.at[d % n_buf], ld_sems.at[d % n_buf])

        def store(d):
            n = LANE * (ends[d] - WIN_START[d])
            return pltpu.make_async_copy(buf.at[d % n_buf, pl.ds(0, n), :],
                                         o_ref.at[pl.ds(LANE * WIN_START[d], n), :], st_sems.at[d % n_buf])

        ba_copy = pltpu.make_async_copy(ba_ref.at[BA_DEV], o_ref.at[pl.ds(LANE * ba_tile, LANE), :], ba_sem)
        ba_copy.start()
        load(0).start()
        for d in range(N_DEV):
            if d + 1 < N_DEV:
                if d + 1 >= n_buf:
                    store(d + 1 - n_buf).wait()
                load(d + 1).start()
            load(d).wait()
            if d > 0:
                ov = LANE * (WIN_START[d - 1] + WIN_TILES - WIN_START[d])
                buf[d % n_buf, :ov, :] = buf[d % n_buf, :ov, :] + buf[(d - 1) % n_buf, WIN_W - ov:, :]
            store(d).start()
        for d in range(N_DEV - n_buf, N_DEV):
            store(d).wait()
        ba_copy.wait()

    return pl.pallas_call(
        body,
        name="assemble_w_al",
        in_specs=[ANY, ANY],
        out_specs=ANY,
        out_shape=jax.ShapeDtypeStruct((W_AL, cols), wins.dtype),
        scratch_shapes=[pltpu.VMEM((n_buf, WIN_W, cols), wins.dtype), pltpu.SemaphoreType.DMA((n_buf,)),
                        pltpu.SemaphoreType.DMA((n_buf,)), pltpu.SemaphoreType.DMA],
    )(wins, bas)


def _adamw_math(w, g, m, v):
    m_new = ADAM_B1 * m + (1.0 - ADAM_B1) * g
    v_new = ADAM_B2 * v + (1.0 - ADAM_B2) * (g * g)
    m_hat = m_new / (1.0 - ADAM_B1 ** ADAM_STEP)
    v_hat = v_new / (1.0 - ADAM_B2 ** ADAM_STEP)
    return -ADAM_LR * (m_hat / (jnp.sqrt(v_hat) + ADAM_EPS) + ADAM_WD * w), m_new, v_new


def _adamw(w, g, m, v, name, tb=134):
    r, _, c = w.shape
    assert r % tb == 0

    def body(w_ref, g_ref, m_ref, v_ref, d_ref, nm_ref, nv_ref):
        d_ref[...], nm_ref[...], nv_ref[...] = _adamw_math(w_ref[...], g_ref[...], m_ref[...], v_ref[...])

    blk = pl.BlockSpec((tb, 1, c), lambda i: (i, 0, 0))
    o = jax.ShapeDtypeStruct(w.shape, F32)
    return pl.pallas_call(body, name=name, grid=(r // tb,), in_specs=[blk] * 4, out_specs=[blk] * 3,
                          out_shape=[o, o, o])(w, g, m, v)


def _adamw_many(ws, gs, ms, vs, name):
    n = len(ws)

    def body(*refs):
        for k in range(n):
            w_ref, g_ref, m_ref, v_ref = (refs[j * n + k] for j in range(4))
            d_ref, nm_ref, nv_ref = (refs[(4 + j) * n + k] for j in range(3))
            d_ref[...], nm_ref[...], nv_ref[...] = _adamw_math(w_ref[...], g_ref[...], m_ref[...], v_ref[...])

    shapes = [jax.ShapeDtypeStruct(w.shape, F32) for w in ws]
    outs = pl.pallas_call(body, name=name, out_shape=shapes * 3)(*ws, *gs, *ms, *vs)
    return outs[:n], outs[n:2 * n], outs[2 * n:]


def _select(me, table):
    return sum(jnp.where(me == d, jnp.int32(v), jnp.int32(0)) for d, v in enumerate(table))


WIN_SHIFT = tuple(SHARD_W * d - LANE * WIN_START[d] for d in range(N_DEV))
PAD_L = 256
PAD_R = 256


def _shard_to_window(shard_t, me):
    shift = _select(me, WIN_SHIFT)
    start = _select(me, WIN_START)
    padded = jnp.pad(shard_t, ((PAD_L, PAD_R), (0, 0)))
    cols = shard_t.shape[1]
    lo = lax.dynamic_slice(padded, (PAD_L - shift, 0), (WIN_W, cols))
    hi = lax.dynamic_slice(padded, (PAD_L - shift + N_BA, 0), (WIN_W, cols))
    aligned = LANE * start + lax.broadcasted_iota(jnp.int32, (WIN_W, 1), 0)
    return jnp.where(aligned >= ORIG_BA, hi, lo)


def _window_to_shard(win, ba_grad, me):
    shift = _select(me, WIN_SHIFT)
    cols = win.shape[1]
    padded = jnp.pad(win, ((N_BA, PAD_R), (0, 0)))
    lo = lax.dynamic_slice(padded, (N_BA + shift, 0), (SHARD_W, cols))
    hi = lax.dynamic_slice(padded, (shift, 0), (SHARD_W, cols))
    orig = SHARD_W * me + lax.broadcasted_iota(jnp.int32, (SHARD_W, 1), 0)
    ba_full = lax.dynamic_update_slice(jnp.zeros((SHARD_W, cols), win.dtype), ba_grad, (BA_LOCAL, 0))
    return jnp.where(orig < ORIG_BA, lo, jnp.where(orig >= ORIG_BA + N_BA, hi, ba_full))


def _pad_row(v, width=D_MODEL):
    v = v.reshape(1, -1)
    return jnp.pad(v, ((0, 0), (0, width - v.shape[1])))


def _slab(v, rows=8):
    return jnp.pad(v, ((0, rows - v.shape[0]), (0, D_MODEL - v.shape[1])))


def kernel(x, mem, norm_g, mem_norm_g, w_in, conv_w, a_log, dt_bias, dn_norm_g, w_mem_kv, w_br_dn, w_br_sb, w_br_mem, w_out, final_g, loss_target, m_norm_g, m_mem_norm_g, m_w_in, m_conv_w, m_a_log, m_dt_bias, m_dn_norm_g, m_w_mem_kv, m_w_br_dn, m_w_br_sb, m_w_br_mem, m_w_out, m_final_g, v_norm_g, v_mem_norm_g, v_w_in, v_conv_w, v_a_log, v_dt_bias, v_dn_norm_g, v_w_mem_kv, v_w_br_dn, v_w_br_sb, v_w_br_mem, v_w_out, v_final_g):
    xi, yi, ci = _position()
    me = 4 * xi + 2 * yi + ci

    shard_t = w_in[0].T
    win = _shard_to_window(shard_t, me).astype(BF16)
    ba = jnp.pad(shard_t[BA_LOCAL:BA_LOCAL + N_BA, :], ((0, LANE - N_BA), (0, 0))).astype(BF16)
    g_win, g_ba, g_kv, g_dn, g_sb, g_out, g_mem, g_conv = _all_gather(
        [win, ba, w_mem_kv[0].astype(BF16), w_br_dn[0].astype(BF16), w_br_sb[0].astype(BF16), w_out[0].astype(BF16),
         w_br_mem[0].astype(BF16), conv_w[0]], "gather_weights")
    w_alt = _assemble_w_al(g_win, g_ba)
    w_mem_kv_f = g_kv.reshape(D_MODEL, 2 * MEM_W)
    w_br_dn_f = g_dn.reshape(D_MODEL, D_MODEL)
    w_br_sb_f = g_sb.reshape(D_MODEL, D_MODEL)
    w_out_f = g_out.reshape(D_MODEL, D_MODEL)
    w_br_mem_f = g_mem.transpose(1, 0, 2).reshape(MEM_W, D_MODEL)
    conv_w_f = g_conv.transpose(1, 0, 2).reshape(CONV_K, 3 * D_MODEL)

    r = _local_step(x[0], mem[0], loss_target[0], norm_g, mem_norm_g, w_alt, conv_w_f, _pad_row(a_log, LANE),
                    _pad_row(dt_bias, LANE), dn_norm_g, w_mem_kv_f, w_br_dn_f, w_br_sb_f, w_br_mem_f, w_out_f,
                    final_g.reshape(1, D_MODEL))

    dw_alt = r["w_alt"]
    g_win, g_kv, g_dn, g_sb, g_out, g_mem = _reduce_scatter(dw_alt, [
        r["w_mem_kv"].reshape(N_DEV, D_MODEL // N_DEV, 2 * MEM_W),
        r["w_br_dn"].reshape(N_DEV, D_MODEL // N_DEV, D_MODEL),
        r["w_br_sb"].reshape(N_DEV, D_MODEL // N_DEV, D_MODEL),
        r["w_out"].reshape(N_DEV, D_MODEL // N_DEV, D_MODEL),
        r["w_br_mem"].reshape(MEM_W, N_DEV, D_MODEL // N_DEV).transpose(1, 0, 2)])
    parts = [r["norm_g"], r["mem_norm_g"], r["final_g"], r["dn_norm_g"], r["scal"], r["loss"], r["conv_w"],
             dw_alt[O_BA:O_BA + LANE, :]]
    s_norm_g, s_mem_norm_g, s_final_g, s_dn_norm_g, s_scal, s_loss, s_conv, s_ba = _sum_slots(
        _all_gather(parts, "gather_small"))
    loss = s_loss[0, 0]
    cw = conv_w.shape[2]
    g_conv = lax.dynamic_slice(s_conv, (0, cw * me), (CONV_K, cw))
    g_w_in_t = _window_to_shard(g_win, s_ba[:N_BA, :], me)
    grads = dict(norm_g=s_norm_g, mem_norm_g=s_mem_norm_g, w_in=g_w_in_t.T[None], conv_w=g_conv[None],
                 a_log=s_scal[0:1, :N_HEADS], dt_bias=s_scal[1:2, :N_HEADS], dn_norm_g=s_dn_norm_g, w_mem_kv=g_kv[None],
                 w_br_dn=g_dn[None], w_br_sb=g_sb[None], w_br_mem=g_mem[None], w_out=g_out[None],
                 final_g=s_final_g.reshape(D_MODEL))

    params = dict(norm_g=(norm_g, m_norm_g, v_norm_g), mem_norm_g=(mem_norm_g, m_mem_norm_g, v_mem_norm_g),
                  w_in=(w_in, m_w_in, v_w_in), conv_w=(conv_w, m_conv_w, v_conv_w), a_log=(a_log, m_a_log, v_a_log),
                  dt_bias=(dt_bias, m_dt_bias, v_dt_bias), dn_norm_g=(dn_norm_g, m_dn_norm_g, v_dn_norm_g),
                  w_mem_kv=(w_mem_kv, m_w_mem_kv, v_w_mem_kv), w_br_dn=(w_br_dn, m_w_br_dn, v_w_br_dn),
                  w_br_sb=(w_br_sb, m_w_br_sb, v_w_br_sb), w_br_mem=(w_br_mem, m_w_br_mem, v_w_br_mem),
                  w_out=(w_out, m_w_out, v_w_out), final_g=(final_g, m_final_g, v_final_g))
    order = list(params)
    deltas, new_m, new_v = {}, {}, {}
    deltas["w_in"], new_m["w_in"], new_v["w_in"] = (jnp.transpose(o, (1, 2, 0)) for o in _adamw(
        jnp.transpose(w_in, (2, 0, 1)), g_w_in_t[:, None, :], jnp.transpose(m_w_in, (2, 0, 1)),
        jnp.transpose(v_w_in, (2, 0, 1)), "adamw_w_in"))
    rest = [nm for nm in order if nm != "w_in"]

    def two_d(a):
        return a.reshape(1, -1) if a.ndim == 1 else a

    d_l, m_l, v_l = _adamw_many([two_d(params[nm][0]) for nm in rest], [two_d(grads[nm]) for nm in rest],
                                [two_d(params[nm][1]) for nm in rest], [two_d(params[nm][2]) for nm in rest], "adamw_rest")
    for k, nm in enumerate(rest):
        shp = params[nm][0].shape
        deltas[nm], new_m[nm], new_v[nm] = d_l[k].reshape(shp), m_l[k].reshape(shp), v_l[k].reshape(shp)
    return (loss, r["grad_x"][None], *[grads[nm] for nm in order], *[deltas[nm] for nm in order],
            *[new_m[nm] for nm in order], *[new_v[nm] for nm in order])
```

```python
import functools
import math

import jax
import jax.numpy as jnp
from jax import lax
from jax.experimental import pallas as pl
from jax.experimental.pallas import tpu as pltpu

F32 = jnp.float32
BF16 = jnp.bfloat16

D_MODEL = 1024
N_DEV = 8
N_HEADS = 8
D_HEAD = 128
DN_CHUNK = 64
CONV_K = 4
MEM_LEN = 256
MEM_HEADS = 4
MEM_DH = 64
MEM_W = MEM_HEADS * MEM_DH
NORM_EPS = 1e-6
IN_WIDTH = 11792
SHARD_W = IN_WIDTH // N_DEV

LANE = 128
SUPER = 2 * DN_CHUNK

O_QKV_DN = 0
O_Z_DN = 3072
O_QKV_SB = 4096
O_Z_SB = 7168
O_MQ = 8192
O_MZ = 8448
O_GATES = 8704
O_BA = 11776
W_AL = 11904
ORIG_BA = 4096
N_BA = 16

WIN_TILES = 13
WIN_W = WIN_TILES * LANE


def _aligned_col(o):
    return o if o < ORIG_BA else o - N_BA


WIN_START = tuple(min(_aligned_col(SHARD_W * d) // LANE, (W_AL // LANE) - WIN_TILES) for d in range(N_DEV))
WIN_OFF = tuple(_aligned_col(SHARD_W * d) - LANE * WIN_START[d] if SHARD_W * d >= ORIG_BA + N_BA or SHARD_W * d < ORIG_BA
                else None for d in range(N_DEV))
BA_DEV = ORIG_BA // SHARD_W
BA_LOCAL = ORIG_BA - BA_DEV * SHARD_W

ADAM_LR = 0.001
ADAM_B1 = 0.9
ADAM_B2 = 0.999
ADAM_EPS = 1e-08
ADAM_WD = 0.01
ADAM_STEP = 10

NN = (((1,), (0,)), ((), ()))
NT = (((1,), (1,)), ((), ()))
TN = (((0,), (0,)), ((), ()))


def _dot(a, b, dims):
    return lax.dot_general(a.astype(BF16), b.astype(BF16), dims, preferred_element_type=F32)


def _split2(a):
    hi = a.astype(BF16)
    lo = (a - hi.astype(F32)).astype(BF16)
    return hi, lo


def _dot3(a, b, dims):
    ah, al = _split2(a)
    bh, bl = _split2(b)
    d = functools.partial(lax.dot_general, dimension_numbers=dims, preferred_element_type=F32)
    return d(ah, bh) + (d(ah, bl) + d(al, bh))


def _sel_dot_impl(sel01, x, dims):
    sel = sel01.astype(BF16)
    h1 = x.astype(BF16)
    r1 = x - h1.astype(F32)
    h2 = r1.astype(BF16)
    h3 = (r1 - h2.astype(F32)).astype(BF16)
    d = functools.partial(lax.dot_general, dimension_numbers=dims, preferred_element_type=F32)
    return d(sel, h1) + (d(sel, h2) + d(sel, h3))


@jax.custom_vjp
def _sel_dot(sel01, x):
    return _sel_dot_impl(sel01, x, NN)


_sel_dot.defvjp(lambda s, x: (_sel_dot(s, x), s),
                lambda s, g: (jnp.zeros_like(s), _sel_dot_impl(s, g, TN)))


def _make_mm(dotfn):
    @jax.custom_vjp
    def nn(a, b):
        return dotfn(a, b, NN)

    @jax.custom_vjp
    def nt(a, b):
        return dotfn(a, b, NT)

    @jax.custom_vjp
    def tn(a, b):
        return dotfn(a, b, TN)

    nn.defvjp(lambda a, b: (nn(a, b), (a, b)), lambda r, g: (nt(g, r[1]), tn(r[0], g)))
    nt.defvjp(lambda a, b: (nt(a, b), (a, b)), lambda r, g: (nn(g, r[1]), tn(g, r[0])))
    tn.defvjp(lambda a, b: (tn(a, b), (a, b)), lambda r, g: (nt(r[1], g), nn(r[0], g)))
    return nn, nt, tn


mm_nn, mm_nt, mm_tn = _make_mm(_dot)
mm3_nn, mm3_nt, mm3_tn = _make_mm(_dot3)


def _sigmoid(x):
    return jax.nn.sigmoid(x)


def _silu(x):
    return x * _sigmoid(x)


def _softplus_parts(x):
    sp = jnp.log1p(jnp.exp(-jnp.abs(x)))
    return jnp.maximum(x, 0.0) + sp, jnp.maximum(-x, 0.0) + sp


def _rmsnorm(x, g):
    return x * lax.rsqrt(jnp.mean(x * x, axis=-1, keepdims=True) + NORM_EPS) * g


def _iota2(shape, dim):
    return lax.broadcasted_iota(jnp.int32, shape, dim)


def _div64(i):
    return lax.shift_right_logical(i, jnp.full(i.shape, 6, jnp.int32))


def _each(f, *lists):
    return [f(*a) for a in zip(*lists)]


@jax.custom_vjp
def _inv_unit_lower(ms):
    n = ms[0].shape[0]
    eye = (_iota2((n, n), 0) == _iota2((n, n), 1)).astype(F32)
    rs = [eye - m for m in ms]
    ps = ms
    for _ in range(5):
        ps = _each(mm3_nn, ps, ps)
        rs = _each(lambda r, p: r + mm3_nn(r, p), rs, ps)
    return rs


def _inv_fwd(ms):
    rs = _inv_unit_lower(ms)
    return rs, rs


def _inv_bwd(rs, gs):
    ts = _each(mm3_tn, rs, gs)
    return (_each(lambda t, r: -mm3_nt(t, r), ts, rs),)


_inv_unit_lower.defvjp(_inv_fwd, _inv_bwd)


def _dn_block(cq, ck, cv, bcol, acol, zt, alog, dtb, gn, s0):
    n = SUPER
    h = DN_CHUNK
    row = _iota2((n, n), 0)
    col = _iota2((n, n), 1)
    same = _div64(row) == _div64(col)
    incl = jnp.logical_and(same, row >= col)
    strict = jnp.logical_and(same, row > col)
    incl_f = incl.astype(F32)

    qn = _each(lambda x: x * lax.rsqrt(jnp.sum(x * x, axis=-1, keepdims=True) + NORM_EPS) * (D_HEAD ** -0.5), cq)
    kn = _each(lambda x: x * lax.rsqrt(jnp.sum(x * x, axis=-1, keepdims=True) + NORM_EPS), ck)
    beta = _each(_sigmoid, bcol)
    g = _each(lambda al, ac, dt: -(jnp.exp(al) * _softplus_parts(ac + dt)[0]), alog, acol, dtb)
    gcum = _each(lambda x: _sel_dot(incl_f, jnp.broadcast_to(x, (n, n))), g)
    gam_incl = _each(lambda x: jnp.where(incl, jnp.exp(jnp.where(incl, x - x.T, 0.0)), 0.0), gcum)
    kk = _each(mm_nt, kn, kn)
    t_inv = _inv_unit_lower(_each(lambda b, x, gm: b * x * jnp.where(strict, gm, 0.0), beta, kk, gam_incl))
    eg = _each(jnp.exp, gcum)
    u = _each(lambda t, v, b: mm_nn(t, v * b), t_inv, cv, beta)
    w = _each(lambda t, k, b, e: mm_nn(t, k * (b * e)), t_inv, kn, beta, eg)
    a_intra = _each(lambda q, k, gm: mm_nt(q, k) * gm, qn, kn, gam_incl)
    q_dec = _each(lambda q, e: q * e, qn, eg)
    last0 = _each(lambda x: x[h - 1:h, :], gcum)
    last1 = _each(lambda x: x[n - 1:n, :], gcum)
    k_dec = _each(lambda k, x, l0, l1: k * jnp.exp(jnp.concatenate(
        [jnp.broadcast_to(l0, (h, n)), jnp.broadcast_to(l1, (h, n))], axis=0) - x), kn, gcum, last0, last1)
    v0 = _each(lambda uu, ww, s: uu[:h] - mm_nn(ww[:h], s), u, w, s0)
    o0 = _each(lambda q, s: mm_nn(q[:h], s), q_dec, s0)
    s1 = _each(lambda s, l0, k, v: s * jnp.exp(l0) + mm_tn(k[:h], v), s0, last0, k_dec, v0)
    v1 = _each(lambda uu, ww, s: uu[h:] - mm_nn(ww[h:], s), u, w, s1)
    o1 = _each(lambda q, s: mm_nn(q[h:], s), q_dec, s1)
    s2 = _each(lambda s, l1, k, v: s * jnp.exp(l1) + mm_tn(k[h:], v), s1, last1, k_dec, v1)
    o = _each(lambda a, b, am, x, y: jnp.concatenate([a, b], axis=0) + mm_nn(am, jnp.concatenate([x, y], axis=0)),
              o0, o1, a_intra, v0, v1)
    out = _each(lambda x, z: _rmsnorm(x, gn) * _silu(z), o, zt)
    return out, s2


def _mem_fn(mq, mz, mkv):
    mk = mkv[:, :MEM_W]
    mv = mkv[:, MEM_W:]
    lane = _iota2((1, MEM_W), 1)
    out = jnp.zeros(mq.shape, F32)
    for hd in range(MEM_HEADS):
        hm = (_div64(lane) == hd).astype(F32)
        s = mm_nt(mq * hm, mk) * (1.0 / math.sqrt(MEM_DH))
        s = s - jnp.max(s, axis=-1, keepdims=True)
        e = jnp.exp(s)
        p = e / jnp.sum(e, axis=-1, keepdims=True)
        out = out + mm_nn(p, mv) * hm
    return out * _silu(mz)


def _merge_fn(gd, gs, gm, yd, ys, ym):
    return _sigmoid(gd) * yd + _sigmoid(gs) * ys + _sigmoid(gm) * ym


def _loss_fn(x, mo, fg, tgt):
    y = _rmsnorm(x + mo, fg)
    err = y - tgt
    return 0.5 * jnp.sum(jnp.mean(err * err, axis=-1, keepdims=True), axis=0, keepdims=True)


def _matmul(a, b, mode, out_dtype, tm, tn, tk, name, b_col0=0, n_cols=None):
    if mode == "nn":
        m, kdim = a.shape
        n = b.shape[1] if n_cols is None else n_cols
    elif mode == "nt":
        m, kdim = a.shape
        n = b.shape[0]
    else:
        kdim, m = a.shape
        n = b.shape[1] if n_cols is None else n_cols
    tm, tn, tk = min(tm, m), min(tn, n), min(tk, kdim)
    assert m % tm == 0 and n % tn == 0 and kdim % tk == 0 and b_col0 % tn == 0
    nk = kdim // tk
    jb = b_col0 // tn
    dims = {"nn": NN, "nt": NT, "tn": TN}[mode]

    def body(a_ref, b_ref, o_ref, acc_ref):
        k = pl.program_id(2)
        part = _dot(a_ref[...], b_ref[...], dims)

        @pl.when(k == 0)
        def _():
            acc_ref[...] = part

        @pl.when(k > 0)
        def _():
            acc_ref[...] += part

        @pl.when(k == nk - 1)
        def _():
            o_ref[...] = acc_ref[...].astype(o_ref.dtype)

    if mode == "nn":
        a_spec = pl.BlockSpec((tm, tk), lambda i, j, k: (i, k))
        b_spec = pl.BlockSpec((tk, tn), lambda i, j, k: (k, j + jb))
    elif mode == "nt":
        a_spec = pl.BlockSpec((tm, tk), lambda i, j, k: (i, k))
        b_spec = pl.BlockSpec((tn, tk), lambda i, j, k: (j, k))
    else:
        a_spec = pl.BlockSpec((tk, tm), lambda i, j, k: (k, i))
        b_spec = pl.BlockSpec((tk, tn), lambda i, j, k: (k, j + jb))
    return pl.pallas_call(
        body,
        name=name,
        grid=(m // tm, n // tn, nk),
        in_specs=[a_spec, b_spec],
        out_specs=pl.BlockSpec((tm, tn), lambda i, j, k: (i, j)),
        out_shape=jax.ShapeDtypeStruct((m, n), out_dtype),
        scratch_shapes=[pltpu.VMEM((tm, tn), F32)],
        compiler_params=pltpu.CompilerParams(dimension_semantics=("parallel", "parallel", "arbitrary")),
    )(a, b)


def _norm_in(x, g, tm=256):
    t = x.shape[0]

    def body(x_ref, g_ref, h_ref):
        h_ref[...] = _rmsnorm(x_ref[...], g_ref[...]).astype(BF16)

    return pl.pallas_call(
        body,
        name="norm_in",
        grid=(t // tm,),
        in_specs=[pl.BlockSpec((tm, D_MODEL), lambda i: (i, 0)), pl.BlockSpec((1, D_MODEL), lambda i: (0, 0))],
        out_specs=pl.BlockSpec((tm, D_MODEL), lambda i: (i, 0)),
        out_shape=jax.ShapeDtypeStruct((t, D_MODEL), BF16),
    )(x, g)


def _norm_in_bwd(x, g, dh, dres, tm=256):
    t = x.shape[0]

    def body(x_ref, g_ref, dh_ref, dres_ref, dx_ref, dg_ref):
        _, vjp = jax.vjp(_rmsnorm, x_ref[...], g_ref[...])
        dx, dg = vjp(dh_ref[...])
        dx_ref[...] = dx + dres_ref[...]

        @pl.when(pl.program_id(0) == 0)
        def _():
            dg_ref[...] = jnp.zeros_like(dg_ref)

        dg_ref[...] += dg

    row = pl.BlockSpec((tm, D_MODEL), lambda i: (i, 0))
    vec = pl.BlockSpec((1, D_MODEL), lambda i: (0, 0))
    return pl.pallas_call(
        body,
        name="norm_in_bwd",
        grid=(t // tm,),
        in_specs=[row, vec, row, row],
        out_specs=[row, vec],
        out_shape=[jax.ShapeDtypeStruct((t, D_MODEL), F32), jax.ShapeDtypeStruct((1, D_MODEL), F32)],
    )(x, g, dh, dres)


def _merge(proj, yd, ys, ym, tm=256, tc=512):
    t = proj.shape[0]
    g0 = O_GATES // tc
    gstep = D_MODEL // tc

    def body(gd, gs, gm, yd_ref, ys_ref, ym_ref, o_ref):
        o_ref[...] = _merge_fn(gd[...], gs[...], gm[...], yd_ref[...], ys_ref[...], ym_ref[...]).astype(BF16)

    def gate(k):
        return pl.BlockSpec((tm, tc), lambda i, j: (i, g0 + k * gstep + j))

    blk = pl.BlockSpec((tm, tc), lambda i, j: (i, j))
    return pl.pallas_call(
        body,
        name="merge",
        grid=(t // tm, D_MODEL // tc),
        in_specs=[gate(0), gate(1), gate(2), blk, blk, blk],
        out_specs=blk,
        out_shape=jax.ShapeDtypeStruct((t, D_MODEL), BF16),
    )(proj, proj, proj, yd, ys, ym)


def _merge_bwd(proj, yd, ys, ym, dmerged, tm=256, tc=512):
    t = proj.shape[0]
    g0 = O_GATES // tc
    gstep = D_MODEL // tc

    def body(gd, gs, gm, yd_ref, ys_ref, ym_ref, dm_ref, dyd, dys, dym, dgd, dgs, dgm):
        _, vjp = jax.vjp(_merge_fn, gd[...], gs[...], gm[...], yd_ref[...], ys_ref[...], ym_ref[...])
        outs = vjp(dm_ref[...])
        for ref, val in zip((dgd, dgs, dgm, dyd, dys, dym), outs):
            ref[...] = val.astype(BF16)

    def gate(k):
        return pl.BlockSpec((tm, tc), lambda i, j: (i, g0 + k * gstep + j))

    blk = pl.BlockSpec((tm, tc), lambda i, j: (i, j))
    o = jax.ShapeDtypeStruct((t, D_MODEL), BF16)
    return pl.pallas_call(
        body,
        name="merge_bwd",
        grid=(t // tm, D_MODEL // tc),
        in_specs=[gate(0), gate(1), gate(2), blk, blk, blk, blk],
        out_specs=[blk] * 6,
        out_shape=[o] * 6,
    )(proj, proj, proj, yd, ys, ym, dmerged)


def _loss_head(x, mo, fg, tgt, tm=256):
    t = x.shape[0]

    def body(x_ref, mo_ref, fg_ref, t_ref, loss_ref, dout_ref, dfg_ref):
        loss, vjp = jax.vjp(_loss_fn, x_ref[...], mo_ref[...], fg_ref[...], t_ref[...])
        _, dmo, dfg, _ = vjp(jnp.ones((1, 1), F32))

        @pl.when(pl.program_id(0) == 0)
        def _():
            loss_ref[...] = jnp.zeros_like(loss_ref)
            dfg_ref[...] = jnp.zeros_like(dfg_ref)

        loss_ref[...] += jnp.broadcast_to(loss, loss_ref.shape)
        dfg_ref[...] += dfg
        dout_ref[...] = dmo

    row = pl.BlockSpec((tm, D_MODEL), lambda i: (i, 0))
    vec = pl.BlockSpec((1, D_MODEL), lambda i: (0, 0))
    return pl.pallas_call(
        body,
        name="loss_head",
        grid=(t // tm,),
        in_specs=[row, row, vec, row],
        out_specs=[pl.BlockSpec((1, LANE), lambda i: (0, 0)), row, vec],
        out_shape=[jax.ShapeDtypeStruct((1, LANE), F32), jax.ShapeDtypeStruct((t, D_MODEL), F32),
                   jax.ShapeDtypeStruct((1, D_MODEL), F32)],
    )(x, mo, fg, tgt)


def _shift_rows(x, s):
    t = x.shape[0]
    if s == 0:
        return x
    rolled = pltpu.roll(x, s % t, 0)
    row = _iota2(x.shape, 0)
    keep = row >= s if s > 0 else row < t + s
    return jnp.where(keep, rolled, 0.0)


def _conv_pre(x, w):
    return sum(_shift_rows(x, CONV_K - 1 - j) * w[j:j + 1, :] for j in range(CONV_K))


def _dn_conv(proj, conv_w):
    t = proj.shape[0]
    nb = 3 * D_MODEL // LANE

    def body(x_ref, w_ref, c_ref):
        c_ref[...] = _silu(_conv_pre(x_ref[...], w_ref[...]))

    return pl.pallas_call(
        body,
        name="dn_conv",
        grid=(nb,),
        in_specs=[pl.BlockSpec((t, LANE), lambda j: (0, j)), pl.BlockSpec((CONV_K, LANE), lambda j: (0, j))],
        out_specs=pl.BlockSpec((t, LANE), lambda j: (0, j)),
        out_shape=jax.ShapeDtypeStruct((t, 3 * D_MODEL), F32),
    )(proj, conv_w)


def _dn_conv_bwd(proj, conv_w, dc, part):
    t = proj.shape[0]
    nb = D_MODEL // LANE
    b0 = part * nb

    def body(x_ref, w_ref, dc_ref, dx_ref, dw_ref):
        x = x_ref[...]
        w = w_ref[...]
        pre = _conv_pre(x, w)
        sg = _sigmoid(pre)
        dpre = dc_ref[...] * (sg * (1.0 + pre * (1.0 - sg)))
        dx = sum(_shift_rows(dpre, -(CONV_K - 1 - j)) * w[j:j + 1, :] for j in range(CONV_K))
        dx_ref[...] = dx.astype(BF16)
        dw_ref[...] = jnp.concatenate(
            [jnp.sum(dpre * _shift_rows(x, CONV_K - 1 - j), axis=0, keepdims=True) for j in range(CONV_K)], axis=0)

    return pl.pallas_call(
        body,
        name=f"dn_conv_bwd{part}",
        grid=(nb,),
        in_specs=[pl.BlockSpec((t, LANE), lambda j: (0, b0 + j)), pl.BlockSpec((CONV_K, LANE), lambda j: (0, b0 + j)),
                  pl.BlockSpec((t, LANE), lambda j: (0, j))],
        out_specs=[pl.BlockSpec((t, LANE), lambda j: (0, j)), pl.BlockSpec((CONV_K, LANE), lambda j: (0, j))],
        out_shape=[jax.ShapeDtypeStruct((t, D_MODEL), BF16), jax.ShapeDtypeStruct((CONV_K, D_MODEL), F32)],
    )(proj, conv_w, dc)


def _ba_columns(ba, hd):
    lane = _iota2(ba.shape, 1)
    bcol = jnp.sum(jnp.where(lane == hd, ba, 0.0), axis=1, keepdims=True)
    acol = jnp.sum(jnp.where(lane == N_HEADS + hd, ba, 0.0), axis=1, keepdims=True)
    return bcol, acol


def _head_scalar(row, hd):
    lane = _iota2(row.shape, 1)
    return jnp.sum(jnp.where(lane == hd, row, 0.0), axis=1, keepdims=True)


DN_HP = 4


def _dn_inputs(cq, ck, cv, ba_ref, z_ref, alog_ref, dtb_ref, heads, lanes):
    ba = ba_ref[...]
    cols = [_ba_columns(ba, hd) for hd in heads]
    return ([cq[:, ln] for ln in lanes], [ck[:, ln] for ln in lanes], [cv[:, ln] for ln in lanes],
            [c[0] for c in cols], [c[1] for c in cols], [z_ref[:, ln] for ln in lanes],
            [_head_scalar(alog_ref[...], hd) for hd in heads], [_head_scalar(dtb_ref[...], hd) for hd in heads])


def _dn_specs(nblk, reverse):
    w = DN_HP * LANE
    nq = D_MODEL // w

    def row(i):
        return nblk - 1 - i if reverse else i

    def colblk(b0):
        return pl.BlockSpec((SUPER, w), lambda i, h: (row(i), b0 + h))

    ba = pl.BlockSpec((SUPER, LANE), lambda i, h: (row(i), O_BA // LANE))
    vec = pl.BlockSpec((1, LANE), lambda i, h: (0, 0))
    st = pl.BlockSpec((1, DN_HP, D_HEAD, D_HEAD), lambda i, h: (row(i), h, 0, 0))
    return colblk, nq, ba, vec, st


def _dn_fwd(c, proj, alog_row, dtb_row, gn):
    t = c.shape[0]
    nblk = t // SUPER
    colblk, nq, ba, vec, st = _dn_specs(nblk, False)

    def body(cq, ck, cv, ba_ref, z_ref, alog_ref, dtb_ref, gn_ref, o_ref, s_ref, state):
        @pl.when(jnp.logical_and(pl.program_id(0) == 0, pl.program_id(1) == 0))
        def _():
            state[...] = jnp.zeros_like(state)

        heads = [pl.program_id(1) * DN_HP + j for j in range(DN_HP)]
        lanes = [slice(j * LANE, (j + 1) * LANE) for j in range(DN_HP)]
        s0 = [state[hd] for hd in heads]
        outs, s2 = _dn_block(*_dn_inputs(cq, ck, cv, ba_ref, z_ref, alog_ref, dtb_ref, heads, lanes), gn_ref[...], s0)
        for j, (hd, ln) in enumerate(zip(heads, lanes)):
            s_ref[0, j] = s0[j]
            o_ref[:, ln] = outs[j].astype(BF16)
            state[hd] = s2[j]

    return pl.pallas_call(
        body,
        name="dn_fwd",
        grid=(nblk, N_HEADS // DN_HP),
        in_specs=[colblk(0), colblk(nq), colblk(2 * nq), ba, colblk(O_Z_DN // (DN_HP * LANE)), vec, vec, vec],
        out_specs=[colblk(0), st],
        out_shape=[jax.ShapeDtypeStruct((t, D_MODEL), BF16),
                   jax.ShapeDtypeStruct((nblk, N_HEADS, D_HEAD, D_HEAD), F32)],
        scratch_shapes=[pltpu.VMEM((N_HEADS, D_HEAD, D_HEAD), F32)],
    )(c, c, c, proj, proj, alog_row, dtb_row, gn)


def _dn_bwd(c, proj, alog_row, dtb_row, gn, states, do):
    t = c.shape[0]
    nblk = t // SUPER
    colblk, nq, ba, vec, st = _dn_specs(nblk, True)

    def body(cq, ck, cv, ba_ref, z_ref, alog_ref, dtb_ref, gn_ref, s_ref, do_ref,
             dq_ref, dk_ref, dv_ref, dz_ref, dba_ref, dsc_ref, dgn_ref, dstate):
        i = pl.program_id(0)
        hq = pl.program_id(1)

        @pl.when(jnp.logical_and(i == 0, hq == 0))
        def _():
            dstate[...] = jnp.zeros_like(dstate)
            dsc_ref[...] = jnp.zeros_like(dsc_ref)
            dgn_ref[...] = jnp.zeros_like(dgn_ref)

        @pl.when(hq == 0)
        def _():
            dba_ref[...] = jnp.zeros_like(dba_ref)

        lane = _iota2((SUPER, LANE), 1)
        lane1 = _iota2((1, LANE), 1)
        heads = [hq * DN_HP + j for j in range(DN_HP)]
        lanes = [slice(j * LANE, (j + 1) * LANE) for j in range(DN_HP)]
        ds_in = [dstate[hd] for hd in heads]
        s_in = [s_ref[0, j] for j in range(DN_HP)]
        _, vjp = jax.vjp(_dn_block, *_dn_inputs(cq, ck, cv, ba_ref, z_ref, alog_ref, dtb_ref, heads, lanes),
                         gn_ref[...], s_in)
        dq, dk, dv, dbc, dac, dz, dal, ddt, dgn, ds0 = vjp(([do_ref[:, ln].astype(F32) for ln in lanes], ds_in))
        dba = jnp.zeros((SUPER, LANE), F32)
        dal_row = jnp.zeros((1, LANE), F32)
        ddt_row = jnp.zeros((1, LANE), F32)
        for j, (hd, ln) in enumerate(zip(heads, lanes)):
            dq_ref[:, ln] = dq[j]
            dk_ref[:, ln] = dk[j]
            dv_ref[:, ln] = dv[j]
            dz_ref[:, ln] = dz[j].astype(BF16)
            dstate[hd] = ds0[j]
            dba = dba + jnp.where(lane == hd, dbc[j], 0.0) + jnp.where(lane == N_HEADS + hd, dac[j], 0.0)
            dal_row = dal_row + jnp.where(lane1 == hd, dal[j], 0.0)
            ddt_row = ddt_row + jnp.where(lane1 == hd, ddt[j], 0.0)
        dba_ref[...] += dba
        dsc_ref[0:1, :] += dal_row
        dsc_ref[1:2, :] += ddt_row
        dgn_ref[...] += dgn

    outs = pl.pallas_call(
        body,
        name="dn_bwd",
        grid=(nblk, N_HEADS // DN_HP),
        in_specs=[colblk(0), colblk(nq), colblk(2 * nq), ba, colblk(O_Z_DN // (DN_HP * LANE)), vec, vec, vec, st,
                  colblk(0)],
        out_specs=[colblk(0), colblk(0), colblk(0), colblk(0),
                   pl.BlockSpec((SUPER, LANE), lambda i, h: (nblk - 1 - i, 0)),
                   pl.BlockSpec((2, LANE), lambda i, h: (0, 0)), vec],
        out_shape=[jax.ShapeDtypeStruct((t, D_MODEL), F32)] * 3
        + [jax.ShapeDtypeStruct((t, D_MODEL), BF16), jax.ShapeDtypeStruct((t, LANE), F32),
           jax.ShapeDtypeStruct((2, LANE), F32), jax.ShapeDtypeStruct((1, LANE), F32)],
        scratch_shapes=[pltpu.VMEM((N_HEADS, D_HEAD, D_HEAD), F32)],
    )(c, c, c, proj, proj, alog_row, dtb_row, gn, states, do)
    return outs


SB_TQ = 256
SB_TK = 256
SB_HP = 4


def _sb_logits(z, mask):
    sp = jnp.log(1.0 + jnp.exp(-jnp.abs(z)))
    lf_raw = -(jnp.maximum(z, 0.0) + sp)
    lb = lf_raw + z
    lf = lf_raw if mask is None else jnp.where(mask, lf_raw, 0.0)
    return lb, lf_raw, lf


def _suffix_sums(x, sel):
    hi, lo = _split2(x)
    d = functools.partial(lax.dot_general, dimension_numbers=NN, preferred_element_type=F32)
    return d(hi, sel) + d(lo, sel)


def _sb_diag_mask(tq, r):
    return r * SB_TK + _iota2((tq, SB_TK), 1) < _iota2((tq, SB_TK), 0)


def _sb_specs(t, tq):
    w = SB_HP * LANE
    q0, k0, v0, z0 = (O_QKV_SB // w, (O_QKV_SB + D_MODEL) // w, (O_QKV_SB + 2 * D_MODEL) // w, O_Z_SB // w)

    def blk(b0):
        return pl.BlockSpec((tq, w), lambda h, i: (i, b0 + h))

    def full(b0, **kw):
        return pl.BlockSpec((t, w), lambda h, i: (0, b0 + h), **kw)

    once = dict(pipeline_mode=pl.Buffered(1))
    return blk(q0), full(k0, **once), full(v0, **once), blk(z0), blk(0), full(0)


def _sb_fwd(proj):
    t = proj.shape[0]
    tq = min(SB_TQ, t)
    ndiag = tq // SB_TK
    scale = 1.0 / math.sqrt(D_HEAD)

    def body(q_ref, k_ref, v_ref, z_ref, o_ref, oraw_ref):
        qi = pl.program_id(1)
        lanes = [slice(hd * LANE, (hd + 1) * LANE) for hd in range(SB_HP)]
        qs = [(q_ref[:, ln] * scale).astype(BF16) for ln in lanes]
        after = (_iota2((SB_TK, SB_TK), 0) > _iota2((SB_TK, SB_TK), 1)).astype(BF16)
        oraw_ref[...] = jnp.zeros_like(oraw_ref)

        def block(kb, mask, c_lf):
            rows = pl.ds(pl.multiple_of(kb * SB_TK, SB_TK), SB_TK)
            z = _each(lambda q, ln: _dot(q, k_ref[rows, ln], NT), qs, lanes)
            lg = _each(lambda x: _sb_logits(x, mask), z)
            surv = _each(lambda x: _suffix_sums(x[2], after), lg)
            att = _each(lambda x, s, c: jnp.exp(x[0] + s + c), lg, surv, c_lf)
            if mask is not None:
                att = _each(lambda a: jnp.where(mask, a, 0.0), att)
            pv = _each(lambda a, ln: _dot(a, v_ref[rows, ln], NN), att, lanes)
            for p, ln in zip(pv, lanes):
                oraw_ref[:, ln] += p
            return tuple(_each(lambda c, x: c + jnp.sum(x[2], axis=1, keepdims=True), c_lf, lg))

        carry = tuple(jnp.zeros((tq, 1), F32) for _ in range(SB_HP))
        for r in reversed(range(ndiag)):
            carry = block(qi * ndiag + r, _sb_diag_mask(tq, r), carry)
        lax.fori_loop(0, qi * ndiag, lambda i, c: block(qi * ndiag - 1 - i, None, c), carry)
        o_ref[...] = (oraw_ref[...] * _silu(z_ref[...])).astype(BF16)

    q_spec, k_spec, v_spec, z_spec, out, _ = _sb_specs(t, tq)
    return pl.pallas_call(
        body,
        name="sb_fwd",
        grid=(N_HEADS // SB_HP, t // tq),
        in_specs=[q_spec, k_spec, v_spec, z_spec],
        out_specs=[out, out],
        out_shape=[jax.ShapeDtypeStruct((t, D_MODEL), BF16), jax.ShapeDtypeStruct((t, D_MODEL), F32)],
    )(proj, proj, proj, proj)


def _sb_bwd(proj, oraw, do):
    t = proj.shape[0]
    tq = min(SB_TQ, t)
    ndiag = tq // SB_TK
    scale = 1.0 / math.sqrt(D_HEAD)

    def body(q_ref, k_ref, v_ref, z_ref, oraw_ref, do_ref, dq_ref, dk_ref, dv_ref, dz_ref, dk_acc, dv_acc,
             p_scr, z_scr):
        qi = pl.program_id(1)
        nq = pl.num_programs(1)

        @pl.when(qi == 0)
        def _():
            dk_acc[...] = jnp.zeros_like(dk_acc)
            dv_acc[...] = jnp.zeros_like(dv_acc)

        heads = range(SB_HP)
        lanes = [slice(hd * LANE, (hd + 1) * LANE) for hd in heads]
        zg = z_ref[...]
        sg = _sigmoid(zg)
        dog = do_ref[...].astype(F32)
        dz_ref[...] = (dog * oraw_ref[...] * (sg * (1.0 + zg * (1.0 - sg)))).astype(BF16)
        d_o = (dog * (zg * sg)).astype(BF16)
        d_o16 = [d_o[:, ln] for ln in lanes]
        qs = [(q_ref[:, ln] * scale).astype(BF16) for ln in lanes]
        ri = _iota2((SB_TK, SB_TK), 0)
        ci = _iota2((SB_TK, SB_TK), 1)
        after = (ri > ci).astype(BF16)
        earlier = (ri < ci).astype(BF16)

        def rows_of(kb):
            return pl.ds(pl.multiple_of(kb * SB_TK, SB_TK), SB_TK)

        def down(kb, mask, c_lf):
            rows = rows_of(kb)
            z = _each(lambda q, ln: _dot(q, k_ref[rows, ln], NT), qs, lanes)
            da = _each(lambda d, ln: _dot(d, v_ref[rows, ln], NT), d_o16, lanes)
            lg = _each(lambda x: _sb_logits(x, mask), z)
            surv = _each(lambda x: _suffix_sums(x[2], after), lg)
            att = _each(lambda x, s, c: jnp.exp(x[0] + s + c), lg, surv, c_lf)
            if mask is not None:
                att = _each(lambda a: jnp.where(mask, a, 0.0), att)
            dv = _each(lambda a, d: _dot(a, d, TN), att, d_o16)
            for hd in heads:
                p_scr[hd, kb] = att[hd] * da[hd]
                z_scr[hd, kb] = z[hd]
                dv_acc[rows, lanes[hd]] += dv[hd]
            return tuple(_each(lambda c, x: c + jnp.sum(x[2], axis=1, keepdims=True), c_lf, lg))

        c_lf = tuple(jnp.zeros((tq, 1), F32) for _ in heads)
        for r in reversed(range(ndiag)):
            c_lf = down(qi * ndiag + r, _sb_diag_mask(tq, r), c_lf)
        lax.fori_loop(0, qi * ndiag, lambda i, c: down(qi * ndiag - 1 - i, None, c), c_lf)

        def up(kb, mask, carry):
            dq, c_p = carry
            rows = rows_of(kb)
            p = [p_scr[hd, kb] for hd in heads]
            zs = [z_scr[hd, kb] for hd in heads]
            before = _each(lambda x, c: _suffix_sums(x, earlier) + c, p, c_p)
            e = _each(lambda x: jnp.exp(-jnp.abs(x)), zs)
            r = _each(lambda x: 1.0 / (1.0 + x), e)
            sig = _each(lambda x, a, b: jnp.where(x >= 0.0, b, a * b), zs, e, r)
            oms = _each(lambda x, a, b: jnp.where(x >= 0.0, a * b, b), zs, e, r)
            if mask is not None:
                sig = _each(lambda a: jnp.where(mask, a, 0.0), sig)
            dzz = _each(lambda x, o, g, b: x * o - g * b, p, oms, sig, before)
            dk = _each(lambda x, q: _dot(x, q, TN), dzz, qs)
            dq = _each(lambda a, x, ln: a + _dot(x, k_ref[rows, ln], NN), dq, dzz, lanes)
            for hd in heads:
                dk_acc[rows, lanes[hd]] += dk[hd]
            return tuple(dq), tuple(_each(lambda c, x: c + jnp.sum(x, axis=1, keepdims=True), c_p, p))

        carry = (tuple(jnp.zeros((tq, D_HEAD), F32) for _ in heads), tuple(jnp.zeros((tq, 1), F32) for _ in heads))
        carry = lax.fori_loop(0, qi * ndiag, lambda kb, c: up(kb, None, c), carry)
        for r in range(ndiag):
            carry = up(qi * ndiag + r, _sb_diag_mask(tq, r), carry)
        dq = carry[0]
        for hd in heads:
            dq_ref[:, lanes[hd]] = (dq[hd] * scale).astype(BF16)

        @pl.when(qi == nq - 1)
        def _():
            dk_ref[...] = dk_acc[...].astype(BF16)
            dv_ref[...] = dv_acc[...].astype(BF16)

    q_spec, k_spec, v_spec, z_spec, blk, full = _sb_specs(t, tq)
    o = jax.ShapeDtypeStruct((t, D_MODEL), BF16)
    w = SB_HP * LANE
    return pl.pallas_call(
        body,
        name="sb_bwd",
        grid=(N_HEADS // SB_HP, t // tq),
        in_specs=[q_spec, k_spec, v_spec, z_spec, blk, blk],
        out_specs=[blk, full, full, blk],
        out_shape=[o, o, o, o],
        scratch_shapes=[pltpu.VMEM((t, w), F32), pltpu.VMEM((t, w), F32)]
        + [pltpu.VMEM((SB_HP, t // SB_TK, tq, SB_TK), F32)] * 2,
    )(proj, proj, proj, proj, oraw, do)


def _mem_kv_fn(mem, mg, w):
    return mm_nn(_rmsnorm(mem, mg), w)


def _mem_kv(mem, mg, w):
    def body(m_ref, g_ref, w_ref, o_ref):
        o_ref[...] = _mem_kv_fn(m_ref[...], g_ref[...], w_ref[...])

    return pl.pallas_call(body, name="mem_kv", out_shape=jax.ShapeDtypeStruct((MEM_LEN, 2 * MEM_W), F32))(mem, mg, w)


def _mem_kv_bwd(mem, mg, w, dmkv):
    def body(m_ref, g_ref, w_ref, d_ref, dg_ref, dw_ref):
        _, vjp = jax.vjp(_mem_kv_fn, m_ref[...], g_ref[...], w_ref[...].astype(F32))
        _, dg, dw = vjp(d_ref[...])
        dg_ref[...] = dg
        dw_ref[...] = dw

    return pl.pallas_call(
        body, name="mem_kv_bwd",
        out_shape=[jax.ShapeDtypeStruct((1, D_MODEL), F32), jax.ShapeDtypeStruct((D_MODEL, 2 * MEM_W), F32)],
    )(mem, mg, w, dmkv)


def _mem_attn(proj, mkv, tm=256):
    t = proj.shape[0]
    tm = min(tm, t)

    def body(q_ref, z_ref, kv_ref, o_ref):
        o_ref[...] = _mem_fn(q_ref[...], z_ref[...], kv_ref[...]).astype(BF16)

    return pl.pallas_call(
        body,
        name="mem_attn",
        grid=(t // tm,),
        in_specs=[pl.BlockSpec((tm, MEM_W), lambda i: (i, O_MQ // MEM_W)),
                  pl.BlockSpec((tm, MEM_W), lambda i: (i, O_MZ // MEM_W)),
                  pl.BlockSpec((MEM_LEN, 2 * MEM_W), lambda i: (0, 0))],
        out_specs=pl.BlockSpec((tm, MEM_W), lambda i: (i, 0)),
        out_shape=jax.ShapeDtypeStruct((t, MEM_W), BF16),
    )(proj, proj, mkv)


def _mem_attn_bwd(proj, mkv, do, tm=256):
    t = proj.shape[0]
    tm = min(tm, t)

    def body(q_ref, z_ref, kv_ref, do_ref, dq_ref, dz_ref, dkv_ref):
        _, vjp = jax.vjp(_mem_fn, q_ref[...], z_ref[...], kv_ref[...])
        dq, dz, dkv = vjp(do_ref[...].astype(F32))
        dq_ref[...] = dq.astype(BF16)
        dz_ref[...] = dz.astype(BF16)

        @pl.when(pl.program_id(0) == 0)
        def _():
            dkv_ref[...] = jnp.zeros_like(dkv_ref)

        dkv_ref[...] += dkv

    blk = pl.BlockSpec((tm, MEM_W), lambda i: (i, 0))
    kv = pl.BlockSpec((MEM_LEN, 2 * MEM_W), lambda i: (0, 0))
    return pl.pallas_call(
        body,
        name="mem_attn_bwd",
        grid=(t // tm,),
        in_specs=[pl.BlockSpec((tm, MEM_W), lambda i: (i, O_MQ // MEM_W)),
                  pl.BlockSpec((tm, MEM_W), lambda i: (i, O_MZ // MEM_W)), kv, blk],
        out_specs=[blk, blk, kv],
        out_shape=[jax.ShapeDtypeStruct((t, MEM_W), BF16), jax.ShapeDtypeStruct((t, MEM_W), BF16),
                   jax.ShapeDtypeStruct((MEM_LEN, 2 * MEM_W), F32)],
    )(proj, proj, mkv, do)


def _local_step(x, mem, tgt, norm_g, mem_norm_g, w_alt, conv_w, alog_row, dtb_row, dn_norm_g, w_mem_kv, w_br_dn, w_br_sb,
                w_br_mem, w_out, final_g):
    h = _norm_in(x, norm_g)
    proj = _matmul(h, w_alt, "nt", F32, 2048, 384, 1024, "proj")

    c = _dn_conv(proj, conv_w)
    o_dn, states = _dn_fwd(c, proj, alog_row, dtb_row, dn_norm_g)
    o_sb, o_sb_raw = _sb_fwd(proj)
    mkv = _mem_kv(mem, mem_norm_g, w_mem_kv)
    o_m = _mem_attn(proj, mkv)

    y_dn = _matmul(o_dn, w_br_dn, "nn", F32, 512, 1024, 1024, "y_dn")
    y_sb = _matmul(o_sb, w_br_sb, "nn", F32, 512, 1024, 1024, "y_sb")
    y_m = _matmul(o_m, w_br_mem, "nn", F32, 512, 1024, 1024, "y_m")
    merged = _merge(proj, y_dn, y_sb, y_m)
    mo = _matmul(merged, w_out, "nn", F32, 512, 1024, 1024, "mo")
    loss, dout, d_final_g = _loss_head(x, mo, final_g, tgt)

    dmerged = _matmul(dout, w_out, "nt", F32, 512, 1024, 1024, "dmerged")
    dw_out = _matmul(merged, dout, "tn", F32, 256, 1024, 2048, "dw_out")
    dy_dn, dy_sb, dy_m, dg_dn, dg_sb, dg_m = _merge_bwd(proj, y_dn, y_sb, y_m, dmerged)
    do_dn = _matmul(dy_dn, w_br_dn, "nt", BF16, 512, 1024, 1024, "do_dn")
    do_sb = _matmul(dy_sb, w_br_sb, "nt", BF16, 512, 1024, 1024, "do_sb")
    do_m = _matmul(dy_m, w_br_mem, "nt", BF16, 512, 256, 1024, "do_m")
    dw_br_dn = _matmul(o_dn, dy_dn, "tn", F32, 256, 1024, 2048, "dw_br_dn")
    dw_br_sb = _matmul(o_sb, dy_sb, "tn", F32, 256, 1024, 2048, "dw_br_sb")
    dw_br_mem = _matmul(o_m, dy_m, "tn", F32, 256, 1024, 2048, "dw_br_mem")

    dmq, dmz, dmkv = _mem_attn_bwd(proj, mkv, do_m)
    d_mem_norm_g, dw_mem_kv = _mem_kv_bwd(mem, mem_norm_g, w_mem_kv, dmkv)
    dq_sb, dk_sb, dv_sb, dz_sb = _sb_bwd(proj, o_sb_raw, do_sb)
    dcq, dck, dcv, dz_dn, dba, dscal, d_dn_norm_g = _dn_bwd(c, proj, alog_row, dtb_row, dn_norm_g, states, do_dn)
    dq_dn, dcw_q = _dn_conv_bwd(proj, conv_w, dcq, 0)
    dk_dn, dcw_k = _dn_conv_bwd(proj, conv_w, dck, 1)
    dv_dn, dcw_v = _dn_conv_bwd(proj, conv_w, dcv, 2)
    d_conv_w = jnp.concatenate([dcw_q, dcw_k, dcw_v], axis=1)

    dproj = jnp.concatenate([dq_dn, dk_dn, dv_dn, dz_dn, dq_sb, dk_sb, dv_sb, dz_sb, dmq, dmz, dg_dn, dg_sb, dg_m,
                             dba.astype(BF16)], axis=1)
    dh = _matmul(dproj, w_alt, "nn", F32, 512, 1024, 3968, "dh")
    dw_alt = _matmul(dproj, h, "tn", F32, 384, 1024, 2048, "dw_alt")
    grad_x, d_norm_g = _norm_in_bwd(x, norm_g, dh, dout)
    return dict(loss=loss, grad_x=grad_x, norm_g=d_norm_g, mem_norm_g=d_mem_norm_g, w_alt=dw_alt, conv_w=d_conv_w,
                scal=dscal, dn_norm_g=d_dn_norm_g, w_mem_kv=dw_mem_kv, w_br_dn=dw_br_dn, w_br_sb=dw_br_sb,
                w_br_mem=dw_br_mem, w_out=dw_out, final_g=d_final_g)


MESH = pl.DeviceIdType.MESH
ANY = pl.BlockSpec(memory_space=pl.ANY)


def _position():
    return lax.axis_index("x"), lax.axis_index("y"), lax.axis_index("c")


def _all_gather(xs, name):
    n = len(xs)

    def body(*refs):
        x_refs, o_refs = refs[:n], refs[n:2 * n]
        send_sems, recv_sems, local_sems = refs[2 * n:]
        x, y, c = _position()
        me, sibling = (x, y, c), (x, y, 1 - c)
        chips = [(1 - x, y), (x, 1 - y), (1 - x, 1 - y)]

        def slot(p):
            return 4 * p[0] + 2 * p[1] + p[2]

        def copy(a, k, block, to, src=None):
            dst = o_refs[a].at[slot(block)]
            return pltpu.make_async_remote_copy(
                src_ref=dst if src is None else src, dst_ref=dst, send_sem=send_sems.at[7 * a + k],
                recv_sem=recv_sems.at[7 * a + k], device_id=to, device_id_type=MESH)

        mine = [pltpu.make_async_copy(x_refs[a], o_refs[a].at[slot(me)], local_sems.at[a]) for a in range(n)]
        for cp in mine:
            cp.start()
        first = []
        for a in range(n):
            first.append(copy(a, 0, me, sibling, src=x_refs[a]))
            first += [copy(a, 1 + j, me, (*chip, c), src=x_refs[a]) for j, chip in enumerate(chips)]
        for cp in first:
            cp.start()
        passed = []
        for j, chip in enumerate(chips):
            for a in range(n):
                copy(a, 1 + j, (*chip, c), me).wait_recv()
                cp = copy(a, 4 + j, (*chip, c), sibling)
                cp.start()
                passed.append(cp)
        for a in range(n):
            copy(a, 0, sibling, me).wait_recv()
            for j, chip in enumerate(chips):
                copy(a, 4 + j, (*chip, 1 - c), me).wait_recv()
        for cp in first + passed:
            cp.wait_send()
        for cp in mine:
            cp.wait()

    return pl.pallas_call(
        body,
        name=name,
        in_specs=[ANY] * n,
        out_specs=[ANY] * n,
        out_shape=[jax.ShapeDtypeStruct((N_DEV, *v.shape), v.dtype) for v in xs],
        scratch_shapes=[pltpu.SemaphoreType.DMA((7 * n,)), pltpu.SemaphoreType.DMA((7 * n,)),
                        pltpu.SemaphoreType.DMA((n,))],
    )(*xs)


def _window_view(ref, dest):
    return ref.at[pl.ds(LANE * WIN_START[dest], WIN_W), :]


def _chunk_rows(rows, cols):
    best = max(ch for ch in range(16, rows + 1, 16) if rows % ch == 0 and ch * cols <= (1 << 17))
    return best


def _halving_stage(xs, axis, name, out_dtype, windowed=()):
    n_arr = len(xs)
    metas = []
    for k, v in enumerate(xs):
        if k in windowed:
            metas.append((N_DEV // 2, WIN_W, v.shape[1]))
        else:
            assert v.shape[1] == 2
            metas.append((v.shape[0], v.shape[2], v.shape[3]))
    chunk = [_chunk_rows(r, c) for (_, r, c) in metas]
    offs = [sum(m[0] for m in metas[:k]) for k in range(n_arr)]
    n_sem = sum(m[0] for m in metas)

    def body(*refs):
        x_refs = refs[:n_arr]
        o_refs = refs[n_arr:2 * n_arr]
        land_refs = refs[2 * n_arr:3 * n_arr]
        rest = refs[3 * n_arr:]
        bufs = rest[:3 * n_arr]
        send_sems, recv_sems, in_sems, out_sems = rest[3 * n_arr:]
        pos = dict(zip("xyc", _position()))
        bit = pos[axis]
        peer = tuple(1 - pos[a] if a == axis else pos[a] for a in "xyc")

        def view(k, i, b):
            if k in windowed:
                return _window_view(x_refs[k], 2 * i + b)
            return x_refs[k].at[i, b]

        def add_blocks(k, a_view, b_view, o_view):
            _, rows, _ = metas[k]
            ch = chunk[k]
            nch = rows // ch
            va, vb, vo = bufs[3 * k:3 * k + 3]

            def rows_of(j):
                return pl.ds(pl.multiple_of(j * ch, 8), ch)

            def loads(j, s):
                return (pltpu.make_async_copy(a_view.at[rows_of(j), :], va.at[s], in_sems.at[0, s]),
                        pltpu.make_async_copy(b_view.at[rows_of(j), :], vb.at[s], in_sems.at[1, s]))

            def store(j, s):
                return pltpu.make_async_copy(vo.at[s], o_view.at[rows_of(j), :], out_sems.at[s])

            for cp in loads(0, 0):
                cp.start()

            def step(j, _):
                s = lax.rem(j, 2)

                @pl.when(j + 1 < nch)
                def _():
                    for cp in loads(j + 1, 1 - s):
                        cp.start()

                for cp in loads(j, s):
                    cp.wait()

                @pl.when(j >= 2)
                def _():
                    store(j - 2, s).wait()

                vo[s] = (va[s] + vb[s]).astype(vo.dtype)
                store(j, s).start()
                return 0

            lax.fori_loop(0, nch, step, 0)
            for j in range(max(0, nch - 2), nch):
                store(j, j % 2).wait()

        for b in (0, 1):
            @pl.when(bit == b)
            def _(b=b):
                sends = []
                for k in range(n_arr):
                    for i in range(metas[k][0]):
                        cp = pltpu.make_async_remote_copy(
                            src_ref=view(k, i, 1 - b), dst_ref=land_refs[k].at[i], send_sem=send_sems.at[offs[k] + i],
                            recv_sem=recv_sems.at[offs[k] + i], device_id=peer, device_id_type=MESH)
                        cp.start()
                        sends.append(cp)
                idx = 0
                for k in range(n_arr):
                    for i in range(metas[k][0]):
                        sends[idx].wait_recv()
                        add_blocks(k, view(k, i, b), land_refs[k].at[i], o_refs[k].at[i])
                        idx += 1
                for cp in sends:
                    cp.wait_send()

    out_shape = [jax.ShapeDtypeStruct(m, out_dtype) for m in metas]
    land_shape = [jax.ShapeDtypeStruct(m, F32) for m in metas]
    scratch = []
    for k in range(n_arr):
        scratch += [pltpu.VMEM((2, chunk[k], metas[k][2]), F32)] * 2 + [pltpu.VMEM((2, chunk[k], metas[k][2]), out_dtype)]
    scratch += [pltpu.SemaphoreType.DMA((n_sem,)), pltpu.SemaphoreType.DMA((n_sem,)),
                pltpu.SemaphoreType.DMA((2, 2)), pltpu.SemaphoreType.DMA((2,))]
    outs = pl.pallas_call(
        body,
        name=name,
        in_specs=[ANY] * n_arr,
        out_specs=[ANY] * (2 * n_arr),
        out_shape=out_shape + land_shape,
        scratch_shapes=scratch,
    )(*xs)
    return outs[:n_arr]


def _hbm_add(a_view, b_view, o_view, bufs, in_sems, out_sems, ch):
    rows = a_view.shape[0]
    nch = rows // ch
    va, vb, vo = bufs

    def rows_of(j):
        return pl.ds(pl.multiple_of(j * ch, 16), ch)

    def loads(j, s):
        return (pltpu.make_async_copy(a_view.at[rows_of(j), :], va.at[s], in_sems.at[0, s]),
                pltpu.make_async_copy(b_view.at[rows_of(j), :], vb.at[s], in_sems.at[1, s]))

    def store(j, s):
        return pltpu.make_async_copy(vo.at[s], o_view.at[rows_of(j), :], out_sems.at[s])

    for cp in loads(0, 0):
        cp.start()

    def step(j, _):
        s = lax.rem(j, 2)

        @pl.when(j + 1 < nch)
        def _():
            for cp in loads(j + 1, 1 - s):
                cp.start()

        for cp in loads(j, s):
            cp.wait()

        @pl.when(j >= 2)
        def _():
            store(j - 2, s).wait()

        vo[s] = (va[s].astype(F32) + vb[s].astype(F32)).astype(vo.dtype)
        store(j, s).start()
        return 0

    lax.fori_loop(0, nch, step, 0)
    for j in range(max(0, nch - 2), nch):
        store(j, j % 2).wait()


def _xy_stage(xs, first, name):
    n_arr = len(xs)
    if first:
        shapes = [(v.shape[2] // 2, v.shape[3]) for v in xs]
        ins = list(xs)
    else:
        shapes = [(a.shape[1], a.shape[2]) for a, _ in xs]
        ins = [v for pair in xs for v in pair]
    n_blk = 2 if first else 1
    out_dtype = BF16 if first else F32
    chunk = [_chunk_rows(r, c) for (r, c) in shapes]
    n_sem = 2 * n_blk * n_arr

    def body(*refs):
        n_in = len(ins)
        in_refs = refs[:n_in]
        n_out = 2 * n_arr if first else n_arr
        o_refs = refs[n_in:n_in + n_out]
        land = refs[n_in + n_out:n_in + n_out + 2 * n_arr]
        rest = refs[n_in + n_out + 2 * n_arr:]
        bufs = rest[:3 * n_arr]
        send_sems, recv_sems, in_sems, out_sems = rest[3 * n_arr:]
        x, y, c = _position()
        peers = {"x": (1 - x, y, c), "y": (x, 1 - y, c)}
        jobs = []
        for k in range(n_arr):
            r, _ = shapes[k]
            half_a, half_b = pl.ds(0, r), pl.ds(r, r)
            if first:
                src = in_refs[k]
                for i in range(2):
                    jobs.append((k, src.at[i, 1 - y, half_a, :], src.at[i, y, half_a, :], land[2 * k].at[i],
                                 o_refs[2 * k].at[i], "y"))
                    jobs.append((k, src.at[1 - x, i, half_b, :], src.at[x, i, half_b, :], land[2 * k + 1].at[i],
                                 o_refs[2 * k + 1].at[i], "x"))
            else:
                a1, b1 = in_refs[2 * k], in_refs[2 * k + 1]
                jobs.append((k, a1.at[1 - x], a1.at[x], land[2 * k], o_refs[k].at[half_a, :], "x"))
                jobs.append((k, b1.at[1 - y], b1.at[y], land[2 * k + 1], o_refs[k].at[half_b, :], "y"))
        sends = []
        for n, (k, send, _, landing, _, axis) in enumerate(jobs):
            cp = pltpu.make_async_remote_copy(src_ref=send, dst_ref=landing, send_sem=send_sems.at[n],
                                              recv_sem=recv_sems.at[n], device_id=peers[axis], device_id_type=MESH)
            cp.start()
            sends.append(cp)
        for cp, (k, _, kept, landing, out, _) in zip(sends, jobs):
            cp.wait_recv()
            _hbm_add(kept, landing, out, bufs[3 * k:3 * k + 3], in_sems, out_sems, chunk[k])
        for cp in sends:
            cp.wait_send()

    if first:
        out_shape = [jax.ShapeDtypeStruct((2, r, c), BF16) for (r, c) in shapes for _ in range(2)]
        land_shape = out_shape
    else:
        out_shape = [jax.ShapeDtypeStruct((2 * r, c), F32) for (r, c) in shapes]
        land_shape = [jax.ShapeDtypeStruct((r, c), BF16) for (r, c) in shapes for _ in range(2)]
    scratch = []
    for k in range(n_arr):
        scratch += [pltpu.VMEM((2, chunk[k], shapes[k][1]), BF16)] * 2 + [pltpu.VMEM((2, chunk[k], shapes[k][1]), out_dtype)]
    scratch += [pltpu.SemaphoreType.DMA((n_sem,)), pltpu.SemaphoreType.DMA((n_sem,)),
                pltpu.SemaphoreType.DMA((2, 2)), pltpu.SemaphoreType.DMA((2,))]
    outs = pl.pallas_call(
        body,
        name=name,
        in_specs=[ANY] * len(ins),
        out_specs=[ANY] * (len(out_shape) + len(land_shape)),
        out_shape=out_shape + land_shape,
        scratch_shapes=scratch,
    )(*ins)
    outs = outs[:len(out_shape)]
    return [(outs[2 * k], outs[2 * k + 1]) for k in range(n_arr)] if first else list(outs)


def _reduce_scatter(dw_al, blocks):
    xs = [dw_al] + [b.reshape(N_DEV // 2, 2, *b.shape[1:]) for b in blocks]
    ys = _halving_stage(xs, "c", "rs_c", BF16, windowed=(0,))
    pairs = _xy_stage([v.reshape(2, 2, *v.shape[1:]) for v in ys], True, "rs_xy1")
    return _xy_stage(pairs, False, "rs_xy2")


def _sum_slots(gs):
    n = len(gs)

    def body(*refs):
        for g_ref, o_ref in zip(refs[:n], refs[n:]):
            acc = g_ref[0]
            for d in range(1, N_DEV):
                acc = acc + g_ref[d]
            o_ref[...] = acc

    return pl.pallas_call(body, name="sum_slots",
                          out_shape=[jax.ShapeDtypeStruct(g.shape[1:], g.dtype) for g in gs])(*gs)


def _assemble_w_al(wins, bas):
    ba_tile = O_BA // LANE
    assert W_AL // LANE == ba_tile + 1
    cols = wins.shape[2]
    n_buf = 3
    ends = [WIN_START[d + 1] if d + 1 < N_DEV else ba_tile for d in range(N_DEV)]

    def body(w_ref, ba_ref, o_ref, buf, ld_sems, st_sems, ba_sem):
        def load(d):
            return pltpu.make_async_copy(w_ref.at[d], buf.at[d % n_buf], ld_sems.at[d % n_buf])

        def store(d):
            n = LANE * (ends[d] - WIN_START[d])
            return pltpu.make_async_copy(buf.at[d % n_buf, pl.ds(0, n), :],
                                         o_ref.at[pl.ds(LANE * WIN_START[d], n), :], st_sems.at[d % n_buf])

        ba_copy = pltpu.make_async_copy(ba_ref.at[BA_DEV], o_ref.at[pl.ds(LANE * ba_tile, LANE), :], ba_sem)
        ba_copy.start()
        load(0).start()
        for d in range(N_DEV):
            if d + 1 < N_DEV:
                if d + 1 >= n_buf:
                    store(d + 1 - n_buf).wait()
                load(d + 1).start()
            load(d).wait()
            if d > 0:
                ov = LANE * (WIN_START[d - 1] + WIN_TILES - WIN_START[d])
                buf[d % n_buf, :ov, :] = buf[d % n_buf, :ov, :] + buf[(d - 1) % n_buf, WIN_W - ov:, :]
            store(d).start()
        for d in range(N_DEV - n_buf, N_DEV):
            store(d).wait()
        ba_copy.wait()

    return pl.pallas_call(
        body,
        name="assemble_w_al",
        in_specs=[ANY, ANY],
        out_specs=ANY,
        out_shape=jax.ShapeDtypeStruct((W_AL, cols), wins.dtype),
        scratch_shapes=[pltpu.VMEM((n_buf, WIN_W, cols), wins.dtype), pltpu.SemaphoreType.DMA((n_buf,)),
                        pltpu.SemaphoreType.DMA((n_buf,)), pltpu.SemaphoreType.DMA],
    )(wins, bas)


def _adamw_math(w, g, m, v):
    m_new = ADAM_B1 * m + (1.0 - ADAM_B1) * g
    v_new = ADAM_B2 * v + (1.0 - ADAM_B2) * (g * g)
    m_hat = m_new / (1.0 - ADAM_B1 ** ADAM_STEP)
    v_hat = v_new / (1.0 - ADAM_B2 ** ADAM_STEP)
    return -ADAM_LR * (m_hat / (jnp.sqrt(v_hat) + ADAM_EPS) + ADAM_WD * w), m_new, v_new


def _adamw(w, g, m, v, name, tb=134):
    r, _, c = w.shape
    assert r % tb == 0

    def body(w_ref, g_ref, m_ref, v_ref, d_ref, nm_ref, nv_ref):
        d_ref[...], nm_ref[...], nv_ref[...] = _adamw_math(w_ref[...], g_ref[...], m_ref[...], v_ref[...])

    blk = pl.BlockSpec((tb, 1, c), lambda i: (i, 0, 0))
    o = jax.ShapeDtypeStruct(w.shape, F32)
    return pl.pallas_call(body, name=name, grid=(r // tb,), in_specs=[blk] * 4, out_specs=[blk] * 3,
                          out_shape=[o, o, o])(w, g, m, v)


def _adamw_many(ws, gs, ms, vs, name):
    n = len(ws)

    def body(*refs):
        for k in range(n):
            w_ref, g_ref, m_ref, v_ref = (refs[j * n + k] for j in range(4))
            d_ref, nm_ref, nv_ref = (refs[(4 + j) * n + k] for j in range(3))
            d_ref[...], nm_ref[...], nv_ref[...] = _adamw_math(w_ref[...], g_ref[...], m_ref[...], v_ref[...])

    shapes = [jax.ShapeDtypeStruct(w.shape, F32) for w in ws]
    outs = pl.pallas_call(body, name=name, out_shape=shapes * 3)(*ws, *gs, *ms, *vs)
    return outs[:n], outs[n:2 * n], outs[2 * n:]


def _select(me, table):
    return sum(jnp.where(me == d, jnp.int32(v), jnp.int32(0)) for d, v in enumerate(table))


WIN_SHIFT = tuple(SHARD_W * d - LANE * WIN_START[d] for d in range(N_DEV))
PAD_L = 256
PAD_R = 256


def _shard_to_window(shard_t, me):
    shift = _select(me, WIN_SHIFT)
    start = _select(me, WIN_START)
    padded = jnp.pad(shard_t, ((PAD_L, PAD_R), (0, 0)))
    cols = shard_t.shape[1]
    lo = lax.dynamic_slice(padded, (PAD_L - shift, 0), (WIN_W, cols))
    hi = lax.dynamic_slice(padded, (PAD_L - shift + N_BA, 0), (WIN_W, cols))
    aligned = LANE * start + lax.broadcasted_iota(jnp.int32, (WIN_W, 1), 0)
    return jnp.where(aligned >= ORIG_BA, hi, lo)


def _window_to_shard(win, ba_grad, me):
    shift = _select(me, WIN_SHIFT)
    cols = win.shape[1]
    padded = jnp.pad(win, ((N_BA, PAD_R), (0, 0)))
    lo = lax.dynamic_slice(padded, (N_BA + shift, 0), (SHARD_W, cols))
    hi = lax.dynamic_slice(padded, (shift, 0), (SHARD_W, cols))
    orig = SHARD_W * me + lax.broadcasted_iota(jnp.int32, (SHARD_W, 1), 0)
    ba_full = lax.dynamic_update_slice(jnp.zeros((SHARD_W, cols), win.dtype), ba_grad, (BA_LOCAL, 0))
    return jnp.where(orig < ORIG_BA, lo, jnp.where(orig >= ORIG_BA + N_BA, hi, ba_full))


def _pad_row(v, width=D_MODEL):
    v = v.reshape(1, -1)
    return jnp.pad(v, ((0, 0), (0, width - v.shape[1])))


def _slab(v, rows=8):
    return jnp.pad(v, ((0, rows - v.shape[0]), (0, D_MODEL - v.shape[1])))


def kernel(x, mem, norm_g, mem_norm_g, w_in, conv_w, a_log, dt_bias, dn_norm_g, w_mem_kv, w_br_dn, w_br_sb, w_br_mem, w_out, final_g, loss_target, m_norm_g, m_mem_norm_g, m_w_in, m_conv_w, m_a_log, m_dt_bias, m_dn_norm_g, m_w_mem_kv, m_w_br_dn, m_w_br_sb, m_w_br_mem, m_w_out, m_final_g, v_norm_g, v_mem_norm_g, v_w_in, v_conv_w, v_a_log, v_dt_bias, v_dn_norm_g, v_w_mem_kv, v_w_br_dn, v_w_br_sb, v_w_br_mem, v_w_out, v_final_g):
    xi, yi, ci = _position()
    me = 4 * xi + 2 * yi + ci

    shard_t = w_in[0].T
    win = _shard_to_window(shard_t, me).astype(BF16)
    ba = jnp.pad(shard_t[BA_LOCAL:BA_LOCAL + N_BA, :], ((0, LANE - N_BA), (0, 0))).astype(BF16)
    g_win, g_ba, g_kv, g_dn, g_sb, g_out, g_mem, g_conv = _all_gather(
        [win, ba, w_mem_kv[0].astype(BF16), w_br_dn[0].astype(BF16), w_br_sb[0].astype(BF16), w_out[0].astype(BF16),
         w_br_mem[0].astype(BF16), conv_w[0]], "gather_weights")
    w_alt = _assemble_w_al(g_win, g_ba)
    w_mem_kv_f = g_kv.reshape(D_MODEL, 2 * MEM_W)
    w_br_dn_f = g_dn.reshape(D_MODEL, D_MODEL)
    w_br_sb_f = g_sb.reshape(D_MODEL, D_MODEL)
    w_out_f = g_out.reshape(D_MODEL, D_MODEL)
    w_br_mem_f = g_mem.transpose(1, 0, 2).reshape(MEM_W, D_MODEL)
    conv_w_f = g_conv.transpose(1, 0, 2).reshape(CONV_K, 3 * D_MODEL)

    r = _local_step(x[0], mem[0], loss_target[0], norm_g, mem_norm_g, w_alt, conv_w_f, _pad_row(a_log, LANE),
                    _pad_row(dt_bias, LANE), dn_norm_g, w_mem_kv_f, w_br_dn_f, w_br_sb_f, w_br_mem_f, w_out_f,
                    final_g.reshape(1, D_MODEL))

    dw_alt = r["w_alt"]
    g_win, g_kv, g_dn, g_sb, g_out, g_mem = _reduce_scatter(dw_alt, [
        r["w_mem_kv"].reshape(N_DEV, D_MODEL // N_DEV, 2 * MEM_W),
        r["w_br_dn"].reshape(N_DEV, D_MODEL // N_DEV, D_MODEL),
        r["w_br_sb"].reshape(N_DEV, D_MODEL // N_DEV, D_MODEL),
        r["w_out"].reshape(N_DEV, D_MODEL // N_DEV, D_MODEL),
        r["w_br_mem"].reshape(MEM_W, N_DEV, D_MODEL // N_DEV).transpose(1, 0, 2)])
    parts = [r["norm_g"], r["mem_norm_g"], r["final_g"], r["dn_norm_g"], r["scal"], r["loss"], r["conv_w"],
             dw_alt[O_BA:O_BA + LANE, :]]
    s_norm_g, s_mem_norm_g, s_final_g, s_dn_norm_g, s_scal, s_loss, s_conv, s_ba = _sum_slots(
        _all_gather(parts, "gather_small"))
    loss = s_loss[0, 0]
    cw = conv_w.shape[2]
    g_conv = lax.dynamic_slice(s_conv, (0, cw * me), (CONV_K, cw))
    g_w_in_t = _window_to_shard(g_win, s_ba[:N_BA, :], me)
    grads = dict(norm_g=s_norm_g, mem_norm_g=s_mem_norm_g, w_in=g_w_in_t.T[None], conv_w=g_conv[None],
                 a_log=s_scal[0:1, :N_HEADS], dt_bias=s_scal[1:2, :N_HEADS], dn_norm_g=s_dn_norm_g, w_mem_kv=g_kv[None],
                 w_br_dn=g_dn[None], w_br_sb=g_sb[None], w_br_mem=g_mem[None], w_out=g_out[None],
                 final_g=s_final_g.reshape(D_MODEL))

    params = dict(norm_g=(norm_g, m_norm_g, v_norm_g), mem_norm_g=(mem_norm_g, m_mem_norm_g, v_mem_norm_g),
                  w_in=(w_in, m_w_in, v_w_in), conv_w=(conv_w, m_conv_w, v_conv_w), a_log=(a_log, m_a_log, v_a_log),
                  dt_bias=(dt_bias, m_dt_bias, v_dt_bias), dn_norm_g=(dn_norm_g, m_dn_norm_g, v_dn_norm_g),
                  w_mem_kv=(w_mem_kv, m_w_mem_kv, v_w_mem_kv), w_br_dn=(w_br_dn, m_w_br_dn, v_w_br_dn),
                  w_br_sb=(w_br_sb, m_w_br_sb, v_w_br_sb), w_br_mem=(w_br_mem, m_w_br_mem, v_w_br_mem),
                  w_out=(w_out, m_w_out, v_w_out), final_g=(final_g, m_final_g, v_final_g))
    order = list(params)
    deltas, new_m, new_v = {}, {}, {}
    deltas["w_in"], new_m["w_in"], new_v["w_in"] = (jnp.transpose(o, (1, 2, 0)) for o in _adamw(
        jnp.transpose(w_in, (2, 0, 1)), g_w_in_t[:, None, :], jnp.transpose(m_w_in, (2, 0, 1)),
        jnp.transpose(v_w_in, (2, 0, 1)), "adamw_w_in"))
    rest = [nm for nm in order if nm != "w_in"]

    def two_d(a):
        return a.reshape(1, -1) if a.ndim == 1 else a

    d_l, m_l, v_l = _adamw_many([two_d(params[nm][0]) for nm in rest], [two_d(grads[nm]) for nm in rest],
                                [two_d(params[nm][1]) for nm in rest], [two_d(params[nm][2]) for nm in rest], "adamw_rest")
    for k, nm in enumerate(rest):
        shp = params[nm][0].shape
        deltas[nm], new_m[nm], new_v[nm] = d_l[k].reshape(shp), m_l[k].reshape(shp), v_l[k].reshape(shp)
    return (loss, r["grad_x"][None], *[grads[nm] for nm in order], *[deltas[nm] for nm in order],
            *[new_m[nm] for nm in order], *[new_v[nm] for nm in order])
```

```python
import functools
import math

import jax
import jax.numpy as jnp
from jax import lax
from jax.experimental import pallas as pl
from jax.experimental.pallas import tpu as pltpu

F32 = jnp.float32
BF16 = jnp.bfloat16

D_MODEL = 1024
N_DEV = 8
N_HEADS = 8
D_HEAD = 128
DN_CHUNK = 64
CONV_K = 4
MEM_LEN = 256
MEM_HEADS = 4
MEM_DH = 64
MEM_W = MEM_HEADS * MEM_DH
NORM_EPS = 1e-6
IN_WIDTH = 11792
SHARD_W = IN_WIDTH // N_DEV

LANE = 128
SUPER = 2 * DN_CHUNK

O_QKV_DN = 0
O_Z_DN = 3072
O_QKV_SB = 4096
O_Z_SB = 7168
O_MQ = 8192
O_MZ = 8448
O_GATES = 8704
O_BA = 11776
W_AL = 11904
ORIG_BA = 4096
N_BA = 16

WIN_TILES = 13
WIN_W = WIN_TILES * LANE


def _aligned_col(o):
    return o if o < ORIG_BA else o - N_BA


WIN_START = tuple(min(_aligned_col(SHARD_W * d) // LANE, (W_AL // LANE) - WIN_TILES) for d in range(N_DEV))
WIN_OFF = tuple(_aligned_col(SHARD_W * d) - LANE * WIN_START[d] if SHARD_W * d >= ORIG_BA + N_BA or SHARD_W * d < ORIG_BA
                else None for d in range(N_DEV))
BA_DEV = ORIG_BA // SHARD_W
BA_LOCAL = ORIG_BA - BA_DEV * SHARD_W

ADAM_LR = 0.001
ADAM_B1 = 0.9
ADAM_B2 = 0.999
ADAM_EPS = 1e-08
ADAM_WD = 0.01
ADAM_STEP = 10

NN = (((1,), (0,)), ((), ()))
NT = (((1,), (1,)), ((), ()))
TN = (((0,), (0,)), ((), ()))


def _dot(a, b, dims):
    return lax.dot_general(a.astype(BF16), b.astype(BF16), dims, preferred_element_type=F32)


def _split2(a):
    hi = a.astype(BF16)
    lo = (a - hi.astype(F32)).astype(BF16)
    return hi, lo


def _dot3(a, b, dims):
    ah, al = _split2(a)
    bh, bl = _split2(b)
    d = functools.partial(lax.dot_general, dimension_numbers=dims, preferred_element_type=F32)
    return d(ah, bh) + (d(ah, bl) + d(al, bh))


def _sel_dot_impl(sel01, x, dims):
    sel = sel01.astype(BF16)
    h1 = x.astype(BF16)
    r1 = x - h1.astype(F32)
    h2 = r1.astype(BF16)
    h3 = (r1 - h2.astype(F32)).astype(BF16)
    d = functools.partial(lax.dot_general, dimension_numbers=dims, preferred_element_type=F32)
    return d(sel, h1) + (d(sel, h2) + d(sel, h3))


@jax.custom_vjp
def _sel_dot(sel01, x):
    return _sel_dot_impl(sel01, x, NN)


_sel_dot.defvjp(lambda s, x: (_sel_dot(s, x), s),
                lambda s, g: (jnp.zeros_like(s), _sel_dot_impl(s, g, TN)))


def _make_mm(dotfn):
    @jax.custom_vjp
    def nn(a, b):
        return dotfn(a, b, NN)

    @jax.custom_vjp
    def nt(a, b):
        return dotfn(a, b, NT)

    @jax.custom_vjp
    def tn(a, b):
        return dotfn(a, b, TN)

    nn.defvjp(lambda a, b: (nn(a, b), (a, b)), lambda r, g: (nt(g, r[1]), tn(r[0], g)))
    nt.defvjp(lambda a, b: (nt(a, b), (a, b)), lambda r, g: (nn(g, r[1]), tn(g, r[0])))
    tn.defvjp(lambda a, b: (tn(a, b), (a, b)), lambda r, g: (nt(r[1], g), nn(r[0], g)))
    return nn, nt, tn


mm_nn, mm_nt, mm_tn = _make_mm(_dot)
mm3_nn, mm3_nt, mm3_tn = _make_mm(_dot3)


def _sigmoid(x):
    return jax.nn.sigmoid(x)


def _silu(x):
    return x * _sigmoid(x)


def _softplus_parts(x):
    sp = jnp.log1p(jnp.exp(-jnp.abs(x)))
    return jnp.maximum(x, 0.0) + sp, jnp.maximum(-x, 0.0) + sp


def _rmsnorm(x, g):
    return x * lax.rsqrt(jnp.mean(x * x, axis=-1, keepdims=True) + NORM_EPS) * g


def _iota2(shape, dim):
    return lax.broadcasted_iota(jnp.int32, shape, dim)


def _div64(i):
    return lax.shift_right_logical(i, jnp.full(i.shape, 6, jnp.int32))


def _each(f, *lists):
    return [f(*a) for a in zip(*lists)]


@jax.custom_vjp
def _inv_unit_lower(ms):
    n = ms[0].shape[0]
    eye = (_iota2((n, n), 0) == _iota2((n, n), 1)).astype(F32)
    rs = [eye - m for m in ms]
    ps = ms
    for _ in range(5):
        ps = _each(mm3_nn, ps, ps)
        rs = _each(lambda r, p: r + mm3_nn(r, p), rs, ps)
    return rs


def _inv_fwd(ms):
    rs = _inv_unit_lower(ms)
    return rs, rs


def _inv_bwd(rs, gs):
    ts = _each(mm3_tn, rs, gs)
    return (_each(lambda t, r: -mm3_nt(t, r), ts, rs),)


_inv_unit_lower.defvjp(_inv_fwd, _inv_bwd)


def _dn_block(cq, ck, cv, bcol, acol, zt, alog, dtb, gn, s0):
    n = SUPER
    h = DN_CHUNK
    row = _iota2((n, n), 0)
    col = _iota2((n, n), 1)
    same = _div64(row) == _div64(col)
    incl = jnp.logical_and(same, row >= col)
    strict = jnp.logical_and(same, row > col)
    incl_f = incl.astype(F32)

    qn = _each(lambda x: x * lax.rsqrt(jnp.sum(x * x, axis=-1, keepdims=True) + NORM_EPS) * (D_HEAD ** -0.5), cq)
    kn = _each(lambda x: x * lax.rsqrt(jnp.sum(x * x, axis=-1, keepdims=True) + NORM_EPS), ck)
    beta = _each(_sigmoid, bcol)
    g = _each(lambda al, ac, dt: -(jnp.exp(al) * _softplus_parts(ac + dt)[0]), alog, acol, dtb)
    gcum = _each(lambda x: _sel_dot(incl_f, jnp.broadcast_to(x, (n, n))), g)
    gam_incl = _each(lambda x: jnp.where(incl, jnp.exp(jnp.where(incl, x - x.T, 0.0)), 0.0), gcum)
    kk = _each(mm_nt, kn, kn)
    t_inv = _inv_unit_lower(_each(lambda b, x, gm: b * x * jnp.where(strict, gm, 0.0), beta, kk, gam_incl))
    eg = _each(jnp.exp, gcum)
    u = _each(lambda t, v, b: mm_nn(t, v * b), t_inv, cv, beta)
    w = _each(lambda t, k, b, e: mm_nn(t, k * (b * e)), t_inv, kn, beta, eg)
    a_intra = _each(lambda q, k, gm: mm_nt(q, k) * gm, qn, kn, gam_incl)
    q_dec = _each(lambda q, e: q * e, qn, eg)
    last0 = _each(lambda x: x[h - 1:h, :], gcum)
    last1 = _each(lambda x: x[n - 1:n, :], gcum)
    k_dec = _each(lambda k, x, l0, l1: k * jnp.exp(jnp.concatenate(
        [jnp.broadcast_to(l0, (h, n)), jnp.broadcast_to(l1, (h, n))], axis=0) - x), kn, gcum, last0, last1)
    v0 = _each(lambda uu, ww, s: uu[:h] - mm_nn(ww[:h], s), u, w, s0)
    o0 = _each(lambda q, s: mm_nn(q[:h], s), q_dec, s0)
    s1 = _each(lambda s, l0, k, v: s * jnp.exp(l0) + mm_tn(k[:h], v), s0, last0, k_dec, v0)
    v1 = _each(lambda uu, ww, s: uu[h:] - mm_nn(ww[h:], s), u, w, s1)
    o1 = _each(lambda q, s: mm_nn(q[h:], s), q_dec, s1)
    s2 = _each(lambda s, l1, k, v: s * jnp.exp(l1) + mm_tn(k[h:], v), s1, last1, k_dec, v1)
    o = _each(lambda a, b, am, x, y: jnp.concatenate([a, b], axis=0) + mm_nn(am, jnp.concatenate([x, y], axis=0)),
              o0, o1, a_intra, v0, v1)
    out = _each(lambda x, z: _rmsnorm(x, gn) * _silu(z), o, zt)
    return out, s2


def _mem_fn(mq, mz, mkv):
    mk = mkv[:, :MEM_W]
    mv = mkv[:, MEM_W:]
    lane = _iota2((1, MEM_W), 1)
    out = jnp.zeros(mq.shape, F32)
    for hd in range(MEM_HEADS):
        hm = (_div64(lane) == hd).astype(F32)
        s = mm_nt(mq * hm, mk) * (1.0 / math.sqrt(MEM_DH))
        s = s - jnp.max(s, axis=-1, keepdims=True)
        e = jnp.exp(s)
        p = e / jnp.sum(e, axis=-1, keepdims=True)
        out = out + mm_nn(p, mv) * hm
    return out * _silu(mz)


def _merge_fn(gd, gs, gm, yd, ys, ym):
    return _sigmoid(gd) * yd + _sigmoid(gs) * ys + _sigmoid(gm) * ym


def _loss_fn(x, mo, fg, tgt):
    y = _rmsnorm(x + mo, fg)
    err = y - tgt
    return 0.5 * jnp.sum(jnp.mean(err * err, axis=-1, keepdims=True), axis=0, keepdims=True)


def _matmul(a, b, mode, out_dtype, tm, tn, tk, name, b_col0=0, n_cols=None):
    if mode == "nn":
        m, kdim = a.shape
        n = b.shape[1] if n_cols is None else n_cols
    elif mode == "nt":
        m, kdim = a.shape
        n = b.shape[0]
    else:
        kdim, m = a.shape
        n = b.shape[1] if n_cols is None else n_cols
    tm, tn, tk = min(tm, m), min(tn, n), min(tk, kdim)
    assert m % tm == 0 and n % tn == 0 and kdim % tk == 0 and b_col0 % tn == 0
    nk = kdim // tk
    jb = b_col0 // tn
    dims = {"nn": NN, "nt": NT, "tn": TN}[mode]

    def body(a_ref, b_ref, o_ref, acc_ref):
        k = pl.program_id(2)
        part = _dot(a_ref[...], b_ref[...], dims)

        @pl.when(k == 0)
        def _():
            acc_ref[...] = part

        @pl.when(k > 0)
        def _():
            acc_ref[...] += part

        @pl.when(k == nk - 1)
        def _():
            o_ref[...] = acc_ref[...].astype(o_ref.dtype)

    if mode == "nn":
        a_spec = pl.BlockSpec((tm, tk), lambda i, j, k: (i, k))
        b_spec = pl.BlockSpec((tk, tn), lambda i, j, k: (k, j + jb))
    elif mode == "nt":
        a_spec = pl.BlockSpec((tm, tk), lambda i, j, k: (i, k))
        b_spec = pl.BlockSpec((tn, tk), lambda i, j, k: (j, k))
    else:
        a_spec = pl.BlockSpec((tk, tm), lambda i, j, k: (k, i))
        b_spec = pl.BlockSpec((tk, tn), lambda i, j, k: (k, j + jb))
    return pl.pallas_call(
        body,
        name=name,
        grid=(m // tm, n // tn, nk),
        in_specs=[a_spec, b_spec],
        out_specs=pl.BlockSpec((tm, tn), lambda i, j, k: (i, j)),
        out_shape=jax.ShapeDtypeStruct((m, n), out_dtype),
        scratch_shapes=[pltpu.VMEM((tm, tn), F32)],
        compiler_params=pltpu.CompilerParams(dimension_semantics=("parallel", "parallel", "arbitrary")),
    )(a, b)


def _norm_in(x, g, tm=256):
    t = x.shape[0]

    def body(x_ref, g_ref, h_ref):
        h_ref[...] = _rmsnorm(x_ref[...], g_ref[...]).astype(BF16)

    return pl.pallas_call(
        body,
        name="norm_in",
        grid=(t // tm,),
        in_specs=[pl.BlockSpec((tm, D_MODEL), lambda i: (i, 0)), pl.BlockSpec((1, D_MODEL), lambda i: (0, 0))],
        out_specs=pl.BlockSpec((tm, D_MODEL), lambda i: (i, 0)),
        out_shape=jax.ShapeDtypeStruct((t, D_MODEL), BF16),
    )(x, g)


def _norm_in_bwd(x, g, dh, dres, tm=256):
    t = x.shape[0]

    def body(x_ref, g_ref, dh_ref, dres_ref, dx_ref, dg_ref):
        _, vjp = jax.vjp(_rmsnorm, x_ref[...], g_ref[...])
        dx, dg = vjp(dh_ref[...])
        dx_ref[...] = dx + dres_ref[...]

        @pl.when(pl.program_id(0) == 0)
        def _():
            dg_ref[...] = jnp.zeros_like(dg_ref)

        dg_ref[...] += dg

    row = pl.BlockSpec((tm, D_MODEL), lambda i: (i, 0))
    vec = pl.BlockSpec((1, D_MODEL), lambda i: (0, 0))
    return pl.pallas_call(
        body,
        name="norm_in_bwd",
        grid=(t // tm,),
        in_specs=[row, vec, row, row],
        out_specs=[row, vec],
        out_shape=[jax.ShapeDtypeStruct((t, D_MODEL), F32), jax.ShapeDtypeStruct((1, D_MODEL), F32)],
    )(x, g, dh, dres)


def _merge(proj, yd, ys, ym, tm=256, tc=512):
    t = proj.shape[0]
    g0 = O_GATES // tc
    gstep = D_MODEL // tc

    def body(gd, gs, gm, yd_ref, ys_ref, ym_ref, o_ref):
        o_ref[...] = _merge_fn(gd[...], gs[...], gm[...], yd_ref[...], ys_ref[...], ym_ref[...]).astype(BF16)

    def gate(k):
        return pl.BlockSpec((tm, tc), lambda i, j: (i, g0 + k * gstep + j))

    blk = pl.BlockSpec((tm, tc), lambda i, j: (i, j))
    return pl.pallas_call(
        body,
        name="merge",
        grid=(t // tm, D_MODEL // tc),
        in_specs=[gate(0), gate(1), gate(2), blk, blk, blk],
        out_specs=blk,
        out_shape=jax.ShapeDtypeStruct((t, D_MODEL), BF16),
    )(proj, proj, proj, yd, ys, ym)


def _merge_bwd(proj, yd, ys, ym, dmerged, tm=256, tc=512):
    t = proj.shape[0]
    g0 = O_GATES // tc
    gstep = D_MODEL // tc

    def body(gd, gs, gm, yd_ref, ys_ref, ym_ref, dm_ref, dyd, dys, dym, dgd, dgs, dgm):
        _, vjp = jax.vjp(_merge_fn, gd[...], gs[...], gm[...], yd_ref[...], ys_ref[...], ym_ref[...])
        outs = vjp(dm_ref[...])
        for ref, val in zip((dgd, dgs, dgm, dyd, dys, dym), outs):
            ref[...] = val.astype(BF16)

    def gate(k):
        return pl.BlockSpec((tm, tc), lambda i, j: (i, g0 + k * gstep + j))

    blk = pl.BlockSpec((tm, tc), lambda i, j: (i, j))
    o = jax.ShapeDtypeStruct((t, D_MODEL), BF16)
    return pl.pallas_call(
        body,
        name="merge_bwd",
        grid=(t // tm, D_MODEL // tc),
        in_specs=[gate(0), gate(1), gate(2), blk, blk, blk, blk],
        out_specs=[blk] * 6,
        out_shape=[o] * 6,
    )(proj, proj, proj, yd, ys, ym, dmerged)


def _loss_head(x, mo, fg, tgt, tm=256):
    t = x.shape[0]

    def body(x_ref, mo_ref, fg_ref, t_ref, loss_ref, dout_ref, dfg_ref):
        loss, vjp = jax.vjp(_loss_fn, x_ref[...], mo_ref[...], fg_ref[...], t_ref[...])
        _, dmo, dfg, _ = vjp(jnp.ones((1, 1), F32))

        @pl.when(pl.program_id(0) == 0)
        def _():
            loss_ref[...] = jnp.zeros_like(loss_ref)
            dfg_ref[...] = jnp.zeros_like(dfg_ref)

        loss_ref[...] += jnp.broadcast_to(loss, loss_ref.shape)
        dfg_ref[...] += dfg
        dout_ref[...] = dmo

    row = pl.BlockSpec((tm, D_MODEL), lambda i: (i, 0))
    vec = pl.BlockSpec((1, D_MODEL), lambda i: (0, 0))
    return pl.pallas_call(
        body,
        name="loss_head",
        grid=(t // tm,),
        in_specs=[row, row, vec, row],
        out_specs=[pl.BlockSpec((1, LANE), lambda i: (0, 0)), row, vec],
        out_shape=[jax.ShapeDtypeStruct((1, LANE), F32), jax.ShapeDtypeStruct((t, D_MODEL), F32),
                   jax.ShapeDtypeStruct((1, D_MODEL), F32)],
    )(x, mo, fg, tgt)


def _shift_rows(x, s):
    t = x.shape[0]
    if s == 0:
        return x
    rolled = pltpu.roll(x, s % t, 0)
    row = _iota2(x.shape, 0)
    keep = row >= s if s > 0 else row < t + s
    return jnp.where(keep, rolled, 0.0)


def _conv_pre(x, w):
    return sum(_shift_rows(x, CONV_K - 1 - j) * w[j:j + 1, :] for j in range(CONV_K))


CONV_TC = 256


def _dn_conv(proj, conv_w):
    t = proj.shape[0]
    nb = 3 * D_MODEL // CONV_TC

    def body(x_ref, w_ref, c_ref):
        c_ref[...] = _silu(_conv_pre(x_ref[...], w_ref[...]))

    return pl.pallas_call(
        body,
        name="dn_conv",
        grid=(nb,),
        in_specs=[pl.BlockSpec((t, CONV_TC), lambda j: (0, j)), pl.BlockSpec((CONV_K, CONV_TC), lambda j: (0, j))],
        out_specs=pl.BlockSpec((t, CONV_TC), lambda j: (0, j)),
        out_shape=jax.ShapeDtypeStruct((t, 3 * D_MODEL), F32),
    )(proj, conv_w)


def _dn_conv_bwd(proj, conv_w, dc, part):
    t = proj.shape[0]
    nb = D_MODEL // CONV_TC
    b0 = part * nb

    def body(x_ref, w_ref, dc_ref, dx_ref, dw_ref):
        x = x_ref[...]
        w = w_ref[...]
        pre = _conv_pre(x, w)
        sg = _sigmoid(pre)
        dpre = dc_ref[...] * (sg * (1.0 + pre * (1.0 - sg)))
        dx = sum(_shift_rows(dpre, -(CONV_K - 1 - j)) * w[j:j + 1, :] for j in range(CONV_K))
        dx_ref[...] = dx.astype(BF16)
        dw_ref[...] = jnp.concatenate(
            [jnp.sum(dpre * _shift_rows(x, CONV_K - 1 - j), axis=0, keepdims=True) for j in range(CONV_K)], axis=0)

    blk = pl.BlockSpec((t, CONV_TC), lambda j: (0, j))
    return pl.pallas_call(
        body,
        name=f"dn_conv_bwd{part}",
        grid=(nb,),
        in_specs=[pl.BlockSpec((t, CONV_TC), lambda j: (0, b0 + j)),
                  pl.BlockSpec((CONV_K, CONV_TC), lambda j: (0, b0 + j)), blk],
        out_specs=[blk, pl.BlockSpec((CONV_K, CONV_TC), lambda j: (0, j))],
        out_shape=[jax.ShapeDtypeStruct((t, D_MODEL), BF16), jax.ShapeDtypeStruct((CONV_K, D_MODEL), F32)],
    )(proj, conv_w, dc)


def _ba_columns(ba, hd):
    lane = _iota2(ba.shape, 1)
    bcol = jnp.sum(jnp.where(lane == hd, ba, 0.0), axis=1, keepdims=True)
    acol = jnp.sum(jnp.where(lane == N_HEADS + hd, ba, 0.0), axis=1, keepdims=True)
    return bcol, acol


def _head_scalar(row, hd):
    lane = _iota2(row.shape, 1)
    return jnp.sum(jnp.where(lane == hd, row, 0.0), axis=1, keepdims=True)


DN_HP = 4


def _dn_inputs(cq, ck, cv, ba_ref, z_ref, alog_ref, dtb_ref, heads, lanes):
    ba = ba_ref[...]
    cols = [_ba_columns(ba, hd) for hd in heads]
    return ([cq[:, ln] for ln in lanes], [ck[:, ln] for ln in lanes], [cv[:, ln] for ln in lanes],
            [c[0] for c in cols], [c[1] for c in cols], [z_ref[:, ln] for ln in lanes],
            [_head_scalar(alog_ref[...], hd) for hd in heads], [_head_scalar(dtb_ref[...], hd) for hd in heads])


def _dn_specs(nblk, reverse):
    w = DN_HP * LANE
    nq = D_MODEL // w

    def row(i):
        return nblk - 1 - i if reverse else i

    def colblk(b0):
        return pl.BlockSpec((SUPER, w), lambda i, h: (row(i), b0 + h))

    ba = pl.BlockSpec((SUPER, LANE), lambda i, h: (row(i), O_BA // LANE))
    vec = pl.BlockSpec((1, LANE), lambda i, h: (0, 0))
    st = pl.BlockSpec((1, DN_HP, D_HEAD, D_HEAD), lambda i, h: (row(i), h, 0, 0))
    return colblk, nq, ba, vec, st


def _dn_fwd(c, proj, alog_row, dtb_row, gn):
    t = c.shape[0]
    nblk = t // SUPER
    colblk, nq, ba, vec, st = _dn_specs(nblk, False)

    def body(cq, ck, cv, ba_ref, z_ref, alog_ref, dtb_ref, gn_ref, o_ref, s_ref, state):
        @pl.when(jnp.logical_and(pl.program_id(0) == 0, pl.program_id(1) == 0))
        def _():
            state[...] = jnp.zeros_like(state)

        heads = [pl.program_id(1) * DN_HP + j for j in range(DN_HP)]
        lanes = [slice(j * LANE, (j + 1) * LANE) for j in range(DN_HP)]
        s0 = [state[hd] for hd in heads]
        outs, s2 = _dn_block(*_dn_inputs(cq, ck, cv, ba_ref, z_ref, alog_ref, dtb_ref, heads, lanes), gn_ref[...], s0)
        for j, (hd, ln) in enumerate(zip(heads, lanes)):
            s_ref[0, j] = s0[j]
            o_ref[:, ln] = outs[j].astype(BF16)
            state[hd] = s2[j]

    return pl.pallas_call(
        body,
        name="dn_fwd",
        grid=(nblk, N_HEADS // DN_HP),
        in_specs=[colblk(0), colblk(nq), colblk(2 * nq), ba, colblk(O_Z_DN // (DN_HP * LANE)), vec, vec, vec],
        out_specs=[colblk(0), st],
        out_shape=[jax.ShapeDtypeStruct((t, D_MODEL), BF16),
                   jax.ShapeDtypeStruct((nblk, N_HEADS, D_HEAD, D_HEAD), F32)],
        scratch_shapes=[pltpu.VMEM((N_HEADS, D_HEAD, D_HEAD), F32)],
    )(c, c, c, proj, proj, alog_row, dtb_row, gn)


def _dn_bwd(c, proj, alog_row, dtb_row, gn, states, do):
    t = c.shape[0]
    nblk = t // SUPER
    colblk, nq, ba, vec, st = _dn_specs(nblk, True)

    def body(cq, ck, cv, ba_ref, z_ref, alog_ref, dtb_ref, gn_ref, s_ref, do_ref,
             dq_ref, dk_ref, dv_ref, dz_ref, dba_ref, dsc_ref, dgn_ref, dstate):
        i = pl.program_id(0)
        hq = pl.program_id(1)

        @pl.when(jnp.logical_and(i == 0, hq == 0))
        def _():
            dstate[...] = jnp.zeros_like(dstate)
            dsc_ref[...] = jnp.zeros_like(dsc_ref)
            dgn_ref[...] = jnp.zeros_like(dgn_ref)

        @pl.when(hq == 0)
        def _():
            dba_ref[...] = jnp.zeros_like(dba_ref)

        lane = _iota2((SUPER, LANE), 1)
        lane1 = _iota2((1, LANE), 1)
        heads = [hq * DN_HP + j for j in range(DN_HP)]
        lanes = [slice(j * LANE, (j + 1) * LANE) for j in range(DN_HP)]
        ds_in = [dstate[hd] for hd in heads]
        s_in = [s_ref[0, j] for j in range(DN_HP)]
        _, vjp = jax.vjp(_dn_block, *_dn_inputs(cq, ck, cv, ba_ref, z_ref, alog_ref, dtb_ref, heads, lanes),
                         gn_ref[...], s_in)
        dq, dk, dv, dbc, dac, dz, dal, ddt, dgn, ds0 = vjp(([do_ref[:, ln].astype(F32) for ln in lanes], ds_in))
        dba = jnp.zeros((SUPER, LANE), F32)
        dal_row = jnp.zeros((1, LANE), F32)
        ddt_row = jnp.zeros((1, LANE), F32)
        for j, (hd, ln) in enumerate(zip(heads, lanes)):
            dq_ref[:, ln] = dq[j]
            dk_ref[:, ln] = dk[j]
            dv_ref[:, ln] = dv[j]
            dz_ref[:, ln] = dz[j].astype(BF16)
            dstate[hd] = ds0[j]
            dba = dba + jnp.where(lane == hd, dbc[j], 0.0) + jnp.where(lane == N_HEADS + hd, dac[j], 0.0)
            dal_row = dal_row + jnp.where(lane1 == hd, dal[j], 0.0)
            ddt_row = ddt_row + jnp.where(lane1 == hd, ddt[j], 0.0)
        dba_ref[...] += dba
        dsc_ref[0:1, :] += dal_row
        dsc_ref[1:2, :] += ddt_row
        dgn_ref[...] += dgn

    outs = pl.pallas_call(
        body,
        name="dn_bwd",
        grid=(nblk, N_HEADS // DN_HP),
        in_specs=[colblk(0), colblk(nq), colblk(2 * nq), ba, colblk(O_Z_DN // (DN_HP * LANE)), vec, vec, vec, st,
                  colblk(0)],
        out_specs=[colblk(0), colblk(0), colblk(0), colblk(0),
                   pl.BlockSpec((SUPER, LANE), lambda i, h: (nblk - 1 - i, 0)),
                   pl.BlockSpec((2, LANE), lambda i, h: (0, 0)), vec],
        out_shape=[jax.ShapeDtypeStruct((t, D_MODEL), F32)] * 3
        + [jax.ShapeDtypeStruct((t, D_MODEL), BF16), jax.ShapeDtypeStruct((t, LANE), F32),
           jax.ShapeDtypeStruct((2, LANE), F32), jax.ShapeDtypeStruct((1, LANE), F32)],
        scratch_shapes=[pltpu.VMEM((N_HEADS, D_HEAD, D_HEAD), F32)],
    )(c, c, c, proj, proj, alog_row, dtb_row, gn, states, do)
    return outs


SB_TQ = 256
SB_TK = 256
SB_HP = 4


def _sb_logits(z, mask):
    sp = jnp.log(1.0 + jnp.exp(-jnp.abs(z)))
    lf_raw = -(jnp.maximum(z, 0.0) + sp)
    lb = lf_raw + z
    lf = lf_raw if mask is None else jnp.where(mask, lf_raw, 0.0)
    return lb, lf_raw, lf


def _suffix_sums(x, sel):
    hi, lo = _split2(x)
    d = functools.partial(lax.dot_general, dimension_numbers=NN, preferred_element_type=F32)
    return d(hi, sel) + d(lo, sel)


def _sb_diag_mask(tq, r):
    return r * SB_TK + _iota2((tq, SB_TK), 1) < _iota2((tq, SB_TK), 0)


def _sb_specs(t, tq):
    w = SB_HP * LANE
    q0, k0, v0, z0 = (O_QKV_SB // w, (O_QKV_SB + D_MODEL) // w, (O_QKV_SB + 2 * D_MODEL) // w, O_Z_SB // w)

    def blk(b0):
        return pl.BlockSpec((tq, w), lambda h, i: (i, b0 + h))

    def full(b0, **kw):
        return pl.BlockSpec((t, w), lambda h, i: (0, b0 + h), **kw)

    once = dict(pipeline_mode=pl.Buffered(1))
    return blk(q0), full(k0, **once), full(v0, **once), blk(z0), blk(0), full(0)


def _sb_fwd(proj):
    t = proj.shape[0]
    tq = min(SB_TQ, t)
    ndiag = tq // SB_TK
    scale = 1.0 / math.sqrt(D_HEAD)

    def body(q_ref, k_ref, v_ref, z_ref, o_ref, oraw_ref):
        qi = pl.program_id(1)
        lanes = [slice(hd * LANE, (hd + 1) * LANE) for hd in range(SB_HP)]
        qs = [(q_ref[:, ln] * scale).astype(BF16) for ln in lanes]
        after = (_iota2((SB_TK, SB_TK), 0) > _iota2((SB_TK, SB_TK), 1)).astype(BF16)
        oraw_ref[...] = jnp.zeros_like(oraw_ref)

        def block(kb, mask, c_lf):
            rows = pl.ds(pl.multiple_of(kb * SB_TK, SB_TK), SB_TK)
            z = _each(lambda q, ln: _dot(q, k_ref[rows, ln], NT), qs, lanes)
            lg = _each(lambda x: _sb_logits(x, mask), z)
            surv = _each(lambda x: _suffix_sums(x[2], after), lg)
            att = _each(lambda x, s, c: jnp.exp(x[0] + s + c), lg, surv, c_lf)
            if mask is not None:
                att = _each(lambda a: jnp.where(mask, a, 0.0), att)
            pv = _each(lambda a, ln: _dot(a, v_ref[rows, ln], NN), att, lanes)
            for p, ln in zip(pv, lanes):
                oraw_ref[:, ln] += p
            return tuple(_each(lambda c, x: c + jnp.sum(x[2], axis=1, keepdims=True), c_lf, lg))

        carry = tuple(jnp.zeros((tq, 1), F32) for _ in range(SB_HP))
        for r in reversed(range(ndiag)):
            carry = block(qi * ndiag + r, _sb_diag_mask(tq, r), carry)
        lax.fori_loop(0, qi * ndiag, lambda i, c: block(qi * ndiag - 1 - i, None, c), carry)
        o_ref[...] = (oraw_ref[...] * _silu(z_ref[...])).astype(BF16)

    q_spec, k_spec, v_spec, z_spec, out, _ = _sb_specs(t, tq)
    return pl.pallas_call(
        body,
        name="sb_fwd",
        grid=(N_HEADS // SB_HP, t // tq),
        in_specs=[q_spec, k_spec, v_spec, z_spec],
        out_specs=[out, out],
        out_shape=[jax.ShapeDtypeStruct((t, D_MODEL), BF16), jax.ShapeDtypeStruct((t, D_MODEL), F32)],
    )(proj, proj, proj, proj)


def _sb_bwd(proj, oraw, do):
    t = proj.shape[0]
    tq = min(SB_TQ, t)
    ndiag = tq // SB_TK
    scale = 1.0 / math.sqrt(D_HEAD)

    def body(q_ref, k_ref, v_ref, z_ref, oraw_ref, do_ref, dq_ref, dk_ref, dv_ref, dz_ref, dk_acc, dv_acc,
             p_scr, z_scr):
        qi = pl.program_id(1)
        nq = pl.num_programs(1)

        @pl.when(qi == 0)
        def _():
            dk_acc[...] = jnp.zeros_like(dk_acc)
            dv_acc[...] = jnp.zeros_like(dv_acc)

        heads = range(SB_HP)
        lanes = [slice(hd * LANE, (hd + 1) * LANE) for hd in heads]
        zg = z_ref[...]
        sg = _sigmoid(zg)
        dog = do_ref[...].astype(F32)
        dz_ref[...] = (dog * oraw_ref[...] * (sg * (1.0 + zg * (1.0 - sg)))).astype(BF16)
        d_o = (dog * (zg * sg)).astype(BF16)
        d_o16 = [d_o[:, ln] for ln in lanes]
        qs = [(q_ref[:, ln] * scale).astype(BF16) for ln in lanes]
        ri = _iota2((SB_TK, SB_TK), 0)
        ci = _iota2((SB_TK, SB_TK), 1)
        after = (ri > ci).astype(BF16)
        earlier = (ri < ci).astype(BF16)

        def rows_of(kb):
            return pl.ds(pl.multiple_of(kb * SB_TK, SB_TK), SB_TK)

        def down(kb, mask, c_lf):
            rows = rows_of(kb)
            z = _each(lambda q, ln: _dot(q, k_ref[rows, ln], NT), qs, lanes)
            da = _each(lambda d, ln: _dot(d, v_ref[rows, ln], NT), d_o16, lanes)
            lg = _each(lambda x: _sb_logits(x, mask), z)
            surv = _each(lambda x: _suffix_sums(x[2], after), lg)
            att = _each(lambda x, s, c: jnp.exp(x[0] + s + c), lg, surv, c_lf)
            if mask is not None:
                att = _each(lambda a: jnp.where(mask, a, 0.0), att)
            dv = _each(lambda a, d: _dot(a, d, TN), att, d_o16)
            for hd in heads:
                p_scr[hd, kb] = att[hd] * da[hd]
                z_scr[hd, kb] = z[hd]
                dv_acc[rows, lanes[hd]] += dv[hd]
            return tuple(_each(lambda c, x: c + jnp.sum(x[2], axis=1, keepdims=True), c_lf, lg))

        c_lf = tuple(jnp.zeros((tq, 1), F32) for _ in heads)
        for r in reversed(range(ndiag)):
            c_lf = down(qi * ndiag + r, _sb_diag_mask(tq, r), c_lf)
        lax.fori_loop(0, qi * ndiag, lambda i, c: down(qi * ndiag - 1 - i, None, c), c_lf)

        def up(kb, mask, carry):
            dq, c_p = carry
            rows = rows_of(kb)
            p = [p_scr[hd, kb] for hd in heads]
            zs = [z_scr[hd, kb] for hd in heads]
            before = _each(lambda x, c: _suffix_sums(x, earlier) + c, p, c_p)
            e = _each(lambda x: jnp.exp(-jnp.abs(x)), zs)
            r = _each(lambda x: 1.0 / (1.0 + x), e)
            sig = _each(lambda x, a, b: jnp.where(x >= 0.0, b, a * b), zs, e, r)
            oms = _each(lambda x, a, b: jnp.where(x >= 0.0, a * b, b), zs, e, r)
            if mask is not None:
                sig = _each(lambda a: jnp.where(mask, a, 0.0), sig)
            dzz = _each(lambda x, o, g, b: x * o - g * b, p, oms, sig, before)
            dk = _each(lambda x, q: _dot(x, q, TN), dzz, qs)
            dq = _each(lambda a, x, ln: a + _dot(x, k_ref[rows, ln], NN), dq, dzz, lanes)
            for hd in heads:
                dk_acc[rows, lanes[hd]] += dk[hd]
            return tuple(dq), tuple(_each(lambda c, x: c + jnp.sum(x, axis=1, keepdims=True), c_p, p))

        carry = (tuple(jnp.zeros((tq, D_HEAD), F32) for _ in heads), tuple(jnp.zeros((tq, 1), F32) for _ in heads))
        carry = lax.fori_loop(0, qi * ndiag, lambda kb, c: up(kb, None, c), carry)
        for r in range(ndiag):
            carry = up(qi * ndiag + r, _sb_diag_mask(tq, r), carry)
        dq = carry[0]
        for hd in heads:
            dq_ref[:, lanes[hd]] = (dq[hd] * scale).astype(BF16)

        @pl.when(qi == nq - 1)
        def _():
            dk_ref[...] = dk_acc[...].astype(BF16)
            dv_ref[...] = dv_acc[...].astype(BF16)

    q_spec, k_spec, v_spec, z_spec, blk, full = _sb_specs(t, tq)
    o = jax.ShapeDtypeStruct((t, D_MODEL), BF16)
    w = SB_HP * LANE
    return pl.pallas_call(
        body,
        name="sb_bwd",
        grid=(N_HEADS // SB_HP, t // tq),
        in_specs=[q_spec, k_spec, v_spec, z_spec, blk, blk],
        out_specs=[blk, full, full, blk],
        out_shape=[o, o, o, o],
        scratch_shapes=[pltpu.VMEM((t, w), F32), pltpu.VMEM((t, w), F32)]
        + [pltpu.VMEM((SB_HP, t // SB_TK, tq, SB_TK), F32)] * 2,
    )(proj, proj, proj, proj, oraw, do)


def _mem_kv_fn(mem, mg, w):
    return mm_nn(_rmsnorm(mem, mg), w)


def _mem_kv(mem, mg, w):
    def body(m_ref, g_ref, w_ref, o_ref):
        o_ref[...] = _mem_kv_fn(m_ref[...], g_ref[...], w_ref[...])

    return pl.pallas_call(body, name="mem_kv", out_shape=jax.ShapeDtypeStruct((MEM_LEN, 2 * MEM_W), F32))(mem, mg, w)


def _mem_kv_bwd(mem, mg, w, dmkv):
    def body(m_ref, g_ref, w_ref, d_ref, dg_ref, dw_ref):
        _, vjp = jax.vjp(_mem_kv_fn, m_ref[...], g_ref[...], w_ref[...].astype(F32))
        _, dg, dw = vjp(d_ref[...])
        dg_ref[...] = dg
        dw_ref[...] = dw

    return pl.pallas_call(
        body, name="mem_kv_bwd",
        out_shape=[jax.ShapeDtypeStruct((1, D_MODEL), F32), jax.ShapeDtypeStruct((D_MODEL, 2 * MEM_W), F32)],
    )(mem, mg, w, dmkv)


def _mem_attn(proj, mkv, tm=256):
    t = proj.shape[0]
    tm = min(tm, t)

    def body(q_ref, z_ref, kv_ref, o_ref):
        o_ref[...] = _mem_fn(q_ref[...], z_ref[...], kv_ref[...]).astype(BF16)

    return pl.pallas_call(
        body,
        name="mem_attn",
        grid=(t // tm,),
        in_specs=[pl.BlockSpec((tm, MEM_W), lambda i: (i, O_MQ // MEM_W)),
                  pl.BlockSpec((tm, MEM_W), lambda i: (i, O_MZ // MEM_W)),
                  pl.BlockSpec((MEM_LEN, 2 * MEM_W), lambda i: (0, 0))],
        out_specs=pl.BlockSpec((tm, MEM_W), lambda i: (i, 0)),
        out_shape=jax.ShapeDtypeStruct((t, MEM_W), BF16),
    )(proj, proj, mkv)


def _mem_attn_bwd(proj, mkv, do, tm=256):
    t = proj.shape[0]
    tm = min(tm, t)

    def body(q_ref, z_ref, kv_ref, do_ref, dq_ref, dz_ref, dkv_ref):
        _, vjp = jax.vjp(_mem_fn, q_ref[...], z_ref[...], kv_ref[...])
        dq, dz, dkv = vjp(do_ref[...].astype(F32))
        dq_ref[...] = dq.astype(BF16)
        dz_ref[...] = dz.astype(BF16)

        @pl.when(pl.program_id(0) == 0)
        def _():
            dkv_ref[...] = jnp.zeros_like(dkv_ref)

        dkv_ref[...] += dkv

    blk = pl.BlockSpec((tm, MEM_W), lambda i: (i, 0))
    kv = pl.BlockSpec((MEM_LEN, 2 * MEM_W), lambda i: (0, 0))
    return pl.pallas_call(
        body,
        name="mem_attn_bwd",
        grid=(t // tm,),
        in_specs=[pl.BlockSpec((tm, MEM_W), lambda i: (i, O_MQ // MEM_W)),
                  pl.BlockSpec((tm, MEM_W), lambda i: (i, O_MZ // MEM_W)), kv, blk],
        out_specs=[blk, blk, kv],
        out_shape=[jax.ShapeDtypeStruct((t, MEM_W), BF16), jax.ShapeDtypeStruct((t, MEM_W), BF16),
                   jax.ShapeDtypeStruct((MEM_LEN, 2 * MEM_W), F32)],
    )(proj, proj, mkv, do)


def _local_step(x, mem, tgt, norm_g, mem_norm_g, w_alt, conv_w, alog_row, dtb_row, dn_norm_g, w_mem_kv, w_br_dn, w_br_sb,
                w_br_mem, w_out, final_g):
    h = _norm_in(x, norm_g)
    proj = _matmul(h, w_alt, "nt", F32, 2048, 384, 1024, "proj")

    c = _dn_conv(proj, conv_w)
    o_dn, states = _dn_fwd(c, proj, alog_row, dtb_row, dn_norm_g)
    o_sb, o_sb_raw = _sb_fwd(proj)
    mkv = _mem_kv(mem, mem_norm_g, w_mem_kv)
    o_m = _mem_attn(proj, mkv)

    y_dn = _matmul(o_dn, w_br_dn, "nn", F32, 512, 1024, 1024, "y_dn")
    y_sb = _matmul(o_sb, w_br_sb, "nn", F32, 512, 1024, 1024, "y_sb")
    y_m = _matmul(o_m, w_br_mem, "nn", F32, 512, 1024, 1024, "y_m")
    merged = _merge(proj, y_dn, y_sb, y_m)
    mo = _matmul(merged, w_out, "nn", F32, 512, 1024, 1024, "mo")
    loss, dout, d_final_g = _loss_head(x, mo, final_g, tgt)

    dmerged = _matmul(dout, w_out, "nt", F32, 512, 1024, 1024, "dmerged")
    dw_out = _matmul(merged, dout, "tn", F32, 256, 1024, 2048, "dw_out")
    dy_dn, dy_sb, dy_m, dg_dn, dg_sb, dg_m = _merge_bwd(proj, y_dn, y_sb, y_m, dmerged)
    do_dn = _matmul(dy_dn, w_br_dn, "nt", BF16, 512, 1024, 1024, "do_dn")
    do_sb = _matmul(dy_sb, w_br_sb, "nt", BF16, 512, 1024, 1024, "do_sb")
    do_m = _matmul(dy_m, w_br_mem, "nt", BF16, 512, 256, 1024, "do_m")
    dw_br_dn = _matmul(o_dn, dy_dn, "tn", F32, 256, 1024, 2048, "dw_br_dn")
    dw_br_sb = _matmul(o_sb, dy_sb, "tn", F32, 256, 1024, 2048, "dw_br_sb")
    dw_br_mem = _matmul(o_m, dy_m, "tn", F32, 256, 1024, 2048, "dw_br_mem")

    dmq, dmz, dmkv = _mem_attn_bwd(proj, mkv, do_m)
    d_mem_norm_g, dw_mem_kv = _mem_kv_bwd(mem, mem_norm_g, w_mem_kv, dmkv)
    dq_sb, dk_sb, dv_sb, dz_sb = _sb_bwd(proj, o_sb_raw, do_sb)
    dcq, dck, dcv, dz_dn, dba, dscal, d_dn_norm_g = _dn_bwd(c, proj, alog_row, dtb_row, dn_norm_g, states, do_dn)
    dq_dn, dcw_q = _dn_conv_bwd(proj, conv_w, dcq, 0)
    dk_dn, dcw_k = _dn_conv_bwd(proj, conv_w, dck, 1)
    dv_dn, dcw_v = _dn_conv_bwd(proj, conv_w, dcv, 2)
    d_conv_w = jnp.concatenate([dcw_q, dcw_k, dcw_v], axis=1)

    dproj = jnp.concatenate([dq_dn, dk_dn, dv_dn, dz_dn, dq_sb, dk_sb, dv_sb, dz_sb, dmq, dmz, dg_dn, dg_sb, dg_m,
                             dba.astype(BF16)], axis=1)
    dh = _matmul(dproj, w_alt, "nn", F32, 512, 1024, 3968, "dh")
    dw_alt = _matmul(dproj, h, "tn", F32, 384, 1024, 2048, "dw_alt")
    grad_x, d_norm_g = _norm_in_bwd(x, norm_g, dh, dout)
    return dict(loss=loss, grad_x=grad_x, norm_g=d_norm_g, mem_norm_g=d_mem_norm_g, w_alt=dw_alt, conv_w=d_conv_w,
                scal=dscal, dn_norm_g=d_dn_norm_g, w_mem_kv=dw_mem_kv, w_br_dn=dw_br_dn, w_br_sb=dw_br_sb,
                w_br_mem=dw_br_mem, w_out=dw_out, final_g=d_final_g)


MESH = pl.DeviceIdType.MESH
ANY = pl.BlockSpec(memory_space=pl.ANY)


def _position():
    return lax.axis_index("x"), lax.axis_index("y"), lax.axis_index("c")


def _all_gather(xs, name):
    n = len(xs)

    def body(*refs):
        x_refs, o_refs = refs[:n], refs[n:2 * n]
        send_sems, recv_sems, local_sems = refs[2 * n:]
        x, y, c = _position()
        me, sibling = (x, y, c), (x, y, 1 - c)
        x_nbr, y_nbr, diag = (1 - x, y, c), (x, 1 - y, c), (1 - x, 1 - y, c)
        south = c == 0
        relay_from = tuple(jnp.where(south, a, b) for a, b in zip(y_nbr, x_nbr))
        relay_to = tuple(jnp.where(south, a, b) for a, b in zip(x_nbr, y_nbr))

        def slot(p):
            return 4 * p[0] + 2 * p[1] + p[2]

        def copy(a, k, block, to, src=None):
            dst = o_refs[a].at[slot(block)]
            return pltpu.make_async_remote_copy(
                src_ref=dst if src is None else src, dst_ref=dst, send_sem=send_sems.at[7 * a + k],
                recv_sem=recv_sems.at[7 * a + k], device_id=to, device_id_type=MESH)

        mine = [pltpu.make_async_copy(x_refs[a], o_refs[a].at[slot(me)], local_sems.at[a]) for a in range(n)]
        for cp in mine:
            cp.start()
        sends = []
        for a in range(n):
            sends += [copy(a, 0, me, sibling, src=x_refs[a]), copy(a, 1, me, x_nbr, src=x_refs[a]),
                      copy(a, 2, me, y_nbr, src=x_refs[a])]
        for cp in sends:
            cp.start()
        later = []
        for a in range(n):
            copy(a, 1, x_nbr, me).wait_recv()
            copy(a, 2, y_nbr, me).wait_recv()
            later += [copy(a, 3, relay_from, relay_to), copy(a, 4, x_nbr, sibling), copy(a, 5, y_nbr, sibling)]
            for cp in later[-3:]:
                cp.start()
        for a in range(n):
            copy(a, 3, diag, me).wait_recv()
            later.append(copy(a, 6, diag, sibling))
            later[-1].start()
        for a in range(n):
            copy(a, 0, sibling, me).wait_recv()
            for k, chip in ((4, x_nbr), (5, y_nbr), (6, diag)):
                copy(a, k, (chip[0], chip[1], 1 - c), me).wait_recv()
        for cp in sends + later:
            cp.wait_send()
        for cp in mine:
            cp.wait()

    return pl.pallas_call(
        body,
        name=name,
        in_specs=[ANY] * n,
        out_specs=[ANY] * n,
        out_shape=[jax.ShapeDtypeStruct((N_DEV, *v.shape), v.dtype) for v in xs],
        scratch_shapes=[pltpu.SemaphoreType.DMA((7 * n,)), pltpu.SemaphoreType.DMA((7 * n,)),
                        pltpu.SemaphoreType.DMA((n,))],
    )(*xs)


def _window_view(ref, dest):
    return ref.at[pl.ds(LANE * WIN_START[dest], WIN_W), :]


def _chunk_rows(rows, cols):
    best = max(ch for ch in range(16, rows + 1, 16) if rows % ch == 0 and ch * cols <= (1 << 17))
    return best


def _halving_stage(xs, axis, name, out_dtype, windowed=()):
    n_arr = len(xs)
    metas = []
    for k, v in enumerate(xs):
        if k in windowed:
            metas.append((N_DEV // 2, WIN_W, v.shape[1]))
        else:
            assert v.shape[1] == 2
            metas.append((v.shape[0], v.shape[2], v.shape[3]))
    chunk = [_chunk_rows(r, c) for (_, r, c) in metas]
    offs = [sum(m[0] for m in metas[:k]) for k in range(n_arr)]
    n_sem = sum(m[0] for m in metas)

    def body(*refs):
        x_refs = refs[:n_arr]
        o_refs = refs[n_arr:2 * n_arr]
        land_refs = refs[2 * n_arr:3 * n_arr]
        rest = refs[3 * n_arr:]
        bufs = rest[:3 * n_arr]
        send_sems, recv_sems, in_sems, out_sems = rest[3 * n_arr:]
        pos = dict(zip("xyc", _position()))
        bit = pos[axis]
        peer = tuple(1 - pos[a] if a == axis else pos[a] for a in "xyc")

        def view(k, i, b):
            if k in windowed:
                return _window_view(x_refs[k], 2 * i + b)
            return x_refs[k].at[i, b]

        def add_blocks(k, a_view, b_view, o_view):
            _, rows, _ = metas[k]
            ch = chunk[k]
            nch = rows // ch
            va, vb, vo = bufs[3 * k:3 * k + 3]

            def rows_of(j):
                return pl.ds(pl.multiple_of(j * ch, 8), ch)

            def loads(j, s):
                return (pltpu.make_async_copy(a_view.at[rows_of(j), :], va.at[s], in_sems.at[0, s]),
                        pltpu.make_async_copy(b_view.at[rows_of(j), :], vb.at[s], in_sems.at[1, s]))

            def store(j, s):
                return pltpu.make_async_copy(vo.at[s], o_view.at[rows_of(j), :], out_sems.at[s])

            for cp in loads(0, 0):
                cp.start()

            def step(j, _):
                s = lax.rem(j, 2)

                @pl.when(j + 1 < nch)
                def _():
                    for cp in loads(j + 1, 1 - s):
                        cp.start()

                for cp in loads(j, s):
                    cp.wait()

                @pl.when(j >= 2)
                def _():
                    store(j - 2, s).wait()

                vo[s] = (va[s] + vb[s]).astype(vo.dtype)
                store(j, s).start()
                return 0

            lax.fori_loop(0, nch, step, 0)
            for j in range(max(0, nch - 2), nch):
                store(j, j % 2).wait()

        for b in (0, 1):
            @pl.when(bit == b)
            def _(b=b):
                sends = []
                for k in range(n_arr):
                    for i in range(metas[k][0]):
                        cp = pltpu.make_async_remote_copy(
                            src_ref=view(k, i, 1 - b), dst_ref=land_refs[k].at[i], send_sem=send_sems.at[offs[k] + i],
                            recv_sem=recv_sems.at[offs[k] + i], device_id=peer, device_id_type=MESH)
                        cp.start()
                        sends.append(cp)
                idx = 0
                for k in range(n_arr):
                    for i in range(metas[k][0]):
                        sends[idx].wait_recv()
                        add_blocks(k, view(k, i, b), land_refs[k].at[i], o_refs[k].at[i])
                        idx += 1
                for cp in sends:
                    cp.wait_send()

    out_shape = [jax.ShapeDtypeStruct(m, out_dtype) for m in metas]
    land_shape = [jax.ShapeDtypeStruct(m, F32) for m in metas]
    scratch = []
    for k in range(n_arr):
        scratch += [pltpu.VMEM((2, chunk[k], metas[k][2]), F32)] * 2 + [pltpu.VMEM((2, chunk[k], metas[k][2]), out_dtype)]
    scratch += [pltpu.SemaphoreType.DMA((n_sem,)), pltpu.SemaphoreType.DMA((n_sem,)),
                pltpu.SemaphoreType.DMA((2, 2)), pltpu.SemaphoreType.DMA((2,))]
    outs = pl.pallas_call(
        body,
        name=name,
        in_specs=[ANY] * n_arr,
        out_specs=[ANY] * (2 * n_arr),
        out_shape=out_shape + land_shape,
        scratch_shapes=scratch,
    )(*xs)
    return outs[:n_arr]


def _hbm_add(a_view, b_view, o_view, bufs, in_sems, out_sems, ch):
    rows = a_view.shape[0]
    nch = rows // ch
    va, vb, vo = bufs

    def rows_of(j):
        return pl.ds(pl.multiple_of(j * ch, 16), ch)

    def loads(j, s):
        return (pltpu.make_async_copy(a_view.at[rows_of(j), :], va.at[s], in_sems.at[0, s]),
                pltpu.make_async_copy(b_view.at[rows_of(j), :], vb.at[s], in_sems.at[1, s]))

    def store(j, s):
        return pltpu.make_async_copy(vo.at[s], o_view.at[rows_of(j), :], out_sems.at[s])

    for cp in loads(0, 0):
        cp.start()

    def step(j, _):
        s = lax.rem(j, 2)

        @pl.when(j + 1 < nch)
        def _():
            for cp in loads(j + 1, 1 - s):
                cp.start()

        for cp in loads(j, s):
            cp.wait()

        @pl.when(j >= 2)
        def _():
            store(j - 2, s).wait()

        vo[s] = (va[s].astype(F32) + vb[s].astype(F32)).astype(vo.dtype)
        store(j, s).start()
        return 0

    lax.fori_loop(0, nch, step, 0)
    for j in range(max(0, nch - 2), nch):
        store(j, j % 2).wait()


def _xy_stage(xs, first, name):
    n_arr = len(xs)
    if first:
        shapes = [(v.shape[2] // 2, v.shape[3]) for v in xs]
        ins = list(xs)
    else:
        shapes = [(a.shape[1], a.shape[2]) for a, _ in xs]
        ins = [v for pair in xs for v in pair]
    n_blk = 2 if first else 1
    out_dtype = BF16 if first else F32
    chunk = [_chunk_rows(r, c) for (r, c) in shapes]
    n_sem = 2 * n_blk * n_arr

    def body(*refs):
        n_in = len(ins)
        in_refs = refs[:n_in]
        n_out = 2 * n_arr if first else n_arr
        o_refs = refs[n_in:n_in + n_out]
        land = refs[n_in + n_out:n_in + n_out + 2 * n_arr]
        rest = refs[n_in + n_out + 2 * n_arr:]
        bufs = rest[:3 * n_arr]
        send_sems, recv_sems, in_sems, out_sems = rest[3 * n_arr:]
        x, y, c = _position()
        peers = {"x": (1 - x, y, c), "y": (x, 1 - y, c)}
        jobs = []
        for k in range(n_arr):
            r, _ = shapes[k]
            half_a, half_b = pl.ds(0, r), pl.ds(r, r)
            if first:
                src = in_refs[k]
                for i in range(2):
                    jobs.append((k, src.at[i, 1 - y, half_a, :], src.at[i, y, half_a, :], land[2 * k].at[i],
                                 o_refs[2 * k].at[i], "y"))
                    jobs.append((k, src.at[1 - x, i, half_b, :], src.at[x, i, half_b, :], land[2 * k + 1].at[i],
                                 o_refs[2 * k + 1].at[i], "x"))
            else:
                a1, b1 = in_refs[2 * k], in_refs[2 * k + 1]
                jobs.append((k, a1.at[1 - x], a1.at[x], land[2 * k], o_refs[k].at[half_a, :], "x"))
                jobs.append((k, b1.at[1 - y], b1.at[y], land[2 * k + 1], o_refs[k].at[half_b, :], "y"))
        sends = []
        for n, (k, send, _, landing, _, axis) in enumerate(jobs):
            cp = pltpu.make_async_remote_copy(src_ref=send, dst_ref=landing, send_sem=send_sems.at[n],
                                              recv_sem=recv_sems.at[n], device_id=peers[axis], device_id_type=MESH)
            cp.start()
            sends.append(cp)
        for cp, (k, _, kept, landing, out, _) in zip(sends, jobs):
            cp.wait_recv()
            _hbm_add(kept, landing, out, bufs[3 * k:3 * k + 3], in_sems, out_sems, chunk[k])
        for cp in sends:
            cp.wait_send()

    if first:
        out_shape = [jax.ShapeDtypeStruct((2, r, c), BF16) for (r, c) in shapes for _ in range(2)]
        land_shape = out_shape
    else:
        out_shape = [jax.ShapeDtypeStruct((2 * r, c), F32) for (r, c) in shapes]
        land_shape = [jax.ShapeDtypeStruct((r, c), BF16) for (r, c) in shapes for _ in range(2)]
    scratch = []
    for k in range(n_arr):
        scratch += [pltpu.VMEM((2, chunk[k], shapes[k][1]), BF16)] * 2 + [pltpu.VMEM((2, chunk[k], shapes[k][1]), out_dtype)]
    scratch += [pltpu.SemaphoreType.DMA((n_sem,)), pltpu.SemaphoreType.DMA((n_sem,)),
                pltpu.SemaphoreType.DMA((2, 2)), pltpu.SemaphoreType.DMA((2,))]
    outs = pl.pallas_call(
        body,
        name=name,
        in_specs=[ANY] * len(ins),
        out_specs=[ANY] * (len(out_shape) + len(land_shape)),
        out_shape=out_shape + land_shape,
        scratch_shapes=scratch,
    )(*ins)
    outs = outs[:len(out_shape)]
    return [(outs[2 * k], outs[2 * k + 1]) for k in range(n_arr)] if first else list(outs)


def _reduce_scatter(dw_al, blocks):
    xs = [dw_al] + [b.reshape(N_DEV // 2, 2, *b.shape[1:]) for b in blocks]
    ys = _halving_stage(xs, "c", "rs_c", BF16, windowed=(0,))
    pairs = _xy_stage([v.reshape(2, 2, *v.shape[1:]) for v in ys], True, "rs_xy1")
    return _xy_stage(pairs, False, "rs_xy2")


def _sum_slots(gs):
    n = len(gs)

    def body(*refs):
        for g_ref, o_ref in zip(refs[:n], refs[n:]):
            acc = g_ref[0]
            for d in range(1, N_DEV):
                acc = acc + g_ref[d]
            o_ref[...] = acc

    return pl.pallas_call(body, name="sum_slots",
                          out_shape=[jax.ShapeDtypeStruct(g.shape[1:], g.dtype) for g in gs])(*gs)


def _assemble_w_al(wins, bas):
    ba_tile = O_BA // LANE
    assert W_AL // LANE == ba_tile + 1
    cols = wins.shape[2]
    n_buf = 3
    ends = [WIN_START[d + 1] if d + 1 < N_DEV else ba_tile + 1 for d in range(N_DEV)]
    assert WIN_START[N_DEV - 1] + WIN_TILES == ba_tile + 1

    def body(w_ref, ba_ref, o_ref, buf, ld_sems, st_sems, ba_sem):
        def load(d):
            return pltpu.make_async_copy(w_ref.at[d], buf.at[d % n_buf], ld_sems.at[d % n_buf])

        def store(d):
            n = LANE * (ends[d] - WIN_START[d])
            return pltpu.make_async_copy(buf.at[d % n_buf, pl.ds(0, n), :],
                                         o_ref.at[pl.ds(LANE * WIN_START[d], n), :], st_sems.at[d % n_buf])

        load(0).start()
        for d in range(N_DEV):
            if d + 1 < N_DEV:
                if d + 1 >= n_buf:
                    store(d + 1 - n_buf).wait()
                load(d + 1).start()
            load(d).wait()
            if d > 0:
                ov = LANE * (WIN_START[d - 1] + WIN_TILES - WIN_START[d])
                buf[d % n_buf, :ov, :] = buf[d % n_buf, :ov, :] + buf[(d - 1) % n_buf, WIN_W - ov:, :]
            if d == N_DEV - 1:
                ba_copy = pltpu.make_async_copy(
                    ba_ref.at[BA_DEV], buf.at[d % n_buf, pl.ds(WIN_W - LANE, ba_ref.shape[1]), :], ba_sem)
                ba_copy.start()
                ba_copy.wait()
            store(d).start()
        for d in range(N_DEV - n_buf, N_DEV):
            store(d).wait()

    return pl.pallas_call(
        body,
        name="assemble_w_al",
        in_specs=[ANY, ANY],
        out_specs=ANY,
        out_shape=jax.ShapeDtypeStruct((W_AL, cols), wins.dtype),
        scratch_shapes=[pltpu.VMEM((n_buf, WIN_W, cols), wins.dtype), pltpu.SemaphoreType.DMA((n_buf,)),
                        pltpu.SemaphoreType.DMA((n_buf,)), pltpu.SemaphoreType.DMA],
    )(wins, bas)


def _adamw_math(w, g, m, v):
    m_new = ADAM_B1 * m + (1.0 - ADAM_B1) * g
    v_new = ADAM_B2 * v + (1.0 - ADAM_B2) * (g * g)
    m_hat = m_new / (1.0 - ADAM_B1 ** ADAM_STEP)
    v_hat = v_new / (1.0 - ADAM_B2 ** ADAM_STEP)
    return -ADAM_LR * (m_hat / (jnp.sqrt(v_hat) + ADAM_EPS) + ADAM_WD * w), m_new, v_new


def _adamw(w, g, m, v, name, tb=134):
    r, _, c = w.shape
    assert r % tb == 0

    def body(w_ref, g_ref, m_ref, v_ref, d_ref, nm_ref, nv_ref):
        d_ref[...], nm_ref[...], nv_ref[...] = _adamw_math(w_ref[...], g_ref[...], m_ref[...], v_ref[...])

    blk = pl.BlockSpec((tb, 1, c), lambda i: (i, 0, 0))
    o = jax.ShapeDtypeStruct(w.shape, F32)
    return pl.pallas_call(body, name=name, grid=(r // tb,), in_specs=[blk] * 4, out_specs=[blk] * 3,
                          out_shape=[o, o, o])(w, g, m, v)


def _adamw_many(ws, gs, ms, vs, name):
    n = len(ws)

    def body(*refs):
        for k in range(n):
            w_ref, g_ref, m_ref, v_ref = (refs[j * n + k] for j in range(4))
            d_ref, nm_ref, nv_ref = (refs[(4 + j) * n + k] for j in range(3))
            d_ref[...], nm_ref[...], nv_ref[...] = _adamw_math(w_ref[...], g_ref[...], m_ref[...], v_ref[...])

    shapes = [jax.ShapeDtypeStruct(w.shape, F32) for w in ws]
    outs = pl.pallas_call(body, name=name, out_shape=shapes * 3)(*ws, *gs, *ms, *vs)
    return outs[:n], outs[n:2 * n], outs[2 * n:]


def _select(me, table):
    return sum(jnp.where(me == d, jnp.int32(v), jnp.int32(0)) for d, v in enumerate(table))


WIN_SHIFT = tuple(SHARD_W * d - LANE * WIN_START[d] for d in range(N_DEV))
PAD_L = 256
PAD_R = 256


def _shard_to_window(shard_t, me):
    shift = _select(me, WIN_SHIFT)
    start = _select(me, WIN_START)
    padded = jnp.pad(shard_t, ((PAD_L, PAD_R), (0, 0)))
    cols = shard_t.shape[1]
    lo = lax.dynamic_slice(padded, (PAD_L - shift, 0), (WIN_W, cols))
    hi = lax.dynamic_slice(padded, (PAD_L - shift + N_BA, 0), (WIN_W, cols))
    aligned = LANE * start + lax.broadcasted_iota(jnp.int32, (WIN_W, 1), 0)
    return jnp.where(aligned >= ORIG_BA, hi, lo)


def _window_to_shard(win, ba_grad, me):
    shift = _select(me, WIN_SHIFT)
    cols = win.shape[1]
    padded = jnp.pad(win, ((N_BA, PAD_R), (0, 0)))
    lo = lax.dynamic_slice(padded, (N_BA + shift, 0), (SHARD_W, cols))
    hi = lax.dynamic_slice(padded, (shift, 0), (SHARD_W, cols))
    orig = SHARD_W * me + lax.broadcasted_iota(jnp.int32, (SHARD_W, 1), 0)
    ba_full = lax.dynamic_update_slice(jnp.zeros((SHARD_W, cols), win.dtype), ba_grad, (BA_LOCAL, 0))
    return jnp.where(orig < ORIG_BA, lo, jnp.where(orig >= ORIG_BA + N_BA, hi, ba_full))


def _pad_row(v, width=D_MODEL):
    v = v.reshape(1, -1)
    return jnp.pad(v, ((0, 0), (0, width - v.shape[1])))


def _slab(v, rows=8):
    return jnp.pad(v, ((0, rows - v.shape[0]), (0, D_MODEL - v.shape[1])))


def kernel(x, mem, norm_g, mem_norm_g, w_in, conv_w, a_log, dt_bias, dn_norm_g, w_mem_kv, w_br_dn, w_br_sb, w_br_mem, w_out, final_g, loss_target, m_norm_g, m_mem_norm_g, m_w_in, m_conv_w, m_a_log, m_dt_bias, m_dn_norm_g, m_w_mem_kv, m_w_br_dn, m_w_br_sb, m_w_br_mem, m_w_out, m_final_g, v_norm_g, v_mem_norm_g, v_w_in, v_conv_w, v_a_log, v_dt_bias, v_dn_norm_g, v_w_mem_kv, v_w_br_dn, v_w_br_sb, v_w_br_mem, v_w_out, v_final_g):
    xi, yi, ci = _position()
    me = 4 * xi + 2 * yi + ci

    shard_t = w_in[0].T
    win = _shard_to_window(shard_t, me).astype(BF16)
    ba = shard_t[BA_LOCAL:BA_LOCAL + N_BA, :].astype(BF16)
    g_win, g_ba, g_kv, g_dn, g_sb, g_out, g_mem, g_conv = _all_gather(
        [win, ba, w_mem_kv[0].astype(BF16), w_br_dn[0].astype(BF16), w_br_sb[0].astype(BF16), w_out[0].astype(BF16),
         w_br_mem[0].astype(BF16), conv_w[0]], "gather_weights")
    w_alt = _assemble_w_al(g_win, g_ba)
    w_mem_kv_f = g_kv.reshape(D_MODEL, 2 * MEM_W)
    w_br_dn_f = g_dn.reshape(D_MODEL, D_MODEL)
    w_br_sb_f = g_sb.reshape(D_MODEL, D_MODEL)
    w_out_f = g_out.reshape(D_MODEL, D_MODEL)
    w_br_mem_f = g_mem.transpose(1, 0, 2).reshape(MEM_W, D_MODEL)
    conv_w_f = g_conv.transpose(1, 0, 2).reshape(CONV_K, 3 * D_MODEL)

    r = _local_step(x[0], mem[0], loss_target[0], norm_g, mem_norm_g, w_alt, conv_w_f, _pad_row(a_log, LANE),
                    _pad_row(dt_bias, LANE), dn_norm_g, w_mem_kv_f, w_br_dn_f, w_br_sb_f, w_br_mem_f, w_out_f,
                    final_g.reshape(1, D_MODEL))

    dw_alt = r["w_alt"]
    g_win, g_kv, g_dn, g_sb, g_out, g_mem = _reduce_scatter(dw_alt, [
        r["w_mem_kv"].reshape(N_DEV, D_MODEL // N_DEV, 2 * MEM_W),
        r["w_br_dn"].reshape(N_DEV, D_MODEL // N_DEV, D_MODEL),
        r["w_br_sb"].reshape(N_DEV, D_MODEL // N_DEV, D_MODEL),
        r["w_out"].reshape(N_DEV, D_MODEL // N_DEV, D_MODEL),
        r["w_br_mem"].reshape(MEM_W, N_DEV, D_MODEL // N_DEV).transpose(1, 0, 2)])
    parts = [r["norm_g"], r["mem_norm_g"], r["final_g"], r["dn_norm_g"], r["scal"], r["loss"], r["conv_w"],
             dw_alt[O_BA:O_BA + N_BA, :]]
    s_norm_g, s_mem_norm_g, s_final_g, s_dn_norm_g, s_scal, s_loss, s_conv, s_ba = _sum_slots(
        _all_gather(parts, "gather_small"))
    loss = s_loss[0, 0]
    cw = conv_w.shape[2]
    g_conv = lax.dynamic_slice(s_conv, (0, cw * me), (CONV_K, cw))
    g_w_in_t = _window_to_shard(g_win, s_ba, me)
    grads = dict(norm_g=s_norm_g, mem_norm_g=s_mem_norm_g, w_in=g_w_in_t.T[None], conv_w=g_conv[None],
                 a_log=s_scal[0:1, :N_HEADS], dt_bias=s_scal[1:2, :N_HEADS], dn_norm_g=s_dn_norm_g, w_mem_kv=g_kv[None],
                 w_br_dn=g_dn[None], w_br_sb=g_sb[None], w_br_mem=g_mem[None], w_out=g_out[None],
                 final_g=s_final_g.reshape(D_MODEL))

    params = dict(norm_g=(norm_g, m_norm_g, v_norm_g), mem_norm_g=(mem_norm_g, m_mem_norm_g, v_mem_norm_g),
                  w_in=(w_in, m_w_in, v_w_in), conv_w=(conv_w, m_conv_w, v_conv_w), a_log=(a_log, m_a_log, v_a_log),
                  dt_bias=(dt_bias, m_dt_bias, v_dt_bias), dn_norm_g=(dn_norm_g, m_dn_norm_g, v_dn_norm_g),
                  w_mem_kv=(w_mem_kv, m_w_mem_kv, v_w_mem_kv), w_br_dn=(w_br_dn, m_w_br_dn, v_w_br_dn),
                  w_br_sb=(w_br_sb, m_w_br_sb, v_w_br_sb), w_br_mem=(w_br_mem, m_w_br_mem, v_w_br_mem),
                  w_out=(w_out, m_w_out, v_w_out), final_g=(final_g, m_final_g, v_final_g))
    order = list(params)
    deltas, new_m, new_v = {}, {}, {}
    deltas["w_in"], new_m["w_in"], new_v["w_in"] = (jnp.transpose(o, (1, 2, 0)) for o in _adamw(
        jnp.transpose(w_in, (2, 0, 1)), g_w_in_t[:, None, :], jnp.transpose(m_w_in, (2, 0, 1)),
        jnp.transpose(v_w_in, (2, 0, 1)), "adamw_w_in"))
    rest = [nm for nm in order if nm != "w_in"]

    def two_d(a):
        return a.reshape(1, -1) if a.ndim == 1 else a

    d_l, m_l, v_l = _adamw_many([two_d(params[nm][0]) for nm in rest], [two_d(grads[nm]) for nm in rest],
                                [two_d(params[nm][1]) for nm in rest], [two_d(params[nm][2]) for nm in rest], "adamw_rest")
    for k, nm in enumerate(rest):
        shp = params[nm][0].shape
        deltas[nm], new_m[nm], new_v[nm] = d_l[k].reshape(shp), m_l[k].reshape(shp), v_l[k].reshape(shp)
    return (loss, r["grad_x"][None], *[grads[nm] for nm in order], *[deltas[nm] for nm in order],
            *[new_m[nm] for nm in order], *[new_v[nm] for nm in order])
```

```python
import functools
import math

import jax
import jax.numpy as jnp
from jax import lax
from jax.experimental import pallas as pl
from jax.experimental.pallas import tpu as pltpu

F32 = jnp.float32
BF16 = jnp.bfloat16

D_MODEL = 1024
N_DEV = 8
N_HEADS = 8
D_HEAD = 128
DN_CHUNK = 64
CONV_K = 4
MEM_LEN = 256
MEM_HEADS = 4
MEM_DH = 64
MEM_W = MEM_HEADS * MEM_DH
NORM_EPS = 1e-6
IN_WIDTH = 11792
SHARD_W = IN_WIDTH // N_DEV

LANE = 128
SUPER = 2 * DN_CHUNK

O_QKV_DN = 0
O_Z_DN = 3072
O_QKV_SB = 4096
O_Z_SB = 7168
O_MQ = 8192
O_MZ = 8448
O_GATES = 8704
O_BA = 11776
W_AL = 11904
ORIG_BA = 4096
N_BA = 16

WIN_TILES = 13
WIN_W = WIN_TILES * LANE


def _aligned_col(o):
    return o if o < ORIG_BA else o - N_BA


WIN_START = tuple(min(_aligned_col(SHARD_W * d) // LANE, (W_AL // LANE) - WIN_TILES) for d in range(N_DEV))
WIN_OFF = tuple(_aligned_col(SHARD_W * d) - LANE * WIN_START[d] if SHARD_W * d >= ORIG_BA + N_BA or SHARD_W * d < ORIG_BA
                else None for d in range(N_DEV))
BA_DEV = ORIG_BA // SHARD_W
BA_LOCAL = ORIG_BA - BA_DEV * SHARD_W

ADAM_LR = 0.001
ADAM_B1 = 0.9
ADAM_B2 = 0.999
ADAM_EPS = 1e-08
ADAM_WD = 0.01
ADAM_STEP = 10

NN = (((1,), (0,)), ((), ()))
NT = (((1,), (1,)), ((), ()))
TN = (((0,), (0,)), ((), ()))


def _dot(a, b, dims):
    return lax.dot_general(a.astype(BF16), b.astype(BF16), dims, preferred_element_type=F32)


def _split2(a):
    hi = a.astype(BF16)
    lo = (a - hi.astype(F32)).astype(BF16)
    return hi, lo


def _dot3(a, b, dims):
    ah, al = _split2(a)
    bh, bl = _split2(b)
    d = functools.partial(lax.dot_general, dimension_numbers=dims, preferred_element_type=F32)
    return d(ah, bh) + (d(ah, bl) + d(al, bh))


def _sel_dot_impl(sel01, x, dims):
    sel = sel01.astype(BF16)
    h1 = x.astype(BF16)
    r1 = x - h1.astype(F32)
    h2 = r1.astype(BF16)
    h3 = (r1 - h2.astype(F32)).astype(BF16)
    d = functools.partial(lax.dot_general, dimension_numbers=dims, preferred_element_type=F32)
    return d(sel, h1) + (d(sel, h2) + d(sel, h3))


@jax.custom_vjp
def _sel_dot(sel01, x):
    return _sel_dot_impl(sel01, x, NN)


_sel_dot.defvjp(lambda s, x: (_sel_dot(s, x), s),
                lambda s, g: (jnp.zeros_like(s), _sel_dot_impl(s, g, TN)))


def _make_mm(dotfn):
    @jax.custom_vjp
    def nn(a, b):
        return dotfn(a, b, NN)

    @jax.custom_vjp
    def nt(a, b):
        return dotfn(a, b, NT)

    @jax.custom_vjp
    def tn(a, b):
        return dotfn(a, b, TN)

    nn.defvjp(lambda a, b: (nn(a, b), (a, b)), lambda r, g: (nt(g, r[1]), tn(r[0], g)))
    nt.defvjp(lambda a, b: (nt(a, b), (a, b)), lambda r, g: (nn(g, r[1]), tn(g, r[0])))
    tn.defvjp(lambda a, b: (tn(a, b), (a, b)), lambda r, g: (nt(r[1], g), nn(r[0], g)))
    return nn, nt, tn


mm_nn, mm_nt, mm_tn = _make_mm(_dot)
mm3_nn, mm3_nt, mm3_tn = _make_mm(_dot3)


def _sigmoid(x):
    return jax.nn.sigmoid(x)


def _silu(x):
    return x * _sigmoid(x)


def _softplus_parts(x):
    sp = jnp.log1p(jnp.exp(-jnp.abs(x)))
    return jnp.maximum(x, 0.0) + sp, jnp.maximum(-x, 0.0) + sp


def _rmsnorm(x, g):
    return x * lax.rsqrt(jnp.mean(x * x, axis=-1, keepdims=True) + NORM_EPS) * g


def _iota2(shape, dim):
    return lax.broadcasted_iota(jnp.int32, shape, dim)


def _div64(i):
    return lax.shift_right_logical(i, jnp.full(i.shape, 6, jnp.int32))


def _each(f, *lists):
    return [f(*a) for a in zip(*lists)]


@jax.custom_vjp
def _inv_unit_lower(ms):
    n = ms[0].shape[0]
    eye = (_iota2((n, n), 0) == _iota2((n, n), 1)).astype(F32)
    rs = [eye - m for m in ms]
    ps = ms
    for _ in range(5):
        ps = _each(mm3_nn, ps, ps)
        rs = _each(lambda r, p: r + mm3_nn(r, p), rs, ps)
    return rs


def _inv_fwd(ms):
    rs = _inv_unit_lower(ms)
    return rs, rs


def _inv_bwd(rs, gs):
    ts = _each(mm3_tn, rs, gs)
    return (_each(lambda t, r: -mm3_nt(t, r), ts, rs),)


_inv_unit_lower.defvjp(_inv_fwd, _inv_bwd)


def _dn_block(cq, ck, cv, bcol, acol, zt, alog, dtb, gn, s0):
    n = SUPER
    h = DN_CHUNK
    row = _iota2((n, n), 0)
    col = _iota2((n, n), 1)
    same = _div64(row) == _div64(col)
    incl = jnp.logical_and(same, row >= col)
    strict = jnp.logical_and(same, row > col)
    incl_f = incl.astype(F32)

    qn = _each(lambda x: x * lax.rsqrt(jnp.sum(x * x, axis=-1, keepdims=True) + NORM_EPS) * (D_HEAD ** -0.5), cq)
    kn = _each(lambda x: x * lax.rsqrt(jnp.sum(x * x, axis=-1, keepdims=True) + NORM_EPS), ck)
    beta = _each(_sigmoid, bcol)
    g = _each(lambda al, ac, dt: -(jnp.exp(al) * _softplus_parts(ac + dt)[0]), alog, acol, dtb)
    gcum = _each(lambda x: _sel_dot(incl_f, jnp.broadcast_to(x, (n, n))), g)
    gam_incl = _each(lambda x: jnp.where(incl, jnp.exp(jnp.where(incl, x - x.T, 0.0)), 0.0), gcum)
    kk = _each(mm_nt, kn, kn)
    t_inv = _inv_unit_lower(_each(lambda b, x, gm: b * x * jnp.where(strict, gm, 0.0), beta, kk, gam_incl))
    eg = _each(jnp.exp, gcum)
    u = _each(lambda t, v, b: mm_nn(t, v * b), t_inv, cv, beta)
    w = _each(lambda t, k, b, e: mm_nn(t, k * (b * e)), t_inv, kn, beta, eg)
    a_intra = _each(lambda q, k, gm: mm_nt(q, k) * gm, qn, kn, gam_incl)
    q_dec = _each(lambda q, e: q * e, qn, eg)
    last0 = _each(lambda x: x[h - 1:h, :], gcum)
    last1 = _each(lambda x: x[n - 1:n, :], gcum)
    k_dec = _each(lambda k, x, l0, l1: k * jnp.exp(jnp.concatenate(
        [jnp.broadcast_to(l0, (h, n)), jnp.broadcast_to(l1, (h, n))], axis=0) - x), kn, gcum, last0, last1)
    v0 = _each(lambda uu, ww, s: uu[:h] - mm_nn(ww[:h], s), u, w, s0)
    o0 = _each(lambda q, s: mm_nn(q[:h], s), q_dec, s0)
    s1 = _each(lambda s, l0, k, v: s * jnp.exp(l0) + mm_tn(k[:h], v), s0, last0, k_dec, v0)
    v1 = _each(lambda uu, ww, s: uu[h:] - mm_nn(ww[h:], s), u, w, s1)
    o1 = _each(lambda q, s: mm_nn(q[h:], s), q_dec, s1)
    s2 = _each(lambda s, l1, k, v: s * jnp.exp(l1) + mm_tn(k[h:], v), s1, last1, k_dec, v1)
    o = _each(lambda a, b, am, x, y: jnp.concatenate([a, b], axis=0) + mm_nn(am, jnp.concatenate([x, y], axis=0)),
              o0, o1, a_intra, v0, v1)
    out = _each(lambda x, z: _rmsnorm(x, gn) * _silu(z), o, zt)
    return out, s2


def _mem_fn(mq, mz, mkv):
    mk = mkv[:, :MEM_W]
    mv = mkv[:, MEM_W:]
    lane = _iota2((1, MEM_W), 1)
    out = jnp.zeros(mq.shape, F32)
    for hd in range(MEM_HEADS):
        hm = (_div64(lane) == hd).astype(F32)
        s = mm_nt(mq * hm, mk) * (1.0 / math.sqrt(MEM_DH))
        s = s - jnp.max(s, axis=-1, keepdims=True)
        e = jnp.exp(s)
        p = e / jnp.sum(e, axis=-1, keepdims=True)
        out = out + mm_nn(p, mv) * hm
    return out * _silu(mz)


def _merge_fn(gd, gs, gm, yd, ys, ym):
    return _sigmoid(gd) * yd + _sigmoid(gs) * ys + _sigmoid(gm) * ym


def _loss_fn(x, mo, fg, tgt):
    y = _rmsnorm(x + mo, fg)
    err = y - tgt
    return 0.5 * jnp.sum(jnp.mean(err * err, axis=-1, keepdims=True), axis=0, keepdims=True)


def _matmul(a, b, mode, out_dtype, tm, tn, tk, name, b_col0=0, n_cols=None):
    if mode == "nn":
        m, kdim = a.shape
        n = b.shape[1] if n_cols is None else n_cols
    elif mode == "nt":
        m, kdim = a.shape
        n = b.shape[0]
    else:
        kdim, m = a.shape
        n = b.shape[1] if n_cols is None else n_cols
    tm, tn, tk = min(tm, m), min(tn, n), min(tk, kdim)
    assert m % tm == 0 and n % tn == 0 and kdim % tk == 0 and b_col0 % tn == 0
    nk = kdim // tk
    jb = b_col0 // tn
    dims = {"nn": NN, "nt": NT, "tn": TN}[mode]

    def body(a_ref, b_ref, o_ref, acc_ref):
        k = pl.program_id(2)
        part = _dot(a_ref[...], b_ref[...], dims)

        @pl.when(k == 0)
        def _():
            acc_ref[...] = part

        @pl.when(k > 0)
        def _():
            acc_ref[...] += part

        @pl.when(k == nk - 1)
        def _():
            o_ref[...] = acc_ref[...].astype(o_ref.dtype)

    if mode == "nn":
        a_spec = pl.BlockSpec((tm, tk), lambda i, j, k: (i, k))
        b_spec = pl.BlockSpec((tk, tn), lambda i, j, k: (k, j + jb))
    elif mode == "nt":
        a_spec = pl.BlockSpec((tm, tk), lambda i, j, k: (i, k))
        b_spec = pl.BlockSpec((tn, tk), lambda i, j, k: (j, k))
    else:
        a_spec = pl.BlockSpec((tk, tm), lambda i, j, k: (k, i))
        b_spec = pl.BlockSpec((tk, tn), lambda i, j, k: (k, j + jb))
    return pl.pallas_call(
        body,
        name=name,
        grid=(m // tm, n // tn, nk),
        in_specs=[a_spec, b_spec],
        out_specs=pl.BlockSpec((tm, tn), lambda i, j, k: (i, j)),
        out_shape=jax.ShapeDtypeStruct((m, n), out_dtype),
        scratch_shapes=[pltpu.VMEM((tm, tn), F32)],
        compiler_params=pltpu.CompilerParams(dimension_semantics=("parallel", "parallel", "arbitrary")),
    )(a, b)


def _norm_in(x, g, tm=256):
    t = x.shape[0]

    def body(x_ref, g_ref, h_ref):
        h_ref[...] = _rmsnorm(x_ref[...], g_ref[...]).astype(BF16)

    return pl.pallas_call(
        body,
        name="norm_in",
        grid=(t // tm,),
        in_specs=[pl.BlockSpec((tm, D_MODEL), lambda i: (i, 0)), pl.BlockSpec((1, D_MODEL), lambda i: (0, 0))],
        out_specs=pl.BlockSpec((tm, D_MODEL), lambda i: (i, 0)),
        out_shape=jax.ShapeDtypeStruct((t, D_MODEL), BF16),
    )(x, g)


def _norm_in_bwd(x, g, dh, dres, tm=256):
    t = x.shape[0]

    def body(x_ref, g_ref, dh_ref, dres_ref, dx_ref, dg_ref):
        _, vjp = jax.vjp(_rmsnorm, x_ref[...], g_ref[...])
        dx, dg = vjp(dh_ref[...])
        dx_ref[...] = dx + dres_ref[...]

        @pl.when(pl.program_id(0) == 0)
        def _():
            dg_ref[...] = jnp.zeros_like(dg_ref)

        dg_ref[...] += dg

    row = pl.BlockSpec((tm, D_MODEL), lambda i: (i, 0))
    vec = pl.BlockSpec((1, D_MODEL), lambda i: (0, 0))
    return pl.pallas_call(
        body,
        name="norm_in_bwd",
        grid=(t // tm,),
        in_specs=[row, vec, row, row],
        out_specs=[row, vec],
        out_shape=[jax.ShapeDtypeStruct((t, D_MODEL), F32), jax.ShapeDtypeStruct((1, D_MODEL), F32)],
    )(x, g, dh, dres)


def _merge(proj, yd, ys, ym, tm=256, tc=512):
    t = proj.shape[0]
    g0 = O_GATES // tc
    gstep = D_MODEL // tc

    def body(gd, gs, gm, yd_ref, ys_ref, ym_ref, o_ref):
        o_ref[...] = _merge_fn(gd[...], gs[...], gm[...], yd_ref[...], ys_ref[...], ym_ref[...]).astype(BF16)

    def gate(k):
        return pl.BlockSpec((tm, tc), lambda i, j: (i, g0 + k * gstep + j))

    blk = pl.BlockSpec((tm, tc), lambda i, j: (i, j))
    return pl.pallas_call(
        body,
        name="merge",
        grid=(t // tm, D_MODEL // tc),
        in_specs=[gate(0), gate(1), gate(2), blk, blk, blk],
        out_specs=blk,
        out_shape=jax.ShapeDtypeStruct((t, D_MODEL), BF16),
    )(proj, proj, proj, yd, ys, ym)


def _merge_bwd(proj, yd, ys, ym, dmerged, tm=256, tc=512):
    t = proj.shape[0]
    g0 = O_GATES // tc
    gstep = D_MODEL // tc

    def body(gd, gs, gm, yd_ref, ys_ref, ym_ref, dm_ref, dyd, dys, dym, dgd, dgs, dgm):
        _, vjp = jax.vjp(_merge_fn, gd[...], gs[...], gm[...], yd_ref[...], ys_ref[...], ym_ref[...])
        outs = vjp(dm_ref[...])
        for ref, val in zip((dgd, dgs, dgm, dyd, dys, dym), outs):
            ref[...] = val.astype(BF16)

    def gate(k):
        return pl.BlockSpec((tm, tc), lambda i, j: (i, g0 + k * gstep + j))

    blk = pl.BlockSpec((tm, tc), lambda i, j: (i, j))
    o = jax.ShapeDtypeStruct((t, D_MODEL), BF16)
    return pl.pallas_call(
        body,
        name="merge_bwd",
        grid=(t // tm, D_MODEL // tc),
        in_specs=[gate(0), gate(1), gate(2), blk, blk, blk, blk],
        out_specs=[blk] * 6,
        out_shape=[o] * 6,
    )(proj, proj, proj, yd, ys, ym, dmerged)


def _loss_head(x, mo, fg, tgt, tm=256):
    t = x.shape[0]

    def body(x_ref, mo_ref, fg_ref, t_ref, loss_ref, dout_ref, dfg_ref):
        loss, vjp = jax.vjp(_loss_fn, x_ref[...], mo_ref[...], fg_ref[...], t_ref[...])
        _, dmo, dfg, _ = vjp(jnp.ones((1, 1), F32))

        @pl.when(pl.program_id(0) == 0)
        def _():
            loss_ref[...] = jnp.zeros_like(loss_ref)
            dfg_ref[...] = jnp.zeros_like(dfg_ref)

        loss_ref[...] += jnp.broadcast_to(loss, loss_ref.shape)
        dfg_ref[...] += dfg
        dout_ref[...] = dmo

    row = pl.BlockSpec((tm, D_MODEL), lambda i: (i, 0))
    vec = pl.BlockSpec((1, D_MODEL), lambda i: (0, 0))
    return pl.pallas_call(
        body,
        name="loss_head",
        grid=(t // tm,),
        in_specs=[row, row, vec, row],
        out_specs=[pl.BlockSpec((1, LANE), lambda i: (0, 0)), row, vec],
        out_shape=[jax.ShapeDtypeStruct((1, LANE), F32), jax.ShapeDtypeStruct((t, D_MODEL), F32),
                   jax.ShapeDtypeStruct((1, D_MODEL), F32)],
    )(x, mo, fg, tgt)


def _shift_rows(x, s):
    t = x.shape[0]
    if s == 0:
        return x
    rolled = pltpu.roll(x, s % t, 0)
    row = _iota2(x.shape, 0)
    keep = row >= s if s > 0 else row < t + s
    return jnp.where(keep, rolled, 0.0)


def _conv_pre(x, w):
    return sum(_shift_rows(x, CONV_K - 1 - j) * w[j:j + 1, :] for j in range(CONV_K))


CONV_TC = 256


def _dn_conv(proj, conv_w):
    t = proj.shape[0]
    nb = 3 * D_MODEL // CONV_TC

    def body(x_ref, w_ref, c_ref):
        c_ref[...] = _silu(_conv_pre(x_ref[...], w_ref[...]))

    return pl.pallas_call(
        body,
        name="dn_conv",
        grid=(nb,),
        in_specs=[pl.BlockSpec((t, CONV_TC), lambda j: (0, j)), pl.BlockSpec((CONV_K, CONV_TC), lambda j: (0, j))],
        out_specs=pl.BlockSpec((t, CONV_TC), lambda j: (0, j)),
        out_shape=jax.ShapeDtypeStruct((t, 3 * D_MODEL), F32),
    )(proj, conv_w)


def _dn_conv_bwd(proj, conv_w, dc, part):
    t = proj.shape[0]
    nb = D_MODEL // CONV_TC
    b0 = part * nb

    def body(x_ref, w_ref, dc_ref, dx_ref, dw_ref):
        x = x_ref[...]
        w = w_ref[...]
        pre = _conv_pre(x, w)
        sg = _sigmoid(pre)
        dpre = dc_ref[...] * (sg * (1.0 + pre * (1.0 - sg)))
        dx = sum(_shift_rows(dpre, -(CONV_K - 1 - j)) * w[j:j + 1, :] for j in range(CONV_K))
        dx_ref[...] = dx.astype(BF16)
        dw_ref[...] = jnp.concatenate(
            [jnp.sum(dpre * _shift_rows(x, CONV_K - 1 - j), axis=0, keepdims=True) for j in range(CONV_K)], axis=0)

    blk = pl.BlockSpec((t, CONV_TC), lambda j: (0, j))
    return pl.pallas_call(
        body,
        name=f"dn_conv_bwd{part}",
        grid=(nb,),
        in_specs=[pl.BlockSpec((t, CONV_TC), lambda j: (0, b0 + j)),
                  pl.BlockSpec((CONV_K, CONV_TC), lambda j: (0, b0 + j)), blk],
        out_specs=[blk, pl.BlockSpec((CONV_K, CONV_TC), lambda j: (0, j))],
        out_shape=[jax.ShapeDtypeStruct((t, D_MODEL), BF16), jax.ShapeDtypeStruct((CONV_K, D_MODEL), F32)],
    )(proj, conv_w, dc)


def _ba_columns(ba, hd):
    lane = _iota2(ba.shape, 1)
    bcol = jnp.sum(jnp.where(lane == hd, ba, 0.0), axis=1, keepdims=True)
    acol = jnp.sum(jnp.where(lane == N_HEADS + hd, ba, 0.0), axis=1, keepdims=True)
    return bcol, acol


def _head_scalar(row, hd):
    lane = _iota2(row.shape, 1)
    return jnp.sum(jnp.where(lane == hd, row, 0.0), axis=1, keepdims=True)


DN_HP = 4


def _dn_inputs(cq, ck, cv, ba_ref, z_ref, alog_ref, dtb_ref, heads, lanes):
    ba = ba_ref[...]
    cols = [_ba_columns(ba, hd) for hd in heads]
    return ([cq[:, ln] for ln in lanes], [ck[:, ln] for ln in lanes], [cv[:, ln] for ln in lanes],
            [c[0] for c in cols], [c[1] for c in cols], [z_ref[:, ln] for ln in lanes],
            [_head_scalar(alog_ref[...], hd) for hd in heads], [_head_scalar(dtb_ref[...], hd) for hd in heads])


def _dn_specs(nblk, reverse):
    w = DN_HP * LANE
    nq = D_MODEL // w

    def row(i):
        return nblk - 1 - i if reverse else i

    def colblk(b0):
        return pl.BlockSpec((SUPER, w), lambda i, h: (row(i), b0 + h))

    ba = pl.BlockSpec((SUPER, LANE), lambda i, h: (row(i), O_BA // LANE))
    vec = pl.BlockSpec((1, LANE), lambda i, h: (0, 0))
    st = pl.BlockSpec((1, DN_HP, D_HEAD, D_HEAD), lambda i, h: (row(i), h, 0, 0))
    return colblk, nq, ba, vec, st


def _dn_fwd(c, proj, alog_row, dtb_row, gn):
    t = c.shape[0]
    nblk = t // SUPER
    colblk, nq, ba, vec, st = _dn_specs(nblk, False)

    def body(cq, ck, cv, ba_ref, z_ref, alog_ref, dtb_ref, gn_ref, o_ref, s_ref, state):
        @pl.when(jnp.logical_and(pl.program_id(0) == 0, pl.program_id(1) == 0))
        def _():
            state[...] = jnp.zeros_like(state)

        heads = [pl.program_id(1) * DN_HP + j for j in range(DN_HP)]
        lanes = [slice(j * LANE, (j + 1) * LANE) for j in range(DN_HP)]
        s0 = [state[hd] for hd in heads]
        outs, s2 = _dn_block(*_dn_inputs(cq, ck, cv, ba_ref, z_ref, alog_ref, dtb_ref, heads, lanes), gn_ref[...], s0)
        for j, (hd, ln) in enumerate(zip(heads, lanes)):
            s_ref[0, j] = s0[j]
            o_ref[:, ln] = outs[j].astype(BF16)
            state[hd] = s2[j]

    return pl.pallas_call(
        body,
        name="dn_fwd",
        grid=(nblk, N_HEADS // DN_HP),
        in_specs=[colblk(0), colblk(nq), colblk(2 * nq), ba, colblk(O_Z_DN // (DN_HP * LANE)), vec, vec, vec],
        out_specs=[colblk(0), st],
        out_shape=[jax.ShapeDtypeStruct((t, D_MODEL), BF16),
                   jax.ShapeDtypeStruct((nblk, N_HEADS, D_HEAD, D_HEAD), F32)],
        scratch_shapes=[pltpu.VMEM((N_HEADS, D_HEAD, D_HEAD), F32)],
    )(c, c, c, proj, proj, alog_row, dtb_row, gn)


def _dn_bwd(c, proj, alog_row, dtb_row, gn, states, do):
    t = c.shape[0]
    nblk = t // SUPER
    colblk, nq, ba, vec, st = _dn_specs(nblk, True)

    def body(cq, ck, cv, ba_ref, z_ref, alog_ref, dtb_ref, gn_ref, s_ref, do_ref,
             dq_ref, dk_ref, dv_ref, dz_ref, dba_ref, dsc_ref, dgn_ref, dstate):
        i = pl.program_id(0)
        hq = pl.program_id(1)

        @pl.when(jnp.logical_and(i == 0, hq == 0))
        def _():
            dstate[...] = jnp.zeros_like(dstate)
            dsc_ref[...] = jnp.zeros_like(dsc_ref)
            dgn_ref[...] = jnp.zeros_like(dgn_ref)

        @pl.when(hq == 0)
        def _():
            dba_ref[...] = jnp.zeros_like(dba_ref)

        lane = _iota2((SUPER, LANE), 1)
        lane1 = _iota2((1, LANE), 1)
        heads = [hq * DN_HP + j for j in range(DN_HP)]
        lanes = [slice(j * LANE, (j + 1) * LANE) for j in range(DN_HP)]
        ds_in = [dstate[hd] for hd in heads]
        s_in = [s_ref[0, j] for j in range(DN_HP)]
        _, vjp = jax.vjp(_dn_block, *_dn_inputs(cq, ck, cv, ba_ref, z_ref, alog_ref, dtb_ref, heads, lanes),
                         gn_ref[...], s_in)
        dq, dk, dv, dbc, dac, dz, dal, ddt, dgn, ds0 = vjp(([do_ref[:, ln].astype(F32) for ln in lanes], ds_in))
        dba = jnp.zeros((SUPER, LANE), F32)
        dal_row = jnp.zeros((1, LANE), F32)
        ddt_row = jnp.zeros((1, LANE), F32)
        for j, (hd, ln) in enumerate(zip(heads, lanes)):
            dq_ref[:, ln] = dq[j]
            dk_ref[:, ln] = dk[j]
            dv_ref[:, ln] = dv[j]
            dz_ref[:, ln] = dz[j].astype(BF16)
            dstate[hd] = ds0[j]
            dba = dba + jnp.where(lane == hd, dbc[j], 0.0) + jnp.where(lane == N_HEADS + hd, dac[j], 0.0)
            dal_row = dal_row + jnp.where(lane1 == hd, dal[j], 0.0)
            ddt_row = ddt_row + jnp.where(lane1 == hd, ddt[j], 0.0)
        dba_ref[...] += dba
        dsc_ref[0:1, :] += dal_row
        dsc_ref[1:2, :] += ddt_row
        dgn_ref[...] += dgn

    outs = pl.pallas_call(
        body,
        name="dn_bwd",
        grid=(nblk, N_HEADS // DN_HP),
        in_specs=[colblk(0), colblk(nq), colblk(2 * nq), ba, colblk(O_Z_DN // (DN_HP * LANE)), vec, vec, vec, st,
                  colblk(0)],
        out_specs=[colblk(0), colblk(0), colblk(0), colblk(0),
                   pl.BlockSpec((SUPER, LANE), lambda i, h: (nblk - 1 - i, 0)),
                   pl.BlockSpec((2, LANE), lambda i, h: (0, 0)), vec],
        out_shape=[jax.ShapeDtypeStruct((t, D_MODEL), F32)] * 3
        + [jax.ShapeDtypeStruct((t, D_MODEL), BF16), jax.ShapeDtypeStruct((t, LANE), F32),
           jax.ShapeDtypeStruct((2, LANE), F32), jax.ShapeDtypeStruct((1, LANE), F32)],
        scratch_shapes=[pltpu.VMEM((N_HEADS, D_HEAD, D_HEAD), F32)],
    )(c, c, c, proj, proj, alog_row, dtb_row, gn, states, do)
    return outs


SB_TQ = 256
SB_TK = 256
SB_HP = 4


def _sb_logits(z, mask):
    sp = jnp.log(1.0 + jnp.exp(-jnp.abs(z)))
    lf_raw = -(jnp.maximum(z, 0.0) + sp)
    lb = lf_raw + z
    lf = lf_raw if mask is None else jnp.where(mask, lf_raw, 0.0)
    return lb, lf_raw, lf


def _suffix_sums(x, sel):
    hi, lo = _split2(x)
    d = functools.partial(lax.dot_general, dimension_numbers=NN, preferred_element_type=F32)
    return d(hi, sel) + d(lo, sel)


def _sb_diag_mask(tq, r):
    return r * SB_TK + _iota2((tq, SB_TK), 1) < _iota2((tq, SB_TK), 0)


def _sb_specs(t, tq):
    w = SB_HP * LANE
    q0, k0, v0, z0 = (O_QKV_SB // w, (O_QKV_SB + D_MODEL) // w, (O_QKV_SB + 2 * D_MODEL) // w, O_Z_SB // w)

    def blk(b0):
        return pl.BlockSpec((tq, w), lambda h, i: (i, b0 + h))

    def full(b0, **kw):
        return pl.BlockSpec((t, w), lambda h, i: (0, b0 + h), **kw)

    once = dict(pipeline_mode=pl.Buffered(1))
    return blk(q0), full(k0, **once), full(v0, **once), blk(z0), blk(0), full(0)


def _sb_fwd(proj):
    t = proj.shape[0]
    tq = min(SB_TQ, t)
    ndiag = tq // SB_TK
    scale = 1.0 / math.sqrt(D_HEAD)

    def body(q_ref, k_ref, v_ref, z_ref, o_ref, oraw_ref):
        qi = pl.program_id(1)
        lanes = [slice(hd * LANE, (hd + 1) * LANE) for hd in range(SB_HP)]
        qs = [(q_ref[:, ln] * scale).astype(BF16) for ln in lanes]
        after = (_iota2((SB_TK, SB_TK), 0) > _iota2((SB_TK, SB_TK), 1)).astype(BF16)
        oraw_ref[...] = jnp.zeros_like(oraw_ref)

        def block(kb, mask, c_lf):
            rows = pl.ds(pl.multiple_of(kb * SB_TK, SB_TK), SB_TK)
            z = _each(lambda q, ln: _dot(q, k_ref[rows, ln], NT), qs, lanes)
            lg = _each(lambda x: _sb_logits(x, mask), z)
            surv = _each(lambda x: _suffix_sums(x[2], after), lg)
            att = _each(lambda x, s, c: jnp.exp(x[0] + s + c), lg, surv, c_lf)
            if mask is not None:
                att = _each(lambda a: jnp.where(mask, a, 0.0), att)
            pv = _each(lambda a, ln: _dot(a, v_ref[rows, ln], NN), att, lanes)
            for p, ln in zip(pv, lanes):
                oraw_ref[:, ln] += p
            return tuple(_each(lambda c, x: c + jnp.sum(x[2], axis=1, keepdims=True), c_lf, lg))

        carry = tuple(jnp.zeros((tq, 1), F32) for _ in range(SB_HP))
        for r in reversed(range(ndiag)):
            carry = block(qi * ndiag + r, _sb_diag_mask(tq, r), carry)
        lax.fori_loop(0, qi * ndiag, lambda i, c: block(qi * ndiag - 1 - i, None, c), carry)
        o_ref[...] = (oraw_ref[...] * _silu(z_ref[...])).astype(BF16)

    q_spec, k_spec, v_spec, z_spec, out, _ = _sb_specs(t, tq)
    return pl.pallas_call(
        body,
        name="sb_fwd",
        grid=(N_HEADS // SB_HP, t // tq),
        in_specs=[q_spec, k_spec, v_spec, z_spec],
        out_specs=[out, out],
        out_shape=[jax.ShapeDtypeStruct((t, D_MODEL), BF16), jax.ShapeDtypeStruct((t, D_MODEL), F32)],
    )(proj, proj, proj, proj)


def _sb_bwd(proj, oraw, do):
    t = proj.shape[0]
    tq = min(SB_TQ, t)
    ndiag = tq // SB_TK
    scale = 1.0 / math.sqrt(D_HEAD)

    def body(q_ref, k_ref, v_ref, z_ref, oraw_ref, do_ref, dq_ref, dk_ref, dv_ref, dz_ref, dk_acc, dv_acc,
             p_scr, z_scr):
        qi = pl.program_id(1)
        nq = pl.num_programs(1)

        @pl.when(qi == 0)
        def _():
            dk_acc[...] = jnp.zeros_like(dk_acc)
            dv_acc[...] = jnp.zeros_like(dv_acc)

        heads = range(SB_HP)
        lanes = [slice(hd * LANE, (hd + 1) * LANE) for hd in heads]
        zg = z_ref[...]
        sg = _sigmoid(zg)
        dog = do_ref[...].astype(F32)
        dz_ref[...] = (dog * oraw_ref[...] * (sg * (1.0 + zg * (1.0 - sg)))).astype(BF16)
        d_o = (dog * (zg * sg)).astype(BF16)
        d_o16 = [d_o[:, ln] for ln in lanes]
        qs = [(q_ref[:, ln] * scale).astype(BF16) for ln in lanes]
        ri = _iota2((SB_TK, SB_TK), 0)
        ci = _iota2((SB_TK, SB_TK), 1)
        after = (ri > ci).astype(BF16)
        earlier = (ri < ci).astype(BF16)

        def rows_of(kb):
            return pl.ds(pl.multiple_of(kb * SB_TK, SB_TK), SB_TK)

        def down(kb, mask, c_lf):
            rows = rows_of(kb)
            z = _each(lambda q, ln: _dot(q, k_ref[rows, ln], NT), qs, lanes)
            da = _each(lambda d, ln: _dot(d, v_ref[rows, ln], NT), d_o16, lanes)
            lg = _each(lambda x: _sb_logits(x, mask), z)
            surv = _each(lambda x: _suffix_sums(x[2], after), lg)
            att = _each(lambda x, s, c: jnp.exp(x[0] + s + c), lg, surv, c_lf)
            if mask is not None:
                att = _each(lambda a: jnp.where(mask, a, 0.0), att)
            dv = _each(lambda a, d: _dot(a, d, TN), att, d_o16)
            for hd in heads:
                p_scr[hd, kb] = att[hd] * da[hd]
                z_scr[hd, kb] = z[hd]
                dv_acc[rows, lanes[hd]] += dv[hd]
            return tuple(_each(lambda c, x: c + jnp.sum(x[2], axis=1, keepdims=True), c_lf, lg))

        c_lf = tuple(jnp.zeros((tq, 1), F32) for _ in heads)
        for r in reversed(range(ndiag)):
            c_lf = down(qi * ndiag + r, _sb_diag_mask(tq, r), c_lf)
        lax.fori_loop(0, qi * ndiag, lambda i, c: down(qi * ndiag - 1 - i, None, c), c_lf)

        def up(kb, mask, carry):
            dq, c_p = carry
            rows = rows_of(kb)
            p = [p_scr[hd, kb] for hd in heads]
            zs = [z_scr[hd, kb] for hd in heads]
            before = _each(lambda x, c: _suffix_sums(x, earlier) + c, p, c_p)
            e = _each(lambda x: jnp.exp(-jnp.abs(x)), zs)
            r = _each(lambda x: 1.0 / (1.0 + x), e)
            sig = _each(lambda x, a, b: jnp.where(x >= 0.0, b, a * b), zs, e, r)
            oms = _each(lambda x, a, b: jnp.where(x >= 0.0, a * b, b), zs, e, r)
            if mask is not None:
                sig = _each(lambda a: jnp.where(mask, a, 0.0), sig)
            dzz = _each(lambda x, o, g, b: x * o - g * b, p, oms, sig, before)
            dk = _each(lambda x, q: _dot(x, q, TN), dzz, qs)
            dq = _each(lambda a, x, ln: a + _dot(x, k_ref[rows, ln], NN), dq, dzz, lanes)
            for hd in heads:
                dk_acc[rows, lanes[hd]] += dk[hd]
            return tuple(dq), tuple(_each(lambda c, x: c + jnp.sum(x, axis=1, keepdims=True), c_p, p))

        carry = (tuple(jnp.zeros((tq, D_HEAD), F32) for _ in heads), tuple(jnp.zeros((tq, 1), F32) for _ in heads))
        carry = lax.fori_loop(0, qi * ndiag, lambda kb, c: up(kb, None, c), carry)
        for r in range(ndiag):
            carry = up(qi * ndiag + r, _sb_diag_mask(tq, r), carry)
        dq = carry[0]
        for hd in heads:
            dq_ref[:, lanes[hd]] = (dq[hd] * scale).astype(BF16)

        @pl.when(qi == nq - 1)
        def _():
            dk_ref[...] = dk_acc[...].astype(BF16)
            dv_ref[...] = dv_acc[...].astype(BF16)

    q_spec, k_spec, v_spec, z_spec, blk, full = _sb_specs(t, tq)
    o = jax.ShapeDtypeStruct((t, D_MODEL), BF16)
    w = SB_HP * LANE
    return pl.pallas_call(
        body,
        name="sb_bwd",
        grid=(N_HEADS // SB_HP, t // tq),
        in_specs=[q_spec, k_spec, v_spec, z_spec, blk, blk],
        out_specs=[blk, full, full, blk],
        out_shape=[o, o, o, o],
        scratch_shapes=[pltpu.VMEM((t, w), F32), pltpu.VMEM((t, w), F32)]
        + [pltpu.VMEM((SB_HP, t // SB_TK, tq, SB_TK), F32)] * 2,
    )(proj, proj, proj, proj, oraw, do)


def _mem_kv_fn(mem, mg, w):
    return mm_nn(_rmsnorm(mem, mg), w)


def _mem_kv(mem, mg, w):
    def body(m_ref, g_ref, w_ref, o_ref):
        o_ref[...] = _mem_kv_fn(m_ref[...], g_ref[...], w_ref[...])

    return pl.pallas_call(body, name="mem_kv", out_shape=jax.ShapeDtypeStruct((MEM_LEN, 2 * MEM_W), F32))(mem, mg, w)


def _mem_kv_bwd(mem, mg, w, dmkv):
    def body(m_ref, g_ref, w_ref, d_ref, dg_ref, dw_ref):
        _, vjp = jax.vjp(_mem_kv_fn, m_ref[...], g_ref[...], w_ref[...].astype(F32))
        _, dg, dw = vjp(d_ref[...])
        dg_ref[...] = dg
        dw_ref[...] = dw.astype(BF16)

    return pl.pallas_call(
        body, name="mem_kv_bwd",
        out_shape=[jax.ShapeDtypeStruct((1, D_MODEL), F32), jax.ShapeDtypeStruct((D_MODEL, 2 * MEM_W), BF16)],
    )(mem, mg, w, dmkv)


def _mem_attn(proj, mkv, tm=256):
    t = proj.shape[0]
    tm = min(tm, t)

    def body(q_ref, z_ref, kv_ref, o_ref):
        o_ref[...] = _mem_fn(q_ref[...], z_ref[...], kv_ref[...]).astype(BF16)

    return pl.pallas_call(
        body,
        name="mem_attn",
        grid=(t // tm,),
        in_specs=[pl.BlockSpec((tm, MEM_W), lambda i: (i, O_MQ // MEM_W)),
                  pl.BlockSpec((tm, MEM_W), lambda i: (i, O_MZ // MEM_W)),
                  pl.BlockSpec((MEM_LEN, 2 * MEM_W), lambda i: (0, 0))],
        out_specs=pl.BlockSpec((tm, MEM_W), lambda i: (i, 0)),
        out_shape=jax.ShapeDtypeStruct((t, MEM_W), BF16),
    )(proj, proj, mkv)


def _mem_attn_bwd(proj, mkv, do, tm=256):
    t = proj.shape[0]
    tm = min(tm, t)

    def body(q_ref, z_ref, kv_ref, do_ref, dq_ref, dz_ref, dkv_ref):
        _, vjp = jax.vjp(_mem_fn, q_ref[...], z_ref[...], kv_ref[...])
        dq, dz, dkv = vjp(do_ref[...].astype(F32))
        dq_ref[...] = dq.astype(BF16)
        dz_ref[...] = dz.astype(BF16)

        @pl.when(pl.program_id(0) == 0)
        def _():
            dkv_ref[...] = jnp.zeros_like(dkv_ref)

        dkv_ref[...] += dkv

    blk = pl.BlockSpec((tm, MEM_W), lambda i: (i, 0))
    kv = pl.BlockSpec((MEM_LEN, 2 * MEM_W), lambda i: (0, 0))
    return pl.pallas_call(
        body,
        name="mem_attn_bwd",
        grid=(t // tm,),
        in_specs=[pl.BlockSpec((tm, MEM_W), lambda i: (i, O_MQ // MEM_W)),
                  pl.BlockSpec((tm, MEM_W), lambda i: (i, O_MZ // MEM_W)), kv, blk],
        out_specs=[blk, blk, kv],
        out_shape=[jax.ShapeDtypeStruct((t, MEM_W), BF16), jax.ShapeDtypeStruct((t, MEM_W), BF16),
                   jax.ShapeDtypeStruct((MEM_LEN, 2 * MEM_W), F32)],
    )(proj, proj, mkv, do)


def _local_step(x, mem, tgt, norm_g, mem_norm_g, w_alt, conv_w, alog_row, dtb_row, dn_norm_g, w_mem_kv, w_br_dn, w_br_sb,
                w_br_mem, w_out, final_g):
    h = _norm_in(x, norm_g)
    proj = _matmul(h, w_alt, "nt", F32, 2048, 384, 1024, "proj")

    c = _dn_conv(proj, conv_w)
    o_dn, states = _dn_fwd(c, proj, alog_row, dtb_row, dn_norm_g)
    o_sb, o_sb_raw = _sb_fwd(proj)
    mkv = _mem_kv(mem, mem_norm_g, w_mem_kv)
    o_m = _mem_attn(proj, mkv)

    y_dn = _matmul(o_dn, w_br_dn, "nn", F32, 512, 1024, 1024, "y_dn")
    y_sb = _matmul(o_sb, w_br_sb, "nn", F32, 512, 1024, 1024, "y_sb")
    y_m = _matmul(o_m, w_br_mem, "nn", F32, 512, 1024, 1024, "y_m")
    merged = _merge(proj, y_dn, y_sb, y_m)
    mo = _matmul(merged, w_out, "nn", F32, 512, 1024, 1024, "mo")
    loss, dout, d_final_g = _loss_head(x, mo, final_g, tgt)

    dmerged = _matmul(dout, w_out, "nt", F32, 512, 1024, 1024, "dmerged")
    dw_out = _matmul(merged, dout, "tn", BF16, 256, 1024, 2048, "dw_out")
    dy_dn, dy_sb, dy_m, dg_dn, dg_sb, dg_m = _merge_bwd(proj, y_dn, y_sb, y_m, dmerged)
    do_dn = _matmul(dy_dn, w_br_dn, "nt", BF16, 512, 1024, 1024, "do_dn")
    do_sb = _matmul(dy_sb, w_br_sb, "nt", BF16, 512, 1024, 1024, "do_sb")
    do_m = _matmul(dy_m, w_br_mem, "nt", BF16, 512, 256, 1024, "do_m")
    dw_br_dn = _matmul(o_dn, dy_dn, "tn", BF16, 256, 1024, 2048, "dw_br_dn")
    dw_br_sb = _matmul(o_sb, dy_sb, "tn", BF16, 256, 1024, 2048, "dw_br_sb")
    dw_br_mem = _matmul(o_m, dy_m, "tn", BF16, 256, 1024, 2048, "dw_br_mem")

    dmq, dmz, dmkv = _mem_attn_bwd(proj, mkv, do_m)
    d_mem_norm_g, dw_mem_kv = _mem_kv_bwd(mem, mem_norm_g, w_mem_kv, dmkv)
    dq_sb, dk_sb, dv_sb, dz_sb = _sb_bwd(proj, o_sb_raw, do_sb)
    dcq, dck, dcv, dz_dn, dba, dscal, d_dn_norm_g = _dn_bwd(c, proj, alog_row, dtb_row, dn_norm_g, states, do_dn)
    dq_dn, dcw_q = _dn_conv_bwd(proj, conv_w, dcq, 0)
    dk_dn, dcw_k = _dn_conv_bwd(proj, conv_w, dck, 1)
    dv_dn, dcw_v = _dn_conv_bwd(proj, conv_w, dcv, 2)
    d_conv_w = jnp.concatenate([dcw_q, dcw_k, dcw_v], axis=1)

    dproj = jnp.concatenate([dq_dn, dk_dn, dv_dn, dz_dn, dq_sb, dk_sb, dv_sb, dz_sb, dmq, dmz, dg_dn, dg_sb, dg_m,
                             dba.astype(BF16)], axis=1)
    dh = _matmul(dproj, w_alt, "nn", F32, 512, 1024, 3968, "dh")
    dw_alt = _matmul(dproj, h, "tn", BF16, 384, 1024, 2048, "dw_alt")
    grad_x, d_norm_g = _norm_in_bwd(x, norm_g, dh, dout)
    return dict(loss=loss, grad_x=grad_x, norm_g=d_norm_g, mem_norm_g=d_mem_norm_g, w_alt=dw_alt, conv_w=d_conv_w,
                scal=dscal, dn_norm_g=d_dn_norm_g, w_mem_kv=dw_mem_kv, w_br_dn=dw_br_dn, w_br_sb=dw_br_sb,
                w_br_mem=dw_br_mem, w_out=dw_out, final_g=d_final_g)


MESH = pl.DeviceIdType.MESH
ANY = pl.BlockSpec(memory_space=pl.ANY)


def _position():
    return lax.axis_index("x"), lax.axis_index("y"), lax.axis_index("c")


def _all_gather(xs, name):
    n = len(xs)

    def body(*refs):
        x_refs, o_refs = refs[:n], refs[n:2 * n]
        send_sems, recv_sems, local_sems = refs[2 * n:]
        x, y, c = _position()
        me, sibling = (x, y, c), (x, y, 1 - c)
        x_nbr, y_nbr, diag = (1 - x, y, c), (x, 1 - y, c), (1 - x, 1 - y, c)
        south = c == 0
        relay_from = tuple(jnp.where(south, a, b) for a, b in zip(y_nbr, x_nbr))
        relay_to = tuple(jnp.where(south, a, b) for a, b in zip(x_nbr, y_nbr))

        def slot(p):
            return 4 * p[0] + 2 * p[1] + p[2]

        def copy(a, k, block, to, src=None):
            dst = o_refs[a].at[slot(block)]
            return pltpu.make_async_remote_copy(
                src_ref=dst if src is None else src, dst_ref=dst, send_sem=send_sems.at[7 * a + k],
                recv_sem=recv_sems.at[7 * a + k], device_id=to, device_id_type=MESH)

        mine = [pltpu.make_async_copy(x_refs[a], o_refs[a].at[slot(me)], local_sems.at[a]) for a in range(n)]
        for cp in mine:
            cp.start()
        sends = []
        for a in range(n):
            sends += [copy(a, 0, me, sibling, src=x_refs[a]), copy(a, 1, me, x_nbr, src=x_refs[a]),
                      copy(a, 2, me, y_nbr, src=x_refs[a])]
        for cp in sends:
            cp.start()
        later = []
        for a in range(n):
            copy(a, 1, x_nbr, me).wait_recv()
            copy(a, 2, y_nbr, me).wait_recv()
            later += [copy(a, 3, relay_from, relay_to), copy(a, 4, x_nbr, sibling), copy(a, 5, y_nbr, sibling)]
            for cp in later[-3:]:
                cp.start()
        for a in range(n):
            copy(a, 3, diag, me).wait_recv()
            later.append(copy(a, 6, diag, sibling))
            later[-1].start()
        for a in range(n):
            copy(a, 0, sibling, me).wait_recv()
            for k, chip in ((4, x_nbr), (5, y_nbr), (6, diag)):
                copy(a, k, (chip[0], chip[1], 1 - c), me).wait_recv()
        for cp in sends + later:
            cp.wait_send()
        for cp in mine:
            cp.wait()

    return pl.pallas_call(
        body,
        name=name,
        in_specs=[ANY] * n,
        out_specs=[ANY] * n,
        out_shape=[jax.ShapeDtypeStruct((N_DEV, *v.shape), v.dtype) for v in xs],
        scratch_shapes=[pltpu.SemaphoreType.DMA((7 * n,)), pltpu.SemaphoreType.DMA((7 * n,)),
                        pltpu.SemaphoreType.DMA((n,))],
    )(*xs)


def _window_view(ref, dest):
    return ref.at[pl.ds(LANE * WIN_START[dest], WIN_W), :]


def _chunk_rows(rows, cols):
    best = max(ch for ch in range(16, rows + 1, 16) if rows % ch == 0 and ch * cols <= (1 << 17))
    return best


def _halving_stage(xs, axis, name, out_dtype, windowed=()):
    n_arr = len(xs)
    metas = []
    for k, v in enumerate(xs):
        if k in windowed:
            metas.append((N_DEV // 2, WIN_W, v.shape[1]))
        else:
            assert v.shape[1] == 2
            metas.append((v.shape[0], v.shape[2], v.shape[3]))
    chunk = [_chunk_rows(r, c) for (_, r, c) in metas]
    offs = [sum(m[0] for m in metas[:k]) for k in range(n_arr)]
    n_sem = sum(m[0] for m in metas)

    def body(*refs):
        x_refs = refs[:n_arr]
        o_refs = refs[n_arr:2 * n_arr]
        land_refs = refs[2 * n_arr:3 * n_arr]
        rest = refs[3 * n_arr:]
        bufs = rest[:3 * n_arr]
        send_sems, recv_sems, in_sems, out_sems = rest[3 * n_arr:]
        pos = dict(zip("xyc", _position()))
        bit = pos[axis]
        peer = tuple(1 - pos[a] if a == axis else pos[a] for a in "xyc")

        def view(k, i, b):
            if k in windowed:
                return _window_view(x_refs[k], 2 * i + b)
            return x_refs[k].at[i, b]

        def add_blocks(k, a_view, b_view, o_view):
            _hbm_add(a_view, b_view, o_view, bufs[3 * k:3 * k + 3], in_sems, out_sems, chunk[k])

        for b in (0, 1):
            @pl.when(bit == b)
            def _(b=b):
                sends = []
                for k in range(n_arr):
                    for i in range(metas[k][0]):
                        cp = pltpu.make_async_remote_copy(
                            src_ref=view(k, i, 1 - b), dst_ref=land_refs[k].at[i], send_sem=send_sems.at[offs[k] + i],
                            recv_sem=recv_sems.at[offs[k] + i], device_id=peer, device_id_type=MESH)
                        cp.start()
                        sends.append(cp)
                idx = 0
                for k in range(n_arr):
                    for i in range(metas[k][0]):
                        sends[idx].wait_recv()
                        add_blocks(k, view(k, i, b), land_refs[k].at[i], o_refs[k].at[i])
                        idx += 1
                for cp in sends:
                    cp.wait_send()

    out_shape = [jax.ShapeDtypeStruct(m, out_dtype) for m in metas]
    land_shape = [jax.ShapeDtypeStruct(m, v.dtype) for m, v in zip(metas, xs)]
    scratch = []
    for k in range(n_arr):
        blk = (2, chunk[k], metas[k][2])
        scratch += [pltpu.VMEM(blk, xs[k].dtype)] * 2 + [pltpu.VMEM(blk, out_dtype)]
    scratch += [pltpu.SemaphoreType.DMA((n_sem,)), pltpu.SemaphoreType.DMA((n_sem,)),
                pltpu.SemaphoreType.DMA((2, 2)), pltpu.SemaphoreType.DMA((2,))]
    outs = pl.pallas_call(
        body,
        name=name,
        in_specs=[ANY] * n_arr,
        out_specs=[ANY] * (2 * n_arr),
        out_shape=out_shape + land_shape,
        scratch_shapes=scratch,
    )(*xs)
    return outs[:n_arr]


def _hbm_add(a_view, b_view, o_view, bufs, in_sems, out_sems, ch):
    rows = a_view.shape[0]
    nch = rows // ch
    va, vb, vo = bufs

    def rows_of(j):
        return pl.ds(pl.multiple_of(j * ch, 16), ch)

    def loads(j, s):
        return (pltpu.make_async_copy(a_view.at[rows_of(j), :], va.at[s], in_sems.at[0, s]),
                pltpu.make_async_copy(b_view.at[rows_of(j), :], vb.at[s], in_sems.at[1, s]))

    def store(j, s):
        return pltpu.make_async_copy(vo.at[s], o_view.at[rows_of(j), :], out_sems.at[s])

    for cp in loads(0, 0):
        cp.start()

    def step(j, _):
        s = lax.rem(j, 2)

        @pl.when(j + 1 < nch)
        def _():
            for cp in loads(j + 1, 1 - s):
                cp.start()

        for cp in loads(j, s):
            cp.wait()

        @pl.when(j >= 2)
        def _():
            store(j - 2, s).wait()

        vo[s] = (va[s].astype(F32) + vb[s].astype(F32)).astype(vo.dtype)
        store(j, s).start()
        return 0

    lax.fori_loop(0, nch, step, 0)
    for j in range(max(0, nch - 2), nch):
        store(j, j % 2).wait()


def _xy_stage(xs, first, name):
    n_arr = len(xs)
    if first:
        shapes = [(v.shape[2] // 2, v.shape[3]) for v in xs]
        ins = list(xs)
    else:
        shapes = [(a.shape[1], a.shape[2]) for a, _ in xs]
        ins = [v for pair in xs for v in pair]
    n_blk = 2 if first else 1
    out_dtype = BF16 if first else F32
    chunk = [_chunk_rows(r, c) for (r, c) in shapes]
    n_sem = 2 * n_blk * n_arr

    def body(*refs):
        n_in = len(ins)
        in_refs = refs[:n_in]
        n_out = 2 * n_arr if first else n_arr
        o_refs = refs[n_in:n_in + n_out]
        land = refs[n_in + n_out:n_in + n_out + 2 * n_arr]
        rest = refs[n_in + n_out + 2 * n_arr:]
        bufs = rest[:3 * n_arr]
        send_sems, recv_sems, in_sems, out_sems = rest[3 * n_arr:]
        x, y, c = _position()
        peers = {"x": (1 - x, y, c), "y": (x, 1 - y, c)}
        jobs = []
        for k in range(n_arr):
            r, _ = shapes[k]
            half_a, half_b = pl.ds(0, r), pl.ds(r, r)
            if first:
                src = in_refs[k]
                for i in range(2):
                    jobs.append((k, src.at[i, 1 - y, half_a, :], src.at[i, y, half_a, :], land[2 * k].at[i],
                                 o_refs[2 * k].at[i], "y"))
                    jobs.append((k, src.at[1 - x, i, half_b, :], src.at[x, i, half_b, :], land[2 * k + 1].at[i],
                                 o_refs[2 * k + 1].at[i], "x"))
            else:
                a1, b1 = in_refs[2 * k], in_refs[2 * k + 1]
                jobs.append((k, a1.at[1 - x], a1.at[x], land[2 * k], o_refs[k].at[half_a, :], "x"))
                jobs.append((k, b1.at[1 - y], b1.at[y], land[2 * k + 1], o_refs[k].at[half_b, :], "y"))
        sends = []
        for n, (k, send, _, landing, _, axis) in enumerate(jobs):
            cp = pltpu.make_async_remote_copy(src_ref=send, dst_ref=landing, send_sem=send_sems.at[n],
                                              recv_sem=recv_sems.at[n], device_id=peers[axis], device_id_type=MESH)
            cp.start()
            sends.append(cp)
        for cp, (k, _, kept, landing, out, _) in zip(sends, jobs):
            cp.wait_recv()
            _hbm_add(kept, landing, out, bufs[3 * k:3 * k + 3], in_sems, out_sems, chunk[k])
        for cp in sends:
            cp.wait_send()

    if first:
        out_shape = [jax.ShapeDtypeStruct((2, r, c), BF16) for (r, c) in shapes for _ in range(2)]
        land_shape = out_shape
    else:
        out_shape = [jax.ShapeDtypeStruct((2 * r, c), F32) for (r, c) in shapes]
        land_shape = [jax.ShapeDtypeStruct((r, c), BF16) for (r, c) in shapes for _ in range(2)]
    scratch = []
    for k in range(n_arr):
        scratch += [pltpu.VMEM((2, chunk[k], shapes[k][1]), BF16)] * 2 + [pltpu.VMEM((2, chunk[k], shapes[k][1]), out_dtype)]
    scratch += [pltpu.SemaphoreType.DMA((n_sem,)), pltpu.SemaphoreType.DMA((n_sem,)),
                pltpu.SemaphoreType.DMA((2, 2)), pltpu.SemaphoreType.DMA((2,))]
    outs = pl.pallas_call(
        body,
        name=name,
        in_specs=[ANY] * len(ins),
        out_specs=[ANY] * (len(out_shape) + len(land_shape)),
        out_shape=out_shape + land_shape,
        scratch_shapes=scratch,
    )(*ins)
    outs = outs[:len(out_shape)]
    return [(outs[2 * k], outs[2 * k + 1]) for k in range(n_arr)] if first else list(outs)


def _reduce_scatter(dw_al, blocks):
    xs = [dw_al] + [b.reshape(N_DEV // 2, 2, *b.shape[1:]) for b in blocks]
    ys = _halving_stage(xs, "c", "rs_c", BF16, windowed=(0,))
    pairs = _xy_stage([v.reshape(2, 2, *v.shape[1:]) for v in ys], True, "rs_xy1")
    return _xy_stage(pairs, False, "rs_xy2")


def _sum_slots(gs):
    n = len(gs)

    def body(*refs):
        for g_ref, o_ref in zip(refs[:n], refs[n:]):
            acc = g_ref[0]
            for d in range(1, N_DEV):
                acc = acc + g_ref[d]
            o_ref[...] = acc

    return pl.pallas_call(body, name="sum_slots",
                          out_shape=[jax.ShapeDtypeStruct(g.shape[1:], g.dtype) for g in gs])(*gs)


def _assemble_w_al(wins, bas):
    ba_tile = O_BA // LANE
    assert W_AL // LANE == ba_tile + 1
    cols = wins.shape[2]
    n_buf = 3
    ends = [WIN_START[d + 1] if d + 1 < N_DEV else ba_tile + 1 for d in range(N_DEV)]
    assert WIN_START[N_DEV - 1] + WIN_TILES == ba_tile + 1

    def body(w_ref, ba_ref, o_ref, buf, ld_sems, st_sems, ba_sem):
        def load(d):
            return pltpu.make_async_copy(w_ref.at[d], buf.at[d % n_buf], ld_sems.at[d % n_buf])

        def store(d):
            n = LANE * (ends[d] - WIN_START[d])
            return pltpu.make_async_copy(buf.at[d % n_buf, pl.ds(0, n), :],
                                         o_ref.at[pl.ds(LANE * WIN_START[d], n), :], st_sems.at[d % n_buf])

        load(0).start()
        for d in range(N_DEV):
            if d + 1 < N_DEV:
                if d + 1 >= n_buf:
                    store(d + 1 - n_buf).wait()
                load(d + 1).start()
            load(d).wait()
            if d > 0:
                ov = LANE * (WIN_START[d - 1] + WIN_TILES - WIN_START[d])
                buf[d % n_buf, :ov, :] = buf[d % n_buf, :ov, :] + buf[(d - 1) % n_buf, WIN_W - ov:, :]
            if d == N_DEV - 1:
                ba_copy = pltpu.make_async_copy(
                    ba_ref.at[BA_DEV], buf.at[d % n_buf, pl.ds(WIN_W - LANE, ba_ref.shape[1]), :], ba_sem)
                ba_copy.start()
                ba_copy.wait()
            store(d).start()
        for d in range(N_DEV - n_buf, N_DEV):
            store(d).wait()

    return pl.pallas_call(
        body,
        name="assemble_w_al",
        in_specs=[ANY, ANY],
        out_specs=ANY,
        out_shape=jax.ShapeDtypeStruct((W_AL, cols), wins.dtype),
        scratch_shapes=[pltpu.VMEM((n_buf, WIN_W, cols), wins.dtype), pltpu.SemaphoreType.DMA((n_buf,)),
                        pltpu.SemaphoreType.DMA((n_buf,)), pltpu.SemaphoreType.DMA],
    )(wins, bas)


def _adamw_math(w, g, m, v):
    m_new = ADAM_B1 * m + (1.0 - ADAM_B1) * g
    v_new = ADAM_B2 * v + (1.0 - ADAM_B2) * (g * g)
    m_hat = m_new / (1.0 - ADAM_B1 ** ADAM_STEP)
    v_hat = v_new / (1.0 - ADAM_B2 ** ADAM_STEP)
    return -ADAM_LR * (m_hat / (jnp.sqrt(v_hat) + ADAM_EPS) + ADAM_WD * w), m_new, v_new


def _adamw(w, g, m, v, name, tb=134):
    r, _, c = w.shape
    assert r % tb == 0

    def body(w_ref, g_ref, m_ref, v_ref, d_ref, nm_ref, nv_ref):
        d_ref[...], nm_ref[...], nv_ref[...] = _adamw_math(w_ref[...], g_ref[...], m_ref[...], v_ref[...])

    blk = pl.BlockSpec((tb, 1, c), lambda i: (i, 0, 0))
    o = jax.ShapeDtypeStruct(w.shape, F32)
    return pl.pallas_call(body, name=name, grid=(r // tb,), in_specs=[blk] * 4, out_specs=[blk] * 3,
                          out_shape=[o, o, o])(w, g, m, v)


def _adamw_many(ws, gs, ms, vs, name):
    n = len(ws)

    def body(*refs):
        for k in range(n):
            w_ref, g_ref, m_ref, v_ref = (refs[j * n + k] for j in range(4))
            d_ref, nm_ref, nv_ref = (refs[(4 + j) * n + k] for j in range(3))
            d_ref[...], nm_ref[...], nv_ref[...] = _adamw_math(w_ref[...], g_ref[...], m_ref[...], v_ref[...])

    shapes = [jax.ShapeDtypeStruct(w.shape, F32) for w in ws]
    outs = pl.pallas_call(body, name=name, out_shape=shapes * 3)(*ws, *gs, *ms, *vs)
    return outs[:n], outs[n:2 * n], outs[2 * n:]


def _select(me, table):
    return sum(jnp.where(me == d, jnp.int32(v), jnp.int32(0)) for d, v in enumerate(table))


WIN_SHIFT = tuple(SHARD_W * d - LANE * WIN_START[d] for d in range(N_DEV))
PAD_L = 256
PAD_R = 256


def _shard_to_window(shard_t, me):
    shift = _select(me, WIN_SHIFT)
    start = _select(me, WIN_START)
    padded = jnp.pad(shard_t, ((PAD_L, PAD_R), (0, 0)))
    cols = shard_t.shape[1]
    lo = lax.dynamic_slice(padded, (PAD_L - shift, 0), (WIN_W, cols))
    hi = lax.dynamic_slice(padded, (PAD_L - shift + N_BA, 0), (WIN_W, cols))
    aligned = LANE * start + lax.broadcasted_iota(jnp.int32, (WIN_W, 1), 0)
    return jnp.where(aligned >= ORIG_BA, hi, lo)


def _window_to_shard(win, ba_grad, me):
    shift = _select(me, WIN_SHIFT)
    cols = win.shape[1]
    padded = jnp.pad(win, ((N_BA, PAD_R), (0, 0)))
    lo = lax.dynamic_slice(padded, (N_BA + shift, 0), (SHARD_W, cols))
    hi = lax.dynamic_slice(padded, (shift, 0), (SHARD_W, cols))
    orig = SHARD_W * me + lax.broadcasted_iota(jnp.int32, (SHARD_W, 1), 0)
    ba_full = lax.dynamic_update_slice(jnp.zeros((SHARD_W, cols), win.dtype), ba_grad, (BA_LOCAL, 0))
    return jnp.where(orig < ORIG_BA, lo, jnp.where(orig >= ORIG_BA + N_BA, hi, ba_full))


def _pad_row(v, width=D_MODEL):
    v = v.reshape(1, -1)
    return jnp.pad(v, ((0, 0), (0, width - v.shape[1])))


def _slab(v, rows=8):
    return jnp.pad(v, ((0, rows - v.shape[0]), (0, D_MODEL - v.shape[1])))


def kernel(x, mem, norm_g, mem_norm_g, w_in, conv_w, a_log, dt_bias, dn_norm_g, w_mem_kv, w_br_dn, w_br_sb, w_br_mem, w_out, final_g, loss_target, m_norm_g, m_mem_norm_g, m_w_in, m_conv_w, m_a_log, m_dt_bias, m_dn_norm_g, m_w_mem_kv, m_w_br_dn, m_w_br_sb, m_w_br_mem, m_w_out, m_final_g, v_norm_g, v_mem_norm_g, v_w_in, v_conv_w, v_a_log, v_dt_bias, v_dn_norm_g, v_w_mem_kv, v_w_br_dn, v_w_br_sb, v_w_br_mem, v_w_out, v_final_g):
    xi, yi, ci = _position()
    me = 4 * xi + 2 * yi + ci

    shard_t = w_in[0].T
    win = _shard_to_window(shard_t, me).astype(BF16)
    ba = shard_t[BA_LOCAL:BA_LOCAL + N_BA, :].astype(BF16)
    g_win, g_ba, g_kv, g_dn, g_sb, g_out, g_mem, g_conv = _all_gather(
        [win, ba, w_mem_kv[0].astype(BF16), w_br_dn[0].astype(BF16), w_br_sb[0].astype(BF16), w_out[0].astype(BF16),
         w_br_mem[0].astype(BF16), conv_w[0]], "gather_weights")
    w_alt = _assemble_w_al(g_win, g_ba)
    w_mem_kv_f = g_kv.reshape(D_MODEL, 2 * MEM_W)
    w_br_dn_f = g_dn.reshape(D_MODEL, D_MODEL)
    w_br_sb_f = g_sb.reshape(D_MODEL, D_MODEL)
    w_out_f = g_out.reshape(D_MODEL, D_MODEL)
    w_br_mem_f = g_mem.transpose(1, 0, 2).reshape(MEM_W, D_MODEL)
    conv_w_f = g_conv.transpose(1, 0, 2).reshape(CONV_K, 3 * D_MODEL)

    r = _local_step(x[0], mem[0], loss_target[0], norm_g, mem_norm_g, w_alt, conv_w_f, _pad_row(a_log, LANE),
                    _pad_row(dt_bias, LANE), dn_norm_g, w_mem_kv_f, w_br_dn_f, w_br_sb_f, w_br_mem_f, w_out_f,
                    final_g.reshape(1, D_MODEL))

    dw_alt = r["w_alt"]
    g_win, g_kv, g_dn, g_sb, g_out, g_mem = _reduce_scatter(dw_alt, [
        r["w_mem_kv"].reshape(N_DEV, D_MODEL // N_DEV, 2 * MEM_W),
        r["w_br_dn"].reshape(N_DEV, D_MODEL // N_DEV, D_MODEL),
        r["w_br_sb"].reshape(N_DEV, D_MODEL // N_DEV, D_MODEL),
        r["w_out"].reshape(N_DEV, D_MODEL // N_DEV, D_MODEL),
        r["w_br_mem"].reshape(MEM_W, N_DEV, D_MODEL // N_DEV).transpose(1, 0, 2)])
    parts = [r["norm_g"], r["mem_norm_g"], r["final_g"], r["dn_norm_g"], r["scal"], r["loss"], r["conv_w"],
             dw_alt[O_BA:O_BA + N_BA, :].astype(F32)]
    s_norm_g, s_mem_norm_g, s_final_g, s_dn_norm_g, s_scal, s_loss, s_conv, s_ba = _sum_slots(
        _all_gather(parts, "gather_small"))
    loss = s_loss[0, 0]
    cw = conv_w.shape[2]
    g_conv = lax.dynamic_slice(s_conv, (0, cw * me), (CONV_K, cw))
    g_w_in_t = _window_to_shard(g_win, s_ba, me)
    grads = dict(norm_g=s_norm_g, mem_norm_g=s_mem_norm_g, w_in=g_w_in_t.T[None], conv_w=g_conv[None],
                 a_log=s_scal[0:1, :N_HEADS], dt_bias=s_scal[1:2, :N_HEADS], dn_norm_g=s_dn_norm_g, w_mem_kv=g_kv[None],
                 w_br_dn=g_dn[None], w_br_sb=g_sb[None], w_br_mem=g_mem[None], w_out=g_out[None],
                 final_g=s_final_g.reshape(D_MODEL))

    params = dict(norm_g=(norm_g, m_norm_g, v_norm_g), mem_norm_g=(mem_norm_g, m_mem_norm_g, v_mem_norm_g),
                  w_in=(w_in, m_w_in, v_w_in), conv_w=(conv_w, m_conv_w, v_conv_w), a_log=(a_log, m_a_log, v_a_log),
                  dt_bias=(dt_bias, m_dt_bias, v_dt_bias), dn_norm_g=(dn_norm_g, m_dn_norm_g, v_dn_norm_g),
                  w_mem_kv=(w_mem_kv, m_w_mem_kv, v_w_mem_kv), w_br_dn=(w_br_dn, m_w_br_dn, v_w_br_dn),
                  w_br_sb=(w_br_sb, m_w_br_sb, v_w_br_sb), w_br_mem=(w_br_mem, m_w_br_mem, v_w_br_mem),
                  w_out=(w_out, m_w_out, v_w_out), final_g=(final_g, m_final_g, v_final_g))
    order = list(params)
    deltas, new_m, new_v = {}, {}, {}
    deltas["w_in"], new_m["w_in"], new_v["w_in"] = (jnp.transpose(o, (1, 2, 0)) for o in _adamw(
        jnp.transpose(w_in, (2, 0, 1)), g_w_in_t[:, None, :], jnp.transpose(m_w_in, (2, 0, 1)),
        jnp.transpose(v_w_in, (2, 0, 1)), "adamw_w_in"))
    rest = [nm for nm in order if nm != "w_in"]

    def two_d(a):
        return a.reshape(1, -1) if a.ndim == 1 else a

    d_l, m_l, v_l = _adamw_many([two_d(params[nm][0]) for nm in rest], [two_d(grads[nm]) for nm in rest],
                                [two_d(params[nm][1]) for nm in rest], [two_d(params[nm][2]) for nm in rest], "adamw_rest")
    for k, nm in enumerate(rest):
        shp = params[nm][0].shape
        deltas[nm], new_m[nm], new_v[nm] = d_l[k].reshape(shp), m_l[k].reshape(shp), v_l[k].reshape(shp)
    return (loss, r["grad_x"][None], *[grads[nm] for nm in order], *[deltas[nm] for nm in order],
            *[new_m[nm] for nm in order], *[new_v[nm] for nm in order])
```

```python
import functools
import math

import jax
import jax.numpy as jnp
from jax import lax
from jax.experimental import pallas as pl
from jax.experimental.pallas import tpu as pltpu

F32 = jnp.float32
BF16 = jnp.bfloat16

D_MODEL = 1024
N_DEV = 8
N_HEADS = 8
D_HEAD = 128
DN_CHUNK = 64
CONV_K = 4
MEM_LEN = 256
MEM_HEADS = 4
MEM_DH = 64
MEM_W = MEM_HEADS * MEM_DH
NORM_EPS = 1e-6
IN_WIDTH = 11792
SHARD_W = IN_WIDTH // N_DEV

LANE = 128
SUPER = 2 * DN_CHUNK

O_QKV_DN = 0
O_Z_DN = 3072
O_QKV_SB = 4096
O_Z_SB = 7168
O_MQ = 8192
O_MZ = 8448
O_GATES = 8704
O_BA = 11776
W_AL = 11904
ORIG_BA = 4096
N_BA = 16

WIN_TILES = 13
WIN_W = WIN_TILES * LANE


def _aligned_col(o):
    return o if o < ORIG_BA else o - N_BA


WIN_START = tuple(min(_aligned_col(SHARD_W * d) // LANE, (W_AL // LANE) - WIN_TILES) for d in range(N_DEV))
WIN_OFF = tuple(_aligned_col(SHARD_W * d) - LANE * WIN_START[d] if SHARD_W * d >= ORIG_BA + N_BA or SHARD_W * d < ORIG_BA
                else None for d in range(N_DEV))
BA_DEV = ORIG_BA // SHARD_W
BA_LOCAL = ORIG_BA - BA_DEV * SHARD_W

ADAM_LR = 0.001
ADAM_B1 = 0.9
ADAM_B2 = 0.999
ADAM_EPS = 1e-08
ADAM_WD = 0.01
ADAM_STEP = 10

NN = (((1,), (0,)), ((), ()))
NT = (((1,), (1,)), ((), ()))
TN = (((0,), (0,)), ((), ()))


def _dot(a, b, dims):
    return lax.dot_general(a.astype(BF16), b.astype(BF16), dims, preferred_element_type=F32)


def _split2(a):
    hi = a.astype(BF16)
    lo = (a - hi.astype(F32)).astype(BF16)
    return hi, lo


def _dot3(a, b, dims):
    ah, al = _split2(a)
    bh, bl = _split2(b)
    d = functools.partial(lax.dot_general, dimension_numbers=dims, preferred_element_type=F32)
    return d(ah, bh) + (d(ah, bl) + d(al, bh))


def _sel_dot_impl(sel01, x, dims):
    sel = sel01.astype(BF16)
    h1 = x.astype(BF16)
    r1 = x - h1.astype(F32)
    h2 = r1.astype(BF16)
    h3 = (r1 - h2.astype(F32)).astype(BF16)
    d = functools.partial(lax.dot_general, dimension_numbers=dims, preferred_element_type=F32)
    return d(sel, h1) + (d(sel, h2) + d(sel, h3))


@jax.custom_vjp
def _sel_dot(sel01, x):
    return _sel_dot_impl(sel01, x, NN)


_sel_dot.defvjp(lambda s, x: (_sel_dot(s, x), s),
                lambda s, g: (jnp.zeros_like(s), _sel_dot_impl(s, g, TN)))


def _make_mm(dotfn):
    @jax.custom_vjp
    def nn(a, b):
        return dotfn(a, b, NN)

    @jax.custom_vjp
    def nt(a, b):
        return dotfn(a, b, NT)

    @jax.custom_vjp
    def tn(a, b):
        return dotfn(a, b, TN)

    nn.defvjp(lambda a, b: (nn(a, b), (a, b)), lambda r, g: (nt(g, r[1]), tn(r[0], g)))
    nt.defvjp(lambda a, b: (nt(a, b), (a, b)), lambda r, g: (nn(g, r[1]), tn(g, r[0])))
    tn.defvjp(lambda a, b: (tn(a, b), (a, b)), lambda r, g: (nt(r[1], g), nn(r[0], g)))
    return nn, nt, tn


mm_nn, mm_nt, mm_tn = _make_mm(_dot)
mm3_nn, mm3_nt, mm3_tn = _make_mm(_dot3)


def _sigmoid(x):
    return jax.nn.sigmoid(x)


def _silu(x):
    return x * _sigmoid(x)


def _softplus_parts(x):
    sp = jnp.log1p(jnp.exp(-jnp.abs(x)))
    return jnp.maximum(x, 0.0) + sp, jnp.maximum(-x, 0.0) + sp


def _rmsnorm(x, g):
    return x * lax.rsqrt(jnp.mean(x * x, axis=-1, keepdims=True) + NORM_EPS) * g


def _iota2(shape, dim):
    return lax.broadcasted_iota(jnp.int32, shape, dim)


def _div64(i):
    return lax.shift_right_logical(i, jnp.full(i.shape, 6, jnp.int32))


def _each(f, *lists):
    return [f(*a) for a in zip(*lists)]


@jax.custom_vjp
def _inv_unit_lower(ms):
    n = ms[0].shape[0]
    eye = (_iota2((n, n), 0) == _iota2((n, n), 1)).astype(F32)
    rs = [eye - m for m in ms]
    ps = ms
    for _ in range(5):
        ps = _each(mm3_nn, ps, ps)
        rs = _each(lambda r, p: r + mm3_nn(r, p), rs, ps)
    return rs


def _inv_fwd(ms):
    rs = _inv_unit_lower(ms)
    return rs, rs


def _inv_bwd(rs, gs):
    ts = _each(mm3_tn, rs, gs)
    return (_each(lambda t, r: -mm3_nt(t, r), ts, rs),)


_inv_unit_lower.defvjp(_inv_fwd, _inv_bwd)


def _dn_block(cq, ck, cv, bcol, acol, zt, alog, dtb, gn, s0):
    n = SUPER
    h = DN_CHUNK
    row = _iota2((n, n), 0)
    col = _iota2((n, n), 1)
    same = _div64(row) == _div64(col)
    incl = jnp.logical_and(same, row >= col)
    strict = jnp.logical_and(same, row > col)
    incl_f = incl.astype(F32)

    qn = _each(lambda x: x * lax.rsqrt(jnp.sum(x * x, axis=-1, keepdims=True) + NORM_EPS) * (D_HEAD ** -0.5), cq)
    kn = _each(lambda x: x * lax.rsqrt(jnp.sum(x * x, axis=-1, keepdims=True) + NORM_EPS), ck)
    beta = _each(_sigmoid, bcol)
    g = _each(lambda al, ac, dt: -(jnp.exp(al) * _softplus_parts(ac + dt)[0]), alog, acol, dtb)
    gcum = _each(lambda x: _sel_dot(incl_f, jnp.broadcast_to(x, (n, n))), g)
    gam_incl = _each(lambda x: jnp.where(incl, jnp.exp(jnp.where(incl, x - x.T, 0.0)), 0.0), gcum)
    kk = _each(mm_nt, kn, kn)
    t_inv = _inv_unit_lower(_each(lambda b, x, gm: b * x * jnp.where(strict, gm, 0.0), beta, kk, gam_incl))
    eg = _each(jnp.exp, gcum)
    u = _each(lambda t, v, b: mm_nn(t, v * b), t_inv, cv, beta)
    w = _each(lambda t, k, b, e: mm_nn(t, k * (b * e)), t_inv, kn, beta, eg)
    a_intra = _each(lambda q, k, gm: mm_nt(q, k) * gm, qn, kn, gam_incl)
    q_dec = _each(lambda q, e: q * e, qn, eg)
    last0 = _each(lambda x: x[h - 1:h, :], gcum)
    last1 = _each(lambda x: x[n - 1:n, :], gcum)
    k_dec = _each(lambda k, x, l0, l1: k * jnp.exp(jnp.concatenate(
        [jnp.broadcast_to(l0, (h, n)), jnp.broadcast_to(l1, (h, n))], axis=0) - x), kn, gcum, last0, last1)
    v0 = _each(lambda uu, ww, s: uu[:h] - mm_nn(ww[:h], s), u, w, s0)
    o0 = _each(lambda q, s: mm_nn(q[:h], s), q_dec, s0)
    s1 = _each(lambda s, l0, k, v: s * jnp.exp(l0) + mm_tn(k[:h], v), s0, last0, k_dec, v0)
    v1 = _each(lambda uu, ww, s: uu[h:] - mm_nn(ww[h:], s), u, w, s1)
    o1 = _each(lambda q, s: mm_nn(q[h:], s), q_dec, s1)
    s2 = _each(lambda s, l1, k, v: s * jnp.exp(l1) + mm_tn(k[h:], v), s1, last1, k_dec, v1)
    o = _each(lambda a, b, am, x, y: jnp.concatenate([a, b], axis=0) + mm_nn(am, jnp.concatenate([x, y], axis=0)),
              o0, o1, a_intra, v0, v1)
    out = _each(lambda x, z: _rmsnorm(x, gn) * _silu(z), o, zt)
    return out, s2


def _mem_fn(mq, mz, mkv):
    mk = mkv[:, :MEM_W]
    mv = mkv[:, MEM_W:]
    lane = _iota2((1, MEM_W), 1)
    out = jnp.zeros(mq.shape, F32)
    for hd in range(MEM_HEADS):
        hm = (_div64(lane) == hd).astype(F32)
        s = mm_nt(mq * hm, mk) * (1.0 / math.sqrt(MEM_DH))
        s = s - jnp.max(s, axis=-1, keepdims=True)
        e = jnp.exp(s)
        p = e / jnp.sum(e, axis=-1, keepdims=True)
        out = out + mm_nn(p, mv) * hm
    return out * _silu(mz)


def _merge_fn(gd, gs, gm, yd, ys, ym):
    return _sigmoid(gd) * yd + _sigmoid(gs) * ys + _sigmoid(gm) * ym


def _loss_fn(x, mo, fg, tgt):
    y = _rmsnorm(x + mo, fg)
    err = y - tgt
    return 0.5 * jnp.sum(jnp.mean(err * err, axis=-1, keepdims=True), axis=0, keepdims=True)


def _matmul(a, b, mode, out_dtype, tm, tn, tk, name, b_col0=0, n_cols=None):
    if mode == "nn":
        m, kdim = a.shape
        n = b.shape[1] if n_cols is None else n_cols
    elif mode == "nt":
        m, kdim = a.shape
        n = b.shape[0]
    else:
        kdim, m = a.shape
        n = b.shape[1] if n_cols is None else n_cols
    tm, tn, tk = min(tm, m), min(tn, n), min(tk, kdim)
    assert m % tm == 0 and n % tn == 0 and kdim % tk == 0 and b_col0 % tn == 0
    nk = kdim // tk
    jb = b_col0 // tn
    dims = {"nn": NN, "nt": NT, "tn": TN}[mode]

    def body(a_ref, b_ref, o_ref, acc_ref):
        k = pl.program_id(2)
        part = _dot(a_ref[...], b_ref[...], dims)

        @pl.when(k == 0)
        def _():
            acc_ref[...] = part

        @pl.when(k > 0)
        def _():
            acc_ref[...] += part

        @pl.when(k == nk - 1)
        def _():
            o_ref[...] = acc_ref[...].astype(o_ref.dtype)

    if mode == "nn":
        a_spec = pl.BlockSpec((tm, tk), lambda i, j, k: (i, k))
        b_spec = pl.BlockSpec((tk, tn), lambda i, j, k: (k, j + jb))
    elif mode == "nt":
        a_spec = pl.BlockSpec((tm, tk), lambda i, j, k: (i, k))
        b_spec = pl.BlockSpec((tn, tk), lambda i, j, k: (j, k))
    else:
        a_spec = pl.BlockSpec((tk, tm), lambda i, j, k: (k, i))
        b_spec = pl.BlockSpec((tk, tn), lambda i, j, k: (k, j + jb))
    return pl.pallas_call(
        body,
        name=name,
        grid=(m // tm, n // tn, nk),
        in_specs=[a_spec, b_spec],
        out_specs=pl.BlockSpec((tm, tn), lambda i, j, k: (i, j)),
        out_shape=jax.ShapeDtypeStruct((m, n), out_dtype),
        scratch_shapes=[pltpu.VMEM((tm, tn), F32)],
        compiler_params=pltpu.CompilerParams(dimension_semantics=("parallel", "parallel", "arbitrary")),
    )(a, b)


def _norm_in(x, g, tm=256):
    t = x.shape[0]

    def body(x_ref, g_ref, h_ref):
        h_ref[...] = _rmsnorm(x_ref[...], g_ref[...]).astype(BF16)

    return pl.pallas_call(
        body,
        name="norm_in",
        grid=(t // tm,),
        in_specs=[pl.BlockSpec((tm, D_MODEL), lambda i: (i, 0)), pl.BlockSpec((1, D_MODEL), lambda i: (0, 0))],
        out_specs=pl.BlockSpec((tm, D_MODEL), lambda i: (i, 0)),
        out_shape=jax.ShapeDtypeStruct((t, D_MODEL), BF16),
    )(x, g)


def _norm_in_bwd(x, g, dh, dres, tm=256):
    t = x.shape[0]

    def body(x_ref, g_ref, dh_ref, dres_ref, dx_ref, dg_ref):
        _, vjp = jax.vjp(_rmsnorm, x_ref[...], g_ref[...])
        dx, dg = vjp(dh_ref[...])
        dx_ref[...] = dx + dres_ref[...]

        @pl.when(pl.program_id(0) == 0)
        def _():
            dg_ref[...] = jnp.zeros_like(dg_ref)

        dg_ref[...] += dg

    row = pl.BlockSpec((tm, D_MODEL), lambda i: (i, 0))
    vec = pl.BlockSpec((1, D_MODEL), lambda i: (0, 0))
    return pl.pallas_call(
        body,
        name="norm_in_bwd",
        grid=(t // tm,),
        in_specs=[row, vec, row, row],
        out_specs=[row, vec],
        out_shape=[jax.ShapeDtypeStruct((t, D_MODEL), F32), jax.ShapeDtypeStruct((1, D_MODEL), F32)],
    )(x, g, dh, dres)


def _merge(proj, yd, ys, ym, tm=256, tc=512):
    t = proj.shape[0]
    g0 = O_GATES // tc
    gstep = D_MODEL // tc

    def body(gd, gs, gm, yd_ref, ys_ref, ym_ref, o_ref):
        o_ref[...] = _merge_fn(gd[...], gs[...], gm[...], yd_ref[...], ys_ref[...], ym_ref[...]).astype(BF16)

    def gate(k):
        return pl.BlockSpec((tm, tc), lambda i, j: (i, g0 + k * gstep + j))

    blk = pl.BlockSpec((tm, tc), lambda i, j: (i, j))
    return pl.pallas_call(
        body,
        name="merge",
        grid=(t // tm, D_MODEL // tc),
        in_specs=[gate(0), gate(1), gate(2), blk, blk, blk],
        out_specs=blk,
        out_shape=jax.ShapeDtypeStruct((t, D_MODEL), BF16),
    )(proj, proj, proj, yd, ys, ym)


def _merge_bwd(proj, yd, ys, ym, dmerged, tm=256, tc=512):
    t = proj.shape[0]
    g0 = O_GATES // tc
    gstep = D_MODEL // tc

    def body(gd, gs, gm, yd_ref, ys_ref, ym_ref, dm_ref, dyd, dys, dym, dgd, dgs, dgm):
        _, vjp = jax.vjp(_merge_fn, gd[...], gs[...], gm[...], yd_ref[...], ys_ref[...], ym_ref[...])
        outs = vjp(dm_ref[...])
        for ref, val in zip((dgd, dgs, dgm, dyd, dys, dym), outs):
            ref[...] = val.astype(BF16)

    def gate(k):
        return pl.BlockSpec((tm, tc), lambda i, j: (i, g0 + k * gstep + j))

    blk = pl.BlockSpec((tm, tc), lambda i, j: (i, j))
    o = jax.ShapeDtypeStruct((t, D_MODEL), BF16)
    return pl.pallas_call(
        body,
        name="merge_bwd",
        grid=(t // tm, D_MODEL // tc),
        in_specs=[gate(0), gate(1), gate(2), blk, blk, blk, blk],
        out_specs=[blk] * 6,
        out_shape=[o] * 6,
    )(proj, proj, proj, yd, ys, ym, dmerged)


def _loss_head(x, mo, fg, tgt, tm=256):
    t = x.shape[0]

    def body(x_ref, mo_ref, fg_ref, t_ref, loss_ref, dout_ref, dfg_ref):
        loss, vjp = jax.vjp(_loss_fn, x_ref[...], mo_ref[...], fg_ref[...], t_ref[...])
        _, dmo, dfg, _ = vjp(jnp.ones((1, 1), F32))

        @pl.when(pl.program_id(0) == 0)
        def _():
            loss_ref[...] = jnp.zeros_like(loss_ref)
            dfg_ref[...] = jnp.zeros_like(dfg_ref)

        loss_ref[...] += jnp.broadcast_to(loss, loss_ref.shape)
        dfg_ref[...] += dfg
        dout_ref[...] = dmo

    row = pl.BlockSpec((tm, D_MODEL), lambda i: (i, 0))
    vec = pl.BlockSpec((1, D_MODEL), lambda i: (0, 0))
    return pl.pallas_call(
        body,
        name="loss_head",
        grid=(t // tm,),
        in_specs=[row, row, vec, row],
        out_specs=[pl.BlockSpec((1, LANE), lambda i: (0, 0)), row, vec],
        out_shape=[jax.ShapeDtypeStruct((1, LANE), F32), jax.ShapeDtypeStruct((t, D_MODEL), F32),
                   jax.ShapeDtypeStruct((1, D_MODEL), F32)],
    )(x, mo, fg, tgt)


def _shift_rows(x, s):
    t = x.shape[0]
    if s == 0:
        return x
    rolled = pltpu.roll(x, s % t, 0)
    row = _iota2(x.shape, 0)
    keep = row >= s if s > 0 else row < t + s
    return jnp.where(keep, rolled, 0.0)


def _conv_pre(x, w):
    return sum(_shift_rows(x, CONV_K - 1 - j) * w[j:j + 1, :] for j in range(CONV_K))


CONV_TC = 256


def _dn_conv(proj, conv_w):
    t = proj.shape[0]
    nb = 3 * D_MODEL // CONV_TC

    def body(x_ref, w_ref, c_ref):
        c_ref[...] = _silu(_conv_pre(x_ref[...], w_ref[...]))

    return pl.pallas_call(
        body,
        name="dn_conv",
        grid=(nb,),
        in_specs=[pl.BlockSpec((t, CONV_TC), lambda j: (0, j)), pl.BlockSpec((CONV_K, CONV_TC), lambda j: (0, j))],
        out_specs=pl.BlockSpec((t, CONV_TC), lambda j: (0, j)),
        out_shape=jax.ShapeDtypeStruct((t, 3 * D_MODEL), F32),
    )(proj, conv_w)


def _dn_conv_bwd(proj, conv_w, dc, part):
    t = proj.shape[0]
    nb = D_MODEL // CONV_TC
    b0 = part * nb

    def body(x_ref, w_ref, dc_ref, dx_ref, dw_ref):
        x = x_ref[...]
        w = w_ref[...]
        pre = _conv_pre(x, w)
        sg = _sigmoid(pre)
        dpre = dc_ref[...] * (sg * (1.0 + pre * (1.0 - sg)))
        dx = sum(_shift_rows(dpre, -(CONV_K - 1 - j)) * w[j:j + 1, :] for j in range(CONV_K))
        dx_ref[...] = dx.astype(BF16)
        dw_ref[...] = jnp.concatenate(
            [jnp.sum(dpre * _shift_rows(x, CONV_K - 1 - j), axis=0, keepdims=True) for j in range(CONV_K)], axis=0)

    blk = pl.BlockSpec((t, CONV_TC), lambda j: (0, j))
    return pl.pallas_call(
        body,
        name=f"dn_conv_bwd{part}",
        grid=(nb,),
        in_specs=[pl.BlockSpec((t, CONV_TC), lambda j: (0, b0 + j)),
                  pl.BlockSpec((CONV_K, CONV_TC), lambda j: (0, b0 + j)), blk],
        out_specs=[blk, pl.BlockSpec((CONV_K, CONV_TC), lambda j: (0, j))],
        out_shape=[jax.ShapeDtypeStruct((t, D_MODEL), BF16), jax.ShapeDtypeStruct((CONV_K, D_MODEL), F32)],
    )(proj, conv_w, dc)


def _ba_columns(ba, hd):
    lane = _iota2(ba.shape, 1)
    bcol = jnp.sum(jnp.where(lane == hd, ba, 0.0), axis=1, keepdims=True)
    acol = jnp.sum(jnp.where(lane == N_HEADS + hd, ba, 0.0), axis=1, keepdims=True)
    return bcol, acol


def _head_scalar(row, hd):
    lane = _iota2(row.shape, 1)
    return jnp.sum(jnp.where(lane == hd, row, 0.0), axis=1, keepdims=True)


DN_HP = 4


def _dn_inputs(cq, ck, cv, ba_ref, z_ref, alog_ref, dtb_ref, heads, lanes):
    ba = ba_ref[...]
    cols = [_ba_columns(ba, hd) for hd in heads]
    return ([cq[:, ln] for ln in lanes], [ck[:, ln] for ln in lanes], [cv[:, ln] for ln in lanes],
            [c[0] for c in cols], [c[1] for c in cols], [z_ref[:, ln] for ln in lanes],
            [_head_scalar(alog_ref[...], hd) for hd in heads], [_head_scalar(dtb_ref[...], hd) for hd in heads])


def _dn_specs(nblk, reverse):
    w = DN_HP * LANE
    nq = D_MODEL // w

    def row(i):
        return nblk - 1 - i if reverse else i

    def colblk(b0):
        return pl.BlockSpec((SUPER, w), lambda i, h: (row(i), b0 + h))

    ba = pl.BlockSpec((SUPER, LANE), lambda i, h: (row(i), O_BA // LANE))
    vec = pl.BlockSpec((1, LANE), lambda i, h: (0, 0))
    st = pl.BlockSpec((1, DN_HP, D_HEAD, D_HEAD), lambda i, h: (row(i), h, 0, 0))
    return colblk, nq, ba, vec, st


def _dn_fwd(c, proj, alog_row, dtb_row, gn):
    t = c.shape[0]
    nblk = t // SUPER
    colblk, nq, ba, vec, st = _dn_specs(nblk, False)

    def body(cq, ck, cv, ba_ref, z_ref, alog_ref, dtb_ref, gn_ref, o_ref, s_ref, state):
        @pl.when(jnp.logical_and(pl.program_id(0) == 0, pl.program_id(1) == 0))
        def _():
            state[...] = jnp.zeros_like(state)

        heads = [pl.program_id(1) * DN_HP + j for j in range(DN_HP)]
        lanes = [slice(j * LANE, (j + 1) * LANE) for j in range(DN_HP)]
        s0 = [state[hd] for hd in heads]
        outs, s2 = _dn_block(*_dn_inputs(cq, ck, cv, ba_ref, z_ref, alog_ref, dtb_ref, heads, lanes), gn_ref[...], s0)
        for j, (hd, ln) in enumerate(zip(heads, lanes)):
            s_ref[0, j] = s0[j]
            o_ref[:, ln] = outs[j].astype(BF16)
            state[hd] = s2[j]

    return pl.pallas_call(
        body,
        name="dn_fwd",
        grid=(nblk, N_HEADS // DN_HP),
        in_specs=[colblk(0), colblk(nq), colblk(2 * nq), ba, colblk(O_Z_DN // (DN_HP * LANE)), vec, vec, vec],
        out_specs=[colblk(0), st],
        out_shape=[jax.ShapeDtypeStruct((t, D_MODEL), BF16),
                   jax.ShapeDtypeStruct((nblk, N_HEADS, D_HEAD, D_HEAD), F32)],
        scratch_shapes=[pltpu.VMEM((N_HEADS, D_HEAD, D_HEAD), F32)],
    )(c, c, c, proj, proj, alog_row, dtb_row, gn)


def _dn_bwd(c, proj, alog_row, dtb_row, gn, states, do):
    t = c.shape[0]
    nblk = t // SUPER
    colblk, nq, ba, vec, st = _dn_specs(nblk, True)

    def body(cq, ck, cv, ba_ref, z_ref, alog_ref, dtb_ref, gn_ref, s_ref, do_ref,
             dq_ref, dk_ref, dv_ref, dz_ref, dba_ref, dsc_ref, dgn_ref, dstate):
        i = pl.program_id(0)
        hq = pl.program_id(1)

        @pl.when(jnp.logical_and(i == 0, hq == 0))
        def _():
            dstate[...] = jnp.zeros_like(dstate)
            dsc_ref[...] = jnp.zeros_like(dsc_ref)
            dgn_ref[...] = jnp.zeros_like(dgn_ref)

        @pl.when(hq == 0)
        def _():
            dba_ref[...] = jnp.zeros_like(dba_ref)

        lane = _iota2((SUPER, LANE), 1)
        lane1 = _iota2((1, LANE), 1)
        heads = [hq * DN_HP + j for j in range(DN_HP)]
        lanes = [slice(j * LANE, (j + 1) * LANE) for j in range(DN_HP)]
        ds_in = [dstate[hd] for hd in heads]
        s_in = [s_ref[0, j] for j in range(DN_HP)]
        _, vjp = jax.vjp(_dn_block, *_dn_inputs(cq, ck, cv, ba_ref, z_ref, alog_ref, dtb_ref, heads, lanes),
                         gn_ref[...], s_in)
        dq, dk, dv, dbc, dac, dz, dal, ddt, dgn, ds0 = vjp(([do_ref[:, ln].astype(F32) for ln in lanes], ds_in))
        dba = jnp.zeros((SUPER, LANE), F32)
        dal_row = jnp.zeros((1, LANE), F32)
        ddt_row = jnp.zeros((1, LANE), F32)
        for j, (hd, ln) in enumerate(zip(heads, lanes)):
            dq_ref[:, ln] = dq[j]
            dk_ref[:, ln] = dk[j]
            dv_ref[:, ln] = dv[j]
            dz_ref[:, ln] = dz[j].astype(BF16)
            dstate[hd] = ds0[j]
            dba = dba + jnp.where(lane == hd, dbc[j], 0.0) + jnp.where(lane == N_HEADS + hd, dac[j], 0.0)
            dal_row = dal_row + jnp.where(lane1 == hd, dal[j], 0.0)
            ddt_row = ddt_row + jnp.where(lane1 == hd, ddt[j], 0.0)
        dba_ref[...] += dba
        dsc_ref[0:1, :] += dal_row
        dsc_ref[1:2, :] += ddt_row
        dgn_ref[...] += dgn

    outs = pl.pallas_call(
        body,
        name="dn_bwd",
        grid=(nblk, N_HEADS // DN_HP),
        in_specs=[colblk(0), colblk(nq), colblk(2 * nq), ba, colblk(O_Z_DN // (DN_HP * LANE)), vec, vec, vec, st,
                  colblk(0)],
        out_specs=[colblk(0), colblk(0), colblk(0), colblk(0),
                   pl.BlockSpec((SUPER, LANE), lambda i, h: (nblk - 1 - i, 0)),
                   pl.BlockSpec((2, LANE), lambda i, h: (0, 0)), vec],
        out_shape=[jax.ShapeDtypeStruct((t, D_MODEL), F32)] * 3
        + [jax.ShapeDtypeStruct((t, D_MODEL), BF16), jax.ShapeDtypeStruct((t, LANE), F32),
           jax.ShapeDtypeStruct((2, LANE), F32), jax.ShapeDtypeStruct((1, LANE), F32)],
        scratch_shapes=[pltpu.VMEM((N_HEADS, D_HEAD, D_HEAD), F32)],
    )(c, c, c, proj, proj, alog_row, dtb_row, gn, states, do)
    return outs


SB_TQ = 256
SB_TK = 256
SB_HP = 4


def _sb_logits(z, mask):
    sp = jnp.log(1.0 + jnp.exp(-jnp.abs(z)))
    lf_raw = -(jnp.maximum(z, 0.0) + sp)
    lb = lf_raw + z
    lf = lf_raw if mask is None else jnp.where(mask, lf_raw, 0.0)
    return lb, lf_raw, lf


def _suffix_sums(x, sel):
    hi, lo = _split2(x)
    d = functools.partial(lax.dot_general, dimension_numbers=NN, preferred_element_type=F32)
    return d(hi, sel) + d(lo, sel)


def _sb_diag_mask(tq, r):
    return r * SB_TK + _iota2((tq, SB_TK), 1) < _iota2((tq, SB_TK), 0)


def _sb_specs(t, tq):
    w = SB_HP * LANE
    q0, k0, v0, z0 = (O_QKV_SB // w, (O_QKV_SB + D_MODEL) // w, (O_QKV_SB + 2 * D_MODEL) // w, O_Z_SB // w)

    def blk(b0):
        return pl.BlockSpec((tq, w), lambda h, i: (i, b0 + h))

    def full(b0, **kw):
        return pl.BlockSpec((t, w), lambda h, i: (0, b0 + h), **kw)

    once = dict(pipeline_mode=pl.Buffered(1))
    return blk(q0), full(k0, **once), full(v0, **once), blk(z0), blk(0), full(0)


def _sb_fwd(proj):
    t = proj.shape[0]
    tq = min(SB_TQ, t)
    ndiag = tq // SB_TK
    scale = 1.0 / math.sqrt(D_HEAD)

    def body(q_ref, k_ref, v_ref, z_ref, o_ref, oraw_ref):
        qi = pl.program_id(1)
        lanes = [slice(hd * LANE, (hd + 1) * LANE) for hd in range(SB_HP)]
        qs = [(q_ref[:, ln] * scale).astype(BF16) for ln in lanes]
        after = (_iota2((SB_TK, SB_TK), 0) > _iota2((SB_TK, SB_TK), 1)).astype(BF16)
        oraw_ref[...] = jnp.zeros_like(oraw_ref)

        def block(kb, mask, c_lf):
            rows = pl.ds(pl.multiple_of(kb * SB_TK, SB_TK), SB_TK)
            z = _each(lambda q, ln: _dot(q, k_ref[rows, ln], NT), qs, lanes)
            lg = _each(lambda x: _sb_logits(x, mask), z)
            surv = _each(lambda x: _suffix_sums(x[2], after), lg)
            att = _each(lambda x, s, c: jnp.exp(x[0] + s + c), lg, surv, c_lf)
            if mask is not None:
                att = _each(lambda a: jnp.where(mask, a, 0.0), att)
            pv = _each(lambda a, ln: _dot(a, v_ref[rows, ln], NN), att, lanes)
            for p, ln in zip(pv, lanes):
                oraw_ref[:, ln] += p
            return tuple(_each(lambda c, x: c + jnp.sum(x[2], axis=1, keepdims=True), c_lf, lg))

        carry = tuple(jnp.zeros((tq, 1), F32) for _ in range(SB_HP))
        for r in reversed(range(ndiag)):
            carry = block(qi * ndiag + r, _sb_diag_mask(tq, r), carry)
        lax.fori_loop(0, qi * ndiag, lambda i, c: block(qi * ndiag - 1 - i, None, c), carry)
        o_ref[...] = (oraw_ref[...] * _silu(z_ref[...])).astype(BF16)

    q_spec, k_spec, v_spec, z_spec, out, _ = _sb_specs(t, tq)
    return pl.pallas_call(
        body,
        name="sb_fwd",
        grid=(N_HEADS // SB_HP, t // tq),
        in_specs=[q_spec, k_spec, v_spec, z_spec],
        out_specs=[out, out],
        out_shape=[jax.ShapeDtypeStruct((t, D_MODEL), BF16), jax.ShapeDtypeStruct((t, D_MODEL), F32)],
    )(proj, proj, proj, proj)


def _sb_bwd(proj, oraw, do):
    t = proj.shape[0]
    tq = min(SB_TQ, t)
    ndiag = tq // SB_TK
    scale = 1.0 / math.sqrt(D_HEAD)

    def body(q_ref, k_ref, v_ref, z_ref, oraw_ref, do_ref, dq_ref, dk_ref, dv_ref, dz_ref, dk_acc, dv_acc,
             p_scr, z_scr):
        qi = pl.program_id(1)
        nq = pl.num_programs(1)

        @pl.when(qi == 0)
        def _():
            dk_acc[...] = jnp.zeros_like(dk_acc)
            dv_acc[...] = jnp.zeros_like(dv_acc)

        heads = range(SB_HP)
        lanes = [slice(hd * LANE, (hd + 1) * LANE) for hd in heads]
        zg = z_ref[...]
        sg = _sigmoid(zg)
        dog = do_ref[...].astype(F32)
        dz_ref[...] = (dog * oraw_ref[...] * (sg * (1.0 + zg * (1.0 - sg)))).astype(BF16)
        d_o = (dog * (zg * sg)).astype(BF16)
        d_o16 = [d_o[:, ln] for ln in lanes]
        qs = [(q_ref[:, ln] * scale).astype(BF16) for ln in lanes]
        ri = _iota2((SB_TK, SB_TK), 0)
        ci = _iota2((SB_TK, SB_TK), 1)
        after = (ri > ci).astype(BF16)
        earlier = (ri < ci).astype(BF16)

        def rows_of(kb):
            return pl.ds(pl.multiple_of(kb * SB_TK, SB_TK), SB_TK)

        def down(kb, mask, c_lf):
            rows = rows_of(kb)
            z = _each(lambda q, ln: _dot(q, k_ref[rows, ln], NT), qs, lanes)
            da = _each(lambda d, ln: _dot(d, v_ref[rows, ln], NT), d_o16, lanes)
            lg = _each(lambda x: _sb_logits(x, mask), z)
            surv = _each(lambda x: _suffix_sums(x[2], after), lg)
            att = _each(lambda x, s, c: jnp.exp(x[0] + s + c), lg, surv, c_lf)
            if mask is not None:
                att = _each(lambda a: jnp.where(mask, a, 0.0), att)
            dv = _each(lambda a, d: _dot(a, d, TN), att, d_o16)
            for hd in heads:
                p_scr[hd, kb] = att[hd] * da[hd]
                z_scr[hd, kb] = z[hd]
                dv_acc[rows, lanes[hd]] += dv[hd]
            return tuple(_each(lambda c, x: c + jnp.sum(x[2], axis=1, keepdims=True), c_lf, lg))

        c_lf = tuple(jnp.zeros((tq, 1), F32) for _ in heads)
        for r in reversed(range(ndiag)):
            c_lf = down(qi * ndiag + r, _sb_diag_mask(tq, r), c_lf)
        lax.fori_loop(0, qi * ndiag, lambda i, c: down(qi * ndiag - 1 - i, None, c), c_lf)

        def up(kb, mask, carry):
            dq, c_p = carry
            rows = rows_of(kb)
            p = [p_scr[hd, kb] for hd in heads]
            zs = [z_scr[hd, kb] for hd in heads]
            before = _each(lambda x, c: _suffix_sums(x, earlier) + c, p, c_p)
            e = _each(lambda x: jnp.exp(-jnp.abs(x)), zs)
            r = _each(lambda x: 1.0 / (1.0 + x), e)
            sig = _each(lambda x, a, b: jnp.where(x >= 0.0, b, a * b), zs, e, r)
            oms = _each(lambda x, a, b: jnp.where(x >= 0.0, a * b, b), zs, e, r)
            if mask is not None:
                sig = _each(lambda a: jnp.where(mask, a, 0.0), sig)
            dzz = _each(lambda x, o, g, b: x * o - g * b, p, oms, sig, before)
            dk = _each(lambda x, q: _dot(x, q, TN), dzz, qs)
            dq = _each(lambda a, x, ln: a + _dot(x, k_ref[rows, ln], NN), dq, dzz, lanes)
            for hd in heads:
                dk_acc[rows, lanes[hd]] += dk[hd]
            return tuple(dq), tuple(_each(lambda c, x: c + jnp.sum(x, axis=1, keepdims=True), c_p, p))

        carry = (tuple(jnp.zeros((tq, D_HEAD), F32) for _ in heads), tuple(jnp.zeros((tq, 1), F32) for _ in heads))
        carry = lax.fori_loop(0, qi * ndiag, lambda kb, c: up(kb, None, c), carry)
        for r in range(ndiag):
            carry = up(qi * ndiag + r, _sb_diag_mask(tq, r), carry)
        dq = carry[0]
        for hd in heads:
            dq_ref[:, lanes[hd]] = (dq[hd] * scale).astype(BF16)

        @pl.when(qi == nq - 1)
        def _():
            dk_ref[...] = dk_acc[...].astype(BF16)
            dv_ref[...] = dv_acc[...].astype(BF16)

    q_spec, k_spec, v_spec, z_spec, blk, full = _sb_specs(t, tq)
    o = jax.ShapeDtypeStruct((t, D_MODEL), BF16)
    w = SB_HP * LANE
    return pl.pallas_call(
        body,
        name="sb_bwd",
        grid=(N_HEADS // SB_HP, t // tq),
        in_specs=[q_spec, k_spec, v_spec, z_spec, blk, blk],
        out_specs=[blk, full, full, blk],
        out_shape=[o, o, o, o],
        scratch_shapes=[pltpu.VMEM((t, w), F32), pltpu.VMEM((t, w), F32)]
        + [pltpu.VMEM((SB_HP, t // SB_TK, tq, SB_TK), F32)] * 2,
    )(proj, proj, proj, proj, oraw, do)


def _mem_kv_fn(mem, mg, w):
    return mm_nn(_rmsnorm(mem, mg), w)


def _mem_kv(mem, mg, w):
    def body(m_ref, g_ref, w_ref, o_ref):
        o_ref[...] = _mem_kv_fn(m_ref[...], g_ref[...], w_ref[...])

    return pl.pallas_call(body, name="mem_kv", out_shape=jax.ShapeDtypeStruct((MEM_LEN, 2 * MEM_W), F32))(mem, mg, w)


def _mem_kv_bwd(mem, mg, w, dmkv):
    def body(m_ref, g_ref, w_ref, d_ref, dg_ref, dw_ref):
        _, vjp = jax.vjp(_mem_kv_fn, m_ref[...], g_ref[...], w_ref[...].astype(F32))
        _, dg, dw = vjp(d_ref[...])
        dg_ref[...] = dg
        dw_ref[...] = dw.astype(BF16)

    return pl.pallas_call(
        body, name="mem_kv_bwd",
        out_shape=[jax.ShapeDtypeStruct((1, D_MODEL), F32), jax.ShapeDtypeStruct((D_MODEL, 2 * MEM_W), BF16)],
    )(mem, mg, w, dmkv)


def _mem_attn(proj, mkv, tm=256):
    t = proj.shape[0]
    tm = min(tm, t)

    def body(q_ref, z_ref, kv_ref, o_ref):
        o_ref[...] = _mem_fn(q_ref[...], z_ref[...], kv_ref[...]).astype(BF16)

    return pl.pallas_call(
        body,
        name="mem_attn",
        grid=(t // tm,),
        in_specs=[pl.BlockSpec((tm, MEM_W), lambda i: (i, O_MQ // MEM_W)),
                  pl.BlockSpec((tm, MEM_W), lambda i: (i, O_MZ // MEM_W)),
                  pl.BlockSpec((MEM_LEN, 2 * MEM_W), lambda i: (0, 0))],
        out_specs=pl.BlockSpec((tm, MEM_W), lambda i: (i, 0)),
        out_shape=jax.ShapeDtypeStruct((t, MEM_W), BF16),
    )(proj, proj, mkv)


def _mem_attn_bwd(proj, mkv, do, tm=256):
    t = proj.shape[0]
    tm = min(tm, t)

    def body(q_ref, z_ref, kv_ref, do_ref, dq_ref, dz_ref, dkv_ref):
        _, vjp = jax.vjp(_mem_fn, q_ref[...], z_ref[...], kv_ref[...])
        dq, dz, dkv = vjp(do_ref[...].astype(F32))
        dq_ref[...] = dq.astype(BF16)
        dz_ref[...] = dz.astype(BF16)

        @pl.when(pl.program_id(0) == 0)
        def _():
            dkv_ref[...] = jnp.zeros_like(dkv_ref)

        dkv_ref[...] += dkv

    blk = pl.BlockSpec((tm, MEM_W), lambda i: (i, 0))
    kv = pl.BlockSpec((MEM_LEN, 2 * MEM_W), lambda i: (0, 0))
    return pl.pallas_call(
        body,
        name="mem_attn_bwd",
        grid=(t // tm,),
        in_specs=[pl.BlockSpec((tm, MEM_W), lambda i: (i, O_MQ // MEM_W)),
                  pl.BlockSpec((tm, MEM_W), lambda i: (i, O_MZ // MEM_W)), kv, blk],
        out_specs=[blk, blk, kv],
        out_shape=[jax.ShapeDtypeStruct((t, MEM_W), BF16), jax.ShapeDtypeStruct((t, MEM_W), BF16),
                   jax.ShapeDtypeStruct((MEM_LEN, 2 * MEM_W), F32)],
    )(proj, proj, mkv, do)


def _local_step(x, mem, tgt, norm_g, mem_norm_g, w_alt, conv_w, alog_row, dtb_row, dn_norm_g, w_mem_kv, w_br_dn, w_br_sb,
                w_br_mem, w_out, final_g):
    h = _norm_in(x, norm_g)
    proj = _matmul(h, w_alt, "nt", F32, 2048, 384, 1024, "proj")

    c = _dn_conv(proj, conv_w)
    o_dn, states = _dn_fwd(c, proj, alog_row, dtb_row, dn_norm_g)
    o_sb, o_sb_raw = _sb_fwd(proj)
    mkv = _mem_kv(mem, mem_norm_g, w_mem_kv)
    o_m = _mem_attn(proj, mkv)

    y_dn = _matmul(o_dn, w_br_dn, "nn", F32, 512, 1024, 1024, "y_dn")
    y_sb = _matmul(o_sb, w_br_sb, "nn", F32, 512, 1024, 1024, "y_sb")
    y_m = _matmul(o_m, w_br_mem, "nn", F32, 512, 1024, 1024, "y_m")
    merged = _merge(proj, y_dn, y_sb, y_m)
    mo = _matmul(merged, w_out, "nn", F32, 512, 1024, 1024, "mo")
    loss, dout, d_final_g = _loss_head(x, mo, final_g, tgt)

    dmerged = _matmul(dout, w_out, "nt", F32, 512, 1024, 1024, "dmerged")
    dw_out = _matmul(merged, dout, "tn", BF16, 256, 1024, 2048, "dw_out")
    dy_dn, dy_sb, dy_m, dg_dn, dg_sb, dg_m = _merge_bwd(proj, y_dn, y_sb, y_m, dmerged)
    do_dn = _matmul(dy_dn, w_br_dn, "nt", BF16, 512, 1024, 1024, "do_dn")
    do_sb = _matmul(dy_sb, w_br_sb, "nt", BF16, 512, 1024, 1024, "do_sb")
    do_m = _matmul(dy_m, w_br_mem, "nt", BF16, 512, 256, 1024, "do_m")
    dw_br_dn = _matmul(o_dn, dy_dn, "tn", BF16, 256, 1024, 2048, "dw_br_dn")
    dw_br_sb = _matmul(o_sb, dy_sb, "tn", BF16, 256, 1024, 2048, "dw_br_sb")
    dw_br_mem = _matmul(o_m, dy_m, "tn", BF16, 256, 1024, 2048, "dw_br_mem")

    dmq, dmz, dmkv = _mem_attn_bwd(proj, mkv, do_m)
    d_mem_norm_g, dw_mem_kv = _mem_kv_bwd(mem, mem_norm_g, w_mem_kv, dmkv)
    dq_sb, dk_sb, dv_sb, dz_sb = _sb_bwd(proj, o_sb_raw, do_sb)
    dcq, dck, dcv, dz_dn, dba, dscal, d_dn_norm_g = _dn_bwd(c, proj, alog_row, dtb_row, dn_norm_g, states, do_dn)
    dq_dn, dcw_q = _dn_conv_bwd(proj, conv_w, dcq, 0)
    dk_dn, dcw_k = _dn_conv_bwd(proj, conv_w, dck, 1)
    dv_dn, dcw_v = _dn_conv_bwd(proj, conv_w, dcv, 2)
    d_conv_w = jnp.concatenate([dcw_q, dcw_k, dcw_v], axis=1)

    dproj = jnp.concatenate([dq_dn, dk_dn, dv_dn, dz_dn, dq_sb, dk_sb, dv_sb, dz_sb, dmq, dmz, dg_dn, dg_sb, dg_m,
                             dba.astype(BF16)], axis=1)
    dh = _matmul(dproj, w_alt, "nn", F32, 512, 1024, 3968, "dh")
    dw_alt = _matmul(dproj, h, "tn", BF16, 384, 1024, 2048, "dw_alt")
    grad_x, d_norm_g = _norm_in_bwd(x, norm_g, dh, dout)
    return dict(loss=loss, grad_x=grad_x, norm_g=d_norm_g, mem_norm_g=d_mem_norm_g, w_alt=dw_alt, conv_w=d_conv_w,
                scal=dscal, dn_norm_g=d_dn_norm_g, w_mem_kv=dw_mem_kv, w_br_dn=dw_br_dn, w_br_sb=dw_br_sb,
                w_br_mem=dw_br_mem, w_out=dw_out, final_g=d_final_g)


MESH = pl.DeviceIdType.MESH
ANY = pl.BlockSpec(memory_space=pl.ANY)


def _position():
    return lax.axis_index("x"), lax.axis_index("y"), lax.axis_index("c")


def _all_gather(xs, name):
    n = len(xs)

    def body(*refs):
        x_refs, o_refs = refs[:n], refs[n:2 * n]
        send_sems, recv_sems, local_sems = refs[2 * n:]
        x, y, c = _position()
        me, sibling = (x, y, c), (x, y, 1 - c)
        x_nbr, y_nbr, diag = (1 - x, y, c), (x, 1 - y, c), (1 - x, 1 - y, c)
        south = c == 0
        relay_from = tuple(jnp.where(south, a, b) for a, b in zip(y_nbr, x_nbr))
        relay_to = tuple(jnp.where(south, a, b) for a, b in zip(x_nbr, y_nbr))

        def slot(p):
            return 4 * p[0] + 2 * p[1] + p[2]

        def copy(a, k, block, to, src=None):
            dst = o_refs[a].at[slot(block)]
            return pltpu.make_async_remote_copy(
                src_ref=dst if src is None else src, dst_ref=dst, send_sem=send_sems.at[7 * a + k],
                recv_sem=recv_sems.at[7 * a + k], device_id=to, device_id_type=MESH)

        mine = [pltpu.make_async_copy(x_refs[a], o_refs[a].at[slot(me)], local_sems.at[a]) for a in range(n)]
        for cp in mine:
            cp.start()
        sends = []
        for a in range(n):
            sends += [copy(a, 0, me, sibling, src=x_refs[a]), copy(a, 1, me, x_nbr, src=x_refs[a]),
                      copy(a, 2, me, y_nbr, src=x_refs[a])]
        for cp in sends:
            cp.start()
        later = []
        for a in range(n):
            copy(a, 1, x_nbr, me).wait_recv()
            copy(a, 2, y_nbr, me).wait_recv()
            later += [copy(a, 3, relay_from, relay_to), copy(a, 4, x_nbr, sibling), copy(a, 5, y_nbr, sibling)]
            for cp in later[-3:]:
                cp.start()
        for a in range(n):
            copy(a, 3, diag, me).wait_recv()
            later.append(copy(a, 6, diag, sibling))
            later[-1].start()
        for a in range(n):
            copy(a, 0, sibling, me).wait_recv()
            for k, chip in ((4, x_nbr), (5, y_nbr), (6, diag)):
                copy(a, k, (chip[0], chip[1], 1 - c), me).wait_recv()
        for cp in sends + later:
            cp.wait_send()
        for cp in mine:
            cp.wait()

    return pl.pallas_call(
        body,
        name=name,
        in_specs=[ANY] * n,
        out_specs=[ANY] * n,
        out_shape=[jax.ShapeDtypeStruct((N_DEV, *v.shape), v.dtype) for v in xs],
        scratch_shapes=[pltpu.SemaphoreType.DMA((7 * n,)), pltpu.SemaphoreType.DMA((7 * n,)),
                        pltpu.SemaphoreType.DMA((n,))],
    )(*xs)


def _window_view(ref, dest):
    return ref.at[pl.ds(LANE * WIN_START[dest], WIN_W), :]


def _chunk_rows(rows, cols):
    best = max(ch for ch in range(16, rows + 1, 16) if rows % ch == 0 and ch * cols <= (1 << 17))
    return best


def _halving_stage(xs, axis, name, out_dtype, windowed=()):
    n_arr = len(xs)
    metas = []
    for k, v in enumerate(xs):
        if k in windowed:
            metas.append((N_DEV // 2, WIN_W, v.shape[1]))
        else:
            assert v.shape[1] == 2
            metas.append((v.shape[0], v.shape[2], v.shape[3]))
    chunk = [_chunk_rows(r, c) for (_, r, c) in metas]
    offs = [sum(m[0] for m in metas[:k]) for k in range(n_arr)]
    n_sem = sum(m[0] for m in metas)

    def body(*refs):
        x_refs = refs[:n_arr]
        o_refs = refs[n_arr:2 * n_arr]
        land_refs = refs[2 * n_arr:3 * n_arr]
        rest = refs[3 * n_arr:]
        bufs = rest[:3 * n_arr]
        send_sems, recv_sems, in_sems, out_sems = rest[3 * n_arr:]
        pos = dict(zip("xyc", _position()))
        bit = pos[axis]
        peer = tuple(1 - pos[a] if a == axis else pos[a] for a in "xyc")

        def view(k, i, b):
            if k in windowed:
                return _window_view(x_refs[k], 2 * i + b)
            return x_refs[k].at[i, b]

        def add_blocks(k, a_view, b_view, o_view):
            _hbm_add(a_view, b_view, o_view, bufs[3 * k:3 * k + 3], in_sems, out_sems, chunk[k])

        for b in (0, 1):
            @pl.when(bit == b)
            def _(b=b):
                sends = []
                for k in range(n_arr):
                    for i in range(metas[k][0]):
                        cp = pltpu.make_async_remote_copy(
                            src_ref=view(k, i, 1 - b), dst_ref=land_refs[k].at[i], send_sem=send_sems.at[offs[k] + i],
                            recv_sem=recv_sems.at[offs[k] + i], device_id=peer, device_id_type=MESH)
                        cp.start()
                        sends.append(cp)
                idx = 0
                for k in range(n_arr):
                    for i in range(metas[k][0]):
                        sends[idx].wait_recv()
                        add_blocks(k, view(k, i, b), land_refs[k].at[i], o_refs[k].at[i])
                        idx += 1
                for cp in sends:
                    cp.wait_send()

    out_shape = [jax.ShapeDtypeStruct(m, out_dtype) for m in metas]
    land_shape = [jax.ShapeDtypeStruct(m, v.dtype) for m, v in zip(metas, xs)]
    scratch = []
    for k in range(n_arr):
        blk = (2, chunk[k], metas[k][2])
        scratch += [pltpu.VMEM(blk, xs[k].dtype)] * 2 + [pltpu.VMEM(blk, out_dtype)]
    scratch += [pltpu.SemaphoreType.DMA((n_sem,)), pltpu.SemaphoreType.DMA((n_sem,)),
                pltpu.SemaphoreType.DMA((2, 2)), pltpu.SemaphoreType.DMA((2,))]
    outs = pl.pallas_call(
        body,
        name=name,
        in_specs=[ANY] * n_arr,
        out_specs=[ANY] * (2 * n_arr),
        out_shape=out_shape + land_shape,
        scratch_shapes=scratch,
    )(*xs)
    return outs[:n_arr]


def _hbm_add(a_view, b_view, o_view, bufs, in_sems, out_sems, ch):
    rows = a_view.shape[0]
    nch = rows // ch
    va, vb, vo = bufs

    def rows_of(j):
        return pl.ds(pl.multiple_of(j * ch, 16), ch)

    def loads(j, s):
        return (pltpu.make_async_copy(a_view.at[rows_of(j), :], va.at[s], in_sems.at[0, s]),
                pltpu.make_async_copy(b_view.at[rows_of(j), :], vb.at[s], in_sems.at[1, s]))

    def store(j, s):
        return pltpu.make_async_copy(vo.at[s], o_view.at[rows_of(j), :], out_sems.at[s])

    for cp in loads(0, 0):
        cp.start()

    def step(j, _):
        s = lax.rem(j, 2)

        @pl.when(j + 1 < nch)
        def _():
            for cp in loads(j + 1, 1 - s):
                cp.start()

        for cp in loads(j, s):
            cp.wait()

        @pl.when(j >= 2)
        def _():
            store(j - 2, s).wait()

        vo[s] = (va[s].astype(F32) + vb[s].astype(F32)).astype(vo.dtype)
        store(j, s).start()
        return 0

    lax.fori_loop(0, nch, step, 0)
    for j in range(max(0, nch - 2), nch):
        store(j, j % 2).wait()


def _xy_stage(xs, first, name):
    n_arr = len(xs)
    if first:
        shapes = [(v.shape[2] // 2, v.shape[3]) for v in xs]
        ins = list(xs)
    else:
        shapes = [(a.shape[1], a.shape[2]) for a, _ in xs]
        ins = [v for pair in xs for v in pair]
    n_blk = 2 if first else 1
    out_dtype = BF16 if first else F32
    chunk = [_chunk_rows(r, c) for (r, c) in shapes]
    n_sem = 2 * n_blk * n_arr

    def body(*refs):
        n_in = len(ins)
        in_refs = refs[:n_in]
        n_out = 2 * n_arr if first else n_arr
        o_refs = refs[n_in:n_in + n_out]
        land = refs[n_in + n_out:n_in + n_out + 2 * n_arr]
        rest = refs[n_in + n_out + 2 * n_arr:]
        bufs = rest[:3 * n_arr]
        send_sems, recv_sems, in_sems, out_sems = rest[3 * n_arr:]
        x, y, c = _position()
        peers = {"x": (1 - x, y, c), "y": (x, 1 - y, c)}
        jobs = []
        for k in range(n_arr):
            r, _ = shapes[k]
            half_a, half_b = pl.ds(0, r), pl.ds(r, r)
            if first:
                src = in_refs[k]
                for i in range(2):
                    jobs.append((k, src.at[i, 1 - y, half_a, :], src.at[i, y, half_a, :], land[2 * k].at[i],
                                 o_refs[2 * k].at[i], "y"))
                    jobs.append((k, src.at[1 - x, i, half_b, :], src.at[x, i, half_b, :], land[2 * k + 1].at[i],
                                 o_refs[2 * k + 1].at[i], "x"))
            else:
                a1, b1 = in_refs[2 * k], in_refs[2 * k + 1]
                jobs.append((k, a1.at[1 - x], a1.at[x], land[2 * k], o_refs[k].at[half_a, :], "x"))
                jobs.append((k, b1.at[1 - y], b1.at[y], land[2 * k + 1], o_refs[k].at[half_b, :], "y"))
        sends = []
        for n, (k, send, _, landing, _, axis) in enumerate(jobs):
            cp = pltpu.make_async_remote_copy(src_ref=send, dst_ref=landing, send_sem=send_sems.at[n],
                                              recv_sem=recv_sems.at[n], device_id=peers[axis], device_id_type=MESH)
            cp.start()
            sends.append(cp)
        for cp, (k, _, kept, landing, out, _) in zip(sends, jobs):
            cp.wait_recv()
            _hbm_add(kept, landing, out, bufs[3 * k:3 * k + 3], in_sems, out_sems, chunk[k])
        for cp in sends:
            cp.wait_send()

    if first:
        out_shape = [jax.ShapeDtypeStruct((2, r, c), BF16) for (r, c) in shapes for _ in range(2)]
        land_shape = out_shape
    else:
        out_shape = [jax.ShapeDtypeStruct((2 * r, c), F32) for (r, c) in shapes]
        land_shape = [jax.ShapeDtypeStruct((r, c), BF16) for (r, c) in shapes for _ in range(2)]
    scratch = []
    for k in range(n_arr):
        scratch += [pltpu.VMEM((2, chunk[k], shapes[k][1]), BF16)] * 2 + [pltpu.VMEM((2, chunk[k], shapes[k][1]), out_dtype)]
    scratch += [pltpu.SemaphoreType.DMA((n_sem,)), pltpu.SemaphoreType.DMA((n_sem,)),
                pltpu.SemaphoreType.DMA((2, 2)), pltpu.SemaphoreType.DMA((2,))]
    outs = pl.pallas_call(
        body,
        name=name,
        in_specs=[ANY] * len(ins),
        out_specs=[ANY] * (len(out_shape) + len(land_shape)),
        out_shape=out_shape + land_shape,
        scratch_shapes=scratch,
    )(*ins)
    outs = outs[:len(out_shape)]
    return [(outs[2 * k], outs[2 * k + 1]) for k in range(n_arr)] if first else list(outs)


def _reduce_scatter(dw_al, blocks):
    xs = [dw_al] + [b.reshape(N_DEV // 2, 2, *b.shape[1:]) for b in blocks]
    ys = _halving_stage(xs, "c", "rs_c", BF16, windowed=(0,))
    pairs = _xy_stage([v.reshape(2, 2, *v.shape[1:]) for v in ys], True, "rs_xy1")
    return _xy_stage(pairs, False, "rs_xy2")


def _sum_slots(gs):
    n = len(gs)

    def body(*refs):
        for g_ref, o_ref in zip(refs[:n], refs[n:]):
            acc = g_ref[0]
            for d in range(1, N_DEV):
                acc = acc + g_ref[d]
            o_ref[...] = acc

    return pl.pallas_call(body, name="sum_slots",
                          out_shape=[jax.ShapeDtypeStruct(g.shape[1:], g.dtype) for g in gs])(*gs)


def _assemble_w_al(wins, bas):
    ba_tile = O_BA // LANE
    assert W_AL // LANE == ba_tile + 1
    cols = wins.shape[2]
    n_buf = 3
    ends = [WIN_START[d + 1] if d + 1 < N_DEV else ba_tile + 1 for d in range(N_DEV)]
    assert WIN_START[N_DEV - 1] + WIN_TILES == ba_tile + 1

    def body(w_ref, ba_ref, o_ref, buf, ld_sems, st_sems, ba_sem):
        def load(d):
            return pltpu.make_async_copy(w_ref.at[d], buf.at[d % n_buf], ld_sems.at[d % n_buf])

        def store(d):
            n = LANE * (ends[d] - WIN_START[d])
            return pltpu.make_async_copy(buf.at[d % n_buf, pl.ds(0, n), :],
                                         o_ref.at[pl.ds(LANE * WIN_START[d], n), :], st_sems.at[d % n_buf])

        load(0).start()
        for d in range(N_DEV):
            if d + 1 < N_DEV:
                if d + 1 >= n_buf:
                    store(d + 1 - n_buf).wait()
                load(d + 1).start()
            load(d).wait()
            if d > 0:
                ov = LANE * (WIN_START[d - 1] + WIN_TILES - WIN_START[d])
                buf[d % n_buf, :ov, :] = buf[d % n_buf, :ov, :] + buf[(d - 1) % n_buf, WIN_W - ov:, :]
            if d == N_DEV - 1:
                ba_copy = pltpu.make_async_copy(
                    ba_ref.at[BA_DEV], buf.at[d % n_buf, pl.ds(WIN_W - LANE, ba_ref.shape[1]), :], ba_sem)
                ba_copy.start()
                ba_copy.wait()
            store(d).start()
        for d in range(N_DEV - n_buf, N_DEV):
            store(d).wait()

    return pl.pallas_call(
        body,
        name="assemble_w_al",
        in_specs=[ANY, ANY],
        out_specs=ANY,
        out_shape=jax.ShapeDtypeStruct((W_AL, cols), wins.dtype),
        scratch_shapes=[pltpu.VMEM((n_buf, WIN_W, cols), wins.dtype), pltpu.SemaphoreType.DMA((n_buf,)),
                        pltpu.SemaphoreType.DMA((n_buf,)), pltpu.SemaphoreType.DMA],
    )(wins, bas)


def _adamw_math(w, g, m, v):
    m_new = ADAM_B1 * m + (1.0 - ADAM_B1) * g
    v_new = ADAM_B2 * v + (1.0 - ADAM_B2) * (g * g)
    m_hat = m_new / (1.0 - ADAM_B1 ** ADAM_STEP)
    v_hat = v_new / (1.0 - ADAM_B2 ** ADAM_STEP)
    return -ADAM_LR * (m_hat / (jnp.sqrt(v_hat) + ADAM_EPS) + ADAM_WD * w), m_new, v_new


def _adamw(w, g, m, v, name, tb=134):
    r, _, c = w.shape
    assert r % tb == 0

    def body(w_ref, g_ref, m_ref, v_ref, d_ref, nm_ref, nv_ref):
        d_ref[...], nm_ref[...], nv_ref[...] = _adamw_math(w_ref[...], g_ref[...], m_ref[...], v_ref[...])

    blk = pl.BlockSpec((tb, 1, c), lambda i: (i, 0, 0))
    o = jax.ShapeDtypeStruct(w.shape, F32)
    return pl.pallas_call(body, name=name, grid=(r // tb,), in_specs=[blk] * 4, out_specs=[blk] * 3,
                          out_shape=[o, o, o])(w, g, m, v)


def _adamw_many(ws, gs, ms, vs, name):
    n = len(ws)

    def body(*refs):
        for k in range(n):
            w_ref, g_ref, m_ref, v_ref = (refs[j * n + k] for j in range(4))
            d_ref, nm_ref, nv_ref = (refs[(4 + j) * n + k] for j in range(3))
            d_ref[...], nm_ref[...], nv_ref[...] = _adamw_math(w_ref[...], g_ref[...], m_ref[...], v_ref[...])

    shapes = [jax.ShapeDtypeStruct(w.shape, F32) for w in ws]
    outs = pl.pallas_call(body, name=name, out_shape=shapes * 3)(*ws, *gs, *ms, *vs)
    return outs[:n], outs[n:2 * n], outs[2 * n:]


def _select(me, table):
    return sum(jnp.where(me == d, jnp.int32(v), jnp.int32(0)) for d, v in enumerate(table))


WIN_SHIFT = tuple(SHARD_W * d - LANE * WIN_START[d] for d in range(N_DEV))
PAD_L = 256
PAD_R = 256


def _shard_to_window(shard_t, me):
    shift = _select(me, WIN_SHIFT)
    start = _select(me, WIN_START)
    padded = jnp.pad(shard_t, ((PAD_L, PAD_R), (0, 0)))
    cols = shard_t.shape[1]
    lo = lax.dynamic_slice(padded, (PAD_L - shift, 0), (WIN_W, cols))
    hi = lax.dynamic_slice(padded, (PAD_L - shift + N_BA, 0), (WIN_W, cols))
    aligned = LANE * start + lax.broadcasted_iota(jnp.int32, (WIN_W, 1), 0)
    return jnp.where(aligned >= ORIG_BA, hi, lo)


def _window_to_shard(win, ba_grad, me):
    shift = _select(me, WIN_SHIFT)
    cols = win.shape[1]
    padded = jnp.pad(win, ((N_BA, PAD_R), (0, 0)))
    lo = lax.dynamic_slice(padded, (N_BA + shift, 0), (SHARD_W, cols))
    hi = lax.dynamic_slice(padded, (shift, 0), (SHARD_W, cols))
    orig = SHARD_W * me + lax.broadcasted_iota(jnp.int32, (SHARD_W, 1), 0)
    ba_full = lax.dynamic_update_slice(jnp.zeros((SHARD_W, cols), win.dtype), ba_grad, (BA_LOCAL, 0))
    return jnp.where(orig < ORIG_BA, lo, jnp.where(orig >= ORIG_BA + N_BA, hi, ba_full))


def _pad_row(v, width=D_MODEL):
    v = v.reshape(1, -1)
    return jnp.pad(v, ((0, 0), (0, width - v.shape[1])))


def _slab(v, rows=8):
    return jnp.pad(v, ((0, rows - v.shape[0]), (0, D_MODEL - v.shape[1])))


def kernel(x, mem, norm_g, mem_norm_g, w_in, conv_w, a_log, dt_bias, dn_norm_g, w_mem_kv, w_br_dn, w_br_sb, w_br_mem, w_out, final_g, loss_target, m_norm_g, m_mem_norm_g, m_w_in, m_conv_w, m_a_log, m_dt_bias, m_dn_norm_g, m_w_mem_kv, m_w_br_dn, m_w_br_sb, m_w_br_mem, m_w_out, m_final_g, v_norm_g, v_mem_norm_g, v_w_in, v_conv_w, v_a_log, v_dt_bias, v_dn_norm_g, v_w_mem_kv, v_w_br_dn, v_w_br_sb, v_w_br_mem, v_w_out, v_final_g):
    xi, yi, ci = _position()
    me = 4 * xi + 2 * yi + ci

    shard_t = w_in[0].T
    win = _shard_to_window(shard_t, me).astype(BF16)
    ba = shard_t[BA_LOCAL:BA_LOCAL + N_BA, :].astype(BF16)
    g_win, g_ba, g_kv, g_dn, g_sb, g_out, g_mem, g_conv = _all_gather(
        [win, ba, w_mem_kv[0].astype(BF16), w_br_dn[0].astype(BF16), w_br_sb[0].astype(BF16), w_out[0].astype(BF16),
         w_br_mem[0].astype(BF16), conv_w[0]], "gather_weights")
    w_alt = _assemble_w_al(g_win, g_ba)
    w_mem_kv_f = g_kv.reshape(D_MODEL, 2 * MEM_W)
    w_br_dn_f = g_dn.reshape(D_MODEL, D_MODEL)
    w_br_sb_f = g_sb.reshape(D_MODEL, D_MODEL)
    w_out_f = g_out.reshape(D_MODEL, D_MODEL)
    w_br_mem_f = g_mem.transpose(1, 0, 2).reshape(MEM_W, D_MODEL)
    conv_w_f = g_conv.transpose(1, 0, 2).reshape(CONV_K, 3 * D_MODEL)

    r = _local_step(x[0], mem[0], loss_target[0], norm_g, mem_norm_g, w_alt, conv_w_f, _pad_row(a_log, LANE),
                    _pad_row(dt_bias, LANE), dn_norm_g, w_mem_kv_f, w_br_dn_f, w_br_sb_f, w_br_mem_f, w_out_f,
                    final_g.reshape(1, D_MODEL))

    dw_alt = r["w_alt"]
    rows_d = D_MODEL // N_DEV
    small_blocks = jnp.concatenate([
        r["w_br_dn"].reshape(N_DEV, rows_d, D_MODEL), r["w_br_sb"].reshape(N_DEV, rows_d, D_MODEL),
        r["w_out"].reshape(N_DEV, rows_d, D_MODEL), r["w_mem_kv"].reshape(N_DEV, rows_d // 2, D_MODEL),
        r["w_br_mem"].reshape(MEM_W, N_DEV, rows_d).transpose(1, 0, 2).reshape(N_DEV, MEM_W // N_DEV, D_MODEL)], axis=1)
    g_win, g_small = _reduce_scatter(dw_alt, [small_blocks])
    g_dn, g_sb, g_out = (g_small[k * rows_d:(k + 1) * rows_d] for k in range(3))
    g_kv = g_small[3 * rows_d:3 * rows_d + rows_d // 2].reshape(rows_d, 2 * MEM_W)
    g_mem = g_small[3 * rows_d + rows_d // 2:].reshape(MEM_W, rows_d)
    parts = [r["norm_g"], r["mem_norm_g"], r["final_g"], r["dn_norm_g"], r["scal"], r["loss"], r["conv_w"],
             dw_alt[O_BA:O_BA + N_BA, :].astype(F32)]
    s_norm_g, s_mem_norm_g, s_final_g, s_dn_norm_g, s_scal, s_loss, s_conv, s_ba = _sum_slots(
        _all_gather(parts, "gather_small"))
    loss = s_loss[0, 0]
    cw = conv_w.shape[2]
    g_conv = lax.dynamic_slice(s_conv, (0, cw * me), (CONV_K, cw))
    g_w_in_t = _window_to_shard(g_win, s_ba, me)
    grads = dict(norm_g=s_norm_g, mem_norm_g=s_mem_norm_g, w_in=g_w_in_t.T[None], conv_w=g_conv[None],
                 a_log=s_scal[0:1, :N_HEADS], dt_bias=s_scal[1:2, :N_HEADS], dn_norm_g=s_dn_norm_g, w_mem_kv=g_kv[None],
                 w_br_dn=g_dn[None], w_br_sb=g_sb[None], w_br_mem=g_mem[None], w_out=g_out[None],
                 final_g=s_final_g.reshape(D_MODEL))

    params = dict(norm_g=(norm_g, m_norm_g, v_norm_g), mem_norm_g=(mem_norm_g, m_mem_norm_g, v_mem_norm_g),
                  w_in=(w_in, m_w_in, v_w_in), conv_w=(conv_w, m_conv_w, v_conv_w), a_log=(a_log, m_a_log, v_a_log),
                  dt_bias=(dt_bias, m_dt_bias, v_dt_bias), dn_norm_g=(dn_norm_g, m_dn_norm_g, v_dn_norm_g),
                  w_mem_kv=(w_mem_kv, m_w_mem_kv, v_w_mem_kv), w_br_dn=(w_br_dn, m_w_br_dn, v_w_br_dn),
                  w_br_sb=(w_br_sb, m_w_br_sb, v_w_br_sb), w_br_mem=(w_br_mem, m_w_br_mem, v_w_br_mem),
                  w_out=(w_out, m_w_out, v_w_out), final_g=(final_g, m_final_g, v_final_g))
    order = list(params)
    deltas, new_m, new_v = {}, {}, {}
    deltas["w_in"], new_m["w_in"], new_v["w_in"] = (jnp.transpose(o, (1, 2, 0)) for o in _adamw(
        jnp.transpose(w_in, (2, 0, 1)), g_w_in_t[:, None, :], jnp.transpose(m_w_in, (2, 0, 1)),
        jnp.transpose(v_w_in, (2, 0, 1)), "adamw_w_in"))
    rest = [nm for nm in order if nm != "w_in"]

    def two_d(a):
        return a.reshape(1, -1) if a.ndim == 1 else a

    d_l, m_l, v_l = _adamw_many([two_d(params[nm][0]) for nm in rest], [two_d(grads[nm]) for nm in rest],
                                [two_d(params[nm][1]) for nm in rest], [two_d(params[nm][2]) for nm in rest], "adamw_rest")
    for k, nm in enumerate(rest):
        shp = params[nm][0].shape
        deltas[nm], new_m[nm], new_v[nm] = d_l[k].reshape(shp), m_l[k].reshape(shp), v_l[k].reshape(shp)
    return (loss, r["grad_x"][None], *[grads[nm] for nm in order], *[deltas[nm] for nm in order],
            *[new_m[nm] for nm in order], *[new_v[nm] for nm in order])
```

```python
import functools
import math

import jax
import jax.numpy as jnp
from jax import lax
from jax.experimental import pallas as pl
from jax.experimental.pallas import tpu as pltpu

F32 = jnp.float32
BF16 = jnp.bfloat16

D_MODEL = 1024
N_DEV = 8
N_HEADS = 8
D_HEAD = 128
DN_CHUNK = 64
CONV_K = 4
MEM_LEN = 256
MEM_HEADS = 4
MEM_DH = 64
MEM_W = MEM_HEADS * MEM_DH
NORM_EPS = 1e-6
IN_WIDTH = 11792
SHARD_W = IN_WIDTH // N_DEV

LANE = 128
SUPER = 2 * DN_CHUNK

O_QKV_DN = 0
O_Z_DN = 3072
O_QKV_SB = 4096
O_Z_SB = 7168
O_MQ = 8192
O_MZ = 8448
O_GATES = 8704
O_BA = 11776
W_AL = 11904
ORIG_BA = 4096
N_BA = 16

WIN_TILES = 13
WIN_W = WIN_TILES * LANE


def _aligned_col(o):
    return o if o < ORIG_BA else o - N_BA


WIN_START = tuple(min(_aligned_col(SHARD_W * d) // LANE, (W_AL // LANE) - WIN_TILES) for d in range(N_DEV))
WIN_OFF = tuple(_aligned_col(SHARD_W * d) - LANE * WIN_START[d] if SHARD_W * d >= ORIG_BA + N_BA or SHARD_W * d < ORIG_BA
                else None for d in range(N_DEV))
BA_DEV = ORIG_BA // SHARD_W
BA_LOCAL = ORIG_BA - BA_DEV * SHARD_W

ADAM_LR = 0.001
ADAM_B1 = 0.9
ADAM_B2 = 0.999
ADAM_EPS = 1e-08
ADAM_WD = 0.01
ADAM_STEP = 10

NN = (((1,), (0,)), ((), ()))
NT = (((1,), (1,)), ((), ()))
TN = (((0,), (0,)), ((), ()))


def _dot(a, b, dims):
    return lax.dot_general(a.astype(BF16), b.astype(BF16), dims, preferred_element_type=F32)


def _split2(a):
    hi = a.astype(BF16)
    lo = (a - hi.astype(F32)).astype(BF16)
    return hi, lo


def _dot3(a, b, dims):
    ah, al = _split2(a)
    bh, bl = _split2(b)
    d = functools.partial(lax.dot_general, dimension_numbers=dims, preferred_element_type=F32)
    return d(ah, bh) + (d(ah, bl) + d(al, bh))


def _sel_dot_impl(sel01, x, dims):
    sel = sel01.astype(BF16)
    h1 = x.astype(BF16)
    r1 = x - h1.astype(F32)
    h2 = r1.astype(BF16)
    h3 = (r1 - h2.astype(F32)).astype(BF16)
    d = functools.partial(lax.dot_general, dimension_numbers=dims, preferred_element_type=F32)
    return d(sel, h1) + (d(sel, h2) + d(sel, h3))


@jax.custom_vjp
def _sel_dot(sel01, x):
    return _sel_dot_impl(sel01, x, NN)


_sel_dot.defvjp(lambda s, x: (_sel_dot(s, x), s),
                lambda s, g: (jnp.zeros_like(s), _sel_dot_impl(s, g, TN)))


def _make_mm(dotfn):
    @jax.custom_vjp
    def nn(a, b):
        return dotfn(a, b, NN)

    @jax.custom_vjp
    def nt(a, b):
        return dotfn(a, b, NT)

    @jax.custom_vjp
    def tn(a, b):
        return dotfn(a, b, TN)

    nn.defvjp(lambda a, b: (nn(a, b), (a, b)), lambda r, g: (nt(g, r[1]), tn(r[0], g)))
    nt.defvjp(lambda a, b: (nt(a, b), (a, b)), lambda r, g: (nn(g, r[1]), tn(g, r[0])))
    tn.defvjp(lambda a, b: (tn(a, b), (a, b)), lambda r, g: (nt(r[1], g), nn(r[0], g)))
    return nn, nt, tn


mm_nn, mm_nt, mm_tn = _make_mm(_dot)
mm3_nn, mm3_nt, mm3_tn = _make_mm(_dot3)


def _sigmoid(x):
    return jax.nn.sigmoid(x)


def _silu(x):
    return x * _sigmoid(x)


def _softplus_parts(x):
    sp = jnp.log1p(jnp.exp(-jnp.abs(x)))
    return jnp.maximum(x, 0.0) + sp, jnp.maximum(-x, 0.0) + sp


def _rmsnorm(x, g):
    return x * lax.rsqrt(jnp.mean(x * x, axis=-1, keepdims=True) + NORM_EPS) * g


def _iota2(shape, dim):
    return lax.broadcasted_iota(jnp.int32, shape, dim)


def _div64(i):
    return lax.shift_right_logical(i, jnp.full(i.shape, 6, jnp.int32))


def _each(f, *lists):
    return [f(*a) for a in zip(*lists)]


@jax.custom_vjp
def _inv_unit_lower(ms):
    n = ms[0].shape[0]
    eye = (_iota2((n, n), 0) == _iota2((n, n), 1)).astype(F32)
    rs = [eye - m for m in ms]
    ps = ms
    for _ in range(5):
        ps = _each(mm3_nn, ps, ps)
        rs = _each(lambda r, p: r + mm_nn(r, p), rs, ps)
    return rs


def _inv_fwd(ms):
    rs = _inv_unit_lower(ms)
    return rs, rs


def _inv_bwd(rs, gs):
    ts = _each(mm_tn, rs, gs)
    return (_each(lambda t, r: -mm_nt(t, r), ts, rs),)


_inv_unit_lower.defvjp(_inv_fwd, _inv_bwd)


def _dn_block(cq, ck, cv, bcol, acol, zt, alog, dtb, gn, s0):
    n = SUPER
    h = DN_CHUNK
    row = _iota2((n, n), 0)
    col = _iota2((n, n), 1)
    same = _div64(row) == _div64(col)
    incl = jnp.logical_and(same, row >= col)
    strict = jnp.logical_and(same, row > col)
    incl_f = incl.astype(F32)

    qn = _each(lambda x: x * lax.rsqrt(jnp.sum(x * x, axis=-1, keepdims=True) + NORM_EPS) * (D_HEAD ** -0.5), cq)
    kn = _each(lambda x: x * lax.rsqrt(jnp.sum(x * x, axis=-1, keepdims=True) + NORM_EPS), ck)
    beta = _each(_sigmoid, bcol)
    g = _each(lambda al, ac, dt: -(jnp.exp(al) * _softplus_parts(ac + dt)[0]), alog, acol, dtb)
    gcum = _each(lambda x: _sel_dot(incl_f, jnp.broadcast_to(x, (n, n))), g)
    gam_incl = _each(lambda x: jnp.where(incl, jnp.exp(jnp.where(incl, x - x.T, 0.0)), 0.0), gcum)
    kk = _each(mm_nt, kn, kn)
    t_inv = _inv_unit_lower(_each(lambda b, x, gm: b * x * jnp.where(strict, gm, 0.0), beta, kk, gam_incl))
    eg = _each(jnp.exp, gcum)
    u = _each(lambda t, v, b: mm_nn(t, v * b), t_inv, cv, beta)
    w = _each(lambda t, k, b, e: mm_nn(t, k * (b * e)), t_inv, kn, beta, eg)
    a_intra = _each(lambda q, k, gm: mm_nt(q, k) * gm, qn, kn, gam_incl)
    q_dec = _each(lambda q, e: q * e, qn, eg)
    last0 = _each(lambda x: x[h - 1:h, :], gcum)
    last1 = _each(lambda x: x[n - 1:n, :], gcum)
    k_dec = _each(lambda k, x, l0, l1: k * jnp.exp(jnp.concatenate(
        [jnp.broadcast_to(l0, (h, n)), jnp.broadcast_to(l1, (h, n))], axis=0) - x), kn, gcum, last0, last1)
    v0 = _each(lambda uu, ww, s: uu[:h] - mm_nn(ww[:h], s), u, w, s0)
    o0 = _each(lambda q, s: mm_nn(q[:h], s), q_dec, s0)
    s1 = _each(lambda s, l0, k, v: s * jnp.exp(l0) + mm_tn(k[:h], v), s0, last0, k_dec, v0)
    v1 = _each(lambda uu, ww, s: uu[h:] - mm_nn(ww[h:], s), u, w, s1)
    o1 = _each(lambda q, s: mm_nn(q[h:], s), q_dec, s1)
    s2 = _each(lambda s, l1, k, v: s * jnp.exp(l1) + mm_tn(k[h:], v), s1, last1, k_dec, v1)
    o = _each(lambda a, b, am, x, y: jnp.concatenate([a, b], axis=0) + mm_nn(am, jnp.concatenate([x, y], axis=0)),
              o0, o1, a_intra, v0, v1)
    out = _each(lambda x, z: _rmsnorm(x, gn) * _silu(z), o, zt)
    return out, s2


def _mem_fn(mq, mz, mkv):
    mk = mkv[:, :MEM_W]
    mv = mkv[:, MEM_W:]
    lane = _iota2((1, MEM_W), 1)
    out = jnp.zeros(mq.shape, F32)
    for hd in range(MEM_HEADS):
        hm = (_div64(lane) == hd).astype(F32)
        s = mm_nt(mq * hm, mk) * (1.0 / math.sqrt(MEM_DH))
        s = s - jnp.max(s, axis=-1, keepdims=True)
        e = jnp.exp(s)
        p = e / jnp.sum(e, axis=-1, keepdims=True)
        out = out + mm_nn(p, mv) * hm
    return out * _silu(mz)


def _merge_fn(gd, gs, gm, yd, ys, ym):
    return _sigmoid(gd) * yd + _sigmoid(gs) * ys + _sigmoid(gm) * ym


def _loss_fn(x, mo, fg, tgt):
    y = _rmsnorm(x + mo, fg)
    err = y - tgt
    return 0.5 * jnp.sum(jnp.mean(err * err, axis=-1, keepdims=True), axis=0, keepdims=True)


def _matmul(a, b, mode, out_dtype, tm, tn, tk, name, b_col0=0, n_cols=None):
    if mode == "nn":
        m, kdim = a.shape
        n = b.shape[1] if n_cols is None else n_cols
    elif mode == "nt":
        m, kdim = a.shape
        n = b.shape[0]
    else:
        kdim, m = a.shape
        n = b.shape[1] if n_cols is None else n_cols
    tm, tn, tk = min(tm, m), min(tn, n), min(tk, kdim)
    assert m % tm == 0 and n % tn == 0 and kdim % tk == 0 and b_col0 % tn == 0
    nk = kdim // tk
    jb = b_col0 // tn
    dims = {"nn": NN, "nt": NT, "tn": TN}[mode]

    def body(a_ref, b_ref, o_ref, acc_ref):
        k = pl.program_id(2)
        part = _dot(a_ref[...], b_ref[...], dims)

        @pl.when(k == 0)
        def _():
            acc_ref[...] = part

        @pl.when(k > 0)
        def _():
            acc_ref[...] += part

        @pl.when(k == nk - 1)
        def _():
            o_ref[...] = acc_ref[...].astype(o_ref.dtype)

    if mode == "nn":
        a_spec = pl.BlockSpec((tm, tk), lambda i, j, k: (i, k))
        b_spec = pl.BlockSpec((tk, tn), lambda i, j, k: (k, j + jb))
    elif mode == "nt":
        a_spec = pl.BlockSpec((tm, tk), lambda i, j, k: (i, k))
        b_spec = pl.BlockSpec((tn, tk), lambda i, j, k: (j, k))
    else:
        a_spec = pl.BlockSpec((tk, tm), lambda i, j, k: (k, i))
        b_spec = pl.BlockSpec((tk, tn), lambda i, j, k: (k, j + jb))
    return pl.pallas_call(
        body,
        name=name,
        grid=(m // tm, n // tn, nk),
        in_specs=[a_spec, b_spec],
        out_specs=pl.BlockSpec((tm, tn), lambda i, j, k: (i, j)),
        out_shape=jax.ShapeDtypeStruct((m, n), out_dtype),
        scratch_shapes=[pltpu.VMEM((tm, tn), F32)],
        compiler_params=pltpu.CompilerParams(dimension_semantics=("parallel", "parallel", "arbitrary")),
    )(a, b)


def _norm_in(x, g, tm=256):
    t = x.shape[0]

    def body(x_ref, g_ref, h_ref):
        h_ref[...] = _rmsnorm(x_ref[...], g_ref[...]).astype(BF16)

    return pl.pallas_call(
        body,
        name="norm_in",
        grid=(t // tm,),
        in_specs=[pl.BlockSpec((tm, D_MODEL), lambda i: (i, 0)), pl.BlockSpec((1, D_MODEL), lambda i: (0, 0))],
        out_specs=pl.BlockSpec((tm, D_MODEL), lambda i: (i, 0)),
        out_shape=jax.ShapeDtypeStruct((t, D_MODEL), BF16),
    )(x, g)


def _norm_in_bwd(x, g, dh, dres, tm=256):
    t = x.shape[0]

    def body(x_ref, g_ref, dh_ref, dres_ref, dx_ref, dg_ref):
        _, vjp = jax.vjp(_rmsnorm, x_ref[...], g_ref[...])
        dx, dg = vjp(dh_ref[...])
        dx_ref[...] = dx + dres_ref[...]

        @pl.when(pl.program_id(0) == 0)
        def _():
            dg_ref[...] = jnp.zeros_like(dg_ref)

        dg_ref[...] += dg

    row = pl.BlockSpec((tm, D_MODEL), lambda i: (i, 0))
    vec = pl.BlockSpec((1, D_MODEL), lambda i: (0, 0))
    return pl.pallas_call(
        body,
        name="norm_in_bwd",
        grid=(t // tm,),
        in_specs=[row, vec, row, row],
        out_specs=[row, vec],
        out_shape=[jax.ShapeDtypeStruct((t, D_MODEL), F32), jax.ShapeDtypeStruct((1, D_MODEL), F32)],
    )(x, g, dh, dres)


def _merge(proj, yd, ys, ym, tm=256, tc=512):
    t = proj.shape[0]
    g0 = O_GATES // tc
    gstep = D_MODEL // tc

    def body(gd, gs, gm, yd_ref, ys_ref, ym_ref, o_ref):
        o_ref[...] = _merge_fn(gd[...], gs[...], gm[...], yd_ref[...], ys_ref[...], ym_ref[...]).astype(BF16)

    def gate(k):
        return pl.BlockSpec((tm, tc), lambda i, j: (i, g0 + k * gstep + j))

    blk = pl.BlockSpec((tm, tc), lambda i, j: (i, j))
    return pl.pallas_call(
        body,
        name="merge",
        grid=(t // tm, D_MODEL // tc),
        in_specs=[gate(0), gate(1), gate(2), blk, blk, blk],
        out_specs=blk,
        out_shape=jax.ShapeDtypeStruct((t, D_MODEL), BF16),
    )(proj, proj, proj, yd, ys, ym)


def _merge_bwd(proj, yd, ys, ym, dmerged, tm=256, tc=512):
    t = proj.shape[0]
    g0 = O_GATES // tc
    gstep = D_MODEL // tc

    def body(gd, gs, gm, yd_ref, ys_ref, ym_ref, dm_ref, dyd, dys, dym, dgd, dgs, dgm):
        _, vjp = jax.vjp(_merge_fn, gd[...], gs[...], gm[...], yd_ref[...], ys_ref[...], ym_ref[...])
        outs = vjp(dm_ref[...])
        for ref, val in zip((dgd, dgs, dgm, dyd, dys, dym), outs):
            ref[...] = val.astype(BF16)

    def gate(k):
        return pl.BlockSpec((tm, tc), lambda i, j: (i, g0 + k * gstep + j))

    blk = pl.BlockSpec((tm, tc), lambda i, j: (i, j))
    o = jax.ShapeDtypeStruct((t, D_MODEL), BF16)
    return pl.pallas_call(
        body,
        name="merge_bwd",
        grid=(t // tm, D_MODEL // tc),
        in_specs=[gate(0), gate(1), gate(2), blk, blk, blk, blk],
        out_specs=[blk] * 6,
        out_shape=[o] * 6,
    )(proj, proj, proj, yd, ys, ym, dmerged)


def _loss_head(x, mo, fg, tgt, tm=256):
    t = x.shape[0]

    def body(x_ref, mo_ref, fg_ref, t_ref, loss_ref, dout_ref, dfg_ref):
        loss, vjp = jax.vjp(_loss_fn, x_ref[...], mo_ref[...], fg_ref[...], t_ref[...])
        _, dmo, dfg, _ = vjp(jnp.ones((1, 1), F32))

        @pl.when(pl.program_id(0) == 0)
        def _():
            loss_ref[...] = jnp.zeros_like(loss_ref)
            dfg_ref[...] = jnp.zeros_like(dfg_ref)

        loss_ref[...] += jnp.broadcast_to(loss, loss_ref.shape)
        dfg_ref[...] += dfg
        dout_ref[...] = dmo

    row = pl.BlockSpec((tm, D_MODEL), lambda i: (i, 0))
    vec = pl.BlockSpec((1, D_MODEL), lambda i: (0, 0))
    return pl.pallas_call(
        body,
        name="loss_head",
        grid=(t // tm,),
        in_specs=[row, row, vec, row],
        out_specs=[pl.BlockSpec((1, LANE), lambda i: (0, 0)), row, vec],
        out_shape=[jax.ShapeDtypeStruct((1, LANE), F32), jax.ShapeDtypeStruct((t, D_MODEL), F32),
                   jax.ShapeDtypeStruct((1, D_MODEL), F32)],
    )(x, mo, fg, tgt)


def _shift_rows(x, s):
    t = x.shape[0]
    if s == 0:
        return x
    rolled = pltpu.roll(x, s % t, 0)
    row = _iota2(x.shape, 0)
    keep = row >= s if s > 0 else row < t + s
    return jnp.where(keep, rolled, 0.0)


def _conv_pre(x, w):
    return sum(_shift_rows(x, CONV_K - 1 - j) * w[j:j + 1, :] for j in range(CONV_K))


CONV_TC = 256


def _dn_conv(proj, conv_w):
    t = proj.shape[0]
    nb = 3 * D_MODEL // CONV_TC

    def body(x_ref, w_ref, c_ref):
        c_ref[...] = _silu(_conv_pre(x_ref[...], w_ref[...]))

    return pl.pallas_call(
        body,
        name="dn_conv",
        grid=(nb,),
        in_specs=[pl.BlockSpec((t, CONV_TC), lambda j: (0, j)), pl.BlockSpec((CONV_K, CONV_TC), lambda j: (0, j))],
        out_specs=pl.BlockSpec((t, CONV_TC), lambda j: (0, j)),
        out_shape=jax.ShapeDtypeStruct((t, 3 * D_MODEL), F32),
    )(proj, conv_w)


def _dn_conv_bwd(proj, conv_w, dc, part):
    t = proj.shape[0]
    nb = D_MODEL // CONV_TC
    b0 = part * nb

    def body(x_ref, w_ref, dc_ref, dx_ref, dw_ref):
        x = x_ref[...]
        w = w_ref[...]
        pre = _conv_pre(x, w)
        sg = _sigmoid(pre)
        dpre = dc_ref[...] * (sg * (1.0 + pre * (1.0 - sg)))
        dx = sum(_shift_rows(dpre, -(CONV_K - 1 - j)) * w[j:j + 1, :] for j in range(CONV_K))
        dx_ref[...] = dx.astype(BF16)
        dw_ref[...] = jnp.concatenate(
            [jnp.sum(dpre * _shift_rows(x, CONV_K - 1 - j), axis=0, keepdims=True) for j in range(CONV_K)], axis=0)

    blk = pl.BlockSpec((t, CONV_TC), lambda j: (0, j))
    return pl.pallas_call(
        body,
        name=f"dn_conv_bwd{part}",
        grid=(nb,),
        in_specs=[pl.BlockSpec((t, CONV_TC), lambda j: (0, b0 + j)),
                  pl.BlockSpec((CONV_K, CONV_TC), lambda j: (0, b0 + j)), blk],
        out_specs=[blk, pl.BlockSpec((CONV_K, CONV_TC), lambda j: (0, j))],
        out_shape=[jax.ShapeDtypeStruct((t, D_MODEL), BF16), jax.ShapeDtypeStruct((CONV_K, D_MODEL), F32)],
    )(proj, conv_w, dc)


def _ba_columns(ba, hd):
    lane = _iota2(ba.shape, 1)
    bcol = jnp.sum(jnp.where(lane == hd, ba, 0.0), axis=1, keepdims=True)
    acol = jnp.sum(jnp.where(lane == N_HEADS + hd, ba, 0.0), axis=1, keepdims=True)
    return bcol, acol


def _head_scalar(row, hd):
    lane = _iota2(row.shape, 1)
    return jnp.sum(jnp.where(lane == hd, row, 0.0), axis=1, keepdims=True)


DN_HP = 8


def _dn_inputs(cq, ck, cv, ba_ref, z_ref, alog_ref, dtb_ref, heads, lanes):
    ba = ba_ref[...]
    cols = [_ba_columns(ba, hd) for hd in heads]
    return ([cq[:, ln] for ln in lanes], [ck[:, ln] for ln in lanes], [cv[:, ln] for ln in lanes],
            [c[0] for c in cols], [c[1] for c in cols], [z_ref[:, ln] for ln in lanes],
            [_head_scalar(alog_ref[...], hd) for hd in heads], [_head_scalar(dtb_ref[...], hd) for hd in heads])


def _dn_specs(nblk, reverse):
    w = DN_HP * LANE
    nq = D_MODEL // w

    def row(i):
        return nblk - 1 - i if reverse else i

    def colblk(b0):
        return pl.BlockSpec((SUPER, w), lambda i, h: (row(i), b0 + h))

    ba = pl.BlockSpec((SUPER, LANE), lambda i, h: (row(i), O_BA // LANE))
    vec = pl.BlockSpec((1, LANE), lambda i, h: (0, 0))
    st = pl.BlockSpec((1, DN_HP, D_HEAD, D_HEAD), lambda i, h: (row(i), h, 0, 0))
    return colblk, nq, ba, vec, st


def _dn_fwd(c, proj, alog_row, dtb_row, gn):
    t = c.shape[0]
    nblk = t // SUPER
    colblk, nq, ba, vec, st = _dn_specs(nblk, False)

    def body(cq, ck, cv, ba_ref, z_ref, alog_ref, dtb_ref, gn_ref, o_ref, s_ref, state):
        @pl.when(jnp.logical_and(pl.program_id(0) == 0, pl.program_id(1) == 0))
        def _():
            state[...] = jnp.zeros_like(state)

        heads = [pl.program_id(1) * DN_HP + j for j in range(DN_HP)]
        lanes = [slice(j * LANE, (j + 1) * LANE) for j in range(DN_HP)]
        s0 = [state[hd] for hd in heads]
        outs, s2 = _dn_block(*_dn_inputs(cq, ck, cv, ba_ref, z_ref, alog_ref, dtb_ref, heads, lanes), gn_ref[...], s0)
        for j, (hd, ln) in enumerate(zip(heads, lanes)):
            s_ref[0, j] = s0[j]
            o_ref[:, ln] = outs[j].astype(BF16)
            state[hd] = s2[j]

    return pl.pallas_call(
        body,
        name="dn_fwd",
        grid=(nblk, N_HEADS // DN_HP),
        in_specs=[colblk(0), colblk(nq), colblk(2 * nq), ba, colblk(O_Z_DN // (DN_HP * LANE)), vec, vec, vec],
        out_specs=[colblk(0), st],
        out_shape=[jax.ShapeDtypeStruct((t, D_MODEL), BF16),
                   jax.ShapeDtypeStruct((nblk, N_HEADS, D_HEAD, D_HEAD), F32)],
        scratch_shapes=[pltpu.VMEM((N_HEADS, D_HEAD, D_HEAD), F32)],
    )(c, c, c, proj, proj, alog_row, dtb_row, gn)


def _dn_bwd(c, proj, alog_row, dtb_row, gn, states, do):
    t = c.shape[0]
    nblk = t // SUPER
    colblk, nq, ba, vec, st = _dn_specs(nblk, True)

    def body(cq, ck, cv, ba_ref, z_ref, alog_ref, dtb_ref, gn_ref, s_ref, do_ref,
             dq_ref, dk_ref, dv_ref, dz_ref, dba_ref, dsc_ref, dgn_ref, dstate):
        i = pl.program_id(0)
        hq = pl.program_id(1)

        @pl.when(jnp.logical_and(i == 0, hq == 0))
        def _():
            dstate[...] = jnp.zeros_like(dstate)
            dsc_ref[...] = jnp.zeros_like(dsc_ref)
            dgn_ref[...] = jnp.zeros_like(dgn_ref)

        @pl.when(hq == 0)
        def _():
            dba_ref[...] = jnp.zeros_like(dba_ref)

        lane = _iota2((SUPER, LANE), 1)
        lane1 = _iota2((1, LANE), 1)
        heads = [hq * DN_HP + j for j in range(DN_HP)]
        lanes = [slice(j * LANE, (j + 1) * LANE) for j in range(DN_HP)]
        ds_in = [dstate[hd] for hd in heads]
        s_in = [s_ref[0, j] for j in range(DN_HP)]
        _, vjp = jax.vjp(_dn_block, *_dn_inputs(cq, ck, cv, ba_ref, z_ref, alog_ref, dtb_ref, heads, lanes),
                         gn_ref[...], s_in)
        dq, dk, dv, dbc, dac, dz, dal, ddt, dgn, ds0 = vjp(([do_ref[:, ln].astype(F32) for ln in lanes], ds_in))
        dba = jnp.zeros((SUPER, LANE), F32)
        dal_row = jnp.zeros((1, LANE), F32)
        ddt_row = jnp.zeros((1, LANE), F32)
        for j, (hd, ln) in enumerate(zip(heads, lanes)):
            dq_ref[:, ln] = dq[j]
            dk_ref[:, ln] = dk[j]
            dv_ref[:, ln] = dv[j]
            dz_ref[:, ln] = dz[j].astype(BF16)
            dstate[hd] = ds0[j]
            dba = dba + jnp.where(lane == hd, dbc[j], 0.0) + jnp.where(lane == N_HEADS + hd, dac[j], 0.0)
            dal_row = dal_row + jnp.where(lane1 == hd, dal[j], 0.0)
            ddt_row = ddt_row + jnp.where(lane1 == hd, ddt[j], 0.0)
        dba_ref[...] += dba
        dsc_ref[0:1, :] += dal_row
        dsc_ref[1:2, :] += ddt_row
        dgn_ref[...] += dgn

    outs = pl.pallas_call(
        body,
        name="dn_bwd",
        grid=(nblk, N_HEADS // DN_HP),
        in_specs=[colblk(0), colblk(nq), colblk(2 * nq), ba, colblk(O_Z_DN // (DN_HP * LANE)), vec, vec, vec, st,
                  colblk(0)],
        out_specs=[colblk(0), colblk(0), colblk(0), colblk(0),
                   pl.BlockSpec((SUPER, LANE), lambda i, h: (nblk - 1 - i, 0)),
                   pl.BlockSpec((2, LANE), lambda i, h: (0, 0)), vec],
        out_shape=[jax.ShapeDtypeStruct((t, D_MODEL), F32)] * 3
        + [jax.ShapeDtypeStruct((t, D_MODEL), BF16), jax.ShapeDtypeStruct((t, LANE), F32),
           jax.ShapeDtypeStruct((2, LANE), F32), jax.ShapeDtypeStruct((1, LANE), F32)],
        scratch_shapes=[pltpu.VMEM((N_HEADS, D_HEAD, D_HEAD), F32)],
    )(c, c, c, proj, proj, alog_row, dtb_row, gn, states, do)
    return outs


SB_TQ = 256
SB_TK = 256
SB_HP = 4


def _sb_logits(z, mask):
    sp = jnp.log(1.0 + jnp.exp(-jnp.abs(z)))
    lf_raw = -(jnp.maximum(z, 0.0) + sp)
    lb = lf_raw + z
    lf = lf_raw if mask is None else jnp.where(mask, lf_raw, 0.0)
    return lb, lf_raw, lf


def _suffix_sums(x, sel):
    hi, lo = _split2(x)
    d = functools.partial(lax.dot_general, dimension_numbers=NN, preferred_element_type=F32)
    return d(hi, sel) + d(lo, sel)


def _sb_diag_mask(tq, r):
    return r * SB_TK + _iota2((tq, SB_TK), 1) < _iota2((tq, SB_TK), 0)


def _sb_specs(t, tq):
    w = SB_HP * LANE
    q0, k0, v0, z0 = (O_QKV_SB // w, (O_QKV_SB + D_MODEL) // w, (O_QKV_SB + 2 * D_MODEL) // w, O_Z_SB // w)

    def blk(b0):
        return pl.BlockSpec((tq, w), lambda h, i: (i, b0 + h))

    def full(b0, **kw):
        return pl.BlockSpec((t, w), lambda h, i: (0, b0 + h), **kw)

    once = dict(pipeline_mode=pl.Buffered(1))
    return blk(q0), full(k0, **once), full(v0, **once), blk(z0), blk(0), full(0)


def _sb_fwd(proj):
    t = proj.shape[0]
    tq = min(SB_TQ, t)
    ndiag = tq // SB_TK
    scale = 1.0 / math.sqrt(D_HEAD)

    def body(q_ref, k_ref, v_ref, z_ref, o_ref, oraw_ref):
        qi = pl.program_id(1)
        lanes = [slice(hd * LANE, (hd + 1) * LANE) for hd in range(SB_HP)]
        qs = [(q_ref[:, ln] * scale).astype(BF16) for ln in lanes]
        after = (_iota2((SB_TK, SB_TK), 0) > _iota2((SB_TK, SB_TK), 1)).astype(BF16)
        oraw_ref[...] = jnp.zeros_like(oraw_ref)

        def block(kb, mask, c_lf):
            rows = pl.ds(pl.multiple_of(kb * SB_TK, SB_TK), SB_TK)
            z = _each(lambda q, ln: _dot(q, k_ref[rows, ln], NT), qs, lanes)
            lg = _each(lambda x: _sb_logits(x, mask), z)
            surv = _each(lambda x: _suffix_sums(x[2], after), lg)
            att = _each(lambda x, s, c: jnp.exp(x[0] + s + c), lg, surv, c_lf)
            if mask is not None:
                att = _each(lambda a: jnp.where(mask, a, 0.0), att)
            pv = _each(lambda a, ln: _dot(a, v_ref[rows, ln], NN), att, lanes)
            for p, ln in zip(pv, lanes):
                oraw_ref[:, ln] += p
            return tuple(_each(lambda c, x: c + jnp.sum(x[2], axis=1, keepdims=True), c_lf, lg))

        carry = tuple(jnp.zeros((tq, 1), F32) for _ in range(SB_HP))
        for r in reversed(range(ndiag)):
            carry = block(qi * ndiag + r, _sb_diag_mask(tq, r), carry)
        lax.fori_loop(0, qi * ndiag, lambda i, c: block(qi * ndiag - 1 - i, None, c), carry)
        o_ref[...] = (oraw_ref[...] * _silu(z_ref[...])).astype(BF16)

    q_spec, k_spec, v_spec, z_spec, out, _ = _sb_specs(t, tq)
    return pl.pallas_call(
        body,
        name="sb_fwd",
        grid=(N_HEADS // SB_HP, t // tq),
        in_specs=[q_spec, k_spec, v_spec, z_spec],
        out_specs=[out, out],
        out_shape=[jax.ShapeDtypeStruct((t, D_MODEL), BF16), jax.ShapeDtypeStruct((t, D_MODEL), F32)],
    )(proj, proj, proj, proj)


def _sb_bwd(proj, oraw, do):
    t = proj.shape[0]
    tq = min(SB_TQ, t)
    ndiag = tq // SB_TK
    scale = 1.0 / math.sqrt(D_HEAD)

    def body(q_ref, k_ref, v_ref, z_ref, oraw_ref, do_ref, dq_ref, dk_ref, dv_ref, dz_ref, dk_acc, dv_acc,
             p_scr, z_scr):
        qi = pl.program_id(1)
        nq = pl.num_programs(1)

        @pl.when(qi == 0)
        def _():
            dk_acc[...] = jnp.zeros_like(dk_acc)
            dv_acc[...] = jnp.zeros_like(dv_acc)

        heads = range(SB_HP)
        lanes = [slice(hd * LANE, (hd + 1) * LANE) for hd in heads]
        zg = z_ref[...]
        sg = _sigmoid(zg)
        dog = do_ref[...].astype(F32)
        dz_ref[...] = (dog * oraw_ref[...] * (sg * (1.0 + zg * (1.0 - sg)))).astype(BF16)
        d_o = (dog * (zg * sg)).astype(BF16)
        d_o16 = [d_o[:, ln] for ln in lanes]
        qs = [(q_ref[:, ln] * scale).astype(BF16) for ln in lanes]
        ri = _iota2((SB_TK, SB_TK), 0)
        ci = _iota2((SB_TK, SB_TK), 1)
        after = (ri > ci).astype(BF16)
        earlier = (ri < ci).astype(BF16)

        def rows_of(kb):
            return pl.ds(pl.multiple_of(kb * SB_TK, SB_TK), SB_TK)

        def down(kb, mask, c_lf):
            rows = rows_of(kb)
            z = _each(lambda q, ln: _dot(q, k_ref[rows, ln], NT), qs, lanes)
            da = _each(lambda d, ln: _dot(d, v_ref[rows, ln], NT), d_o16, lanes)
            lg = _each(lambda x: _sb_logits(x, mask), z)
            surv = _each(lambda x: _suffix_sums(x[2], after), lg)
            att = _each(lambda x, s, c: jnp.exp(x[0] + s + c), lg, surv, c_lf)
            if mask is not None:
                att = _each(lambda a: jnp.where(mask, a, 0.0), att)
            dv = _each(lambda a, d: _dot(a, d, TN), att, d_o16)
            for hd in heads:
                p_scr[hd, kb] = att[hd] * da[hd]
                z_scr[hd, kb] = z[hd]
                dv_acc[rows, lanes[hd]] += dv[hd]
            return tuple(_each(lambda c, x: c + jnp.sum(x[2], axis=1, keepdims=True), c_lf, lg))

        c_lf = tuple(jnp.zeros((tq, 1), F32) for _ in heads)
        for r in reversed(range(ndiag)):
            c_lf = down(qi * ndiag + r, _sb_diag_mask(tq, r), c_lf)
        lax.fori_loop(0, qi * ndiag, lambda i, c: down(qi * ndiag - 1 - i, None, c), c_lf)

        def up(kb, mask, carry):
            dq, c_p = carry
            rows = rows_of(kb)
            p = [p_scr[hd, kb] for hd in heads]
            zs = [z_scr[hd, kb] for hd in heads]
            before = _each(lambda x, c: _suffix_sums(x, earlier) + c, p, c_p)
            e = _each(lambda x: jnp.exp(-jnp.abs(x)), zs)
            r = _each(lambda x: 1.0 / (1.0 + x), e)
            sig = _each(lambda x, a, b: jnp.where(x >= 0.0, b, a * b), zs, e, r)
            oms = _each(lambda x, a, b: jnp.where(x >= 0.0, a * b, b), zs, e, r)
            if mask is not None:
                sig = _each(lambda a: jnp.where(mask, a, 0.0), sig)
            dzz = _each(lambda x, o, g, b: x * o - g * b, p, oms, sig, before)
            dk = _each(lambda x, q: _dot(x, q, TN), dzz, qs)
            dq = _each(lambda a, x, ln: a + _dot(x, k_ref[rows, ln], NN), dq, dzz, lanes)
            for hd in heads:
                dk_acc[rows, lanes[hd]] += dk[hd]
            return tuple(dq), tuple(_each(lambda c, x: c + jnp.sum(x, axis=1, keepdims=True), c_p, p))

        carry = (tuple(jnp.zeros((tq, D_HEAD), F32) for _ in heads), tuple(jnp.zeros((tq, 1), F32) for _ in heads))
        carry = lax.fori_loop(0, qi * ndiag, lambda kb, c: up(kb, None, c), carry)
        for r in range(ndiag):
            carry = up(qi * ndiag + r, _sb_diag_mask(tq, r), carry)
        dq = carry[0]
        for hd in heads:
            dq_ref[:, lanes[hd]] = (dq[hd] * scale).astype(BF16)

        @pl.when(qi == nq - 1)
        def _():
            dk_ref[...] = dk_acc[...].astype(BF16)
            dv_ref[...] = dv_acc[...].astype(BF16)

    q_spec, k_spec, v_spec, z_spec, blk, full = _sb_specs(t, tq)
    o = jax.ShapeDtypeStruct((t, D_MODEL), BF16)
    w = SB_HP * LANE
    return pl.pallas_call(
        body,
        name="sb_bwd",
        grid=(N_HEADS // SB_HP, t // tq),
        in_specs=[q_spec, k_spec, v_spec, z_spec, blk, blk],
        out_specs=[blk, full, full, blk],
        out_shape=[o, o, o, o],
        scratch_shapes=[pltpu.VMEM((t, w), F32), pltpu.VMEM((t, w), F32)]
        + [pltpu.VMEM((SB_HP, t // SB_TK, tq, SB_TK), F32)] * 2,
    )(proj, proj, proj, proj, oraw, do)


def _mem_kv_fn(mem, mg, w):
    return mm_nn(_rmsnorm(mem, mg), w)


def _mem_kv(mem, mg, w):
    def body(m_ref, g_ref, w_ref, o_ref):
        o_ref[...] = _mem_kv_fn(m_ref[...], g_ref[...], w_ref[...])

    return pl.pallas_call(body, name="mem_kv", out_shape=jax.ShapeDtypeStruct((MEM_LEN, 2 * MEM_W), F32))(mem, mg, w)


def _mem_kv_bwd(mem, mg, w, dmkv):
    def body(m_ref, g_ref, w_ref, d_ref, dg_ref, dw_ref):
        _, vjp = jax.vjp(_mem_kv_fn, m_ref[...], g_ref[...], w_ref[...].astype(F32))
        _, dg, dw = vjp(d_ref[...])
        dg_ref[...] = dg
        dw_ref[...] = dw.astype(BF16)

    return pl.pallas_call(
        body, name="mem_kv_bwd",
        out_shape=[jax.ShapeDtypeStruct((1, D_MODEL), F32), jax.ShapeDtypeStruct((D_MODEL, 2 * MEM_W), BF16)],
    )(mem, mg, w, dmkv)


def _mem_attn(proj, mkv, tm=256):
    t = proj.shape[0]
    tm = min(tm, t)

    def body(q_ref, z_ref, kv_ref, o_ref):
        o_ref[...] = _mem_fn(q_ref[...], z_ref[...], kv_ref[...]).astype(BF16)

    return pl.pallas_call(
        body,
        name="mem_attn",
        grid=(t // tm,),
        in_specs=[pl.BlockSpec((tm, MEM_W), lambda i: (i, O_MQ // MEM_W)),
                  pl.BlockSpec((tm, MEM_W), lambda i: (i, O_MZ // MEM_W)),
                  pl.BlockSpec((MEM_LEN, 2 * MEM_W), lambda i: (0, 0))],
        out_specs=pl.BlockSpec((tm, MEM_W), lambda i: (i, 0)),
        out_shape=jax.ShapeDtypeStruct((t, MEM_W), BF16),
    )(proj, proj, mkv)


def _mem_attn_bwd(proj, mkv, do, tm=256):
    t = proj.shape[0]
    tm = min(tm, t)

    def body(q_ref, z_ref, kv_ref, do_ref, dq_ref, dz_ref, dkv_ref):
        _, vjp = jax.vjp(_mem_fn, q_ref[...], z_ref[...], kv_ref[...])
        dq, dz, dkv = vjp(do_ref[...].astype(F32))
        dq_ref[...] = dq.astype(BF16)
        dz_ref[...] = dz.astype(BF16)

        @pl.when(pl.program_id(0) == 0)
        def _():
            dkv_ref[...] = jnp.zeros_like(dkv_ref)

        dkv_ref[...] += dkv

    blk = pl.BlockSpec((tm, MEM_W), lambda i: (i, 0))
    kv = pl.BlockSpec((MEM_LEN, 2 * MEM_W), lambda i: (0, 0))
    return pl.pallas_call(
        body,
        name="mem_attn_bwd",
        grid=(t // tm,),
        in_specs=[pl.BlockSpec((tm, MEM_W), lambda i: (i, O_MQ // MEM_W)),
                  pl.BlockSpec((tm, MEM_W), lambda i: (i, O_MZ // MEM_W)), kv, blk],
        out_specs=[blk, blk, kv],
        out_shape=[jax.ShapeDtypeStruct((t, MEM_W), BF16), jax.ShapeDtypeStruct((t, MEM_W), BF16),
                   jax.ShapeDtypeStruct((MEM_LEN, 2 * MEM_W), F32)],
    )(proj, proj, mkv, do)


def _local_step(x, mem, tgt, norm_g, mem_norm_g, w_alt, conv_w, alog_row, dtb_row, dn_norm_g, w_mem_kv, w_br_dn, w_br_sb,
                w_br_mem, w_out, final_g):
    h = _norm_in(x, norm_g)
    proj = _matmul(h, w_alt, "nt", F32, 2048, 384, 1024, "proj")

    c = _dn_conv(proj, conv_w)
    o_dn, states = _dn_fwd(c, proj, alog_row, dtb_row, dn_norm_g)
    o_sb, o_sb_raw = _sb_fwd(proj)
    mkv = _mem_kv(mem, mem_norm_g, w_mem_kv)
    o_m = _mem_attn(proj, mkv)

    y_dn = _matmul(o_dn, w_br_dn, "nn", F32, 512, 1024, 1024, "y_dn")
    y_sb = _matmul(o_sb, w_br_sb, "nn", F32, 512, 1024, 1024, "y_sb")
    y_m = _matmul(o_m, w_br_mem, "nn", F32, 512, 1024, 1024, "y_m")
    merged = _merge(proj, y_dn, y_sb, y_m)
    mo = _matmul(merged, w_out, "nn", F32, 512, 1024, 1024, "mo")
    loss, dout, d_final_g = _loss_head(x, mo, final_g, tgt)

    dmerged = _matmul(dout, w_out, "nt", F32, 512, 1024, 1024, "dmerged")
    dw_out = _matmul(merged, dout, "tn", BF16, 256, 1024, 2048, "dw_out")
    dy_dn, dy_sb, dy_m, dg_dn, dg_sb, dg_m = _merge_bwd(proj, y_dn, y_sb, y_m, dmerged)
    do_dn = _matmul(dy_dn, w_br_dn, "nt", BF16, 512, 1024, 1024, "do_dn")
    do_sb = _matmul(dy_sb, w_br_sb, "nt", BF16, 512, 1024, 1024, "do_sb")
    do_m = _matmul(dy_m, w_br_mem, "nt", BF16, 512, 256, 1024, "do_m")
    dw_br_dn = _matmul(o_dn, dy_dn, "tn", BF16, 256, 1024, 2048, "dw_br_dn")
    dw_br_sb = _matmul(o_sb, dy_sb, "tn", BF16, 256, 1024, 2048, "dw_br_sb")
    dw_br_mem = _matmul(o_m, dy_m, "tn", BF16, 256, 1024, 2048, "dw_br_mem")

    dmq, dmz, dmkv = _mem_attn_bwd(proj, mkv, do_m)
    d_mem_norm_g, dw_mem_kv = _mem_kv_bwd(mem, mem_norm_g, w_mem_kv, dmkv)
    dq_sb, dk_sb, dv_sb, dz_sb = _sb_bwd(proj, o_sb_raw, do_sb)
    dcq, dck, dcv, dz_dn, dba, dscal, d_dn_norm_g = _dn_bwd(c, proj, alog_row, dtb_row, dn_norm_g, states, do_dn)
    dq_dn, dcw_q = _dn_conv_bwd(proj, conv_w, dcq, 0)
    dk_dn, dcw_k = _dn_conv_bwd(proj, conv_w, dck, 1)
    dv_dn, dcw_v = _dn_conv_bwd(proj, conv_w, dcv, 2)
    d_conv_w = jnp.concatenate([dcw_q, dcw_k, dcw_v], axis=1)

    dproj = jnp.concatenate([dq_dn, dk_dn, dv_dn, dz_dn, dq_sb, dk_sb, dv_sb, dz_sb, dmq, dmz, dg_dn, dg_sb, dg_m,
                             dba.astype(BF16)], axis=1)
    dh = _matmul(dproj, w_alt, "nn", F32, 512, 1024, 3968, "dh")
    dw_alt = _matmul(dproj, h, "tn", BF16, 384, 1024, 2048, "dw_alt")
    grad_x, d_norm_g = _norm_in_bwd(x, norm_g, dh, dout)
    return dict(loss=loss, grad_x=grad_x, norm_g=d_norm_g, mem_norm_g=d_mem_norm_g, w_alt=dw_alt, conv_w=d_conv_w,
                scal=dscal, dn_norm_g=d_dn_norm_g, w_mem_kv=dw_mem_kv, w_br_dn=dw_br_dn, w_br_sb=dw_br_sb,
                w_br_mem=dw_br_mem, w_out=dw_out, final_g=d_final_g)


MESH = pl.DeviceIdType.MESH
ANY = pl.BlockSpec(memory_space=pl.ANY)


def _position():
    return lax.axis_index("x"), lax.axis_index("y"), lax.axis_index("c")


def _all_gather(xs, name):
    n = len(xs)

    def body(*refs):
        x_refs, o_refs = refs[:n], refs[n:2 * n]
        send_sems, recv_sems, local_sems = refs[2 * n:]
        x, y, c = _position()
        me, sibling = (x, y, c), (x, y, 1 - c)
        x_nbr, y_nbr, diag = (1 - x, y, c), (x, 1 - y, c), (1 - x, 1 - y, c)
        south = c == 0
        relay_from = tuple(jnp.where(south, a, b) for a, b in zip(y_nbr, x_nbr))
        relay_to = tuple(jnp.where(south, a, b) for a, b in zip(x_nbr, y_nbr))

        def slot(p):
            return 4 * p[0] + 2 * p[1] + p[2]

        def copy(a, k, block, to, src=None):
            dst = o_refs[a].at[slot(block)]
            return pltpu.make_async_remote_copy(
                src_ref=dst if src is None else src, dst_ref=dst, send_sem=send_sems.at[7 * a + k],
                recv_sem=recv_sems.at[7 * a + k], device_id=to, device_id_type=MESH)

        mine = [pltpu.make_async_copy(x_refs[a], o_refs[a].at[slot(me)], local_sems.at[a]) for a in range(n)]
        for cp in mine:
            cp.start()
        sends = []
        for a in range(n):
            sends += [copy(a, 0, me, sibling, src=x_refs[a]), copy(a, 1, me, x_nbr, src=x_refs[a]),
                      copy(a, 2, me, y_nbr, src=x_refs[a])]
        for cp in sends:
            cp.start()
        later = []
        for a in range(n):
            copy(a, 1, x_nbr, me).wait_recv()
            copy(a, 2, y_nbr, me).wait_recv()
            later += [copy(a, 3, relay_from, relay_to), copy(a, 4, x_nbr, sibling), copy(a, 5, y_nbr, sibling)]
            for cp in later[-3:]:
                cp.start()
        for a in range(n):
            copy(a, 3, diag, me).wait_recv()
            later.append(copy(a, 6, diag, sibling))
            later[-1].start()
        for a in range(n):
            copy(a, 0, sibling, me).wait_recv()
            for k, chip in ((4, x_nbr), (5, y_nbr), (6, diag)):
                copy(a, k, (chip[0], chip[1], 1 - c), me).wait_recv()
        for cp in sends + later:
            cp.wait_send()
        for cp in mine:
            cp.wait()

    return pl.pallas_call(
        body,
        name=name,
        in_specs=[ANY] * n,
        out_specs=[ANY] * n,
        out_shape=[jax.ShapeDtypeStruct((N_DEV, *v.shape), v.dtype) for v in xs],
        scratch_shapes=[pltpu.SemaphoreType.DMA((7 * n,)), pltpu.SemaphoreType.DMA((7 * n,)),
                        pltpu.SemaphoreType.DMA((n,))],
    )(*xs)


def _window_view(ref, dest):
    return ref.at[pl.ds(LANE * WIN_START[dest], WIN_W), :]


def _chunk_rows(rows, cols):
    best = max(ch for ch in range(16, rows + 1, 16) if rows % ch == 0 and ch * cols <= (1 << 17))
    return best


def _halving_stage(xs, axis, name, out_dtype, windowed=()):
    n_arr = len(xs)
    metas = []
    for k, v in enumerate(xs):
        if k in windowed:
            metas.append((N_DEV // 2, WIN_W, v.shape[1]))
        else:
            assert v.shape[1] == 2
            metas.append((v.shape[0], v.shape[2], v.shape[3]))
    chunk = [_chunk_rows(r, c) for (_, r, c) in metas]
    offs = [sum(m[0] for m in metas[:k]) for k in range(n_arr)]
    n_sem = sum(m[0] for m in metas)

    def body(*refs):
        x_refs = refs[:n_arr]
        o_refs = refs[n_arr:2 * n_arr]
        land_refs = refs[2 * n_arr:3 * n_arr]
        rest = refs[3 * n_arr:]
        bufs = rest[:3 * n_arr]
        send_sems, recv_sems, in_sems, out_sems = rest[3 * n_arr:]
        pos = dict(zip("xyc", _position()))
        bit = pos[axis]
        peer = tuple(1 - pos[a] if a == axis else pos[a] for a in "xyc")

        def view(k, i, b):
            if k in windowed:
                return _window_view(x_refs[k], 2 * i + b)
            return x_refs[k].at[i, b]

        def add_blocks(k, a_view, b_view, o_view):
            _hbm_add(a_view, b_view, o_view, bufs[3 * k:3 * k + 3], in_sems, out_sems, chunk[k])

        for b in (0, 1):
            @pl.when(bit == b)
            def _(b=b):
                sends = []
                for k in range(n_arr):
                    for i in range(metas[k][0]):
                        cp = pltpu.make_async_remote_copy(
                            src_ref=view(k, i, 1 - b), dst_ref=land_refs[k].at[i], send_sem=send_sems.at[offs[k] + i],
                            recv_sem=recv_sems.at[offs[k] + i], device_id=peer, device_id_type=MESH)
                        cp.start()
                        sends.append(cp)
                idx = 0
                for k in range(n_arr):
                    for i in range(metas[k][0]):
                        sends[idx].wait_recv()
                        add_blocks(k, view(k, i, b), land_refs[k].at[i], o_refs[k].at[i])
                        idx += 1
                for cp in sends:
                    cp.wait_send()

    out_shape = [jax.ShapeDtypeStruct(m, out_dtype) for m in metas]
    land_shape = [jax.ShapeDtypeStruct(m, v.dtype) for m, v in zip(metas, xs)]
    scratch = []
    for k in range(n_arr):
        blk = (2, chunk[k], metas[k][2])
        scratch += [pltpu.VMEM(blk, xs[k].dtype)] * 2 + [pltpu.VMEM(blk, out_dtype)]
    scratch += [pltpu.SemaphoreType.DMA((n_sem,)), pltpu.SemaphoreType.DMA((n_sem,)),
                pltpu.SemaphoreType.DMA((2, 2)), pltpu.SemaphoreType.DMA((2,))]
    outs = pl.pallas_call(
        body,
        name=name,
        in_specs=[ANY] * n_arr,
        out_specs=[ANY] * (2 * n_arr),
        out_shape=out_shape + land_shape,
        scratch_shapes=scratch,
    )(*xs)
    return outs[:n_arr]


def _hbm_add(a_view, b_view, o_view, bufs, in_sems, out_sems, ch):
    rows = a_view.shape[0]
    nch = rows // ch
    va, vb, vo = bufs

    def rows_of(j):
        return pl.ds(pl.multiple_of(j * ch, 16), ch)

    def loads(j, s):
        return (pltpu.make_async_copy(a_view.at[rows_of(j), :], va.at[s], in_sems.at[0, s]),
                pltpu.make_async_copy(b_view.at[rows_of(j), :], vb.at[s], in_sems.at[1, s]))

    def store(j, s):
        return pltpu.make_async_copy(vo.at[s], o_view.at[rows_of(j), :], out_sems.at[s])

    for cp in loads(0, 0):
        cp.start()

    def step(j, _):
        s = lax.rem(j, 2)

        @pl.when(j + 1 < nch)
        def _():
            for cp in loads(j + 1, 1 - s):
                cp.start()

        for cp in loads(j, s):
            cp.wait()

        @pl.when(j >= 2)
        def _():
            store(j - 2, s).wait()

        vo[s] = (va[s].astype(F32) + vb[s].astype(F32)).astype(vo.dtype)
        store(j, s).start()
        return 0

    lax.fori_loop(0, nch, step, 0)
    for j in range(max(0, nch - 2), nch):
        store(j, j % 2).wait()


def _xy_stage(xs, first, name):
    n_arr = len(xs)
    if first:
        shapes = [(v.shape[2] // 2, v.shape[3]) for v in xs]
        ins = list(xs)
    else:
        shapes = [(a.shape[1], a.shape[2]) for a, _ in xs]
        ins = [v for pair in xs for v in pair]
    n_blk = 2 if first else 1
    out_dtype = BF16 if first else F32
    chunk = [_chunk_rows(r, c) for (r, c) in shapes]
    n_sem = 2 * n_blk * n_arr

    def body(*refs):
        n_in = len(ins)
        in_refs = refs[:n_in]
        n_out = 2 * n_arr if first else n_arr
        o_refs = refs[n_in:n_in + n_out]
        land = refs[n_in + n_out:n_in + n_out + 2 * n_arr]
        rest = refs[n_in + n_out + 2 * n_arr:]
        bufs = rest[:3 * n_arr]
        send_sems, recv_sems, in_sems, out_sems = rest[3 * n_arr:]
        x, y, c = _position()
        peers = {"x": (1 - x, y, c), "y": (x, 1 - y, c)}
        jobs = []
        for k in range(n_arr):
            r, _ = shapes[k]
            half_a, half_b = pl.ds(0, r), pl.ds(r, r)
            if first:
                src = in_refs[k]
                for i in range(2):
                    jobs.append((k, src.at[i, 1 - y, half_a, :], src.at[i, y, half_a, :], land[2 * k].at[i],
                                 o_refs[2 * k].at[i], "y"))
                    jobs.append((k, src.at[1 - x, i, half_b, :], src.at[x, i, half_b, :], land[2 * k + 1].at[i],
                                 o_refs[2 * k + 1].at[i], "x"))
            else:
                a1, b1 = in_refs[2 * k], in_refs[2 * k + 1]
                jobs.append((k, a1.at[1 - x], a1.at[x], land[2 * k], o_refs[k].at[half_a, :], "x"))
                jobs.append((k, b1.at[1 - y], b1.at[y], land[2 * k + 1], o_refs[k].at[half_b, :], "y"))
        sends = []
        for n, (k, send, _, landing, _, axis) in enumerate(jobs):
            cp = pltpu.make_async_remote_copy(src_ref=send, dst_ref=landing, send_sem=send_sems.at[n],
                                              recv_sem=recv_sems.at[n], device_id=peers[axis], device_id_type=MESH)
            cp.start()
            sends.append(cp)
        for cp, (k, _, kept, landing, out, _) in zip(sends, jobs):
            cp.wait_recv()
            _hbm_add(kept, landing, out, bufs[3 * k:3 * k + 3], in_sems, out_sems, chunk[k])
        for cp in sends:
            cp.wait_send()

    if first:
        out_shape = [jax.ShapeDtypeStruct((2, r, c), BF16) for (r, c) in shapes for _ in range(2)]
        land_shape = out_shape
    else:
        out_shape = [jax.ShapeDtypeStruct((2 * r, c), F32) for (r, c) in shapes]
        land_shape = [jax.ShapeDtypeStruct((r, c), BF16) for (r, c) in shapes for _ in range(2)]
    scratch = []
    for k in range(n_arr):
        scratch += [pltpu.VMEM((2, chunk[k], shapes[k][1]), BF16)] * 2 + [pltpu.VMEM((2, chunk[k], shapes[k][1]), out_dtype)]
    scratch += [pltpu.SemaphoreType.DMA((n_sem,)), pltpu.SemaphoreType.DMA((n_sem,)),
                pltpu.SemaphoreType.DMA((2, 2)), pltpu.SemaphoreType.DMA((2,))]
    outs = pl.pallas_call(
        body,
        name=name,
        in_specs=[ANY] * len(ins),
        out_specs=[ANY] * (len(out_shape) + len(land_shape)),
        out_shape=out_shape + land_shape,
        scratch_shapes=scratch,
    )(*ins)
    outs = outs[:len(out_shape)]
    return [(outs[2 * k], outs[2 * k + 1]) for k in range(n_arr)] if first else list(outs)


def _reduce_scatter(dw_al, blocks):
    xs = [dw_al] + [b.reshape(N_DEV // 2, 2, *b.shape[1:]) for b in blocks]
    ys = _halving_stage(xs, "c", "rs_c", BF16, windowed=(0,))
    pairs = _xy_stage([v.reshape(2, 2, *v.shape[1:]) for v in ys], True, "rs_xy1")
    return _xy_stage(pairs, False, "rs_xy2")


def _sum_slots(gs):
    n = len(gs)

    def body(*refs):
        for g_ref, o_ref in zip(refs[:n], refs[n:]):
            acc = g_ref[0]
            for d in range(1, N_DEV):
                acc = acc + g_ref[d]
            o_ref[...] = acc

    return pl.pallas_call(body, name="sum_slots",
                          out_shape=[jax.ShapeDtypeStruct(g.shape[1:], g.dtype) for g in gs])(*gs)


def _assemble_w_al(wins, bas):
    ba_tile = O_BA // LANE
    assert W_AL // LANE == ba_tile + 1
    cols = wins.shape[2]
    n_buf = 3
    ends = [WIN_START[d + 1] if d + 1 < N_DEV else ba_tile + 1 for d in range(N_DEV)]
    assert WIN_START[N_DEV - 1] + WIN_TILES == ba_tile + 1

    def body(w_ref, ba_ref, o_ref, buf, ld_sems, st_sems, ba_sem):
        def load(d):
            return pltpu.make_async_copy(w_ref.at[d], buf.at[d % n_buf], ld_sems.at[d % n_buf])

        def store(d):
            n = LANE * (ends[d] - WIN_START[d])
            return pltpu.make_async_copy(buf.at[d % n_buf, pl.ds(0, n), :],
                                         o_ref.at[pl.ds(LANE * WIN_START[d], n), :], st_sems.at[d % n_buf])

        load(0).start()
        for d in range(N_DEV):
            if d + 1 < N_DEV:
                if d + 1 >= n_buf:
                    store(d + 1 - n_buf).wait()
                load(d + 1).start()
            load(d).wait()
            if d > 0:
                ov = LANE * (WIN_START[d - 1] + WIN_TILES - WIN_START[d])
                buf[d % n_buf, :ov, :] = buf[d % n_buf, :ov, :] + buf[(d - 1) % n_buf, WIN_W - ov:, :]
            if d == N_DEV - 1:
                ba_copy = pltpu.make_async_copy(
                    ba_ref.at[BA_DEV], buf.at[d % n_buf, pl.ds(WIN_W - LANE, ba_ref.shape[1]), :], ba_sem)
                ba_copy.start()
                ba_copy.wait()
            store(d).start()
        for d in range(N_DEV - n_buf, N_DEV):
            store(d).wait()

    return pl.pallas_call(
        body,
        name="assemble_w_al",
        in_specs=[ANY, ANY],
        out_specs=ANY,
        out_shape=jax.ShapeDtypeStruct((W_AL, cols), wins.dtype),
        scratch_shapes=[pltpu.VMEM((n_buf, WIN_W, cols), wins.dtype), pltpu.SemaphoreType.DMA((n_buf,)),
                        pltpu.SemaphoreType.DMA((n_buf,)), pltpu.SemaphoreType.DMA],
    )(wins, bas)


def _adamw_math(w, g, m, v):
    m_new = ADAM_B1 * m + (1.0 - ADAM_B1) * g
    v_new = ADAM_B2 * v + (1.0 - ADAM_B2) * (g * g)
    m_hat = m_new / (1.0 - ADAM_B1 ** ADAM_STEP)
    v_hat = v_new / (1.0 - ADAM_B2 ** ADAM_STEP)
    return -ADAM_LR * (m_hat / (jnp.sqrt(v_hat) + ADAM_EPS) + ADAM_WD * w), m_new, v_new


def _adamw(w, g, m, v, name, tb=134):
    r, _, c = w.shape
    assert r % tb == 0

    def body(w_ref, g_ref, m_ref, v_ref, d_ref, nm_ref, nv_ref):
        d_ref[...], nm_ref[...], nv_ref[...] = _adamw_math(w_ref[...], g_ref[...], m_ref[...], v_ref[...])

    blk = pl.BlockSpec((tb, 1, c), lambda i: (i, 0, 0))
    o = jax.ShapeDtypeStruct(w.shape, F32)
    return pl.pallas_call(body, name=name, grid=(r // tb,), in_specs=[blk] * 4, out_specs=[blk] * 3,
                          out_shape=[o, o, o])(w, g, m, v)


def _adamw_many(ws, gs, ms, vs, name):
    n = len(ws)

    def body(*refs):
        for k in range(n):
            w_ref, g_ref, m_ref, v_ref = (refs[j * n + k] for j in range(4))
            d_ref, nm_ref, nv_ref = (refs[(4 + j) * n + k] for j in range(3))
            d_ref[...], nm_ref[...], nv_ref[...] = _adamw_math(w_ref[...], g_ref[...], m_ref[...], v_ref[...])

    shapes = [jax.ShapeDtypeStruct(w.shape, F32) for w in ws]
    outs = pl.pallas_call(body, name=name, out_shape=shapes * 3)(*ws, *gs, *ms, *vs)
    return outs[:n], outs[n:2 * n], outs[2 * n:]


def _select(me, table):
    return sum(jnp.where(me == d, jnp.int32(v), jnp.int32(0)) for d, v in enumerate(table))


WIN_SHIFT = tuple(SHARD_W * d - LANE * WIN_START[d] for d in range(N_DEV))
PAD_L = 256
PAD_R = 256


def _shard_to_window(shard_t, me):
    shift = _select(me, WIN_SHIFT)
    start = _select(me, WIN_START)
    padded = jnp.pad(shard_t, ((PAD_L, PAD_R), (0, 0)))
    cols = shard_t.shape[1]
    lo = lax.dynamic_slice(padded, (PAD_L - shift, 0), (WIN_W, cols))
    hi = lax.dynamic_slice(padded, (PAD_L - shift + N_BA, 0), (WIN_W, cols))
    aligned = LANE * start + lax.broadcasted_iota(jnp.int32, (WIN_W, 1), 0)
    return jnp.where(aligned >= ORIG_BA, hi, lo)


def _window_to_shard(win, ba_grad, me):
    shift = _select(me, WIN_SHIFT)
    cols = win.shape[1]
    padded = jnp.pad(win, ((N_BA, PAD_R), (0, 0)))
    lo = lax.dynamic_slice(padded, (N_BA + shift, 0), (SHARD_W, cols))
    hi = lax.dynamic_slice(padded, (shift, 0), (SHARD_W, cols))
    orig = SHARD_W * me + lax.broadcasted_iota(jnp.int32, (SHARD_W, 1), 0)
    ba_full = lax.dynamic_update_slice(jnp.zeros((SHARD_W, cols), win.dtype), ba_grad, (BA_LOCAL, 0))
    return jnp.where(orig < ORIG_BA, lo, jnp.where(orig >= ORIG_BA + N_BA, hi, ba_full))


def _pad_row(v, width=D_MODEL):
    v = v.reshape(1, -1)
    return jnp.pad(v, ((0, 0), (0, width - v.shape[1])))


def _slab(v, rows=8):
    return jnp.pad(v, ((0, rows - v.shape[0]), (0, D_MODEL - v.shape[1])))


def kernel(x, mem, norm_g, mem_norm_g, w_in, conv_w, a_log, dt_bias, dn_norm_g, w_mem_kv, w_br_dn, w_br_sb, w_br_mem, w_out, final_g, loss_target, m_norm_g, m_mem_norm_g, m_w_in, m_conv_w, m_a_log, m_dt_bias, m_dn_norm_g, m_w_mem_kv, m_w_br_dn, m_w_br_sb, m_w_br_mem, m_w_out, m_final_g, v_norm_g, v_mem_norm_g, v_w_in, v_conv_w, v_a_log, v_dt_bias, v_dn_norm_g, v_w_mem_kv, v_w_br_dn, v_w_br_sb, v_w_br_mem, v_w_out, v_final_g):
    xi, yi, ci = _position()
    me = 4 * xi + 2 * yi + ci

    shard_t = w_in[0].T
    win = _shard_to_window(shard_t, me).astype(BF16)
    ba = shard_t[BA_LOCAL:BA_LOCAL + N_BA, :].astype(BF16)
    g_win, g_ba, g_kv, g_dn, g_sb, g_out, g_mem, g_conv = _all_gather(
        [win, ba, w_mem_kv[0].astype(BF16), w_br_dn[0].astype(BF16), w_br_sb[0].astype(BF16), w_out[0].astype(BF16),
         w_br_mem[0].astype(BF16), conv_w[0]], "gather_weights")
    w_alt = _assemble_w_al(g_win, g_ba)
    w_mem_kv_f = g_kv.reshape(D_MODEL, 2 * MEM_W)
    w_br_dn_f = g_dn.reshape(D_MODEL, D_MODEL)
    w_br_sb_f = g_sb.reshape(D_MODEL, D_MODEL)
    w_out_f = g_out.reshape(D_MODEL, D_MODEL)
    w_br_mem_f = g_mem.transpose(1, 0, 2).reshape(MEM_W, D_MODEL)
    conv_w_f = g_conv.transpose(1, 0, 2).reshape(CONV_K, 3 * D_MODEL)

    r = _local_step(x[0], mem[0], loss_target[0], norm_g, mem_norm_g, w_alt, conv_w_f, _pad_row(a_log, LANE),
                    _pad_row(dt_bias, LANE), dn_norm_g, w_mem_kv_f, w_br_dn_f, w_br_sb_f, w_br_mem_f, w_out_f,
                    final_g.reshape(1, D_MODEL))

    dw_alt = r["w_alt"]
    rows_d = D_MODEL // N_DEV
    small_blocks = jnp.concatenate([
        r["w_br_dn"].reshape(N_DEV, rows_d, D_MODEL), r["w_br_sb"].reshape(N_DEV, rows_d, D_MODEL),
        r["w_out"].reshape(N_DEV, rows_d, D_MODEL), r["w_mem_kv"].reshape(N_DEV, rows_d // 2, D_MODEL),
        r["w_br_mem"].reshape(MEM_W, N_DEV, rows_d).transpose(1, 0, 2).reshape(N_DEV, MEM_W // N_DEV, D_MODEL)], axis=1)
    g_win, g_small = _reduce_scatter(dw_alt, [small_blocks])
    g_dn, g_sb, g_out = (g_small[k * rows_d:(k + 1) * rows_d] for k in range(3))
    g_kv = g_small[3 * rows_d:3 * rows_d + rows_d // 2].reshape(rows_d, 2 * MEM_W)
    g_mem = g_small[3 * rows_d + rows_d // 2:].reshape(MEM_W, rows_d)
    parts = [r["norm_g"], r["mem_norm_g"], r["final_g"], r["dn_norm_g"], r["scal"], r["loss"], r["conv_w"],
             dw_alt[O_BA:O_BA + N_BA, :].astype(F32)]
    s_norm_g, s_mem_norm_g, s_final_g, s_dn_norm_g, s_scal, s_loss, s_conv, s_ba = _sum_slots(
        _all_gather(parts, "gather_small"))
    loss = s_loss[0, 0]
    cw = conv_w.shape[2]
    g_conv = lax.dynamic_slice(s_conv, (0, cw * me), (CONV_K, cw))
    g_w_in_t = _window_to_shard(g_win, s_ba, me)
    grads = dict(norm_g=s_norm_g, mem_norm_g=s_mem_norm_g, w_in=g_w_in_t.T[None], conv_w=g_conv[None],
                 a_log=s_scal[0:1, :N_HEADS], dt_bias=s_scal[1:2, :N_HEADS], dn_norm_g=s_dn_norm_g, w_mem_kv=g_kv[None],
                 w_br_dn=g_dn[None], w_br_sb=g_sb[None], w_br_mem=g_mem[None], w_out=g_out[None],
                 final_g=s_final_g.reshape(D_MODEL))

    params = dict(norm_g=(norm_g, m_norm_g, v_norm_g), mem_norm_g=(mem_norm_g, m_mem_norm_g, v_mem_norm_g),
                  w_in=(w_in, m_w_in, v_w_in), conv_w=(conv_w, m_conv_w, v_conv_w), a_log=(a_log, m_a_log, v_a_log),
                  dt_bias=(dt_bias, m_dt_bias, v_dt_bias), dn_norm_g=(dn_norm_g, m_dn_norm_g, v_dn_norm_g),
                  w_mem_kv=(w_mem_kv, m_w_mem_kv, v_w_mem_kv), w_br_dn=(w_br_dn, m_w_br_dn, v_w_br_dn),
                  w_br_sb=(w_br_sb, m_w_br_sb, v_w_br_sb), w_br_mem=(w_br_mem, m_w_br_mem, v_w_br_mem),
                  w_out=(w_out, m_w_out, v_w_out), final_g=(final_g, m_final_g, v_final_g))
    order = list(params)
    deltas, new_m, new_v = {}, {}, {}
    deltas["w_in"], new_m["w_in"], new_v["w_in"] = (jnp.transpose(o, (1, 2, 0)) for o in _adamw(
        jnp.transpose(w_in, (2, 0, 1)), g_w_in_t[:, None, :], jnp.transpose(m_w_in, (2, 0, 1)),
        jnp.transpose(v_w_in, (2, 0, 1)), "adamw_w_in"))
    rest = [nm for nm in order if nm != "w_in"]

    def two_d(a):
        return a.reshape(1, -1) if a.ndim == 1 else a

    d_l, m_l, v_l = _adamw_many([two_d(params[nm][0]) for nm in rest], [two_d(grads[nm]) for nm in rest],
                                [two_d(params[nm][1]) for nm in rest], [two_d(params[nm][2]) for nm in rest], "adamw_rest")
    for k, nm in enumerate(rest):
        shp = params[nm][0].shape
        deltas[nm], new_m[nm], new_v[nm] = d_l[k].reshape(shp), m_l[k].reshape(shp), v_l[k].reshape(shp)
    return (loss, r["grad_x"][None], *[grads[nm] for nm in order], *[deltas[nm] for nm in order],
            *[new_m[nm] for nm in order], *[new_v[nm] for nm in order])
```

```python
import functools
import math

import jax
import jax.numpy as jnp
from jax import lax
from jax.experimental import pallas as pl
from jax.experimental.pallas import tpu as pltpu

F32 = jnp.float32
BF16 = jnp.bfloat16

D_MODEL = 1024
N_DEV = 8
N_HEADS = 8
D_HEAD = 128
DN_CHUNK = 64
CONV_K = 4
MEM_LEN = 256
MEM_HEADS = 4
MEM_DH = 64
MEM_W = MEM_HEADS * MEM_DH
NORM_EPS = 1e-6
IN_WIDTH = 11792
SHARD_W = IN_WIDTH // N_DEV

LANE = 128
SUPER = 2 * DN_CHUNK

O_QKV_DN = 0
O_Z_DN = 3072
O_QKV_SB = 4096
O_Z_SB = 7168
O_MQ = 8192
O_MZ = 8448
O_GATES = 8704
O_BA = 11776
W_AL = 11904
ORIG_BA = 4096
N_BA = 16

WIN_TILES = 13
WIN_W = WIN_TILES * LANE


def _aligned_col(o):
    return o if o < ORIG_BA else o - N_BA


WIN_START = tuple(min(_aligned_col(SHARD_W * d) // LANE, (W_AL // LANE) - WIN_TILES) for d in range(N_DEV))
WIN_OFF = tuple(_aligned_col(SHARD_W * d) - LANE * WIN_START[d] if SHARD_W * d >= ORIG_BA + N_BA or SHARD_W * d < ORIG_BA
                else None for d in range(N_DEV))
BA_DEV = ORIG_BA // SHARD_W
BA_LOCAL = ORIG_BA - BA_DEV * SHARD_W

ADAM_LR = 0.001
ADAM_B1 = 0.9
ADAM_B2 = 0.999
ADAM_EPS = 1e-08
ADAM_WD = 0.01
ADAM_STEP = 10

NN = (((1,), (0,)), ((), ()))
NT = (((1,), (1,)), ((), ()))
TN = (((0,), (0,)), ((), ()))


def _dot(a, b, dims):
    return lax.dot_general(a.astype(BF16), b.astype(BF16), dims, preferred_element_type=F32)


def _split2(a):
    hi = a.astype(BF16)
    lo = (a - hi.astype(F32)).astype(BF16)
    return hi, lo


def _dot3(a, b, dims):
    ah, al = _split2(a)
    bh, bl = _split2(b)
    d = functools.partial(lax.dot_general, dimension_numbers=dims, preferred_element_type=F32)
    return d(ah, bh) + (d(ah, bl) + d(al, bh))


def _sel_dot_impl(sel01, x, dims):
    sel = sel01.astype(BF16)
    h1 = x.astype(BF16)
    r1 = x - h1.astype(F32)
    h2 = r1.astype(BF16)
    h3 = (r1 - h2.astype(F32)).astype(BF16)
    d = functools.partial(lax.dot_general, dimension_numbers=dims, preferred_element_type=F32)
    return d(sel, h1) + (d(sel, h2) + d(sel, h3))


@jax.custom_vjp
def _sel_dot(sel01, x):
    return _sel_dot_impl(sel01, x, NN)


_sel_dot.defvjp(lambda s, x: (_sel_dot(s, x), s),
                lambda s, g: (jnp.zeros_like(s), _sel_dot_impl(s, g, TN)))


def _make_mm(dotfn):
    @jax.custom_vjp
    def nn(a, b):
        return dotfn(a, b, NN)

    @jax.custom_vjp
    def nt(a, b):
        return dotfn(a, b, NT)

    @jax.custom_vjp
    def tn(a, b):
        return dotfn(a, b, TN)

    nn.defvjp(lambda a, b: (nn(a, b), (a, b)), lambda r, g: (nt(g, r[1]), tn(r[0], g)))
    nt.defvjp(lambda a, b: (nt(a, b), (a, b)), lambda r, g: (nn(g, r[1]), tn(g, r[0])))
    tn.defvjp(lambda a, b: (tn(a, b), (a, b)), lambda r, g: (nt(r[1], g), nn(r[0], g)))
    return nn, nt, tn


mm_nn, mm_nt, mm_tn = _make_mm(_dot)
mm3_nn, mm3_nt, mm3_tn = _make_mm(_dot3)


def _sigmoid(x):
    return jax.nn.sigmoid(x)


def _silu(x):
    return x * _sigmoid(x)


def _softplus_parts(x):
    sp = jnp.log1p(jnp.exp(-jnp.abs(x)))
    return jnp.maximum(x, 0.0) + sp, jnp.maximum(-x, 0.0) + sp


def _rmsnorm(x, g):
    return x * lax.rsqrt(jnp.mean(x * x, axis=-1, keepdims=True) + NORM_EPS) * g


def _iota2(shape, dim):
    return lax.broadcasted_iota(jnp.int32, shape, dim)


def _div64(i):
    return lax.shift_right_logical(i, jnp.full(i.shape, 6, jnp.int32))


def _each(f, *lists):
    return [f(*a) for a in zip(*lists)]


@jax.custom_vjp
def _inv_unit_lower(ms):
    n = ms[0].shape[0]
    eye = (_iota2((n, n), 0) == _iota2((n, n), 1)).astype(F32)
    rs = [eye - m for m in ms]
    ps = ms
    for _ in range(5):
        ps = _each(mm3_nn, ps, ps)
        rs = _each(lambda r, p: r + mm_nn(r, p), rs, ps)
    return rs


def _inv_fwd(ms):
    rs = _inv_unit_lower(ms)
    return rs, rs


def _inv_bwd(rs, gs):
    ts = _each(mm_tn, rs, gs)
    return (_each(lambda t, r: -mm_nt(t, r), ts, rs),)


_inv_unit_lower.defvjp(_inv_fwd, _inv_bwd)


def _dn_block(cq, ck, cv, bcol, acol, zt, alog, dtb, gn, s0):
    n = SUPER
    h = DN_CHUNK
    row = _iota2((n, n), 0)
    col = _iota2((n, n), 1)
    same = _div64(row) == _div64(col)
    incl = jnp.logical_and(same, row >= col)
    strict = jnp.logical_and(same, row > col)
    incl_f = incl.astype(F32)

    qn = _each(lambda x: x * lax.rsqrt(jnp.sum(x * x, axis=-1, keepdims=True) + NORM_EPS) * (D_HEAD ** -0.5), cq)
    kn = _each(lambda x: x * lax.rsqrt(jnp.sum(x * x, axis=-1, keepdims=True) + NORM_EPS), ck)
    beta = _each(_sigmoid, bcol)
    g = _each(lambda al, ac, dt: -(jnp.exp(al) * _softplus_parts(ac + dt)[0]), alog, acol, dtb)
    gcum = _each(lambda x: _sel_dot(incl_f, jnp.broadcast_to(x, (n, n))), g)
    gam_incl = _each(lambda x: jnp.where(incl, jnp.exp(jnp.where(incl, x - x.T, 0.0)), 0.0), gcum)
    kk = _each(mm_nt, kn, kn)
    t_inv = _inv_unit_lower(_each(lambda b, x, gm: b * x * jnp.where(strict, gm, 0.0), beta, kk, gam_incl))
    eg = _each(jnp.exp, gcum)
    u = _each(lambda t, v, b: mm_nn(t, v * b), t_inv, cv, beta)
    w = _each(lambda t, k, b, e: mm_nn(t, k * (b * e)), t_inv, kn, beta, eg)
    a_intra = _each(lambda q, k, gm: mm_nt(q, k) * gm, qn, kn, gam_incl)
    q_dec = _each(lambda q, e: q * e, qn, eg)
    last0 = _each(lambda x: x[h - 1:h, :], gcum)
    last1 = _each(lambda x: x[n - 1:n, :], gcum)
    k_dec = _each(lambda k, x, l0, l1: k * jnp.exp(jnp.concatenate(
        [jnp.broadcast_to(l0, (h, n)), jnp.broadcast_to(l1, (h, n))], axis=0) - x), kn, gcum, last0, last1)
    v0 = _each(lambda uu, ww, s: uu[:h] - mm_nn(ww[:h], s), u, w, s0)
    o0 = _each(lambda q, s: mm_nn(q[:h], s), q_dec, s0)
    s1 = _each(lambda s, l0, k, v: s * jnp.exp(l0) + mm_tn(k[:h], v), s0, last0, k_dec, v0)
    v1 = _each(lambda uu, ww, s: uu[h:] - mm_nn(ww[h:], s), u, w, s1)
    o1 = _each(lambda q, s: mm_nn(q[h:], s), q_dec, s1)
    s2 = _each(lambda s, l1, k, v: s * jnp.exp(l1) + mm_tn(k[h:], v), s1, last1, k_dec, v1)
    o = _each(lambda a, b, am, x, y: jnp.concatenate([a, b], axis=0) + mm_nn(am, jnp.concatenate([x, y], axis=0)),
              o0, o1, a_intra, v0, v1)
    out = _each(lambda x, z: _rmsnorm(x, gn) * _silu(z), o, zt)
    return out, s2


def _mem_fn(mq, mz, mkv):
    mk = mkv[:, :MEM_W]
    mv = mkv[:, MEM_W:]
    lane = _iota2((1, MEM_W), 1)
    out = jnp.zeros(mq.shape, F32)
    for hd in range(MEM_HEADS):
        hm = (_div64(lane) == hd).astype(F32)
        s = mm_nt(mq * hm, mk) * (1.0 / math.sqrt(MEM_DH))
        s = s - jnp.max(s, axis=-1, keepdims=True)
        e = jnp.exp(s)
        p = e / jnp.sum(e, axis=-1, keepdims=True)
        out = out + mm_nn(p, mv) * hm
    return out * _silu(mz)


def _merge_fn(gd, gs, gm, yd, ys, ym):
    return _sigmoid(gd) * yd + _sigmoid(gs) * ys + _sigmoid(gm) * ym


def _loss_fn(x, mo, fg, tgt):
    y = _rmsnorm(x + mo, fg)
    err = y - tgt
    return 0.5 * jnp.sum(jnp.mean(err * err, axis=-1, keepdims=True), axis=0, keepdims=True)


def _matmul(a, b, mode, out_dtype, tm, tn, tk, name, b_col0=0, n_cols=None):
    if mode == "nn":
        m, kdim = a.shape
        n = b.shape[1] if n_cols is None else n_cols
    elif mode == "nt":
        m, kdim = a.shape
        n = b.shape[0]
    else:
        kdim, m = a.shape
        n = b.shape[1] if n_cols is None else n_cols
    tm, tn, tk = min(tm, m), min(tn, n), min(tk, kdim)
    assert m % tm == 0 and n % tn == 0 and kdim % tk == 0 and b_col0 % tn == 0
    nk = kdim // tk
    jb = b_col0 // tn
    dims = {"nn": NN, "nt": NT, "tn": TN}[mode]

    def body(a_ref, b_ref, o_ref, acc_ref):
        k = pl.program_id(2)
        part = _dot(a_ref[...], b_ref[...], dims)

        @pl.when(k == 0)
        def _():
            acc_ref[...] = part

        @pl.when(k > 0)
        def _():
            acc_ref[...] += part

        @pl.when(k == nk - 1)
        def _():
            o_ref[...] = acc_ref[...].astype(o_ref.dtype)

    if mode == "nn":
        a_spec = pl.BlockSpec((tm, tk), lambda i, j, k: (i, k))
        b_spec = pl.BlockSpec((tk, tn), lambda i, j, k: (k, j + jb))
    elif mode == "nt":
        a_spec = pl.BlockSpec((tm, tk), lambda i, j, k: (i, k))
        b_spec = pl.BlockSpec((tn, tk), lambda i, j, k: (j, k))
    else:
        a_spec = pl.BlockSpec((tk, tm), lambda i, j, k: (k, i))
        b_spec = pl.BlockSpec((tk, tn), lambda i, j, k: (k, j + jb))
    return pl.pallas_call(
        body,
        name=name,
        grid=(m // tm, n // tn, nk),
        in_specs=[a_spec, b_spec],
        out_specs=pl.BlockSpec((tm, tn), lambda i, j, k: (i, j)),
        out_shape=jax.ShapeDtypeStruct((m, n), out_dtype),
        scratch_shapes=[pltpu.VMEM((tm, tn), F32)],
        compiler_params=pltpu.CompilerParams(dimension_semantics=("parallel", "parallel", "arbitrary")),
    )(a, b)


def _norm_in(x, g, tm=256):
    t = x.shape[0]

    def body(x_ref, g_ref, h_ref):
        h_ref[...] = _rmsnorm(x_ref[...], g_ref[...]).astype(BF16)

    return pl.pallas_call(
        body,
        name="norm_in",
        grid=(t // tm,),
        in_specs=[pl.BlockSpec((tm, D_MODEL), lambda i: (i, 0)), pl.BlockSpec((1, D_MODEL), lambda i: (0, 0))],
        out_specs=pl.BlockSpec((tm, D_MODEL), lambda i: (i, 0)),
        out_shape=jax.ShapeDtypeStruct((t, D_MODEL), BF16),
    )(x, g)


def _norm_in_bwd(x, g, dh, dres, tm=256):
    t = x.shape[0]

    def body(x_ref, g_ref, dh_ref, dres_ref, dx_ref, dg_ref):
        _, vjp = jax.vjp(_rmsnorm, x_ref[...], g_ref[...])
        dx, dg = vjp(dh_ref[...])
        dx_ref[...] = dx + dres_ref[...]

        @pl.when(pl.program_id(0) == 0)
        def _():
            dg_ref[...] = jnp.zeros_like(dg_ref)

        dg_ref[...] += dg

    row = pl.BlockSpec((tm, D_MODEL), lambda i: (i, 0))
    vec = pl.BlockSpec((1, D_MODEL), lambda i: (0, 0))
    return pl.pallas_call(
        body,
        name="norm_in_bwd",
        grid=(t // tm,),
        in_specs=[row, vec, row, row],
        out_specs=[row, vec],
        out_shape=[jax.ShapeDtypeStruct((t, D_MODEL), F32), jax.ShapeDtypeStruct((1, D_MODEL), F32)],
    )(x, g, dh, dres)


def _merge(proj, yd, ys, ym, tm=256, tc=512):
    t = proj.shape[0]
    g0 = O_GATES // tc
    gstep = D_MODEL // tc

    def body(gd, gs, gm, yd_ref, ys_ref, ym_ref, o_ref):
        o_ref[...] = _merge_fn(gd[...], gs[...], gm[...], yd_ref[...], ys_ref[...], ym_ref[...]).astype(BF16)

    def gate(k):
        return pl.BlockSpec((tm, tc), lambda i, j: (i, g0 + k * gstep + j))

    blk = pl.BlockSpec((tm, tc), lambda i, j: (i, j))
    return pl.pallas_call(
        body,
        name="merge",
        grid=(t // tm, D_MODEL // tc),
        in_specs=[gate(0), gate(1), gate(2), blk, blk, blk],
        out_specs=blk,
        out_shape=jax.ShapeDtypeStruct((t, D_MODEL), BF16),
    )(proj, proj, proj, yd, ys, ym)


def _merge_bwd(proj, yd, ys, ym, dmerged, tm=256, tc=512):
    t = proj.shape[0]
    g0 = O_GATES // tc
    gstep = D_MODEL // tc

    def body(gd, gs, gm, yd_ref, ys_ref, ym_ref, dm_ref, dyd, dys, dym, dgd, dgs, dgm):
        _, vjp = jax.vjp(_merge_fn, gd[...], gs[...], gm[...], yd_ref[...], ys_ref[...], ym_ref[...])
        outs = vjp(dm_ref[...])
        for ref, val in zip((dgd, dgs, dgm, dyd, dys, dym), outs):
            ref[...] = val.astype(BF16)

    def gate(k):
        return pl.BlockSpec((tm, tc), lambda i, j: (i, g0 + k * gstep + j))

    blk = pl.BlockSpec((tm, tc), lambda i, j: (i, j))
    o = jax.ShapeDtypeStruct((t, D_MODEL), BF16)
    return pl.pallas_call(
        body,
        name="merge_bwd",
        grid=(t // tm, D_MODEL // tc),
        in_specs=[gate(0), gate(1), gate(2), blk, blk, blk, blk],
        out_specs=[blk] * 6,
        out_shape=[o] * 6,
    )(proj, proj, proj, yd, ys, ym, dmerged)


def _loss_head(x, mo, fg, tgt, tm=256):
    t = x.shape[0]

    def body(x_ref, mo_ref, fg_ref, t_ref, loss_ref, dout_ref, dfg_ref):
        loss, vjp = jax.vjp(_loss_fn, x_ref[...], mo_ref[...], fg_ref[...], t_ref[...])
        _, dmo, dfg, _ = vjp(jnp.ones((1, 1), F32))

        @pl.when(pl.program_id(0) == 0)
        def _():
            loss_ref[...] = jnp.zeros_like(loss_ref)
            dfg_ref[...] = jnp.zeros_like(dfg_ref)

        loss_ref[...] += jnp.broadcast_to(loss, loss_ref.shape)
        dfg_ref[...] += dfg
        dout_ref[...] = dmo

    row = pl.BlockSpec((tm, D_MODEL), lambda i: (i, 0))
    vec = pl.BlockSpec((1, D_MODEL), lambda i: (0, 0))
    return pl.pallas_call(
        body,
        name="loss_head",
        grid=(t // tm,),
        in_specs=[row, row, vec, row],
        out_specs=[pl.BlockSpec((1, LANE), lambda i: (0, 0)), row, vec],
        out_shape=[jax.ShapeDtypeStruct((1, LANE), F32), jax.ShapeDtypeStruct((t, D_MODEL), F32),
                   jax.ShapeDtypeStruct((1, D_MODEL), F32)],
    )(x, mo, fg, tgt)


def _shift_rows(x, s):
    t = x.shape[0]
    if s == 0:
        return x
    rolled = pltpu.roll(x, s % t, 0)
    row = _iota2(x.shape, 0)
    keep = row >= s if s > 0 else row < t + s
    return jnp.where(keep, rolled, 0.0)


def _conv_pre(x, w):
    return sum(_shift_rows(x, CONV_K - 1 - j) * w[j:j + 1, :] for j in range(CONV_K))


CONV_TC = 256


def _dn_conv(proj, conv_w):
    t = proj.shape[0]
    nb = 3 * D_MODEL // CONV_TC

    def body(x_ref, w_ref, c_ref):
        c_ref[...] = _silu(_conv_pre(x_ref[...], w_ref[...]))

    return pl.pallas_call(
        body,
        name="dn_conv",
        grid=(nb,),
        in_specs=[pl.BlockSpec((t, CONV_TC), lambda j: (0, j)), pl.BlockSpec((CONV_K, CONV_TC), lambda j: (0, j))],
        out_specs=pl.BlockSpec((t, CONV_TC), lambda j: (0, j)),
        out_shape=jax.ShapeDtypeStruct((t, 3 * D_MODEL), F32),
    )(proj, conv_w)


def _dn_conv_bwd(proj, conv_w, dc, part):
    t = proj.shape[0]
    nb = D_MODEL // CONV_TC
    b0 = part * nb

    def body(x_ref, w_ref, dc_ref, dx_ref, dw_ref):
        x = x_ref[...]
        w = w_ref[...]
        pre = _conv_pre(x, w)
        sg = _sigmoid(pre)
        dpre = dc_ref[...] * (sg * (1.0 + pre * (1.0 - sg)))
        dx = sum(_shift_rows(dpre, -(CONV_K - 1 - j)) * w[j:j + 1, :] for j in range(CONV_K))
        dx_ref[...] = dx.astype(BF16)
        dw_ref[...] = jnp.concatenate(
            [jnp.sum(dpre * _shift_rows(x, CONV_K - 1 - j), axis=0, keepdims=True) for j in range(CONV_K)], axis=0)

    blk = pl.BlockSpec((t, CONV_TC), lambda j: (0, j))
    return pl.pallas_call(
        body,
        name=f"dn_conv_bwd{part}",
        grid=(nb,),
        in_specs=[pl.BlockSpec((t, CONV_TC), lambda j: (0, b0 + j)),
                  pl.BlockSpec((CONV_K, CONV_TC), lambda j: (0, b0 + j)), blk],
        out_specs=[blk, pl.BlockSpec((CONV_K, CONV_TC), lambda j: (0, j))],
        out_shape=[jax.ShapeDtypeStruct((t, D_MODEL), BF16), jax.ShapeDtypeStruct((CONV_K, D_MODEL), F32)],
    )(proj, conv_w, dc)


def _ba_columns(ba, hd):
    lane = _iota2(ba.shape, 1)
    bcol = jnp.sum(jnp.where(lane == hd, ba, 0.0), axis=1, keepdims=True)
    acol = jnp.sum(jnp.where(lane == N_HEADS + hd, ba, 0.0), axis=1, keepdims=True)
    return bcol, acol


def _head_scalar(row, hd):
    lane = _iota2(row.shape, 1)
    return jnp.sum(jnp.where(lane == hd, row, 0.0), axis=1, keepdims=True)


DN_HP = 8


def _dn_inputs(cq, ck, cv, ba_ref, z_ref, alog_ref, dtb_ref, heads, lanes):
    ba = ba_ref[...]
    cols = [_ba_columns(ba, hd) for hd in heads]
    return ([cq[:, ln] for ln in lanes], [ck[:, ln] for ln in lanes], [cv[:, ln] for ln in lanes],
            [c[0] for c in cols], [c[1] for c in cols], [z_ref[:, ln] for ln in lanes],
            [_head_scalar(alog_ref[...], hd) for hd in heads], [_head_scalar(dtb_ref[...], hd) for hd in heads])


def _dn_specs(nblk, reverse):
    w = DN_HP * LANE
    nq = D_MODEL // w

    def row(i):
        return nblk - 1 - i if reverse else i

    def colblk(b0):
        return pl.BlockSpec((SUPER, w), lambda i, h: (row(i), b0 + h))

    ba = pl.BlockSpec((SUPER, LANE), lambda i, h: (row(i), O_BA // LANE))
    vec = pl.BlockSpec((1, LANE), lambda i, h: (0, 0))
    st = pl.BlockSpec((1, DN_HP, D_HEAD, D_HEAD), lambda i, h: (row(i), h, 0, 0))
    return colblk, nq, ba, vec, st


def _dn_fwd(c, proj, alog_row, dtb_row, gn):
    t = c.shape[0]
    nblk = t // SUPER
    colblk, nq, ba, vec, st = _dn_specs(nblk, False)

    def body(cq, ck, cv, ba_ref, z_ref, alog_ref, dtb_ref, gn_ref, o_ref, s_ref, state):
        @pl.when(jnp.logical_and(pl.program_id(0) == 0, pl.program_id(1) == 0))
        def _():
            state[...] = jnp.zeros_like(state)

        heads = [pl.program_id(1) * DN_HP + j for j in range(DN_HP)]
        lanes = [slice(j * LANE, (j + 1) * LANE) for j in range(DN_HP)]
        s0 = [state[hd] for hd in heads]
        outs, s2 = _dn_block(*_dn_inputs(cq, ck, cv, ba_ref, z_ref, alog_ref, dtb_ref, heads, lanes), gn_ref[...], s0)
        for j, (hd, ln) in enumerate(zip(heads, lanes)):
            s_ref[0, j] = s0[j]
            o_ref[:, ln] = outs[j].astype(BF16)
            state[hd] = s2[j]

    return pl.pallas_call(
        body,
        name="dn_fwd",
        grid=(nblk, N_HEADS // DN_HP),
        in_specs=[colblk(0), colblk(nq), colblk(2 * nq), ba, colblk(O_Z_DN // (DN_HP * LANE)), vec, vec, vec],
        out_specs=[colblk(0), st],
        out_shape=[jax.ShapeDtypeStruct((t, D_MODEL), BF16),
                   jax.ShapeDtypeStruct((nblk, N_HEADS, D_HEAD, D_HEAD), F32)],
        scratch_shapes=[pltpu.VMEM((N_HEADS, D_HEAD, D_HEAD), F32)],
    )(c, c, c, proj, proj, alog_row, dtb_row, gn)


def _dn_bwd(c, proj, alog_row, dtb_row, gn, states, do):
    t = c.shape[0]
    nblk = t // SUPER
    colblk, nq, ba, vec, st = _dn_specs(nblk, True)

    def body(cq, ck, cv, ba_ref, z_ref, alog_ref, dtb_ref, gn_ref, s_ref, do_ref,
             dq_ref, dk_ref, dv_ref, dz_ref, dba_ref, dsc_ref, dgn_ref, dstate):
        i = pl.program_id(0)
        hq = pl.program_id(1)

        @pl.when(jnp.logical_and(i == 0, hq == 0))
        def _():
            dstate[...] = jnp.zeros_like(dstate)
            dsc_ref[...] = jnp.zeros_like(dsc_ref)
            dgn_ref[...] = jnp.zeros_like(dgn_ref)

        @pl.when(hq == 0)
        def _():
            dba_ref[...] = jnp.zeros_like(dba_ref)

        lane = _iota2((SUPER, LANE), 1)
        lane1 = _iota2((1, LANE), 1)
        heads = [hq * DN_HP + j for j in range(DN_HP)]
        lanes = [slice(j * LANE, (j + 1) * LANE) for j in range(DN_HP)]
        ds_in = [dstate[hd] for hd in heads]
        s_in = [s_ref[0, j] for j in range(DN_HP)]
        _, vjp = jax.vjp(_dn_block, *_dn_inputs(cq, ck, cv, ba_ref, z_ref, alog_ref, dtb_ref, heads, lanes),
                         gn_ref[...], s_in)
        dq, dk, dv, dbc, dac, dz, dal, ddt, dgn, ds0 = vjp(([do_ref[:, ln].astype(F32) for ln in lanes], ds_in))
        dba = jnp.zeros((SUPER, LANE), F32)
        dal_row = jnp.zeros((1, LANE), F32)
        ddt_row = jnp.zeros((1, LANE), F32)
        for j, (hd, ln) in enumerate(zip(heads, lanes)):
            dq_ref[:, ln] = dq[j]
            dk_ref[:, ln] = dk[j]
            dv_ref[:, ln] = dv[j]
            dz_ref[:, ln] = dz[j].astype(BF16)
            dstate[hd] = ds0[j]
            dba = dba + jnp.where(lane == hd, dbc[j], 0.0) + jnp.where(lane == N_HEADS + hd, dac[j], 0.0)
            dal_row = dal_row + jnp.where(lane1 == hd, dal[j], 0.0)
            ddt_row = ddt_row + jnp.where(lane1 == hd, ddt[j], 0.0)
        dba_ref[...] += dba
        dsc_ref[0:1, :] += dal_row
        dsc_ref[1:2, :] += ddt_row
        dgn_ref[...] += dgn

    outs = pl.pallas_call(
        body,
        name="dn_bwd",
        grid=(nblk, N_HEADS // DN_HP),
        in_specs=[colblk(0), colblk(nq), colblk(2 * nq), ba, colblk(O_Z_DN // (DN_HP * LANE)), vec, vec, vec, st,
                  colblk(0)],
        out_specs=[colblk(0), colblk(0), colblk(0), colblk(0),
                   pl.BlockSpec((SUPER, LANE), lambda i, h: (nblk - 1 - i, 0)),
                   pl.BlockSpec((2, LANE), lambda i, h: (0, 0)), vec],
        out_shape=[jax.ShapeDtypeStruct((t, D_MODEL), F32)] * 3
        + [jax.ShapeDtypeStruct((t, D_MODEL), BF16), jax.ShapeDtypeStruct((t, LANE), F32),
           jax.ShapeDtypeStruct((2, LANE), F32), jax.ShapeDtypeStruct((1, LANE), F32)],
        scratch_shapes=[pltpu.VMEM((N_HEADS, D_HEAD, D_HEAD), F32)],
    )(c, c, c, proj, proj, alog_row, dtb_row, gn, states, do)
    return outs


SB_TQ = 256
SB_TK = 256
SB_HP_FWD = 8
SB_HP_BWD = 4


def _sb_logits(z, mask):
    sp = jnp.log(1.0 + jnp.exp(-jnp.abs(z)))
    lf_raw = -(jnp.maximum(z, 0.0) + sp)
    lb = lf_raw + z
    lf = lf_raw if mask is None else jnp.where(mask, lf_raw, 0.0)
    return lb, lf_raw, lf


def _suffix_sums(x, sel):
    hi, lo = _split2(x)
    d = functools.partial(lax.dot_general, dimension_numbers=NN, preferred_element_type=F32)
    return d(hi, sel) + d(lo, sel)


def _sb_diag_mask(tq, r):
    return r * SB_TK + _iota2((tq, SB_TK), 1) < _iota2((tq, SB_TK), 0)


def _sb_specs(t, tq, hp):
    w = hp * LANE
    q0, k0, v0, z0 = (O_QKV_SB // w, (O_QKV_SB + D_MODEL) // w, (O_QKV_SB + 2 * D_MODEL) // w, O_Z_SB // w)

    def blk(b0):
        return pl.BlockSpec((tq, w), lambda h, i: (i, b0 + h))

    def full(b0, **kw):
        return pl.BlockSpec((t, w), lambda h, i: (0, b0 + h), **kw)

    once = dict(pipeline_mode=pl.Buffered(1))
    return blk(q0), full(k0, **once), full(v0, **once), blk(z0), blk(0), full(0)


def _sb_fwd(proj):
    t = proj.shape[0]
    tq = min(SB_TQ, t)
    ndiag = tq // SB_TK
    scale = 1.0 / math.sqrt(D_HEAD)

    def body(q_ref, k_ref, v_ref, z_ref, o_ref, oraw_ref):
        qi = pl.program_id(1)
        lanes = [slice(hd * LANE, (hd + 1) * LANE) for hd in range(SB_HP_FWD)]
        qs = [(q_ref[:, ln] * scale).astype(BF16) for ln in lanes]
        after = (_iota2((SB_TK, SB_TK), 0) > _iota2((SB_TK, SB_TK), 1)).astype(BF16)
        oraw_ref[...] = jnp.zeros_like(oraw_ref)

        def block(kb, mask, c_lf):
            rows = pl.ds(pl.multiple_of(kb * SB_TK, SB_TK), SB_TK)
            z = _each(lambda q, ln: _dot(q, k_ref[rows, ln], NT), qs, lanes)
            lg = _each(lambda x: _sb_logits(x, mask), z)
            surv = _each(lambda x: _suffix_sums(x[2], after), lg)
            att = _each(lambda x, s, c: jnp.exp(x[0] + s + c), lg, surv, c_lf)
            if mask is not None:
                att = _each(lambda a: jnp.where(mask, a, 0.0), att)
            pv = _each(lambda a, ln: _dot(a, v_ref[rows, ln], NN), att, lanes)
            for p, ln in zip(pv, lanes):
                oraw_ref[:, ln] += p
            return tuple(_each(lambda c, x: c + jnp.sum(x[2], axis=1, keepdims=True), c_lf, lg))

        carry = tuple(jnp.zeros((tq, 1), F32) for _ in range(SB_HP_FWD))
        for r in reversed(range(ndiag)):
            carry = block(qi * ndiag + r, _sb_diag_mask(tq, r), carry)
        lax.fori_loop(0, qi * ndiag, lambda i, c: block(qi * ndiag - 1 - i, None, c), carry)
        o_ref[...] = (oraw_ref[...] * _silu(z_ref[...])).astype(BF16)

    q_spec, k_spec, v_spec, z_spec, out, _ = _sb_specs(t, tq, SB_HP_FWD)
    return pl.pallas_call(
        body,
        name="sb_fwd",
        grid=(N_HEADS // SB_HP_FWD, t // tq),
        in_specs=[q_spec, k_spec, v_spec, z_spec],
        out_specs=[out, out],
        out_shape=[jax.ShapeDtypeStruct((t, D_MODEL), BF16), jax.ShapeDtypeStruct((t, D_MODEL), F32)],
    )(proj, proj, proj, proj)


def _sb_bwd(proj, oraw, do):
    t = proj.shape[0]
    tq = min(SB_TQ, t)
    ndiag = tq // SB_TK
    scale = 1.0 / math.sqrt(D_HEAD)

    def body(q_ref, k_ref, v_ref, z_ref, oraw_ref, do_ref, dq_ref, dk_ref, dv_ref, dz_ref, dk_acc, dv_acc,
             p_scr, z_scr):
        qi = pl.program_id(1)
        nq = pl.num_programs(1)

        @pl.when(qi == 0)
        def _():
            dk_acc[...] = jnp.zeros_like(dk_acc)
            dv_acc[...] = jnp.zeros_like(dv_acc)

        heads = range(SB_HP_BWD)
        lanes = [slice(hd * LANE, (hd + 1) * LANE) for hd in heads]
        zg = z_ref[...]
        sg = _sigmoid(zg)
        dog = do_ref[...].astype(F32)
        dz_ref[...] = (dog * oraw_ref[...] * (sg * (1.0 + zg * (1.0 - sg)))).astype(BF16)
        d_o = (dog * (zg * sg)).astype(BF16)
        d_o16 = [d_o[:, ln] for ln in lanes]
        qs = [(q_ref[:, ln] * scale).astype(BF16) for ln in lanes]
        ri = _iota2((SB_TK, SB_TK), 0)
        ci = _iota2((SB_TK, SB_TK), 1)
        after = (ri > ci).astype(BF16)
        earlier = (ri < ci).astype(BF16)

        def rows_of(kb):
            return pl.ds(pl.multiple_of(kb * SB_TK, SB_TK), SB_TK)

        def down(kb, mask, c_lf):
            rows = rows_of(kb)
            z = _each(lambda q, ln: _dot(q, k_ref[rows, ln], NT), qs, lanes)
            da = _each(lambda d, ln: _dot(d, v_ref[rows, ln], NT), d_o16, lanes)
            lg = _each(lambda x: _sb_logits(x, mask), z)
            surv = _each(lambda x: _suffix_sums(x[2], after), lg)
            att = _each(lambda x, s, c: jnp.exp(x[0] + s + c), lg, surv, c_lf)
            if mask is not None:
                att = _each(lambda a: jnp.where(mask, a, 0.0), att)
            dv = _each(lambda a, d: _dot(a, d, TN), att, d_o16)
            for hd in heads:
                p_scr[hd, kb] = att[hd] * da[hd]
                z_scr[hd, kb] = z[hd]
                dv_acc[rows, lanes[hd]] += dv[hd]
            return tuple(_each(lambda c, x: c + jnp.sum(x[2], axis=1, keepdims=True), c_lf, lg))

        c_lf = tuple(jnp.zeros((tq, 1), F32) for _ in heads)
        for r in reversed(range(ndiag)):
            c_lf = down(qi * ndiag + r, _sb_diag_mask(tq, r), c_lf)
        lax.fori_loop(0, qi * ndiag, lambda i, c: down(qi * ndiag - 1 - i, None, c), c_lf)

        def up(kb, mask, carry):
            dq, c_p = carry
            rows = rows_of(kb)
            p = [p_scr[hd, kb] for hd in heads]
            zs = [z_scr[hd, kb] for hd in heads]
            before = _each(lambda x, c: _suffix_sums(x, earlier) + c, p, c_p)
            e = _each(lambda x: jnp.exp(-jnp.abs(x)), zs)
            r = _each(lambda x: 1.0 / (1.0 + x), e)
            sig = _each(lambda x, a, b: jnp.where(x >= 0.0, b, a * b), zs, e, r)
            oms = _each(lambda x, a, b: jnp.where(x >= 0.0, a * b, b), zs, e, r)
            if mask is not None:
                sig = _each(lambda a: jnp.where(mask, a, 0.0), sig)
            dzz = _each(lambda x, o, g, b: x * o - g * b, p, oms, sig, before)
            dk = _each(lambda x, q: _dot(x, q, TN), dzz, qs)
            dq = _each(lambda a, x, ln: a + _dot(x, k_ref[rows, ln], NN), dq, dzz, lanes)
            for hd in heads:
                dk_acc[rows, lanes[hd]] += dk[hd]
            return tuple(dq), tuple(_each(lambda c, x: c + jnp.sum(x, axis=1, keepdims=True), c_p, p))

        carry = (tuple(jnp.zeros((tq, D_HEAD), F32) for _ in heads), tuple(jnp.zeros((tq, 1), F32) for _ in heads))
        carry = lax.fori_loop(0, qi * ndiag, lambda kb, c: up(kb, None, c), carry)
        for r in range(ndiag):
            carry = up(qi * ndiag + r, _sb_diag_mask(tq, r), carry)
        dq = carry[0]
        for hd in heads:
            dq_ref[:, lanes[hd]] = (dq[hd] * scale).astype(BF16)

        @pl.when(qi == nq - 1)
        def _():
            dk_ref[...] = dk_acc[...].astype(BF16)
            dv_ref[...] = dv_acc[...].astype(BF16)

    q_spec, k_spec, v_spec, z_spec, blk, full = _sb_specs(t, tq, SB_HP_BWD)
    o = jax.ShapeDtypeStruct((t, D_MODEL), BF16)
    w = SB_HP_BWD * LANE
    return pl.pallas_call(
        body,
        name="sb_bwd",
        grid=(N_HEADS // SB_HP_BWD, t // tq),
        in_specs=[q_spec, k_spec, v_spec, z_spec, blk, blk],
        out_specs=[blk, full, full, blk],
        out_shape=[o, o, o, o],
        scratch_shapes=[pltpu.VMEM((t, w), F32), pltpu.VMEM((t, w), F32)]
        + [pltpu.VMEM((SB_HP_BWD, t // SB_TK, tq, SB_TK), F32)] * 2,
    )(proj, proj, proj, proj, oraw, do)


def _mem_kv_fn(mem, mg, w):
    return mm_nn(_rmsnorm(mem, mg), w)


def _mem_kv(mem, mg, w):
    def body(m_ref, g_ref, w_ref, o_ref):
        o_ref[...] = _mem_kv_fn(m_ref[...], g_ref[...], w_ref[...])

    return pl.pallas_call(body, name="mem_kv", out_shape=jax.ShapeDtypeStruct((MEM_LEN, 2 * MEM_W), F32))(mem, mg, w)


def _mem_kv_bwd(mem, mg, w, dmkv):
    def body(m_ref, g_ref, w_ref, d_ref, dg_ref, dw_ref):
        _, vjp = jax.vjp(_mem_kv_fn, m_ref[...], g_ref[...], w_ref[...].astype(F32))
        _, dg, dw = vjp(d_ref[...])
        dg_ref[...] = dg
        dw_ref[...] = dw.astype(BF16)

    return pl.pallas_call(
        body, name="mem_kv_bwd",
        out_shape=[jax.ShapeDtypeStruct((1, D_MODEL), F32), jax.ShapeDtypeStruct((D_MODEL, 2 * MEM_W), BF16)],
    )(mem, mg, w, dmkv)


def _mem_attn(proj, mkv, tm=256):
    t = proj.shape[0]
    tm = min(tm, t)

    def body(q_ref, z_ref, kv_ref, o_ref):
        o_ref[...] = _mem_fn(q_ref[...], z_ref[...], kv_ref[...]).astype(BF16)

    return pl.pallas_call(
        body,
        name="mem_attn",
        grid=(t // tm,),
        in_specs=[pl.BlockSpec((tm, MEM_W), lambda i: (i, O_MQ // MEM_W)),
                  pl.BlockSpec((tm, MEM_W), lambda i: (i, O_MZ // MEM_W)),
                  pl.BlockSpec((MEM_LEN, 2 * MEM_W), lambda i: (0, 0))],
        out_specs=pl.BlockSpec((tm, MEM_W), lambda i: (i, 0)),
        out_shape=jax.ShapeDtypeStruct((t, MEM_W), BF16),
    )(proj, proj, mkv)


def _mem_attn_bwd(proj, mkv, do, tm=256):
    t = proj.shape[0]
    tm = min(tm, t)

    def body(q_ref, z_ref, kv_ref, do_ref, dq_ref, dz_ref, dkv_ref):
        _, vjp = jax.vjp(_mem_fn, q_ref[...], z_ref[...], kv_ref[...])
        dq, dz, dkv = vjp(do_ref[...].astype(F32))
        dq_ref[...] = dq.astype(BF16)
        dz_ref[...] = dz.astype(BF16)

        @pl.when(pl.program_id(0) == 0)
        def _():
            dkv_ref[...] = jnp.zeros_like(dkv_ref)

        dkv_ref[...] += dkv

    blk = pl.BlockSpec((tm, MEM_W), lambda i: (i, 0))
    kv = pl.BlockSpec((MEM_LEN, 2 * MEM_W), lambda i: (0, 0))
    return pl.pallas_call(
        body,
        name="mem_attn_bwd",
        grid=(t // tm,),
        in_specs=[pl.BlockSpec((tm, MEM_W), lambda i: (i, O_MQ // MEM_W)),
                  pl.BlockSpec((tm, MEM_W), lambda i: (i, O_MZ // MEM_W)), kv, blk],
        out_specs=[blk, blk, kv],
        out_shape=[jax.ShapeDtypeStruct((t, MEM_W), BF16), jax.ShapeDtypeStruct((t, MEM_W), BF16),
                   jax.ShapeDtypeStruct((MEM_LEN, 2 * MEM_W), F32)],
    )(proj, proj, mkv, do)


def _local_step(x, mem, tgt, norm_g, mem_norm_g, w_alt, conv_w, alog_row, dtb_row, dn_norm_g, w_mem_kv, w_br_dn, w_br_sb,
                w_br_mem, w_out, final_g):
    h = _norm_in(x, norm_g)
    proj = _matmul(h, w_alt, "nt", F32, 2048, 384, 1024, "proj")

    c = _dn_conv(proj, conv_w)
    o_dn, states = _dn_fwd(c, proj, alog_row, dtb_row, dn_norm_g)
    o_sb, o_sb_raw = _sb_fwd(proj)
    mkv = _mem_kv(mem, mem_norm_g, w_mem_kv)
    o_m = _mem_attn(proj, mkv)

    y_dn = _matmul(o_dn, w_br_dn, "nn", F32, 512, 1024, 1024, "y_dn")
    y_sb = _matmul(o_sb, w_br_sb, "nn", F32, 512, 1024, 1024, "y_sb")
    y_m = _matmul(o_m, w_br_mem, "nn", F32, 512, 1024, 1024, "y_m")
    merged = _merge(proj, y_dn, y_sb, y_m)
    mo = _matmul(merged, w_out, "nn", F32, 512, 1024, 1024, "mo")
    loss, dout, d_final_g = _loss_head(x, mo, final_g, tgt)

    dmerged = _matmul(dout, w_out, "nt", F32, 512, 1024, 1024, "dmerged")
    dw_out = _matmul(merged, dout, "tn", BF16, 256, 1024, 2048, "dw_out")
    dy_dn, dy_sb, dy_m, dg_dn, dg_sb, dg_m = _merge_bwd(proj, y_dn, y_sb, y_m, dmerged)
    do_dn = _matmul(dy_dn, w_br_dn, "nt", BF16, 512, 1024, 1024, "do_dn")
    do_sb = _matmul(dy_sb, w_br_sb, "nt", BF16, 512, 1024, 1024, "do_sb")
    do_m = _matmul(dy_m, w_br_mem, "nt", BF16, 512, 256, 1024, "do_m")
    dw_br_dn = _matmul(o_dn, dy_dn, "tn", BF16, 256, 1024, 2048, "dw_br_dn")
    dw_br_sb = _matmul(o_sb, dy_sb, "tn", BF16, 256, 1024, 2048, "dw_br_sb")
    dw_br_mem = _matmul(o_m, dy_m, "tn", BF16, 256, 1024, 2048, "dw_br_mem")

    dmq, dmz, dmkv = _mem_attn_bwd(proj, mkv, do_m)
    d_mem_norm_g, dw_mem_kv = _mem_kv_bwd(mem, mem_norm_g, w_mem_kv, dmkv)
    dq_sb, dk_sb, dv_sb, dz_sb = _sb_bwd(proj, o_sb_raw, do_sb)
    dcq, dck, dcv, dz_dn, dba, dscal, d_dn_norm_g = _dn_bwd(c, proj, alog_row, dtb_row, dn_norm_g, states, do_dn)
    dq_dn, dcw_q = _dn_conv_bwd(proj, conv_w, dcq, 0)
    dk_dn, dcw_k = _dn_conv_bwd(proj, conv_w, dck, 1)
    dv_dn, dcw_v = _dn_conv_bwd(proj, conv_w, dcv, 2)
    d_conv_w = jnp.concatenate([dcw_q, dcw_k, dcw_v], axis=1)

    dproj = jnp.concatenate([dq_dn, dk_dn, dv_dn, dz_dn, dq_sb, dk_sb, dv_sb, dz_sb, dmq, dmz, dg_dn, dg_sb, dg_m,
                             dba.astype(BF16)], axis=1)
    dh = _matmul(dproj, w_alt, "nn", F32, 512, 1024, 3968, "dh")
    dw_alt = _matmul(dproj, h, "tn", BF16, 384, 1024, 2048, "dw_alt")
    grad_x, d_norm_g = _norm_in_bwd(x, norm_g, dh, dout)
    return dict(loss=loss, grad_x=grad_x, norm_g=d_norm_g, mem_norm_g=d_mem_norm_g, w_alt=dw_alt, conv_w=d_conv_w,
                scal=dscal, dn_norm_g=d_dn_norm_g, w_mem_kv=dw_mem_kv, w_br_dn=dw_br_dn, w_br_sb=dw_br_sb,
                w_br_mem=dw_br_mem, w_out=dw_out, final_g=d_final_g)


MESH = pl.DeviceIdType.MESH
ANY = pl.BlockSpec(memory_space=pl.ANY)


def _position():
    return lax.axis_index("x"), lax.axis_index("y"), lax.axis_index("c")


def _all_gather(xs, name):
    n = len(xs)

    def body(*refs):
        x_refs, o_refs = refs[:n], refs[n:2 * n]
        send_sems, recv_sems, local_sems = refs[2 * n:]
        x, y, c = _position()
        me, sibling = (x, y, c), (x, y, 1 - c)
        x_nbr, y_nbr, diag = (1 - x, y, c), (x, 1 - y, c), (1 - x, 1 - y, c)
        south = c == 0
        relay_from = tuple(jnp.where(south, a, b) for a, b in zip(y_nbr, x_nbr))
        relay_to = tuple(jnp.where(south, a, b) for a, b in zip(x_nbr, y_nbr))

        def slot(p):
            return 4 * p[0] + 2 * p[1] + p[2]

        def copy(a, k, block, to, src=None):
            dst = o_refs[a].at[slot(block)]
            return pltpu.make_async_remote_copy(
                src_ref=dst if src is None else src, dst_ref=dst, send_sem=send_sems.at[7 * a + k],
                recv_sem=recv_sems.at[7 * a + k], device_id=to, device_id_type=MESH)

        mine = [pltpu.make_async_copy(x_refs[a], o_refs[a].at[slot(me)], local_sems.at[a]) for a in range(n)]
        for cp in mine:
            cp.start()
        sends = []
        for a in range(n):
            sends += [copy(a, 0, me, sibling, src=x_refs[a]), copy(a, 1, me, x_nbr, src=x_refs[a]),
                      copy(a, 2, me, y_nbr, src=x_refs[a])]
        for cp in sends:
            cp.start()
        later = []
        for a in range(n):
            copy(a, 1, x_nbr, me).wait_recv()
            copy(a, 2, y_nbr, me).wait_recv()
            later += [copy(a, 3, relay_from, relay_to), copy(a, 4, x_nbr, sibling), copy(a, 5, y_nbr, sibling)]
            for cp in later[-3:]:
                cp.start()
        for a in range(n):
            copy(a, 3, diag, me).wait_recv()
            later.append(copy(a, 6, diag, sibling))
            later[-1].start()
        for a in range(n):
            copy(a, 0, sibling, me).wait_recv()
            for k, chip in ((4, x_nbr), (5, y_nbr), (6, diag)):
                copy(a, k, (chip[0], chip[1], 1 - c), me).wait_recv()
        for cp in sends + later:
            cp.wait_send()
        for cp in mine:
            cp.wait()

    return pl.pallas_call(
        body,
        name=name,
        in_specs=[ANY] * n,
        out_specs=[ANY] * n,
        out_shape=[jax.ShapeDtypeStruct((N_DEV, *v.shape), v.dtype) for v in xs],
        scratch_shapes=[pltpu.SemaphoreType.DMA((7 * n,)), pltpu.SemaphoreType.DMA((7 * n,)),
                        pltpu.SemaphoreType.DMA((n,))],
    )(*xs)


def _window_view(ref, dest):
    return ref.at[pl.ds(LANE * WIN_START[dest], WIN_W), :]


def _chunk_rows(rows, cols):
    return max(ch for ch in range(16, rows + 1, 16) if rows % ch == 0 and ch * cols <= (1 << 19))


def _halving_stage(xs, axis, name, out_dtype, windowed=()):
    n_arr = len(xs)
    metas = []
    for k, v in enumerate(xs):
        if k in windowed:
            metas.append((N_DEV // 2, WIN_W, v.shape[1]))
        else:
            assert v.shape[1] == 2
            metas.append((v.shape[0], v.shape[2], v.shape[3]))
    chunk = [_chunk_rows(r, c) for (_, r, c) in metas]
    offs = [sum(m[0] for m in metas[:k]) for k in range(n_arr)]
    n_sem = sum(m[0] for m in metas)

    def body(*refs):
        x_refs = refs[:n_arr]
        o_refs = refs[n_arr:2 * n_arr]
        land_refs = refs[2 * n_arr:3 * n_arr]
        rest = refs[3 * n_arr:]
        bufs = rest[:3 * n_arr]
        send_sems, recv_sems, in_sems, out_sems = rest[3 * n_arr:]
        pos = dict(zip("xyc", _position()))
        bit = pos[axis]
        peer = tuple(1 - pos[a] if a == axis else pos[a] for a in "xyc")

        def view(k, i, b):
            if k in windowed:
                return _window_view(x_refs[k], 2 * i + b)
            return x_refs[k].at[i, b]

        def add_blocks(k, a_view, b_view, o_view):
            _hbm_add(a_view, b_view, o_view, bufs[3 * k:3 * k + 3], in_sems, out_sems, chunk[k])

        for b in (0, 1):
            @pl.when(bit == b)
            def _(b=b):
                sends = []
                for k in range(n_arr):
                    for i in range(metas[k][0]):
                        cp = pltpu.make_async_remote_copy(
                            src_ref=view(k, i, 1 - b), dst_ref=land_refs[k].at[i], send_sem=send_sems.at[offs[k] + i],
                            recv_sem=recv_sems.at[offs[k] + i], device_id=peer, device_id_type=MESH)
                        cp.start()
                        sends.append(cp)
                idx = 0
                for k in range(n_arr):
                    for i in range(metas[k][0]):
                        sends[idx].wait_recv()
                        add_blocks(k, view(k, i, b), land_refs[k].at[i], o_refs[k].at[i])
                        idx += 1
                for cp in sends:
                    cp.wait_send()

    out_shape = [jax.ShapeDtypeStruct(m, out_dtype) for m in metas]
    land_shape = [jax.ShapeDtypeStruct(m, v.dtype) for m, v in zip(metas, xs)]
    scratch = []
    for k in range(n_arr):
        blk = (2, chunk[k], metas[k][2])
        scratch += [pltpu.VMEM(blk, xs[k].dtype)] * 2 + [pltpu.VMEM(blk, out_dtype)]
    scratch += [pltpu.SemaphoreType.DMA((n_sem,)), pltpu.SemaphoreType.DMA((n_sem,)),
                pltpu.SemaphoreType.DMA((2, 2)), pltpu.SemaphoreType.DMA((2,))]
    outs = pl.pallas_call(
        body,
        name=name,
        in_specs=[ANY] * n_arr,
        out_specs=[ANY] * (2 * n_arr),
        out_shape=out_shape + land_shape,
        scratch_shapes=scratch,
    )(*xs)
    return outs[:n_arr]


def _hbm_add(a_view, b_view, o_view, bufs, in_sems, out_sems, ch):
    rows = a_view.shape[0]
    nch = rows // ch
    va, vb, vo = bufs

    def rows_of(j):
        return pl.ds(pl.multiple_of(j * ch, 16), ch)

    def loads(j, s):
        return (pltpu.make_async_copy(a_view.at[rows_of(j), :], va.at[s], in_sems.at[0, s]),
                pltpu.make_async_copy(b_view.at[rows_of(j), :], vb.at[s], in_sems.at[1, s]))

    def store(j, s):
        return pltpu.make_async_copy(vo.at[s], o_view.at[rows_of(j), :], out_sems.at[s])

    for cp in loads(0, 0):
        cp.start()

    def step(j, _):
        s = lax.rem(j, 2)

        @pl.when(j + 1 < nch)
        def _():
            for cp in loads(j + 1, 1 - s):
                cp.start()

        for cp in loads(j, s):
            cp.wait()

        @pl.when(j >= 2)
        def _():
            store(j - 2, s).wait()

        vo[s] = (va[s].astype(F32) + vb[s].astype(F32)).astype(vo.dtype)
        store(j, s).start()
        return 0

    lax.fori_loop(0, nch, step, 0)
    for j in range(max(0, nch - 2), nch):
        store(j, j % 2).wait()


def _xy_stage(xs, first, name):
    n_arr = len(xs)
    if first:
        shapes = [(v.shape[2] // 2, v.shape[3]) for v in xs]
        ins = list(xs)
    else:
        shapes = [(a.shape[1], a.shape[2]) for a, _ in xs]
        ins = [v for pair in xs for v in pair]
    n_blk = 2 if first else 1
    out_dtype = BF16 if first else F32
    chunk = [_chunk_rows(r, c) for (r, c) in shapes]
    n_sem = 2 * n_blk * n_arr

    def body(*refs):
        n_in = len(ins)
        in_refs = refs[:n_in]
        n_out = 2 * n_arr if first else n_arr
        o_refs = refs[n_in:n_in + n_out]
        land = refs[n_in + n_out:n_in + n_out + 2 * n_arr]
        rest = refs[n_in + n_out + 2 * n_arr:]
        bufs = rest[:3 * n_arr]
        send_sems, recv_sems, in_sems, out_sems = rest[3 * n_arr:]
        x, y, c = _position()
        peers = {"x": (1 - x, y, c), "y": (x, 1 - y, c)}
        jobs = []
        for k in range(n_arr):
            r, _ = shapes[k]
            half_a, half_b = pl.ds(0, r), pl.ds(r, r)
            if first:
                src = in_refs[k]
                for i in range(2):
                    jobs.append((k, src.at[i, 1 - y, half_a, :], src.at[i, y, half_a, :], land[2 * k].at[i],
                                 o_refs[2 * k].at[i], "y"))
                    jobs.append((k, src.at[1 - x, i, half_b, :], src.at[x, i, half_b, :], land[2 * k + 1].at[i],
                                 o_refs[2 * k + 1].at[i], "x"))
            else:
                a1, b1 = in_refs[2 * k], in_refs[2 * k + 1]
                jobs.append((k, a1.at[1 - x], a1.at[x], land[2 * k], o_refs[k].at[half_a, :], "x"))
                jobs.append((k, b1.at[1 - y], b1.at[y], land[2 * k + 1], o_refs[k].at[half_b, :], "y"))
        sends = []
        for n, (k, send, _, landing, _, axis) in enumerate(jobs):
            cp = pltpu.make_async_remote_copy(src_ref=send, dst_ref=landing, send_sem=send_sems.at[n],
                                              recv_sem=recv_sems.at[n], device_id=peers[axis], device_id_type=MESH)
            cp.start()
            sends.append(cp)
        for cp, (k, _, kept, landing, out, _) in zip(sends, jobs):
            cp.wait_recv()
            _hbm_add(kept, landing, out, bufs[3 * k:3 * k + 3], in_sems, out_sems, chunk[k])
        for cp in sends:
            cp.wait_send()

    if first:
        out_shape = [jax.ShapeDtypeStruct((2, r, c), BF16) for (r, c) in shapes for _ in range(2)]
        land_shape = out_shape
    else:
        out_shape = [jax.ShapeDtypeStruct((2 * r, c), F32) for (r, c) in shapes]
        land_shape = [jax.ShapeDtypeStruct((r, c), BF16) for (r, c) in shapes for _ in range(2)]
    scratch = []
    for k in range(n_arr):
        scratch += [pltpu.VMEM((2, chunk[k], shapes[k][1]), BF16)] * 2 + [pltpu.VMEM((2, chunk[k], shapes[k][1]), out_dtype)]
    scratch += [pltpu.SemaphoreType.DMA((n_sem,)), pltpu.SemaphoreType.DMA((n_sem,)),
                pltpu.SemaphoreType.DMA((2, 2)), pltpu.SemaphoreType.DMA((2,))]
    outs = pl.pallas_call(
        body,
        name=name,
        in_specs=[ANY] * len(ins),
        out_specs=[ANY] * (len(out_shape) + len(land_shape)),
        out_shape=out_shape + land_shape,
        scratch_shapes=scratch,
    )(*ins)
    outs = outs[:len(out_shape)]
    return [(outs[2 * k], outs[2 * k + 1]) for k in range(n_arr)] if first else list(outs)


def _reduce_scatter(dw_al, blocks):
    xs = [dw_al] + [b.reshape(N_DEV // 2, 2, *b.shape[1:]) for b in blocks]
    ys = _halving_stage(xs, "c", "rs_c", BF16, windowed=(0,))
    pairs = _xy_stage([v.reshape(2, 2, *v.shape[1:]) for v in ys], True, "rs_xy1")
    return _xy_stage(pairs, False, "rs_xy2")


def _sum_slots(gs):
    n = len(gs)

    def body(*refs):
        for g_ref, o_ref in zip(refs[:n], refs[n:]):
            acc = g_ref[0]
            for d in range(1, N_DEV):
                acc = acc + g_ref[d]
            o_ref[...] = acc

    return pl.pallas_call(body, name="sum_slots",
                          out_shape=[jax.ShapeDtypeStruct(g.shape[1:], g.dtype) for g in gs])(*gs)


def _assemble_w_al(wins, bas):
    ba_tile = O_BA // LANE
    assert W_AL // LANE == ba_tile + 1
    cols = wins.shape[2]
    n_buf = 3
    ends = [WIN_START[d + 1] if d + 1 < N_DEV else ba_tile + 1 for d in range(N_DEV)]
    assert WIN_START[N_DEV - 1] + WIN_TILES == ba_tile + 1

    def body(w_ref, ba_ref, o_ref, buf, ld_sems, st_sems, ba_sem):
        def load(d):
            return pltpu.make_async_copy(w_ref.at[d], buf.at[d % n_buf], ld_sems.at[d % n_buf])

        def store(d):
            n = LANE * (ends[d] - WIN_START[d])
            return pltpu.make_async_copy(buf.at[d % n_buf, pl.ds(0, n), :],
                                         o_ref.at[pl.ds(LANE * WIN_START[d], n), :], st_sems.at[d % n_buf])

        load(0).start()
        for d in range(N_DEV):
            if d + 1 < N_DEV:
                if d + 1 >= n_buf:
                    store(d + 1 - n_buf).wait()
                load(d + 1).start()
            load(d).wait()
            if d > 0:
                ov = LANE * (WIN_START[d - 1] + WIN_TILES - WIN_START[d])
                buf[d % n_buf, :ov, :] = buf[d % n_buf, :ov, :] + buf[(d - 1) % n_buf, WIN_W - ov:, :]
            if d == N_DEV - 1:
                ba_copy = pltpu.make_async_copy(
                    ba_ref.at[BA_DEV], buf.at[d % n_buf, pl.ds(WIN_W - LANE, ba_ref.shape[1]), :], ba_sem)
                ba_copy.start()
                ba_copy.wait()
            store(d).start()
        for d in range(N_DEV - n_buf, N_DEV):
            store(d).wait()

    return pl.pallas_call(
        body,
        name="assemble_w_al",
        in_specs=[ANY, ANY],
        out_specs=ANY,
        out_shape=jax.ShapeDtypeStruct((W_AL, cols), wins.dtype),
        scratch_shapes=[pltpu.VMEM((n_buf, WIN_W, cols), wins.dtype), pltpu.SemaphoreType.DMA((n_buf,)),
                        pltpu.SemaphoreType.DMA((n_buf,)), pltpu.SemaphoreType.DMA],
    )(wins, bas)


def _adamw_math(w, g, m, v):
    m_new = ADAM_B1 * m + (1.0 - ADAM_B1) * g
    v_new = ADAM_B2 * v + (1.0 - ADAM_B2) * (g * g)
    m_hat = m_new / (1.0 - ADAM_B1 ** ADAM_STEP)
    v_hat = v_new / (1.0 - ADAM_B2 ** ADAM_STEP)
    return -ADAM_LR * (m_hat / (jnp.sqrt(v_hat) + ADAM_EPS) + ADAM_WD * w), m_new, v_new


def _adamw(w, g, m, v, name, tb=134):
    r, _, c = w.shape
    assert r % tb == 0

    def body(w_ref, g_ref, m_ref, v_ref, d_ref, nm_ref, nv_ref):
        d_ref[...], nm_ref[...], nv_ref[...] = _adamw_math(w_ref[...], g_ref[...], m_ref[...], v_ref[...])

    blk = pl.BlockSpec((tb, 1, c), lambda i: (i, 0, 0))
    o = jax.ShapeDtypeStruct(w.shape, F32)
    return pl.pallas_call(body, name=name, grid=(r // tb,), in_specs=[blk] * 4, out_specs=[blk] * 3,
                          out_shape=[o, o, o])(w, g, m, v)


def _adamw_many(ws, gs, ms, vs, name):
    n = len(ws)

    def body(*refs):
        for k in range(n):
            w_ref, g_ref, m_ref, v_ref = (refs[j * n + k] for j in range(4))
            d_ref, nm_ref, nv_ref = (refs[(4 + j) * n + k] for j in range(3))
            d_ref[...], nm_ref[...], nv_ref[...] = _adamw_math(w_ref[...], g_ref[...], m_ref[...], v_ref[...])

    shapes = [jax.ShapeDtypeStruct(w.shape, F32) for w in ws]
    outs = pl.pallas_call(body, name=name, out_shape=shapes * 3)(*ws, *gs, *ms, *vs)
    return outs[:n], outs[n:2 * n], outs[2 * n:]


def _select(me, table):
    return sum(jnp.where(me == d, jnp.int32(v), jnp.int32(0)) for d, v in enumerate(table))


WIN_SHIFT = tuple(SHARD_W * d - LANE * WIN_START[d] for d in range(N_DEV))
PAD_L = 256
PAD_R = 256


def _shard_to_window(shard_t, me):
    shift = _select(me, WIN_SHIFT)
    start = _select(me, WIN_START)
    padded = jnp.pad(shard_t, ((PAD_L, PAD_R), (0, 0)))
    cols = shard_t.shape[1]
    lo = lax.dynamic_slice(padded, (PAD_L - shift, 0), (WIN_W, cols))
    hi = lax.dynamic_slice(padded, (PAD_L - shift + N_BA, 0), (WIN_W, cols))
    aligned = LANE * start + lax.broadcasted_iota(jnp.int32, (WIN_W, 1), 0)
    return jnp.where(aligned >= ORIG_BA, hi, lo)


def _window_to_shard(win, ba_grad, me):
    shift = _select(me, WIN_SHIFT)
    cols = win.shape[1]
    padded = jnp.pad(win, ((N_BA, PAD_R), (0, 0)))
    lo = lax.dynamic_slice(padded, (N_BA + shift, 0), (SHARD_W, cols))
    hi = lax.dynamic_slice(padded, (shift, 0), (SHARD_W, cols))
    orig = SHARD_W * me + lax.broadcasted_iota(jnp.int32, (SHARD_W, 1), 0)
    ba_full = lax.dynamic_update_slice(jnp.zeros((SHARD_W, cols), win.dtype), ba_grad, (BA_LOCAL, 0))
    return jnp.where(orig < ORIG_BA, lo, jnp.where(orig >= ORIG_BA + N_BA, hi, ba_full))


def _pad_row(v, width=D_MODEL):
    v = v.reshape(1, -1)
    return jnp.pad(v, ((0, 0), (0, width - v.shape[1])))


def _slab(v, rows=8):
    return jnp.pad(v, ((0, rows - v.shape[0]), (0, D_MODEL - v.shape[1])))


def kernel(x, mem, norm_g, mem_norm_g, w_in, conv_w, a_log, dt_bias, dn_norm_g, w_mem_kv, w_br_dn, w_br_sb, w_br_mem, w_out, final_g, loss_target, m_norm_g, m_mem_norm_g, m_w_in, m_conv_w, m_a_log, m_dt_bias, m_dn_norm_g, m_w_mem_kv, m_w_br_dn, m_w_br_sb, m_w_br_mem, m_w_out, m_final_g, v_norm_g, v_mem_norm_g, v_w_in, v_conv_w, v_a_log, v_dt_bias, v_dn_norm_g, v_w_mem_kv, v_w_br_dn, v_w_br_sb, v_w_br_mem, v_w_out, v_final_g):
    xi, yi, ci = _position()
    me = 4 * xi + 2 * yi + ci

    shard_t = w_in[0].T
    win = _shard_to_window(shard_t, me).astype(BF16)
    ba = shard_t[BA_LOCAL:BA_LOCAL + N_BA, :].astype(BF16)
    g_win, g_ba, g_kv, g_dn, g_sb, g_out, g_mem, g_conv = _all_gather(
        [win, ba, w_mem_kv[0].astype(BF16), w_br_dn[0].astype(BF16), w_br_sb[0].astype(BF16), w_out[0].astype(BF16),
         w_br_mem[0].astype(BF16), conv_w[0]], "gather_weights")
    w_alt = _assemble_w_al(g_win, g_ba)
    w_mem_kv_f = g_kv.reshape(D_MODEL, 2 * MEM_W)
    w_br_dn_f = g_dn.reshape(D_MODEL, D_MODEL)
    w_br_sb_f = g_sb.reshape(D_MODEL, D_MODEL)
    w_out_f = g_out.reshape(D_MODEL, D_MODEL)
    w_br_mem_f = g_mem.transpose(1, 0, 2).reshape(MEM_W, D_MODEL)
    conv_w_f = g_conv.transpose(1, 0, 2).reshape(CONV_K, 3 * D_MODEL)

    r = _local_step(x[0], mem[0], loss_target[0], norm_g, mem_norm_g, w_alt, conv_w_f, _pad_row(a_log, LANE),
                    _pad_row(dt_bias, LANE), dn_norm_g, w_mem_kv_f, w_br_dn_f, w_br_sb_f, w_br_mem_f, w_out_f,
                    final_g.reshape(1, D_MODEL))

    dw_alt = r["w_alt"]
    rows_d = D_MODEL // N_DEV
    small_blocks = jnp.concatenate([
        r["w_br_dn"].reshape(N_DEV, rows_d, D_MODEL), r["w_br_sb"].reshape(N_DEV, rows_d, D_MODEL),
        r["w_out"].reshape(N_DEV, rows_d, D_MODEL), r["w_mem_kv"].reshape(N_DEV, rows_d // 2, D_MODEL),
        r["w_br_mem"].reshape(MEM_W, N_DEV, rows_d).transpose(1, 0, 2).reshape(N_DEV, MEM_W // N_DEV, D_MODEL)], axis=1)
    g_win, g_small = _reduce_scatter(dw_alt, [small_blocks])
    g_dn, g_sb, g_out = (g_small[k * rows_d:(k + 1) * rows_d] for k in range(3))
    g_kv = g_small[3 * rows_d:3 * rows_d + rows_d // 2].reshape(rows_d, 2 * MEM_W)
    g_mem = g_small[3 * rows_d + rows_d // 2:].reshape(MEM_W, rows_d)
    parts = [r["norm_g"], r["mem_norm_g"], r["final_g"], r["dn_norm_g"], r["scal"], r["loss"], r["conv_w"],
             dw_alt[O_BA:O_BA + N_BA, :].astype(F32)]
    s_norm_g, s_mem_norm_g, s_final_g, s_dn_norm_g, s_scal, s_loss, s_conv, s_ba = _sum_slots(
        _all_gather(parts, "gather_small"))
    loss = s_loss[0, 0]
    cw = conv_w.shape[2]
    g_conv = lax.dynamic_slice(s_conv, (0, cw * me), (CONV_K, cw))
    g_w_in_t = _window_to_shard(g_win, s_ba, me)
    grads = dict(norm_g=s_norm_g, mem_norm_g=s_mem_norm_g, w_in=g_w_in_t.T[None], conv_w=g_conv[None],
                 a_log=s_scal[0:1, :N_HEADS], dt_bias=s_scal[1:2, :N_HEADS], dn_norm_g=s_dn_norm_g, w_mem_kv=g_kv[None],
                 w_br_dn=g_dn[None], w_br_sb=g_sb[None], w_br_mem=g_mem[None], w_out=g_out[None],
                 final_g=s_final_g.reshape(D_MODEL))

    params = dict(norm_g=(norm_g, m_norm_g, v_norm_g), mem_norm_g=(mem_norm_g, m_mem_norm_g, v_mem_norm_g),
                  w_in=(w_in, m_w_in, v_w_in), conv_w=(conv_w, m_conv_w, v_conv_w), a_log=(a_log, m_a_log, v_a_log),
                  dt_bias=(dt_bias, m_dt_bias, v_dt_bias), dn_norm_g=(dn_norm_g, m_dn_norm_g, v_dn_norm_g),
                  w_mem_kv=(w_mem_kv, m_w_mem_kv, v_w_mem_kv), w_br_dn=(w_br_dn, m_w_br_dn, v_w_br_dn),
                  w_br_sb=(w_br_sb, m_w_br_sb, v_w_br_sb), w_br_mem=(w_br_mem, m_w_br_mem, v_w_br_mem),
                  w_out=(w_out, m_w_out, v_w_out), final_g=(final_g, m_final_g, v_final_g))
    order = list(params)
    deltas, new_m, new_v = {}, {}, {}
    deltas["w_in"], new_m["w_in"], new_v["w_in"] = (jnp.transpose(o, (1, 2, 0)) for o in _adamw(
        jnp.transpose(w_in, (2, 0, 1)), g_w_in_t[:, None, :], jnp.transpose(m_w_in, (2, 0, 1)),
        jnp.transpose(v_w_in, (2, 0, 1)), "adamw_w_in"))
    rest = [nm for nm in order if nm != "w_in"]

    def two_d(a):
        return a.reshape(1, -1) if a.ndim == 1 else a

    d_l, m_l, v_l = _adamw_many([two_d(params[nm][0]) for nm in rest], [two_d(grads[nm]) for nm in rest],
                                [two_d(params[nm][1]) for nm in rest], [two_d(params[nm][2]) for nm in rest], "adamw_rest")
    for k, nm in enumerate(rest):
        shp = params[nm][0].shape
        deltas[nm], new_m[nm], new_v[nm] = d_l[k].reshape(shp), m_l[k].reshape(shp), v_l[k].reshape(shp)
    return (loss, r["grad_x"][None], *[grads[nm] for nm in order], *[deltas[nm] for nm in order],
            *[new_m[nm] for nm in order], *[new_v[nm] for nm in order])
```

```python
import functools
import math

import jax
import jax.numpy as jnp
from jax import lax
from jax.experimental import pallas as pl
from jax.experimental.pallas import tpu as pltpu

F32 = jnp.float32
BF16 = jnp.bfloat16

D_MODEL = 1024
N_DEV = 8
N_HEADS = 8
D_HEAD = 128
DN_CHUNK = 64
CONV_K = 4
MEM_LEN = 256
MEM_HEADS = 4
MEM_DH = 64
MEM_W = MEM_HEADS * MEM_DH
NORM_EPS = 1e-6
IN_WIDTH = 11792
SHARD_W = IN_WIDTH // N_DEV

LANE = 128
SUPER = 2 * DN_CHUNK

O_QKV_DN = 0
O_Z_DN = 3072
O_QKV_SB = 4096
O_Z_SB = 7168
O_MQ = 8192
O_MZ = 8448
O_GATES = 8704
O_BA = 11776
W_AL = 11904
ORIG_BA = 4096
N_BA = 16

WIN_TILES = 13
WIN_W = WIN_TILES * LANE


def _aligned_col(o):
    return o if o < ORIG_BA else o - N_BA


WIN_START = tuple(min(_aligned_col(SHARD_W * d) // LANE, (W_AL // LANE) - WIN_TILES) for d in range(N_DEV))
WIN_OFF = tuple(_aligned_col(SHARD_W * d) - LANE * WIN_START[d] if SHARD_W * d >= ORIG_BA + N_BA or SHARD_W * d < ORIG_BA
                else None for d in range(N_DEV))
BA_DEV = ORIG_BA // SHARD_W
BA_LOCAL = ORIG_BA - BA_DEV * SHARD_W

ADAM_LR = 0.001
ADAM_B1 = 0.9
ADAM_B2 = 0.999
ADAM_EPS = 1e-08
ADAM_WD = 0.01
ADAM_STEP = 10

NN = (((1,), (0,)), ((), ()))
NT = (((1,), (1,)), ((), ()))
TN = (((0,), (0,)), ((), ()))


def _dot(a, b, dims):
    return lax.dot_general(a.astype(BF16), b.astype(BF16), dims, preferred_element_type=F32)


def _split2(a):
    hi = a.astype(BF16)
    lo = (a - hi.astype(F32)).astype(BF16)
    return hi, lo


def _dot3(a, b, dims):
    ah, al = _split2(a)
    bh, bl = _split2(b)
    d = functools.partial(lax.dot_general, dimension_numbers=dims, preferred_element_type=F32)
    return d(ah, bh) + (d(ah, bl) + d(al, bh))


def _sel_dot_impl(sel01, x, dims):
    sel = sel01.astype(BF16)
    h1 = x.astype(BF16)
    r1 = x - h1.astype(F32)
    h2 = r1.astype(BF16)
    h3 = (r1 - h2.astype(F32)).astype(BF16)
    d = functools.partial(lax.dot_general, dimension_numbers=dims, preferred_element_type=F32)
    return d(sel, h1) + (d(sel, h2) + d(sel, h3))


@jax.custom_vjp
def _sel_dot(sel01, x):
    return _sel_dot_impl(sel01, x, NN)


_sel_dot.defvjp(lambda s, x: (_sel_dot(s, x), s),
                lambda s, g: (jnp.zeros_like(s), _sel_dot_impl(s, g, TN)))


def _make_mm(dotfn):
    @jax.custom_vjp
    def nn(a, b):
        return dotfn(a, b, NN)

    @jax.custom_vjp
    def nt(a, b):
        return dotfn(a, b, NT)

    @jax.custom_vjp
    def tn(a, b):
        return dotfn(a, b, TN)

    nn.defvjp(lambda a, b: (nn(a, b), (a, b)), lambda r, g: (nt(g, r[1]), tn(r[0], g)))
    nt.defvjp(lambda a, b: (nt(a, b), (a, b)), lambda r, g: (nn(g, r[1]), tn(g, r[0])))
    tn.defvjp(lambda a, b: (tn(a, b), (a, b)), lambda r, g: (nt(r[1], g), nn(r[0], g)))
    return nn, nt, tn


mm_nn, mm_nt, mm_tn = _make_mm(_dot)
mm3_nn, mm3_nt, mm3_tn = _make_mm(_dot3)


def _sigmoid(x):
    return jax.nn.sigmoid(x)


def _silu(x):
    return x * _sigmoid(x)


def _softplus_parts(x):
    sp = jnp.log1p(jnp.exp(-jnp.abs(x)))
    return jnp.maximum(x, 0.0) + sp, jnp.maximum(-x, 0.0) + sp


def _rmsnorm(x, g):
    return x * lax.rsqrt(jnp.mean(x * x, axis=-1, keepdims=True) + NORM_EPS) * g


def _iota2(shape, dim):
    return lax.broadcasted_iota(jnp.int32, shape, dim)


def _div64(i):
    return lax.shift_right_logical(i, jnp.full(i.shape, 6, jnp.int32))


def _each(f, *lists):
    return [f(*a) for a in zip(*lists)]


@jax.custom_vjp
def _inv_unit_lower(ms):
    n = ms[0].shape[0]
    eye = (_iota2((n, n), 0) == _iota2((n, n), 1)).astype(F32)
    rs = [eye - m for m in ms]
    ps = ms
    for _ in range(5):
        ps = _each(mm3_nn, ps, ps)
        rs = _each(lambda r, p: r + mm_nn(r, p), rs, ps)
    return rs


def _inv_fwd(ms):
    rs = _inv_unit_lower(ms)
    return rs, rs


def _inv_bwd(rs, gs):
    ts = _each(mm_tn, rs, gs)
    return (_each(lambda t, r: -mm_nt(t, r), ts, rs),)


_inv_unit_lower.defvjp(_inv_fwd, _inv_bwd)


def _dn_block(cq, ck, cv, bcol, acol, zt, alog, dtb, gn, s0):
    n = SUPER
    h = DN_CHUNK
    row = _iota2((n, n), 0)
    col = _iota2((n, n), 1)
    same = _div64(row) == _div64(col)
    incl = jnp.logical_and(same, row >= col)
    strict = jnp.logical_and(same, row > col)
    incl_f = incl.astype(F32)

    qn = _each(lambda x: x * lax.rsqrt(jnp.sum(x * x, axis=-1, keepdims=True) + NORM_EPS) * (D_HEAD ** -0.5), cq)
    kn = _each(lambda x: x * lax.rsqrt(jnp.sum(x * x, axis=-1, keepdims=True) + NORM_EPS), ck)
    beta = _each(_sigmoid, bcol)
    g = _each(lambda al, ac, dt: -(jnp.exp(al) * _softplus_parts(ac + dt)[0]), alog, acol, dtb)
    gcum = _each(lambda x: _sel_dot(incl_f, jnp.broadcast_to(x, (n, n))), g)
    gam_incl = _each(lambda x: jnp.where(incl, jnp.exp(jnp.where(incl, x - x.T, 0.0)), 0.0), gcum)
    kk = _each(mm_nt, kn, kn)
    t_inv = _inv_unit_lower(_each(lambda b, x, gm: b * x * jnp.where(strict, gm, 0.0), beta, kk, gam_incl))
    eg = _each(jnp.exp, gcum)
    u = _each(lambda t, v, b: mm_nn(t, v * b), t_inv, cv, beta)
    w = _each(lambda t, k, b, e: mm_nn(t, k * (b * e)), t_inv, kn, beta, eg)
    a_intra = _each(lambda q, k, gm: mm_nt(q, k) * gm, qn, kn, gam_incl)
    q_dec = _each(lambda q, e: q * e, qn, eg)
    last0 = _each(lambda x: x[h - 1:h, :], gcum)
    last1 = _each(lambda x: x[n - 1:n, :], gcum)
    k_dec = _each(lambda k, x, l0, l1: k * jnp.exp(jnp.concatenate(
        [jnp.broadcast_to(l0, (h, n)), jnp.broadcast_to(l1, (h, n))], axis=0) - x), kn, gcum, last0, last1)
    v0 = _each(lambda uu, ww, s: uu[:h] - mm_nn(ww[:h], s), u, w, s0)
    o0 = _each(lambda q, s: mm_nn(q[:h], s), q_dec, s0)
    s1 = _each(lambda s, l0, k, v: s * jnp.exp(l0) + mm_tn(k[:h], v), s0, last0, k_dec, v0)
    v1 = _each(lambda uu, ww, s: uu[h:] - mm_nn(ww[h:], s), u, w, s1)
    o1 = _each(lambda q, s: mm_nn(q[h:], s), q_dec, s1)
    s2 = _each(lambda s, l1, k, v: s * jnp.exp(l1) + mm_tn(k[h:], v), s1, last1, k_dec, v1)
    o = _each(lambda a, b, am, x, y: jnp.concatenate([a, b], axis=0) + mm_nn(am, jnp.concatenate([x, y], axis=0)),
              o0, o1, a_intra, v0, v1)
    out = _each(lambda x, z: _rmsnorm(x, gn) * _silu(z), o, zt)
    return out, s2


def _mem_fn(mq, mz, mkv):
    mk = mkv[:, :MEM_W]
    mv = mkv[:, MEM_W:]
    lane = _iota2((1, MEM_W), 1)
    out = jnp.zeros(mq.shape, F32)
    for hd in range(MEM_HEADS):
        hm = (_div64(lane) == hd).astype(F32)
        s = mm_nt(mq * hm, mk) * (1.0 / math.sqrt(MEM_DH))
        s = s - jnp.max(s, axis=-1, keepdims=True)
        e = jnp.exp(s)
        p = e / jnp.sum(e, axis=-1, keepdims=True)
        out = out + mm_nn(p, mv) * hm
    return out * _silu(mz)


def _merge_fn(gd, gs, gm, yd, ys, ym):
    return _sigmoid(gd) * yd + _sigmoid(gs) * ys + _sigmoid(gm) * ym


def _loss_fn(x, mo, fg, tgt):
    y = _rmsnorm(x + mo, fg)
    err = y - tgt
    return 0.5 * jnp.sum(jnp.mean(err * err, axis=-1, keepdims=True), axis=0, keepdims=True)


def _matmul(a, b, mode, out_dtype, tm, tn, tk, name, b_col0=0, n_cols=None):
    if mode == "nn":
        m, kdim = a.shape
        n = b.shape[1] if n_cols is None else n_cols
    elif mode == "nt":
        m, kdim = a.shape
        n = b.shape[0]
    else:
        kdim, m = a.shape
        n = b.shape[1] if n_cols is None else n_cols
    tm, tn, tk = min(tm, m), min(tn, n), min(tk, kdim)
    assert m % tm == 0 and n % tn == 0 and kdim % tk == 0 and b_col0 % tn == 0
    nk = kdim // tk
    jb = b_col0 // tn
    dims = {"nn": NN, "nt": NT, "tn": TN}[mode]

    def body(a_ref, b_ref, o_ref, acc_ref):
        k = pl.program_id(2)
        part = _dot(a_ref[...], b_ref[...], dims)

        @pl.when(k == 0)
        def _():
            acc_ref[...] = part

        @pl.when(k > 0)
        def _():
            acc_ref[...] += part

        @pl.when(k == nk - 1)
        def _():
            o_ref[...] = acc_ref[...].astype(o_ref.dtype)

    if mode == "nn":
        a_spec = pl.BlockSpec((tm, tk), lambda i, j, k: (i, k))
        b_spec = pl.BlockSpec((tk, tn), lambda i, j, k: (k, j + jb))
    elif mode == "nt":
        a_spec = pl.BlockSpec((tm, tk), lambda i, j, k: (i, k))
        b_spec = pl.BlockSpec((tn, tk), lambda i, j, k: (j, k))
    else:
        a_spec = pl.BlockSpec((tk, tm), lambda i, j, k: (k, i))
        b_spec = pl.BlockSpec((tk, tn), lambda i, j, k: (k, j + jb))
    return pl.pallas_call(
        body,
        name=name,
        grid=(m // tm, n // tn, nk),
        in_specs=[a_spec, b_spec],
        out_specs=pl.BlockSpec((tm, tn), lambda i, j, k: (i, j)),
        out_shape=jax.ShapeDtypeStruct((m, n), out_dtype),
        scratch_shapes=[pltpu.VMEM((tm, tn), F32)],
        compiler_params=pltpu.CompilerParams(dimension_semantics=("parallel", "parallel", "arbitrary")),
    )(a, b)


def _norm_in(x, g, tm=256):
    t = x.shape[0]

    def body(x_ref, g_ref, h_ref):
        h_ref[...] = _rmsnorm(x_ref[...], g_ref[...]).astype(BF16)

    return pl.pallas_call(
        body,
        name="norm_in",
        grid=(t // tm,),
        in_specs=[pl.BlockSpec((tm, D_MODEL), lambda i: (i, 0)), pl.BlockSpec((1, D_MODEL), lambda i: (0, 0))],
        out_specs=pl.BlockSpec((tm, D_MODEL), lambda i: (i, 0)),
        out_shape=jax.ShapeDtypeStruct((t, D_MODEL), BF16),
    )(x, g)


def _norm_in_bwd(x, g, dh, dres, tm=256):
    t = x.shape[0]

    def body(x_ref, g_ref, dh_ref, dres_ref, dx_ref, dg_ref):
        _, vjp = jax.vjp(_rmsnorm, x_ref[...], g_ref[...])
        dx, dg = vjp(dh_ref[...])
        dx_ref[...] = dx + dres_ref[...]

        @pl.when(pl.program_id(0) == 0)
        def _():
            dg_ref[...] = jnp.zeros_like(dg_ref)

        dg_ref[...] += dg

    row = pl.BlockSpec((tm, D_MODEL), lambda i: (i, 0))
    vec = pl.BlockSpec((1, D_MODEL), lambda i: (0, 0))
    return pl.pallas_call(
        body,
        name="norm_in_bwd",
        grid=(t // tm,),
        in_specs=[row, vec, row, row],
        out_specs=[row, vec],
        out_shape=[jax.ShapeDtypeStruct((t, D_MODEL), F32), jax.ShapeDtypeStruct((1, D_MODEL), F32)],
    )(x, g, dh, dres)


def _merge(proj, yd, ys, ym, tm=256, tc=512):
    t = proj.shape[0]
    g0 = O_GATES // tc
    gstep = D_MODEL // tc

    def body(gd, gs, gm, yd_ref, ys_ref, ym_ref, o_ref):
        o_ref[...] = _merge_fn(gd[...], gs[...], gm[...], yd_ref[...], ys_ref[...], ym_ref[...]).astype(BF16)

    def gate(k):
        return pl.BlockSpec((tm, tc), lambda i, j: (i, g0 + k * gstep + j))

    blk = pl.BlockSpec((tm, tc), lambda i, j: (i, j))
    return pl.pallas_call(
        body,
        name="merge",
        grid=(t // tm, D_MODEL // tc),
        in_specs=[gate(0), gate(1), gate(2), blk, blk, blk],
        out_specs=blk,
        out_shape=jax.ShapeDtypeStruct((t, D_MODEL), BF16),
    )(proj, proj, proj, yd, ys, ym)


def _merge_bwd(proj, yd, ys, ym, dmerged, tm=256, tc=512):
    t = proj.shape[0]
    g0 = O_GATES // tc
    gstep = D_MODEL // tc

    def body(gd, gs, gm, yd_ref, ys_ref, ym_ref, dm_ref, dyd, dys, dym, dgd, dgs, dgm):
        _, vjp = jax.vjp(_merge_fn, gd[...], gs[...], gm[...], yd_ref[...], ys_ref[...], ym_ref[...])
        outs = vjp(dm_ref[...])
        for ref, val in zip((dgd, dgs, dgm, dyd, dys, dym), outs):
            ref[...] = val.astype(BF16)

    def gate(k):
        return pl.BlockSpec((tm, tc), lambda i, j: (i, g0 + k * gstep + j))

    blk = pl.BlockSpec((tm, tc), lambda i, j: (i, j))
    o = jax.ShapeDtypeStruct((t, D_MODEL), BF16)
    return pl.pallas_call(
        body,
        name="merge_bwd",
        grid=(t // tm, D_MODEL // tc),
        in_specs=[gate(0), gate(1), gate(2), blk, blk, blk, blk],
        out_specs=[blk] * 6,
        out_shape=[o] * 6,
    )(proj, proj, proj, yd, ys, ym, dmerged)


def _loss_head(x, mo, fg, tgt, tm=256):
    t = x.shape[0]

    def body(x_ref, mo_ref, fg_ref, t_ref, loss_ref, dout_ref, dfg_ref):
        loss, vjp = jax.vjp(_loss_fn, x_ref[...], mo_ref[...], fg_ref[...], t_ref[...])
        _, dmo, dfg, _ = vjp(jnp.ones((1, 1), F32))

        @pl.when(pl.program_id(0) == 0)
        def _():
            loss_ref[...] = jnp.zeros_like(loss_ref)
            dfg_ref[...] = jnp.zeros_like(dfg_ref)

        loss_ref[...] += jnp.broadcast_to(loss, loss_ref.shape)
        dfg_ref[...] += dfg
        dout_ref[...] = dmo

    row = pl.BlockSpec((tm, D_MODEL), lambda i: (i, 0))
    vec = pl.BlockSpec((1, D_MODEL), lambda i: (0, 0))
    return pl.pallas_call(
        body,
        name="loss_head",
        grid=(t // tm,),
        in_specs=[row, row, vec, row],
        out_specs=[pl.BlockSpec((1, LANE), lambda i: (0, 0)), row, vec],
        out_shape=[jax.ShapeDtypeStruct((1, LANE), F32), jax.ShapeDtypeStruct((t, D_MODEL), F32),
                   jax.ShapeDtypeStruct((1, D_MODEL), F32)],
    )(x, mo, fg, tgt)


def _block_tail(proj, o_dn, o_sb, o_m, x, tgt, w_br_dn, w_br_sb, w_br_mem, w_out, fg, tm=256):
    t = x.shape[0]
    tm = min(tm, t)
    gw = 512
    n_g = 3 * D_MODEL // gw

    def body(*refs):
        g_refs = refs[:n_g]
        (odn_ref, osb_ref, om_ref, x_ref, t_ref, wdn_ref, wsb_ref, wm_ref, wo_ref, fg_ref, loss_ref, dout_ref, dfg_ref,
         mg_ref, dyd_ref, dys_ref, dym_ref, dg_ref, dod_ref, dos_ref, dom_ref) = refs[n_g:]
        y = [_dot(odn_ref[...], wdn_ref[...], NN), _dot(osb_ref[...], wsb_ref[...], NN),
             _dot(om_ref[...], wm_ref[...], NN)]
        s = [_sigmoid(jnp.concatenate([g_refs[2 * k][...], g_refs[2 * k + 1][...]], axis=1)) for k in range(3)]
        merged16 = (s[0] * y[0] + s[1] * y[1] + s[2] * y[2]).astype(BF16)
        mg_ref[...] = merged16
        mo = _dot(merged16, wo_ref[...], NN)
        loss, vjp = jax.vjp(_loss_fn, x_ref[...], mo, fg_ref[...], t_ref[...])
        _, dout, dfg, _ = vjp(jnp.ones((1, 1), F32))

        @pl.when(pl.program_id(0) == 0)
        def _():
            loss_ref[...] = jnp.zeros_like(loss_ref)
            dfg_ref[...] = jnp.zeros_like(dfg_ref)

        loss_ref[...] += jnp.broadcast_to(loss, loss_ref.shape)
        dfg_ref[...] += dfg
        dout_ref[...] = dout
        dmerged = _dot(dout, wo_ref[...], NT)
        dy = [(sk * dmerged).astype(BF16) for sk in s]
        dyd_ref[...], dys_ref[...], dym_ref[...] = dy
        dg_ref[...] = jnp.concatenate([dmerged * yk * (sk * (1.0 - sk)) for yk, sk in zip(y, s)], axis=1).astype(BF16)
        dod_ref[...] = _dot(dy[0], wdn_ref[...], NT).astype(BF16)
        dos_ref[...] = _dot(dy[1], wsb_ref[...], NT).astype(BF16)
        dom_ref[...] = _dot(dy[2], wm_ref[...], NT).astype(BF16)

    gates = [pl.BlockSpec((tm, gw), lambda i, j=j: (i, O_GATES // gw + j)) for j in range(n_g)]
    row = pl.BlockSpec((tm, D_MODEL), lambda i: (i, 0))
    rowm = pl.BlockSpec((tm, MEM_W), lambda i: (i, 0))
    vec = pl.BlockSpec((1, D_MODEL), lambda i: (0, 0))

    def whole(a):
        return pl.BlockSpec(a.shape, lambda i: (0, 0), pipeline_mode=pl.Buffered(1))

    def bf(c):
        return jax.ShapeDtypeStruct((t, c), BF16)

    return pl.pallas_call(
        body,
        name="block_tail",
        grid=(t // tm,),
        in_specs=gates + [row, row, rowm, row, row, whole(w_br_dn), whole(w_br_sb), whole(w_br_mem), whole(w_out), vec],
        out_specs=[pl.BlockSpec((1, LANE), lambda i: (0, 0)), row, vec, row, row, row, row,
                   pl.BlockSpec((tm, 3 * D_MODEL), lambda i: (i, 0)), row, row, rowm],
        out_shape=[jax.ShapeDtypeStruct((1, LANE), F32), jax.ShapeDtypeStruct((t, D_MODEL), F32),
                   jax.ShapeDtypeStruct((1, D_MODEL), F32), bf(D_MODEL), bf(D_MODEL), bf(D_MODEL), bf(D_MODEL),
                   bf(3 * D_MODEL), bf(D_MODEL), bf(D_MODEL), bf(MEM_W)],
    )(*([proj] * n_g), o_dn, o_sb, o_m, x, tgt, w_br_dn, w_br_sb, w_br_mem, w_out, fg)


def _shift_rows(x, s):
    t = x.shape[0]
    if s == 0:
        return x
    rolled = pltpu.roll(x, s % t, 0)
    row = _iota2(x.shape, 0)
    keep = row >= s if s > 0 else row < t + s
    return jnp.where(keep, rolled, 0.0)


def _conv_pre(x, w):
    return sum(_shift_rows(x, CONV_K - 1 - j) * w[j:j + 1, :] for j in range(CONV_K))


CONV_TC = 256


def _dn_conv(proj, conv_w):
    t = proj.shape[0]
    nb = 3 * D_MODEL // CONV_TC

    def body(x_ref, w_ref, c_ref):
        c_ref[...] = _silu(_conv_pre(x_ref[...], w_ref[...]))

    return pl.pallas_call(
        body,
        name="dn_conv",
        grid=(nb,),
        in_specs=[pl.BlockSpec((t, CONV_TC), lambda j: (0, j)), pl.BlockSpec((CONV_K, CONV_TC), lambda j: (0, j))],
        out_specs=pl.BlockSpec((t, CONV_TC), lambda j: (0, j)),
        out_shape=jax.ShapeDtypeStruct((t, 3 * D_MODEL), F32),
    )(proj, conv_w)


def _dn_conv_bwd(proj, conv_w, dc, part):
    t = proj.shape[0]
    nb = D_MODEL // CONV_TC
    b0 = part * nb

    def body(x_ref, w_ref, dc_ref, dx_ref, dw_ref):
        x = x_ref[...]
        w = w_ref[...]
        pre = _conv_pre(x, w)
        sg = _sigmoid(pre)
        dpre = dc_ref[...] * (sg * (1.0 + pre * (1.0 - sg)))
        dx = sum(_shift_rows(dpre, -(CONV_K - 1 - j)) * w[j:j + 1, :] for j in range(CONV_K))
        dx_ref[...] = dx.astype(BF16)
        dw_ref[...] = jnp.concatenate(
            [jnp.sum(dpre * _shift_rows(x, CONV_K - 1 - j), axis=0, keepdims=True) for j in range(CONV_K)], axis=0)

    blk = pl.BlockSpec((t, CONV_TC), lambda j: (0, j))
    return pl.pallas_call(
        body,
        name=f"dn_conv_bwd{part}",
        grid=(nb,),
        in_specs=[pl.BlockSpec((t, CONV_TC), lambda j: (0, b0 + j)),
                  pl.BlockSpec((CONV_K, CONV_TC), lambda j: (0, b0 + j)), blk],
        out_specs=[blk, pl.BlockSpec((CONV_K, CONV_TC), lambda j: (0, j))],
        out_shape=[jax.ShapeDtypeStruct((t, D_MODEL), BF16), jax.ShapeDtypeStruct((CONV_K, D_MODEL), F32)],
    )(proj, conv_w, dc)


def _ba_columns(ba, hd):
    lane = _iota2(ba.shape, 1)
    bcol = jnp.sum(jnp.where(lane == hd, ba, 0.0), axis=1, keepdims=True)
    acol = jnp.sum(jnp.where(lane == N_HEADS + hd, ba, 0.0), axis=1, keepdims=True)
    return bcol, acol


def _head_scalar(row, hd):
    lane = _iota2(row.shape, 1)
    return jnp.sum(jnp.where(lane == hd, row, 0.0), axis=1, keepdims=True)


DN_HP = 8


def _dn_inputs(cq, ck, cv, ba_ref, z_ref, alog_ref, dtb_ref, heads, lanes):
    ba = ba_ref[...]
    cols = [_ba_columns(ba, hd) for hd in heads]
    return ([cq[:, ln] for ln in lanes], [ck[:, ln] for ln in lanes], [cv[:, ln] for ln in lanes],
            [c[0] for c in cols], [c[1] for c in cols], [z_ref[:, ln] for ln in lanes],
            [_head_scalar(alog_ref[...], hd) for hd in heads], [_head_scalar(dtb_ref[...], hd) for hd in heads])


def _dn_specs(nblk, reverse):
    w = DN_HP * LANE
    nq = D_MODEL // w

    def row(i):
        return nblk - 1 - i if reverse else i

    def colblk(b0):
        return pl.BlockSpec((SUPER, w), lambda i, h: (row(i), b0 + h))

    ba = pl.BlockSpec((SUPER, LANE), lambda i, h: (row(i), O_BA // LANE))
    vec = pl.BlockSpec((1, LANE), lambda i, h: (0, 0))
    st = pl.BlockSpec((1, DN_HP, D_HEAD, D_HEAD), lambda i, h: (row(i), h, 0, 0))
    return colblk, nq, ba, vec, st


def _dn_fwd(c, proj, alog_row, dtb_row, gn):
    t = c.shape[0]
    nblk = t // SUPER
    colblk, nq, ba, vec, st = _dn_specs(nblk, False)

    def body(cq, ck, cv, ba_ref, z_ref, alog_ref, dtb_ref, gn_ref, o_ref, s_ref, state):
        @pl.when(jnp.logical_and(pl.program_id(0) == 0, pl.program_id(1) == 0))
        def _():
            state[...] = jnp.zeros_like(state)

        heads = [pl.program_id(1) * DN_HP + j for j in range(DN_HP)]
        lanes = [slice(j * LANE, (j + 1) * LANE) for j in range(DN_HP)]
        s0 = [state[hd] for hd in heads]
        outs, s2 = _dn_block(*_dn_inputs(cq, ck, cv, ba_ref, z_ref, alog_ref, dtb_ref, heads, lanes), gn_ref[...], s0)
        for j, (hd, ln) in enumerate(zip(heads, lanes)):
            s_ref[0, j] = s0[j]
            o_ref[:, ln] = outs[j].astype(BF16)
            state[hd] = s2[j]

    return pl.pallas_call(
        body,
        name="dn_fwd",
        grid=(nblk, N_HEADS // DN_HP),
        in_specs=[colblk(0), colblk(nq), colblk(2 * nq), ba, colblk(O_Z_DN // (DN_HP * LANE)), vec, vec, vec],
        out_specs=[colblk(0), st],
        out_shape=[jax.ShapeDtypeStruct((t, D_MODEL), BF16),
                   jax.ShapeDtypeStruct((nblk, N_HEADS, D_HEAD, D_HEAD), F32)],
        scratch_shapes=[pltpu.VMEM((N_HEADS, D_HEAD, D_HEAD), F32)],
    )(c, c, c, proj, proj, alog_row, dtb_row, gn)


def _dn_bwd(c, proj, alog_row, dtb_row, gn, states, do):
    t = c.shape[0]
    nblk = t // SUPER
    colblk, nq, ba, vec, st = _dn_specs(nblk, True)

    def body(cq, ck, cv, ba_ref, z_ref, alog_ref, dtb_ref, gn_ref, s_ref, do_ref,
             dq_ref, dk_ref, dv_ref, dz_ref, dba_ref, dsc_ref, dgn_ref, dstate):
        i = pl.program_id(0)
        hq = pl.program_id(1)

        @pl.when(jnp.logical_and(i == 0, hq == 0))
        def _():
            dstate[...] = jnp.zeros_like(dstate)
            dsc_ref[...] = jnp.zeros_like(dsc_ref)
            dgn_ref[...] = jnp.zeros_like(dgn_ref)

        @pl.when(hq == 0)
        def _():
            dba_ref[...] = jnp.zeros_like(dba_ref)

        lane = _iota2((SUPER, LANE), 1)
        lane1 = _iota2((1, LANE), 1)
        heads = [hq * DN_HP + j for j in range(DN_HP)]
        lanes = [slice(j * LANE, (j + 1) * LANE) for j in range(DN_HP)]
        ds_in = [dstate[hd] for hd in heads]
        s_in = [s_ref[0, j] for j in range(DN_HP)]
        _, vjp = jax.vjp(_dn_block, *_dn_inputs(cq, ck, cv, ba_ref, z_ref, alog_ref, dtb_ref, heads, lanes),
                         gn_ref[...], s_in)
        dq, dk, dv, dbc, dac, dz, dal, ddt, dgn, ds0 = vjp(([do_ref[:, ln].astype(F32) for ln in lanes], ds_in))
        dba = jnp.zeros((SUPER, LANE), F32)
        dal_row = jnp.zeros((1, LANE), F32)
        ddt_row = jnp.zeros((1, LANE), F32)
        for j, (hd, ln) in enumerate(zip(heads, lanes)):
            dq_ref[:, ln] = dq[j]
            dk_ref[:, ln] = dk[j]
            dv_ref[:, ln] = dv[j]
            dz_ref[:, ln] = dz[j].astype(BF16)
            dstate[hd] = ds0[j]
            dba = dba + jnp.where(lane == hd, dbc[j], 0.0) + jnp.where(lane == N_HEADS + hd, dac[j], 0.0)
            dal_row = dal_row + jnp.where(lane1 == hd, dal[j], 0.0)
            ddt_row = ddt_row + jnp.where(lane1 == hd, ddt[j], 0.0)
        dba_ref[...] += dba
        dsc_ref[0:1, :] += dal_row
        dsc_ref[1:2, :] += ddt_row
        dgn_ref[...] += dgn

    outs = pl.pallas_call(
        body,
        name="dn_bwd",
        grid=(nblk, N_HEADS // DN_HP),
        in_specs=[colblk(0), colblk(nq), colblk(2 * nq), ba, colblk(O_Z_DN // (DN_HP * LANE)), vec, vec, vec, st,
                  colblk(0)],
        out_specs=[colblk(0), colblk(0), colblk(0), colblk(0),
                   pl.BlockSpec((SUPER, LANE), lambda i, h: (nblk - 1 - i, 0)),
                   pl.BlockSpec((2, LANE), lambda i, h: (0, 0)), vec],
        out_shape=[jax.ShapeDtypeStruct((t, D_MODEL), F32)] * 3
        + [jax.ShapeDtypeStruct((t, D_MODEL), BF16), jax.ShapeDtypeStruct((t, LANE), F32),
           jax.ShapeDtypeStruct((2, LANE), F32), jax.ShapeDtypeStruct((1, LANE), F32)],
        scratch_shapes=[pltpu.VMEM((N_HEADS, D_HEAD, D_HEAD), F32)],
    )(c, c, c, proj, proj, alog_row, dtb_row, gn, states, do)
    return outs


SB_TQ = 256
SB_TK = 256
SB_HP_FWD = 8
SB_HP_BWD = 4


def _sb_logits(z, mask):
    sp = jnp.log(1.0 + jnp.exp(-jnp.abs(z)))
    lf_raw = -(jnp.maximum(z, 0.0) + sp)
    lb = lf_raw + z
    lf = lf_raw if mask is None else jnp.where(mask, lf_raw, 0.0)
    return lb, lf_raw, lf


def _suffix_sums(x, sel):
    hi, lo = _split2(x)
    d = functools.partial(lax.dot_general, dimension_numbers=NN, preferred_element_type=F32)
    return d(hi, sel) + d(lo, sel)


def _sb_diag_mask(tq, r):
    return r * SB_TK + _iota2((tq, SB_TK), 1) < _iota2((tq, SB_TK), 0)


def _sb_specs(t, tq, hp):
    w = hp * LANE
    q0, k0, v0, z0 = (O_QKV_SB // w, (O_QKV_SB + D_MODEL) // w, (O_QKV_SB + 2 * D_MODEL) // w, O_Z_SB // w)

    def blk(b0):
        return pl.BlockSpec((tq, w), lambda h, i: (i, b0 + h))

    def full(b0, **kw):
        return pl.BlockSpec((t, w), lambda h, i: (0, b0 + h), **kw)

    once = dict(pipeline_mode=pl.Buffered(1))
    return blk(q0), full(k0, **once), full(v0, **once), blk(z0), blk(0), full(0)


def _sb_fwd(proj):
    t = proj.shape[0]
    tq = min(SB_TQ, t)
    ndiag = tq // SB_TK
    scale = 1.0 / math.sqrt(D_HEAD)

    def body(q_ref, k_ref, v_ref, z_ref, o_ref, oraw_ref):
        qi = pl.program_id(1)
        lanes = [slice(hd * LANE, (hd + 1) * LANE) for hd in range(SB_HP_FWD)]
        qs = [(q_ref[:, ln] * scale).astype(BF16) for ln in lanes]
        after = (_iota2((SB_TK, SB_TK), 0) > _iota2((SB_TK, SB_TK), 1)).astype(BF16)
        oraw_ref[...] = jnp.zeros_like(oraw_ref)

        def block(kb, mask, c_lf):
            rows = pl.ds(pl.multiple_of(kb * SB_TK, SB_TK), SB_TK)
            z = _each(lambda q, ln: _dot(q, k_ref[rows, ln], NT), qs, lanes)
            lg = _each(lambda x: _sb_logits(x, mask), z)
            surv = _each(lambda x: _suffix_sums(x[2], after), lg)
            att = _each(lambda x, s, c: jnp.exp(x[0] + s + c), lg, surv, c_lf)
            if mask is not None:
                att = _each(lambda a: jnp.where(mask, a, 0.0), att)
            pv = _each(lambda a, ln: _dot(a, v_ref[rows, ln], NN), att, lanes)
            for p, ln in zip(pv, lanes):
                oraw_ref[:, ln] += p
            return tuple(_each(lambda c, x: c + jnp.sum(x[2], axis=1, keepdims=True), c_lf, lg))

        carry = tuple(jnp.zeros((tq, 1), F32) for _ in range(SB_HP_FWD))
        for r in reversed(range(ndiag)):
            carry = block(qi * ndiag + r, _sb_diag_mask(tq, r), carry)
        lax.fori_loop(0, qi * ndiag, lambda i, c: block(qi * ndiag - 1 - i, None, c), carry)
        o_ref[...] = (oraw_ref[...] * _silu(z_ref[...])).astype(BF16)

    q_spec, k_spec, v_spec, z_spec, out, _ = _sb_specs(t, tq, SB_HP_FWD)
    return pl.pallas_call(
        body,
        name="sb_fwd",
        grid=(N_HEADS // SB_HP_FWD, t // tq),
        in_specs=[q_spec, k_spec, v_spec, z_spec],
        out_specs=[out, out],
        out_shape=[jax.ShapeDtypeStruct((t, D_MODEL), BF16), jax.ShapeDtypeStruct((t, D_MODEL), F32)],
    )(proj, proj, proj, proj)


def _sb_bwd(proj, oraw, do):
    t = proj.shape[0]
    tq = min(SB_TQ, t)
    ndiag = tq // SB_TK
    scale = 1.0 / math.sqrt(D_HEAD)

    def body(q_ref, k_ref, v_ref, z_ref, oraw_ref, do_ref, dq_ref, dk_ref, dv_ref, dz_ref, dk_acc, dv_acc,
             p_scr, z_scr):
        qi = pl.program_id(1)
        nq = pl.num_programs(1)

        @pl.when(qi == 0)
        def _():
            dk_acc[...] = jnp.zeros_like(dk_acc)
            dv_acc[...] = jnp.zeros_like(dv_acc)

        heads = range(SB_HP_BWD)
        lanes = [slice(hd * LANE, (hd + 1) * LANE) for hd in heads]
        zg = z_ref[...]
        sg = _sigmoid(zg)
        dog = do_ref[...].astype(F32)
        dz_ref[...] = (dog * oraw_ref[...] * (sg * (1.0 + zg * (1.0 - sg)))).astype(BF16)
        d_o = (dog * (zg * sg)).astype(BF16)
        d_o16 = [d_o[:, ln] for ln in lanes]
        qs = [(q_ref[:, ln] * scale).astype(BF16) for ln in lanes]
        ri = _iota2((SB_TK, SB_TK), 0)
        ci = _iota2((SB_TK, SB_TK), 1)
        after = (ri > ci).astype(BF16)
        earlier = (ri < ci).astype(BF16)

        def rows_of(kb):
            return pl.ds(pl.multiple_of(kb * SB_TK, SB_TK), SB_TK)

        def down(kb, mask, c_lf):
            rows = rows_of(kb)
            z = _each(lambda q, ln: _dot(q, k_ref[rows, ln], NT), qs, lanes)
            da = _each(lambda d, ln: _dot(d, v_ref[rows, ln], NT), d_o16, lanes)
            lg = _each(lambda x: _sb_logits(x, mask), z)
            surv = _each(lambda x: _suffix_sums(x[2], after), lg)
            att = _each(lambda x, s, c: jnp.exp(x[0] + s + c), lg, surv, c_lf)
            if mask is not None:
                att = _each(lambda a: jnp.where(mask, a, 0.0), att)
            dv = _each(lambda a, d: _dot(a, d, TN), att, d_o16)
            for hd in heads:
                p_scr[hd, kb] = att[hd] * da[hd]
                z_scr[hd, kb] = z[hd]
                dv_acc[rows, lanes[hd]] += dv[hd]
            return tuple(_each(lambda c, x: c + jnp.sum(x[2], axis=1, keepdims=True), c_lf, lg))

        c_lf = tuple(jnp.zeros((tq, 1), F32) for _ in heads)
        for r in reversed(range(ndiag)):
            c_lf = down(qi * ndiag + r, _sb_diag_mask(tq, r), c_lf)
        lax.fori_loop(0, qi * ndiag, lambda i, c: down(qi * ndiag - 1 - i, None, c), c_lf)

        def up(kb, mask, carry):
            dq, c_p = carry
            rows = rows_of(kb)
            p = [p_scr[hd, kb] for hd in heads]
            zs = [z_scr[hd, kb] for hd in heads]
            before = _each(lambda x, c: _suffix_sums(x, earlier) + c, p, c_p)
            e = _each(lambda x: jnp.exp(-jnp.abs(x)), zs)
            r = _each(lambda x: 1.0 / (1.0 + x), e)
            sig = _each(lambda x, a, b: jnp.where(x >= 0.0, b, a * b), zs, e, r)
            oms = _each(lambda x, a, b: jnp.where(x >= 0.0, a * b, b), zs, e, r)
            if mask is not None:
                sig = _each(lambda a: jnp.where(mask, a, 0.0), sig)
            dzz = _each(lambda x, o, g, b: x * o - g * b, p, oms, sig, before)
            dk = _each(lambda x, q: _dot(x, q, TN), dzz, qs)
            dq = _each(lambda a, x, ln: a + _dot(x, k_ref[rows, ln], NN), dq, dzz, lanes)
            for hd in heads:
                dk_acc[rows, lanes[hd]] += dk[hd]
            return tuple(dq), tuple(_each(lambda c, x: c + jnp.sum(x, axis=1, keepdims=True), c_p, p))

        carry = (tuple(jnp.zeros((tq, D_HEAD), F32) for _ in heads), tuple(jnp.zeros((tq, 1), F32) for _ in heads))
        carry = lax.fori_loop(0, qi * ndiag, lambda kb, c: up(kb, None, c), carry)
        for r in range(ndiag):
            carry = up(qi * ndiag + r, _sb_diag_mask(tq, r), carry)
        dq = carry[0]
        for hd in heads:
            dq_ref[:, lanes[hd]] = (dq[hd] * scale).astype(BF16)

        @pl.when(qi == nq - 1)
        def _():
            dk_ref[...] = dk_acc[...].astype(BF16)
            dv_ref[...] = dv_acc[...].astype(BF16)

    q_spec, k_spec, v_spec, z_spec, blk, full = _sb_specs(t, tq, SB_HP_BWD)
    o = jax.ShapeDtypeStruct((t, D_MODEL), BF16)
    w = SB_HP_BWD * LANE
    return pl.pallas_call(
        body,
        name="sb_bwd",
        grid=(N_HEADS // SB_HP_BWD, t // tq),
        in_specs=[q_spec, k_spec, v_spec, z_spec, blk, blk],
        out_specs=[blk, full, full, blk],
        out_shape=[o, o, o, o],
        scratch_shapes=[pltpu.VMEM((t, w), F32), pltpu.VMEM((t, w), F32)]
        + [pltpu.VMEM((SB_HP_BWD, t // SB_TK, tq, SB_TK), F32)] * 2,
    )(proj, proj, proj, proj, oraw, do)


def _mem_kv_fn(mem, mg, w):
    return mm_nn(_rmsnorm(mem, mg), w)


def _mem_kv(mem, mg, w):
    def body(m_ref, g_ref, w_ref, o_ref):
        o_ref[...] = _mem_kv_fn(m_ref[...], g_ref[...], w_ref[...])

    return pl.pallas_call(body, name="mem_kv", out_shape=jax.ShapeDtypeStruct((MEM_LEN, 2 * MEM_W), F32))(mem, mg, w)


def _mem_kv_bwd(mem, mg, w, dmkv):
    def body(m_ref, g_ref, w_ref, d_ref, dg_ref, dw_ref):
        _, vjp = jax.vjp(_mem_kv_fn, m_ref[...], g_ref[...], w_ref[...].astype(F32))
        _, dg, dw = vjp(d_ref[...])
        dg_ref[...] = dg
        dw_ref[...] = dw.astype(BF16)

    return pl.pallas_call(
        body, name="mem_kv_bwd",
        out_shape=[jax.ShapeDtypeStruct((1, D_MODEL), F32), jax.ShapeDtypeStruct((D_MODEL, 2 * MEM_W), BF16)],
    )(mem, mg, w, dmkv)


def _mem_attn(proj, mkv, tm=256):
    t = proj.shape[0]
    tm = min(tm, t)

    def body(q_ref, z_ref, kv_ref, o_ref):
        o_ref[...] = _mem_fn(q_ref[...], z_ref[...], kv_ref[...]).astype(BF16)

    return pl.pallas_call(
        body,
        name="mem_attn",
        grid=(t // tm,),
        in_specs=[pl.BlockSpec((tm, MEM_W), lambda i: (i, O_MQ // MEM_W)),
                  pl.BlockSpec((tm, MEM_W), lambda i: (i, O_MZ // MEM_W)),
                  pl.BlockSpec((MEM_LEN, 2 * MEM_W), lambda i: (0, 0))],
        out_specs=pl.BlockSpec((tm, MEM_W), lambda i: (i, 0)),
        out_shape=jax.ShapeDtypeStruct((t, MEM_W), BF16),
    )(proj, proj, mkv)


def _mem_attn_bwd(proj, mkv, do, tm=256):
    t = proj.shape[0]
    tm = min(tm, t)

    def body(q_ref, z_ref, kv_ref, do_ref, dq_ref, dz_ref, dkv_ref):
        _, vjp = jax.vjp(_mem_fn, q_ref[...], z_ref[...], kv_ref[...])
        dq, dz, dkv = vjp(do_ref[...].astype(F32))
        dq_ref[...] = dq.astype(BF16)
        dz_ref[...] = dz.astype(BF16)

        @pl.when(pl.program_id(0) == 0)
        def _():
            dkv_ref[...] = jnp.zeros_like(dkv_ref)

        dkv_ref[...] += dkv

    blk = pl.BlockSpec((tm, MEM_W), lambda i: (i, 0))
    kv = pl.BlockSpec((MEM_LEN, 2 * MEM_W), lambda i: (0, 0))
    return pl.pallas_call(
        body,
        name="mem_attn_bwd",
        grid=(t // tm,),
        in_specs=[pl.BlockSpec((tm, MEM_W), lambda i: (i, O_MQ // MEM_W)),
                  pl.BlockSpec((tm, MEM_W), lambda i: (i, O_MZ // MEM_W)), kv, blk],
        out_specs=[blk, blk, kv],
        out_shape=[jax.ShapeDtypeStruct((t, MEM_W), BF16), jax.ShapeDtypeStruct((t, MEM_W), BF16),
                   jax.ShapeDtypeStruct((MEM_LEN, 2 * MEM_W), F32)],
    )(proj, proj, mkv, do)


def _local_step(x, mem, tgt, norm_g, mem_norm_g, w_alt, conv_w, alog_row, dtb_row, dn_norm_g, w_mem_kv, w_br_dn, w_br_sb,
                w_br_mem, w_out, final_g):
    h = _norm_in(x, norm_g)
    proj = _matmul(h, w_alt, "nt", F32, 2048, 384, 1024, "proj")

    c = _dn_conv(proj, conv_w)
    o_dn, states = _dn_fwd(c, proj, alog_row, dtb_row, dn_norm_g)
    o_sb, o_sb_raw = _sb_fwd(proj)
    mkv = _mem_kv(mem, mem_norm_g, w_mem_kv)
    o_m = _mem_attn(proj, mkv)

    (loss, dout, d_final_g, merged, dy_dn, dy_sb, dy_m, dgates, do_dn, do_sb, do_m) = _block_tail(
        proj, o_dn, o_sb, o_m, x, tgt, w_br_dn, w_br_sb, w_br_mem, w_out, final_g)
    dw_out = _matmul(merged, dout, "tn", BF16, 256, 1024, 2048, "dw_out")
    dw_br_dn = _matmul(o_dn, dy_dn, "tn", BF16, 256, 1024, 2048, "dw_br_dn")
    dw_br_sb = _matmul(o_sb, dy_sb, "tn", BF16, 256, 1024, 2048, "dw_br_sb")
    dw_br_mem = _matmul(o_m, dy_m, "tn", BF16, 256, 1024, 2048, "dw_br_mem")

    dmq, dmz, dmkv = _mem_attn_bwd(proj, mkv, do_m)
    d_mem_norm_g, dw_mem_kv = _mem_kv_bwd(mem, mem_norm_g, w_mem_kv, dmkv)
    dq_sb, dk_sb, dv_sb, dz_sb = _sb_bwd(proj, o_sb_raw, do_sb)
    dcq, dck, dcv, dz_dn, dba, dscal, d_dn_norm_g = _dn_bwd(c, proj, alog_row, dtb_row, dn_norm_g, states, do_dn)
    dq_dn, dcw_q = _dn_conv_bwd(proj, conv_w, dcq, 0)
    dk_dn, dcw_k = _dn_conv_bwd(proj, conv_w, dck, 1)
    dv_dn, dcw_v = _dn_conv_bwd(proj, conv_w, dcv, 2)
    d_conv_w = jnp.concatenate([dcw_q, dcw_k, dcw_v], axis=1)

    dproj = jnp.concatenate([dq_dn, dk_dn, dv_dn, dz_dn, dq_sb, dk_sb, dv_sb, dz_sb, dmq, dmz, dgates,
                             dba.astype(BF16)], axis=1)
    dh = _matmul(dproj, w_alt, "nn", F32, 512, 1024, 3968, "dh")
    dw_alt = _matmul(dproj, h, "tn", BF16, 384, 1024, 2048, "dw_alt")
    grad_x, d_norm_g = _norm_in_bwd(x, norm_g, dh, dout)
    return dict(loss=loss, grad_x=grad_x, norm_g=d_norm_g, mem_norm_g=d_mem_norm_g, w_alt=dw_alt, conv_w=d_conv_w,
                scal=dscal, dn_norm_g=d_dn_norm_g, w_mem_kv=dw_mem_kv, w_br_dn=dw_br_dn, w_br_sb=dw_br_sb,
                w_br_mem=dw_br_mem, w_out=dw_out, final_g=d_final_g)


MESH = pl.DeviceIdType.MESH
ANY = pl.BlockSpec(memory_space=pl.ANY)


def _position():
    return lax.axis_index("x"), lax.axis_index("y"), lax.axis_index("c")


def _all_gather(xs, name):
    n = len(xs)

    def body(*refs):
        x_refs, o_refs = refs[:n], refs[n:2 * n]
        send_sems, recv_sems, local_sems = refs[2 * n:]
        x, y, c = _position()
        me, sibling = (x, y, c), (x, y, 1 - c)
        x_nbr, y_nbr, diag = (1 - x, y, c), (x, 1 - y, c), (1 - x, 1 - y, c)
        south = c == 0
        relay_from = tuple(jnp.where(south, a, b) for a, b in zip(y_nbr, x_nbr))
        relay_to = tuple(jnp.where(south, a, b) for a, b in zip(x_nbr, y_nbr))

        def slot(p):
            return 4 * p[0] + 2 * p[1] + p[2]

        def copy(a, k, block, to, src=None):
            dst = o_refs[a].at[slot(block)]
            return pltpu.make_async_remote_copy(
                src_ref=dst if src is None else src, dst_ref=dst, send_sem=send_sems.at[7 * a + k],
                recv_sem=recv_sems.at[7 * a + k], device_id=to, device_id_type=MESH)

        mine = [pltpu.make_async_copy(x_refs[a], o_refs[a].at[slot(me)], local_sems.at[a]) for a in range(n)]
        for cp in mine:
            cp.start()
        sends = []
        for a in range(n):
            sends += [copy(a, 0, me, sibling, src=x_refs[a]), copy(a, 1, me, x_nbr, src=x_refs[a]),
                      copy(a, 2, me, y_nbr, src=x_refs[a])]
        for cp in sends:
            cp.start()
        later = []
        for a in range(n):
            copy(a, 1, x_nbr, me).wait_recv()
            copy(a, 2, y_nbr, me).wait_recv()
            later += [copy(a, 3, relay_from, relay_to), copy(a, 4, x_nbr, sibling), copy(a, 5, y_nbr, sibling)]
            for cp in later[-3:]:
                cp.start()
        for a in range(n):
            copy(a, 3, diag, me).wait_recv()
            later.append(copy(a, 6, diag, sibling))
            later[-1].start()
        for a in range(n):
            copy(a, 0, sibling, me).wait_recv()
            for k, chip in ((4, x_nbr), (5, y_nbr), (6, diag)):
                copy(a, k, (chip[0], chip[1], 1 - c), me).wait_recv()
        for cp in sends + later:
            cp.wait_send()
        for cp in mine:
            cp.wait()

    return pl.pallas_call(
        body,
        name=name,
        in_specs=[ANY] * n,
        out_specs=[ANY] * n,
        out_shape=[jax.ShapeDtypeStruct((N_DEV, *v.shape), v.dtype) for v in xs],
        scratch_shapes=[pltpu.SemaphoreType.DMA((7 * n,)), pltpu.SemaphoreType.DMA((7 * n,)),
                        pltpu.SemaphoreType.DMA((n,))],
    )(*xs)


def _window_view(ref, dest):
    return ref.at[pl.ds(LANE * WIN_START[dest], WIN_W), :]


def _chunk_rows(rows, cols):
    return max(ch for ch in range(16, rows + 1, 16) if rows % ch == 0 and ch * cols <= (1 << 19))


def _halving_stage(xs, axis, name, out_dtype, windowed=()):
    n_arr = len(xs)
    metas = []
    for k, v in enumerate(xs):
        if k in windowed:
            metas.append((N_DEV // 2, WIN_W, v.shape[1]))
        else:
            assert v.shape[1] == 2
            metas.append((v.shape[0], v.shape[2], v.shape[3]))
    chunk = [_chunk_rows(r, c) for (_, r, c) in metas]
    offs = [sum(m[0] for m in metas[:k]) for k in range(n_arr)]
    n_sem = sum(m[0] for m in metas)

    def body(*refs):
        x_refs = refs[:n_arr]
        o_refs = refs[n_arr:2 * n_arr]
        land_refs = refs[2 * n_arr:3 * n_arr]
        rest = refs[3 * n_arr:]
        bufs = rest[:3 * n_arr]
        send_sems, recv_sems, in_sems, out_sems = rest[3 * n_arr:]
        pos = dict(zip("xyc", _position()))
        bit = pos[axis]
        peer = tuple(1 - pos[a] if a == axis else pos[a] for a in "xyc")

        def view(k, i, b):
            if k in windowed:
                return _window_view(x_refs[k], 2 * i + b)
            return x_refs[k].at[i, b]

        def add_blocks(k, a_view, b_view, o_view):
            _hbm_add(a_view, b_view, o_view, bufs[3 * k:3 * k + 3], in_sems, out_sems, chunk[k])

        for b in (0, 1):
            @pl.when(bit == b)
            def _(b=b):
                sends = []
                for k in range(n_arr):
                    for i in range(metas[k][0]):
                        cp = pltpu.make_async_remote_copy(
                            src_ref=view(k, i, 1 - b), dst_ref=land_refs[k].at[i], send_sem=send_sems.at[offs[k] + i],
                            recv_sem=recv_sems.at[offs[k] + i], device_id=peer, device_id_type=MESH)
                        cp.start()
                        sends.append(cp)
                idx = 0
                for k in range(n_arr):
                    for i in range(metas[k][0]):
                        sends[idx].wait_recv()
                        add_blocks(k, view(k, i, b), land_refs[k].at[i], o_refs[k].at[i])
                        idx += 1
                for cp in sends:
                    cp.wait_send()

    out_shape = [jax.ShapeDtypeStruct(m, out_dtype) for m in metas]
    land_shape = [jax.ShapeDtypeStruct(m, v.dtype) for m, v in zip(metas, xs)]
    scratch = []
    for k in range(n_arr):
        blk = (2, chunk[k], metas[k][2])
        scratch += [pltpu.VMEM(blk, xs[k].dtype)] * 2 + [pltpu.VMEM(blk, out_dtype)]
    scratch += [pltpu.SemaphoreType.DMA((n_sem,)), pltpu.SemaphoreType.DMA((n_sem,)),
                pltpu.SemaphoreType.DMA((2, 2)), pltpu.SemaphoreType.DMA((2,))]
    outs = pl.pallas_call(
        body,
        name=name,
        in_specs=[ANY] * n_arr,
        out_specs=[ANY] * (2 * n_arr),
        out_shape=out_shape + land_shape,
        scratch_shapes=scratch,
    )(*xs)
    return outs[:n_arr]


def _hbm_add(a_view, b_view, o_view, bufs, in_sems, out_sems, ch):
    rows = a_view.shape[0]
    nch = rows // ch
    va, vb, vo = bufs

    def rows_of(j):
        return pl.ds(pl.multiple_of(j * ch, 16), ch)

    def loads(j, s):
        return (pltpu.make_async_copy(a_view.at[rows_of(j), :], va.at[s], in_sems.at[0, s]),
                pltpu.make_async_copy(b_view.at[rows_of(j), :], vb.at[s], in_sems.at[1, s]))

    def store(j, s):
        return pltpu.make_async_copy(vo.at[s], o_view.at[rows_of(j), :], out_sems.at[s])

    for cp in loads(0, 0):
        cp.start()

    def step(j, _):
        s = lax.rem(j, 2)

        @pl.when(j + 1 < nch)
        def _():
            for cp in loads(j + 1, 1 - s):
                cp.start()

        for cp in loads(j, s):
            cp.wait()

        @pl.when(j >= 2)
        def _():
            store(j - 2, s).wait()

        vo[s] = (va[s].astype(F32) + vb[s].astype(F32)).astype(vo.dtype)
        store(j, s).start()
        return 0

    lax.fori_loop(0, nch, step, 0)
    for j in range(max(0, nch - 2), nch):
        store(j, j % 2).wait()


def _xy_stage(xs, first, name):
    n_arr = len(xs)
    if first:
        shapes = [(v.shape[2] // 2, v.shape[3]) for v in xs]
        ins = list(xs)
    else:
        shapes = [(a.shape[1], a.shape[2]) for a, _ in xs]
        ins = [v for pair in xs for v in pair]
    n_blk = 2 if first else 1
    out_dtype = BF16 if first else F32
    chunk = [_chunk_rows(r, c) for (r, c) in shapes]
    n_sem = 2 * n_blk * n_arr

    def body(*refs):
        n_in = len(ins)
        in_refs = refs[:n_in]
        n_out = 2 * n_arr if first else n_arr
        o_refs = refs[n_in:n_in + n_out]
        land = refs[n_in + n_out:n_in + n_out + 2 * n_arr]
        rest = refs[n_in + n_out + 2 * n_arr:]
        bufs = rest[:3 * n_arr]
        send_sems, recv_sems, in_sems, out_sems = rest[3 * n_arr:]
        x, y, c = _position()
        peers = {"x": (1 - x, y, c), "y": (x, 1 - y, c)}
        jobs = []
        for k in range(n_arr):
            r, _ = shapes[k]
            half_a, half_b = pl.ds(0, r), pl.ds(r, r)
            if first:
                src = in_refs[k]
                for i in range(2):
                    jobs.append((k, src.at[i, 1 - y, half_a, :], src.at[i, y, half_a, :], land[2 * k].at[i],
                                 o_refs[2 * k].at[i], "y"))
                    jobs.append((k, src.at[1 - x, i, half_b, :], src.at[x, i, half_b, :], land[2 * k + 1].at[i],
                                 o_refs[2 * k + 1].at[i], "x"))
            else:
                a1, b1 = in_refs[2 * k], in_refs[2 * k + 1]
                jobs.append((k, a1.at[1 - x], a1.at[x], land[2 * k], o_refs[k].at[half_a, :], "x"))
                jobs.append((k, b1.at[1 - y], b1.at[y], land[2 * k + 1], o_refs[k].at[half_b, :], "y"))
        sends = []
        for n, (k, send, _, landing, _, axis) in enumerate(jobs):
            cp = pltpu.make_async_remote_copy(src_ref=send, dst_ref=landing, send_sem=send_sems.at[n],
                                              recv_sem=recv_sems.at[n], device_id=peers[axis], device_id_type=MESH)
            cp.start()
            sends.append(cp)
        for cp, (k, _, kept, landing, out, _) in zip(sends, jobs):
            cp.wait_recv()
            _hbm_add(kept, landing, out, bufs[3 * k:3 * k + 3], in_sems, out_sems, chunk[k])
        for cp in sends:
            cp.wait_send()

    if first:
        out_shape = [jax.ShapeDtypeStruct((2, r, c), BF16) for (r, c) in shapes for _ in range(2)]
        land_shape = out_shape
    else:
        out_shape = [jax.ShapeDtypeStruct((2 * r, c), F32) for (r, c) in shapes]
        land_shape = [jax.ShapeDtypeStruct((r, c), BF16) for (r, c) in shapes for _ in range(2)]
    scratch = []
    for k in range(n_arr):
        scratch += [pltpu.VMEM((2, chunk[k], shapes[k][1]), BF16)] * 2 + [pltpu.VMEM((2, chunk[k], shapes[k][1]), out_dtype)]
    scratch += [pltpu.SemaphoreType.DMA((n_sem,)), pltpu.SemaphoreType.DMA((n_sem,)),
                pltpu.SemaphoreType.DMA((2, 2)), pltpu.SemaphoreType.DMA((2,))]
    outs = pl.pallas_call(
        body,
        name=name,
        in_specs=[ANY] * len(ins),
        out_specs=[ANY] * (len(out_shape) + len(land_shape)),
        out_shape=out_shape + land_shape,
        scratch_shapes=scratch,
    )(*ins)
    outs = outs[:len(out_shape)]
    return [(outs[2 * k], outs[2 * k + 1]) for k in range(n_arr)] if first else list(outs)


def _reduce_scatter(dw_al, blocks):
    xs = [dw_al] + [b.reshape(N_DEV // 2, 2, *b.shape[1:]) for b in blocks]
    ys = _halving_stage(xs, "c", "rs_c", BF16, windowed=(0,))
    pairs = _xy_stage([v.reshape(2, 2, *v.shape[1:]) for v in ys], True, "rs_xy1")
    return _xy_stage(pairs, False, "rs_xy2")


def _sum_slots(gs):
    n = len(gs)

    def body(*refs):
        for g_ref, o_ref in zip(refs[:n], refs[n:]):
            acc = g_ref[0]
            for d in range(1, N_DEV):
                acc = acc + g_ref[d]
            o_ref[...] = acc

    return pl.pallas_call(body, name="sum_slots",
                          out_shape=[jax.ShapeDtypeStruct(g.shape[1:], g.dtype) for g in gs])(*gs)


def _assemble_w_al(wins, bas):
    ba_tile = O_BA // LANE
    assert W_AL // LANE == ba_tile + 1
    cols = wins.shape[2]
    n_buf = 3
    ends = [WIN_START[d + 1] if d + 1 < N_DEV else ba_tile + 1 for d in range(N_DEV)]
    assert WIN_START[N_DEV - 1] + WIN_TILES == ba_tile + 1

    def body(w_ref, ba_ref, o_ref, buf, ld_sems, st_sems, ba_sem):
        def load(d):
            return pltpu.make_async_copy(w_ref.at[d], buf.at[d % n_buf], ld_sems.at[d % n_buf])

        def store(d):
            n = LANE * (ends[d] - WIN_START[d])
            return pltpu.make_async_copy(buf.at[d % n_buf, pl.ds(0, n), :],
                                         o_ref.at[pl.ds(LANE * WIN_START[d], n), :], st_sems.at[d % n_buf])

        load(0).start()
        for d in range(N_DEV):
            if d + 1 < N_DEV:
                if d + 1 >= n_buf:
                    store(d + 1 - n_buf).wait()
                load(d + 1).start()
            load(d).wait()
            if d > 0:
                ov = LANE * (WIN_START[d - 1] + WIN_TILES - WIN_START[d])
                buf[d % n_buf, :ov, :] = buf[d % n_buf, :ov, :] + buf[(d - 1) % n_buf, WIN_W - ov:, :]
            if d == N_DEV - 1:
                ba_copy = pltpu.make_async_copy(
                    ba_ref.at[BA_DEV], buf.at[d % n_buf, pl.ds(WIN_W - LANE, ba_ref.shape[1]), :], ba_sem)
                ba_copy.start()
                ba_copy.wait()
            store(d).start()
        for d in range(N_DEV - n_buf, N_DEV):
            store(d).wait()

    return pl.pallas_call(
        body,
        name="assemble_w_al",
        in_specs=[ANY, ANY],
        out_specs=ANY,
        out_shape=jax.ShapeDtypeStruct((W_AL, cols), wins.dtype),
        scratch_shapes=[pltpu.VMEM((n_buf, WIN_W, cols), wins.dtype), pltpu.SemaphoreType.DMA((n_buf,)),
                        pltpu.SemaphoreType.DMA((n_buf,)), pltpu.SemaphoreType.DMA],
    )(wins, bas)


def _adamw_math(w, g, m, v):
    m_new = ADAM_B1 * m + (1.0 - ADAM_B1) * g
    v_new = ADAM_B2 * v + (1.0 - ADAM_B2) * (g * g)
    m_hat = m_new / (1.0 - ADAM_B1 ** ADAM_STEP)
    v_hat = v_new / (1.0 - ADAM_B2 ** ADAM_STEP)
    return -ADAM_LR * (m_hat / (jnp.sqrt(v_hat) + ADAM_EPS) + ADAM_WD * w), m_new, v_new


def _adamw(w, g, m, v, name, tb=134):
    r, _, c = w.shape
    assert r % tb == 0

    def body(w_ref, g_ref, m_ref, v_ref, d_ref, nm_ref, nv_ref):
        d_ref[...], nm_ref[...], nv_ref[...] = _adamw_math(w_ref[...], g_ref[...], m_ref[...], v_ref[...])

    blk = pl.BlockSpec((tb, 1, c), lambda i: (i, 0, 0))
    o = jax.ShapeDtypeStruct(w.shape, F32)
    return pl.pallas_call(body, name=name, grid=(r // tb,), in_specs=[blk] * 4, out_specs=[blk] * 3,
                          out_shape=[o, o, o])(w, g, m, v)


def _adamw_many(ws, gs, ms, vs, name):
    n = len(ws)

    def body(*refs):
        for k in range(n):
            w_ref, g_ref, m_ref, v_ref = (refs[j * n + k] for j in range(4))
            d_ref, nm_ref, nv_ref = (refs[(4 + j) * n + k] for j in range(3))
            d_ref[...], nm_ref[...], nv_ref[...] = _adamw_math(w_ref[...], g_ref[...], m_ref[...], v_ref[...])

    shapes = [jax.ShapeDtypeStruct(w.shape, F32) for w in ws]
    outs = pl.pallas_call(body, name=name, out_shape=shapes * 3)(*ws, *gs, *ms, *vs)
    return outs[:n], outs[n:2 * n], outs[2 * n:]


def _select(me, table):
    return sum(jnp.where(me == d, jnp.int32(v), jnp.int32(0)) for d, v in enumerate(table))


WIN_SHIFT = tuple(SHARD_W * d - LANE * WIN_START[d] for d in range(N_DEV))
PAD_L = 256
PAD_R = 256


def _shard_to_window(shard_t, me):
    shift = _select(me, WIN_SHIFT)
    start = _select(me, WIN_START)
    padded = jnp.pad(shard_t, ((PAD_L, PAD_R), (0, 0)))
    cols = shard_t.shape[1]
    lo = lax.dynamic_slice(padded, (PAD_L - shift, 0), (WIN_W, cols))
    hi = lax.dynamic_slice(padded, (PAD_L - shift + N_BA, 0), (WIN_W, cols))
    aligned = LANE * start + lax.broadcasted_iota(jnp.int32, (WIN_W, 1), 0)
    return jnp.where(aligned >= ORIG_BA, hi, lo)


def _window_to_shard(win, ba_grad, me):
    shift = _select(me, WIN_SHIFT)
    cols = win.shape[1]
    padded = jnp.pad(win, ((N_BA, PAD_R), (0, 0)))
    lo = lax.dynamic_slice(padded, (N_BA + shift, 0), (SHARD_W, cols))
    hi = lax.dynamic_slice(padded, (shift, 0), (SHARD_W, cols))
    orig = SHARD_W * me + lax.broadcasted_iota(jnp.int32, (SHARD_W, 1), 0)
    ba_full = lax.dynamic_update_slice(jnp.zeros((SHARD_W, cols), win.dtype), ba_grad, (BA_LOCAL, 0))
    return jnp.where(orig < ORIG_BA, lo, jnp.where(orig >= ORIG_BA + N_BA, hi, ba_full))


def _pad_row(v, width=D_MODEL):
    v = v.reshape(1, -1)
    return jnp.pad(v, ((0, 0), (0, width - v.shape[1])))


def _slab(v, rows=8):
    return jnp.pad(v, ((0, rows - v.shape[0]), (0, D_MODEL - v.shape[1])))


def kernel(x, mem, norm_g, mem_norm_g, w_in, conv_w, a_log, dt_bias, dn_norm_g, w_mem_kv, w_br_dn, w_br_sb, w_br_mem, w_out, final_g, loss_target, m_norm_g, m_mem_norm_g, m_w_in, m_conv_w, m_a_log, m_dt_bias, m_dn_norm_g, m_w_mem_kv, m_w_br_dn, m_w_br_sb, m_w_br_mem, m_w_out, m_final_g, v_norm_g, v_mem_norm_g, v_w_in, v_conv_w, v_a_log, v_dt_bias, v_dn_norm_g, v_w_mem_kv, v_w_br_dn, v_w_br_sb, v_w_br_mem, v_w_out, v_final_g):
    xi, yi, ci = _position()
    me = 4 * xi + 2 * yi + ci

    shard_t = w_in[0].T
    win = _shard_to_window(shard_t, me).astype(BF16)
    ba = shard_t[BA_LOCAL:BA_LOCAL + N_BA, :].astype(BF16)
    g_win, g_ba, g_kv, g_dn, g_sb, g_out, g_mem, g_conv = _all_gather(
        [win, ba, w_mem_kv[0].astype(BF16), w_br_dn[0].astype(BF16), w_br_sb[0].astype(BF16), w_out[0].astype(BF16),
         w_br_mem[0].astype(BF16), conv_w[0]], "gather_weights")
    w_alt = _assemble_w_al(g_win, g_ba)
    w_mem_kv_f = g_kv.reshape(D_MODEL, 2 * MEM_W)
    w_br_dn_f = g_dn.reshape(D_MODEL, D_MODEL)
    w_br_sb_f = g_sb.reshape(D_MODEL, D_MODEL)
    w_out_f = g_out.reshape(D_MODEL, D_MODEL)
    w_br_mem_f = g_mem.transpose(1, 0, 2).reshape(MEM_W, D_MODEL)
    conv_w_f = g_conv.transpose(1, 0, 2).reshape(CONV_K, 3 * D_MODEL)

    r = _local_step(x[0], mem[0], loss_target[0], norm_g, mem_norm_g, w_alt, conv_w_f, _pad_row(a_log, LANE),
                    _pad_row(dt_bias, LANE), dn_norm_g, w_mem_kv_f, w_br_dn_f, w_br_sb_f, w_br_mem_f, w_out_f,
                    final_g.reshape(1, D_MODEL))

    dw_alt = r["w_alt"]
    rows_d = D_MODEL // N_DEV
    small_blocks = jnp.concatenate([
        r["w_br_dn"].reshape(N_DEV, rows_d, D_MODEL), r["w_br_sb"].reshape(N_DEV, rows_d, D_MODEL),
        r["w_out"].reshape(N_DEV, rows_d, D_MODEL), r["w_mem_kv"].reshape(N_DEV, rows_d // 2, D_MODEL),
        r["w_br_mem"].reshape(MEM_W, N_DEV, rows_d).transpose(1, 0, 2).reshape(N_DEV, MEM_W // N_DEV, D_MODEL)], axis=1)
    g_win, g_small = _reduce_scatter(dw_alt, [small_blocks])
    g_dn, g_sb, g_out = (g_small[k * rows_d:(k + 1) * rows_d] for k in range(3))
    g_kv = g_small[3 * rows_d:3 * rows_d + rows_d // 2].reshape(rows_d, 2 * MEM_W)
    g_mem = g_small[3 * rows_d + rows_d // 2:].reshape(MEM_W, rows_d)
    parts = [r["norm_g"], r["mem_norm_g"], r["final_g"], r["dn_norm_g"], r["scal"], r["loss"], r["conv_w"],
             dw_alt[O_BA:O_BA + N_BA, :].astype(F32)]
    s_norm_g, s_mem_norm_g, s_final_g, s_dn_norm_g, s_scal, s_loss, s_conv, s_ba = _sum_slots(
        _all_gather(parts, "gather_small"))
    loss = s_loss[0, 0]
    cw = conv_w.shape[2]
    g_conv = lax.dynamic_slice(s_conv, (0, cw * me), (CONV_K, cw))
    g_w_in_t = _window_to_shard(g_win, s_ba, me)
    grads = dict(norm_g=s_norm_g, mem_norm_g=s_mem_norm_g, w_in=g_w_in_t.T[None], conv_w=g_conv[None],
                 a_log=s_scal[0:1, :N_HEADS], dt_bias=s_scal[1:2, :N_HEADS], dn_norm_g=s_dn_norm_g, w_mem_kv=g_kv[None],
                 w_br_dn=g_dn[None], w_br_sb=g_sb[None], w_br_mem=g_mem[None], w_out=g_out[None],
                 final_g=s_final_g.reshape(D_MODEL))

    params = dict(norm_g=(norm_g, m_norm_g, v_norm_g), mem_norm_g=(mem_norm_g, m_mem_norm_g, v_mem_norm_g),
                  w_in=(w_in, m_w_in, v_w_in), conv_w=(conv_w, m_conv_w, v_conv_w), a_log=(a_log, m_a_log, v_a_log),
                  dt_bias=(dt_bias, m_dt_bias, v_dt_bias), dn_norm_g=(dn_norm_g, m_dn_norm_g, v_dn_norm_g),
                  w_mem_kv=(w_mem_kv, m_w_mem_kv, v_w_mem_kv), w_br_dn=(w_br_dn, m_w_br_dn, v_w_br_dn),
                  w_br_sb=(w_br_sb, m_w_br_sb, v_w_br_sb), w_br_mem=(w_br_mem, m_w_br_mem, v_w_br_mem),
                  w_out=(w_out, m_w_out, v_w_out), final_g=(final_g, m_final_g, v_final_g))
    order = list(params)
    deltas, new_m, new_v = {}, {}, {}
    deltas["w_in"], new_m["w_in"], new_v["w_in"] = (jnp.transpose(o, (1, 2, 0)) for o in _adamw(
        jnp.transpose(w_in, (2, 0, 1)), g_w_in_t[:, None, :], jnp.transpose(m_w_in, (2, 0, 1)),
        jnp.transpose(v_w_in, (2, 0, 1)), "adamw_w_in"))
    rest = [nm for nm in order if nm != "w_in"]

    def two_d(a):
        return a.reshape(1, -1) if a.ndim == 1 else a

    d_l, m_l, v_l = _adamw_many([two_d(params[nm][0]) for nm in rest], [two_d(grads[nm]) for nm in rest],
                                [two_d(params[nm][1]) for nm in rest], [two_d(params[nm][2]) for nm in rest], "adamw_rest")
    for k, nm in enumerate(rest):
        shp = params[nm][0].shape
        deltas[nm], new_m[nm], new_v[nm] = d_l[k].reshape(shp), m_l[k].reshape(shp), v_l[k].reshape(shp)
    return (loss, r["grad_x"][None], *[grads[nm] for nm in order], *[deltas[nm] for nm in order],
            *[new_m[nm] for nm in order], *[new_v[nm] for nm in order])
```

```python
import functools
import math

import jax
import jax.numpy as jnp
from jax import lax
from jax.experimental import pallas as pl
from jax.experimental.pallas import tpu as pltpu

F32 = jnp.float32
BF16 = jnp.bfloat16

D_MODEL = 1024
N_DEV = 8
N_HEADS = 8
D_HEAD = 128
DN_CHUNK = 64
CONV_K = 4
MEM_LEN = 256
MEM_HEADS = 4
MEM_DH = 64
MEM_W = MEM_HEADS * MEM_DH
NORM_EPS = 1e-6
IN_WIDTH = 11792
SHARD_W = IN_WIDTH // N_DEV

LANE = 128
SUPER = 2 * DN_CHUNK

O_QKV_DN = 0
O_Z_DN = 3072
O_QKV_SB = 4096
O_Z_SB = 7168
O_MQ = 8192
O_MZ = 8448
O_GATES = 8704
O_BA = 11776
W_AL = 11904
ORIG_BA = 4096
N_BA = 16

WIN_TILES = 13
WIN_W = WIN_TILES * LANE


def _aligned_col(o):
    return o if o < ORIG_BA else o - N_BA


WIN_START = tuple(min(_aligned_col(SHARD_W * d) // LANE, (W_AL // LANE) - WIN_TILES) for d in range(N_DEV))
WIN_OFF = tuple(_aligned_col(SHARD_W * d) - LANE * WIN_START[d] if SHARD_W * d >= ORIG_BA + N_BA or SHARD_W * d < ORIG_BA
                else None for d in range(N_DEV))
BA_DEV = ORIG_BA // SHARD_W
BA_LOCAL = ORIG_BA - BA_DEV * SHARD_W

ADAM_LR = 0.001
ADAM_B1 = 0.9
ADAM_B2 = 0.999
ADAM_EPS = 1e-08
ADAM_WD = 0.01
ADAM_STEP = 10

NN = (((1,), (0,)), ((), ()))
NT = (((1,), (1,)), ((), ()))
TN = (((0,), (0,)), ((), ()))


def _dot(a, b, dims):
    return lax.dot_general(a.astype(BF16), b.astype(BF16), dims, preferred_element_type=F32)


def _split2(a):
    hi = a.astype(BF16)
    lo = (a - hi.astype(F32)).astype(BF16)
    return hi, lo


def _dot3(a, b, dims):
    ah, al = _split2(a)
    bh, bl = _split2(b)
    d = functools.partial(lax.dot_general, dimension_numbers=dims, preferred_element_type=F32)
    return d(ah, bh) + (d(ah, bl) + d(al, bh))


def _sel_dot_impl(sel01, x, dims):
    sel = sel01.astype(BF16)
    h1 = x.astype(BF16)
    r1 = x - h1.astype(F32)
    h2 = r1.astype(BF16)
    h3 = (r1 - h2.astype(F32)).astype(BF16)
    d = functools.partial(lax.dot_general, dimension_numbers=dims, preferred_element_type=F32)
    return d(sel, h1) + (d(sel, h2) + d(sel, h3))


@jax.custom_vjp
def _sel_dot(sel01, x):
    return _sel_dot_impl(sel01, x, NN)


_sel_dot.defvjp(lambda s, x: (_sel_dot(s, x), s),
                lambda s, g: (jnp.zeros_like(s), _sel_dot_impl(s, g, TN)))


def _make_mm(dotfn):
    @jax.custom_vjp
    def nn(a, b):
        return dotfn(a, b, NN)

    @jax.custom_vjp
    def nt(a, b):
        return dotfn(a, b, NT)

    @jax.custom_vjp
    def tn(a, b):
        return dotfn(a, b, TN)

    nn.defvjp(lambda a, b: (nn(a, b), (a, b)), lambda r, g: (nt(g, r[1]), tn(r[0], g)))
    nt.defvjp(lambda a, b: (nt(a, b), (a, b)), lambda r, g: (nn(g, r[1]), tn(g, r[0])))
    tn.defvjp(lambda a, b: (tn(a, b), (a, b)), lambda r, g: (nt(r[1], g), nn(r[0], g)))
    return nn, nt, tn


mm_nn, mm_nt, mm_tn = _make_mm(_dot)
mm3_nn, mm3_nt, mm3_tn = _make_mm(_dot3)


def _sigmoid(x):
    return jax.nn.sigmoid(x)


def _silu(x):
    return x * _sigmoid(x)


def _softplus_parts(x):
    sp = jnp.log1p(jnp.exp(-jnp.abs(x)))
    return jnp.maximum(x, 0.0) + sp, jnp.maximum(-x, 0.0) + sp


def _rmsnorm(x, g):
    return x * lax.rsqrt(jnp.mean(x * x, axis=-1, keepdims=True) + NORM_EPS) * g


def _iota2(shape, dim):
    return lax.broadcasted_iota(jnp.int32, shape, dim)


def _div64(i):
    return lax.shift_right_logical(i, jnp.full(i.shape, 6, jnp.int32))


def _each(f, *lists):
    return [f(*a) for a in zip(*lists)]


@jax.custom_vjp
def _inv_unit_lower(ms):
    n = ms[0].shape[0]
    eye = (_iota2((n, n), 0) == _iota2((n, n), 1)).astype(F32)
    rs = [eye - m for m in ms]
    ps = ms
    for _ in range(5):
        ps = _each(mm3_nn, ps, ps)
        rs = _each(lambda r, p: r + mm_nn(r, p), rs, ps)
    return rs


def _inv_fwd(ms):
    rs = _inv_unit_lower(ms)
    return rs, rs


def _inv_bwd(rs, gs):
    ts = _each(mm_tn, rs, gs)
    return (_each(lambda t, r: -mm_nt(t, r), ts, rs),)


_inv_unit_lower.defvjp(_inv_fwd, _inv_bwd)


def _dn_block(cq, ck, cv, bcol, acol, zt, alog, dtb, gn, s0):
    n = SUPER
    h = DN_CHUNK
    row = _iota2((n, n), 0)
    col = _iota2((n, n), 1)
    same = _div64(row) == _div64(col)
    incl = jnp.logical_and(same, row >= col)
    strict = jnp.logical_and(same, row > col)
    incl_f = incl.astype(F32)

    qn = _each(lambda x: x * lax.rsqrt(jnp.sum(x * x, axis=-1, keepdims=True) + NORM_EPS) * (D_HEAD ** -0.5), cq)
    kn = _each(lambda x: x * lax.rsqrt(jnp.sum(x * x, axis=-1, keepdims=True) + NORM_EPS), ck)
    beta = _each(_sigmoid, bcol)
    g = _each(lambda al, ac, dt: -(jnp.exp(al) * _softplus_parts(ac + dt)[0]), alog, acol, dtb)
    gcum = _each(lambda x: _sel_dot(incl_f, jnp.broadcast_to(x, (n, n))), g)
    gam_incl = _each(lambda x: jnp.where(incl, jnp.exp(jnp.where(incl, x - x.T, 0.0)), 0.0), gcum)
    kk = _each(mm_nt, kn, kn)
    t_inv = _inv_unit_lower(_each(lambda b, x, gm: b * x * jnp.where(strict, gm, 0.0), beta, kk, gam_incl))
    eg = _each(jnp.exp, gcum)
    u = _each(lambda t, v, b: mm_nn(t, v * b), t_inv, cv, beta)
    w = _each(lambda t, k, b, e: mm_nn(t, k * (b * e)), t_inv, kn, beta, eg)
    a_intra = _each(lambda q, k, gm: mm_nt(q, k) * gm, qn, kn, gam_incl)
    q_dec = _each(lambda q, e: q * e, qn, eg)
    last0 = _each(lambda x: x[h - 1:h, :], gcum)
    last1 = _each(lambda x: x[n - 1:n, :], gcum)
    k_dec = _each(lambda k, x, l0, l1: k * jnp.exp(jnp.concatenate(
        [jnp.broadcast_to(l0, (h, n)), jnp.broadcast_to(l1, (h, n))], axis=0) - x), kn, gcum, last0, last1)
    v0 = _each(lambda uu, ww, s: uu[:h] - mm_nn(ww[:h], s), u, w, s0)
    o0 = _each(lambda q, s: mm_nn(q[:h], s), q_dec, s0)
    s1 = _each(lambda s, l0, k, v: s * jnp.exp(l0) + mm_tn(k[:h], v), s0, last0, k_dec, v0)
    v1 = _each(lambda uu, ww, s: uu[h:] - mm_nn(ww[h:], s), u, w, s1)
    o1 = _each(lambda q, s: mm_nn(q[h:], s), q_dec, s1)
    s2 = _each(lambda s, l1, k, v: s * jnp.exp(l1) + mm_tn(k[h:], v), s1, last1, k_dec, v1)
    o = _each(lambda a, b, am, x, y: jnp.concatenate([a, b], axis=0) + mm_nn(am, jnp.concatenate([x, y], axis=0)),
              o0, o1, a_intra, v0, v1)
    out = _each(lambda x, z: _rmsnorm(x, gn) * _silu(z), o, zt)
    return out, s2


def _mem_fn(mq, mz, mkv):
    mk = mkv[:, :MEM_W]
    mv = mkv[:, MEM_W:]
    lane = _iota2((1, MEM_W), 1)
    out = jnp.zeros(mq.shape, F32)
    for hd in range(MEM_HEADS):
        hm = (_div64(lane) == hd).astype(F32)
        s = mm_nt(mq * hm, mk) * (1.0 / math.sqrt(MEM_DH))
        s = s - jnp.max(s, axis=-1, keepdims=True)
        e = jnp.exp(s)
        p = e / jnp.sum(e, axis=-1, keepdims=True)
        out = out + mm_nn(p, mv) * hm
    return out * _silu(mz)


def _merge_fn(gd, gs, gm, yd, ys, ym):
    return _sigmoid(gd) * yd + _sigmoid(gs) * ys + _sigmoid(gm) * ym


def _loss_fn(x, mo, fg, tgt):
    y = _rmsnorm(x + mo, fg)
    err = y - tgt
    return 0.5 * jnp.sum(jnp.mean(err * err, axis=-1, keepdims=True), axis=0, keepdims=True)


def _matmul(a, b, mode, out_dtype, tm, tn, tk, name, b_col0=0, n_cols=None):
    if mode == "nn":
        m, kdim = a.shape
        n = b.shape[1] if n_cols is None else n_cols
    elif mode == "nt":
        m, kdim = a.shape
        n = b.shape[0]
    else:
        kdim, m = a.shape
        n = b.shape[1] if n_cols is None else n_cols
    tm, tn, tk = min(tm, m), min(tn, n), min(tk, kdim)
    assert m % tm == 0 and n % tn == 0 and kdim % tk == 0 and b_col0 % tn == 0
    nk = kdim // tk
    jb = b_col0 // tn
    dims = {"nn": NN, "nt": NT, "tn": TN}[mode]

    def body(a_ref, b_ref, o_ref, acc_ref):
        k = pl.program_id(2)
        part = _dot(a_ref[...], b_ref[...], dims)

        @pl.when(k == 0)
        def _():
            acc_ref[...] = part

        @pl.when(k > 0)
        def _():
            acc_ref[...] += part

        @pl.when(k == nk - 1)
        def _():
            o_ref[...] = acc_ref[...].astype(o_ref.dtype)

    if mode == "nn":
        a_spec = pl.BlockSpec((tm, tk), lambda i, j, k: (i, k))
        b_spec = pl.BlockSpec((tk, tn), lambda i, j, k: (k, j + jb))
    elif mode == "nt":
        a_spec = pl.BlockSpec((tm, tk), lambda i, j, k: (i, k))
        b_spec = pl.BlockSpec((tn, tk), lambda i, j, k: (j, k))
    else:
        a_spec = pl.BlockSpec((tk, tm), lambda i, j, k: (k, i))
        b_spec = pl.BlockSpec((tk, tn), lambda i, j, k: (k, j + jb))
    return pl.pallas_call(
        body,
        name=name,
        grid=(m // tm, n // tn, nk),
        in_specs=[a_spec, b_spec],
        out_specs=pl.BlockSpec((tm, tn), lambda i, j, k: (i, j)),
        out_shape=jax.ShapeDtypeStruct((m, n), out_dtype),
        scratch_shapes=[pltpu.VMEM((tm, tn), F32)],
        compiler_params=pltpu.CompilerParams(dimension_semantics=("parallel", "parallel", "arbitrary")),
    )(a, b)


def _norm_in(x, g, tm=256):
    t = x.shape[0]

    def body(x_ref, g_ref, h_ref):
        h_ref[...] = _rmsnorm(x_ref[...], g_ref[...]).astype(BF16)

    return pl.pallas_call(
        body,
        name="norm_in",
        grid=(t // tm,),
        in_specs=[pl.BlockSpec((tm, D_MODEL), lambda i: (i, 0)), pl.BlockSpec((1, D_MODEL), lambda i: (0, 0))],
        out_specs=pl.BlockSpec((tm, D_MODEL), lambda i: (i, 0)),
        out_shape=jax.ShapeDtypeStruct((t, D_MODEL), BF16),
    )(x, g)


def _norm_in_bwd(x, g, dh, dres, tm=256):
    t = x.shape[0]

    def body(x_ref, g_ref, dh_ref, dres_ref, dx_ref, dg_ref):
        _, vjp = jax.vjp(_rmsnorm, x_ref[...], g_ref[...])
        dx, dg = vjp(dh_ref[...])
        dx_ref[...] = dx + dres_ref[...]

        @pl.when(pl.program_id(0) == 0)
        def _():
            dg_ref[...] = jnp.zeros_like(dg_ref)

        dg_ref[...] += dg

    row = pl.BlockSpec((tm, D_MODEL), lambda i: (i, 0))
    vec = pl.BlockSpec((1, D_MODEL), lambda i: (0, 0))
    return pl.pallas_call(
        body,
        name="norm_in_bwd",
        grid=(t // tm,),
        in_specs=[row, vec, row, row],
        out_specs=[row, vec],
        out_shape=[jax.ShapeDtypeStruct((t, D_MODEL), F32), jax.ShapeDtypeStruct((1, D_MODEL), F32)],
    )(x, g, dh, dres)


def _merge(proj, yd, ys, ym, tm=256, tc=512):
    t = proj.shape[0]
    g0 = O_GATES // tc
    gstep = D_MODEL // tc

    def body(gd, gs, gm, yd_ref, ys_ref, ym_ref, o_ref):
        o_ref[...] = _merge_fn(gd[...], gs[...], gm[...], yd_ref[...], ys_ref[...], ym_ref[...]).astype(BF16)

    def gate(k):
        return pl.BlockSpec((tm, tc), lambda i, j: (i, g0 + k * gstep + j))

    blk = pl.BlockSpec((tm, tc), lambda i, j: (i, j))
    return pl.pallas_call(
        body,
        name="merge",
        grid=(t // tm, D_MODEL // tc),
        in_specs=[gate(0), gate(1), gate(2), blk, blk, blk],
        out_specs=blk,
        out_shape=jax.ShapeDtypeStruct((t, D_MODEL), BF16),
    )(proj, proj, proj, yd, ys, ym)


def _merge_bwd(proj, yd, ys, ym, dmerged, tm=256, tc=512):
    t = proj.shape[0]
    g0 = O_GATES // tc
    gstep = D_MODEL // tc

    def body(gd, gs, gm, yd_ref, ys_ref, ym_ref, dm_ref, dyd, dys, dym, dgd, dgs, dgm):
        _, vjp = jax.vjp(_merge_fn, gd[...], gs[...], gm[...], yd_ref[...], ys_ref[...], ym_ref[...])
        outs = vjp(dm_ref[...])
        for ref, val in zip((dgd, dgs, dgm, dyd, dys, dym), outs):
            ref[...] = val.astype(BF16)

    def gate(k):
        return pl.BlockSpec((tm, tc), lambda i, j: (i, g0 + k * gstep + j))

    blk = pl.BlockSpec((tm, tc), lambda i, j: (i, j))
    o = jax.ShapeDtypeStruct((t, D_MODEL), BF16)
    return pl.pallas_call(
        body,
        name="merge_bwd",
        grid=(t // tm, D_MODEL // tc),
        in_specs=[gate(0), gate(1), gate(2), blk, blk, blk, blk],
        out_specs=[blk] * 6,
        out_shape=[o] * 6,
    )(proj, proj, proj, yd, ys, ym, dmerged)


def _loss_head(x, mo, fg, tgt, tm=256):
    t = x.shape[0]

    def body(x_ref, mo_ref, fg_ref, t_ref, loss_ref, dout_ref, dfg_ref):
        loss, vjp = jax.vjp(_loss_fn, x_ref[...], mo_ref[...], fg_ref[...], t_ref[...])
        _, dmo, dfg, _ = vjp(jnp.ones((1, 1), F32))

        @pl.when(pl.program_id(0) == 0)
        def _():
            loss_ref[...] = jnp.zeros_like(loss_ref)
            dfg_ref[...] = jnp.zeros_like(dfg_ref)

        loss_ref[...] += jnp.broadcast_to(loss, loss_ref.shape)
        dfg_ref[...] += dfg
        dout_ref[...] = dmo

    row = pl.BlockSpec((tm, D_MODEL), lambda i: (i, 0))
    vec = pl.BlockSpec((1, D_MODEL), lambda i: (0, 0))
    return pl.pallas_call(
        body,
        name="loss_head",
        grid=(t // tm,),
        in_specs=[row, row, vec, row],
        out_specs=[pl.BlockSpec((1, LANE), lambda i: (0, 0)), row, vec],
        out_shape=[jax.ShapeDtypeStruct((1, LANE), F32), jax.ShapeDtypeStruct((t, D_MODEL), F32),
                   jax.ShapeDtypeStruct((1, D_MODEL), F32)],
    )(x, mo, fg, tgt)


def _block_tail(proj, o_dn, o_sb, o_m, x, tgt, w_br_dn, w_br_sb, w_br_mem, w_out, fg, tm=256):
    t = x.shape[0]
    tm = min(tm, t)
    gw = 512
    n_g = 3 * D_MODEL // gw

    def body(*refs):
        g_refs = refs[:n_g]
        (odn_ref, osb_ref, om_ref, x_ref, t_ref, wdn_ref, wsb_ref, wm_ref, wo_ref, fg_ref, loss_ref, dout_ref, dfg_ref,
         mg_ref, dyd_ref, dys_ref, dym_ref, dg_ref, dod_ref, dos_ref, dom_ref) = refs[n_g:]
        y = [_dot(odn_ref[...], wdn_ref[...], NN), _dot(osb_ref[...], wsb_ref[...], NN),
             _dot(om_ref[...], wm_ref[...], NN)]
        s = [_sigmoid(jnp.concatenate([g_refs[2 * k][...], g_refs[2 * k + 1][...]], axis=1)) for k in range(3)]
        merged16 = (s[0] * y[0] + s[1] * y[1] + s[2] * y[2]).astype(BF16)
        mg_ref[...] = merged16
        mo = _dot(merged16, wo_ref[...], NN)
        loss, vjp = jax.vjp(_loss_fn, x_ref[...], mo, fg_ref[...], t_ref[...])
        _, dout, dfg, _ = vjp(jnp.ones((1, 1), F32))

        @pl.when(pl.program_id(0) == 0)
        def _():
            loss_ref[...] = jnp.zeros_like(loss_ref)
            dfg_ref[...] = jnp.zeros_like(dfg_ref)

        loss_ref[...] += jnp.broadcast_to(loss, loss_ref.shape)
        dfg_ref[...] += dfg
        dout_ref[...] = dout
        dmerged = _dot(dout, wo_ref[...], NT)
        dy = [(sk * dmerged).astype(BF16) for sk in s]
        dyd_ref[...], dys_ref[...], dym_ref[...] = dy
        dg_ref[...] = jnp.concatenate([dmerged * yk * (sk * (1.0 - sk)) for yk, sk in zip(y, s)], axis=1).astype(BF16)
        dod_ref[...] = _dot(dy[0], wdn_ref[...], NT).astype(BF16)
        dos_ref[...] = _dot(dy[1], wsb_ref[...], NT).astype(BF16)
        dom_ref[...] = _dot(dy[2], wm_ref[...], NT).astype(BF16)

    gates = [pl.BlockSpec((tm, gw), lambda i, j=j: (i, O_GATES // gw + j)) for j in range(n_g)]
    row = pl.BlockSpec((tm, D_MODEL), lambda i: (i, 0))
    rowm = pl.BlockSpec((tm, MEM_W), lambda i: (i, 0))
    vec = pl.BlockSpec((1, D_MODEL), lambda i: (0, 0))

    def whole(a):
        return pl.BlockSpec(a.shape, lambda i: (0, 0), pipeline_mode=pl.Buffered(1))

    def bf(c):
        return jax.ShapeDtypeStruct((t, c), BF16)

    return pl.pallas_call(
        body,
        name="block_tail",
        grid=(t // tm,),
        in_specs=gates + [row, row, rowm, row, row, whole(w_br_dn), whole(w_br_sb), whole(w_br_mem), whole(w_out), vec],
        out_specs=[pl.BlockSpec((1, LANE), lambda i: (0, 0)), row, vec, row, row, row, row,
                   pl.BlockSpec((tm, 3 * D_MODEL), lambda i: (i, 0)), row, row, rowm],
        out_shape=[jax.ShapeDtypeStruct((1, LANE), F32), jax.ShapeDtypeStruct((t, D_MODEL), F32),
                   jax.ShapeDtypeStruct((1, D_MODEL), F32), bf(D_MODEL), bf(D_MODEL), bf(D_MODEL), bf(D_MODEL),
                   bf(3 * D_MODEL), bf(D_MODEL), bf(D_MODEL), bf(MEM_W)],
    )(*([proj] * n_g), o_dn, o_sb, o_m, x, tgt, w_br_dn, w_br_sb, w_br_mem, w_out, fg)


def _shift_rows(x, s):
    t = x.shape[0]
    if s == 0:
        return x
    rolled = pltpu.roll(x, s % t, 0)
    row = _iota2(x.shape, 0)
    keep = row >= s if s > 0 else row < t + s
    return jnp.where(keep, rolled, 0.0)


def _conv_pre(x, w):
    return sum(_shift_rows(x, CONV_K - 1 - j) * w[j:j + 1, :] for j in range(CONV_K))


CONV_TC = 256


def _dn_conv(proj, conv_w):
    t = proj.shape[0]
    nb = 3 * D_MODEL // CONV_TC

    def body(x_ref, w_ref, c_ref):
        c_ref[...] = _silu(_conv_pre(x_ref[...], w_ref[...]))

    return pl.pallas_call(
        body,
        name="dn_conv",
        grid=(nb,),
        in_specs=[pl.BlockSpec((t, CONV_TC), lambda j: (0, j)), pl.BlockSpec((CONV_K, CONV_TC), lambda j: (0, j))],
        out_specs=pl.BlockSpec((t, CONV_TC), lambda j: (0, j)),
        out_shape=jax.ShapeDtypeStruct((t, 3 * D_MODEL), F32),
    )(proj, conv_w)


def _dn_conv_bwd(proj, conv_w, dc, part):
    t = proj.shape[0]
    nb = D_MODEL // CONV_TC
    b0 = part * nb

    def body(x_ref, w_ref, dc_ref, dx_ref, dw_ref):
        x = x_ref[...]
        w = w_ref[...]
        pre = _conv_pre(x, w)
        sg = _sigmoid(pre)
        dpre = dc_ref[...] * (sg * (1.0 + pre * (1.0 - sg)))
        dx = sum(_shift_rows(dpre, -(CONV_K - 1 - j)) * w[j:j + 1, :] for j in range(CONV_K))
        dx_ref[...] = dx.astype(BF16)
        dw_ref[...] = jnp.concatenate(
            [jnp.sum(dpre * _shift_rows(x, CONV_K - 1 - j), axis=0, keepdims=True) for j in range(CONV_K)], axis=0)

    blk = pl.BlockSpec((t, CONV_TC), lambda j: (0, j))
    return pl.pallas_call(
        body,
        name=f"dn_conv_bwd{part}",
        grid=(nb,),
        in_specs=[pl.BlockSpec((t, CONV_TC), lambda j: (0, b0 + j)),
                  pl.BlockSpec((CONV_K, CONV_TC), lambda j: (0, b0 + j)), blk],
        out_specs=[blk, pl.BlockSpec((CONV_K, CONV_TC), lambda j: (0, j))],
        out_shape=[jax.ShapeDtypeStruct((t, D_MODEL), BF16), jax.ShapeDtypeStruct((CONV_K, D_MODEL), F32)],
    )(proj, conv_w, dc)


def _ba_columns(ba, hd):
    lane = _iota2(ba.shape, 1)
    bcol = jnp.sum(jnp.where(lane == hd, ba, 0.0), axis=1, keepdims=True)
    acol = jnp.sum(jnp.where(lane == N_HEADS + hd, ba, 0.0), axis=1, keepdims=True)
    return bcol, acol


def _head_scalar(row, hd):
    lane = _iota2(row.shape, 1)
    return jnp.sum(jnp.where(lane == hd, row, 0.0), axis=1, keepdims=True)


DN_HP = 8


def _dn_inputs(cq, ck, cv, ba_ref, z_ref, alog_ref, dtb_ref, heads, lanes):
    ba = ba_ref[...]
    cols = [_ba_columns(ba, hd) for hd in heads]
    return ([cq[:, ln] for ln in lanes], [ck[:, ln] for ln in lanes], [cv[:, ln] for ln in lanes],
            [c[0] for c in cols], [c[1] for c in cols], [z_ref[:, ln] for ln in lanes],
            [_head_scalar(alog_ref[...], hd) for hd in heads], [_head_scalar(dtb_ref[...], hd) for hd in heads])


def _dn_specs(nblk, reverse):
    w = DN_HP * LANE
    nq = D_MODEL // w

    def row(i):
        return nblk - 1 - i if reverse else i

    def colblk(b0):
        return pl.BlockSpec((SUPER, w), lambda i, h: (row(i), b0 + h))

    ba = pl.BlockSpec((SUPER, LANE), lambda i, h: (row(i), O_BA // LANE))
    vec = pl.BlockSpec((1, LANE), lambda i, h: (0, 0))
    st = pl.BlockSpec((1, DN_HP, D_HEAD, D_HEAD), lambda i, h: (row(i), h, 0, 0))
    return colblk, nq, ba, vec, st


def _dn_fwd(c, proj, alog_row, dtb_row, gn):
    t = c.shape[0]
    nblk = t // SUPER
    colblk, nq, ba, vec, st = _dn_specs(nblk, False)

    def body(cq, ck, cv, ba_ref, z_ref, alog_ref, dtb_ref, gn_ref, o_ref, s_ref, state):
        @pl.when(jnp.logical_and(pl.program_id(0) == 0, pl.program_id(1) == 0))
        def _():
            state[...] = jnp.zeros_like(state)

        heads = [pl.program_id(1) * DN_HP + j for j in range(DN_HP)]
        lanes = [slice(j * LANE, (j + 1) * LANE) for j in range(DN_HP)]
        s0 = [state[hd] for hd in heads]
        outs, s2 = _dn_block(*_dn_inputs(cq, ck, cv, ba_ref, z_ref, alog_ref, dtb_ref, heads, lanes), gn_ref[...], s0)
        for j, (hd, ln) in enumerate(zip(heads, lanes)):
            s_ref[0, j] = s0[j]
            o_ref[:, ln] = outs[j].astype(BF16)
            state[hd] = s2[j]

    return pl.pallas_call(
        body,
        name="dn_fwd",
        grid=(nblk, N_HEADS // DN_HP),
        in_specs=[colblk(0), colblk(nq), colblk(2 * nq), ba, colblk(O_Z_DN // (DN_HP * LANE)), vec, vec, vec],
        out_specs=[colblk(0), st],
        out_shape=[jax.ShapeDtypeStruct((t, D_MODEL), BF16),
                   jax.ShapeDtypeStruct((nblk, N_HEADS, D_HEAD, D_HEAD), F32)],
        scratch_shapes=[pltpu.VMEM((N_HEADS, D_HEAD, D_HEAD), F32)],
    )(c, c, c, proj, proj, alog_row, dtb_row, gn)


def _dn_bwd(c, proj, alog_row, dtb_row, gn, states, do):
    t = c.shape[0]
    nblk = t // SUPER
    colblk, nq, ba, vec, st = _dn_specs(nblk, True)

    def body(cq, ck, cv, ba_ref, z_ref, alog_ref, dtb_ref, gn_ref, s_ref, do_ref,
             dq_ref, dk_ref, dv_ref, dz_ref, dba_ref, dsc_ref, dgn_ref, dstate):
        i = pl.program_id(0)
        hq = pl.program_id(1)

        @pl.when(jnp.logical_and(i == 0, hq == 0))
        def _():
            dstate[...] = jnp.zeros_like(dstate)
            dsc_ref[...] = jnp.zeros_like(dsc_ref)
            dgn_ref[...] = jnp.zeros_like(dgn_ref)

        @pl.when(hq == 0)
        def _():
            dba_ref[...] = jnp.zeros_like(dba_ref)

        lane = _iota2((SUPER, LANE), 1)
        lane1 = _iota2((1, LANE), 1)
        heads = [hq * DN_HP + j for j in range(DN_HP)]
        lanes = [slice(j * LANE, (j + 1) * LANE) for j in range(DN_HP)]
        ds_in = [dstate[hd] for hd in heads]
        s_in = [s_ref[0, j] for j in range(DN_HP)]
        _, vjp = jax.vjp(_dn_block, *_dn_inputs(cq, ck, cv, ba_ref, z_ref, alog_ref, dtb_ref, heads, lanes),
                         gn_ref[...], s_in)
        dq, dk, dv, dbc, dac, dz, dal, ddt, dgn, ds0 = vjp(([do_ref[:, ln].astype(F32) for ln in lanes], ds_in))
        dba = jnp.zeros((SUPER, LANE), F32)
        dal_row = jnp.zeros((1, LANE), F32)
        ddt_row = jnp.zeros((1, LANE), F32)
        for j, (hd, ln) in enumerate(zip(heads, lanes)):
            dq_ref[:, ln] = dq[j]
            dk_ref[:, ln] = dk[j]
            dv_ref[:, ln] = dv[j]
            dz_ref[:, ln] = dz[j].astype(BF16)
            dstate[hd] = ds0[j]
            dba = dba + jnp.where(lane == hd, dbc[j], 0.0) + jnp.where(lane == N_HEADS + hd, dac[j], 0.0)
            dal_row = dal_row + jnp.where(lane1 == hd, dal[j], 0.0)
            ddt_row = ddt_row + jnp.where(lane1 == hd, ddt[j], 0.0)
        dba_ref[...] += dba
        dsc_ref[0:1, :] += dal_row
        dsc_ref[1:2, :] += ddt_row
        dgn_ref[...] += dgn

    outs = pl.pallas_call(
        body,
        name="dn_bwd",
        grid=(nblk, N_HEADS // DN_HP),
        in_specs=[colblk(0), colblk(nq), colblk(2 * nq), ba, colblk(O_Z_DN // (DN_HP * LANE)), vec, vec, vec, st,
                  colblk(0)],
        out_specs=[colblk(0), colblk(0), colblk(0), colblk(0),
                   pl.BlockSpec((SUPER, LANE), lambda i, h: (nblk - 1 - i, 0)),
                   pl.BlockSpec((2, LANE), lambda i, h: (0, 0)), vec],
        out_shape=[jax.ShapeDtypeStruct((t, D_MODEL), F32)] * 3
        + [jax.ShapeDtypeStruct((t, D_MODEL), BF16), jax.ShapeDtypeStruct((t, LANE), F32),
           jax.ShapeDtypeStruct((2, LANE), F32), jax.ShapeDtypeStruct((1, LANE), F32)],
        scratch_shapes=[pltpu.VMEM((N_HEADS, D_HEAD, D_HEAD), F32)],
    )(c, c, c, proj, proj, alog_row, dtb_row, gn, states, do)
    return outs


SB_TQ = 256
SB_TK = 256
SB_HP_FWD = 8
SB_HP_BWD = 4


def _sb_logits(z, mask):
    sp = jnp.log(1.0 + jnp.exp(-jnp.abs(z)))
    lf_raw = -(jnp.maximum(z, 0.0) + sp)
    lb = lf_raw + z
    lf = lf_raw if mask is None else jnp.where(mask, lf_raw, 0.0)
    return lb, lf_raw, lf


def _suffix_sums(x, sel):
    hi, lo = _split2(x)
    d = functools.partial(lax.dot_general, dimension_numbers=NN, preferred_element_type=F32)
    return d(hi, sel) + d(lo, sel)


def _sb_diag_mask(tq, r):
    return r * SB_TK + _iota2((tq, SB_TK), 1) < _iota2((tq, SB_TK), 0)


def _sb_specs(t, tq, hp):
    w = hp * LANE
    q0, k0, v0, z0 = (O_QKV_SB // w, (O_QKV_SB + D_MODEL) // w, (O_QKV_SB + 2 * D_MODEL) // w, O_Z_SB // w)

    def blk(b0):
        return pl.BlockSpec((tq, w), lambda h, i: (i, b0 + h))

    def full(b0, **kw):
        return pl.BlockSpec((t, w), lambda h, i: (0, b0 + h), **kw)

    once = dict(pipeline_mode=pl.Buffered(1))
    return blk(q0), full(k0, **once), full(v0, **once), blk(z0), blk(0), full(0)


def _sb_fwd(proj):
    t = proj.shape[0]
    tq = min(SB_TQ, t)
    ndiag = tq // SB_TK
    scale = 1.0 / math.sqrt(D_HEAD)

    def body(q_ref, k_ref, v_ref, z_ref, o_ref, oraw_ref):
        qi = pl.program_id(1)
        lanes = [slice(hd * LANE, (hd + 1) * LANE) for hd in range(SB_HP_FWD)]
        qs = [(q_ref[:, ln] * scale).astype(BF16) for ln in lanes]
        after = (_iota2((SB_TK, SB_TK), 0) > _iota2((SB_TK, SB_TK), 1)).astype(BF16)
        oraw_ref[...] = jnp.zeros_like(oraw_ref)

        def block(kb, mask, c_lf):
            rows = pl.ds(pl.multiple_of(kb * SB_TK, SB_TK), SB_TK)
            z = _each(lambda q, ln: _dot(q, k_ref[rows, ln], NT), qs, lanes)
            lg = _each(lambda x: _sb_logits(x, mask), z)
            surv = _each(lambda x: _suffix_sums(x[2], after), lg)
            att = _each(lambda x, s, c: jnp.exp(x[0] + s + c), lg, surv, c_lf)
            if mask is not None:
                att = _each(lambda a: jnp.where(mask, a, 0.0), att)
            pv = _each(lambda a, ln: _dot(a, v_ref[rows, ln], NN), att, lanes)
            for p, ln in zip(pv, lanes):
                oraw_ref[:, ln] += p
            return tuple(_each(lambda c, x: c + jnp.sum(x[2], axis=1, keepdims=True), c_lf, lg))

        carry = tuple(jnp.zeros((tq, 1), F32) for _ in range(SB_HP_FWD))
        for r in reversed(range(ndiag)):
            carry = block(qi * ndiag + r, _sb_diag_mask(tq, r), carry)
        lax.fori_loop(0, qi * ndiag, lambda i, c: block(qi * ndiag - 1 - i, None, c), carry)
        o_ref[...] = (oraw_ref[...] * _silu(z_ref[...])).astype(BF16)

    q_spec, k_spec, v_spec, z_spec, out, _ = _sb_specs(t, tq, SB_HP_FWD)
    return pl.pallas_call(
        body,
        name="sb_fwd",
        grid=(N_HEADS // SB_HP_FWD, t // tq),
        in_specs=[q_spec, k_spec, v_spec, z_spec],
        out_specs=[out, out],
        out_shape=[jax.ShapeDtypeStruct((t, D_MODEL), BF16), jax.ShapeDtypeStruct((t, D_MODEL), F32)],
    )(proj, proj, proj, proj)


def _sb_bwd(proj, oraw, do, blocks):
    t = proj.shape[0]
    tq = min(SB_TQ, t)
    ndiag = tq // SB_TK
    scale = 1.0 / math.sqrt(D_HEAD)

    def body(q_ref, k_ref, v_ref, z_ref, oraw_ref, do_ref, blk_ref, dq_ref, dk_ref, dv_ref, dz_ref, land_ref,
             dk_acc, dv_acc, p_scr, z_scr, send_sems, recv_sems, local_sem):
        qi = pl.program_id(1)
        nq = pl.num_programs(1)
        hg = pl.program_id(0)
        me = _position()
        mine = 4 * me[0] + 2 * me[1] + me[2]

        def exchange():
            cps = [pltpu.make_async_copy(blk_ref.at[mine], land_ref.at[mine], local_sem)]
            for k, peer in enumerate(_other_devices(me)):
                cps.append(pltpu.make_async_remote_copy(
                    src_ref=blk_ref.at[4 * peer[0] + 2 * peer[1] + peer[2]], dst_ref=land_ref.at[mine],
                    send_sem=send_sems.at[k], recv_sem=recv_sems.at[k], device_id=peer, device_id_type=MESH))
            return cps

        @pl.when(jnp.logical_and(hg == 0, qi == 0))
        def _():
            for cp in exchange():
                cp.start()

        @pl.when(qi == 0)
        def _():
            dk_acc[...] = jnp.zeros_like(dk_acc)
            dv_acc[...] = jnp.zeros_like(dv_acc)

        heads = range(SB_HP_BWD)
        lanes = [slice(hd * LANE, (hd + 1) * LANE) for hd in heads]
        zg = z_ref[...]
        sg = _sigmoid(zg)
        dog = do_ref[...].astype(F32)
        dz_ref[...] = (dog * oraw_ref[...] * (sg * (1.0 + zg * (1.0 - sg)))).astype(BF16)
        d_o = (dog * (zg * sg)).astype(BF16)
        d_o16 = [d_o[:, ln] for ln in lanes]
        qs = [(q_ref[:, ln] * scale).astype(BF16) for ln in lanes]
        ri = _iota2((SB_TK, SB_TK), 0)
        ci = _iota2((SB_TK, SB_TK), 1)
        after = (ri > ci).astype(BF16)
        earlier = (ri < ci).astype(BF16)

        def rows_of(kb):
            return pl.ds(pl.multiple_of(kb * SB_TK, SB_TK), SB_TK)

        def down(kb, mask, c_lf):
            rows = rows_of(kb)
            z = _each(lambda q, ln: _dot(q, k_ref[rows, ln], NT), qs, lanes)
            da = _each(lambda d, ln: _dot(d, v_ref[rows, ln], NT), d_o16, lanes)
            lg = _each(lambda x: _sb_logits(x, mask), z)
            surv = _each(lambda x: _suffix_sums(x[2], after), lg)
            att = _each(lambda x, s, c: jnp.exp(x[0] + s + c), lg, surv, c_lf)
            if mask is not None:
                att = _each(lambda a: jnp.where(mask, a, 0.0), att)
            dv = _each(lambda a, d: _dot(a, d, TN), att, d_o16)
            for hd in heads:
                p_scr[hd, kb] = att[hd] * da[hd]
                z_scr[hd, kb] = z[hd]
                dv_acc[rows, lanes[hd]] += dv[hd]
            return tuple(_each(lambda c, x: c + jnp.sum(x[2], axis=1, keepdims=True), c_lf, lg))

        c_lf = tuple(jnp.zeros((tq, 1), F32) for _ in heads)
        for r in reversed(range(ndiag)):
            c_lf = down(qi * ndiag + r, _sb_diag_mask(tq, r), c_lf)
        lax.fori_loop(0, qi * ndiag, lambda i, c: down(qi * ndiag - 1 - i, None, c), c_lf)

        def up(kb, mask, carry):
            dq, c_p = carry
            rows = rows_of(kb)
            p = [p_scr[hd, kb] for hd in heads]
            zs = [z_scr[hd, kb] for hd in heads]
            before = _each(lambda x, c: _suffix_sums(x, earlier) + c, p, c_p)
            e = _each(lambda x: jnp.exp(-jnp.abs(x)), zs)
            r = _each(lambda x: 1.0 / (1.0 + x), e)
            sig = _each(lambda x, a, b: jnp.where(x >= 0.0, b, a * b), zs, e, r)
            oms = _each(lambda x, a, b: jnp.where(x >= 0.0, a * b, b), zs, e, r)
            if mask is not None:
                sig = _each(lambda a: jnp.where(mask, a, 0.0), sig)
            dzz = _each(lambda x, o, g, b: x * o - g * b, p, oms, sig, before)
            dk = _each(lambda x, q: _dot(x, q, TN), dzz, qs)
            dq = _each(lambda a, x, ln: a + _dot(x, k_ref[rows, ln], NN), dq, dzz, lanes)
            for hd in heads:
                dk_acc[rows, lanes[hd]] += dk[hd]
            return tuple(dq), tuple(_each(lambda c, x: c + jnp.sum(x, axis=1, keepdims=True), c_p, p))

        carry = (tuple(jnp.zeros((tq, D_HEAD), F32) for _ in heads), tuple(jnp.zeros((tq, 1), F32) for _ in heads))
        carry = lax.fori_loop(0, qi * ndiag, lambda kb, c: up(kb, None, c), carry)
        for r in range(ndiag):
            carry = up(qi * ndiag + r, _sb_diag_mask(tq, r), carry)
        dq = carry[0]
        for hd in heads:
            dq_ref[:, lanes[hd]] = (dq[hd] * scale).astype(BF16)

        @pl.when(qi == nq - 1)
        def _():
            dk_ref[...] = dk_acc[...].astype(BF16)
            dv_ref[...] = dv_acc[...].astype(BF16)

        @pl.when(jnp.logical_and(hg == pl.num_programs(0) - 1, qi == nq - 1))
        def _():
            for cp in exchange():
                cp.wait()

    q_spec, k_spec, v_spec, z_spec, blk, full = _sb_specs(t, tq, SB_HP_BWD)
    o = jax.ShapeDtypeStruct((t, D_MODEL), BF16)
    w = SB_HP_BWD * LANE
    return pl.pallas_call(
        body,
        name="sb_bwd",
        grid=(N_HEADS // SB_HP_BWD, t // tq),
        in_specs=[q_spec, k_spec, v_spec, z_spec, blk, blk, ANY],
        out_specs=[blk, full, full, blk, ANY],
        out_shape=[o, o, o, o, jax.ShapeDtypeStruct(blocks.shape, blocks.dtype)],
        scratch_shapes=[pltpu.VMEM((t, w), F32), pltpu.VMEM((t, w), F32)]
        + [pltpu.VMEM((SB_HP_BWD, t // SB_TK, tq, SB_TK), F32)] * 2
        + [pltpu.SemaphoreType.DMA((N_DEV - 1,)), pltpu.SemaphoreType.DMA((N_DEV - 1,)), pltpu.SemaphoreType.DMA],
    )(proj, proj, proj, proj, oraw, do, blocks)


def _mem_kv_fn(mem, mg, w):
    return mm_nn(_rmsnorm(mem, mg), w)


def _mem_kv(mem, mg, w):
    def body(m_ref, g_ref, w_ref, o_ref):
        o_ref[...] = _mem_kv_fn(m_ref[...], g_ref[...], w_ref[...])

    return pl.pallas_call(body, name="mem_kv", out_shape=jax.ShapeDtypeStruct((MEM_LEN, 2 * MEM_W), F32))(mem, mg, w)


def _mem_kv_bwd(mem, mg, w, dmkv):
    def body(m_ref, g_ref, w_ref, d_ref, dg_ref, dw_ref):
        _, vjp = jax.vjp(_mem_kv_fn, m_ref[...], g_ref[...], w_ref[...].astype(F32))
        _, dg, dw = vjp(d_ref[...])
        dg_ref[...] = dg
        dw_ref[...] = dw.astype(BF16)

    return pl.pallas_call(
        body, name="mem_kv_bwd",
        out_shape=[jax.ShapeDtypeStruct((1, D_MODEL), F32), jax.ShapeDtypeStruct((D_MODEL, 2 * MEM_W), BF16)],
    )(mem, mg, w, dmkv)


def _mem_attn(proj, mkv, tm=256):
    t = proj.shape[0]
    tm = min(tm, t)

    def body(q_ref, z_ref, kv_ref, o_ref):
        o_ref[...] = _mem_fn(q_ref[...], z_ref[...], kv_ref[...]).astype(BF16)

    return pl.pallas_call(
        body,
        name="mem_attn",
        grid=(t // tm,),
        in_specs=[pl.BlockSpec((tm, MEM_W), lambda i: (i, O_MQ // MEM_W)),
                  pl.BlockSpec((tm, MEM_W), lambda i: (i, O_MZ // MEM_W)),
                  pl.BlockSpec((MEM_LEN, 2 * MEM_W), lambda i: (0, 0))],
        out_specs=pl.BlockSpec((tm, MEM_W), lambda i: (i, 0)),
        out_shape=jax.ShapeDtypeStruct((t, MEM_W), BF16),
    )(proj, proj, mkv)


def _mem_attn_bwd(proj, mkv, do, tm=256):
    t = proj.shape[0]
    tm = min(tm, t)

    def body(q_ref, z_ref, kv_ref, do_ref, dq_ref, dz_ref, dkv_ref):
        _, vjp = jax.vjp(_mem_fn, q_ref[...], z_ref[...], kv_ref[...])
        dq, dz, dkv = vjp(do_ref[...].astype(F32))
        dq_ref[...] = dq.astype(BF16)
        dz_ref[...] = dz.astype(BF16)

        @pl.when(pl.program_id(0) == 0)
        def _():
            dkv_ref[...] = jnp.zeros_like(dkv_ref)

        dkv_ref[...] += dkv

    blk = pl.BlockSpec((tm, MEM_W), lambda i: (i, 0))
    kv = pl.BlockSpec((MEM_LEN, 2 * MEM_W), lambda i: (0, 0))
    return pl.pallas_call(
        body,
        name="mem_attn_bwd",
        grid=(t // tm,),
        in_specs=[pl.BlockSpec((tm, MEM_W), lambda i: (i, O_MQ // MEM_W)),
                  pl.BlockSpec((tm, MEM_W), lambda i: (i, O_MZ // MEM_W)), kv, blk],
        out_specs=[blk, blk, kv],
        out_shape=[jax.ShapeDtypeStruct((t, MEM_W), BF16), jax.ShapeDtypeStruct((t, MEM_W), BF16),
                   jax.ShapeDtypeStruct((MEM_LEN, 2 * MEM_W), F32)],
    )(proj, proj, mkv, do)


def _proj_gather(h, w_alt, shards, tn=384):
    t = h.shape[0]
    n, kdim = w_alt.shape
    assert n % tn == 0
    nj = n // tn
    na = len(shards)

    def body(h_ref, w_ref, *rest):
        x_refs, o_ref, land = rest[:na], rest[na], rest[na + 1:2 * na + 1]
        send_sems, recv_sems, local_sems = rest[2 * na + 1:]
        j = pl.program_id(0)
        me = _position()
        mine = 4 * me[0] + 2 * me[1] + me[2]

        def copies():
            cps = []
            for a in range(na):
                cps.append(pltpu.make_async_copy(x_refs[a], land[a].at[mine], local_sems.at[a]))
                for k, peer in enumerate(_other_devices(me)):
                    cps.append(pltpu.make_async_remote_copy(
                        src_ref=x_refs[a], dst_ref=land[a].at[mine], send_sem=send_sems.at[7 * a + k],
                        recv_sem=recv_sems.at[7 * a + k], device_id=peer, device_id_type=MESH))
            return cps

        @pl.when(j == 0)
        def _():
            for cp in copies():
                cp.start()

        o_ref[...] = _dot(h_ref[...], w_ref[...], NT)

        @pl.when(j == nj - 1)
        def _():
            for cp in copies():
                cp.wait()

    outs = pl.pallas_call(
        body,
        name="proj",
        grid=(nj,),
        in_specs=[pl.BlockSpec((t, kdim), lambda j: (0, 0)), pl.BlockSpec((tn, kdim), lambda j: (j, 0))] + [ANY] * na,
        out_specs=[pl.BlockSpec((t, tn), lambda j: (0, j))] + [ANY] * na,
        out_shape=[jax.ShapeDtypeStruct((t, n), F32)]
        + [jax.ShapeDtypeStruct((N_DEV, *v.shape), v.dtype) for v in shards],
        scratch_shapes=[pltpu.SemaphoreType.DMA((7 * na,)), pltpu.SemaphoreType.DMA((7 * na,)),
                        pltpu.SemaphoreType.DMA((na,))],
    )(h, w_alt, *shards)
    return outs[0], outs[1:]


def _local_step(x, mem, tgt, norm_g, mem_norm_g, w_alt, alog_row, dtb_row, dn_norm_g, final_g, shards):
    h = _norm_in(x, norm_g)
    proj, (g_kv, g_dn, g_sb, g_out, g_mem, g_conv) = _proj_gather(h, w_alt, shards)
    w_mem_kv = g_kv.reshape(D_MODEL, 2 * MEM_W)
    w_br_dn = g_dn.reshape(D_MODEL, D_MODEL)
    w_br_sb = g_sb.reshape(D_MODEL, D_MODEL)
    w_out = g_out.reshape(D_MODEL, D_MODEL)
    w_br_mem = g_mem.transpose(1, 0, 2).reshape(MEM_W, D_MODEL)
    conv_w = g_conv.transpose(1, 0, 2).reshape(CONV_K, 3 * D_MODEL)

    c = _dn_conv(proj, conv_w)
    o_dn, states = _dn_fwd(c, proj, alog_row, dtb_row, dn_norm_g)
    o_sb, o_sb_raw = _sb_fwd(proj)
    mkv = _mem_kv(mem, mem_norm_g, w_mem_kv)
    o_m = _mem_attn(proj, mkv)

    (loss, dout, d_final_g, merged, dy_dn, dy_sb, dy_m, dgates, do_dn, do_sb, do_m) = _block_tail(
        proj, o_dn, o_sb, o_m, x, tgt, w_br_dn, w_br_sb, w_br_mem, w_out, final_g)
    dw_out = _matmul(merged, dout, "tn", BF16, 256, 1024, 2048, "dw_out")
    dw_br_dn = _matmul(o_dn, dy_dn, "tn", BF16, 256, 1024, 2048, "dw_br_dn")
    dw_br_sb = _matmul(o_sb, dy_sb, "tn", BF16, 256, 1024, 2048, "dw_br_sb")
    dw_br_mem = _matmul(o_m, dy_m, "tn", BF16, 256, 1024, 2048, "dw_br_mem")

    dmq, dmz, dmkv = _mem_attn_bwd(proj, mkv, do_m)
    d_mem_norm_g, dw_mem_kv = _mem_kv_bwd(mem, mem_norm_g, w_mem_kv, dmkv)
    rows_d = D_MODEL // N_DEV
    small_blocks = jnp.concatenate([
        dw_br_dn.reshape(N_DEV, rows_d, D_MODEL), dw_br_sb.reshape(N_DEV, rows_d, D_MODEL),
        dw_out.reshape(N_DEV, rows_d, D_MODEL), dw_mem_kv.reshape(N_DEV, rows_d // 2, D_MODEL),
        dw_br_mem.reshape(MEM_W, N_DEV, rows_d).transpose(1, 0, 2).reshape(N_DEV, MEM_W // N_DEV, D_MODEL)], axis=1)
    dq_sb, dk_sb, dv_sb, dz_sb, small_parts = _sb_bwd(proj, o_sb_raw, do_sb, small_blocks)
    (d_small,) = _sum_slots([small_parts], "sum_small_grads")
    dcq, dck, dcv, dz_dn, dba, dscal, d_dn_norm_g = _dn_bwd(c, proj, alog_row, dtb_row, dn_norm_g, states, do_dn)
    dq_dn, dcw_q = _dn_conv_bwd(proj, conv_w, dcq, 0)
    dk_dn, dcw_k = _dn_conv_bwd(proj, conv_w, dck, 1)
    dv_dn, dcw_v = _dn_conv_bwd(proj, conv_w, dcv, 2)
    d_conv_w = jnp.concatenate([dcw_q, dcw_k, dcw_v], axis=1)

    dproj = jnp.concatenate([dq_dn, dk_dn, dv_dn, dz_dn, dq_sb, dk_sb, dv_sb, dz_sb, dmq, dmz, dgates,
                             dba.astype(BF16)], axis=1)
    dh = _matmul(dproj, w_alt, "nn", F32, 512, 1024, 3968, "dh")
    dw_alt = _matmul(dproj, h, "tn", BF16, 384, 1024, 2048, "dw_alt")
    grad_x, d_norm_g = _norm_in_bwd(x, norm_g, dh, dout)
    return dict(loss=loss, grad_x=grad_x, norm_g=d_norm_g, mem_norm_g=d_mem_norm_g, w_alt=dw_alt, conv_w=d_conv_w,
                scal=dscal, dn_norm_g=d_dn_norm_g, small=d_small, final_g=d_final_g)


MESH = pl.DeviceIdType.MESH
ANY = pl.BlockSpec(memory_space=pl.ANY)


def _position():
    return lax.axis_index("x"), lax.axis_index("y"), lax.axis_index("c")


def _other_devices(me):
    return [tuple(1 - p if (f >> s) & 1 else p for p, s in zip(me, (2, 1, 0))) for f in range(1, N_DEV)]


def _all_gather(xs, name):
    n = len(xs)

    def body(*refs):
        x_refs, o_refs = refs[:n], refs[n:2 * n]
        send_sems, recv_sems, local_sems = refs[2 * n:]
        x, y, c = _position()
        me, sibling = (x, y, c), (x, y, 1 - c)
        x_nbr, y_nbr, diag = (1 - x, y, c), (x, 1 - y, c), (1 - x, 1 - y, c)
        south = c == 0
        relay_from = tuple(jnp.where(south, a, b) for a, b in zip(y_nbr, x_nbr))
        relay_to = tuple(jnp.where(south, a, b) for a, b in zip(x_nbr, y_nbr))

        def slot(p):
            return 4 * p[0] + 2 * p[1] + p[2]

        def copy(a, k, block, to, src=None):
            dst = o_refs[a].at[slot(block)]
            return pltpu.make_async_remote_copy(
                src_ref=dst if src is None else src, dst_ref=dst, send_sem=send_sems.at[7 * a + k],
                recv_sem=recv_sems.at[7 * a + k], device_id=to, device_id_type=MESH)

        mine = [pltpu.make_async_copy(x_refs[a], o_refs[a].at[slot(me)], local_sems.at[a]) for a in range(n)]
        for cp in mine:
            cp.start()
        sends = []
        for a in range(n):
            sends += [copy(a, 0, me, sibling, src=x_refs[a]), copy(a, 1, me, x_nbr, src=x_refs[a]),
                      copy(a, 2, me, y_nbr, src=x_refs[a])]
        for cp in sends:
            cp.start()
        later = []
        for a in range(n):
            copy(a, 1, x_nbr, me).wait_recv()
            copy(a, 2, y_nbr, me).wait_recv()
            later += [copy(a, 3, relay_from, relay_to), copy(a, 4, x_nbr, sibling), copy(a, 5, y_nbr, sibling)]
            for cp in later[-3:]:
                cp.start()
        for a in range(n):
            copy(a, 3, diag, me).wait_recv()
            later.append(copy(a, 6, diag, sibling))
            later[-1].start()
        for a in range(n):
            copy(a, 0, sibling, me).wait_recv()
            for k, chip in ((4, x_nbr), (5, y_nbr), (6, diag)):
                copy(a, k, (chip[0], chip[1], 1 - c), me).wait_recv()
        for cp in sends + later:
            cp.wait_send()
        for cp in mine:
            cp.wait()

    return pl.pallas_call(
        body,
        name=name,
        in_specs=[ANY] * n,
        out_specs=[ANY] * n,
        out_shape=[jax.ShapeDtypeStruct((N_DEV, *v.shape), v.dtype) for v in xs],
        scratch_shapes=[pltpu.SemaphoreType.DMA((7 * n,)), pltpu.SemaphoreType.DMA((7 * n,)),
                        pltpu.SemaphoreType.DMA((n,))],
    )(*xs)


def _window_view(ref, dest):
    return ref.at[pl.ds(LANE * WIN_START[dest], WIN_W), :]


def _chunk_rows(rows, cols):
    return max(ch for ch in range(16, rows + 1, 16) if rows % ch == 0 and ch * cols <= (1 << 19))


def _halving_stage(xs, axis, name, out_dtype, windowed=()):
    n_arr = len(xs)
    metas = []
    for k, v in enumerate(xs):
        if k in windowed:
            metas.append((N_DEV // 2, WIN_W, v.shape[1]))
        else:
            assert v.shape[1] == 2
            metas.append((v.shape[0], v.shape[2], v.shape[3]))
    chunk = [_chunk_rows(r, c) for (_, r, c) in metas]
    offs = [sum(m[0] for m in metas[:k]) for k in range(n_arr)]
    n_sem = sum(m[0] for m in metas)

    def body(*refs):
        x_refs = refs[:n_arr]
        o_refs = refs[n_arr:2 * n_arr]
        land_refs = refs[2 * n_arr:3 * n_arr]
        rest = refs[3 * n_arr:]
        bufs = rest[:3 * n_arr]
        send_sems, recv_sems, in_sems, out_sems = rest[3 * n_arr:]
        pos = dict(zip("xyc", _position()))
        bit = pos[axis]
        peer = tuple(1 - pos[a] if a == axis else pos[a] for a in "xyc")

        def view(k, i, b):
            if k in windowed:
                return _window_view(x_refs[k], 2 * i + b)
            return x_refs[k].at[i, b]

        def add_blocks(k, a_view, b_view, o_view):
            _hbm_add(a_view, b_view, o_view, bufs[3 * k:3 * k + 3], in_sems, out_sems, chunk[k])

        for b in (0, 1):
            @pl.when(bit == b)
            def _(b=b):
                sends = []
                for k in range(n_arr):
                    for i in range(metas[k][0]):
                        cp = pltpu.make_async_remote_copy(
                            src_ref=view(k, i, 1 - b), dst_ref=land_refs[k].at[i], send_sem=send_sems.at[offs[k] + i],
                            recv_sem=recv_sems.at[offs[k] + i], device_id=peer, device_id_type=MESH)
                        cp.start()
                        sends.append(cp)
                idx = 0
                for k in range(n_arr):
                    for i in range(metas[k][0]):
                        sends[idx].wait_recv()
                        add_blocks(k, view(k, i, b), land_refs[k].at[i], o_refs[k].at[i])
                        idx += 1
                for cp in sends:
                    cp.wait_send()

    out_shape = [jax.ShapeDtypeStruct(m, out_dtype) for m in metas]
    land_shape = [jax.ShapeDtypeStruct(m, v.dtype) for m, v in zip(metas, xs)]
    scratch = []
    for k in range(n_arr):
        blk = (2, chunk[k], metas[k][2])
        scratch += [pltpu.VMEM(blk, xs[k].dtype)] * 2 + [pltpu.VMEM(blk, out_dtype)]
    scratch += [pltpu.SemaphoreType.DMA((n_sem,)), pltpu.SemaphoreType.DMA((n_sem,)),
                pltpu.SemaphoreType.DMA((2, 2)), pltpu.SemaphoreType.DMA((2,))]
    outs = pl.pallas_call(
        body,
        name=name,
        in_specs=[ANY] * n_arr,
        out_specs=[ANY] * (2 * n_arr),
        out_shape=out_shape + land_shape,
        scratch_shapes=scratch,
    )(*xs)
    return outs[:n_arr]


def _hbm_add(a_view, b_view, o_view, bufs, in_sems, out_sems, ch):
    rows = a_view.shape[0]
    nch = rows // ch
    va, vb, vo = bufs

    def rows_of(j):
        return pl.ds(pl.multiple_of(j * ch, 16), ch)

    def loads(j, s):
        return (pltpu.make_async_copy(a_view.at[rows_of(j), :], va.at[s], in_sems.at[0, s]),
                pltpu.make_async_copy(b_view.at[rows_of(j), :], vb.at[s], in_sems.at[1, s]))

    def store(j, s):
        return pltpu.make_async_copy(vo.at[s], o_view.at[rows_of(j), :], out_sems.at[s])

    for cp in loads(0, 0):
        cp.start()

    def step(j, _):
        s = lax.rem(j, 2)

        @pl.when(j + 1 < nch)
        def _():
            for cp in loads(j + 1, 1 - s):
                cp.start()

        for cp in loads(j, s):
            cp.wait()

        @pl.when(j >= 2)
        def _():
            store(j - 2, s).wait()

        vo[s] = (va[s].astype(F32) + vb[s].astype(F32)).astype(vo.dtype)
        store(j, s).start()
        return 0

    lax.fori_loop(0, nch, step, 0)
    for j in range(max(0, nch - 2), nch):
        store(j, j % 2).wait()


def _xy_stage(xs, first, name):
    n_arr = len(xs)
    if first:
        shapes = [(v.shape[2] // 2, v.shape[3]) for v in xs]
        ins = list(xs)
    else:
        shapes = [(a.shape[1], a.shape[2]) for a, _ in xs]
        ins = [v for pair in xs for v in pair]
    n_blk = 2 if first else 1
    out_dtype = BF16 if first else F32
    chunk = [_chunk_rows(r, c) for (r, c) in shapes]
    n_sem = 2 * n_blk * n_arr

    def body(*refs):
        n_in = len(ins)
        in_refs = refs[:n_in]
        n_out = 2 * n_arr if first else n_arr
        o_refs = refs[n_in:n_in + n_out]
        land = refs[n_in + n_out:n_in + n_out + 2 * n_arr]
        rest = refs[n_in + n_out + 2 * n_arr:]
        bufs = rest[:3 * n_arr]
        send_sems, recv_sems, in_sems, out_sems = rest[3 * n_arr:]
        x, y, c = _position()
        peers = {"x": (1 - x, y, c), "y": (x, 1 - y, c)}
        jobs = []
        for k in range(n_arr):
            r, _ = shapes[k]
            half_a, half_b = pl.ds(0, r), pl.ds(r, r)
            if first:
                src = in_refs[k]
                for i in range(2):
                    jobs.append((k, src.at[i, 1 - y, half_a, :], src.at[i, y, half_a, :], land[2 * k].at[i],
                                 o_refs[2 * k].at[i], "y"))
                    jobs.append((k, src.at[1 - x, i, half_b, :], src.at[x, i, half_b, :], land[2 * k + 1].at[i],
                                 o_refs[2 * k + 1].at[i], "x"))
            else:
                a1, b1 = in_refs[2 * k], in_refs[2 * k + 1]
                jobs.append((k, a1.at[1 - x], a1.at[x], land[2 * k], o_refs[k].at[half_a, :], "x"))
                jobs.append((k, b1.at[1 - y], b1.at[y], land[2 * k + 1], o_refs[k].at[half_b, :], "y"))
        sends = []
        for n, (k, send, _, landing, _, axis) in enumerate(jobs):
            cp = pltpu.make_async_remote_copy(src_ref=send, dst_ref=landing, send_sem=send_sems.at[n],
                                              recv_sem=recv_sems.at[n], device_id=peers[axis], device_id_type=MESH)
            cp.start()
            sends.append(cp)
        for cp, (k, _, kept, landing, out, _) in zip(sends, jobs):
            cp.wait_recv()
            _hbm_add(kept, landing, out, bufs[3 * k:3 * k + 3], in_sems, out_sems, chunk[k])
        for cp in sends:
            cp.wait_send()

    if first:
        out_shape = [jax.ShapeDtypeStruct((2, r, c), BF16) for (r, c) in shapes for _ in range(2)]
        land_shape = out_shape
    else:
        out_shape = [jax.ShapeDtypeStruct((2 * r, c), F32) for (r, c) in shapes]
        land_shape = [jax.ShapeDtypeStruct((r, c), BF16) for (r, c) in shapes for _ in range(2)]
    scratch = []
    for k in range(n_arr):
        scratch += [pltpu.VMEM((2, chunk[k], shapes[k][1]), BF16)] * 2 + [pltpu.VMEM((2, chunk[k], shapes[k][1]), out_dtype)]
    scratch += [pltpu.SemaphoreType.DMA((n_sem,)), pltpu.SemaphoreType.DMA((n_sem,)),
                pltpu.SemaphoreType.DMA((2, 2)), pltpu.SemaphoreType.DMA((2,))]
    outs = pl.pallas_call(
        body,
        name=name,
        in_specs=[ANY] * len(ins),
        out_specs=[ANY] * (len(out_shape) + len(land_shape)),
        out_shape=out_shape + land_shape,
        scratch_shapes=scratch,
    )(*ins)
    outs = outs[:len(out_shape)]
    return [(outs[2 * k], outs[2 * k + 1]) for k in range(n_arr)] if first else list(outs)


def _reduce_scatter(dw_al, blocks):
    xs = [dw_al] + [b.reshape(N_DEV // 2, 2, *b.shape[1:]) for b in blocks]
    ys = _halving_stage(xs, "c", "rs_c", BF16, windowed=(0,))
    pairs = _xy_stage([v.reshape(2, 2, *v.shape[1:]) for v in ys], True, "rs_xy1")
    return _xy_stage(pairs, False, "rs_xy2")


def _sum_slots(gs, name):
    n = len(gs)

    def body(*refs):
        for g_ref, o_ref in zip(refs[:n], refs[n:]):
            acc = g_ref[0].astype(F32)
            for d in range(1, N_DEV):
                acc = acc + g_ref[d].astype(F32)
            o_ref[...] = acc

    return pl.pallas_call(body, name=name, out_shape=[jax.ShapeDtypeStruct(g.shape[1:], F32) for g in gs])(*gs)


def _assemble_w_al(wins, bas):
    ba_tile = O_BA // LANE
    assert W_AL // LANE == ba_tile + 1
    cols = wins.shape[2]
    n_buf = 3
    ends = [WIN_START[d + 1] if d + 1 < N_DEV else ba_tile + 1 for d in range(N_DEV)]
    assert WIN_START[N_DEV - 1] + WIN_TILES == ba_tile + 1

    def body(w_ref, ba_ref, o_ref, buf, ld_sems, st_sems, ba_sem):
        def load(d):
            return pltpu.make_async_copy(w_ref.at[d], buf.at[d % n_buf], ld_sems.at[d % n_buf])

        def store(d):
            n = LANE * (ends[d] - WIN_START[d])
            return pltpu.make_async_copy(buf.at[d % n_buf, pl.ds(0, n), :],
                                         o_ref.at[pl.ds(LANE * WIN_START[d], n), :], st_sems.at[d % n_buf])

        load(0).start()
        for d in range(N_DEV):
            if d + 1 < N_DEV:
                if d + 1 >= n_buf:
                    store(d + 1 - n_buf).wait()
                load(d + 1).start()
            load(d).wait()
            if d > 0:
                ov = LANE * (WIN_START[d - 1] + WIN_TILES - WIN_START[d])
                buf[d % n_buf, :ov, :] = buf[d % n_buf, :ov, :] + buf[(d - 1) % n_buf, WIN_W - ov:, :]
            if d == N_DEV - 1:
                ba_copy = pltpu.make_async_copy(
                    ba_ref.at[BA_DEV], buf.at[d % n_buf, pl.ds(WIN_W - LANE, ba_ref.shape[1]), :], ba_sem)
                ba_copy.start()
                ba_copy.wait()
            store(d).start()
        for d in range(N_DEV - n_buf, N_DEV):
            store(d).wait()

    return pl.pallas_call(
        body,
        name="assemble_w_al",
        in_specs=[ANY, ANY],
        out_specs=ANY,
        out_shape=jax.ShapeDtypeStruct((W_AL, cols), wins.dtype),
        scratch_shapes=[pltpu.VMEM((n_buf, WIN_W, cols), wins.dtype), pltpu.SemaphoreType.DMA((n_buf,)),
                        pltpu.SemaphoreType.DMA((n_buf,)), pltpu.SemaphoreType.DMA],
    )(wins, bas)


def _adamw_math(w, g, m, v):
    m_new = ADAM_B1 * m + (1.0 - ADAM_B1) * g
    v_new = ADAM_B2 * v + (1.0 - ADAM_B2) * (g * g)
    m_hat = m_new / (1.0 - ADAM_B1 ** ADAM_STEP)
    v_hat = v_new / (1.0 - ADAM_B2 ** ADAM_STEP)
    return -ADAM_LR * (m_hat / (jnp.sqrt(v_hat) + ADAM_EPS) + ADAM_WD * w), m_new, v_new


def _adamw(w, g, m, v, name, tb=134):
    r, _, c = w.shape
    assert r % tb == 0

    def body(w_ref, g_ref, m_ref, v_ref, d_ref, nm_ref, nv_ref):
        d_ref[...], nm_ref[...], nv_ref[...] = _adamw_math(w_ref[...], g_ref[...], m_ref[...], v_ref[...])

    blk = pl.BlockSpec((tb, 1, c), lambda i: (i, 0, 0))
    o = jax.ShapeDtypeStruct(w.shape, F32)
    return pl.pallas_call(body, name=name, grid=(r // tb,), in_specs=[blk] * 4, out_specs=[blk] * 3,
                          out_shape=[o, o, o])(w, g, m, v)


def _adamw_many(ws, gs, ms, vs, name):
    n = len(ws)

    def body(*refs):
        for k in range(n):
            w_ref, g_ref, m_ref, v_ref = (refs[j * n + k] for j in range(4))
            d_ref, nm_ref, nv_ref = (refs[(4 + j) * n + k] for j in range(3))
            d_ref[...], nm_ref[...], nv_ref[...] = _adamw_math(w_ref[...], g_ref[...], m_ref[...], v_ref[...])

    shapes = [jax.ShapeDtypeStruct(w.shape, F32) for w in ws]
    outs = pl.pallas_call(body, name=name, out_shape=shapes * 3)(*ws, *gs, *ms, *vs)
    return outs[:n], outs[n:2 * n], outs[2 * n:]


def _select(me, table):
    return sum(jnp.where(me == d, jnp.int32(v), jnp.int32(0)) for d, v in enumerate(table))


WIN_SHIFT = tuple(SHARD_W * d - LANE * WIN_START[d] for d in range(N_DEV))
PAD_L = 256
PAD_R = 256


def _shard_to_window(shard_t, me):
    shift = _select(me, WIN_SHIFT)
    start = _select(me, WIN_START)
    padded = jnp.pad(shard_t, ((PAD_L, PAD_R), (0, 0)))
    cols = shard_t.shape[1]
    lo = lax.dynamic_slice(padded, (PAD_L - shift, 0), (WIN_W, cols))
    hi = lax.dynamic_slice(padded, (PAD_L - shift + N_BA, 0), (WIN_W, cols))
    aligned = LANE * start + lax.broadcasted_iota(jnp.int32, (WIN_W, 1), 0)
    return jnp.where(aligned >= ORIG_BA, hi, lo)


def _window_to_shard(win, ba_grad, me):
    shift = _select(me, WIN_SHIFT)
    cols = win.shape[1]
    padded = jnp.pad(win, ((N_BA, PAD_R), (0, 0)))
    lo = lax.dynamic_slice(padded, (N_BA + shift, 0), (SHARD_W, cols))
    hi = lax.dynamic_slice(padded, (shift, 0), (SHARD_W, cols))
    orig = SHARD_W * me + lax.broadcasted_iota(jnp.int32, (SHARD_W, 1), 0)
    ba_full = lax.dynamic_update_slice(jnp.zeros((SHARD_W, cols), win.dtype), ba_grad, (BA_LOCAL, 0))
    return jnp.where(orig < ORIG_BA, lo, jnp.where(orig >= ORIG_BA + N_BA, hi, ba_full))


def _pad_row(v, width=D_MODEL):
    v = v.reshape(1, -1)
    return jnp.pad(v, ((0, 0), (0, width - v.shape[1])))


def _slab(v, rows=8):
    return jnp.pad(v, ((0, rows - v.shape[0]), (0, D_MODEL - v.shape[1])))


def kernel(x, mem, norm_g, mem_norm_g, w_in, conv_w, a_log, dt_bias, dn_norm_g, w_mem_kv, w_br_dn, w_br_sb, w_br_mem, w_out, final_g, loss_target, m_norm_g, m_mem_norm_g, m_w_in, m_conv_w, m_a_log, m_dt_bias, m_dn_norm_g, m_w_mem_kv, m_w_br_dn, m_w_br_sb, m_w_br_mem, m_w_out, m_final_g, v_norm_g, v_mem_norm_g, v_w_in, v_conv_w, v_a_log, v_dt_bias, v_dn_norm_g, v_w_mem_kv, v_w_br_dn, v_w_br_sb, v_w_br_mem, v_w_out, v_final_g):
    xi, yi, ci = _position()
    me = 4 * xi + 2 * yi + ci

    shard_t = w_in[0].T
    win = _shard_to_window(shard_t, me).astype(BF16)
    ba = shard_t[BA_LOCAL:BA_LOCAL + N_BA, :].astype(BF16)
    g_win, g_ba = _all_gather([win, ba], "gather_weights")
    w_alt = _assemble_w_al(g_win, g_ba)

    shards = [w_mem_kv[0].astype(BF16), w_br_dn[0].astype(BF16), w_br_sb[0].astype(BF16), w_out[0].astype(BF16),
              w_br_mem[0].astype(BF16), conv_w[0]]
    r = _local_step(x[0], mem[0], loss_target[0], norm_g, mem_norm_g, w_alt, _pad_row(a_log, LANE),
                    _pad_row(dt_bias, LANE), dn_norm_g, final_g.reshape(1, D_MODEL), shards)

    dw_alt = r["w_alt"]
    (g_win,) = _reduce_scatter(dw_alt, [])
    rows_d = D_MODEL // N_DEV
    g_small = r["small"]
    g_dn, g_sb, g_out = (g_small[k * rows_d:(k + 1) * rows_d] for k in range(3))
    g_kv = g_small[3 * rows_d:3 * rows_d + rows_d // 2].reshape(rows_d, 2 * MEM_W)
    g_mem = g_small[3 * rows_d + rows_d // 2:].reshape(MEM_W, rows_d)
    parts = [r["norm_g"], r["mem_norm_g"], r["final_g"], r["dn_norm_g"], r["scal"], r["loss"], r["conv_w"],
             dw_alt[O_BA:O_BA + N_BA, :].astype(F32)]
    s_norm_g, s_mem_norm_g, s_final_g, s_dn_norm_g, s_scal, s_loss, s_conv, s_ba = _sum_slots(
        _all_gather(parts, "gather_small"), "sum_small")
    loss = s_loss[0, 0]
    cw = conv_w.shape[2]
    g_conv = lax.dynamic_slice(s_conv, (0, cw * me), (CONV_K, cw))
    g_w_in_t = _window_to_shard(g_win, s_ba, me)
    grads = dict(norm_g=s_norm_g, mem_norm_g=s_mem_norm_g, w_in=g_w_in_t.T[None], conv_w=g_conv[None],
                 a_log=s_scal[0:1, :N_HEADS], dt_bias=s_scal[1:2, :N_HEADS], dn_norm_g=s_dn_norm_g, w_mem_kv=g_kv[None],
                 w_br_dn=g_dn[None], w_br_sb=g_sb[None], w_br_mem=g_mem[None], w_out=g_out[None],
                 final_g=s_final_g.reshape(D_MODEL))

    params = dict(norm_g=(norm_g, m_norm_g, v_norm_g), mem_norm_g=(mem_norm_g, m_mem_norm_g, v_mem_norm_g),
                  w_in=(w_in, m_w_in, v_w_in), conv_w=(conv_w, m_conv_w, v_conv_w), a_log=(a_log, m_a_log, v_a_log),
                  dt_bias=(dt_bias, m_dt_bias, v_dt_bias), dn_norm_g=(dn_norm_g, m_dn_norm_g, v_dn_norm_g),
                  w_mem_kv=(w_mem_kv, m_w_mem_kv, v_w_mem_kv), w_br_dn=(w_br_dn, m_w_br_dn, v_w_br_dn),
                  w_br_sb=(w_br_sb, m_w_br_sb, v_w_br_sb), w_br_mem=(w_br_mem, m_w_br_mem, v_w_br_mem),
                  w_out=(w_out, m_w_out, v_w_out), final_g=(final_g, m_final_g, v_final_g))
    order = list(params)
    deltas, new_m, new_v = {}, {}, {}
    deltas["w_in"], new_m["w_in"], new_v["w_in"] = (jnp.transpose(o, (1, 2, 0)) for o in _adamw(
        jnp.transpose(w_in, (2, 0, 1)), g_w_in_t[:, None, :], jnp.transpose(m_w_in, (2, 0, 1)),
        jnp.transpose(v_w_in, (2, 0, 1)), "adamw_w_in"))
    rest = [nm for nm in order if nm != "w_in"]

    def two_d(a):
        return a.reshape(1, -1) if a.ndim == 1 else a

    d_l, m_l, v_l = _adamw_many([two_d(params[nm][0]) for nm in rest], [two_d(grads[nm]) for nm in rest],
                                [two_d(params[nm][1]) for nm in rest], [two_d(params[nm][2]) for nm in rest], "adamw_rest")
    for k, nm in enumerate(rest):
        shp = params[nm][0].shape
        deltas[nm], new_m[nm], new_v[nm] = d_l[k].reshape(shp), m_l[k].reshape(shp), v_l[k].reshape(shp)
    return (loss, r["grad_x"][None], *[grads[nm] for nm in order], *[deltas[nm] for nm in order],
            *[new_m[nm] for nm in order], *[new_v[nm] for nm in order])
```

```python
import functools
import math

import jax
import jax.numpy as jnp
from jax import lax
from jax.experimental import pallas as pl
from jax.experimental.pallas import tpu as pltpu

F32 = jnp.float32
BF16 = jnp.bfloat16

D_MODEL = 1024
N_DEV = 8
N_HEADS = 8
D_HEAD = 128
DN_CHUNK = 64
CONV_K = 4
MEM_LEN = 256
MEM_HEADS = 4
MEM_DH = 64
MEM_W = MEM_HEADS * MEM_DH
NORM_EPS = 1e-6
IN_WIDTH = 11792
SHARD_W = IN_WIDTH // N_DEV

LANE = 128
SUPER = 2 * DN_CHUNK

O_QKV_DN = 0
O_Z_DN = 3072
O_QKV_SB = 4096
O_Z_SB = 7168
O_MQ = 8192
O_MZ = 8448
O_GATES = 8704
O_BA = 11776
W_AL = 11904
ORIG_BA = 4096
N_BA = 16

WIN_TILES = 13
WIN_W = WIN_TILES * LANE


def _aligned_col(o):
    return o if o < ORIG_BA else o - N_BA


WIN_START = tuple(min(_aligned_col(SHARD_W * d) // LANE, (W_AL // LANE) - WIN_TILES) for d in range(N_DEV))
WIN_OFF = tuple(_aligned_col(SHARD_W * d) - LANE * WIN_START[d] if SHARD_W * d >= ORIG_BA + N_BA or SHARD_W * d < ORIG_BA
                else None for d in range(N_DEV))
BA_DEV = ORIG_BA // SHARD_W
BA_LOCAL = ORIG_BA - BA_DEV * SHARD_W

ADAM_LR = 0.001
ADAM_B1 = 0.9
ADAM_B2 = 0.999
ADAM_EPS = 1e-08
ADAM_WD = 0.01
ADAM_STEP = 10

NN = (((1,), (0,)), ((), ()))
NT = (((1,), (1,)), ((), ()))
TN = (((0,), (0,)), ((), ()))


def _dot(a, b, dims):
    return lax.dot_general(a.astype(BF16), b.astype(BF16), dims, preferred_element_type=F32)


def _split2(a):
    hi = a.astype(BF16)
    lo = (a - hi.astype(F32)).astype(BF16)
    return hi, lo


def _dot3(a, b, dims):
    ah, al = _split2(a)
    bh, bl = _split2(b)
    d = functools.partial(lax.dot_general, dimension_numbers=dims, preferred_element_type=F32)
    return d(ah, bh) + (d(ah, bl) + d(al, bh))


def _sel_dot_impl(sel01, x, dims):
    sel = sel01.astype(BF16)
    h1 = x.astype(BF16)
    r1 = x - h1.astype(F32)
    h2 = r1.astype(BF16)
    h3 = (r1 - h2.astype(F32)).astype(BF16)
    d = functools.partial(lax.dot_general, dimension_numbers=dims, preferred_element_type=F32)
    return d(sel, h1) + (d(sel, h2) + d(sel, h3))


@jax.custom_vjp
def _sel_dot(sel01, x):
    return _sel_dot_impl(sel01, x, NN)


_sel_dot.defvjp(lambda s, x: (_sel_dot(s, x), s),
                lambda s, g: (jnp.zeros_like(s), _sel_dot_impl(s, g, TN)))


def _make_mm(dotfn):
    @jax.custom_vjp
    def nn(a, b):
        return dotfn(a, b, NN)

    @jax.custom_vjp
    def nt(a, b):
        return dotfn(a, b, NT)

    @jax.custom_vjp
    def tn(a, b):
        return dotfn(a, b, TN)

    nn.defvjp(lambda a, b: (nn(a, b), (a, b)), lambda r, g: (nt(g, r[1]), tn(r[0], g)))
    nt.defvjp(lambda a, b: (nt(a, b), (a, b)), lambda r, g: (nn(g, r[1]), tn(g, r[0])))
    tn.defvjp(lambda a, b: (tn(a, b), (a, b)), lambda r, g: (nt(r[1], g), nn(r[0], g)))
    return nn, nt, tn


mm_nn, mm_nt, mm_tn = _make_mm(_dot)
mm3_nn, mm3_nt, mm3_tn = _make_mm(_dot3)


def _sigmoid(x):
    return jax.nn.sigmoid(x)


def _silu(x):
    return x * _sigmoid(x)


def _softplus_parts(x):
    sp = jnp.log1p(jnp.exp(-jnp.abs(x)))
    return jnp.maximum(x, 0.0) + sp, jnp.maximum(-x, 0.0) + sp


def _rmsnorm(x, g):
    return x * lax.rsqrt(jnp.mean(x * x, axis=-1, keepdims=True) + NORM_EPS) * g


def _iota2(shape, dim):
    return lax.broadcasted_iota(jnp.int32, shape, dim)


def _div64(i):
    return lax.shift_right_logical(i, jnp.full(i.shape, 6, jnp.int32))


def _each(f, *lists):
    return [f(*a) for a in zip(*lists)]


@jax.custom_vjp
def _inv_unit_lower(ms):
    n = ms[0].shape[0]
    eye = (_iota2((n, n), 0) == _iota2((n, n), 1)).astype(F32)
    rs = [eye - m for m in ms]
    ps = ms
    for _ in range(5):
        ps = _each(mm3_nn, ps, ps)
        rs = _each(lambda r, p: r + mm_nn(r, p), rs, ps)
    return rs


def _inv_fwd(ms):
    rs = _inv_unit_lower(ms)
    return rs, rs


def _inv_bwd(rs, gs):
    ts = _each(mm_tn, rs, gs)
    return (_each(lambda t, r: -mm_nt(t, r), ts, rs),)


_inv_unit_lower.defvjp(_inv_fwd, _inv_bwd)


def _dn_block(cq, ck, cv, bcol, acol, zt, alog, dtb, gn, s0):
    n = SUPER
    h = DN_CHUNK
    row = _iota2((n, n), 0)
    col = _iota2((n, n), 1)
    same = _div64(row) == _div64(col)
    incl = jnp.logical_and(same, row >= col)
    strict = jnp.logical_and(same, row > col)
    incl_f = incl.astype(F32)

    qn = _each(lambda x: x * lax.rsqrt(jnp.sum(x * x, axis=-1, keepdims=True) + NORM_EPS) * (D_HEAD ** -0.5), cq)
    kn = _each(lambda x: x * lax.rsqrt(jnp.sum(x * x, axis=-1, keepdims=True) + NORM_EPS), ck)
    beta = _each(_sigmoid, bcol)
    g = _each(lambda al, ac, dt: -(jnp.exp(al) * _softplus_parts(ac + dt)[0]), alog, acol, dtb)
    gcum = _each(lambda x: _sel_dot(incl_f, jnp.broadcast_to(x, (n, n))), g)
    gam_incl = _each(lambda x: jnp.where(incl, jnp.exp(jnp.where(incl, x - x.T, 0.0)), 0.0), gcum)
    kk = _each(mm_nt, kn, kn)
    t_inv = _inv_unit_lower(_each(lambda b, x, gm: b * x * jnp.where(strict, gm, 0.0), beta, kk, gam_incl))
    eg = _each(jnp.exp, gcum)
    u = _each(lambda t, v, b: mm_nn(t, v * b), t_inv, cv, beta)
    w = _each(lambda t, k, b, e: mm_nn(t, k * (b * e)), t_inv, kn, beta, eg)
    a_intra = _each(lambda q, k, gm: mm_nt(q, k) * gm, qn, kn, gam_incl)
    q_dec = _each(lambda q, e: q * e, qn, eg)
    last0 = _each(lambda x: x[h - 1:h, :], gcum)
    last1 = _each(lambda x: x[n - 1:n, :], gcum)
    k_dec = _each(lambda k, x, l0, l1: k * jnp.exp(jnp.concatenate(
        [jnp.broadcast_to(l0, (h, n)), jnp.broadcast_to(l1, (h, n))], axis=0) - x), kn, gcum, last0, last1)
    v0 = _each(lambda uu, ww, s: uu[:h] - mm_nn(ww[:h], s), u, w, s0)
    o0 = _each(lambda q, s: mm_nn(q[:h], s), q_dec, s0)
    s1 = _each(lambda s, l0, k, v: s * jnp.exp(l0) + mm_tn(k[:h], v), s0, last0, k_dec, v0)
    v1 = _each(lambda uu, ww, s: uu[h:] - mm_nn(ww[h:], s), u, w, s1)
    o1 = _each(lambda q, s: mm_nn(q[h:], s), q_dec, s1)
    s2 = _each(lambda s, l1, k, v: s * jnp.exp(l1) + mm_tn(k[h:], v), s1, last1, k_dec, v1)
    o = _each(lambda a, b, am, x, y: jnp.concatenate([a, b], axis=0) + mm_nn(am, jnp.concatenate([x, y], axis=0)),
              o0, o1, a_intra, v0, v1)
    out = _each(lambda x, z: _rmsnorm(x, gn) * _silu(z), o, zt)
    return out, s2


def _mem_fn(mq, mz, mkv):
    mk = mkv[:, :MEM_W]
    mv = mkv[:, MEM_W:]
    lane = _iota2((1, MEM_W), 1)
    out = jnp.zeros(mq.shape, F32)
    for hd in range(MEM_HEADS):
        hm = (_div64(lane) == hd).astype(F32)
        s = mm_nt(mq * hm, mk) * (1.0 / math.sqrt(MEM_DH))
        s = s - jnp.max(s, axis=-1, keepdims=True)
        e = jnp.exp(s)
        p = e / jnp.sum(e, axis=-1, keepdims=True)
        out = out + mm_nn(p, mv) * hm
    return out * _silu(mz)


def _merge_fn(gd, gs, gm, yd, ys, ym):
    return _sigmoid(gd) * yd + _sigmoid(gs) * ys + _sigmoid(gm) * ym


def _loss_fn(x, mo, fg, tgt):
    y = _rmsnorm(x + mo, fg)
    err = y - tgt
    return 0.5 * jnp.sum(jnp.mean(err * err, axis=-1, keepdims=True), axis=0, keepdims=True)


def _matmul(a, b, mode, out_dtype, tm, tn, tk, name, b_col0=0, n_cols=None):
    if mode == "nn":
        m, kdim = a.shape
        n = b.shape[1] if n_cols is None else n_cols
    elif mode == "nt":
        m, kdim = a.shape
        n = b.shape[0]
    else:
        kdim, m = a.shape
        n = b.shape[1] if n_cols is None else n_cols
    tm, tn, tk = min(tm, m), min(tn, n), min(tk, kdim)
    assert m % tm == 0 and n % tn == 0 and kdim % tk == 0 and b_col0 % tn == 0
    nk = kdim // tk
    jb = b_col0 // tn
    dims = {"nn": NN, "nt": NT, "tn": TN}[mode]

    def body(a_ref, b_ref, o_ref, acc_ref):
        k = pl.program_id(2)
        part = _dot(a_ref[...], b_ref[...], dims)

        @pl.when(k == 0)
        def _():
            acc_ref[...] = part

        @pl.when(k > 0)
        def _():
            acc_ref[...] += part

        @pl.when(k == nk - 1)
        def _():
            o_ref[...] = acc_ref[...].astype(o_ref.dtype)

    if mode == "nn":
        a_spec = pl.BlockSpec((tm, tk), lambda i, j, k: (i, k))
        b_spec = pl.BlockSpec((tk, tn), lambda i, j, k: (k, j + jb))
    elif mode == "nt":
        a_spec = pl.BlockSpec((tm, tk), lambda i, j, k: (i, k))
        b_spec = pl.BlockSpec((tn, tk), lambda i, j, k: (j, k))
    else:
        a_spec = pl.BlockSpec((tk, tm), lambda i, j, k: (k, i))
        b_spec = pl.BlockSpec((tk, tn), lambda i, j, k: (k, j + jb))
    return pl.pallas_call(
        body,
        name=name,
        grid=(m // tm, n // tn, nk),
        in_specs=[a_spec, b_spec],
        out_specs=pl.BlockSpec((tm, tn), lambda i, j, k: (i, j)),
        out_shape=jax.ShapeDtypeStruct((m, n), out_dtype),
        scratch_shapes=[pltpu.VMEM((tm, tn), F32)],
        compiler_params=pltpu.CompilerParams(dimension_semantics=("parallel", "parallel", "arbitrary")),
    )(a, b)


def _norm_in(x, g, tm=256):
    t = x.shape[0]

    def body(x_ref, g_ref, h_ref):
        h_ref[...] = _rmsnorm(x_ref[...], g_ref[...]).astype(BF16)

    return pl.pallas_call(
        body,
        name="norm_in",
        grid=(t // tm,),
        in_specs=[pl.BlockSpec((tm, D_MODEL), lambda i: (i, 0)), pl.BlockSpec((1, D_MODEL), lambda i: (0, 0))],
        out_specs=pl.BlockSpec((tm, D_MODEL), lambda i: (i, 0)),
        out_shape=jax.ShapeDtypeStruct((t, D_MODEL), BF16),
    )(x, g)


def _norm_in_bwd(x, g, dh, dres, tm=256):
    t = x.shape[0]

    def body(x_ref, g_ref, dh_ref, dres_ref, dx_ref, dg_ref):
        _, vjp = jax.vjp(_rmsnorm, x_ref[...], g_ref[...])
        dx, dg = vjp(dh_ref[...])
        dx_ref[...] = dx + dres_ref[...]

        @pl.when(pl.program_id(0) == 0)
        def _():
            dg_ref[...] = jnp.zeros_like(dg_ref)

        dg_ref[...] += dg

    row = pl.BlockSpec((tm, D_MODEL), lambda i: (i, 0))
    vec = pl.BlockSpec((1, D_MODEL), lambda i: (0, 0))
    return pl.pallas_call(
        body,
        name="norm_in_bwd",
        grid=(t // tm,),
        in_specs=[row, vec, row, row],
        out_specs=[row, vec],
        out_shape=[jax.ShapeDtypeStruct((t, D_MODEL), F32), jax.ShapeDtypeStruct((1, D_MODEL), F32)],
    )(x, g, dh, dres)


def _merge(proj, yd, ys, ym, tm=256, tc=512):
    t = proj.shape[0]
    g0 = O_GATES // tc
    gstep = D_MODEL // tc

    def body(gd, gs, gm, yd_ref, ys_ref, ym_ref, o_ref):
        o_ref[...] = _merge_fn(gd[...], gs[...], gm[...], yd_ref[...], ys_ref[...], ym_ref[...]).astype(BF16)

    def gate(k):
        return pl.BlockSpec((tm, tc), lambda i, j: (i, g0 + k * gstep + j))

    blk = pl.BlockSpec((tm, tc), lambda i, j: (i, j))
    return pl.pallas_call(
        body,
        name="merge",
        grid=(t // tm, D_MODEL // tc),
        in_specs=[gate(0), gate(1), gate(2), blk, blk, blk],
        out_specs=blk,
        out_shape=jax.ShapeDtypeStruct((t, D_MODEL), BF16),
    )(proj, proj, proj, yd, ys, ym)


def _merge_bwd(proj, yd, ys, ym, dmerged, tm=256, tc=512):
    t = proj.shape[0]
    g0 = O_GATES // tc
    gstep = D_MODEL // tc

    def body(gd, gs, gm, yd_ref, ys_ref, ym_ref, dm_ref, dyd, dys, dym, dgd, dgs, dgm):
        _, vjp = jax.vjp(_merge_fn, gd[...], gs[...], gm[...], yd_ref[...], ys_ref[...], ym_ref[...])
        outs = vjp(dm_ref[...])
        for ref, val in zip((dgd, dgs, dgm, dyd, dys, dym), outs):
            ref[...] = val.astype(BF16)

    def gate(k):
        return pl.BlockSpec((tm, tc), lambda i, j: (i, g0 + k * gstep + j))

    blk = pl.BlockSpec((tm, tc), lambda i, j: (i, j))
    o = jax.ShapeDtypeStruct((t, D_MODEL), BF16)
    return pl.pallas_call(
        body,
        name="merge_bwd",
        grid=(t // tm, D_MODEL // tc),
        in_specs=[gate(0), gate(1), gate(2), blk, blk, blk, blk],
        out_specs=[blk] * 6,
        out_shape=[o] * 6,
    )(proj, proj, proj, yd, ys, ym, dmerged)


def _loss_head(x, mo, fg, tgt, tm=256):
    t = x.shape[0]

    def body(x_ref, mo_ref, fg_ref, t_ref, loss_ref, dout_ref, dfg_ref):
        loss, vjp = jax.vjp(_loss_fn, x_ref[...], mo_ref[...], fg_ref[...], t_ref[...])
        _, dmo, dfg, _ = vjp(jnp.ones((1, 1), F32))

        @pl.when(pl.program_id(0) == 0)
        def _():
            loss_ref[...] = jnp.zeros_like(loss_ref)
            dfg_ref[...] = jnp.zeros_like(dfg_ref)

        loss_ref[...] += jnp.broadcast_to(loss, loss_ref.shape)
        dfg_ref[...] += dfg
        dout_ref[...] = dmo

    row = pl.BlockSpec((tm, D_MODEL), lambda i: (i, 0))
    vec = pl.BlockSpec((1, D_MODEL), lambda i: (0, 0))
    return pl.pallas_call(
        body,
        name="loss_head",
        grid=(t // tm,),
        in_specs=[row, row, vec, row],
        out_specs=[pl.BlockSpec((1, LANE), lambda i: (0, 0)), row, vec],
        out_shape=[jax.ShapeDtypeStruct((1, LANE), F32), jax.ShapeDtypeStruct((t, D_MODEL), F32),
                   jax.ShapeDtypeStruct((1, D_MODEL), F32)],
    )(x, mo, fg, tgt)


def _block_tail(proj, o_dn, o_sb, o_m, x, tgt, w_br_dn, w_br_sb, w_br_mem, w_out, fg, tm=256):
    t = x.shape[0]
    tm = min(tm, t)
    gw = 512
    n_g = 3 * D_MODEL // gw

    def body(*refs):
        g_refs = refs[:n_g]
        (odn_ref, osb_ref, om_ref, x_ref, t_ref, wdn_ref, wsb_ref, wm_ref, wo_ref, fg_ref, loss_ref, dout_ref, dfg_ref,
         mg_ref, dyd_ref, dys_ref, dym_ref, dg_ref, dod_ref, dos_ref, dom_ref) = refs[n_g:]
        y = [_dot(odn_ref[...], wdn_ref[...], NN), _dot(osb_ref[...], wsb_ref[...], NN),
             _dot(om_ref[...], wm_ref[...], NN)]
        s = [_sigmoid(jnp.concatenate([g_refs[2 * k][...], g_refs[2 * k + 1][...]], axis=1)) for k in range(3)]
        merged16 = (s[0] * y[0] + s[1] * y[1] + s[2] * y[2]).astype(BF16)
        mg_ref[...] = merged16
        mo = _dot(merged16, wo_ref[...], NN)
        loss, vjp = jax.vjp(_loss_fn, x_ref[...], mo, fg_ref[...], t_ref[...])
        _, dout, dfg, _ = vjp(jnp.ones((1, 1), F32))

        @pl.when(pl.program_id(0) == 0)
        def _():
            loss_ref[...] = jnp.zeros_like(loss_ref)
            dfg_ref[...] = jnp.zeros_like(dfg_ref)

        loss_ref[...] += jnp.broadcast_to(loss, loss_ref.shape)
        dfg_ref[...] += dfg
        dout_ref[...] = dout
        dmerged = _dot(dout, wo_ref[...], NT)
        dy = [(sk * dmerged).astype(BF16) for sk in s]
        dyd_ref[...], dys_ref[...], dym_ref[...] = dy
        dg_ref[...] = jnp.concatenate([dmerged * yk * (sk * (1.0 - sk)) for yk, sk in zip(y, s)], axis=1).astype(BF16)
        dod_ref[...] = _dot(dy[0], wdn_ref[...], NT).astype(BF16)
        dos_ref[...] = _dot(dy[1], wsb_ref[...], NT).astype(BF16)
        dom_ref[...] = _dot(dy[2], wm_ref[...], NT).astype(BF16)

    gates = [pl.BlockSpec((tm, gw), lambda i, j=j: (i, O_GATES // gw + j)) for j in range(n_g)]
    row = pl.BlockSpec((tm, D_MODEL), lambda i: (i, 0))
    rowm = pl.BlockSpec((tm, MEM_W), lambda i: (i, 0))
    vec = pl.BlockSpec((1, D_MODEL), lambda i: (0, 0))

    def whole(a):
        return pl.BlockSpec(a.shape, lambda i: (0, 0), pipeline_mode=pl.Buffered(1))

    def bf(c):
        return jax.ShapeDtypeStruct((t, c), BF16)

    return pl.pallas_call(
        body,
        name="block_tail",
        grid=(t // tm,),
        in_specs=gates + [row, row, rowm, row, row, whole(w_br_dn), whole(w_br_sb), whole(w_br_mem), whole(w_out), vec],
        out_specs=[pl.BlockSpec((1, LANE), lambda i: (0, 0)), row, vec, row, row, row, row,
                   pl.BlockSpec((tm, 3 * D_MODEL), lambda i: (i, 0)), row, row, rowm],
        out_shape=[jax.ShapeDtypeStruct((1, LANE), F32), jax.ShapeDtypeStruct((t, D_MODEL), F32),
                   jax.ShapeDtypeStruct((1, D_MODEL), F32), bf(D_MODEL), bf(D_MODEL), bf(D_MODEL), bf(D_MODEL),
                   bf(3 * D_MODEL), bf(D_MODEL), bf(D_MODEL), bf(MEM_W)],
    )(*([proj] * n_g), o_dn, o_sb, o_m, x, tgt, w_br_dn, w_br_sb, w_br_mem, w_out, fg)


def _shift_rows(x, s):
    t = x.shape[0]
    if s == 0:
        return x
    rolled = pltpu.roll(x, s % t, 0)
    row = _iota2(x.shape, 0)
    keep = row >= s if s > 0 else row < t + s
    return jnp.where(keep, rolled, 0.0)


def _conv_pre(x, w):
    return sum(_shift_rows(x, CONV_K - 1 - j) * w[j:j + 1, :] for j in range(CONV_K))


CONV_TC = 256


def _dn_conv(proj, conv_w):
    t = proj.shape[0]
    nb = 3 * D_MODEL // CONV_TC

    def body(x_ref, w_ref, c_ref):
        c_ref[...] = _silu(_conv_pre(x_ref[...], w_ref[...]))

    return pl.pallas_call(
        body,
        name="dn_conv",
        grid=(nb,),
        in_specs=[pl.BlockSpec((t, CONV_TC), lambda j: (0, j)), pl.BlockSpec((CONV_K, CONV_TC), lambda j: (0, j))],
        out_specs=pl.BlockSpec((t, CONV_TC), lambda j: (0, j)),
        out_shape=jax.ShapeDtypeStruct((t, 3 * D_MODEL), F32),
    )(proj, conv_w)


def _dn_conv_bwd(proj, conv_w, dc, part):
    t = proj.shape[0]
    nb = D_MODEL // CONV_TC
    b0 = part * nb

    def body(x_ref, w_ref, dc_ref, dx_ref, dw_ref):
        x = x_ref[...]
        w = w_ref[...]
        pre = _conv_pre(x, w)
        sg = _sigmoid(pre)
        dpre = dc_ref[...] * (sg * (1.0 + pre * (1.0 - sg)))
        ahead = [_shift_rows(dpre, -(CONV_K - 1 - j)) for j in range(CONV_K)]
        dx_ref[...] = sum(a * w[j:j + 1, :] for j, a in enumerate(ahead)).astype(BF16)
        dw_ref[...] = jnp.concatenate([jnp.sum(a * x, axis=0, keepdims=True) for a in ahead], axis=0)

    blk = pl.BlockSpec((t, CONV_TC), lambda j: (0, j))
    return pl.pallas_call(
        body,
        name=f"dn_conv_bwd{part}",
        grid=(nb,),
        in_specs=[pl.BlockSpec((t, CONV_TC), lambda j: (0, b0 + j)),
                  pl.BlockSpec((CONV_K, CONV_TC), lambda j: (0, b0 + j)), blk],
        out_specs=[blk, pl.BlockSpec((CONV_K, CONV_TC), lambda j: (0, j))],
        out_shape=[jax.ShapeDtypeStruct((t, D_MODEL), BF16), jax.ShapeDtypeStruct((CONV_K, D_MODEL), F32)],
    )(proj, conv_w, dc)


def _ba_columns(ba, hd):
    lane = _iota2(ba.shape, 1)
    bcol = jnp.sum(jnp.where(lane == hd, ba, 0.0), axis=1, keepdims=True)
    acol = jnp.sum(jnp.where(lane == N_HEADS + hd, ba, 0.0), axis=1, keepdims=True)
    return bcol, acol


def _head_scalar(row, hd):
    lane = _iota2(row.shape, 1)
    return jnp.sum(jnp.where(lane == hd, row, 0.0), axis=1, keepdims=True)


DN_HP = 8


def _dn_inputs(cq, ck, cv, ba_ref, z_ref, alog_ref, dtb_ref, heads, lanes):
    ba = ba_ref[...]
    cols = [_ba_columns(ba, hd) for hd in heads]
    return ([cq[:, ln] for ln in lanes], [ck[:, ln] for ln in lanes], [cv[:, ln] for ln in lanes],
            [c[0] for c in cols], [c[1] for c in cols], [z_ref[:, ln] for ln in lanes],
            [_head_scalar(alog_ref[...], hd) for hd in heads], [_head_scalar(dtb_ref[...], hd) for hd in heads])


def _dn_specs(nblk, reverse):
    w = DN_HP * LANE
    nq = D_MODEL // w

    def row(i):
        return nblk - 1 - i if reverse else i

    def colblk(b0):
        return pl.BlockSpec((SUPER, w), lambda i, h: (row(i), b0 + h))

    ba = pl.BlockSpec((SUPER, LANE), lambda i, h: (row(i), O_BA // LANE))
    vec = pl.BlockSpec((1, LANE), lambda i, h: (0, 0))
    st = pl.BlockSpec((1, DN_HP, D_HEAD, D_HEAD), lambda i, h: (row(i), h, 0, 0))
    return colblk, nq, ba, vec, st


def _dn_fwd(c, proj, alog_row, dtb_row, gn):
    t = c.shape[0]
    nblk = t // SUPER
    colblk, nq, ba, vec, st = _dn_specs(nblk, False)

    def body(cq, ck, cv, ba_ref, z_ref, alog_ref, dtb_ref, gn_ref, o_ref, s_ref, state):
        @pl.when(jnp.logical_and(pl.program_id(0) == 0, pl.program_id(1) == 0))
        def _():
            state[...] = jnp.zeros_like(state)

        heads = [pl.program_id(1) * DN_HP + j for j in range(DN_HP)]
        lanes = [slice(j * LANE, (j + 1) * LANE) for j in range(DN_HP)]
        s0 = [state[hd] for hd in heads]
        outs, s2 = _dn_block(*_dn_inputs(cq, ck, cv, ba_ref, z_ref, alog_ref, dtb_ref, heads, lanes), gn_ref[...], s0)
        for j, (hd, ln) in enumerate(zip(heads, lanes)):
            s_ref[0, j] = s0[j]
            o_ref[:, ln] = outs[j].astype(BF16)
            state[hd] = s2[j]

    return pl.pallas_call(
        body,
        name="dn_fwd",
        grid=(nblk, N_HEADS // DN_HP),
        in_specs=[colblk(0), colblk(nq), colblk(2 * nq), ba, colblk(O_Z_DN // (DN_HP * LANE)), vec, vec, vec],
        out_specs=[colblk(0), st],
        out_shape=[jax.ShapeDtypeStruct((t, D_MODEL), BF16),
                   jax.ShapeDtypeStruct((nblk, N_HEADS, D_HEAD, D_HEAD), F32)],
        scratch_shapes=[pltpu.VMEM((N_HEADS, D_HEAD, D_HEAD), F32)],
    )(c, c, c, proj, proj, alog_row, dtb_row, gn)


def _dn_bwd(c, proj, alog_row, dtb_row, gn, states, do):
    t = c.shape[0]
    nblk = t // SUPER
    colblk, nq, ba, vec, st = _dn_specs(nblk, True)

    def body(cq, ck, cv, ba_ref, z_ref, alog_ref, dtb_ref, gn_ref, s_ref, do_ref,
             dq_ref, dk_ref, dv_ref, dz_ref, dba_ref, dsc_ref, dgn_ref, dstate):
        i = pl.program_id(0)
        hq = pl.program_id(1)

        @pl.when(jnp.logical_and(i == 0, hq == 0))
        def _():
            dstate[...] = jnp.zeros_like(dstate)
            dsc_ref[...] = jnp.zeros_like(dsc_ref)
            dgn_ref[...] = jnp.zeros_like(dgn_ref)

        @pl.when(hq == 0)
        def _():
            dba_ref[...] = jnp.zeros_like(dba_ref)

        lane = _iota2((SUPER, LANE), 1)
        lane1 = _iota2((1, LANE), 1)
        heads = [hq * DN_HP + j for j in range(DN_HP)]
        lanes = [slice(j * LANE, (j + 1) * LANE) for j in range(DN_HP)]
        ds_in = [dstate[hd] for hd in heads]
        s_in = [s_ref[0, j] for j in range(DN_HP)]
        _, vjp = jax.vjp(_dn_block, *_dn_inputs(cq, ck, cv, ba_ref, z_ref, alog_ref, dtb_ref, heads, lanes),
                         gn_ref[...], s_in)
        dq, dk, dv, dbc, dac, dz, dal, ddt, dgn, ds0 = vjp(([do_ref[:, ln].astype(F32) for ln in lanes], ds_in))
        dba = jnp.zeros((SUPER, LANE), F32)
        dal_row = jnp.zeros((1, LANE), F32)
        ddt_row = jnp.zeros((1, LANE), F32)
        for j, (hd, ln) in enumerate(zip(heads, lanes)):
            dq_ref[:, ln] = dq[j]
            dk_ref[:, ln] = dk[j]
            dv_ref[:, ln] = dv[j]
            dz_ref[:, ln] = dz[j].astype(BF16)
            dstate[hd] = ds0[j]
            dba = dba + jnp.where(lane == hd, dbc[j], 0.0) + jnp.where(lane == N_HEADS + hd, dac[j], 0.0)
            dal_row = dal_row + jnp.where(lane1 == hd, dal[j], 0.0)
            ddt_row = ddt_row + jnp.where(lane1 == hd, ddt[j], 0.0)
        dba_ref[...] += dba
        dsc_ref[0:1, :] += dal_row
        dsc_ref[1:2, :] += ddt_row
        dgn_ref[...] += dgn

    outs = pl.pallas_call(
        body,
        name="dn_bwd",
        grid=(nblk, N_HEADS // DN_HP),
        in_specs=[colblk(0), colblk(nq), colblk(2 * nq), ba, colblk(O_Z_DN // (DN_HP * LANE)), vec, vec, vec, st,
                  colblk(0)],
        out_specs=[colblk(0), colblk(0), colblk(0), colblk(0),
                   pl.BlockSpec((SUPER, LANE), lambda i, h: (nblk - 1 - i, 0)),
                   pl.BlockSpec((2, LANE), lambda i, h: (0, 0)), vec],
        out_shape=[jax.ShapeDtypeStruct((t, D_MODEL), F32)] * 3
        + [jax.ShapeDtypeStruct((t, D_MODEL), BF16), jax.ShapeDtypeStruct((t, LANE), F32),
           jax.ShapeDtypeStruct((2, LANE), F32), jax.ShapeDtypeStruct((1, LANE), F32)],
        scratch_shapes=[pltpu.VMEM((N_HEADS, D_HEAD, D_HEAD), F32)],
    )(c, c, c, proj, proj, alog_row, dtb_row, gn, states, do)
    return outs


SB_TQ = 256
SB_TK = 256
SB_HP_FWD = 8
SB_HP_BWD = 4


def _sb_logits(z, mask):
    sp = jnp.log(1.0 + jnp.exp(-jnp.abs(z)))
    lf_raw = -(jnp.maximum(z, 0.0) + sp)
    lb = lf_raw + z
    lf = lf_raw if mask is None else jnp.where(mask, lf_raw, 0.0)
    return lb, lf_raw, lf


def _suffix_sums(x, sel):
    hi, lo = _split2(x)
    d = functools.partial(lax.dot_general, dimension_numbers=NN, preferred_element_type=F32)
    return d(hi, sel) + d(lo, sel)


def _sb_diag_mask(tq, r):
    return r * SB_TK + _iota2((tq, SB_TK), 1) < _iota2((tq, SB_TK), 0)


def _sb_specs(t, tq, hp):
    w = hp * LANE
    q0, k0, v0, z0 = (O_QKV_SB // w, (O_QKV_SB + D_MODEL) // w, (O_QKV_SB + 2 * D_MODEL) // w, O_Z_SB // w)

    def blk(b0):
        return pl.BlockSpec((tq, w), lambda h, i: (i, b0 + h))

    def full(b0, **kw):
        return pl.BlockSpec((t, w), lambda h, i: (0, b0 + h), **kw)

    once = dict(pipeline_mode=pl.Buffered(1))
    return blk(q0), full(k0, **once), full(v0, **once), blk(z0), blk(0), full(0)


def _sb_fwd(proj):
    t = proj.shape[0]
    tq = min(SB_TQ, t)
    ndiag = tq // SB_TK
    scale = 1.0 / math.sqrt(D_HEAD)

    def body(q_ref, k_ref, v_ref, z_ref, o_ref, oraw_ref):
        qi = pl.program_id(1)
        lanes = [slice(hd * LANE, (hd + 1) * LANE) for hd in range(SB_HP_FWD)]
        qs = [(q_ref[:, ln] * scale).astype(BF16) for ln in lanes]
        after = (_iota2((SB_TK, SB_TK), 0) > _iota2((SB_TK, SB_TK), 1)).astype(BF16)
        oraw_ref[...] = jnp.zeros_like(oraw_ref)

        def block(kb, mask, c_lf):
            rows = pl.ds(pl.multiple_of(kb * SB_TK, SB_TK), SB_TK)
            z = _each(lambda q, ln: _dot(q, k_ref[rows, ln], NT), qs, lanes)
            lg = _each(lambda x: _sb_logits(x, mask), z)
            surv = _each(lambda x: _suffix_sums(x[2], after), lg)
            att = _each(lambda x, s, c: jnp.exp(x[0] + s + c), lg, surv, c_lf)
            if mask is not None:
                att = _each(lambda a: jnp.where(mask, a, 0.0), att)
            pv = _each(lambda a, ln: _dot(a, v_ref[rows, ln], NN), att, lanes)
            for p, ln in zip(pv, lanes):
                oraw_ref[:, ln] += p
            return tuple(_each(lambda c, x: c + jnp.sum(x[2], axis=1, keepdims=True), c_lf, lg))

        carry = tuple(jnp.zeros((tq, 1), F32) for _ in range(SB_HP_FWD))
        for r in reversed(range(ndiag)):
            carry = block(qi * ndiag + r, _sb_diag_mask(tq, r), carry)
        lax.fori_loop(0, qi * ndiag, lambda i, c: block(qi * ndiag - 1 - i, None, c), carry)
        o_ref[...] = (oraw_ref[...] * _silu(z_ref[...])).astype(BF16)

    q_spec, k_spec, v_spec, z_spec, out, _ = _sb_specs(t, tq, SB_HP_FWD)
    return pl.pallas_call(
        body,
        name="sb_fwd",
        grid=(N_HEADS // SB_HP_FWD, t // tq),
        in_specs=[q_spec, k_spec, v_spec, z_spec],
        out_specs=[out, out],
        out_shape=[jax.ShapeDtypeStruct((t, D_MODEL), BF16), jax.ShapeDtypeStruct((t, D_MODEL), F32)],
    )(proj, proj, proj, proj)


def _sb_bwd(proj, oraw, do, blocks):
    t = proj.shape[0]
    tq = min(SB_TQ, t)
    ndiag = tq // SB_TK
    scale = 1.0 / math.sqrt(D_HEAD)

    def body(q_ref, k_ref, v_ref, z_ref, oraw_ref, do_ref, blk_ref, dq_ref, dk_ref, dv_ref, dz_ref, land_ref,
             dk_acc, dv_acc, p_scr, z_scr, send_sems, recv_sems, local_sem):
        qi = pl.program_id(1)
        nq = pl.num_programs(1)
        hg = pl.program_id(0)
        me = _position()
        mine = 4 * me[0] + 2 * me[1] + me[2]

        def exchange():
            cps = [pltpu.make_async_copy(blk_ref.at[mine], land_ref.at[mine], local_sem)]
            for k, peer in enumerate(_other_devices(me)):
                cps.append(pltpu.make_async_remote_copy(
                    src_ref=blk_ref.at[4 * peer[0] + 2 * peer[1] + peer[2]], dst_ref=land_ref.at[mine],
                    send_sem=send_sems.at[k], recv_sem=recv_sems.at[k], device_id=peer, device_id_type=MESH))
            return cps

        @pl.when(jnp.logical_and(hg == 0, qi == 0))
        def _():
            for cp in exchange():
                cp.start()

        @pl.when(qi == 0)
        def _():
            dk_acc[...] = jnp.zeros_like(dk_acc)
            dv_acc[...] = jnp.zeros_like(dv_acc)

        heads = range(SB_HP_BWD)
        lanes = [slice(hd * LANE, (hd + 1) * LANE) for hd in heads]
        zg = z_ref[...]
        sg = _sigmoid(zg)
        dog = do_ref[...].astype(F32)
        dz_ref[...] = (dog * oraw_ref[...] * (sg * (1.0 + zg * (1.0 - sg)))).astype(BF16)
        d_o = (dog * (zg * sg)).astype(BF16)
        d_o16 = [d_o[:, ln] for ln in lanes]
        qs = [(q_ref[:, ln] * scale).astype(BF16) for ln in lanes]
        ri = _iota2((SB_TK, SB_TK), 0)
        ci = _iota2((SB_TK, SB_TK), 1)
        after = (ri > ci).astype(BF16)
        earlier = (ri < ci).astype(BF16)

        def rows_of(kb):
            return pl.ds(pl.multiple_of(kb * SB_TK, SB_TK), SB_TK)

        def down(kb, mask, c_lf):
            rows = rows_of(kb)
            z = _each(lambda q, ln: _dot(q, k_ref[rows, ln], NT), qs, lanes)
            da = _each(lambda d, ln: _dot(d, v_ref[rows, ln], NT), d_o16, lanes)
            lg = _each(lambda x: _sb_logits(x, mask), z)
            surv = _each(lambda x: _suffix_sums(x[2], after), lg)
            att = _each(lambda x, s, c: jnp.exp(x[0] + s + c), lg, surv, c_lf)
            if mask is not None:
                att = _each(lambda a: jnp.where(mask, a, 0.0), att)
            dv = _each(lambda a, d: _dot(a, d, TN), att, d_o16)
            for hd in heads:
                p_scr[hd, kb] = att[hd] * da[hd]
                z_scr[hd, kb] = z[hd]
                dv_acc[rows, lanes[hd]] += dv[hd]
            return tuple(_each(lambda c, x: c + jnp.sum(x[2], axis=1, keepdims=True), c_lf, lg))

        c_lf = tuple(jnp.zeros((tq, 1), F32) for _ in heads)
        for r in reversed(range(ndiag)):
            c_lf = down(qi * ndiag + r, _sb_diag_mask(tq, r), c_lf)
        lax.fori_loop(0, qi * ndiag, lambda i, c: down(qi * ndiag - 1 - i, None, c), c_lf)

        def up(kb, mask, carry):
            dq, c_p = carry
            rows = rows_of(kb)
            p = [p_scr[hd, kb] for hd in heads]
            zs = [z_scr[hd, kb] for hd in heads]
            before = _each(lambda x, c: _suffix_sums(x, earlier) + c, p, c_p)
            e = _each(lambda x: jnp.exp(-jnp.abs(x)), zs)
            r = _each(lambda x: 1.0 / (1.0 + x), e)
            sig = _each(lambda x, a, b: jnp.where(x >= 0.0, b, a * b), zs, e, r)
            oms = _each(lambda x, a, b: jnp.where(x >= 0.0, a * b, b), zs, e, r)
            if mask is not None:
                sig = _each(lambda a: jnp.where(mask, a, 0.0), sig)
            dzz = _each(lambda x, o, g, b: x * o - g * b, p, oms, sig, before)
            dk = _each(lambda x, q: _dot(x, q, TN), dzz, qs)
            dq = _each(lambda a, x, ln: a + _dot(x, k_ref[rows, ln], NN), dq, dzz, lanes)
            for hd in heads:
                dk_acc[rows, lanes[hd]] += dk[hd]
            return tuple(dq), tuple(_each(lambda c, x: c + jnp.sum(x, axis=1, keepdims=True), c_p, p))

        carry = (tuple(jnp.zeros((tq, D_HEAD), F32) for _ in heads), tuple(jnp.zeros((tq, 1), F32) for _ in heads))
        carry = lax.fori_loop(0, qi * ndiag, lambda kb, c: up(kb, None, c), carry)
        for r in range(ndiag):
            carry = up(qi * ndiag + r, _sb_diag_mask(tq, r), carry)
        dq = carry[0]
        for hd in heads:
            dq_ref[:, lanes[hd]] = (dq[hd] * scale).astype(BF16)

        @pl.when(qi == nq - 1)
        def _():
            dk_ref[...] = dk_acc[...].astype(BF16)
            dv_ref[...] = dv_acc[...].astype(BF16)

        @pl.when(jnp.logical_and(hg == pl.num_programs(0) - 1, qi == nq - 1))
        def _():
            for cp in exchange():
                cp.wait()

    q_spec, k_spec, v_spec, z_spec, blk, full = _sb_specs(t, tq, SB_HP_BWD)
    o = jax.ShapeDtypeStruct((t, D_MODEL), BF16)
    w = SB_HP_BWD * LANE
    return pl.pallas_call(
        body,
        name="sb_bwd",
        grid=(N_HEADS // SB_HP_BWD, t // tq),
        in_specs=[q_spec, k_spec, v_spec, z_spec, blk, blk, ANY],
        out_specs=[blk, full, full, blk, ANY],
        out_shape=[o, o, o, o, jax.ShapeDtypeStruct(blocks.shape, blocks.dtype)],
        scratch_shapes=[pltpu.VMEM((t, w), F32), pltpu.VMEM((t, w), F32)]
        + [pltpu.VMEM((SB_HP_BWD, t // SB_TK, tq, SB_TK), F32)] * 2
        + [pltpu.SemaphoreType.DMA((N_DEV - 1,)), pltpu.SemaphoreType.DMA((N_DEV - 1,)), pltpu.SemaphoreType.DMA],
    )(proj, proj, proj, proj, oraw, do, blocks)


def _mem_kv_fn(mem, mg, w):
    return mm_nn(_rmsnorm(mem, mg), w)


def _mem_kv(mem, mg, w):
    def body(m_ref, g_ref, w_ref, o_ref):
        o_ref[...] = _mem_kv_fn(m_ref[...], g_ref[...], w_ref[...])

    return pl.pallas_call(body, name="mem_kv", out_shape=jax.ShapeDtypeStruct((MEM_LEN, 2 * MEM_W), F32))(mem, mg, w)


def _mem_kv_bwd(mem, mg, w, dmkv):
    def body(m_ref, g_ref, w_ref, d_ref, dg_ref, dw_ref):
        _, vjp = jax.vjp(_mem_kv_fn, m_ref[...], g_ref[...], w_ref[...].astype(F32))
        _, dg, dw = vjp(d_ref[...])
        dg_ref[...] = dg
        dw_ref[...] = dw.astype(BF16)

    return pl.pallas_call(
        body, name="mem_kv_bwd",
        out_shape=[jax.ShapeDtypeStruct((1, D_MODEL), F32), jax.ShapeDtypeStruct((D_MODEL, 2 * MEM_W), BF16)],
    )(mem, mg, w, dmkv)


def _mem_attn(proj, mkv, tm=256):
    t = proj.shape[0]
    tm = min(tm, t)

    def body(q_ref, z_ref, kv_ref, o_ref):
        o_ref[...] = _mem_fn(q_ref[...], z_ref[...], kv_ref[...]).astype(BF16)

    return pl.pallas_call(
        body,
        name="mem_attn",
        grid=(t // tm,),
        in_specs=[pl.BlockSpec((tm, MEM_W), lambda i: (i, O_MQ // MEM_W)),
                  pl.BlockSpec((tm, MEM_W), lambda i: (i, O_MZ // MEM_W)),
                  pl.BlockSpec((MEM_LEN, 2 * MEM_W), lambda i: (0, 0))],
        out_specs=pl.BlockSpec((tm, MEM_W), lambda i: (i, 0)),
        out_shape=jax.ShapeDtypeStruct((t, MEM_W), BF16),
    )(proj, proj, mkv)


def _mem_attn_bwd(proj, mkv, do, tm=256):
    t = proj.shape[0]
    tm = min(tm, t)

    def body(q_ref, z_ref, kv_ref, do_ref, dq_ref, dz_ref, dkv_ref):
        _, vjp = jax.vjp(_mem_fn, q_ref[...], z_ref[...], kv_ref[...])
        dq, dz, dkv = vjp(do_ref[...].astype(F32))
        dq_ref[...] = dq.astype(BF16)
        dz_ref[...] = dz.astype(BF16)

        @pl.when(pl.program_id(0) == 0)
        def _():
            dkv_ref[...] = jnp.zeros_like(dkv_ref)

        dkv_ref[...] += dkv

    blk = pl.BlockSpec((tm, MEM_W), lambda i: (i, 0))
    kv = pl.BlockSpec((MEM_LEN, 2 * MEM_W), lambda i: (0, 0))
    return pl.pallas_call(
        body,
        name="mem_attn_bwd",
        grid=(t // tm,),
        in_specs=[pl.BlockSpec((tm, MEM_W), lambda i: (i, O_MQ // MEM_W)),
                  pl.BlockSpec((tm, MEM_W), lambda i: (i, O_MZ // MEM_W)), kv, blk],
        out_specs=[blk, blk, kv],
        out_shape=[jax.ShapeDtypeStruct((t, MEM_W), BF16), jax.ShapeDtypeStruct((t, MEM_W), BF16),
                   jax.ShapeDtypeStruct((MEM_LEN, 2 * MEM_W), F32)],
    )(proj, proj, mkv, do)


def _proj_gather(h, w_alt, shards, tn=384):
    t = h.shape[0]
    n, kdim = w_alt.shape
    assert n % tn == 0
    nj = n // tn
    na = len(shards)

    def body(h_ref, w_ref, *rest):
        x_refs, o_ref, land = rest[:na], rest[na], rest[na + 1:2 * na + 1]
        send_sems, recv_sems, local_sems = rest[2 * na + 1:]
        j = pl.program_id(0)

        def copies():
            return _direct_gather_copies(x_refs, land, send_sems, recv_sems, local_sems)

        @pl.when(j == 0)
        def _():
            for cp in copies():
                cp.start()

        o_ref[...] = _dot(h_ref[...], w_ref[...], NT)

        @pl.when(j == nj - 1)
        def _():
            for cp in copies():
                cp.wait()

    outs = pl.pallas_call(
        body,
        name="proj",
        grid=(nj,),
        in_specs=[pl.BlockSpec((t, kdim), lambda j: (0, 0)), pl.BlockSpec((tn, kdim), lambda j: (j, 0))] + [ANY] * na,
        out_specs=[pl.BlockSpec((t, tn), lambda j: (0, j))] + [ANY] * na,
        out_shape=[jax.ShapeDtypeStruct((t, n), F32)]
        + [jax.ShapeDtypeStruct((N_DEV, *v.shape), v.dtype) for v in shards],
        scratch_shapes=_gather_sems(na),
    )(h, w_alt, *shards)
    return outs[0], outs[1:]


def _local_step(x, mem, tgt, norm_g, mem_norm_g, w_alt, alog_row, dtb_row, dn_norm_g, final_g, shards):
    h = _norm_in(x, norm_g)
    proj, (g_kv, g_dn, g_sb, g_out, g_mem, g_conv) = _proj_gather(h, w_alt, shards)
    w_mem_kv = g_kv.reshape(D_MODEL, 2 * MEM_W)
    w_br_dn = g_dn.reshape(D_MODEL, D_MODEL)
    w_br_sb = g_sb.reshape(D_MODEL, D_MODEL)
    w_out = g_out.reshape(D_MODEL, D_MODEL)
    w_br_mem = g_mem.transpose(1, 0, 2).reshape(MEM_W, D_MODEL)
    conv_w = g_conv.transpose(1, 0, 2).reshape(CONV_K, 3 * D_MODEL)

    c = _dn_conv(proj, conv_w)
    o_dn, states = _dn_fwd(c, proj, alog_row, dtb_row, dn_norm_g)
    o_sb, o_sb_raw = _sb_fwd(proj)
    mkv = _mem_kv(mem, mem_norm_g, w_mem_kv)
    o_m = _mem_attn(proj, mkv)

    (loss, dout, d_final_g, merged, dy_dn, dy_sb, dy_m, dgates, do_dn, do_sb, do_m) = _block_tail(
        proj, o_dn, o_sb, o_m, x, tgt, w_br_dn, w_br_sb, w_br_mem, w_out, final_g)
    dw_out = _matmul(merged, dout, "tn", BF16, 256, 1024, 2048, "dw_out")
    dw_br_dn = _matmul(o_dn, dy_dn, "tn", BF16, 256, 1024, 2048, "dw_br_dn")
    dw_br_sb = _matmul(o_sb, dy_sb, "tn", BF16, 256, 1024, 2048, "dw_br_sb")
    dw_br_mem = _matmul(o_m, dy_m, "tn", BF16, 256, 1024, 2048, "dw_br_mem")

    dmq, dmz, dmkv = _mem_attn_bwd(proj, mkv, do_m)
    d_mem_norm_g, dw_mem_kv = _mem_kv_bwd(mem, mem_norm_g, w_mem_kv, dmkv)
    rows_d = D_MODEL // N_DEV
    small_blocks = jnp.concatenate([
        dw_br_dn.reshape(N_DEV, rows_d, D_MODEL), dw_br_sb.reshape(N_DEV, rows_d, D_MODEL),
        dw_out.reshape(N_DEV, rows_d, D_MODEL), dw_mem_kv.reshape(N_DEV, rows_d // 2, D_MODEL),
        dw_br_mem.reshape(MEM_W, N_DEV, rows_d).transpose(1, 0, 2).reshape(N_DEV, MEM_W // N_DEV, D_MODEL)], axis=1)
    dq_sb, dk_sb, dv_sb, dz_sb, small_parts = _sb_bwd(proj, o_sb_raw, do_sb, small_blocks)
    (d_small,) = _sum_slots([small_parts], "sum_small_grads")
    dcq, dck, dcv, dz_dn, dba, dscal, d_dn_norm_g = _dn_bwd(c, proj, alog_row, dtb_row, dn_norm_g, states, do_dn)
    dq_dn, dcw_q = _dn_conv_bwd(proj, conv_w, dcq, 0)
    dk_dn, dcw_k = _dn_conv_bwd(proj, conv_w, dck, 1)
    dv_dn, dcw_v = _dn_conv_bwd(proj, conv_w, dcv, 2)
    d_conv_w = jnp.concatenate([dcw_q, dcw_k, dcw_v], axis=1)

    dproj = jnp.concatenate([dq_dn, dk_dn, dv_dn, dz_dn, dq_sb, dk_sb, dv_sb, dz_sb, dmq, dmz, dgates,
                             dba.astype(BF16)], axis=1)
    dh = _matmul(dproj, w_alt, "nn", F32, 512, 1024, 3968, "dh")
    dw_alt = _matmul(dproj, h, "tn", BF16, 384, 1024, 2048, "dw_alt")
    grad_x, d_norm_g = _norm_in_bwd(x, norm_g, dh, dout)
    return dict(loss=loss, grad_x=grad_x, norm_g=d_norm_g, mem_norm_g=d_mem_norm_g, w_alt=dw_alt, conv_w=d_conv_w,
                scal=dscal, dn_norm_g=d_dn_norm_g, small=d_small, final_g=d_final_g)


MESH = pl.DeviceIdType.MESH
ANY = pl.BlockSpec(memory_space=pl.ANY)


def _position():
    return lax.axis_index("x"), lax.axis_index("y"), lax.axis_index("c")


def _other_devices(me):
    return [tuple(1 - p if (f >> s) & 1 else p for p, s in zip(me, (2, 1, 0))) for f in range(1, N_DEV)]


def _direct_gather_copies(x_refs, land_refs, send_sems, recv_sems, local_sems):
    me = _position()
    mine = 4 * me[0] + 2 * me[1] + me[2]
    cps = []
    for a, (x_ref, land) in enumerate(zip(x_refs, land_refs)):
        cps.append(pltpu.make_async_copy(x_ref, land.at[mine], local_sems.at[a]))
        for k, peer in enumerate(_other_devices(me)):
            cps.append(pltpu.make_async_remote_copy(
                src_ref=x_ref, dst_ref=land.at[mine], send_sem=send_sems.at[7 * a + k],
                recv_sem=recv_sems.at[7 * a + k], device_id=peer, device_id_type=MESH))
    return cps


def _gather_sems(n):
    return [pltpu.SemaphoreType.DMA((7 * n,)), pltpu.SemaphoreType.DMA((7 * n,)), pltpu.SemaphoreType.DMA((n,))]


def _all_gather(xs, name):
    n = len(xs)

    def body(*refs):
        x_refs, o_refs = refs[:n], refs[n:2 * n]
        send_sems, recv_sems, local_sems = refs[2 * n:]
        x, y, c = _position()
        me, sibling = (x, y, c), (x, y, 1 - c)
        x_nbr, y_nbr, diag = (1 - x, y, c), (x, 1 - y, c), (1 - x, 1 - y, c)
        south = c == 0
        relay_from = tuple(jnp.where(south, a, b) for a, b in zip(y_nbr, x_nbr))
        relay_to = tuple(jnp.where(south, a, b) for a, b in zip(x_nbr, y_nbr))

        def slot(p):
            return 4 * p[0] + 2 * p[1] + p[2]

        def copy(a, k, block, to, src=None):
            dst = o_refs[a].at[slot(block)]
            return pltpu.make_async_remote_copy(
                src_ref=dst if src is None else src, dst_ref=dst, send_sem=send_sems.at[7 * a + k],
                recv_sem=recv_sems.at[7 * a + k], device_id=to, device_id_type=MESH)

        mine = [pltpu.make_async_copy(x_refs[a], o_refs[a].at[slot(me)], local_sems.at[a]) for a in range(n)]
        for cp in mine:
            cp.start()
        sends = []
        for a in range(n):
            sends += [copy(a, 0, me, sibling, src=x_refs[a]), copy(a, 1, me, x_nbr, src=x_refs[a]),
                      copy(a, 2, me, y_nbr, src=x_refs[a])]
        for cp in sends:
            cp.start()
        later = []
        for a in range(n):
            copy(a, 1, x_nbr, me).wait_recv()
            copy(a, 2, y_nbr, me).wait_recv()
            later += [copy(a, 3, relay_from, relay_to), copy(a, 4, x_nbr, sibling), copy(a, 5, y_nbr, sibling)]
            for cp in later[-3:]:
                cp.start()
        for a in range(n):
            copy(a, 3, diag, me).wait_recv()
            later.append(copy(a, 6, diag, sibling))
            later[-1].start()
        for a in range(n):
            copy(a, 0, sibling, me).wait_recv()
            for k, chip in ((4, x_nbr), (5, y_nbr), (6, diag)):
                copy(a, k, (chip[0], chip[1], 1 - c), me).wait_recv()
        for cp in sends + later:
            cp.wait_send()
        for cp in mine:
            cp.wait()

    return pl.pallas_call(
        body,
        name=name,
        in_specs=[ANY] * n,
        out_specs=[ANY] * n,
        out_shape=[jax.ShapeDtypeStruct((N_DEV, *v.shape), v.dtype) for v in xs],
        scratch_shapes=[pltpu.SemaphoreType.DMA((7 * n,)), pltpu.SemaphoreType.DMA((7 * n,)),
                        pltpu.SemaphoreType.DMA((n,))],
    )(*xs)


def _window_view(ref, dest):
    return ref.at[pl.ds(LANE * WIN_START[dest], WIN_W), :]


def _chunk_rows(rows, cols):
    return max(ch for ch in range(16, rows + 1, 16) if rows % ch == 0 and ch * cols <= (1 << 19))


def _halving_stage(xs, axis, name, out_dtype, windowed=(), gather=()):
    n_arr = len(xs)
    metas = []
    for k, v in enumerate(xs):
        if k in windowed:
            metas.append((N_DEV // 2, WIN_W, v.shape[1]))
        else:
            assert v.shape[1] == 2
            metas.append((v.shape[0], v.shape[2], v.shape[3]))
    chunk = [_chunk_rows(r, c) for (_, r, c) in metas]
    offs = [sum(m[0] for m in metas[:k]) for k in range(n_arr)]
    n_sem = sum(m[0] for m in metas)

    n_g = len(gather)

    def body(*refs):
        x_refs, g_refs = refs[:n_arr], refs[n_arr:n_arr + n_g]
        outs = refs[n_arr + n_g:]
        o_refs, land_refs, gl_refs = outs[:n_arr], outs[n_arr:2 * n_arr], outs[2 * n_arr:2 * n_arr + n_g]
        rest = outs[2 * n_arr + n_g:]
        bufs = rest[:3 * n_arr]
        send_sems, recv_sems, in_sems, out_sems = rest[3 * n_arr:3 * n_arr + 4]
        gathers = _direct_gather_copies(g_refs, gl_refs, *rest[3 * n_arr + 4:]) if n_g else []
        for cp in gathers:
            cp.start()
        pos = dict(zip("xyc", _position()))
        bit = pos[axis]
        peer = tuple(1 - pos[a] if a == axis else pos[a] for a in "xyc")

        def view(k, i, b):
            if k in windowed:
                return _window_view(x_refs[k], 2 * i + b)
            return x_refs[k].at[i, b]

        def add_blocks(k, a_view, b_view, o_view):
            _hbm_add(a_view, b_view, o_view, bufs[3 * k:3 * k + 3], in_sems, out_sems, chunk[k])

        for b in (0, 1):
            @pl.when(bit == b)
            def _(b=b):
                sends = []
                for k in range(n_arr):
                    for i in range(metas[k][0]):
                        cp = pltpu.make_async_remote_copy(
                            src_ref=view(k, i, 1 - b), dst_ref=land_refs[k].at[i], send_sem=send_sems.at[offs[k] + i],
                            recv_sem=recv_sems.at[offs[k] + i], device_id=peer, device_id_type=MESH)
                        cp.start()
                        sends.append(cp)
                idx = 0
                for k in range(n_arr):
                    for i in range(metas[k][0]):
                        sends[idx].wait_recv()
                        add_blocks(k, view(k, i, b), land_refs[k].at[i], o_refs[k].at[i])
                        idx += 1
                for cp in sends:
                    cp.wait_send()

        for cp in gathers:
            cp.wait()

    out_shape = [jax.ShapeDtypeStruct(m, out_dtype) for m in metas]
    land_shape = [jax.ShapeDtypeStruct(m, v.dtype) for m, v in zip(metas, xs)]
    g_shape = [jax.ShapeDtypeStruct((N_DEV, *v.shape), v.dtype) for v in gather]
    scratch = []
    for k in range(n_arr):
        blk = (2, chunk[k], metas[k][2])
        scratch += [pltpu.VMEM(blk, xs[k].dtype)] * 2 + [pltpu.VMEM(blk, out_dtype)]
    scratch += [pltpu.SemaphoreType.DMA((n_sem,)), pltpu.SemaphoreType.DMA((n_sem,)),
                pltpu.SemaphoreType.DMA((2, 2)), pltpu.SemaphoreType.DMA((2,))]
    if n_g:
        scratch += _gather_sems(n_g)
    outs = pl.pallas_call(
        body,
        name=name,
        in_specs=[ANY] * (n_arr + n_g),
        out_specs=[ANY] * (2 * n_arr + n_g),
        out_shape=out_shape + land_shape + g_shape,
        scratch_shapes=scratch,
    )(*xs, *gather)
    return outs[:n_arr], outs[2 * n_arr:]


def _hbm_add(a_view, b_view, o_view, bufs, in_sems, out_sems, ch):
    rows = a_view.shape[0]
    nch = rows // ch
    va, vb, vo = bufs

    def rows_of(j):
        return pl.ds(pl.multiple_of(j * ch, 16), ch)

    def loads(j, s):
        return (pltpu.make_async_copy(a_view.at[rows_of(j), :], va.at[s], in_sems.at[0, s]),
                pltpu.make_async_copy(b_view.at[rows_of(j), :], vb.at[s], in_sems.at[1, s]))

    def store(j, s):
        return pltpu.make_async_copy(vo.at[s], o_view.at[rows_of(j), :], out_sems.at[s])

    for cp in loads(0, 0):
        cp.start()

    def step(j, _):
        s = lax.rem(j, 2)

        @pl.when(j + 1 < nch)
        def _():
            for cp in loads(j + 1, 1 - s):
                cp.start()

        for cp in loads(j, s):
            cp.wait()

        @pl.when(j >= 2)
        def _():
            store(j - 2, s).wait()

        vo[s] = (va[s].astype(F32) + vb[s].astype(F32)).astype(vo.dtype)
        store(j, s).start()
        return 0

    lax.fori_loop(0, nch, step, 0)
    for j in range(max(0, nch - 2), nch):
        store(j, j % 2).wait()


def _xy_stage(xs, first, name):
    n_arr = len(xs)
    if first:
        shapes = [(v.shape[2] // 2, v.shape[3]) for v in xs]
        ins = list(xs)
    else:
        shapes = [(a.shape[1], a.shape[2]) for a, _ in xs]
        ins = [v for pair in xs for v in pair]
    n_blk = 2 if first else 1
    out_dtype = BF16 if first else F32
    chunk = [_chunk_rows(r, c) for (r, c) in shapes]
    n_sem = 2 * n_blk * n_arr

    def body(*refs):
        n_in = len(ins)
        in_refs = refs[:n_in]
        n_out = 2 * n_arr if first else n_arr
        o_refs = refs[n_in:n_in + n_out]
        land = refs[n_in + n_out:n_in + n_out + 2 * n_arr]
        rest = refs[n_in + n_out + 2 * n_arr:]
        bufs = rest[:3 * n_arr]
        send_sems, recv_sems, in_sems, out_sems = rest[3 * n_arr:]
        x, y, c = _position()
        peers = {"x": (1 - x, y, c), "y": (x, 1 - y, c)}
        jobs = []
        for k in range(n_arr):
            r, _ = shapes[k]
            half_a, half_b = pl.ds(0, r), pl.ds(r, r)
            if first:
                src = in_refs[k]
                for i in range(2):
                    jobs.append((k, src.at[i, 1 - y, half_a, :], src.at[i, y, half_a, :], land[2 * k].at[i],
                                 o_refs[2 * k].at[i], "y"))
                    jobs.append((k, src.at[1 - x, i, half_b, :], src.at[x, i, half_b, :], land[2 * k + 1].at[i],
                                 o_refs[2 * k + 1].at[i], "x"))
            else:
                a1, b1 = in_refs[2 * k], in_refs[2 * k + 1]
                jobs.append((k, a1.at[1 - x], a1.at[x], land[2 * k], o_refs[k].at[half_a, :], "x"))
                jobs.append((k, b1.at[1 - y], b1.at[y], land[2 * k + 1], o_refs[k].at[half_b, :], "y"))
        sends = []
        for n, (k, send, _, landing, _, axis) in enumerate(jobs):
            cp = pltpu.make_async_remote_copy(src_ref=send, dst_ref=landing, send_sem=send_sems.at[n],
                                              recv_sem=recv_sems.at[n], device_id=peers[axis], device_id_type=MESH)
            cp.start()
            sends.append(cp)
        for cp, (k, _, kept, landing, out, _) in zip(sends, jobs):
            cp.wait_recv()
            _hbm_add(kept, landing, out, bufs[3 * k:3 * k + 3], in_sems, out_sems, chunk[k])
        for cp in sends:
            cp.wait_send()

    if first:
        out_shape = [jax.ShapeDtypeStruct((2, r, c), BF16) for (r, c) in shapes for _ in range(2)]
        land_shape = out_shape
    else:
        out_shape = [jax.ShapeDtypeStruct((2 * r, c), F32) for (r, c) in shapes]
        land_shape = [jax.ShapeDtypeStruct((r, c), BF16) for (r, c) in shapes for _ in range(2)]
    scratch = []
    for k in range(n_arr):
        scratch += [pltpu.VMEM((2, chunk[k], shapes[k][1]), BF16)] * 2 + [pltpu.VMEM((2, chunk[k], shapes[k][1]), out_dtype)]
    scratch += [pltpu.SemaphoreType.DMA((n_sem,)), pltpu.SemaphoreType.DMA((n_sem,)),
                pltpu.SemaphoreType.DMA((2, 2)), pltpu.SemaphoreType.DMA((2,))]
    outs = pl.pallas_call(
        body,
        name=name,
        in_specs=[ANY] * len(ins),
        out_specs=[ANY] * (len(out_shape) + len(land_shape)),
        out_shape=out_shape + land_shape,
        scratch_shapes=scratch,
    )(*ins)
    outs = outs[:len(out_shape)]
    return [(outs[2 * k], outs[2 * k + 1]) for k in range(n_arr)] if first else list(outs)


def _reduce_scatter(dw_al, blocks, gather=()):
    xs = [dw_al] + [b.reshape(N_DEV // 2, 2, *b.shape[1:]) for b in blocks]
    ys, gathered = _halving_stage(xs, "c", "rs_c", BF16, windowed=(0,), gather=gather)
    pairs = _xy_stage([v.reshape(2, 2, *v.shape[1:]) for v in ys], True, "rs_xy1")
    return _xy_stage(pairs, False, "rs_xy2"), gathered


def _sum_slots(gs, name):
    n = len(gs)

    def body(*refs):
        for g_ref, o_ref in zip(refs[:n], refs[n:]):
            acc = g_ref[0].astype(F32)
            for d in range(1, N_DEV):
                acc = acc + g_ref[d].astype(F32)
            o_ref[...] = acc

    return pl.pallas_call(body, name=name, out_shape=[jax.ShapeDtypeStruct(g.shape[1:], F32) for g in gs])(*gs)


def _assemble_w_al(wins, bas):
    ba_tile = O_BA // LANE
    assert W_AL // LANE == ba_tile + 1
    cols = wins.shape[2]
    n_buf = 3
    ends = [WIN_START[d + 1] if d + 1 < N_DEV else ba_tile + 1 for d in range(N_DEV)]
    assert WIN_START[N_DEV - 1] + WIN_TILES == ba_tile + 1

    def body(w_ref, ba_ref, o_ref, buf, ld_sems, st_sems, ba_sem):
        def load(d):
            return pltpu.make_async_copy(w_ref.at[d], buf.at[d % n_buf], ld_sems.at[d % n_buf])

        def store(d):
            n = LANE * (ends[d] - WIN_START[d])
            return pltpu.make_async_copy(buf.at[d % n_buf, pl.ds(0, n), :],
                                         o_ref.at[pl.ds(LANE * WIN_START[d], n), :], st_sems.at[d % n_buf])

        load(0).start()
        for d in range(N_DEV):
            if d + 1 < N_DEV:
                if d + 1 >= n_buf:
                    store(d + 1 - n_buf).wait()
                load(d + 1).start()
            load(d).wait()
            if d > 0:
                ov = LANE * (WIN_START[d - 1] + WIN_TILES - WIN_START[d])
                buf[d % n_buf, :ov, :] = buf[d % n_buf, :ov, :] + buf[(d - 1) % n_buf, WIN_W - ov:, :]
            if d == N_DEV - 1:
                ba_copy = pltpu.make_async_copy(
                    ba_ref.at[BA_DEV], buf.at[d % n_buf, pl.ds(WIN_W - LANE, ba_ref.shape[1]), :], ba_sem)
                ba_copy.start()
                ba_copy.wait()
            store(d).start()
        for d in range(N_DEV - n_buf, N_DEV):
            store(d).wait()

    return pl.pallas_call(
        body,
        name="assemble_w_al",
        in_specs=[ANY, ANY],
        out_specs=ANY,
        out_shape=jax.ShapeDtypeStruct((W_AL, cols), wins.dtype),
        scratch_shapes=[pltpu.VMEM((n_buf, WIN_W, cols), wins.dtype), pltpu.SemaphoreType.DMA((n_buf,)),
                        pltpu.SemaphoreType.DMA((n_buf,)), pltpu.SemaphoreType.DMA],
    )(wins, bas)


def _adamw_math(w, g, m, v):
    m_new = ADAM_B1 * m + (1.0 - ADAM_B1) * g
    v_new = ADAM_B2 * v + (1.0 - ADAM_B2) * (g * g)
    m_hat = m_new / (1.0 - ADAM_B1 ** ADAM_STEP)
    v_hat = v_new / (1.0 - ADAM_B2 ** ADAM_STEP)
    return -ADAM_LR * (m_hat / (jnp.sqrt(v_hat) + ADAM_EPS) + ADAM_WD * w), m_new, v_new


def _adamw(w, g, m, v, name, tb=134):
    r, _, c = w.shape
    assert r % tb == 0

    def body(w_ref, g_ref, m_ref, v_ref, d_ref, nm_ref, nv_ref):
        d_ref[...], nm_ref[...], nv_ref[...] = _adamw_math(w_ref[...], g_ref[...], m_ref[...], v_ref[...])

    blk = pl.BlockSpec((tb, 1, c), lambda i: (i, 0, 0))
    o = jax.ShapeDtypeStruct(w.shape, F32)
    return pl.pallas_call(body, name=name, grid=(r // tb,), in_specs=[blk] * 4, out_specs=[blk] * 3,
                          out_shape=[o, o, o])(w, g, m, v)


def _adamw_many(ws, gs, ms, vs, name):
    n = len(ws)

    def body(*refs):
        for k in range(n):
            w_ref, g_ref, m_ref, v_ref = (refs[j * n + k] for j in range(4))
            d_ref, nm_ref, nv_ref = (refs[(4 + j) * n + k] for j in range(3))
            d_ref[...], nm_ref[...], nv_ref[...] = _adamw_math(w_ref[...], g_ref[...], m_ref[...], v_ref[...])

    shapes = [jax.ShapeDtypeStruct(w.shape, F32) for w in ws]
    outs = pl.pallas_call(body, name=name, out_shape=shapes * 3)(*ws, *gs, *ms, *vs)
    return outs[:n], outs[n:2 * n], outs[2 * n:]


def _select(me, table):
    return sum(jnp.where(me == d, jnp.int32(v), jnp.int32(0)) for d, v in enumerate(table))


WIN_SHIFT = tuple(SHARD_W * d - LANE * WIN_START[d] for d in range(N_DEV))
PAD_L = 256
PAD_R = 256


def _shard_to_window(shard_t, me):
    shift = _select(me, WIN_SHIFT)
    start = _select(me, WIN_START)
    padded = jnp.pad(shard_t, ((PAD_L, PAD_R), (0, 0)))
    cols = shard_t.shape[1]
    lo = lax.dynamic_slice(padded, (PAD_L - shift, 0), (WIN_W, cols))
    hi = lax.dynamic_slice(padded, (PAD_L - shift + N_BA, 0), (WIN_W, cols))
    aligned = LANE * start + lax.broadcasted_iota(jnp.int32, (WIN_W, 1), 0)
    return jnp.where(aligned >= ORIG_BA, hi, lo)


def _window_to_shard(win, ba_grad, me):
    shift = _select(me, WIN_SHIFT)
    cols = win.shape[1]
    padded = jnp.pad(win, ((N_BA, PAD_R), (0, 0)))
    lo = lax.dynamic_slice(padded, (N_BA + shift, 0), (SHARD_W, cols))
    hi = lax.dynamic_slice(padded, (shift, 0), (SHARD_W, cols))
    orig = SHARD_W * me + lax.broadcasted_iota(jnp.int32, (SHARD_W, 1), 0)
    ba_full = lax.dynamic_update_slice(jnp.zeros((SHARD_W, cols), win.dtype), ba_grad, (BA_LOCAL, 0))
    return jnp.where(orig < ORIG_BA, lo, jnp.where(orig >= ORIG_BA + N_BA, hi, ba_full))


def _pad_row(v, width=D_MODEL):
    v = v.reshape(1, -1)
    return jnp.pad(v, ((0, 0), (0, width - v.shape[1])))


def _slab(v, rows=8):
    return jnp.pad(v, ((0, rows - v.shape[0]), (0, D_MODEL - v.shape[1])))


def kernel(x, mem, norm_g, mem_norm_g, w_in, conv_w, a_log, dt_bias, dn_norm_g, w_mem_kv, w_br_dn, w_br_sb, w_br_mem, w_out, final_g, loss_target, m_norm_g, m_mem_norm_g, m_w_in, m_conv_w, m_a_log, m_dt_bias, m_dn_norm_g, m_w_mem_kv, m_w_br_dn, m_w_br_sb, m_w_br_mem, m_w_out, m_final_g, v_norm_g, v_mem_norm_g, v_w_in, v_conv_w, v_a_log, v_dt_bias, v_dn_norm_g, v_w_mem_kv, v_w_br_dn, v_w_br_sb, v_w_br_mem, v_w_out, v_final_g):
    xi, yi, ci = _position()
    me = 4 * xi + 2 * yi + ci

    shard_t = w_in[0].T
    win = _shard_to_window(shard_t, me).astype(BF16)
    ba = shard_t[BA_LOCAL:BA_LOCAL + N_BA, :].astype(BF16)
    g_win, g_ba = _all_gather([win, ba], "gather_weights")
    w_alt = _assemble_w_al(g_win, g_ba)

    shards = [w_mem_kv[0].astype(BF16), w_br_dn[0].astype(BF16), w_br_sb[0].astype(BF16), w_out[0].astype(BF16),
              w_br_mem[0].astype(BF16), conv_w[0]]
    r = _local_step(x[0], mem[0], loss_target[0], norm_g, mem_norm_g, w_alt, _pad_row(a_log, LANE),
                    _pad_row(dt_bias, LANE), dn_norm_g, final_g.reshape(1, D_MODEL), shards)

    dw_alt = r["w_alt"]
    parts = [r["norm_g"], r["mem_norm_g"], r["final_g"], r["dn_norm_g"], r["scal"], r["loss"], r["conv_w"],
             dw_alt[O_BA:O_BA + N_BA, :].astype(F32)]
    (g_win,), gathered = _reduce_scatter(dw_alt, [], gather=parts)
    rows_d = D_MODEL // N_DEV
    g_small = r["small"]
    g_dn, g_sb, g_out = (g_small[k * rows_d:(k + 1) * rows_d] for k in range(3))
    g_kv = g_small[3 * rows_d:3 * rows_d + rows_d // 2].reshape(rows_d, 2 * MEM_W)
    g_mem = g_small[3 * rows_d + rows_d // 2:].reshape(MEM_W, rows_d)
    s_norm_g, s_mem_norm_g, s_final_g, s_dn_norm_g, s_scal, s_loss, s_conv, s_ba = _sum_slots(gathered, "sum_small")
    loss = s_loss[0, 0]
    cw = conv_w.shape[2]
    g_conv = lax.dynamic_slice(s_conv, (0, cw * me), (CONV_K, cw))
    g_w_in_t = _window_to_shard(g_win, s_ba, me)
    grads = dict(norm_g=s_norm_g, mem_norm_g=s_mem_norm_g, w_in=g_w_in_t.T[None], conv_w=g_conv[None],
                 a_log=s_scal[0:1, :N_HEADS], dt_bias=s_scal[1:2, :N_HEADS], dn_norm_g=s_dn_norm_g, w_mem_kv=g_kv[None],
                 w_br_dn=g_dn[None], w_br_sb=g_sb[None], w_br_mem=g_mem[None], w_out=g_out[None],
                 final_g=s_final_g.reshape(D_MODEL))

    params = dict(norm_g=(norm_g, m_norm_g, v_norm_g), mem_norm_g=(mem_norm_g, m_mem_norm_g, v_mem_norm_g),
                  w_in=(w_in, m_w_in, v_w_in), conv_w=(conv_w, m_conv_w, v_conv_w), a_log=(a_log, m_a_log, v_a_log),
                  dt_bias=(dt_bias, m_dt_bias, v_dt_bias), dn_norm_g=(dn_norm_g, m_dn_norm_g, v_dn_norm_g),
                  w_mem_kv=(w_mem_kv, m_w_mem_kv, v_w_mem_kv), w_br_dn=(w_br_dn, m_w_br_dn, v_w_br_dn),
                  w_br_sb=(w_br_sb, m_w_br_sb, v_w_br_sb), w_br_mem=(w_br_mem, m_w_br_mem, v_w_br_mem),
                  w_out=(w_out, m_w_out, v_w_out), final_g=(final_g, m_final_g, v_final_g))
    order = list(params)
    deltas, new_m, new_v = {}, {}, {}
    deltas["w_in"], new_m["w_in"], new_v["w_in"] = (jnp.transpose(o, (1, 2, 0)) for o in _adamw(
        jnp.transpose(w_in, (2, 0, 1)), g_w_in_t[:, None, :], jnp.transpose(m_w_in, (2, 0, 1)),
        jnp.transpose(v_w_in, (2, 0, 1)), "adamw_w_in"))
    rest = [nm for nm in order if nm != "w_in"]

    def two_d(a):
        return a.reshape(1, -1) if a.ndim == 1 else a

    d_l, m_l, v_l = _adamw_many([two_d(params[nm][0]) for nm in rest], [two_d(grads[nm]) for nm in rest],
                                [two_d(params[nm][1]) for nm in rest], [two_d(params[nm][2]) for nm in rest], "adamw_rest")
    for k, nm in enumerate(rest):
        shp = params[nm][0].shape
        deltas[nm], new_m[nm], new_v[nm] = d_l[k].reshape(shp), m_l[k].reshape(shp), v_l[k].reshape(shp)
    return (loss, r["grad_x"][None], *[grads[nm] for nm in order], *[deltas[nm] for nm in order],
            *[new_m[nm] for nm in order], *[new_v[nm] for nm in order])
```

```python
import functools
import math

import jax
import jax.numpy as jnp
from jax import lax
from jax.experimental import pallas as pl
from jax.experimental.pallas import tpu as pltpu

F32 = jnp.float32
BF16 = jnp.bfloat16

D_MODEL = 1024
N_DEV = 8
N_HEADS = 8
D_HEAD = 128
DN_CHUNK = 64
CONV_K = 4
MEM_LEN = 256
MEM_HEADS = 4
MEM_DH = 64
MEM_W = MEM_HEADS * MEM_DH
NORM_EPS = 1e-6
IN_WIDTH = 11792
SHARD_W = IN_WIDTH // N_DEV

LANE = 128
SUPER = 2 * DN_CHUNK

O_QKV_DN = 0
O_Z_DN = 3072
O_QKV_SB = 4096
O_Z_SB = 7168
O_MQ = 8192
O_MZ = 8448
O_GATES = 8704
O_BA = 11776
W_AL = 11904
ORIG_BA = 4096
N_BA = 16

WIN_TILES = 13
WIN_W = WIN_TILES * LANE


def _aligned_col(o):
    return o if o < ORIG_BA else o - N_BA


WIN_START = tuple(min(_aligned_col(SHARD_W * d) // LANE, (W_AL // LANE) - WIN_TILES) for d in range(N_DEV))
WIN_OFF = tuple(_aligned_col(SHARD_W * d) - LANE * WIN_START[d] if SHARD_W * d >= ORIG_BA + N_BA or SHARD_W * d < ORIG_BA
                else None for d in range(N_DEV))
BA_DEV = ORIG_BA // SHARD_W
BA_LOCAL = ORIG_BA - BA_DEV * SHARD_W

ADAM_LR = 0.001
ADAM_B1 = 0.9
ADAM_B2 = 0.999
ADAM_EPS = 1e-08
ADAM_WD = 0.01
ADAM_STEP = 10

NN = (((1,), (0,)), ((), ()))
NT = (((1,), (1,)), ((), ()))
TN = (((0,), (0,)), ((), ()))


def _dot(a, b, dims):
    return lax.dot_general(a.astype(BF16), b.astype(BF16), dims, preferred_element_type=F32)


def _split2(a):
    hi = a.astype(BF16)
    lo = (a - hi.astype(F32)).astype(BF16)
    return hi, lo


def _dot3(a, b, dims):
    ah, al = _split2(a)
    bh, bl = _split2(b)
    d = functools.partial(lax.dot_general, dimension_numbers=dims, preferred_element_type=F32)
    return d(ah, bh) + (d(ah, bl) + d(al, bh))


def _sel_dot_impl(sel01, x, dims):
    sel = sel01.astype(BF16)
    h1 = x.astype(BF16)
    r1 = x - h1.astype(F32)
    h2 = r1.astype(BF16)
    h3 = (r1 - h2.astype(F32)).astype(BF16)
    d = functools.partial(lax.dot_general, dimension_numbers=dims, preferred_element_type=F32)
    return d(sel, h1) + (d(sel, h2) + d(sel, h3))


@jax.custom_vjp
def _sel_dot(sel01, x):
    return _sel_dot_impl(sel01, x, NN)


_sel_dot.defvjp(lambda s, x: (_sel_dot(s, x), s),
                lambda s, g: (jnp.zeros_like(s), _sel_dot_impl(s, g, TN)))


def _make_mm(dotfn):
    @jax.custom_vjp
    def nn(a, b):
        return dotfn(a, b, NN)

    @jax.custom_vjp
    def nt(a, b):
        return dotfn(a, b, NT)

    @jax.custom_vjp
    def tn(a, b):
        return dotfn(a, b, TN)

    nn.defvjp(lambda a, b: (nn(a, b), (a, b)), lambda r, g: (nt(g, r[1]), tn(r[0], g)))
    nt.defvjp(lambda a, b: (nt(a, b), (a, b)), lambda r, g: (nn(g, r[1]), tn(g, r[0])))
    tn.defvjp(lambda a, b: (tn(a, b), (a, b)), lambda r, g: (nt(r[1], g), nn(r[0], g)))
    return nn, nt, tn


mm_nn, mm_nt, mm_tn = _make_mm(_dot)
mm3_nn, mm3_nt, mm3_tn = _make_mm(_dot3)


def _sigmoid(x):
    return jax.nn.sigmoid(x)


def _silu(x):
    return x * _sigmoid(x)


def _softplus_parts(x):
    sp = jnp.log1p(jnp.exp(-jnp.abs(x)))
    return jnp.maximum(x, 0.0) + sp, jnp.maximum(-x, 0.0) + sp


def _rmsnorm(x, g):
    return x * lax.rsqrt(jnp.mean(x * x, axis=-1, keepdims=True) + NORM_EPS) * g


def _iota2(shape, dim):
    return lax.broadcasted_iota(jnp.int32, shape, dim)


def _div64(i):
    return lax.shift_right_logical(i, jnp.full(i.shape, 6, jnp.int32))


def _each(f, *lists):
    return [f(*a) for a in zip(*lists)]


@jax.custom_vjp
def _inv_unit_lower(ms):
    n = ms[0].shape[0]
    eye = (_iota2((n, n), 0) == _iota2((n, n), 1)).astype(F32)
    rs = [eye - m for m in ms]
    ps = ms
    for _ in range(5):
        ps = _each(mm3_nn, ps, ps)
        rs = _each(lambda r, p: r + mm_nn(r, p), rs, ps)
    return rs


def _inv_fwd(ms):
    rs = _inv_unit_lower(ms)
    return rs, rs


def _inv_bwd(rs, gs):
    ts = _each(mm_tn, rs, gs)
    return (_each(lambda t, r: -mm_nt(t, r), ts, rs),)


_inv_unit_lower.defvjp(_inv_fwd, _inv_bwd)


def _dn_block(cq, ck, cv, bcol, acol, zt, alog, dtb, gn, s0):
    n = SUPER
    h = DN_CHUNK
    row = _iota2((n, n), 0)
    col = _iota2((n, n), 1)
    same = _div64(row) == _div64(col)
    incl = jnp.logical_and(same, row >= col)
    strict = jnp.logical_and(same, row > col)
    incl_f = incl.astype(F32)

    qn = _each(lambda x: x * lax.rsqrt(jnp.sum(x * x, axis=-1, keepdims=True) + NORM_EPS) * (D_HEAD ** -0.5), cq)
    kn = _each(lambda x: x * lax.rsqrt(jnp.sum(x * x, axis=-1, keepdims=True) + NORM_EPS), ck)
    beta = _each(_sigmoid, bcol)
    g = _each(lambda al, ac, dt: -(jnp.exp(al) * _softplus_parts(ac + dt)[0]), alog, acol, dtb)
    gcum = _each(lambda x: _sel_dot(incl_f, jnp.broadcast_to(x, (n, n))), g)
    gam_incl = _each(lambda x: jnp.where(incl, jnp.exp(jnp.where(incl, x - x.T, 0.0)), 0.0), gcum)
    kk = _each(mm_nt, kn, kn)
    t_inv = _inv_unit_lower(_each(lambda b, x, gm: b * x * jnp.where(strict, gm, 0.0), beta, kk, gam_incl))
    eg = _each(jnp.exp, gcum)
    u = _each(lambda t, v, b: mm_nn(t, v * b), t_inv, cv, beta)
    w = _each(lambda t, k, b, e: mm_nn(t, k * (b * e)), t_inv, kn, beta, eg)
    a_intra = _each(lambda q, k, gm: mm_nt(q, k) * gm, qn, kn, gam_incl)
    q_dec = _each(lambda q, e: q * e, qn, eg)
    last0 = _each(lambda x: x[h - 1:h, :], gcum)
    last1 = _each(lambda x: x[n - 1:n, :], gcum)
    k_dec = _each(lambda k, x, l0, l1: k * jnp.exp(jnp.concatenate(
        [jnp.broadcast_to(l0, (h, n)), jnp.broadcast_to(l1, (h, n))], axis=0) - x), kn, gcum, last0, last1)
    v0 = _each(lambda uu, ww, s: uu[:h] - mm_nn(ww[:h], s), u, w, s0)
    o0 = _each(lambda q, s: mm_nn(q[:h], s), q_dec, s0)
    s1 = _each(lambda s, l0, k, v: s * jnp.exp(l0) + mm_tn(k[:h], v), s0, last0, k_dec, v0)
    v1 = _each(lambda uu, ww, s: uu[h:] - mm_nn(ww[h:], s), u, w, s1)
    o1 = _each(lambda q, s: mm_nn(q[h:], s), q_dec, s1)
    s2 = _each(lambda s, l1, k, v: s * jnp.exp(l1) + mm_tn(k[h:], v), s1, last1, k_dec, v1)
    o = _each(lambda a, b, am, x, y: jnp.concatenate([a, b], axis=0) + mm_nn(am, jnp.concatenate([x, y], axis=0)),
              o0, o1, a_intra, v0, v1)
    out = _each(lambda x, z: _rmsnorm(x, gn) * _silu(z), o, zt)
    return out, s2


def _mem_fn(mq, mz, mkv):
    mk = mkv[:, :MEM_W]
    mv = mkv[:, MEM_W:]
    lane = _iota2((1, MEM_W), 1)
    out = jnp.zeros(mq.shape, F32)
    for hd in range(MEM_HEADS):
        hm = (_div64(lane) == hd).astype(F32)
        s = mm_nt(mq * hm, mk) * (1.0 / math.sqrt(MEM_DH))
        s = s - jnp.max(s, axis=-1, keepdims=True)
        e = jnp.exp(s)
        p = e / jnp.sum(e, axis=-1, keepdims=True)
        out = out + mm_nn(p, mv) * hm
    return out * _silu(mz)


def _merge_fn(gd, gs, gm, yd, ys, ym):
    return _sigmoid(gd) * yd + _sigmoid(gs) * ys + _sigmoid(gm) * ym


def _loss_fn(x, mo, fg, tgt):
    y = _rmsnorm(x + mo, fg)
    err = y - tgt
    return 0.5 * jnp.sum(jnp.mean(err * err, axis=-1, keepdims=True), axis=0, keepdims=True)


def _matmul(a, b, mode, out_dtype, tm, tn, tk, name, b_col0=0, n_cols=None):
    if mode == "nn":
        m, kdim = a.shape
        n = b.shape[1] if n_cols is None else n_cols
    elif mode == "nt":
        m, kdim = a.shape
        n = b.shape[0]
    else:
        kdim, m = a.shape
        n = b.shape[1] if n_cols is None else n_cols
    tm, tn, tk = min(tm, m), min(tn, n), min(tk, kdim)
    assert m % tm == 0 and n % tn == 0 and kdim % tk == 0 and b_col0 % tn == 0
    nk = kdim // tk
    jb = b_col0 // tn
    dims = {"nn": NN, "nt": NT, "tn": TN}[mode]

    def body(a_ref, b_ref, o_ref, acc_ref):
        k = pl.program_id(2)
        part = _dot(a_ref[...], b_ref[...], dims)

        @pl.when(k == 0)
        def _():
            acc_ref[...] = part

        @pl.when(k > 0)
        def _():
            acc_ref[...] += part

        @pl.when(k == nk - 1)
        def _():
            o_ref[...] = acc_ref[...].astype(o_ref.dtype)

    if mode == "nn":
        a_spec = pl.BlockSpec((tm, tk), lambda i, j, k: (i, k))
        b_spec = pl.BlockSpec((tk, tn), lambda i, j, k: (k, j + jb))
    elif mode == "nt":
        a_spec = pl.BlockSpec((tm, tk), lambda i, j, k: (i, k))
        b_spec = pl.BlockSpec((tn, tk), lambda i, j, k: (j, k))
    else:
        a_spec = pl.BlockSpec((tk, tm), lambda i, j, k: (k, i))
        b_spec = pl.BlockSpec((tk, tn), lambda i, j, k: (k, j + jb))
    return pl.pallas_call(
        body,
        name=name,
        grid=(m // tm, n // tn, nk),
        in_specs=[a_spec, b_spec],
        out_specs=pl.BlockSpec((tm, tn), lambda i, j, k: (i, j)),
        out_shape=jax.ShapeDtypeStruct((m, n), out_dtype),
        scratch_shapes=[pltpu.VMEM((tm, tn), F32)],
        compiler_params=pltpu.CompilerParams(dimension_semantics=("parallel", "parallel", "arbitrary")),
    )(a, b)


def _norm_in(x, g, tm=256):
    t = x.shape[0]

    def body(x_ref, g_ref, h_ref):
        h_ref[...] = _rmsnorm(x_ref[...], g_ref[...]).astype(BF16)

    return pl.pallas_call(
        body,
        name="norm_in",
        grid=(t // tm,),
        in_specs=[pl.BlockSpec((tm, D_MODEL), lambda i: (i, 0)), pl.BlockSpec((1, D_MODEL), lambda i: (0, 0))],
        out_specs=pl.BlockSpec((tm, D_MODEL), lambda i: (i, 0)),
        out_shape=jax.ShapeDtypeStruct((t, D_MODEL), BF16),
    )(x, g)


def _grad_x(dproj, w_alt, x, g, dres, tm=512, tk=3968):
    t, kdim = dproj.shape
    tm = min(tm, t)
    assert kdim % tk == 0 and t % tm == 0
    nk = kdim // tk

    def body(a_ref, b_ref, x_ref, g_ref, dres_ref, dx_ref, dg_ref, acc_ref):
        i, k = pl.program_id(0), pl.program_id(1)
        part = _dot(a_ref[...], b_ref[...], NN)

        @pl.when(k == 0)
        def _():
            acc_ref[...] = part

        @pl.when(k > 0)
        def _():
            acc_ref[...] += part

        @pl.when(jnp.logical_and(i == 0, k == 0))
        def _():
            dg_ref[...] = jnp.zeros_like(dg_ref)

        @pl.when(k == nk - 1)
        def _():
            _, vjp = jax.vjp(_rmsnorm, x_ref[...], g_ref[...])
            dx, dg = vjp(acc_ref[...])
            dx_ref[...] = dx + dres_ref[...]
            dg_ref[...] += dg

    row = pl.BlockSpec((tm, D_MODEL), lambda i, k: (i, 0))
    vec = pl.BlockSpec((1, D_MODEL), lambda i, k: (0, 0))
    return pl.pallas_call(
        body,
        name="grad_x",
        grid=(t // tm, nk),
        in_specs=[pl.BlockSpec((tm, tk), lambda i, k: (i, k)), pl.BlockSpec((tk, D_MODEL), lambda i, k: (k, 0)),
                  row, vec, row],
        out_specs=[row, vec],
        out_shape=[jax.ShapeDtypeStruct((t, D_MODEL), F32), jax.ShapeDtypeStruct((1, D_MODEL), F32)],
        scratch_shapes=[pltpu.VMEM((tm, D_MODEL), F32)],
    )(dproj, w_alt, x, g, dres)


def _block_tail(proj, o_dn, o_sb, o_m, x, tgt, w_br_dn, w_br_sb, w_br_mem, w_out, fg, tm=256):
    t = x.shape[0]
    tm = min(tm, t)
    gw = 512
    n_g = 3 * D_MODEL // gw

    def body(*refs):
        g_refs = refs[:n_g]
        (odn_ref, osb_ref, om_ref, x_ref, t_ref, wdn_ref, wsb_ref, wm_ref, wo_ref, fg_ref, loss_ref, dout_ref, dfg_ref,
         mg_ref, dyd_ref, dys_ref, dym_ref, dg_ref, dod_ref, dos_ref, dom_ref) = refs[n_g:]
        y = [_dot(odn_ref[...], wdn_ref[...], NN), _dot(osb_ref[...], wsb_ref[...], NN),
             _dot(om_ref[...], wm_ref[...], NN)]
        s = [_sigmoid(jnp.concatenate([g_refs[2 * k][...], g_refs[2 * k + 1][...]], axis=1)) for k in range(3)]
        merged16 = (s[0] * y[0] + s[1] * y[1] + s[2] * y[2]).astype(BF16)
        mg_ref[...] = merged16
        mo = _dot(merged16, wo_ref[...], NN)
        loss, vjp = jax.vjp(_loss_fn, x_ref[...], mo, fg_ref[...], t_ref[...])
        _, dout, dfg, _ = vjp(jnp.ones((1, 1), F32))

        @pl.when(pl.program_id(0) == 0)
        def _():
            loss_ref[...] = jnp.zeros_like(loss_ref)
            dfg_ref[...] = jnp.zeros_like(dfg_ref)

        loss_ref[...] += jnp.broadcast_to(loss, loss_ref.shape)
        dfg_ref[...] += dfg
        dout_ref[...] = dout
        dmerged = _dot(dout, wo_ref[...], NT)
        dy = [(sk * dmerged).astype(BF16) for sk in s]
        dyd_ref[...], dys_ref[...], dym_ref[...] = dy
        dg_ref[...] = jnp.concatenate([dmerged * yk * (sk * (1.0 - sk)) for yk, sk in zip(y, s)], axis=1).astype(BF16)
        dod_ref[...] = _dot(dy[0], wdn_ref[...], NT).astype(BF16)
        dos_ref[...] = _dot(dy[1], wsb_ref[...], NT).astype(BF16)
        dom_ref[...] = _dot(dy[2], wm_ref[...], NT).astype(BF16)

    gates = [pl.BlockSpec((tm, gw), lambda i, j=j: (i, O_GATES // gw + j)) for j in range(n_g)]
    row = pl.BlockSpec((tm, D_MODEL), lambda i: (i, 0))
    rowm = pl.BlockSpec((tm, MEM_W), lambda i: (i, 0))
    vec = pl.BlockSpec((1, D_MODEL), lambda i: (0, 0))

    def whole(a):
        return pl.BlockSpec(a.shape, lambda i: (0, 0), pipeline_mode=pl.Buffered(1))

    def bf(c):
        return jax.ShapeDtypeStruct((t, c), BF16)

    return pl.pallas_call(
        body,
        name="block_tail",
        grid=(t // tm,),
        in_specs=gates + [row, row, rowm, row, row, whole(w_br_dn), whole(w_br_sb), whole(w_br_mem), whole(w_out), vec],
        out_specs=[pl.BlockSpec((1, LANE), lambda i: (0, 0)), row, vec, row, row, row, row,
                   pl.BlockSpec((tm, 3 * D_MODEL), lambda i: (i, 0)), row, row, rowm],
        out_shape=[jax.ShapeDtypeStruct((1, LANE), F32), jax.ShapeDtypeStruct((t, D_MODEL), F32),
                   jax.ShapeDtypeStruct((1, D_MODEL), F32), bf(D_MODEL), bf(D_MODEL), bf(D_MODEL), bf(D_MODEL),
                   bf(3 * D_MODEL), bf(D_MODEL), bf(D_MODEL), bf(MEM_W)],
    )(*([proj] * n_g), o_dn, o_sb, o_m, x, tgt, w_br_dn, w_br_sb, w_br_mem, w_out, fg)


def _shift_rows(x, s):
    t = x.shape[0]
    if s == 0:
        return x
    rolled = pltpu.roll(x, s % t, 0)
    row = _iota2(x.shape, 0)
    keep = row >= s if s > 0 else row < t + s
    return jnp.where(keep, rolled, 0.0)


def _conv_pre(x, w):
    return sum(_shift_rows(x, CONV_K - 1 - j) * w[j:j + 1, :] for j in range(CONV_K))


CONV_TC = 256


def _dn_conv(proj, conv_w):
    t = proj.shape[0]
    nb = 3 * D_MODEL // CONV_TC

    def body(x_ref, w_ref, c_ref):
        c_ref[...] = _silu(_conv_pre(x_ref[...], w_ref[...]))

    return pl.pallas_call(
        body,
        name="dn_conv",
        grid=(nb,),
        in_specs=[pl.BlockSpec((t, CONV_TC), lambda j: (0, j)), pl.BlockSpec((CONV_K, CONV_TC), lambda j: (0, j))],
        out_specs=pl.BlockSpec((t, CONV_TC), lambda j: (0, j)),
        out_shape=jax.ShapeDtypeStruct((t, 3 * D_MODEL), F32),
    )(proj, conv_w)


def _dn_conv_bwd(proj, conv_w, dc, part):
    t = proj.shape[0]
    nb = D_MODEL // CONV_TC
    b0 = part * nb

    def body(x_ref, w_ref, dc_ref, dx_ref, dw_ref):
        x = x_ref[...]
        w = w_ref[...]
        pre = _conv_pre(x, w)
        sg = _sigmoid(pre)
        dpre = dc_ref[...] * (sg * (1.0 + pre * (1.0 - sg)))
        ahead = [_shift_rows(dpre, -(CONV_K - 1 - j)) for j in range(CONV_K)]
        dx_ref[...] = sum(a * w[j:j + 1, :] for j, a in enumerate(ahead)).astype(BF16)
        dw_ref[...] = jnp.concatenate([jnp.sum(a * x, axis=0, keepdims=True) for a in ahead], axis=0)

    blk = pl.BlockSpec((t, CONV_TC), lambda j: (0, j))
    return pl.pallas_call(
        body,
        name=f"dn_conv_bwd{part}",
        grid=(nb,),
        in_specs=[pl.BlockSpec((t, CONV_TC), lambda j: (0, b0 + j)),
                  pl.BlockSpec((CONV_K, CONV_TC), lambda j: (0, b0 + j)), blk],
        out_specs=[blk, pl.BlockSpec((CONV_K, CONV_TC), lambda j: (0, j))],
        out_shape=[jax.ShapeDtypeStruct((t, D_MODEL), BF16), jax.ShapeDtypeStruct((CONV_K, D_MODEL), F32)],
    )(proj, conv_w, dc)


def _ba_columns(ba, hd):
    lane = _iota2(ba.shape, 1)
    bcol = jnp.sum(jnp.where(lane == hd, ba, 0.0), axis=1, keepdims=True)
    acol = jnp.sum(jnp.where(lane == N_HEADS + hd, ba, 0.0), axis=1, keepdims=True)
    return bcol, acol


def _head_scalar(row, hd):
    lane = _iota2(row.shape, 1)
    return jnp.sum(jnp.where(lane == hd, row, 0.0), axis=1, keepdims=True)


DN_HP = 8


def _dn_inputs(cq, ck, cv, ba_ref, z_ref, alog_ref, dtb_ref, heads, lanes):
    ba = ba_ref[...]
    cols = [_ba_columns(ba, hd) for hd in heads]
    return ([cq[:, ln] for ln in lanes], [ck[:, ln] for ln in lanes], [cv[:, ln] for ln in lanes],
            [c[0] for c in cols], [c[1] for c in cols], [z_ref[:, ln] for ln in lanes],
            [_head_scalar(alog_ref[...], hd) for hd in heads], [_head_scalar(dtb_ref[...], hd) for hd in heads])


def _dn_specs(nblk, reverse):
    w = DN_HP * LANE
    nq = D_MODEL // w

    def row(i):
        return nblk - 1 - i if reverse else i

    def colblk(b0):
        return pl.BlockSpec((SUPER, w), lambda i, h: (row(i), b0 + h))

    ba = pl.BlockSpec((SUPER, LANE), lambda i, h: (row(i), O_BA // LANE))
    vec = pl.BlockSpec((1, LANE), lambda i, h: (0, 0))
    st = pl.BlockSpec((1, DN_HP, D_HEAD, D_HEAD), lambda i, h: (row(i), h, 0, 0))
    return colblk, nq, ba, vec, st


def _dn_fwd(c, proj, alog_row, dtb_row, gn):
    t = c.shape[0]
    nblk = t // SUPER
    colblk, nq, ba, vec, st = _dn_specs(nblk, False)

    def body(cq, ck, cv, ba_ref, z_ref, alog_ref, dtb_ref, gn_ref, o_ref, s_ref, state):
        @pl.when(jnp.logical_and(pl.program_id(0) == 0, pl.program_id(1) == 0))
        def _():
            state[...] = jnp.zeros_like(state)

        heads = [pl.program_id(1) * DN_HP + j for j in range(DN_HP)]
        lanes = [slice(j * LANE, (j + 1) * LANE) for j in range(DN_HP)]
        s0 = [state[hd] for hd in heads]
        outs, s2 = _dn_block(*_dn_inputs(cq, ck, cv, ba_ref, z_ref, alog_ref, dtb_ref, heads, lanes), gn_ref[...], s0)
        for j, (hd, ln) in enumerate(zip(heads, lanes)):
            s_ref[0, j] = s0[j]
            o_ref[:, ln] = outs[j].astype(BF16)
            state[hd] = s2[j]

    return pl.pallas_call(
        body,
        name="dn_fwd",
        grid=(nblk, N_HEADS // DN_HP),
        in_specs=[colblk(0), colblk(nq), colblk(2 * nq), ba, colblk(O_Z_DN // (DN_HP * LANE)), vec, vec, vec],
        out_specs=[colblk(0), st],
        out_shape=[jax.ShapeDtypeStruct((t, D_MODEL), BF16),
                   jax.ShapeDtypeStruct((nblk, N_HEADS, D_HEAD, D_HEAD), F32)],
        scratch_shapes=[pltpu.VMEM((N_HEADS, D_HEAD, D_HEAD), F32)],
    )(c, c, c, proj, proj, alog_row, dtb_row, gn)


def _dn_bwd(c, proj, alog_row, dtb_row, gn, states, do):
    t = c.shape[0]
    nblk = t // SUPER
    colblk, nq, ba, vec, st = _dn_specs(nblk, True)

    def body(cq, ck, cv, ba_ref, z_ref, alog_ref, dtb_ref, gn_ref, s_ref, do_ref,
             dq_ref, dk_ref, dv_ref, dz_ref, dba_ref, dsc_ref, dgn_ref, dstate):
        i = pl.program_id(0)
        hq = pl.program_id(1)

        @pl.when(jnp.logical_and(i == 0, hq == 0))
        def _():
            dstate[...] = jnp.zeros_like(dstate)
            dsc_ref[...] = jnp.zeros_like(dsc_ref)
            dgn_ref[...] = jnp.zeros_like(dgn_ref)

        @pl.when(hq == 0)
        def _():
            dba_ref[...] = jnp.zeros_like(dba_ref)

        lane = _iota2((SUPER, LANE), 1)
        lane1 = _iota2((1, LANE), 1)
        heads = [hq * DN_HP + j for j in range(DN_HP)]
        lanes = [slice(j * LANE, (j + 1) * LANE) for j in range(DN_HP)]
        ds_in = [dstate[hd] for hd in heads]
        s_in = [s_ref[0, j] for j in range(DN_HP)]
        _, vjp = jax.vjp(_dn_block, *_dn_inputs(cq, ck, cv, ba_ref, z_ref, alog_ref, dtb_ref, heads, lanes),
                         gn_ref[...], s_in)
        dq, dk, dv, dbc, dac, dz, dal, ddt, dgn, ds0 = vjp(([do_ref[:, ln].astype(F32) for ln in lanes], ds_in))
        dba = jnp.zeros((SUPER, LANE), F32)
        dal_row = jnp.zeros((1, LANE), F32)
        ddt_row = jnp.zeros((1, LANE), F32)
        for j, (hd, ln) in enumerate(zip(heads, lanes)):
            dq_ref[:, ln] = dq[j]
            dk_ref[:, ln] = dk[j]
            dv_ref[:, ln] = dv[j]
            dz_ref[:, ln] = dz[j].astype(BF16)
            dstate[hd] = ds0[j]
            dba = dba + jnp.where(lane == hd, dbc[j], 0.0) + jnp.where(lane == N_HEADS + hd, dac[j], 0.0)
            dal_row = dal_row + jnp.where(lane1 == hd, dal[j], 0.0)
            ddt_row = ddt_row + jnp.where(lane1 == hd, ddt[j], 0.0)
        dba_ref[...] += dba
        dsc_ref[0:1, :] += dal_row
        dsc_ref[1:2, :] += ddt_row
        dgn_ref[...] += dgn

    outs = pl.pallas_call(
        body,
        name="dn_bwd",
        grid=(nblk, N_HEADS // DN_HP),
        in_specs=[colblk(0), colblk(nq), colblk(2 * nq), ba, colblk(O_Z_DN // (DN_HP * LANE)), vec, vec, vec, st,
                  colblk(0)],
        out_specs=[colblk(0), colblk(0), colblk(0), colblk(0),
                   pl.BlockSpec((SUPER, LANE), lambda i, h: (nblk - 1 - i, 0)),
                   pl.BlockSpec((2, LANE), lambda i, h: (0, 0)), vec],
        out_shape=[jax.ShapeDtypeStruct((t, D_MODEL), F32)] * 3
        + [jax.ShapeDtypeStruct((t, D_MODEL), BF16), jax.ShapeDtypeStruct((t, LANE), F32),
           jax.ShapeDtypeStruct((2, LANE), F32), jax.ShapeDtypeStruct((1, LANE), F32)],
        scratch_shapes=[pltpu.VMEM((N_HEADS, D_HEAD, D_HEAD), F32)],
    )(c, c, c, proj, proj, alog_row, dtb_row, gn, states, do)
    return outs


SB_TQ = 256
SB_TK = 256
SB_HP_FWD = 8
SB_HP_BWD = 4


def _sb_logits(z, mask):
    sp = jnp.log(1.0 + jnp.exp(-jnp.abs(z)))
    lf_raw = -(jnp.maximum(z, 0.0) + sp)
    lb = lf_raw + z
    lf = lf_raw if mask is None else jnp.where(mask, lf_raw, 0.0)
    return lb, lf_raw, lf


def _suffix_sums(x, sel):
    hi, lo = _split2(x)
    d = functools.partial(lax.dot_general, dimension_numbers=NN, preferred_element_type=F32)
    return d(hi, sel) + d(lo, sel)


def _sb_diag_mask(tq, r):
    return r * SB_TK + _iota2((tq, SB_TK), 1) < _iota2((tq, SB_TK), 0)


def _sb_specs(t, tq, hp):
    w = hp * LANE
    q0, k0, v0, z0 = (O_QKV_SB // w, (O_QKV_SB + D_MODEL) // w, (O_QKV_SB + 2 * D_MODEL) // w, O_Z_SB // w)

    def blk(b0):
        return pl.BlockSpec((tq, w), lambda h, i: (i, b0 + h))

    def full(b0, **kw):
        return pl.BlockSpec((t, w), lambda h, i: (0, b0 + h), **kw)

    once = dict(pipeline_mode=pl.Buffered(1))
    return blk(q0), full(k0, **once), full(v0, **once), blk(z0), blk(0), full(0)


def _sb_fwd(proj, shards):
    t = proj.shape[0]
    tq = min(SB_TQ, t)
    ndiag = tq // SB_TK
    scale = 1.0 / math.sqrt(D_HEAD)
    na = len(shards)

    def body(q_ref, k_ref, v_ref, z_ref, *rest):
        x_refs, (o_ref, oraw_ref), land = rest[:na], rest[na:na + 2], rest[na + 2:2 * na + 2]
        sems = rest[2 * na + 2:]
        qi = pl.program_id(1)
        first = jnp.logical_and(pl.program_id(0) == 0, qi == 0)
        last = jnp.logical_and(pl.program_id(0) == pl.num_programs(0) - 1, qi == pl.num_programs(1) - 1)

        @pl.when(first)
        def _():
            for cp in _direct_gather_copies(x_refs, land, *sems):
                cp.start()

        lanes = [slice(hd * LANE, (hd + 1) * LANE) for hd in range(SB_HP_FWD)]
        qs = [(q_ref[:, ln] * scale).astype(BF16) for ln in lanes]
        after = (_iota2((SB_TK, SB_TK), 0) > _iota2((SB_TK, SB_TK), 1)).astype(BF16)
        oraw_ref[...] = jnp.zeros_like(oraw_ref)

        def block(kb, mask, c_lf):
            rows = pl.ds(pl.multiple_of(kb * SB_TK, SB_TK), SB_TK)
            z = _each(lambda q, ln: _dot(q, k_ref[rows, ln], NT), qs, lanes)
            lg = _each(lambda x: _sb_logits(x, mask), z)
            surv = _each(lambda x: _suffix_sums(x[2], after), lg)
            att = _each(lambda x, s, c: jnp.exp(x[0] + s + c), lg, surv, c_lf)
            if mask is not None:
                att = _each(lambda a: jnp.where(mask, a, 0.0), att)
            pv = _each(lambda a, ln: _dot(a, v_ref[rows, ln], NN), att, lanes)
            for p, ln in zip(pv, lanes):
                oraw_ref[:, ln] += p
            return tuple(_each(lambda c, x: c + jnp.sum(x[2], axis=1, keepdims=True), c_lf, lg))

        carry = tuple(jnp.zeros((tq, 1), F32) for _ in range(SB_HP_FWD))
        for r in reversed(range(ndiag)):
            carry = block(qi * ndiag + r, _sb_diag_mask(tq, r), carry)
        lax.fori_loop(0, qi * ndiag, lambda i, c: block(qi * ndiag - 1 - i, None, c), carry)
        o_ref[...] = (oraw_ref[...] * _silu(z_ref[...])).astype(BF16)

        @pl.when(last)
        def _():
            for cp in _direct_gather_copies(x_refs, land, *sems):
                cp.wait()

    q_spec, k_spec, v_spec, z_spec, out, _ = _sb_specs(t, tq, SB_HP_FWD)
    outs = pl.pallas_call(
        body,
        name="sb_fwd",
        grid=(N_HEADS // SB_HP_FWD, t // tq),
        in_specs=[q_spec, k_spec, v_spec, z_spec] + [ANY] * na,
        out_specs=[out, out] + [ANY] * na,
        out_shape=[jax.ShapeDtypeStruct((t, D_MODEL), BF16), jax.ShapeDtypeStruct((t, D_MODEL), F32)]
        + [jax.ShapeDtypeStruct((N_DEV, *v.shape), v.dtype) for v in shards],
        scratch_shapes=_gather_sems(na),
    )(proj, proj, proj, proj, *shards)
    return outs[0], outs[1], outs[2:]


def _sb_bwd(proj, oraw, do, blocks):
    t = proj.shape[0]
    tq = min(SB_TQ, t)
    ndiag = tq // SB_TK
    scale = 1.0 / math.sqrt(D_HEAD)

    def body(q_ref, k_ref, v_ref, z_ref, oraw_ref, do_ref, blk_ref, dq_ref, dk_ref, dv_ref, dz_ref, land_ref,
             dk_acc, dv_acc, p_scr, z_scr, send_sems, recv_sems, local_sem):
        qi = pl.program_id(1)
        nq = pl.num_programs(1)
        hg = pl.program_id(0)
        me = _position()
        mine = 4 * me[0] + 2 * me[1] + me[2]

        def exchange():
            cps = [pltpu.make_async_copy(blk_ref.at[mine], land_ref.at[mine], local_sem)]
            for k, peer in enumerate(_other_devices(me)):
                cps.append(pltpu.make_async_remote_copy(
                    src_ref=blk_ref.at[4 * peer[0] + 2 * peer[1] + peer[2]], dst_ref=land_ref.at[mine],
                    send_sem=send_sems.at[k], recv_sem=recv_sems.at[k], device_id=peer, device_id_type=MESH))
            return cps

        @pl.when(jnp.logical_and(hg == 0, qi == 0))
        def _():
            for cp in exchange():
                cp.start()

        @pl.when(qi == 0)
        def _():
            dk_acc[...] = jnp.zeros_like(dk_acc)
            dv_acc[...] = jnp.zeros_like(dv_acc)

        heads = range(SB_HP_BWD)
        lanes = [slice(hd * LANE, (hd + 1) * LANE) for hd in heads]
        zg = z_ref[...]
        sg = _sigmoid(zg)
        dog = do_ref[...].astype(F32)
        dz_ref[...] = (dog * oraw_ref[...] * (sg * (1.0 + zg * (1.0 - sg)))).astype(BF16)
        d_o = (dog * (zg * sg)).astype(BF16)
        d_o16 = [d_o[:, ln] for ln in lanes]
        qs = [(q_ref[:, ln] * scale).astype(BF16) for ln in lanes]
        ri = _iota2((SB_TK, SB_TK), 0)
        ci = _iota2((SB_TK, SB_TK), 1)
        after = (ri > ci).astype(BF16)
        earlier = (ri < ci).astype(BF16)

        def rows_of(kb):
            return pl.ds(pl.multiple_of(kb * SB_TK, SB_TK), SB_TK)

        def down(kb, mask, c_lf):
            rows = rows_of(kb)
            z = _each(lambda q, ln: _dot(q, k_ref[rows, ln], NT), qs, lanes)
            da = _each(lambda d, ln: _dot(d, v_ref[rows, ln], NT), d_o16, lanes)
            lg = _each(lambda x: _sb_logits(x, mask), z)
            surv = _each(lambda x: _suffix_sums(x[2], after), lg)
            att = _each(lambda x, s, c: jnp.exp(x[0] + s + c), lg, surv, c_lf)
            if mask is not None:
                att = _each(lambda a: jnp.where(mask, a, 0.0), att)
            dv = _each(lambda a, d: _dot(a, d, TN), att, d_o16)
            for hd in heads:
                p_scr[hd, kb] = att[hd] * da[hd]
                z_scr[hd, kb] = z[hd]
                dv_acc[rows, lanes[hd]] += dv[hd]
            return tuple(_each(lambda c, x: c + jnp.sum(x[2], axis=1, keepdims=True), c_lf, lg))

        c_lf = tuple(jnp.zeros((tq, 1), F32) for _ in heads)
        for r in reversed(range(ndiag)):
            c_lf = down(qi * ndiag + r, _sb_diag_mask(tq, r), c_lf)
        lax.fori_loop(0, qi * ndiag, lambda i, c: down(qi * ndiag - 1 - i, None, c), c_lf)

        def up(kb, mask, carry):
            dq, c_p = carry
            rows = rows_of(kb)
            p = [p_scr[hd, kb] for hd in heads]
            zs = [z_scr[hd, kb] for hd in heads]
            before = _each(lambda x, c: _suffix_sums(x, earlier) + c, p, c_p)
            e = _each(lambda x: jnp.exp(-jnp.abs(x)), zs)
            r = _each(lambda x: 1.0 / (1.0 + x), e)
            sig = _each(lambda x, a, b: jnp.where(x >= 0.0, b, a * b), zs, e, r)
            oms = _each(lambda x, a, b: jnp.where(x >= 0.0, a * b, b), zs, e, r)
            if mask is not None:
                sig = _each(lambda a: jnp.where(mask, a, 0.0), sig)
            dzz = _each(lambda x, o, g, b: x * o - g * b, p, oms, sig, before)
            dk = _each(lambda x, q: _dot(x, q, TN), dzz, qs)
            dq = _each(lambda a, x, ln: a + _dot(x, k_ref[rows, ln], NN), dq, dzz, lanes)
            for hd in heads:
                dk_acc[rows, lanes[hd]] += dk[hd]
            return tuple(dq), tuple(_each(lambda c, x: c + jnp.sum(x, axis=1, keepdims=True), c_p, p))

        carry = (tuple(jnp.zeros((tq, D_HEAD), F32) for _ in heads), tuple(jnp.zeros((tq, 1), F32) for _ in heads))
        carry = lax.fori_loop(0, qi * ndiag, lambda kb, c: up(kb, None, c), carry)
        for r in range(ndiag):
            carry = up(qi * ndiag + r, _sb_diag_mask(tq, r), carry)
        dq = carry[0]
        for hd in heads:
            dq_ref[:, lanes[hd]] = (dq[hd] * scale).astype(BF16)

        @pl.when(qi == nq - 1)
        def _():
            dk_ref[...] = dk_acc[...].astype(BF16)
            dv_ref[...] = dv_acc[...].astype(BF16)

        @pl.when(jnp.logical_and(hg == pl.num_programs(0) - 1, qi == nq - 1))
        def _():
            for cp in exchange():
                cp.wait()

    q_spec, k_spec, v_spec, z_spec, blk, full = _sb_specs(t, tq, SB_HP_BWD)
    o = jax.ShapeDtypeStruct((t, D_MODEL), BF16)
    w = SB_HP_BWD * LANE
    return pl.pallas_call(
        body,
        name="sb_bwd",
        grid=(N_HEADS // SB_HP_BWD, t // tq),
        in_specs=[q_spec, k_spec, v_spec, z_spec, blk, blk, ANY],
        out_specs=[blk, full, full, blk, ANY],
        out_shape=[o, o, o, o, jax.ShapeDtypeStruct(blocks.shape, blocks.dtype)],
        scratch_shapes=[pltpu.VMEM((t, w), F32), pltpu.VMEM((t, w), F32)]
        + [pltpu.VMEM((SB_HP_BWD, t // SB_TK, tq, SB_TK), F32)] * 2
        + [pltpu.SemaphoreType.DMA((N_DEV - 1,)), pltpu.SemaphoreType.DMA((N_DEV - 1,)), pltpu.SemaphoreType.DMA],
    )(proj, proj, proj, proj, oraw, do, blocks)


def _mem_kv_fn(mem, mg, w):
    return mm_nn(_rmsnorm(mem, mg), w)


def _mem_kv(mem, mg, w):
    def body(m_ref, g_ref, w_ref, o_ref):
        o_ref[...] = _mem_kv_fn(m_ref[...], g_ref[...], w_ref[...])

    return pl.pallas_call(body, name="mem_kv", out_shape=jax.ShapeDtypeStruct((MEM_LEN, 2 * MEM_W), F32))(mem, mg, w)


def _mem_kv_bwd(mem, mg, w, dmkv):
    def body(m_ref, g_ref, w_ref, d_ref, dg_ref, dw_ref):
        _, vjp = jax.vjp(_mem_kv_fn, m_ref[...], g_ref[...], w_ref[...].astype(F32))
        _, dg, dw = vjp(d_ref[...])
        dg_ref[...] = dg
        dw_ref[...] = dw.astype(BF16)

    return pl.pallas_call(
        body, name="mem_kv_bwd",
        out_shape=[jax.ShapeDtypeStruct((1, D_MODEL), F32), jax.ShapeDtypeStruct((D_MODEL, 2 * MEM_W), BF16)],
    )(mem, mg, w, dmkv)


def _mem_attn(proj, mkv, tm=256):
    t = proj.shape[0]
    tm = min(tm, t)

    def body(q_ref, z_ref, kv_ref, o_ref):
        o_ref[...] = _mem_fn(q_ref[...], z_ref[...], kv_ref[...]).astype(BF16)

    return pl.pallas_call(
        body,
        name="mem_attn",
        grid=(t // tm,),
        in_specs=[pl.BlockSpec((tm, MEM_W), lambda i: (i, O_MQ // MEM_W)),
                  pl.BlockSpec((tm, MEM_W), lambda i: (i, O_MZ // MEM_W)),
                  pl.BlockSpec((MEM_LEN, 2 * MEM_W), lambda i: (0, 0))],
        out_specs=pl.BlockSpec((tm, MEM_W), lambda i: (i, 0)),
        out_shape=jax.ShapeDtypeStruct((t, MEM_W), BF16),
    )(proj, proj, mkv)


def _mem_attn_bwd(proj, mkv, do, tm=256):
    t = proj.shape[0]
    tm = min(tm, t)

    def body(q_ref, z_ref, kv_ref, do_ref, dq_ref, dz_ref, dkv_ref):
        _, vjp = jax.vjp(_mem_fn, q_ref[...], z_ref[...], kv_ref[...])
        dq, dz, dkv = vjp(do_ref[...].astype(F32))
        dq_ref[...] = dq.astype(BF16)
        dz_ref[...] = dz.astype(BF16)

        @pl.when(pl.program_id(0) == 0)
        def _():
            dkv_ref[...] = jnp.zeros_like(dkv_ref)

        dkv_ref[...] += dkv

    blk = pl.BlockSpec((tm, MEM_W), lambda i: (i, 0))
    kv = pl.BlockSpec((MEM_LEN, 2 * MEM_W), lambda i: (0, 0))
    return pl.pallas_call(
        body,
        name="mem_attn_bwd",
        grid=(t // tm,),
        in_specs=[pl.BlockSpec((tm, MEM_W), lambda i: (i, O_MQ // MEM_W)),
                  pl.BlockSpec((tm, MEM_W), lambda i: (i, O_MZ // MEM_W)), kv, blk],
        out_specs=[blk, blk, kv],
        out_shape=[jax.ShapeDtypeStruct((t, MEM_W), BF16), jax.ShapeDtypeStruct((t, MEM_W), BF16),
                   jax.ShapeDtypeStruct((MEM_LEN, 2 * MEM_W), F32)],
    )(proj, proj, mkv, do)


def _proj_gather(h, w_alt, shards, tn=384):
    t = h.shape[0]
    n, kdim = w_alt.shape
    assert n % tn == 0
    nj = n // tn
    na = len(shards)

    def body(h_ref, w_ref, *rest):
        x_refs, o_ref, land = rest[:na], rest[na], rest[na + 1:2 * na + 1]
        send_sems, recv_sems, local_sems = rest[2 * na + 1:]
        j = pl.program_id(0)

        def copies():
            return _direct_gather_copies(x_refs, land, send_sems, recv_sems, local_sems)

        @pl.when(j == 0)
        def _():
            for cp in copies():
                cp.start()

        o_ref[...] = _dot(h_ref[...], w_ref[...], NT)

        @pl.when(j == nj - 1)
        def _():
            for cp in copies():
                cp.wait()

    outs = pl.pallas_call(
        body,
        name="proj",
        grid=(nj,),
        in_specs=[pl.BlockSpec((t, kdim), lambda j: (0, 0)), pl.BlockSpec((tn, kdim), lambda j: (j, 0))] + [ANY] * na,
        out_specs=[pl.BlockSpec((t, tn), lambda j: (0, j))] + [ANY] * na,
        out_shape=[jax.ShapeDtypeStruct((t, n), F32)]
        + [jax.ShapeDtypeStruct((N_DEV, *v.shape), v.dtype) for v in shards],
        scratch_shapes=_gather_sems(na),
    )(h, w_alt, *shards)
    return outs[0], outs[1:]


def _local_step(x, mem, tgt, norm_g, mem_norm_g, w_alt, alog_row, dtb_row, dn_norm_g, final_g, shards):
    h = _norm_in(x, norm_g)
    s_kv, s_dn, s_sb, s_out, s_mem, s_conv = shards
    proj, (g_kv, g_conv) = _proj_gather(h, w_alt, [s_kv, s_conv])
    w_mem_kv = g_kv.reshape(D_MODEL, 2 * MEM_W)
    conv_w = g_conv.transpose(1, 0, 2).reshape(CONV_K, 3 * D_MODEL)

    c = _dn_conv(proj, conv_w)
    o_dn, states = _dn_fwd(c, proj, alog_row, dtb_row, dn_norm_g)
    o_sb, o_sb_raw, (g_dn, g_sb, g_out, g_mem) = _sb_fwd(proj, [s_dn, s_sb, s_out, s_mem])
    w_br_dn = g_dn.reshape(D_MODEL, D_MODEL)
    w_br_sb = g_sb.reshape(D_MODEL, D_MODEL)
    w_out = g_out.reshape(D_MODEL, D_MODEL)
    w_br_mem = g_mem.transpose(1, 0, 2).reshape(MEM_W, D_MODEL)
    mkv = _mem_kv(mem, mem_norm_g, w_mem_kv)
    o_m = _mem_attn(proj, mkv)

    (loss, dout, d_final_g, merged, dy_dn, dy_sb, dy_m, dgates, do_dn, do_sb, do_m) = _block_tail(
        proj, o_dn, o_sb, o_m, x, tgt, w_br_dn, w_br_sb, w_br_mem, w_out, final_g)
    dw_out = _matmul(merged, dout, "tn", BF16, 256, 1024, 2048, "dw_out")
    dw_br_dn = _matmul(o_dn, dy_dn, "tn", BF16, 256, 1024, 2048, "dw_br_dn")
    dw_br_sb = _matmul(o_sb, dy_sb, "tn", BF16, 256, 1024, 2048, "dw_br_sb")
    dw_br_mem = _matmul(o_m, dy_m, "tn", BF16, 256, 1024, 2048, "dw_br_mem")

    dmq, dmz, dmkv = _mem_attn_bwd(proj, mkv, do_m)
    d_mem_norm_g, dw_mem_kv = _mem_kv_bwd(mem, mem_norm_g, w_mem_kv, dmkv)
    rows_d = D_MODEL // N_DEV
    small_blocks = jnp.concatenate([
        dw_br_dn.reshape(N_DEV, rows_d, D_MODEL), dw_br_sb.reshape(N_DEV, rows_d, D_MODEL),
        dw_out.reshape(N_DEV, rows_d, D_MODEL), dw_mem_kv.reshape(N_DEV, rows_d // 2, D_MODEL),
        dw_br_mem.reshape(MEM_W, N_DEV, rows_d).transpose(1, 0, 2).reshape(N_DEV, MEM_W // N_DEV, D_MODEL)], axis=1)
    dq_sb, dk_sb, dv_sb, dz_sb, small_parts = _sb_bwd(proj, o_sb_raw, do_sb, small_blocks)
    (d_small,) = _sum_slots([small_parts], "sum_small_grads")
    dcq, dck, dcv, dz_dn, dba, dscal, d_dn_norm_g = _dn_bwd(c, proj, alog_row, dtb_row, dn_norm_g, states, do_dn)
    dq_dn, dcw_q = _dn_conv_bwd(proj, conv_w, dcq, 0)
    dk_dn, dcw_k = _dn_conv_bwd(proj, conv_w, dck, 1)
    dv_dn, dcw_v = _dn_conv_bwd(proj, conv_w, dcv, 2)
    d_conv_w = jnp.concatenate([dcw_q, dcw_k, dcw_v], axis=1)

    dproj = jnp.concatenate([dq_dn, dk_dn, dv_dn, dz_dn, dq_sb, dk_sb, dv_sb, dz_sb, dmq, dmz, dgates,
                             dba.astype(BF16)], axis=1)
    dw_alt = _matmul(dproj, h, "tn", BF16, 384, 1024, 2048, "dw_alt")
    grad_x, d_norm_g = _grad_x(dproj, w_alt, x, norm_g, dout)
    return dict(loss=loss, grad_x=grad_x, norm_g=d_norm_g, mem_norm_g=d_mem_norm_g, w_alt=dw_alt, conv_w=d_conv_w,
                scal=dscal, dn_norm_g=d_dn_norm_g, small=d_small, final_g=d_final_g)


MESH = pl.DeviceIdType.MESH
ANY = pl.BlockSpec(memory_space=pl.ANY)


def _position():
    return lax.axis_index("x"), lax.axis_index("y"), lax.axis_index("c")


def _other_devices(me):
    return [tuple(1 - p if (f >> s) & 1 else p for p, s in zip(me, (2, 1, 0))) for f in range(1, N_DEV)]


def _direct_gather_copies(x_refs, land_refs, send_sems, recv_sems, local_sems):
    me = _position()
    mine = 4 * me[0] + 2 * me[1] + me[2]
    cps = []
    for a, (x_ref, land) in enumerate(zip(x_refs, land_refs)):
        cps.append(pltpu.make_async_copy(x_ref, land.at[mine], local_sems.at[a]))
        for k, peer in enumerate(_other_devices(me)):
            cps.append(pltpu.make_async_remote_copy(
                src_ref=x_ref, dst_ref=land.at[mine], send_sem=send_sems.at[7 * a + k],
                recv_sem=recv_sems.at[7 * a + k], device_id=peer, device_id_type=MESH))
    return cps


def _gather_sems(n):
    return [pltpu.SemaphoreType.DMA((7 * n,)), pltpu.SemaphoreType.DMA((7 * n,)), pltpu.SemaphoreType.DMA((n,))]


def _all_gather(xs, name):
    n = len(xs)

    def body(*refs):
        x_refs, o_refs = refs[:n], refs[n:2 * n]
        send_sems, recv_sems, local_sems = refs[2 * n:]
        x, y, c = _position()
        me, sibling = (x, y, c), (x, y, 1 - c)
        x_nbr, y_nbr, diag = (1 - x, y, c), (x, 1 - y, c), (1 - x, 1 - y, c)
        south = c == 0
        relay_from = tuple(jnp.where(south, a, b) for a, b in zip(y_nbr, x_nbr))
        relay_to = tuple(jnp.where(south, a, b) for a, b in zip(x_nbr, y_nbr))

        def slot(p):
            return 4 * p[0] + 2 * p[1] + p[2]

        def copy(a, k, block, to, src=None):
            dst = o_refs[a].at[slot(block)]
            return pltpu.make_async_remote_copy(
                src_ref=dst if src is None else src, dst_ref=dst, send_sem=send_sems.at[7 * a + k],
                recv_sem=recv_sems.at[7 * a + k], device_id=to, device_id_type=MESH)

        mine = [pltpu.make_async_copy(x_refs[a], o_refs[a].at[slot(me)], local_sems.at[a]) for a in range(n)]
        for cp in mine:
            cp.start()
        sends = []
        for a in range(n):
            sends += [copy(a, 0, me, sibling, src=x_refs[a]), copy(a, 1, me, x_nbr, src=x_refs[a]),
                      copy(a, 2, me, y_nbr, src=x_refs[a])]
        for cp in sends:
            cp.start()
        later = []
        for a in range(n):
            copy(a, 1, x_nbr, me).wait_recv()
            copy(a, 2, y_nbr, me).wait_recv()
            later += [copy(a, 3, relay_from, relay_to), copy(a, 4, x_nbr, sibling), copy(a, 5, y_nbr, sibling)]
            for cp in later[-3:]:
                cp.start()
        for a in range(n):
            copy(a, 3, diag, me).wait_recv()
            later.append(copy(a, 6, diag, sibling))
            later[-1].start()
        for a in range(n):
            copy(a, 0, sibling, me).wait_recv()
            for k, chip in ((4, x_nbr), (5, y_nbr), (6, diag)):
                copy(a, k, (chip[0], chip[1], 1 - c), me).wait_recv()
        for cp in sends + later:
            cp.wait_send()
        for cp in mine:
            cp.wait()

    return pl.pallas_call(
        body,
        name=name,
        in_specs=[ANY] * n,
        out_specs=[ANY] * n,
        out_shape=[jax.ShapeDtypeStruct((N_DEV, *v.shape), v.dtype) for v in xs],
        scratch_shapes=[pltpu.SemaphoreType.DMA((7 * n,)), pltpu.SemaphoreType.DMA((7 * n,)),
                        pltpu.SemaphoreType.DMA((n,))],
    )(*xs)


def _window_view(ref, dest):
    return ref.at[pl.ds(LANE * WIN_START[dest], WIN_W), :]


def _chunk_rows(rows, cols):
    return max(ch for ch in range(16, rows + 1, 16) if rows % ch == 0 and ch * cols <= (1 << 19))


def _halving_stage(xs, axis, name, out_dtype, windowed=(), gather=()):
    n_arr = len(xs)
    metas = []
    for k, v in enumerate(xs):
        if k in windowed:
            metas.append((N_DEV // 2, WIN_W, v.shape[1]))
        else:
            assert v.shape[1] == 2
            metas.append((v.shape[0], v.shape[2], v.shape[3]))
    chunk = [_chunk_rows(r, c) for (_, r, c) in metas]
    offs = [sum(m[0] for m in metas[:k]) for k in range(n_arr)]
    n_sem = sum(m[0] for m in metas)

    n_g = len(gather)

    def body(*refs):
        x_refs, g_refs = refs[:n_arr], refs[n_arr:n_arr + n_g]
        outs = refs[n_arr + n_g:]
        o_refs, land_refs, gl_refs = outs[:n_arr], outs[n_arr:2 * n_arr], outs[2 * n_arr:2 * n_arr + n_g]
        rest = outs[2 * n_arr + n_g:]
        bufs = rest[:3 * n_arr]
        send_sems, recv_sems, in_sems, out_sems = rest[3 * n_arr:3 * n_arr + 4]
        gathers = _direct_gather_copies(g_refs, gl_refs, *rest[3 * n_arr + 4:]) if n_g else []
        for cp in gathers:
            cp.start()
        pos = dict(zip("xyc", _position()))
        bit = pos[axis]
        peer = tuple(1 - pos[a] if a == axis else pos[a] for a in "xyc")

        def view(k, i, b):
            if k in windowed:
                return _window_view(x_refs[k], 2 * i + b)
            return x_refs[k].at[i, b]

        def add_blocks(k, a_view, b_view, o_view):
            _hbm_add(a_view, b_view, o_view, bufs[3 * k:3 * k + 3], in_sems, out_sems, chunk[k])

        for b in (0, 1):
            @pl.when(bit == b)
            def _(b=b):
                sends = []
                for k in range(n_arr):
                    for i in range(metas[k][0]):
                        cp = pltpu.make_async_remote_copy(
                            src_ref=view(k, i, 1 - b), dst_ref=land_refs[k].at[i], send_sem=send_sems.at[offs[k] + i],
                            recv_sem=recv_sems.at[offs[k] + i], device_id=peer, device_id_type=MESH)
                        cp.start()
                        sends.append(cp)
                idx = 0
                for k in range(n_arr):
                    for i in range(metas[k][0]):
                        sends[idx].wait_recv()
                        add_blocks(k, view(k, i, b), land_refs[k].at[i], o_refs[k].at[i])
                        idx += 1
                for cp in sends:
                    cp.wait_send()

        for cp in gathers:
            cp.wait()

    out_shape = [jax.ShapeDtypeStruct(m, out_dtype) for m in metas]
    land_shape = [jax.ShapeDtypeStruct(m, v.dtype) for m, v in zip(metas, xs)]
    g_shape = [jax.ShapeDtypeStruct((N_DEV, *v.shape), v.dtype) for v in gather]
    scratch = []
    for k in range(n_arr):
        blk = (2, chunk[k], metas[k][2])
        scratch += [pltpu.VMEM(blk, xs[k].dtype)] * 2 + [pltpu.VMEM(blk, out_dtype)]
    scratch += [pltpu.SemaphoreType.DMA((n_sem,)), pltpu.SemaphoreType.DMA((n_sem,)),
                pltpu.SemaphoreType.DMA((2, 2)), pltpu.SemaphoreType.DMA((2,))]
    if n_g:
        scratch += _gather_sems(n_g)
    outs = pl.pallas_call(
        body,
        name=name,
        in_specs=[ANY] * (n_arr + n_g),
        out_specs=[ANY] * (2 * n_arr + n_g),
        out_shape=out_shape + land_shape + g_shape,
        scratch_shapes=scratch,
    )(*xs, *gather)
    return outs[:n_arr], outs[2 * n_arr:]


def _hbm_add(a_view, b_view, o_view, bufs, in_sems, out_sems, ch):
    rows = a_view.shape[0]
    nch = rows // ch
    va, vb, vo = bufs

    def rows_of(j):
        return pl.ds(pl.multiple_of(j * ch, 16), ch)

    def loads(j, s):
        return (pltpu.make_async_copy(a_view.at[rows_of(j), :], va.at[s], in_sems.at[0, s]),
                pltpu.make_async_copy(b_view.at[rows_of(j), :], vb.at[s], in_sems.at[1, s]))

    def store(j, s):
        return pltpu.make_async_copy(vo.at[s], o_view.at[rows_of(j), :], out_sems.at[s])

    for cp in loads(0, 0):
        cp.start()

    def step(j, _):
        s = lax.rem(j, 2)

        @pl.when(j + 1 < nch)
        def _():
            for cp in loads(j + 1, 1 - s):
                cp.start()

        for cp in loads(j, s):
            cp.wait()

        @pl.when(j >= 2)
        def _():
            store(j - 2, s).wait()

        vo[s] = (va[s].astype(F32) + vb[s].astype(F32)).astype(vo.dtype)
        store(j, s).start()
        return 0

    lax.fori_loop(0, nch, step, 0)
    for j in range(max(0, nch - 2), nch):
        store(j, j % 2).wait()


def _xy_stage(xs, first, name):
    n_arr = len(xs)
    if first:
        shapes = [(v.shape[2] // 2, v.shape[3]) for v in xs]
        ins = list(xs)
    else:
        shapes = [(a.shape[1], a.shape[2]) for a, _ in xs]
        ins = [v for pair in xs for v in pair]
    n_blk = 2 if first else 1
    out_dtype = BF16 if first else F32
    chunk = [_chunk_rows(r, c) for (r, c) in shapes]
    n_sem = 2 * n_blk * n_arr

    def body(*refs):
        n_in = len(ins)
        in_refs = refs[:n_in]
        n_out = 2 * n_arr if first else n_arr
        o_refs = refs[n_in:n_in + n_out]
        land = refs[n_in + n_out:n_in + n_out + 2 * n_arr]
        rest = refs[n_in + n_out + 2 * n_arr:]
        bufs = rest[:3 * n_arr]
        send_sems, recv_sems, in_sems, out_sems = rest[3 * n_arr:]
        x, y, c = _position()
        peers = {"x": (1 - x, y, c), "y": (x, 1 - y, c)}
        jobs = []
        for k in range(n_arr):
            r, _ = shapes[k]
            half_a, half_b = pl.ds(0, r), pl.ds(r, r)
            if first:
                src = in_refs[k]
                for i in range(2):
                    jobs.append((k, src.at[i, 1 - y, half_a, :], src.at[i, y, half_a, :], land[2 * k].at[i],
                                 o_refs[2 * k].at[i], "y"))
                    jobs.append((k, src.at[1 - x, i, half_b, :], src.at[x, i, half_b, :], land[2 * k + 1].at[i],
                                 o_refs[2 * k + 1].at[i], "x"))
            else:
                a1, b1 = in_refs[2 * k], in_refs[2 * k + 1]
                jobs.append((k, a1.at[1 - x], a1.at[x], land[2 * k], o_refs[k].at[half_a, :], "x"))
                jobs.append((k, b1.at[1 - y], b1.at[y], land[2 * k + 1], o_refs[k].at[half_b, :], "y"))
        sends = []
        for n, (k, send, _, landing, _, axis) in enumerate(jobs):
            cp = pltpu.make_async_remote_copy(src_ref=send, dst_ref=landing, send_sem=send_sems.at[n],
                                              recv_sem=recv_sems.at[n], device_id=peers[axis], device_id_type=MESH)
            cp.start()
            sends.append(cp)
        for cp, (k, _, kept, landing, out, _) in zip(sends, jobs):
            cp.wait_recv()
            _hbm_add(kept, landing, out, bufs[3 * k:3 * k + 3], in_sems, out_sems, chunk[k])
        for cp in sends:
            cp.wait_send()

    if first:
        out_shape = [jax.ShapeDtypeStruct((2, r, c), BF16) for (r, c) in shapes for _ in range(2)]
        land_shape = out_shape
    else:
        out_shape = [jax.ShapeDtypeStruct((2 * r, c), F32) for (r, c) in shapes]
        land_shape = [jax.ShapeDtypeStruct((r, c), BF16) for (r, c) in shapes for _ in range(2)]
    scratch = []
    for k in range(n_arr):
        scratch += [pltpu.VMEM((2, chunk[k], shapes[k][1]), BF16)] * 2 + [pltpu.VMEM((2, chunk[k], shapes[k][1]), out_dtype)]
    scratch += [pltpu.SemaphoreType.DMA((n_sem,)), pltpu.SemaphoreType.DMA((n_sem,)),
                pltpu.SemaphoreType.DMA((2, 2)), pltpu.SemaphoreType.DMA((2,))]
    outs = pl.pallas_call(
        body,
        name=name,
        in_specs=[ANY] * len(ins),
        out_specs=[ANY] * (len(out_shape) + len(land_shape)),
        out_shape=out_shape + land_shape,
        scratch_shapes=scratch,
    )(*ins)
    outs = outs[:len(out_shape)]
    return [(outs[2 * k], outs[2 * k + 1]) for k in range(n_arr)] if first else list(outs)


def _reduce_scatter(dw_al, blocks, gather=()):
    xs = [dw_al] + [b.reshape(N_DEV // 2, 2, *b.shape[1:]) for b in blocks]
    ys, gathered = _halving_stage(xs, "c", "rs_c", BF16, windowed=(0,), gather=gather)
    pairs = _xy_stage([v.reshape(2, 2, *v.shape[1:]) for v in ys], True, "rs_xy1")
    return _xy_stage(pairs, False, "rs_xy2"), gathered


def _sum_slots(gs, name):
    n = len(gs)

    def body(*refs):
        for g_ref, o_ref in zip(refs[:n], refs[n:]):
            acc = g_ref[0].astype(F32)
            for d in range(1, N_DEV):
                acc = acc + g_ref[d].astype(F32)
            o_ref[...] = acc

    return pl.pallas_call(body, name=name, out_shape=[jax.ShapeDtypeStruct(g.shape[1:], F32) for g in gs])(*gs)


def _assemble_w_al(wins, bas):
    ba_tile = O_BA // LANE
    assert W_AL // LANE == ba_tile + 1
    cols = wins.shape[2]
    n_buf = 3
    ends = [WIN_START[d + 1] if d + 1 < N_DEV else ba_tile + 1 for d in range(N_DEV)]
    assert WIN_START[N_DEV - 1] + WIN_TILES == ba_tile + 1

    def body(w_ref, ba_ref, o_ref, buf, ld_sems, st_sems, ba_sem):
        def load(d):
            return pltpu.make_async_copy(w_ref.at[d], buf.at[d % n_buf], ld_sems.at[d % n_buf])

        def store(d):
            n = LANE * (ends[d] - WIN_START[d])
            return pltpu.make_async_copy(buf.at[d % n_buf, pl.ds(0, n), :],
                                         o_ref.at[pl.ds(LANE * WIN_START[d], n), :], st_sems.at[d % n_buf])

        load(0).start()
        for d in range(N_DEV):
            if d + 1 < N_DEV:
                if d + 1 >= n_buf:
                    store(d + 1 - n_buf).wait()
                load(d + 1).start()
            load(d).wait()
            if d > 0:
                ov = LANE * (WIN_START[d - 1] + WIN_TILES - WIN_START[d])
                buf[d % n_buf, :ov, :] = buf[d % n_buf, :ov, :] + buf[(d - 1) % n_buf, WIN_W - ov:, :]
            if d == N_DEV - 1:
                ba_copy = pltpu.make_async_copy(
                    ba_ref.at[BA_DEV], buf.at[d % n_buf, pl.ds(WIN_W - LANE, ba_ref.shape[1]), :], ba_sem)
                ba_copy.start()
                ba_copy.wait()
            store(d).start()
        for d in range(N_DEV - n_buf, N_DEV):
            store(d).wait()

    return pl.pallas_call(
        body,
        name="assemble_w_al",
        in_specs=[ANY, ANY],
        out_specs=ANY,
        out_shape=jax.ShapeDtypeStruct((W_AL, cols), wins.dtype),
        scratch_shapes=[pltpu.VMEM((n_buf, WIN_W, cols), wins.dtype), pltpu.SemaphoreType.DMA((n_buf,)),
                        pltpu.SemaphoreType.DMA((n_buf,)), pltpu.SemaphoreType.DMA],
    )(wins, bas)


def _adamw_math(w, g, m, v):
    m_new = ADAM_B1 * m + (1.0 - ADAM_B1) * g
    v_new = ADAM_B2 * v + (1.0 - ADAM_B2) * (g * g)
    m_hat = m_new / (1.0 - ADAM_B1 ** ADAM_STEP)
    v_hat = v_new / (1.0 - ADAM_B2 ** ADAM_STEP)
    return -ADAM_LR * (m_hat / (jnp.sqrt(v_hat) + ADAM_EPS) + ADAM_WD * w), m_new, v_new


def _adamw(w, g, m, v, name, tb=134):
    r, _, c = w.shape
    assert r % tb == 0

    def body(w_ref, g_ref, m_ref, v_ref, d_ref, nm_ref, nv_ref):
        d_ref[...], nm_ref[...], nv_ref[...] = _adamw_math(w_ref[...], g_ref[...], m_ref[...], v_ref[...])

    blk = pl.BlockSpec((tb, 1, c), lambda i: (i, 0, 0))
    o = jax.ShapeDtypeStruct(w.shape, F32)
    return pl.pallas_call(body, name=name, grid=(r // tb,), in_specs=[blk] * 4, out_specs=[blk] * 3,
                          out_shape=[o, o, o])(w, g, m, v)


def _adamw_many(ws, gs, ms, vs, name):
    n = len(ws)

    def body(*refs):
        for k in range(n):
            w_ref, g_ref, m_ref, v_ref = (refs[j * n + k] for j in range(4))
            d_ref, nm_ref, nv_ref = (refs[(4 + j) * n + k] for j in range(3))
            d_ref[...], nm_ref[...], nv_ref[...] = _adamw_math(w_ref[...], g_ref[...], m_ref[...], v_ref[...])

    shapes = [jax.ShapeDtypeStruct(w.shape, F32) for w in ws]
    outs = pl.pallas_call(body, name=name, out_shape=shapes * 3)(*ws, *gs, *ms, *vs)
    return outs[:n], outs[n:2 * n], outs[2 * n:]


def _select(me, table):
    return sum(jnp.where(me == d, jnp.int32(v), jnp.int32(0)) for d, v in enumerate(table))


WIN_SHIFT = tuple(SHARD_W * d - LANE * WIN_START[d] for d in range(N_DEV))
PAD_L = 256
PAD_R = 256


def _shard_to_window(shard_t, me):
    shift = _select(me, WIN_SHIFT)
    start = _select(me, WIN_START)
    padded = jnp.pad(shard_t, ((PAD_L, PAD_R), (0, 0)))
    cols = shard_t.shape[1]
    lo = lax.dynamic_slice(padded, (PAD_L - shift, 0), (WIN_W, cols))
    hi = lax.dynamic_slice(padded, (PAD_L - shift + N_BA, 0), (WIN_W, cols))
    aligned = LANE * start + lax.broadcasted_iota(jnp.int32, (WIN_W, 1), 0)
    return jnp.where(aligned >= ORIG_BA, hi, lo)


def _window_to_shard(win, ba_grad, me):
    shift = _select(me, WIN_SHIFT)
    cols = win.shape[1]
    padded = jnp.pad(win, ((N_BA, PAD_R), (0, 0)))
    lo = lax.dynamic_slice(padded, (N_BA + shift, 0), (SHARD_W, cols))
    hi = lax.dynamic_slice(padded, (shift, 0), (SHARD_W, cols))
    orig = SHARD_W * me + lax.broadcasted_iota(jnp.int32, (SHARD_W, 1), 0)
    ba_full = lax.dynamic_update_slice(jnp.zeros((SHARD_W, cols), win.dtype), ba_grad, (BA_LOCAL, 0))
    return jnp.where(orig < ORIG_BA, lo, jnp.where(orig >= ORIG_BA + N_BA, hi, ba_full))


def _pad_row(v, width=D_MODEL):
    v = v.reshape(1, -1)
    return jnp.pad(v, ((0, 0), (0, width - v.shape[1])))


def _slab(v, rows=8):
    return jnp.pad(v, ((0, rows - v.shape[0]), (0, D_MODEL - v.shape[1])))


def kernel(x, mem, norm_g, mem_norm_g, w_in, conv_w, a_log, dt_bias, dn_norm_g, w_mem_kv, w_br_dn, w_br_sb, w_br_mem, w_out, final_g, loss_target, m_norm_g, m_mem_norm_g, m_w_in, m_conv_w, m_a_log, m_dt_bias, m_dn_norm_g, m_w_mem_kv, m_w_br_dn, m_w_br_sb, m_w_br_mem, m_w_out, m_final_g, v_norm_g, v_mem_norm_g, v_w_in, v_conv_w, v_a_log, v_dt_bias, v_dn_norm_g, v_w_mem_kv, v_w_br_dn, v_w_br_sb, v_w_br_mem, v_w_out, v_final_g):
    xi, yi, ci = _position()
    me = 4 * xi + 2 * yi + ci

    shard_t = w_in[0].T
    win = _shard_to_window(shard_t, me).astype(BF16)
    ba = shard_t[BA_LOCAL:BA_LOCAL + N_BA, :].astype(BF16)
    g_win, g_ba = _all_gather([win, ba], "gather_weights")
    w_alt = _assemble_w_al(g_win, g_ba)

    shards = [w_mem_kv[0].astype(BF16), w_br_dn[0].astype(BF16), w_br_sb[0].astype(BF16), w_out[0].astype(BF16),
              w_br_mem[0].astype(BF16), conv_w[0]]
    r = _local_step(x[0], mem[0], loss_target[0], norm_g, mem_norm_g, w_alt, _pad_row(a_log, LANE),
                    _pad_row(dt_bias, LANE), dn_norm_g, final_g.reshape(1, D_MODEL), shards)

    dw_alt = r["w_alt"]
    parts = [r["norm_g"], r["mem_norm_g"], r["final_g"], r["dn_norm_g"], r["scal"], r["loss"], r["conv_w"],
             dw_alt[O_BA:O_BA + N_BA, :].astype(F32)]
    (g_win,), gathered = _reduce_scatter(dw_alt, [], gather=parts)
    rows_d = D_MODEL // N_DEV
    g_small = r["small"]
    g_dn, g_sb, g_out = (g_small[k * rows_d:(k + 1) * rows_d] for k in range(3))
    g_kv = g_small[3 * rows_d:3 * rows_d + rows_d // 2].reshape(rows_d, 2 * MEM_W)
    g_mem = g_small[3 * rows_d + rows_d // 2:].reshape(MEM_W, rows_d)
    s_norm_g, s_mem_norm_g, s_final_g, s_dn_norm_g, s_scal, s_loss, s_conv, s_ba = _sum_slots(gathered, "sum_small")
    loss = s_loss[0, 0]
    cw = conv_w.shape[2]
    g_conv = lax.dynamic_slice(s_conv, (0, cw * me), (CONV_K, cw))
    g_w_in_t = _window_to_shard(g_win, s_ba, me)
    grads = dict(norm_g=s_norm_g, mem_norm_g=s_mem_norm_g, w_in=g_w_in_t.T[None], conv_w=g_conv[None],
                 a_log=s_scal[0:1, :N_HEADS], dt_bias=s_scal[1:2, :N_HEADS], dn_norm_g=s_dn_norm_g, w_mem_kv=g_kv[None],
                 w_br_dn=g_dn[None], w_br_sb=g_sb[None], w_br_mem=g_mem[None], w_out=g_out[None],
                 final_g=s_final_g.reshape(D_MODEL))

    params = dict(norm_g=(norm_g, m_norm_g, v_norm_g), mem_norm_g=(mem_norm_g, m_mem_norm_g, v_mem_norm_g),
                  w_in=(w_in, m_w_in, v_w_in), conv_w=(conv_w, m_conv_w, v_conv_w), a_log=(a_log, m_a_log, v_a_log),
                  dt_bias=(dt_bias, m_dt_bias, v_dt_bias), dn_norm_g=(dn_norm_g, m_dn_norm_g, v_dn_norm_g),
                  w_mem_kv=(w_mem_kv, m_w_mem_kv, v_w_mem_kv), w_br_dn=(w_br_dn, m_w_br_dn, v_w_br_dn),
                  w_br_sb=(w_br_sb, m_w_br_sb, v_w_br_sb), w_br_mem=(w_br_mem, m_w_br_mem, v_w_br_mem),
                  w_out=(w_out, m_w_out, v_w_out), final_g=(final_g, m_final_g, v_final_g))
    order = list(params)
    deltas, new_m, new_v = {}, {}, {}
    deltas["w_in"], new_m["w_in"], new_v["w_in"] = (jnp.transpose(o, (1, 2, 0)) for o in _adamw(
        jnp.transpose(w_in, (2, 0, 1)), g_w_in_t[:, None, :], jnp.transpose(m_w_in, (2, 0, 1)),
        jnp.transpose(v_w_in, (2, 0, 1)), "adamw_w_in"))
    rest = [nm for nm in order if nm != "w_in"]

    def two_d(a):
        return a.reshape(1, -1) if a.ndim == 1 else a

    d_l, m_l, v_l = _adamw_many([two_d(params[nm][0]) for nm in rest], [two_d(grads[nm]) for nm in rest],
                                [two_d(params[nm][1]) for nm in rest], [two_d(params[nm][2]) for nm in rest], "adamw_rest")
    for k, nm in enumerate(rest):
        shp = params[nm][0].shape
        deltas[nm], new_m[nm], new_v[nm] = d_l[k].reshape(shp), m_l[k].reshape(shp), v_l[k].reshape(shp)
    return (loss, r["grad_x"][None], *[grads[nm] for nm in order], *[deltas[nm] for nm in order],
            *[new_m[nm] for nm in order], *[new_v[nm] for nm in order])
```

```python
import functools
import math

import jax
import jax.numpy as jnp
from jax import lax
from jax.experimental import pallas as pl
from jax.experimental.pallas import tpu as pltpu

F32 = jnp.float32
BF16 = jnp.bfloat16

D_MODEL = 1024
N_DEV = 8
N_HEADS = 8
D_HEAD = 128
DN_CHUNK = 64
CONV_K = 4
MEM_LEN = 256
MEM_HEADS = 4
MEM_DH = 64
MEM_W = MEM_HEADS * MEM_DH
NORM_EPS = 1e-6
IN_WIDTH = 11792
SHARD_W = IN_WIDTH // N_DEV

LANE = 128
SUPER = 2 * DN_CHUNK

O_QKV_DN = 0
O_Z_DN = 3072
O_QKV_SB = 4096
O_Z_SB = 7168
O_MQ = 8192
O_MZ = 8448
O_GATES = 8704
O_BA = 11776
W_AL = 11904
ORIG_BA = 4096
N_BA = 16

BA_DEV = ORIG_BA // SHARD_W
BA_LOCAL = ORIG_BA - BA_DEV * SHARD_W


def _aligned_col(o):
    return o if o < ORIG_BA else o - N_BA


ROW_TILE = 16
WIN_W = 1504
WIN_ROW0 = tuple(_aligned_col(SHARD_W * d) // ROW_TILE * ROW_TILE for d in range(N_DEV))
assert not any(ORIG_BA <= SHARD_W * d < ORIG_BA + N_BA for d in range(N_DEV))
assert all(WIN_ROW0[d] + WIN_W >= _aligned_col(SHARD_W * (d + 1) - 1) + 1 for d in range(N_DEV))
assert all(WIN_ROW0[d + 1] <= WIN_ROW0[d] + WIN_W for d in range(N_DEV - 1))
assert WIN_ROW0[-1] + WIN_W == O_BA + N_BA

ADAM_LR = 0.001
ADAM_B1 = 0.9
ADAM_B2 = 0.999
ADAM_EPS = 1e-08
ADAM_WD = 0.01
ADAM_STEP = 10

NN = (((1,), (0,)), ((), ()))
NT = (((1,), (1,)), ((), ()))
TN = (((0,), (0,)), ((), ()))


def _dot(a, b, dims):
    return lax.dot_general(a.astype(BF16), b.astype(BF16), dims, preferred_element_type=F32)


def _split2(a):
    hi = a.astype(BF16)
    lo = (a - hi.astype(F32)).astype(BF16)
    return hi, lo


def _dot3(a, b, dims):
    ah, al = _split2(a)
    bh, bl = _split2(b)
    d = functools.partial(lax.dot_general, dimension_numbers=dims, preferred_element_type=F32)
    return d(ah, bh) + (d(ah, bl) + d(al, bh))


def _sel_dot_impl(sel01, x, dims):
    sel = sel01.astype(BF16)
    h1 = x.astype(BF16)
    r1 = x - h1.astype(F32)
    h2 = r1.astype(BF16)
    h3 = (r1 - h2.astype(F32)).astype(BF16)
    d = functools.partial(lax.dot_general, dimension_numbers=dims, preferred_element_type=F32)
    return d(sel, h1) + (d(sel, h2) + d(sel, h3))


@jax.custom_vjp
def _sel_dot(sel01, x):
    return _sel_dot_impl(sel01, x, NN)


_sel_dot.defvjp(lambda s, x: (_sel_dot(s, x), s),
                lambda s, g: (jnp.zeros_like(s), _sel_dot_impl(s, g, TN)))


def _make_mm(dotfn):
    @jax.custom_vjp
    def nn(a, b):
        return dotfn(a, b, NN)

    @jax.custom_vjp
    def nt(a, b):
        return dotfn(a, b, NT)

    @jax.custom_vjp
    def tn(a, b):
        return dotfn(a, b, TN)

    nn.defvjp(lambda a, b: (nn(a, b), (a, b)), lambda r, g: (nt(g, r[1]), tn(r[0], g)))
    nt.defvjp(lambda a, b: (nt(a, b), (a, b)), lambda r, g: (nn(g, r[1]), tn(g, r[0])))
    tn.defvjp(lambda a, b: (tn(a, b), (a, b)), lambda r, g: (nt(r[1], g), nn(r[0], g)))
    return nn, nt, tn


mm_nn, mm_nt, mm_tn = _make_mm(_dot)
mm3_nn, mm3_nt, mm3_tn = _make_mm(_dot3)


def _sigmoid(x):
    return jax.nn.sigmoid(x)


def _silu(x):
    return x * _sigmoid(x)


def _softplus_parts(x):
    sp = jnp.log1p(jnp.exp(-jnp.abs(x)))
    return jnp.maximum(x, 0.0) + sp, jnp.maximum(-x, 0.0) + sp


def _rmsnorm(x, g):
    return x * lax.rsqrt(jnp.mean(x * x, axis=-1, keepdims=True) + NORM_EPS) * g


def _iota2(shape, dim):
    return lax.broadcasted_iota(jnp.int32, shape, dim)


def _div64(i):
    return lax.shift_right_logical(i, jnp.full(i.shape, 6, jnp.int32))


def _each(f, *lists):
    return [f(*a) for a in zip(*lists)]


@jax.custom_vjp
def _inv_unit_lower(ms):
    n = ms[0].shape[0]
    eye = (_iota2((n, n), 0) == _iota2((n, n), 1)).astype(F32)
    rs = [eye - m for m in ms]
    ps = ms
    for _ in range(5):
        ps = _each(mm3_nn, ps, ps)
        rs = _each(lambda r, p: r + mm_nn(r, p), rs, ps)
    return rs


def _inv_fwd(ms):
    rs = _inv_unit_lower(ms)
    return rs, rs


def _inv_bwd(rs, gs):
    ts = _each(mm_tn, rs, gs)
    return (_each(lambda t, r: -mm_nt(t, r), ts, rs),)


_inv_unit_lower.defvjp(_inv_fwd, _inv_bwd)


def _dn_block(cq, ck, cv, bcol, acol, zt, alog, dtb, gn, s0):
    n = SUPER
    h = DN_CHUNK
    row = _iota2((n, n), 0)
    col = _iota2((n, n), 1)
    same = _div64(row) == _div64(col)
    incl = jnp.logical_and(same, row >= col)
    strict = jnp.logical_and(same, row > col)
    incl_f = incl.astype(F32)

    qn = _each(lambda x: x * lax.rsqrt(jnp.sum(x * x, axis=-1, keepdims=True) + NORM_EPS) * (D_HEAD ** -0.5), cq)
    kn = _each(lambda x: x * lax.rsqrt(jnp.sum(x * x, axis=-1, keepdims=True) + NORM_EPS), ck)
    beta = _each(_sigmoid, bcol)
    g = _each(lambda al, ac, dt: -(jnp.exp(al) * _softplus_parts(ac + dt)[0]), alog, acol, dtb)
    gcum = _each(lambda x: _sel_dot(incl_f, jnp.broadcast_to(x, (n, n))), g)
    gam_incl = _each(lambda x: jnp.where(incl, jnp.exp(jnp.where(incl, x - x.T, 0.0)), 0.0), gcum)
    kk = _each(mm_nt, kn, kn)
    t_inv = _inv_unit_lower(_each(lambda b, x, gm: b * x * jnp.where(strict, gm, 0.0), beta, kk, gam_incl))
    eg = _each(jnp.exp, gcum)
    u = _each(lambda t, v, b: mm_nn(t, v * b), t_inv, cv, beta)
    w = _each(lambda t, k, b, e: mm_nn(t, k * (b * e)), t_inv, kn, beta, eg)
    a_intra = _each(lambda q, k, gm: mm_nt(q, k) * gm, qn, kn, gam_incl)
    q_dec = _each(lambda q, e: q * e, qn, eg)
    last0 = _each(lambda x: x[h - 1:h, :], gcum)
    last1 = _each(lambda x: x[n - 1:n, :], gcum)
    k_dec = _each(lambda k, x, l0, l1: k * jnp.exp(jnp.concatenate(
        [jnp.broadcast_to(l0, (h, n)), jnp.broadcast_to(l1, (h, n))], axis=0) - x), kn, gcum, last0, last1)
    v0 = _each(lambda uu, ww, s: uu[:h] - mm_nn(ww[:h], s), u, w, s0)
    o0 = _each(lambda q, s: mm_nn(q[:h], s), q_dec, s0)
    s1 = _each(lambda s, l0, k, v: s * jnp.exp(l0) + mm_tn(k[:h], v), s0, last0, k_dec, v0)
    v1 = _each(lambda uu, ww, s: uu[h:] - mm_nn(ww[h:], s), u, w, s1)
    o1 = _each(lambda q, s: mm_nn(q[h:], s), q_dec, s1)
    s2 = _each(lambda s, l1, k, v: s * jnp.exp(l1) + mm_tn(k[h:], v), s1, last1, k_dec, v1)
    o = _each(lambda a, b, am, x, y: jnp.concatenate([a, b], axis=0) + mm_nn(am, jnp.concatenate([x, y], axis=0)),
              o0, o1, a_intra, v0, v1)
    out = _each(lambda x, z: _rmsnorm(x, gn) * _silu(z), o, zt)
    return out, s2


def _mem_fn(mq, mz, mkv):
    mk = mkv[:, :MEM_W]
    mv = mkv[:, MEM_W:]
    lane = _iota2((1, MEM_W), 1)
    out = jnp.zeros(mq.shape, F32)
    for hd in range(MEM_HEADS):
        hm = (_div64(lane) == hd).astype(F32)
        s = mm_nt(mq * hm, mk) * (1.0 / math.sqrt(MEM_DH))
        s = s - jnp.max(s, axis=-1, keepdims=True)
        e = jnp.exp(s)
        p = e / jnp.sum(e, axis=-1, keepdims=True)
        out = out + mm_nn(p, mv) * hm
    return out * _silu(mz)


def _merge_fn(gd, gs, gm, yd, ys, ym):
    return _sigmoid(gd) * yd + _sigmoid(gs) * ys + _sigmoid(gm) * ym


def _loss_fn(x, mo, fg, tgt):
    y = _rmsnorm(x + mo, fg)
    err = y - tgt
    return 0.5 * jnp.sum(jnp.mean(err * err, axis=-1, keepdims=True), axis=0, keepdims=True)


def _matmul(a, b, mode, out_dtype, tm, tn, tk, name, b_col0=0, n_cols=None):
    if mode == "nn":
        m, kdim = a.shape
        n = b.shape[1] if n_cols is None else n_cols
    elif mode == "nt":
        m, kdim = a.shape
        n = b.shape[0]
    else:
        kdim, m = a.shape
        n = b.shape[1] if n_cols is None else n_cols
    tm, tn, tk = min(tm, m), min(tn, n), min(tk, kdim)
    assert m % tm == 0 and n % tn == 0 and kdim % tk == 0 and b_col0 % tn == 0
    nk = kdim // tk
    jb = b_col0 // tn
    dims = {"nn": NN, "nt": NT, "tn": TN}[mode]

    def body(a_ref, b_ref, o_ref, acc_ref):
        k = pl.program_id(2)
        part = _dot(a_ref[...], b_ref[...], dims)

        @pl.when(k == 0)
        def _():
            acc_ref[...] = part

        @pl.when(k > 0)
        def _():
            acc_ref[...] += part

        @pl.when(k == nk - 1)
        def _():
            o_ref[...] = acc_ref[...].astype(o_ref.dtype)

    if mode == "nn":
        a_spec = pl.BlockSpec((tm, tk), lambda i, j, k: (i, k))
        b_spec = pl.BlockSpec((tk, tn), lambda i, j, k: (k, j + jb))
    elif mode == "nt":
        a_spec = pl.BlockSpec((tm, tk), lambda i, j, k: (i, k))
        b_spec = pl.BlockSpec((tn, tk), lambda i, j, k: (j, k))
    else:
        a_spec = pl.BlockSpec((tk, tm), lambda i, j, k: (k, i))
        b_spec = pl.BlockSpec((tk, tn), lambda i, j, k: (k, j + jb))
    return pl.pallas_call(
        body,
        name=name,
        grid=(m // tm, n // tn, nk),
        in_specs=[a_spec, b_spec],
        out_specs=pl.BlockSpec((tm, tn), lambda i, j, k: (i, j)),
        out_shape=jax.ShapeDtypeStruct((m, n), out_dtype),
        scratch_shapes=[pltpu.VMEM((tm, tn), F32)],
        compiler_params=pltpu.CompilerParams(dimension_semantics=("parallel", "parallel", "arbitrary")),
    )(a, b)


def _norm_in(x, g, tm=256):
    t = x.shape[0]

    def body(x_ref, g_ref, h_ref):
        h_ref[...] = _rmsnorm(x_ref[...], g_ref[...]).astype(BF16)

    return pl.pallas_call(
        body,
        name="norm_in",
        grid=(t // tm,),
        in_specs=[pl.BlockSpec((tm, D_MODEL), lambda i: (i, 0)), pl.BlockSpec((1, D_MODEL), lambda i: (0, 0))],
        out_specs=pl.BlockSpec((tm, D_MODEL), lambda i: (i, 0)),
        out_shape=jax.ShapeDtypeStruct((t, D_MODEL), BF16),
    )(x, g)


def _grad_x(dproj, w_alt, x, g, dres, tm=512, tk=3968):
    t, kdim = dproj.shape
    tm = min(tm, t)
    assert kdim % tk == 0 and t % tm == 0
    nk = kdim // tk

    def body(a_ref, b_ref, x_ref, g_ref, dres_ref, dx_ref, dg_ref, acc_ref):
        i, k = pl.program_id(0), pl.program_id(1)
        part = _dot(a_ref[...], b_ref[...], NN)

        @pl.when(k == 0)
        def _():
            acc_ref[...] = part

        @pl.when(k > 0)
        def _():
            acc_ref[...] += part

        @pl.when(jnp.logical_and(i == 0, k == 0))
        def _():
            dg_ref[...] = jnp.zeros_like(dg_ref)

        @pl.when(k == nk - 1)
        def _():
            _, vjp = jax.vjp(_rmsnorm, x_ref[...], g_ref[...])
            dx, dg = vjp(acc_ref[...])
            dx_ref[...] = dx + dres_ref[...]
            dg_ref[...] += dg

    row = pl.BlockSpec((tm, D_MODEL), lambda i, k: (i, 0))
    vec = pl.BlockSpec((1, D_MODEL), lambda i, k: (0, 0))
    return pl.pallas_call(
        body,
        name="grad_x",
        grid=(t // tm, nk),
        in_specs=[pl.BlockSpec((tm, tk), lambda i, k: (i, k)), pl.BlockSpec((tk, D_MODEL), lambda i, k: (k, 0)),
                  row, vec, row],
        out_specs=[row, vec],
        out_shape=[jax.ShapeDtypeStruct((t, D_MODEL), F32), jax.ShapeDtypeStruct((1, D_MODEL), F32)],
        scratch_shapes=[pltpu.VMEM((tm, D_MODEL), F32)],
    )(dproj, w_alt, x, g, dres)


def _block_tail(proj, o_dn, o_sb, o_m, x, tgt, w_br_dn, w_br_sb, w_br_mem, w_out, fg, tm=256):
    t = x.shape[0]
    tm = min(tm, t)
    gw = 512
    n_g = 3 * D_MODEL // gw

    def body(*refs):
        g_refs = refs[:n_g]
        (odn_ref, osb_ref, om_ref, x_ref, t_ref, wdn_ref, wsb_ref, wm_ref, wo_ref, fg_ref, loss_ref, dout_ref, dfg_ref,
         mg_ref, dyd_ref, dys_ref, dym_ref, dg_ref, dod_ref, dos_ref, dom_ref) = refs[n_g:]
        y = [_dot(odn_ref[...], wdn_ref[...], NN), _dot(osb_ref[...], wsb_ref[...], NN),
             _dot(om_ref[...], wm_ref[...], NN)]
        s = [_sigmoid(jnp.concatenate([g_refs[2 * k][...], g_refs[2 * k + 1][...]], axis=1)) for k in range(3)]
        merged16 = (s[0] * y[0] + s[1] * y[1] + s[2] * y[2]).astype(BF16)
        mg_ref[...] = merged16
        mo = _dot(merged16, wo_ref[...], NN)
        loss, vjp = jax.vjp(_loss_fn, x_ref[...], mo, fg_ref[...], t_ref[...])
        _, dout, dfg, _ = vjp(jnp.ones((1, 1), F32))

        @pl.when(pl.program_id(0) == 0)
        def _():
            loss_ref[...] = jnp.zeros_like(loss_ref)
            dfg_ref[...] = jnp.zeros_like(dfg_ref)

        loss_ref[...] += jnp.broadcast_to(loss, loss_ref.shape)
        dfg_ref[...] += dfg
        dout_ref[...] = dout
        dmerged = _dot(dout, wo_ref[...], NT)
        dy = [(sk * dmerged).astype(BF16) for sk in s]
        dyd_ref[...], dys_ref[...], dym_ref[...] = dy
        dg_ref[...] = jnp.concatenate([dmerged * yk * (sk * (1.0 - sk)) for yk, sk in zip(y, s)], axis=1).astype(BF16)
        dod_ref[...] = _dot(dy[0], wdn_ref[...], NT).astype(BF16)
        dos_ref[...] = _dot(dy[1], wsb_ref[...], NT).astype(BF16)
        dom_ref[...] = _dot(dy[2], wm_ref[...], NT).astype(BF16)

    gates = [pl.BlockSpec((tm, gw), lambda i, j=j: (i, O_GATES // gw + j)) for j in range(n_g)]
    row = pl.BlockSpec((tm, D_MODEL), lambda i: (i, 0))
    rowm = pl.BlockSpec((tm, MEM_W), lambda i: (i, 0))
    vec = pl.BlockSpec((1, D_MODEL), lambda i: (0, 0))

    def whole(a):
        return pl.BlockSpec(a.shape, lambda i: (0, 0), pipeline_mode=pl.Buffered(1))

    def bf(c):
        return jax.ShapeDtypeStruct((t, c), BF16)

    return pl.pallas_call(
        body,
        name="block_tail",
        grid=(t // tm,),
        in_specs=gates + [row, row, rowm, row, row, whole(w_br_dn), whole(w_br_sb), whole(w_br_mem), whole(w_out), vec],
        out_specs=[pl.BlockSpec((1, LANE), lambda i: (0, 0)), row, vec, row, row, row, row,
                   pl.BlockSpec((tm, 3 * D_MODEL), lambda i: (i, 0)), row, row, rowm],
        out_shape=[jax.ShapeDtypeStruct((1, LANE), F32), jax.ShapeDtypeStruct((t, D_MODEL), F32),
                   jax.ShapeDtypeStruct((1, D_MODEL), F32), bf(D_MODEL), bf(D_MODEL), bf(D_MODEL), bf(D_MODEL),
                   bf(3 * D_MODEL), bf(D_MODEL), bf(D_MODEL), bf(MEM_W)],
    )(*([proj] * n_g), o_dn, o_sb, o_m, x, tgt, w_br_dn, w_br_sb, w_br_mem, w_out, fg)


def _shift_rows(x, s):
    t = x.shape[0]
    if s == 0:
        return x
    rolled = pltpu.roll(x, s % t, 0)
    row = _iota2(x.shape, 0)
    keep = row >= s if s > 0 else row < t + s
    return jnp.where(keep, rolled, 0.0)


def _conv_pre(x, w):
    return sum(_shift_rows(x, CONV_K - 1 - j) * w[j:j + 1, :] for j in range(CONV_K))


CONV_TC = 256


def _dn_conv(proj, conv_w):
    t = proj.shape[0]
    nb = 3 * D_MODEL // CONV_TC

    def body(x_ref, w_ref, c_ref):
        c_ref[...] = _silu(_conv_pre(x_ref[...], w_ref[...]))

    return pl.pallas_call(
        body,
        name="dn_conv",
        grid=(nb,),
        in_specs=[pl.BlockSpec((t, CONV_TC), lambda j: (0, j)), pl.BlockSpec((CONV_K, CONV_TC), lambda j: (0, j))],
        out_specs=pl.BlockSpec((t, CONV_TC), lambda j: (0, j)),
        out_shape=jax.ShapeDtypeStruct((t, 3 * D_MODEL), F32),
    )(proj, conv_w)


def _dn_conv_bwd(proj, conv_w, dc, part):
    t = proj.shape[0]
    nb = D_MODEL // CONV_TC
    b0 = part * nb

    def body(x_ref, w_ref, dc_ref, dx_ref, dw_ref):
        x = x_ref[...]
        w = w_ref[...]
        pre = _conv_pre(x, w)
        sg = _sigmoid(pre)
        dpre = dc_ref[...] * (sg * (1.0 + pre * (1.0 - sg)))
        ahead = [_shift_rows(dpre, -(CONV_K - 1 - j)) for j in range(CONV_K)]
        dx_ref[...] = sum(a * w[j:j + 1, :] for j, a in enumerate(ahead)).astype(BF16)
        dw_ref[...] = jnp.concatenate([jnp.sum(a * x, axis=0, keepdims=True) for a in ahead], axis=0)

    blk = pl.BlockSpec((t, CONV_TC), lambda j: (0, j))
    return pl.pallas_call(
        body,
        name=f"dn_conv_bwd{part}",
        grid=(nb,),
        in_specs=[pl.BlockSpec((t, CONV_TC), lambda j: (0, b0 + j)),
                  pl.BlockSpec((CONV_K, CONV_TC), lambda j: (0, b0 + j)), blk],
        out_specs=[blk, pl.BlockSpec((CONV_K, CONV_TC), lambda j: (0, j))],
        out_shape=[jax.ShapeDtypeStruct((t, D_MODEL), BF16), jax.ShapeDtypeStruct((CONV_K, D_MODEL), F32)],
    )(proj, conv_w, dc)


def _ba_columns(ba, hd):
    lane = _iota2(ba.shape, 1)
    bcol = jnp.sum(jnp.where(lane == hd, ba, 0.0), axis=1, keepdims=True)
    acol = jnp.sum(jnp.where(lane == N_HEADS + hd, ba, 0.0), axis=1, keepdims=True)
    return bcol, acol


def _head_scalar(row, hd):
    lane = _iota2(row.shape, 1)
    return jnp.sum(jnp.where(lane == hd, row, 0.0), axis=1, keepdims=True)


DN_HP = 8


def _dn_inputs(cq, ck, cv, ba_ref, z_ref, alog_ref, dtb_ref, heads, lanes):
    ba = ba_ref[...]
    cols = [_ba_columns(ba, hd) for hd in heads]
    return ([cq[:, ln] for ln in lanes], [ck[:, ln] for ln in lanes], [cv[:, ln] for ln in lanes],
            [c[0] for c in cols], [c[1] for c in cols], [z_ref[:, ln] for ln in lanes],
            [_head_scalar(alog_ref[...], hd) for hd in heads], [_head_scalar(dtb_ref[...], hd) for hd in heads])


def _dn_specs(nblk, reverse):
    w = DN_HP * LANE
    nq = D_MODEL // w

    def row(i):
        return nblk - 1 - i if reverse else i

    def colblk(b0):
        return pl.BlockSpec((SUPER, w), lambda i, h: (row(i), b0 + h))

    ba = pl.BlockSpec((SUPER, LANE), lambda i, h: (row(i), O_BA // LANE))
    vec = pl.BlockSpec((1, LANE), lambda i, h: (0, 0))
    st = pl.BlockSpec((1, DN_HP, D_HEAD, D_HEAD), lambda i, h: (row(i), h, 0, 0))
    return colblk, nq, ba, vec, st


def _dn_fwd(c, proj, alog_row, dtb_row, gn):
    t = c.shape[0]
    nblk = t // SUPER
    colblk, nq, ba, vec, st = _dn_specs(nblk, False)

    def body(cq, ck, cv, ba_ref, z_ref, alog_ref, dtb_ref, gn_ref, o_ref, s_ref, state):
        @pl.when(jnp.logical_and(pl.program_id(0) == 0, pl.program_id(1) == 0))
        def _():
            state[...] = jnp.zeros_like(state)

        heads = [pl.program_id(1) * DN_HP + j for j in range(DN_HP)]
        lanes = [slice(j * LANE, (j + 1) * LANE) for j in range(DN_HP)]
        s0 = [state[hd] for hd in heads]
        outs, s2 = _dn_block(*_dn_inputs(cq, ck, cv, ba_ref, z_ref, alog_ref, dtb_ref, heads, lanes), gn_ref[...], s0)
        for j, (hd, ln) in enumerate(zip(heads, lanes)):
            s_ref[0, j] = s0[j]
            o_ref[:, ln] = outs[j].astype(BF16)
            state[hd] = s2[j]

    return pl.pallas_call(
        body,
        name="dn_fwd",
        grid=(nblk, N_HEADS // DN_HP),
        in_specs=[colblk(0), colblk(nq), colblk(2 * nq), ba, colblk(O_Z_DN // (DN_HP * LANE)), vec, vec, vec],
        out_specs=[colblk(0), st],
        out_shape=[jax.ShapeDtypeStruct((t, D_MODEL), BF16),
                   jax.ShapeDtypeStruct((nblk, N_HEADS, D_HEAD, D_HEAD), F32)],
        scratch_shapes=[pltpu.VMEM((N_HEADS, D_HEAD, D_HEAD), F32)],
    )(c, c, c, proj, proj, alog_row, dtb_row, gn)


def _dn_bwd(c, proj, alog_row, dtb_row, gn, states, do):
    t = c.shape[0]
    nblk = t // SUPER
    colblk, nq, ba, vec, st = _dn_specs(nblk, True)

    def body(cq, ck, cv, ba_ref, z_ref, alog_ref, dtb_ref, gn_ref, s_ref, do_ref,
             dq_ref, dk_ref, dv_ref, dz_ref, dba_ref, dsc_ref, dgn_ref, dstate):
        i = pl.program_id(0)
        hq = pl.program_id(1)

        @pl.when(jnp.logical_and(i == 0, hq == 0))
        def _():
            dstate[...] = jnp.zeros_like(dstate)
            dsc_ref[...] = jnp.zeros_like(dsc_ref)
            dgn_ref[...] = jnp.zeros_like(dgn_ref)

        @pl.when(hq == 0)
        def _():
            dba_ref[...] = jnp.zeros_like(dba_ref)

        lane = _iota2((SUPER, LANE), 1)
        lane1 = _iota2((1, LANE), 1)
        heads = [hq * DN_HP + j for j in range(DN_HP)]
        lanes = [slice(j * LANE, (j + 1) * LANE) for j in range(DN_HP)]
        ds_in = [dstate[hd] for hd in heads]
        s_in = [s_ref[0, j] for j in range(DN_HP)]
        _, vjp = jax.vjp(_dn_block, *_dn_inputs(cq, ck, cv, ba_ref, z_ref, alog_ref, dtb_ref, heads, lanes),
                         gn_ref[...], s_in)
        dq, dk, dv, dbc, dac, dz, dal, ddt, dgn, ds0 = vjp(([do_ref[:, ln].astype(F32) for ln in lanes], ds_in))
        dba = jnp.zeros((SUPER, LANE), F32)
        dal_row = jnp.zeros((1, LANE), F32)
        ddt_row = jnp.zeros((1, LANE), F32)
        for j, (hd, ln) in enumerate(zip(heads, lanes)):
            dq_ref[:, ln] = dq[j]
            dk_ref[:, ln] = dk[j]
            dv_ref[:, ln] = dv[j]
            dz_ref[:, ln] = dz[j].astype(BF16)
            dstate[hd] = ds0[j]
            dba = dba + jnp.where(lane == hd, dbc[j], 0.0) + jnp.where(lane == N_HEADS + hd, dac[j], 0.0)
            dal_row = dal_row + jnp.where(lane1 == hd, dal[j], 0.0)
            ddt_row = ddt_row + jnp.where(lane1 == hd, ddt[j], 0.0)
        dba_ref[...] += dba
        dsc_ref[0:1, :] += dal_row
        dsc_ref[1:2, :] += ddt_row
        dgn_ref[...] += dgn

    outs = pl.pallas_call(
        body,
        name="dn_bwd",
        grid=(nblk, N_HEADS // DN_HP),
        in_specs=[colblk(0), colblk(nq), colblk(2 * nq), ba, colblk(O_Z_DN // (DN_HP * LANE)), vec, vec, vec, st,
                  colblk(0)],
        out_specs=[colblk(0), colblk(0), colblk(0), colblk(0),
                   pl.BlockSpec((SUPER, LANE), lambda i, h: (nblk - 1 - i, 0)),
                   pl.BlockSpec((2, LANE), lambda i, h: (0, 0)), vec],
        out_shape=[jax.ShapeDtypeStruct((t, D_MODEL), F32)] * 3
        + [jax.ShapeDtypeStruct((t, D_MODEL), BF16), jax.ShapeDtypeStruct((t, LANE), F32),
           jax.ShapeDtypeStruct((2, LANE), F32), jax.ShapeDtypeStruct((1, LANE), F32)],
        scratch_shapes=[pltpu.VMEM((N_HEADS, D_HEAD, D_HEAD), F32)],
    )(c, c, c, proj, proj, alog_row, dtb_row, gn, states, do)
    return outs


SB_TQ = 256
SB_TK = 256
SB_HP_FWD = 8
SB_HP_BWD = 4


def _sb_logits(z, mask):
    sp = jnp.log(1.0 + jnp.exp(-jnp.abs(z)))
    lf_raw = -(jnp.maximum(z, 0.0) + sp)
    lb = lf_raw + z
    lf = lf_raw if mask is None else jnp.where(mask, lf_raw, 0.0)
    return lb, lf_raw, lf


def _suffix_sums(x, sel):
    hi, lo = _split2(x)
    d = functools.partial(lax.dot_general, dimension_numbers=NN, preferred_element_type=F32)
    return d(hi, sel) + d(lo, sel)


def _sb_diag_mask(tq, r):
    return r * SB_TK + _iota2((tq, SB_TK), 1) < _iota2((tq, SB_TK), 0)


def _sb_specs(t, tq, hp):
    w = hp * LANE
    q0, k0, v0, z0 = (O_QKV_SB // w, (O_QKV_SB + D_MODEL) // w, (O_QKV_SB + 2 * D_MODEL) // w, O_Z_SB // w)

    def blk(b0):
        return pl.BlockSpec((tq, w), lambda h, i: (i, b0 + h))

    def full(b0, **kw):
        return pl.BlockSpec((t, w), lambda h, i: (0, b0 + h), **kw)

    once = dict(pipeline_mode=pl.Buffered(1))
    return blk(q0), full(k0, **once), full(v0, **once), blk(z0), blk(0), full(0)


def _sb_fwd(proj, shards):
    t = proj.shape[0]
    tq = min(SB_TQ, t)
    ndiag = tq // SB_TK
    scale = 1.0 / math.sqrt(D_HEAD)
    na = len(shards)

    def body(q_ref, k_ref, v_ref, z_ref, *rest):
        x_refs, (o_ref, oraw_ref), land = rest[:na], rest[na:na + 2], rest[na + 2:2 * na + 2]
        sems = rest[2 * na + 2:]
        qi = pl.program_id(1)
        first = jnp.logical_and(pl.program_id(0) == 0, qi == 0)
        last = jnp.logical_and(pl.program_id(0) == pl.num_programs(0) - 1, qi == pl.num_programs(1) - 1)

        @pl.when(first)
        def _():
            for cp in _direct_gather_copies(x_refs, land, *sems):
                cp.start()

        lanes = [slice(hd * LANE, (hd + 1) * LANE) for hd in range(SB_HP_FWD)]
        qs = [(q_ref[:, ln] * scale).astype(BF16) for ln in lanes]
        after = (_iota2((SB_TK, SB_TK), 0) > _iota2((SB_TK, SB_TK), 1)).astype(BF16)
        oraw_ref[...] = jnp.zeros_like(oraw_ref)

        def block(kb, mask, c_lf):
            rows = pl.ds(pl.multiple_of(kb * SB_TK, SB_TK), SB_TK)
            z = _each(lambda q, ln: _dot(q, k_ref[rows, ln], NT), qs, lanes)
            lg = _each(lambda x: _sb_logits(x, mask), z)
            surv = _each(lambda x: _suffix_sums(x[2], after), lg)
            att = _each(lambda x, s, c: jnp.exp(x[0] + s + c), lg, surv, c_lf)
            if mask is not None:
                att = _each(lambda a: jnp.where(mask, a, 0.0), att)
            pv = _each(lambda a, ln: _dot(a, v_ref[rows, ln], NN), att, lanes)
            for p, ln in zip(pv, lanes):
                oraw_ref[:, ln] += p
            return tuple(_each(lambda c, x: c + jnp.sum(x[2], axis=1, keepdims=True), c_lf, lg))

        carry = tuple(jnp.zeros((tq, 1), F32) for _ in range(SB_HP_FWD))
        for r in reversed(range(ndiag)):
            carry = block(qi * ndiag + r, _sb_diag_mask(tq, r), carry)
        lax.fori_loop(0, qi * ndiag, lambda i, c: block(qi * ndiag - 1 - i, None, c), carry)
        o_ref[...] = (oraw_ref[...] * _silu(z_ref[...])).astype(BF16)

        @pl.when(last)
        def _():
            for cp in _direct_gather_copies(x_refs, land, *sems):
                cp.wait()

    q_spec, k_spec, v_spec, z_spec, out, _ = _sb_specs(t, tq, SB_HP_FWD)
    outs = pl.pallas_call(
        body,
        name="sb_fwd",
        grid=(N_HEADS // SB_HP_FWD, t // tq),
        in_specs=[q_spec, k_spec, v_spec, z_spec] + [ANY] * na,
        out_specs=[out, out] + [ANY] * na,
        out_shape=[jax.ShapeDtypeStruct((t, D_MODEL), BF16), jax.ShapeDtypeStruct((t, D_MODEL), F32)]
        + [jax.ShapeDtypeStruct((N_DEV, *v.shape), v.dtype) for v in shards],
        scratch_shapes=_gather_sems(na),
    )(proj, proj, proj, proj, *shards)
    return outs[0], outs[1], outs[2:]


def _sb_bwd(proj, oraw, do, blocks):
    t = proj.shape[0]
    tq = min(SB_TQ, t)
    ndiag = tq // SB_TK
    scale = 1.0 / math.sqrt(D_HEAD)

    def body(q_ref, k_ref, v_ref, z_ref, oraw_ref, do_ref, blk_ref, dq_ref, dk_ref, dv_ref, dz_ref, land_ref,
             dk_acc, dv_acc, p_scr, z_scr, send_sems, recv_sems, local_sem):
        qi = pl.program_id(1)
        nq = pl.num_programs(1)
        hg = pl.program_id(0)
        me = _position()
        mine = 4 * me[0] + 2 * me[1] + me[2]

        def exchange():
            cps = [pltpu.make_async_copy(blk_ref.at[mine], land_ref.at[mine], local_sem)]
            for k, peer in enumerate(_other_devices(me)):
                cps.append(pltpu.make_async_remote_copy(
                    src_ref=blk_ref.at[4 * peer[0] + 2 * peer[1] + peer[2]], dst_ref=land_ref.at[mine],
                    send_sem=send_sems.at[k], recv_sem=recv_sems.at[k], device_id=peer, device_id_type=MESH))
            return cps

        @pl.when(jnp.logical_and(hg == 0, qi == 0))
        def _():
            for cp in exchange():
                cp.start()

        @pl.when(qi == 0)
        def _():
            dk_acc[...] = jnp.zeros_like(dk_acc)
            dv_acc[...] = jnp.zeros_like(dv_acc)

        heads = range(SB_HP_BWD)
        lanes = [slice(hd * LANE, (hd + 1) * LANE) for hd in heads]
        zg = z_ref[...]
        sg = _sigmoid(zg)
        dog = do_ref[...].astype(F32)
        dz_ref[...] = (dog * oraw_ref[...] * (sg * (1.0 + zg * (1.0 - sg)))).astype(BF16)
        d_o = (dog * (zg * sg)).astype(BF16)
        d_o16 = [d_o[:, ln] for ln in lanes]
        qs = [(q_ref[:, ln] * scale).astype(BF16) for ln in lanes]
        ri = _iota2((SB_TK, SB_TK), 0)
        ci = _iota2((SB_TK, SB_TK), 1)
        after = (ri > ci).astype(BF16)
        earlier = (ri < ci).astype(BF16)

        def rows_of(kb):
            return pl.ds(pl.multiple_of(kb * SB_TK, SB_TK), SB_TK)

        def down(kb, mask, c_lf):
            rows = rows_of(kb)
            z = _each(lambda q, ln: _dot(q, k_ref[rows, ln], NT), qs, lanes)
            da = _each(lambda d, ln: _dot(d, v_ref[rows, ln], NT), d_o16, lanes)
            lg = _each(lambda x: _sb_logits(x, mask), z)
            surv = _each(lambda x: _suffix_sums(x[2], after), lg)
            att = _each(lambda x, s, c: jnp.exp(x[0] + s + c), lg, surv, c_lf)
            if mask is not None:
                att = _each(lambda a: jnp.where(mask, a, 0.0), att)
            dv = _each(lambda a, d: _dot(a, d, TN), att, d_o16)
            for hd in heads:
                p_scr[hd, kb] = att[hd] * da[hd]
                z_scr[hd, kb] = z[hd]
                dv_acc[rows, lanes[hd]] += dv[hd]
            return tuple(_each(lambda c, x: c + jnp.sum(x[2], axis=1, keepdims=True), c_lf, lg))

        c_lf = tuple(jnp.zeros((tq, 1), F32) for _ in heads)
        for r in reversed(range(ndiag)):
            c_lf = down(qi * ndiag + r, _sb_diag_mask(tq, r), c_lf)
        lax.fori_loop(0, qi * ndiag, lambda i, c: down(qi * ndiag - 1 - i, None, c), c_lf)

        def up(kb, mask, carry):
            dq, c_p = carry
            rows = rows_of(kb)
            p = [p_scr[hd, kb] for hd in heads]
            zs = [z_scr[hd, kb] for hd in heads]
            before = _each(lambda x, c: _suffix_sums(x, earlier) + c, p, c_p)
            e = _each(lambda x: jnp.exp(-jnp.abs(x)), zs)
            r = _each(lambda x: 1.0 / (1.0 + x), e)
            sig = _each(lambda x, a, b: jnp.where(x >= 0.0, b, a * b), zs, e, r)
            oms = _each(lambda x, a, b: jnp.where(x >= 0.0, a * b, b), zs, e, r)
            if mask is not None:
                sig = _each(lambda a: jnp.where(mask, a, 0.0), sig)
            dzz = _each(lambda x, o, g, b: x * o - g * b, p, oms, sig, before)
            dk = _each(lambda x, q: _dot(x, q, TN), dzz, qs)
            dq = _each(lambda a, x, ln: a + _dot(x, k_ref[rows, ln], NN), dq, dzz, lanes)
            for hd in heads:
                dk_acc[rows, lanes[hd]] += dk[hd]
            return tuple(dq), tuple(_each(lambda c, x: c + jnp.sum(x, axis=1, keepdims=True), c_p, p))

        carry = (tuple(jnp.zeros((tq, D_HEAD), F32) for _ in heads), tuple(jnp.zeros((tq, 1), F32) for _ in heads))
        carry = lax.fori_loop(0, qi * ndiag, lambda kb, c: up(kb, None, c), carry)
        for r in range(ndiag):
            carry = up(qi * ndiag + r, _sb_diag_mask(tq, r), carry)
        dq = carry[0]
        for hd in heads:
            dq_ref[:, lanes[hd]] = (dq[hd] * scale).astype(BF16)

        @pl.when(qi == nq - 1)
        def _():
            dk_ref[...] = dk_acc[...].astype(BF16)
            dv_ref[...] = dv_acc[...].astype(BF16)

        @pl.when(jnp.logical_and(hg == pl.num_programs(0) - 1, qi == nq - 1))
        def _():
            for cp in exchange():
                cp.wait()

    q_spec, k_spec, v_spec, z_spec, blk, full = _sb_specs(t, tq, SB_HP_BWD)
    o = jax.ShapeDtypeStruct((t, D_MODEL), BF16)
    w = SB_HP_BWD * LANE
    return pl.pallas_call(
        body,
        name="sb_bwd",
        grid=(N_HEADS // SB_HP_BWD, t // tq),
        in_specs=[q_spec, k_spec, v_spec, z_spec, blk, blk, ANY],
        out_specs=[blk, full, full, blk, ANY],
        out_shape=[o, o, o, o, jax.ShapeDtypeStruct(blocks.shape, blocks.dtype)],
        scratch_shapes=[pltpu.VMEM((t, w), F32), pltpu.VMEM((t, w), F32)]
        + [pltpu.VMEM((SB_HP_BWD, t // SB_TK, tq, SB_TK), F32)] * 2
        + [pltpu.SemaphoreType.DMA((N_DEV - 1,)), pltpu.SemaphoreType.DMA((N_DEV - 1,)), pltpu.SemaphoreType.DMA],
    )(proj, proj, proj, proj, oraw, do, blocks)


def _mem_kv_fn(mem, mg, w):
    return mm_nn(_rmsnorm(mem, mg), w)


def _mem_kv(mem, mg, w):
    def body(m_ref, g_ref, w_ref, o_ref):
        o_ref[...] = _mem_kv_fn(m_ref[...], g_ref[...], w_ref[...])

    return pl.pallas_call(body, name="mem_kv", out_shape=jax.ShapeDtypeStruct((MEM_LEN, 2 * MEM_W), F32))(mem, mg, w)


def _mem_kv_bwd(mem, mg, w, dmkv):
    def body(m_ref, g_ref, w_ref, d_ref, dg_ref, dw_ref):
        _, vjp = jax.vjp(_mem_kv_fn, m_ref[...], g_ref[...], w_ref[...].astype(F32))
        _, dg, dw = vjp(d_ref[...])
        dg_ref[...] = dg
        dw_ref[...] = dw.astype(BF16)

    return pl.pallas_call(
        body, name="mem_kv_bwd",
        out_shape=[jax.ShapeDtypeStruct((1, D_MODEL), F32), jax.ShapeDtypeStruct((D_MODEL, 2 * MEM_W), BF16)],
    )(mem, mg, w, dmkv)


def _mem_attn(proj, mkv, tm=256):
    t = proj.shape[0]
    tm = min(tm, t)

    def body(q_ref, z_ref, kv_ref, o_ref):
        o_ref[...] = _mem_fn(q_ref[...], z_ref[...], kv_ref[...]).astype(BF16)

    return pl.pallas_call(
        body,
        name="mem_attn",
        grid=(t // tm,),
        in_specs=[pl.BlockSpec((tm, MEM_W), lambda i: (i, O_MQ // MEM_W)),
                  pl.BlockSpec((tm, MEM_W), lambda i: (i, O_MZ // MEM_W)),
                  pl.BlockSpec((MEM_LEN, 2 * MEM_W), lambda i: (0, 0))],
        out_specs=pl.BlockSpec((tm, MEM_W), lambda i: (i, 0)),
        out_shape=jax.ShapeDtypeStruct((t, MEM_W), BF16),
    )(proj, proj, mkv)


def _mem_attn_bwd(proj, mkv, do, tm=256):
    t = proj.shape[0]
    tm = min(tm, t)

    def body(q_ref, z_ref, kv_ref, do_ref, dq_ref, dz_ref, dkv_ref):
        _, vjp = jax.vjp(_mem_fn, q_ref[...], z_ref[...], kv_ref[...])
        dq, dz, dkv = vjp(do_ref[...].astype(F32))
        dq_ref[...] = dq.astype(BF16)
        dz_ref[...] = dz.astype(BF16)

        @pl.when(pl.program_id(0) == 0)
        def _():
            dkv_ref[...] = jnp.zeros_like(dkv_ref)

        dkv_ref[...] += dkv

    blk = pl.BlockSpec((tm, MEM_W), lambda i: (i, 0))
    kv = pl.BlockSpec((MEM_LEN, 2 * MEM_W), lambda i: (0, 0))
    return pl.pallas_call(
        body,
        name="mem_attn_bwd",
        grid=(t // tm,),
        in_specs=[pl.BlockSpec((tm, MEM_W), lambda i: (i, O_MQ // MEM_W)),
                  pl.BlockSpec((tm, MEM_W), lambda i: (i, O_MZ // MEM_W)), kv, blk],
        out_specs=[blk, blk, kv],
        out_shape=[jax.ShapeDtypeStruct((t, MEM_W), BF16), jax.ShapeDtypeStruct((t, MEM_W), BF16),
                   jax.ShapeDtypeStruct((MEM_LEN, 2 * MEM_W), F32)],
    )(proj, proj, mkv, do)


def _proj_gather(h, w_alt, shards, tn=384):
    t = h.shape[0]
    n, kdim = w_alt.shape
    assert n % tn == 0
    nj = n // tn
    na = len(shards)

    def body(h_ref, w_ref, *rest):
        x_refs, o_ref, land = rest[:na], rest[na], rest[na + 1:2 * na + 1]
        send_sems, recv_sems, local_sems = rest[2 * na + 1:]
        j = pl.program_id(0)

        def copies():
            return _direct_gather_copies(x_refs, land, send_sems, recv_sems, local_sems)

        @pl.when(j == 0)
        def _():
            for cp in copies():
                cp.start()

        o_ref[...] = _dot(h_ref[...], w_ref[...], NT)

        @pl.when(j == nj - 1)
        def _():
            for cp in copies():
                cp.wait()

    outs = pl.pallas_call(
        body,
        name="proj",
        grid=(nj,),
        in_specs=[pl.BlockSpec((t, kdim), lambda j: (0, 0)), pl.BlockSpec((tn, kdim), lambda j: (j, 0))] + [ANY] * na,
        out_specs=[pl.BlockSpec((t, tn), lambda j: (0, j))] + [ANY] * na,
        out_shape=[jax.ShapeDtypeStruct((t, n), F32)]
        + [jax.ShapeDtypeStruct((N_DEV, *v.shape), v.dtype) for v in shards],
        scratch_shapes=_gather_sems(na),
    )(h, w_alt, *shards)
    return outs[0], outs[1:]


def _local_step(x, mem, tgt, norm_g, mem_norm_g, w_alt, alog_row, dtb_row, dn_norm_g, final_g, shards):
    h = _norm_in(x, norm_g)
    s_kv, s_dn, s_sb, s_out, s_mem, s_conv = shards
    proj, (g_kv, g_conv) = _proj_gather(h, w_alt, [s_kv, s_conv])
    w_mem_kv = g_kv.reshape(D_MODEL, 2 * MEM_W)
    conv_w = g_conv.transpose(1, 0, 2).reshape(CONV_K, 3 * D_MODEL)

    c = _dn_conv(proj, conv_w)
    o_dn, states = _dn_fwd(c, proj, alog_row, dtb_row, dn_norm_g)
    o_sb, o_sb_raw, (g_dn, g_sb, g_out, g_mem) = _sb_fwd(proj, [s_dn, s_sb, s_out, s_mem])
    w_br_dn = g_dn.reshape(D_MODEL, D_MODEL)
    w_br_sb = g_sb.reshape(D_MODEL, D_MODEL)
    w_out = g_out.reshape(D_MODEL, D_MODEL)
    w_br_mem = g_mem.transpose(1, 0, 2).reshape(MEM_W, D_MODEL)
    mkv = _mem_kv(mem, mem_norm_g, w_mem_kv)
    o_m = _mem_attn(proj, mkv)

    (loss, dout, d_final_g, merged, dy_dn, dy_sb, dy_m, dgates, do_dn, do_sb, do_m) = _block_tail(
        proj, o_dn, o_sb, o_m, x, tgt, w_br_dn, w_br_sb, w_br_mem, w_out, final_g)
    dw_out = _matmul(merged, dout, "tn", BF16, 256, 1024, 2048, "dw_out")
    dw_br_dn = _matmul(o_dn, dy_dn, "tn", BF16, 256, 1024, 2048, "dw_br_dn")
    dw_br_sb = _matmul(o_sb, dy_sb, "tn", BF16, 256, 1024, 2048, "dw_br_sb")
    dw_br_mem = _matmul(o_m, dy_m, "tn", BF16, 256, 1024, 2048, "dw_br_mem")

    dmq, dmz, dmkv = _mem_attn_bwd(proj, mkv, do_m)
    d_mem_norm_g, dw_mem_kv = _mem_kv_bwd(mem, mem_norm_g, w_mem_kv, dmkv)
    rows_d = D_MODEL // N_DEV
    small_blocks = jnp.concatenate([
        dw_br_dn.reshape(N_DEV, rows_d, D_MODEL), dw_br_sb.reshape(N_DEV, rows_d, D_MODEL),
        dw_out.reshape(N_DEV, rows_d, D_MODEL), dw_mem_kv.reshape(N_DEV, rows_d // 2, D_MODEL),
        dw_br_mem.reshape(MEM_W, N_DEV, rows_d).transpose(1, 0, 2).reshape(N_DEV, MEM_W // N_DEV, D_MODEL)], axis=1)
    dq_sb, dk_sb, dv_sb, dz_sb, small_parts = _sb_bwd(proj, o_sb_raw, do_sb, small_blocks)
    (d_small,) = _sum_slots([small_parts], "sum_small_grads")
    dcq, dck, dcv, dz_dn, dba, dscal, d_dn_norm_g = _dn_bwd(c, proj, alog_row, dtb_row, dn_norm_g, states, do_dn)
    dq_dn, dcw_q = _dn_conv_bwd(proj, conv_w, dcq, 0)
    dk_dn, dcw_k = _dn_conv_bwd(proj, conv_w, dck, 1)
    dv_dn, dcw_v = _dn_conv_bwd(proj, conv_w, dcv, 2)
    d_conv_w = jnp.concatenate([dcw_q, dcw_k, dcw_v], axis=1)

    dproj = jnp.concatenate([dq_dn, dk_dn, dv_dn, dz_dn, dq_sb, dk_sb, dv_sb, dz_sb, dmq, dmz, dgates,
                             dba.astype(BF16)], axis=1)
    dw_alt = _matmul(dproj, h, "tn", BF16, 384, 1024, 2048, "dw_alt")
    grad_x, d_norm_g = _grad_x(dproj, w_alt, x, norm_g, dout)
    return dict(loss=loss, grad_x=grad_x, norm_g=d_norm_g, mem_norm_g=d_mem_norm_g, w_alt=dw_alt, conv_w=d_conv_w,
                scal=dscal, dn_norm_g=d_dn_norm_g, small=d_small, final_g=d_final_g)


MESH = pl.DeviceIdType.MESH
ANY = pl.BlockSpec(memory_space=pl.ANY)


def _position():
    return lax.axis_index("x"), lax.axis_index("y"), lax.axis_index("c")


def _other_devices(me):
    return [tuple(1 - p if (f >> s) & 1 else p for p, s in zip(me, (2, 1, 0))) for f in range(1, N_DEV)]


def _direct_gather_copies(x_refs, land_refs, send_sems, recv_sems, local_sems):
    me = _position()
    mine = 4 * me[0] + 2 * me[1] + me[2]
    cps = []
    for a, (x_ref, land) in enumerate(zip(x_refs, land_refs)):
        cps.append(pltpu.make_async_copy(x_ref, land.at[mine], local_sems.at[a]))
        for k, peer in enumerate(_other_devices(me)):
            cps.append(pltpu.make_async_remote_copy(
                src_ref=x_ref, dst_ref=land.at[mine], send_sem=send_sems.at[7 * a + k],
                recv_sem=recv_sems.at[7 * a + k], device_id=peer, device_id_type=MESH))
    return cps


def _gather_sems(n):
    return [pltpu.SemaphoreType.DMA((7 * n,)), pltpu.SemaphoreType.DMA((7 * n,)), pltpu.SemaphoreType.DMA((n,))]


def _all_gather(xs, name):
    n = len(xs)

    def body(*refs):
        x_refs, o_refs = refs[:n], refs[n:2 * n]
        send_sems, recv_sems, local_sems = refs[2 * n:]
        x, y, c = _position()
        me, sibling = (x, y, c), (x, y, 1 - c)
        x_nbr, y_nbr, diag = (1 - x, y, c), (x, 1 - y, c), (1 - x, 1 - y, c)
        south = c == 0
        relay_from = tuple(jnp.where(south, a, b) for a, b in zip(y_nbr, x_nbr))
        relay_to = tuple(jnp.where(south, a, b) for a, b in zip(x_nbr, y_nbr))

        def slot(p):
            return 4 * p[0] + 2 * p[1] + p[2]

        def copy(a, k, block, to, src=None):
            dst = o_refs[a].at[slot(block)]
            return pltpu.make_async_remote_copy(
                src_ref=dst if src is None else src, dst_ref=dst, send_sem=send_sems.at[7 * a + k],
                recv_sem=recv_sems.at[7 * a + k], device_id=to, device_id_type=MESH)

        mine = [pltpu.make_async_copy(x_refs[a], o_refs[a].at[slot(me)], local_sems.at[a]) for a in range(n)]
        for cp in mine:
            cp.start()
        sends = []
        for a in range(n):
            sends += [copy(a, 0, me, sibling, src=x_refs[a]), copy(a, 1, me, x_nbr, src=x_refs[a]),
                      copy(a, 2, me, y_nbr, src=x_refs[a])]
        for cp in sends:
            cp.start()
        later = []
        for a in range(n):
            copy(a, 1, x_nbr, me).wait_recv()
            copy(a, 2, y_nbr, me).wait_recv()
            later += [copy(a, 3, relay_from, relay_to), copy(a, 4, x_nbr, sibling), copy(a, 5, y_nbr, sibling)]
            for cp in later[-3:]:
                cp.start()
        for a in range(n):
            copy(a, 3, diag, me).wait_recv()
            later.append(copy(a, 6, diag, sibling))
            later[-1].start()
        for a in range(n):
            copy(a, 0, sibling, me).wait_recv()
            for k, chip in ((4, x_nbr), (5, y_nbr), (6, diag)):
                copy(a, k, (chip[0], chip[1], 1 - c), me).wait_recv()
        for cp in sends + later:
            cp.wait_send()
        for cp in mine:
            cp.wait()

    return pl.pallas_call(
        body,
        name=name,
        in_specs=[ANY] * n,
        out_specs=[ANY] * n,
        out_shape=[jax.ShapeDtypeStruct((N_DEV, *v.shape), v.dtype) for v in xs],
        scratch_shapes=[pltpu.SemaphoreType.DMA((7 * n,)), pltpu.SemaphoreType.DMA((7 * n,)),
                        pltpu.SemaphoreType.DMA((n,))],
    )(*xs)


def _window_view(ref, dest):
    return ref.at[pl.ds(WIN_ROW0[dest], WIN_W), :]


def _chunk_rows(rows, cols):
    return max(ch for ch in range(ROW_TILE, rows + 1, ROW_TILE) if rows % ch == 0 and ch * cols <= (1 << 20))


def _halving_stage(xs, axis, name, out_dtype, windowed=(), gather=()):
    n_arr = len(xs)
    metas = []
    for k, v in enumerate(xs):
        if k in windowed:
            metas.append((N_DEV // 2, WIN_W, v.shape[1]))
        else:
            assert v.shape[1] == 2
            metas.append((v.shape[0], v.shape[2], v.shape[3]))
    chunk = [_chunk_rows(r, c) for (_, r, c) in metas]
    offs = [sum(m[0] for m in metas[:k]) for k in range(n_arr)]
    n_sem = sum(m[0] for m in metas)

    n_g = len(gather)

    def body(*refs):
        x_refs, g_refs = refs[:n_arr], refs[n_arr:n_arr + n_g]
        outs = refs[n_arr + n_g:]
        o_refs, land_refs, gl_refs = outs[:n_arr], outs[n_arr:2 * n_arr], outs[2 * n_arr:2 * n_arr + n_g]
        rest = outs[2 * n_arr + n_g:]
        bufs = rest[:3 * n_arr]
        send_sems, recv_sems, in_sems, out_sems = rest[3 * n_arr:3 * n_arr + 4]
        gathers = _direct_gather_copies(g_refs, gl_refs, *rest[3 * n_arr + 4:]) if n_g else []
        for cp in gathers:
            cp.start()
        pos = dict(zip("xyc", _position()))
        bit = pos[axis]
        peer = tuple(1 - pos[a] if a == axis else pos[a] for a in "xyc")

        def view(k, i, b):
            if k in windowed:
                return _window_view(x_refs[k], 2 * i + b)
            return x_refs[k].at[i, b]

        def add_blocks(k, a_view, b_view, o_view):
            _hbm_add(a_view, b_view, o_view, bufs[3 * k:3 * k + 3], in_sems, out_sems, chunk[k])

        for b in (0, 1):
            @pl.when(bit == b)
            def _(b=b):
                sends = []
                for k in range(n_arr):
                    for i in range(metas[k][0]):
                        cp = pltpu.make_async_remote_copy(
                            src_ref=view(k, i, 1 - b), dst_ref=land_refs[k].at[i], send_sem=send_sems.at[offs[k] + i],
                            recv_sem=recv_sems.at[offs[k] + i], device_id=peer, device_id_type=MESH)
                        cp.start()
                        sends.append(cp)
                idx = 0
                for k in range(n_arr):
                    for i in range(metas[k][0]):
                        sends[idx].wait_recv()
                        add_blocks(k, view(k, i, b), land_refs[k].at[i], o_refs[k].at[i])
                        idx += 1
                for cp in sends:
                    cp.wait_send()

        for cp in gathers:
            cp.wait()

    out_shape = [jax.ShapeDtypeStruct(m, out_dtype) for m in metas]
    land_shape = [jax.ShapeDtypeStruct(m, v.dtype) for m, v in zip(metas, xs)]
    g_shape = [jax.ShapeDtypeStruct((N_DEV, *v.shape), v.dtype) for v in gather]
    scratch = []
    for k in range(n_arr):
        blk = (2, chunk[k], metas[k][2])
        scratch += [pltpu.VMEM(blk, xs[k].dtype)] * 2 + [pltpu.VMEM(blk, out_dtype)]
    scratch += [pltpu.SemaphoreType.DMA((n_sem,)), pltpu.SemaphoreType.DMA((n_sem,)),
                pltpu.SemaphoreType.DMA((2, 2)), pltpu.SemaphoreType.DMA((2,))]
    if n_g:
        scratch += _gather_sems(n_g)
    outs = pl.pallas_call(
        body,
        name=name,
        in_specs=[ANY] * (n_arr + n_g),
        out_specs=[ANY] * (2 * n_arr + n_g),
        out_shape=out_shape + land_shape + g_shape,
        scratch_shapes=scratch,
    )(*xs, *gather)
    return outs[:n_arr], outs[2 * n_arr:]


def _hbm_add(a_view, b_view, o_view, bufs, in_sems, out_sems, ch):
    rows = a_view.shape[0]
    nch = rows // ch
    va, vb, vo = bufs

    def rows_of(j):
        return pl.ds(pl.multiple_of(j * ch, 16), ch)

    def loads(j, s):
        return (pltpu.make_async_copy(a_view.at[rows_of(j), :], va.at[s], in_sems.at[0, s]),
                pltpu.make_async_copy(b_view.at[rows_of(j), :], vb.at[s], in_sems.at[1, s]))

    def store(j, s):
        return pltpu.make_async_copy(vo.at[s], o_view.at[rows_of(j), :], out_sems.at[s])

    for cp in loads(0, 0):
        cp.start()

    def step(j, _):
        s = lax.rem(j, 2)

        @pl.when(j + 1 < nch)
        def _():
            for cp in loads(j + 1, 1 - s):
                cp.start()

        for cp in loads(j, s):
            cp.wait()

        @pl.when(j >= 2)
        def _():
            store(j - 2, s).wait()

        vo[s] = (va[s].astype(F32) + vb[s].astype(F32)).astype(vo.dtype)
        store(j, s).start()
        return 0

    lax.fori_loop(0, nch, step, 0)
    for j in range(max(0, nch - 2), nch):
        store(j, j % 2).wait()


def _xy_stage(xs, first, name):
    n_arr = len(xs)
    if first:
        shapes = [(v.shape[2] // 2, v.shape[3]) for v in xs]
        ins = list(xs)
    else:
        shapes = [(a.shape[1], a.shape[2]) for a, _ in xs]
        ins = [v for pair in xs for v in pair]
    n_blk = 2 if first else 1
    out_dtype = BF16 if first else F32
    chunk = [_chunk_rows(r, c) for (r, c) in shapes]
    n_sem = 2 * n_blk * n_arr

    def body(*refs):
        n_in = len(ins)
        in_refs = refs[:n_in]
        n_out = 2 * n_arr if first else n_arr
        o_refs = refs[n_in:n_in + n_out]
        land = refs[n_in + n_out:n_in + n_out + 2 * n_arr]
        rest = refs[n_in + n_out + 2 * n_arr:]
        bufs = rest[:3 * n_arr]
        send_sems, recv_sems, in_sems, out_sems = rest[3 * n_arr:]
        x, y, c = _position()
        peers = {"x": (1 - x, y, c), "y": (x, 1 - y, c)}
        jobs = []
        for k in range(n_arr):
            r, _ = shapes[k]
            half_a, half_b = pl.ds(0, r), pl.ds(r, r)
            if first:
                src = in_refs[k]
                for i in range(2):
                    jobs.append((k, src.at[i, 1 - y, half_a, :], src.at[i, y, half_a, :], land[2 * k].at[i],
                                 o_refs[2 * k].at[i], "y"))
                    jobs.append((k, src.at[1 - x, i, half_b, :], src.at[x, i, half_b, :], land[2 * k + 1].at[i],
                                 o_refs[2 * k + 1].at[i], "x"))
            else:
                a1, b1 = in_refs[2 * k], in_refs[2 * k + 1]
                jobs.append((k, a1.at[1 - x], a1.at[x], land[2 * k], o_refs[k].at[half_a, :], "x"))
                jobs.append((k, b1.at[1 - y], b1.at[y], land[2 * k + 1], o_refs[k].at[half_b, :], "y"))
        sends = []
        for n, (k, send, _, landing, _, axis) in enumerate(jobs):
            cp = pltpu.make_async_remote_copy(src_ref=send, dst_ref=landing, send_sem=send_sems.at[n],
                                              recv_sem=recv_sems.at[n], device_id=peers[axis], device_id_type=MESH)
            cp.start()
            sends.append(cp)
        for cp, (k, _, kept, landing, out, _) in zip(sends, jobs):
            cp.wait_recv()
            _hbm_add(kept, landing, out, bufs[3 * k:3 * k + 3], in_sems, out_sems, chunk[k])
        for cp in sends:
            cp.wait_send()

    if first:
        out_shape = [jax.ShapeDtypeStruct((2, r, c), BF16) for (r, c) in shapes for _ in range(2)]
        land_shape = out_shape
    else:
        out_shape = [jax.ShapeDtypeStruct((2 * r, c), F32) for (r, c) in shapes]
        land_shape = [jax.ShapeDtypeStruct((r, c), BF16) for (r, c) in shapes for _ in range(2)]
    scratch = []
    for k in range(n_arr):
        scratch += [pltpu.VMEM((2, chunk[k], shapes[k][1]), BF16)] * 2 + [pltpu.VMEM((2, chunk[k], shapes[k][1]), out_dtype)]
    scratch += [pltpu.SemaphoreType.DMA((n_sem,)), pltpu.SemaphoreType.DMA((n_sem,)),
                pltpu.SemaphoreType.DMA((2, 2)), pltpu.SemaphoreType.DMA((2,))]
    outs = pl.pallas_call(
        body,
        name=name,
        in_specs=[ANY] * len(ins),
        out_specs=[ANY] * (len(out_shape) + len(land_shape)),
        out_shape=out_shape + land_shape,
        scratch_shapes=scratch,
    )(*ins)
    outs = outs[:len(out_shape)]
    return [(outs[2 * k], outs[2 * k + 1]) for k in range(n_arr)] if first else list(outs)


def _reduce_scatter(dw_al, blocks, gather=()):
    xs = [dw_al] + [b.reshape(N_DEV // 2, 2, *b.shape[1:]) for b in blocks]
    ys, gathered = _halving_stage(xs, "c", "rs_c", BF16, windowed=(0,), gather=gather)
    pairs = _xy_stage([v.reshape(2, 2, *v.shape[1:]) for v in ys], True, "rs_xy1")
    return _xy_stage(pairs, False, "rs_xy2"), gathered


def _sum_slots(gs, name):
    n = len(gs)

    def body(*refs):
        for g_ref, o_ref in zip(refs[:n], refs[n:]):
            acc = g_ref[0].astype(F32)
            for d in range(1, N_DEV):
                acc = acc + g_ref[d].astype(F32)
            o_ref[...] = acc

    return pl.pallas_call(body, name=name, out_shape=[jax.ShapeDtypeStruct(g.shape[1:], F32) for g in gs])(*gs)


def _assemble_w_al(wins, bas):
    cols = wins.shape[2]
    n_buf = 3
    ends = [WIN_ROW0[d + 1] if d + 1 < N_DEV else WIN_ROW0[d] + WIN_W for d in range(N_DEV)]
    tail = W_AL - ends[-1]

    def body(w_ref, ba_ref, o_ref, buf, zeros, ld_sems, st_sems, ba_sem):
        def load(d):
            return pltpu.make_async_copy(w_ref.at[d], buf.at[d % n_buf], ld_sems.at[d % n_buf])

        def store(d):
            n = ends[d] - WIN_ROW0[d]
            return pltpu.make_async_copy(buf.at[d % n_buf, pl.ds(0, n), :],
                                         o_ref.at[pl.ds(WIN_ROW0[d], n), :], st_sems.at[d % n_buf])

        zeros[...] = jnp.zeros_like(zeros)
        fill = pltpu.make_async_copy(zeros, o_ref.at[pl.ds(ends[-1], tail), :], ba_sem)
        fill.start()
        fill.wait()
        load(0).start()
        for d in range(N_DEV):
            if d + 1 < N_DEV:
                if d + 1 >= n_buf:
                    store(d + 1 - n_buf).wait()
                load(d + 1).start()
            load(d).wait()
            if d > 0:
                ov = WIN_ROW0[d - 1] + WIN_W - WIN_ROW0[d]
                buf[d % n_buf, :ov, :] = buf[d % n_buf, :ov, :] + buf[(d - 1) % n_buf, WIN_W - ov:, :]
            if d == N_DEV - 1:
                ba_copy = pltpu.make_async_copy(
                    ba_ref.at[BA_DEV], buf.at[d % n_buf, pl.ds(WIN_W - N_BA, N_BA), :], ba_sem)
                ba_copy.start()
                ba_copy.wait()
            store(d).start()
        for d in range(N_DEV - n_buf, N_DEV):
            store(d).wait()

    return pl.pallas_call(
        body,
        name="assemble_w_al",
        in_specs=[ANY, ANY],
        out_specs=ANY,
        out_shape=jax.ShapeDtypeStruct((W_AL, cols), wins.dtype),
        scratch_shapes=[pltpu.VMEM((n_buf, WIN_W, cols), wins.dtype), pltpu.VMEM((tail, cols), wins.dtype),
                        pltpu.SemaphoreType.DMA((n_buf,)), pltpu.SemaphoreType.DMA((n_buf,)), pltpu.SemaphoreType.DMA],
    )(wins, bas)


def _adamw_math(w, g, m, v):
    m_new = ADAM_B1 * m + (1.0 - ADAM_B1) * g
    v_new = ADAM_B2 * v + (1.0 - ADAM_B2) * (g * g)
    m_hat = m_new / (1.0 - ADAM_B1 ** ADAM_STEP)
    v_hat = v_new / (1.0 - ADAM_B2 ** ADAM_STEP)
    return -ADAM_LR * (m_hat / (jnp.sqrt(v_hat) + ADAM_EPS) + ADAM_WD * w), m_new, v_new


def _adamw(w, g, m, v, name, tb=134):
    r, _, c = w.shape
    assert r % tb == 0

    def body(w_ref, g_ref, m_ref, v_ref, d_ref, nm_ref, nv_ref):
        d_ref[...], nm_ref[...], nv_ref[...] = _adamw_math(w_ref[...], g_ref[...], m_ref[...], v_ref[...])

    blk = pl.BlockSpec((tb, 1, c), lambda i: (i, 0, 0))
    o = jax.ShapeDtypeStruct(w.shape, F32)
    return pl.pallas_call(body, name=name, grid=(r // tb,), in_specs=[blk] * 4, out_specs=[blk] * 3,
                          out_shape=[o, o, o])(w, g, m, v)


def _adamw_many(ws, gs, ms, vs, name):
    n = len(ws)

    def body(*refs):
        for k in range(n):
            w_ref, g_ref, m_ref, v_ref = (refs[j * n + k] for j in range(4))
            d_ref, nm_ref, nv_ref = (refs[(4 + j) * n + k] for j in range(3))
            d_ref[...], nm_ref[...], nv_ref[...] = _adamw_math(w_ref[...], g_ref[...], m_ref[...], v_ref[...])

    shapes = [jax.ShapeDtypeStruct(w.shape, F32) for w in ws]
    outs = pl.pallas_call(body, name=name, out_shape=shapes * 3)(*ws, *gs, *ms, *vs)
    return outs[:n], outs[n:2 * n], outs[2 * n:]


def _select(me, table):
    return sum(jnp.where(me == d, jnp.int32(v), jnp.int32(0)) for d, v in enumerate(table))


WIN_SHIFT = tuple(SHARD_W * d - WIN_ROW0[d] for d in range(N_DEV))
PAD_L = 64
PAD_R = 64
assert max(WIN_SHIFT) <= PAD_L and WIN_W + N_BA - SHARD_W <= PAD_R


def _shard_to_window(shard_t, me):
    shift = _select(me, WIN_SHIFT)
    padded = jnp.pad(shard_t, ((PAD_L, PAD_R), (0, 0)))
    cols = shard_t.shape[1]
    lo = lax.dynamic_slice(padded, (PAD_L - shift, 0), (WIN_W, cols))
    hi = lax.dynamic_slice(padded, (PAD_L - shift + N_BA, 0), (WIN_W, cols))
    aligned = _select(me, WIN_ROW0) + lax.broadcasted_iota(jnp.int32, (WIN_W, 1), 0)
    return jnp.where(aligned >= ORIG_BA, hi, lo)


def _window_to_shard(win, ba_grad, me):
    shift = _select(me, WIN_SHIFT)
    cols = win.shape[1]
    padded = jnp.pad(win, ((N_BA, PAD_R), (0, 0)))
    lo = lax.dynamic_slice(padded, (N_BA + shift, 0), (SHARD_W, cols))
    hi = lax.dynamic_slice(padded, (shift, 0), (SHARD_W, cols))
    orig = SHARD_W * me + lax.broadcasted_iota(jnp.int32, (SHARD_W, 1), 0)
    ba_full = lax.dynamic_update_slice(jnp.zeros((SHARD_W, cols), win.dtype), ba_grad, (BA_LOCAL, 0))
    return jnp.where(orig < ORIG_BA, lo, jnp.where(orig >= ORIG_BA + N_BA, hi, ba_full))


def _pad_row(v, width=D_MODEL):
    v = v.reshape(1, -1)
    return jnp.pad(v, ((0, 0), (0, width - v.shape[1])))


def _slab(v, rows=8):
    return jnp.pad(v, ((0, rows - v.shape[0]), (0, D_MODEL - v.shape[1])))


def kernel(x, mem, norm_g, mem_norm_g, w_in, conv_w, a_log, dt_bias, dn_norm_g, w_mem_kv, w_br_dn, w_br_sb, w_br_mem, w_out, final_g, loss_target, m_norm_g, m_mem_norm_g, m_w_in, m_conv_w, m_a_log, m_dt_bias, m_dn_norm_g, m_w_mem_kv, m_w_br_dn, m_w_br_sb, m_w_br_mem, m_w_out, m_final_g, v_norm_g, v_mem_norm_g, v_w_in, v_conv_w, v_a_log, v_dt_bias, v_dn_norm_g, v_w_mem_kv, v_w_br_dn, v_w_br_sb, v_w_br_mem, v_w_out, v_final_g):
    xi, yi, ci = _position()
    me = 4 * xi + 2 * yi + ci

    shard_t = w_in[0].T
    win = _shard_to_window(shard_t, me).astype(BF16)
    ba = shard_t[BA_LOCAL:BA_LOCAL + N_BA, :].astype(BF16)
    g_win, g_ba = _all_gather([win, ba], "gather_weights")
    w_alt = _assemble_w_al(g_win, g_ba)

    shards = [w_mem_kv[0].astype(BF16), w_br_dn[0].astype(BF16), w_br_sb[0].astype(BF16), w_out[0].astype(BF16),
              w_br_mem[0].astype(BF16), conv_w[0]]
    r = _local_step(x[0], mem[0], loss_target[0], norm_g, mem_norm_g, w_alt, _pad_row(a_log, LANE),
                    _pad_row(dt_bias, LANE), dn_norm_g, final_g.reshape(1, D_MODEL), shards)

    dw_alt = r["w_alt"]
    parts = [r["norm_g"], r["mem_norm_g"], r["final_g"], r["dn_norm_g"], r["scal"], r["loss"], r["conv_w"],
             dw_alt[O_BA:O_BA + N_BA, :].astype(F32)]
    (g_win,), gathered = _reduce_scatter(dw_alt, [], gather=parts)
    rows_d = D_MODEL // N_DEV
    g_small = r["small"]
    g_dn, g_sb, g_out = (g_small[k * rows_d:(k + 1) * rows_d] for k in range(3))
    g_kv = g_small[3 * rows_d:3 * rows_d + rows_d // 2].reshape(rows_d, 2 * MEM_W)
    g_mem = g_small[3 * rows_d + rows_d // 2:].reshape(MEM_W, rows_d)
    s_norm_g, s_mem_norm_g, s_final_g, s_dn_norm_g, s_scal, s_loss, s_conv, s_ba = _sum_slots(gathered, "sum_small")
    loss = s_loss[0, 0]
    cw = conv_w.shape[2]
    g_conv = lax.dynamic_slice(s_conv, (0, cw * me), (CONV_K, cw))
    g_w_in_t = _window_to_shard(g_win, s_ba, me)
    grads = dict(norm_g=s_norm_g, mem_norm_g=s_mem_norm_g, w_in=g_w_in_t.T[None], conv_w=g_conv[None],
                 a_log=s_scal[0:1, :N_HEADS], dt_bias=s_scal[1:2, :N_HEADS], dn_norm_g=s_dn_norm_g, w_mem_kv=g_kv[None],
                 w_br_dn=g_dn[None], w_br_sb=g_sb[None], w_br_mem=g_mem[None], w_out=g_out[None],
                 final_g=s_final_g.reshape(D_MODEL))

    params = dict(norm_g=(norm_g, m_norm_g, v_norm_g), mem_norm_g=(mem_norm_g, m_mem_norm_g, v_mem_norm_g),
                  w_in=(w_in, m_w_in, v_w_in), conv_w=(conv_w, m_conv_w, v_conv_w), a_log=(a_log, m_a_log, v_a_log),
                  dt_bias=(dt_bias, m_dt_bias, v_dt_bias), dn_norm_g=(dn_norm_g, m_dn_norm_g, v_dn_norm_g),
                  w_mem_kv=(w_mem_kv, m_w_mem_kv, v_w_mem_kv), w_br_dn=(w_br_dn, m_w_br_dn, v_w_br_dn),
                  w_br_sb=(w_br_sb, m_w_br_sb, v_w_br_sb), w_br_mem=(w_br_mem, m_w_br_mem, v_w_br_mem),
                  w_out=(w_out, m_w_out, v_w_out), final_g=(final_g, m_final_g, v_final_g))
    order = list(params)
    deltas, new_m, new_v = {}, {}, {}
    deltas["w_in"], new_m["w_in"], new_v["w_in"] = (jnp.transpose(o, (1, 2, 0)) for o in _adamw(
        jnp.transpose(w_in, (2, 0, 1)), g_w_in_t[:, None, :], jnp.transpose(m_w_in, (2, 0, 1)),
        jnp.transpose(v_w_in, (2, 0, 1)), "adamw_w_in"))
    rest = [nm for nm in order if nm != "w_in"]

    def two_d(a):
        return a.reshape(1, -1) if a.ndim == 1 else a

    d_l, m_l, v_l = _adamw_many([two_d(params[nm][0]) for nm in rest], [two_d(grads[nm]) for nm in rest],
                                [two_d(params[nm][1]) for nm in rest], [two_d(params[nm][2]) for nm in rest], "adamw_rest")
    for k, nm in enumerate(rest):
        shp = params[nm][0].shape
        deltas[nm], new_m[nm], new_v[nm] = d_l[k].reshape(shp), m_l[k].reshape(shp), v_l[k].reshape(shp)
    return (loss, r["grad_x"][None], *[grads[nm] for nm in order], *[deltas[nm] for nm in order],
            *[new_m[nm] for nm in order], *[new_v[nm] for nm in order])
```

```python
import functools
import math

import jax
import jax.numpy as jnp
from jax import lax
from jax.experimental import pallas as pl
from jax.experimental.pallas import tpu as pltpu

F32 = jnp.float32
BF16 = jnp.bfloat16

D_MODEL = 1024
N_DEV = 8
N_HEADS = 8
D_HEAD = 128
DN_CHUNK = 64
CONV_K = 4
MEM_LEN = 256
MEM_HEADS = 4
MEM_DH = 64
MEM_W = MEM_HEADS * MEM_DH
NORM_EPS = 1e-6
IN_WIDTH = 11792
SHARD_W = IN_WIDTH // N_DEV

LANE = 128
SUPER = 2 * DN_CHUNK

O_QKV_DN = 0
O_Z_DN = 3072
O_QKV_SB = 4096
O_Z_SB = 7168
O_MQ = 8192
O_MZ = 8448
O_GATES = 8704
O_BA = 11776
W_AL = 11904
ORIG_BA = 4096
N_BA = 16

BA_DEV = ORIG_BA // SHARD_W
BA_LOCAL = ORIG_BA - BA_DEV * SHARD_W


def _aligned_col(o):
    return o if o < ORIG_BA else o - N_BA


ROW_TILE = 16
WIN_W = 1504
WIN_ROW0 = tuple(_aligned_col(SHARD_W * d) // ROW_TILE * ROW_TILE for d in range(N_DEV))
assert not any(ORIG_BA <= SHARD_W * d < ORIG_BA + N_BA for d in range(N_DEV))
assert all(WIN_ROW0[d] + WIN_W >= _aligned_col(SHARD_W * (d + 1) - 1) + 1 for d in range(N_DEV))
assert all(WIN_ROW0[d + 1] <= WIN_ROW0[d] + WIN_W for d in range(N_DEV - 1))
assert WIN_ROW0[-1] + WIN_W == O_BA + N_BA

ADAM_LR = 0.001
ADAM_B1 = 0.9
ADAM_B2 = 0.999
ADAM_EPS = 1e-08
ADAM_WD = 0.01
ADAM_STEP = 10

NN = (((1,), (0,)), ((), ()))
NT = (((1,), (1,)), ((), ()))
TN = (((0,), (0,)), ((), ()))


def _dot(a, b, dims):
    return lax.dot_general(a.astype(BF16), b.astype(BF16), dims, preferred_element_type=F32)


def _split2(a):
    hi = a.astype(BF16)
    lo = (a - hi.astype(F32)).astype(BF16)
    return hi, lo


def _dot3(a, b, dims):
    ah, al = _split2(a)
    bh, bl = _split2(b)
    d = functools.partial(lax.dot_general, dimension_numbers=dims, preferred_element_type=F32)
    return d(ah, bh) + (d(ah, bl) + d(al, bh))


def _sel_dot_impl(sel01, x, dims):
    sel = sel01.astype(BF16)
    h1 = x.astype(BF16)
    r1 = x - h1.astype(F32)
    h2 = r1.astype(BF16)
    h3 = (r1 - h2.astype(F32)).astype(BF16)
    d = functools.partial(lax.dot_general, dimension_numbers=dims, preferred_element_type=F32)
    return d(sel, h1) + (d(sel, h2) + d(sel, h3))


@jax.custom_vjp
def _sel_dot(sel01, x):
    return _sel_dot_impl(sel01, x, NN)


_sel_dot.defvjp(lambda s, x: (_sel_dot(s, x), s),
                lambda s, g: (jnp.zeros_like(s), _sel_dot_impl(s, g, TN)))


def _make_mm(dotfn):
    @jax.custom_vjp
    def nn(a, b):
        return dotfn(a, b, NN)

    @jax.custom_vjp
    def nt(a, b):
        return dotfn(a, b, NT)

    @jax.custom_vjp
    def tn(a, b):
        return dotfn(a, b, TN)

    nn.defvjp(lambda a, b: (nn(a, b), (a, b)), lambda r, g: (nt(g, r[1]), tn(r[0], g)))
    nt.defvjp(lambda a, b: (nt(a, b), (a, b)), lambda r, g: (nn(g, r[1]), tn(g, r[0])))
    tn.defvjp(lambda a, b: (tn(a, b), (a, b)), lambda r, g: (nt(r[1], g), nn(r[0], g)))
    return nn, nt, tn


mm_nn, mm_nt, mm_tn = _make_mm(_dot)
mm3_nn, mm3_nt, mm3_tn = _make_mm(_dot3)


def _sigmoid(x):
    return jax.nn.sigmoid(x)


def _silu(x):
    return x * _sigmoid(x)


def _softplus_parts(x):
    sp = jnp.log1p(jnp.exp(-jnp.abs(x)))
    return jnp.maximum(x, 0.0) + sp, jnp.maximum(-x, 0.0) + sp


def _rmsnorm(x, g):
    return x * lax.rsqrt(jnp.mean(x * x, axis=-1, keepdims=True) + NORM_EPS) * g


def _iota2(shape, dim):
    return lax.broadcasted_iota(jnp.int32, shape, dim)


def _div64(i):
    return lax.shift_right_logical(i, jnp.full(i.shape, 6, jnp.int32))


def _each(f, *lists):
    return [f(*a) for a in zip(*lists)]


@jax.custom_vjp
def _inv_unit_lower(ms):
    n = ms[0].shape[0]
    eye = (_iota2((n, n), 0) == _iota2((n, n), 1)).astype(F32)
    rs = [eye - m for m in ms]
    ps = ms
    for _ in range(5):
        ps = _each(mm3_nn, ps, ps)
        rs = _each(lambda r, p: r + mm_nn(r, p), rs, ps)
    return rs


def _inv_fwd(ms):
    rs = _inv_unit_lower(ms)
    return rs, rs


def _inv_bwd(rs, gs):
    ts = _each(mm_tn, rs, gs)
    return (_each(lambda t, r: -mm_nt(t, r), ts, rs),)


_inv_unit_lower.defvjp(_inv_fwd, _inv_bwd)


def _dn_block(cq, ck, cv, bcol, acol, zt, alog, dtb, gn, s0):
    n = SUPER
    h = DN_CHUNK
    row = _iota2((n, n), 0)
    col = _iota2((n, n), 1)
    same = _div64(row) == _div64(col)
    incl = jnp.logical_and(same, row >= col)
    strict = jnp.logical_and(same, row > col)
    incl_f = incl.astype(F32)

    qn = _each(lambda x: x * lax.rsqrt(jnp.sum(x * x, axis=-1, keepdims=True) + NORM_EPS) * (D_HEAD ** -0.5), cq)
    kn = _each(lambda x: x * lax.rsqrt(jnp.sum(x * x, axis=-1, keepdims=True) + NORM_EPS), ck)
    beta = _each(_sigmoid, bcol)
    g = _each(lambda al, ac, dt: -(jnp.exp(al) * _softplus_parts(ac + dt)[0]), alog, acol, dtb)
    gcum = _each(lambda x: _sel_dot(incl_f, jnp.broadcast_to(x, (n, n))), g)
    gam_incl = _each(lambda x: jnp.where(incl, jnp.exp(jnp.where(incl, x - x.T, 0.0)), 0.0), gcum)
    kk = _each(mm_nt, kn, kn)
    t_inv = _inv_unit_lower(_each(lambda b, x, gm: b * x * jnp.where(strict, gm, 0.0), beta, kk, gam_incl))
    eg = _each(jnp.exp, gcum)
    u = _each(lambda t, v, b: mm_nn(t, v * b), t_inv, cv, beta)
    w = _each(lambda t, k, b, e: mm_nn(t, k * (b * e)), t_inv, kn, beta, eg)
    a_intra = _each(lambda q, k, gm: mm_nt(q, k) * gm, qn, kn, gam_incl)
    q_dec = _each(lambda q, e: q * e, qn, eg)
    last0 = _each(lambda x: x[h - 1:h, :], gcum)
    last1 = _each(lambda x: x[n - 1:n, :], gcum)
    k_dec = _each(lambda k, x, l0, l1: k * jnp.exp(jnp.concatenate(
        [jnp.broadcast_to(l0, (h, n)), jnp.broadcast_to(l1, (h, n))], axis=0) - x), kn, gcum, last0, last1)
    v0 = _each(lambda uu, ww, s: uu[:h] - mm_nn(ww[:h], s), u, w, s0)
    o0 = _each(lambda q, s: mm_nn(q[:h], s), q_dec, s0)
    s1 = _each(lambda s, l0, k, v: s * jnp.exp(l0) + mm_tn(k[:h], v), s0, last0, k_dec, v0)
    v1 = _each(lambda uu, ww, s: uu[h:] - mm_nn(ww[h:], s), u, w, s1)
    o1 = _each(lambda q, s: mm_nn(q[h:], s), q_dec, s1)
    s2 = _each(lambda s, l1, k, v: s * jnp.exp(l1) + mm_tn(k[h:], v), s1, last1, k_dec, v1)
    o = _each(lambda a, b, am, x, y: jnp.concatenate([a, b], axis=0) + mm_nn(am, jnp.concatenate([x, y], axis=0)),
              o0, o1, a_intra, v0, v1)
    out = _each(lambda x, z: _rmsnorm(x, gn) * _silu(z), o, zt)
    return out, s2


def _mem_fn(mq, mz, mkv):
    mk = mkv[:, :MEM_W]
    mv = mkv[:, MEM_W:]
    lane = _iota2((1, MEM_W), 1)
    out = jnp.zeros(mq.shape, F32)
    for hd in range(MEM_HEADS):
        hm = (_div64(lane) == hd).astype(F32)
        s = mm_nt(mq * hm, mk) * (1.0 / math.sqrt(MEM_DH))
        s = s - jnp.max(s, axis=-1, keepdims=True)
        e = jnp.exp(s)
        p = e / jnp.sum(e, axis=-1, keepdims=True)
        out = out + mm_nn(p, mv) * hm
    return out * _silu(mz)


def _loss_fn(x, mo, fg, tgt):
    y = _rmsnorm(x + mo, fg)
    err = y - tgt
    return 0.5 * jnp.sum(jnp.mean(err * err, axis=-1, keepdims=True), axis=0, keepdims=True)


def _matmul_tn(a, b, out_dtype, tm, tn, tk, name):
    kdim, m = a.shape
    n = b.shape[1]
    tm, tn, tk = min(tm, m), min(tn, n), min(tk, kdim)
    assert m % tm == 0 and n % tn == 0 and kdim % tk == 0
    nk = kdim // tk

    def body(a_ref, b_ref, o_ref, acc_ref):
        k = pl.program_id(2)
        part = _dot(a_ref[...], b_ref[...], TN)

        @pl.when(k == 0)
        def _():
            acc_ref[...] = part

        @pl.when(k > 0)
        def _():
            acc_ref[...] += part

        @pl.when(k == nk - 1)
        def _():
            o_ref[...] = acc_ref[...].astype(o_ref.dtype)

    return pl.pallas_call(
        body,
        name=name,
        grid=(m // tm, n // tn, nk),
        in_specs=[pl.BlockSpec((tk, tm), lambda i, j, k: (k, i)), pl.BlockSpec((tk, tn), lambda i, j, k: (k, j))],
        out_specs=pl.BlockSpec((tm, tn), lambda i, j, k: (i, j)),
        out_shape=jax.ShapeDtypeStruct((m, n), out_dtype),
        scratch_shapes=[pltpu.VMEM((tm, tn), F32)],
        compiler_params=pltpu.CompilerParams(dimension_semantics=("parallel", "parallel", "arbitrary")),
    )(a, b)


def _norm_in(x, g, tm=256):
    t = x.shape[0]

    def body(x_ref, g_ref, h_ref):
        h_ref[...] = _rmsnorm(x_ref[...], g_ref[...]).astype(BF16)

    return pl.pallas_call(
        body,
        name="norm_in",
        grid=(t // tm,),
        in_specs=[pl.BlockSpec((tm, D_MODEL), lambda i: (i, 0)), pl.BlockSpec((1, D_MODEL), lambda i: (0, 0))],
        out_specs=pl.BlockSpec((tm, D_MODEL), lambda i: (i, 0)),
        out_shape=jax.ShapeDtypeStruct((t, D_MODEL), BF16),
    )(x, g)


def _grad_x(dproj, w_alt, x, g, dres, tm=512, tk=3968):
    t, kdim = dproj.shape
    tm = min(tm, t)
    assert kdim % tk == 0 and t % tm == 0
    nk = kdim // tk

    def body(a_ref, b_ref, x_ref, g_ref, dres_ref, dx_ref, dg_ref, acc_ref):
        i, k = pl.program_id(0), pl.program_id(1)
        part = _dot(a_ref[...], b_ref[...], NN)

        @pl.when(k == 0)
        def _():
            acc_ref[...] = part

        @pl.when(k > 0)
        def _():
            acc_ref[...] += part

        @pl.when(jnp.logical_and(i == 0, k == 0))
        def _():
            dg_ref[...] = jnp.zeros_like(dg_ref)

        @pl.when(k == nk - 1)
        def _():
            _, vjp = jax.vjp(_rmsnorm, x_ref[...], g_ref[...])
            dx, dg = vjp(acc_ref[...])
            dx_ref[...] = dx + dres_ref[...]
            dg_ref[...] += dg

    row = pl.BlockSpec((tm, D_MODEL), lambda i, k: (i, 0))
    vec = pl.BlockSpec((1, D_MODEL), lambda i, k: (0, 0))
    return pl.pallas_call(
        body,
        name="grad_x",
        grid=(t // tm, nk),
        in_specs=[pl.BlockSpec((tm, tk), lambda i, k: (i, k)), pl.BlockSpec((tk, D_MODEL), lambda i, k: (k, 0)),
                  row, vec, row],
        out_specs=[row, vec],
        out_shape=[jax.ShapeDtypeStruct((t, D_MODEL), F32), jax.ShapeDtypeStruct((1, D_MODEL), F32)],
        scratch_shapes=[pltpu.VMEM((tm, D_MODEL), F32)],
    )(dproj, w_alt, x, g, dres)


def _block_tail(proj, o_dn, o_sb, o_m, x, tgt, w_br_dn, w_br_sb, w_br_mem, w_out, fg, tm=256):
    t = x.shape[0]
    tm = min(tm, t)
    gw = 512
    n_g = 3 * D_MODEL // gw

    def body(*refs):
        g_refs = refs[:n_g]
        (odn_ref, osb_ref, om_ref, x_ref, t_ref, wdn_ref, wsb_ref, wm_ref, wo_ref, fg_ref, loss_ref, dout_ref, dfg_ref,
         mg_ref, dyd_ref, dys_ref, dym_ref, dg_ref, dod_ref, dos_ref, dom_ref) = refs[n_g:]
        y = [_dot(odn_ref[...], wdn_ref[...], NN), _dot(osb_ref[...], wsb_ref[...], NN),
             _dot(om_ref[...], wm_ref[...], NN)]
        s = [_sigmoid(jnp.concatenate([g_refs[2 * k][...], g_refs[2 * k + 1][...]], axis=1)) for k in range(3)]
        merged16 = (s[0] * y[0] + s[1] * y[1] + s[2] * y[2]).astype(BF16)
        mg_ref[...] = merged16
        mo = _dot(merged16, wo_ref[...], NN)
        loss, vjp = jax.vjp(_loss_fn, x_ref[...], mo, fg_ref[...], t_ref[...])
        _, dout, dfg, _ = vjp(jnp.ones((1, 1), F32))

        @pl.when(pl.program_id(0) == 0)
        def _():
            loss_ref[...] = jnp.zeros_like(loss_ref)
            dfg_ref[...] = jnp.zeros_like(dfg_ref)

        loss_ref[...] += jnp.broadcast_to(loss, loss_ref.shape)
        dfg_ref[...] += dfg
        dout_ref[...] = dout
        dmerged = _dot(dout, wo_ref[...], NT)
        dy = [(sk * dmerged).astype(BF16) for sk in s]
        dyd_ref[...], dys_ref[...], dym_ref[...] = dy
        dg_ref[...] = jnp.concatenate([dmerged * yk * (sk * (1.0 - sk)) for yk, sk in zip(y, s)], axis=1).astype(BF16)
        dod_ref[...] = _dot(dy[0], wdn_ref[...], NT).astype(BF16)
        dos_ref[...] = _dot(dy[1], wsb_ref[...], NT).astype(BF16)
        dom_ref[...] = _dot(dy[2], wm_ref[...], NT).astype(BF16)

    gates = [pl.BlockSpec((tm, gw), lambda i, j=j: (i, O_GATES // gw + j)) for j in range(n_g)]
    row = pl.BlockSpec((tm, D_MODEL), lambda i: (i, 0))
    rowm = pl.BlockSpec((tm, MEM_W), lambda i: (i, 0))
    vec = pl.BlockSpec((1, D_MODEL), lambda i: (0, 0))

    def whole(a):
        return pl.BlockSpec(a.shape, lambda i: (0, 0), pipeline_mode=pl.Buffered(1))

    def bf(c):
        return jax.ShapeDtypeStruct((t, c), BF16)

    return pl.pallas_call(
        body,
        name="block_tail",
        grid=(t // tm,),
        in_specs=gates + [row, row, rowm, row, row, whole(w_br_dn), whole(w_br_sb), whole(w_br_mem), whole(w_out), vec],
        out_specs=[pl.BlockSpec((1, LANE), lambda i: (0, 0)), row, vec, row, row, row, row,
                   pl.BlockSpec((tm, 3 * D_MODEL), lambda i: (i, 0)), row, row, rowm],
        out_shape=[jax.ShapeDtypeStruct((1, LANE), F32), jax.ShapeDtypeStruct((t, D_MODEL), F32),
                   jax.ShapeDtypeStruct((1, D_MODEL), F32), bf(D_MODEL), bf(D_MODEL), bf(D_MODEL), bf(D_MODEL),
                   bf(3 * D_MODEL), bf(D_MODEL), bf(D_MODEL), bf(MEM_W)],
    )(*([proj] * n_g), o_dn, o_sb, o_m, x, tgt, w_br_dn, w_br_sb, w_br_mem, w_out, fg)


def _shift_rows(x, s):
    t = x.shape[0]
    if s == 0:
        return x
    rolled = pltpu.roll(x, s % t, 0)
    row = _iota2(x.shape, 0)
    keep = row >= s if s > 0 else row < t + s
    return jnp.where(keep, rolled, 0.0)


def _conv_pre(x, w):
    return sum(_shift_rows(x, CONV_K - 1 - j) * w[j:j + 1, :] for j in range(CONV_K))


CONV_TC = 256


def _dn_conv(proj, conv_w):
    t = proj.shape[0]
    nb = 3 * D_MODEL // CONV_TC

    def body(x_ref, w_ref, c_ref):
        c_ref[...] = _silu(_conv_pre(x_ref[...], w_ref[...]))

    return pl.pallas_call(
        body,
        name="dn_conv",
        grid=(nb,),
        in_specs=[pl.BlockSpec((t, CONV_TC), lambda j: (0, j)), pl.BlockSpec((CONV_K, CONV_TC), lambda j: (0, j))],
        out_specs=pl.BlockSpec((t, CONV_TC), lambda j: (0, j)),
        out_shape=jax.ShapeDtypeStruct((t, 3 * D_MODEL), F32),
    )(proj, conv_w)


def _dn_conv_bwd(proj, conv_w, dc, part):
    t = proj.shape[0]
    nb = D_MODEL // CONV_TC
    b0 = part * nb

    def body(x_ref, w_ref, dc_ref, dx_ref, dw_ref):
        x = x_ref[...]
        w = w_ref[...]
        pre = _conv_pre(x, w)
        sg = _sigmoid(pre)
        dpre = dc_ref[...] * (sg * (1.0 + pre * (1.0 - sg)))
        ahead = [_shift_rows(dpre, -(CONV_K - 1 - j)) for j in range(CONV_K)]
        dx_ref[...] = sum(a * w[j:j + 1, :] for j, a in enumerate(ahead)).astype(BF16)
        dw_ref[...] = jnp.concatenate([jnp.sum(a * x, axis=0, keepdims=True) for a in ahead], axis=0)

    blk = pl.BlockSpec((t, CONV_TC), lambda j: (0, j))
    return pl.pallas_call(
        body,
        name=f"dn_conv_bwd{part}",
        grid=(nb,),
        in_specs=[pl.BlockSpec((t, CONV_TC), lambda j: (0, b0 + j)),
                  pl.BlockSpec((CONV_K, CONV_TC), lambda j: (0, b0 + j)), blk],
        out_specs=[blk, pl.BlockSpec((CONV_K, CONV_TC), lambda j: (0, j))],
        out_shape=[jax.ShapeDtypeStruct((t, D_MODEL), BF16), jax.ShapeDtypeStruct((CONV_K, D_MODEL), F32)],
    )(proj, conv_w, dc)


def _ba_columns(ba, hd):
    lane = _iota2(ba.shape, 1)
    bcol = jnp.sum(jnp.where(lane == hd, ba, 0.0), axis=1, keepdims=True)
    acol = jnp.sum(jnp.where(lane == N_HEADS + hd, ba, 0.0), axis=1, keepdims=True)
    return bcol, acol


def _head_scalar(row, hd):
    lane = _iota2(row.shape, 1)
    return jnp.sum(jnp.where(lane == hd, row, 0.0), axis=1, keepdims=True)


DN_HP = 8


def _dn_inputs(cq, ck, cv, ba_ref, z_ref, alog_ref, dtb_ref, heads, lanes):
    ba = ba_ref[...]
    cols = [_ba_columns(ba, hd) for hd in heads]
    return ([cq[:, ln] for ln in lanes], [ck[:, ln] for ln in lanes], [cv[:, ln] for ln in lanes],
            [c[0] for c in cols], [c[1] for c in cols], [z_ref[:, ln] for ln in lanes],
            [_head_scalar(alog_ref[...], hd) for hd in heads], [_head_scalar(dtb_ref[...], hd) for hd in heads])


def _dn_specs(nblk, reverse):
    w = DN_HP * LANE
    nq = D_MODEL // w

    def row(i):
        return nblk - 1 - i if reverse else i

    def colblk(b0):
        return pl.BlockSpec((SUPER, w), lambda i, h: (row(i), b0 + h))

    ba = pl.BlockSpec((SUPER, LANE), lambda i, h: (row(i), O_BA // LANE))
    vec = pl.BlockSpec((1, LANE), lambda i, h: (0, 0))
    st = pl.BlockSpec((1, DN_HP, D_HEAD, D_HEAD), lambda i, h: (row(i), h, 0, 0))
    return colblk, nq, ba, vec, st


def _dn_fwd(c, proj, alog_row, dtb_row, gn):
    t = c.shape[0]
    nblk = t // SUPER
    colblk, nq, ba, vec, st = _dn_specs(nblk, False)

    def body(cq, ck, cv, ba_ref, z_ref, alog_ref, dtb_ref, gn_ref, o_ref, s_ref, state):
        @pl.when(jnp.logical_and(pl.program_id(0) == 0, pl.program_id(1) == 0))
        def _():
            state[...] = jnp.zeros_like(state)

        heads = [pl.program_id(1) * DN_HP + j for j in range(DN_HP)]
        lanes = [slice(j * LANE, (j + 1) * LANE) for j in range(DN_HP)]
        s0 = [state[hd] for hd in heads]
        outs, s2 = _dn_block(*_dn_inputs(cq, ck, cv, ba_ref, z_ref, alog_ref, dtb_ref, heads, lanes), gn_ref[...], s0)
        for j, (hd, ln) in enumerate(zip(heads, lanes)):
            s_ref[0, j] = s0[j]
            o_ref[:, ln] = outs[j].astype(BF16)
            state[hd] = s2[j]

    return pl.pallas_call(
        body,
        name="dn_fwd",
        grid=(nblk, N_HEADS // DN_HP),
        in_specs=[colblk(0), colblk(nq), colblk(2 * nq), ba, colblk(O_Z_DN // (DN_HP * LANE)), vec, vec, vec],
        out_specs=[colblk(0), st],
        out_shape=[jax.ShapeDtypeStruct((t, D_MODEL), BF16),
                   jax.ShapeDtypeStruct((nblk, N_HEADS, D_HEAD, D_HEAD), F32)],
        scratch_shapes=[pltpu.VMEM((N_HEADS, D_HEAD, D_HEAD), F32)],
    )(c, c, c, proj, proj, alog_row, dtb_row, gn)


def _dn_bwd(c, proj, alog_row, dtb_row, gn, states, do):
    t = c.shape[0]
    nblk = t // SUPER
    colblk, nq, ba, vec, st = _dn_specs(nblk, True)

    def body(cq, ck, cv, ba_ref, z_ref, alog_ref, dtb_ref, gn_ref, s_ref, do_ref,
             dq_ref, dk_ref, dv_ref, dz_ref, dba_ref, dsc_ref, dgn_ref, dstate):
        i = pl.program_id(0)
        hq = pl.program_id(1)

        @pl.when(jnp.logical_and(i == 0, hq == 0))
        def _():
            dstate[...] = jnp.zeros_like(dstate)
            dsc_ref[...] = jnp.zeros_like(dsc_ref)
            dgn_ref[...] = jnp.zeros_like(dgn_ref)

        @pl.when(hq == 0)
        def _():
            dba_ref[...] = jnp.zeros_like(dba_ref)

        lane = _iota2((SUPER, LANE), 1)
        lane1 = _iota2((1, LANE), 1)
        heads = [hq * DN_HP + j for j in range(DN_HP)]
        lanes = [slice(j * LANE, (j + 1) * LANE) for j in range(DN_HP)]
        ds_in = [dstate[hd] for hd in heads]
        s_in = [s_ref[0, j] for j in range(DN_HP)]
        _, vjp = jax.vjp(_dn_block, *_dn_inputs(cq, ck, cv, ba_ref, z_ref, alog_ref, dtb_ref, heads, lanes),
                         gn_ref[...], s_in)
        dq, dk, dv, dbc, dac, dz, dal, ddt, dgn, ds0 = vjp(([do_ref[:, ln].astype(F32) for ln in lanes], ds_in))
        dba = jnp.zeros((SUPER, LANE), F32)
        dal_row = jnp.zeros((1, LANE), F32)
        ddt_row = jnp.zeros((1, LANE), F32)
        for j, (hd, ln) in enumerate(zip(heads, lanes)):
            dq_ref[:, ln] = dq[j]
            dk_ref[:, ln] = dk[j]
            dv_ref[:, ln] = dv[j]
            dz_ref[:, ln] = dz[j].astype(BF16)
            dstate[hd] = ds0[j]
            dba = dba + jnp.where(lane == hd, dbc[j], 0.0) + jnp.where(lane == N_HEADS + hd, dac[j], 0.0)
            dal_row = dal_row + jnp.where(lane1 == hd, dal[j], 0.0)
            ddt_row = ddt_row + jnp.where(lane1 == hd, ddt[j], 0.0)
        dba_ref[...] += dba
        dsc_ref[0:1, :] += dal_row
        dsc_ref[1:2, :] += ddt_row
        dgn_ref[...] += dgn

    outs = pl.pallas_call(
        body,
        name="dn_bwd",
        grid=(nblk, N_HEADS // DN_HP),
        in_specs=[colblk(0), colblk(nq), colblk(2 * nq), ba, colblk(O_Z_DN // (DN_HP * LANE)), vec, vec, vec, st,
                  colblk(0)],
        out_specs=[colblk(0), colblk(0), colblk(0), colblk(0),
                   pl.BlockSpec((SUPER, LANE), lambda i, h: (nblk - 1 - i, 0)),
                   pl.BlockSpec((2, LANE), lambda i, h: (0, 0)), vec],
        out_shape=[jax.ShapeDtypeStruct((t, D_MODEL), F32)] * 3
        + [jax.ShapeDtypeStruct((t, D_MODEL), BF16), jax.ShapeDtypeStruct((t, LANE), F32),
           jax.ShapeDtypeStruct((2, LANE), F32), jax.ShapeDtypeStruct((1, LANE), F32)],
        scratch_shapes=[pltpu.VMEM((N_HEADS, D_HEAD, D_HEAD), F32)],
    )(c, c, c, proj, proj, alog_row, dtb_row, gn, states, do)
    return outs


SB_TQ = 256
SB_TK = 256
SB_HP_FWD = 8
SB_HP_BWD = 4


def _sb_logits(z, mask):
    sp = jnp.log(1.0 + jnp.exp(-jnp.abs(z)))
    lf_raw = -(jnp.maximum(z, 0.0) + sp)
    lb = lf_raw + z
    lf = lf_raw if mask is None else jnp.where(mask, lf_raw, 0.0)
    return lb, lf_raw, lf


def _suffix_sums(x, sel):
    hi, lo = _split2(x)
    d = functools.partial(lax.dot_general, dimension_numbers=NN, preferred_element_type=F32)
    return d(hi, sel) + d(lo, sel)


def _sb_diag_mask(tq, r):
    return r * SB_TK + _iota2((tq, SB_TK), 1) < _iota2((tq, SB_TK), 0)


def _sb_specs(t, tq, hp):
    w = hp * LANE
    q0, k0, v0, z0 = (O_QKV_SB // w, (O_QKV_SB + D_MODEL) // w, (O_QKV_SB + 2 * D_MODEL) // w, O_Z_SB // w)

    def blk(b0):
        return pl.BlockSpec((tq, w), lambda h, i: (i, b0 + h))

    def full(b0, **kw):
        return pl.BlockSpec((t, w), lambda h, i: (0, b0 + h), **kw)

    once = dict(pipeline_mode=pl.Buffered(1))
    return blk(q0), full(k0, **once), full(v0, **once), blk(z0), blk(0), full(0)


def _sb_fwd(proj, shards):
    t = proj.shape[0]
    tq = min(SB_TQ, t)
    ndiag = tq // SB_TK
    scale = 1.0 / math.sqrt(D_HEAD)
    na = len(shards)

    def body(q_ref, k_ref, v_ref, z_ref, *rest):
        x_refs, (o_ref, oraw_ref), land = rest[:na], rest[na:na + 2], rest[na + 2:2 * na + 2]
        sems = rest[2 * na + 2:]
        qi = pl.program_id(1)
        first = jnp.logical_and(pl.program_id(0) == 0, qi == 0)
        last = jnp.logical_and(pl.program_id(0) == pl.num_programs(0) - 1, qi == pl.num_programs(1) - 1)

        @pl.when(first)
        def _():
            for cp in _direct_gather_copies(x_refs, land, *sems):
                cp.start()

        lanes = [slice(hd * LANE, (hd + 1) * LANE) for hd in range(SB_HP_FWD)]
        qs = [(q_ref[:, ln] * scale).astype(BF16) for ln in lanes]
        after = (_iota2((SB_TK, SB_TK), 0) > _iota2((SB_TK, SB_TK), 1)).astype(BF16)
        oraw_ref[...] = jnp.zeros_like(oraw_ref)

        def block(kb, mask, c_lf):
            rows = pl.ds(pl.multiple_of(kb * SB_TK, SB_TK), SB_TK)
            z = _each(lambda q, ln: _dot(q, k_ref[rows, ln], NT), qs, lanes)
            lg = _each(lambda x: _sb_logits(x, mask), z)
            surv = _each(lambda x: _suffix_sums(x[2], after), lg)
            att = _each(lambda x, s, c: jnp.exp(x[0] + s + c), lg, surv, c_lf)
            if mask is not None:
                att = _each(lambda a: jnp.where(mask, a, 0.0), att)
            pv = _each(lambda a, ln: _dot(a, v_ref[rows, ln], NN), att, lanes)
            for p, ln in zip(pv, lanes):
                oraw_ref[:, ln] += p
            return tuple(_each(lambda c, x: c + jnp.sum(x[2], axis=1, keepdims=True), c_lf, lg))

        carry = tuple(jnp.zeros((tq, 1), F32) for _ in range(SB_HP_FWD))
        for r in reversed(range(ndiag)):
            carry = block(qi * ndiag + r, _sb_diag_mask(tq, r), carry)
        lax.fori_loop(0, qi * ndiag, lambda i, c: block(qi * ndiag - 1 - i, None, c), carry)
        o_ref[...] = (oraw_ref[...] * _silu(z_ref[...])).astype(BF16)

        @pl.when(last)
        def _():
            for cp in _direct_gather_copies(x_refs, land, *sems):
                cp.wait()

    q_spec, k_spec, v_spec, z_spec, out, _ = _sb_specs(t, tq, SB_HP_FWD)
    outs = pl.pallas_call(
        body,
        name="sb_fwd",
        grid=(N_HEADS // SB_HP_FWD, t // tq),
        in_specs=[q_spec, k_spec, v_spec, z_spec] + [ANY] * na,
        out_specs=[out, out] + [ANY] * na,
        out_shape=[jax.ShapeDtypeStruct((t, D_MODEL), BF16), jax.ShapeDtypeStruct((t, D_MODEL), F32)]
        + [jax.ShapeDtypeStruct((N_DEV, *v.shape), v.dtype) for v in shards],
        scratch_shapes=_gather_sems(na),
    )(proj, proj, proj, proj, *shards)
    return outs[0], outs[1], outs[2:]


def _sb_bwd(proj, oraw, do, blocks):
    t = proj.shape[0]
    tq = min(SB_TQ, t)
    ndiag = tq // SB_TK
    scale = 1.0 / math.sqrt(D_HEAD)

    def body(q_ref, k_ref, v_ref, z_ref, oraw_ref, do_ref, blk_ref, dq_ref, dk_ref, dv_ref, dz_ref, land_ref,
             dk_acc, dv_acc, p_scr, z_scr, send_sems, recv_sems, local_sem):
        qi = pl.program_id(1)
        nq = pl.num_programs(1)
        hg = pl.program_id(0)
        me = _position()
        mine = 4 * me[0] + 2 * me[1] + me[2]

        def exchange():
            cps = [pltpu.make_async_copy(blk_ref.at[mine], land_ref.at[mine], local_sem)]
            for k, peer in enumerate(_other_devices(me)):
                cps.append(pltpu.make_async_remote_copy(
                    src_ref=blk_ref.at[4 * peer[0] + 2 * peer[1] + peer[2]], dst_ref=land_ref.at[mine],
                    send_sem=send_sems.at[k], recv_sem=recv_sems.at[k], device_id=peer, device_id_type=MESH))
            return cps

        @pl.when(jnp.logical_and(hg == 0, qi == 0))
        def _():
            for cp in exchange():
                cp.start()

        @pl.when(qi == 0)
        def _():
            dk_acc[...] = jnp.zeros_like(dk_acc)
            dv_acc[...] = jnp.zeros_like(dv_acc)

        heads = range(SB_HP_BWD)
        lanes = [slice(hd * LANE, (hd + 1) * LANE) for hd in heads]
        zg = z_ref[...]
        sg = _sigmoid(zg)
        dog = do_ref[...].astype(F32)
        dz_ref[...] = (dog * oraw_ref[...] * (sg * (1.0 + zg * (1.0 - sg)))).astype(BF16)
        d_o = (dog * (zg * sg)).astype(BF16)
        d_o16 = [d_o[:, ln] for ln in lanes]
        qs = [(q_ref[:, ln] * scale).astype(BF16) for ln in lanes]
        ri = _iota2((SB_TK, SB_TK), 0)
        ci = _iota2((SB_TK, SB_TK), 1)
        after = (ri > ci).astype(BF16)
        earlier = (ri < ci).astype(BF16)

        def rows_of(kb):
            return pl.ds(pl.multiple_of(kb * SB_TK, SB_TK), SB_TK)

        def down(kb, mask, c_lf):
            rows = rows_of(kb)
            z = _each(lambda q, ln: _dot(q, k_ref[rows, ln], NT), qs, lanes)
            da = _each(lambda d, ln: _dot(d, v_ref[rows, ln], NT), d_o16, lanes)
            lg = _each(lambda x: _sb_logits(x, mask), z)
            surv = _each(lambda x: _suffix_sums(x[2], after), lg)
            att = _each(lambda x, s, c: jnp.exp(x[0] + s + c), lg, surv, c_lf)
            if mask is not None:
                att = _each(lambda a: jnp.where(mask, a, 0.0), att)
            dv = _each(lambda a, d: _dot(a, d, TN), att, d_o16)
            for hd in heads:
                p_scr[hd, kb] = att[hd] * da[hd]
                z_scr[hd, kb] = z[hd]
                dv_acc[rows, lanes[hd]] += dv[hd]
            return tuple(_each(lambda c, x: c + jnp.sum(x[2], axis=1, keepdims=True), c_lf, lg))

        c_lf = tuple(jnp.zeros((tq, 1), F32) for _ in heads)
        for r in reversed(range(ndiag)):
            c_lf = down(qi * ndiag + r, _sb_diag_mask(tq, r), c_lf)
        lax.fori_loop(0, qi * ndiag, lambda i, c: down(qi * ndiag - 1 - i, None, c), c_lf)

        def up(kb, mask, carry):
            dq, c_p = carry
            rows = rows_of(kb)
            p = [p_scr[hd, kb] for hd in heads]
            zs = [z_scr[hd, kb] for hd in heads]
            before = _each(lambda x, c: _suffix_sums(x, earlier) + c, p, c_p)
            e = _each(lambda x: jnp.exp(-jnp.abs(x)), zs)
            r = _each(lambda x: 1.0 / (1.0 + x), e)
            sig = _each(lambda x, a, b: jnp.where(x >= 0.0, b, a * b), zs, e, r)
            oms = _each(lambda x, a, b: jnp.where(x >= 0.0, a * b, b), zs, e, r)
            if mask is not None:
                sig = _each(lambda a: jnp.where(mask, a, 0.0), sig)
            dzz = _each(lambda x, o, g, b: x * o - g * b, p, oms, sig, before)
            dk = _each(lambda x, q: _dot(x, q, TN), dzz, qs)
            dq = _each(lambda a, x, ln: a + _dot(x, k_ref[rows, ln], NN), dq, dzz, lanes)
            for hd in heads:
                dk_acc[rows, lanes[hd]] += dk[hd]
            return tuple(dq), tuple(_each(lambda c, x: c + jnp.sum(x, axis=1, keepdims=True), c_p, p))

        carry = (tuple(jnp.zeros((tq, D_HEAD), F32) for _ in heads), tuple(jnp.zeros((tq, 1), F32) for _ in heads))
        carry = lax.fori_loop(0, qi * ndiag, lambda kb, c: up(kb, None, c), carry)
        for r in range(ndiag):
            carry = up(qi * ndiag + r, _sb_diag_mask(tq, r), carry)
        dq = carry[0]
        for hd in heads:
            dq_ref[:, lanes[hd]] = (dq[hd] * scale).astype(BF16)

        @pl.when(qi == nq - 1)
        def _():
            dk_ref[...] = dk_acc[...].astype(BF16)
            dv_ref[...] = dv_acc[...].astype(BF16)

        @pl.when(jnp.logical_and(hg == pl.num_programs(0) - 1, qi == nq - 1))
        def _():
            for cp in exchange():
                cp.wait()

    q_spec, k_spec, v_spec, z_spec, blk, full = _sb_specs(t, tq, SB_HP_BWD)
    o = jax.ShapeDtypeStruct((t, D_MODEL), BF16)
    w = SB_HP_BWD * LANE
    return pl.pallas_call(
        body,
        name="sb_bwd",
        grid=(N_HEADS // SB_HP_BWD, t // tq),
        in_specs=[q_spec, k_spec, v_spec, z_spec, blk, blk, ANY],
        out_specs=[blk, full, full, blk, ANY],
        out_shape=[o, o, o, o, jax.ShapeDtypeStruct(blocks.shape, blocks.dtype)],
        scratch_shapes=[pltpu.VMEM((t, w), F32), pltpu.VMEM((t, w), F32)]
        + [pltpu.VMEM((SB_HP_BWD, t // SB_TK, tq, SB_TK), F32)] * 2
        + [pltpu.SemaphoreType.DMA((N_DEV - 1,)), pltpu.SemaphoreType.DMA((N_DEV - 1,)), pltpu.SemaphoreType.DMA],
    )(proj, proj, proj, proj, oraw, do, blocks)


def _mem_kv_fn(mem, mg, w):
    return mm_nn(_rmsnorm(mem, mg), w)


def _mem_kv(mem, mg, w):
    def body(m_ref, g_ref, w_ref, o_ref):
        o_ref[...] = _mem_kv_fn(m_ref[...], g_ref[...], w_ref[...])

    return pl.pallas_call(body, name="mem_kv", out_shape=jax.ShapeDtypeStruct((MEM_LEN, 2 * MEM_W), F32))(mem, mg, w)


def _mem_kv_bwd(mem, mg, w, dmkv):
    def body(m_ref, g_ref, w_ref, d_ref, dg_ref, dw_ref):
        _, vjp = jax.vjp(_mem_kv_fn, m_ref[...], g_ref[...], w_ref[...].astype(F32))
        _, dg, dw = vjp(d_ref[...])
        dg_ref[...] = dg
        dw_ref[...] = dw.astype(BF16)

    return pl.pallas_call(
        body, name="mem_kv_bwd",
        out_shape=[jax.ShapeDtypeStruct((1, D_MODEL), F32), jax.ShapeDtypeStruct((D_MODEL, 2 * MEM_W), BF16)],
    )(mem, mg, w, dmkv)


def _mem_attn(proj, mkv, tm=256):
    t = proj.shape[0]
    tm = min(tm, t)

    def body(q_ref, z_ref, kv_ref, o_ref):
        o_ref[...] = _mem_fn(q_ref[...], z_ref[...], kv_ref[...]).astype(BF16)

    return pl.pallas_call(
        body,
        name="mem_attn",
        grid=(t // tm,),
        in_specs=[pl.BlockSpec((tm, MEM_W), lambda i: (i, O_MQ // MEM_W)),
                  pl.BlockSpec((tm, MEM_W), lambda i: (i, O_MZ // MEM_W)),
                  pl.BlockSpec((MEM_LEN, 2 * MEM_W), lambda i: (0, 0))],
        out_specs=pl.BlockSpec((tm, MEM_W), lambda i: (i, 0)),
        out_shape=jax.ShapeDtypeStruct((t, MEM_W), BF16),
    )(proj, proj, mkv)


def _mem_attn_bwd(proj, mkv, do, tm=256):
    t = proj.shape[0]
    tm = min(tm, t)

    def body(q_ref, z_ref, kv_ref, do_ref, dq_ref, dz_ref, dkv_ref):
        _, vjp = jax.vjp(_mem_fn, q_ref[...], z_ref[...], kv_ref[...])
        dq, dz, dkv = vjp(do_ref[...].astype(F32))
        dq_ref[...] = dq.astype(BF16)
        dz_ref[...] = dz.astype(BF16)

        @pl.when(pl.program_id(0) == 0)
        def _():
            dkv_ref[...] = jnp.zeros_like(dkv_ref)

        dkv_ref[...] += dkv

    blk = pl.BlockSpec((tm, MEM_W), lambda i: (i, 0))
    kv = pl.BlockSpec((MEM_LEN, 2 * MEM_W), lambda i: (0, 0))
    return pl.pallas_call(
        body,
        name="mem_attn_bwd",
        grid=(t // tm,),
        in_specs=[pl.BlockSpec((tm, MEM_W), lambda i: (i, O_MQ // MEM_W)),
                  pl.BlockSpec((tm, MEM_W), lambda i: (i, O_MZ // MEM_W)), kv, blk],
        out_specs=[blk, blk, kv],
        out_shape=[jax.ShapeDtypeStruct((t, MEM_W), BF16), jax.ShapeDtypeStruct((t, MEM_W), BF16),
                   jax.ShapeDtypeStruct((MEM_LEN, 2 * MEM_W), F32)],
    )(proj, proj, mkv, do)


def _proj_gather(h, w_alt, shards, tn=384):
    t = h.shape[0]
    n, kdim = w_alt.shape
    assert n % tn == 0
    nj = n // tn
    na = len(shards)

    def body(h_ref, w_ref, *rest):
        x_refs, o_ref, land = rest[:na], rest[na], rest[na + 1:2 * na + 1]
        send_sems, recv_sems, local_sems = rest[2 * na + 1:]
        j = pl.program_id(0)

        def copies():
            return _direct_gather_copies(x_refs, land, send_sems, recv_sems, local_sems)

        @pl.when(j == 0)
        def _():
            for cp in copies():
                cp.start()

        o_ref[...] = _dot(h_ref[...], w_ref[...], NT)

        @pl.when(j == nj - 1)
        def _():
            for cp in copies():
                cp.wait()

    outs = pl.pallas_call(
        body,
        name="proj",
        grid=(nj,),
        in_specs=[pl.BlockSpec((t, kdim), lambda j: (0, 0)), pl.BlockSpec((tn, kdim), lambda j: (j, 0))] + [ANY] * na,
        out_specs=[pl.BlockSpec((t, tn), lambda j: (0, j))] + [ANY] * na,
        out_shape=[jax.ShapeDtypeStruct((t, n), F32)]
        + [jax.ShapeDtypeStruct((N_DEV, *v.shape), v.dtype) for v in shards],
        scratch_shapes=_gather_sems(na),
    )(h, w_alt, *shards)
    return outs[0], outs[1:]


def _local_step(x, mem, tgt, norm_g, mem_norm_g, w_alt, alog_row, dtb_row, dn_norm_g, final_g, shards):
    h = _norm_in(x, norm_g)
    s_kv, s_dn, s_sb, s_out, s_mem, s_conv = shards
    proj, (g_kv, g_conv) = _proj_gather(h, w_alt, [s_kv, s_conv])
    w_mem_kv = g_kv.reshape(D_MODEL, 2 * MEM_W)
    conv_w = g_conv.transpose(1, 0, 2).reshape(CONV_K, 3 * D_MODEL)

    c = _dn_conv(proj, conv_w)
    o_dn, states = _dn_fwd(c, proj, alog_row, dtb_row, dn_norm_g)
    o_sb, o_sb_raw, (g_dn, g_sb, g_out, g_mem) = _sb_fwd(proj, [s_dn, s_sb, s_out, s_mem])
    w_br_dn = g_dn.reshape(D_MODEL, D_MODEL)
    w_br_sb = g_sb.reshape(D_MODEL, D_MODEL)
    w_out = g_out.reshape(D_MODEL, D_MODEL)
    w_br_mem = g_mem.transpose(1, 0, 2).reshape(MEM_W, D_MODEL)
    mkv = _mem_kv(mem, mem_norm_g, w_mem_kv)
    o_m = _mem_attn(proj, mkv)

    (loss, dout, d_final_g, merged, dy_dn, dy_sb, dy_m, dgates, do_dn, do_sb, do_m) = _block_tail(
        proj, o_dn, o_sb, o_m, x, tgt, w_br_dn, w_br_sb, w_br_mem, w_out, final_g)
    dw_out = _matmul_tn(merged, dout, BF16, 256, 1024, 2048, "dw_out")
    dw_br_dn = _matmul_tn(o_dn, dy_dn, BF16, 256, 1024, 2048, "dw_br_dn")
    dw_br_sb = _matmul_tn(o_sb, dy_sb, BF16, 256, 1024, 2048, "dw_br_sb")
    dw_br_mem = _matmul_tn(o_m, dy_m, BF16, 256, 1024, 2048, "dw_br_mem")

    dmq, dmz, dmkv = _mem_attn_bwd(proj, mkv, do_m)
    d_mem_norm_g, dw_mem_kv = _mem_kv_bwd(mem, mem_norm_g, w_mem_kv, dmkv)
    rows_d = D_MODEL // N_DEV
    small_blocks = jnp.concatenate([
        dw_br_dn.reshape(N_DEV, rows_d, D_MODEL), dw_br_sb.reshape(N_DEV, rows_d, D_MODEL),
        dw_out.reshape(N_DEV, rows_d, D_MODEL), dw_mem_kv.reshape(N_DEV, rows_d // 2, D_MODEL),
        dw_br_mem.reshape(MEM_W, N_DEV, rows_d).transpose(1, 0, 2).reshape(N_DEV, MEM_W // N_DEV, D_MODEL)], axis=1)
    dq_sb, dk_sb, dv_sb, dz_sb, small_parts = _sb_bwd(proj, o_sb_raw, do_sb, small_blocks)
    (d_small,) = _sum_slots([small_parts], "sum_small_grads")
    dcq, dck, dcv, dz_dn, dba, dscal, d_dn_norm_g = _dn_bwd(c, proj, alog_row, dtb_row, dn_norm_g, states, do_dn)
    dq_dn, dcw_q = _dn_conv_bwd(proj, conv_w, dcq, 0)
    dk_dn, dcw_k = _dn_conv_bwd(proj, conv_w, dck, 1)
    dv_dn, dcw_v = _dn_conv_bwd(proj, conv_w, dcv, 2)
    d_conv_w = jnp.concatenate([dcw_q, dcw_k, dcw_v], axis=1)

    dproj = jnp.concatenate([dq_dn, dk_dn, dv_dn, dz_dn, dq_sb, dk_sb, dv_sb, dz_sb, dmq, dmz, dgates,
                             dba.astype(BF16)], axis=1)
    dw_alt = _matmul_tn(dproj, h, BF16, 384, 1024, 2048, "dw_alt")
    grad_x, d_norm_g = _grad_x(dproj, w_alt, x, norm_g, dout)
    return dict(loss=loss, grad_x=grad_x, norm_g=d_norm_g, mem_norm_g=d_mem_norm_g, w_alt=dw_alt, conv_w=d_conv_w,
                scal=dscal, dn_norm_g=d_dn_norm_g, small=d_small, final_g=d_final_g)


MESH = pl.DeviceIdType.MESH
ANY = pl.BlockSpec(memory_space=pl.ANY)


def _position():
    return lax.axis_index("x"), lax.axis_index("y"), lax.axis_index("c")


def _other_devices(me):
    return [tuple(1 - p if (f >> s) & 1 else p for p, s in zip(me, (2, 1, 0))) for f in range(1, N_DEV)]


def _direct_gather_copies(x_refs, land_refs, send_sems, recv_sems, local_sems):
    me = _position()
    mine = 4 * me[0] + 2 * me[1] + me[2]
    cps = []
    for a, (x_ref, land) in enumerate(zip(x_refs, land_refs)):
        cps.append(pltpu.make_async_copy(x_ref, land.at[mine], local_sems.at[a]))
        for k, peer in enumerate(_other_devices(me)):
            cps.append(pltpu.make_async_remote_copy(
                src_ref=x_ref, dst_ref=land.at[mine], send_sem=send_sems.at[7 * a + k],
                recv_sem=recv_sems.at[7 * a + k], device_id=peer, device_id_type=MESH))
    return cps


def _gather_sems(n):
    return [pltpu.SemaphoreType.DMA((7 * n,)), pltpu.SemaphoreType.DMA((7 * n,)), pltpu.SemaphoreType.DMA((n,))]


def _all_gather(xs, name):
    n = len(xs)

    def body(*refs):
        x_refs, o_refs = refs[:n], refs[n:2 * n]
        send_sems, recv_sems, local_sems = refs[2 * n:]
        x, y, c = _position()
        me, sibling = (x, y, c), (x, y, 1 - c)
        x_nbr, y_nbr, diag = (1 - x, y, c), (x, 1 - y, c), (1 - x, 1 - y, c)
        south = c == 0
        relay_from = tuple(jnp.where(south, a, b) for a, b in zip(y_nbr, x_nbr))
        relay_to = tuple(jnp.where(south, a, b) for a, b in zip(x_nbr, y_nbr))

        def slot(p):
            return 4 * p[0] + 2 * p[1] + p[2]

        def copy(a, k, block, to, src=None):
            dst = o_refs[a].at[slot(block)]
            return pltpu.make_async_remote_copy(
                src_ref=dst if src is None else src, dst_ref=dst, send_sem=send_sems.at[7 * a + k],
                recv_sem=recv_sems.at[7 * a + k], device_id=to, device_id_type=MESH)

        mine = [pltpu.make_async_copy(x_refs[a], o_refs[a].at[slot(me)], local_sems.at[a]) for a in range(n)]
        for cp in mine:
            cp.start()
        sends = []
        for a in range(n):
            sends += [copy(a, 0, me, sibling, src=x_refs[a]), copy(a, 1, me, x_nbr, src=x_refs[a]),
                      copy(a, 2, me, y_nbr, src=x_refs[a])]
        for cp in sends:
            cp.start()
        later = []
        for a in range(n):
            copy(a, 1, x_nbr, me).wait_recv()
            copy(a, 2, y_nbr, me).wait_recv()
            later += [copy(a, 3, relay_from, relay_to), copy(a, 4, x_nbr, sibling), copy(a, 5, y_nbr, sibling)]
            for cp in later[-3:]:
                cp.start()
        for a in range(n):
            copy(a, 3, diag, me).wait_recv()
            later.append(copy(a, 6, diag, sibling))
            later[-1].start()
        for a in range(n):
            copy(a, 0, sibling, me).wait_recv()
            for k, chip in ((4, x_nbr), (5, y_nbr), (6, diag)):
                copy(a, k, (chip[0], chip[1], 1 - c), me).wait_recv()
        for cp in sends + later:
            cp.wait_send()
        for cp in mine:
            cp.wait()

    return pl.pallas_call(
        body,
        name=name,
        in_specs=[ANY] * n,
        out_specs=[ANY] * n,
        out_shape=[jax.ShapeDtypeStruct((N_DEV, *v.shape), v.dtype) for v in xs],
        scratch_shapes=[pltpu.SemaphoreType.DMA((7 * n,)), pltpu.SemaphoreType.DMA((7 * n,)),
                        pltpu.SemaphoreType.DMA((n,))],
    )(*xs)


def _window_view(ref, dest):
    return ref.at[pl.ds(WIN_ROW0[dest], WIN_W), :]


def _chunk_rows(rows, cols):
    return max(ch for ch in range(ROW_TILE, rows + 1, ROW_TILE) if rows % ch == 0 and ch * cols <= (1 << 20))


def _halving_stage(xs, axis, name, out_dtype, windowed=(), gather=()):
    n_arr = len(xs)
    metas = []
    for k, v in enumerate(xs):
        if k in windowed:
            metas.append((N_DEV // 2, WIN_W, v.shape[1]))
        else:
            assert v.shape[1] == 2
            metas.append((v.shape[0], v.shape[2], v.shape[3]))
    chunk = [_chunk_rows(r, c) for (_, r, c) in metas]
    offs = [sum(m[0] for m in metas[:k]) for k in range(n_arr)]
    n_sem = sum(m[0] for m in metas)

    n_g = len(gather)

    def body(*refs):
        x_refs, g_refs = refs[:n_arr], refs[n_arr:n_arr + n_g]
        outs = refs[n_arr + n_g:]
        o_refs, land_refs, gl_refs = outs[:n_arr], outs[n_arr:2 * n_arr], outs[2 * n_arr:2 * n_arr + n_g]
        rest = outs[2 * n_arr + n_g:]
        bufs = rest[:3 * n_arr]
        send_sems, recv_sems, in_sems, out_sems = rest[3 * n_arr:3 * n_arr + 4]
        gathers = _direct_gather_copies(g_refs, gl_refs, *rest[3 * n_arr + 4:]) if n_g else []
        for cp in gathers:
            cp.start()
        pos = dict(zip("xyc", _position()))
        bit = pos[axis]
        peer = tuple(1 - pos[a] if a == axis else pos[a] for a in "xyc")

        def view(k, i, b):
            if k in windowed:
                return _window_view(x_refs[k], 2 * i + b)
            return x_refs[k].at[i, b]

        def add_blocks(k, a_view, b_view, o_view):
            _hbm_add(a_view, b_view, o_view, bufs[3 * k:3 * k + 3], in_sems, out_sems, chunk[k])

        for b in (0, 1):
            @pl.when(bit == b)
            def _(b=b):
                sends = []
                for k in range(n_arr):
                    for i in range(metas[k][0]):
                        cp = pltpu.make_async_remote_copy(
                            src_ref=view(k, i, 1 - b), dst_ref=land_refs[k].at[i], send_sem=send_sems.at[offs[k] + i],
                            recv_sem=recv_sems.at[offs[k] + i], device_id=peer, device_id_type=MESH)
                        cp.start()
                        sends.append(cp)
                idx = 0
                for k in range(n_arr):
                    for i in range(metas[k][0]):
                        sends[idx].wait_recv()
                        add_blocks(k, view(k, i, b), land_refs[k].at[i], o_refs[k].at[i])
                        idx += 1
                for cp in sends:
                    cp.wait_send()

        for cp in gathers:
            cp.wait()

    out_shape = [jax.ShapeDtypeStruct(m, out_dtype) for m in metas]
    land_shape = [jax.ShapeDtypeStruct(m, v.dtype) for m, v in zip(metas, xs)]
    g_shape = [jax.ShapeDtypeStruct((N_DEV, *v.shape), v.dtype) for v in gather]
    scratch = []
    for k in range(n_arr):
        blk = (2, chunk[k], metas[k][2])
        scratch += [pltpu.VMEM(blk, xs[k].dtype)] * 2 + [pltpu.VMEM(blk, out_dtype)]
    scratch += [pltpu.SemaphoreType.DMA((n_sem,)), pltpu.SemaphoreType.DMA((n_sem,)),
                pltpu.SemaphoreType.DMA((2, 2)), pltpu.SemaphoreType.DMA((2,))]
    if n_g:
        scratch += _gather_sems(n_g)
    outs = pl.pallas_call(
        body,
        name=name,
        in_specs=[ANY] * (n_arr + n_g),
        out_specs=[ANY] * (2 * n_arr + n_g),
        out_shape=out_shape + land_shape + g_shape,
        scratch_shapes=scratch,
    )(*xs, *gather)
    return outs[:n_arr], outs[2 * n_arr:]


def _hbm_add(a_view, b_view, o_view, bufs, in_sems, out_sems, ch):
    rows = a_view.shape[0]
    nch = rows // ch
    va, vb, vo = bufs

    def rows_of(j):
        return pl.ds(pl.multiple_of(j * ch, 16), ch)

    def loads(j, s):
        return (pltpu.make_async_copy(a_view.at[rows_of(j), :], va.at[s], in_sems.at[0, s]),
                pltpu.make_async_copy(b_view.at[rows_of(j), :], vb.at[s], in_sems.at[1, s]))

    def store(j, s):
        return pltpu.make_async_copy(vo.at[s], o_view.at[rows_of(j), :], out_sems.at[s])

    for cp in loads(0, 0):
        cp.start()

    def step(j, _):
        s = lax.rem(j, 2)

        @pl.when(j + 1 < nch)
        def _():
            for cp in loads(j + 1, 1 - s):
                cp.start()

        for cp in loads(j, s):
            cp.wait()

        @pl.when(j >= 2)
        def _():
            store(j - 2, s).wait()

        vo[s] = (va[s].astype(F32) + vb[s].astype(F32)).astype(vo.dtype)
        store(j, s).start()
        return 0

    lax.fori_loop(0, nch, step, 0)
    for j in range(max(0, nch - 2), nch):
        store(j, j % 2).wait()


def _xy_stage(xs, first, name):
    n_arr = len(xs)
    if first:
        shapes = [(v.shape[2] // 2, v.shape[3]) for v in xs]
        ins = list(xs)
    else:
        shapes = [(a.shape[1], a.shape[2]) for a, _ in xs]
        ins = [v for pair in xs for v in pair]
    n_blk = 2 if first else 1
    out_dtype = BF16 if first else F32
    chunk = [_chunk_rows(r, c) for (r, c) in shapes]
    n_sem = 2 * n_blk * n_arr

    def body(*refs):
        n_in = len(ins)
        in_refs = refs[:n_in]
        n_out = 2 * n_arr if first else n_arr
        o_refs = refs[n_in:n_in + n_out]
        land = refs[n_in + n_out:n_in + n_out + 2 * n_arr]
        rest = refs[n_in + n_out + 2 * n_arr:]
        bufs = rest[:3 * n_arr]
        send_sems, recv_sems, in_sems, out_sems = rest[3 * n_arr:]
        x, y, c = _position()
        peers = {"x": (1 - x, y, c), "y": (x, 1 - y, c)}
        jobs = []
        for k in range(n_arr):
            r, _ = shapes[k]
            half_a, half_b = pl.ds(0, r), pl.ds(r, r)
            if first:
                src = in_refs[k]
                for i in range(2):
                    jobs.append((k, src.at[i, 1 - y, half_a, :], src.at[i, y, half_a, :], land[2 * k].at[i],
                                 o_refs[2 * k].at[i], "y"))
                    jobs.append((k, src.at[1 - x, i, half_b, :], src.at[x, i, half_b, :], land[2 * k + 1].at[i],
                                 o_refs[2 * k + 1].at[i], "x"))
            else:
                a1, b1 = in_refs[2 * k], in_refs[2 * k + 1]
                jobs.append((k, a1.at[1 - x], a1.at[x], land[2 * k], o_refs[k].at[half_a, :], "x"))
                jobs.append((k, b1.at[1 - y], b1.at[y], land[2 * k + 1], o_refs[k].at[half_b, :], "y"))
        sends = []
        for n, (k, send, _, landing, _, axis) in enumerate(jobs):
            cp = pltpu.make_async_remote_copy(src_ref=send, dst_ref=landing, send_sem=send_sems.at[n],
                                              recv_sem=recv_sems.at[n], device_id=peers[axis], device_id_type=MESH)
            cp.start()
            sends.append(cp)
        for cp, (k, _, kept, landing, out, _) in zip(sends, jobs):
            cp.wait_recv()
            _hbm_add(kept, landing, out, bufs[3 * k:3 * k + 3], in_sems, out_sems, chunk[k])
        for cp in sends:
            cp.wait_send()

    if first:
        out_shape = [jax.ShapeDtypeStruct((2, r, c), BF16) for (r, c) in shapes for _ in range(2)]
        land_shape = out_shape
    else:
        out_shape = [jax.ShapeDtypeStruct((2 * r, c), F32) for (r, c) in shapes]
        land_shape = [jax.ShapeDtypeStruct((r, c), BF16) for (r, c) in shapes for _ in range(2)]
    scratch = []
    for k in range(n_arr):
        scratch += [pltpu.VMEM((2, chunk[k], shapes[k][1]), BF16)] * 2 + [pltpu.VMEM((2, chunk[k], shapes[k][1]), out_dtype)]
    scratch += [pltpu.SemaphoreType.DMA((n_sem,)), pltpu.SemaphoreType.DMA((n_sem,)),
                pltpu.SemaphoreType.DMA((2, 2)), pltpu.SemaphoreType.DMA((2,))]
    outs = pl.pallas_call(
        body,
        name=name,
        in_specs=[ANY] * len(ins),
        out_specs=[ANY] * (len(out_shape) + len(land_shape)),
        out_shape=out_shape + land_shape,
        scratch_shapes=scratch,
    )(*ins)
    outs = outs[:len(out_shape)]
    return [(outs[2 * k], outs[2 * k + 1]) for k in range(n_arr)] if first else list(outs)


def _reduce_scatter(dw_al, blocks, gather=()):
    xs = [dw_al] + [b.reshape(N_DEV // 2, 2, *b.shape[1:]) for b in blocks]
    ys, gathered = _halving_stage(xs, "c", "rs_c", BF16, windowed=(0,), gather=gather)
    pairs = _xy_stage([v.reshape(2, 2, *v.shape[1:]) for v in ys], True, "rs_xy1")
    return _xy_stage(pairs, False, "rs_xy2"), gathered


def _sum_slots(gs, name):
    n = len(gs)

    def body(*refs):
        for g_ref, o_ref in zip(refs[:n], refs[n:]):
            acc = g_ref[0].astype(F32)
            for d in range(1, N_DEV):
                acc = acc + g_ref[d].astype(F32)
            o_ref[...] = acc

    return pl.pallas_call(body, name=name, out_shape=[jax.ShapeDtypeStruct(g.shape[1:], F32) for g in gs])(*gs)


def _assemble_w_al(wins, bas):
    cols = wins.shape[2]
    n_buf = 3
    ends = [WIN_ROW0[d + 1] if d + 1 < N_DEV else WIN_ROW0[d] + WIN_W for d in range(N_DEV)]
    tail = W_AL - ends[-1]

    def body(w_ref, ba_ref, o_ref, buf, zeros, ld_sems, st_sems, ba_sem):
        def load(d):
            return pltpu.make_async_copy(w_ref.at[d], buf.at[d % n_buf], ld_sems.at[d % n_buf])

        def store(d):
            n = ends[d] - WIN_ROW0[d]
            return pltpu.make_async_copy(buf.at[d % n_buf, pl.ds(0, n), :],
                                         o_ref.at[pl.ds(WIN_ROW0[d], n), :], st_sems.at[d % n_buf])

        zeros[...] = jnp.zeros_like(zeros)
        fill = pltpu.make_async_copy(zeros, o_ref.at[pl.ds(ends[-1], tail), :], ba_sem)
        fill.start()
        fill.wait()
        load(0).start()
        for d in range(N_DEV):
            if d + 1 < N_DEV:
                if d + 1 >= n_buf:
                    store(d + 1 - n_buf).wait()
                load(d + 1).start()
            load(d).wait()
            if d > 0:
                ov = WIN_ROW0[d - 1] + WIN_W - WIN_ROW0[d]
                buf[d % n_buf, :ov, :] = buf[d % n_buf, :ov, :] + buf[(d - 1) % n_buf, WIN_W - ov:, :]
            if d == N_DEV - 1:
                ba_copy = pltpu.make_async_copy(
                    ba_ref.at[BA_DEV], buf.at[d % n_buf, pl.ds(WIN_W - N_BA, N_BA), :], ba_sem)
                ba_copy.start()
                ba_copy.wait()
            store(d).start()
        for d in range(N_DEV - n_buf, N_DEV):
            store(d).wait()

    return pl.pallas_call(
        body,
        name="assemble_w_al",
        in_specs=[ANY, ANY],
        out_specs=ANY,
        out_shape=jax.ShapeDtypeStruct((W_AL, cols), wins.dtype),
        scratch_shapes=[pltpu.VMEM((n_buf, WIN_W, cols), wins.dtype), pltpu.VMEM((tail, cols), wins.dtype),
                        pltpu.SemaphoreType.DMA((n_buf,)), pltpu.SemaphoreType.DMA((n_buf,)), pltpu.SemaphoreType.DMA],
    )(wins, bas)


def _adamw_math(w, g, m, v):
    m_new = ADAM_B1 * m + (1.0 - ADAM_B1) * g
    v_new = ADAM_B2 * v + (1.0 - ADAM_B2) * (g * g)
    m_hat = m_new / (1.0 - ADAM_B1 ** ADAM_STEP)
    v_hat = v_new / (1.0 - ADAM_B2 ** ADAM_STEP)
    return -ADAM_LR * (m_hat / (jnp.sqrt(v_hat) + ADAM_EPS) + ADAM_WD * w), m_new, v_new


def _adamw(w, g, m, v, name, tb=134):
    r, _, c = w.shape
    assert r % tb == 0

    def body(w_ref, g_ref, m_ref, v_ref, d_ref, nm_ref, nv_ref):
        d_ref[...], nm_ref[...], nv_ref[...] = _adamw_math(w_ref[...], g_ref[...], m_ref[...], v_ref[...])

    blk = pl.BlockSpec((tb, 1, c), lambda i: (i, 0, 0))
    o = jax.ShapeDtypeStruct(w.shape, F32)
    return pl.pallas_call(body, name=name, grid=(r // tb,), in_specs=[blk] * 4, out_specs=[blk] * 3,
                          out_shape=[o, o, o])(w, g, m, v)


def _adamw_many(ws, gs, ms, vs, name):
    n = len(ws)

    def body(*refs):
        for k in range(n):
            w_ref, g_ref, m_ref, v_ref = (refs[j * n + k] for j in range(4))
            d_ref, nm_ref, nv_ref = (refs[(4 + j) * n + k] for j in range(3))
            d_ref[...], nm_ref[...], nv_ref[...] = _adamw_math(w_ref[...], g_ref[...], m_ref[...], v_ref[...])

    shapes = [jax.ShapeDtypeStruct(w.shape, F32) for w in ws]
    outs = pl.pallas_call(body, name=name, out_shape=shapes * 3)(*ws, *gs, *ms, *vs)
    return outs[:n], outs[n:2 * n], outs[2 * n:]


def _select(me, table):
    return sum(jnp.where(me == d, jnp.int32(v), jnp.int32(0)) for d, v in enumerate(table))


WIN_SHIFT = tuple(SHARD_W * d - WIN_ROW0[d] for d in range(N_DEV))
PAD_L = 64
PAD_R = 64
assert max(WIN_SHIFT) <= PAD_L and WIN_W + N_BA - SHARD_W <= PAD_R


def _shard_to_window(shard_t, me):
    shift = _select(me, WIN_SHIFT)
    padded = jnp.pad(shard_t, ((PAD_L, PAD_R), (0, 0)))
    cols = shard_t.shape[1]
    lo = lax.dynamic_slice(padded, (PAD_L - shift, 0), (WIN_W, cols))
    hi = lax.dynamic_slice(padded, (PAD_L - shift + N_BA, 0), (WIN_W, cols))
    aligned = _select(me, WIN_ROW0) + lax.broadcasted_iota(jnp.int32, (WIN_W, 1), 0)
    return jnp.where(aligned >= ORIG_BA, hi, lo)


def _window_to_shard(win, ba_grad, me):
    shift = _select(me, WIN_SHIFT)
    cols = win.shape[1]
    padded = jnp.pad(win, ((N_BA, PAD_R), (0, 0)))
    lo = lax.dynamic_slice(padded, (N_BA + shift, 0), (SHARD_W, cols))
    hi = lax.dynamic_slice(padded, (shift, 0), (SHARD_W, cols))
    orig = SHARD_W * me + lax.broadcasted_iota(jnp.int32, (SHARD_W, 1), 0)
    ba_full = lax.dynamic_update_slice(jnp.zeros((SHARD_W, cols), win.dtype), ba_grad, (BA_LOCAL, 0))
    return jnp.where(orig < ORIG_BA, lo, jnp.where(orig >= ORIG_BA + N_BA, hi, ba_full))


def _pad_row(v, width=D_MODEL):
    v = v.reshape(1, -1)
    return jnp.pad(v, ((0, 0), (0, width - v.shape[1])))


def kernel(x, mem, norm_g, mem_norm_g, w_in, conv_w, a_log, dt_bias, dn_norm_g, w_mem_kv, w_br_dn, w_br_sb, w_br_mem, w_out, final_g, loss_target, m_norm_g, m_mem_norm_g, m_w_in, m_conv_w, m_a_log, m_dt_bias, m_dn_norm_g, m_w_mem_kv, m_w_br_dn, m_w_br_sb, m_w_br_mem, m_w_out, m_final_g, v_norm_g, v_mem_norm_g, v_w_in, v_conv_w, v_a_log, v_dt_bias, v_dn_norm_g, v_w_mem_kv, v_w_br_dn, v_w_br_sb, v_w_br_mem, v_w_out, v_final_g):
    xi, yi, ci = _position()
    me = 4 * xi + 2 * yi + ci

    shard_t = w_in[0].T
    win = _shard_to_window(shard_t, me).astype(BF16)
    ba = shard_t[BA_LOCAL:BA_LOCAL + N_BA, :].astype(BF16)
    g_win, g_ba = _all_gather([win, ba], "gather_weights")
    w_alt = _assemble_w_al(g_win, g_ba)

    shards = [w_mem_kv[0].astype(BF16), w_br_dn[0].astype(BF16), w_br_sb[0].astype(BF16), w_out[0].astype(BF16),
              w_br_mem[0].astype(BF16), conv_w[0]]
    r = _local_step(x[0], mem[0], loss_target[0], norm_g, mem_norm_g, w_alt, _pad_row(a_log, LANE),
                    _pad_row(dt_bias, LANE), dn_norm_g, final_g.reshape(1, D_MODEL), shards)

    dw_alt = r["w_alt"]
    parts = [r["norm_g"], r["mem_norm_g"], r["final_g"], r["dn_norm_g"], r["scal"], r["loss"], r["conv_w"],
             dw_alt[O_BA:O_BA + N_BA, :].astype(F32)]
    (g_win,), gathered = _reduce_scatter(dw_alt, [], gather=parts)
    rows_d = D_MODEL // N_DEV
    g_small = r["small"]
    g_dn, g_sb, g_out = (g_small[k * rows_d:(k + 1) * rows_d] for k in range(3))
    g_kv = g_small[3 * rows_d:3 * rows_d + rows_d // 2].reshape(rows_d, 2 * MEM_W)
    g_mem = g_small[3 * rows_d + rows_d // 2:].reshape(MEM_W, rows_d)
    s_norm_g, s_mem_norm_g, s_final_g, s_dn_norm_g, s_scal, s_loss, s_conv, s_ba = _sum_slots(gathered, "sum_small")
    loss = s_loss[0, 0]
    cw = conv_w.shape[2]
    g_conv = lax.dynamic_slice(s_conv, (0, cw * me), (CONV_K, cw))
    g_w_in_t = _window_to_shard(g_win, s_ba, me)
    grads = dict(norm_g=s_norm_g, mem_norm_g=s_mem_norm_g, w_in=g_w_in_t.T[None], conv_w=g_conv[None],
                 a_log=s_scal[0:1, :N_HEADS], dt_bias=s_scal[1:2, :N_HEADS], dn_norm_g=s_dn_norm_g, w_mem_kv=g_kv[None],
                 w_br_dn=g_dn[None], w_br_sb=g_sb[None], w_br_mem=g_mem[None], w_out=g_out[None],
                 final_g=s_final_g.reshape(D_MODEL))

    params = dict(norm_g=(norm_g, m_norm_g, v_norm_g), mem_norm_g=(mem_norm_g, m_mem_norm_g, v_mem_norm_g),
                  w_in=(w_in, m_w_in, v_w_in), conv_w=(conv_w, m_conv_w, v_conv_w), a_log=(a_log, m_a_log, v_a_log),
                  dt_bias=(dt_bias, m_dt_bias, v_dt_bias), dn_norm_g=(dn_norm_g, m_dn_norm_g, v_dn_norm_g),
                  w_mem_kv=(w_mem_kv, m_w_mem_kv, v_w_mem_kv), w_br_dn=(w_br_dn, m_w_br_dn, v_w_br_dn),
                  w_br_sb=(w_br_sb, m_w_br_sb, v_w_br_sb), w_br_mem=(w_br_mem, m_w_br_mem, v_w_br_mem),
                  w_out=(w_out, m_w_out, v_w_out), final_g=(final_g, m_final_g, v_final_g))
    order = list(params)
    deltas, new_m, new_v = {}, {}, {}
    deltas["w_in"], new_m["w_in"], new_v["w_in"] = (jnp.transpose(o, (1, 2, 0)) for o in _adamw(
        jnp.transpose(w_in, (2, 0, 1)), g_w_in_t[:, None, :], jnp.transpose(m_w_in, (2, 0, 1)),
        jnp.transpose(v_w_in, (2, 0, 1)), "adamw_w_in"))
    rest = [nm for nm in order if nm != "w_in"]

    def two_d(a):
        return a.reshape(1, -1) if a.ndim == 1 else a

    d_l, m_l, v_l = _adamw_many([two_d(params[nm][0]) for nm in rest], [two_d(grads[nm]) for nm in rest],
                                [two_d(params[nm][1]) for nm in rest], [two_d(params[nm][2]) for nm in rest], "adamw_rest")
    for k, nm in enumerate(rest):
        shp = params[nm][0].shape
        deltas[nm], new_m[nm], new_v[nm] = d_l[k].reshape(shp), m_l[k].reshape(shp), v_l[k].reshape(shp)
    return (loss, r["grad_x"][None], *[grads[nm] for nm in order], *[deltas[nm] for nm in order],
            *[new_m[nm] for nm in order], *[new_v[nm] for nm in order])
```

```python
import functools
import math

import jax
import jax.numpy as jnp
from jax import lax
from jax.experimental import pallas as pl
from jax.experimental.pallas import tpu as pltpu

F32 = jnp.float32
BF16 = jnp.bfloat16

D_MODEL = 1024
N_DEV = 8
N_HEADS = 8
D_HEAD = 128
DN_CHUNK = 64
CONV_K = 4
MEM_LEN = 256
MEM_HEADS = 4
MEM_DH = 64
MEM_W = MEM_HEADS * MEM_DH
NORM_EPS = 1e-6
IN_WIDTH = 11792
SHARD_W = IN_WIDTH // N_DEV

LANE = 128
SUPER = 2 * DN_CHUNK

O_QKV_DN = 0
O_Z_DN = 3072
O_QKV_SB = 4096
O_Z_SB = 7168
O_MQ = 8192
O_MZ = 8448
O_GATES = 8704
O_BA = 11776
W_AL = 11904
ORIG_BA = 4096
N_BA = 16

BA_DEV = ORIG_BA // SHARD_W
BA_LOCAL = ORIG_BA - BA_DEV * SHARD_W


def _aligned_col(o):
    return o if o < ORIG_BA else o - N_BA


ROW_TILE = 16
WIN_W = 1504
WIN_ROW0 = tuple(_aligned_col(SHARD_W * d) // ROW_TILE * ROW_TILE for d in range(N_DEV))
assert not any(ORIG_BA <= SHARD_W * d < ORIG_BA + N_BA for d in range(N_DEV))
assert all(WIN_ROW0[d] + WIN_W >= _aligned_col(SHARD_W * (d + 1) - 1) + 1 for d in range(N_DEV))
assert all(WIN_ROW0[d + 1] <= WIN_ROW0[d] + WIN_W for d in range(N_DEV - 1))
assert WIN_ROW0[-1] + WIN_W == O_BA + N_BA

ADAM_LR = 0.001
ADAM_B1 = 0.9
ADAM_B2 = 0.999
ADAM_EPS = 1e-08
ADAM_WD = 0.01
ADAM_STEP = 10

NN = (((1,), (0,)), ((), ()))
NT = (((1,), (1,)), ((), ()))
TN = (((0,), (0,)), ((), ()))


def _dot(a, b, dims):
    return lax.dot_general(a.astype(BF16), b.astype(BF16), dims, preferred_element_type=F32)


def _split2(a):
    hi = a.astype(BF16)
    lo = (a - hi.astype(F32)).astype(BF16)
    return hi, lo


def _dot3(a, b, dims):
    ah, al = _split2(a)
    bh, bl = _split2(b)
    d = functools.partial(lax.dot_general, dimension_numbers=dims, preferred_element_type=F32)
    return d(ah, bh) + (d(ah, bl) + d(al, bh))


def _sel_dot_impl(sel01, x, dims):
    sel = sel01.astype(BF16)
    h1 = x.astype(BF16)
    r1 = x - h1.astype(F32)
    h2 = r1.astype(BF16)
    h3 = (r1 - h2.astype(F32)).astype(BF16)
    d = functools.partial(lax.dot_general, dimension_numbers=dims, preferred_element_type=F32)
    return d(sel, h1) + (d(sel, h2) + d(sel, h3))


@jax.custom_vjp
def _sel_dot(sel01, x):
    return _sel_dot_impl(sel01, x, NN)


_sel_dot.defvjp(lambda s, x: (_sel_dot(s, x), s),
                lambda s, g: (jnp.zeros_like(s), _sel_dot_impl(s, g, TN)))


def _make_mm(dotfn):
    @jax.custom_vjp
    def nn(a, b):
        return dotfn(a, b, NN)

    @jax.custom_vjp
    def nt(a, b):
        return dotfn(a, b, NT)

    @jax.custom_vjp
    def tn(a, b):
        return dotfn(a, b, TN)

    nn.defvjp(lambda a, b: (nn(a, b), (a, b)), lambda r, g: (nt(g, r[1]), tn(r[0], g)))
    nt.defvjp(lambda a, b: (nt(a, b), (a, b)), lambda r, g: (nn(g, r[1]), tn(g, r[0])))
    tn.defvjp(lambda a, b: (tn(a, b), (a, b)), lambda r, g: (nt(r[1], g), nn(r[0], g)))
    return nn, nt, tn


mm_nn, mm_nt, mm_tn = _make_mm(_dot)
mm3_nn, mm3_nt, mm3_tn = _make_mm(_dot3)


def _sigmoid(x):
    return jax.nn.sigmoid(x)


def _silu(x):
    return x * _sigmoid(x)


def _softplus_parts(x):
    sp = jnp.log1p(jnp.exp(-jnp.abs(x)))
    return jnp.maximum(x, 0.0) + sp, jnp.maximum(-x, 0.0) + sp


def _rmsnorm(x, g):
    return x * lax.rsqrt(jnp.mean(x * x, axis=-1, keepdims=True) + NORM_EPS) * g


def _iota2(shape, dim):
    return lax.broadcasted_iota(jnp.int32, shape, dim)


def _div64(i):
    return lax.shift_right_logical(i, jnp.full(i.shape, 6, jnp.int32))


def _each(f, *lists):
    return [f(*a) for a in zip(*lists)]


@jax.custom_vjp
def _inv_unit_lower(ms):
    n = ms[0].shape[0]
    eye = (_iota2((n, n), 0) == _iota2((n, n), 1)).astype(F32)
    rs = [eye - m for m in ms]
    ps = ms
    for _ in range(5):
        ps = _each(mm3_nn, ps, ps)
        rs = _each(lambda r, p: r + mm_nn(r, p), rs, ps)
    return rs


def _inv_fwd(ms):
    rs = _inv_unit_lower(ms)
    return rs, rs


def _inv_bwd(rs, gs):
    ts = _each(mm_tn, rs, gs)
    return (_each(lambda t, r: -mm_nt(t, r), ts, rs),)


_inv_unit_lower.defvjp(_inv_fwd, _inv_bwd)


def _dn_block(cq, ck, cv, bcol, acol, zt, alog, dtb, gn, s0):
    n = SUPER
    h = DN_CHUNK
    row = _iota2((n, n), 0)
    col = _iota2((n, n), 1)
    same = _div64(row) == _div64(col)
    incl = jnp.logical_and(same, row >= col)
    strict = jnp.logical_and(same, row > col)
    incl_f = incl.astype(F32)

    qn = _each(lambda x: x * lax.rsqrt(jnp.sum(x * x, axis=-1, keepdims=True) + NORM_EPS) * (D_HEAD ** -0.5), cq)
    kn = _each(lambda x: x * lax.rsqrt(jnp.sum(x * x, axis=-1, keepdims=True) + NORM_EPS), ck)
    beta = _each(_sigmoid, bcol)
    g = _each(lambda al, ac, dt: -(jnp.exp(al) * _softplus_parts(ac + dt)[0]), alog, acol, dtb)
    gcum = _each(lambda x: _sel_dot(incl_f, jnp.broadcast_to(x, (n, n))), g)
    gam_incl = _each(lambda x: jnp.where(incl, jnp.exp(jnp.where(incl, x - x.T, 0.0)), 0.0), gcum)
    kk = _each(mm_nt, kn, kn)
    t_inv = _inv_unit_lower(_each(lambda b, x, gm: b * x * jnp.where(strict, gm, 0.0), beta, kk, gam_incl))
    eg = _each(jnp.exp, gcum)
    u = _each(lambda t, v, b: mm_nn(t, v * b), t_inv, cv, beta)
    w = _each(lambda t, k, b, e: mm_nn(t, k * (b * e)), t_inv, kn, beta, eg)
    a_intra = _each(lambda q, k, gm: mm_nt(q, k) * gm, qn, kn, gam_incl)
    q_dec = _each(lambda q, e: q * e, qn, eg)
    last0 = _each(lambda x: x[h - 1:h, :], gcum)
    last1 = _each(lambda x: x[n - 1:n, :], gcum)
    k_dec = _each(lambda k, x, l0, l1: k * jnp.exp(jnp.concatenate(
        [jnp.broadcast_to(l0, (h, n)), jnp.broadcast_to(l1, (h, n))], axis=0) - x), kn, gcum, last0, last1)
    v0 = _each(lambda uu, ww, s: uu[:h] - mm_nn(ww[:h], s), u, w, s0)
    o0 = _each(lambda q, s: mm_nn(q[:h], s), q_dec, s0)
    s1 = _each(lambda s, l0, k, v: s * jnp.exp(l0) + mm_tn(k[:h], v), s0, last0, k_dec, v0)
    v1 = _each(lambda uu, ww, s: uu[h:] - mm_nn(ww[h:], s), u, w, s1)
    o1 = _each(lambda q, s: mm_nn(q[h:], s), q_dec, s1)
    s2 = _each(lambda s, l1, k, v: s * jnp.exp(l1) + mm_tn(k[h:], v), s1, last1, k_dec, v1)
    o = _each(lambda a, b, am, x, y: jnp.concatenate([a, b], axis=0) + mm_nn(am, jnp.concatenate([x, y], axis=0)),
              o0, o1, a_intra, v0, v1)
    out = _each(lambda x, z: _rmsnorm(x, gn) * _silu(z), o, zt)
    return out, s2


def _mem_fn(mq, mz, mkv):
    mk = mkv[:, :MEM_W]
    mv = mkv[:, MEM_W:]
    lane = _iota2((1, MEM_W), 1)
    out = jnp.zeros(mq.shape, F32)
    for hd in range(MEM_HEADS):
        hm = (_div64(lane) == hd).astype(F32)
        s = mm_nt(mq * hm, mk) * (1.0 / math.sqrt(MEM_DH))
        s = s - jnp.max(s, axis=-1, keepdims=True)
        e = jnp.exp(s)
        p = e / jnp.sum(e, axis=-1, keepdims=True)
        out = out + mm_nn(p, mv) * hm
    return out * _silu(mz)


def _loss_fn(x, mo, fg, tgt):
    y = _rmsnorm(x + mo, fg)
    err = y - tgt
    return 0.5 * jnp.sum(jnp.mean(err * err, axis=-1, keepdims=True), axis=0, keepdims=True)


def _matmul_tn(a, b, out_dtype, tm, tn, tk, name):
    kdim, m = a.shape
    n = b.shape[1]
    tm, tn, tk = min(tm, m), min(tn, n), min(tk, kdim)
    assert m % tm == 0 and n % tn == 0 and kdim % tk == 0
    nk = kdim // tk

    def body(a_ref, b_ref, o_ref, acc_ref):
        k = pl.program_id(2)
        part = _dot(a_ref[...], b_ref[...], TN)

        @pl.when(k == 0)
        def _():
            acc_ref[...] = part

        @pl.when(k > 0)
        def _():
            acc_ref[...] += part

        @pl.when(k == nk - 1)
        def _():
            o_ref[...] = acc_ref[...].astype(o_ref.dtype)

    return pl.pallas_call(
        body,
        name=name,
        grid=(m // tm, n // tn, nk),
        in_specs=[pl.BlockSpec((tk, tm), lambda i, j, k: (k, i)), pl.BlockSpec((tk, tn), lambda i, j, k: (k, j))],
        out_specs=pl.BlockSpec((tm, tn), lambda i, j, k: (i, j)),
        out_shape=jax.ShapeDtypeStruct((m, n), out_dtype),
        scratch_shapes=[pltpu.VMEM((tm, tn), F32)],
        compiler_params=pltpu.CompilerParams(dimension_semantics=("parallel", "parallel", "arbitrary")),
    )(a, b)


def _norm_in(x, g, tm=256):
    t = x.shape[0]

    def body(x_ref, g_ref, h_ref):
        h_ref[...] = _rmsnorm(x_ref[...], g_ref[...]).astype(BF16)

    return pl.pallas_call(
        body,
        name="norm_in",
        grid=(t // tm,),
        in_specs=[pl.BlockSpec((tm, D_MODEL), lambda i: (i, 0)), pl.BlockSpec((1, D_MODEL), lambda i: (0, 0))],
        out_specs=pl.BlockSpec((tm, D_MODEL), lambda i: (i, 0)),
        out_shape=jax.ShapeDtypeStruct((t, D_MODEL), BF16),
    )(x, g)


def _grad_x(dproj, w_alt, x, g, dres, dw_al, tm=512, tk=3968):
    t, kdim = dproj.shape
    tm = min(tm, t)
    assert kdim % tk == 0 and t % tm == 0
    nk = kdim // tk
    ni = t // tm

    def body(a_ref, b_ref, x_ref, g_ref, dres_ref, dw_ref, dx_ref, dg_ref, land_ref, acc_ref, send_sems, recv_sems):
        i, k = pl.program_id(0), pl.program_id(1)
        px, py, pc = _position()

        def exchange(b):
            return [pltpu.make_async_remote_copy(
                src_ref=_window_view(dw_ref, 2 * n + 1 - b), dst_ref=land_ref.at[n], send_sem=send_sems.at[n],
                recv_sem=recv_sems.at[n], device_id=(px, py, 1 - pc), device_id_type=MESH)
                for n in range(N_DEV // 2)]

        for b in (0, 1):
            @pl.when(jnp.logical_and(jnp.logical_and(i == 0, k == 0), pc == b))
            def _(b=b):
                for cp in exchange(b):
                    cp.start()

        part = _dot(a_ref[...], b_ref[...], NN)

        @pl.when(k == 0)
        def _():
            acc_ref[...] = part

        @pl.when(k > 0)
        def _():
            acc_ref[...] += part

        @pl.when(jnp.logical_and(i == 0, k == 0))
        def _():
            dg_ref[...] = jnp.zeros_like(dg_ref)

        @pl.when(k == nk - 1)
        def _():
            _, vjp = jax.vjp(_rmsnorm, x_ref[...], g_ref[...])
            dx, dg = vjp(acc_ref[...])
            dx_ref[...] = dx + dres_ref[...]
            dg_ref[...] += dg

        for b in (0, 1):
            @pl.when(jnp.logical_and(jnp.logical_and(i == ni - 1, k == nk - 1), pc == b))
            def _(b=b):
                for cp in exchange(b):
                    cp.wait()

    row = pl.BlockSpec((tm, D_MODEL), lambda i, k: (i, 0))
    vec = pl.BlockSpec((1, D_MODEL), lambda i, k: (0, 0))
    n_win = N_DEV // 2
    return pl.pallas_call(
        body,
        name="grad_x",
        grid=(ni, nk),
        in_specs=[pl.BlockSpec((tm, tk), lambda i, k: (i, k)), pl.BlockSpec((tk, D_MODEL), lambda i, k: (k, 0)),
                  row, vec, row, ANY],
        out_specs=[row, vec, ANY],
        out_shape=[jax.ShapeDtypeStruct((t, D_MODEL), F32), jax.ShapeDtypeStruct((1, D_MODEL), F32),
                   jax.ShapeDtypeStruct((n_win, WIN_W, dw_al.shape[1]), dw_al.dtype)],
        scratch_shapes=[pltpu.VMEM((tm, D_MODEL), F32), pltpu.SemaphoreType.DMA((n_win,)),
                        pltpu.SemaphoreType.DMA((n_win,))],
    )(dproj, w_alt, x, g, dres, dw_al)


def _block_tail(proj, o_dn, o_sb, o_m, x, tgt, w_br_dn, w_br_sb, w_br_mem, w_out, fg, tm=256):
    t = x.shape[0]
    tm = min(tm, t)
    gw = 512
    n_g = 3 * D_MODEL // gw

    def body(*refs):
        g_refs = refs[:n_g]
        (odn_ref, osb_ref, om_ref, x_ref, t_ref, wdn_ref, wsb_ref, wm_ref, wo_ref, fg_ref, loss_ref, dout_ref, dfg_ref,
         mg_ref, dyd_ref, dys_ref, dym_ref, dg_ref, dod_ref, dos_ref, dom_ref) = refs[n_g:]
        y = [_dot(odn_ref[...], wdn_ref[...], NN), _dot(osb_ref[...], wsb_ref[...], NN),
             _dot(om_ref[...], wm_ref[...], NN)]
        s = [_sigmoid(jnp.concatenate([g_refs[2 * k][...], g_refs[2 * k + 1][...]], axis=1)) for k in range(3)]
        merged16 = (s[0] * y[0] + s[1] * y[1] + s[2] * y[2]).astype(BF16)
        mg_ref[...] = merged16
        mo = _dot(merged16, wo_ref[...], NN)
        loss, vjp = jax.vjp(_loss_fn, x_ref[...], mo, fg_ref[...], t_ref[...])
        _, dout, dfg, _ = vjp(jnp.ones((1, 1), F32))

        @pl.when(pl.program_id(0) == 0)
        def _():
            loss_ref[...] = jnp.zeros_like(loss_ref)
            dfg_ref[...] = jnp.zeros_like(dfg_ref)

        loss_ref[...] += jnp.broadcast_to(loss, loss_ref.shape)
        dfg_ref[...] += dfg
        dout_ref[...] = dout
        dmerged = _dot(dout, wo_ref[...], NT)
        dy = [(sk * dmerged).astype(BF16) for sk in s]
        dyd_ref[...], dys_ref[...], dym_ref[...] = dy
        dg_ref[...] = jnp.concatenate([dmerged * yk * (sk * (1.0 - sk)) for yk, sk in zip(y, s)], axis=1).astype(BF16)
        dod_ref[...] = _dot(dy[0], wdn_ref[...], NT).astype(BF16)
        dos_ref[...] = _dot(dy[1], wsb_ref[...], NT).astype(BF16)
        dom_ref[...] = _dot(dy[2], wm_ref[...], NT).astype(BF16)

    gates = [pl.BlockSpec((tm, gw), lambda i, j=j: (i, O_GATES // gw + j)) for j in range(n_g)]
    row = pl.BlockSpec((tm, D_MODEL), lambda i: (i, 0))
    rowm = pl.BlockSpec((tm, MEM_W), lambda i: (i, 0))
    vec = pl.BlockSpec((1, D_MODEL), lambda i: (0, 0))

    def whole(a):
        return pl.BlockSpec(a.shape, lambda i: (0, 0), pipeline_mode=pl.Buffered(1))

    def bf(c):
        return jax.ShapeDtypeStruct((t, c), BF16)

    return pl.pallas_call(
        body,
        name="block_tail",
        grid=(t // tm,),
        in_specs=gates + [row, row, rowm, row, row, whole(w_br_dn), whole(w_br_sb), whole(w_br_mem), whole(w_out), vec],
        out_specs=[pl.BlockSpec((1, LANE), lambda i: (0, 0)), row, vec, row, row, row, row,
                   pl.BlockSpec((tm, 3 * D_MODEL), lambda i: (i, 0)), row, row, rowm],
        out_shape=[jax.ShapeDtypeStruct((1, LANE), F32), jax.ShapeDtypeStruct((t, D_MODEL), F32),
                   jax.ShapeDtypeStruct((1, D_MODEL), F32), bf(D_MODEL), bf(D_MODEL), bf(D_MODEL), bf(D_MODEL),
                   bf(3 * D_MODEL), bf(D_MODEL), bf(D_MODEL), bf(MEM_W)],
    )(*([proj] * n_g), o_dn, o_sb, o_m, x, tgt, w_br_dn, w_br_sb, w_br_mem, w_out, fg)


def _shift_rows(x, s):
    t = x.shape[0]
    if s == 0:
        return x
    rolled = pltpu.roll(x, s % t, 0)
    row = _iota2(x.shape, 0)
    keep = row >= s if s > 0 else row < t + s
    return jnp.where(keep, rolled, 0.0)


def _conv_pre(x, w):
    return sum(_shift_rows(x, CONV_K - 1 - j) * w[j:j + 1, :] for j in range(CONV_K))


CONV_TC = 256


def _dn_conv(proj, conv_w):
    t = proj.shape[0]
    nb = 3 * D_MODEL // CONV_TC

    def body(x_ref, w_ref, c_ref):
        c_ref[...] = _silu(_conv_pre(x_ref[...], w_ref[...]))

    return pl.pallas_call(
        body,
        name="dn_conv",
        grid=(nb,),
        in_specs=[pl.BlockSpec((t, CONV_TC), lambda j: (0, j)), pl.BlockSpec((CONV_K, CONV_TC), lambda j: (0, j))],
        out_specs=pl.BlockSpec((t, CONV_TC), lambda j: (0, j)),
        out_shape=jax.ShapeDtypeStruct((t, 3 * D_MODEL), F32),
    )(proj, conv_w)


def _dn_conv_bwd(proj, conv_w, dc, part):
    t = proj.shape[0]
    nb = D_MODEL // CONV_TC
    b0 = part * nb

    def body(x_ref, w_ref, dc_ref, dx_ref, dw_ref):
        x = x_ref[...]
        w = w_ref[...]
        pre = _conv_pre(x, w)
        sg = _sigmoid(pre)
        dpre = dc_ref[...] * (sg * (1.0 + pre * (1.0 - sg)))
        ahead = [_shift_rows(dpre, -(CONV_K - 1 - j)) for j in range(CONV_K)]
        dx_ref[...] = sum(a * w[j:j + 1, :] for j, a in enumerate(ahead)).astype(BF16)
        dw_ref[...] = jnp.concatenate([jnp.sum(a * x, axis=0, keepdims=True) for a in ahead], axis=0)

    blk = pl.BlockSpec((t, CONV_TC), lambda j: (0, j))
    return pl.pallas_call(
        body,
        name=f"dn_conv_bwd{part}",
        grid=(nb,),
        in_specs=[pl.BlockSpec((t, CONV_TC), lambda j: (0, b0 + j)),
                  pl.BlockSpec((CONV_K, CONV_TC), lambda j: (0, b0 + j)), blk],
        out_specs=[blk, pl.BlockSpec((CONV_K, CONV_TC), lambda j: (0, j))],
        out_shape=[jax.ShapeDtypeStruct((t, D_MODEL), BF16), jax.ShapeDtypeStruct((CONV_K, D_MODEL), F32)],
    )(proj, conv_w, dc)


def _ba_columns(ba, hd):
    lane = _iota2(ba.shape, 1)
    bcol = jnp.sum(jnp.where(lane == hd, ba, 0.0), axis=1, keepdims=True)
    acol = jnp.sum(jnp.where(lane == N_HEADS + hd, ba, 0.0), axis=1, keepdims=True)
    return bcol, acol


def _head_scalar(row, hd):
    lane = _iota2(row.shape, 1)
    return jnp.sum(jnp.where(lane == hd, row, 0.0), axis=1, keepdims=True)


DN_HP = 8


def _dn_inputs(cq, ck, cv, ba_ref, z_ref, alog_ref, dtb_ref, heads, lanes):
    ba = ba_ref[...]
    cols = [_ba_columns(ba, hd) for hd in heads]
    return ([cq[:, ln] for ln in lanes], [ck[:, ln] for ln in lanes], [cv[:, ln] for ln in lanes],
            [c[0] for c in cols], [c[1] for c in cols], [z_ref[:, ln] for ln in lanes],
            [_head_scalar(alog_ref[...], hd) for hd in heads], [_head_scalar(dtb_ref[...], hd) for hd in heads])


def _dn_specs(nblk, reverse):
    w = DN_HP * LANE
    nq = D_MODEL // w

    def row(i):
        return nblk - 1 - i if reverse else i

    def colblk(b0):
        return pl.BlockSpec((SUPER, w), lambda i, h: (row(i), b0 + h))

    ba = pl.BlockSpec((SUPER, LANE), lambda i, h: (row(i), O_BA // LANE))
    vec = pl.BlockSpec((1, LANE), lambda i, h: (0, 0))
    st = pl.BlockSpec((1, DN_HP, D_HEAD, D_HEAD), lambda i, h: (row(i), h, 0, 0))
    return colblk, nq, ba, vec, st


def _dn_fwd(c, proj, alog_row, dtb_row, gn):
    t = c.shape[0]
    nblk = t // SUPER
    colblk, nq, ba, vec, st = _dn_specs(nblk, False)

    def body(cq, ck, cv, ba_ref, z_ref, alog_ref, dtb_ref, gn_ref, o_ref, s_ref, state):
        @pl.when(jnp.logical_and(pl.program_id(0) == 0, pl.program_id(1) == 0))
        def _():
            state[...] = jnp.zeros_like(state)

        heads = [pl.program_id(1) * DN_HP + j for j in range(DN_HP)]
        lanes = [slice(j * LANE, (j + 1) * LANE) for j in range(DN_HP)]
        s0 = [state[hd] for hd in heads]
        outs, s2 = _dn_block(*_dn_inputs(cq, ck, cv, ba_ref, z_ref, alog_ref, dtb_ref, heads, lanes), gn_ref[...], s0)
        for j, (hd, ln) in enumerate(zip(heads, lanes)):
            s_ref[0, j] = s0[j]
            o_ref[:, ln] = outs[j].astype(BF16)
            state[hd] = s2[j]

    return pl.pallas_call(
        body,
        name="dn_fwd",
        grid=(nblk, N_HEADS // DN_HP),
        in_specs=[colblk(0), colblk(nq), colblk(2 * nq), ba, colblk(O_Z_DN // (DN_HP * LANE)), vec, vec, vec],
        out_specs=[colblk(0), st],
        out_shape=[jax.ShapeDtypeStruct((t, D_MODEL), BF16),
                   jax.ShapeDtypeStruct((nblk, N_HEADS, D_HEAD, D_HEAD), F32)],
        scratch_shapes=[pltpu.VMEM((N_HEADS, D_HEAD, D_HEAD), F32)],
    )(c, c, c, proj, proj, alog_row, dtb_row, gn)


def _dn_bwd(c, proj, alog_row, dtb_row, gn, states, do):
    t = c.shape[0]
    nblk = t // SUPER
    colblk, nq, ba, vec, st = _dn_specs(nblk, True)

    def body(cq, ck, cv, ba_ref, z_ref, alog_ref, dtb_ref, gn_ref, s_ref, do_ref,
             dq_ref, dk_ref, dv_ref, dz_ref, dba_ref, dsc_ref, dgn_ref, dstate):
        i = pl.program_id(0)
        hq = pl.program_id(1)

        @pl.when(jnp.logical_and(i == 0, hq == 0))
        def _():
            dstate[...] = jnp.zeros_like(dstate)
            dsc_ref[...] = jnp.zeros_like(dsc_ref)
            dgn_ref[...] = jnp.zeros_like(dgn_ref)

        @pl.when(hq == 0)
        def _():
            dba_ref[...] = jnp.zeros_like(dba_ref)

        lane = _iota2((SUPER, LANE), 1)
        lane1 = _iota2((1, LANE), 1)
        heads = [hq * DN_HP + j for j in range(DN_HP)]
        lanes = [slice(j * LANE, (j + 1) * LANE) for j in range(DN_HP)]
        ds_in = [dstate[hd] for hd in heads]
        s_in = [s_ref[0, j] for j in range(DN_HP)]
        _, vjp = jax.vjp(_dn_block, *_dn_inputs(cq, ck, cv, ba_ref, z_ref, alog_ref, dtb_ref, heads, lanes),
                         gn_ref[...], s_in)
        dq, dk, dv, dbc, dac, dz, dal, ddt, dgn, ds0 = vjp(([do_ref[:, ln].astype(F32) for ln in lanes], ds_in))
        dba = jnp.zeros((SUPER, LANE), F32)
        dal_row = jnp.zeros((1, LANE), F32)
        ddt_row = jnp.zeros((1, LANE), F32)
        for j, (hd, ln) in enumerate(zip(heads, lanes)):
            dq_ref[:, ln] = dq[j]
            dk_ref[:, ln] = dk[j]
            dv_ref[:, ln] = dv[j]
            dz_ref[:, ln] = dz[j].astype(BF16)
            dstate[hd] = ds0[j]
            dba = dba + jnp.where(lane == hd, dbc[j], 0.0) + jnp.where(lane == N_HEADS + hd, dac[j], 0.0)
            dal_row = dal_row + jnp.where(lane1 == hd, dal[j], 0.0)
            ddt_row = ddt_row + jnp.where(lane1 == hd, ddt[j], 0.0)
        dba_ref[...] += dba
        dsc_ref[0:1, :] += dal_row
        dsc_ref[1:2, :] += ddt_row
        dgn_ref[...] += dgn

    outs = pl.pallas_call(
        body,
        name="dn_bwd",
        grid=(nblk, N_HEADS // DN_HP),
        in_specs=[colblk(0), colblk(nq), colblk(2 * nq), ba, colblk(O_Z_DN // (DN_HP * LANE)), vec, vec, vec, st,
                  colblk(0)],
        out_specs=[colblk(0), colblk(0), colblk(0), colblk(0),
                   pl.BlockSpec((SUPER, LANE), lambda i, h: (nblk - 1 - i, 0)),
                   pl.BlockSpec((2, LANE), lambda i, h: (0, 0)), vec],
        out_shape=[jax.ShapeDtypeStruct((t, D_MODEL), F32)] * 3
        + [jax.ShapeDtypeStruct((t, D_MODEL), BF16), jax.ShapeDtypeStruct((t, LANE), F32),
           jax.ShapeDtypeStruct((2, LANE), F32), jax.ShapeDtypeStruct((1, LANE), F32)],
        scratch_shapes=[pltpu.VMEM((N_HEADS, D_HEAD, D_HEAD), F32)],
    )(c, c, c, proj, proj, alog_row, dtb_row, gn, states, do)
    return outs


SB_TQ = 256
SB_TK = 256
SB_HP_FWD = 8
SB_HP_BWD = 4


def _sb_logits(z, mask):
    sp = jnp.log(1.0 + jnp.exp(-jnp.abs(z)))
    lf_raw = -(jnp.maximum(z, 0.0) + sp)
    lb = lf_raw + z
    lf = lf_raw if mask is None else jnp.where(mask, lf_raw, 0.0)
    return lb, lf_raw, lf


def _suffix_sums(x, sel):
    hi, lo = _split2(x)
    d = functools.partial(lax.dot_general, dimension_numbers=NN, preferred_element_type=F32)
    return d(hi, sel) + d(lo, sel)


def _sb_diag_mask(tq, r):
    return r * SB_TK + _iota2((tq, SB_TK), 1) < _iota2((tq, SB_TK), 0)


def _sb_specs(t, tq, hp):
    w = hp * LANE
    q0, k0, v0, z0 = (O_QKV_SB // w, (O_QKV_SB + D_MODEL) // w, (O_QKV_SB + 2 * D_MODEL) // w, O_Z_SB // w)

    def blk(b0):
        return pl.BlockSpec((tq, w), lambda h, i: (i, b0 + h))

    def full(b0, **kw):
        return pl.BlockSpec((t, w), lambda h, i: (0, b0 + h), **kw)

    once = dict(pipeline_mode=pl.Buffered(1))
    return blk(q0), full(k0, **once), full(v0, **once), blk(z0), blk(0), full(0)


def _sb_fwd(proj, shards):
    t = proj.shape[0]
    tq = min(SB_TQ, t)
    ndiag = tq // SB_TK
    scale = 1.0 / math.sqrt(D_HEAD)
    na = len(shards)

    def body(q_ref, k_ref, v_ref, z_ref, *rest):
        x_refs, (o_ref, oraw_ref), land = rest[:na], rest[na:na + 2], rest[na + 2:2 * na + 2]
        sems = rest[2 * na + 2:]
        qi = pl.program_id(1)
        first = jnp.logical_and(pl.program_id(0) == 0, qi == 0)
        last = jnp.logical_and(pl.program_id(0) == pl.num_programs(0) - 1, qi == pl.num_programs(1) - 1)

        @pl.when(first)
        def _():
            for cp in _direct_gather_copies(x_refs, land, *sems):
                cp.start()

        lanes = [slice(hd * LANE, (hd + 1) * LANE) for hd in range(SB_HP_FWD)]
        qs = [(q_ref[:, ln] * scale).astype(BF16) for ln in lanes]
        after = (_iota2((SB_TK, SB_TK), 0) > _iota2((SB_TK, SB_TK), 1)).astype(BF16)
        oraw_ref[...] = jnp.zeros_like(oraw_ref)

        def block(kb, mask, c_lf):
            rows = pl.ds(pl.multiple_of(kb * SB_TK, SB_TK), SB_TK)
            z = _each(lambda q, ln: _dot(q, k_ref[rows, ln], NT), qs, lanes)
            lg = _each(lambda x: _sb_logits(x, mask), z)
            surv = _each(lambda x: _suffix_sums(x[2], after), lg)
            att = _each(lambda x, s, c: jnp.exp(x[0] + s + c), lg, surv, c_lf)
            if mask is not None:
                att = _each(lambda a: jnp.where(mask, a, 0.0), att)
            pv = _each(lambda a, ln: _dot(a, v_ref[rows, ln], NN), att, lanes)
            for p, ln in zip(pv, lanes):
                oraw_ref[:, ln] += p
            return tuple(_each(lambda c, x: c + jnp.sum(x[2], axis=1, keepdims=True), c_lf, lg))

        carry = tuple(jnp.zeros((tq, 1), F32) for _ in range(SB_HP_FWD))
        for r in reversed(range(ndiag)):
            carry = block(qi * ndiag + r, _sb_diag_mask(tq, r), carry)
        lax.fori_loop(0, qi * ndiag, lambda i, c: block(qi * ndiag - 1 - i, None, c), carry)
        o_ref[...] = (oraw_ref[...] * _silu(z_ref[...])).astype(BF16)

        @pl.when(last)
        def _():
            for cp in _direct_gather_copies(x_refs, land, *sems):
                cp.wait()

    q_spec, k_spec, v_spec, z_spec, out, _ = _sb_specs(t, tq, SB_HP_FWD)
    outs = pl.pallas_call(
        body,
        name="sb_fwd",
        grid=(N_HEADS // SB_HP_FWD, t // tq),
        in_specs=[q_spec, k_spec, v_spec, z_spec] + [ANY] * na,
        out_specs=[out, out] + [ANY] * na,
        out_shape=[jax.ShapeDtypeStruct((t, D_MODEL), BF16), jax.ShapeDtypeStruct((t, D_MODEL), F32)]
        + [jax.ShapeDtypeStruct((N_DEV, *v.shape), v.dtype) for v in shards],
        scratch_shapes=_gather_sems(na),
    )(proj, proj, proj, proj, *shards)
    return outs[0], outs[1], outs[2:]


def _sb_bwd(proj, oraw, do, blocks):
    t = proj.shape[0]
    tq = min(SB_TQ, t)
    ndiag = tq // SB_TK
    scale = 1.0 / math.sqrt(D_HEAD)

    def body(q_ref, k_ref, v_ref, z_ref, oraw_ref, do_ref, blk_ref, dq_ref, dk_ref, dv_ref, dz_ref, land_ref,
             dk_acc, dv_acc, p_scr, z_scr, send_sems, recv_sems, local_sem):
        qi = pl.program_id(1)
        nq = pl.num_programs(1)
        hg = pl.program_id(0)
        me = _position()
        mine = 4 * me[0] + 2 * me[1] + me[2]

        def exchange():
            cps = [pltpu.make_async_copy(blk_ref.at[mine], land_ref.at[mine], local_sem)]
            for k, peer in enumerate(_other_devices(me)):
                cps.append(pltpu.make_async_remote_copy(
                    src_ref=blk_ref.at[4 * peer[0] + 2 * peer[1] + peer[2]], dst_ref=land_ref.at[mine],
                    send_sem=send_sems.at[k], recv_sem=recv_sems.at[k], device_id=peer, device_id_type=MESH))
            return cps

        @pl.when(jnp.logical_and(hg == 0, qi == 0))
        def _():
            for cp in exchange():
                cp.start()

        @pl.when(qi == 0)
        def _():
            dk_acc[...] = jnp.zeros_like(dk_acc)
            dv_acc[...] = jnp.zeros_like(dv_acc)

        heads = range(SB_HP_BWD)
        lanes = [slice(hd * LANE, (hd + 1) * LANE) for hd in heads]
        zg = z_ref[...]
        sg = _sigmoid(zg)
        dog = do_ref[...].astype(F32)
        dz_ref[...] = (dog * oraw_ref[...] * (sg * (1.0 + zg * (1.0 - sg)))).astype(BF16)
        d_o = (dog * (zg * sg)).astype(BF16)
        d_o16 = [d_o[:, ln] for ln in lanes]
        qs = [(q_ref[:, ln] * scale).astype(BF16) for ln in lanes]
        ri = _iota2((SB_TK, SB_TK), 0)
        ci = _iota2((SB_TK, SB_TK), 1)
        after = (ri > ci).astype(BF16)
        earlier = (ri < ci).astype(BF16)

        def rows_of(kb):
            return pl.ds(pl.multiple_of(kb * SB_TK, SB_TK), SB_TK)

        def down(kb, mask, c_lf):
            rows = rows_of(kb)
            z = _each(lambda q, ln: _dot(q, k_ref[rows, ln], NT), qs, lanes)
            da = _each(lambda d, ln: _dot(d, v_ref[rows, ln], NT), d_o16, lanes)
            lg = _each(lambda x: _sb_logits(x, mask), z)
            surv = _each(lambda x: _suffix_sums(x[2], after), lg)
            att = _each(lambda x, s, c: jnp.exp(x[0] + s + c), lg, surv, c_lf)
            if mask is not None:
                att = _each(lambda a: jnp.where(mask, a, 0.0), att)
            dv = _each(lambda a, d: _dot(a, d, TN), att, d_o16)
            for hd in heads:
                p_scr[hd, kb] = att[hd] * da[hd]
                z_scr[hd, kb] = z[hd]
                dv_acc[rows, lanes[hd]] += dv[hd]
            return tuple(_each(lambda c, x: c + jnp.sum(x[2], axis=1, keepdims=True), c_lf, lg))

        c_lf = tuple(jnp.zeros((tq, 1), F32) for _ in heads)
        for r in reversed(range(ndiag)):
            c_lf = down(qi * ndiag + r, _sb_diag_mask(tq, r), c_lf)
        lax.fori_loop(0, qi * ndiag, lambda i, c: down(qi * ndiag - 1 - i, None, c), c_lf)

        def up(kb, mask, carry):
            dq, c_p = carry
            rows = rows_of(kb)
            p = [p_scr[hd, kb] for hd in heads]
            zs = [z_scr[hd, kb] for hd in heads]
            before = _each(lambda x, c: _suffix_sums(x, earlier) + c, p, c_p)
            e = _each(lambda x: jnp.exp(-jnp.abs(x)), zs)
            r = _each(lambda x: 1.0 / (1.0 + x), e)
            sig = _each(lambda x, a, b: jnp.where(x >= 0.0, b, a * b), zs, e, r)
            oms = _each(lambda x, a, b: jnp.where(x >= 0.0, a * b, b), zs, e, r)
            if mask is not None:
                sig = _each(lambda a: jnp.where(mask, a, 0.0), sig)
            dzz = _each(lambda x, o, g, b: x * o - g * b, p, oms, sig, before)
            dk = _each(lambda x, q: _dot(x, q, TN), dzz, qs)
            dq = _each(lambda a, x, ln: a + _dot(x, k_ref[rows, ln], NN), dq, dzz, lanes)
            for hd in heads:
                dk_acc[rows, lanes[hd]] += dk[hd]
            return tuple(dq), tuple(_each(lambda c, x: c + jnp.sum(x, axis=1, keepdims=True), c_p, p))

        carry = (tuple(jnp.zeros((tq, D_HEAD), F32) for _ in heads), tuple(jnp.zeros((tq, 1), F32) for _ in heads))
        carry = lax.fori_loop(0, qi * ndiag, lambda kb, c: up(kb, None, c), carry)
        for r in range(ndiag):
            carry = up(qi * ndiag + r, _sb_diag_mask(tq, r), carry)
        dq = carry[0]
        for hd in heads:
            dq_ref[:, lanes[hd]] = (dq[hd] * scale).astype(BF16)

        @pl.when(qi == nq - 1)
        def _():
            dk_ref[...] = dk_acc[...].astype(BF16)
            dv_ref[...] = dv_acc[...].astype(BF16)

        @pl.when(jnp.logical_and(hg == pl.num_programs(0) - 1, qi == nq - 1))
        def _():
            for cp in exchange():
                cp.wait()

    q_spec, k_spec, v_spec, z_spec, blk, full = _sb_specs(t, tq, SB_HP_BWD)
    o = jax.ShapeDtypeStruct((t, D_MODEL), BF16)
    w = SB_HP_BWD * LANE
    return pl.pallas_call(
        body,
        name="sb_bwd",
        grid=(N_HEADS // SB_HP_BWD, t // tq),
        in_specs=[q_spec, k_spec, v_spec, z_spec, blk, blk, ANY],
        out_specs=[blk, full, full, blk, ANY],
        out_shape=[o, o, o, o, jax.ShapeDtypeStruct(blocks.shape, blocks.dtype)],
        scratch_shapes=[pltpu.VMEM((t, w), F32), pltpu.VMEM((t, w), F32)]
        + [pltpu.VMEM((SB_HP_BWD, t // SB_TK, tq, SB_TK), F32)] * 2
        + [pltpu.SemaphoreType.DMA((N_DEV - 1,)), pltpu.SemaphoreType.DMA((N_DEV - 1,)), pltpu.SemaphoreType.DMA],
    )(proj, proj, proj, proj, oraw, do, blocks)


def _mem_kv_fn(mem, mg, w):
    return mm_nn(_rmsnorm(mem, mg), w)


def _mem_kv(mem, mg, w):
    def body(m_ref, g_ref, w_ref, o_ref):
        o_ref[...] = _mem_kv_fn(m_ref[...], g_ref[...], w_ref[...])

    return pl.pallas_call(body, name="mem_kv", out_shape=jax.ShapeDtypeStruct((MEM_LEN, 2 * MEM_W), F32))(mem, mg, w)


def _mem_kv_bwd(mem, mg, w, dmkv):
    def body(m_ref, g_ref, w_ref, d_ref, dg_ref, dw_ref):
        _, vjp = jax.vjp(_mem_kv_fn, m_ref[...], g_ref[...], w_ref[...].astype(F32))
        _, dg, dw = vjp(d_ref[...])
        dg_ref[...] = dg
        dw_ref[...] = dw.astype(BF16)

    return pl.pallas_call(
        body, name="mem_kv_bwd",
        out_shape=[jax.ShapeDtypeStruct((1, D_MODEL), F32), jax.ShapeDtypeStruct((D_MODEL, 2 * MEM_W), BF16)],
    )(mem, mg, w, dmkv)


def _mem_attn(proj, mkv, tm=256):
    t = proj.shape[0]
    tm = min(tm, t)

    def body(q_ref, z_ref, kv_ref, o_ref):
        o_ref[...] = _mem_fn(q_ref[...], z_ref[...], kv_ref[...]).astype(BF16)

    return pl.pallas_call(
        body,
        name="mem_attn",
        grid=(t // tm,),
        in_specs=[pl.BlockSpec((tm, MEM_W), lambda i: (i, O_MQ // MEM_W)),
                  pl.BlockSpec((tm, MEM_W), lambda i: (i, O_MZ // MEM_W)),
                  pl.BlockSpec((MEM_LEN, 2 * MEM_W), lambda i: (0, 0))],
        out_specs=pl.BlockSpec((tm, MEM_W), lambda i: (i, 0)),
        out_shape=jax.ShapeDtypeStruct((t, MEM_W), BF16),
    )(proj, proj, mkv)


def _mem_attn_bwd(proj, mkv, do, tm=256):
    t = proj.shape[0]
    tm = min(tm, t)

    def body(q_ref, z_ref, kv_ref, do_ref, dq_ref, dz_ref, dkv_ref):
        _, vjp = jax.vjp(_mem_fn, q_ref[...], z_ref[...], kv_ref[...])
        dq, dz, dkv = vjp(do_ref[...].astype(F32))
        dq_ref[...] = dq.astype(BF16)
        dz_ref[...] = dz.astype(BF16)

        @pl.when(pl.program_id(0) == 0)
        def _():
            dkv_ref[...] = jnp.zeros_like(dkv_ref)

        dkv_ref[...] += dkv

    blk = pl.BlockSpec((tm, MEM_W), lambda i: (i, 0))
    kv = pl.BlockSpec((MEM_LEN, 2 * MEM_W), lambda i: (0, 0))
    return pl.pallas_call(
        body,
        name="mem_attn_bwd",
        grid=(t // tm,),
        in_specs=[pl.BlockSpec((tm, MEM_W), lambda i: (i, O_MQ // MEM_W)),
                  pl.BlockSpec((tm, MEM_W), lambda i: (i, O_MZ // MEM_W)), kv, blk],
        out_specs=[blk, blk, kv],
        out_shape=[jax.ShapeDtypeStruct((t, MEM_W), BF16), jax.ShapeDtypeStruct((t, MEM_W), BF16),
                   jax.ShapeDtypeStruct((MEM_LEN, 2 * MEM_W), F32)],
    )(proj, proj, mkv, do)


def _proj_gather(h, w_alt, shards, tn=384):
    t = h.shape[0]
    n, kdim = w_alt.shape
    assert n % tn == 0
    nj = n // tn
    na = len(shards)

    def body(h_ref, w_ref, *rest):
        x_refs, o_ref, land = rest[:na], rest[na], rest[na + 1:2 * na + 1]
        send_sems, recv_sems, local_sems = rest[2 * na + 1:]
        j = pl.program_id(0)

        def copies():
            return _direct_gather_copies(x_refs, land, send_sems, recv_sems, local_sems)

        @pl.when(j == 0)
        def _():
            for cp in copies():
                cp.start()

        o_ref[...] = _dot(h_ref[...], w_ref[...], NT)

        @pl.when(j == nj - 1)
        def _():
            for cp in copies():
                cp.wait()

    outs = pl.pallas_call(
        body,
        name="proj",
        grid=(nj,),
        in_specs=[pl.BlockSpec((t, kdim), lambda j: (0, 0)), pl.BlockSpec((tn, kdim), lambda j: (j, 0))] + [ANY] * na,
        out_specs=[pl.BlockSpec((t, tn), lambda j: (0, j))] + [ANY] * na,
        out_shape=[jax.ShapeDtypeStruct((t, n), F32)]
        + [jax.ShapeDtypeStruct((N_DEV, *v.shape), v.dtype) for v in shards],
        scratch_shapes=_gather_sems(na),
    )(h, w_alt, *shards)
    return outs[0], outs[1:]


def _local_step(x, mem, tgt, norm_g, mem_norm_g, w_alt, alog_row, dtb_row, dn_norm_g, final_g, shards):
    h = _norm_in(x, norm_g)
    s_kv, s_dn, s_sb, s_out, s_mem, s_conv = shards
    proj, (g_kv, g_conv) = _proj_gather(h, w_alt, [s_kv, s_conv])
    w_mem_kv = g_kv.reshape(D_MODEL, 2 * MEM_W)
    conv_w = g_conv.transpose(1, 0, 2).reshape(CONV_K, 3 * D_MODEL)

    c = _dn_conv(proj, conv_w)
    o_dn, states = _dn_fwd(c, proj, alog_row, dtb_row, dn_norm_g)
    o_sb, o_sb_raw, (g_dn, g_sb, g_out, g_mem) = _sb_fwd(proj, [s_dn, s_sb, s_out, s_mem])
    w_br_dn = g_dn.reshape(D_MODEL, D_MODEL)
    w_br_sb = g_sb.reshape(D_MODEL, D_MODEL)
    w_out = g_out.reshape(D_MODEL, D_MODEL)
    w_br_mem = g_mem.transpose(1, 0, 2).reshape(MEM_W, D_MODEL)
    mkv = _mem_kv(mem, mem_norm_g, w_mem_kv)
    o_m = _mem_attn(proj, mkv)

    (loss, dout, d_final_g, merged, dy_dn, dy_sb, dy_m, dgates, do_dn, do_sb, do_m) = _block_tail(
        proj, o_dn, o_sb, o_m, x, tgt, w_br_dn, w_br_sb, w_br_mem, w_out, final_g)
    dw_out = _matmul_tn(merged, dout, BF16, 256, 1024, 2048, "dw_out")
    dw_br_dn = _matmul_tn(o_dn, dy_dn, BF16, 256, 1024, 2048, "dw_br_dn")
    dw_br_sb = _matmul_tn(o_sb, dy_sb, BF16, 256, 1024, 2048, "dw_br_sb")
    dw_br_mem = _matmul_tn(o_m, dy_m, BF16, 256, 1024, 2048, "dw_br_mem")

    dmq, dmz, dmkv = _mem_attn_bwd(proj, mkv, do_m)
    d_mem_norm_g, dw_mem_kv = _mem_kv_bwd(mem, mem_norm_g, w_mem_kv, dmkv)
    rows_d = D_MODEL // N_DEV
    small_blocks = jnp.concatenate([
        dw_br_dn.reshape(N_DEV, rows_d, D_MODEL), dw_br_sb.reshape(N_DEV, rows_d, D_MODEL),
        dw_out.reshape(N_DEV, rows_d, D_MODEL), dw_mem_kv.reshape(N_DEV, rows_d // 2, D_MODEL),
        dw_br_mem.reshape(MEM_W, N_DEV, rows_d).transpose(1, 0, 2).reshape(N_DEV, MEM_W // N_DEV, D_MODEL)], axis=1)
    dq_sb, dk_sb, dv_sb, dz_sb, small_parts = _sb_bwd(proj, o_sb_raw, do_sb, small_blocks)
    (d_small,) = _sum_slots([small_parts], "sum_small_grads")
    dcq, dck, dcv, dz_dn, dba, dscal, d_dn_norm_g = _dn_bwd(c, proj, alog_row, dtb_row, dn_norm_g, states, do_dn)
    dq_dn, dcw_q = _dn_conv_bwd(proj, conv_w, dcq, 0)
    dk_dn, dcw_k = _dn_conv_bwd(proj, conv_w, dck, 1)
    dv_dn, dcw_v = _dn_conv_bwd(proj, conv_w, dcv, 2)
    d_conv_w = jnp.concatenate([dcw_q, dcw_k, dcw_v], axis=1)

    dproj = jnp.concatenate([dq_dn, dk_dn, dv_dn, dz_dn, dq_sb, dk_sb, dv_sb, dz_sb, dmq, dmz, dgates,
                             dba.astype(BF16)], axis=1)
    dw_alt = _matmul_tn(dproj, h, BF16, 384, 1024, 2048, "dw_alt")
    grad_x, d_norm_g, dw_landed = _grad_x(dproj, w_alt, x, norm_g, dout, dw_alt)
    return dict(loss=loss, grad_x=grad_x, norm_g=d_norm_g, mem_norm_g=d_mem_norm_g, w_alt=dw_alt, w_alt_landed=dw_landed,
                conv_w=d_conv_w,
                scal=dscal, dn_norm_g=d_dn_norm_g, small=d_small, final_g=d_final_g)


MESH = pl.DeviceIdType.MESH
ANY = pl.BlockSpec(memory_space=pl.ANY)


def _position():
    return lax.axis_index("x"), lax.axis_index("y"), lax.axis_index("c")


def _other_devices(me):
    return [tuple(1 - p if (f >> s) & 1 else p for p, s in zip(me, (2, 1, 0))) for f in range(1, N_DEV)]


def _direct_gather_copies(x_refs, land_refs, send_sems, recv_sems, local_sems):
    me = _position()
    mine = 4 * me[0] + 2 * me[1] + me[2]
    cps = []
    for a, (x_ref, land) in enumerate(zip(x_refs, land_refs)):
        cps.append(pltpu.make_async_copy(x_ref, land.at[mine], local_sems.at[a]))
        for k, peer in enumerate(_other_devices(me)):
            cps.append(pltpu.make_async_remote_copy(
                src_ref=x_ref, dst_ref=land.at[mine], send_sem=send_sems.at[7 * a + k],
                recv_sem=recv_sems.at[7 * a + k], device_id=peer, device_id_type=MESH))
    return cps


def _gather_sems(n):
    return [pltpu.SemaphoreType.DMA((7 * n,)), pltpu.SemaphoreType.DMA((7 * n,)), pltpu.SemaphoreType.DMA((n,))]


def _all_gather(xs, name):
    n = len(xs)

    def body(*refs):
        x_refs, o_refs = refs[:n], refs[n:2 * n]
        send_sems, recv_sems, local_sems = refs[2 * n:]
        x, y, c = _position()
        me, sibling = (x, y, c), (x, y, 1 - c)
        x_nbr, y_nbr, diag = (1 - x, y, c), (x, 1 - y, c), (1 - x, 1 - y, c)
        south = c == 0
        relay_from = tuple(jnp.where(south, a, b) for a, b in zip(y_nbr, x_nbr))
        relay_to = tuple(jnp.where(south, a, b) for a, b in zip(x_nbr, y_nbr))

        def slot(p):
            return 4 * p[0] + 2 * p[1] + p[2]

        def copy(a, k, block, to, src=None):
            dst = o_refs[a].at[slot(block)]
            return pltpu.make_async_remote_copy(
                src_ref=dst if src is None else src, dst_ref=dst, send_sem=send_sems.at[7 * a + k],
                recv_sem=recv_sems.at[7 * a + k], device_id=to, device_id_type=MESH)

        mine = [pltpu.make_async_copy(x_refs[a], o_refs[a].at[slot(me)], local_sems.at[a]) for a in range(n)]
        for cp in mine:
            cp.start()
        sends = []
        for a in range(n):
            sends += [copy(a, 0, me, sibling, src=x_refs[a]), copy(a, 1, me, x_nbr, src=x_refs[a]),
                      copy(a, 2, me, y_nbr, src=x_refs[a])]
        for cp in sends:
            cp.start()
        later = []
        for a in range(n):
            copy(a, 1, x_nbr, me).wait_recv()
            copy(a, 2, y_nbr, me).wait_recv()
            later += [copy(a, 3, relay_from, relay_to), copy(a, 4, x_nbr, sibling), copy(a, 5, y_nbr, sibling)]
            for cp in later[-3:]:
                cp.start()
        for a in range(n):
            copy(a, 3, diag, me).wait_recv()
            later.append(copy(a, 6, diag, sibling))
            later[-1].start()
        for a in range(n):
            copy(a, 0, sibling, me).wait_recv()
            for k, chip in ((4, x_nbr), (5, y_nbr), (6, diag)):
                copy(a, k, (chip[0], chip[1], 1 - c), me).wait_recv()
        for cp in sends + later:
            cp.wait_send()
        for cp in mine:
            cp.wait()

    return pl.pallas_call(
        body,
        name=name,
        in_specs=[ANY] * n,
        out_specs=[ANY] * n,
        out_shape=[jax.ShapeDtypeStruct((N_DEV, *v.shape), v.dtype) for v in xs],
        scratch_shapes=[pltpu.SemaphoreType.DMA((7 * n,)), pltpu.SemaphoreType.DMA((7 * n,)),
                        pltpu.SemaphoreType.DMA((n,))],
    )(*xs)


def _window_view(ref, dest):
    return ref.at[pl.ds(WIN_ROW0[dest], WIN_W), :]


def _chunk_rows(rows, cols):
    return max(ch for ch in range(ROW_TILE, rows + 1, ROW_TILE) if rows % ch == 0 and ch * cols <= (1 << 20))


def _window_sum(dw_al, landed):
    cols = dw_al.shape[1]
    ch = _chunk_rows(WIN_W, cols)

    def body(dw_ref, land_ref, o_ref, va, vb, vo, in_sems, out_sems):
        c = lax.axis_index("c")
        for b in (0, 1):
            @pl.when(c == b)
            def _(b=b):
                for i in range(N_DEV // 2):
                    _hbm_add(_window_view(dw_ref, 2 * i + b), land_ref.at[i], o_ref.at[i], (va, vb, vo), in_sems,
                             out_sems, ch)

    buf = pltpu.VMEM((2, ch, cols), BF16)
    return pl.pallas_call(
        body,
        name="rs_c",
        in_specs=[ANY, ANY],
        out_specs=ANY,
        out_shape=jax.ShapeDtypeStruct((N_DEV // 2, WIN_W, cols), BF16),
        scratch_shapes=[buf, buf, buf, pltpu.SemaphoreType.DMA((2, 2)), pltpu.SemaphoreType.DMA((2,))],
    )(dw_al, landed)


def _hbm_add(a_view, b_view, o_view, bufs, in_sems, out_sems, ch):
    rows = a_view.shape[0]
    nch = rows // ch
    va, vb, vo = bufs

    def rows_of(j):
        return pl.ds(pl.multiple_of(j * ch, 16), ch)

    def loads(j, s):
        return (pltpu.make_async_copy(a_view.at[rows_of(j), :], va.at[s], in_sems.at[0, s]),
                pltpu.make_async_copy(b_view.at[rows_of(j), :], vb.at[s], in_sems.at[1, s]))

    def store(j, s):
        return pltpu.make_async_copy(vo.at[s], o_view.at[rows_of(j), :], out_sems.at[s])

    for cp in loads(0, 0):
        cp.start()

    def step(j, _):
        s = lax.rem(j, 2)

        @pl.when(j + 1 < nch)
        def _():
            for cp in loads(j + 1, 1 - s):
                cp.start()

        for cp in loads(j, s):
            cp.wait()

        @pl.when(j >= 2)
        def _():
            store(j - 2, s).wait()

        vo[s] = (va[s].astype(F32) + vb[s].astype(F32)).astype(vo.dtype)
        store(j, s).start()
        return 0

    lax.fori_loop(0, nch, step, 0)
    for j in range(max(0, nch - 2), nch):
        store(j, j % 2).wait()


def _xy_stage(xs, first, name, gather=()):
    n_arr = len(xs)
    if first:
        shapes = [(v.shape[2] // 2, v.shape[3]) for v in xs]
        ins = list(xs)
    else:
        shapes = [(a.shape[1], a.shape[2]) for a, _ in xs]
        ins = [v for pair in xs for v in pair]
    n_blk = 2 if first else 1
    out_dtype = BF16 if first else F32
    chunk = [_chunk_rows(r, c) for (r, c) in shapes]
    n_sem = 2 * n_blk * n_arr
    n_g = len(gather)

    def body(*refs):
        n_in = len(ins)
        in_refs, g_refs = refs[:n_in], refs[n_in:n_in + n_g]
        n_out = 2 * n_arr if first else n_arr
        outs = refs[n_in + n_g:]
        o_refs, land = outs[:n_out], outs[n_out:n_out + 2 * n_arr]
        gl_refs = outs[n_out + 2 * n_arr:n_out + 2 * n_arr + n_g]
        rest = outs[n_out + 2 * n_arr + n_g:]
        bufs = rest[:3 * n_arr]
        send_sems, recv_sems, in_sems, out_sems = rest[3 * n_arr:3 * n_arr + 4]
        gathers = _direct_gather_copies(g_refs, gl_refs, *rest[3 * n_arr + 4:]) if n_g else []
        for cp in gathers:
            cp.start()
        x, y, c = _position()
        peers = {"x": (1 - x, y, c), "y": (x, 1 - y, c)}
        jobs = []
        for k in range(n_arr):
            r, _ = shapes[k]
            half_a, half_b = pl.ds(0, r), pl.ds(r, r)
            if first:
                src = in_refs[k]
                for i in range(2):
                    jobs.append((k, src.at[i, 1 - y, half_a, :], src.at[i, y, half_a, :], land[2 * k].at[i],
                                 o_refs[2 * k].at[i], "y"))
                    jobs.append((k, src.at[1 - x, i, half_b, :], src.at[x, i, half_b, :], land[2 * k + 1].at[i],
                                 o_refs[2 * k + 1].at[i], "x"))
            else:
                a1, b1 = in_refs[2 * k], in_refs[2 * k + 1]
                jobs.append((k, a1.at[1 - x], a1.at[x], land[2 * k], o_refs[k].at[half_a, :], "x"))
                jobs.append((k, b1.at[1 - y], b1.at[y], land[2 * k + 1], o_refs[k].at[half_b, :], "y"))
        sends = []
        for n, (k, send, _, landing, _, axis) in enumerate(jobs):
            cp = pltpu.make_async_remote_copy(src_ref=send, dst_ref=landing, send_sem=send_sems.at[n],
                                              recv_sem=recv_sems.at[n], device_id=peers[axis], device_id_type=MESH)
            cp.start()
            sends.append(cp)
        for cp, (k, _, kept, landing, out, _) in zip(sends, jobs):
            cp.wait_recv()
            _hbm_add(kept, landing, out, bufs[3 * k:3 * k + 3], in_sems, out_sems, chunk[k])
        for cp in sends:
            cp.wait_send()
        for cp in gathers:
            cp.wait()

    if first:
        out_shape = [jax.ShapeDtypeStruct((2, r, c), BF16) for (r, c) in shapes for _ in range(2)]
        land_shape = out_shape
    else:
        out_shape = [jax.ShapeDtypeStruct((2 * r, c), F32) for (r, c) in shapes]
        land_shape = [jax.ShapeDtypeStruct((r, c), BF16) for (r, c) in shapes for _ in range(2)]
    scratch = []
    for k in range(n_arr):
        scratch += [pltpu.VMEM((2, chunk[k], shapes[k][1]), BF16)] * 2 + [pltpu.VMEM((2, chunk[k], shapes[k][1]), out_dtype)]
    scratch += [pltpu.SemaphoreType.DMA((n_sem,)), pltpu.SemaphoreType.DMA((n_sem,)),
                pltpu.SemaphoreType.DMA((2, 2)), pltpu.SemaphoreType.DMA((2,))]
    if n_g:
        scratch += _gather_sems(n_g)
    g_shape = [jax.ShapeDtypeStruct((N_DEV, *v.shape), v.dtype) for v in gather]
    outs = pl.pallas_call(
        body,
        name=name,
        in_specs=[ANY] * (len(ins) + n_g),
        out_specs=[ANY] * (len(out_shape) + len(land_shape) + n_g),
        out_shape=out_shape + land_shape + g_shape,
        scratch_shapes=scratch,
    )(*ins, *gather)
    gathered = outs[len(out_shape) + len(land_shape):]
    outs = outs[:len(out_shape)]
    return ([(outs[2 * k], outs[2 * k + 1]) for k in range(n_arr)] if first else list(outs)), gathered


def _reduce_scatter(dw_al, landed, gather=()):
    ys = _window_sum(dw_al, landed)
    pairs, gathered = _xy_stage([ys.reshape(2, 2, *ys.shape[1:])], True, "rs_xy1", gather=gather)
    (win,), _ = _xy_stage(pairs, False, "rs_xy2")
    return win, gathered


def _sum_slots(gs, name):
    n = len(gs)

    def body(*refs):
        for g_ref, o_ref in zip(refs[:n], refs[n:]):
            acc = g_ref[0].astype(F32)
            for d in range(1, N_DEV):
                acc = acc + g_ref[d].astype(F32)
            o_ref[...] = acc

    return pl.pallas_call(body, name=name, out_shape=[jax.ShapeDtypeStruct(g.shape[1:], F32) for g in gs])(*gs)


def _assemble_w_al(wins, bas):
    cols = wins.shape[2]
    n_buf = 3
    ends = [WIN_ROW0[d + 1] if d + 1 < N_DEV else WIN_ROW0[d] + WIN_W for d in range(N_DEV)]
    tail = W_AL - ends[-1]

    def body(w_ref, ba_ref, o_ref, buf, zeros, ld_sems, st_sems, ba_sem):
        def load(d):
            return pltpu.make_async_copy(w_ref.at[d], buf.at[d % n_buf], ld_sems.at[d % n_buf])

        def store(d):
            n = ends[d] - WIN_ROW0[d]
            return pltpu.make_async_copy(buf.at[d % n_buf, pl.ds(0, n), :],
                                         o_ref.at[pl.ds(WIN_ROW0[d], n), :], st_sems.at[d % n_buf])

        zeros[...] = jnp.zeros_like(zeros)
        fill = pltpu.make_async_copy(zeros, o_ref.at[pl.ds(ends[-1], tail), :], ba_sem)
        fill.start()
        fill.wait()
        load(0).start()
        for d in range(N_DEV):
            if d + 1 < N_DEV:
                if d + 1 >= n_buf:
                    store(d + 1 - n_buf).wait()
                load(d + 1).start()
            load(d).wait()
            if d > 0:
                ov = WIN_ROW0[d - 1] + WIN_W - WIN_ROW0[d]
                buf[d % n_buf, :ov, :] = buf[d % n_buf, :ov, :] + buf[(d - 1) % n_buf, WIN_W - ov:, :]
            if d == N_DEV - 1:
                ba_copy = pltpu.make_async_copy(
                    ba_ref.at[BA_DEV], buf.at[d % n_buf, pl.ds(WIN_W - N_BA, N_BA), :], ba_sem)
                ba_copy.start()
                ba_copy.wait()
            store(d).start()
        for d in range(N_DEV - n_buf, N_DEV):
            store(d).wait()

    return pl.pallas_call(
        body,
        name="assemble_w_al",
        in_specs=[ANY, ANY],
        out_specs=ANY,
        out_shape=jax.ShapeDtypeStruct((W_AL, cols), wins.dtype),
        scratch_shapes=[pltpu.VMEM((n_buf, WIN_W, cols), wins.dtype), pltpu.VMEM((tail, cols), wins.dtype),
                        pltpu.SemaphoreType.DMA((n_buf,)), pltpu.SemaphoreType.DMA((n_buf,)), pltpu.SemaphoreType.DMA],
    )(wins, bas)


def _adamw_math(w, g, m, v):
    m_new = ADAM_B1 * m + (1.0 - ADAM_B1) * g
    v_new = ADAM_B2 * v + (1.0 - ADAM_B2) * (g * g)
    m_hat = m_new / (1.0 - ADAM_B1 ** ADAM_STEP)
    v_hat = v_new / (1.0 - ADAM_B2 ** ADAM_STEP)
    return -ADAM_LR * (m_hat / (jnp.sqrt(v_hat) + ADAM_EPS) + ADAM_WD * w), m_new, v_new


def _adamw(w, g, m, v, name, tb=134):
    r, _, c = w.shape
    assert r % tb == 0

    def body(w_ref, g_ref, m_ref, v_ref, d_ref, nm_ref, nv_ref):
        d_ref[...], nm_ref[...], nv_ref[...] = _adamw_math(w_ref[...], g_ref[...], m_ref[...], v_ref[...])

    blk = pl.BlockSpec((tb, 1, c), lambda i: (i, 0, 0))
    o = jax.ShapeDtypeStruct(w.shape, F32)
    return pl.pallas_call(body, name=name, grid=(r // tb,), in_specs=[blk] * 4, out_specs=[blk] * 3,
                          out_shape=[o, o, o])(w, g, m, v)


def _adamw_many(ws, gs, ms, vs, name):
    n = len(ws)

    def body(*refs):
        for k in range(n):
            w_ref, g_ref, m_ref, v_ref = (refs[j * n + k] for j in range(4))
            d_ref, nm_ref, nv_ref = (refs[(4 + j) * n + k] for j in range(3))
            d_ref[...], nm_ref[...], nv_ref[...] = _adamw_math(w_ref[...], g_ref[...], m_ref[...], v_ref[...])

    shapes = [jax.ShapeDtypeStruct(w.shape, F32) for w in ws]
    outs = pl.pallas_call(body, name=name, out_shape=shapes * 3)(*ws, *gs, *ms, *vs)
    return outs[:n], outs[n:2 * n], outs[2 * n:]


def _select(me, table):
    return sum(jnp.where(me == d, jnp.int32(v), jnp.int32(0)) for d, v in enumerate(table))


WIN_SHIFT = tuple(SHARD_W * d - WIN_ROW0[d] for d in range(N_DEV))
PAD_L = 64
PAD_R = 64
assert max(WIN_SHIFT) <= PAD_L and WIN_W + N_BA - SHARD_W <= PAD_R


def _shard_to_window(shard_t, me):
    shift = _select(me, WIN_SHIFT)
    padded = jnp.pad(shard_t, ((PAD_L, PAD_R), (0, 0)))
    cols = shard_t.shape[1]
    lo = lax.dynamic_slice(padded, (PAD_L - shift, 0), (WIN_W, cols))
    hi = lax.dynamic_slice(padded, (PAD_L - shift + N_BA, 0), (WIN_W, cols))
    aligned = _select(me, WIN_ROW0) + lax.broadcasted_iota(jnp.int32, (WIN_W, 1), 0)
    return jnp.where(aligned >= ORIG_BA, hi, lo)


def _window_to_shard(win, ba_grad, me):
    shift = _select(me, WIN_SHIFT)
    cols = win.shape[1]
    padded = jnp.pad(win, ((N_BA, PAD_R), (0, 0)))
    lo = lax.dynamic_slice(padded, (N_BA + shift, 0), (SHARD_W, cols))
    hi = lax.dynamic_slice(padded, (shift, 0), (SHARD_W, cols))
    orig = SHARD_W * me + lax.broadcasted_iota(jnp.int32, (SHARD_W, 1), 0)
    ba_full = lax.dynamic_update_slice(jnp.zeros((SHARD_W, cols), win.dtype), ba_grad, (BA_LOCAL, 0))
    return jnp.where(orig < ORIG_BA, lo, jnp.where(orig >= ORIG_BA + N_BA, hi, ba_full))


def _pad_row(v, width=D_MODEL):
    v = v.reshape(1, -1)
    return jnp.pad(v, ((0, 0), (0, width - v.shape[1])))


def kernel(x, mem, norm_g, mem_norm_g, w_in, conv_w, a_log, dt_bias, dn_norm_g, w_mem_kv, w_br_dn, w_br_sb, w_br_mem, w_out, final_g, loss_target, m_norm_g, m_mem_norm_g, m_w_in, m_conv_w, m_a_log, m_dt_bias, m_dn_norm_g, m_w_mem_kv, m_w_br_dn, m_w_br_sb, m_w_br_mem, m_w_out, m_final_g, v_norm_g, v_mem_norm_g, v_w_in, v_conv_w, v_a_log, v_dt_bias, v_dn_norm_g, v_w_mem_kv, v_w_br_dn, v_w_br_sb, v_w_br_mem, v_w_out, v_final_g):
    xi, yi, ci = _position()
    me = 4 * xi + 2 * yi + ci

    shard_t = w_in[0].T
    win = _shard_to_window(shard_t, me).astype(BF16)
    ba = shard_t[BA_LOCAL:BA_LOCAL + N_BA, :].astype(BF16)
    g_win, g_ba = _all_gather([win, ba], "gather_weights")
    w_alt = _assemble_w_al(g_win, g_ba)

    shards = [w_mem_kv[0].astype(BF16), w_br_dn[0].astype(BF16), w_br_sb[0].astype(BF16), w_out[0].astype(BF16),
              w_br_mem[0].astype(BF16), conv_w[0]]
    r = _local_step(x[0], mem[0], loss_target[0], norm_g, mem_norm_g, w_alt, _pad_row(a_log, LANE),
                    _pad_row(dt_bias, LANE), dn_norm_g, final_g.reshape(1, D_MODEL), shards)

    dw_alt = r["w_alt"]
    parts = [r["norm_g"], r["mem_norm_g"], r["final_g"], r["dn_norm_g"], r["scal"], r["loss"], r["conv_w"],
             dw_alt[O_BA:O_BA + N_BA, :].astype(F32)]
    g_win, gathered = _reduce_scatter(dw_alt, r["w_alt_landed"], gather=parts)
    rows_d = D_MODEL // N_DEV
    g_small = r["small"]
    g_dn, g_sb, g_out = (g_small[k * rows_d:(k + 1) * rows_d] for k in range(3))
    g_kv = g_small[3 * rows_d:3 * rows_d + rows_d // 2].reshape(rows_d, 2 * MEM_W)
    g_mem = g_small[3 * rows_d + rows_d // 2:].reshape(MEM_W, rows_d)
    s_norm_g, s_mem_norm_g, s_final_g, s_dn_norm_g, s_scal, s_loss, s_conv, s_ba = _sum_slots(gathered, "sum_small")
    loss = s_loss[0, 0]
    cw = conv_w.shape[2]
    g_conv = lax.dynamic_slice(s_conv, (0, cw * me), (CONV_K, cw))
    g_w_in_t = _window_to_shard(g_win, s_ba, me)
    grads = dict(norm_g=s_norm_g, mem_norm_g=s_mem_norm_g, w_in=g_w_in_t.T[None], conv_w=g_conv[None],
                 a_log=s_scal[0:1, :N_HEADS], dt_bias=s_scal[1:2, :N_HEADS], dn_norm_g=s_dn_norm_g, w_mem_kv=g_kv[None],
                 w_br_dn=g_dn[None], w_br_sb=g_sb[None], w_br_mem=g_mem[None], w_out=g_out[None],
                 final_g=s_final_g.reshape(D_MODEL))

    params = dict(norm_g=(norm_g, m_norm_g, v_norm_g), mem_norm_g=(mem_norm_g, m_mem_norm_g, v_mem_norm_g),
                  w_in=(w_in, m_w_in, v_w_in), conv_w=(conv_w, m_conv_w, v_conv_w), a_log=(a_log, m_a_log, v_a_log),
                  dt_bias=(dt_bias, m_dt_bias, v_dt_bias), dn_norm_g=(dn_norm_g, m_dn_norm_g, v_dn_norm_g),
                  w_mem_kv=(w_mem_kv, m_w_mem_kv, v_w_mem_kv), w_br_dn=(w_br_dn, m_w_br_dn, v_w_br_dn),
                  w_br_sb=(w_br_sb, m_w_br_sb, v_w_br_sb), w_br_mem=(w_br_mem, m_w_br_mem, v_w_br_mem),
                  w_out=(w_out, m_w_out, v_w_out), final_g=(final_g, m_final_g, v_final_g))
    order = list(params)
    deltas, new_m, new_v = {}, {}, {}
    deltas["w_in"], new_m["w_in"], new_v["w_in"] = (jnp.transpose(o, (1, 2, 0)) for o in _adamw(
        jnp.transpose(w_in, (2, 0, 1)), g_w_in_t[:, None, :], jnp.transpose(m_w_in, (2, 0, 1)),
        jnp.transpose(v_w_in, (2, 0, 1)), "adamw_w_in"))
    rest = [nm for nm in order if nm != "w_in"]

    def two_d(a):
        return a.reshape(1, -1) if a.ndim == 1 else a

    d_l, m_l, v_l = _adamw_many([two_d(params[nm][0]) for nm in rest], [two_d(grads[nm]) for nm in rest],
                                [two_d(params[nm][1]) for nm in rest], [two_d(params[nm][2]) for nm in rest], "adamw_rest")
    for k, nm in enumerate(rest):
        shp = params[nm][0].shape
        deltas[nm], new_m[nm], new_v[nm] = d_l[k].reshape(shp), m_l[k].reshape(shp), v_l[k].reshape(shp)
    return (loss, r["grad_x"][None], *[grads[nm] for nm in order], *[deltas[nm] for nm in order],
            *[new_m[nm] for nm in order], *[new_v[nm] for nm in order])
```

```python
import functools
import math

import jax
import jax.numpy as jnp
from jax import lax
from jax.experimental import pallas as pl
from jax.experimental.pallas import tpu as pltpu

F32 = jnp.float32
BF16 = jnp.bfloat16

D_MODEL = 1024
N_DEV = 8
N_HEADS = 8
D_HEAD = 128
DN_CHUNK = 64
CONV_K = 4
MEM_LEN = 256
MEM_HEADS = 4
MEM_DH = 64
MEM_W = MEM_HEADS * MEM_DH
NORM_EPS = 1e-6
IN_WIDTH = 11792
SHARD_W = IN_WIDTH // N_DEV

LANE = 128
SUPER = 2 * DN_CHUNK

O_QKV_DN = 0
O_Z_DN = 3072
O_QKV_SB = 4096
O_Z_SB = 7168
O_MQ = 8192
O_MZ = 8448
O_GATES = 8704
O_BA = 11776
W_AL = 11904
ORIG_BA = 4096
N_BA = 16

BA_DEV = ORIG_BA // SHARD_W
BA_LOCAL = ORIG_BA - BA_DEV * SHARD_W


def _aligned_col(o):
    return o if o < ORIG_BA else o - N_BA


ROW_TILE = 16
WIN_W = 1504
WIN_ROW0 = tuple(_aligned_col(SHARD_W * d) // ROW_TILE * ROW_TILE for d in range(N_DEV))
assert not any(ORIG_BA <= SHARD_W * d < ORIG_BA + N_BA for d in range(N_DEV))
assert all(WIN_ROW0[d] + WIN_W >= _aligned_col(SHARD_W * (d + 1) - 1) + 1 for d in range(N_DEV))
assert all(WIN_ROW0[d + 1] <= WIN_ROW0[d] + WIN_W for d in range(N_DEV - 1))
assert WIN_ROW0[-1] + WIN_W == O_BA + N_BA

ADAM_LR = 0.001
ADAM_B1 = 0.9
ADAM_B2 = 0.999
ADAM_EPS = 1e-08
ADAM_WD = 0.01
ADAM_STEP = 10

NN = (((1,), (0,)), ((), ()))
NT = (((1,), (1,)), ((), ()))
TN = (((0,), (0,)), ((), ()))


def _dot(a, b, dims):
    return lax.dot_general(a.astype(BF16), b.astype(BF16), dims, preferred_element_type=F32)


def _split2(a):
    hi = a.astype(BF16)
    lo = (a - hi.astype(F32)).astype(BF16)
    return hi, lo


def _dot3(a, b, dims):
    ah, al = _split2(a)
    bh, bl = _split2(b)
    d = functools.partial(lax.dot_general, dimension_numbers=dims, preferred_element_type=F32)
    return d(ah, bh) + (d(ah, bl) + d(al, bh))


def _sel_dot_impl(sel01, x, dims):
    sel = sel01.astype(BF16)
    h1 = x.astype(BF16)
    r1 = x - h1.astype(F32)
    h2 = r1.astype(BF16)
    h3 = (r1 - h2.astype(F32)).astype(BF16)
    d = functools.partial(lax.dot_general, dimension_numbers=dims, preferred_element_type=F32)
    return d(sel, h1) + (d(sel, h2) + d(sel, h3))


@jax.custom_vjp
def _sel_dot(sel01, x):
    return _sel_dot_impl(sel01, x, NN)


_sel_dot.defvjp(lambda s, x: (_sel_dot(s, x), s),
                lambda s, g: (jnp.zeros_like(s), _sel_dot_impl(s, g, TN)))


def _make_mm(dotfn):
    @jax.custom_vjp
    def nn(a, b):
        return dotfn(a, b, NN)

    @jax.custom_vjp
    def nt(a, b):
        return dotfn(a, b, NT)

    @jax.custom_vjp
    def tn(a, b):
        return dotfn(a, b, TN)

    nn.defvjp(lambda a, b: (nn(a, b), (a, b)), lambda r, g: (nt(g, r[1]), tn(r[0], g)))
    nt.defvjp(lambda a, b: (nt(a, b), (a, b)), lambda r, g: (nn(g, r[1]), tn(g, r[0])))
    tn.defvjp(lambda a, b: (tn(a, b), (a, b)), lambda r, g: (nt(r[1], g), nn(r[0], g)))
    return nn, nt, tn


mm_nn, mm_nt, mm_tn = _make_mm(_dot)
mm3_nn, mm3_nt, mm3_tn = _make_mm(_dot3)


def _sigmoid(x):
    return jax.nn.sigmoid(x)


def _silu(x):
    return x * _sigmoid(x)


def _softplus_parts(x):
    sp = jnp.log1p(jnp.exp(-jnp.abs(x)))
    return jnp.maximum(x, 0.0) + sp, jnp.maximum(-x, 0.0) + sp


def _rmsnorm(x, g):
    return x * lax.rsqrt(jnp.mean(x * x, axis=-1, keepdims=True) + NORM_EPS) * g


def _iota2(shape, dim):
    return lax.broadcasted_iota(jnp.int32, shape, dim)


def _div64(i):
    return lax.shift_right_logical(i, jnp.full(i.shape, 6, jnp.int32))


def _each(f, *lists):
    return [f(*a) for a in zip(*lists)]


@jax.custom_vjp
def _inv_unit_lower(ms):
    n = ms[0].shape[0]
    eye = (_iota2((n, n), 0) == _iota2((n, n), 1)).astype(F32)
    rs = [eye - m for m in ms]
    ps = ms
    for _ in range(5):
        ps = _each(mm3_nn, ps, ps)
        rs = _each(lambda r, p: r + mm_nn(r, p), rs, ps)
    return rs


def _inv_fwd(ms):
    rs = _inv_unit_lower(ms)
    return rs, rs


def _inv_bwd(rs, gs):
    ts = _each(mm_tn, rs, gs)
    return (_each(lambda t, r: -mm_nt(t, r), ts, rs),)


_inv_unit_lower.defvjp(_inv_fwd, _inv_bwd)


def _dn_block(cq, ck, cv, bcol, acol, zt, alog, dtb, gn, s0):
    n = SUPER
    h = DN_CHUNK
    row = _iota2((n, n), 0)
    col = _iota2((n, n), 1)
    same = _div64(row) == _div64(col)
    incl = jnp.logical_and(same, row >= col)
    strict = jnp.logical_and(same, row > col)
    incl_f = incl.astype(F32)

    qn = _each(lambda x: x * lax.rsqrt(jnp.sum(x * x, axis=-1, keepdims=True) + NORM_EPS) * (D_HEAD ** -0.5), cq)
    kn = _each(lambda x: x * lax.rsqrt(jnp.sum(x * x, axis=-1, keepdims=True) + NORM_EPS), ck)
    beta = _each(_sigmoid, bcol)
    g = _each(lambda al, ac, dt: -(jnp.exp(al) * _softplus_parts(ac + dt)[0]), alog, acol, dtb)
    gcum = _each(lambda x: _sel_dot(incl_f, jnp.broadcast_to(x, (n, n))), g)
    gam_incl = _each(lambda x: jnp.where(incl, jnp.exp(jnp.where(incl, x - x.T, 0.0)), 0.0), gcum)
    kk = _each(mm_nt, kn, kn)
    t_inv = _inv_unit_lower(_each(lambda b, x, gm: b * x * jnp.where(strict, gm, 0.0), beta, kk, gam_incl))
    eg = _each(jnp.exp, gcum)
    u = _each(lambda t, v, b: mm_nn(t, v * b), t_inv, cv, beta)
    w = _each(lambda t, k, b, e: mm_nn(t, k * (b * e)), t_inv, kn, beta, eg)
    a_intra = _each(lambda q, k, gm: mm_nt(q, k) * gm, qn, kn, gam_incl)
    q_dec = _each(lambda q, e: q * e, qn, eg)
    last0 = _each(lambda x: x[h - 1:h, :], gcum)
    last1 = _each(lambda x: x[n - 1:n, :], gcum)
    k_dec = _each(lambda k, x, l0, l1: k * jnp.exp(jnp.concatenate(
        [jnp.broadcast_to(l0, (h, n)), jnp.broadcast_to(l1, (h, n))], axis=0) - x), kn, gcum, last0, last1)
    v0 = _each(lambda uu, ww, s: uu[:h] - mm_nn(ww[:h], s), u, w, s0)
    o0 = _each(lambda q, s: mm_nn(q[:h], s), q_dec, s0)
    s1 = _each(lambda s, l0, k, v: s * jnp.exp(l0) + mm_tn(k[:h], v), s0, last0, k_dec, v0)
    v1 = _each(lambda uu, ww, s: uu[h:] - mm_nn(ww[h:], s), u, w, s1)
    o1 = _each(lambda q, s: mm_nn(q[h:], s), q_dec, s1)
    s2 = _each(lambda s, l1, k, v: s * jnp.exp(l1) + mm_tn(k[h:], v), s1, last1, k_dec, v1)
    o = _each(lambda a, b, am, x, y: jnp.concatenate([a, b], axis=0) + mm_nn(am, jnp.concatenate([x, y], axis=0)),
              o0, o1, a_intra, v0, v1)
    out = _each(lambda x, z: _rmsnorm(x, gn) * _silu(z), o, zt)
    return out, s2


def _mem_fn(mq, mz, mkv):
    mk = mkv[:, :MEM_W]
    mv = mkv[:, MEM_W:]
    lane = _iota2((1, MEM_W), 1)
    out = jnp.zeros(mq.shape, F32)
    for hd in range(MEM_HEADS):
        hm = (_div64(lane) == hd).astype(F32)
        s = mm_nt(mq * hm, mk) * (1.0 / math.sqrt(MEM_DH))
        s = s - jnp.max(s, axis=-1, keepdims=True)
        e = jnp.exp(s)
        p = e / jnp.sum(e, axis=-1, keepdims=True)
        out = out + mm_nn(p, mv) * hm
    return out * _silu(mz)


def _loss_fn(x, mo, fg, tgt):
    y = _rmsnorm(x + mo, fg)
    err = y - tgt
    return 0.5 * jnp.sum(jnp.mean(err * err, axis=-1, keepdims=True), axis=0, keepdims=True)


def _matmul_tn(a, b, out_dtype, tm, tn, tk, name):
    kdim, m = a.shape
    n = b.shape[1]
    tm, tn, tk = min(tm, m), min(tn, n), min(tk, kdim)
    assert m % tm == 0 and n % tn == 0 and kdim % tk == 0
    nk = kdim // tk

    def body(a_ref, b_ref, o_ref, acc_ref):
        k = pl.program_id(2)
        part = _dot(a_ref[...], b_ref[...], TN)

        @pl.when(k == 0)
        def _():
            acc_ref[...] = part

        @pl.when(k > 0)
        def _():
            acc_ref[...] += part

        @pl.when(k == nk - 1)
        def _():
            o_ref[...] = acc_ref[...].astype(o_ref.dtype)

    return pl.pallas_call(
        body,
        name=name,
        grid=(m // tm, n // tn, nk),
        in_specs=[pl.BlockSpec((tk, tm), lambda i, j, k: (k, i)), pl.BlockSpec((tk, tn), lambda i, j, k: (k, j))],
        out_specs=pl.BlockSpec((tm, tn), lambda i, j, k: (i, j)),
        out_shape=jax.ShapeDtypeStruct((m, n), out_dtype),
        scratch_shapes=[pltpu.VMEM((tm, tn), F32)],
        compiler_params=pltpu.CompilerParams(dimension_semantics=("parallel", "parallel", "arbitrary")),
    )(a, b)


def _norm_in(x, g, tm=256):
    t = x.shape[0]

    def body(x_ref, g_ref, h_ref):
        h_ref[...] = _rmsnorm(x_ref[...], g_ref[...]).astype(BF16)

    return pl.pallas_call(
        body,
        name="norm_in",
        grid=(t // tm,),
        in_specs=[pl.BlockSpec((tm, D_MODEL), lambda i: (i, 0)), pl.BlockSpec((1, D_MODEL), lambda i: (0, 0))],
        out_specs=pl.BlockSpec((tm, D_MODEL), lambda i: (i, 0)),
        out_shape=jax.ShapeDtypeStruct((t, D_MODEL), BF16),
    )(x, g)


def _dw_alt(parts, h, tm=512):
    t = h.shape[0]
    n_p = len(parts)
    widths = [p.shape[1] for p in parts]
    offs = [sum(widths[:s]) for s in range(n_p)]
    total = sum(widths)
    n_tiles = pl.cdiv(total, tm)

    specs = []
    for off, w in zip(offs, widths):
        if w >= tm:
            assert w % tm == 0 and off % tm == 0
            specs.append(pl.BlockSpec(
                (t, tm), lambda i, lo=off // tm, n=w // tm: (0, jnp.minimum(jnp.maximum(i - lo, 0), n - 1))))
        else:
            assert off // tm == (off + w - 1) // tm
            specs.append(pl.BlockSpec((t, w), lambda i: (0, 0), pipeline_mode=pl.Buffered(1)))

    def body(*refs):
        a_refs, h_ref, o_ref = refs[:n_p], refs[n_p], refs[n_p + 1]
        i = pl.program_id(0)
        for a_ref, off, w in zip(a_refs, offs, widths):
            if w >= tm:
                @pl.when(jnp.logical_and(i >= off // tm, i < (off + w) // tm))
                def _(a_ref=a_ref):
                    o_ref[...] = _dot(a_ref[...], h_ref[...], TN).astype(o_ref.dtype)
            else:
                @pl.when(i == off // tm)
                def _(a_ref=a_ref, r0=off % tm, w=w):
                    o_ref[r0:r0 + w, :] = _dot(a_ref[...], h_ref[...], TN).astype(o_ref.dtype)
        if total % tm:
            @pl.when(i == n_tiles - 1)
            def _():
                o_ref[total % tm:, :] = jnp.zeros((tm - total % tm, D_MODEL), o_ref.dtype)

    return pl.pallas_call(
        body,
        name="dw_alt",
        grid=(n_tiles,),
        in_specs=specs + [pl.BlockSpec((t, D_MODEL), lambda i: (0, 0), pipeline_mode=pl.Buffered(1))],
        out_specs=pl.BlockSpec((tm, D_MODEL), lambda i: (i, 0)),
        out_shape=jax.ShapeDtypeStruct((n_tiles * tm, D_MODEL), BF16),
    )(*parts, h)


def _grad_x(parts, w_alt, x, g, dres, tm=256):
    t = x.shape[0]
    tm = min(tm, t)
    n_p = len(parts)
    assert sum(p.shape[1] for p in parts) == w_alt.shape[0] and t % tm == 0

    def body(*refs):
        a_refs = refs[:n_p]
        w_ref, x_ref, g_ref, dres_ref, dx_ref, dg_ref = refs[n_p:]
        dproj = jnp.concatenate([a_ref[...] for a_ref in a_refs], axis=1)
        dh = _dot(dproj, w_ref[...], NN)

        @pl.when(pl.program_id(0) == 0)
        def _():
            dg_ref[...] = jnp.zeros_like(dg_ref)

        _, vjp = jax.vjp(_rmsnorm, x_ref[...], g_ref[...])
        dx, dg = vjp(dh)
        dx_ref[...] = dx + dres_ref[...]
        dg_ref[...] += dg

    row = pl.BlockSpec((tm, D_MODEL), lambda i: (i, 0))
    vec = pl.BlockSpec((1, D_MODEL), lambda i: (0, 0))
    return pl.pallas_call(
        body,
        name="grad_x",
        grid=(t // tm,),
        in_specs=[pl.BlockSpec((tm, p.shape[1]), lambda i: (i, 0)) for p in parts]
        + [pl.BlockSpec(w_alt.shape, lambda i: (0, 0), pipeline_mode=pl.Buffered(1)), row, vec, row],
        out_specs=[row, vec],
        out_shape=[jax.ShapeDtypeStruct((t, D_MODEL), F32), jax.ShapeDtypeStruct((1, D_MODEL), F32)],
    )(*parts, w_alt, x, g, dres)


def _block_tail(proj, o_dn, o_sb, o_m, x, tgt, w_br_dn, w_br_sb, w_br_mem, w_out, fg, tm=256):
    t = x.shape[0]
    tm = min(tm, t)
    gw = 512
    n_g = 3 * D_MODEL // gw

    def body(*refs):
        g_refs = refs[:n_g]
        (odn_ref, osb_ref, om_ref, x_ref, t_ref, wdn_ref, wsb_ref, wm_ref, wo_ref, fg_ref, loss_ref, dout_ref, dfg_ref,
         mg_ref, dyd_ref, dys_ref, dym_ref, dg_ref, dod_ref, dos_ref, dom_ref) = refs[n_g:]
        y = [_dot(odn_ref[...], wdn_ref[...], NN), _dot(osb_ref[...], wsb_ref[...], NN),
             _dot(om_ref[...], wm_ref[...], NN)]
        s = [_sigmoid(jnp.concatenate([g_refs[2 * k][...], g_refs[2 * k + 1][...]], axis=1)) for k in range(3)]
        merged16 = (s[0] * y[0] + s[1] * y[1] + s[2] * y[2]).astype(BF16)
        mg_ref[...] = merged16
        mo = _dot(merged16, wo_ref[...], NN)
        loss, vjp = jax.vjp(_loss_fn, x_ref[...], mo, fg_ref[...], t_ref[...])
        _, dout, dfg, _ = vjp(jnp.ones((1, 1), F32))

        @pl.when(pl.program_id(0) == 0)
        def _():
            loss_ref[...] = jnp.zeros_like(loss_ref)
            dfg_ref[...] = jnp.zeros_like(dfg_ref)

        loss_ref[...] += jnp.broadcast_to(loss, loss_ref.shape)
        dfg_ref[...] += dfg
        dout_ref[...] = dout
        dmerged = _dot(dout, wo_ref[...], NT)
        dy = [(sk * dmerged).astype(BF16) for sk in s]
        dyd_ref[...], dys_ref[...], dym_ref[...] = dy
        dg_ref[...] = jnp.concatenate([dmerged * yk * (sk * (1.0 - sk)) for yk, sk in zip(y, s)], axis=1).astype(BF16)
        dod_ref[...] = _dot(dy[0], wdn_ref[...], NT).astype(BF16)
        dos_ref[...] = _dot(dy[1], wsb_ref[...], NT).astype(BF16)
        dom_ref[...] = _dot(dy[2], wm_ref[...], NT).astype(BF16)

    gates = [pl.BlockSpec((tm, gw), lambda i, j=j: (i, O_GATES // gw + j)) for j in range(n_g)]
    row = pl.BlockSpec((tm, D_MODEL), lambda i: (i, 0))
    rowm = pl.BlockSpec((tm, MEM_W), lambda i: (i, 0))
    vec = pl.BlockSpec((1, D_MODEL), lambda i: (0, 0))

    def whole(a):
        return pl.BlockSpec(a.shape, lambda i: (0, 0), pipeline_mode=pl.Buffered(1))

    def bf(c):
        return jax.ShapeDtypeStruct((t, c), BF16)

    return pl.pallas_call(
        body,
        name="block_tail",
        grid=(t // tm,),
        in_specs=gates + [row, row, rowm, row, row, whole(w_br_dn), whole(w_br_sb), whole(w_br_mem), whole(w_out), vec],
        out_specs=[pl.BlockSpec((1, LANE), lambda i: (0, 0)), row, vec, row, row, row, row,
                   pl.BlockSpec((tm, 3 * D_MODEL), lambda i: (i, 0)), row, row, rowm],
        out_shape=[jax.ShapeDtypeStruct((1, LANE), F32), jax.ShapeDtypeStruct((t, D_MODEL), F32),
                   jax.ShapeDtypeStruct((1, D_MODEL), F32), bf(D_MODEL), bf(D_MODEL), bf(D_MODEL), bf(D_MODEL),
                   bf(3 * D_MODEL), bf(D_MODEL), bf(D_MODEL), bf(MEM_W)],
    )(*([proj] * n_g), o_dn, o_sb, o_m, x, tgt, w_br_dn, w_br_sb, w_br_mem, w_out, fg)


def _shift_rows(x, s):
    t = x.shape[0]
    if s == 0:
        return x
    rolled = pltpu.roll(x, s % t, 0)
    row = _iota2(x.shape, 0)
    keep = row >= s if s > 0 else row < t + s
    return jnp.where(keep, rolled, 0.0)


def _conv_pre(x, w):
    return sum(_shift_rows(x, CONV_K - 1 - j) * w[j:j + 1, :] for j in range(CONV_K))


CONV_TC = 256


def _dn_conv(proj, conv_w):
    t = proj.shape[0]
    nb = 3 * D_MODEL // CONV_TC

    def body(x_ref, w_ref, c_ref):
        c_ref[...] = _silu(_conv_pre(x_ref[...], w_ref[...]))

    return pl.pallas_call(
        body,
        name="dn_conv",
        grid=(nb,),
        in_specs=[pl.BlockSpec((t, CONV_TC), lambda j: (0, j)), pl.BlockSpec((CONV_K, CONV_TC), lambda j: (0, j))],
        out_specs=pl.BlockSpec((t, CONV_TC), lambda j: (0, j)),
        out_shape=jax.ShapeDtypeStruct((t, 3 * D_MODEL), F32),
    )(proj, conv_w)


def _dn_conv_bwd(proj, conv_w, dc, part):
    t = proj.shape[0]
    nb = D_MODEL // CONV_TC
    b0 = part * nb

    def body(x_ref, w_ref, dc_ref, dx_ref, dw_ref):
        x = x_ref[...]
        w = w_ref[...]
        pre = _conv_pre(x, w)
        sg = _sigmoid(pre)
        dpre = dc_ref[...] * (sg * (1.0 + pre * (1.0 - sg)))
        ahead = [_shift_rows(dpre, -(CONV_K - 1 - j)) for j in range(CONV_K)]
        dx_ref[...] = sum(a * w[j:j + 1, :] for j, a in enumerate(ahead)).astype(BF16)
        dw_ref[...] = jnp.concatenate([jnp.sum(a * x, axis=0, keepdims=True) for a in ahead], axis=0)

    blk = pl.BlockSpec((t, CONV_TC), lambda j: (0, j))
    return pl.pallas_call(
        body,
        name=f"dn_conv_bwd{part}",
        grid=(nb,),
        in_specs=[pl.BlockSpec((t, CONV_TC), lambda j: (0, b0 + j)),
                  pl.BlockSpec((CONV_K, CONV_TC), lambda j: (0, b0 + j)), blk],
        out_specs=[blk, pl.BlockSpec((CONV_K, CONV_TC), lambda j: (0, j))],
        out_shape=[jax.ShapeDtypeStruct((t, D_MODEL), BF16), jax.ShapeDtypeStruct((CONV_K, D_MODEL), F32)],
    )(proj, conv_w, dc)


def _ba_columns(ba, hd):
    lane = _iota2(ba.shape, 1)
    bcol = jnp.sum(jnp.where(lane == hd, ba, 0.0), axis=1, keepdims=True)
    acol = jnp.sum(jnp.where(lane == N_HEADS + hd, ba, 0.0), axis=1, keepdims=True)
    return bcol, acol


def _head_scalar(row, hd):
    lane = _iota2(row.shape, 1)
    return jnp.sum(jnp.where(lane == hd, row, 0.0), axis=1, keepdims=True)


DN_HP = 8


def _dn_inputs(cq, ck, cv, ba_ref, z_ref, alog_ref, dtb_ref, heads, lanes):
    ba = ba_ref[...]
    cols = [_ba_columns(ba, hd) for hd in heads]
    return ([cq[:, ln] for ln in lanes], [ck[:, ln] for ln in lanes], [cv[:, ln] for ln in lanes],
            [c[0] for c in cols], [c[1] for c in cols], [z_ref[:, ln] for ln in lanes],
            [_head_scalar(alog_ref[...], hd) for hd in heads], [_head_scalar(dtb_ref[...], hd) for hd in heads])


def _dn_specs(nblk, reverse):
    w = DN_HP * LANE
    nq = D_MODEL // w

    def row(i):
        return nblk - 1 - i if reverse else i

    def colblk(b0):
        return pl.BlockSpec((SUPER, w), lambda i, h: (row(i), b0 + h))

    ba = pl.BlockSpec((SUPER, LANE), lambda i, h: (row(i), O_BA // LANE))
    vec = pl.BlockSpec((1, LANE), lambda i, h: (0, 0))
    st = pl.BlockSpec((1, DN_HP, D_HEAD, D_HEAD), lambda i, h: (row(i), h, 0, 0))
    return colblk, nq, ba, vec, st


def _dn_fwd(c, proj, alog_row, dtb_row, gn):
    t = c.shape[0]
    nblk = t // SUPER
    colblk, nq, ba, vec, st = _dn_specs(nblk, False)

    def body(cq, ck, cv, ba_ref, z_ref, alog_ref, dtb_ref, gn_ref, o_ref, s_ref, state):
        @pl.when(jnp.logical_and(pl.program_id(0) == 0, pl.program_id(1) == 0))
        def _():
            state[...] = jnp.zeros_like(state)

        heads = [pl.program_id(1) * DN_HP + j for j in range(DN_HP)]
        lanes = [slice(j * LANE, (j + 1) * LANE) for j in range(DN_HP)]
        s0 = [state[hd] for hd in heads]
        outs, s2 = _dn_block(*_dn_inputs(cq, ck, cv, ba_ref, z_ref, alog_ref, dtb_ref, heads, lanes), gn_ref[...], s0)
        for j, (hd, ln) in enumerate(zip(heads, lanes)):
            s_ref[0, j] = s0[j]
            o_ref[:, ln] = outs[j].astype(BF16)
            state[hd] = s2[j]

    return pl.pallas_call(
        body,
        name="dn_fwd",
        grid=(nblk, N_HEADS // DN_HP),
        in_specs=[colblk(0), colblk(nq), colblk(2 * nq), ba, colblk(O_Z_DN // (DN_HP * LANE)), vec, vec, vec],
        out_specs=[colblk(0), st],
        out_shape=[jax.ShapeDtypeStruct((t, D_MODEL), BF16),
                   jax.ShapeDtypeStruct((nblk, N_HEADS, D_HEAD, D_HEAD), F32)],
        scratch_shapes=[pltpu.VMEM((N_HEADS, D_HEAD, D_HEAD), F32)],
    )(c, c, c, proj, proj, alog_row, dtb_row, gn)


def _dn_bwd(c, proj, alog_row, dtb_row, gn, states, do):
    t = c.shape[0]
    nblk = t // SUPER
    colblk, nq, ba, vec, st = _dn_specs(nblk, True)

    def body(cq, ck, cv, ba_ref, z_ref, alog_ref, dtb_ref, gn_ref, s_ref, do_ref,
             dq_ref, dk_ref, dv_ref, dz_ref, dba_ref, dsc_ref, dgn_ref, dstate):
        i = pl.program_id(0)
        hq = pl.program_id(1)

        @pl.when(jnp.logical_and(i == 0, hq == 0))
        def _():
            dstate[...] = jnp.zeros_like(dstate)
            dsc_ref[...] = jnp.zeros_like(dsc_ref)
            dgn_ref[...] = jnp.zeros_like(dgn_ref)

        @pl.when(hq == 0)
        def _():
            dba_ref[...] = jnp.zeros_like(dba_ref)

        lane = _iota2((SUPER, LANE), 1)
        lane1 = _iota2((1, LANE), 1)
        heads = [hq * DN_HP + j for j in range(DN_HP)]
        lanes = [slice(j * LANE, (j + 1) * LANE) for j in range(DN_HP)]
        ds_in = [dstate[hd] for hd in heads]
        s_in = [s_ref[0, j] for j in range(DN_HP)]
        _, vjp = jax.vjp(_dn_block, *_dn_inputs(cq, ck, cv, ba_ref, z_ref, alog_ref, dtb_ref, heads, lanes),
                         gn_ref[...], s_in)
        dq, dk, dv, dbc, dac, dz, dal, ddt, dgn, ds0 = vjp(([do_ref[:, ln].astype(F32) for ln in lanes], ds_in))
        dba = jnp.zeros((SUPER, LANE), F32)
        dal_row = jnp.zeros((1, LANE), F32)
        ddt_row = jnp.zeros((1, LANE), F32)
        for j, (hd, ln) in enumerate(zip(heads, lanes)):
            dq_ref[:, ln] = dq[j]
            dk_ref[:, ln] = dk[j]
            dv_ref[:, ln] = dv[j]
            dz_ref[:, ln] = dz[j].astype(BF16)
            dstate[hd] = ds0[j]
            dba = dba + jnp.where(lane == hd, dbc[j], 0.0) + jnp.where(lane == N_HEADS + hd, dac[j], 0.0)
            dal_row = dal_row + jnp.where(lane1 == hd, dal[j], 0.0)
            ddt_row = ddt_row + jnp.where(lane1 == hd, ddt[j], 0.0)
        dba_ref[...] += dba
        dsc_ref[0:1, :] += dal_row
        dsc_ref[1:2, :] += ddt_row
        dgn_ref[...] += dgn

    outs = pl.pallas_call(
        body,
        name="dn_bwd",
        grid=(nblk, N_HEADS // DN_HP),
        in_specs=[colblk(0), colblk(nq), colblk(2 * nq), ba, colblk(O_Z_DN // (DN_HP * LANE)), vec, vec, vec, st,
                  colblk(0)],
        out_specs=[colblk(0), colblk(0), colblk(0), colblk(0),
                   pl.BlockSpec((SUPER, LANE), lambda i, h: (nblk - 1 - i, 0)),
                   pl.BlockSpec((2, LANE), lambda i, h: (0, 0)), vec],
        out_shape=[jax.ShapeDtypeStruct((t, D_MODEL), F32)] * 3
        + [jax.ShapeDtypeStruct((t, D_MODEL), BF16), jax.ShapeDtypeStruct((t, LANE), F32),
           jax.ShapeDtypeStruct((2, LANE), F32), jax.ShapeDtypeStruct((1, LANE), F32)],
        scratch_shapes=[pltpu.VMEM((N_HEADS, D_HEAD, D_HEAD), F32)],
    )(c, c, c, proj, proj, alog_row, dtb_row, gn, states, do)
    return outs


SB_TQ = 256
SB_TK = 256
SB_HP_FWD = 8
SB_HP_BWD = 4


def _sb_logits(z, mask):
    sp = jnp.log(1.0 + jnp.exp(-jnp.abs(z)))
    lf_raw = -(jnp.maximum(z, 0.0) + sp)
    lb = lf_raw + z
    lf = lf_raw if mask is None else jnp.where(mask, lf_raw, 0.0)
    return lb, lf_raw, lf


def _suffix_sums(x, sel):
    hi, lo = _split2(x)
    d = functools.partial(lax.dot_general, dimension_numbers=NN, preferred_element_type=F32)
    return d(hi, sel) + d(lo, sel)


def _sb_diag_mask(tq, r):
    return r * SB_TK + _iota2((tq, SB_TK), 1) < _iota2((tq, SB_TK), 0)


def _sb_specs(t, tq, hp):
    w = hp * LANE
    q0, k0, v0, z0 = (O_QKV_SB // w, (O_QKV_SB + D_MODEL) // w, (O_QKV_SB + 2 * D_MODEL) // w, O_Z_SB // w)

    def blk(b0):
        return pl.BlockSpec((tq, w), lambda h, i: (i, b0 + h))

    def full(b0, **kw):
        return pl.BlockSpec((t, w), lambda h, i: (0, b0 + h), **kw)

    once = dict(pipeline_mode=pl.Buffered(1))
    return blk(q0), full(k0, **once), full(v0, **once), blk(z0), blk(0), full(0)


def _sb_fwd(proj, shards):
    t = proj.shape[0]
    tq = min(SB_TQ, t)
    ndiag = tq // SB_TK
    scale = 1.0 / math.sqrt(D_HEAD)
    na = len(shards)

    def body(q_ref, k_ref, v_ref, z_ref, *rest):
        x_refs, (o_ref, oraw_ref), land = rest[:na], rest[na:na + 2], rest[na + 2:2 * na + 2]
        sems = rest[2 * na + 2:]
        qi = pl.program_id(1)
        first = jnp.logical_and(pl.program_id(0) == 0, qi == 0)
        last = jnp.logical_and(pl.program_id(0) == pl.num_programs(0) - 1, qi == pl.num_programs(1) - 1)

        @pl.when(first)
        def _():
            for cp in _direct_gather_copies(x_refs, land, *sems):
                cp.start()

        lanes = [slice(hd * LANE, (hd + 1) * LANE) for hd in range(SB_HP_FWD)]
        qs = [(q_ref[:, ln] * scale).astype(BF16) for ln in lanes]
        after = (_iota2((SB_TK, SB_TK), 0) > _iota2((SB_TK, SB_TK), 1)).astype(BF16)
        oraw_ref[...] = jnp.zeros_like(oraw_ref)

        def block(kb, mask, c_lf):
            rows = pl.ds(pl.multiple_of(kb * SB_TK, SB_TK), SB_TK)
            z = _each(lambda q, ln: _dot(q, k_ref[rows, ln], NT), qs, lanes)
            lg = _each(lambda x: _sb_logits(x, mask), z)
            surv = _each(lambda x: _suffix_sums(x[2], after), lg)
            att = _each(lambda x, s, c: jnp.exp(x[0] + s + c), lg, surv, c_lf)
            if mask is not None:
                att = _each(lambda a: jnp.where(mask, a, 0.0), att)
            pv = _each(lambda a, ln: _dot(a, v_ref[rows, ln], NN), att, lanes)
            for p, ln in zip(pv, lanes):
                oraw_ref[:, ln] += p
            return tuple(_each(lambda c, x: c + jnp.sum(x[2], axis=1, keepdims=True), c_lf, lg))

        carry = tuple(jnp.zeros((tq, 1), F32) for _ in range(SB_HP_FWD))
        for r in reversed(range(ndiag)):
            carry = block(qi * ndiag + r, _sb_diag_mask(tq, r), carry)
        lax.fori_loop(0, qi * ndiag, lambda i, c: block(qi * ndiag - 1 - i, None, c), carry)
        o_ref[...] = (oraw_ref[...] * _silu(z_ref[...])).astype(BF16)

        @pl.when(last)
        def _():
            for cp in _direct_gather_copies(x_refs, land, *sems):
                cp.wait()

    q_spec, k_spec, v_spec, z_spec, out, _ = _sb_specs(t, tq, SB_HP_FWD)
    outs = pl.pallas_call(
        body,
        name="sb_fwd",
        grid=(N_HEADS // SB_HP_FWD, t // tq),
        in_specs=[q_spec, k_spec, v_spec, z_spec] + [ANY] * na,
        out_specs=[out, out] + [ANY] * na,
        out_shape=[jax.ShapeDtypeStruct((t, D_MODEL), BF16), jax.ShapeDtypeStruct((t, D_MODEL), F32)]
        + [jax.ShapeDtypeStruct((N_DEV, *v.shape), v.dtype) for v in shards],
        scratch_shapes=_gather_sems(na),
    )(proj, proj, proj, proj, *shards)
    return outs[0], outs[1], outs[2:]


def _sb_bwd(proj, oraw, do, blocks):
    t = proj.shape[0]
    tq = min(SB_TQ, t)
    ndiag = tq // SB_TK
    scale = 1.0 / math.sqrt(D_HEAD)

    def body(q_ref, k_ref, v_ref, z_ref, oraw_ref, do_ref, blk_ref, dq_ref, dk_ref, dv_ref, dz_ref, land_ref,
             dk_acc, dv_acc, p_scr, z_scr, send_sems, recv_sems, local_sem):
        qi = pl.program_id(1)
        nq = pl.num_programs(1)
        hg = pl.program_id(0)
        me = _position()
        mine = 4 * me[0] + 2 * me[1] + me[2]

        def exchange():
            cps = [pltpu.make_async_copy(blk_ref.at[mine], land_ref.at[mine], local_sem)]
            for k, peer in enumerate(_other_devices(me)):
                cps.append(pltpu.make_async_remote_copy(
                    src_ref=blk_ref.at[4 * peer[0] + 2 * peer[1] + peer[2]], dst_ref=land_ref.at[mine],
                    send_sem=send_sems.at[k], recv_sem=recv_sems.at[k], device_id=peer, device_id_type=MESH))
            return cps

        @pl.when(jnp.logical_and(hg == 0, qi == 0))
        def _():
            for cp in exchange():
                cp.start()

        @pl.when(qi == 0)
        def _():
            dk_acc[...] = jnp.zeros_like(dk_acc)
            dv_acc[...] = jnp.zeros_like(dv_acc)

        heads = range(SB_HP_BWD)
        lanes = [slice(hd * LANE, (hd + 1) * LANE) for hd in heads]
        zg = z_ref[...]
        sg = _sigmoid(zg)
        dog = do_ref[...].astype(F32)
        dz_ref[...] = (dog * oraw_ref[...] * (sg * (1.0 + zg * (1.0 - sg)))).astype(BF16)
        d_o = (dog * (zg * sg)).astype(BF16)
        d_o16 = [d_o[:, ln] for ln in lanes]
        qs = [(q_ref[:, ln] * scale).astype(BF16) for ln in lanes]
        ri = _iota2((SB_TK, SB_TK), 0)
        ci = _iota2((SB_TK, SB_TK), 1)
        after = (ri > ci).astype(BF16)
        earlier = (ri < ci).astype(BF16)

        def rows_of(kb):
            return pl.ds(pl.multiple_of(kb * SB_TK, SB_TK), SB_TK)

        def down(kb, mask, c_lf):
            rows = rows_of(kb)
            z = _each(lambda q, ln: _dot(q, k_ref[rows, ln], NT), qs, lanes)
            da = _each(lambda d, ln: _dot(d, v_ref[rows, ln], NT), d_o16, lanes)
            lg = _each(lambda x: _sb_logits(x, mask), z)
            surv = _each(lambda x: _suffix_sums(x[2], after), lg)
            att = _each(lambda x, s, c: jnp.exp(x[0] + s + c), lg, surv, c_lf)
            if mask is not None:
                att = _each(lambda a: jnp.where(mask, a, 0.0), att)
            dv = _each(lambda a, d: _dot(a, d, TN), att, d_o16)
            for hd in heads:
                p_scr[hd, kb] = att[hd] * da[hd]
                z_scr[hd, kb] = z[hd]
                dv_acc[rows, lanes[hd]] += dv[hd]
            return tuple(_each(lambda c, x: c + jnp.sum(x[2], axis=1, keepdims=True), c_lf, lg))

        c_lf = tuple(jnp.zeros((tq, 1), F32) for _ in heads)
        for r in reversed(range(ndiag)):
            c_lf = down(qi * ndiag + r, _sb_diag_mask(tq, r), c_lf)
        lax.fori_loop(0, qi * ndiag, lambda i, c: down(qi * ndiag - 1 - i, None, c), c_lf)

        def up(kb, mask, carry):
            dq, c_p = carry
            rows = rows_of(kb)
            p = [p_scr[hd, kb] for hd in heads]
            zs = [z_scr[hd, kb] for hd in heads]
            before = _each(lambda x, c: _suffix_sums(x, earlier) + c, p, c_p)
            e = _each(lambda x: jnp.exp(-jnp.abs(x)), zs)
            r = _each(lambda x: 1.0 / (1.0 + x), e)
            sig = _each(lambda x, a, b: jnp.where(x >= 0.0, b, a * b), zs, e, r)
            oms = _each(lambda x, a, b: jnp.where(x >= 0.0, a * b, b), zs, e, r)
            if mask is not None:
                sig = _each(lambda a: jnp.where(mask, a, 0.0), sig)
            dzz = _each(lambda x, o, g, b: x * o - g * b, p, oms, sig, before)
            dk = _each(lambda x, q: _dot(x, q, TN), dzz, qs)
            dq = _each(lambda a, x, ln: a + _dot(x, k_ref[rows, ln], NN), dq, dzz, lanes)
            for hd in heads:
                dk_acc[rows, lanes[hd]] += dk[hd]
            return tuple(dq), tuple(_each(lambda c, x: c + jnp.sum(x, axis=1, keepdims=True), c_p, p))

        carry = (tuple(jnp.zeros((tq, D_HEAD), F32) for _ in heads), tuple(jnp.zeros((tq, 1), F32) for _ in heads))
        carry = lax.fori_loop(0, qi * ndiag, lambda kb, c: up(kb, None, c), carry)
        for r in range(ndiag):
            carry = up(qi * ndiag + r, _sb_diag_mask(tq, r), carry)
        dq = carry[0]
        for hd in heads:
            dq_ref[:, lanes[hd]] = (dq[hd] * scale).astype(BF16)

        @pl.when(qi == nq - 1)
        def _():
            dk_ref[...] = dk_acc[...].astype(BF16)
            dv_ref[...] = dv_acc[...].astype(BF16)

        @pl.when(jnp.logical_and(hg == pl.num_programs(0) - 1, qi == nq - 1))
        def _():
            for cp in exchange():
                cp.wait()

    q_spec, k_spec, v_spec, z_spec, blk, full = _sb_specs(t, tq, SB_HP_BWD)
    o = jax.ShapeDtypeStruct((t, D_MODEL), BF16)
    w = SB_HP_BWD * LANE
    return pl.pallas_call(
        body,
        name="sb_bwd",
        grid=(N_HEADS // SB_HP_BWD, t // tq),
        in_specs=[q_spec, k_spec, v_spec, z_spec, blk, blk, ANY],
        out_specs=[blk, full, full, blk, ANY],
        out_shape=[o, o, o, o, jax.ShapeDtypeStruct(blocks.shape, blocks.dtype)],
        scratch_shapes=[pltpu.VMEM((t, w), F32), pltpu.VMEM((t, w), F32)]
        + [pltpu.VMEM((SB_HP_BWD, t // SB_TK, tq, SB_TK), F32)] * 2
        + [pltpu.SemaphoreType.DMA((N_DEV - 1,)), pltpu.SemaphoreType.DMA((N_DEV - 1,)), pltpu.SemaphoreType.DMA],
    )(proj, proj, proj, proj, oraw, do, blocks)


def _mem_kv_fn(mem, mg, w):
    return mm_nn(_rmsnorm(mem, mg), w)


def _mem_kv(mem, mg, w):
    def body(m_ref, g_ref, w_ref, o_ref):
        o_ref[...] = _mem_kv_fn(m_ref[...], g_ref[...], w_ref[...])

    return pl.pallas_call(body, name="mem_kv", out_shape=jax.ShapeDtypeStruct((MEM_LEN, 2 * MEM_W), F32))(mem, mg, w)


def _mem_kv_bwd(mem, mg, w, dmkv):
    def body(m_ref, g_ref, w_ref, d_ref, dg_ref, dw_ref):
        _, vjp = jax.vjp(_mem_kv_fn, m_ref[...], g_ref[...], w_ref[...].astype(F32))
        _, dg, dw = vjp(d_ref[...])
        dg_ref[...] = dg
        dw_ref[...] = dw.astype(BF16)

    return pl.pallas_call(
        body, name="mem_kv_bwd",
        out_shape=[jax.ShapeDtypeStruct((1, D_MODEL), F32), jax.ShapeDtypeStruct((D_MODEL, 2 * MEM_W), BF16)],
    )(mem, mg, w, dmkv)


def _mem_attn(proj, mkv, tm=256):
    t = proj.shape[0]
    tm = min(tm, t)

    def body(q_ref, z_ref, kv_ref, o_ref):
        o_ref[...] = _mem_fn(q_ref[...], z_ref[...], kv_ref[...]).astype(BF16)

    return pl.pallas_call(
        body,
        name="mem_attn",
        grid=(t // tm,),
        in_specs=[pl.BlockSpec((tm, MEM_W), lambda i: (i, O_MQ // MEM_W)),
                  pl.BlockSpec((tm, MEM_W), lambda i: (i, O_MZ // MEM_W)),
                  pl.BlockSpec((MEM_LEN, 2 * MEM_W), lambda i: (0, 0))],
        out_specs=pl.BlockSpec((tm, MEM_W), lambda i: (i, 0)),
        out_shape=jax.ShapeDtypeStruct((t, MEM_W), BF16),
    )(proj, proj, mkv)


def _mem_attn_bwd(proj, mkv, do, tm=256):
    t = proj.shape[0]
    tm = min(tm, t)

    def body(q_ref, z_ref, kv_ref, do_ref, dq_ref, dz_ref, dkv_ref):
        _, vjp = jax.vjp(_mem_fn, q_ref[...], z_ref[...], kv_ref[...])
        dq, dz, dkv = vjp(do_ref[...].astype(F32))
        dq_ref[...] = dq.astype(BF16)
        dz_ref[...] = dz.astype(BF16)

        @pl.when(pl.program_id(0) == 0)
        def _():
            dkv_ref[...] = jnp.zeros_like(dkv_ref)

        dkv_ref[...] += dkv

    blk = pl.BlockSpec((tm, MEM_W), lambda i: (i, 0))
    kv = pl.BlockSpec((MEM_LEN, 2 * MEM_W), lambda i: (0, 0))
    return pl.pallas_call(
        body,
        name="mem_attn_bwd",
        grid=(t // tm,),
        in_specs=[pl.BlockSpec((tm, MEM_W), lambda i: (i, O_MQ // MEM_W)),
                  pl.BlockSpec((tm, MEM_W), lambda i: (i, O_MZ // MEM_W)), kv, blk],
        out_specs=[blk, blk, kv],
        out_shape=[jax.ShapeDtypeStruct((t, MEM_W), BF16), jax.ShapeDtypeStruct((t, MEM_W), BF16),
                   jax.ShapeDtypeStruct((MEM_LEN, 2 * MEM_W), F32)],
    )(proj, proj, mkv, do)


def _proj_gather(h, w_alt, shards, tn=384):
    t = h.shape[0]
    n, kdim = w_alt.shape
    assert n % tn == 0
    nj = n // tn
    na = len(shards)

    def body(h_ref, w_ref, *rest):
        x_refs, o_ref, land = rest[:na], rest[na], rest[na + 1:2 * na + 1]
        send_sems, recv_sems, local_sems = rest[2 * na + 1:]
        j = pl.program_id(0)

        def copies():
            return _direct_gather_copies(x_refs, land, send_sems, recv_sems, local_sems)

        @pl.when(j == 0)
        def _():
            for cp in copies():
                cp.start()

        o_ref[...] = _dot(h_ref[...], w_ref[...], NT)

        @pl.when(j == nj - 1)
        def _():
            for cp in copies():
                cp.wait()

    outs = pl.pallas_call(
        body,
        name="proj",
        grid=(nj,),
        in_specs=[pl.BlockSpec((t, kdim), lambda j: (0, 0)), pl.BlockSpec((tn, kdim), lambda j: (j, 0))] + [ANY] * na,
        out_specs=[pl.BlockSpec((t, tn), lambda j: (0, j))] + [ANY] * na,
        out_shape=[jax.ShapeDtypeStruct((t, n), F32)]
        + [jax.ShapeDtypeStruct((N_DEV, *v.shape), v.dtype) for v in shards],
        scratch_shapes=_gather_sems(na),
    )(h, w_alt, *shards)
    return outs[0], outs[1:]


def _local_step(x, mem, tgt, norm_g, mem_norm_g, w_alt, alog_row, dtb_row, dn_norm_g, final_g, shards):
    h = _norm_in(x, norm_g)
    s_kv, s_dn, s_sb, s_out, s_mem, s_conv = shards
    proj, (g_kv, g_conv) = _proj_gather(h, w_alt, [s_kv, s_conv])
    w_mem_kv = g_kv.reshape(D_MODEL, 2 * MEM_W)
    conv_w = g_conv.transpose(1, 0, 2).reshape(CONV_K, 3 * D_MODEL)

    c = _dn_conv(proj, conv_w)
    o_dn, states = _dn_fwd(c, proj, alog_row, dtb_row, dn_norm_g)
    o_sb, o_sb_raw, (g_dn, g_sb, g_out, g_mem) = _sb_fwd(proj, [s_dn, s_sb, s_out, s_mem])
    w_br_dn = g_dn.reshape(D_MODEL, D_MODEL)
    w_br_sb = g_sb.reshape(D_MODEL, D_MODEL)
    w_out = g_out.reshape(D_MODEL, D_MODEL)
    w_br_mem = g_mem.transpose(1, 0, 2).reshape(MEM_W, D_MODEL)
    mkv = _mem_kv(mem, mem_norm_g, w_mem_kv)
    o_m = _mem_attn(proj, mkv)

    (loss, dout, d_final_g, merged, dy_dn, dy_sb, dy_m, dgates, do_dn, do_sb, do_m) = _block_tail(
        proj, o_dn, o_sb, o_m, x, tgt, w_br_dn, w_br_sb, w_br_mem, w_out, final_g)
    dw_out = _matmul_tn(merged, dout, BF16, 256, 1024, 2048, "dw_out")
    dw_br_dn = _matmul_tn(o_dn, dy_dn, BF16, 256, 1024, 2048, "dw_br_dn")
    dw_br_sb = _matmul_tn(o_sb, dy_sb, BF16, 256, 1024, 2048, "dw_br_sb")
    dw_br_mem = _matmul_tn(o_m, dy_m, BF16, 256, 1024, 2048, "dw_br_mem")

    dmq, dmz, dmkv = _mem_attn_bwd(proj, mkv, do_m)
    d_mem_norm_g, dw_mem_kv = _mem_kv_bwd(mem, mem_norm_g, w_mem_kv, dmkv)
    rows_d = D_MODEL // N_DEV
    small_blocks = jnp.concatenate([
        dw_br_dn.reshape(N_DEV, rows_d, D_MODEL), dw_br_sb.reshape(N_DEV, rows_d, D_MODEL),
        dw_out.reshape(N_DEV, rows_d, D_MODEL), dw_mem_kv.reshape(N_DEV, rows_d // 2, D_MODEL),
        dw_br_mem.reshape(MEM_W, N_DEV, rows_d).transpose(1, 0, 2).reshape(N_DEV, MEM_W // N_DEV, D_MODEL)], axis=1)
    dq_sb, dk_sb, dv_sb, dz_sb, small_parts = _sb_bwd(proj, o_sb_raw, do_sb, small_blocks)
    (d_small,) = _sum_slots([small_parts], "sum_small_grads")
    dcq, dck, dcv, dz_dn, dba, dscal, d_dn_norm_g = _dn_bwd(c, proj, alog_row, dtb_row, dn_norm_g, states, do_dn)
    dq_dn, dcw_q = _dn_conv_bwd(proj, conv_w, dcq, 0)
    dk_dn, dcw_k = _dn_conv_bwd(proj, conv_w, dck, 1)
    dv_dn, dcw_v = _dn_conv_bwd(proj, conv_w, dcv, 2)
    d_conv_w = jnp.concatenate([dcw_q, dcw_k, dcw_v], axis=1)

    dproj = [dq_dn, dk_dn, dv_dn, dz_dn, dq_sb, dk_sb, dv_sb, dz_sb, dmq, dmz, dgates, dba.astype(BF16)]
    dw_alt = _dw_alt(dproj, h)
    grad_x, d_norm_g = _grad_x(dproj, w_alt, x, norm_g, dout)
    return dict(loss=loss, grad_x=grad_x, norm_g=d_norm_g, mem_norm_g=d_mem_norm_g, w_alt=dw_alt, conv_w=d_conv_w,
                scal=dscal, dn_norm_g=d_dn_norm_g, small=d_small, final_g=d_final_g)


MESH = pl.DeviceIdType.MESH
ANY = pl.BlockSpec(memory_space=pl.ANY)


def _position():
    return lax.axis_index("x"), lax.axis_index("y"), lax.axis_index("c")


def _other_devices(me):
    return [tuple(1 - p if (f >> s) & 1 else p for p, s in zip(me, (2, 1, 0))) for f in range(1, N_DEV)]


def _direct_gather_copies(x_refs, land_refs, send_sems, recv_sems, local_sems):
    me = _position()
    mine = 4 * me[0] + 2 * me[1] + me[2]
    cps = []
    for a, (x_ref, land) in enumerate(zip(x_refs, land_refs)):
        cps.append(pltpu.make_async_copy(x_ref, land.at[mine], local_sems.at[a]))
        for k, peer in enumerate(_other_devices(me)):
            cps.append(pltpu.make_async_remote_copy(
                src_ref=x_ref, dst_ref=land.at[mine], send_sem=send_sems.at[7 * a + k],
                recv_sem=recv_sems.at[7 * a + k], device_id=peer, device_id_type=MESH))
    return cps


def _gather_sems(n):
    return [pltpu.SemaphoreType.DMA((7 * n,)), pltpu.SemaphoreType.DMA((7 * n,)), pltpu.SemaphoreType.DMA((n,))]


def _all_gather(xs, name):
    n = len(xs)

    def body(*refs):
        x_refs, o_refs = refs[:n], refs[n:2 * n]
        send_sems, recv_sems, local_sems = refs[2 * n:]
        x, y, c = _position()
        me, sibling = (x, y, c), (x, y, 1 - c)
        x_nbr, y_nbr, diag = (1 - x, y, c), (x, 1 - y, c), (1 - x, 1 - y, c)
        south = c == 0
        relay_from = tuple(jnp.where(south, a, b) for a, b in zip(y_nbr, x_nbr))
        relay_to = tuple(jnp.where(south, a, b) for a, b in zip(x_nbr, y_nbr))

        def slot(p):
            return 4 * p[0] + 2 * p[1] + p[2]

        def copy(a, k, block, to, src=None):
            dst = o_refs[a].at[slot(block)]
            return pltpu.make_async_remote_copy(
                src_ref=dst if src is None else src, dst_ref=dst, send_sem=send_sems.at[7 * a + k],
                recv_sem=recv_sems.at[7 * a + k], device_id=to, device_id_type=MESH)

        mine = [pltpu.make_async_copy(x_refs[a], o_refs[a].at[slot(me)], local_sems.at[a]) for a in range(n)]
        for cp in mine:
            cp.start()
        sends = []
        for a in range(n):
            sends += [copy(a, 0, me, sibling, src=x_refs[a]), copy(a, 1, me, x_nbr, src=x_refs[a]),
                      copy(a, 2, me, y_nbr, src=x_refs[a])]
        for cp in sends:
            cp.start()
        later = []
        for a in range(n):
            copy(a, 1, x_nbr, me).wait_recv()
            copy(a, 2, y_nbr, me).wait_recv()
            later += [copy(a, 3, relay_from, relay_to), copy(a, 4, x_nbr, sibling), copy(a, 5, y_nbr, sibling)]
            for cp in later[-3:]:
                cp.start()
        for a in range(n):
            copy(a, 3, diag, me).wait_recv()
            later.append(copy(a, 6, diag, sibling))
            later[-1].start()
        for a in range(n):
            copy(a, 0, sibling, me).wait_recv()
            for k, chip in ((4, x_nbr), (5, y_nbr), (6, diag)):
                copy(a, k, (chip[0], chip[1], 1 - c), me).wait_recv()
        for cp in sends + later:
            cp.wait_send()
        for cp in mine:
            cp.wait()

    return pl.pallas_call(
        body,
        name=name,
        in_specs=[ANY] * n,
        out_specs=[ANY] * n,
        out_shape=[jax.ShapeDtypeStruct((N_DEV, *v.shape), v.dtype) for v in xs],
        scratch_shapes=[pltpu.SemaphoreType.DMA((7 * n,)), pltpu.SemaphoreType.DMA((7 * n,)),
                        pltpu.SemaphoreType.DMA((n,))],
    )(*xs)


def _window_view(ref, dest):
    return ref.at[pl.ds(WIN_ROW0[dest], WIN_W), :]


def _chunk_rows(rows, cols):
    return max(ch for ch in range(ROW_TILE, rows + 1, ROW_TILE) if rows % ch == 0 and ch * cols <= (1 << 20))


def _halving_stage(xs, axis, name, out_dtype, windowed=(), gather=()):
    n_arr = len(xs)
    metas = []
    for k, v in enumerate(xs):
        if k in windowed:
            metas.append((N_DEV // 2, WIN_W, v.shape[1]))
        else:
            assert v.shape[1] == 2
            metas.append((v.shape[0], v.shape[2], v.shape[3]))
    chunk = [_chunk_rows(r, c) for (_, r, c) in metas]
    offs = [sum(m[0] for m in metas[:k]) for k in range(n_arr)]
    n_sem = sum(m[0] for m in metas)

    n_g = len(gather)

    def body(*refs):
        x_refs, g_refs = refs[:n_arr], refs[n_arr:n_arr + n_g]
        outs = refs[n_arr + n_g:]
        o_refs, land_refs, gl_refs = outs[:n_arr], outs[n_arr:2 * n_arr], outs[2 * n_arr:2 * n_arr + n_g]
        rest = outs[2 * n_arr + n_g:]
        bufs = rest[:3 * n_arr]
        send_sems, recv_sems, in_sems, out_sems = rest[3 * n_arr:3 * n_arr + 4]
        gathers = _direct_gather_copies(g_refs, gl_refs, *rest[3 * n_arr + 4:]) if n_g else []
        for cp in gathers:
            cp.start()
        pos = dict(zip("xyc", _position()))
        bit = pos[axis]
        peer = tuple(1 - pos[a] if a == axis else pos[a] for a in "xyc")

        def view(k, i, b):
            if k in windowed:
                return _window_view(x_refs[k], 2 * i + b)
            return x_refs[k].at[i, b]

        def add_blocks(k, a_view, b_view, o_view):
            _hbm_add(a_view, b_view, o_view, bufs[3 * k:3 * k + 3], in_sems, out_sems, chunk[k])

        for b in (0, 1):
            @pl.when(bit == b)
            def _(b=b):
                sends = []
                for k in range(n_arr):
                    for i in range(metas[k][0]):
                        cp = pltpu.make_async_remote_copy(
                            src_ref=view(k, i, 1 - b), dst_ref=land_refs[k].at[i], send_sem=send_sems.at[offs[k] + i],
                            recv_sem=recv_sems.at[offs[k] + i], device_id=peer, device_id_type=MESH)
                        cp.start()
                        sends.append(cp)
                idx = 0
                for k in range(n_arr):
                    for i in range(metas[k][0]):
                        sends[idx].wait_recv()
                        add_blocks(k, view(k, i, b), land_refs[k].at[i], o_refs[k].at[i])
                        idx += 1
                for cp in sends:
                    cp.wait_send()

        for cp in gathers:
            cp.wait()

    out_shape = [jax.ShapeDtypeStruct(m, out_dtype) for m in metas]
    land_shape = [jax.ShapeDtypeStruct(m, v.dtype) for m, v in zip(metas, xs)]
    g_shape = [jax.ShapeDtypeStruct((N_DEV, *v.shape), v.dtype) for v in gather]
    scratch = []
    for k in range(n_arr):
        blk = (2, chunk[k], metas[k][2])
        scratch += [pltpu.VMEM(blk, xs[k].dtype)] * 2 + [pltpu.VMEM(blk, out_dtype)]
    scratch += [pltpu.SemaphoreType.DMA((n_sem,)), pltpu.SemaphoreType.DMA((n_sem,)),
                pltpu.SemaphoreType.DMA((2, 2)), pltpu.SemaphoreType.DMA((2,))]
    if n_g:
        scratch += _gather_sems(n_g)
    outs = pl.pallas_call(
        body,
        name=name,
        in_specs=[ANY] * (n_arr + n_g),
        out_specs=[ANY] * (2 * n_arr + n_g),
        out_shape=out_shape + land_shape + g_shape,
        scratch_shapes=scratch,
    )(*xs, *gather)
    return outs[:n_arr], outs[2 * n_arr:]


def _hbm_add(a_view, b_view, o_view, bufs, in_sems, out_sems, ch):
    rows = a_view.shape[0]
    nch = rows // ch
    va, vb, vo = bufs

    def rows_of(j):
        return pl.ds(pl.multiple_of(j * ch, 16), ch)

    def loads(j, s):
        return (pltpu.make_async_copy(a_view.at[rows_of(j), :], va.at[s], in_sems.at[0, s]),
                pltpu.make_async_copy(b_view.at[rows_of(j), :], vb.at[s], in_sems.at[1, s]))

    def store(j, s):
        return pltpu.make_async_copy(vo.at[s], o_view.at[rows_of(j), :], out_sems.at[s])

    for cp in loads(0, 0):
        cp.start()

    def step(j, _):
        s = lax.rem(j, 2)

        @pl.when(j + 1 < nch)
        def _():
            for cp in loads(j + 1, 1 - s):
                cp.start()

        for cp in loads(j, s):
            cp.wait()

        @pl.when(j >= 2)
        def _():
            store(j - 2, s).wait()

        vo[s] = (va[s].astype(F32) + vb[s].astype(F32)).astype(vo.dtype)
        store(j, s).start()
        return 0

    lax.fori_loop(0, nch, step, 0)
    for j in range(max(0, nch - 2), nch):
        store(j, j % 2).wait()


def _xy_stage(xs, first, name):
    n_arr = len(xs)
    if first:
        shapes = [(v.shape[2] // 2, v.shape[3]) for v in xs]
        ins = list(xs)
    else:
        shapes = [(a.shape[1], a.shape[2]) for a, _ in xs]
        ins = [v for pair in xs for v in pair]
    n_blk = 2 if first else 1
    out_dtype = BF16 if first else F32
    chunk = [_chunk_rows(r, c) for (r, c) in shapes]
    n_sem = 2 * n_blk * n_arr

    def body(*refs):
        n_in = len(ins)
        in_refs = refs[:n_in]
        n_out = 2 * n_arr if first else n_arr
        o_refs = refs[n_in:n_in + n_out]
        land = refs[n_in + n_out:n_in + n_out + 2 * n_arr]
        rest = refs[n_in + n_out + 2 * n_arr:]
        bufs = rest[:3 * n_arr]
        send_sems, recv_sems, in_sems, out_sems = rest[3 * n_arr:]
        x, y, c = _position()
        peers = {"x": (1 - x, y, c), "y": (x, 1 - y, c)}
        jobs = []
        for k in range(n_arr):
            r, _ = shapes[k]
            half_a, half_b = pl.ds(0, r), pl.ds(r, r)
            if first:
                src = in_refs[k]
                for i in range(2):
                    jobs.append((k, src.at[i, 1 - y, half_a, :], src.at[i, y, half_a, :], land[2 * k].at[i],
                                 o_refs[2 * k].at[i], "y"))
                    jobs.append((k, src.at[1 - x, i, half_b, :], src.at[x, i, half_b, :], land[2 * k + 1].at[i],
                                 o_refs[2 * k + 1].at[i], "x"))
            else:
                a1, b1 = in_refs[2 * k], in_refs[2 * k + 1]
                jobs.append((k, a1.at[1 - x], a1.at[x], land[2 * k], o_refs[k].at[half_a, :], "x"))
                jobs.append((k, b1.at[1 - y], b1.at[y], land[2 * k + 1], o_refs[k].at[half_b, :], "y"))
        sends = []
        for n, (k, send, _, landing, _, axis) in enumerate(jobs):
            cp = pltpu.make_async_remote_copy(src_ref=send, dst_ref=landing, send_sem=send_sems.at[n],
                                              recv_sem=recv_sems.at[n], device_id=peers[axis], device_id_type=MESH)
            cp.start()
            sends.append(cp)
        for cp, (k, _, kept, landing, out, _) in zip(sends, jobs):
            cp.wait_recv()
            _hbm_add(kept, landing, out, bufs[3 * k:3 * k + 3], in_sems, out_sems, chunk[k])
        for cp in sends:
            cp.wait_send()

    if first:
        out_shape = [jax.ShapeDtypeStruct((2, r, c), BF16) for (r, c) in shapes for _ in range(2)]
        land_shape = out_shape
    else:
        out_shape = [jax.ShapeDtypeStruct((2 * r, c), F32) for (r, c) in shapes]
        land_shape = [jax.ShapeDtypeStruct((r, c), BF16) for (r, c) in shapes for _ in range(2)]
    scratch = []
    for k in range(n_arr):
        scratch += [pltpu.VMEM((2, chunk[k], shapes[k][1]), BF16)] * 2 + [pltpu.VMEM((2, chunk[k], shapes[k][1]), out_dtype)]
    scratch += [pltpu.SemaphoreType.DMA((n_sem,)), pltpu.SemaphoreType.DMA((n_sem,)),
                pltpu.SemaphoreType.DMA((2, 2)), pltpu.SemaphoreType.DMA((2,))]
    outs = pl.pallas_call(
        body,
        name=name,
        in_specs=[ANY] * len(ins),
        out_specs=[ANY] * (len(out_shape) + len(land_shape)),
        out_shape=out_shape + land_shape,
        scratch_shapes=scratch,
    )(*ins)
    outs = outs[:len(out_shape)]
    return [(outs[2 * k], outs[2 * k + 1]) for k in range(n_arr)] if first else list(outs)


def _reduce_scatter(dw_al, blocks, gather=()):
    xs = [dw_al] + [b.reshape(N_DEV // 2, 2, *b.shape[1:]) for b in blocks]
    ys, gathered = _halving_stage(xs, "c", "rs_c", BF16, windowed=(0,), gather=gather)
    pairs = _xy_stage([v.reshape(2, 2, *v.shape[1:]) for v in ys], True, "rs_xy1")
    return _xy_stage(pairs, False, "rs_xy2"), gathered


def _sum_slots(gs, name):
    n = len(gs)

    def body(*refs):
        for g_ref, o_ref in zip(refs[:n], refs[n:]):
            acc = g_ref[0].astype(F32)
            for d in range(1, N_DEV):
                acc = acc + g_ref[d].astype(F32)
            o_ref[...] = acc

    return pl.pallas_call(body, name=name, out_shape=[jax.ShapeDtypeStruct(g.shape[1:], F32) for g in gs])(*gs)


def _assemble_w_al(wins, bas):
    cols = wins.shape[2]
    n_buf = 3
    ends = [WIN_ROW0[d + 1] if d + 1 < N_DEV else WIN_ROW0[d] + WIN_W for d in range(N_DEV)]
    tail = W_AL - ends[-1]

    def body(w_ref, ba_ref, o_ref, buf, zeros, ld_sems, st_sems, ba_sem):
        def load(d):
            return pltpu.make_async_copy(w_ref.at[d], buf.at[d % n_buf], ld_sems.at[d % n_buf])

        def store(d):
            n = ends[d] - WIN_ROW0[d]
            return pltpu.make_async_copy(buf.at[d % n_buf, pl.ds(0, n), :],
                                         o_ref.at[pl.ds(WIN_ROW0[d], n), :], st_sems.at[d % n_buf])

        zeros[...] = jnp.zeros_like(zeros)
        fill = pltpu.make_async_copy(zeros, o_ref.at[pl.ds(ends[-1], tail), :], ba_sem)
        fill.start()
        fill.wait()
        load(0).start()
        for d in range(N_DEV):
            if d + 1 < N_DEV:
                if d + 1 >= n_buf:
                    store(d + 1 - n_buf).wait()
                load(d + 1).start()
            load(d).wait()
            if d > 0:
                ov = WIN_ROW0[d - 1] + WIN_W - WIN_ROW0[d]
                buf[d % n_buf, :ov, :] = buf[d % n_buf, :ov, :] + buf[(d - 1) % n_buf, WIN_W - ov:, :]
            if d == N_DEV - 1:
                ba_copy = pltpu.make_async_copy(
                    ba_ref.at[BA_DEV], buf.at[d % n_buf, pl.ds(WIN_W - N_BA, N_BA), :], ba_sem)
                ba_copy.start()
                ba_copy.wait()
            store(d).start()
        for d in range(N_DEV - n_buf, N_DEV):
            store(d).wait()

    return pl.pallas_call(
        body,
        name="assemble_w_al",
        in_specs=[ANY, ANY],
        out_specs=ANY,
        out_shape=jax.ShapeDtypeStruct((W_AL, cols), wins.dtype),
        scratch_shapes=[pltpu.VMEM((n_buf, WIN_W, cols), wins.dtype), pltpu.VMEM((tail, cols), wins.dtype),
                        pltpu.SemaphoreType.DMA((n_buf,)), pltpu.SemaphoreType.DMA((n_buf,)), pltpu.SemaphoreType.DMA],
    )(wins, bas)


def _adamw_math(w, g, m, v):
    m_new = ADAM_B1 * m + (1.0 - ADAM_B1) * g
    v_new = ADAM_B2 * v + (1.0 - ADAM_B2) * (g * g)
    m_hat = m_new / (1.0 - ADAM_B1 ** ADAM_STEP)
    v_hat = v_new / (1.0 - ADAM_B2 ** ADAM_STEP)
    return -ADAM_LR * (m_hat / (jnp.sqrt(v_hat) + ADAM_EPS) + ADAM_WD * w), m_new, v_new


def _adamw(w, g, m, v, name, tb=134):
    r, _, c = w.shape
    assert r % tb == 0

    def body(w_ref, g_ref, m_ref, v_ref, d_ref, nm_ref, nv_ref):
        d_ref[...], nm_ref[...], nv_ref[...] = _adamw_math(w_ref[...], g_ref[...], m_ref[...], v_ref[...])

    blk = pl.BlockSpec((tb, 1, c), lambda i: (i, 0, 0))
    o = jax.ShapeDtypeStruct(w.shape, F32)
    return pl.pallas_call(body, name=name, grid=(r // tb,), in_specs=[blk] * 4, out_specs=[blk] * 3,
                          out_shape=[o, o, o])(w, g, m, v)


def _adamw_many(ws, gs, ms, vs, name):
    n = len(ws)

    def body(*refs):
        for k in range(n):
            w_ref, g_ref, m_ref, v_ref = (refs[j * n + k] for j in range(4))
            d_ref, nm_ref, nv_ref = (refs[(4 + j) * n + k] for j in range(3))
            d_ref[...], nm_ref[...], nv_ref[...] = _adamw_math(w_ref[...], g_ref[...], m_ref[...], v_ref[...])

    shapes = [jax.ShapeDtypeStruct(w.shape, F32) for w in ws]
    outs = pl.pallas_call(body, name=name, out_shape=shapes * 3)(*ws, *gs, *ms, *vs)
    return outs[:n], outs[n:2 * n], outs[2 * n:]


def _select(me, table):
    return sum(jnp.where(me == d, jnp.int32(v), jnp.int32(0)) for d, v in enumerate(table))


WIN_SHIFT = tuple(SHARD_W * d - WIN_ROW0[d] for d in range(N_DEV))
PAD_L = 64
PAD_R = 64
assert max(WIN_SHIFT) <= PAD_L and WIN_W + N_BA - SHARD_W <= PAD_R


def _shard_to_window(shard_t, me):
    shift = _select(me, WIN_SHIFT)
    padded = jnp.pad(shard_t, ((PAD_L, PAD_R), (0, 0)))
    cols = shard_t.shape[1]
    lo = lax.dynamic_slice(padded, (PAD_L - shift, 0), (WIN_W, cols))
    hi = lax.dynamic_slice(padded, (PAD_L - shift + N_BA, 0), (WIN_W, cols))
    aligned = _select(me, WIN_ROW0) + lax.broadcasted_iota(jnp.int32, (WIN_W, 1), 0)
    return jnp.where(aligned >= ORIG_BA, hi, lo)


def _window_to_shard(win, ba_grad, me):
    shift = _select(me, WIN_SHIFT)
    cols = win.shape[1]
    padded = jnp.pad(win, ((N_BA, PAD_R), (0, 0)))
    lo = lax.dynamic_slice(padded, (N_BA + shift, 0), (SHARD_W, cols))
    hi = lax.dynamic_slice(padded, (shift, 0), (SHARD_W, cols))
    orig = SHARD_W * me + lax.broadcasted_iota(jnp.int32, (SHARD_W, 1), 0)
    ba_full = lax.dynamic_update_slice(jnp.zeros((SHARD_W, cols), win.dtype), ba_grad, (BA_LOCAL, 0))
    return jnp.where(orig < ORIG_BA, lo, jnp.where(orig >= ORIG_BA + N_BA, hi, ba_full))


def _pad_row(v, width=D_MODEL):
    v = v.reshape(1, -1)
    return jnp.pad(v, ((0, 0), (0, width - v.shape[1])))


def kernel(x, mem, norm_g, mem_norm_g, w_in, conv_w, a_log, dt_bias, dn_norm_g, w_mem_kv, w_br_dn, w_br_sb, w_br_mem, w_out, final_g, loss_target, m_norm_g, m_mem_norm_g, m_w_in, m_conv_w, m_a_log, m_dt_bias, m_dn_norm_g, m_w_mem_kv, m_w_br_dn, m_w_br_sb, m_w_br_mem, m_w_out, m_final_g, v_norm_g, v_mem_norm_g, v_w_in, v_conv_w, v_a_log, v_dt_bias, v_dn_norm_g, v_w_mem_kv, v_w_br_dn, v_w_br_sb, v_w_br_mem, v_w_out, v_final_g):
    xi, yi, ci = _position()
    me = 4 * xi + 2 * yi + ci

    shard_t = w_in[0].T
    win = _shard_to_window(shard_t, me).astype(BF16)
    ba = shard_t[BA_LOCAL:BA_LOCAL + N_BA, :].astype(BF16)
    g_win, g_ba = _all_gather([win, ba], "gather_weights")
    w_alt = _assemble_w_al(g_win, g_ba)

    shards = [w_mem_kv[0].astype(BF16), w_br_dn[0].astype(BF16), w_br_sb[0].astype(BF16), w_out[0].astype(BF16),
              w_br_mem[0].astype(BF16), conv_w[0]]
    r = _local_step(x[0], mem[0], loss_target[0], norm_g, mem_norm_g, w_alt, _pad_row(a_log, LANE),
                    _pad_row(dt_bias, LANE), dn_norm_g, final_g.reshape(1, D_MODEL), shards)

    dw_alt = r["w_alt"]
    parts = [r["norm_g"], r["mem_norm_g"], r["final_g"], r["dn_norm_g"], r["scal"], r["loss"], r["conv_w"],
             dw_alt[O_BA:O_BA + N_BA, :].astype(F32)]
    (g_win,), gathered = _reduce_scatter(dw_alt, [], gather=parts)
    rows_d = D_MODEL // N_DEV
    g_small = r["small"]
    g_dn, g_sb, g_out = (g_small[k * rows_d:(k + 1) * rows_d] for k in range(3))
    g_kv = g_small[3 * rows_d:3 * rows_d + rows_d // 2].reshape(rows_d, 2 * MEM_W)
    g_mem = g_small[3 * rows_d + rows_d // 2:].reshape(MEM_W, rows_d)
    s_norm_g, s_mem_norm_g, s_final_g, s_dn_norm_g, s_scal, s_loss, s_conv, s_ba = _sum_slots(gathered, "sum_small")
    loss = s_loss[0, 0]
    cw = conv_w.shape[2]
    g_conv = lax.dynamic_slice(s_conv, (0, cw * me), (CONV_K, cw))
    g_w_in_t = _window_to_shard(g_win, s_ba, me)
    grads = dict(norm_g=s_norm_g, mem_norm_g=s_mem_norm_g, w_in=g_w_in_t.T[None], conv_w=g_conv[None],
                 a_log=s_scal[0:1, :N_HEADS], dt_bias=s_scal[1:2, :N_HEADS], dn_norm_g=s_dn_norm_g, w_mem_kv=g_kv[None],
                 w_br_dn=g_dn[None], w_br_sb=g_sb[None], w_br_mem=g_mem[None], w_out=g_out[None],
                 final_g=s_final_g.reshape(D_MODEL))

    params = dict(norm_g=(norm_g, m_norm_g, v_norm_g), mem_norm_g=(mem_norm_g, m_mem_norm_g, v_mem_norm_g),
                  w_in=(w_in, m_w_in, v_w_in), conv_w=(conv_w, m_conv_w, v_conv_w), a_log=(a_log, m_a_log, v_a_log),
                  dt_bias=(dt_bias, m_dt_bias, v_dt_bias), dn_norm_g=(dn_norm_g, m_dn_norm_g, v_dn_norm_g),
                  w_mem_kv=(w_mem_kv, m_w_mem_kv, v_w_mem_kv), w_br_dn=(w_br_dn, m_w_br_dn, v_w_br_dn),
                  w_br_sb=(w_br_sb, m_w_br_sb, v_w_br_sb), w_br_mem=(w_br_mem, m_w_br_mem, v_w_br_mem),
                  w_out=(w_out, m_w_out, v_w_out), final_g=(final_g, m_final_g, v_final_g))
    order = list(params)
    deltas, new_m, new_v = {}, {}, {}
    deltas["w_in"], new_m["w_in"], new_v["w_in"] = (jnp.transpose(o, (1, 2, 0)) for o in _adamw(
        jnp.transpose(w_in, (2, 0, 1)), g_w_in_t[:, None, :], jnp.transpose(m_w_in, (2, 0, 1)),
        jnp.transpose(v_w_in, (2, 0, 1)), "adamw_w_in"))
    rest = [nm for nm in order if nm != "w_in"]

    def two_d(a):
        return a.reshape(1, -1) if a.ndim == 1 else a

    d_l, m_l, v_l = _adamw_many([two_d(params[nm][0]) for nm in rest], [two_d(grads[nm]) for nm in rest],
                                [two_d(params[nm][1]) for nm in rest], [two_d(params[nm][2]) for nm in rest], "adamw_rest")
    for k, nm in enumerate(rest):
        shp = params[nm][0].shape
        deltas[nm], new_m[nm], new_v[nm] = d_l[k].reshape(shp), m_l[k].reshape(shp), v_l[k].reshape(shp)
    return (loss, r["grad_x"][None], *[grads[nm] for nm in order], *[deltas[nm] for nm in order],
            *[new_m[nm] for nm in order], *[new_v[nm] for nm in order])
```

```python
import functools
import math

import jax
import jax.numpy as jnp
from jax import lax
from jax.experimental import pallas as pl
from jax.experimental.pallas import tpu as pltpu

F32 = jnp.float32
BF16 = jnp.bfloat16

D_MODEL = 1024
N_DEV = 8
N_HEADS = 8
D_HEAD = 128
DN_CHUNK = 64
CONV_K = 4
MEM_LEN = 256
MEM_HEADS = 4
MEM_DH = 64
MEM_W = MEM_HEADS * MEM_DH
NORM_EPS = 1e-6
IN_WIDTH = 11792
SHARD_W = IN_WIDTH // N_DEV

LANE = 128
SUPER = 2 * DN_CHUNK

O_QKV_DN = 0
O_Z_DN = 3072
O_QKV_SB = 4096
O_Z_SB = 7168
O_MQ = 8192
O_MZ = 8448
O_GATES = 8704
O_BA = 11776
W_AL = 11904
ORIG_BA = 4096
N_BA = 16

BA_DEV = ORIG_BA // SHARD_W
BA_LOCAL = ORIG_BA - BA_DEV * SHARD_W


def _aligned_col(o):
    return o if o < ORIG_BA else o - N_BA


ROW_TILE = 16
WIN_W = 1504
WIN_ROW0 = tuple(_aligned_col(SHARD_W * d) // ROW_TILE * ROW_TILE for d in range(N_DEV))
assert not any(ORIG_BA <= SHARD_W * d < ORIG_BA + N_BA for d in range(N_DEV))
assert all(WIN_ROW0[d] + WIN_W >= _aligned_col(SHARD_W * (d + 1) - 1) + 1 for d in range(N_DEV))
assert all(WIN_ROW0[d + 1] <= WIN_ROW0[d] + WIN_W for d in range(N_DEV - 1))
assert WIN_ROW0[-1] + WIN_W == O_BA + N_BA

ADAM_LR = 0.001
ADAM_B1 = 0.9
ADAM_B2 = 0.999
ADAM_EPS = 1e-08
ADAM_WD = 0.01
ADAM_STEP = 10

NN = (((1,), (0,)), ((), ()))
NT = (((1,), (1,)), ((), ()))
TN = (((0,), (0,)), ((), ()))


def _dot(a, b, dims):
    return lax.dot_general(a.astype(BF16), b.astype(BF16), dims, preferred_element_type=F32)


def _split2(a):
    hi = a.astype(BF16)
    lo = (a - hi.astype(F32)).astype(BF16)
    return hi, lo


def _dot3(a, b, dims):
    ah, al = _split2(a)
    bh, bl = _split2(b)
    d = functools.partial(lax.dot_general, dimension_numbers=dims, preferred_element_type=F32)
    return d(ah, bh) + (d(ah, bl) + d(al, bh))


def _sel_dot_impl(sel01, x, dims):
    sel = sel01.astype(BF16)
    h1 = x.astype(BF16)
    r1 = x - h1.astype(F32)
    h2 = r1.astype(BF16)
    h3 = (r1 - h2.astype(F32)).astype(BF16)
    d = functools.partial(lax.dot_general, dimension_numbers=dims, preferred_element_type=F32)
    return d(sel, h1) + (d(sel, h2) + d(sel, h3))


@jax.custom_vjp
def _sel_dot(sel01, x):
    return _sel_dot_impl(sel01, x, NN)


_sel_dot.defvjp(lambda s, x: (_sel_dot(s, x), s),
                lambda s, g: (jnp.zeros_like(s), _sel_dot_impl(s, g, TN)))


def _make_mm(dotfn):
    @jax.custom_vjp
    def nn(a, b):
        return dotfn(a, b, NN)

    @jax.custom_vjp
    def nt(a, b):
        return dotfn(a, b, NT)

    @jax.custom_vjp
    def tn(a, b):
        return dotfn(a, b, TN)

    nn.defvjp(lambda a, b: (nn(a, b), (a, b)), lambda r, g: (nt(g, r[1]), tn(r[0], g)))
    nt.defvjp(lambda a, b: (nt(a, b), (a, b)), lambda r, g: (nn(g, r[1]), tn(g, r[0])))
    tn.defvjp(lambda a, b: (tn(a, b), (a, b)), lambda r, g: (nt(r[1], g), nn(r[0], g)))
    return nn, nt, tn


mm_nn, mm_nt, mm_tn = _make_mm(_dot)
mm3_nn, mm3_nt, mm3_tn = _make_mm(_dot3)


def _sigmoid(x):
    return jax.nn.sigmoid(x)


def _silu(x):
    return x * _sigmoid(x)


def _softplus_parts(x):
    sp = jnp.log1p(jnp.exp(-jnp.abs(x)))
    return jnp.maximum(x, 0.0) + sp, jnp.maximum(-x, 0.0) + sp


def _rmsnorm(x, g):
    return x * lax.rsqrt(jnp.mean(x * x, axis=-1, keepdims=True) + NORM_EPS) * g


def _iota2(shape, dim):
    return lax.broadcasted_iota(jnp.int32, shape, dim)


def _div64(i):
    return lax.shift_right_logical(i, jnp.full(i.shape, 6, jnp.int32))


def _each(f, *lists):
    return [f(*a) for a in zip(*lists)]


@jax.custom_vjp
def _inv_unit_lower(ms):
    n = ms[0].shape[0]
    eye = (_iota2((n, n), 0) == _iota2((n, n), 1)).astype(F32)
    rs = [eye - m for m in ms]
    ps = ms
    for _ in range(5):
        ps = _each(mm3_nn, ps, ps)
        rs = _each(lambda r, p: r + mm_nn(r, p), rs, ps)
    return rs


def _inv_fwd(ms):
    rs = _inv_unit_lower(ms)
    return rs, rs


def _inv_bwd(rs, gs):
    ts = _each(mm_tn, rs, gs)
    return (_each(lambda t, r: -mm_nt(t, r), ts, rs),)


_inv_unit_lower.defvjp(_inv_fwd, _inv_bwd)


def _dn_block(cq, ck, cv, bcol, acol, zt, alog, dtb, gn, s0):
    n = SUPER
    h = DN_CHUNK
    row = _iota2((n, n), 0)
    col = _iota2((n, n), 1)
    same = _div64(row) == _div64(col)
    incl = jnp.logical_and(same, row >= col)
    strict = jnp.logical_and(same, row > col)
    incl_f = incl.astype(F32)

    qn = _each(lambda x: x * lax.rsqrt(jnp.sum(x * x, axis=-1, keepdims=True) + NORM_EPS) * (D_HEAD ** -0.5), cq)
    kn = _each(lambda x: x * lax.rsqrt(jnp.sum(x * x, axis=-1, keepdims=True) + NORM_EPS), ck)
    beta = _each(_sigmoid, bcol)
    g = _each(lambda al, ac, dt: -(jnp.exp(al) * _softplus_parts(ac + dt)[0]), alog, acol, dtb)
    gcum = _each(lambda x: _sel_dot(incl_f, jnp.broadcast_to(x, (n, n))), g)
    gam_incl = _each(lambda x: jnp.where(incl, jnp.exp(jnp.where(incl, x - x.T, 0.0)), 0.0), gcum)
    kk = _each(mm_nt, kn, kn)
    t_inv = _inv_unit_lower(_each(lambda b, x, gm: b * x * jnp.where(strict, gm, 0.0), beta, kk, gam_incl))
    eg = _each(jnp.exp, gcum)
    u = _each(lambda t, v, b: mm_nn(t, v * b), t_inv, cv, beta)
    w = _each(lambda t, k, b, e: mm_nn(t, k * (b * e)), t_inv, kn, beta, eg)
    a_intra = _each(lambda q, k, gm: mm_nt(q, k) * gm, qn, kn, gam_incl)
    q_dec = _each(lambda q, e: q * e, qn, eg)
    last0 = _each(lambda x: x[h - 1:h, :], gcum)
    last1 = _each(lambda x: x[n - 1:n, :], gcum)
    k_dec = _each(lambda k, x, l0, l1: k * jnp.exp(jnp.concatenate(
        [jnp.broadcast_to(l0, (h, n)), jnp.broadcast_to(l1, (h, n))], axis=0) - x), kn, gcum, last0, last1)
    v0 = _each(lambda uu, ww, s: uu[:h] - mm_nn(ww[:h], s), u, w, s0)
    o0 = _each(lambda q, s: mm_nn(q[:h], s), q_dec, s0)
    s1 = _each(lambda s, l0, k, v: s * jnp.exp(l0) + mm_tn(k[:h], v), s0, last0, k_dec, v0)
    v1 = _each(lambda uu, ww, s: uu[h:] - mm_nn(ww[h:], s), u, w, s1)
    o1 = _each(lambda q, s: mm_nn(q[h:], s), q_dec, s1)
    s2 = _each(lambda s, l1, k, v: s * jnp.exp(l1) + mm_tn(k[h:], v), s1, last1, k_dec, v1)
    o = _each(lambda a, b, am, x, y: jnp.concatenate([a, b], axis=0) + mm_nn(am, jnp.concatenate([x, y], axis=0)),
              o0, o1, a_intra, v0, v1)
    out = _each(lambda x, z: _rmsnorm(x, gn) * _silu(z), o, zt)
    return out, s2


def _mem_fn(mq, mz, mkv):
    mk = mkv[:, :MEM_W]
    mv = mkv[:, MEM_W:]
    lane = _iota2((1, MEM_W), 1)
    out = jnp.zeros(mq.shape, F32)
    for hd in range(MEM_HEADS):
        hm = (_div64(lane) == hd).astype(F32)
        s = mm_nt(mq * hm, mk) * (1.0 / math.sqrt(MEM_DH))
        s = s - jnp.max(s, axis=-1, keepdims=True)
        e = jnp.exp(s)
        p = e / jnp.sum(e, axis=-1, keepdims=True)
        out = out + mm_nn(p, mv) * hm
    return out * _silu(mz)


def _loss_fn(x, mo, fg, tgt):
    y = _rmsnorm(x + mo, fg)
    err = y - tgt
    return 0.5 * jnp.sum(jnp.mean(err * err, axis=-1, keepdims=True), axis=0, keepdims=True)


def _matmul_tn(a, b, out_dtype, tm, tn, tk, name):
    kdim, m = a.shape
    n = b.shape[1]
    tm, tn, tk = min(tm, m), min(tn, n), min(tk, kdim)
    assert m % tm == 0 and n % tn == 0 and kdim % tk == 0
    nk = kdim // tk

    def body(a_ref, b_ref, o_ref, acc_ref):
        k = pl.program_id(2)
        part = _dot(a_ref[...], b_ref[...], TN)

        @pl.when(k == 0)
        def _():
            acc_ref[...] = part

        @pl.when(k > 0)
        def _():
            acc_ref[...] += part

        @pl.when(k == nk - 1)
        def _():
            o_ref[...] = acc_ref[...].astype(o_ref.dtype)

    return pl.pallas_call(
        body,
        name=name,
        grid=(m // tm, n // tn, nk),
        in_specs=[pl.BlockSpec((tk, tm), lambda i, j, k: (k, i)), pl.BlockSpec((tk, tn), lambda i, j, k: (k, j))],
        out_specs=pl.BlockSpec((tm, tn), lambda i, j, k: (i, j)),
        out_shape=jax.ShapeDtypeStruct((m, n), out_dtype),
        scratch_shapes=[pltpu.VMEM((tm, tn), F32)],
        compiler_params=pltpu.CompilerParams(dimension_semantics=("parallel", "parallel", "arbitrary")),
    )(a, b)


def _norm_in(x, g, tm=256):
    t = x.shape[0]

    def body(x_ref, g_ref, h_ref):
        h_ref[...] = _rmsnorm(x_ref[...], g_ref[...]).astype(BF16)

    return pl.pallas_call(
        body,
        name="norm_in",
        grid=(t // tm,),
        in_specs=[pl.BlockSpec((tm, D_MODEL), lambda i: (i, 0)), pl.BlockSpec((1, D_MODEL), lambda i: (0, 0))],
        out_specs=pl.BlockSpec((tm, D_MODEL), lambda i: (i, 0)),
        out_shape=jax.ShapeDtypeStruct((t, D_MODEL), BF16),
    )(x, g)


def _dw_alt(parts, h, tm=512):
    t = h.shape[0]
    n_p = len(parts)
    widths = [p.shape[1] for p in parts]
    offs = [sum(widths[:s]) for s in range(n_p)]
    total = sum(widths)
    n_tiles = pl.cdiv(total, tm)

    specs = []
    for off, w in zip(offs, widths):
        if w >= tm:
            assert w % tm == 0 and off % tm == 0
            specs.append(pl.BlockSpec(
                (t, tm), lambda i, lo=off // tm, n=w // tm: (0, jnp.minimum(jnp.maximum(i - lo, 0), n - 1))))
        else:
            assert off // tm == (off + w - 1) // tm
            specs.append(pl.BlockSpec((t, w), lambda i: (0, 0), pipeline_mode=pl.Buffered(1)))

    def body(*refs):
        a_refs, h_ref, o_ref = refs[:n_p], refs[n_p], refs[n_p + 1]
        i = pl.program_id(0)
        for a_ref, off, w in zip(a_refs, offs, widths):
            if w >= tm:
                @pl.when(jnp.logical_and(i >= off // tm, i < (off + w) // tm))
                def _(a_ref=a_ref):
                    o_ref[...] = _dot(a_ref[...], h_ref[...], TN).astype(o_ref.dtype)
            else:
                @pl.when(i == off // tm)
                def _(a_ref=a_ref, r0=off % tm, w=w):
                    o_ref[r0:r0 + w, :] = _dot(a_ref[...], h_ref[...], TN).astype(o_ref.dtype)
        if total % tm:
            @pl.when(i == n_tiles - 1)
            def _():
                o_ref[total % tm:, :] = jnp.zeros((tm - total % tm, D_MODEL), o_ref.dtype)

    return pl.pallas_call(
        body,
        name="dw_alt",
        grid=(n_tiles,),
        in_specs=specs + [pl.BlockSpec((t, D_MODEL), lambda i: (0, 0), pipeline_mode=pl.Buffered(1))],
        out_specs=pl.BlockSpec((tm, D_MODEL), lambda i: (i, 0)),
        out_shape=jax.ShapeDtypeStruct((n_tiles * tm, D_MODEL), BF16),
    )(*parts, h)


def _grad_x(parts, w_alt, x, g, dres, tm=256, n_chunks=3):
    t = x.shape[0]
    tm = min(tm, t)
    n_p = len(parts)
    kdim = w_alt.shape[0]
    kc = kdim // n_chunks
    assert sum(p.shape[1] for p in parts) == kdim and t % tm == 0 and kdim % n_chunks == 0 and kc % LANE == 0

    def body(*refs):
        a_refs = refs[:n_p]
        w_hbm, x_ref, g_ref, dres_ref, dx_ref, dg_ref, w_ref, w_sems = refs[n_p:]
        first = pl.program_id(0) == 0

        def w_copy(c):
            rows = pl.ds(c * kc, kc)
            return pltpu.make_async_copy(w_hbm.at[rows, :], w_ref.at[rows, :], w_sems.at[c])

        @pl.when(first)
        def _():
            for c in range(n_chunks):
                w_copy(c).start()
            dg_ref[...] = jnp.zeros_like(dg_ref)

        dproj = jnp.concatenate([a_ref[...] for a_ref in a_refs], axis=1)
        dh = None
        for c in range(n_chunks):
            @pl.when(first)
            def _(c=c):
                w_copy(c).wait()

            part = _dot(dproj[:, c * kc:(c + 1) * kc], w_ref[c * kc:(c + 1) * kc, :], NN)
            dh = part if dh is None else dh + part

        _, vjp = jax.vjp(_rmsnorm, x_ref[...], g_ref[...])
        dx, dg = vjp(dh)
        dx_ref[...] = dx + dres_ref[...]
        dg_ref[...] += dg

    row = pl.BlockSpec((tm, D_MODEL), lambda i: (i, 0))
    vec = pl.BlockSpec((1, D_MODEL), lambda i: (0, 0))
    return pl.pallas_call(
        body,
        name="grad_x",
        grid=(t // tm,),
        in_specs=[pl.BlockSpec((tm, p.shape[1]), lambda i: (i, 0)) for p in parts]
        + [pl.BlockSpec(memory_space=pl.ANY), row, vec, row],
        out_specs=[row, vec],
        out_shape=[jax.ShapeDtypeStruct((t, D_MODEL), F32), jax.ShapeDtypeStruct((1, D_MODEL), F32)],
        scratch_shapes=[pltpu.VMEM(w_alt.shape, w_alt.dtype), pltpu.SemaphoreType.DMA((n_chunks,))],
    )(*parts, w_alt, x, g, dres)


def _block_tail(proj, o_dn, o_sb, o_m, x, tgt, w_br_dn, w_br_sb, w_br_mem, w_out, fg, tm=256):
    t = x.shape[0]
    tm = min(tm, t)
    gw = 512
    n_g = 3 * D_MODEL // gw

    def body(*refs):
        g_refs = refs[:n_g]
        (odn_ref, osb_ref, om_ref, x_ref, t_ref, wdn_ref, wsb_ref, wm_ref, wo_ref, fg_ref, loss_ref, dout_ref, dfg_ref,
         mg_ref, dyd_ref, dys_ref, dym_ref, dg_ref, dod_ref, dos_ref, dom_ref) = refs[n_g:]
        y = [_dot(odn_ref[...], wdn_ref[...], NN), _dot(osb_ref[...], wsb_ref[...], NN),
             _dot(om_ref[...], wm_ref[...], NN)]
        s = [_sigmoid(jnp.concatenate([g_refs[2 * k][...], g_refs[2 * k + 1][...]], axis=1)) for k in range(3)]
        merged16 = (s[0] * y[0] + s[1] * y[1] + s[2] * y[2]).astype(BF16)
        mg_ref[...] = merged16
        mo = _dot(merged16, wo_ref[...], NN)
        loss, vjp = jax.vjp(_loss_fn, x_ref[...], mo, fg_ref[...], t_ref[...])
        _, dout, dfg, _ = vjp(jnp.ones((1, 1), F32))

        @pl.when(pl.program_id(0) == 0)
        def _():
            loss_ref[...] = jnp.zeros_like(loss_ref)
            dfg_ref[...] = jnp.zeros_like(dfg_ref)

        loss_ref[...] += jnp.broadcast_to(loss, loss_ref.shape)
        dfg_ref[...] += dfg
        dout_ref[...] = dout
        dmerged = _dot(dout, wo_ref[...], NT)
        dy = [(sk * dmerged).astype(BF16) for sk in s]
        dyd_ref[...], dys_ref[...], dym_ref[...] = dy
        dg_ref[...] = jnp.concatenate([dmerged * yk * (sk * (1.0 - sk)) for yk, sk in zip(y, s)], axis=1).astype(BF16)
        dod_ref[...] = _dot(dy[0], wdn_ref[...], NT).astype(BF16)
        dos_ref[...] = _dot(dy[1], wsb_ref[...], NT).astype(BF16)
        dom_ref[...] = _dot(dy[2], wm_ref[...], NT).astype(BF16)

    gates = [pl.BlockSpec((tm, gw), lambda i, j=j: (i, O_GATES // gw + j)) for j in range(n_g)]
    row = pl.BlockSpec((tm, D_MODEL), lambda i: (i, 0))
    rowm = pl.BlockSpec((tm, MEM_W), lambda i: (i, 0))
    vec = pl.BlockSpec((1, D_MODEL), lambda i: (0, 0))

    def whole(a):
        return pl.BlockSpec(a.shape, lambda i: (0, 0), pipeline_mode=pl.Buffered(1))

    def bf(c):
        return jax.ShapeDtypeStruct((t, c), BF16)

    return pl.pallas_call(
        body,
        name="block_tail",
        grid=(t // tm,),
        in_specs=gates + [row, row, rowm, row, row, whole(w_br_dn), whole(w_br_sb), whole(w_br_mem), whole(w_out), vec],
        out_specs=[pl.BlockSpec((1, LANE), lambda i: (0, 0)), row, vec, row, row, row, row,
                   pl.BlockSpec((tm, 3 * D_MODEL), lambda i: (i, 0)), row, row, rowm],
        out_shape=[jax.ShapeDtypeStruct((1, LANE), F32), jax.ShapeDtypeStruct((t, D_MODEL), F32),
                   jax.ShapeDtypeStruct((1, D_MODEL), F32), bf(D_MODEL), bf(D_MODEL), bf(D_MODEL), bf(D_MODEL),
                   bf(3 * D_MODEL), bf(D_MODEL), bf(D_MODEL), bf(MEM_W)],
    )(*([proj] * n_g), o_dn, o_sb, o_m, x, tgt, w_br_dn, w_br_sb, w_br_mem, w_out, fg)


def _shift_rows(x, s):
    t = x.shape[0]
    if s == 0:
        return x
    rolled = pltpu.roll(x, s % t, 0)
    row = _iota2(x.shape, 0)
    keep = row >= s if s > 0 else row < t + s
    return jnp.where(keep, rolled, 0.0)


def _conv_pre(x, w):
    return sum(_shift_rows(x, CONV_K - 1 - j) * w[j:j + 1, :] for j in range(CONV_K))


CONV_TC = 256


def _dn_conv(proj, conv_w):
    t = proj.shape[0]
    nb = 3 * D_MODEL // CONV_TC

    def body(x_ref, w_ref, c_ref):
        c_ref[...] = _silu(_conv_pre(x_ref[...], w_ref[...]))

    return pl.pallas_call(
        body,
        name="dn_conv",
        grid=(nb,),
        in_specs=[pl.BlockSpec((t, CONV_TC), lambda j: (0, j)), pl.BlockSpec((CONV_K, CONV_TC), lambda j: (0, j))],
        out_specs=pl.BlockSpec((t, CONV_TC), lambda j: (0, j)),
        out_shape=jax.ShapeDtypeStruct((t, 3 * D_MODEL), F32),
    )(proj, conv_w)


def _dn_conv_bwd(proj, conv_w, dc, part):
    t = proj.shape[0]
    nb = D_MODEL // CONV_TC
    b0 = part * nb

    def body(x_ref, w_ref, dc_ref, dx_ref, dw_ref):
        x = x_ref[...]
        w = w_ref[...]
        pre = _conv_pre(x, w)
        sg = _sigmoid(pre)
        dpre = dc_ref[...] * (sg * (1.0 + pre * (1.0 - sg)))
        ahead = [_shift_rows(dpre, -(CONV_K - 1 - j)) for j in range(CONV_K)]
        dx_ref[...] = sum(a * w[j:j + 1, :] for j, a in enumerate(ahead)).astype(BF16)
        dw_ref[...] = jnp.concatenate([jnp.sum(a * x, axis=0, keepdims=True) for a in ahead], axis=0)

    blk = pl.BlockSpec((t, CONV_TC), lambda j: (0, j))
    return pl.pallas_call(
        body,
        name=f"dn_conv_bwd{part}",
        grid=(nb,),
        in_specs=[pl.BlockSpec((t, CONV_TC), lambda j: (0, b0 + j)),
                  pl.BlockSpec((CONV_K, CONV_TC), lambda j: (0, b0 + j)), blk],
        out_specs=[blk, pl.BlockSpec((CONV_K, CONV_TC), lambda j: (0, j))],
        out_shape=[jax.ShapeDtypeStruct((t, D_MODEL), BF16), jax.ShapeDtypeStruct((CONV_K, D_MODEL), F32)],
    )(proj, conv_w, dc)


def _ba_columns(ba, hd):
    lane = _iota2(ba.shape, 1)
    bcol = jnp.sum(jnp.where(lane == hd, ba, 0.0), axis=1, keepdims=True)
    acol = jnp.sum(jnp.where(lane == N_HEADS + hd, ba, 0.0), axis=1, keepdims=True)
    return bcol, acol


def _head_scalar(row, hd):
    lane = _iota2(row.shape, 1)
    return jnp.sum(jnp.where(lane == hd, row, 0.0), axis=1, keepdims=True)


DN_HP = 8


def _dn_inputs(cq, ck, cv, ba_ref, z_ref, alog_ref, dtb_ref, heads, lanes):
    ba = ba_ref[...]
    cols = [_ba_columns(ba, hd) for hd in heads]
    return ([cq[:, ln] for ln in lanes], [ck[:, ln] for ln in lanes], [cv[:, ln] for ln in lanes],
            [c[0] for c in cols], [c[1] for c in cols], [z_ref[:, ln] for ln in lanes],
            [_head_scalar(alog_ref[...], hd) for hd in heads], [_head_scalar(dtb_ref[...], hd) for hd in heads])


def _dn_specs(nblk, reverse):
    w = DN_HP * LANE
    nq = D_MODEL // w

    def row(i):
        return nblk - 1 - i if reverse else i

    def colblk(b0):
        return pl.BlockSpec((SUPER, w), lambda i, h: (row(i), b0 + h))

    ba = pl.BlockSpec((SUPER, LANE), lambda i, h: (row(i), O_BA // LANE))
    vec = pl.BlockSpec((1, LANE), lambda i, h: (0, 0))
    st = pl.BlockSpec((1, DN_HP, D_HEAD, D_HEAD), lambda i, h: (row(i), h, 0, 0))
    return colblk, nq, ba, vec, st


def _dn_fwd(c, proj, alog_row, dtb_row, gn):
    t = c.shape[0]
    nblk = t // SUPER
    colblk, nq, ba, vec, st = _dn_specs(nblk, False)

    def body(cq, ck, cv, ba_ref, z_ref, alog_ref, dtb_ref, gn_ref, o_ref, s_ref, state):
        @pl.when(jnp.logical_and(pl.program_id(0) == 0, pl.program_id(1) == 0))
        def _():
            state[...] = jnp.zeros_like(state)

        heads = [pl.program_id(1) * DN_HP + j for j in range(DN_HP)]
        lanes = [slice(j * LANE, (j + 1) * LANE) for j in range(DN_HP)]
        s0 = [state[hd] for hd in heads]
        outs, s2 = _dn_block(*_dn_inputs(cq, ck, cv, ba_ref, z_ref, alog_ref, dtb_ref, heads, lanes), gn_ref[...], s0)
        for j, (hd, ln) in enumerate(zip(heads, lanes)):
            s_ref[0, j] = s0[j]
            o_ref[:, ln] = outs[j].astype(BF16)
            state[hd] = s2[j]

    return pl.pallas_call(
        body,
        name="dn_fwd",
        grid=(nblk, N_HEADS // DN_HP),
        in_specs=[colblk(0), colblk(nq), colblk(2 * nq), ba, colblk(O_Z_DN // (DN_HP * LANE)), vec, vec, vec],
        out_specs=[colblk(0), st],
        out_shape=[jax.ShapeDtypeStruct((t, D_MODEL), BF16),
                   jax.ShapeDtypeStruct((nblk, N_HEADS, D_HEAD, D_HEAD), F32)],
        scratch_shapes=[pltpu.VMEM((N_HEADS, D_HEAD, D_HEAD), F32)],
    )(c, c, c, proj, proj, alog_row, dtb_row, gn)


def _dn_bwd(c, proj, alog_row, dtb_row, gn, states, do):
    t = c.shape[0]
    nblk = t // SUPER
    colblk, nq, ba, vec, st = _dn_specs(nblk, True)

    def body(cq, ck, cv, ba_ref, z_ref, alog_ref, dtb_ref, gn_ref, s_ref, do_ref,
             dq_ref, dk_ref, dv_ref, dz_ref, dba_ref, dsc_ref, dgn_ref, dstate):
        i = pl.program_id(0)
        hq = pl.program_id(1)

        @pl.when(jnp.logical_and(i == 0, hq == 0))
        def _():
            dstate[...] = jnp.zeros_like(dstate)
            dsc_ref[...] = jnp.zeros_like(dsc_ref)
            dgn_ref[...] = jnp.zeros_like(dgn_ref)

        @pl.when(hq == 0)
        def _():
            dba_ref[...] = jnp.zeros_like(dba_ref)

        lane = _iota2((SUPER, LANE), 1)
        lane1 = _iota2((1, LANE), 1)
        heads = [hq * DN_HP + j for j in range(DN_HP)]
        lanes = [slice(j * LANE, (j + 1) * LANE) for j in range(DN_HP)]
        ds_in = [dstate[hd] for hd in heads]
        s_in = [s_ref[0, j] for j in range(DN_HP)]
        _, vjp = jax.vjp(_dn_block, *_dn_inputs(cq, ck, cv, ba_ref, z_ref, alog_ref, dtb_ref, heads, lanes),
                         gn_ref[...], s_in)
        dq, dk, dv, dbc, dac, dz, dal, ddt, dgn, ds0 = vjp(([do_ref[:, ln].astype(F32) for ln in lanes], ds_in))
        dba = jnp.zeros((SUPER, LANE), F32)
        dal_row = jnp.zeros((1, LANE), F32)
        ddt_row = jnp.zeros((1, LANE), F32)
        for j, (hd, ln) in enumerate(zip(heads, lanes)):
            dq_ref[:, ln] = dq[j]
            dk_ref[:, ln] = dk[j]
            dv_ref[:, ln] = dv[j]
            dz_ref[:, ln] = dz[j].astype(BF16)
            dstate[hd] = ds0[j]
            dba = dba + jnp.where(lane == hd, dbc[j], 0.0) + jnp.where(lane == N_HEADS + hd, dac[j], 0.0)
            dal_row = dal_row + jnp.where(lane1 == hd, dal[j], 0.0)
            ddt_row = ddt_row + jnp.where(lane1 == hd, ddt[j], 0.0)
        dba_ref[...] += dba
        dsc_ref[0:1, :] += dal_row
        dsc_ref[1:2, :] += ddt_row
        dgn_ref[...] += dgn

    outs = pl.pallas_call(
        body,
        name="dn_bwd",
        grid=(nblk, N_HEADS // DN_HP),
        in_specs=[colblk(0), colblk(nq), colblk(2 * nq), ba, colblk(O_Z_DN // (DN_HP * LANE)), vec, vec, vec, st,
                  colblk(0)],
        out_specs=[colblk(0), colblk(0), colblk(0), colblk(0),
                   pl.BlockSpec((SUPER, LANE), lambda i, h: (nblk - 1 - i, 0)),
                   pl.BlockSpec((2, LANE), lambda i, h: (0, 0)), vec],
        out_shape=[jax.ShapeDtypeStruct((t, D_MODEL), F32)] * 3
        + [jax.ShapeDtypeStruct((t, D_MODEL), BF16), jax.ShapeDtypeStruct((t, LANE), F32),
           jax.ShapeDtypeStruct((2, LANE), F32), jax.ShapeDtypeStruct((1, LANE), F32)],
        scratch_shapes=[pltpu.VMEM((N_HEADS, D_HEAD, D_HEAD), F32)],
    )(c, c, c, proj, proj, alog_row, dtb_row, gn, states, do)
    return outs


SB_TQ = 256
SB_TK = 256
SB_HP_FWD = 8
SB_HP_BWD = 4


def _sb_logits(z, mask):
    sp = jnp.log(1.0 + jnp.exp(-jnp.abs(z)))
    lf_raw = -(jnp.maximum(z, 0.0) + sp)
    lb = lf_raw + z
    lf = lf_raw if mask is None else jnp.where(mask, lf_raw, 0.0)
    return lb, lf_raw, lf


def _suffix_sums(x, sel):
    hi, lo = _split2(x)
    d = functools.partial(lax.dot_general, dimension_numbers=NN, preferred_element_type=F32)
    return d(hi, sel) + d(lo, sel)


def _sb_diag_mask(tq, r):
    return r * SB_TK + _iota2((tq, SB_TK), 1) < _iota2((tq, SB_TK), 0)


def _sb_specs(t, tq, hp):
    w = hp * LANE
    q0, k0, v0, z0 = (O_QKV_SB // w, (O_QKV_SB + D_MODEL) // w, (O_QKV_SB + 2 * D_MODEL) // w, O_Z_SB // w)

    def blk(b0):
        return pl.BlockSpec((tq, w), lambda h, i: (i, b0 + h))

    def full(b0, **kw):
        return pl.BlockSpec((t, w), lambda h, i: (0, b0 + h), **kw)

    once = dict(pipeline_mode=pl.Buffered(1))
    return blk(q0), full(k0, **once), full(v0, **once), blk(z0), blk(0), full(0)


def _sb_fwd(proj, shards):
    t = proj.shape[0]
    tq = min(SB_TQ, t)
    ndiag = tq // SB_TK
    scale = 1.0 / math.sqrt(D_HEAD)
    na = len(shards)

    def body(q_ref, k_ref, v_ref, z_ref, *rest):
        x_refs, (o_ref, oraw_ref), land = rest[:na], rest[na:na + 2], rest[na + 2:2 * na + 2]
        sems = rest[2 * na + 2:]
        qi = pl.program_id(1)
        first = jnp.logical_and(pl.program_id(0) == 0, qi == 0)
        last = jnp.logical_and(pl.program_id(0) == pl.num_programs(0) - 1, qi == pl.num_programs(1) - 1)

        @pl.when(first)
        def _():
            for cp in _direct_gather_copies(x_refs, land, *sems):
                cp.start()

        lanes = [slice(hd * LANE, (hd + 1) * LANE) for hd in range(SB_HP_FWD)]
        qs = [(q_ref[:, ln] * scale).astype(BF16) for ln in lanes]
        after = (_iota2((SB_TK, SB_TK), 0) > _iota2((SB_TK, SB_TK), 1)).astype(BF16)
        oraw_ref[...] = jnp.zeros_like(oraw_ref)

        def block(kb, mask, c_lf):
            rows = pl.ds(pl.multiple_of(kb * SB_TK, SB_TK), SB_TK)
            z = _each(lambda q, ln: _dot(q, k_ref[rows, ln], NT), qs, lanes)
            lg = _each(lambda x: _sb_logits(x, mask), z)
            surv = _each(lambda x: _suffix_sums(x[2], after), lg)
            att = _each(lambda x, s, c: jnp.exp(x[0] + s + c), lg, surv, c_lf)
            if mask is not None:
                att = _each(lambda a: jnp.where(mask, a, 0.0), att)
            pv = _each(lambda a, ln: _dot(a, v_ref[rows, ln], NN), att, lanes)
            for p, ln in zip(pv, lanes):
                oraw_ref[:, ln] += p
            return tuple(_each(lambda c, x: c + jnp.sum(x[2], axis=1, keepdims=True), c_lf, lg))

        carry = tuple(jnp.zeros((tq, 1), F32) for _ in range(SB_HP_FWD))
        for r in reversed(range(ndiag)):
            carry = block(qi * ndiag + r, _sb_diag_mask(tq, r), carry)
        lax.fori_loop(0, qi * ndiag, lambda i, c: block(qi * ndiag - 1 - i, None, c), carry)
        o_ref[...] = (oraw_ref[...] * _silu(z_ref[...])).astype(BF16)

        @pl.when(last)
        def _():
            for cp in _direct_gather_copies(x_refs, land, *sems):
                cp.wait()

    q_spec, k_spec, v_spec, z_spec, out, _ = _sb_specs(t, tq, SB_HP_FWD)
    outs = pl.pallas_call(
        body,
        name="sb_fwd",
        grid=(N_HEADS // SB_HP_FWD, t // tq),
        in_specs=[q_spec, k_spec, v_spec, z_spec] + [ANY] * na,
        out_specs=[out, out] + [ANY] * na,
        out_shape=[jax.ShapeDtypeStruct((t, D_MODEL), BF16), jax.ShapeDtypeStruct((t, D_MODEL), F32)]
        + [jax.ShapeDtypeStruct((N_DEV, *v.shape), v.dtype) for v in shards],
        scratch_shapes=_gather_sems(na),
    )(proj, proj, proj, proj, *shards)
    return outs[0], outs[1], outs[2:]


def _sb_bwd(proj, oraw, do, blocks):
    t = proj.shape[0]
    tq = min(SB_TQ, t)
    ndiag = tq // SB_TK
    scale = 1.0 / math.sqrt(D_HEAD)

    def body(q_ref, k_ref, v_ref, z_ref, oraw_ref, do_ref, blk_ref, dq_ref, dk_ref, dv_ref, dz_ref, land_ref,
             dk_acc, dv_acc, p_scr, z_scr, send_sems, recv_sems, local_sem):
        qi = pl.program_id(1)
        nq = pl.num_programs(1)
        hg = pl.program_id(0)
        me = _position()
        mine = 4 * me[0] + 2 * me[1] + me[2]

        def exchange():
            cps = [pltpu.make_async_copy(blk_ref.at[mine], land_ref.at[mine], local_sem)]
            for k, peer in enumerate(_other_devices(me)):
                cps.append(pltpu.make_async_remote_copy(
                    src_ref=blk_ref.at[4 * peer[0] + 2 * peer[1] + peer[2]], dst_ref=land_ref.at[mine],
                    send_sem=send_sems.at[k], recv_sem=recv_sems.at[k], device_id=peer, device_id_type=MESH))
            return cps

        @pl.when(jnp.logical_and(hg == 0, qi == 0))
        def _():
            for cp in exchange():
                cp.start()

        @pl.when(qi == 0)
        def _():
            dk_acc[...] = jnp.zeros_like(dk_acc)
            dv_acc[...] = jnp.zeros_like(dv_acc)

        heads = range(SB_HP_BWD)
        lanes = [slice(hd * LANE, (hd + 1) * LANE) for hd in heads]
        zg = z_ref[...]
        sg = _sigmoid(zg)
        dog = do_ref[...].astype(F32)
        dz_ref[...] = (dog * oraw_ref[...] * (sg * (1.0 + zg * (1.0 - sg)))).astype(BF16)
        d_o = (dog * (zg * sg)).astype(BF16)
        d_o16 = [d_o[:, ln] for ln in lanes]
        qs = [(q_ref[:, ln] * scale).astype(BF16) for ln in lanes]
        ri = _iota2((SB_TK, SB_TK), 0)
        ci = _iota2((SB_TK, SB_TK), 1)
        after = (ri > ci).astype(BF16)
        earlier = (ri < ci).astype(BF16)

        def rows_of(kb):
            return pl.ds(pl.multiple_of(kb * SB_TK, SB_TK), SB_TK)

        def down(kb, mask, c_lf):
            rows = rows_of(kb)
            z = _each(lambda q, ln: _dot(q, k_ref[rows, ln], NT), qs, lanes)
            da = _each(lambda d, ln: _dot(d, v_ref[rows, ln], NT), d_o16, lanes)
            lg = _each(lambda x: _sb_logits(x, mask), z)
            surv = _each(lambda x: _suffix_sums(x[2], after), lg)
            att = _each(lambda x, s, c: jnp.exp(x[0] + s + c), lg, surv, c_lf)
            if mask is not None:
                att = _each(lambda a: jnp.where(mask, a, 0.0), att)
            dv = _each(lambda a, d: _dot(a, d, TN), att, d_o16)
            for hd in heads:
                p_scr[hd, kb] = att[hd] * da[hd]
                z_scr[hd, kb] = z[hd]
                dv_acc[rows, lanes[hd]] += dv[hd]
            return tuple(_each(lambda c, x: c + jnp.sum(x[2], axis=1, keepdims=True), c_lf, lg))

        c_lf = tuple(jnp.zeros((tq, 1), F32) for _ in heads)
        for r in reversed(range(ndiag)):
            c_lf = down(qi * ndiag + r, _sb_diag_mask(tq, r), c_lf)
        lax.fori_loop(0, qi * ndiag, lambda i, c: down(qi * ndiag - 1 - i, None, c), c_lf)

        def up(kb, mask, carry):
            dq, c_p = carry
            rows = rows_of(kb)
            p = [p_scr[hd, kb] for hd in heads]
            zs = [z_scr[hd, kb] for hd in heads]
            before = _each(lambda x, c: _suffix_sums(x, earlier) + c, p, c_p)
            e = _each(lambda x: jnp.exp(-jnp.abs(x)), zs)
            r = _each(lambda x: 1.0 / (1.0 + x), e)
            sig = _each(lambda x, a, b: jnp.where(x >= 0.0, b, a * b), zs, e, r)
            oms = _each(lambda x, a, b: jnp.where(x >= 0.0, a * b, b), zs, e, r)
            if mask is not None:
                sig = _each(lambda a: jnp.where(mask, a, 0.0), sig)
            dzz = _each(lambda x, o, g, b: x * o - g * b, p, oms, sig, before)
            dk = _each(lambda x, q: _dot(x, q, TN), dzz, qs)
            dq = _each(lambda a, x, ln: a + _dot(x, k_ref[rows, ln], NN), dq, dzz, lanes)
            for hd in heads:
                dk_acc[rows, lanes[hd]] += dk[hd]
            return tuple(dq), tuple(_each(lambda c, x: c + jnp.sum(x, axis=1, keepdims=True), c_p, p))

        carry = (tuple(jnp.zeros((tq, D_HEAD), F32) for _ in heads), tuple(jnp.zeros((tq, 1), F32) for _ in heads))
        carry = lax.fori_loop(0, qi * ndiag, lambda kb, c: up(kb, None, c), carry)
        for r in range(ndiag):
            carry = up(qi * ndiag + r, _sb_diag_mask(tq, r), carry)
        dq = carry[0]
        for hd in heads:
            dq_ref[:, lanes[hd]] = (dq[hd] * scale).astype(BF16)

        @pl.when(qi == nq - 1)
        def _():
            dk_ref[...] = dk_acc[...].astype(BF16)
            dv_ref[...] = dv_acc[...].astype(BF16)

        @pl.when(jnp.logical_and(hg == pl.num_programs(0) - 1, qi == nq - 1))
        def _():
            for cp in exchange():
                cp.wait()

    q_spec, k_spec, v_spec, z_spec, blk, full = _sb_specs(t, tq, SB_HP_BWD)
    o = jax.ShapeDtypeStruct((t, D_MODEL), BF16)
    w = SB_HP_BWD * LANE
    return pl.pallas_call(
        body,
        name="sb_bwd",
        grid=(N_HEADS // SB_HP_BWD, t // tq),
        in_specs=[q_spec, k_spec, v_spec, z_spec, blk, blk, ANY],
        out_specs=[blk, full, full, blk, ANY],
        out_shape=[o, o, o, o, jax.ShapeDtypeStruct(blocks.shape, blocks.dtype)],
        scratch_shapes=[pltpu.VMEM((t, w), F32), pltpu.VMEM((t, w), F32)]
        + [pltpu.VMEM((SB_HP_BWD, t // SB_TK, tq, SB_TK), F32)] * 2
        + [pltpu.SemaphoreType.DMA((N_DEV - 1,)), pltpu.SemaphoreType.DMA((N_DEV - 1,)), pltpu.SemaphoreType.DMA],
    )(proj, proj, proj, proj, oraw, do, blocks)


def _mem_kv_fn(mem, mg, w):
    return mm_nn(_rmsnorm(mem, mg), w)


def _mem_kv(mem, mg, w):
    def body(m_ref, g_ref, w_ref, o_ref):
        o_ref[...] = _mem_kv_fn(m_ref[...], g_ref[...], w_ref[...])

    return pl.pallas_call(body, name="mem_kv", out_shape=jax.ShapeDtypeStruct((MEM_LEN, 2 * MEM_W), F32))(mem, mg, w)


def _mem_kv_bwd(mem, mg, w, dmkv):
    def body(m_ref, g_ref, w_ref, d_ref, dg_ref, dw_ref):
        _, vjp = jax.vjp(_mem_kv_fn, m_ref[...], g_ref[...], w_ref[...].astype(F32))
        _, dg, dw = vjp(d_ref[...])
        dg_ref[...] = dg
        dw_ref[...] = dw.astype(BF16)

    return pl.pallas_call(
        body, name="mem_kv_bwd",
        out_shape=[jax.ShapeDtypeStruct((1, D_MODEL), F32), jax.ShapeDtypeStruct((D_MODEL, 2 * MEM_W), BF16)],
    )(mem, mg, w, dmkv)


def _mem_attn(proj, mkv, tm=256):
    t = proj.shape[0]
    tm = min(tm, t)

    def body(q_ref, z_ref, kv_ref, o_ref):
        o_ref[...] = _mem_fn(q_ref[...], z_ref[...], kv_ref[...]).astype(BF16)

    return pl.pallas_call(
        body,
        name="mem_attn",
        grid=(t // tm,),
        in_specs=[pl.BlockSpec((tm, MEM_W), lambda i: (i, O_MQ // MEM_W)),
                  pl.BlockSpec((tm, MEM_W), lambda i: (i, O_MZ // MEM_W)),
                  pl.BlockSpec((MEM_LEN, 2 * MEM_W), lambda i: (0, 0))],
        out_specs=pl.BlockSpec((tm, MEM_W), lambda i: (i, 0)),
        out_shape=jax.ShapeDtypeStruct((t, MEM_W), BF16),
    )(proj, proj, mkv)


def _mem_attn_bwd(proj, mkv, do, tm=256):
    t = proj.shape[0]
    tm = min(tm, t)

    def body(q_ref, z_ref, kv_ref, do_ref, dq_ref, dz_ref, dkv_ref):
        _, vjp = jax.vjp(_mem_fn, q_ref[...], z_ref[...], kv_ref[...])
        dq, dz, dkv = vjp(do_ref[...].astype(F32))
        dq_ref[...] = dq.astype(BF16)
        dz_ref[...] = dz.astype(BF16)

        @pl.when(pl.program_id(0) == 0)
        def _():
            dkv_ref[...] = jnp.zeros_like(dkv_ref)

        dkv_ref[...] += dkv

    blk = pl.BlockSpec((tm, MEM_W), lambda i: (i, 0))
    kv = pl.BlockSpec((MEM_LEN, 2 * MEM_W), lambda i: (0, 0))
    return pl.pallas_call(
        body,
        name="mem_attn_bwd",
        grid=(t // tm,),
        in_specs=[pl.BlockSpec((tm, MEM_W), lambda i: (i, O_MQ // MEM_W)),
                  pl.BlockSpec((tm, MEM_W), lambda i: (i, O_MZ // MEM_W)), kv, blk],
        out_specs=[blk, blk, kv],
        out_shape=[jax.ShapeDtypeStruct((t, MEM_W), BF16), jax.ShapeDtypeStruct((t, MEM_W), BF16),
                   jax.ShapeDtypeStruct((MEM_LEN, 2 * MEM_W), F32)],
    )(proj, proj, mkv, do)


def _proj_gather(h, w_alt, shards, tm=512, tn=3968):
    t = h.shape[0]
    tm = min(tm, t)
    n, kdim = w_alt.shape
    assert n % tn == 0 and t % tm == 0
    nj, ni = n // tn, t // tm
    na = len(shards)

    def body(h_ref, w_ref, *rest):
        x_refs, o_ref, land = rest[:na], rest[na], rest[na + 1:2 * na + 1]
        send_sems, recv_sems, local_sems = rest[2 * na + 1:]
        j, i = pl.program_id(0), pl.program_id(1)

        def copies():
            return _direct_gather_copies(x_refs, land, send_sems, recv_sems, local_sems)

        @pl.when(jnp.logical_and(j == 0, i == 0))
        def _():
            for cp in copies():
                cp.start()

        o_ref[...] = _dot(h_ref[...], w_ref[...], NT)

        @pl.when(jnp.logical_and(j == nj - 1, i == ni - 1))
        def _():
            for cp in copies():
                cp.wait()

    outs = pl.pallas_call(
        body,
        name="proj",
        grid=(nj, ni),
        in_specs=[pl.BlockSpec((tm, kdim), lambda j, i: (i, 0)), pl.BlockSpec((tn, kdim), lambda j, i: (j, 0))]
        + [ANY] * na,
        out_specs=[pl.BlockSpec((tm, tn), lambda j, i: (i, j))] + [ANY] * na,
        out_shape=[jax.ShapeDtypeStruct((t, n), F32)]
        + [jax.ShapeDtypeStruct((N_DEV, *v.shape), v.dtype) for v in shards],
        scratch_shapes=_gather_sems(na),
    )(h, w_alt, *shards)
    return outs[0], outs[1:]


def _local_step(x, mem, tgt, norm_g, mem_norm_g, w_alt, alog_row, dtb_row, dn_norm_g, final_g, shards):
    h = _norm_in(x, norm_g)
    s_kv, s_dn, s_sb, s_out, s_mem, s_conv = shards
    proj, (g_kv, g_conv) = _proj_gather(h, w_alt, [s_kv, s_conv])
    w_mem_kv = g_kv.reshape(D_MODEL, 2 * MEM_W)
    conv_w = g_conv.transpose(1, 0, 2).reshape(CONV_K, 3 * D_MODEL)

    c = _dn_conv(proj, conv_w)
    o_dn, states = _dn_fwd(c, proj, alog_row, dtb_row, dn_norm_g)
    o_sb, o_sb_raw, (g_dn, g_sb, g_out, g_mem) = _sb_fwd(proj, [s_dn, s_sb, s_out, s_mem])
    w_br_dn = g_dn.reshape(D_MODEL, D_MODEL)
    w_br_sb = g_sb.reshape(D_MODEL, D_MODEL)
    w_out = g_out.reshape(D_MODEL, D_MODEL)
    w_br_mem = g_mem.transpose(1, 0, 2).reshape(MEM_W, D_MODEL)
    mkv = _mem_kv(mem, mem_norm_g, w_mem_kv)
    o_m = _mem_attn(proj, mkv)

    (loss, dout, d_final_g, merged, dy_dn, dy_sb, dy_m, dgates, do_dn, do_sb, do_m) = _block_tail(
        proj, o_dn, o_sb, o_m, x, tgt, w_br_dn, w_br_sb, w_br_mem, w_out, final_g)
    dw_out = _matmul_tn(merged, dout, BF16, 256, 1024, 2048, "dw_out")
    dw_br_dn = _matmul_tn(o_dn, dy_dn, BF16, 256, 1024, 2048, "dw_br_dn")
    dw_br_sb = _matmul_tn(o_sb, dy_sb, BF16, 256, 1024, 2048, "dw_br_sb")
    dw_br_mem = _matmul_tn(o_m, dy_m, BF16, 256, 1024, 2048, "dw_br_mem")

    dmq, dmz, dmkv = _mem_attn_bwd(proj, mkv, do_m)
    d_mem_norm_g, dw_mem_kv = _mem_kv_bwd(mem, mem_norm_g, w_mem_kv, dmkv)
    rows_d = D_MODEL // N_DEV
    small_blocks = jnp.concatenate([
        dw_br_dn.reshape(N_DEV, rows_d, D_MODEL), dw_br_sb.reshape(N_DEV, rows_d, D_MODEL),
        dw_out.reshape(N_DEV, rows_d, D_MODEL), dw_mem_kv.reshape(N_DEV, rows_d // 2, D_MODEL),
        dw_br_mem.reshape(MEM_W, N_DEV, rows_d).transpose(1, 0, 2).reshape(N_DEV, MEM_W // N_DEV, D_MODEL)], axis=1)
    dq_sb, dk_sb, dv_sb, dz_sb, small_parts = _sb_bwd(proj, o_sb_raw, do_sb, small_blocks)
    (d_small,) = _sum_slots([small_parts], "sum_small_grads")
    dcq, dck, dcv, dz_dn, dba, dscal, d_dn_norm_g = _dn_bwd(c, proj, alog_row, dtb_row, dn_norm_g, states, do_dn)
    dq_dn, dcw_q = _dn_conv_bwd(proj, conv_w, dcq, 0)
    dk_dn, dcw_k = _dn_conv_bwd(proj, conv_w, dck, 1)
    dv_dn, dcw_v = _dn_conv_bwd(proj, conv_w, dcv, 2)
    d_conv_w = jnp.concatenate([dcw_q, dcw_k, dcw_v], axis=1)

    dproj = [dq_dn, dk_dn, dv_dn, dz_dn, dq_sb, dk_sb, dv_sb, dz_sb, dmq, dmz, dgates, dba.astype(BF16)]
    dw_alt = _dw_alt(dproj, h)
    grad_x, d_norm_g = _grad_x(dproj, w_alt, x, norm_g, dout)
    return dict(loss=loss, grad_x=grad_x, norm_g=d_norm_g, mem_norm_g=d_mem_norm_g, w_alt=dw_alt, conv_w=d_conv_w,
                scal=dscal, dn_norm_g=d_dn_norm_g, small=d_small, final_g=d_final_g)


MESH = pl.DeviceIdType.MESH
ANY = pl.BlockSpec(memory_space=pl.ANY)


def _position():
    return lax.axis_index("x"), lax.axis_index("y"), lax.axis_index("c")


def _other_devices(me):
    return [tuple(1 - p if (f >> s) & 1 else p for p, s in zip(me, (2, 1, 0))) for f in range(1, N_DEV)]


def _direct_gather_copies(x_refs, land_refs, send_sems, recv_sems, local_sems):
    me = _position()
    mine = 4 * me[0] + 2 * me[1] + me[2]
    cps = []
    for a, (x_ref, land) in enumerate(zip(x_refs, land_refs)):
        cps.append(pltpu.make_async_copy(x_ref, land.at[mine], local_sems.at[a]))
        for k, peer in enumerate(_other_devices(me)):
            cps.append(pltpu.make_async_remote_copy(
                src_ref=x_ref, dst_ref=land.at[mine], send_sem=send_sems.at[7 * a + k],
                recv_sem=recv_sems.at[7 * a + k], device_id=peer, device_id_type=MESH))
    return cps


def _gather_sems(n):
    return [pltpu.SemaphoreType.DMA((7 * n,)), pltpu.SemaphoreType.DMA((7 * n,)), pltpu.SemaphoreType.DMA((n,))]


def _all_gather(xs, name):
    n = len(xs)

    def body(*refs):
        x_refs, o_refs = refs[:n], refs[n:2 * n]
        send_sems, recv_sems, local_sems = refs[2 * n:]
        x, y, c = _position()
        me, sibling = (x, y, c), (x, y, 1 - c)
        x_nbr, y_nbr, diag = (1 - x, y, c), (x, 1 - y, c), (1 - x, 1 - y, c)
        south = c == 0
        relay_from = tuple(jnp.where(south, a, b) for a, b in zip(y_nbr, x_nbr))
        relay_to = tuple(jnp.where(south, a, b) for a, b in zip(x_nbr, y_nbr))

        def slot(p):
            return 4 * p[0] + 2 * p[1] + p[2]

        def copy(a, k, block, to, src=None):
            dst = o_refs[a].at[slot(block)]
            return pltpu.make_async_remote_copy(
                src_ref=dst if src is None else src, dst_ref=dst, send_sem=send_sems.at[7 * a + k],
                recv_sem=recv_sems.at[7 * a + k], device_id=to, device_id_type=MESH)

        mine = [pltpu.make_async_copy(x_refs[a], o_refs[a].at[slot(me)], local_sems.at[a]) for a in range(n)]
        for cp in mine:
            cp.start()
        sends = []
        for a in range(n):
            sends += [copy(a, 0, me, sibling, src=x_refs[a]), copy(a, 1, me, x_nbr, src=x_refs[a]),
                      copy(a, 2, me, y_nbr, src=x_refs[a])]
        for cp in sends:
            cp.start()
        later = []
        for a in range(n):
            copy(a, 1, x_nbr, me).wait_recv()
            copy(a, 2, y_nbr, me).wait_recv()
            later += [copy(a, 3, relay_from, relay_to), copy(a, 4, x_nbr, sibling), copy(a, 5, y_nbr, sibling)]
            for cp in later[-3:]:
                cp.start()
        for a in range(n):
            copy(a, 3, diag, me).wait_recv()
            later.append(copy(a, 6, diag, sibling))
            later[-1].start()
        for a in range(n):
            copy(a, 0, sibling, me).wait_recv()
            for k, chip in ((4, x_nbr), (5, y_nbr), (6, diag)):
                copy(a, k, (chip[0], chip[1], 1 - c), me).wait_recv()
        for cp in sends + later:
            cp.wait_send()
        for cp in mine:
            cp.wait()

    return pl.pallas_call(
        body,
        name=name,
        in_specs=[ANY] * n,
        out_specs=[ANY] * n,
        out_shape=[jax.ShapeDtypeStruct((N_DEV, *v.shape), v.dtype) for v in xs],
        scratch_shapes=[pltpu.SemaphoreType.DMA((7 * n,)), pltpu.SemaphoreType.DMA((7 * n,)),
                        pltpu.SemaphoreType.DMA((n,))],
    )(*xs)


def _window_view(ref, dest):
    return ref.at[pl.ds(WIN_ROW0[dest], WIN_W), :]


def _chunk_rows(rows, cols):
    return max(ch for ch in range(ROW_TILE, rows + 1, ROW_TILE) if rows % ch == 0 and ch * cols <= (1 << 20))


def _halving_stage(xs, axis, name, out_dtype, windowed=(), gather=()):
    n_arr = len(xs)
    metas = []
    for k, v in enumerate(xs):
        if k in windowed:
            metas.append((N_DEV // 2, WIN_W, v.shape[1]))
        else:
            assert v.shape[1] == 2
            metas.append((v.shape[0], v.shape[2], v.shape[3]))
    chunk = [_chunk_rows(r, c) for (_, r, c) in metas]
    offs = [sum(m[0] for m in metas[:k]) for k in range(n_arr)]
    n_sem = sum(m[0] for m in metas)

    n_g = len(gather)

    def body(*refs):
        x_refs, g_refs = refs[:n_arr], refs[n_arr:n_arr + n_g]
        outs = refs[n_arr + n_g:]
        o_refs, land_refs, gl_refs = outs[:n_arr], outs[n_arr:2 * n_arr], outs[2 * n_arr:2 * n_arr + n_g]
        rest = outs[2 * n_arr + n_g:]
        bufs = rest[:3 * n_arr]
        send_sems, recv_sems, in_sems, out_sems = rest[3 * n_arr:3 * n_arr + 4]
        gathers = _direct_gather_copies(g_refs, gl_refs, *rest[3 * n_arr + 4:]) if n_g else []
        for cp in gathers:
            cp.start()
        pos = dict(zip("xyc", _position()))
        bit = pos[axis]
        peer = tuple(1 - pos[a] if a == axis else pos[a] for a in "xyc")

        def view(k, i, b):
            if k in windowed:
                return _window_view(x_refs[k], 2 * i + b)
            return x_refs[k].at[i, b]

        def add_blocks(k, a_view, b_view, o_view):
            _hbm_add(a_view, b_view, o_view, bufs[3 * k:3 * k + 3], in_sems, out_sems, chunk[k])

        for b in (0, 1):
            @pl.when(bit == b)
            def _(b=b):
                sends = []
                for k in range(n_arr):
                    for i in range(metas[k][0]):
                        cp = pltpu.make_async_remote_copy(
                            src_ref=view(k, i, 1 - b), dst_ref=land_refs[k].at[i], send_sem=send_sems.at[offs[k] + i],
                            recv_sem=recv_sems.at[offs[k] + i], device_id=peer, device_id_type=MESH)
                        cp.start()
                        sends.append(cp)
                idx = 0
                for k in range(n_arr):
                    for i in range(metas[k][0]):
                        sends[idx].wait_recv()
                        add_blocks(k, view(k, i, b), land_refs[k].at[i], o_refs[k].at[i])
                        idx += 1
                for cp in sends:
                    cp.wait_send()

        for cp in gathers:
            cp.wait()

    out_shape = [jax.ShapeDtypeStruct(m, out_dtype) for m in metas]
    land_shape = [jax.ShapeDtypeStruct(m, v.dtype) for m, v in zip(metas, xs)]
    g_shape = [jax.ShapeDtypeStruct((N_DEV, *v.shape), v.dtype) for v in gather]
    scratch = []
    for k in range(n_arr):
        blk = (2, chunk[k], metas[k][2])
        scratch += [pltpu.VMEM(blk, xs[k].dtype)] * 2 + [pltpu.VMEM(blk, out_dtype)]
    scratch += [pltpu.SemaphoreType.DMA((n_sem,)), pltpu.SemaphoreType.DMA((n_sem,)),
                pltpu.SemaphoreType.DMA((2, 2)), pltpu.SemaphoreType.DMA((2,))]
    if n_g:
        scratch += _gather_sems(n_g)
    outs = pl.pallas_call(
        body,
        name=name,
        in_specs=[ANY] * (n_arr + n_g),
        out_specs=[ANY] * (2 * n_arr + n_g),
        out_shape=out_shape + land_shape + g_shape,
        scratch_shapes=scratch,
    )(*xs, *gather)
    return outs[:n_arr], outs[2 * n_arr:]


def _hbm_add(a_view, b_view, o_view, bufs, in_sems, out_sems, ch):
    rows = a_view.shape[0]
    nch = rows // ch
    va, vb, vo = bufs

    def rows_of(j):
        return pl.ds(pl.multiple_of(j * ch, 16), ch)

    def loads(j, s):
        return (pltpu.make_async_copy(a_view.at[rows_of(j), :], va.at[s], in_sems.at[0, s]),
                pltpu.make_async_copy(b_view.at[rows_of(j), :], vb.at[s], in_sems.at[1, s]))

    def store(j, s):
        return pltpu.make_async_copy(vo.at[s], o_view.at[rows_of(j), :], out_sems.at[s])

    for cp in loads(0, 0):
        cp.start()

    def step(j, _):
        s = lax.rem(j, 2)

        @pl.when(j + 1 < nch)
        def _():
            for cp in loads(j + 1, 1 - s):
                cp.start()

        for cp in loads(j, s):
            cp.wait()

        @pl.when(j >= 2)
        def _():
            store(j - 2, s).wait()

        vo[s] = (va[s].astype(F32) + vb[s].astype(F32)).astype(vo.dtype)
        store(j, s).start()
        return 0

    lax.fori_loop(0, nch, step, 0)
    for j in range(max(0, nch - 2), nch):
        store(j, j % 2).wait()


def _xy_stage(xs, first, name):
    n_arr = len(xs)
    if first:
        shapes = [(v.shape[2] // 2, v.shape[3]) for v in xs]
        ins = list(xs)
    else:
        shapes = [(a.shape[1], a.shape[2]) for a, _ in xs]
        ins = [v for pair in xs for v in pair]
    n_blk = 2 if first else 1
    out_dtype = BF16 if first else F32
    chunk = [_chunk_rows(r, c) for (r, c) in shapes]
    n_sem = 2 * n_blk * n_arr

    def body(*refs):
        n_in = len(ins)
        in_refs = refs[:n_in]
        n_out = 2 * n_arr if first else n_arr
        o_refs = refs[n_in:n_in + n_out]
        land = refs[n_in + n_out:n_in + n_out + 2 * n_arr]
        rest = refs[n_in + n_out + 2 * n_arr:]
        bufs = rest[:3 * n_arr]
        send_sems, recv_sems, in_sems, out_sems = rest[3 * n_arr:]
        x, y, c = _position()
        peers = {"x": (1 - x, y, c), "y": (x, 1 - y, c)}
        jobs = []
        for k in range(n_arr):
            r, _ = shapes[k]
            half_a, half_b = pl.ds(0, r), pl.ds(r, r)
            if first:
                src = in_refs[k]
                for i in range(2):
                    jobs.append((k, src.at[i, 1 - y, half_a, :], src.at[i, y, half_a, :], land[2 * k].at[i],
                                 o_refs[2 * k].at[i], "y"))
                    jobs.append((k, src.at[1 - x, i, half_b, :], src.at[x, i, half_b, :], land[2 * k + 1].at[i],
                                 o_refs[2 * k + 1].at[i], "x"))
            else:
                a1, b1 = in_refs[2 * k], in_refs[2 * k + 1]
                jobs.append((k, a1.at[1 - x], a1.at[x], land[2 * k], o_refs[k].at[half_a, :], "x"))
                jobs.append((k, b1.at[1 - y], b1.at[y], land[2 * k + 1], o_refs[k].at[half_b, :], "y"))
        sends = []
        for n, (k, send, _, landing, _, axis) in enumerate(jobs):
            cp = pltpu.make_async_remote_copy(src_ref=send, dst_ref=landing, send_sem=send_sems.at[n],
                                              recv_sem=recv_sems.at[n], device_id=peers[axis], device_id_type=MESH)
            cp.start()
            sends.append(cp)
        for cp, (k, _, kept, landing, out, _) in zip(sends, jobs):
            cp.wait_recv()
            _hbm_add(kept, landing, out, bufs[3 * k:3 * k + 3], in_sems, out_sems, chunk[k])
        for cp in sends:
            cp.wait_send()

    if first:
        out_shape = [jax.ShapeDtypeStruct((2, r, c), BF16) for (r, c) in shapes for _ in range(2)]
        land_shape = out_shape
    else:
        out_shape = [jax.ShapeDtypeStruct((2 * r, c), F32) for (r, c) in shapes]
        land_shape = [jax.ShapeDtypeStruct((r, c), BF16) for (r, c) in shapes for _ in range(2)]
    scratch = []
    for k in range(n_arr):
        scratch += [pltpu.VMEM((2, chunk[k], shapes[k][1]), BF16)] * 2 + [pltpu.VMEM((2, chunk[k], shapes[k][1]), out_dtype)]
    scratch += [pltpu.SemaphoreType.DMA((n_sem,)), pltpu.SemaphoreType.DMA((n_sem,)),
                pltpu.SemaphoreType.DMA((2, 2)), pltpu.SemaphoreType.DMA((2,))]
    outs = pl.pallas_call(
        body,
        name=name,
        in_specs=[ANY] * len(ins),
        out_specs=[ANY] * (len(out_shape) + len(land_shape)),
        out_shape=out_shape + land_shape,
        scratch_shapes=scratch,
    )(*ins)
    outs = outs[:len(out_shape)]
    return [(outs[2 * k], outs[2 * k + 1]) for k in range(n_arr)] if first else list(outs)


def _reduce_scatter(dw_al, blocks, gather=()):
    xs = [dw_al] + [b.reshape(N_DEV // 2, 2, *b.shape[1:]) for b in blocks]
    ys, gathered = _halving_stage(xs, "c", "rs_c", BF16, windowed=(0,), gather=gather)
    pairs = _xy_stage([v.reshape(2, 2, *v.shape[1:]) for v in ys], True, "rs_xy1")
    return _xy_stage(pairs, False, "rs_xy2"), gathered


def _sum_slots(gs, name):
    n = len(gs)

    def body(*refs):
        for g_ref, o_ref in zip(refs[:n], refs[n:]):
            acc = g_ref[0].astype(F32)
            for d in range(1, N_DEV):
                acc = acc + g_ref[d].astype(F32)
            o_ref[...] = acc

    return pl.pallas_call(body, name=name, out_shape=[jax.ShapeDtypeStruct(g.shape[1:], F32) for g in gs])(*gs)


def _assemble_w_al(wins, bas):
    cols = wins.shape[2]
    n_buf = 3
    ends = [WIN_ROW0[d + 1] if d + 1 < N_DEV else WIN_ROW0[d] + WIN_W for d in range(N_DEV)]
    tail = W_AL - ends[-1]

    def body(w_ref, ba_ref, o_ref, buf, zeros, ld_sems, st_sems, ba_sem):
        def load(d):
            return pltpu.make_async_copy(w_ref.at[d], buf.at[d % n_buf], ld_sems.at[d % n_buf])

        def store(d):
            n = ends[d] - WIN_ROW0[d]
            return pltpu.make_async_copy(buf.at[d % n_buf, pl.ds(0, n), :],
                                         o_ref.at[pl.ds(WIN_ROW0[d], n), :], st_sems.at[d % n_buf])

        zeros[...] = jnp.zeros_like(zeros)
        fill = pltpu.make_async_copy(zeros, o_ref.at[pl.ds(ends[-1], tail), :], ba_sem)
        fill.start()
        fill.wait()
        load(0).start()
        for d in range(N_DEV):
            if d + 1 < N_DEV:
                if d + 1 >= n_buf:
                    store(d + 1 - n_buf).wait()
                load(d + 1).start()
            load(d).wait()
            if d > 0:
                ov = WIN_ROW0[d - 1] + WIN_W - WIN_ROW0[d]
                buf[d % n_buf, :ov, :] = buf[d % n_buf, :ov, :] + buf[(d - 1) % n_buf, WIN_W - ov:, :]
            if d == N_DEV - 1:
                ba_copy = pltpu.make_async_copy(
                    ba_ref.at[BA_DEV], buf.at[d % n_buf, pl.ds(WIN_W - N_BA, N_BA), :], ba_sem)
                ba_copy.start()
                ba_copy.wait()
            store(d).start()
        for d in range(N_DEV - n_buf, N_DEV):
            store(d).wait()

    return pl.pallas_call(
        body,
        name="assemble_w_al",
        in_specs=[ANY, ANY],
        out_specs=ANY,
        out_shape=jax.ShapeDtypeStruct((W_AL, cols), wins.dtype),
        scratch_shapes=[pltpu.VMEM((n_buf, WIN_W, cols), wins.dtype), pltpu.VMEM((tail, cols), wins.dtype),
                        pltpu.SemaphoreType.DMA((n_buf,)), pltpu.SemaphoreType.DMA((n_buf,)), pltpu.SemaphoreType.DMA],
    )(wins, bas)


def _adamw_math(w, g, m, v):
    m_new = ADAM_B1 * m + (1.0 - ADAM_B1) * g
    v_new = ADAM_B2 * v + (1.0 - ADAM_B2) * (g * g)
    m_hat = m_new / (1.0 - ADAM_B1 ** ADAM_STEP)
    v_hat = v_new / (1.0 - ADAM_B2 ** ADAM_STEP)
    return -ADAM_LR * (m_hat / (jnp.sqrt(v_hat) + ADAM_EPS) + ADAM_WD * w), m_new, v_new


def _adamw(w, g, m, v, name, tb=134):
    r, _, c = w.shape
    assert r % tb == 0

    def body(w_ref, g_ref, m_ref, v_ref, d_ref, nm_ref, nv_ref):
        d_ref[...], nm_ref[...], nv_ref[...] = _adamw_math(w_ref[...], g_ref[...], m_ref[...], v_ref[...])

    blk = pl.BlockSpec((tb, 1, c), lambda i: (i, 0, 0))
    o = jax.ShapeDtypeStruct(w.shape, F32)
    return pl.pallas_call(body, name=name, grid=(r // tb,), in_specs=[blk] * 4, out_specs=[blk] * 3,
                          out_shape=[o, o, o])(w, g, m, v)


def _adamw_many(ws, gs, ms, vs, name):
    n = len(ws)

    def body(*refs):
        for k in range(n):
            w_ref, g_ref, m_ref, v_ref = (refs[j * n + k] for j in range(4))
            d_ref, nm_ref, nv_ref = (refs[(4 + j) * n + k] for j in range(3))
            d_ref[...], nm_ref[...], nv_ref[...] = _adamw_math(w_ref[...], g_ref[...], m_ref[...], v_ref[...])

    shapes = [jax.ShapeDtypeStruct(w.shape, F32) for w in ws]
    outs = pl.pallas_call(body, name=name, out_shape=shapes * 3)(*ws, *gs, *ms, *vs)
    return outs[:n], outs[n:2 * n], outs[2 * n:]


def _select(me, table):
    return sum(jnp.where(me == d, jnp.int32(v), jnp.int32(0)) for d, v in enumerate(table))


WIN_SHIFT = tuple(SHARD_W * d - WIN_ROW0[d] for d in range(N_DEV))
PAD_L = 64
PAD_R = 64
assert max(WIN_SHIFT) <= PAD_L and WIN_W + N_BA - SHARD_W <= PAD_R


def _shard_to_window(shard_t, me):
    shift = _select(me, WIN_SHIFT)
    padded = jnp.pad(shard_t, ((PAD_L, PAD_R), (0, 0)))
    cols = shard_t.shape[1]
    lo = lax.dynamic_slice(padded, (PAD_L - shift, 0), (WIN_W, cols))
    hi = lax.dynamic_slice(padded, (PAD_L - shift + N_BA, 0), (WIN_W, cols))
    aligned = _select(me, WIN_ROW0) + lax.broadcasted_iota(jnp.int32, (WIN_W, 1), 0)
    return jnp.where(aligned >= ORIG_BA, hi, lo)


def _window_to_shard(win, ba_grad, me):
    shift = _select(me, WIN_SHIFT)
    cols = win.shape[1]
    padded = jnp.pad(win, ((N_BA, PAD_R), (0, 0)))
    lo = lax.dynamic_slice(padded, (N_BA + shift, 0), (SHARD_W, cols))
    hi = lax.dynamic_slice(padded, (shift, 0), (SHARD_W, cols))
    orig = SHARD_W * me + lax.broadcasted_iota(jnp.int32, (SHARD_W, 1), 0)
    ba_full = lax.dynamic_update_slice(jnp.zeros((SHARD_W, cols), win.dtype), ba_grad, (BA_LOCAL, 0))
    return jnp.where(orig < ORIG_BA, lo, jnp.where(orig >= ORIG_BA + N_BA, hi, ba_full))


def _pad_row(v, width=D_MODEL):
    v = v.reshape(1, -1)
    return jnp.pad(v, ((0, 0), (0, width - v.shape[1])))


def kernel(x, mem, norm_g, mem_norm_g, w_in, conv_w, a_log, dt_bias, dn_norm_g, w_mem_kv, w_br_dn, w_br_sb, w_br_mem, w_out, final_g, loss_target, m_norm_g, m_mem_norm_g, m_w_in, m_conv_w, m_a_log, m_dt_bias, m_dn_norm_g, m_w_mem_kv, m_w_br_dn, m_w_br_sb, m_w_br_mem, m_w_out, m_final_g, v_norm_g, v_mem_norm_g, v_w_in, v_conv_w, v_a_log, v_dt_bias, v_dn_norm_g, v_w_mem_kv, v_w_br_dn, v_w_br_sb, v_w_br_mem, v_w_out, v_final_g):
    xi, yi, ci = _position()
    me = 4 * xi + 2 * yi + ci

    shard_t = w_in[0].T
    win = _shard_to_window(shard_t, me).astype(BF16)
    ba = shard_t[BA_LOCAL:BA_LOCAL + N_BA, :].astype(BF16)
    g_win, g_ba = _all_gather([win, ba], "gather_weights")
    w_alt = _assemble_w_al(g_win, g_ba)

    shards = [w_mem_kv[0].astype(BF16), w_br_dn[0].astype(BF16), w_br_sb[0].astype(BF16), w_out[0].astype(BF16),
              w_br_mem[0].astype(BF16), conv_w[0]]
    r = _local_step(x[0], mem[0], loss_target[0], norm_g, mem_norm_g, w_alt, _pad_row(a_log, LANE),
                    _pad_row(dt_bias, LANE), dn_norm_g, final_g.reshape(1, D_MODEL), shards)

    dw_alt = r["w_alt"]
    parts = [r["norm_g"], r["mem_norm_g"], r["final_g"], r["dn_norm_g"], r["scal"], r["loss"], r["conv_w"],
             dw_alt[O_BA:O_BA + N_BA, :].astype(F32)]
    (g_win,), gathered = _reduce_scatter(dw_alt, [], gather=parts)
    rows_d = D_MODEL // N_DEV
    g_small = r["small"]
    g_dn, g_sb, g_out = (g_small[k * rows_d:(k + 1) * rows_d] for k in range(3))
    g_kv = g_small[3 * rows_d:3 * rows_d + rows_d // 2].reshape(rows_d, 2 * MEM_W)
    g_mem = g_small[3 * rows_d + rows_d // 2:].reshape(MEM_W, rows_d)
    s_norm_g, s_mem_norm_g, s_final_g, s_dn_norm_g, s_scal, s_loss, s_conv, s_ba = _sum_slots(gathered, "sum_small")
    loss = s_loss[0, 0]
    cw = conv_w.shape[2]
    g_conv = lax.dynamic_slice(s_conv, (0, cw * me), (CONV_K, cw))
    g_w_in_t = _window_to_shard(g_win, s_ba, me)
    grads = dict(norm_g=s_norm_g, mem_norm_g=s_mem_norm_g, w_in=g_w_in_t.T[None], conv_w=g_conv[None],
                 a_log=s_scal[0:1, :N_HEADS], dt_bias=s_scal[1:2, :N_HEADS], dn_norm_g=s_dn_norm_g, w_mem_kv=g_kv[None],
                 w_br_dn=g_dn[None], w_br_sb=g_sb[None], w_br_mem=g_mem[None], w_out=g_out[None],
                 final_g=s_final_g.reshape(D_MODEL))

    params = dict(norm_g=(norm_g, m_norm_g, v_norm_g), mem_norm_g=(mem_norm_g, m_mem_norm_g, v_mem_norm_g),
                  w_in=(w_in, m_w_in, v_w_in), conv_w=(conv_w, m_conv_w, v_conv_w), a_log=(a_log, m_a_log, v_a_log),
                  dt_bias=(dt_bias, m_dt_bias, v_dt_bias), dn_norm_g=(dn_norm_g, m_dn_norm_g, v_dn_norm_g),
                  w_mem_kv=(w_mem_kv, m_w_mem_kv, v_w_mem_kv), w_br_dn=(w_br_dn, m_w_br_dn, v_w_br_dn),
                  w_br_sb=(w_br_sb, m_w_br_sb, v_w_br_sb), w_br_mem=(w_br_mem, m_w_br_mem, v_w_br_mem),
                  w_out=(w_out, m_w_out, v_w_out), final_g=(final_g, m_final_g, v_final_g))
    order = list(params)
    deltas, new_m, new_v = {}, {}, {}
    deltas["w_in"], new_m["w_in"], new_v["w_in"] = (jnp.transpose(o, (1, 2, 0)) for o in _adamw(
        jnp.transpose(w_in, (2, 0, 1)), g_w_in_t[:, None, :], jnp.transpose(m_w_in, (2, 0, 1)),
        jnp.transpose(v_w_in, (2, 0, 1)), "adamw_w_in"))
    rest = [nm for nm in order if nm != "w_in"]

    def two_d(a):
        return a.reshape(1, -1) if a.ndim == 1 else a

    d_l, m_l, v_l = _adamw_many([two_d(params[nm][0]) for nm in rest], [two_d(grads[nm]) for nm in rest],
                                [two_d(params[nm][1]) for nm in rest], [two_d(params[nm][2]) for nm in rest], "adamw_rest")
    for k, nm in enumerate(rest):
        shp = params[nm][0].shape
        deltas[nm], new_m[nm], new_v[nm] = d_l[k].reshape(shp), m_l[k].reshape(shp), v_l[k].reshape(shp)
    return (loss, r["grad_x"][None], *[grads[nm] for nm in order], *[deltas[nm] for nm in order],
            *[new_m[nm] for nm in order], *[new_v[nm] for nm in order])
```

```python
import functools
import math

import jax
import jax.numpy as jnp
from jax import lax
from jax.experimental import pallas as pl
from jax.experimental.pallas import tpu as pltpu

F32 = jnp.float32
BF16 = jnp.bfloat16

D_MODEL = 1024
N_DEV = 8
N_HEADS = 8
D_HEAD = 128
DN_CHUNK = 64
CONV_K = 4
MEM_LEN = 256
MEM_HEADS = 4
MEM_DH = 64
MEM_W = MEM_HEADS * MEM_DH
NORM_EPS = 1e-6
IN_WIDTH = 11792
SHARD_W = IN_WIDTH // N_DEV

LANE = 128
SUPER = 2 * DN_CHUNK

O_QKV_DN = 0
O_Z_DN = 3072
O_QKV_SB = 4096
O_Z_SB = 7168
O_MQ = 8192
O_MZ = 8448
O_GATES = 8704
O_BA = 11776
W_AL = 11904
ORIG_BA = 4096
N_BA = 16

BA_DEV = ORIG_BA // SHARD_W
BA_LOCAL = ORIG_BA - BA_DEV * SHARD_W


def _aligned_col(o):
    return o if o < ORIG_BA else o - N_BA


ROW_TILE = 16
WIN_W = 1504
WIN_ROW0 = tuple(_aligned_col(SHARD_W * d) // ROW_TILE * ROW_TILE for d in range(N_DEV))
assert not any(ORIG_BA <= SHARD_W * d < ORIG_BA + N_BA for d in range(N_DEV))
assert all(WIN_ROW0[d] + WIN_W >= _aligned_col(SHARD_W * (d + 1) - 1) + 1 for d in range(N_DEV))
assert all(WIN_ROW0[d + 1] <= WIN_ROW0[d] + WIN_W for d in range(N_DEV - 1))
assert WIN_ROW0[-1] + WIN_W == O_BA + N_BA

ADAM_LR = 0.001
ADAM_B1 = 0.9
ADAM_B2 = 0.999
ADAM_EPS = 1e-08
ADAM_WD = 0.01
ADAM_STEP = 10

NN = (((1,), (0,)), ((), ()))
NT = (((1,), (1,)), ((), ()))
TN = (((0,), (0,)), ((), ()))


def _dot(a, b, dims):
    return lax.dot_general(a.astype(BF16), b.astype(BF16), dims, preferred_element_type=F32)


def _split2(a):
    hi = a.astype(BF16)
    lo = (a - hi.astype(F32)).astype(BF16)
    return hi, lo


def _dot3(a, b, dims):
    ah, al = _split2(a)
    bh, bl = _split2(b)
    d = functools.partial(lax.dot_general, dimension_numbers=dims, preferred_element_type=F32)
    return d(ah, bh) + (d(ah, bl) + d(al, bh))


def _sel_dot_impl(sel01, x, dims):
    sel = sel01.astype(BF16)
    h1 = x.astype(BF16)
    r1 = x - h1.astype(F32)
    h2 = r1.astype(BF16)
    h3 = (r1 - h2.astype(F32)).astype(BF16)
    d = functools.partial(lax.dot_general, dimension_numbers=dims, preferred_element_type=F32)
    return d(sel, h1) + (d(sel, h2) + d(sel, h3))


@jax.custom_vjp
def _sel_dot(sel01, x):
    return _sel_dot_impl(sel01, x, NN)


_sel_dot.defvjp(lambda s, x: (_sel_dot(s, x), s),
                lambda s, g: (jnp.zeros_like(s), _sel_dot_impl(s, g, TN)))


def _make_mm(dotfn):
    @jax.custom_vjp
    def nn(a, b):
        return dotfn(a, b, NN)

    @jax.custom_vjp
    def nt(a, b):
        return dotfn(a, b, NT)

    @jax.custom_vjp
    def tn(a, b):
        return dotfn(a, b, TN)

    nn.defvjp(lambda a, b: (nn(a, b), (a, b)), lambda r, g: (nt(g, r[1]), tn(r[0], g)))
    nt.defvjp(lambda a, b: (nt(a, b), (a, b)), lambda r, g: (nn(g, r[1]), tn(g, r[0])))
    tn.defvjp(lambda a, b: (tn(a, b), (a, b)), lambda r, g: (nt(r[1], g), nn(r[0], g)))
    return nn, nt, tn


mm_nn, mm_nt, mm_tn = _make_mm(_dot)
mm3_nn, mm3_nt, mm3_tn = _make_mm(_dot3)


def _sigmoid(x):
    return jax.nn.sigmoid(x)


def _silu(x):
    return x * _sigmoid(x)


def _softplus_parts(x):
    sp = jnp.log1p(jnp.exp(-jnp.abs(x)))
    return jnp.maximum(x, 0.0) + sp, jnp.maximum(-x, 0.0) + sp


def _rmsnorm(x, g):
    return x * lax.rsqrt(jnp.mean(x * x, axis=-1, keepdims=True) + NORM_EPS) * g


def _iota2(shape, dim):
    return lax.broadcasted_iota(jnp.int32, shape, dim)


def _div64(i):
    return lax.shift_right_logical(i, jnp.full(i.shape, 6, jnp.int32))


def _each(f, *lists):
    return [f(*a) for a in zip(*lists)]


@jax.custom_vjp
def _inv_unit_lower(ms):
    n = ms[0].shape[0]
    eye = (_iota2((n, n), 0) == _iota2((n, n), 1)).astype(F32)
    rs = [eye - m for m in ms]
    ps = ms
    for _ in range(5):
        ps = _each(mm3_nn, ps, ps)
        rs = _each(lambda r, p: r + mm_nn(r, p), rs, ps)
    return rs


def _inv_fwd(ms):
    rs = _inv_unit_lower(ms)
    return rs, rs


def _inv_bwd(rs, gs):
    ts = _each(mm_tn, rs, gs)
    return (_each(lambda t, r: -mm_nt(t, r), ts, rs),)


_inv_unit_lower.defvjp(_inv_fwd, _inv_bwd)


def _dn_block(cq, ck, cv, bcol, acol, zt, alog, dtb, gn, s0):
    n = SUPER
    h = DN_CHUNK
    row = _iota2((n, n), 0)
    col = _iota2((n, n), 1)
    same = _div64(row) == _div64(col)
    incl = jnp.logical_and(same, row >= col)
    strict = jnp.logical_and(same, row > col)
    incl_f = incl.astype(F32)

    qn = _each(lambda x: x * lax.rsqrt(jnp.sum(x * x, axis=-1, keepdims=True) + NORM_EPS) * (D_HEAD ** -0.5), cq)
    kn = _each(lambda x: x * lax.rsqrt(jnp.sum(x * x, axis=-1, keepdims=True) + NORM_EPS), ck)
    beta = _each(_sigmoid, bcol)
    g = _each(lambda al, ac, dt: -(jnp.exp(al) * _softplus_parts(ac + dt)[0]), alog, acol, dtb)
    gcum = _each(lambda x: _sel_dot(incl_f, jnp.broadcast_to(x, (n, n))), g)
    gam_incl = _each(lambda x: jnp.where(incl, jnp.exp(jnp.where(incl, x - x.T, 0.0)), 0.0), gcum)
    kk = _each(mm_nt, kn, kn)
    t_inv = _inv_unit_lower(_each(lambda b, x, gm: b * x * jnp.where(strict, gm, 0.0), beta, kk, gam_incl))
    eg = _each(jnp.exp, gcum)
    u = _each(lambda t, v, b: mm_nn(t, v * b), t_inv, cv, beta)
    w = _each(lambda t, k, b, e: mm_nn(t, k * (b * e)), t_inv, kn, beta, eg)
    a_intra = _each(lambda q, k, gm: mm_nt(q, k) * gm, qn, kn, gam_incl)
    q_dec = _each(lambda q, e: q * e, qn, eg)
    last0 = _each(lambda x: x[h - 1:h, :], gcum)
    last1 = _each(lambda x: x[n - 1:n, :], gcum)
    k_dec = _each(lambda k, x, l0, l1: k * jnp.exp(jnp.concatenate(
        [jnp.broadcast_to(l0, (h, n)), jnp.broadcast_to(l1, (h, n))], axis=0) - x), kn, gcum, last0, last1)
    v0 = _each(lambda uu, ww, s: uu[:h] - mm_nn(ww[:h], s), u, w, s0)
    o0 = _each(lambda q, s: mm_nn(q[:h], s), q_dec, s0)
    s1 = _each(lambda s, l0, k, v: s * jnp.exp(l0) + mm_tn(k[:h], v), s0, last0, k_dec, v0)
    v1 = _each(lambda uu, ww, s: uu[h:] - mm_nn(ww[h:], s), u, w, s1)
    o1 = _each(lambda q, s: mm_nn(q[h:], s), q_dec, s1)
    s2 = _each(lambda s, l1, k, v: s * jnp.exp(l1) + mm_tn(k[h:], v), s1, last1, k_dec, v1)
    o = _each(lambda a, b, am, x, y: jnp.concatenate([a, b], axis=0) + mm_nn(am, jnp.concatenate([x, y], axis=0)),
              o0, o1, a_intra, v0, v1)
    out = _each(lambda x, z: _rmsnorm(x, gn) * _silu(z), o, zt)
    return out, s2


def _mem_fn(mq, mz, mkv):
    mk = mkv[:, :MEM_W]
    mv = mkv[:, MEM_W:]
    lane = _iota2((1, MEM_W), 1)
    out = jnp.zeros(mq.shape, F32)
    for hd in range(MEM_HEADS):
        hm = (_div64(lane) == hd).astype(F32)
        s = mm_nt(mq * hm, mk) * (1.0 / math.sqrt(MEM_DH))
        s = s - jnp.max(s, axis=-1, keepdims=True)
        e = jnp.exp(s)
        p = e / jnp.sum(e, axis=-1, keepdims=True)
        out = out + mm_nn(p, mv) * hm
    return out * _silu(mz)


def _loss_fn(x, mo, fg, tgt):
    y = _rmsnorm(x + mo, fg)
    err = y - tgt
    return 0.5 * jnp.sum(jnp.mean(err * err, axis=-1, keepdims=True), axis=0, keepdims=True)


def _matmul_tn(a, b, out_dtype, tm, tn, tk, name):
    kdim, m = a.shape
    n = b.shape[1]
    tm, tn, tk = min(tm, m), min(tn, n), min(tk, kdim)
    assert m % tm == 0 and n % tn == 0 and kdim % tk == 0
    nk = kdim // tk

    def body(a_ref, b_ref, o_ref, acc_ref):
        k = pl.program_id(2)
        part = _dot(a_ref[...], b_ref[...], TN)

        @pl.when(k == 0)
        def _():
            acc_ref[...] = part

        @pl.when(k > 0)
        def _():
            acc_ref[...] += part

        @pl.when(k == nk - 1)
        def _():
            o_ref[...] = acc_ref[...].astype(o_ref.dtype)

    return pl.pallas_call(
        body,
        name=name,
        grid=(m // tm, n // tn, nk),
        in_specs=[pl.BlockSpec((tk, tm), lambda i, j, k: (k, i)), pl.BlockSpec((tk, tn), lambda i, j, k: (k, j))],
        out_specs=pl.BlockSpec((tm, tn), lambda i, j, k: (i, j)),
        out_shape=jax.ShapeDtypeStruct((m, n), out_dtype),
        scratch_shapes=[pltpu.VMEM((tm, tn), F32)],
        compiler_params=pltpu.CompilerParams(dimension_semantics=("parallel", "parallel", "arbitrary")),
    )(a, b)


def _norm_in(x, g, tm=256):
    t = x.shape[0]

    def body(x_ref, g_ref, h_ref):
        h_ref[...] = _rmsnorm(x_ref[...], g_ref[...]).astype(BF16)

    return pl.pallas_call(
        body,
        name="norm_in",
        grid=(t // tm,),
        in_specs=[pl.BlockSpec((tm, D_MODEL), lambda i: (i, 0)), pl.BlockSpec((1, D_MODEL), lambda i: (0, 0))],
        out_specs=pl.BlockSpec((tm, D_MODEL), lambda i: (i, 0)),
        out_shape=jax.ShapeDtypeStruct((t, D_MODEL), BF16),
    )(x, g)


def _dw_alt(parts, h, tm=512):
    t = h.shape[0]
    n_p = len(parts)
    widths = [p.shape[1] for p in parts]
    offs = [sum(widths[:s]) for s in range(n_p)]
    total = sum(widths)
    n_tiles = pl.cdiv(total, tm)

    specs = []
    for off, w in zip(offs, widths):
        if w >= tm:
            assert w % tm == 0 and off % tm == 0
            specs.append(pl.BlockSpec(
                (t, tm), lambda i, lo=off // tm, n=w // tm: (0, jnp.minimum(jnp.maximum(i - lo, 0), n - 1))))
        else:
            assert off // tm == (off + w - 1) // tm
            specs.append(pl.BlockSpec((t, w), lambda i: (0, 0), pipeline_mode=pl.Buffered(1)))

    def body(*refs):
        a_refs, h_ref, o_ref = refs[:n_p], refs[n_p], refs[n_p + 1]
        i = pl.program_id(0)
        for a_ref, off, w in zip(a_refs, offs, widths):
            if w >= tm:
                @pl.when(jnp.logical_and(i >= off // tm, i < (off + w) // tm))
                def _(a_ref=a_ref):
                    o_ref[...] = _dot(a_ref[...], h_ref[...], TN).astype(o_ref.dtype)
            else:
                @pl.when(i == off // tm)
                def _(a_ref=a_ref, r0=off % tm, w=w):
                    o_ref[r0:r0 + w, :] = _dot(a_ref[...], h_ref[...], TN).astype(o_ref.dtype)
        if total % tm:
            @pl.when(i == n_tiles - 1)
            def _():
                o_ref[total % tm:, :] = jnp.zeros((tm - total % tm, D_MODEL), o_ref.dtype)

    return pl.pallas_call(
        body,
        name="dw_alt",
        grid=(n_tiles,),
        in_specs=specs + [pl.BlockSpec((t, D_MODEL), lambda i: (0, 0), pipeline_mode=pl.Buffered(1))],
        out_specs=pl.BlockSpec((tm, D_MODEL), lambda i: (i, 0)),
        out_shape=jax.ShapeDtypeStruct((n_tiles * tm, D_MODEL), BF16),
    )(*parts, h)


def _grad_x(parts, w_alt, x, g, dres, tm=256):
    t = x.shape[0]
    tm = min(tm, t)
    n_p = len(parts)
    assert sum(p.shape[1] for p in parts) == w_alt.shape[0] and t % tm == 0

    def body(*refs):
        a_refs = refs[:n_p]
        w_ref, x_ref, g_ref, dres_ref, dx_ref, dg_ref = refs[n_p:]
        dproj = jnp.concatenate([a_ref[...] for a_ref in a_refs], axis=1)
        dh = _dot(dproj, w_ref[...], NN)

        @pl.when(pl.program_id(0) == 0)
        def _():
            dg_ref[...] = jnp.zeros_like(dg_ref)

        _, vjp = jax.vjp(_rmsnorm, x_ref[...], g_ref[...])
        dx, dg = vjp(dh)
        dx_ref[...] = dx + dres_ref[...]
        dg_ref[...] += dg

    row = pl.BlockSpec((tm, D_MODEL), lambda i: (i, 0))
    vec = pl.BlockSpec((1, D_MODEL), lambda i: (0, 0))
    return pl.pallas_call(
        body,
        name="grad_x",
        grid=(t // tm,),
        in_specs=[pl.BlockSpec((tm, p.shape[1]), lambda i: (i, 0)) for p in parts]
        + [pl.BlockSpec(w_alt.shape, lambda i: (0, 0), pipeline_mode=pl.Buffered(1)), row, vec, row],
        out_specs=[row, vec],
        out_shape=[jax.ShapeDtypeStruct((t, D_MODEL), F32), jax.ShapeDtypeStruct((1, D_MODEL), F32)],
    )(*parts, w_alt, x, g, dres)


def _block_tail(proj, o_dn, o_sb, o_m, x, tgt, w_br_dn, w_br_sb, w_br_mem, w_out, fg, tm=256):
    t = x.shape[0]
    tm = min(tm, t)
    gw = 512
    n_g = 3 * D_MODEL // gw

    def body(*refs):
        g_refs = refs[:n_g]
        (odn_ref, osb_ref, om_ref, x_ref, t_ref, wdn_ref, wsb_ref, wm_ref, wo_ref, fg_ref, loss_ref, dout_ref, dfg_ref,
         mg_ref, dyd_ref, dys_ref, dym_ref, dg_ref, dod_ref, dos_ref, dom_ref) = refs[n_g:]
        y = [_dot(odn_ref[...], wdn_ref[...], NN), _dot(osb_ref[...], wsb_ref[...], NN),
             _dot(om_ref[...], wm_ref[...], NN)]
        s = [_sigmoid(jnp.concatenate([g_refs[2 * k][...], g_refs[2 * k + 1][...]], axis=1)) for k in range(3)]
        merged16 = (s[0] * y[0] + s[1] * y[1] + s[2] * y[2]).astype(BF16)
        mg_ref[...] = merged16
        mo = _dot(merged16, wo_ref[...], NN)
        loss, vjp = jax.vjp(_loss_fn, x_ref[...], mo, fg_ref[...], t_ref[...])
        _, dout, dfg, _ = vjp(jnp.ones((1, 1), F32))

        @pl.when(pl.program_id(0) == 0)
        def _():
            loss_ref[...] = jnp.zeros_like(loss_ref)
            dfg_ref[...] = jnp.zeros_like(dfg_ref)

        loss_ref[...] += jnp.broadcast_to(loss, loss_ref.shape)
        dfg_ref[...] += dfg
        dout_ref[...] = dout
        dmerged = _dot(dout, wo_ref[...], NT)
        dy = [(sk * dmerged).astype(BF16) for sk in s]
        dyd_ref[...], dys_ref[...], dym_ref[...] = dy
        dg_ref[...] = jnp.concatenate([dmerged * yk * (sk * (1.0 - sk)) for yk, sk in zip(y, s)], axis=1).astype(BF16)
        dod_ref[...] = _dot(dy[0], wdn_ref[...], NT).astype(BF16)
        dos_ref[...] = _dot(dy[1], wsb_ref[...], NT).astype(BF16)
        dom_ref[...] = _dot(dy[2], wm_ref[...], NT).astype(BF16)

    gates = [pl.BlockSpec((tm, gw), lambda i, j=j: (i, O_GATES // gw + j)) for j in range(n_g)]
    row = pl.BlockSpec((tm, D_MODEL), lambda i: (i, 0))
    rowm = pl.BlockSpec((tm, MEM_W), lambda i: (i, 0))
    vec = pl.BlockSpec((1, D_MODEL), lambda i: (0, 0))

    def whole(a):
        return pl.BlockSpec(a.shape, lambda i: (0, 0), pipeline_mode=pl.Buffered(1))

    def bf(c):
        return jax.ShapeDtypeStruct((t, c), BF16)

    return pl.pallas_call(
        body,
        name="block_tail",
        grid=(t // tm,),
        in_specs=gates + [row, row, rowm, row, row, whole(w_br_dn), whole(w_br_sb), whole(w_br_mem), whole(w_out), vec],
        out_specs=[pl.BlockSpec((1, LANE), lambda i: (0, 0)), row, vec, row, row, row, row,
                   pl.BlockSpec((tm, 3 * D_MODEL), lambda i: (i, 0)), row, row, rowm],
        out_shape=[jax.ShapeDtypeStruct((1, LANE), F32), jax.ShapeDtypeStruct((t, D_MODEL), F32),
                   jax.ShapeDtypeStruct((1, D_MODEL), F32), bf(D_MODEL), bf(D_MODEL), bf(D_MODEL), bf(D_MODEL),
                   bf(3 * D_MODEL), bf(D_MODEL), bf(D_MODEL), bf(MEM_W)],
    )(*([proj] * n_g), o_dn, o_sb, o_m, x, tgt, w_br_dn, w_br_sb, w_br_mem, w_out, fg)


def _shift_rows(x, s):
    t = x.shape[0]
    if s == 0:
        return x
    rolled = pltpu.roll(x, s % t, 0)
    row = _iota2(x.shape, 0)
    keep = row >= s if s > 0 else row < t + s
    return jnp.where(keep, rolled, 0.0)


def _conv_pre(x, w):
    return sum(_shift_rows(x, CONV_K - 1 - j) * w[j:j + 1, :] for j in range(CONV_K))


CONV_TC = 256


def _dn_conv(proj, conv_w):
    t = proj.shape[0]
    nb = 3 * D_MODEL // CONV_TC

    def body(x_ref, w_ref, c_ref):
        c_ref[...] = _silu(_conv_pre(x_ref[...], w_ref[...]))

    return pl.pallas_call(
        body,
        name="dn_conv",
        grid=(nb,),
        in_specs=[pl.BlockSpec((t, CONV_TC), lambda j: (0, j)), pl.BlockSpec((CONV_K, CONV_TC), lambda j: (0, j))],
        out_specs=pl.BlockSpec((t, CONV_TC), lambda j: (0, j)),
        out_shape=jax.ShapeDtypeStruct((t, 3 * D_MODEL), F32),
    )(proj, conv_w)


def _dn_conv_bwd(proj, conv_w, dc, part):
    t = proj.shape[0]
    nb = D_MODEL // CONV_TC
    b0 = part * nb

    def body(x_ref, w_ref, dc_ref, dx_ref, dw_ref):
        x = x_ref[...]
        w = w_ref[...]
        pre = _conv_pre(x, w)
        sg = _sigmoid(pre)
        dpre = dc_ref[...] * (sg * (1.0 + pre * (1.0 - sg)))
        ahead = [_shift_rows(dpre, -(CONV_K - 1 - j)) for j in range(CONV_K)]
        dx_ref[...] = sum(a * w[j:j + 1, :] for j, a in enumerate(ahead)).astype(BF16)
        dw_ref[...] = jnp.concatenate([jnp.sum(a * x, axis=0, keepdims=True) for a in ahead], axis=0)

    blk = pl.BlockSpec((t, CONV_TC), lambda j: (0, j))
    return pl.pallas_call(
        body,
        name=f"dn_conv_bwd{part}",
        grid=(nb,),
        in_specs=[pl.BlockSpec((t, CONV_TC), lambda j: (0, b0 + j)),
                  pl.BlockSpec((CONV_K, CONV_TC), lambda j: (0, b0 + j)), blk],
        out_specs=[blk, pl.BlockSpec((CONV_K, CONV_TC), lambda j: (0, j))],
        out_shape=[jax.ShapeDtypeStruct((t, D_MODEL), BF16), jax.ShapeDtypeStruct((CONV_K, D_MODEL), F32)],
    )(proj, conv_w, dc)


def _ba_columns(ba, hd):
    lane = _iota2(ba.shape, 1)
    bcol = jnp.sum(jnp.where(lane == hd, ba, 0.0), axis=1, keepdims=True)
    acol = jnp.sum(jnp.where(lane == N_HEADS + hd, ba, 0.0), axis=1, keepdims=True)
    return bcol, acol


def _head_scalar(row, hd):
    lane = _iota2(row.shape, 1)
    return jnp.sum(jnp.where(lane == hd, row, 0.0), axis=1, keepdims=True)


DN_HP = 8


def _dn_inputs(cq, ck, cv, ba_ref, z_ref, alog_ref, dtb_ref, heads, lanes):
    ba = ba_ref[...]
    cols = [_ba_columns(ba, hd) for hd in heads]
    return ([cq[:, ln] for ln in lanes], [ck[:, ln] for ln in lanes], [cv[:, ln] for ln in lanes],
            [c[0] for c in cols], [c[1] for c in cols], [z_ref[:, ln] for ln in lanes],
            [_head_scalar(alog_ref[...], hd) for hd in heads], [_head_scalar(dtb_ref[...], hd) for hd in heads])


def _dn_specs(nblk, reverse):
    w = DN_HP * LANE
    nq = D_MODEL // w

    def row(i):
        return nblk - 1 - i if reverse else i

    def colblk(b0):
        return pl.BlockSpec((SUPER, w), lambda i, h: (row(i), b0 + h))

    ba = pl.BlockSpec((SUPER, LANE), lambda i, h: (row(i), O_BA // LANE))
    vec = pl.BlockSpec((1, LANE), lambda i, h: (0, 0))
    st = pl.BlockSpec((1, DN_HP, D_HEAD, D_HEAD), lambda i, h: (row(i), h, 0, 0))
    return colblk, nq, ba, vec, st


def _dn_fwd(c, proj, alog_row, dtb_row, gn):
    t = c.shape[0]
    nblk = t // SUPER
    colblk, nq, ba, vec, st = _dn_specs(nblk, False)

    def body(cq, ck, cv, ba_ref, z_ref, alog_ref, dtb_ref, gn_ref, o_ref, s_ref, state):
        @pl.when(jnp.logical_and(pl.program_id(0) == 0, pl.program_id(1) == 0))
        def _():
            state[...] = jnp.zeros_like(state)

        heads = [pl.program_id(1) * DN_HP + j for j in range(DN_HP)]
        lanes = [slice(j * LANE, (j + 1) * LANE) for j in range(DN_HP)]
        s0 = [state[hd] for hd in heads]
        outs, s2 = _dn_block(*_dn_inputs(cq, ck, cv, ba_ref, z_ref, alog_ref, dtb_ref, heads, lanes), gn_ref[...], s0)
        for j, (hd, ln) in enumerate(zip(heads, lanes)):
            s_ref[0, j] = s0[j]
            o_ref[:, ln] = outs[j].astype(BF16)
            state[hd] = s2[j]

    return pl.pallas_call(
        body,
        name="dn_fwd",
        grid=(nblk, N_HEADS // DN_HP),
        in_specs=[colblk(0), colblk(nq), colblk(2 * nq), ba, colblk(O_Z_DN // (DN_HP * LANE)), vec, vec, vec],
        out_specs=[colblk(0), st],
        out_shape=[jax.ShapeDtypeStruct((t, D_MODEL), BF16),
                   jax.ShapeDtypeStruct((nblk, N_HEADS, D_HEAD, D_HEAD), F32)],
        scratch_shapes=[pltpu.VMEM((N_HEADS, D_HEAD, D_HEAD), F32)],
    )(c, c, c, proj, proj, alog_row, dtb_row, gn)


def _dn_bwd(c, proj, alog_row, dtb_row, gn, states, do):
    t = c.shape[0]
    nblk = t // SUPER
    colblk, nq, ba, vec, st = _dn_specs(nblk, True)

    def body(cq, ck, cv, ba_ref, z_ref, alog_ref, dtb_ref, gn_ref, s_ref, do_ref,
             dq_ref, dk_ref, dv_ref, dz_ref, dba_ref, dsc_ref, dgn_ref, dstate):
        i = pl.program_id(0)
        hq = pl.program_id(1)

        @pl.when(jnp.logical_and(i == 0, hq == 0))
        def _():
            dstate[...] = jnp.zeros_like(dstate)
            dsc_ref[...] = jnp.zeros_like(dsc_ref)
            dgn_ref[...] = jnp.zeros_like(dgn_ref)

        @pl.when(hq == 0)
        def _():
            dba_ref[...] = jnp.zeros_like(dba_ref)

        lane = _iota2((SUPER, LANE), 1)
        lane1 = _iota2((1, LANE), 1)
        heads = [hq * DN_HP + j for j in range(DN_HP)]
        lanes = [slice(j * LANE, (j + 1) * LANE) for j in range(DN_HP)]
        ds_in = [dstate[hd] for hd in heads]
        s_in = [s_ref[0, j] for j in range(DN_HP)]
        _, vjp = jax.vjp(_dn_block, *_dn_inputs(cq, ck, cv, ba_ref, z_ref, alog_ref, dtb_ref, heads, lanes),
                         gn_ref[...], s_in)
        dq, dk, dv, dbc, dac, dz, dal, ddt, dgn, ds0 = vjp(([do_ref[:, ln].astype(F32) for ln in lanes], ds_in))
        dba = jnp.zeros((SUPER, LANE), F32)
        dal_row = jnp.zeros((1, LANE), F32)
        ddt_row = jnp.zeros((1, LANE), F32)
        for j, (hd, ln) in enumerate(zip(heads, lanes)):
            dq_ref[:, ln] = dq[j]
            dk_ref[:, ln] = dk[j]
            dv_ref[:, ln] = dv[j]
            dz_ref[:, ln] = dz[j].astype(BF16)
            dstate[hd] = ds0[j]
            dba = dba + jnp.where(lane == hd, dbc[j], 0.0) + jnp.where(lane == N_HEADS + hd, dac[j], 0.0)
            dal_row = dal_row + jnp.where(lane1 == hd, dal[j], 0.0)
            ddt_row = ddt_row + jnp.where(lane1 == hd, ddt[j], 0.0)
        dba_ref[...] += dba
        dsc_ref[0:1, :] += dal_row
        dsc_ref[1:2, :] += ddt_row
        dgn_ref[...] += dgn

    outs = pl.pallas_call(
        body,
        name="dn_bwd",
        grid=(nblk, N_HEADS // DN_HP),
        in_specs=[colblk(0), colblk(nq), colblk(2 * nq), ba, colblk(O_Z_DN // (DN_HP * LANE)), vec, vec, vec, st,
                  colblk(0)],
        out_specs=[colblk(0), colblk(0), colblk(0), colblk(0),
                   pl.BlockSpec((SUPER, LANE), lambda i, h: (nblk - 1 - i, 0)),
                   pl.BlockSpec((2, LANE), lambda i, h: (0, 0)), vec],
        out_shape=[jax.ShapeDtypeStruct((t, D_MODEL), F32)] * 3
        + [jax.ShapeDtypeStruct((t, D_MODEL), BF16), jax.ShapeDtypeStruct((t, LANE), F32),
           jax.ShapeDtypeStruct((2, LANE), F32), jax.ShapeDtypeStruct((1, LANE), F32)],
        scratch_shapes=[pltpu.VMEM((N_HEADS, D_HEAD, D_HEAD), F32)],
    )(c, c, c, proj, proj, alog_row, dtb_row, gn, states, do)
    return outs


SB_TQ = 256
SB_TK = 256
SB_HP_FWD = 8
SB_HP_BWD = 4


def _sb_logits(z, mask):
    sp = jnp.log(1.0 + jnp.exp(-jnp.abs(z)))
    lf_raw = -(jnp.maximum(z, 0.0) + sp)
    lb = lf_raw + z
    lf = lf_raw if mask is None else jnp.where(mask, lf_raw, 0.0)
    return lb, lf_raw, lf


def _suffix_sums(x, sel):
    hi, lo = _split2(x)
    d = functools.partial(lax.dot_general, dimension_numbers=NN, preferred_element_type=F32)
    return d(hi, sel) + d(lo, sel)


def _sb_diag_mask(tq, r):
    return r * SB_TK + _iota2((tq, SB_TK), 1) < _iota2((tq, SB_TK), 0)


def _sb_specs(t, tq, hp):
    w = hp * LANE
    q0, k0, v0, z0 = (O_QKV_SB // w, (O_QKV_SB + D_MODEL) // w, (O_QKV_SB + 2 * D_MODEL) // w, O_Z_SB // w)

    def blk(b0):
        return pl.BlockSpec((tq, w), lambda h, i: (i, b0 + h))

    def full(b0, **kw):
        return pl.BlockSpec((t, w), lambda h, i: (0, b0 + h), **kw)

    once = dict(pipeline_mode=pl.Buffered(1))
    return blk(q0), full(k0, **once), full(v0, **once), blk(z0), blk(0), full(0)


def _sb_fwd(proj, shards):
    t = proj.shape[0]
    tq = min(SB_TQ, t)
    ndiag = tq // SB_TK
    scale = 1.0 / math.sqrt(D_HEAD)
    na = len(shards)

    def body(q_ref, k_ref, v_ref, z_ref, *rest):
        x_refs, (o_ref, oraw_ref), land = rest[:na], rest[na:na + 2], rest[na + 2:2 * na + 2]
        sems = rest[2 * na + 2:]
        qi = pl.program_id(1)
        first = jnp.logical_and(pl.program_id(0) == 0, qi == 0)
        last = jnp.logical_and(pl.program_id(0) == pl.num_programs(0) - 1, qi == pl.num_programs(1) - 1)

        @pl.when(first)
        def _():
            for cp in _direct_gather_copies(x_refs, land, *sems):
                cp.start()

        lanes = [slice(hd * LANE, (hd + 1) * LANE) for hd in range(SB_HP_FWD)]
        qs = [(q_ref[:, ln] * scale).astype(BF16) for ln in lanes]
        after = (_iota2((SB_TK, SB_TK), 0) > _iota2((SB_TK, SB_TK), 1)).astype(BF16)
        oraw_ref[...] = jnp.zeros_like(oraw_ref)

        def block(kb, mask, c_lf):
            rows = pl.ds(pl.multiple_of(kb * SB_TK, SB_TK), SB_TK)
            z = _each(lambda q, ln: _dot(q, k_ref[rows, ln], NT), qs, lanes)
            lg = _each(lambda x: _sb_logits(x, mask), z)
            surv = _each(lambda x: _suffix_sums(x[2], after), lg)
            att = _each(lambda x, s, c: jnp.exp(x[0] + s + c), lg, surv, c_lf)
            if mask is not None:
                att = _each(lambda a: jnp.where(mask, a, 0.0), att)
            pv = _each(lambda a, ln: _dot(a, v_ref[rows, ln], NN), att, lanes)
            for p, ln in zip(pv, lanes):
                oraw_ref[:, ln] += p
            return tuple(_each(lambda c, x: c + jnp.sum(x[2], axis=1, keepdims=True), c_lf, lg))

        carry = tuple(jnp.zeros((tq, 1), F32) for _ in range(SB_HP_FWD))
        for r in reversed(range(ndiag)):
            carry = block(qi * ndiag + r, _sb_diag_mask(tq, r), carry)
        lax.fori_loop(0, qi * ndiag, lambda i, c: block(qi * ndiag - 1 - i, None, c), carry)
        o_ref[...] = (oraw_ref[...] * _silu(z_ref[...])).astype(BF16)

        @pl.when(last)
        def _():
            for cp in _direct_gather_copies(x_refs, land, *sems):
                cp.wait()

    q_spec, k_spec, v_spec, z_spec, out, _ = _sb_specs(t, tq, SB_HP_FWD)
    outs = pl.pallas_call(
        body,
        name="sb_fwd",
        grid=(N_HEADS // SB_HP_FWD, t // tq),
        in_specs=[q_spec, k_spec, v_spec, z_spec] + [ANY] * na,
        out_specs=[out, out] + [ANY] * na,
        out_shape=[jax.ShapeDtypeStruct((t, D_MODEL), BF16), jax.ShapeDtypeStruct((t, D_MODEL), F32)]
        + [jax.ShapeDtypeStruct((N_DEV, *v.shape), v.dtype) for v in shards],
        scratch_shapes=_gather_sems(na),
    )(proj, proj, proj, proj, *shards)
    return outs[0], outs[1], outs[2:]


def _sb_bwd(proj, oraw, do, blocks):
    t = proj.shape[0]
    tq = min(SB_TQ, t)
    ndiag = tq // SB_TK
    scale = 1.0 / math.sqrt(D_HEAD)
    nb = len(blocks)

    def body(q_ref, k_ref, v_ref, z_ref, oraw_ref, do_ref, *rest):
        blk_refs, (dq_ref, dk_ref, dv_ref, dz_ref), land_refs = rest[:nb], rest[nb:nb + 4], rest[nb + 4:2 * nb + 4]
        dk_acc, dv_acc, p_scr, z_scr, send_sems, recv_sems, local_sems = rest[2 * nb + 4:]
        qi = pl.program_id(1)
        nq = pl.num_programs(1)
        hg = pl.program_id(0)
        me = _position()
        mine = 4 * me[0] + 2 * me[1] + me[2]

        def exchange():
            cps = []
            for a, (blk_ref, land_ref) in enumerate(zip(blk_refs, land_refs)):
                cps.append(pltpu.make_async_copy(blk_ref.at[mine], land_ref.at[mine], local_sems.at[a]))
                for k, peer in enumerate(_other_devices(me)):
                    cps.append(pltpu.make_async_remote_copy(
                        src_ref=blk_ref.at[4 * peer[0] + 2 * peer[1] + peer[2]], dst_ref=land_ref.at[mine],
                        send_sem=send_sems.at[7 * a + k], recv_sem=recv_sems.at[7 * a + k], device_id=peer,
                        device_id_type=MESH))
            return cps

        @pl.when(jnp.logical_and(hg == 0, qi == 0))
        def _():
            for cp in exchange():
                cp.start()

        @pl.when(qi == 0)
        def _():
            dk_acc[...] = jnp.zeros_like(dk_acc)
            dv_acc[...] = jnp.zeros_like(dv_acc)

        heads = range(SB_HP_BWD)
        lanes = [slice(hd * LANE, (hd + 1) * LANE) for hd in heads]
        zg = z_ref[...]
        sg = _sigmoid(zg)
        dog = do_ref[...].astype(F32)
        dz_ref[...] = (dog * oraw_ref[...] * (sg * (1.0 + zg * (1.0 - sg)))).astype(BF16)
        d_o = (dog * (zg * sg)).astype(BF16)
        d_o16 = [d_o[:, ln] for ln in lanes]
        qs = [(q_ref[:, ln] * scale).astype(BF16) for ln in lanes]
        ri = _iota2((SB_TK, SB_TK), 0)
        ci = _iota2((SB_TK, SB_TK), 1)
        after = (ri > ci).astype(BF16)
        earlier = (ri < ci).astype(BF16)

        def rows_of(kb):
            return pl.ds(pl.multiple_of(kb * SB_TK, SB_TK), SB_TK)

        def down(kb, mask, c_lf):
            rows = rows_of(kb)
            z = _each(lambda q, ln: _dot(q, k_ref[rows, ln], NT), qs, lanes)
            da = _each(lambda d, ln: _dot(d, v_ref[rows, ln], NT), d_o16, lanes)
            lg = _each(lambda x: _sb_logits(x, mask), z)
            surv = _each(lambda x: _suffix_sums(x[2], after), lg)
            att = _each(lambda x, s, c: jnp.exp(x[0] + s + c), lg, surv, c_lf)
            if mask is not None:
                att = _each(lambda a: jnp.where(mask, a, 0.0), att)
            dv = _each(lambda a, d: _dot(a, d, TN), att, d_o16)
            for hd in heads:
                p_scr[hd, kb] = att[hd] * da[hd]
                z_scr[hd, kb] = z[hd]
                dv_acc[rows, lanes[hd]] += dv[hd]
            return tuple(_each(lambda c, x: c + jnp.sum(x[2], axis=1, keepdims=True), c_lf, lg))

        c_lf = tuple(jnp.zeros((tq, 1), F32) for _ in heads)
        for r in reversed(range(ndiag)):
            c_lf = down(qi * ndiag + r, _sb_diag_mask(tq, r), c_lf)
        lax.fori_loop(0, qi * ndiag, lambda i, c: down(qi * ndiag - 1 - i, None, c), c_lf)

        def up(kb, mask, carry):
            dq, c_p = carry
            rows = rows_of(kb)
            p = [p_scr[hd, kb] for hd in heads]
            zs = [z_scr[hd, kb] for hd in heads]
            before = _each(lambda x, c: _suffix_sums(x, earlier) + c, p, c_p)
            e = _each(lambda x: jnp.exp(-jnp.abs(x)), zs)
            r = _each(lambda x: 1.0 / (1.0 + x), e)
            sig = _each(lambda x, a, b: jnp.where(x >= 0.0, b, a * b), zs, e, r)
            oms = _each(lambda x, a, b: jnp.where(x >= 0.0, a * b, b), zs, e, r)
            if mask is not None:
                sig = _each(lambda a: jnp.where(mask, a, 0.0), sig)
            dzz = _each(lambda x, o, g, b: x * o - g * b, p, oms, sig, before)
            dk = _each(lambda x, q: _dot(x, q, TN), dzz, qs)
            dq = _each(lambda a, x, ln: a + _dot(x, k_ref[rows, ln], NN), dq, dzz, lanes)
            for hd in heads:
                dk_acc[rows, lanes[hd]] += dk[hd]
            return tuple(dq), tuple(_each(lambda c, x: c + jnp.sum(x, axis=1, keepdims=True), c_p, p))

        carry = (tuple(jnp.zeros((tq, D_HEAD), F32) for _ in heads), tuple(jnp.zeros((tq, 1), F32) for _ in heads))
        carry = lax.fori_loop(0, qi * ndiag, lambda kb, c: up(kb, None, c), carry)
        for r in range(ndiag):
            carry = up(qi * ndiag + r, _sb_diag_mask(tq, r), carry)
        dq = carry[0]
        for hd in heads:
            dq_ref[:, lanes[hd]] = (dq[hd] * scale).astype(BF16)

        @pl.when(qi == nq - 1)
        def _():
            dk_ref[...] = dk_acc[...].astype(BF16)
            dv_ref[...] = dv_acc[...].astype(BF16)

        @pl.when(jnp.logical_and(hg == pl.num_programs(0) - 1, qi == nq - 1))
        def _():
            for cp in exchange():
                cp.wait()

    q_spec, k_spec, v_spec, z_spec, blk, full = _sb_specs(t, tq, SB_HP_BWD)
    o = jax.ShapeDtypeStruct((t, D_MODEL), BF16)
    w = SB_HP_BWD * LANE
    outs = pl.pallas_call(
        body,
        name="sb_bwd",
        grid=(N_HEADS // SB_HP_BWD, t // tq),
        in_specs=[q_spec, k_spec, v_spec, z_spec, blk, blk] + [ANY] * nb,
        out_specs=[blk, full, full, blk] + [ANY] * nb,
        out_shape=[o, o, o, o] + [jax.ShapeDtypeStruct(b.shape, b.dtype) for b in blocks],
        scratch_shapes=[pltpu.VMEM((t, w), F32), pltpu.VMEM((t, w), F32)]
        + [pltpu.VMEM((SB_HP_BWD, t // SB_TK, tq, SB_TK), F32)] * 2 + _gather_sems(nb),
    )(proj, proj, proj, proj, oraw, do, *blocks)
    return outs[0], outs[1], outs[2], outs[3], outs[4:]


def _mem_kv_fn(mem, mg, w):
    return mm_nn(_rmsnorm(mem, mg), w)


def _mem_kv(mem, mg, w):
    def body(m_ref, g_ref, w_ref, o_ref):
        o_ref[...] = _mem_kv_fn(m_ref[...], g_ref[...], w_ref[...])

    return pl.pallas_call(body, name="mem_kv", out_shape=jax.ShapeDtypeStruct((MEM_LEN, 2 * MEM_W), F32))(mem, mg, w)


def _mem_kv_bwd(mem, mg, w, dmkv):
    def body(m_ref, g_ref, w_ref, d_ref, dg_ref, dw_ref):
        _, vjp = jax.vjp(_mem_kv_fn, m_ref[...], g_ref[...], w_ref[...].astype(F32))
        _, dg, dw = vjp(d_ref[...])
        dg_ref[...] = dg
        dw_ref[...] = dw.astype(BF16)

    return pl.pallas_call(
        body, name="mem_kv_bwd",
        out_shape=[jax.ShapeDtypeStruct((1, D_MODEL), F32), jax.ShapeDtypeStruct((D_MODEL, 2 * MEM_W), BF16)],
    )(mem, mg, w, dmkv)


def _mem_attn(proj, mkv, tm=256):
    t = proj.shape[0]
    tm = min(tm, t)

    def body(q_ref, z_ref, kv_ref, o_ref):
        o_ref[...] = _mem_fn(q_ref[...], z_ref[...], kv_ref[...]).astype(BF16)

    return pl.pallas_call(
        body,
        name="mem_attn",
        grid=(t // tm,),
        in_specs=[pl.BlockSpec((tm, MEM_W), lambda i: (i, O_MQ // MEM_W)),
                  pl.BlockSpec((tm, MEM_W), lambda i: (i, O_MZ // MEM_W)),
                  pl.BlockSpec((MEM_LEN, 2 * MEM_W), lambda i: (0, 0))],
        out_specs=pl.BlockSpec((tm, MEM_W), lambda i: (i, 0)),
        out_shape=jax.ShapeDtypeStruct((t, MEM_W), BF16),
    )(proj, proj, mkv)


def _mem_attn_bwd(proj, mkv, do, tm=256):
    t = proj.shape[0]
    tm = min(tm, t)

    def body(q_ref, z_ref, kv_ref, do_ref, dq_ref, dz_ref, dkv_ref):
        _, vjp = jax.vjp(_mem_fn, q_ref[...], z_ref[...], kv_ref[...])
        dq, dz, dkv = vjp(do_ref[...].astype(F32))
        dq_ref[...] = dq.astype(BF16)
        dz_ref[...] = dz.astype(BF16)

        @pl.when(pl.program_id(0) == 0)
        def _():
            dkv_ref[...] = jnp.zeros_like(dkv_ref)

        dkv_ref[...] += dkv

    blk = pl.BlockSpec((tm, MEM_W), lambda i: (i, 0))
    kv = pl.BlockSpec((MEM_LEN, 2 * MEM_W), lambda i: (0, 0))
    return pl.pallas_call(
        body,
        name="mem_attn_bwd",
        grid=(t // tm,),
        in_specs=[pl.BlockSpec((tm, MEM_W), lambda i: (i, O_MQ // MEM_W)),
                  pl.BlockSpec((tm, MEM_W), lambda i: (i, O_MZ // MEM_W)), kv, blk],
        out_specs=[blk, blk, kv],
        out_shape=[jax.ShapeDtypeStruct((t, MEM_W), BF16), jax.ShapeDtypeStruct((t, MEM_W), BF16),
                   jax.ShapeDtypeStruct((MEM_LEN, 2 * MEM_W), F32)],
    )(proj, proj, mkv, do)


def _proj_gather(h, w_alt, shards, tm=512, tn=3968):
    t = h.shape[0]
    tm = min(tm, t)
    n, kdim = w_alt.shape
    assert n % tn == 0 and t % tm == 0
    nj, ni = n // tn, t // tm
    na = len(shards)

    def body(h_ref, w_ref, *rest):
        x_refs, o_ref, land = rest[:na], rest[na], rest[na + 1:2 * na + 1]
        send_sems, recv_sems, local_sems = rest[2 * na + 1:]
        j, i = pl.program_id(0), pl.program_id(1)

        def copies():
            return _direct_gather_copies(x_refs, land, send_sems, recv_sems, local_sems)

        @pl.when(jnp.logical_and(j == 0, i == 0))
        def _():
            for cp in copies():
                cp.start()

        o_ref[...] = _dot(h_ref[...], w_ref[...], NT)

        @pl.when(jnp.logical_and(j == nj - 1, i == ni - 1))
        def _():
            for cp in copies():
                cp.wait()

    outs = pl.pallas_call(
        body,
        name="proj",
        grid=(nj, ni),
        in_specs=[pl.BlockSpec((tm, kdim), lambda j, i: (i, 0)), pl.BlockSpec((tn, kdim), lambda j, i: (j, 0))]
        + [ANY] * na,
        out_specs=[pl.BlockSpec((tm, tn), lambda j, i: (i, j))] + [ANY] * na,
        out_shape=[jax.ShapeDtypeStruct((t, n), F32)]
        + [jax.ShapeDtypeStruct((N_DEV, *v.shape), v.dtype) for v in shards],
        scratch_shapes=_gather_sems(na),
    )(h, w_alt, *shards)
    return outs[0], outs[1:]


def _local_step(x, mem, tgt, norm_g, mem_norm_g, w_alt, alog_row, dtb_row, dn_norm_g, final_g, shards):
    h = _norm_in(x, norm_g)
    s_kv, s_dn, s_sb, s_out, s_mem, s_conv = shards
    proj, (g_kv, g_conv) = _proj_gather(h, w_alt, [s_kv, s_conv])
    w_mem_kv = g_kv.reshape(D_MODEL, 2 * MEM_W)
    conv_w = g_conv.transpose(1, 0, 2).reshape(CONV_K, 3 * D_MODEL)

    c = _dn_conv(proj, conv_w)
    o_dn, states = _dn_fwd(c, proj, alog_row, dtb_row, dn_norm_g)
    o_sb, o_sb_raw, (g_dn, g_sb, g_out, g_mem) = _sb_fwd(proj, [s_dn, s_sb, s_out, s_mem])
    w_br_dn = g_dn.reshape(D_MODEL, D_MODEL)
    w_br_sb = g_sb.reshape(D_MODEL, D_MODEL)
    w_out = g_out.reshape(D_MODEL, D_MODEL)
    w_br_mem = g_mem.transpose(1, 0, 2).reshape(MEM_W, D_MODEL)
    mkv = _mem_kv(mem, mem_norm_g, w_mem_kv)
    o_m = _mem_attn(proj, mkv)

    (loss, dout, d_final_g, merged, dy_dn, dy_sb, dy_m, dgates, do_dn, do_sb, do_m) = _block_tail(
        proj, o_dn, o_sb, o_m, x, tgt, w_br_dn, w_br_sb, w_br_mem, w_out, final_g)
    dw_out = _matmul_tn(merged, dout, BF16, 256, 1024, 2048, "dw_out")
    dw_br_dn = _matmul_tn(o_dn, dy_dn, BF16, 256, 1024, 2048, "dw_br_dn")
    dw_br_sb = _matmul_tn(o_sb, dy_sb, BF16, 256, 1024, 2048, "dw_br_sb")
    dw_br_mem = _matmul_tn(o_m, dy_m, BF16, 256, 1024, 2048, "dw_br_mem")

    dmq, dmz, dmkv = _mem_attn_bwd(proj, mkv, do_m)
    d_mem_norm_g, dw_mem_kv = _mem_kv_bwd(mem, mem_norm_g, w_mem_kv, dmkv)
    rows_d = D_MODEL // N_DEV
    small_blocks = [
        dw_br_dn.reshape(N_DEV, rows_d, D_MODEL), dw_br_sb.reshape(N_DEV, rows_d, D_MODEL),
        dw_out.reshape(N_DEV, rows_d, D_MODEL), dw_mem_kv.reshape(N_DEV, rows_d // 2, D_MODEL),
        dw_br_mem.reshape(MEM_W, N_DEV, rows_d).transpose(1, 0, 2).reshape(N_DEV, MEM_W // N_DEV, D_MODEL)]
    dq_sb, dk_sb, dv_sb, dz_sb, small_parts = _sb_bwd(proj, o_sb_raw, do_sb, small_blocks)
    d_small = _sum_slots(list(small_parts), "sum_small_grads")
    dcq, dck, dcv, dz_dn, dba, dscal, d_dn_norm_g = _dn_bwd(c, proj, alog_row, dtb_row, dn_norm_g, states, do_dn)
    dq_dn, dcw_q = _dn_conv_bwd(proj, conv_w, dcq, 0)
    dk_dn, dcw_k = _dn_conv_bwd(proj, conv_w, dck, 1)
    dv_dn, dcw_v = _dn_conv_bwd(proj, conv_w, dcv, 2)
    d_conv_w = jnp.concatenate([dcw_q, dcw_k, dcw_v], axis=1)

    dproj = [dq_dn, dk_dn, dv_dn, dz_dn, dq_sb, dk_sb, dv_sb, dz_sb, dmq, dmz, dgates, dba.astype(BF16)]
    dw_alt = _dw_alt(dproj, h)
    grad_x, d_norm_g = _grad_x(dproj, w_alt, x, norm_g, dout)
    return dict(loss=loss, grad_x=grad_x, norm_g=d_norm_g, mem_norm_g=d_mem_norm_g, w_alt=dw_alt, conv_w=d_conv_w,
                scal=dscal, dn_norm_g=d_dn_norm_g, small=d_small, final_g=d_final_g)


MESH = pl.DeviceIdType.MESH
ANY = pl.BlockSpec(memory_space=pl.ANY)


def _position():
    return lax.axis_index("x"), lax.axis_index("y"), lax.axis_index("c")


def _other_devices(me):
    return [tuple(1 - p if (f >> s) & 1 else p for p, s in zip(me, (2, 1, 0))) for f in range(1, N_DEV)]


def _direct_gather_copies(x_refs, land_refs, send_sems, recv_sems, local_sems):
    me = _position()
    mine = 4 * me[0] + 2 * me[1] + me[2]
    cps = []
    for a, (x_ref, land) in enumerate(zip(x_refs, land_refs)):
        cps.append(pltpu.make_async_copy(x_ref, land.at[mine], local_sems.at[a]))
        for k, peer in enumerate(_other_devices(me)):
            cps.append(pltpu.make_async_remote_copy(
                src_ref=x_ref, dst_ref=land.at[mine], send_sem=send_sems.at[7 * a + k],
                recv_sem=recv_sems.at[7 * a + k], device_id=peer, device_id_type=MESH))
    return cps


def _gather_sems(n):
    return [pltpu.SemaphoreType.DMA((7 * n,)), pltpu.SemaphoreType.DMA((7 * n,)), pltpu.SemaphoreType.DMA((n,))]


def _all_gather(xs, name):
    n = len(xs)

    def body(*refs):
        x_refs, o_refs = refs[:n], refs[n:2 * n]
        send_sems, recv_sems, local_sems = refs[2 * n:]
        x, y, c = _position()
        me, sibling = (x, y, c), (x, y, 1 - c)
        x_nbr, y_nbr, diag = (1 - x, y, c), (x, 1 - y, c), (1 - x, 1 - y, c)
        south = c == 0
        relay_from = tuple(jnp.where(south, a, b) for a, b in zip(y_nbr, x_nbr))
        relay_to = tuple(jnp.where(south, a, b) for a, b in zip(x_nbr, y_nbr))

        def slot(p):
            return 4 * p[0] + 2 * p[1] + p[2]

        def copy(a, k, block, to, src=None):
            dst = o_refs[a].at[slot(block)]
            return pltpu.make_async_remote_copy(
                src_ref=dst if src is None else src, dst_ref=dst, send_sem=send_sems.at[7 * a + k],
                recv_sem=recv_sems.at[7 * a + k], device_id=to, device_id_type=MESH)

        mine = [pltpu.make_async_copy(x_refs[a], o_refs[a].at[slot(me)], local_sems.at[a]) for a in range(n)]
        for cp in mine:
            cp.start()
        sends = []
        for a in range(n):
            sends += [copy(a, 0, me, sibling, src=x_refs[a]), copy(a, 1, me, x_nbr, src=x_refs[a]),
                      copy(a, 2, me, y_nbr, src=x_refs[a])]
        for cp in sends:
            cp.start()
        later = []
        for a in range(n):
            copy(a, 1, x_nbr, me).wait_recv()
            copy(a, 2, y_nbr, me).wait_recv()
            later += [copy(a, 3, relay_from, relay_to), copy(a, 4, x_nbr, sibling), copy(a, 5, y_nbr, sibling)]
            for cp in later[-3:]:
                cp.start()
        for a in range(n):
            copy(a, 3, diag, me).wait_recv()
            later.append(copy(a, 6, diag, sibling))
            later[-1].start()
        for a in range(n):
            copy(a, 0, sibling, me).wait_recv()
            for k, chip in ((4, x_nbr), (5, y_nbr), (6, diag)):
                copy(a, k, (chip[0], chip[1], 1 - c), me).wait_recv()
        for cp in sends + later:
            cp.wait_send()
        for cp in mine:
            cp.wait()

    return pl.pallas_call(
        body,
        name=name,
        in_specs=[ANY] * n,
        out_specs=[ANY] * n,
        out_shape=[jax.ShapeDtypeStruct((N_DEV, *v.shape), v.dtype) for v in xs],
        scratch_shapes=[pltpu.SemaphoreType.DMA((7 * n,)), pltpu.SemaphoreType.DMA((7 * n,)),
                        pltpu.SemaphoreType.DMA((n,))],
    )(*xs)


def _window_view(ref, dest):
    return ref.at[pl.ds(WIN_ROW0[dest], WIN_W), :]


def _chunk_rows(rows, cols):
    return max(ch for ch in range(ROW_TILE, rows + 1, ROW_TILE) if rows % ch == 0 and ch * cols <= (1 << 20))


def _halving_stage(xs, axis, name, out_dtype, windowed=(), gather=()):
    n_arr = len(xs)
    metas = []
    for k, v in enumerate(xs):
        if k in windowed:
            metas.append((N_DEV // 2, WIN_W, v.shape[1]))
        else:
            assert v.shape[1] == 2
            metas.append((v.shape[0], v.shape[2], v.shape[3]))
    chunk = [_chunk_rows(r, c) for (_, r, c) in metas]
    offs = [sum(m[0] for m in metas[:k]) for k in range(n_arr)]
    n_sem = sum(m[0] for m in metas)

    n_g = len(gather)

    def body(*refs):
        x_refs, g_refs = refs[:n_arr], refs[n_arr:n_arr + n_g]
        outs = refs[n_arr + n_g:]
        o_refs, land_refs, gl_refs = outs[:n_arr], outs[n_arr:2 * n_arr], outs[2 * n_arr:2 * n_arr + n_g]
        rest = outs[2 * n_arr + n_g:]
        bufs = rest[:3 * n_arr]
        send_sems, recv_sems, in_sems, out_sems = rest[3 * n_arr:3 * n_arr + 4]
        gathers = _direct_gather_copies(g_refs, gl_refs, *rest[3 * n_arr + 4:]) if n_g else []
        for cp in gathers:
            cp.start()
        pos = dict(zip("xyc", _position()))
        bit = pos[axis]
        peer = tuple(1 - pos[a] if a == axis else pos[a] for a in "xyc")

        def view(k, i, b):
            if k in windowed:
                return _window_view(x_refs[k], 2 * i + b)
            return x_refs[k].at[i, b]

        def add_blocks(k, a_view, b_view, o_view):
            _hbm_add(a_view, b_view, o_view, bufs[3 * k:3 * k + 3], in_sems, out_sems, chunk[k])

        for b in (0, 1):
            @pl.when(bit == b)
            def _(b=b):
                sends = []
                for k in range(n_arr):
                    for i in range(metas[k][0]):
                        cp = pltpu.make_async_remote_copy(
                            src_ref=view(k, i, 1 - b), dst_ref=land_refs[k].at[i], send_sem=send_sems.at[offs[k] + i],
                            recv_sem=recv_sems.at[offs[k] + i], device_id=peer, device_id_type=MESH)
                        cp.start()
                        sends.append(cp)
                idx = 0
                for k in range(n_arr):
                    for i in range(metas[k][0]):
                        sends[idx].wait_recv()
                        add_blocks(k, view(k, i, b), land_refs[k].at[i], o_refs[k].at[i])
                        idx += 1
                for cp in sends:
                    cp.wait_send()

        for cp in gathers:
            cp.wait()

    out_shape = [jax.ShapeDtypeStruct(m, out_dtype) for m in metas]
    land_shape = [jax.ShapeDtypeStruct(m, v.dtype) for m, v in zip(metas, xs)]
    g_shape = [jax.ShapeDtypeStruct((N_DEV, *v.shape), v.dtype) for v in gather]
    scratch = []
    for k in range(n_arr):
        blk = (2, chunk[k], metas[k][2])
        scratch += [pltpu.VMEM(blk, xs[k].dtype)] * 2 + [pltpu.VMEM(blk, out_dtype)]
    scratch += [pltpu.SemaphoreType.DMA((n_sem,)), pltpu.SemaphoreType.DMA((n_sem,)),
                pltpu.SemaphoreType.DMA((2, 2)), pltpu.SemaphoreType.DMA((2,))]
    if n_g:
        scratch += _gather_sems(n_g)
    outs = pl.pallas_call(
        body,
        name=name,
        in_specs=[ANY] * (n_arr + n_g),
        out_specs=[ANY] * (2 * n_arr + n_g),
        out_shape=out_shape + land_shape + g_shape,
        scratch_shapes=scratch,
    )(*xs, *gather)
    return outs[:n_arr], outs[2 * n_arr:]


def _hbm_add(a_view, b_view, o_view, bufs, in_sems, out_sems, ch):
    rows = a_view.shape[0]
    nch = rows // ch
    va, vb, vo = bufs

    def rows_of(j):
        return pl.ds(pl.multiple_of(j * ch, 16), ch)

    def loads(j, s):
        return (pltpu.make_async_copy(a_view.at[rows_of(j), :], va.at[s], in_sems.at[0, s]),
                pltpu.make_async_copy(b_view.at[rows_of(j), :], vb.at[s], in_sems.at[1, s]))

    def store(j, s):
        return pltpu.make_async_copy(vo.at[s], o_view.at[rows_of(j), :], out_sems.at[s])

    for cp in loads(0, 0):
        cp.start()

    def step(j, _):
        s = lax.rem(j, 2)

        @pl.when(j + 1 < nch)
        def _():
            for cp in loads(j + 1, 1 - s):
                cp.start()

        for cp in loads(j, s):
            cp.wait()

        @pl.when(j >= 2)
        def _():
            store(j - 2, s).wait()

        vo[s] = (va[s].astype(F32) + vb[s].astype(F32)).astype(vo.dtype)
        store(j, s).start()
        return 0

    lax.fori_loop(0, nch, step, 0)
    for j in range(max(0, nch - 2), nch):
        store(j, j % 2).wait()


def _xy_stage(xs, first, name):
    n_arr = len(xs)
    if first:
        shapes = [(v.shape[2] // 2, v.shape[3]) for v in xs]
        ins = list(xs)
    else:
        shapes = [(a.shape[1], a.shape[2]) for a, _ in xs]
        ins = [v for pair in xs for v in pair]
    n_blk = 2 if first else 1
    out_dtype = BF16 if first else F32
    chunk = [_chunk_rows(r, c) for (r, c) in shapes]
    n_sem = 2 * n_blk * n_arr

    def body(*refs):
        n_in = len(ins)
        in_refs = refs[:n_in]
        n_out = 2 * n_arr if first else n_arr
        o_refs = refs[n_in:n_in + n_out]
        land = refs[n_in + n_out:n_in + n_out + 2 * n_arr]
        rest = refs[n_in + n_out + 2 * n_arr:]
        bufs = rest[:3 * n_arr]
        send_sems, recv_sems, in_sems, out_sems = rest[3 * n_arr:]
        x, y, c = _position()
        peers = {"x": (1 - x, y, c), "y": (x, 1 - y, c)}
        jobs = []
        for k in range(n_arr):
            r, _ = shapes[k]
            half_a, half_b = pl.ds(0, r), pl.ds(r, r)
            if first:
                src = in_refs[k]
                for i in range(2):
                    jobs.append((k, src.at[i, 1 - y, half_a, :], src.at[i, y, half_a, :], land[2 * k].at[i],
                                 o_refs[2 * k].at[i], "y"))
                    jobs.append((k, src.at[1 - x, i, half_b, :], src.at[x, i, half_b, :], land[2 * k + 1].at[i],
                                 o_refs[2 * k + 1].at[i], "x"))
            else:
                a1, b1 = in_refs[2 * k], in_refs[2 * k + 1]
                jobs.append((k, a1.at[1 - x], a1.at[x], land[2 * k], o_refs[k].at[half_a, :], "x"))
                jobs.append((k, b1.at[1 - y], b1.at[y], land[2 * k + 1], o_refs[k].at[half_b, :], "y"))
        sends = []
        for n, (k, send, _, landing, _, axis) in enumerate(jobs):
            cp = pltpu.make_async_remote_copy(src_ref=send, dst_ref=landing, send_sem=send_sems.at[n],
                                              recv_sem=recv_sems.at[n], device_id=peers[axis], device_id_type=MESH)
            cp.start()
            sends.append(cp)
        for cp, (k, _, kept, landing, out, _) in zip(sends, jobs):
            cp.wait_recv()
            _hbm_add(kept, landing, out, bufs[3 * k:3 * k + 3], in_sems, out_sems, chunk[k])
        for cp in sends:
            cp.wait_send()

    if first:
        out_shape = [jax.ShapeDtypeStruct((2, r, c), BF16) for (r, c) in shapes for _ in range(2)]
        land_shape = out_shape
    else:
        out_shape = [jax.ShapeDtypeStruct((2 * r, c), F32) for (r, c) in shapes]
        land_shape = [jax.ShapeDtypeStruct((r, c), BF16) for (r, c) in shapes for _ in range(2)]
    scratch = []
    for k in range(n_arr):
        scratch += [pltpu.VMEM((2, chunk[k], shapes[k][1]), BF16)] * 2 + [pltpu.VMEM((2, chunk[k], shapes[k][1]), out_dtype)]
    scratch += [pltpu.SemaphoreType.DMA((n_sem,)), pltpu.SemaphoreType.DMA((n_sem,)),
                pltpu.SemaphoreType.DMA((2, 2)), pltpu.SemaphoreType.DMA((2,))]
    outs = pl.pallas_call(
        body,
        name=name,
        in_specs=[ANY] * len(ins),
        out_specs=[ANY] * (len(out_shape) + len(land_shape)),
        out_shape=out_shape + land_shape,
        scratch_shapes=scratch,
    )(*ins)
    outs = outs[:len(out_shape)]
    return [(outs[2 * k], outs[2 * k + 1]) for k in range(n_arr)] if first else list(outs)


def _reduce_scatter(dw_al, blocks, gather=()):
    xs = [dw_al] + [b.reshape(N_DEV // 2, 2, *b.shape[1:]) for b in blocks]
    ys, gathered = _halving_stage(xs, "c", "rs_c", BF16, windowed=(0,), gather=gather)
    pairs = _xy_stage([v.reshape(2, 2, *v.shape[1:]) for v in ys], True, "rs_xy1")
    return _xy_stage(pairs, False, "rs_xy2"), gathered


def _sum_slots(gs, name):
    n = len(gs)

    def body(*refs):
        for g_ref, o_ref in zip(refs[:n], refs[n:]):
            acc = g_ref[0].astype(F32)
            for d in range(1, N_DEV):
                acc = acc + g_ref[d].astype(F32)
            o_ref[...] = acc

    return pl.pallas_call(body, name=name, out_shape=[jax.ShapeDtypeStruct(g.shape[1:], F32) for g in gs])(*gs)


def _assemble_w_al(wins, bas):
    cols = wins.shape[2]
    n_buf = 3
    ends = [WIN_ROW0[d + 1] if d + 1 < N_DEV else WIN_ROW0[d] + WIN_W for d in range(N_DEV)]
    tail = W_AL - ends[-1]

    def body(w_ref, ba_ref, o_ref, buf, zeros, ld_sems, st_sems, ba_sem):
        def load(d):
            return pltpu.make_async_copy(w_ref.at[d], buf.at[d % n_buf], ld_sems.at[d % n_buf])

        def store(d):
            n = ends[d] - WIN_ROW0[d]
            return pltpu.make_async_copy(buf.at[d % n_buf, pl.ds(0, n), :],
                                         o_ref.at[pl.ds(WIN_ROW0[d], n), :], st_sems.at[d % n_buf])

        zeros[...] = jnp.zeros_like(zeros)
        fill = pltpu.make_async_copy(zeros, o_ref.at[pl.ds(ends[-1], tail), :], ba_sem)
        fill.start()
        fill.wait()
        load(0).start()
        for d in range(N_DEV):
            if d + 1 < N_DEV:
                if d + 1 >= n_buf:
                    store(d + 1 - n_buf).wait()
                load(d + 1).start()
            load(d).wait()
            if d > 0:
                ov = WIN_ROW0[d - 1] + WIN_W - WIN_ROW0[d]
                buf[d % n_buf, :ov, :] = buf[d % n_buf, :ov, :] + buf[(d - 1) % n_buf, WIN_W - ov:, :]
            if d == N_DEV - 1:
                ba_copy = pltpu.make_async_copy(
                    ba_ref.at[BA_DEV], buf.at[d % n_buf, pl.ds(WIN_W - N_BA, N_BA), :], ba_sem)
                ba_copy.start()
                ba_copy.wait()
            store(d).start()
        for d in range(N_DEV - n_buf, N_DEV):
            store(d).wait()

    return pl.pallas_call(
        body,
        name="assemble_w_al",
        in_specs=[ANY, ANY],
        out_specs=ANY,
        out_shape=jax.ShapeDtypeStruct((W_AL, cols), wins.dtype),
        scratch_shapes=[pltpu.VMEM((n_buf, WIN_W, cols), wins.dtype), pltpu.VMEM((tail, cols), wins.dtype),
                        pltpu.SemaphoreType.DMA((n_buf,)), pltpu.SemaphoreType.DMA((n_buf,)), pltpu.SemaphoreType.DMA],
    )(wins, bas)


def _adamw_math(w, g, m, v):
    m_new = ADAM_B1 * m + (1.0 - ADAM_B1) * g
    v_new = ADAM_B2 * v + (1.0 - ADAM_B2) * (g * g)
    m_hat = m_new / (1.0 - ADAM_B1 ** ADAM_STEP)
    v_hat = v_new / (1.0 - ADAM_B2 ** ADAM_STEP)
    return -ADAM_LR * (m_hat / (jnp.sqrt(v_hat) + ADAM_EPS) + ADAM_WD * w), m_new, v_new


def _adamw(w, g, m, v, name, tb=134):
    r, _, c = w.shape
    assert r % tb == 0

    def body(w_ref, g_ref, m_ref, v_ref, d_ref, nm_ref, nv_ref):
        d_ref[...], nm_ref[...], nv_ref[...] = _adamw_math(w_ref[...], g_ref[...], m_ref[...], v_ref[...])

    blk = pl.BlockSpec((tb, 1, c), lambda i: (i, 0, 0))
    o = jax.ShapeDtypeStruct(w.shape, F32)
    return pl.pallas_call(body, name=name, grid=(r // tb,), in_specs=[blk] * 4, out_specs=[blk] * 3,
                          out_shape=[o, o, o])(w, g, m, v)


def _adamw_many(ws, gs, ms, vs, name):
    n = len(ws)

    def body(*refs):
        for k in range(n):
            w_ref, g_ref, m_ref, v_ref = (refs[j * n + k] for j in range(4))
            d_ref, nm_ref, nv_ref = (refs[(4 + j) * n + k] for j in range(3))
            d_ref[...], nm_ref[...], nv_ref[...] = _adamw_math(w_ref[...], g_ref[...], m_ref[...], v_ref[...])

    shapes = [jax.ShapeDtypeStruct(w.shape, F32) for w in ws]
    outs = pl.pallas_call(body, name=name, out_shape=shapes * 3)(*ws, *gs, *ms, *vs)
    return outs[:n], outs[n:2 * n], outs[2 * n:]


def _select(me, table):
    return sum(jnp.where(me == d, jnp.int32(v), jnp.int32(0)) for d, v in enumerate(table))


WIN_SHIFT = tuple(SHARD_W * d - WIN_ROW0[d] for d in range(N_DEV))
PAD_L = 64
PAD_R = 64
assert max(WIN_SHIFT) <= PAD_L and WIN_W + N_BA - SHARD_W <= PAD_R


def _shard_to_window(shard_t, me):
    shift = _select(me, WIN_SHIFT)
    padded = jnp.pad(shard_t, ((PAD_L, PAD_R), (0, 0)))
    cols = shard_t.shape[1]
    lo = lax.dynamic_slice(padded, (PAD_L - shift, 0), (WIN_W, cols))
    hi = lax.dynamic_slice(padded, (PAD_L - shift + N_BA, 0), (WIN_W, cols))
    aligned = _select(me, WIN_ROW0) + lax.broadcasted_iota(jnp.int32, (WIN_W, 1), 0)
    return jnp.where(aligned >= ORIG_BA, hi, lo)


def _window_to_shard(win, ba_grad, me):
    shift = _select(me, WIN_SHIFT)
    cols = win.shape[1]
    padded = jnp.pad(win, ((N_BA, PAD_R), (0, 0)))
    lo = lax.dynamic_slice(padded, (N_BA + shift, 0), (SHARD_W, cols))
    hi = lax.dynamic_slice(padded, (shift, 0), (SHARD_W, cols))
    orig = SHARD_W * me + lax.broadcasted_iota(jnp.int32, (SHARD_W, 1), 0)
    ba_full = lax.dynamic_update_slice(jnp.zeros((SHARD_W, cols), win.dtype), ba_grad, (BA_LOCAL, 0))
    return jnp.where(orig < ORIG_BA, lo, jnp.where(orig >= ORIG_BA + N_BA, hi, ba_full))


def _pad_row(v, width=D_MODEL):
    v = v.reshape(1, -1)
    return jnp.pad(v, ((0, 0), (0, width - v.shape[1])))


def kernel(x, mem, norm_g, mem_norm_g, w_in, conv_w, a_log, dt_bias, dn_norm_g, w_mem_kv, w_br_dn, w_br_sb, w_br_mem, w_out, final_g, loss_target, m_norm_g, m_mem_norm_g, m_w_in, m_conv_w, m_a_log, m_dt_bias, m_dn_norm_g, m_w_mem_kv, m_w_br_dn, m_w_br_sb, m_w_br_mem, m_w_out, m_final_g, v_norm_g, v_mem_norm_g, v_w_in, v_conv_w, v_a_log, v_dt_bias, v_dn_norm_g, v_w_mem_kv, v_w_br_dn, v_w_br_sb, v_w_br_mem, v_w_out, v_final_g):
    xi, yi, ci = _position()
    me = 4 * xi + 2 * yi + ci

    shard_t = w_in[0].T
    win = _shard_to_window(shard_t, me).astype(BF16)
    ba = shard_t[BA_LOCAL:BA_LOCAL + N_BA, :].astype(BF16)
    g_win, g_ba = _all_gather([win, ba], "gather_weights")
    w_alt = _assemble_w_al(g_win, g_ba)

    shards = [w_mem_kv[0].astype(BF16), w_br_dn[0].astype(BF16), w_br_sb[0].astype(BF16), w_out[0].astype(BF16),
              w_br_mem[0].astype(BF16), conv_w[0]]
    r = _local_step(x[0], mem[0], loss_target[0], norm_g, mem_norm_g, w_alt, _pad_row(a_log, LANE),
                    _pad_row(dt_bias, LANE), dn_norm_g, final_g.reshape(1, D_MODEL), shards)

    dw_alt = r["w_alt"]
    parts = [r["norm_g"], r["mem_norm_g"], r["final_g"], r["dn_norm_g"], r["scal"], r["loss"], r["conv_w"],
             dw_alt[O_BA:O_BA + N_BA, :].astype(F32)]
    (g_win,), gathered = _reduce_scatter(dw_alt, [], gather=parts)
    rows_d = D_MODEL // N_DEV
    g_dn, g_sb, g_out, g_kv, g_mem = r["small"]
    g_kv = g_kv.reshape(rows_d, 2 * MEM_W)
    g_mem = g_mem.reshape(MEM_W, rows_d)
    s_norm_g, s_mem_norm_g, s_final_g, s_dn_norm_g, s_scal, s_loss, s_conv, s_ba = _sum_slots(gathered, "sum_small")
    loss = s_loss[0, 0]
    cw = conv_w.shape[2]
    g_conv = lax.dynamic_slice(s_conv, (0, cw * me), (CONV_K, cw))
    g_w_in_t = _window_to_shard(g_win, s_ba, me)
    grads = dict(norm_g=s_norm_g, mem_norm_g=s_mem_norm_g, w_in=g_w_in_t.T[None], conv_w=g_conv[None],
                 a_log=s_scal[0:1, :N_HEADS], dt_bias=s_scal[1:2, :N_HEADS], dn_norm_g=s_dn_norm_g, w_mem_kv=g_kv[None],
                 w_br_dn=g_dn[None], w_br_sb=g_sb[None], w_br_mem=g_mem[None], w_out=g_out[None],
                 final_g=s_final_g.reshape(D_MODEL))

    params = dict(norm_g=(norm_g, m_norm_g, v_norm_g), mem_norm_g=(mem_norm_g, m_mem_norm_g, v_mem_norm_g),
                  w_in=(w_in, m_w_in, v_w_in), conv_w=(conv_w, m_conv_w, v_conv_w), a_log=(a_log, m_a_log, v_a_log),
                  dt_bias=(dt_bias, m_dt_bias, v_dt_bias), dn_norm_g=(dn_norm_g, m_dn_norm_g, v_dn_norm_g),
                  w_mem_kv=(w_mem_kv, m_w_mem_kv, v_w_mem_kv), w_br_dn=(w_br_dn, m_w_br_dn, v_w_br_dn),
                  w_br_sb=(w_br_sb, m_w_br_sb, v_w_br_sb), w_br_mem=(w_br_mem, m_w_br_mem, v_w_br_mem),
                  w_out=(w_out, m_w_out, v_w_out), final_g=(final_g, m_final_g, v_final_g))
    order = list(params)
    deltas, new_m, new_v = {}, {}, {}
    deltas["w_in"], new_m["w_in"], new_v["w_in"] = (jnp.transpose(o, (1, 2, 0)) for o in _adamw(
        jnp.transpose(w_in, (2, 0, 1)), g_w_in_t[:, None, :], jnp.transpose(m_w_in, (2, 0, 1)),
        jnp.transpose(v_w_in, (2, 0, 1)), "adamw_w_in"))
    rest = [nm for nm in order if nm != "w_in"]

    def two_d(a):
        return a.reshape(1, -1) if a.ndim == 1 else a

    d_l, m_l, v_l = _adamw_many([two_d(params[nm][0]) for nm in rest], [two_d(grads[nm]) for nm in rest],
                                [two_d(params[nm][1]) for nm in rest], [two_d(params[nm][2]) for nm in rest], "adamw_rest")
    for k, nm in enumerate(rest):
        shp = params[nm][0].shape
        deltas[nm], new_m[nm], new_v[nm] = d_l[k].reshape(shp), m_l[k].reshape(shp), v_l[k].reshape(shp)
    return (loss, r["grad_x"][None], *[grads[nm] for nm in order], *[deltas[nm] for nm in order],
            *[new_m[nm] for nm in order], *[new_v[nm] for nm in order])
```

```python
import functools
import math

import jax
import jax.numpy as jnp
from jax import lax
from jax.experimental import pallas as pl
from jax.experimental.pallas import tpu as pltpu

F32 = jnp.float32
BF16 = jnp.bfloat16

D_MODEL = 1024
N_DEV = 8
N_HEADS = 8
D_HEAD = 128
DN_CHUNK = 64
CONV_K = 4
MEM_LEN = 256
MEM_HEADS = 4
MEM_DH = 64
MEM_W = MEM_HEADS * MEM_DH
NORM_EPS = 1e-6
IN_WIDTH = 11792
SHARD_W = IN_WIDTH // N_DEV

LANE = 128
SUPER = 2 * DN_CHUNK

O_QKV_DN = 0
O_Z_DN = 3072
O_QKV_SB = 4096
O_Z_SB = 7168
O_MQ = 8192
O_MZ = 8448
O_GATES = 8704
O_BA = 11776
W_AL = 11904
ORIG_BA = 4096
N_BA = 16

BA_DEV = ORIG_BA // SHARD_W
BA_LOCAL = ORIG_BA - BA_DEV * SHARD_W


def _aligned_col(o):
    return o if o < ORIG_BA else o - N_BA


ROW_TILE = 16
WIN_W = 1504
WIN_ROW0 = tuple(_aligned_col(SHARD_W * d) // ROW_TILE * ROW_TILE for d in range(N_DEV))
assert not any(ORIG_BA <= SHARD_W * d < ORIG_BA + N_BA for d in range(N_DEV))
assert all(WIN_ROW0[d] + WIN_W >= _aligned_col(SHARD_W * (d + 1) - 1) + 1 for d in range(N_DEV))
assert all(WIN_ROW0[d + 1] <= WIN_ROW0[d] + WIN_W for d in range(N_DEV - 1))
assert WIN_ROW0[-1] + WIN_W == O_BA + N_BA

ADAM_LR = 0.001
ADAM_B1 = 0.9
ADAM_B2 = 0.999
ADAM_EPS = 1e-08
ADAM_WD = 0.01
ADAM_STEP = 10

NN = (((1,), (0,)), ((), ()))
NT = (((1,), (1,)), ((), ()))
TN = (((0,), (0,)), ((), ()))


def _dot(a, b, dims):
    return lax.dot_general(a.astype(BF16), b.astype(BF16), dims, preferred_element_type=F32)


def _split2(a):
    hi = a.astype(BF16)
    lo = (a - hi.astype(F32)).astype(BF16)
    return hi, lo


def _dot3(a, b, dims):
    ah, al = _split2(a)
    bh, bl = _split2(b)
    d = functools.partial(lax.dot_general, dimension_numbers=dims, preferred_element_type=F32)
    return d(ah, bh) + (d(ah, bl) + d(al, bh))


def _sel_dot_impl(sel01, x, dims):
    sel = sel01.astype(BF16)
    h1 = x.astype(BF16)
    r1 = x - h1.astype(F32)
    h2 = r1.astype(BF16)
    h3 = (r1 - h2.astype(F32)).astype(BF16)
    d = functools.partial(lax.dot_general, dimension_numbers=dims, preferred_element_type=F32)
    return d(sel, h1) + (d(sel, h2) + d(sel, h3))


@jax.custom_vjp
def _sel_dot(sel01, x):
    return _sel_dot_impl(sel01, x, NN)


_sel_dot.defvjp(lambda s, x: (_sel_dot(s, x), s),
                lambda s, g: (jnp.zeros_like(s), _sel_dot_impl(s, g, TN)))


def _make_mm(dotfn):
    @jax.custom_vjp
    def nn(a, b):
        return dotfn(a, b, NN)

    @jax.custom_vjp
    def nt(a, b):
        return dotfn(a, b, NT)

    @jax.custom_vjp
    def tn(a, b):
        return dotfn(a, b, TN)

    nn.defvjp(lambda a, b: (nn(a, b), (a, b)), lambda r, g: (nt(g, r[1]), tn(r[0], g)))
    nt.defvjp(lambda a, b: (nt(a, b), (a, b)), lambda r, g: (nn(g, r[1]), tn(g, r[0])))
    tn.defvjp(lambda a, b: (tn(a, b), (a, b)), lambda r, g: (nt(r[1], g), nn(r[0], g)))
    return nn, nt, tn


mm_nn, mm_nt, mm_tn = _make_mm(_dot)
mm3_nn, mm3_nt, mm3_tn = _make_mm(_dot3)


def _sigmoid(x):
    return jax.nn.sigmoid(x)


def _silu(x):
    return x * _sigmoid(x)


def _softplus_parts(x):
    sp = jnp.log1p(jnp.exp(-jnp.abs(x)))
    return jnp.maximum(x, 0.0) + sp, jnp.maximum(-x, 0.0) + sp


def _rmsnorm(x, g):
    return x * lax.rsqrt(jnp.mean(x * x, axis=-1, keepdims=True) + NORM_EPS) * g


def _iota2(shape, dim):
    return lax.broadcasted_iota(jnp.int32, shape, dim)


def _div64(i):
    return lax.shift_right_logical(i, jnp.full(i.shape, 6, jnp.int32))


def _each(f, *lists):
    return [f(*a) for a in zip(*lists)]


@jax.custom_vjp
def _inv_unit_lower(ms):
    n = ms[0].shape[0]
    eye = (_iota2((n, n), 0) == _iota2((n, n), 1)).astype(F32)
    rs = [eye - m for m in ms]
    ps = ms
    for _ in range(5):
        ps = _each(mm3_nn, ps, ps)
        rs = _each(lambda r, p: r + mm_nn(r, p), rs, ps)
    return rs


def _inv_fwd(ms):
    rs = _inv_unit_lower(ms)
    return rs, rs


def _inv_bwd(rs, gs):
    ts = _each(mm_tn, rs, gs)
    return (_each(lambda t, r: -mm_nt(t, r), ts, rs),)


_inv_unit_lower.defvjp(_inv_fwd, _inv_bwd)


def _dn_block(cq, ck, cv, bcol, acol, zt, alog, dtb, gn, s0):
    n = SUPER
    h = DN_CHUNK
    row = _iota2((n, n), 0)
    col = _iota2((n, n), 1)
    same = _div64(row) == _div64(col)
    incl = jnp.logical_and(same, row >= col)
    strict = jnp.logical_and(same, row > col)
    incl_f = incl.astype(F32)

    qn = _each(lambda x: x * lax.rsqrt(jnp.sum(x * x, axis=-1, keepdims=True) + NORM_EPS) * (D_HEAD ** -0.5), cq)
    kn = _each(lambda x: x * lax.rsqrt(jnp.sum(x * x, axis=-1, keepdims=True) + NORM_EPS), ck)
    beta = _each(_sigmoid, bcol)
    g = _each(lambda al, ac, dt: -(jnp.exp(al) * _softplus_parts(ac + dt)[0]), alog, acol, dtb)
    gcum = _each(lambda x: _sel_dot(incl_f, jnp.broadcast_to(x, (n, n))), g)
    gam_incl = _each(lambda x: jnp.where(incl, jnp.exp(jnp.where(incl, x - x.T, 0.0)), 0.0), gcum)
    kk = _each(mm_nt, kn, kn)
    t_inv = _inv_unit_lower(_each(lambda b, x, gm: b * x * jnp.where(strict, gm, 0.0), beta, kk, gam_incl))
    eg = _each(jnp.exp, gcum)
    u = _each(lambda t, v, b: mm_nn(t, v * b), t_inv, cv, beta)
    w = _each(lambda t, k, b, e: mm_nn(t, k * (b * e)), t_inv, kn, beta, eg)
    a_intra = _each(lambda q, k, gm: mm_nt(q, k) * gm, qn, kn, gam_incl)
    q_dec = _each(lambda q, e: q * e, qn, eg)
    last0 = _each(lambda x: x[h - 1:h, :], gcum)
    last1 = _each(lambda x: x[n - 1:n, :], gcum)
    k_dec = _each(lambda k, x, l0, l1: k * jnp.exp(jnp.concatenate(
        [jnp.broadcast_to(l0, (h, n)), jnp.broadcast_to(l1, (h, n))], axis=0) - x), kn, gcum, last0, last1)
    v0 = _each(lambda uu, ww, s: uu[:h] - mm_nn(ww[:h], s), u, w, s0)
    o0 = _each(lambda q, s: mm_nn(q[:h], s), q_dec, s0)
    s1 = _each(lambda s, l0, k, v: s * jnp.exp(l0) + mm_tn(k[:h], v), s0, last0, k_dec, v0)
    v1 = _each(lambda uu, ww, s: uu[h:] - mm_nn(ww[h:], s), u, w, s1)
    o1 = _each(lambda q, s: mm_nn(q[h:], s), q_dec, s1)
    s2 = _each(lambda s, l1, k, v: s * jnp.exp(l1) + mm_tn(k[h:], v), s1, last1, k_dec, v1)
    o = _each(lambda a, b, am, x, y: jnp.concatenate([a, b], axis=0) + mm_nn(am, jnp.concatenate([x, y], axis=0)),
              o0, o1, a_intra, v0, v1)
    out = _each(lambda x, z: _rmsnorm(x, gn) * _silu(z), o, zt)
    return out, s2


def _mem_fn(mq, mz, mkv):
    mk = mkv[:, :MEM_W]
    mv = mkv[:, MEM_W:]
    lane = _iota2((1, MEM_W), 1)
    out = jnp.zeros(mq.shape, F32)
    for hd in range(MEM_HEADS):
        hm = (_div64(lane) == hd).astype(F32)
        s = mm_nt(mq * hm, mk) * (1.0 / math.sqrt(MEM_DH))
        s = s - jnp.max(s, axis=-1, keepdims=True)
        e = jnp.exp(s)
        p = e / jnp.sum(e, axis=-1, keepdims=True)
        out = out + mm_nn(p, mv) * hm
    return out * _silu(mz)


def _loss_fn(x, mo, fg, tgt):
    y = _rmsnorm(x + mo, fg)
    err = y - tgt
    return 0.5 * jnp.sum(jnp.mean(err * err, axis=-1, keepdims=True), axis=0, keepdims=True)


def _matmul_tn(a, b, out_dtype, tm, tn, tk, name):
    kdim, m = a.shape
    n = b.shape[1]
    tm, tn, tk = min(tm, m), min(tn, n), min(tk, kdim)
    assert m % tm == 0 and n % tn == 0 and kdim % tk == 0
    nk = kdim // tk

    def body(a_ref, b_ref, o_ref, acc_ref):
        k = pl.program_id(2)
        part = _dot(a_ref[...], b_ref[...], TN)

        @pl.when(k == 0)
        def _():
            acc_ref[...] = part

        @pl.when(k > 0)
        def _():
            acc_ref[...] += part

        @pl.when(k == nk - 1)
        def _():
            o_ref[...] = acc_ref[...].astype(o_ref.dtype)

    return pl.pallas_call(
        body,
        name=name,
        grid=(m // tm, n // tn, nk),
        in_specs=[pl.BlockSpec((tk, tm), lambda i, j, k: (k, i)), pl.BlockSpec((tk, tn), lambda i, j, k: (k, j))],
        out_specs=pl.BlockSpec((tm, tn), lambda i, j, k: (i, j)),
        out_shape=jax.ShapeDtypeStruct((m, n), out_dtype),
        scratch_shapes=[pltpu.VMEM((tm, tn), F32)],
        compiler_params=pltpu.CompilerParams(dimension_semantics=("parallel", "parallel", "arbitrary")),
    )(a, b)


def _norm_in(x, g, tm=256):
    t = x.shape[0]

    def body(x_ref, g_ref, h_ref):
        h_ref[...] = _rmsnorm(x_ref[...], g_ref[...]).astype(BF16)

    return pl.pallas_call(
        body,
        name="norm_in",
        grid=(t // tm,),
        in_specs=[pl.BlockSpec((tm, D_MODEL), lambda i: (i, 0)), pl.BlockSpec((1, D_MODEL), lambda i: (0, 0))],
        out_specs=pl.BlockSpec((tm, D_MODEL), lambda i: (i, 0)),
        out_shape=jax.ShapeDtypeStruct((t, D_MODEL), BF16),
    )(x, g)


def _dw_alt(parts, h, tm=512):
    t = h.shape[0]
    n_p = len(parts)
    widths = [p.shape[1] for p in parts]
    offs = [sum(widths[:s]) for s in range(n_p)]
    total = sum(widths)
    n_tiles = pl.cdiv(total, tm)

    specs = []
    for off, w in zip(offs, widths):
        if w >= tm:
            assert w % tm == 0 and off % tm == 0
            specs.append(pl.BlockSpec(
                (t, tm), lambda i, lo=off // tm, n=w // tm: (0, jnp.minimum(jnp.maximum(i - lo, 0), n - 1))))
        else:
            assert off // tm == (off + w - 1) // tm
            specs.append(pl.BlockSpec((t, w), lambda i: (0, 0), pipeline_mode=pl.Buffered(1)))

    def body(*refs):
        a_refs, h_ref, o_ref = refs[:n_p], refs[n_p], refs[n_p + 1]
        i = pl.program_id(0)
        for a_ref, off, w in zip(a_refs, offs, widths):
            if w >= tm:
                @pl.when(jnp.logical_and(i >= off // tm, i < (off + w) // tm))
                def _(a_ref=a_ref):
                    o_ref[...] = _dot(a_ref[...], h_ref[...], TN).astype(o_ref.dtype)
            else:
                @pl.when(i == off // tm)
                def _(a_ref=a_ref, r0=off % tm, w=w):
                    o_ref[r0:r0 + w, :] = _dot(a_ref[...], h_ref[...], TN).astype(o_ref.dtype)
        if total % tm:
            @pl.when(i == n_tiles - 1)
            def _():
                o_ref[total % tm:, :] = jnp.zeros((tm - total % tm, D_MODEL), o_ref.dtype)

    return pl.pallas_call(
        body,
        name="dw_alt",
        grid=(n_tiles,),
        in_specs=specs + [pl.BlockSpec((t, D_MODEL), lambda i: (0, 0), pipeline_mode=pl.Buffered(1))],
        out_specs=pl.BlockSpec((tm, D_MODEL), lambda i: (i, 0)),
        out_shape=jax.ShapeDtypeStruct((n_tiles * tm, D_MODEL), BF16),
    )(*parts, h)


def _grad_x(parts, w_alt, x, g, dres, tm=256):
    t = x.shape[0]
    tm = min(tm, t)
    n_p = len(parts)
    assert sum(p.shape[1] for p in parts) == w_alt.shape[0] and t % tm == 0

    def body(*refs):
        a_refs = refs[:n_p]
        w_ref, x_ref, g_ref, dres_ref, dx_ref, dg_ref = refs[n_p:]
        dproj = jnp.concatenate([a_ref[...] for a_ref in a_refs], axis=1)
        dh = _dot(dproj, w_ref[...], NN)

        @pl.when(pl.program_id(0) == 0)
        def _():
            dg_ref[...] = jnp.zeros_like(dg_ref)

        _, vjp = jax.vjp(_rmsnorm, x_ref[...], g_ref[...])
        dx, dg = vjp(dh)
        dx_ref[...] = dx + dres_ref[...]
        dg_ref[...] += dg

    row = pl.BlockSpec((tm, D_MODEL), lambda i: (i, 0))
    vec = pl.BlockSpec((1, D_MODEL), lambda i: (0, 0))
    return pl.pallas_call(
        body,
        name="grad_x",
        grid=(t // tm,),
        in_specs=[pl.BlockSpec((tm, p.shape[1]), lambda i: (i, 0)) for p in parts]
        + [pl.BlockSpec(w_alt.shape, lambda i: (0, 0), pipeline_mode=pl.Buffered(1)), row, vec, row],
        out_specs=[row, vec],
        out_shape=[jax.ShapeDtypeStruct((t, D_MODEL), F32), jax.ShapeDtypeStruct((1, D_MODEL), F32)],
    )(*parts, w_alt, x, g, dres)


def _block_tail(proj, o_dn, o_sb, o_m, x, tgt, w_br_dn, w_br_sb, w_br_mem, w_out, fg, tm=256):
    t = x.shape[0]
    tm = min(tm, t)
    gw = 512
    n_g = 3 * D_MODEL // gw

    def body(*refs):
        g_refs = refs[:n_g]
        (odn_ref, osb_ref, om_ref, x_ref, t_ref, wdn_ref, wsb_ref, wm_ref, wo_ref, fg_ref, loss_ref, dout_ref, dfg_ref,
         mg_ref, dyd_ref, dys_ref, dym_ref, dg_ref, dod_ref, dos_ref, dom_ref) = refs[n_g:]
        y = [_dot(odn_ref[...], wdn_ref[...], NN), _dot(osb_ref[...], wsb_ref[...], NN),
             _dot(om_ref[...], wm_ref[...], NN)]
        s = [_sigmoid(jnp.concatenate([g_refs[2 * k][...], g_refs[2 * k + 1][...]], axis=1)) for k in range(3)]
        merged16 = (s[0] * y[0] + s[1] * y[1] + s[2] * y[2]).astype(BF16)
        mg_ref[...] = merged16
        mo = _dot(merged16, wo_ref[...], NN)
        loss, vjp = jax.vjp(_loss_fn, x_ref[...], mo, fg_ref[...], t_ref[...])
        _, dout, dfg, _ = vjp(jnp.ones((1, 1), F32))

        @pl.when(pl.program_id(0) == 0)
        def _():
            loss_ref[...] = jnp.zeros_like(loss_ref)
            dfg_ref[...] = jnp.zeros_like(dfg_ref)

        loss_ref[...] += jnp.broadcast_to(loss, loss_ref.shape)
        dfg_ref[...] += dfg
        dout_ref[...] = dout
        dmerged = _dot(dout, wo_ref[...], NT)
        dy = [(sk * dmerged).astype(BF16) for sk in s]
        dyd_ref[...], dys_ref[...], dym_ref[...] = dy
        dg_ref[...] = jnp.concatenate([dmerged * yk * (sk * (1.0 - sk)) for yk, sk in zip(y, s)], axis=1).astype(BF16)
        dod_ref[...] = _dot(dy[0], wdn_ref[...], NT).astype(BF16)
        dos_ref[...] = _dot(dy[1], wsb_ref[...], NT).astype(BF16)
        dom_ref[...] = _dot(dy[2], wm_ref[...], NT).astype(BF16)

    gates = [pl.BlockSpec((tm, gw), lambda i, j=j: (i, O_GATES // gw + j)) for j in range(n_g)]
    row = pl.BlockSpec((tm, D_MODEL), lambda i: (i, 0))
    rowm = pl.BlockSpec((tm, MEM_W), lambda i: (i, 0))
    vec = pl.BlockSpec((1, D_MODEL), lambda i: (0, 0))

    def whole(a):
        return pl.BlockSpec(a.shape, lambda i: (0, 0), pipeline_mode=pl.Buffered(1))

    def bf(c):
        return jax.ShapeDtypeStruct((t, c), BF16)

    return pl.pallas_call(
        body,
        name="block_tail",
        grid=(t // tm,),
        in_specs=gates + [row, row, rowm, row, row, whole(w_br_dn), whole(w_br_sb), whole(w_br_mem), whole(w_out), vec],
        out_specs=[pl.BlockSpec((1, LANE), lambda i: (0, 0)), row, vec, row, row, row, row,
                   pl.BlockSpec((tm, 3 * D_MODEL), lambda i: (i, 0)), row, row, rowm],
        out_shape=[jax.ShapeDtypeStruct((1, LANE), F32), jax.ShapeDtypeStruct((t, D_MODEL), F32),
                   jax.ShapeDtypeStruct((1, D_MODEL), F32), bf(D_MODEL), bf(D_MODEL), bf(D_MODEL), bf(D_MODEL),
                   bf(3 * D_MODEL), bf(D_MODEL), bf(D_MODEL), bf(MEM_W)],
    )(*([proj] * n_g), o_dn, o_sb, o_m, x, tgt, w_br_dn, w_br_sb, w_br_mem, w_out, fg)


def _shift_rows(x, s):
    t = x.shape[0]
    if s == 0:
        return x
    rolled = pltpu.roll(x, s % t, 0)
    row = _iota2(x.shape, 0)
    keep = row >= s if s > 0 else row < t + s
    return jnp.where(keep, rolled, 0.0)


def _conv_pre(x, w):
    return sum(_shift_rows(x, CONV_K - 1 - j) * w[j:j + 1, :] for j in range(CONV_K))


CONV_TC = 256


def _dn_conv(proj, conv_w):
    t = proj.shape[0]
    nb = 3 * D_MODEL // CONV_TC

    def body(x_ref, w_ref, c_ref):
        c_ref[...] = _silu(_conv_pre(x_ref[...], w_ref[...]))

    return pl.pallas_call(
        body,
        name="dn_conv",
        grid=(nb,),
        in_specs=[pl.BlockSpec((t, CONV_TC), lambda j: (0, j)), pl.BlockSpec((CONV_K, CONV_TC), lambda j: (0, j))],
        out_specs=pl.BlockSpec((t, CONV_TC), lambda j: (0, j)),
        out_shape=jax.ShapeDtypeStruct((t, 3 * D_MODEL), F32),
    )(proj, conv_w)


def _dn_conv_bwd(proj, conv_w, dc):
    t = proj.shape[0]
    nb = 3 * D_MODEL // CONV_TC

    def body(x_ref, w_ref, dc_ref, dx_ref, dw_ref):
        x = x_ref[...]
        w = w_ref[...]
        pre = _conv_pre(x, w)
        sg = _sigmoid(pre)
        dpre = dc_ref[...] * (sg * (1.0 + pre * (1.0 - sg)))
        ahead = [_shift_rows(dpre, -(CONV_K - 1 - j)) for j in range(CONV_K)]
        dx_ref[...] = sum(a * w[j:j + 1, :] for j, a in enumerate(ahead)).astype(BF16)
        dw_ref[...] = jnp.concatenate([jnp.sum(a * x, axis=0, keepdims=True) for a in ahead], axis=0)

    blk = pl.BlockSpec((t, CONV_TC), lambda j: (0, j))
    wblk = pl.BlockSpec((CONV_K, CONV_TC), lambda j: (0, j))
    return pl.pallas_call(
        body,
        name="dn_conv_bwd",
        grid=(nb,),
        in_specs=[blk, wblk, blk],
        out_specs=[blk, wblk],
        out_shape=[jax.ShapeDtypeStruct((t, 3 * D_MODEL), BF16), jax.ShapeDtypeStruct((CONV_K, 3 * D_MODEL), F32)],
    )(proj, conv_w, dc)


def _ba_columns(ba, hd):
    lane = _iota2(ba.shape, 1)
    bcol = jnp.sum(jnp.where(lane == hd, ba, 0.0), axis=1, keepdims=True)
    acol = jnp.sum(jnp.where(lane == N_HEADS + hd, ba, 0.0), axis=1, keepdims=True)
    return bcol, acol


def _head_scalar(row, hd):
    lane = _iota2(row.shape, 1)
    return jnp.sum(jnp.where(lane == hd, row, 0.0), axis=1, keepdims=True)


DN_HP = 8


def _dn_inputs(cq, ck, cv, ba_ref, z_ref, alog_ref, dtb_ref, heads, lanes):
    ba = ba_ref[...]
    cols = [_ba_columns(ba, hd) for hd in heads]
    return ([cq[:, ln] for ln in lanes], [ck[:, ln] for ln in lanes], [cv[:, ln] for ln in lanes],
            [c[0] for c in cols], [c[1] for c in cols], [z_ref[:, ln] for ln in lanes],
            [_head_scalar(alog_ref[...], hd) for hd in heads], [_head_scalar(dtb_ref[...], hd) for hd in heads])


def _dn_specs(nblk, reverse):
    w = DN_HP * LANE
    nq = D_MODEL // w

    def row(i):
        return nblk - 1 - i if reverse else i

    def colblk(b0):
        return pl.BlockSpec((SUPER, w), lambda i, h: (row(i), b0 + h))

    ba = pl.BlockSpec((SUPER, LANE), lambda i, h: (row(i), O_BA // LANE))
    vec = pl.BlockSpec((1, LANE), lambda i, h: (0, 0))
    st = pl.BlockSpec((1, DN_HP, D_HEAD, D_HEAD), lambda i, h: (row(i), h, 0, 0))
    return colblk, nq, ba, vec, st


def _dn_fwd(c, proj, alog_row, dtb_row, gn):
    t = c.shape[0]
    nblk = t // SUPER
    colblk, nq, ba, vec, st = _dn_specs(nblk, False)

    def body(cq, ck, cv, ba_ref, z_ref, alog_ref, dtb_ref, gn_ref, o_ref, s_ref, state):
        @pl.when(jnp.logical_and(pl.program_id(0) == 0, pl.program_id(1) == 0))
        def _():
            state[...] = jnp.zeros_like(state)

        heads = [pl.program_id(1) * DN_HP + j for j in range(DN_HP)]
        lanes = [slice(j * LANE, (j + 1) * LANE) for j in range(DN_HP)]
        s0 = [state[hd] for hd in heads]
        outs, s2 = _dn_block(*_dn_inputs(cq, ck, cv, ba_ref, z_ref, alog_ref, dtb_ref, heads, lanes), gn_ref[...], s0)
        for j, (hd, ln) in enumerate(zip(heads, lanes)):
            s_ref[0, j] = s0[j]
            o_ref[:, ln] = outs[j].astype(BF16)
            state[hd] = s2[j]

    return pl.pallas_call(
        body,
        name="dn_fwd",
        grid=(nblk, N_HEADS // DN_HP),
        in_specs=[colblk(0), colblk(nq), colblk(2 * nq), ba, colblk(O_Z_DN // (DN_HP * LANE)), vec, vec, vec],
        out_specs=[colblk(0), st],
        out_shape=[jax.ShapeDtypeStruct((t, D_MODEL), BF16),
                   jax.ShapeDtypeStruct((nblk, N_HEADS, D_HEAD, D_HEAD), F32)],
        scratch_shapes=[pltpu.VMEM((N_HEADS, D_HEAD, D_HEAD), F32)],
    )(c, c, c, proj, proj, alog_row, dtb_row, gn)


def _dn_bwd(c, proj, alog_row, dtb_row, gn, states, do):
    t = c.shape[0]
    nblk = t // SUPER
    colblk, nq, ba, vec, st = _dn_specs(nblk, True)
    assert nq == 1

    def body(cq, ck, cv, ba_ref, z_ref, alog_ref, dtb_ref, gn_ref, s_ref, do_ref,
             dc_ref, dz_ref, dba_ref, dsc_ref, dgn_ref, dstate):
        i = pl.program_id(0)
        hq = pl.program_id(1)

        @pl.when(jnp.logical_and(i == 0, hq == 0))
        def _():
            dstate[...] = jnp.zeros_like(dstate)
            dsc_ref[...] = jnp.zeros_like(dsc_ref)
            dgn_ref[...] = jnp.zeros_like(dgn_ref)

        @pl.when(hq == 0)
        def _():
            dba_ref[...] = jnp.zeros_like(dba_ref)

        lane = _iota2((SUPER, LANE), 1)
        lane1 = _iota2((1, LANE), 1)
        heads = [hq * DN_HP + j for j in range(DN_HP)]
        lanes = [slice(j * LANE, (j + 1) * LANE) for j in range(DN_HP)]
        ds_in = [dstate[hd] for hd in heads]
        s_in = [s_ref[0, j] for j in range(DN_HP)]
        _, vjp = jax.vjp(_dn_block, *_dn_inputs(cq, ck, cv, ba_ref, z_ref, alog_ref, dtb_ref, heads, lanes),
                         gn_ref[...], s_in)
        dq, dk, dv, dbc, dac, dz, dal, ddt, dgn, ds0 = vjp(([do_ref[:, ln].astype(F32) for ln in lanes], ds_in))
        dba = jnp.zeros((SUPER, LANE), F32)
        dal_row = jnp.zeros((1, LANE), F32)
        ddt_row = jnp.zeros((1, LANE), F32)
        for j, (hd, ln) in enumerate(zip(heads, lanes)):
            for part, d in enumerate((dq, dk, dv)):
                dc_ref[:, part * D_MODEL + j * LANE:part * D_MODEL + (j + 1) * LANE] = d[j]
            dz_ref[:, ln] = dz[j].astype(BF16)
            dstate[hd] = ds0[j]
            dba = dba + jnp.where(lane == hd, dbc[j], 0.0) + jnp.where(lane == N_HEADS + hd, dac[j], 0.0)
            dal_row = dal_row + jnp.where(lane1 == hd, dal[j], 0.0)
            ddt_row = ddt_row + jnp.where(lane1 == hd, ddt[j], 0.0)
        dba_ref[...] += dba
        dsc_ref[0:1, :] += dal_row
        dsc_ref[1:2, :] += ddt_row
        dgn_ref[...] += dgn

    outs = pl.pallas_call(
        body,
        name="dn_bwd",
        grid=(nblk, N_HEADS // DN_HP),
        in_specs=[colblk(0), colblk(nq), colblk(2 * nq), ba, colblk(O_Z_DN // (DN_HP * LANE)), vec, vec, vec, st,
                  colblk(0)],
        out_specs=[pl.BlockSpec((SUPER, 3 * D_MODEL), lambda i, h: (nblk - 1 - i, 0)), colblk(0),
                   pl.BlockSpec((SUPER, LANE), lambda i, h: (nblk - 1 - i, 0)),
                   pl.BlockSpec((2, LANE), lambda i, h: (0, 0)), vec],
        out_shape=[jax.ShapeDtypeStruct((t, 3 * D_MODEL), F32), jax.ShapeDtypeStruct((t, D_MODEL), BF16),
                   jax.ShapeDtypeStruct((t, LANE), F32), jax.ShapeDtypeStruct((2, LANE), F32),
                   jax.ShapeDtypeStruct((1, LANE), F32)],
        scratch_shapes=[pltpu.VMEM((N_HEADS, D_HEAD, D_HEAD), F32)],
    )(c, c, c, proj, proj, alog_row, dtb_row, gn, states, do)
    return outs


SB_TQ = 256
SB_TK = 256
SB_HP_FWD = 8
SB_HP_BWD = 4


def _sb_logits(z, mask):
    sp = jnp.log(1.0 + jnp.exp(-jnp.abs(z)))
    lf_raw = -(jnp.maximum(z, 0.0) + sp)
    lb = lf_raw + z
    lf = lf_raw if mask is None else jnp.where(mask, lf_raw, 0.0)
    return lb, lf_raw, lf


def _suffix_sums(x, sel):
    hi, lo = _split2(x)
    d = functools.partial(lax.dot_general, dimension_numbers=NN, preferred_element_type=F32)
    return d(hi, sel) + d(lo, sel)


def _sb_diag_mask(tq, r):
    return r * SB_TK + _iota2((tq, SB_TK), 1) < _iota2((tq, SB_TK), 0)


def _sb_specs(t, tq, hp):
    w = hp * LANE
    q0, k0, v0, z0 = (O_QKV_SB // w, (O_QKV_SB + D_MODEL) // w, (O_QKV_SB + 2 * D_MODEL) // w, O_Z_SB // w)

    def blk(b0):
        return pl.BlockSpec((tq, w), lambda h, i: (i, b0 + h))

    def full(b0, **kw):
        return pl.BlockSpec((t, w), lambda h, i: (0, b0 + h), **kw)

    once = dict(pipeline_mode=pl.Buffered(1))
    return blk(q0), full(k0, **once), full(v0, **once), blk(z0), blk(0), full(0)


def _sb_fwd(proj, shards):
    t = proj.shape[0]
    tq = min(SB_TQ, t)
    ndiag = tq // SB_TK
    scale = 1.0 / math.sqrt(D_HEAD)
    na = len(shards)

    def body(q_ref, k_ref, v_ref, z_ref, *rest):
        x_refs, (o_ref, oraw_ref), land = rest[:na], rest[na:na + 2], rest[na + 2:2 * na + 2]
        sems = rest[2 * na + 2:]
        qi = pl.program_id(1)
        first = jnp.logical_and(pl.program_id(0) == 0, qi == 0)
        last = jnp.logical_and(pl.program_id(0) == pl.num_programs(0) - 1, qi == pl.num_programs(1) - 1)

        @pl.when(first)
        def _():
            for cp in _direct_gather_copies(x_refs, land, *sems):
                cp.start()

        lanes = [slice(hd * LANE, (hd + 1) * LANE) for hd in range(SB_HP_FWD)]
        qs = [(q_ref[:, ln] * scale).astype(BF16) for ln in lanes]
        after = (_iota2((SB_TK, SB_TK), 0) > _iota2((SB_TK, SB_TK), 1)).astype(BF16)
        oraw_ref[...] = jnp.zeros_like(oraw_ref)

        def block(kb, mask, c_lf):
            rows = pl.ds(pl.multiple_of(kb * SB_TK, SB_TK), SB_TK)
            z = _each(lambda q, ln: _dot(q, k_ref[rows, ln], NT), qs, lanes)
            lg = _each(lambda x: _sb_logits(x, mask), z)
            surv = _each(lambda x: _suffix_sums(x[2], after), lg)
            att = _each(lambda x, s, c: jnp.exp(x[0] + s + c), lg, surv, c_lf)
            if mask is not None:
                att = _each(lambda a: jnp.where(mask, a, 0.0), att)
            pv = _each(lambda a, ln: _dot(a, v_ref[rows, ln], NN), att, lanes)
            for p, ln in zip(pv, lanes):
                oraw_ref[:, ln] += p
            return tuple(_each(lambda c, x: c + jnp.sum(x[2], axis=1, keepdims=True), c_lf, lg))

        carry = tuple(jnp.zeros((tq, 1), F32) for _ in range(SB_HP_FWD))
        for r in reversed(range(ndiag)):
            carry = block(qi * ndiag + r, _sb_diag_mask(tq, r), carry)
        lax.fori_loop(0, qi * ndiag, lambda i, c: block(qi * ndiag - 1 - i, None, c), carry)
        o_ref[...] = (oraw_ref[...] * _silu(z_ref[...])).astype(BF16)

        @pl.when(last)
        def _():
            for cp in _direct_gather_copies(x_refs, land, *sems):
                cp.wait()

    q_spec, k_spec, v_spec, z_spec, out, _ = _sb_specs(t, tq, SB_HP_FWD)
    outs = pl.pallas_call(
        body,
        name="sb_fwd",
        grid=(N_HEADS // SB_HP_FWD, t // tq),
        in_specs=[q_spec, k_spec, v_spec, z_spec] + [ANY] * na,
        out_specs=[out, out] + [ANY] * na,
        out_shape=[jax.ShapeDtypeStruct((t, D_MODEL), BF16), jax.ShapeDtypeStruct((t, D_MODEL), F32)]
        + [jax.ShapeDtypeStruct((N_DEV, *v.shape), v.dtype) for v in shards],
        scratch_shapes=_gather_sems(na),
    )(proj, proj, proj, proj, *shards)
    return outs[0], outs[1], outs[2:]


def _sb_bwd(proj, oraw, do, blocks):
    t = proj.shape[0]
    tq = min(SB_TQ, t)
    ndiag = tq // SB_TK
    scale = 1.0 / math.sqrt(D_HEAD)
    nb = len(blocks)

    def body(q_ref, k_ref, v_ref, z_ref, oraw_ref, do_ref, *rest):
        blk_refs, (dq_ref, dk_ref, dv_ref, dz_ref), land_refs = rest[:nb], rest[nb:nb + 4], rest[nb + 4:2 * nb + 4]
        dk_acc, dv_acc, p_scr, z_scr, send_sems, recv_sems, local_sems = rest[2 * nb + 4:]
        qi = pl.program_id(1)
        nq = pl.num_programs(1)
        hg = pl.program_id(0)
        me = _position()
        mine = 4 * me[0] + 2 * me[1] + me[2]

        def exchange():
            cps = []
            for a, (blk_ref, land_ref) in enumerate(zip(blk_refs, land_refs)):
                cps.append(pltpu.make_async_copy(blk_ref.at[mine], land_ref.at[mine], local_sems.at[a]))
                for k, peer in enumerate(_other_devices(me)):
                    cps.append(pltpu.make_async_remote_copy(
                        src_ref=blk_ref.at[4 * peer[0] + 2 * peer[1] + peer[2]], dst_ref=land_ref.at[mine],
                        send_sem=send_sems.at[7 * a + k], recv_sem=recv_sems.at[7 * a + k], device_id=peer,
                        device_id_type=MESH))
            return cps

        @pl.when(jnp.logical_and(hg == 0, qi == 0))
        def _():
            for cp in exchange():
                cp.start()

        @pl.when(qi == 0)
        def _():
            dk_acc[...] = jnp.zeros_like(dk_acc)
            dv_acc[...] = jnp.zeros_like(dv_acc)

        heads = range(SB_HP_BWD)
        lanes = [slice(hd * LANE, (hd + 1) * LANE) for hd in heads]
        zg = z_ref[...]
        sg = _sigmoid(zg)
        dog = do_ref[...].astype(F32)
        dz_ref[...] = (dog * oraw_ref[...] * (sg * (1.0 + zg * (1.0 - sg)))).astype(BF16)
        d_o = (dog * (zg * sg)).astype(BF16)
        d_o16 = [d_o[:, ln] for ln in lanes]
        qs = [(q_ref[:, ln] * scale).astype(BF16) for ln in lanes]
        ri = _iota2((SB_TK, SB_TK), 0)
        ci = _iota2((SB_TK, SB_TK), 1)
        after = (ri > ci).astype(BF16)
        earlier = (ri < ci).astype(BF16)

        def rows_of(kb):
            return pl.ds(pl.multiple_of(kb * SB_TK, SB_TK), SB_TK)

        def down(kb, mask, c_lf):
            rows = rows_of(kb)
            z = _each(lambda q, ln: _dot(q, k_ref[rows, ln], NT), qs, lanes)
            da = _each(lambda d, ln: _dot(d, v_ref[rows, ln], NT), d_o16, lanes)
            lg = _each(lambda x: _sb_logits(x, mask), z)
            surv = _each(lambda x: _suffix_sums(x[2], after), lg)
            att = _each(lambda x, s, c: jnp.exp(x[0] + s + c), lg, surv, c_lf)
            if mask is not None:
                att = _each(lambda a: jnp.where(mask, a, 0.0), att)
            dv = _each(lambda a, d: _dot(a, d, TN), att, d_o16)
            for hd in heads:
                p_scr[hd, kb] = att[hd] * da[hd]
                z_scr[hd, kb] = z[hd]
                dv_acc[rows, lanes[hd]] += dv[hd]
            return tuple(_each(lambda c, x: c + jnp.sum(x[2], axis=1, keepdims=True), c_lf, lg))

        c_lf = tuple(jnp.zeros((tq, 1), F32) for _ in heads)
        for r in reversed(range(ndiag)):
            c_lf = down(qi * ndiag + r, _sb_diag_mask(tq, r), c_lf)
        lax.fori_loop(0, qi * ndiag, lambda i, c: down(qi * ndiag - 1 - i, None, c), c_lf)

        def up(kb, mask, carry):
            dq, c_p = carry
            rows = rows_of(kb)
            p = [p_scr[hd, kb] for hd in heads]
            zs = [z_scr[hd, kb] for hd in heads]
            before = _each(lambda x, c: _suffix_sums(x, earlier) + c, p, c_p)
            e = _each(lambda x: jnp.exp(-jnp.abs(x)), zs)
            r = _each(lambda x: 1.0 / (1.0 + x), e)
            sig = _each(lambda x, a, b: jnp.where(x >= 0.0, b, a * b), zs, e, r)
            oms = _each(lambda x, a, b: jnp.where(x >= 0.0, a * b, b), zs, e, r)
            if mask is not None:
                sig = _each(lambda a: jnp.where(mask, a, 0.0), sig)
            dzz = _each(lambda x, o, g, b: x * o - g * b, p, oms, sig, before)
            dk = _each(lambda x, q: _dot(x, q, TN), dzz, qs)
            dq = _each(lambda a, x, ln: a + _dot(x, k_ref[rows, ln], NN), dq, dzz, lanes)
            for hd in heads:
                dk_acc[rows, lanes[hd]] += dk[hd]
            return tuple(dq), tuple(_each(lambda c, x: c + jnp.sum(x, axis=1, keepdims=True), c_p, p))

        carry = (tuple(jnp.zeros((tq, D_HEAD), F32) for _ in heads), tuple(jnp.zeros((tq, 1), F32) for _ in heads))
        carry = lax.fori_loop(0, qi * ndiag, lambda kb, c: up(kb, None, c), carry)
        for r in range(ndiag):
            carry = up(qi * ndiag + r, _sb_diag_mask(tq, r), carry)
        dq = carry[0]
        for hd in heads:
            dq_ref[:, lanes[hd]] = (dq[hd] * scale).astype(BF16)

        @pl.when(qi == nq - 1)
        def _():
            dk_ref[...] = dk_acc[...].astype(BF16)
            dv_ref[...] = dv_acc[...].astype(BF16)

        @pl.when(jnp.logical_and(hg == pl.num_programs(0) - 1, qi == nq - 1))
        def _():
            for cp in exchange():
                cp.wait()

    q_spec, k_spec, v_spec, z_spec, blk, full = _sb_specs(t, tq, SB_HP_BWD)
    o = jax.ShapeDtypeStruct((t, D_MODEL), BF16)
    w = SB_HP_BWD * LANE
    outs = pl.pallas_call(
        body,
        name="sb_bwd",
        grid=(N_HEADS // SB_HP_BWD, t // tq),
        in_specs=[q_spec, k_spec, v_spec, z_spec, blk, blk] + [ANY] * nb,
        out_specs=[blk, full, full, blk] + [ANY] * nb,
        out_shape=[o, o, o, o] + [jax.ShapeDtypeStruct(b.shape, b.dtype) for b in blocks],
        scratch_shapes=[pltpu.VMEM((t, w), F32), pltpu.VMEM((t, w), F32)]
        + [pltpu.VMEM((SB_HP_BWD, t // SB_TK, tq, SB_TK), F32)] * 2 + _gather_sems(nb),
    )(proj, proj, proj, proj, oraw, do, *blocks)
    return outs[0], outs[1], outs[2], outs[3], outs[4:]


def _mem_kv_fn(mem, mg, w):
    return mm_nn(_rmsnorm(mem, mg), w)


def _mem_kv(mem, mg, w):
    def body(m_ref, g_ref, w_ref, o_ref):
        o_ref[...] = _mem_kv_fn(m_ref[...], g_ref[...], w_ref[...])

    return pl.pallas_call(body, name="mem_kv", out_shape=jax.ShapeDtypeStruct((MEM_LEN, 2 * MEM_W), F32))(mem, mg, w)


def _mem_kv_bwd(mem, mg, w, dmkv):
    def body(m_ref, g_ref, w_ref, d_ref, dg_ref, dw_ref):
        _, vjp = jax.vjp(_mem_kv_fn, m_ref[...], g_ref[...], w_ref[...].astype(F32))
        _, dg, dw = vjp(d_ref[...])
        dg_ref[...] = dg
        dw_ref[...] = dw.astype(BF16)

    return pl.pallas_call(
        body, name="mem_kv_bwd",
        out_shape=[jax.ShapeDtypeStruct((1, D_MODEL), F32), jax.ShapeDtypeStruct((D_MODEL, 2 * MEM_W), BF16)],
    )(mem, mg, w, dmkv)


def _mem_attn(proj, mkv, tm=256):
    t = proj.shape[0]
    tm = min(tm, t)

    def body(q_ref, z_ref, kv_ref, o_ref):
        o_ref[...] = _mem_fn(q_ref[...], z_ref[...], kv_ref[...]).astype(BF16)

    return pl.pallas_call(
        body,
        name="mem_attn",
        grid=(t // tm,),
        in_specs=[pl.BlockSpec((tm, MEM_W), lambda i: (i, O_MQ // MEM_W)),
                  pl.BlockSpec((tm, MEM_W), lambda i: (i, O_MZ // MEM_W)),
                  pl.BlockSpec((MEM_LEN, 2 * MEM_W), lambda i: (0, 0))],
        out_specs=pl.BlockSpec((tm, MEM_W), lambda i: (i, 0)),
        out_shape=jax.ShapeDtypeStruct((t, MEM_W), BF16),
    )(proj, proj, mkv)


def _mem_attn_bwd(proj, mkv, do, tm=256):
    t = proj.shape[0]
    tm = min(tm, t)

    def body(q_ref, z_ref, kv_ref, do_ref, dq_ref, dz_ref, dkv_ref):
        _, vjp = jax.vjp(_mem_fn, q_ref[...], z_ref[...], kv_ref[...])
        dq, dz, dkv = vjp(do_ref[...].astype(F32))
        dq_ref[...] = dq.astype(BF16)
        dz_ref[...] = dz.astype(BF16)

        @pl.when(pl.program_id(0) == 0)
        def _():
            dkv_ref[...] = jnp.zeros_like(dkv_ref)

        dkv_ref[...] += dkv

    blk = pl.BlockSpec((tm, MEM_W), lambda i: (i, 0))
    kv = pl.BlockSpec((MEM_LEN, 2 * MEM_W), lambda i: (0, 0))
    return pl.pallas_call(
        body,
        name="mem_attn_bwd",
        grid=(t // tm,),
        in_specs=[pl.BlockSpec((tm, MEM_W), lambda i: (i, O_MQ // MEM_W)),
                  pl.BlockSpec((tm, MEM_W), lambda i: (i, O_MZ // MEM_W)), kv, blk],
        out_specs=[blk, blk, kv],
        out_shape=[jax.ShapeDtypeStruct((t, MEM_W), BF16), jax.ShapeDtypeStruct((t, MEM_W), BF16),
                   jax.ShapeDtypeStruct((MEM_LEN, 2 * MEM_W), F32)],
    )(proj, proj, mkv, do)


def _proj_gather(h, w_alt, shards, tm=512, tn=3968):
    t = h.shape[0]
    tm = min(tm, t)
    n, kdim = w_alt.shape
    assert n % tn == 0 and t % tm == 0
    nj, ni = n // tn, t // tm
    na = len(shards)

    def body(h_ref, w_ref, *rest):
        x_refs, o_ref, land = rest[:na], rest[na], rest[na + 1:2 * na + 1]
        send_sems, recv_sems, local_sems = rest[2 * na + 1:]
        j, i = pl.program_id(0), pl.program_id(1)

        def copies():
            return _direct_gather_copies(x_refs, land, send_sems, recv_sems, local_sems)

        @pl.when(jnp.logical_and(j == 0, i == 0))
        def _():
            for cp in copies():
                cp.start()

        o_ref[...] = _dot(h_ref[...], w_ref[...], NT)

        @pl.when(jnp.logical_and(j == nj - 1, i == ni - 1))
        def _():
            for cp in copies():
                cp.wait()

    outs = pl.pallas_call(
        body,
        name="proj",
        grid=(nj, ni),
        in_specs=[pl.BlockSpec((tm, kdim), lambda j, i: (i, 0)), pl.BlockSpec((tn, kdim), lambda j, i: (j, 0))]
        + [ANY] * na,
        out_specs=[pl.BlockSpec((tm, tn), lambda j, i: (i, j))] + [ANY] * na,
        out_shape=[jax.ShapeDtypeStruct((t, n), F32)]
        + [jax.ShapeDtypeStruct((N_DEV, *v.shape), v.dtype) for v in shards],
        scratch_shapes=_gather_sems(na),
    )(h, w_alt, *shards)
    return outs[0], outs[1:]


def _local_step(x, mem, tgt, norm_g, mem_norm_g, w_alt, alog_row, dtb_row, dn_norm_g, final_g, shards):
    h = _norm_in(x, norm_g)
    s_kv, s_dn, s_sb, s_out, s_mem, s_conv = shards
    proj, (g_kv, g_conv) = _proj_gather(h, w_alt, [s_kv, s_conv])
    w_mem_kv = g_kv.reshape(D_MODEL, 2 * MEM_W)
    conv_w = g_conv.transpose(1, 0, 2).reshape(CONV_K, 3 * D_MODEL)

    c = _dn_conv(proj, conv_w)
    o_dn, states = _dn_fwd(c, proj, alog_row, dtb_row, dn_norm_g)
    o_sb, o_sb_raw, (g_dn, g_sb, g_out, g_mem) = _sb_fwd(proj, [s_dn, s_sb, s_out, s_mem])
    w_br_dn = g_dn.reshape(D_MODEL, D_MODEL)
    w_br_sb = g_sb.reshape(D_MODEL, D_MODEL)
    w_out = g_out.reshape(D_MODEL, D_MODEL)
    w_br_mem = g_mem.transpose(1, 0, 2).reshape(MEM_W, D_MODEL)
    mkv = _mem_kv(mem, mem_norm_g, w_mem_kv)
    o_m = _mem_attn(proj, mkv)

    (loss, dout, d_final_g, merged, dy_dn, dy_sb, dy_m, dgates, do_dn, do_sb, do_m) = _block_tail(
        proj, o_dn, o_sb, o_m, x, tgt, w_br_dn, w_br_sb, w_br_mem, w_out, final_g)
    dw_out = _matmul_tn(merged, dout, BF16, 256, 1024, 2048, "dw_out")
    dw_br_dn = _matmul_tn(o_dn, dy_dn, BF16, 256, 1024, 2048, "dw_br_dn")
    dw_br_sb = _matmul_tn(o_sb, dy_sb, BF16, 256, 1024, 2048, "dw_br_sb")
    dw_br_mem = _matmul_tn(o_m, dy_m, BF16, 256, 1024, 2048, "dw_br_mem")

    dmq, dmz, dmkv = _mem_attn_bwd(proj, mkv, do_m)
    d_mem_norm_g, dw_mem_kv = _mem_kv_bwd(mem, mem_norm_g, w_mem_kv, dmkv)
    rows_d = D_MODEL // N_DEV
    small_blocks = [
        dw_br_dn.reshape(N_DEV, rows_d, D_MODEL), dw_br_sb.reshape(N_DEV, rows_d, D_MODEL),
        dw_out.reshape(N_DEV, rows_d, D_MODEL), dw_mem_kv.reshape(N_DEV, rows_d // 2, D_MODEL),
        dw_br_mem.reshape(MEM_W, N_DEV, rows_d).transpose(1, 0, 2).reshape(N_DEV, MEM_W // N_DEV, D_MODEL)]
    dq_sb, dk_sb, dv_sb, dz_sb, small_parts = _sb_bwd(proj, o_sb_raw, do_sb, small_blocks)
    d_small = _sum_slots(list(small_parts), "sum_small_grads")
    dc, dz_dn, dba, dscal, d_dn_norm_g = _dn_bwd(c, proj, alog_row, dtb_row, dn_norm_g, states, do_dn)
    dqkv_dn, d_conv_w = _dn_conv_bwd(proj, conv_w, dc)

    dproj = [dqkv_dn, dz_dn, dq_sb, dk_sb, dv_sb, dz_sb, dmq, dmz, dgates, dba.astype(BF16)]
    dw_alt = _dw_alt(dproj, h)
    grad_x, d_norm_g = _grad_x(dproj, w_alt, x, norm_g, dout)
    return dict(loss=loss, grad_x=grad_x, norm_g=d_norm_g, mem_norm_g=d_mem_norm_g, w_alt=dw_alt, conv_w=d_conv_w,
                scal=dscal, dn_norm_g=d_dn_norm_g, small=d_small, final_g=d_final_g)


MESH = pl.DeviceIdType.MESH
ANY = pl.BlockSpec(memory_space=pl.ANY)


def _position():
    return lax.axis_index("x"), lax.axis_index("y"), lax.axis_index("c")


def _other_devices(me):
    return [tuple(1 - p if (f >> s) & 1 else p for p, s in zip(me, (2, 1, 0))) for f in range(1, N_DEV)]


def _direct_gather_copies(x_refs, land_refs, send_sems, recv_sems, local_sems):
    me = _position()
    mine = 4 * me[0] + 2 * me[1] + me[2]
    cps = []
    for a, (x_ref, land) in enumerate(zip(x_refs, land_refs)):
        cps.append(pltpu.make_async_copy(x_ref, land.at[mine], local_sems.at[a]))
        for k, peer in enumerate(_other_devices(me)):
            cps.append(pltpu.make_async_remote_copy(
                src_ref=x_ref, dst_ref=land.at[mine], send_sem=send_sems.at[7 * a + k],
                recv_sem=recv_sems.at[7 * a + k], device_id=peer, device_id_type=MESH))
    return cps


def _gather_sems(n):
    return [pltpu.SemaphoreType.DMA((7 * n,)), pltpu.SemaphoreType.DMA((7 * n,)), pltpu.SemaphoreType.DMA((n,))]


def _all_gather(xs, name):
    n = len(xs)

    def body(*refs):
        x_refs, o_refs = refs[:n], refs[n:2 * n]
        send_sems, recv_sems, local_sems = refs[2 * n:]
        x, y, c = _position()
        me, sibling = (x, y, c), (x, y, 1 - c)
        x_nbr, y_nbr, diag = (1 - x, y, c), (x, 1 - y, c), (1 - x, 1 - y, c)
        south = c == 0
        relay_from = tuple(jnp.where(south, a, b) for a, b in zip(y_nbr, x_nbr))
        relay_to = tuple(jnp.where(south, a, b) for a, b in zip(x_nbr, y_nbr))

        def slot(p):
            return 4 * p[0] + 2 * p[1] + p[2]

        def copy(a, k, block, to, src=None):
            dst = o_refs[a].at[slot(block)]
            return pltpu.make_async_remote_copy(
                src_ref=dst if src is None else src, dst_ref=dst, send_sem=send_sems.at[7 * a + k],
                recv_sem=recv_sems.at[7 * a + k], device_id=to, device_id_type=MESH)

        mine = [pltpu.make_async_copy(x_refs[a], o_refs[a].at[slot(me)], local_sems.at[a]) for a in range(n)]
        for cp in mine:
            cp.start()
        sends = []
        for a in range(n):
            sends += [copy(a, 0, me, sibling, src=x_refs[a]), copy(a, 1, me, x_nbr, src=x_refs[a]),
                      copy(a, 2, me, y_nbr, src=x_refs[a])]
        for cp in sends:
            cp.start()
        later = []
        for a in range(n):
            copy(a, 1, x_nbr, me).wait_recv()
            copy(a, 2, y_nbr, me).wait_recv()
            later += [copy(a, 3, relay_from, relay_to), copy(a, 4, x_nbr, sibling), copy(a, 5, y_nbr, sibling)]
            for cp in later[-3:]:
                cp.start()
        for a in range(n):
            copy(a, 3, diag, me).wait_recv()
            later.append(copy(a, 6, diag, sibling))
            later[-1].start()
        for a in range(n):
            copy(a, 0, sibling, me).wait_recv()
            for k, chip in ((4, x_nbr), (5, y_nbr), (6, diag)):
                copy(a, k, (chip[0], chip[1], 1 - c), me).wait_recv()
        for cp in sends + later:
            cp.wait_send()
        for cp in mine:
            cp.wait()

    return pl.pallas_call(
        body,
        name=name,
        in_specs=[ANY] * n,
        out_specs=[ANY] * n,
        out_shape=[jax.ShapeDtypeStruct((N_DEV, *v.shape), v.dtype) for v in xs],
        scratch_shapes=[pltpu.SemaphoreType.DMA((7 * n,)), pltpu.SemaphoreType.DMA((7 * n,)),
                        pltpu.SemaphoreType.DMA((n,))],
    )(*xs)


def _window_view(ref, dest):
    return ref.at[pl.ds(WIN_ROW0[dest], WIN_W), :]


def _chunk_rows(rows, cols):
    return max(ch for ch in range(ROW_TILE, rows + 1, ROW_TILE) if rows % ch == 0 and ch * cols <= (1 << 20))


def _halving_stage(xs, axis, name, out_dtype, windowed=(), gather=()):
    n_arr = len(xs)
    metas = []
    for k, v in enumerate(xs):
        if k in windowed:
            metas.append((N_DEV // 2, WIN_W, v.shape[1]))
        else:
            assert v.shape[1] == 2
            metas.append((v.shape[0], v.shape[2], v.shape[3]))
    chunk = [_chunk_rows(r, c) for (_, r, c) in metas]
    offs = [sum(m[0] for m in metas[:k]) for k in range(n_arr)]
    n_sem = sum(m[0] for m in metas)

    n_g = len(gather)

    def body(*refs):
        x_refs, g_refs = refs[:n_arr], refs[n_arr:n_arr + n_g]
        outs = refs[n_arr + n_g:]
        o_refs, land_refs, gl_refs = outs[:n_arr], outs[n_arr:2 * n_arr], outs[2 * n_arr:2 * n_arr + n_g]
        rest = outs[2 * n_arr + n_g:]
        bufs = rest[:3 * n_arr]
        send_sems, recv_sems, in_sems, out_sems = rest[3 * n_arr:3 * n_arr + 4]
        gathers = _direct_gather_copies(g_refs, gl_refs, *rest[3 * n_arr + 4:]) if n_g else []
        for cp in gathers:
            cp.start()
        pos = dict(zip("xyc", _position()))
        bit = pos[axis]
        peer = tuple(1 - pos[a] if a == axis else pos[a] for a in "xyc")

        def view(k, i, b):
            if k in windowed:
                return _window_view(x_refs[k], 2 * i + b)
            return x_refs[k].at[i, b]

        def add_blocks(k, a_view, b_view, o_view):
            _hbm_add(a_view, b_view, o_view, bufs[3 * k:3 * k + 3], in_sems, out_sems, chunk[k])

        for b in (0, 1):
            @pl.when(bit == b)
            def _(b=b):
                sends = []
                for k in range(n_arr):
                    for i in range(metas[k][0]):
                        cp = pltpu.make_async_remote_copy(
                            src_ref=view(k, i, 1 - b), dst_ref=land_refs[k].at[i], send_sem=send_sems.at[offs[k] + i],
                            recv_sem=recv_sems.at[offs[k] + i], device_id=peer, device_id_type=MESH)
                        cp.start()
                        sends.append(cp)
                idx = 0
                for k in range(n_arr):
                    for i in range(metas[k][0]):
                        sends[idx].wait_recv()
                        add_blocks(k, view(k, i, b), land_refs[k].at[i], o_refs[k].at[i])
                        idx += 1
                for cp in sends:
                    cp.wait_send()

        for cp in gathers:
            cp.wait()

    out_shape = [jax.ShapeDtypeStruct(m, out_dtype) for m in metas]
    land_shape = [jax.ShapeDtypeStruct(m, v.dtype) for m, v in zip(metas, xs)]
    g_shape = [jax.ShapeDtypeStruct((N_DEV, *v.shape), v.dtype) for v in gather]
    scratch = []
    for k in range(n_arr):
        blk = (2, chunk[k], metas[k][2])
        scratch += [pltpu.VMEM(blk, xs[k].dtype)] * 2 + [pltpu.VMEM(blk, out_dtype)]
    scratch += [pltpu.SemaphoreType.DMA((n_sem,)), pltpu.SemaphoreType.DMA((n_sem,)),
                pltpu.SemaphoreType.DMA((2, 2)), pltpu.SemaphoreType.DMA((2,))]
    if n_g:
        scratch += _gather_sems(n_g)
    outs = pl.pallas_call(
        body,
        name=name,
        in_specs=[ANY] * (n_arr + n_g),
        out_specs=[ANY] * (2 * n_arr + n_g),
        out_shape=out_shape + land_shape + g_shape,
        scratch_shapes=scratch,
    )(*xs, *gather)
    return outs[:n_arr], outs[2 * n_arr:]


def _hbm_add(a_view, b_view, o_view, bufs, in_sems, out_sems, ch):
    rows = a_view.shape[0]
    nch = rows // ch
    va, vb, vo = bufs

    def rows_of(j):
        return pl.ds(pl.multiple_of(j * ch, 16), ch)

    def loads(j, s):
        return (pltpu.make_async_copy(a_view.at[rows_of(j), :], va.at[s], in_sems.at[0, s]),
                pltpu.make_async_copy(b_view.at[rows_of(j), :], vb.at[s], in_sems.at[1, s]))

    def store(j, s):
        return pltpu.make_async_copy(vo.at[s], o_view.at[rows_of(j), :], out_sems.at[s])

    for cp in loads(0, 0):
        cp.start()

    def step(j, _):
        s = lax.rem(j, 2)

        @pl.when(j + 1 < nch)
        def _():
            for cp in loads(j + 1, 1 - s):
                cp.start()

        for cp in loads(j, s):
            cp.wait()

        @pl.when(j >= 2)
        def _():
            store(j - 2, s).wait()

        vo[s] = (va[s].astype(F32) + vb[s].astype(F32)).astype(vo.dtype)
        store(j, s).start()
        return 0

    lax.fori_loop(0, nch, step, 0)
    for j in range(max(0, nch - 2), nch):
        store(j, j % 2).wait()


def _xy_stage(xs, first, name):
    n_arr = len(xs)
    if first:
        shapes = [(v.shape[2] // 2, v.shape[3]) for v in xs]
        ins = list(xs)
    else:
        shapes = [(a.shape[1], a.shape[2]) for a, _ in xs]
        ins = [v for pair in xs for v in pair]
    n_blk = 2 if first else 1
    out_dtype = BF16 if first else F32
    chunk = [_chunk_rows(r, c) for (r, c) in shapes]
    n_sem = 2 * n_blk * n_arr

    def body(*refs):
        n_in = len(ins)
        in_refs = refs[:n_in]
        n_out = 2 * n_arr if first else n_arr
        o_refs = refs[n_in:n_in + n_out]
        land = refs[n_in + n_out:n_in + n_out + 2 * n_arr]
        rest = refs[n_in + n_out + 2 * n_arr:]
        bufs = rest[:3 * n_arr]
        send_sems, recv_sems, in_sems, out_sems = rest[3 * n_arr:]
        x, y, c = _position()
        peers = {"x": (1 - x, y, c), "y": (x, 1 - y, c)}
        jobs = []
        for k in range(n_arr):
            r, _ = shapes[k]
            half_a, half_b = pl.ds(0, r), pl.ds(r, r)
            if first:
                src = in_refs[k]
                for i in range(2):
                    jobs.append((k, src.at[i, 1 - y, half_a, :], src.at[i, y, half_a, :], land[2 * k].at[i],
                                 o_refs[2 * k].at[i], "y"))
                    jobs.append((k, src.at[1 - x, i, half_b, :], src.at[x, i, half_b, :], land[2 * k + 1].at[i],
                                 o_refs[2 * k + 1].at[i], "x"))
            else:
                a1, b1 = in_refs[2 * k], in_refs[2 * k + 1]
                jobs.append((k, a1.at[1 - x], a1.at[x], land[2 * k], o_refs[k].at[half_a, :], "x"))
                jobs.append((k, b1.at[1 - y], b1.at[y], land[2 * k + 1], o_refs[k].at[half_b, :], "y"))
        sends = []
        for n, (k, send, _, landing, _, axis) in enumerate(jobs):
            cp = pltpu.make_async_remote_copy(src_ref=send, dst_ref=landing, send_sem=send_sems.at[n],
                                              recv_sem=recv_sems.at[n], device_id=peers[axis], device_id_type=MESH)
            cp.start()
            sends.append(cp)
        for cp, (k, _, kept, landing, out, _) in zip(sends, jobs):
            cp.wait_recv()
            _hbm_add(kept, landing, out, bufs[3 * k:3 * k + 3], in_sems, out_sems, chunk[k])
        for cp in sends:
            cp.wait_send()

    if first:
        out_shape = [jax.ShapeDtypeStruct((2, r, c), BF16) for (r, c) in shapes for _ in range(2)]
        land_shape = out_shape
    else:
        out_shape = [jax.ShapeDtypeStruct((2 * r, c), F32) for (r, c) in shapes]
        land_shape = [jax.ShapeDtypeStruct((r, c), BF16) for (r, c) in shapes for _ in range(2)]
    scratch = []
    for k in range(n_arr):
        scratch += [pltpu.VMEM((2, chunk[k], shapes[k][1]), BF16)] * 2 + [pltpu.VMEM((2, chunk[k], shapes[k][1]), out_dtype)]
    scratch += [pltpu.SemaphoreType.DMA((n_sem,)), pltpu.SemaphoreType.DMA((n_sem,)),
                pltpu.SemaphoreType.DMA((2, 2)), pltpu.SemaphoreType.DMA((2,))]
    outs = pl.pallas_call(
        body,
        name=name,
        in_specs=[ANY] * len(ins),
        out_specs=[ANY] * (len(out_shape) + len(land_shape)),
        out_shape=out_shape + land_shape,
        scratch_shapes=scratch,
    )(*ins)
    outs = outs[:len(out_shape)]
    return [(outs[2 * k], outs[2 * k + 1]) for k in range(n_arr)] if first else list(outs)


def _reduce_scatter(dw_al, blocks, gather=()):
    xs = [dw_al] + [b.reshape(N_DEV // 2, 2, *b.shape[1:]) for b in blocks]
    ys, gathered = _halving_stage(xs, "c", "rs_c", BF16, windowed=(0,), gather=gather)
    pairs = _xy_stage([v.reshape(2, 2, *v.shape[1:]) for v in ys], True, "rs_xy1")
    return _xy_stage(pairs, False, "rs_xy2"), gathered


def _sum_slots(gs, name):
    n = len(gs)

    def body(*refs):
        for g_ref, o_ref in zip(refs[:n], refs[n:]):
            acc = g_ref[0].astype(F32)
            for d in range(1, N_DEV):
                acc = acc + g_ref[d].astype(F32)
            o_ref[...] = acc

    return pl.pallas_call(body, name=name, out_shape=[jax.ShapeDtypeStruct(g.shape[1:], F32) for g in gs])(*gs)


def _assemble_w_al(wins, bas):
    cols = wins.shape[2]
    n_buf = 3
    ends = [WIN_ROW0[d + 1] if d + 1 < N_DEV else WIN_ROW0[d] + WIN_W for d in range(N_DEV)]
    tail = W_AL - ends[-1]

    def body(w_ref, ba_ref, o_ref, buf, zeros, ld_sems, st_sems, ba_sem):
        def load(d):
            return pltpu.make_async_copy(w_ref.at[d], buf.at[d % n_buf], ld_sems.at[d % n_buf])

        def store(d):
            n = ends[d] - WIN_ROW0[d]
            return pltpu.make_async_copy(buf.at[d % n_buf, pl.ds(0, n), :],
                                         o_ref.at[pl.ds(WIN_ROW0[d], n), :], st_sems.at[d % n_buf])

        zeros[...] = jnp.zeros_like(zeros)
        fill = pltpu.make_async_copy(zeros, o_ref.at[pl.ds(ends[-1], tail), :], ba_sem)
        fill.start()
        fill.wait()
        load(0).start()
        for d in range(N_DEV):
            if d + 1 < N_DEV:
                if d + 1 >= n_buf:
                    store(d + 1 - n_buf).wait()
                load(d + 1).start()
            load(d).wait()
            if d > 0:
                ov = WIN_ROW0[d - 1] + WIN_W - WIN_ROW0[d]
                buf[d % n_buf, :ov, :] = buf[d % n_buf, :ov, :] + buf[(d - 1) % n_buf, WIN_W - ov:, :]
            if d == N_DEV - 1:
                ba_copy = pltpu.make_async_copy(
                    ba_ref.at[BA_DEV], buf.at[d % n_buf, pl.ds(WIN_W - N_BA, N_BA), :], ba_sem)
                ba_copy.start()
                ba_copy.wait()
            store(d).start()
        for d in range(N_DEV - n_buf, N_DEV):
            store(d).wait()

    return pl.pallas_call(
        body,
        name="assemble_w_al",
        in_specs=[ANY, ANY],
        out_specs=ANY,
        out_shape=jax.ShapeDtypeStruct((W_AL, cols), wins.dtype),
        scratch_shapes=[pltpu.VMEM((n_buf, WIN_W, cols), wins.dtype), pltpu.VMEM((tail, cols), wins.dtype),
                        pltpu.SemaphoreType.DMA((n_buf,)), pltpu.SemaphoreType.DMA((n_buf,)), pltpu.SemaphoreType.DMA],
    )(wins, bas)


def _adamw_math(w, g, m, v):
    m_new = ADAM_B1 * m + (1.0 - ADAM_B1) * g
    v_new = ADAM_B2 * v + (1.0 - ADAM_B2) * (g * g)
    m_hat = m_new / (1.0 - ADAM_B1 ** ADAM_STEP)
    v_hat = v_new / (1.0 - ADAM_B2 ** ADAM_STEP)
    return -ADAM_LR * (m_hat / (jnp.sqrt(v_hat) + ADAM_EPS) + ADAM_WD * w), m_new, v_new


def _adamw(w, g, m, v, name, tb=134):
    r, _, c = w.shape
    assert r % tb == 0

    def body(w_ref, g_ref, m_ref, v_ref, d_ref, nm_ref, nv_ref):
        d_ref[...], nm_ref[...], nv_ref[...] = _adamw_math(w_ref[...], g_ref[...], m_ref[...], v_ref[...])

    blk = pl.BlockSpec((tb, 1, c), lambda i: (i, 0, 0))
    o = jax.ShapeDtypeStruct(w.shape, F32)
    return pl.pallas_call(body, name=name, grid=(r // tb,), in_specs=[blk] * 4, out_specs=[blk] * 3,
                          out_shape=[o, o, o])(w, g, m, v)


def _adamw_many(ws, gs, ms, vs, name):
    n = len(ws)

    def body(*refs):
        for k in range(n):
            w_ref, g_ref, m_ref, v_ref = (refs[j * n + k] for j in range(4))
            d_ref, nm_ref, nv_ref = (refs[(4 + j) * n + k] for j in range(3))
            d_ref[...], nm_ref[...], nv_ref[...] = _adamw_math(w_ref[...], g_ref[...], m_ref[...], v_ref[...])

    shapes = [jax.ShapeDtypeStruct(w.shape, F32) for w in ws]
    outs = pl.pallas_call(body, name=name, out_shape=shapes * 3)(*ws, *gs, *ms, *vs)
    return outs[:n], outs[n:2 * n], outs[2 * n:]


def _select(me, table):
    return sum(jnp.where(me == d, jnp.int32(v), jnp.int32(0)) for d, v in enumerate(table))


WIN_SHIFT = tuple(SHARD_W * d - WIN_ROW0[d] for d in range(N_DEV))
PAD_L = 64
PAD_R = 64
assert max(WIN_SHIFT) <= PAD_L and WIN_W + N_BA - SHARD_W <= PAD_R


def _shard_to_window(shard_t, me):
    shift = _select(me, WIN_SHIFT)
    padded = jnp.pad(shard_t, ((PAD_L, PAD_R), (0, 0)))
    cols = shard_t.shape[1]
    lo = lax.dynamic_slice(padded, (PAD_L - shift, 0), (WIN_W, cols))
    hi = lax.dynamic_slice(padded, (PAD_L - shift + N_BA, 0), (WIN_W, cols))
    aligned = _select(me, WIN_ROW0) + lax.broadcasted_iota(jnp.int32, (WIN_W, 1), 0)
    return jnp.where(aligned >= ORIG_BA, hi, lo)


def _window_to_shard(win, ba_grad, me):
    shift = _select(me, WIN_SHIFT)
    cols = win.shape[1]
    padded = jnp.pad(win, ((N_BA, PAD_R), (0, 0)))
    lo = lax.dynamic_slice(padded, (N_BA + shift, 0), (SHARD_W, cols))
    hi = lax.dynamic_slice(padded, (shift, 0), (SHARD_W, cols))
    orig = SHARD_W * me + lax.broadcasted_iota(jnp.int32, (SHARD_W, 1), 0)
    ba_full = lax.dynamic_update_slice(jnp.zeros((SHARD_W, cols), win.dtype), ba_grad, (BA_LOCAL, 0))
    return jnp.where(orig < ORIG_BA, lo, jnp.where(orig >= ORIG_BA + N_BA, hi, ba_full))


def _pad_row(v, width=D_MODEL):
    v = v.reshape(1, -1)
    return jnp.pad(v, ((0, 0), (0, width - v.shape[1])))


def kernel(x, mem, norm_g, mem_norm_g, w_in, conv_w, a_log, dt_bias, dn_norm_g, w_mem_kv, w_br_dn, w_br_sb, w_br_mem, w_out, final_g, loss_target, m_norm_g, m_mem_norm_g, m_w_in, m_conv_w, m_a_log, m_dt_bias, m_dn_norm_g, m_w_mem_kv, m_w_br_dn, m_w_br_sb, m_w_br_mem, m_w_out, m_final_g, v_norm_g, v_mem_norm_g, v_w_in, v_conv_w, v_a_log, v_dt_bias, v_dn_norm_g, v_w_mem_kv, v_w_br_dn, v_w_br_sb, v_w_br_mem, v_w_out, v_final_g):
    xi, yi, ci = _position()
    me = 4 * xi + 2 * yi + ci

    shard_t = w_in[0].T
    win = _shard_to_window(shard_t, me).astype(BF16)
    ba = shard_t[BA_LOCAL:BA_LOCAL + N_BA, :].astype(BF16)
    g_win, g_ba = _all_gather([win, ba], "gather_weights")
    w_alt = _assemble_w_al(g_win, g_ba)

    shards = [w_mem_kv[0].astype(BF16), w_br_dn[0].astype(BF16), w_br_sb[0].astype(BF16), w_out[0].astype(BF16),
              w_br_mem[0].astype(BF16), conv_w[0]]
    r = _local_step(x[0], mem[0], loss_target[0], norm_g, mem_norm_g, w_alt, _pad_row(a_log, LANE),
                    _pad_row(dt_bias, LANE), dn_norm_g, final_g.reshape(1, D_MODEL), shards)

    dw_alt = r["w_alt"]
    parts = [r["norm_g"], r["mem_norm_g"], r["final_g"], r["dn_norm_g"], r["scal"], r["loss"], r["conv_w"],
             dw_alt[O_BA:O_BA + N_BA, :].astype(F32)]
    (g_win,), gathered = _reduce_scatter(dw_alt, [], gather=parts)
    rows_d = D_MODEL // N_DEV
    g_dn, g_sb, g_out, g_kv, g_mem = r["small"]
    g_kv = g_kv.reshape(rows_d, 2 * MEM_W)
    g_mem = g_mem.reshape(MEM_W, rows_d)
    s_norm_g, s_mem_norm_g, s_final_g, s_dn_norm_g, s_scal, s_loss, s_conv, s_ba = _sum_slots(gathered, "sum_small")
    loss = s_loss[0, 0]
    cw = conv_w.shape[2]
    g_conv = lax.dynamic_slice(s_conv, (0, cw * me), (CONV_K, cw))
    g_w_in_t = _window_to_shard(g_win, s_ba, me)
    grads = dict(norm_g=s_norm_g, mem_norm_g=s_mem_norm_g, w_in=g_w_in_t.T[None], conv_w=g_conv[None],
                 a_log=s_scal[0:1, :N_HEADS], dt_bias=s_scal[1:2, :N_HEADS], dn_norm_g=s_dn_norm_g, w_mem_kv=g_kv[None],
                 w_br_dn=g_dn[None], w_br_sb=g_sb[None], w_br_mem=g_mem[None], w_out=g_out[None],
                 final_g=s_final_g.reshape(D_MODEL))

    params = dict(norm_g=(norm_g, m_norm_g, v_norm_g), mem_norm_g=(mem_norm_g, m_mem_norm_g, v_mem_norm_g),
                  w_in=(w_in, m_w_in, v_w_in), conv_w=(conv_w, m_conv_w, v_conv_w), a_log=(a_log, m_a_log, v_a_log),
                  dt_bias=(dt_bias, m_dt_bias, v_dt_bias), dn_norm_g=(dn_norm_g, m_dn_norm_g, v_dn_norm_g),
                  w_mem_kv=(w_mem_kv, m_w_mem_kv, v_w_mem_kv), w_br_dn=(w_br_dn, m_w_br_dn, v_w_br_dn),
                  w_br_sb=(w_br_sb, m_w_br_sb, v_w_br_sb), w_br_mem=(w_br_mem, m_w_br_mem, v_w_br_mem),
                  w_out=(w_out, m_w_out, v_w_out), final_g=(final_g, m_final_g, v_final_g))
    order = list(params)
    deltas, new_m, new_v = {}, {}, {}
    deltas["w_in"], new_m["w_in"], new_v["w_in"] = (jnp.transpose(o, (1, 2, 0)) for o in _adamw(
        jnp.transpose(w_in, (2, 0, 1)), g_w_in_t[:, None, :], jnp.transpose(m_w_in, (2, 0, 1)),
        jnp.transpose(v_w_in, (2, 0, 1)), "adamw_w_in"))
    rest = [nm for nm in order if nm != "w_in"]

    def two_d(a):
        return a.reshape(1, -1) if a.ndim == 1 else a

    d_l, m_l, v_l = _adamw_many([two_d(params[nm][0]) for nm in rest], [two_d(grads[nm]) for nm in rest],
                                [two_d(params[nm][1]) for nm in rest], [two_d(params[nm][2]) for nm in rest], "adamw_rest")
    for k, nm in enumerate(rest):
        shp = params[nm][0].shape
        deltas[nm], new_m[nm], new_v[nm] = d_l[k].reshape(shp), m_l[k].reshape(shp), v_l[k].reshape(shp)
    return (loss, r["grad_x"][None], *[grads[nm] for nm in order], *[deltas[nm] for nm in order],
            *[new_m[nm] for nm in order], *[new_v[nm] for nm in order])
```

```python
import functools
import math

import jax
import jax.numpy as jnp
from jax import lax
from jax.experimental import pallas as pl
from jax.experimental.pallas import tpu as pltpu

F32 = jnp.float32
BF16 = jnp.bfloat16

D_MODEL = 1024
N_DEV = 8
N_HEADS = 8
D_HEAD = 128
DN_CHUNK = 64
CONV_K = 4
MEM_LEN = 256
MEM_HEADS = 4
MEM_DH = 64
MEM_W = MEM_HEADS * MEM_DH
NORM_EPS = 1e-6
IN_WIDTH = 11792
SHARD_W = IN_WIDTH // N_DEV

LANE = 128
SUPER = 2 * DN_CHUNK

O_QKV_DN = 0
O_Z_DN = 3072
O_QKV_SB = 4096
O_Z_SB = 7168
O_MQ = 8192
O_MZ = 8448
O_GATES = 8704
O_BA = 11776
W_AL = 11904
ORIG_BA = 4096
N_BA = 16

BA_DEV = ORIG_BA // SHARD_W
BA_LOCAL = ORIG_BA - BA_DEV * SHARD_W


def _aligned_col(o):
    return o if o < ORIG_BA else o - N_BA


ROW_TILE = 16
WIN_W = 1504
WIN_ROW0 = tuple(_aligned_col(SHARD_W * d) // ROW_TILE * ROW_TILE for d in range(N_DEV))
assert not any(ORIG_BA <= SHARD_W * d < ORIG_BA + N_BA for d in range(N_DEV))
assert all(WIN_ROW0[d] + WIN_W >= _aligned_col(SHARD_W * (d + 1) - 1) + 1 for d in range(N_DEV))
assert all(WIN_ROW0[d + 1] <= WIN_ROW0[d] + WIN_W for d in range(N_DEV - 1))
assert WIN_ROW0[-1] + WIN_W == O_BA + N_BA

ADAM_LR = 0.001
ADAM_B1 = 0.9
ADAM_B2 = 0.999
ADAM_EPS = 1e-08
ADAM_WD = 0.01
ADAM_STEP = 10

NN = (((1,), (0,)), ((), ()))
NT = (((1,), (1,)), ((), ()))
TN = (((0,), (0,)), ((), ()))


def _dot(a, b, dims):
    return lax.dot_general(a.astype(BF16), b.astype(BF16), dims, preferred_element_type=F32)


def _split2(a):
    hi = a.astype(BF16)
    lo = (a - hi.astype(F32)).astype(BF16)
    return hi, lo


def _dot3(a, b, dims):
    ah, al = _split2(a)
    bh, bl = _split2(b)
    d = functools.partial(lax.dot_general, dimension_numbers=dims, preferred_element_type=F32)
    return d(ah, bh) + (d(ah, bl) + d(al, bh))


def _sel_dot_impl(sel01, x, dims):
    sel = sel01.astype(BF16)
    h1 = x.astype(BF16)
    r1 = x - h1.astype(F32)
    h2 = r1.astype(BF16)
    h3 = (r1 - h2.astype(F32)).astype(BF16)
    d = functools.partial(lax.dot_general, dimension_numbers=dims, preferred_element_type=F32)
    return d(sel, h1) + (d(sel, h2) + d(sel, h3))


@jax.custom_vjp
def _sel_dot(sel01, x):
    return _sel_dot_impl(sel01, x, NN)


_sel_dot.defvjp(lambda s, x: (_sel_dot(s, x), s),
                lambda s, g: (jnp.zeros_like(s), _sel_dot_impl(s, g, TN)))


def _make_mm(dotfn):
    @jax.custom_vjp
    def nn(a, b):
        return dotfn(a, b, NN)

    @jax.custom_vjp
    def nt(a, b):
        return dotfn(a, b, NT)

    @jax.custom_vjp
    def tn(a, b):
        return dotfn(a, b, TN)

    nn.defvjp(lambda a, b: (nn(a, b), (a, b)), lambda r, g: (nt(g, r[1]), tn(r[0], g)))
    nt.defvjp(lambda a, b: (nt(a, b), (a, b)), lambda r, g: (nn(g, r[1]), tn(g, r[0])))
    tn.defvjp(lambda a, b: (tn(a, b), (a, b)), lambda r, g: (nt(r[1], g), nn(r[0], g)))
    return nn, nt, tn


mm_nn, mm_nt, mm_tn = _make_mm(_dot)
mm3_nn, mm3_nt, mm3_tn = _make_mm(_dot3)


def _sigmoid(x):
    return jax.nn.sigmoid(x)


def _silu(x):
    return x * _sigmoid(x)


def _softplus_parts(x):
    sp = jnp.log1p(jnp.exp(-jnp.abs(x)))
    return jnp.maximum(x, 0.0) + sp, jnp.maximum(-x, 0.0) + sp


def _rmsnorm(x, g):
    return x * lax.rsqrt(jnp.mean(x * x, axis=-1, keepdims=True) + NORM_EPS) * g


def _iota2(shape, dim):
    return lax.broadcasted_iota(jnp.int32, shape, dim)


def _div64(i):
    return lax.shift_right_logical(i, jnp.full(i.shape, 6, jnp.int32))


def _each(f, *lists):
    return [f(*a) for a in zip(*lists)]


@jax.custom_vjp
def _inv_unit_lower(ms):
    n = ms[0].shape[0]
    eye = (_iota2((n, n), 0) == _iota2((n, n), 1)).astype(F32)
    rs = [eye - m for m in ms]
    ps = ms
    for _ in range(5):
        ps = _each(mm3_nn, ps, ps)
        rs = _each(lambda r, p: r + mm_nn(r, p), rs, ps)
    return rs


def _inv_fwd(ms):
    rs = _inv_unit_lower(ms)
    return rs, rs


def _inv_bwd(rs, gs):
    ts = _each(mm_tn, rs, gs)
    return (_each(lambda t, r: -mm_nt(t, r), ts, rs),)


_inv_unit_lower.defvjp(_inv_fwd, _inv_bwd)


def _dn_block(cq, ck, cv, bcol, acol, zt, alog, dtb, gn, s0):
    n = SUPER
    h = DN_CHUNK
    row = _iota2((n, n), 0)
    col = _iota2((n, n), 1)
    same = _div64(row) == _div64(col)
    incl = jnp.logical_and(same, row >= col)
    strict = jnp.logical_and(same, row > col)
    incl_f = incl.astype(F32)

    qn = _each(lambda x: x * lax.rsqrt(jnp.sum(x * x, axis=-1, keepdims=True) + NORM_EPS) * (D_HEAD ** -0.5), cq)
    kn = _each(lambda x: x * lax.rsqrt(jnp.sum(x * x, axis=-1, keepdims=True) + NORM_EPS), ck)
    beta = _each(_sigmoid, bcol)
    g = _each(lambda al, ac, dt: -(jnp.exp(al) * _softplus_parts(ac + dt)[0]), alog, acol, dtb)
    gcum = _each(lambda x: _sel_dot(incl_f, jnp.broadcast_to(x, (n, n))), g)
    gam_incl = _each(lambda x: jnp.where(incl, jnp.exp(jnp.where(incl, x - x.T, 0.0)), 0.0), gcum)
    kk = _each(mm_nt, kn, kn)
    t_inv = _inv_unit_lower(_each(lambda b, x, gm: b * x * jnp.where(strict, gm, 0.0), beta, kk, gam_incl))
    eg = _each(jnp.exp, gcum)
    u = _each(lambda t, v, b: mm_nn(t, v * b), t_inv, cv, beta)
    w = _each(lambda t, k, b, e: mm_nn(t, k * (b * e)), t_inv, kn, beta, eg)
    a_intra = _each(lambda q, k, gm: mm_nt(q, k) * gm, qn, kn, gam_incl)
    q_dec = _each(lambda q, e: q * e, qn, eg)
    last0 = _each(lambda x: x[h - 1:h, :], gcum)
    last1 = _each(lambda x: x[n - 1:n, :], gcum)
    k_dec = _each(lambda k, x, l0, l1: k * jnp.exp(jnp.concatenate(
        [jnp.broadcast_to(l0, (h, n)), jnp.broadcast_to(l1, (h, n))], axis=0) - x), kn, gcum, last0, last1)
    v0 = _each(lambda uu, ww, s: uu[:h] - mm_nn(ww[:h], s), u, w, s0)
    o0 = _each(lambda q, s: mm_nn(q[:h], s), q_dec, s0)
    s1 = _each(lambda s, l0, k, v: s * jnp.exp(l0) + mm_tn(k[:h], v), s0, last0, k_dec, v0)
    v1 = _each(lambda uu, ww, s: uu[h:] - mm_nn(ww[h:], s), u, w, s1)
    o1 = _each(lambda q, s: mm_nn(q[h:], s), q_dec, s1)
    s2 = _each(lambda s, l1, k, v: s * jnp.exp(l1) + mm_tn(k[h:], v), s1, last1, k_dec, v1)
    o = _each(lambda a, b, am, x, y: jnp.concatenate([a, b], axis=0) + mm_nn(am, jnp.concatenate([x, y], axis=0)),
              o0, o1, a_intra, v0, v1)
    out = _each(lambda x, z: _rmsnorm(x, gn) * _silu(z), o, zt)
    return out, s2


def _mem_fn(mq, mz, mkv):
    mk = mkv[:, :MEM_W]
    mv = mkv[:, MEM_W:]
    lane = _iota2((1, MEM_W), 1)
    out = jnp.zeros(mq.shape, F32)
    for hd in range(MEM_HEADS):
        hm = (_div64(lane) == hd).astype(F32)
        s = mm_nt(mq * hm, mk) * (1.0 / math.sqrt(MEM_DH))
        s = s - jnp.max(s, axis=-1, keepdims=True)
        e = jnp.exp(s)
        p = e / jnp.sum(e, axis=-1, keepdims=True)
        out = out + mm_nn(p, mv) * hm
    return out * _silu(mz)


def _loss_fn(x, mo, fg, tgt):
    y = _rmsnorm(x + mo, fg)
    err = y - tgt
    return 0.5 * jnp.sum(jnp.mean(err * err, axis=-1, keepdims=True), axis=0, keepdims=True)


def _matmul_tn(a, b, out_dtype, tm, tn, tk, name):
    kdim, m = a.shape
    n = b.shape[1]
    tm, tn, tk = min(tm, m), min(tn, n), min(tk, kdim)
    assert m % tm == 0 and n % tn == 0 and kdim % tk == 0
    nk = kdim // tk

    def body(a_ref, b_ref, o_ref, acc_ref):
        k = pl.program_id(2)
        part = _dot(a_ref[...], b_ref[...], TN)

        @pl.when(k == 0)
        def _():
            acc_ref[...] = part

        @pl.when(k > 0)
        def _():
            acc_ref[...] += part

        @pl.when(k == nk - 1)
        def _():
            o_ref[...] = acc_ref[...].astype(o_ref.dtype)

    return pl.pallas_call(
        body,
        name=name,
        grid=(m // tm, n // tn, nk),
        in_specs=[pl.BlockSpec((tk, tm), lambda i, j, k: (k, i)), pl.BlockSpec((tk, tn), lambda i, j, k: (k, j))],
        out_specs=pl.BlockSpec((tm, tn), lambda i, j, k: (i, j)),
        out_shape=jax.ShapeDtypeStruct((m, n), out_dtype),
        scratch_shapes=[pltpu.VMEM((tm, tn), F32)],
        compiler_params=pltpu.CompilerParams(dimension_semantics=("parallel", "parallel", "arbitrary")),
    )(a, b)


def _norm_in(x, g, tm=256):
    t = x.shape[0]

    def body(x_ref, g_ref, h_ref):
        h_ref[...] = _rmsnorm(x_ref[...], g_ref[...]).astype(BF16)

    return pl.pallas_call(
        body,
        name="norm_in",
        grid=(t // tm,),
        in_specs=[pl.BlockSpec((tm, D_MODEL), lambda i: (i, 0)), pl.BlockSpec((1, D_MODEL), lambda i: (0, 0))],
        out_specs=pl.BlockSpec((tm, D_MODEL), lambda i: (i, 0)),
        out_shape=jax.ShapeDtypeStruct((t, D_MODEL), BF16),
    )(x, g)


def _dw_alt(parts, h, tm=512):
    t = h.shape[0]
    n_p = len(parts)
    widths = [p.shape[1] for p in parts]
    offs = [sum(widths[:s]) for s in range(n_p)]
    total = sum(widths)
    n_tiles = pl.cdiv(total, tm)

    specs = []
    for off, w in zip(offs, widths):
        if w >= tm:
            assert w % tm == 0 and off % tm == 0
            specs.append(pl.BlockSpec(
                (t, tm), lambda i, lo=off // tm, n=w // tm: (0, jnp.minimum(jnp.maximum(i - lo, 0), n - 1))))
        else:
            assert off // tm == (off + w - 1) // tm
            specs.append(pl.BlockSpec((t, w), lambda i: (0, 0), pipeline_mode=pl.Buffered(1)))

    def body(*refs):
        a_refs, h_ref, o_ref = refs[:n_p], refs[n_p], refs[n_p + 1]
        i = pl.program_id(0)
        for a_ref, off, w in zip(a_refs, offs, widths):
            if w >= tm:
                @pl.when(jnp.logical_and(i >= off // tm, i < (off + w) // tm))
                def _(a_ref=a_ref):
                    o_ref[...] = _dot(a_ref[...], h_ref[...], TN).astype(o_ref.dtype)
            else:
                @pl.when(i == off // tm)
                def _(a_ref=a_ref, r0=off % tm, w=w):
                    o_ref[r0:r0 + w, :] = _dot(a_ref[...], h_ref[...], TN).astype(o_ref.dtype)
        if total % tm:
            @pl.when(i == n_tiles - 1)
            def _():
                o_ref[total % tm:, :] = jnp.zeros((tm - total % tm, D_MODEL), o_ref.dtype)

    return pl.pallas_call(
        body,
        name="dw_alt",
        grid=(n_tiles,),
        in_specs=specs + [pl.BlockSpec((t, D_MODEL), lambda i: (0, 0), pipeline_mode=pl.Buffered(1))],
        out_specs=pl.BlockSpec((tm, D_MODEL), lambda i: (i, 0)),
        out_shape=jax.ShapeDtypeStruct((n_tiles * tm, D_MODEL), BF16),
    )(*parts, h)


def _grad_x(parts, w_alt, x, g, dres, tm=256):
    t = x.shape[0]
    tm = min(tm, t)
    n_p = len(parts)
    assert sum(p.shape[1] for p in parts) == w_alt.shape[0] and t % tm == 0

    def body(*refs):
        a_refs = refs[:n_p]
        w_ref, x_ref, g_ref, dres_ref, dx_ref, dg_ref = refs[n_p:]
        dproj = jnp.concatenate([a_ref[...] for a_ref in a_refs], axis=1)
        dh = _dot(dproj, w_ref[...], NN)

        @pl.when(pl.program_id(0) == 0)
        def _():
            dg_ref[...] = jnp.zeros_like(dg_ref)

        _, vjp = jax.vjp(_rmsnorm, x_ref[...], g_ref[...])
        dx, dg = vjp(dh)
        dx_ref[...] = dx + dres_ref[...]
        dg_ref[...] += dg

    row = pl.BlockSpec((tm, D_MODEL), lambda i: (i, 0))
    vec = pl.BlockSpec((1, D_MODEL), lambda i: (0, 0))
    return pl.pallas_call(
        body,
        name="grad_x",
        grid=(t // tm,),
        in_specs=[pl.BlockSpec((tm, p.shape[1]), lambda i: (i, 0)) for p in parts]
        + [pl.BlockSpec(w_alt.shape, lambda i: (0, 0), pipeline_mode=pl.Buffered(1)), row, vec, row],
        out_specs=[row, vec],
        out_shape=[jax.ShapeDtypeStruct((t, D_MODEL), F32), jax.ShapeDtypeStruct((1, D_MODEL), F32)],
    )(*parts, w_alt, x, g, dres)


def _block_tail(proj, o_dn, o_sb, o_m, x, tgt, w_br_dn, w_br_sb, w_br_mem, w_out, fg, tm=256):
    t = x.shape[0]
    tm = min(tm, t)
    gw = 512
    n_g = 3 * D_MODEL // gw

    def body(*refs):
        g_refs = refs[:n_g]
        (odn_ref, osb_ref, om_ref, x_ref, t_ref, wdn_ref, wsb_ref, wm_ref, wo_ref, fg_ref, loss_ref, dout_ref, dfg_ref,
         mg_ref, dyd_ref, dys_ref, dym_ref, dg_ref, dod_ref, dos_ref, dom_ref) = refs[n_g:]
        y = [_dot(odn_ref[...], wdn_ref[...], NN), _dot(osb_ref[...], wsb_ref[...], NN),
             _dot(om_ref[...], wm_ref[...], NN)]
        s = [_sigmoid(jnp.concatenate([g_refs[2 * k][...], g_refs[2 * k + 1][...]], axis=1)) for k in range(3)]
        merged16 = (s[0] * y[0] + s[1] * y[1] + s[2] * y[2]).astype(BF16)
        mg_ref[...] = merged16
        mo = _dot(merged16, wo_ref[...], NN)
        loss, vjp = jax.vjp(_loss_fn, x_ref[...], mo, fg_ref[...], t_ref[...])
        _, dout, dfg, _ = vjp(jnp.ones((1, 1), F32))

        @pl.when(pl.program_id(0) == 0)
        def _():
            loss_ref[...] = jnp.zeros_like(loss_ref)
            dfg_ref[...] = jnp.zeros_like(dfg_ref)

        loss_ref[...] += jnp.broadcast_to(loss, loss_ref.shape)
        dfg_ref[...] += dfg
        dout_ref[...] = dout
        dmerged = _dot(dout, wo_ref[...], NT)
        dy = [(sk * dmerged).astype(BF16) for sk in s]
        dyd_ref[...], dys_ref[...], dym_ref[...] = dy
        dg_ref[...] = jnp.concatenate([dmerged * yk * (sk * (1.0 - sk)) for yk, sk in zip(y, s)], axis=1).astype(BF16)
        dod_ref[...] = _dot(dy[0], wdn_ref[...], NT).astype(BF16)
        dos_ref[...] = _dot(dy[1], wsb_ref[...], NT).astype(BF16)
        dom_ref[...] = _dot(dy[2], wm_ref[...], NT).astype(BF16)

    gates = [pl.BlockSpec((tm, gw), lambda i, j=j: (i, O_GATES // gw + j)) for j in range(n_g)]
    row = pl.BlockSpec((tm, D_MODEL), lambda i: (i, 0))
    rowm = pl.BlockSpec((tm, MEM_W), lambda i: (i, 0))
    vec = pl.BlockSpec((1, D_MODEL), lambda i: (0, 0))

    def whole(a):
        return pl.BlockSpec(a.shape, lambda i: (0, 0), pipeline_mode=pl.Buffered(1))

    def bf(c):
        return jax.ShapeDtypeStruct((t, c), BF16)

    return pl.pallas_call(
        body,
        name="block_tail",
        grid=(t // tm,),
        in_specs=gates + [row, row, rowm, row, row, whole(w_br_dn), whole(w_br_sb), whole(w_br_mem), whole(w_out), vec],
        out_specs=[pl.BlockSpec((1, LANE), lambda i: (0, 0)), row, vec, row, row, row, row,
                   pl.BlockSpec((tm, 3 * D_MODEL), lambda i: (i, 0)), row, row, rowm],
        out_shape=[jax.ShapeDtypeStruct((1, LANE), F32), jax.ShapeDtypeStruct((t, D_MODEL), F32),
                   jax.ShapeDtypeStruct((1, D_MODEL), F32), bf(D_MODEL), bf(D_MODEL), bf(D_MODEL), bf(D_MODEL),
                   bf(3 * D_MODEL), bf(D_MODEL), bf(D_MODEL), bf(MEM_W)],
    )(*([proj] * n_g), o_dn, o_sb, o_m, x, tgt, w_br_dn, w_br_sb, w_br_mem, w_out, fg)


def _shift_rows(x, s):
    t = x.shape[0]
    if s == 0:
        return x
    rolled = pltpu.roll(x, s % t, 0)
    row = _iota2(x.shape, 0)
    keep = row >= s if s > 0 else row < t + s
    return jnp.where(keep, rolled, 0.0)


def _conv_pre(x, w):
    return sum(_shift_rows(x, CONV_K - 1 - j) * w[j:j + 1, :] for j in range(CONV_K))


CONV_TC = 256


def _dn_conv(proj, conv_w):
    t = proj.shape[0]
    nb = 3 * D_MODEL // CONV_TC

    def body(x_ref, w_ref, c_ref):
        c_ref[...] = _silu(_conv_pre(x_ref[...], w_ref[...]))

    return pl.pallas_call(
        body,
        name="dn_conv",
        grid=(nb,),
        in_specs=[pl.BlockSpec((t, CONV_TC), lambda j: (0, j)), pl.BlockSpec((CONV_K, CONV_TC), lambda j: (0, j))],
        out_specs=pl.BlockSpec((t, CONV_TC), lambda j: (0, j)),
        out_shape=jax.ShapeDtypeStruct((t, 3 * D_MODEL), F32),
    )(proj, conv_w)


def _dn_conv_bwd(proj, conv_w, dc):
    t = proj.shape[0]
    nb = 3 * D_MODEL // CONV_TC

    def body(x_ref, w_ref, dc_ref, dx_ref, dw_ref):
        x = x_ref[...]
        w = w_ref[...]
        pre = _conv_pre(x, w)
        sg = _sigmoid(pre)
        dpre = dc_ref[...] * (sg * (1.0 + pre * (1.0 - sg)))
        ahead = [_shift_rows(dpre, -(CONV_K - 1 - j)) for j in range(CONV_K)]
        dx_ref[...] = sum(a * w[j:j + 1, :] for j, a in enumerate(ahead)).astype(BF16)
        dw_ref[...] = jnp.concatenate([jnp.sum(a * x, axis=0, keepdims=True) for a in ahead], axis=0)

    blk = pl.BlockSpec((t, CONV_TC), lambda j: (0, j))
    wblk = pl.BlockSpec((CONV_K, CONV_TC), lambda j: (0, j))
    return pl.pallas_call(
        body,
        name="dn_conv_bwd",
        grid=(nb,),
        in_specs=[blk, wblk, blk],
        out_specs=[blk, wblk],
        out_shape=[jax.ShapeDtypeStruct((t, 3 * D_MODEL), BF16), jax.ShapeDtypeStruct((CONV_K, 3 * D_MODEL), F32)],
    )(proj, conv_w, dc)


def _ba_columns(ba, hd):
    lane = _iota2(ba.shape, 1)
    bcol = jnp.sum(jnp.where(lane == hd, ba, 0.0), axis=1, keepdims=True)
    acol = jnp.sum(jnp.where(lane == N_HEADS + hd, ba, 0.0), axis=1, keepdims=True)
    return bcol, acol


def _head_scalar(row, hd):
    lane = _iota2(row.shape, 1)
    return jnp.sum(jnp.where(lane == hd, row, 0.0), axis=1, keepdims=True)


DN_HP = 8


def _dn_inputs(cq, ck, cv, ba_ref, z_ref, alog_ref, dtb_ref, heads, lanes):
    ba = ba_ref[...]
    cols = [_ba_columns(ba, hd) for hd in heads]
    return ([cq[:, ln] for ln in lanes], [ck[:, ln] for ln in lanes], [cv[:, ln] for ln in lanes],
            [c[0] for c in cols], [c[1] for c in cols], [z_ref[:, ln] for ln in lanes],
            [_head_scalar(alog_ref[...], hd) for hd in heads], [_head_scalar(dtb_ref[...], hd) for hd in heads])


def _dn_specs(nblk, reverse):
    w = DN_HP * LANE
    nq = D_MODEL // w

    def row(i):
        return nblk - 1 - i if reverse else i

    def colblk(b0):
        return pl.BlockSpec((SUPER, w), lambda i, h: (row(i), b0 + h))

    ba = pl.BlockSpec((SUPER, LANE), lambda i, h: (row(i), O_BA // LANE))
    vec = pl.BlockSpec((1, LANE), lambda i, h: (0, 0))
    st = pl.BlockSpec((1, DN_HP, D_HEAD, D_HEAD), lambda i, h: (row(i), h, 0, 0))
    return colblk, nq, ba, vec, st


def _dn_fwd(c, proj, alog_row, dtb_row, gn):
    t = c.shape[0]
    nblk = t // SUPER
    colblk, nq, ba, vec, st = _dn_specs(nblk, False)

    def body(cq, ck, cv, ba_ref, z_ref, alog_ref, dtb_ref, gn_ref, o_ref, s_ref, state):
        @pl.when(jnp.logical_and(pl.program_id(0) == 0, pl.program_id(1) == 0))
        def _():
            state[...] = jnp.zeros_like(state)

        heads = [pl.program_id(1) * DN_HP + j for j in range(DN_HP)]
        lanes = [slice(j * LANE, (j + 1) * LANE) for j in range(DN_HP)]
        s0 = [state[hd] for hd in heads]
        outs, s2 = _dn_block(*_dn_inputs(cq, ck, cv, ba_ref, z_ref, alog_ref, dtb_ref, heads, lanes), gn_ref[...], s0)
        for j, (hd, ln) in enumerate(zip(heads, lanes)):
            s_ref[0, j] = s0[j]
            o_ref[:, ln] = outs[j].astype(BF16)
            state[hd] = s2[j]

    return pl.pallas_call(
        body,
        name="dn_fwd",
        grid=(nblk, N_HEADS // DN_HP),
        in_specs=[colblk(0), colblk(nq), colblk(2 * nq), ba, colblk(O_Z_DN // (DN_HP * LANE)), vec, vec, vec],
        out_specs=[colblk(0), st],
        out_shape=[jax.ShapeDtypeStruct((t, D_MODEL), BF16),
                   jax.ShapeDtypeStruct((nblk, N_HEADS, D_HEAD, D_HEAD), F32)],
        scratch_shapes=[pltpu.VMEM((N_HEADS, D_HEAD, D_HEAD), F32)],
    )(c, c, c, proj, proj, alog_row, dtb_row, gn)


def _dn_bwd(c, proj, alog_row, dtb_row, gn, states, do):
    t = c.shape[0]
    nblk = t // SUPER
    colblk, nq, ba, vec, st = _dn_specs(nblk, True)
    assert nq == 1

    def body(cq, ck, cv, ba_ref, z_ref, alog_ref, dtb_ref, gn_ref, s_ref, do_ref,
             dc_ref, dz_ref, dba_ref, dsc_ref, dgn_ref, dstate):
        i = pl.program_id(0)
        hq = pl.program_id(1)

        @pl.when(jnp.logical_and(i == 0, hq == 0))
        def _():
            dstate[...] = jnp.zeros_like(dstate)
            dsc_ref[...] = jnp.zeros_like(dsc_ref)
            dgn_ref[...] = jnp.zeros_like(dgn_ref)

        @pl.when(hq == 0)
        def _():
            dba_ref[...] = jnp.zeros_like(dba_ref)

        lane = _iota2((SUPER, LANE), 1)
        lane1 = _iota2((1, LANE), 1)
        heads = [hq * DN_HP + j for j in range(DN_HP)]
        lanes = [slice(j * LANE, (j + 1) * LANE) for j in range(DN_HP)]
        ds_in = [dstate[hd] for hd in heads]
        s_in = [s_ref[0, j] for j in range(DN_HP)]
        _, vjp = jax.vjp(_dn_block, *_dn_inputs(cq, ck, cv, ba_ref, z_ref, alog_ref, dtb_ref, heads, lanes),
                         gn_ref[...], s_in)
        dq, dk, dv, dbc, dac, dz, dal, ddt, dgn, ds0 = vjp(([do_ref[:, ln].astype(F32) for ln in lanes], ds_in))
        dba = jnp.zeros((SUPER, LANE), F32)
        dal_row = jnp.zeros((1, LANE), F32)
        ddt_row = jnp.zeros((1, LANE), F32)
        for j, (hd, ln) in enumerate(zip(heads, lanes)):
            for part, d in enumerate((dq, dk, dv)):
                dc_ref[:, part * D_MODEL + j * LANE:part * D_MODEL + (j + 1) * LANE] = d[j]
            dz_ref[:, ln] = dz[j].astype(BF16)
            dstate[hd] = ds0[j]
            dba = dba + jnp.where(lane == hd, dbc[j], 0.0) + jnp.where(lane == N_HEADS + hd, dac[j], 0.0)
            dal_row = dal_row + jnp.where(lane1 == hd, dal[j], 0.0)
            ddt_row = ddt_row + jnp.where(lane1 == hd, ddt[j], 0.0)
        dba_ref[...] += dba
        dsc_ref[0:1, :] += dal_row
        dsc_ref[1:2, :] += ddt_row
        dgn_ref[...] += dgn

    outs = pl.pallas_call(
        body,
        name="dn_bwd",
        grid=(nblk, N_HEADS // DN_HP),
        in_specs=[colblk(0), colblk(nq), colblk(2 * nq), ba, colblk(O_Z_DN // (DN_HP * LANE)), vec, vec, vec, st,
                  colblk(0)],
        out_specs=[pl.BlockSpec((SUPER, 3 * D_MODEL), lambda i, h: (nblk - 1 - i, 0)), colblk(0),
                   pl.BlockSpec((SUPER, LANE), lambda i, h: (nblk - 1 - i, 0)),
                   pl.BlockSpec((2, LANE), lambda i, h: (0, 0)), vec],
        out_shape=[jax.ShapeDtypeStruct((t, 3 * D_MODEL), F32), jax.ShapeDtypeStruct((t, D_MODEL), BF16),
                   jax.ShapeDtypeStruct((t, LANE), F32), jax.ShapeDtypeStruct((2, LANE), F32),
                   jax.ShapeDtypeStruct((1, LANE), F32)],
        scratch_shapes=[pltpu.VMEM((N_HEADS, D_HEAD, D_HEAD), F32)],
    )(c, c, c, proj, proj, alog_row, dtb_row, gn, states, do)
    return outs


SB_TQ = 256
SB_TK = 256
SB_HP_FWD = 8
SB_HP_BWD = 4


def _sb_logits(z, mask):
    sp = jnp.log(1.0 + jnp.exp(-jnp.abs(z)))
    lf_raw = -(jnp.maximum(z, 0.0) + sp)
    lb = lf_raw + z
    lf = lf_raw if mask is None else jnp.where(mask, lf_raw, 0.0)
    return lb, lf_raw, lf


def _suffix_sums(x, sel):
    hi, lo = _split2(x)
    d = functools.partial(lax.dot_general, dimension_numbers=NN, preferred_element_type=F32)
    return d(hi, sel) + d(lo, sel)


def _sb_diag_mask(tq, r):
    return r * SB_TK + _iota2((tq, SB_TK), 1) < _iota2((tq, SB_TK), 0)


def _sb_specs(t, tq, hp):
    w = hp * LANE
    q0, k0, v0, z0 = (O_QKV_SB // w, (O_QKV_SB + D_MODEL) // w, (O_QKV_SB + 2 * D_MODEL) // w, O_Z_SB // w)

    def blk(b0):
        return pl.BlockSpec((tq, w), lambda h, i: (i, b0 + h))

    def full(b0, **kw):
        return pl.BlockSpec((t, w), lambda h, i: (0, b0 + h), **kw)

    once = dict(pipeline_mode=pl.Buffered(1))
    return blk(q0), full(k0, **once), full(v0, **once), blk(z0), blk(0), full(0)


def _sb_fwd(proj, shards):
    t = proj.shape[0]
    tq = min(SB_TQ, t)
    ndiag = tq // SB_TK
    scale = 1.0 / math.sqrt(D_HEAD)
    na = len(shards)

    def body(q_ref, k_ref, v_ref, z_ref, *rest):
        x_refs, (o_ref, oraw_ref), land = rest[:na], rest[na:na + 2], rest[na + 2:2 * na + 2]
        sems = rest[2 * na + 2:]
        qi = pl.program_id(1)
        first = jnp.logical_and(pl.program_id(0) == 0, qi == 0)
        last = jnp.logical_and(pl.program_id(0) == pl.num_programs(0) - 1, qi == pl.num_programs(1) - 1)

        @pl.when(first)
        def _():
            for cp in _direct_gather_copies(x_refs, land, *sems):
                cp.start()

        lanes = [slice(hd * LANE, (hd + 1) * LANE) for hd in range(SB_HP_FWD)]
        qs = [(q_ref[:, ln] * scale).astype(BF16) for ln in lanes]
        after = (_iota2((SB_TK, SB_TK), 0) > _iota2((SB_TK, SB_TK), 1)).astype(BF16)
        oraw_ref[...] = jnp.zeros_like(oraw_ref)

        def block(kb, mask, c_lf):
            rows = pl.ds(pl.multiple_of(kb * SB_TK, SB_TK), SB_TK)
            z = _each(lambda q, ln: _dot(q, k_ref[rows, ln], NT), qs, lanes)
            lg = _each(lambda x: _sb_logits(x, mask), z)
            surv = _each(lambda x: _suffix_sums(x[2], after), lg)
            att = _each(lambda x, s, c: jnp.exp(x[0] + s + c), lg, surv, c_lf)
            if mask is not None:
                att = _each(lambda a: jnp.where(mask, a, 0.0), att)
            pv = _each(lambda a, ln: _dot(a, v_ref[rows, ln], NN), att, lanes)
            for p, ln in zip(pv, lanes):
                oraw_ref[:, ln] += p
            return tuple(_each(lambda c, x: c + jnp.sum(x[2], axis=1, keepdims=True), c_lf, lg))

        carry = tuple(jnp.zeros((tq, 1), F32) for _ in range(SB_HP_FWD))
        for r in reversed(range(ndiag)):
            carry = block(qi * ndiag + r, _sb_diag_mask(tq, r), carry)
        lax.fori_loop(0, qi * ndiag, lambda i, c: block(qi * ndiag - 1 - i, None, c), carry)
        o_ref[...] = (oraw_ref[...] * _silu(z_ref[...])).astype(BF16)

        @pl.when(last)
        def _():
            for cp in _direct_gather_copies(x_refs, land, *sems):
                cp.wait()

    q_spec, k_spec, v_spec, z_spec, out, _ = _sb_specs(t, tq, SB_HP_FWD)
    outs = pl.pallas_call(
        body,
        name="sb_fwd",
        grid=(N_HEADS // SB_HP_FWD, t // tq),
        in_specs=[q_spec, k_spec, v_spec, z_spec] + [ANY] * na,
        out_specs=[out, out] + [ANY] * na,
        out_shape=[jax.ShapeDtypeStruct((t, D_MODEL), BF16), jax.ShapeDtypeStruct((t, D_MODEL), F32)]
        + [jax.ShapeDtypeStruct((N_DEV, *v.shape), v.dtype) for v in shards],
        scratch_shapes=_gather_sems(na),
    )(proj, proj, proj, proj, *shards)
    return outs[0], outs[1], outs[2:]


def _sb_bwd(proj, oraw, do, blocks):
    t = proj.shape[0]
    tq = min(SB_TQ, t)
    ndiag = tq // SB_TK
    scale = 1.0 / math.sqrt(D_HEAD)
    nb = len(blocks)

    def body(q_ref, k_ref, v_ref, z_ref, oraw_ref, do_ref, *rest):
        blk_refs, (dq_ref, dk_ref, dv_ref, dz_ref), land_refs = rest[:nb], rest[nb:nb + 4], rest[nb + 4:2 * nb + 4]
        dk_acc, dv_acc, p_scr, z_scr, send_sems, recv_sems, local_sems = rest[2 * nb + 4:]
        qi = pl.program_id(1)
        nq = pl.num_programs(1)
        hg = pl.program_id(0)
        me = _position()
        mine = 4 * me[0] + 2 * me[1] + me[2]

        def exchange():
            cps = []
            for a, (blk_ref, land_ref) in enumerate(zip(blk_refs, land_refs)):
                cps.append(pltpu.make_async_copy(blk_ref.at[mine], land_ref.at[mine], local_sems.at[a]))
                for k, peer in enumerate(_other_devices(me)):
                    cps.append(pltpu.make_async_remote_copy(
                        src_ref=blk_ref.at[4 * peer[0] + 2 * peer[1] + peer[2]], dst_ref=land_ref.at[mine],
                        send_sem=send_sems.at[7 * a + k], recv_sem=recv_sems.at[7 * a + k], device_id=peer,
                        device_id_type=MESH))
            return cps

        @pl.when(jnp.logical_and(hg == 0, qi == 0))
        def _():
            for cp in exchange():
                cp.start()

        @pl.when(qi == 0)
        def _():
            dk_acc[...] = jnp.zeros_like(dk_acc)
            dv_acc[...] = jnp.zeros_like(dv_acc)

        heads = range(SB_HP_BWD)
        lanes = [slice(hd * LANE, (hd + 1) * LANE) for hd in heads]
        zg = z_ref[...]
        sg = _sigmoid(zg)
        dog = do_ref[...].astype(F32)
        dz_ref[...] = (dog * oraw_ref[...] * (sg * (1.0 + zg * (1.0 - sg)))).astype(BF16)
        d_o = (dog * (zg * sg)).astype(BF16)
        d_o16 = [d_o[:, ln] for ln in lanes]
        qs = [(q_ref[:, ln] * scale).astype(BF16) for ln in lanes]
        ri = _iota2((SB_TK, SB_TK), 0)
        ci = _iota2((SB_TK, SB_TK), 1)
        after = (ri > ci).astype(BF16)
        earlier = (ri < ci).astype(BF16)

        def rows_of(kb):
            return pl.ds(pl.multiple_of(kb * SB_TK, SB_TK), SB_TK)

        def down(kb, mask, c_lf):
            rows = rows_of(kb)
            z = _each(lambda q, ln: _dot(q, k_ref[rows, ln], NT), qs, lanes)
            da = _each(lambda d, ln: _dot(d, v_ref[rows, ln], NT), d_o16, lanes)
            lg = _each(lambda x: _sb_logits(x, mask), z)
            surv = _each(lambda x: _suffix_sums(x[2], after), lg)
            att = _each(lambda x, s, c: jnp.exp(x[0] + s + c), lg, surv, c_lf)
            if mask is not None:
                att = _each(lambda a: jnp.where(mask, a, 0.0), att)
            dv = _each(lambda a, d: _dot(a, d, TN), att, d_o16)
            for hd in heads:
                p_scr[hd, kb] = att[hd] * da[hd]
                z_scr[hd, kb] = z[hd]
                dv_acc[rows, lanes[hd]] += dv[hd]
            return tuple(_each(lambda c, x: c + jnp.sum(x[2], axis=1, keepdims=True), c_lf, lg))

        c_lf = tuple(jnp.zeros((tq, 1), F32) for _ in heads)
        for r in reversed(range(ndiag)):
            c_lf = down(qi * ndiag + r, _sb_diag_mask(tq, r), c_lf)
        lax.fori_loop(0, qi * ndiag, lambda i, c: down(qi * ndiag - 1 - i, None, c), c_lf)

        def up(kb, mask, carry):
            dq, c_p = carry
            rows = rows_of(kb)
            p = [p_scr[hd, kb] for hd in heads]
            zs = [z_scr[hd, kb] for hd in heads]
            before = _each(lambda x, c: _suffix_sums(x, earlier) + c, p, c_p)
            e = _each(lambda x: jnp.exp(-jnp.abs(x)), zs)
            r = _each(lambda x: 1.0 / (1.0 + x), e)
            sig = _each(lambda x, a, b: jnp.where(x >= 0.0, b, a * b), zs, e, r)
            oms = _each(lambda x, a, b: jnp.where(x >= 0.0, a * b, b), zs, e, r)
            if mask is not None:
                sig = _each(lambda a: jnp.where(mask, a, 0.0), sig)
            dzz = _each(lambda x, o, g, b: x * o - g * b, p, oms, sig, before)
            dk = _each(lambda x, q: _dot(x, q, TN), dzz, qs)
            dq = _each(lambda a, x, ln: a + _dot(x, k_ref[rows, ln], NN), dq, dzz, lanes)
            for hd in heads:
                dk_acc[rows, lanes[hd]] += dk[hd]
            return tuple(dq), tuple(_each(lambda c, x: c + jnp.sum(x, axis=1, keepdims=True), c_p, p))

        carry = (tuple(jnp.zeros((tq, D_HEAD), F32) for _ in heads), tuple(jnp.zeros((tq, 1), F32) for _ in heads))
        carry = lax.fori_loop(0, qi * ndiag, lambda kb, c: up(kb, None, c), carry)
        for r in range(ndiag):
            carry = up(qi * ndiag + r, _sb_diag_mask(tq, r), carry)
        dq = carry[0]
        for hd in heads:
            dq_ref[:, lanes[hd]] = (dq[hd] * scale).astype(BF16)

        @pl.when(qi == nq - 1)
        def _():
            dk_ref[...] = dk_acc[...].astype(BF16)
            dv_ref[...] = dv_acc[...].astype(BF16)

        @pl.when(jnp.logical_and(hg == pl.num_programs(0) - 1, qi == nq - 1))
        def _():
            for cp in exchange():
                cp.wait()

    q_spec, k_spec, v_spec, z_spec, blk, full = _sb_specs(t, tq, SB_HP_BWD)
    o = jax.ShapeDtypeStruct((t, D_MODEL), BF16)
    w = SB_HP_BWD * LANE
    outs = pl.pallas_call(
        body,
        name="sb_bwd",
        grid=(N_HEADS // SB_HP_BWD, t // tq),
        in_specs=[q_spec, k_spec, v_spec, z_spec, blk, blk] + [ANY] * nb,
        out_specs=[blk, full, full, blk] + [ANY] * nb,
        out_shape=[o, o, o, o] + [jax.ShapeDtypeStruct(b.shape, b.dtype) for b in blocks],
        scratch_shapes=[pltpu.VMEM((t, w), F32), pltpu.VMEM((t, w), F32)]
        + [pltpu.VMEM((SB_HP_BWD, t // SB_TK, tq, SB_TK), F32)] * 2 + _gather_sems(nb),
    )(proj, proj, proj, proj, oraw, do, *blocks)
    return outs[0], outs[1], outs[2], outs[3], outs[4:]


def _mem_kv_fn(mem, mg, w):
    return mm_nn(_rmsnorm(mem, mg), w)


def _mem_kv(mem, mg, w):
    def body(m_ref, g_ref, w_ref, o_ref):
        o_ref[...] = _mem_kv_fn(m_ref[...], g_ref[...], w_ref[...])

    return pl.pallas_call(body, name="mem_kv", out_shape=jax.ShapeDtypeStruct((MEM_LEN, 2 * MEM_W), F32))(mem, mg, w)


def _mem_kv_bwd(mem, mg, w, dmkv):
    def body(m_ref, g_ref, w_ref, d_ref, dg_ref, dw_ref):
        _, vjp = jax.vjp(_mem_kv_fn, m_ref[...], g_ref[...], w_ref[...].astype(F32))
        _, dg, dw = vjp(d_ref[...])
        dg_ref[...] = dg
        dw_ref[...] = dw.astype(BF16)

    return pl.pallas_call(
        body, name="mem_kv_bwd",
        out_shape=[jax.ShapeDtypeStruct((1, D_MODEL), F32), jax.ShapeDtypeStruct((D_MODEL, 2 * MEM_W), BF16)],
    )(mem, mg, w, dmkv)


def _mem_attn(proj, mkv, tm=1024):
    t = proj.shape[0]
    tm = min(tm, t)

    def body(q_ref, z_ref, kv_ref, o_ref):
        o_ref[...] = _mem_fn(q_ref[...], z_ref[...], kv_ref[...]).astype(BF16)

    return pl.pallas_call(
        body,
        name="mem_attn",
        grid=(t // tm,),
        in_specs=[pl.BlockSpec((tm, MEM_W), lambda i: (i, O_MQ // MEM_W)),
                  pl.BlockSpec((tm, MEM_W), lambda i: (i, O_MZ // MEM_W)),
                  pl.BlockSpec((MEM_LEN, 2 * MEM_W), lambda i: (0, 0))],
        out_specs=pl.BlockSpec((tm, MEM_W), lambda i: (i, 0)),
        out_shape=jax.ShapeDtypeStruct((t, MEM_W), BF16),
    )(proj, proj, mkv)


def _mem_attn_bwd(proj, mkv, do, tm=1024):
    t = proj.shape[0]
    tm = min(tm, t)

    def body(q_ref, z_ref, kv_ref, do_ref, dq_ref, dz_ref, dkv_ref):
        _, vjp = jax.vjp(_mem_fn, q_ref[...], z_ref[...], kv_ref[...])
        dq, dz, dkv = vjp(do_ref[...].astype(F32))
        dq_ref[...] = dq.astype(BF16)
        dz_ref[...] = dz.astype(BF16)

        @pl.when(pl.program_id(0) == 0)
        def _():
            dkv_ref[...] = jnp.zeros_like(dkv_ref)

        dkv_ref[...] += dkv

    blk = pl.BlockSpec((tm, MEM_W), lambda i: (i, 0))
    kv = pl.BlockSpec((MEM_LEN, 2 * MEM_W), lambda i: (0, 0))
    return pl.pallas_call(
        body,
        name="mem_attn_bwd",
        grid=(t // tm,),
        in_specs=[pl.BlockSpec((tm, MEM_W), lambda i: (i, O_MQ // MEM_W)),
                  pl.BlockSpec((tm, MEM_W), lambda i: (i, O_MZ // MEM_W)), kv, blk],
        out_specs=[blk, blk, kv],
        out_shape=[jax.ShapeDtypeStruct((t, MEM_W), BF16), jax.ShapeDtypeStruct((t, MEM_W), BF16),
                   jax.ShapeDtypeStruct((MEM_LEN, 2 * MEM_W), F32)],
    )(proj, proj, mkv, do)


def _proj_gather(h, w_alt, shards, tm=512, tn=3968):
    t = h.shape[0]
    tm = min(tm, t)
    n, kdim = w_alt.shape
    assert n % tn == 0 and t % tm == 0
    nj, ni = n // tn, t // tm
    na = len(shards)

    def body(h_ref, w_ref, *rest):
        x_refs, o_ref, land = rest[:na], rest[na], rest[na + 1:2 * na + 1]
        send_sems, recv_sems, local_sems = rest[2 * na + 1:]
        j, i = pl.program_id(0), pl.program_id(1)

        def copies():
            return _direct_gather_copies(x_refs, land, send_sems, recv_sems, local_sems)

        @pl.when(jnp.logical_and(j == 0, i == 0))
        def _():
            for cp in copies():
                cp.start()

        o_ref[...] = _dot(h_ref[...], w_ref[...], NT)

        @pl.when(jnp.logical_and(j == nj - 1, i == ni - 1))
        def _():
            for cp in copies():
                cp.wait()

    outs = pl.pallas_call(
        body,
        name="proj",
        grid=(nj, ni),
        in_specs=[pl.BlockSpec((tm, kdim), lambda j, i: (i, 0)), pl.BlockSpec((tn, kdim), lambda j, i: (j, 0))]
        + [ANY] * na,
        out_specs=[pl.BlockSpec((tm, tn), lambda j, i: (i, j))] + [ANY] * na,
        out_shape=[jax.ShapeDtypeStruct((t, n), F32)]
        + [jax.ShapeDtypeStruct((N_DEV, *v.shape), v.dtype) for v in shards],
        scratch_shapes=_gather_sems(na),
    )(h, w_alt, *shards)
    return outs[0], outs[1:]


def _local_step(x, mem, tgt, norm_g, mem_norm_g, w_alt, alog_row, dtb_row, dn_norm_g, final_g, shards):
    h = _norm_in(x, norm_g)
    s_kv, s_dn, s_sb, s_out, s_mem, s_conv = shards
    proj, (g_kv, g_conv) = _proj_gather(h, w_alt, [s_kv, s_conv])
    w_mem_kv = g_kv.reshape(D_MODEL, 2 * MEM_W)
    conv_w = g_conv.transpose(1, 0, 2).reshape(CONV_K, 3 * D_MODEL)

    c = _dn_conv(proj, conv_w)
    o_dn, states = _dn_fwd(c, proj, alog_row, dtb_row, dn_norm_g)
    o_sb, o_sb_raw, (g_dn, g_sb, g_out, g_mem) = _sb_fwd(proj, [s_dn, s_sb, s_out, s_mem])
    w_br_dn = g_dn.reshape(D_MODEL, D_MODEL)
    w_br_sb = g_sb.reshape(D_MODEL, D_MODEL)
    w_out = g_out.reshape(D_MODEL, D_MODEL)
    w_br_mem = g_mem.transpose(1, 0, 2).reshape(MEM_W, D_MODEL)
    mkv = _mem_kv(mem, mem_norm_g, w_mem_kv)
    o_m = _mem_attn(proj, mkv)

    (loss, dout, d_final_g, merged, dy_dn, dy_sb, dy_m, dgates, do_dn, do_sb, do_m) = _block_tail(
        proj, o_dn, o_sb, o_m, x, tgt, w_br_dn, w_br_sb, w_br_mem, w_out, final_g)
    dw_out = _matmul_tn(merged, dout, BF16, 256, 1024, 2048, "dw_out")
    dw_br_dn = _matmul_tn(o_dn, dy_dn, BF16, 256, 1024, 2048, "dw_br_dn")
    dw_br_sb = _matmul_tn(o_sb, dy_sb, BF16, 256, 1024, 2048, "dw_br_sb")
    dw_br_mem = _matmul_tn(o_m, dy_m, BF16, 256, 1024, 2048, "dw_br_mem")

    dmq, dmz, dmkv = _mem_attn_bwd(proj, mkv, do_m)
    d_mem_norm_g, dw_mem_kv = _mem_kv_bwd(mem, mem_norm_g, w_mem_kv, dmkv)
    rows_d = D_MODEL // N_DEV
    small_blocks = [
        dw_br_dn.reshape(N_DEV, rows_d, D_MODEL), dw_br_sb.reshape(N_DEV, rows_d, D_MODEL),
        dw_out.reshape(N_DEV, rows_d, D_MODEL), dw_mem_kv.reshape(N_DEV, rows_d // 2, D_MODEL),
        dw_br_mem.reshape(MEM_W, N_DEV, rows_d).transpose(1, 0, 2).reshape(N_DEV, MEM_W // N_DEV, D_MODEL)]
    dq_sb, dk_sb, dv_sb, dz_sb, small_parts = _sb_bwd(proj, o_sb_raw, do_sb, small_blocks)
    d_small = _sum_slots(list(small_parts), "sum_small_grads")
    dc, dz_dn, dba, dscal, d_dn_norm_g = _dn_bwd(c, proj, alog_row, dtb_row, dn_norm_g, states, do_dn)
    dqkv_dn, d_conv_w = _dn_conv_bwd(proj, conv_w, dc)

    dproj = [dqkv_dn, dz_dn, dq_sb, dk_sb, dv_sb, dz_sb, dmq, dmz, dgates, dba.astype(BF16)]
    dw_alt = _dw_alt(dproj, h)
    grad_x, d_norm_g = _grad_x(dproj, w_alt, x, norm_g, dout)
    return dict(loss=loss, grad_x=grad_x, norm_g=d_norm_g, mem_norm_g=d_mem_norm_g, w_alt=dw_alt, conv_w=d_conv_w,
                scal=dscal, dn_norm_g=d_dn_norm_g, small=d_small, final_g=d_final_g)


MESH = pl.DeviceIdType.MESH
ANY = pl.BlockSpec(memory_space=pl.ANY)


def _position():
    return lax.axis_index("x"), lax.axis_index("y"), lax.axis_index("c")


def _other_devices(me):
    return [tuple(1 - p if (f >> s) & 1 else p for p, s in zip(me, (2, 1, 0))) for f in range(1, N_DEV)]


def _direct_gather_copies(x_refs, land_refs, send_sems, recv_sems, local_sems):
    me = _position()
    mine = 4 * me[0] + 2 * me[1] + me[2]
    cps = []
    for a, (x_ref, land) in enumerate(zip(x_refs, land_refs)):
        cps.append(pltpu.make_async_copy(x_ref, land.at[mine], local_sems.at[a]))
        for k, peer in enumerate(_other_devices(me)):
            cps.append(pltpu.make_async_remote_copy(
                src_ref=x_ref, dst_ref=land.at[mine], send_sem=send_sems.at[7 * a + k],
                recv_sem=recv_sems.at[7 * a + k], device_id=peer, device_id_type=MESH))
    return cps


def _gather_sems(n):
    return [pltpu.SemaphoreType.DMA((7 * n,)), pltpu.SemaphoreType.DMA((7 * n,)), pltpu.SemaphoreType.DMA((n,))]


def _all_gather(xs, name):
    n = len(xs)

    def body(*refs):
        x_refs, o_refs = refs[:n], refs[n:2 * n]
        send_sems, recv_sems, local_sems = refs[2 * n:]
        x, y, c = _position()
        me, sibling = (x, y, c), (x, y, 1 - c)
        x_nbr, y_nbr, diag = (1 - x, y, c), (x, 1 - y, c), (1 - x, 1 - y, c)
        south = c == 0
        relay_from = tuple(jnp.where(south, a, b) for a, b in zip(y_nbr, x_nbr))
        relay_to = tuple(jnp.where(south, a, b) for a, b in zip(x_nbr, y_nbr))

        def slot(p):
            return 4 * p[0] + 2 * p[1] + p[2]

        def copy(a, k, block, to, src=None):
            dst = o_refs[a].at[slot(block)]
            return pltpu.make_async_remote_copy(
                src_ref=dst if src is None else src, dst_ref=dst, send_sem=send_sems.at[7 * a + k],
                recv_sem=recv_sems.at[7 * a + k], device_id=to, device_id_type=MESH)

        mine = [pltpu.make_async_copy(x_refs[a], o_refs[a].at[slot(me)], local_sems.at[a]) for a in range(n)]
        for cp in mine:
            cp.start()
        sends = []
        for a in range(n):
            sends += [copy(a, 0, me, sibling, src=x_refs[a]), copy(a, 1, me, x_nbr, src=x_refs[a]),
                      copy(a, 2, me, y_nbr, src=x_refs[a])]
        for cp in sends:
            cp.start()
        later = []
        for a in range(n):
            copy(a, 1, x_nbr, me).wait_recv()
            copy(a, 2, y_nbr, me).wait_recv()
            later += [copy(a, 3, relay_from, relay_to), copy(a, 4, x_nbr, sibling), copy(a, 5, y_nbr, sibling)]
            for cp in later[-3:]:
                cp.start()
        for a in range(n):
            copy(a, 3, diag, me).wait_recv()
            later.append(copy(a, 6, diag, sibling))
            later[-1].start()
        for a in range(n):
            copy(a, 0, sibling, me).wait_recv()
            for k, chip in ((4, x_nbr), (5, y_nbr), (6, diag)):
                copy(a, k, (chip[0], chip[1], 1 - c), me).wait_recv()
        for cp in sends + later:
            cp.wait_send()
        for cp in mine:
            cp.wait()

    return pl.pallas_call(
        body,
        name=name,
        in_specs=[ANY] * n,
        out_specs=[ANY] * n,
        out_shape=[jax.ShapeDtypeStruct((N_DEV, *v.shape), v.dtype) for v in xs],
        scratch_shapes=[pltpu.SemaphoreType.DMA((7 * n,)), pltpu.SemaphoreType.DMA((7 * n,)),
                        pltpu.SemaphoreType.DMA((n,))],
    )(*xs)


def _window_view(ref, dest):
    return ref.at[pl.ds(WIN_ROW0[dest], WIN_W), :]


def _chunk_rows(rows, cols):
    return max(ch for ch in range(ROW_TILE, rows + 1, ROW_TILE) if rows % ch == 0 and ch * cols <= (1 << 20))


def _halving_stage(xs, axis, name, out_dtype, windowed=(), gather=()):
    n_arr = len(xs)
    metas = []
    for k, v in enumerate(xs):
        if k in windowed:
            metas.append((N_DEV // 2, WIN_W, v.shape[1]))
        else:
            assert v.shape[1] == 2
            metas.append((v.shape[0], v.shape[2], v.shape[3]))
    chunk = [_chunk_rows(r, c) for (_, r, c) in metas]
    offs = [sum(m[0] for m in metas[:k]) for k in range(n_arr)]
    n_sem = sum(m[0] for m in metas)

    n_g = len(gather)

    def body(*refs):
        x_refs, g_refs = refs[:n_arr], refs[n_arr:n_arr + n_g]
        outs = refs[n_arr + n_g:]
        o_refs, land_refs, gl_refs = outs[:n_arr], outs[n_arr:2 * n_arr], outs[2 * n_arr:2 * n_arr + n_g]
        rest = outs[2 * n_arr + n_g:]
        bufs = rest[:3 * n_arr]
        send_sems, recv_sems, in_sems, out_sems = rest[3 * n_arr:3 * n_arr + 4]
        gathers = _direct_gather_copies(g_refs, gl_refs, *rest[3 * n_arr + 4:]) if n_g else []
        for cp in gathers:
            cp.start()
        pos = dict(zip("xyc", _position()))
        bit = pos[axis]
        peer = tuple(1 - pos[a] if a == axis else pos[a] for a in "xyc")

        def view(k, i, b):
            if k in windowed:
                return _window_view(x_refs[k], 2 * i + b)
            return x_refs[k].at[i, b]

        def add_blocks(k, a_view, b_view, o_view):
            _hbm_add(a_view, b_view, o_view, bufs[3 * k:3 * k + 3], in_sems, out_sems, chunk[k])

        for b in (0, 1):
            @pl.when(bit == b)
            def _(b=b):
                sends = []
                for k in range(n_arr):
                    for i in range(metas[k][0]):
                        cp = pltpu.make_async_remote_copy(
                            src_ref=view(k, i, 1 - b), dst_ref=land_refs[k].at[i], send_sem=send_sems.at[offs[k] + i],
                            recv_sem=recv_sems.at[offs[k] + i], device_id=peer, device_id_type=MESH)
                        cp.start()
                        sends.append(cp)
                idx = 0
                for k in range(n_arr):
                    for i in range(metas[k][0]):
                        sends[idx].wait_recv()
                        add_blocks(k, view(k, i, b), land_refs[k].at[i], o_refs[k].at[i])
                        idx += 1
                for cp in sends:
                    cp.wait_send()

        for cp in gathers:
            cp.wait()

    out_shape = [jax.ShapeDtypeStruct(m, out_dtype) for m in metas]
    land_shape = [jax.ShapeDtypeStruct(m, v.dtype) for m, v in zip(metas, xs)]
    g_shape = [jax.ShapeDtypeStruct((N_DEV, *v.shape), v.dtype) for v in gather]
    scratch = []
    for k in range(n_arr):
        blk = (2, chunk[k], metas[k][2])
        scratch += [pltpu.VMEM(blk, xs[k].dtype)] * 2 + [pltpu.VMEM(blk, out_dtype)]
    scratch += [pltpu.SemaphoreType.DMA((n_sem,)), pltpu.SemaphoreType.DMA((n_sem,)),
                pltpu.SemaphoreType.DMA((2, 2)), pltpu.SemaphoreType.DMA((2,))]
    if n_g:
        scratch += _gather_sems(n_g)
    outs = pl.pallas_call(
        body,
        name=name,
        in_specs=[ANY] * (n_arr + n_g),
        out_specs=[ANY] * (2 * n_arr + n_g),
        out_shape=out_shape + land_shape + g_shape,
        scratch_shapes=scratch,
    )(*xs, *gather)
    return outs[:n_arr], outs[2 * n_arr:]


def _hbm_add(a_view, b_view, o_view, bufs, in_sems, out_sems, ch):
    rows = a_view.shape[0]
    nch = rows // ch
    va, vb, vo = bufs

    def rows_of(j):
        return pl.ds(pl.multiple_of(j * ch, 16), ch)

    def loads(j, s):
        return (pltpu.make_async_copy(a_view.at[rows_of(j), :], va.at[s], in_sems.at[0, s]),
                pltpu.make_async_copy(b_view.at[rows_of(j), :], vb.at[s], in_sems.at[1, s]))

    def store(j, s):
        return pltpu.make_async_copy(vo.at[s], o_view.at[rows_of(j), :], out_sems.at[s])

    for cp in loads(0, 0):
        cp.start()

    def step(j, _):
        s = lax.rem(j, 2)

        @pl.when(j + 1 < nch)
        def _():
            for cp in loads(j + 1, 1 - s):
                cp.start()

        for cp in loads(j, s):
            cp.wait()

        @pl.when(j >= 2)
        def _():
            store(j - 2, s).wait()

        vo[s] = (va[s].astype(F32) + vb[s].astype(F32)).astype(vo.dtype)
        store(j, s).start()
        return 0

    lax.fori_loop(0, nch, step, 0)
    for j in range(max(0, nch - 2), nch):
        store(j, j % 2).wait()


def _xy_stage(xs, first, name):
    n_arr = len(xs)
    if first:
        shapes = [(v.shape[2] // 2, v.shape[3]) for v in xs]
        ins = list(xs)
    else:
        shapes = [(a.shape[1], a.shape[2]) for a, _ in xs]
        ins = [v for pair in xs for v in pair]
    n_blk = 2 if first else 1
    out_dtype = BF16 if first else F32
    chunk = [_chunk_rows(r, c) for (r, c) in shapes]
    n_sem = 2 * n_blk * n_arr

    def body(*refs):
        n_in = len(ins)
        in_refs = refs[:n_in]
        n_out = 2 * n_arr if first else n_arr
        o_refs = refs[n_in:n_in + n_out]
        land = refs[n_in + n_out:n_in + n_out + 2 * n_arr]
        rest = refs[n_in + n_out + 2 * n_arr:]
        bufs = rest[:3 * n_arr]
        send_sems, recv_sems, in_sems, out_sems = rest[3 * n_arr:]
        x, y, c = _position()
        peers = {"x": (1 - x, y, c), "y": (x, 1 - y, c)}
        jobs = []
        for k in range(n_arr):
            r, _ = shapes[k]
            half_a, half_b = pl.ds(0, r), pl.ds(r, r)
            if first:
                src = in_refs[k]
                for i in range(2):
                    jobs.append((k, src.at[i, 1 - y, half_a, :], src.at[i, y, half_a, :], land[2 * k].at[i],
                                 o_refs[2 * k].at[i], "y"))
                    jobs.append((k, src.at[1 - x, i, half_b, :], src.at[x, i, half_b, :], land[2 * k + 1].at[i],
                                 o_refs[2 * k + 1].at[i], "x"))
            else:
                a1, b1 = in_refs[2 * k], in_refs[2 * k + 1]
                jobs.append((k, a1.at[1 - x], a1.at[x], land[2 * k], o_refs[k].at[half_a, :], "x"))
                jobs.append((k, b1.at[1 - y], b1.at[y], land[2 * k + 1], o_refs[k].at[half_b, :], "y"))
        sends = []
        for n, (k, send, _, landing, _, axis) in enumerate(jobs):
            cp = pltpu.make_async_remote_copy(src_ref=send, dst_ref=landing, send_sem=send_sems.at[n],
                                              recv_sem=recv_sems.at[n], device_id=peers[axis], device_id_type=MESH)
            cp.start()
            sends.append(cp)
        for cp, (k, _, kept, landing, out, _) in zip(sends, jobs):
            cp.wait_recv()
            _hbm_add(kept, landing, out, bufs[3 * k:3 * k + 3], in_sems, out_sems, chunk[k])
        for cp in sends:
            cp.wait_send()

    if first:
        out_shape = [jax.ShapeDtypeStruct((2, r, c), BF16) for (r, c) in shapes for _ in range(2)]
        land_shape = out_shape
    else:
        out_shape = [jax.ShapeDtypeStruct((2 * r, c), F32) for (r, c) in shapes]
        land_shape = [jax.ShapeDtypeStruct((r, c), BF16) for (r, c) in shapes for _ in range(2)]
    scratch = []
    for k in range(n_arr):
        scratch += [pltpu.VMEM((2, chunk[k], shapes[k][1]), BF16)] * 2 + [pltpu.VMEM((2, chunk[k], shapes[k][1]), out_dtype)]
    scratch += [pltpu.SemaphoreType.DMA((n_sem,)), pltpu.SemaphoreType.DMA((n_sem,)),
                pltpu.SemaphoreType.DMA((2, 2)), pltpu.SemaphoreType.DMA((2,))]
    outs = pl.pallas_call(
        body,
        name=name,
        in_specs=[ANY] * len(ins),
        out_specs=[ANY] * (len(out_shape) + len(land_shape)),
        out_shape=out_shape + land_shape,
        scratch_shapes=scratch,
    )(*ins)
    outs = outs[:len(out_shape)]
    return [(outs[2 * k], outs[2 * k + 1]) for k in range(n_arr)] if first else list(outs)


def _reduce_scatter(dw_al, blocks, gather=()):
    xs = [dw_al] + [b.reshape(N_DEV // 2, 2, *b.shape[1:]) for b in blocks]
    ys, gathered = _halving_stage(xs, "c", "rs_c", BF16, windowed=(0,), gather=gather)
    pairs = _xy_stage([v.reshape(2, 2, *v.shape[1:]) for v in ys], True, "rs_xy1")
    return _xy_stage(pairs, False, "rs_xy2"), gathered


def _sum_slots(gs, name):
    n = len(gs)

    def body(*refs):
        for g_ref, o_ref in zip(refs[:n], refs[n:]):
            acc = g_ref[0].astype(F32)
            for d in range(1, N_DEV):
                acc = acc + g_ref[d].astype(F32)
            o_ref[...] = acc

    return pl.pallas_call(body, name=name, out_shape=[jax.ShapeDtypeStruct(g.shape[1:], F32) for g in gs])(*gs)


def _assemble_w_al(wins, bas):
    cols = wins.shape[2]
    n_buf = 3
    ends = [WIN_ROW0[d + 1] if d + 1 < N_DEV else WIN_ROW0[d] + WIN_W for d in range(N_DEV)]
    tail = W_AL - ends[-1]

    def body(w_ref, ba_ref, o_ref, buf, zeros, ld_sems, st_sems, ba_sem):
        def load(d):
            return pltpu.make_async_copy(w_ref.at[d], buf.at[d % n_buf], ld_sems.at[d % n_buf])

        def store(d):
            n = ends[d] - WIN_ROW0[d]
            return pltpu.make_async_copy(buf.at[d % n_buf, pl.ds(0, n), :],
                                         o_ref.at[pl.ds(WIN_ROW0[d], n), :], st_sems.at[d % n_buf])

        zeros[...] = jnp.zeros_like(zeros)
        fill = pltpu.make_async_copy(zeros, o_ref.at[pl.ds(ends[-1], tail), :], ba_sem)
        fill.start()
        fill.wait()
        load(0).start()
        for d in range(N_DEV):
            if d + 1 < N_DEV:
                if d + 1 >= n_buf:
                    store(d + 1 - n_buf).wait()
                load(d + 1).start()
            load(d).wait()
            if d > 0:
                ov = WIN_ROW0[d - 1] + WIN_W - WIN_ROW0[d]
                buf[d % n_buf, :ov, :] = buf[d % n_buf, :ov, :] + buf[(d - 1) % n_buf, WIN_W - ov:, :]
            if d == N_DEV - 1:
                ba_copy = pltpu.make_async_copy(
                    ba_ref.at[BA_DEV], buf.at[d % n_buf, pl.ds(WIN_W - N_BA, N_BA), :], ba_sem)
                ba_copy.start()
                ba_copy.wait()
            store(d).start()
        for d in range(N_DEV - n_buf, N_DEV):
            store(d).wait()

    return pl.pallas_call(
        body,
        name="assemble_w_al",
        in_specs=[ANY, ANY],
        out_specs=ANY,
        out_shape=jax.ShapeDtypeStruct((W_AL, cols), wins.dtype),
        scratch_shapes=[pltpu.VMEM((n_buf, WIN_W, cols), wins.dtype), pltpu.VMEM((tail, cols), wins.dtype),
                        pltpu.SemaphoreType.DMA((n_buf,)), pltpu.SemaphoreType.DMA((n_buf,)), pltpu.SemaphoreType.DMA],
    )(wins, bas)


def _adamw_math(w, g, m, v):
    m_new = ADAM_B1 * m + (1.0 - ADAM_B1) * g
    v_new = ADAM_B2 * v + (1.0 - ADAM_B2) * (g * g)
    m_hat = m_new / (1.0 - ADAM_B1 ** ADAM_STEP)
    v_hat = v_new / (1.0 - ADAM_B2 ** ADAM_STEP)
    return -ADAM_LR * (m_hat / (jnp.sqrt(v_hat) + ADAM_EPS) + ADAM_WD * w), m_new, v_new


def _adamw(w, g, m, v, name, tb=134):
    r, _, c = w.shape
    assert r % tb == 0

    def body(w_ref, g_ref, m_ref, v_ref, d_ref, nm_ref, nv_ref):
        d_ref[...], nm_ref[...], nv_ref[...] = _adamw_math(w_ref[...], g_ref[...], m_ref[...], v_ref[...])

    blk = pl.BlockSpec((tb, 1, c), lambda i: (i, 0, 0))
    o = jax.ShapeDtypeStruct(w.shape, F32)
    return pl.pallas_call(body, name=name, grid=(r // tb,), in_specs=[blk] * 4, out_specs=[blk] * 3,
                          out_shape=[o, o, o])(w, g, m, v)


def _adamw_many(ws, gs, ms, vs, name):
    n = len(ws)

    def body(*refs):
        for k in range(n):
            w_ref, g_ref, m_ref, v_ref = (refs[j * n + k] for j in range(4))
            d_ref, nm_ref, nv_ref = (refs[(4 + j) * n + k] for j in range(3))
            d_ref[...], nm_ref[...], nv_ref[...] = _adamw_math(w_ref[...], g_ref[...], m_ref[...], v_ref[...])

    shapes = [jax.ShapeDtypeStruct(w.shape, F32) for w in ws]
    outs = pl.pallas_call(body, name=name, out_shape=shapes * 3)(*ws, *gs, *ms, *vs)
    return outs[:n], outs[n:2 * n], outs[2 * n:]


def _select(me, table):
    return sum(jnp.where(me == d, jnp.int32(v), jnp.int32(0)) for d, v in enumerate(table))


WIN_SHIFT = tuple(SHARD_W * d - WIN_ROW0[d] for d in range(N_DEV))
PAD_L = 64
PAD_R = 64
assert max(WIN_SHIFT) <= PAD_L and WIN_W + N_BA - SHARD_W <= PAD_R


def _shard_to_window(shard_t, me):
    shift = _select(me, WIN_SHIFT)
    padded = jnp.pad(shard_t, ((PAD_L, PAD_R), (0, 0)))
    cols = shard_t.shape[1]
    lo = lax.dynamic_slice(padded, (PAD_L - shift, 0), (WIN_W, cols))
    hi = lax.dynamic_slice(padded, (PAD_L - shift + N_BA, 0), (WIN_W, cols))
    aligned = _select(me, WIN_ROW0) + lax.broadcasted_iota(jnp.int32, (WIN_W, 1), 0)
    return jnp.where(aligned >= ORIG_BA, hi, lo)


def _window_to_shard(win, ba_grad, me):
    shift = _select(me, WIN_SHIFT)
    cols = win.shape[1]
    padded = jnp.pad(win, ((N_BA, PAD_R), (0, 0)))
    lo = lax.dynamic_slice(padded, (N_BA + shift, 0), (SHARD_W, cols))
    hi = lax.dynamic_slice(padded, (shift, 0), (SHARD_W, cols))
    orig = SHARD_W * me + lax.broadcasted_iota(jnp.int32, (SHARD_W, 1), 0)
    ba_full = lax.dynamic_update_slice(jnp.zeros((SHARD_W, cols), win.dtype), ba_grad, (BA_LOCAL, 0))
    return jnp.where(orig < ORIG_BA, lo, jnp.where(orig >= ORIG_BA + N_BA, hi, ba_full))


def _pad_row(v, width=D_MODEL):
    v = v.reshape(1, -1)
    return jnp.pad(v, ((0, 0), (0, width - v.shape[1])))


def kernel(x, mem, norm_g, mem_norm_g, w_in, conv_w, a_log, dt_bias, dn_norm_g, w_mem_kv, w_br_dn, w_br_sb, w_br_mem, w_out, final_g, loss_target, m_norm_g, m_mem_norm_g, m_w_in, m_conv_w, m_a_log, m_dt_bias, m_dn_norm_g, m_w_mem_kv, m_w_br_dn, m_w_br_sb, m_w_br_mem, m_w_out, m_final_g, v_norm_g, v_mem_norm_g, v_w_in, v_conv_w, v_a_log, v_dt_bias, v_dn_norm_g, v_w_mem_kv, v_w_br_dn, v_w_br_sb, v_w_br_mem, v_w_out, v_final_g):
    xi, yi, ci = _position()
    me = 4 * xi + 2 * yi + ci

    shard_t = w_in[0].T
    win = _shard_to_window(shard_t, me).astype(BF16)
    ba = shard_t[BA_LOCAL:BA_LOCAL + N_BA, :].astype(BF16)
    g_win, g_ba = _all_gather([win, ba], "gather_weights")
    w_alt = _assemble_w_al(g_win, g_ba)

    shards = [w_mem_kv[0].astype(BF16), w_br_dn[0].astype(BF16), w_br_sb[0].astype(BF16), w_out[0].astype(BF16),
              w_br_mem[0].astype(BF16), conv_w[0]]
    r = _local_step(x[0], mem[0], loss_target[0], norm_g, mem_norm_g, w_alt, _pad_row(a_log, LANE),
                    _pad_row(dt_bias, LANE), dn_norm_g, final_g.reshape(1, D_MODEL), shards)

    dw_alt = r["w_alt"]
    parts = [r["norm_g"], r["mem_norm_g"], r["final_g"], r["dn_norm_g"], r["scal"], r["loss"], r["conv_w"],
             dw_alt[O_BA:O_BA + N_BA, :].astype(F32)]
    (g_win,), gathered = _reduce_scatter(dw_alt, [], gather=parts)
    rows_d = D_MODEL // N_DEV
    g_dn, g_sb, g_out, g_kv, g_mem = r["small"]
    g_kv = g_kv.reshape(rows_d, 2 * MEM_W)
    g_mem = g_mem.reshape(MEM_W, rows_d)
    s_norm_g, s_mem_norm_g, s_final_g, s_dn_norm_g, s_scal, s_loss, s_conv, s_ba = _sum_slots(gathered, "sum_small")
    loss = s_loss[0, 0]
    cw = conv_w.shape[2]
    g_conv = lax.dynamic_slice(s_conv, (0, cw * me), (CONV_K, cw))
    g_w_in_t = _window_to_shard(g_win, s_ba, me)
    grads = dict(norm_g=s_norm_g, mem_norm_g=s_mem_norm_g, w_in=g_w_in_t.T[None], conv_w=g_conv[None],
                 a_log=s_scal[0:1, :N_HEADS], dt_bias=s_scal[1:2, :N_HEADS], dn_norm_g=s_dn_norm_g, w_mem_kv=g_kv[None],
                 w_br_dn=g_dn[None], w_br_sb=g_sb[None], w_br_mem=g_mem[None], w_out=g_out[None],
                 final_g=s_final_g.reshape(D_MODEL))

    params = dict(norm_g=(norm_g, m_norm_g, v_norm_g), mem_norm_g=(mem_norm_g, m_mem_norm_g, v_mem_norm_g),
                  w_in=(w_in, m_w_in, v_w_in), conv_w=(conv_w, m_conv_w, v_conv_w), a_log=(a_log, m_a_log, v_a_log),
                  dt_bias=(dt_bias, m_dt_bias, v_dt_bias), dn_norm_g=(dn_norm_g, m_dn_norm_g, v_dn_norm_g),
                  w_mem_kv=(w_mem_kv, m_w_mem_kv, v_w_mem_kv), w_br_dn=(w_br_dn, m_w_br_dn, v_w_br_dn),
                  w_br_sb=(w_br_sb, m_w_br_sb, v_w_br_sb), w_br_mem=(w_br_mem, m_w_br_mem, v_w_br_mem),
                  w_out=(w_out, m_w_out, v_w_out), final_g=(final_g, m_final_g, v_final_g))
    order = list(params)
    deltas, new_m, new_v = {}, {}, {}
    deltas["w_in"], new_m["w_in"], new_v["w_in"] = (jnp.transpose(o, (1, 2, 0)) for o in _adamw(
        jnp.transpose(w_in, (2, 0, 1)), g_w_in_t[:, None, :], jnp.transpose(m_w_in, (2, 0, 1)),
        jnp.transpose(v_w_in, (2, 0, 1)), "adamw_w_in"))
    rest = [nm for nm in order if nm != "w_in"]

    def two_d(a):
        return a.reshape(1, -1) if a.ndim == 1 else a

    d_l, m_l, v_l = _adamw_many([two_d(params[nm][0]) for nm in rest], [two_d(grads[nm]) for nm in rest],
                                [two_d(params[nm][1]) for nm in rest], [two_d(params[nm][2]) for nm in rest], "adamw_rest")
    for k, nm in enumerate(rest):
        shp = params[nm][0].shape
        deltas[nm], new_m[nm], new_v[nm] = d_l[k].reshape(shp), m_l[k].reshape(shp), v_l[k].reshape(shp)
    return (loss, r["grad_x"][None], *[grads[nm] for nm in order], *[deltas[nm] for nm in order],
            *[new_m[nm] for nm in order], *[new_v[nm] for nm in order])
```

```python
import functools
import math

import jax
import jax.numpy as jnp
from jax import lax
from jax.experimental import pallas as pl
from jax.experimental.pallas import tpu as pltpu

F32 = jnp.float32
BF16 = jnp.bfloat16

D_MODEL = 1024
N_DEV = 8
N_HEADS = 8
D_HEAD = 128
DN_CHUNK = 64
CONV_K = 4
MEM_LEN = 256
MEM_HEADS = 4
MEM_DH = 64
MEM_W = MEM_HEADS * MEM_DH
NORM_EPS = 1e-6
IN_WIDTH = 11792
SHARD_W = IN_WIDTH // N_DEV

LANE = 128
SUPER = 2 * DN_CHUNK

O_QKV_DN = 0
O_Z_DN = 3072
O_QKV_SB = 4096
O_Z_SB = 7168
O_MQ = 8192
O_MZ = 8448
O_GATES = 8704
O_BA = 11776
W_AL = 11904
ORIG_BA = 4096
N_BA = 16

BA_DEV = ORIG_BA // SHARD_W
BA_LOCAL = ORIG_BA - BA_DEV * SHARD_W


def _aligned_col(o):
    return o if o < ORIG_BA else o - N_BA


ROW_TILE = 16
WIN_W = 1504
WIN_ROW0 = tuple(_aligned_col(SHARD_W * d) // ROW_TILE * ROW_TILE for d in range(N_DEV))
assert not any(ORIG_BA <= SHARD_W * d < ORIG_BA + N_BA for d in range(N_DEV))
assert all(WIN_ROW0[d] + WIN_W >= _aligned_col(SHARD_W * (d + 1) - 1) + 1 for d in range(N_DEV))
assert all(WIN_ROW0[d + 1] <= WIN_ROW0[d] + WIN_W for d in range(N_DEV - 1))
assert WIN_ROW0[-1] + WIN_W == O_BA + N_BA

ADAM_LR = 0.001
ADAM_B1 = 0.9
ADAM_B2 = 0.999
ADAM_EPS = 1e-08
ADAM_WD = 0.01
ADAM_STEP = 10

NN = (((1,), (0,)), ((), ()))
NT = (((1,), (1,)), ((), ()))
TN = (((0,), (0,)), ((), ()))


def _dot(a, b, dims):
    return lax.dot_general(a.astype(BF16), b.astype(BF16), dims, preferred_element_type=F32)


def _split2(a):
    hi = a.astype(BF16)
    lo = (a - hi.astype(F32)).astype(BF16)
    return hi, lo


def _dot3(a, b, dims):
    ah, al = _split2(a)
    bh, bl = _split2(b)
    d = functools.partial(lax.dot_general, dimension_numbers=dims, preferred_element_type=F32)
    return d(ah, bh) + (d(ah, bl) + d(al, bh))


def _sel_dot_impl(sel01, x, dims):
    sel = sel01.astype(BF16)
    h1 = x.astype(BF16)
    r1 = x - h1.astype(F32)
    h2 = r1.astype(BF16)
    h3 = (r1 - h2.astype(F32)).astype(BF16)
    d = functools.partial(lax.dot_general, dimension_numbers=dims, preferred_element_type=F32)
    return d(sel, h1) + (d(sel, h2) + d(sel, h3))


@jax.custom_vjp
def _sel_dot(sel01, x):
    return _sel_dot_impl(sel01, x, NN)


_sel_dot.defvjp(lambda s, x: (_sel_dot(s, x), s),
                lambda s, g: (jnp.zeros_like(s), _sel_dot_impl(s, g, TN)))


def _make_mm(dotfn):
    @jax.custom_vjp
    def nn(a, b):
        return dotfn(a, b, NN)

    @jax.custom_vjp
    def nt(a, b):
        return dotfn(a, b, NT)

    @jax.custom_vjp
    def tn(a, b):
        return dotfn(a, b, TN)

    nn.defvjp(lambda a, b: (nn(a, b), (a, b)), lambda r, g: (nt(g, r[1]), tn(r[0], g)))
    nt.defvjp(lambda a, b: (nt(a, b), (a, b)), lambda r, g: (nn(g, r[1]), tn(g, r[0])))
    tn.defvjp(lambda a, b: (tn(a, b), (a, b)), lambda r, g: (nt(r[1], g), nn(r[0], g)))
    return nn, nt, tn


mm_nn, mm_nt, mm_tn = _make_mm(_dot)
mm3_nn, mm3_nt, mm3_tn = _make_mm(_dot3)


def _sigmoid(x):
    return jax.nn.sigmoid(x)


def _silu(x):
    return x * _sigmoid(x)


def _softplus_parts(x):
    sp = jnp.log1p(jnp.exp(-jnp.abs(x)))
    return jnp.maximum(x, 0.0) + sp, jnp.maximum(-x, 0.0) + sp


def _rmsnorm(x, g):
    return x * lax.rsqrt(jnp.mean(x * x, axis=-1, keepdims=True) + NORM_EPS) * g


def _iota2(shape, dim):
    return lax.broadcasted_iota(jnp.int32, shape, dim)


def _div64(i):
    return lax.shift_right_logical(i, jnp.full(i.shape, 6, jnp.int32))


def _each(f, *lists):
    return [f(*a) for a in zip(*lists)]


@jax.custom_vjp
def _inv_unit_lower(ms):
    n = ms[0].shape[0]
    eye = (_iota2((n, n), 0) == _iota2((n, n), 1)).astype(F32)
    rs = [eye - m for m in ms]
    ps = ms
    for _ in range(5):
        ps = _each(mm3_nn, ps, ps)
        rs = _each(lambda r, p: r + mm_nn(r, p), rs, ps)
    return rs


def _inv_fwd(ms):
    rs = _inv_unit_lower(ms)
    return rs, rs


def _inv_bwd(rs, gs):
    ts = _each(mm_tn, rs, gs)
    return (_each(lambda t, r: -mm_nt(t, r), ts, rs),)


_inv_unit_lower.defvjp(_inv_fwd, _inv_bwd)


def _dn_block(cq, ck, cv, bcol, acol, zt, alog, dtb, gn, s0):
    n = SUPER
    h = DN_CHUNK
    row = _iota2((n, n), 0)
    col = _iota2((n, n), 1)
    same = _div64(row) == _div64(col)
    incl = jnp.logical_and(same, row >= col)
    strict = jnp.logical_and(same, row > col)
    incl_f = incl.astype(F32)

    qn = _each(lambda x: x * lax.rsqrt(jnp.sum(x * x, axis=-1, keepdims=True) + NORM_EPS) * (D_HEAD ** -0.5), cq)
    kn = _each(lambda x: x * lax.rsqrt(jnp.sum(x * x, axis=-1, keepdims=True) + NORM_EPS), ck)
    beta = _each(_sigmoid, bcol)
    g = _each(lambda al, ac, dt: -(jnp.exp(al) * _softplus_parts(ac + dt)[0]), alog, acol, dtb)
    gcum = _each(lambda x: _sel_dot(incl_f, jnp.broadcast_to(x, (n, n))), g)
    gam_incl = _each(lambda x: jnp.where(incl, jnp.exp(jnp.where(incl, x - x.T, 0.0)), 0.0), gcum)
    kk = _each(mm_nt, kn, kn)
    t_inv = _inv_unit_lower(_each(lambda b, x, gm: b * x * jnp.where(strict, gm, 0.0), beta, kk, gam_incl))
    eg = _each(jnp.exp, gcum)
    u = _each(lambda t, v, b: mm_nn(t, v * b), t_inv, cv, beta)
    w = _each(lambda t, k, b, e: mm_nn(t, k * (b * e)), t_inv, kn, beta, eg)
    a_intra = _each(lambda q, k, gm: mm_nt(q, k) * gm, qn, kn, gam_incl)
    q_dec = _each(lambda q, e: q * e, qn, eg)
    last0 = _each(lambda x: x[h - 1:h, :], gcum)
    last1 = _each(lambda x: x[n - 1:n, :], gcum)
    k_dec = _each(lambda k, x, l0, l1: k * jnp.exp(jnp.concatenate(
        [jnp.broadcast_to(l0, (h, n)), jnp.broadcast_to(l1, (h, n))], axis=0) - x), kn, gcum, last0, last1)
    v0 = _each(lambda uu, ww, s: uu[:h] - mm_nn(ww[:h], s), u, w, s0)
    o0 = _each(lambda q, s: mm_nn(q[:h], s), q_dec, s0)
    s1 = _each(lambda s, l0, k, v: s * jnp.exp(l0) + mm_tn(k[:h], v), s0, last0, k_dec, v0)
    v1 = _each(lambda uu, ww, s: uu[h:] - mm_nn(ww[h:], s), u, w, s1)
    o1 = _each(lambda q, s: mm_nn(q[h:], s), q_dec, s1)
    s2 = _each(lambda s, l1, k, v: s * jnp.exp(l1) + mm_tn(k[h:], v), s1, last1, k_dec, v1)
    o = _each(lambda a, b, am, x, y: jnp.concatenate([a, b], axis=0) + mm_nn(am, jnp.concatenate([x, y], axis=0)),
              o0, o1, a_intra, v0, v1)
    out = _each(lambda x, z: _rmsnorm(x, gn) * _silu(z), o, zt)
    return out, s2


def _mem_fn(mq, mz, mkv):
    mk = mkv[:, :MEM_W]
    mv = mkv[:, MEM_W:]
    lane = _iota2((1, MEM_W), 1)
    hm = [(_div64(lane) == hd).astype(F32) for hd in range(MEM_HEADS)]
    s = _each(lambda m: mm_nt(mq * m, mk) * (1.0 / math.sqrt(MEM_DH)), hm)
    s = _each(lambda x: x - jnp.max(x, axis=-1, keepdims=True), s)
    e = _each(jnp.exp, s)
    p = _each(lambda x: x / jnp.sum(x, axis=-1, keepdims=True), e)
    o = _each(lambda x, m: mm_nn(x, mv) * m, p, hm)
    out = o[0]
    for x in o[1:]:
        out = out + x
    return out * _silu(mz)


def _loss_fn(x, mo, fg, tgt):
    y = _rmsnorm(x + mo, fg)
    err = y - tgt
    return 0.5 * jnp.sum(jnp.mean(err * err, axis=-1, keepdims=True), axis=0, keepdims=True)


def _matmul_tn(a, b, out_dtype, tm, tn, tk, name):
    kdim, m = a.shape
    n = b.shape[1]
    tm, tn, tk = min(tm, m), min(tn, n), min(tk, kdim)
    assert m % tm == 0 and n % tn == 0 and kdim % tk == 0
    nk = kdim // tk

    def body(a_ref, b_ref, o_ref, acc_ref):
        k = pl.program_id(2)
        part = _dot(a_ref[...], b_ref[...], TN)

        @pl.when(k == 0)
        def _():
            acc_ref[...] = part

        @pl.when(k > 0)
        def _():
            acc_ref[...] += part

        @pl.when(k == nk - 1)
        def _():
            o_ref[...] = acc_ref[...].astype(o_ref.dtype)

    return pl.pallas_call(
        body,
        name=name,
        grid=(m // tm, n // tn, nk),
        in_specs=[pl.BlockSpec((tk, tm), lambda i, j, k: (k, i)), pl.BlockSpec((tk, tn), lambda i, j, k: (k, j))],
        out_specs=pl.BlockSpec((tm, tn), lambda i, j, k: (i, j)),
        out_shape=jax.ShapeDtypeStruct((m, n), out_dtype),
        scratch_shapes=[pltpu.VMEM((tm, tn), F32)],
        compiler_params=pltpu.CompilerParams(dimension_semantics=("parallel", "parallel", "arbitrary")),
    )(a, b)


def _norm_in(x, g, tm=256):
    t = x.shape[0]

    def body(x_ref, g_ref, h_ref):
        h_ref[...] = _rmsnorm(x_ref[...], g_ref[...]).astype(BF16)

    return pl.pallas_call(
        body,
        name="norm_in",
        grid=(t // tm,),
        in_specs=[pl.BlockSpec((tm, D_MODEL), lambda i: (i, 0)), pl.BlockSpec((1, D_MODEL), lambda i: (0, 0))],
        out_specs=pl.BlockSpec((tm, D_MODEL), lambda i: (i, 0)),
        out_shape=jax.ShapeDtypeStruct((t, D_MODEL), BF16),
    )(x, g)


def _dw_alt(parts, h, tm=512):
    t = h.shape[0]
    n_p = len(parts)
    widths = [p.shape[1] for p in parts]
    offs = [sum(widths[:s]) for s in range(n_p)]
    total = sum(widths)
    n_tiles = pl.cdiv(total, tm)

    specs = []
    for off, w in zip(offs, widths):
        if w >= tm:
            assert w % tm == 0 and off % tm == 0
            specs.append(pl.BlockSpec(
                (t, tm), lambda i, lo=off // tm, n=w // tm: (0, jnp.minimum(jnp.maximum(i - lo, 0), n - 1))))
        else:
            assert off // tm == (off + w - 1) // tm
            specs.append(pl.BlockSpec((t, w), lambda i: (0, 0), pipeline_mode=pl.Buffered(1)))

    def body(*refs):
        a_refs, h_ref, o_ref = refs[:n_p], refs[n_p], refs[n_p + 1]
        i = pl.program_id(0)
        for a_ref, off, w in zip(a_refs, offs, widths):
            if w >= tm:
                @pl.when(jnp.logical_and(i >= off // tm, i < (off + w) // tm))
                def _(a_ref=a_ref):
                    o_ref[...] = _dot(a_ref[...], h_ref[...], TN).astype(o_ref.dtype)
            else:
                @pl.when(i == off // tm)
                def _(a_ref=a_ref, r0=off % tm, w=w):
                    o_ref[r0:r0 + w, :] = _dot(a_ref[...], h_ref[...], TN).astype(o_ref.dtype)
        if total % tm:
            @pl.when(i == n_tiles - 1)
            def _():
                o_ref[total % tm:, :] = jnp.zeros((tm - total % tm, D_MODEL), o_ref.dtype)

    return pl.pallas_call(
        body,
        name="dw_alt",
        grid=(n_tiles,),
        in_specs=specs + [pl.BlockSpec((t, D_MODEL), lambda i: (0, 0), pipeline_mode=pl.Buffered(1))],
        out_specs=pl.BlockSpec((tm, D_MODEL), lambda i: (i, 0)),
        out_shape=jax.ShapeDtypeStruct((n_tiles * tm, D_MODEL), BF16),
    )(*parts, h)


def _grad_x(parts, w_alt, x, g, dres, tm=256):
    t = x.shape[0]
    tm = min(tm, t)
    n_p = len(parts)
    assert sum(p.shape[1] for p in parts) == w_alt.shape[0] and t % tm == 0

    def body(*refs):
        a_refs = refs[:n_p]
        w_ref, x_ref, g_ref, dres_ref, dx_ref, dg_ref = refs[n_p:]
        dproj = jnp.concatenate([a_ref[...] for a_ref in a_refs], axis=1)
        dh = _dot(dproj, w_ref[...], NN)

        @pl.when(pl.program_id(0) == 0)
        def _():
            dg_ref[...] = jnp.zeros_like(dg_ref)

        _, vjp = jax.vjp(_rmsnorm, x_ref[...], g_ref[...])
        dx, dg = vjp(dh)
        dx_ref[...] = dx + dres_ref[...]
        dg_ref[...] += dg

    row = pl.BlockSpec((tm, D_MODEL), lambda i: (i, 0))
    vec = pl.BlockSpec((1, D_MODEL), lambda i: (0, 0))
    return pl.pallas_call(
        body,
        name="grad_x",
        grid=(t // tm,),
        in_specs=[pl.BlockSpec((tm, p.shape[1]), lambda i: (i, 0)) for p in parts]
        + [pl.BlockSpec(w_alt.shape, lambda i: (0, 0), pipeline_mode=pl.Buffered(1)), row, vec, row],
        out_specs=[row, vec],
        out_shape=[jax.ShapeDtypeStruct((t, D_MODEL), F32), jax.ShapeDtypeStruct((1, D_MODEL), F32)],
    )(*parts, w_alt, x, g, dres)


def _block_tail(proj, o_dn, o_sb, o_m, x, tgt, w_br_dn, w_br_sb, w_br_mem, w_out, fg, tm=256):
    t = x.shape[0]
    tm = min(tm, t)
    gw = 512
    n_g = 3 * D_MODEL // gw

    def body(*refs):
        g_refs = refs[:n_g]
        (odn_ref, osb_ref, om_ref, x_ref, t_ref, wdn_ref, wsb_ref, wm_ref, wo_ref, fg_ref, loss_ref, dout_ref, dfg_ref,
         mg_ref, dyd_ref, dys_ref, dym_ref, dg_ref, dod_ref, dos_ref, dom_ref) = refs[n_g:]
        y = [_dot(odn_ref[...], wdn_ref[...], NN), _dot(osb_ref[...], wsb_ref[...], NN),
             _dot(om_ref[...], wm_ref[...], NN)]
        s = [_sigmoid(jnp.concatenate([g_refs[2 * k][...], g_refs[2 * k + 1][...]], axis=1)) for k in range(3)]
        merged16 = (s[0] * y[0] + s[1] * y[1] + s[2] * y[2]).astype(BF16)
        mg_ref[...] = merged16
        mo = _dot(merged16, wo_ref[...], NN)
        loss, vjp = jax.vjp(_loss_fn, x_ref[...], mo, fg_ref[...], t_ref[...])
        _, dout, dfg, _ = vjp(jnp.ones((1, 1), F32))

        @pl.when(pl.program_id(0) == 0)
        def _():
            loss_ref[...] = jnp.zeros_like(loss_ref)
            dfg_ref[...] = jnp.zeros_like(dfg_ref)

        loss_ref[...] += jnp.broadcast_to(loss, loss_ref.shape)
        dfg_ref[...] += dfg
        dout_ref[...] = dout
        dmerged = _dot(dout, wo_ref[...], NT)
        dy = [(sk * dmerged).astype(BF16) for sk in s]
        dyd_ref[...], dys_ref[...], dym_ref[...] = dy
        dg_ref[...] = jnp.concatenate([dmerged * yk * (sk * (1.0 - sk)) for yk, sk in zip(y, s)], axis=1).astype(BF16)
        dod_ref[...] = _dot(dy[0], wdn_ref[...], NT).astype(BF16)
        dos_ref[...] = _dot(dy[1], wsb_ref[...], NT).astype(BF16)
        dom_ref[...] = _dot(dy[2], wm_ref[...], NT).astype(BF16)

    gates = [pl.BlockSpec((tm, gw), lambda i, j=j: (i, O_GATES // gw + j)) for j in range(n_g)]
    row = pl.BlockSpec((tm, D_MODEL), lambda i: (i, 0))
    rowm = pl.BlockSpec((tm, MEM_W), lambda i: (i, 0))
    vec = pl.BlockSpec((1, D_MODEL), lambda i: (0, 0))

    def whole(a):
        return pl.BlockSpec(a.shape, lambda i: (0, 0), pipeline_mode=pl.Buffered(1))

    def bf(c):
        return jax.ShapeDtypeStruct((t, c), BF16)

    return pl.pallas_call(
        body,
        name="block_tail",
        grid=(t // tm,),
        in_specs=gates + [row, row, rowm, row, row, whole(w_br_dn), whole(w_br_sb), whole(w_br_mem), whole(w_out), vec],
        out_specs=[pl.BlockSpec((1, LANE), lambda i: (0, 0)), row, vec, row, row, row, row,
                   pl.BlockSpec((tm, 3 * D_MODEL), lambda i: (i, 0)), row, row, rowm],
        out_shape=[jax.ShapeDtypeStruct((1, LANE), F32), jax.ShapeDtypeStruct((t, D_MODEL), F32),
                   jax.ShapeDtypeStruct((1, D_MODEL), F32), bf(D_MODEL), bf(D_MODEL), bf(D_MODEL), bf(D_MODEL),
                   bf(3 * D_MODEL), bf(D_MODEL), bf(D_MODEL), bf(MEM_W)],
    )(*([proj] * n_g), o_dn, o_sb, o_m, x, tgt, w_br_dn, w_br_sb, w_br_mem, w_out, fg)


def _shift_rows(x, s):
    t = x.shape[0]
    if s == 0:
        return x
    rolled = pltpu.roll(x, s % t, 0)
    row = _iota2(x.shape, 0)
    keep = row >= s if s > 0 else row < t + s
    return jnp.where(keep, rolled, 0.0)


def _conv_pre(x, w):
    return sum(_shift_rows(x, CONV_K - 1 - j) * w[j:j + 1, :] for j in range(CONV_K))


CONV_TC = 256


def _dn_conv(proj, conv_w):
    t = proj.shape[0]
    nb = 3 * D_MODEL // CONV_TC

    def body(x_ref, w_ref, c_ref):
        c_ref[...] = _silu(_conv_pre(x_ref[...], w_ref[...]))

    return pl.pallas_call(
        body,
        name="dn_conv",
        grid=(nb,),
        in_specs=[pl.BlockSpec((t, CONV_TC), lambda j: (0, j)), pl.BlockSpec((CONV_K, CONV_TC), lambda j: (0, j))],
        out_specs=pl.BlockSpec((t, CONV_TC), lambda j: (0, j)),
        out_shape=jax.ShapeDtypeStruct((t, 3 * D_MODEL), F32),
    )(proj, conv_w)


def _dn_conv_bwd(proj, conv_w, dc):
    t = proj.shape[0]
    nb = 3 * D_MODEL // CONV_TC

    def body(x_ref, w_ref, dc_ref, dx_ref, dw_ref):
        x = x_ref[...]
        w = w_ref[...]
        pre = _conv_pre(x, w)
        sg = _sigmoid(pre)
        dpre = dc_ref[...] * (sg * (1.0 + pre * (1.0 - sg)))
        ahead = [_shift_rows(dpre, -(CONV_K - 1 - j)) for j in range(CONV_K)]
        dx_ref[...] = sum(a * w[j:j + 1, :] for j, a in enumerate(ahead)).astype(BF16)
        dw_ref[...] = jnp.concatenate([jnp.sum(a * x, axis=0, keepdims=True) for a in ahead], axis=0)

    blk = pl.BlockSpec((t, CONV_TC), lambda j: (0, j))
    wblk = pl.BlockSpec((CONV_K, CONV_TC), lambda j: (0, j))
    return pl.pallas_call(
        body,
        name="dn_conv_bwd",
        grid=(nb,),
        in_specs=[blk, wblk, blk],
        out_specs=[blk, wblk],
        out_shape=[jax.ShapeDtypeStruct((t, 3 * D_MODEL), BF16), jax.ShapeDtypeStruct((CONV_K, 3 * D_MODEL), F32)],
    )(proj, conv_w, dc)


def _ba_columns(ba, hd):
    lane = _iota2(ba.shape, 1)
    bcol = jnp.sum(jnp.where(lane == hd, ba, 0.0), axis=1, keepdims=True)
    acol = jnp.sum(jnp.where(lane == N_HEADS + hd, ba, 0.0), axis=1, keepdims=True)
    return bcol, acol


def _head_scalar(row, hd):
    lane = _iota2(row.shape, 1)
    return jnp.sum(jnp.where(lane == hd, row, 0.0), axis=1, keepdims=True)


DN_HP = 8


def _dn_inputs(cq, ck, cv, ba_ref, z_ref, alog_ref, dtb_ref, heads, lanes):
    ba = ba_ref[...]
    cols = [_ba_columns(ba, hd) for hd in heads]
    return ([cq[:, ln] for ln in lanes], [ck[:, ln] for ln in lanes], [cv[:, ln] for ln in lanes],
            [c[0] for c in cols], [c[1] for c in cols], [z_ref[:, ln] for ln in lanes],
            [_head_scalar(alog_ref[...], hd) for hd in heads], [_head_scalar(dtb_ref[...], hd) for hd in heads])


def _dn_specs(nblk, reverse):
    w = DN_HP * LANE
    nq = D_MODEL // w

    def row(i):
        return nblk - 1 - i if reverse else i

    def colblk(b0):
        return pl.BlockSpec((SUPER, w), lambda i, h: (row(i), b0 + h))

    ba = pl.BlockSpec((SUPER, LANE), lambda i, h: (row(i), O_BA // LANE))
    vec = pl.BlockSpec((1, LANE), lambda i, h: (0, 0))
    st = pl.BlockSpec((1, DN_HP, D_HEAD, D_HEAD), lambda i, h: (row(i), h, 0, 0))
    return colblk, nq, ba, vec, st


def _dn_fwd(c, proj, alog_row, dtb_row, gn):
    t = c.shape[0]
    nblk = t // SUPER
    colblk, nq, ba, vec, st = _dn_specs(nblk, False)

    def body(cq, ck, cv, ba_ref, z_ref, alog_ref, dtb_ref, gn_ref, o_ref, s_ref, state):
        @pl.when(jnp.logical_and(pl.program_id(0) == 0, pl.program_id(1) == 0))
        def _():
            state[...] = jnp.zeros_like(state)

        heads = [pl.program_id(1) * DN_HP + j for j in range(DN_HP)]
        lanes = [slice(j * LANE, (j + 1) * LANE) for j in range(DN_HP)]
        s0 = [state[hd] for hd in heads]
        outs, s2 = _dn_block(*_dn_inputs(cq, ck, cv, ba_ref, z_ref, alog_ref, dtb_ref, heads, lanes), gn_ref[...], s0)
        for j, (hd, ln) in enumerate(zip(heads, lanes)):
            s_ref[0, j] = s0[j]
            o_ref[:, ln] = outs[j].astype(BF16)
            state[hd] = s2[j]

    return pl.pallas_call(
        body,
        name="dn_fwd",
        grid=(nblk, N_HEADS // DN_HP),
        in_specs=[colblk(0), colblk(nq), colblk(2 * nq), ba, colblk(O_Z_DN // (DN_HP * LANE)), vec, vec, vec],
        out_specs=[colblk(0), st],
        out_shape=[jax.ShapeDtypeStruct((t, D_MODEL), BF16),
                   jax.ShapeDtypeStruct((nblk, N_HEADS, D_HEAD, D_HEAD), F32)],
        scratch_shapes=[pltpu.VMEM((N_HEADS, D_HEAD, D_HEAD), F32)],
    )(c, c, c, proj, proj, alog_row, dtb_row, gn)


def _dn_bwd(c, proj, alog_row, dtb_row, gn, states, do):
    t = c.shape[0]
    nblk = t // SUPER
    colblk, nq, ba, vec, st = _dn_specs(nblk, True)
    assert nq == 1

    def body(cq, ck, cv, ba_ref, z_ref, alog_ref, dtb_ref, gn_ref, s_ref, do_ref,
             dc_ref, dz_ref, dba_ref, dsc_ref, dgn_ref, dstate):
        i = pl.program_id(0)
        hq = pl.program_id(1)

        @pl.when(jnp.logical_and(i == 0, hq == 0))
        def _():
            dstate[...] = jnp.zeros_like(dstate)
            dsc_ref[...] = jnp.zeros_like(dsc_ref)
            dgn_ref[...] = jnp.zeros_like(dgn_ref)

        @pl.when(hq == 0)
        def _():
            dba_ref[...] = jnp.zeros_like(dba_ref)

        lane = _iota2((SUPER, LANE), 1)
        lane1 = _iota2((1, LANE), 1)
        heads = [hq * DN_HP + j for j in range(DN_HP)]
        lanes = [slice(j * LANE, (j + 1) * LANE) for j in range(DN_HP)]
        ds_in = [dstate[hd] for hd in heads]
        s_in = [s_ref[0, j] for j in range(DN_HP)]
        _, vjp = jax.vjp(_dn_block, *_dn_inputs(cq, ck, cv, ba_ref, z_ref, alog_ref, dtb_ref, heads, lanes),
                         gn_ref[...], s_in)
        dq, dk, dv, dbc, dac, dz, dal, ddt, dgn, ds0 = vjp(([do_ref[:, ln].astype(F32) for ln in lanes], ds_in))
        dba = jnp.zeros((SUPER, LANE), F32)
        dal_row = jnp.zeros((1, LANE), F32)
        ddt_row = jnp.zeros((1, LANE), F32)
        for j, (hd, ln) in enumerate(zip(heads, lanes)):
            for part, d in enumerate((dq, dk, dv)):
                dc_ref[:, part * D_MODEL + j * LANE:part * D_MODEL + (j + 1) * LANE] = d[j]
            dz_ref[:, ln] = dz[j].astype(BF16)
            dstate[hd] = ds0[j]
            dba = dba + jnp.where(lane == hd, dbc[j], 0.0) + jnp.where(lane == N_HEADS + hd, dac[j], 0.0)
            dal_row = dal_row + jnp.where(lane1 == hd, dal[j], 0.0)
            ddt_row = ddt_row + jnp.where(lane1 == hd, ddt[j], 0.0)
        dba_ref[...] += dba
        dsc_ref[0:1, :] += dal_row
        dsc_ref[1:2, :] += ddt_row
        dgn_ref[...] += dgn

    outs = pl.pallas_call(
        body,
        name="dn_bwd",
        grid=(nblk, N_HEADS // DN_HP),
        in_specs=[colblk(0), colblk(nq), colblk(2 * nq), ba, colblk(O_Z_DN // (DN_HP * LANE)), vec, vec, vec, st,
                  colblk(0)],
        out_specs=[pl.BlockSpec((SUPER, 3 * D_MODEL), lambda i, h: (nblk - 1 - i, 0)), colblk(0),
                   pl.BlockSpec((SUPER, LANE), lambda i, h: (nblk - 1 - i, 0)),
                   pl.BlockSpec((2, LANE), lambda i, h: (0, 0)), vec],
        out_shape=[jax.ShapeDtypeStruct((t, 3 * D_MODEL), F32), jax.ShapeDtypeStruct((t, D_MODEL), BF16),
                   jax.ShapeDtypeStruct((t, LANE), F32), jax.ShapeDtypeStruct((2, LANE), F32),
                   jax.ShapeDtypeStruct((1, LANE), F32)],
        scratch_shapes=[pltpu.VMEM((N_HEADS, D_HEAD, D_HEAD), F32)],
    )(c, c, c, proj, proj, alog_row, dtb_row, gn, states, do)
    return outs


SB_TQ = 256
SB_TK = 256
SB_HP_FWD = 8
SB_HP_BWD = 4


def _sb_logits(z, mask):
    sp = jnp.log(1.0 + jnp.exp(-jnp.abs(z)))
    lf_raw = -(jnp.maximum(z, 0.0) + sp)
    lb = lf_raw + z
    lf = lf_raw if mask is None else jnp.where(mask, lf_raw, 0.0)
    return lb, lf_raw, lf


def _suffix_sums(x, sel):
    hi, lo = _split2(x)
    d = functools.partial(lax.dot_general, dimension_numbers=NN, preferred_element_type=F32)
    return d(hi, sel) + d(lo, sel)


def _sb_diag_mask(tq, r):
    return r * SB_TK + _iota2((tq, SB_TK), 1) < _iota2((tq, SB_TK), 0)


def _sb_specs(t, tq, hp):
    w = hp * LANE
    q0, k0, v0, z0 = (O_QKV_SB // w, (O_QKV_SB + D_MODEL) // w, (O_QKV_SB + 2 * D_MODEL) // w, O_Z_SB // w)

    def blk(b0):
        return pl.BlockSpec((tq, w), lambda h, i: (i, b0 + h))

    def full(b0, **kw):
        return pl.BlockSpec((t, w), lambda h, i: (0, b0 + h), **kw)

    once = dict(pipeline_mode=pl.Buffered(1))
    return blk(q0), full(k0, **once), full(v0, **once), blk(z0), blk(0), full(0)


def _sb_fwd(proj, shards):
    t = proj.shape[0]
    tq = min(SB_TQ, t)
    ndiag = tq // SB_TK
    scale = 1.0 / math.sqrt(D_HEAD)
    na = len(shards)

    def body(q_ref, k_ref, v_ref, z_ref, *rest):
        x_refs, (o_ref, oraw_ref), land = rest[:na], rest[na:na + 2], rest[na + 2:2 * na + 2]
        sems = rest[2 * na + 2:]
        qi = pl.program_id(1)
        first = jnp.logical_and(pl.program_id(0) == 0, qi == 0)
        last = jnp.logical_and(pl.program_id(0) == pl.num_programs(0) - 1, qi == pl.num_programs(1) - 1)

        @pl.when(first)
        def _():
            for cp in _direct_gather_copies(x_refs, land, *sems):
                cp.start()

        lanes = [slice(hd * LANE, (hd + 1) * LANE) for hd in range(SB_HP_FWD)]
        qs = [(q_ref[:, ln] * scale).astype(BF16) for ln in lanes]
        after = (_iota2((SB_TK, SB_TK), 0) > _iota2((SB_TK, SB_TK), 1)).astype(BF16)
        oraw_ref[...] = jnp.zeros_like(oraw_ref)

        def block(kb, mask, c_lf):
            rows = pl.ds(pl.multiple_of(kb * SB_TK, SB_TK), SB_TK)
            z = _each(lambda q, ln: _dot(q, k_ref[rows, ln], NT), qs, lanes)
            lg = _each(lambda x: _sb_logits(x, mask), z)
            surv = _each(lambda x: _suffix_sums(x[2], after), lg)
            att = _each(lambda x, s, c: jnp.exp(x[0] + s + c), lg, surv, c_lf)
            if mask is not None:
                att = _each(lambda a: jnp.where(mask, a, 0.0), att)
            pv = _each(lambda a, ln: _dot(a, v_ref[rows, ln], NN), att, lanes)
            for p, ln in zip(pv, lanes):
                oraw_ref[:, ln] += p
            return tuple(_each(lambda c, x: c + jnp.sum(x[2], axis=1, keepdims=True), c_lf, lg))

        carry = tuple(jnp.zeros((tq, 1), F32) for _ in range(SB_HP_FWD))
        for r in reversed(range(ndiag)):
            carry = block(qi * ndiag + r, _sb_diag_mask(tq, r), carry)
        lax.fori_loop(0, qi * ndiag, lambda i, c: block(qi * ndiag - 1 - i, None, c), carry)
        o_ref[...] = (oraw_ref[...] * _silu(z_ref[...])).astype(BF16)

        @pl.when(last)
        def _():
            for cp in _direct_gather_copies(x_refs, land, *sems):
                cp.wait()

    q_spec, k_spec, v_spec, z_spec, out, _ = _sb_specs(t, tq, SB_HP_FWD)
    outs = pl.pallas_call(
        body,
        name="sb_fwd",
        grid=(N_HEADS // SB_HP_FWD, t // tq),
        in_specs=[q_spec, k_spec, v_spec, z_spec] + [ANY] * na,
        out_specs=[out, out] + [ANY] * na,
        out_shape=[jax.ShapeDtypeStruct((t, D_MODEL), BF16), jax.ShapeDtypeStruct((t, D_MODEL), F32)]
        + [jax.ShapeDtypeStruct((N_DEV, *v.shape), v.dtype) for v in shards],
        scratch_shapes=_gather_sems(na),
    )(proj, proj, proj, proj, *shards)
    return outs[0], outs[1], outs[2:]


def _sb_bwd(proj, oraw, do, blocks):
    t = proj.shape[0]
    tq = min(SB_TQ, t)
    ndiag = tq // SB_TK
    scale = 1.0 / math.sqrt(D_HEAD)
    nb = len(blocks)

    def body(q_ref, k_ref, v_ref, z_ref, oraw_ref, do_ref, *rest):
        blk_refs, (dq_ref, dk_ref, dv_ref, dz_ref), land_refs = rest[:nb], rest[nb:nb + 4], rest[nb + 4:2 * nb + 4]
        dk_acc, dv_acc, p_scr, z_scr, send_sems, recv_sems, local_sems = rest[2 * nb + 4:]
        qi = pl.program_id(1)
        nq = pl.num_programs(1)
        hg = pl.program_id(0)
        me = _position()
        mine = 4 * me[0] + 2 * me[1] + me[2]

        def exchange():
            cps = []
            for a, (blk_ref, land_ref) in enumerate(zip(blk_refs, land_refs)):
                cps.append(pltpu.make_async_copy(blk_ref.at[mine], land_ref.at[mine], local_sems.at[a]))
                for k, peer in enumerate(_other_devices(me)):
                    cps.append(pltpu.make_async_remote_copy(
                        src_ref=blk_ref.at[4 * peer[0] + 2 * peer[1] + peer[2]], dst_ref=land_ref.at[mine],
                        send_sem=send_sems.at[7 * a + k], recv_sem=recv_sems.at[7 * a + k], device_id=peer,
                        device_id_type=MESH))
            return cps

        @pl.when(jnp.logical_and(hg == 0, qi == 0))
        def _():
            for cp in exchange():
                cp.start()

        @pl.when(qi == 0)
        def _():
            dk_acc[...] = jnp.zeros_like(dk_acc)
            dv_acc[...] = jnp.zeros_like(dv_acc)

        heads = range(SB_HP_BWD)
        lanes = [slice(hd * LANE, (hd + 1) * LANE) for hd in heads]
        zg = z_ref[...]
        sg = _sigmoid(zg)
        dog = do_ref[...].astype(F32)
        dz_ref[...] = (dog * oraw_ref[...] * (sg * (1.0 + zg * (1.0 - sg)))).astype(BF16)
        d_o = (dog * (zg * sg)).astype(BF16)
        d_o16 = [d_o[:, ln] for ln in lanes]
        qs = [(q_ref[:, ln] * scale).astype(BF16) for ln in lanes]
        ri = _iota2((SB_TK, SB_TK), 0)
        ci = _iota2((SB_TK, SB_TK), 1)
        after = (ri > ci).astype(BF16)
        earlier = (ri < ci).astype(BF16)

        def rows_of(kb):
            return pl.ds(pl.multiple_of(kb * SB_TK, SB_TK), SB_TK)

        def down(kb, mask, c_lf):
            rows = rows_of(kb)
            z = _each(lambda q, ln: _dot(q, k_ref[rows, ln], NT), qs, lanes)
            da = _each(lambda d, ln: _dot(d, v_ref[rows, ln], NT), d_o16, lanes)
            lg = _each(lambda x: _sb_logits(x, mask), z)
            surv = _each(lambda x: _suffix_sums(x[2], after), lg)
            att = _each(lambda x, s, c: jnp.exp(x[0] + s + c), lg, surv, c_lf)
            if mask is not None:
                att = _each(lambda a: jnp.where(mask, a, 0.0), att)
            dv = _each(lambda a, d: _dot(a, d, TN), att, d_o16)
            for hd in heads:
                p_scr[hd, kb] = att[hd] * da[hd]
                z_scr[hd, kb] = z[hd]
                dv_acc[rows, lanes[hd]] += dv[hd]
            return tuple(_each(lambda c, x: c + jnp.sum(x[2], axis=1, keepdims=True), c_lf, lg))

        c_lf = tuple(jnp.zeros((tq, 1), F32) for _ in heads)
        for r in reversed(range(ndiag)):
            c_lf = down(qi * ndiag + r, _sb_diag_mask(tq, r), c_lf)
        lax.fori_loop(0, qi * ndiag, lambda i, c: down(qi * ndiag - 1 - i, None, c), c_lf)

        def up(kb, mask, carry):
            dq, c_p = carry
            rows = rows_of(kb)
            p = [p_scr[hd, kb] for hd in heads]
            zs = [z_scr[hd, kb] for hd in heads]
            before = _each(lambda x, c: _suffix_sums(x, earlier) + c, p, c_p)
            e = _each(lambda x: jnp.exp(-jnp.abs(x)), zs)
            r = _each(lambda x: 1.0 / (1.0 + x), e)
            sig = _each(lambda x, a, b: jnp.where(x >= 0.0, b, a * b), zs, e, r)
            oms = _each(lambda x, a, b: jnp.where(x >= 0.0, a * b, b), zs, e, r)
            if mask is not None:
                sig = _each(lambda a: jnp.where(mask, a, 0.0), sig)
            dzz = _each(lambda x, o, g, b: x * o - g * b, p, oms, sig, before)
            dk = _each(lambda x, q: _dot(x, q, TN), dzz, qs)
            dq = _each(lambda a, x, ln: a + _dot(x, k_ref[rows, ln], NN), dq, dzz, lanes)
            for hd in heads:
                dk_acc[rows, lanes[hd]] += dk[hd]
            return tuple(dq), tuple(_each(lambda c, x: c + jnp.sum(x, axis=1, keepdims=True), c_p, p))

        carry = (tuple(jnp.zeros((tq, D_HEAD), F32) for _ in heads), tuple(jnp.zeros((tq, 1), F32) for _ in heads))
        carry = lax.fori_loop(0, qi * ndiag, lambda kb, c: up(kb, None, c), carry)
        for r in range(ndiag):
            carry = up(qi * ndiag + r, _sb_diag_mask(tq, r), carry)
        dq = carry[0]
        for hd in heads:
            dq_ref[:, lanes[hd]] = (dq[hd] * scale).astype(BF16)

        @pl.when(qi == nq - 1)
        def _():
            dk_ref[...] = dk_acc[...].astype(BF16)
            dv_ref[...] = dv_acc[...].astype(BF16)

        @pl.when(jnp.logical_and(hg == pl.num_programs(0) - 1, qi == nq - 1))
        def _():
            for cp in exchange():
                cp.wait()

    q_spec, k_spec, v_spec, z_spec, blk, full = _sb_specs(t, tq, SB_HP_BWD)
    o = jax.ShapeDtypeStruct((t, D_MODEL), BF16)
    w = SB_HP_BWD * LANE
    outs = pl.pallas_call(
        body,
        name="sb_bwd",
        grid=(N_HEADS // SB_HP_BWD, t // tq),
        in_specs=[q_spec, k_spec, v_spec, z_spec, blk, blk] + [ANY] * nb,
        out_specs=[blk, full, full, blk] + [ANY] * nb,
        out_shape=[o, o, o, o] + [jax.ShapeDtypeStruct(b.shape, b.dtype) for b in blocks],
        scratch_shapes=[pltpu.VMEM((t, w), F32), pltpu.VMEM((t, w), F32)]
        + [pltpu.VMEM((SB_HP_BWD, t // SB_TK, tq, SB_TK), F32)] * 2 + _gather_sems(nb),
    )(proj, proj, proj, proj, oraw, do, *blocks)
    return outs[0], outs[1], outs[2], outs[3], outs[4:]


def _mem_kv_fn(mem, mg, w):
    return mm_nn(_rmsnorm(mem, mg), w)


def _mem_kv(mem, mg, w):
    def body(m_ref, g_ref, w_ref, o_ref):
        o_ref[...] = _mem_kv_fn(m_ref[...], g_ref[...], w_ref[...])

    return pl.pallas_call(body, name="mem_kv", out_shape=jax.ShapeDtypeStruct((MEM_LEN, 2 * MEM_W), F32))(mem, mg, w)


def _mem_kv_bwd(mem, mg, w, dmkv):
    def body(m_ref, g_ref, w_ref, d_ref, dg_ref, dw_ref):
        _, vjp = jax.vjp(_mem_kv_fn, m_ref[...], g_ref[...], w_ref[...].astype(F32))
        _, dg, dw = vjp(d_ref[...])
        dg_ref[...] = dg
        dw_ref[...] = dw.astype(BF16)

    return pl.pallas_call(
        body, name="mem_kv_bwd",
        out_shape=[jax.ShapeDtypeStruct((1, D_MODEL), F32), jax.ShapeDtypeStruct((D_MODEL, 2 * MEM_W), BF16)],
    )(mem, mg, w, dmkv)


def _mem_attn(proj, mkv, tm=1024):
    t = proj.shape[0]
    tm = min(tm, t)

    def body(q_ref, z_ref, kv_ref, o_ref):
        o_ref[...] = _mem_fn(q_ref[...], z_ref[...], kv_ref[...]).astype(BF16)

    return pl.pallas_call(
        body,
        name="mem_attn",
        grid=(t // tm,),
        in_specs=[pl.BlockSpec((tm, MEM_W), lambda i: (i, O_MQ // MEM_W)),
                  pl.BlockSpec((tm, MEM_W), lambda i: (i, O_MZ // MEM_W)),
                  pl.BlockSpec((MEM_LEN, 2 * MEM_W), lambda i: (0, 0))],
        out_specs=pl.BlockSpec((tm, MEM_W), lambda i: (i, 0)),
        out_shape=jax.ShapeDtypeStruct((t, MEM_W), BF16),
    )(proj, proj, mkv)


def _mem_attn_bwd(proj, mkv, do, tm=1024):
    t = proj.shape[0]
    tm = min(tm, t)

    def body(q_ref, z_ref, kv_ref, do_ref, dq_ref, dz_ref, dkv_ref):
        _, vjp = jax.vjp(_mem_fn, q_ref[...], z_ref[...], kv_ref[...])
        dq, dz, dkv = vjp(do_ref[...].astype(F32))
        dq_ref[...] = dq.astype(BF16)
        dz_ref[...] = dz.astype(BF16)

        @pl.when(pl.program_id(0) == 0)
        def _():
            dkv_ref[...] = jnp.zeros_like(dkv_ref)

        dkv_ref[...] += dkv

    blk = pl.BlockSpec((tm, MEM_W), lambda i: (i, 0))
    kv = pl.BlockSpec((MEM_LEN, 2 * MEM_W), lambda i: (0, 0))
    return pl.pallas_call(
        body,
        name="mem_attn_bwd",
        grid=(t // tm,),
        in_specs=[pl.BlockSpec((tm, MEM_W), lambda i: (i, O_MQ // MEM_W)),
                  pl.BlockSpec((tm, MEM_W), lambda i: (i, O_MZ // MEM_W)), kv, blk],
        out_specs=[blk, blk, kv],
        out_shape=[jax.ShapeDtypeStruct((t, MEM_W), BF16), jax.ShapeDtypeStruct((t, MEM_W), BF16),
                   jax.ShapeDtypeStruct((MEM_LEN, 2 * MEM_W), F32)],
    )(proj, proj, mkv, do)


def _proj_gather(h, w_alt, shards, tm=512, tn=3968):
    t = h.shape[0]
    tm = min(tm, t)
    n, kdim = w_alt.shape
    assert n % tn == 0 and t % tm == 0
    nj, ni = n // tn, t // tm
    na = len(shards)

    def body(h_ref, w_ref, *rest):
        x_refs, o_ref, land = rest[:na], rest[na], rest[na + 1:2 * na + 1]
        send_sems, recv_sems, local_sems = rest[2 * na + 1:]
        j, i = pl.program_id(0), pl.program_id(1)

        def copies():
            return _direct_gather_copies(x_refs, land, send_sems, recv_sems, local_sems)

        @pl.when(jnp.logical_and(j == 0, i == 0))
        def _():
            for cp in copies():
                cp.start()

        o_ref[...] = _dot(h_ref[...], w_ref[...], NT)

        @pl.when(jnp.logical_and(j == nj - 1, i == ni - 1))
        def _():
            for cp in copies():
                cp.wait()

    outs = pl.pallas_call(
        body,
        name="proj",
        grid=(nj, ni),
        in_specs=[pl.BlockSpec((tm, kdim), lambda j, i: (i, 0)), pl.BlockSpec((tn, kdim), lambda j, i: (j, 0))]
        + [ANY] * na,
        out_specs=[pl.BlockSpec((tm, tn), lambda j, i: (i, j))] + [ANY] * na,
        out_shape=[jax.ShapeDtypeStruct((t, n), F32)]
        + [jax.ShapeDtypeStruct((N_DEV, *v.shape), v.dtype) for v in shards],
        scratch_shapes=_gather_sems(na),
    )(h, w_alt, *shards)
    return outs[0], outs[1:]


def _local_step(x, mem, tgt, norm_g, mem_norm_g, w_alt, alog_row, dtb_row, dn_norm_g, final_g, shards):
    h = _norm_in(x, norm_g)
    s_kv, s_dn, s_sb, s_out, s_mem, s_conv = shards
    proj, (g_kv, g_conv) = _proj_gather(h, w_alt, [s_kv, s_conv])
    w_mem_kv = g_kv.reshape(D_MODEL, 2 * MEM_W)
    conv_w = g_conv.transpose(1, 0, 2).reshape(CONV_K, 3 * D_MODEL)

    c = _dn_conv(proj, conv_w)
    o_dn, states = _dn_fwd(c, proj, alog_row, dtb_row, dn_norm_g)
    o_sb, o_sb_raw, (g_dn, g_sb, g_out, g_mem) = _sb_fwd(proj, [s_dn, s_sb, s_out, s_mem])
    w_br_dn = g_dn.reshape(D_MODEL, D_MODEL)
    w_br_sb = g_sb.reshape(D_MODEL, D_MODEL)
    w_out = g_out.reshape(D_MODEL, D_MODEL)
    w_br_mem = g_mem.transpose(1, 0, 2).reshape(MEM_W, D_MODEL)
    mkv = _mem_kv(mem, mem_norm_g, w_mem_kv)
    o_m = _mem_attn(proj, mkv)

    (loss, dout, d_final_g, merged, dy_dn, dy_sb, dy_m, dgates, do_dn, do_sb, do_m) = _block_tail(
        proj, o_dn, o_sb, o_m, x, tgt, w_br_dn, w_br_sb, w_br_mem, w_out, final_g)
    dw_out = _matmul_tn(merged, dout, BF16, 256, 1024, 2048, "dw_out")
    dw_br_dn = _matmul_tn(o_dn, dy_dn, BF16, 256, 1024, 2048, "dw_br_dn")
    dw_br_sb = _matmul_tn(o_sb, dy_sb, BF16, 256, 1024, 2048, "dw_br_sb")
    dw_br_mem = _matmul_tn(o_m, dy_m, BF16, 256, 1024, 2048, "dw_br_mem")

    dmq, dmz, dmkv = _mem_attn_bwd(proj, mkv, do_m)
    d_mem_norm_g, dw_mem_kv = _mem_kv_bwd(mem, mem_norm_g, w_mem_kv, dmkv)
    rows_d = D_MODEL // N_DEV
    small_blocks = [
        dw_br_dn.reshape(N_DEV, rows_d, D_MODEL), dw_br_sb.reshape(N_DEV, rows_d, D_MODEL),
        dw_out.reshape(N_DEV, rows_d, D_MODEL), dw_mem_kv.reshape(N_DEV, rows_d // 2, D_MODEL),
        dw_br_mem.reshape(MEM_W, N_DEV, rows_d).transpose(1, 0, 2).reshape(N_DEV, MEM_W // N_DEV, D_MODEL)]
    dq_sb, dk_sb, dv_sb, dz_sb, small_parts = _sb_bwd(proj, o_sb_raw, do_sb, small_blocks)
    d_small = _sum_slots(list(small_parts), "sum_small_grads")
    dc, dz_dn, dba, dscal, d_dn_norm_g = _dn_bwd(c, proj, alog_row, dtb_row, dn_norm_g, states, do_dn)
    dqkv_dn, d_conv_w = _dn_conv_bwd(proj, conv_w, dc)

    dproj = [dqkv_dn, dz_dn, dq_sb, dk_sb, dv_sb, dz_sb, dmq, dmz, dgates, dba.astype(BF16)]
    dw_alt = _dw_alt(dproj, h)
    grad_x, d_norm_g = _grad_x(dproj, w_alt, x, norm_g, dout)
    return dict(loss=loss, grad_x=grad_x, norm_g=d_norm_g, mem_norm_g=d_mem_norm_g, w_alt=dw_alt, conv_w=d_conv_w,
                scal=dscal, dn_norm_g=d_dn_norm_g, small=d_small, final_g=d_final_g)


MESH = pl.DeviceIdType.MESH
ANY = pl.BlockSpec(memory_space=pl.ANY)


def _position():
    return lax.axis_index("x"), lax.axis_index("y"), lax.axis_index("c")


def _other_devices(me):
    return [tuple(1 - p if (f >> s) & 1 else p for p, s in zip(me, (2, 1, 0))) for f in range(1, N_DEV)]


def _direct_gather_copies(x_refs, land_refs, send_sems, recv_sems, local_sems):
    me = _position()
    mine = 4 * me[0] + 2 * me[1] + me[2]
    cps = []
    for a, (x_ref, land) in enumerate(zip(x_refs, land_refs)):
        cps.append(pltpu.make_async_copy(x_ref, land.at[mine], local_sems.at[a]))
        for k, peer in enumerate(_other_devices(me)):
            cps.append(pltpu.make_async_remote_copy(
                src_ref=x_ref, dst_ref=land.at[mine], send_sem=send_sems.at[7 * a + k],
                recv_sem=recv_sems.at[7 * a + k], device_id=peer, device_id_type=MESH))
    return cps


def _gather_sems(n):
    return [pltpu.SemaphoreType.DMA((7 * n,)), pltpu.SemaphoreType.DMA((7 * n,)), pltpu.SemaphoreType.DMA((n,))]


def _all_gather(xs, name):
    n = len(xs)

    def body(*refs):
        x_refs, o_refs = refs[:n], refs[n:2 * n]
        send_sems, recv_sems, local_sems = refs[2 * n:]
        x, y, c = _position()
        me, sibling = (x, y, c), (x, y, 1 - c)
        x_nbr, y_nbr, diag = (1 - x, y, c), (x, 1 - y, c), (1 - x, 1 - y, c)
        south = c == 0
        relay_from = tuple(jnp.where(south, a, b) for a, b in zip(y_nbr, x_nbr))
        relay_to = tuple(jnp.where(south, a, b) for a, b in zip(x_nbr, y_nbr))

        def slot(p):
            return 4 * p[0] + 2 * p[1] + p[2]

        def copy(a, k, block, to, src=None):
            dst = o_refs[a].at[slot(block)]
            return pltpu.make_async_remote_copy(
                src_ref=dst if src is None else src, dst_ref=dst, send_sem=send_sems.at[7 * a + k],
                recv_sem=recv_sems.at[7 * a + k], device_id=to, device_id_type=MESH)

        mine = [pltpu.make_async_copy(x_refs[a], o_refs[a].at[slot(me)], local_sems.at[a]) for a in range(n)]
        for cp in mine:
            cp.start()
        sends = []
        for a in range(n):
            sends += [copy(a, 0, me, sibling, src=x_refs[a]), copy(a, 1, me, x_nbr, src=x_refs[a]),
                      copy(a, 2, me, y_nbr, src=x_refs[a])]
        for cp in sends:
            cp.start()
        later = []
        for a in range(n):
            copy(a, 1, x_nbr, me).wait_recv()
            copy(a, 2, y_nbr, me).wait_recv()
            later += [copy(a, 3, relay_from, relay_to), copy(a, 4, x_nbr, sibling), copy(a, 5, y_nbr, sibling)]
            for cp in later[-3:]:
                cp.start()
        for a in range(n):
            copy(a, 3, diag, me).wait_recv()
            later.append(copy(a, 6, diag, sibling))
            later[-1].start()
        for a in range(n):
            copy(a, 0, sibling, me).wait_recv()
            for k, chip in ((4, x_nbr), (5, y_nbr), (6, diag)):
                copy(a, k, (chip[0], chip[1], 1 - c), me).wait_recv()
        for cp in sends + later:
            cp.wait_send()
        for cp in mine:
            cp.wait()

    return pl.pallas_call(
        body,
        name=name,
        in_specs=[ANY] * n,
        out_specs=[ANY] * n,
        out_shape=[jax.ShapeDtypeStruct((N_DEV, *v.shape), v.dtype) for v in xs],
        scratch_shapes=[pltpu.SemaphoreType.DMA((7 * n,)), pltpu.SemaphoreType.DMA((7 * n,)),
                        pltpu.SemaphoreType.DMA((n,))],
    )(*xs)


def _window_view(ref, dest):
    return ref.at[pl.ds(WIN_ROW0[dest], WIN_W), :]


def _chunk_rows(rows, cols):
    return max(ch for ch in range(ROW_TILE, rows + 1, ROW_TILE) if rows % ch == 0 and ch * cols <= (1 << 20))


def _halving_stage(xs, axis, name, out_dtype, windowed=(), gather=()):
    n_arr = len(xs)
    metas = []
    for k, v in enumerate(xs):
        if k in windowed:
            metas.append((N_DEV // 2, WIN_W, v.shape[1]))
        else:
            assert v.shape[1] == 2
            metas.append((v.shape[0], v.shape[2], v.shape[3]))
    chunk = [_chunk_rows(r, c) for (_, r, c) in metas]
    offs = [sum(m[0] for m in metas[:k]) for k in range(n_arr)]
    n_sem = sum(m[0] for m in metas)

    n_g = len(gather)

    def body(*refs):
        x_refs, g_refs = refs[:n_arr], refs[n_arr:n_arr + n_g]
        outs = refs[n_arr + n_g:]
        o_refs, land_refs, gl_refs = outs[:n_arr], outs[n_arr:2 * n_arr], outs[2 * n_arr:2 * n_arr + n_g]
        rest = outs[2 * n_arr + n_g:]
        bufs = rest[:3 * n_arr]
        send_sems, recv_sems, in_sems, out_sems = rest[3 * n_arr:3 * n_arr + 4]
        gathers = _direct_gather_copies(g_refs, gl_refs, *rest[3 * n_arr + 4:]) if n_g else []
        for cp in gathers:
            cp.start()
        pos = dict(zip("xyc", _position()))
        bit = pos[axis]
        peer = tuple(1 - pos[a] if a == axis else pos[a] for a in "xyc")

        def view(k, i, b):
            if k in windowed:
                return _window_view(x_refs[k], 2 * i + b)
            return x_refs[k].at[i, b]

        def add_blocks(k, a_view, b_view, o_view):
            _hbm_add(a_view, b_view, o_view, bufs[3 * k:3 * k + 3], in_sems, out_sems, chunk[k])

        for b in (0, 1):
            @pl.when(bit == b)
            def _(b=b):
                sends = []
                for k in range(n_arr):
                    for i in range(metas[k][0]):
                        cp = pltpu.make_async_remote_copy(
                            src_ref=view(k, i, 1 - b), dst_ref=land_refs[k].at[i], send_sem=send_sems.at[offs[k] + i],
                            recv_sem=recv_sems.at[offs[k] + i], device_id=peer, device_id_type=MESH)
                        cp.start()
                        sends.append(cp)
                idx = 0
                for k in range(n_arr):
                    for i in range(metas[k][0]):
                        sends[idx].wait_recv()
                        add_blocks(k, view(k, i, b), land_refs[k].at[i], o_refs[k].at[i])
                        idx += 1
                for cp in sends:
                    cp.wait_send()

        for cp in gathers:
            cp.wait()

    out_shape = [jax.ShapeDtypeStruct(m, out_dtype) for m in metas]
    land_shape = [jax.ShapeDtypeStruct(m, v.dtype) for m, v in zip(metas, xs)]
    g_shape = [jax.ShapeDtypeStruct((N_DEV, *v.shape), v.dtype) for v in gather]
    scratch = []
    for k in range(n_arr):
        blk = (2, chunk[k], metas[k][2])
        scratch += [pltpu.VMEM(blk, xs[k].dtype)] * 2 + [pltpu.VMEM(blk, out_dtype)]
    scratch += [pltpu.SemaphoreType.DMA((n_sem,)), pltpu.SemaphoreType.DMA((n_sem,)),
                pltpu.SemaphoreType.DMA((2, 2)), pltpu.SemaphoreType.DMA((2,))]
    if n_g:
        scratch += _gather_sems(n_g)
    outs = pl.pallas_call(
        body,
        name=name,
        in_specs=[ANY] * (n_arr + n_g),
        out_specs=[ANY] * (2 * n_arr + n_g),
        out_shape=out_shape + land_shape + g_shape,
        scratch_shapes=scratch,
    )(*xs, *gather)
    return outs[:n_arr], outs[2 * n_arr:]


def _hbm_add(a_view, b_view, o_view, bufs, in_sems, out_sems, ch):
    rows = a_view.shape[0]
    nch = rows // ch
    va, vb, vo = bufs

    def rows_of(j):
        return pl.ds(pl.multiple_of(j * ch, 16), ch)

    def loads(j, s):
        return (pltpu.make_async_copy(a_view.at[rows_of(j), :], va.at[s], in_sems.at[0, s]),
                pltpu.make_async_copy(b_view.at[rows_of(j), :], vb.at[s], in_sems.at[1, s]))

    def store(j, s):
        return pltpu.make_async_copy(vo.at[s], o_view.at[rows_of(j), :], out_sems.at[s])

    for cp in loads(0, 0):
        cp.start()

    def step(j, _):
        s = lax.rem(j, 2)

        @pl.when(j + 1 < nch)
        def _():
            for cp in loads(j + 1, 1 - s):
                cp.start()

        for cp in loads(j, s):
            cp.wait()

        @pl.when(j >= 2)
        def _():
            store(j - 2, s).wait()

        vo[s] = (va[s].astype(F32) + vb[s].astype(F32)).astype(vo.dtype)
        store(j, s).start()
        return 0

    lax.fori_loop(0, nch, step, 0)
    for j in range(max(0, nch - 2), nch):
        store(j, j % 2).wait()


def _xy_stage(xs, first, name):
    n_arr = len(xs)
    if first:
        shapes = [(v.shape[2] // 2, v.shape[3]) for v in xs]
        ins = list(xs)
    else:
        shapes = [(a.shape[1], a.shape[2]) for a, _ in xs]
        ins = [v for pair in xs for v in pair]
    n_blk = 2 if first else 1
    out_dtype = BF16 if first else F32
    chunk = [_chunk_rows(r, c) for (r, c) in shapes]
    n_sem = 2 * n_blk * n_arr

    def body(*refs):
        n_in = len(ins)
        in_refs = refs[:n_in]
        n_out = 2 * n_arr if first else n_arr
        o_refs = refs[n_in:n_in + n_out]
        land = refs[n_in + n_out:n_in + n_out + 2 * n_arr]
        rest = refs[n_in + n_out + 2 * n_arr:]
        bufs = rest[:3 * n_arr]
        send_sems, recv_sems, in_sems, out_sems = rest[3 * n_arr:]
        x, y, c = _position()
        peers = {"x": (1 - x, y, c), "y": (x, 1 - y, c)}
        jobs = []
        for k in range(n_arr):
            r, _ = shapes[k]
            half_a, half_b = pl.ds(0, r), pl.ds(r, r)
            if first:
                src = in_refs[k]
                for i in range(2):
                    jobs.append((k, src.at[i, 1 - y, half_a, :], src.at[i, y, half_a, :], land[2 * k].at[i],
                                 o_refs[2 * k].at[i], "y"))
                    jobs.append((k, src.at[1 - x, i, half_b, :], src.at[x, i, half_b, :], land[2 * k + 1].at[i],
                                 o_refs[2 * k + 1].at[i], "x"))
            else:
                a1, b1 = in_refs[2 * k], in_refs[2 * k + 1]
                jobs.append((k, a1.at[1 - x], a1.at[x], land[2 * k], o_refs[k].at[half_a, :], "x"))
                jobs.append((k, b1.at[1 - y], b1.at[y], land[2 * k + 1], o_refs[k].at[half_b, :], "y"))
        sends = []
        for n, (k, send, _, landing, _, axis) in enumerate(jobs):
            cp = pltpu.make_async_remote_copy(src_ref=send, dst_ref=landing, send_sem=send_sems.at[n],
                                              recv_sem=recv_sems.at[n], device_id=peers[axis], device_id_type=MESH)
            cp.start()
            sends.append(cp)
        for cp, (k, _, kept, landing, out, _) in zip(sends, jobs):
            cp.wait_recv()
            _hbm_add(kept, landing, out, bufs[3 * k:3 * k + 3], in_sems, out_sems, chunk[k])
        for cp in sends:
            cp.wait_send()

    if first:
        out_shape = [jax.ShapeDtypeStruct((2, r, c), BF16) for (r, c) in shapes for _ in range(2)]
        land_shape = out_shape
    else:
        out_shape = [jax.ShapeDtypeStruct((2 * r, c), F32) for (r, c) in shapes]
        land_shape = [jax.ShapeDtypeStruct((r, c), BF16) for (r, c) in shapes for _ in range(2)]
    scratch = []
    for k in range(n_arr):
        scratch += [pltpu.VMEM((2, chunk[k], shapes[k][1]), BF16)] * 2 + [pltpu.VMEM((2, chunk[k], shapes[k][1]), out_dtype)]
    scratch += [pltpu.SemaphoreType.DMA((n_sem,)), pltpu.SemaphoreType.DMA((n_sem,)),
                pltpu.SemaphoreType.DMA((2, 2)), pltpu.SemaphoreType.DMA((2,))]
    outs = pl.pallas_call(
        body,
        name=name,
        in_specs=[ANY] * len(ins),
        out_specs=[ANY] * (len(out_shape) + len(land_shape)),
        out_shape=out_shape + land_shape,
        scratch_shapes=scratch,
    )(*ins)
    outs = outs[:len(out_shape)]
    return [(outs[2 * k], outs[2 * k + 1]) for k in range(n_arr)] if first else list(outs)


def _reduce_scatter(dw_al, blocks, gather=()):
    xs = [dw_al] + [b.reshape(N_DEV // 2, 2, *b.shape[1:]) for b in blocks]
    ys, gathered = _halving_stage(xs, "c", "rs_c", BF16, windowed=(0,), gather=gather)
    pairs = _xy_stage([v.reshape(2, 2, *v.shape[1:]) for v in ys], True, "rs_xy1")
    return _xy_stage(pairs, False, "rs_xy2"), gathered


def _sum_slots(gs, name):
    n = len(gs)

    def body(*refs):
        for g_ref, o_ref in zip(refs[:n], refs[n:]):
            acc = g_ref[0].astype(F32)
            for d in range(1, N_DEV):
                acc = acc + g_ref[d].astype(F32)
            o_ref[...] = acc

    return pl.pallas_call(body, name=name, out_shape=[jax.ShapeDtypeStruct(g.shape[1:], F32) for g in gs])(*gs)


def _assemble_w_al(wins, bas):
    cols = wins.shape[2]
    n_buf = 3
    ends = [WIN_ROW0[d + 1] if d + 1 < N_DEV else WIN_ROW0[d] + WIN_W for d in range(N_DEV)]
    tail = W_AL - ends[-1]

    def body(w_ref, ba_ref, o_ref, buf, zeros, ld_sems, st_sems, ba_sem):
        def load(d):
            return pltpu.make_async_copy(w_ref.at[d], buf.at[d % n_buf], ld_sems.at[d % n_buf])

        def store(d):
            n = ends[d] - WIN_ROW0[d]
            return pltpu.make_async_copy(buf.at[d % n_buf, pl.ds(0, n), :],
                                         o_ref.at[pl.ds(WIN_ROW0[d], n), :], st_sems.at[d % n_buf])

        zeros[...] = jnp.zeros_like(zeros)
        fill = pltpu.make_async_copy(zeros, o_ref.at[pl.ds(ends[-1], tail), :], ba_sem)
        fill.start()
        fill.wait()
        load(0).start()
        for d in range(N_DEV):
            if d + 1 < N_DEV:
                if d + 1 >= n_buf:
                    store(d + 1 - n_buf).wait()
                load(d + 1).start()
            load(d).wait()
            if d > 0:
                ov = WIN_ROW0[d - 1] + WIN_W - WIN_ROW0[d]
                buf[d % n_buf, :ov, :] = buf[d % n_buf, :ov, :] + buf[(d - 1) % n_buf, WIN_W - ov:, :]
            if d == N_DEV - 1:
                ba_copy = pltpu.make_async_copy(
                    ba_ref.at[BA_DEV], buf.at[d % n_buf, pl.ds(WIN_W - N_BA, N_BA), :], ba_sem)
                ba_copy.start()
                ba_copy.wait()
            store(d).start()
        for d in range(N_DEV - n_buf, N_DEV):
            store(d).wait()

    return pl.pallas_call(
        body,
        name="assemble_w_al",
        in_specs=[ANY, ANY],
        out_specs=ANY,
        out_shape=jax.ShapeDtypeStruct((W_AL, cols), wins.dtype),
        scratch_shapes=[pltpu.VMEM((n_buf, WIN_W, cols), wins.dtype), pltpu.VMEM((tail, cols), wins.dtype),
                        pltpu.SemaphoreType.DMA((n_buf,)), pltpu.SemaphoreType.DMA((n_buf,)), pltpu.SemaphoreType.DMA],
    )(wins, bas)


def _adamw_math(w, g, m, v):
    m_new = ADAM_B1 * m + (1.0 - ADAM_B1) * g
    v_new = ADAM_B2 * v + (1.0 - ADAM_B2) * (g * g)
    m_hat = m_new / (1.0 - ADAM_B1 ** ADAM_STEP)
    v_hat = v_new / (1.0 - ADAM_B2 ** ADAM_STEP)
    return -ADAM_LR * (m_hat / (jnp.sqrt(v_hat) + ADAM_EPS) + ADAM_WD * w), m_new, v_new


def _adamw(w, g, m, v, name, tb=134):
    r, _, c = w.shape
    assert r % tb == 0

    def body(w_ref, g_ref, m_ref, v_ref, d_ref, nm_ref, nv_ref):
        d_ref[...], nm_ref[...], nv_ref[...] = _adamw_math(w_ref[...], g_ref[...], m_ref[...], v_ref[...])

    blk = pl.BlockSpec((tb, 1, c), lambda i: (i, 0, 0))
    o = jax.ShapeDtypeStruct(w.shape, F32)
    return pl.pallas_call(body, name=name, grid=(r // tb,), in_specs=[blk] * 4, out_specs=[blk] * 3,
                          out_shape=[o, o, o])(w, g, m, v)


def _adamw_many(ws, gs, ms, vs, name):
    n = len(ws)

    def body(*refs):
        for k in range(n):
            w_ref, g_ref, m_ref, v_ref = (refs[j * n + k] for j in range(4))
            d_ref, nm_ref, nv_ref = (refs[(4 + j) * n + k] for j in range(3))
            d_ref[...], nm_ref[...], nv_ref[...] = _adamw_math(w_ref[...], g_ref[...], m_ref[...], v_ref[...])

    shapes = [jax.ShapeDtypeStruct(w.shape, F32) for w in ws]
    outs = pl.pallas_call(body, name=name, out_shape=shapes * 3)(*ws, *gs, *ms, *vs)
    return outs[:n], outs[n:2 * n], outs[2 * n:]


def _select(me, table):
    return sum(jnp.where(me == d, jnp.int32(v), jnp.int32(0)) for d, v in enumerate(table))


WIN_SHIFT = tuple(SHARD_W * d - WIN_ROW0[d] for d in range(N_DEV))
PAD_L = 64
PAD_R = 64
assert max(WIN_SHIFT) <= PAD_L and WIN_W + N_BA - SHARD_W <= PAD_R


def _shard_to_window(shard_t, me):
    shift = _select(me, WIN_SHIFT)
    padded = jnp.pad(shard_t, ((PAD_L, PAD_R), (0, 0)))
    cols = shard_t.shape[1]
    lo = lax.dynamic_slice(padded, (PAD_L - shift, 0), (WIN_W, cols))
    hi = lax.dynamic_slice(padded, (PAD_L - shift + N_BA, 0), (WIN_W, cols))
    aligned = _select(me, WIN_ROW0) + lax.broadcasted_iota(jnp.int32, (WIN_W, 1), 0)
    return jnp.where(aligned >= ORIG_BA, hi, lo)


def _window_to_shard(win, ba_grad, me):
    shift = _select(me, WIN_SHIFT)
    cols = win.shape[1]
    padded = jnp.pad(win, ((N_BA, PAD_R), (0, 0)))
    lo = lax.dynamic_slice(padded, (N_BA + shift, 0), (SHARD_W, cols))
    hi = lax.dynamic_slice(padded, (shift, 0), (SHARD_W, cols))
    orig = SHARD_W * me + lax.broadcasted_iota(jnp.int32, (SHARD_W, 1), 0)
    ba_full = lax.dynamic_update_slice(jnp.zeros((SHARD_W, cols), win.dtype), ba_grad, (BA_LOCAL, 0))
    return jnp.where(orig < ORIG_BA, lo, jnp.where(orig >= ORIG_BA + N_BA, hi, ba_full))


def _pad_row(v, width=D_MODEL):
    v = v.reshape(1, -1)
    return jnp.pad(v, ((0, 0), (0, width - v.shape[1])))


def kernel(x, mem, norm_g, mem_norm_g, w_in, conv_w, a_log, dt_bias, dn_norm_g, w_mem_kv, w_br_dn, w_br_sb, w_br_mem, w_out, final_g, loss_target, m_norm_g, m_mem_norm_g, m_w_in, m_conv_w, m_a_log, m_dt_bias, m_dn_norm_g, m_w_mem_kv, m_w_br_dn, m_w_br_sb, m_w_br_mem, m_w_out, m_final_g, v_norm_g, v_mem_norm_g, v_w_in, v_conv_w, v_a_log, v_dt_bias, v_dn_norm_g, v_w_mem_kv, v_w_br_dn, v_w_br_sb, v_w_br_mem, v_w_out, v_final_g):
    xi, yi, ci = _position()
    me = 4 * xi + 2 * yi + ci

    shard_t = w_in[0].T
    win = _shard_to_window(shard_t, me).astype(BF16)
    ba = shard_t[BA_LOCAL:BA_LOCAL + N_BA, :].astype(BF16)
    g_win, g_ba = _all_gather([win, ba], "gather_weights")
    w_alt = _assemble_w_al(g_win, g_ba)

    shards = [w_mem_kv[0].astype(BF16), w_br_dn[0].astype(BF16), w_br_sb[0].astype(BF16), w_out[0].astype(BF16),
              w_br_mem[0].astype(BF16), conv_w[0]]
    r = _local_step(x[0], mem[0], loss_target[0], norm_g, mem_norm_g, w_alt, _pad_row(a_log, LANE),
                    _pad_row(dt_bias, LANE), dn_norm_g, final_g.reshape(1, D_MODEL), shards)

    dw_alt = r["w_alt"]
    parts = [r["norm_g"], r["mem_norm_g"], r["final_g"], r["dn_norm_g"], r["scal"], r["loss"], r["conv_w"],
             dw_alt[O_BA:O_BA + N_BA, :].astype(F32)]
    (g_win,), gathered = _reduce_scatter(dw_alt, [], gather=parts)
    rows_d = D_MODEL // N_DEV
    g_dn, g_sb, g_out, g_kv, g_mem = r["small"]
    g_kv = g_kv.reshape(rows_d, 2 * MEM_W)
    g_mem = g_mem.reshape(MEM_W, rows_d)
    s_norm_g, s_mem_norm_g, s_final_g, s_dn_norm_g, s_scal, s_loss, s_conv, s_ba = _sum_slots(gathered, "sum_small")
    loss = s_loss[0, 0]
    cw = conv_w.shape[2]
    g_conv = lax.dynamic_slice(s_conv, (0, cw * me), (CONV_K, cw))
    g_w_in_t = _window_to_shard(g_win, s_ba, me)
    grads = dict(norm_g=s_norm_g, mem_norm_g=s_mem_norm_g, w_in=g_w_in_t.T[None], conv_w=g_conv[None],
                 a_log=s_scal[0:1, :N_HEADS], dt_bias=s_scal[1:2, :N_HEADS], dn_norm_g=s_dn_norm_g, w_mem_kv=g_kv[None],
                 w_br_dn=g_dn[None], w_br_sb=g_sb[None], w_br_mem=g_mem[None], w_out=g_out[None],
                 final_g=s_final_g.reshape(D_MODEL))

    params = dict(norm_g=(norm_g, m_norm_g, v_norm_g), mem_norm_g=(mem_norm_g, m_mem_norm_g, v_mem_norm_g),
                  w_in=(w_in, m_w_in, v_w_in), conv_w=(conv_w, m_conv_w, v_conv_w), a_log=(a_log, m_a_log, v_a_log),
                  dt_bias=(dt_bias, m_dt_bias, v_dt_bias), dn_norm_g=(dn_norm_g, m_dn_norm_g, v_dn_norm_g),
                  w_mem_kv=(w_mem_kv, m_w_mem_kv, v_w_mem_kv), w_br_dn=(w_br_dn, m_w_br_dn, v_w_br_dn),
                  w_br_sb=(w_br_sb, m_w_br_sb, v_w_br_sb), w_br_mem=(w_br_mem, m_w_br_mem, v_w_br_mem),
                  w_out=(w_out, m_w_out, v_w_out), final_g=(final_g, m_final_g, v_final_g))
    order = list(params)
    deltas, new_m, new_v = {}, {}, {}
    deltas["w_in"], new_m["w_in"], new_v["w_in"] = (jnp.transpose(o, (1, 2, 0)) for o in _adamw(
        jnp.transpose(w_in, (2, 0, 1)), g_w_in_t[:, None, :], jnp.transpose(m_w_in, (2, 0, 1)),
        jnp.transpose(v_w_in, (2, 0, 1)), "adamw_w_in"))
    rest = [nm for nm in order if nm != "w_in"]

    def two_d(a):
        return a.reshape(1, -1) if a.ndim == 1 else a

    d_l, m_l, v_l = _adamw_many([two_d(params[nm][0]) for nm in rest], [two_d(grads[nm]) for nm in rest],
                                [two_d(params[nm][1]) for nm in rest], [two_d(params[nm][2]) for nm in rest], "adamw_rest")
    for k, nm in enumerate(rest):
        shp = params[nm][0].shape
        deltas[nm], new_m[nm], new_v[nm] = d_l[k].reshape(shp), m_l[k].reshape(shp), v_l[k].reshape(shp)
    return (loss, r["grad_x"][None], *[grads[nm] for nm in order], *[deltas[nm] for nm in order],
            *[new_m[nm] for nm in order], *[new_v[nm] for nm in order])
```

```python
import functools
import math

import jax
import jax.numpy as jnp
from jax import lax
from jax.experimental import pallas as pl
from jax.experimental.pallas import tpu as pltpu

F32 = jnp.float32
BF16 = jnp.bfloat16

D_MODEL = 1024
N_DEV = 8
N_HEADS = 8
D_HEAD = 128
DN_CHUNK = 64
CONV_K = 4
MEM_LEN = 256
MEM_HEADS = 4
MEM_DH = 64
MEM_W = MEM_HEADS * MEM_DH
NORM_EPS = 1e-6
IN_WIDTH = 11792
SHARD_W = IN_WIDTH // N_DEV

LANE = 128
SUPER = 2 * DN_CHUNK

O_QKV_DN = 0
O_Z_DN = 3072
O_QKV_SB = 4096
O_Z_SB = 7168
O_MQ = 8192
O_MZ = 8448
O_GATES = 8704
O_BA = 11776
W_AL = 11904
ORIG_BA = 4096
N_BA = 16

BA_DEV = ORIG_BA // SHARD_W
BA_LOCAL = ORIG_BA - BA_DEV * SHARD_W


def _aligned_col(o):
    return o if o < ORIG_BA else o - N_BA


ROW_TILE = 16
WIN_W = 1504
WIN_ROW0 = tuple(_aligned_col(SHARD_W * d) // ROW_TILE * ROW_TILE for d in range(N_DEV))
assert not any(ORIG_BA <= SHARD_W * d < ORIG_BA + N_BA for d in range(N_DEV))
assert all(WIN_ROW0[d] + WIN_W >= _aligned_col(SHARD_W * (d + 1) - 1) + 1 for d in range(N_DEV))
assert all(WIN_ROW0[d + 1] <= WIN_ROW0[d] + WIN_W for d in range(N_DEV - 1))
assert WIN_ROW0[-1] + WIN_W == O_BA + N_BA

ADAM_LR = 0.001
ADAM_B1 = 0.9
ADAM_B2 = 0.999
ADAM_EPS = 1e-08
ADAM_WD = 0.01
ADAM_STEP = 10

NN = (((1,), (0,)), ((), ()))
NT = (((1,), (1,)), ((), ()))
TN = (((0,), (0,)), ((), ()))


def _dot(a, b, dims):
    return lax.dot_general(a.astype(BF16), b.astype(BF16), dims, preferred_element_type=F32)


def _split2(a):
    hi = a.astype(BF16)
    lo = (a - hi.astype(F32)).astype(BF16)
    return hi, lo


def _dot3(a, b, dims):
    ah, al = _split2(a)
    bh, bl = _split2(b)
    d = functools.partial(lax.dot_general, dimension_numbers=dims, preferred_element_type=F32)
    return d(ah, bh) + (d(ah, bl) + d(al, bh))


def _sel_dot_impl(sel01, x, dims):
    sel = sel01.astype(BF16)
    h1 = x.astype(BF16)
    r1 = x - h1.astype(F32)
    h2 = r1.astype(BF16)
    h3 = (r1 - h2.astype(F32)).astype(BF16)
    d = functools.partial(lax.dot_general, dimension_numbers=dims, preferred_element_type=F32)
    return d(sel, h1) + (d(sel, h2) + d(sel, h3))


@jax.custom_vjp
def _sel_dot(sel01, x):
    return _sel_dot_impl(sel01, x, NN)


_sel_dot.defvjp(lambda s, x: (_sel_dot(s, x), s),
                lambda s, g: (jnp.zeros_like(s), _sel_dot_impl(s, g, TN)))


def _make_mm(dotfn):
    @jax.custom_vjp
    def nn(a, b):
        return dotfn(a, b, NN)

    @jax.custom_vjp
    def nt(a, b):
        return dotfn(a, b, NT)

    @jax.custom_vjp
    def tn(a, b):
        return dotfn(a, b, TN)

    nn.defvjp(lambda a, b: (nn(a, b), (a, b)), lambda r, g: (nt(g, r[1]), tn(r[0], g)))
    nt.defvjp(lambda a, b: (nt(a, b), (a, b)), lambda r, g: (nn(g, r[1]), tn(g, r[0])))
    tn.defvjp(lambda a, b: (tn(a, b), (a, b)), lambda r, g: (nt(r[1], g), nn(r[0], g)))
    return nn, nt, tn


mm_nn, mm_nt, mm_tn = _make_mm(_dot)
mm3_nn, mm3_nt, mm3_tn = _make_mm(_dot3)


def _sigmoid(x):
    return jax.nn.sigmoid(x)


def _silu(x):
    return x * _sigmoid(x)


def _softplus_parts(x):
    sp = jnp.log1p(jnp.exp(-jnp.abs(x)))
    return jnp.maximum(x, 0.0) + sp, jnp.maximum(-x, 0.0) + sp


def _rmsnorm(x, g):
    return x * lax.rsqrt(jnp.mean(x * x, axis=-1, keepdims=True) + NORM_EPS) * g


def _iota2(shape, dim):
    return lax.broadcasted_iota(jnp.int32, shape, dim)


def _div64(i):
    return lax.shift_right_logical(i, jnp.full(i.shape, 6, jnp.int32))


def _each(f, *lists):
    return [f(*a) for a in zip(*lists)]


@jax.custom_vjp
def _inv_unit_lower(ms):
    n = ms[0].shape[0]
    eye = (_iota2((n, n), 0) == _iota2((n, n), 1)).astype(F32)
    rs = [eye - m for m in ms]
    ps = ms
    for _ in range(5):
        ps = _each(mm3_nn, ps, ps)
        rs = _each(lambda r, p: r + mm_nn(r, p), rs, ps)
    return rs


def _inv_fwd(ms):
    rs = _inv_unit_lower(ms)
    return rs, rs


def _inv_bwd(rs, gs):
    ts = _each(mm_tn, rs, gs)
    return (_each(lambda t, r: -mm_nt(t, r), ts, rs),)


_inv_unit_lower.defvjp(_inv_fwd, _inv_bwd)


def _dn_block(cq, ck, cv, bcol, acol, zt, alog, dtb, gn, s0):
    n = SUPER
    h = DN_CHUNK
    row = _iota2((n, n), 0)
    col = _iota2((n, n), 1)
    same = _div64(row) == _div64(col)
    incl = jnp.logical_and(same, row >= col)
    strict = jnp.logical_and(same, row > col)
    incl_f = incl.astype(F32)

    qn = _each(lambda x: x * lax.rsqrt(jnp.sum(x * x, axis=-1, keepdims=True) + NORM_EPS) * (D_HEAD ** -0.5), cq)
    kn = _each(lambda x: x * lax.rsqrt(jnp.sum(x * x, axis=-1, keepdims=True) + NORM_EPS), ck)
    beta = _each(_sigmoid, bcol)
    g = _each(lambda al, ac, dt: -(jnp.exp(al) * _softplus_parts(ac + dt)[0]), alog, acol, dtb)
    gcum = _each(lambda x: _sel_dot(incl_f, jnp.broadcast_to(x, (n, n))), g)
    gam_incl = _each(lambda x: jnp.where(incl, jnp.exp(jnp.where(incl, x - x.T, 0.0)), 0.0), gcum)
    kk = _each(mm_nt, kn, kn)
    t_inv = _inv_unit_lower(_each(lambda b, x, gm: b * x * jnp.where(strict, gm, 0.0), beta, kk, gam_incl))
    eg = _each(jnp.exp, gcum)
    u = _each(lambda t, v, b: mm_nn(t, v * b), t_inv, cv, beta)
    w = _each(lambda t, k, b, e: mm_nn(t, k * (b * e)), t_inv, kn, beta, eg)
    a_intra = _each(lambda q, k, gm: mm_nt(q, k) * gm, qn, kn, gam_incl)
    q_dec = _each(lambda q, e: q * e, qn, eg)
    last0 = _each(lambda x: x[h - 1:h, :], gcum)
    last1 = _each(lambda x: x[n - 1:n, :], gcum)
    k_dec = _each(lambda k, x, l0, l1: k * jnp.exp(jnp.concatenate(
        [jnp.broadcast_to(l0, (h, n)), jnp.broadcast_to(l1, (h, n))], axis=0) - x), kn, gcum, last0, last1)
    v0 = _each(lambda uu, ww, s: uu[:h] - mm_nn(ww[:h], s), u, w, s0)
    o0 = _each(lambda q, s: mm_nn(q[:h], s), q_dec, s0)
    s1 = _each(lambda s, l0, k, v: s * jnp.exp(l0) + mm_tn(k[:h], v), s0, last0, k_dec, v0)
    v1 = _each(lambda uu, ww, s: uu[h:] - mm_nn(ww[h:], s), u, w, s1)
    o1 = _each(lambda q, s: mm_nn(q[h:], s), q_dec, s1)
    s2 = _each(lambda s, l1, k, v: s * jnp.exp(l1) + mm_tn(k[h:], v), s1, last1, k_dec, v1)
    o = _each(lambda a, b, am, x, y: jnp.concatenate([a, b], axis=0) + mm_nn(am, jnp.concatenate([x, y], axis=0)),
              o0, o1, a_intra, v0, v1)
    out = _each(lambda x, z: _rmsnorm(x, gn) * _silu(z), o, zt)
    return out, s2


def _mem_fn(mq, mz, mkv):
    mk = mkv[:, :MEM_W]
    mv = mkv[:, MEM_W:]
    lane = _iota2((1, MEM_W), 1)
    hm = [(_div64(lane) == hd).astype(F32) for hd in range(MEM_HEADS)]
    s = _each(lambda m: mm_nt(mq * m, mk) * (1.0 / math.sqrt(MEM_DH)), hm)
    s = _each(lambda x: x - jnp.max(x, axis=-1, keepdims=True), s)
    e = _each(jnp.exp, s)
    p = _each(lambda x: x / jnp.sum(x, axis=-1, keepdims=True), e)
    o = _each(lambda x, m: mm_nn(x, mv) * m, p, hm)
    out = o[0]
    for x in o[1:]:
        out = out + x
    return out * _silu(mz)


def _loss_fn(x, mo, fg, tgt):
    y = _rmsnorm(x + mo, fg)
    err = y - tgt
    return 0.5 * jnp.sum(jnp.mean(err * err, axis=-1, keepdims=True), axis=0, keepdims=True)


def _matmul_tn(a, b, out_dtype, tm, tn, tk, name):
    kdim, m = a.shape
    n = b.shape[1]
    tm, tn, tk = min(tm, m), min(tn, n), min(tk, kdim)
    assert m % tm == 0 and n % tn == 0 and kdim % tk == 0
    nk = kdim // tk

    def body(a_ref, b_ref, o_ref, acc_ref):
        k = pl.program_id(2)
        part = _dot(a_ref[...], b_ref[...], TN)

        @pl.when(k == 0)
        def _():
            acc_ref[...] = part

        @pl.when(k > 0)
        def _():
            acc_ref[...] += part

        @pl.when(k == nk - 1)
        def _():
            o_ref[...] = acc_ref[...].astype(o_ref.dtype)

    def body_one(a_ref, b_ref, o_ref, acc_ref):
        o_ref[...] = _dot(a_ref[...], b_ref[...], TN).astype(o_ref.dtype)

    return pl.pallas_call(
        body_one if nk == 1 else body,
        name=name,
        grid=(m // tm, n // tn, nk),
        in_specs=[pl.BlockSpec((tk, tm), lambda i, j, k: (k, i)), pl.BlockSpec((tk, tn), lambda i, j, k: (k, j))],
        out_specs=pl.BlockSpec((tm, tn), lambda i, j, k: (i, j)),
        out_shape=jax.ShapeDtypeStruct((m, n), out_dtype),
        scratch_shapes=[pltpu.VMEM((tm, tn), F32)],
        compiler_params=pltpu.CompilerParams(dimension_semantics=("parallel", "parallel", "arbitrary")),
    )(a, b)


def _norm_in(x, g, tm=512):
    t = x.shape[0]
    tm = min(tm, t)

    def body(x_ref, g_ref, h_ref):
        h_ref[...] = _rmsnorm(x_ref[...], g_ref[...]).astype(BF16)

    return pl.pallas_call(
        body,
        name="norm_in",
        grid=(t // tm,),
        in_specs=[pl.BlockSpec((tm, D_MODEL), lambda i: (i, 0)), pl.BlockSpec((1, D_MODEL), lambda i: (0, 0))],
        out_specs=pl.BlockSpec((tm, D_MODEL), lambda i: (i, 0)),
        out_shape=jax.ShapeDtypeStruct((t, D_MODEL), BF16),
    )(x, g)


def _dw_alt(parts, h, tm=512):
    t = h.shape[0]
    n_p = len(parts)
    widths = [p.shape[1] for p in parts]
    offs = [sum(widths[:s]) for s in range(n_p)]
    total = sum(widths)
    n_tiles = pl.cdiv(total, tm)

    specs = []
    for off, w in zip(offs, widths):
        if w >= tm:
            assert w % tm == 0 and off % tm == 0
            specs.append(pl.BlockSpec(
                (t, tm), lambda i, lo=off // tm, n=w // tm: (0, jnp.minimum(jnp.maximum(i - lo, 0), n - 1))))
        else:
            assert off // tm == (off + w - 1) // tm
            specs.append(pl.BlockSpec((t, w), lambda i: (0, 0), pipeline_mode=pl.Buffered(1)))

    def body(*refs):
        a_refs, h_ref, o_ref = refs[:n_p], refs[n_p], refs[n_p + 1]
        i = pl.program_id(0)
        for a_ref, off, w in zip(a_refs, offs, widths):
            if w >= tm:
                @pl.when(jnp.logical_and(i >= off // tm, i < (off + w) // tm))
                def _(a_ref=a_ref):
                    o_ref[...] = _dot(a_ref[...], h_ref[...], TN).astype(o_ref.dtype)
            else:
                @pl.when(i == off // tm)
                def _(a_ref=a_ref, r0=off % tm, w=w):
                    o_ref[r0:r0 + w, :] = _dot(a_ref[...], h_ref[...], TN).astype(o_ref.dtype)
        if total % tm:
            @pl.when(i == n_tiles - 1)
            def _():
                o_ref[total % tm:, :] = jnp.zeros((tm - total % tm, D_MODEL), o_ref.dtype)

    return pl.pallas_call(
        body,
        name="dw_alt",
        grid=(n_tiles,),
        in_specs=specs + [pl.BlockSpec((t, D_MODEL), lambda i: (0, 0), pipeline_mode=pl.Buffered(1))],
        out_specs=pl.BlockSpec((tm, D_MODEL), lambda i: (i, 0)),
        out_shape=jax.ShapeDtypeStruct((n_tiles * tm, D_MODEL), BF16),
    )(*parts, h)


def _grad_x(parts, w_alt, x, g, dres, tm=256):
    t = x.shape[0]
    tm = min(tm, t)
    n_p = len(parts)
    assert sum(p.shape[1] for p in parts) == w_alt.shape[0] and t % tm == 0

    def body(*refs):
        a_refs = refs[:n_p]
        w_ref, x_ref, g_ref, dres_ref, dx_ref, dg_ref = refs[n_p:]
        dproj = jnp.concatenate([a_ref[...] for a_ref in a_refs], axis=1)
        dh = _dot(dproj, w_ref[...], NN)

        @pl.when(pl.program_id(0) == 0)
        def _():
            dg_ref[...] = jnp.zeros_like(dg_ref)

        _, vjp = jax.vjp(_rmsnorm, x_ref[...], g_ref[...])
        dx, dg = vjp(dh)
        dx_ref[...] = dx + dres_ref[...]
        dg_ref[...] += dg

    row = pl.BlockSpec((tm, D_MODEL), lambda i: (i, 0))
    vec = pl.BlockSpec((1, D_MODEL), lambda i: (0, 0))
    return pl.pallas_call(
        body,
        name="grad_x",
        grid=(t // tm,),
        in_specs=[pl.BlockSpec((tm, p.shape[1]), lambda i: (i, 0)) for p in parts]
        + [pl.BlockSpec(w_alt.shape, lambda i: (0, 0), pipeline_mode=pl.Buffered(1)), row, vec, row],
        out_specs=[row, vec],
        out_shape=[jax.ShapeDtypeStruct((t, D_MODEL), F32), jax.ShapeDtypeStruct((1, D_MODEL), F32)],
    )(*parts, w_alt, x, g, dres)


def _block_tail(proj, o_dn, o_sb, o_m, x, tgt, w_br_dn, w_br_sb, w_br_mem, w_out, fg, tm=256):
    t = x.shape[0]
    tm = min(tm, t)
    gw = 512
    n_g = 3 * D_MODEL // gw

    def body(*refs):
        g_refs = refs[:n_g]
        (odn_ref, osb_ref, om_ref, x_ref, t_ref, wdn_ref, wsb_ref, wm_ref, wo_ref, fg_ref, loss_ref, dout_ref, dfg_ref,
         mg_ref, dyd_ref, dys_ref, dym_ref, dg_ref, dod_ref, dos_ref, dom_ref) = refs[n_g:]
        y = [_dot(odn_ref[...], wdn_ref[...], NN), _dot(osb_ref[...], wsb_ref[...], NN),
             _dot(om_ref[...], wm_ref[...], NN)]
        s = [_sigmoid(jnp.concatenate([g_refs[2 * k][...], g_refs[2 * k + 1][...]], axis=1)) for k in range(3)]
        merged16 = (s[0] * y[0] + s[1] * y[1] + s[2] * y[2]).astype(BF16)
        mg_ref[...] = merged16
        mo = _dot(merged16, wo_ref[...], NN)
        loss, vjp = jax.vjp(_loss_fn, x_ref[...], mo, fg_ref[...], t_ref[...])
        _, dout, dfg, _ = vjp(jnp.ones((1, 1), F32))

        @pl.when(pl.program_id(0) == 0)
        def _():
            loss_ref[...] = jnp.zeros_like(loss_ref)
            dfg_ref[...] = jnp.zeros_like(dfg_ref)

        loss_ref[...] += jnp.broadcast_to(loss, loss_ref.shape)
        dfg_ref[...] += dfg
        dout_ref[...] = dout
        dmerged = _dot(dout, wo_ref[...], NT)
        dy = [(sk * dmerged).astype(BF16) for sk in s]
        dyd_ref[...], dys_ref[...], dym_ref[...] = dy
        dg_ref[...] = jnp.concatenate([dmerged * yk * (sk * (1.0 - sk)) for yk, sk in zip(y, s)], axis=1).astype(BF16)
        dod_ref[...] = _dot(dy[0], wdn_ref[...], NT).astype(BF16)
        dos_ref[...] = _dot(dy[1], wsb_ref[...], NT).astype(BF16)
        dom_ref[...] = _dot(dy[2], wm_ref[...], NT).astype(BF16)

    gates = [pl.BlockSpec((tm, gw), lambda i, j=j: (i, O_GATES // gw + j)) for j in range(n_g)]
    row = pl.BlockSpec((tm, D_MODEL), lambda i: (i, 0))
    rowm = pl.BlockSpec((tm, MEM_W), lambda i: (i, 0))
    vec = pl.BlockSpec((1, D_MODEL), lambda i: (0, 0))

    def whole(a):
        return pl.BlockSpec(a.shape, lambda i: (0, 0), pipeline_mode=pl.Buffered(1))

    def bf(c):
        return jax.ShapeDtypeStruct((t, c), BF16)

    return pl.pallas_call(
        body,
        name="block_tail",
        grid=(t // tm,),
        in_specs=gates + [row, row, rowm, row, row, whole(w_br_dn), whole(w_br_sb), whole(w_br_mem), whole(w_out), vec],
        out_specs=[pl.BlockSpec((1, LANE), lambda i: (0, 0)), row, vec, row, row, row, row,
                   pl.BlockSpec((tm, 3 * D_MODEL), lambda i: (i, 0)), row, row, rowm],
        out_shape=[jax.ShapeDtypeStruct((1, LANE), F32), jax.ShapeDtypeStruct((t, D_MODEL), F32),
                   jax.ShapeDtypeStruct((1, D_MODEL), F32), bf(D_MODEL), bf(D_MODEL), bf(D_MODEL), bf(D_MODEL),
                   bf(3 * D_MODEL), bf(D_MODEL), bf(D_MODEL), bf(MEM_W)],
    )(*([proj] * n_g), o_dn, o_sb, o_m, x, tgt, w_br_dn, w_br_sb, w_br_mem, w_out, fg)


def _shift_rows(x, s):
    t = x.shape[0]
    if s == 0:
        return x
    rolled = pltpu.roll(x, s % t, 0)
    row = _iota2(x.shape, 0)
    keep = row >= s if s > 0 else row < t + s
    return jnp.where(keep, rolled, 0.0)


def _conv_pre(x, w):
    return sum(_shift_rows(x, CONV_K - 1 - j) * w[j:j + 1, :] for j in range(CONV_K))


CONV_TC = 256


def _dn_conv(proj, conv_w):
    t = proj.shape[0]
    nb = 3 * D_MODEL // CONV_TC

    def body(x_ref, w_ref, c_ref):
        c_ref[...] = _silu(_conv_pre(x_ref[...], w_ref[...]))

    return pl.pallas_call(
        body,
        name="dn_conv",
        grid=(nb,),
        in_specs=[pl.BlockSpec((t, CONV_TC), lambda j: (0, j)), pl.BlockSpec((CONV_K, CONV_TC), lambda j: (0, j))],
        out_specs=pl.BlockSpec((t, CONV_TC), lambda j: (0, j)),
        out_shape=jax.ShapeDtypeStruct((t, 3 * D_MODEL), F32),
    )(proj, conv_w)


def _dn_conv_bwd(proj, conv_w, dc):
    t = proj.shape[0]
    nb = 3 * D_MODEL // CONV_TC

    def body(x_ref, w_ref, dc_ref, dx_ref, dw_ref):
        x = x_ref[...]
        w = w_ref[...]
        pre = _conv_pre(x, w)
        sg = _sigmoid(pre)
        dpre = dc_ref[...] * (sg * (1.0 + pre * (1.0 - sg)))
        ahead = [_shift_rows(dpre, -(CONV_K - 1 - j)) for j in range(CONV_K)]
        dx_ref[...] = sum(a * w[j:j + 1, :] for j, a in enumerate(ahead)).astype(BF16)
        dw_ref[...] = jnp.concatenate([jnp.sum(a * x, axis=0, keepdims=True) for a in ahead], axis=0)

    blk = pl.BlockSpec((t, CONV_TC), lambda j: (0, j))
    wblk = pl.BlockSpec((CONV_K, CONV_TC), lambda j: (0, j))
    return pl.pallas_call(
        body,
        name="dn_conv_bwd",
        grid=(nb,),
        in_specs=[blk, wblk, blk],
        out_specs=[blk, wblk],
        out_shape=[jax.ShapeDtypeStruct((t, 3 * D_MODEL), BF16), jax.ShapeDtypeStruct((CONV_K, 3 * D_MODEL), F32)],
    )(proj, conv_w, dc)


def _ba_columns(ba, hd):
    lane = _iota2(ba.shape, 1)
    bcol = jnp.sum(jnp.where(lane == hd, ba, 0.0), axis=1, keepdims=True)
    acol = jnp.sum(jnp.where(lane == N_HEADS + hd, ba, 0.0), axis=1, keepdims=True)
    return bcol, acol


def _head_scalar(row, hd):
    lane = _iota2(row.shape, 1)
    return jnp.sum(jnp.where(lane == hd, row, 0.0), axis=1, keepdims=True)


DN_HP = 8


def _dn_inputs(cq, ck, cv, ba_ref, z_ref, alog_ref, dtb_ref, heads, lanes):
    ba = ba_ref[...]
    cols = [_ba_columns(ba, hd) for hd in heads]
    return ([cq[:, ln] for ln in lanes], [ck[:, ln] for ln in lanes], [cv[:, ln] for ln in lanes],
            [c[0] for c in cols], [c[1] for c in cols], [z_ref[:, ln] for ln in lanes],
            [_head_scalar(alog_ref[...], hd) for hd in heads], [_head_scalar(dtb_ref[...], hd) for hd in heads])


def _dn_specs(nblk, reverse):
    w = DN_HP * LANE
    nq = D_MODEL // w

    def row(i):
        return nblk - 1 - i if reverse else i

    def colblk(b0):
        return pl.BlockSpec((SUPER, w), lambda i, h: (row(i), b0 + h))

    ba = pl.BlockSpec((SUPER, LANE), lambda i, h: (row(i), O_BA // LANE))
    vec = pl.BlockSpec((1, LANE), lambda i, h: (0, 0))
    st = pl.BlockSpec((1, DN_HP, D_HEAD, D_HEAD), lambda i, h: (row(i), h, 0, 0))
    return colblk, nq, ba, vec, st


def _dn_fwd(c, proj, alog_row, dtb_row, gn):
    t = c.shape[0]
    nblk = t // SUPER
    colblk, nq, ba, vec, st = _dn_specs(nblk, False)

    def body(cq, ck, cv, ba_ref, z_ref, alog_ref, dtb_ref, gn_ref, o_ref, s_ref, state):
        @pl.when(jnp.logical_and(pl.program_id(0) == 0, pl.program_id(1) == 0))
        def _():
            state[...] = jnp.zeros_like(state)

        heads = [pl.program_id(1) * DN_HP + j for j in range(DN_HP)]
        lanes = [slice(j * LANE, (j + 1) * LANE) for j in range(DN_HP)]
        s0 = [state[hd] for hd in heads]
        outs, s2 = _dn_block(*_dn_inputs(cq, ck, cv, ba_ref, z_ref, alog_ref, dtb_ref, heads, lanes), gn_ref[...], s0)
        for j, (hd, ln) in enumerate(zip(heads, lanes)):
            s_ref[0, j] = s0[j]
            o_ref[:, ln] = outs[j].astype(BF16)
            state[hd] = s2[j]

    return pl.pallas_call(
        body,
        name="dn_fwd",
        grid=(nblk, N_HEADS // DN_HP),
        in_specs=[colblk(0), colblk(nq), colblk(2 * nq), ba, colblk(O_Z_DN // (DN_HP * LANE)), vec, vec, vec],
        out_specs=[colblk(0), st],
        out_shape=[jax.ShapeDtypeStruct((t, D_MODEL), BF16),
                   jax.ShapeDtypeStruct((nblk, N_HEADS, D_HEAD, D_HEAD), F32)],
        scratch_shapes=[pltpu.VMEM((N_HEADS, D_HEAD, D_HEAD), F32)],
    )(c, c, c, proj, proj, alog_row, dtb_row, gn)


def _dn_bwd(c, proj, alog_row, dtb_row, gn, states, do):
    t = c.shape[0]
    nblk = t // SUPER
    colblk, nq, ba, vec, st = _dn_specs(nblk, True)
    assert nq == 1

    def body(cq, ck, cv, ba_ref, z_ref, alog_ref, dtb_ref, gn_ref, s_ref, do_ref,
             dc_ref, dz_ref, dba_ref, dsc_ref, dgn_ref, dstate):
        i = pl.program_id(0)
        hq = pl.program_id(1)

        @pl.when(jnp.logical_and(i == 0, hq == 0))
        def _():
            dstate[...] = jnp.zeros_like(dstate)
            dsc_ref[...] = jnp.zeros_like(dsc_ref)
            dgn_ref[...] = jnp.zeros_like(dgn_ref)

        @pl.when(hq == 0)
        def _():
            dba_ref[...] = jnp.zeros_like(dba_ref)

        lane = _iota2((SUPER, LANE), 1)
        lane1 = _iota2((1, LANE), 1)
        heads = [hq * DN_HP + j for j in range(DN_HP)]
        lanes = [slice(j * LANE, (j + 1) * LANE) for j in range(DN_HP)]
        ds_in = [dstate[hd] for hd in heads]
        s_in = [s_ref[0, j] for j in range(DN_HP)]
        _, vjp = jax.vjp(_dn_block, *_dn_inputs(cq, ck, cv, ba_ref, z_ref, alog_ref, dtb_ref, heads, lanes),
                         gn_ref[...], s_in)
        dq, dk, dv, dbc, dac, dz, dal, ddt, dgn, ds0 = vjp(([do_ref[:, ln].astype(F32) for ln in lanes], ds_in))
        dba = jnp.zeros((SUPER, LANE), F32)
        dal_row = jnp.zeros((1, LANE), F32)
        ddt_row = jnp.zeros((1, LANE), F32)
        for j, (hd, ln) in enumerate(zip(heads, lanes)):
            for part, d in enumerate((dq, dk, dv)):
                dc_ref[:, part * D_MODEL + j * LANE:part * D_MODEL + (j + 1) * LANE] = d[j]
            dz_ref[:, ln] = dz[j].astype(BF16)
            dstate[hd] = ds0[j]
            dba = dba + jnp.where(lane == hd, dbc[j], 0.0) + jnp.where(lane == N_HEADS + hd, dac[j], 0.0)
            dal_row = dal_row + jnp.where(lane1 == hd, dal[j], 0.0)
            ddt_row = ddt_row + jnp.where(lane1 == hd, ddt[j], 0.0)
        dba_ref[...] += dba
        dsc_ref[0:1, :] += dal_row
        dsc_ref[1:2, :] += ddt_row
        dgn_ref[...] += dgn

    outs = pl.pallas_call(
        body,
        name="dn_bwd",
        grid=(nblk, N_HEADS // DN_HP),
        in_specs=[colblk(0), colblk(nq), colblk(2 * nq), ba, colblk(O_Z_DN // (DN_HP * LANE)), vec, vec, vec, st,
                  colblk(0)],
        out_specs=[pl.BlockSpec((SUPER, 3 * D_MODEL), lambda i, h: (nblk - 1 - i, 0)), colblk(0),
                   pl.BlockSpec((SUPER, LANE), lambda i, h: (nblk - 1 - i, 0)),
                   pl.BlockSpec((2, LANE), lambda i, h: (0, 0)), vec],
        out_shape=[jax.ShapeDtypeStruct((t, 3 * D_MODEL), F32), jax.ShapeDtypeStruct((t, D_MODEL), BF16),
                   jax.ShapeDtypeStruct((t, LANE), F32), jax.ShapeDtypeStruct((2, LANE), F32),
                   jax.ShapeDtypeStruct((1, LANE), F32)],
        scratch_shapes=[pltpu.VMEM((N_HEADS, D_HEAD, D_HEAD), F32)],
    )(c, c, c, proj, proj, alog_row, dtb_row, gn, states, do)
    return outs


SB_TQ = 256
SB_TK = 256
SB_HP_FWD = 8
SB_HP_BWD = 4


def _sb_logits(z, mask):
    sp = jnp.log(1.0 + jnp.exp(-jnp.abs(z)))
    lf_raw = -(jnp.maximum(z, 0.0) + sp)
    lb = lf_raw + z
    lf = lf_raw if mask is None else jnp.where(mask, lf_raw, 0.0)
    return lb, lf_raw, lf


def _suffix_sums(x, sel):
    hi, lo = _split2(x)
    d = functools.partial(lax.dot_general, dimension_numbers=NN, preferred_element_type=F32)
    return d(hi, sel) + d(lo, sel)


def _sb_diag_mask(tq, r):
    return r * SB_TK + _iota2((tq, SB_TK), 1) < _iota2((tq, SB_TK), 0)


def _sb_specs(t, tq, hp):
    w = hp * LANE
    q0, k0, v0, z0 = (O_QKV_SB // w, (O_QKV_SB + D_MODEL) // w, (O_QKV_SB + 2 * D_MODEL) // w, O_Z_SB // w)

    def blk(b0):
        return pl.BlockSpec((tq, w), lambda h, i: (i, b0 + h))

    def full(b0, **kw):
        return pl.BlockSpec((t, w), lambda h, i: (0, b0 + h), **kw)

    once = dict(pipeline_mode=pl.Buffered(1))
    return blk(q0), full(k0, **once), full(v0, **once), blk(z0), blk(0), full(0)


def _sb_fwd(proj, shards):
    t = proj.shape[0]
    tq = min(SB_TQ, t)
    ndiag = tq // SB_TK
    scale = 1.0 / math.sqrt(D_HEAD)
    na = len(shards)

    def body(q_ref, k_ref, v_ref, z_ref, *rest):
        x_refs, (o_ref, oraw_ref), land = rest[:na], rest[na:na + 2], rest[na + 2:2 * na + 2]
        sems = rest[2 * na + 2:]
        qi = pl.program_id(1)
        first = jnp.logical_and(pl.program_id(0) == 0, qi == 0)
        last = jnp.logical_and(pl.program_id(0) == pl.num_programs(0) - 1, qi == pl.num_programs(1) - 1)

        @pl.when(first)
        def _():
            for cp in _direct_gather_copies(x_refs, land, *sems):
                cp.start()

        lanes = [slice(hd * LANE, (hd + 1) * LANE) for hd in range(SB_HP_FWD)]
        qs = [(q_ref[:, ln] * scale).astype(BF16) for ln in lanes]
        after = (_iota2((SB_TK, SB_TK), 0) > _iota2((SB_TK, SB_TK), 1)).astype(BF16)
        oraw_ref[...] = jnp.zeros_like(oraw_ref)

        def block(kb, mask, c_lf):
            rows = pl.ds(pl.multiple_of(kb * SB_TK, SB_TK), SB_TK)
            z = _each(lambda q, ln: _dot(q, k_ref[rows, ln], NT), qs, lanes)
            lg = _each(lambda x: _sb_logits(x, mask), z)
            surv = _each(lambda x: _suffix_sums(x[2], after), lg)
            att = _each(lambda x, s, c: jnp.exp(x[0] + s + c), lg, surv, c_lf)
            if mask is not None:
                att = _each(lambda a: jnp.where(mask, a, 0.0), att)
            pv = _each(lambda a, ln: _dot(a, v_ref[rows, ln], NN), att, lanes)
            for p, ln in zip(pv, lanes):
                oraw_ref[:, ln] += p
            return tuple(_each(lambda c, x: c + jnp.sum(x[2], axis=1, keepdims=True), c_lf, lg))

        carry = tuple(jnp.zeros((tq, 1), F32) for _ in range(SB_HP_FWD))
        for r in reversed(range(ndiag)):
            carry = block(qi * ndiag + r, _sb_diag_mask(tq, r), carry)
        lax.fori_loop(0, qi * ndiag, lambda i, c: block(qi * ndiag - 1 - i, None, c), carry)
        o_ref[...] = (oraw_ref[...] * _silu(z_ref[...])).astype(BF16)

        @pl.when(last)
        def _():
            for cp in _direct_gather_copies(x_refs, land, *sems):
                cp.wait()

    q_spec, k_spec, v_spec, z_spec, out, _ = _sb_specs(t, tq, SB_HP_FWD)
    outs = pl.pallas_call(
        body,
        name="sb_fwd",
        grid=(N_HEADS // SB_HP_FWD, t // tq),
        in_specs=[q_spec, k_spec, v_spec, z_spec] + [ANY] * na,
        out_specs=[out, out] + [ANY] * na,
        out_shape=[jax.ShapeDtypeStruct((t, D_MODEL), BF16), jax.ShapeDtypeStruct((t, D_MODEL), F32)]
        + [jax.ShapeDtypeStruct((N_DEV, *v.shape), v.dtype) for v in shards],
        scratch_shapes=_gather_sems(na),
    )(proj, proj, proj, proj, *shards)
    return outs[0], outs[1], outs[2:]


def _sb_bwd(proj, oraw, do, blocks):
    t = proj.shape[0]
    tq = min(SB_TQ, t)
    ndiag = tq // SB_TK
    scale = 1.0 / math.sqrt(D_HEAD)
    nb = len(blocks)

    def body(q_ref, k_ref, v_ref, z_ref, oraw_ref, do_ref, *rest):
        blk_refs, (dq_ref, dk_ref, dv_ref, dz_ref), land_refs = rest[:nb], rest[nb:nb + 4], rest[nb + 4:2 * nb + 4]
        dk_acc, dv_acc, p_scr, z_scr, send_sems, recv_sems, local_sems = rest[2 * nb + 4:]
        qi = pl.program_id(1)
        nq = pl.num_programs(1)
        hg = pl.program_id(0)
        me = _position()
        mine = 4 * me[0] + 2 * me[1] + me[2]

        def exchange():
            cps = []
            for a, (blk_ref, land_ref) in enumerate(zip(blk_refs, land_refs)):
                cps.append(pltpu.make_async_copy(blk_ref.at[mine], land_ref.at[mine], local_sems.at[a]))
                for k, peer in enumerate(_other_devices(me)):
                    cps.append(pltpu.make_async_remote_copy(
                        src_ref=blk_ref.at[4 * peer[0] + 2 * peer[1] + peer[2]], dst_ref=land_ref.at[mine],
                        send_sem=send_sems.at[7 * a + k], recv_sem=recv_sems.at[7 * a + k], device_id=peer,
                        device_id_type=MESH))
            return cps

        @pl.when(jnp.logical_and(hg == 0, qi == 0))
        def _():
            for cp in exchange():
                cp.start()

        @pl.when(qi == 0)
        def _():
            dk_acc[...] = jnp.zeros_like(dk_acc)
            dv_acc[...] = jnp.zeros_like(dv_acc)

        heads = range(SB_HP_BWD)
        lanes = [slice(hd * LANE, (hd + 1) * LANE) for hd in heads]
        zg = z_ref[...]
        sg = _sigmoid(zg)
        dog = do_ref[...].astype(F32)
        dz_ref[...] = (dog * oraw_ref[...] * (sg * (1.0 + zg * (1.0 - sg)))).astype(BF16)
        d_o = (dog * (zg * sg)).astype(BF16)
        d_o16 = [d_o[:, ln] for ln in lanes]
        qs = [(q_ref[:, ln] * scale).astype(BF16) for ln in lanes]
        ri = _iota2((SB_TK, SB_TK), 0)
        ci = _iota2((SB_TK, SB_TK), 1)
        after = (ri > ci).astype(BF16)
        earlier = (ri < ci).astype(BF16)

        def rows_of(kb):
            return pl.ds(pl.multiple_of(kb * SB_TK, SB_TK), SB_TK)

        def down(kb, mask, c_lf):
            rows = rows_of(kb)
            z = _each(lambda q, ln: _dot(q, k_ref[rows, ln], NT), qs, lanes)
            da = _each(lambda d, ln: _dot(d, v_ref[rows, ln], NT), d_o16, lanes)
            lg = _each(lambda x: _sb_logits(x, mask), z)
            surv = _each(lambda x: _suffix_sums(x[2], after), lg)
            att = _each(lambda x, s, c: jnp.exp(x[0] + s + c), lg, surv, c_lf)
            if mask is not None:
                att = _each(lambda a: jnp.where(mask, a, 0.0), att)
            dv = _each(lambda a, d: _dot(a, d, TN), att, d_o16)
            for hd in heads:
                p_scr[hd, kb] = att[hd] * da[hd]
                z_scr[hd, kb] = z[hd]
                dv_acc[rows, lanes[hd]] += dv[hd]
            return tuple(_each(lambda c, x: c + jnp.sum(x[2], axis=1, keepdims=True), c_lf, lg))

        c_lf = tuple(jnp.zeros((tq, 1), F32) for _ in heads)
        for r in reversed(range(ndiag)):
            c_lf = down(qi * ndiag + r, _sb_diag_mask(tq, r), c_lf)
        lax.fori_loop(0, qi * ndiag, lambda i, c: down(qi * ndiag - 1 - i, None, c), c_lf)

        def up(kb, mask, carry):
            dq, c_p = carry
            rows = rows_of(kb)
            p = [p_scr[hd, kb] for hd in heads]
            zs = [z_scr[hd, kb] for hd in heads]
            before = _each(lambda x, c: _suffix_sums(x, earlier) + c, p, c_p)
            e = _each(lambda x: jnp.exp(-jnp.abs(x)), zs)
            r = _each(lambda x: 1.0 / (1.0 + x), e)
            sig = _each(lambda x, a, b: jnp.where(x >= 0.0, b, a * b), zs, e, r)
            oms = _each(lambda x, a, b: jnp.where(x >= 0.0, a * b, b), zs, e, r)
            if mask is not None:
                sig = _each(lambda a: jnp.where(mask, a, 0.0), sig)
            dzz = _each(lambda x, o, g, b: x * o - g * b, p, oms, sig, before)
            dk = _each(lambda x, q: _dot(x, q, TN), dzz, qs)
            dq = _each(lambda a, x, ln: a + _dot(x, k_ref[rows, ln], NN), dq, dzz, lanes)
            for hd in heads:
                dk_acc[rows, lanes[hd]] += dk[hd]
            return tuple(dq), tuple(_each(lambda c, x: c + jnp.sum(x, axis=1, keepdims=True), c_p, p))

        carry = (tuple(jnp.zeros((tq, D_HEAD), F32) for _ in heads), tuple(jnp.zeros((tq, 1), F32) for _ in heads))
        carry = lax.fori_loop(0, qi * ndiag, lambda kb, c: up(kb, None, c), carry)
        for r in range(ndiag):
            carry = up(qi * ndiag + r, _sb_diag_mask(tq, r), carry)
        dq = carry[0]
        for hd in heads:
            dq_ref[:, lanes[hd]] = (dq[hd] * scale).astype(BF16)

        @pl.when(qi == nq - 1)
        def _():
            dk_ref[...] = dk_acc[...].astype(BF16)
            dv_ref[...] = dv_acc[...].astype(BF16)

        @pl.when(jnp.logical_and(hg == pl.num_programs(0) - 1, qi == nq - 1))
        def _():
            for cp in exchange():
                cp.wait()

    q_spec, k_spec, v_spec, z_spec, blk, full = _sb_specs(t, tq, SB_HP_BWD)
    o = jax.ShapeDtypeStruct((t, D_MODEL), BF16)
    w = SB_HP_BWD * LANE
    outs = pl.pallas_call(
        body,
        name="sb_bwd",
        grid=(N_HEADS // SB_HP_BWD, t // tq),
        in_specs=[q_spec, k_spec, v_spec, z_spec, blk, blk] + [ANY] * nb,
        out_specs=[blk, full, full, blk] + [ANY] * nb,
        out_shape=[o, o, o, o] + [jax.ShapeDtypeStruct(b.shape, b.dtype) for b in blocks],
        scratch_shapes=[pltpu.VMEM((t, w), F32), pltpu.VMEM((t, w), F32)]
        + [pltpu.VMEM((SB_HP_BWD, t // SB_TK, tq, SB_TK), F32)] * 2 + _gather_sems(nb),
    )(proj, proj, proj, proj, oraw, do, *blocks)
    return outs[0], outs[1], outs[2], outs[3], outs[4:]


def _mem_kv_fn(mem, mg, w):
    return mm_nn(_rmsnorm(mem, mg), w)


def _mem_kv(mem, mg, w):
    def body(m_ref, g_ref, w_ref, o_ref):
        o_ref[...] = _mem_kv_fn(m_ref[...], g_ref[...], w_ref[...])

    return pl.pallas_call(body, name="mem_kv", out_shape=jax.ShapeDtypeStruct((MEM_LEN, 2 * MEM_W), F32))(mem, mg, w)


def _mem_kv_bwd(mem, mg, w, dmkv):
    def body(m_ref, g_ref, w_ref, d_ref, dg_ref, dw_ref):
        _, vjp = jax.vjp(_mem_kv_fn, m_ref[...], g_ref[...], w_ref[...].astype(F32))
        _, dg, dw = vjp(d_ref[...])
        dg_ref[...] = dg
        dw_ref[...] = dw.astype(BF16)

    return pl.pallas_call(
        body, name="mem_kv_bwd",
        out_shape=[jax.ShapeDtypeStruct((1, D_MODEL), F32), jax.ShapeDtypeStruct((D_MODEL, 2 * MEM_W), BF16)],
    )(mem, mg, w, dmkv)


def _mem_attn(proj, mkv, tm=1024):
    t = proj.shape[0]
    tm = min(tm, t)

    def body(q_ref, z_ref, kv_ref, o_ref):
        o_ref[...] = _mem_fn(q_ref[...], z_ref[...], kv_ref[...]).astype(BF16)

    return pl.pallas_call(
        body,
        name="mem_attn",
        grid=(t // tm,),
        in_specs=[pl.BlockSpec((tm, MEM_W), lambda i: (i, O_MQ // MEM_W)),
                  pl.BlockSpec((tm, MEM_W), lambda i: (i, O_MZ // MEM_W)),
                  pl.BlockSpec((MEM_LEN, 2 * MEM_W), lambda i: (0, 0))],
        out_specs=pl.BlockSpec((tm, MEM_W), lambda i: (i, 0)),
        out_shape=jax.ShapeDtypeStruct((t, MEM_W), BF16),
    )(proj, proj, mkv)


def _mem_attn_bwd(proj, mkv, do, tm=1024):
    t = proj.shape[0]
    tm = min(tm, t)

    def body(q_ref, z_ref, kv_ref, do_ref, dq_ref, dz_ref, dkv_ref):
        _, vjp = jax.vjp(_mem_fn, q_ref[...], z_ref[...], kv_ref[...])
        dq, dz, dkv = vjp(do_ref[...].astype(F32))
        dq_ref[...] = dq.astype(BF16)
        dz_ref[...] = dz.astype(BF16)

        @pl.when(pl.program_id(0) == 0)
        def _():
            dkv_ref[...] = jnp.zeros_like(dkv_ref)

        dkv_ref[...] += dkv

    blk = pl.BlockSpec((tm, MEM_W), lambda i: (i, 0))
    kv = pl.BlockSpec((MEM_LEN, 2 * MEM_W), lambda i: (0, 0))
    return pl.pallas_call(
        body,
        name="mem_attn_bwd",
        grid=(t // tm,),
        in_specs=[pl.BlockSpec((tm, MEM_W), lambda i: (i, O_MQ // MEM_W)),
                  pl.BlockSpec((tm, MEM_W), lambda i: (i, O_MZ // MEM_W)), kv, blk],
        out_specs=[blk, blk, kv],
        out_shape=[jax.ShapeDtypeStruct((t, MEM_W), BF16), jax.ShapeDtypeStruct((t, MEM_W), BF16),
                   jax.ShapeDtypeStruct((MEM_LEN, 2 * MEM_W), F32)],
    )(proj, proj, mkv, do)


def _proj_gather(h, w_alt, shards, tm=512, tn=3968):
    t = h.shape[0]
    tm = min(tm, t)
    n, kdim = w_alt.shape
    assert n % tn == 0 and t % tm == 0
    nj, ni = n // tn, t // tm
    na = len(shards)

    def body(h_ref, w_ref, *rest):
        x_refs, o_ref, land = rest[:na], rest[na], rest[na + 1:2 * na + 1]
        send_sems, recv_sems, local_sems = rest[2 * na + 1:]
        j, i = pl.program_id(0), pl.program_id(1)

        def copies():
            return _direct_gather_copies(x_refs, land, send_sems, recv_sems, local_sems)

        @pl.when(jnp.logical_and(j == 0, i == 0))
        def _():
            for cp in copies():
                cp.start()

        o_ref[...] = _dot(h_ref[...], w_ref[...], NT)

        @pl.when(jnp.logical_and(j == nj - 1, i == ni - 1))
        def _():
            for cp in copies():
                cp.wait()

    outs = pl.pallas_call(
        body,
        name="proj",
        grid=(nj, ni),
        in_specs=[pl.BlockSpec((tm, kdim), lambda j, i: (i, 0)), pl.BlockSpec((tn, kdim), lambda j, i: (j, 0))]
        + [ANY] * na,
        out_specs=[pl.BlockSpec((tm, tn), lambda j, i: (i, j))] + [ANY] * na,
        out_shape=[jax.ShapeDtypeStruct((t, n), F32)]
        + [jax.ShapeDtypeStruct((N_DEV, *v.shape), v.dtype) for v in shards],
        scratch_shapes=_gather_sems(na),
    )(h, w_alt, *shards)
    return outs[0], outs[1:]


def _local_step(x, mem, tgt, norm_g, mem_norm_g, w_alt, alog_row, dtb_row, dn_norm_g, final_g, shards):
    h = _norm_in(x, norm_g)
    s_kv, s_dn, s_sb, s_out, s_mem, s_conv = shards
    proj, (g_kv, g_conv) = _proj_gather(h, w_alt, [s_kv, s_conv])
    w_mem_kv = g_kv.reshape(D_MODEL, 2 * MEM_W)
    conv_w = g_conv.transpose(1, 0, 2).reshape(CONV_K, 3 * D_MODEL)

    c = _dn_conv(proj, conv_w)
    o_dn, states = _dn_fwd(c, proj, alog_row, dtb_row, dn_norm_g)
    o_sb, o_sb_raw, (g_dn, g_sb, g_out, g_mem) = _sb_fwd(proj, [s_dn, s_sb, s_out, s_mem])
    w_br_dn = g_dn.reshape(D_MODEL, D_MODEL)
    w_br_sb = g_sb.reshape(D_MODEL, D_MODEL)
    w_out = g_out.reshape(D_MODEL, D_MODEL)
    w_br_mem = g_mem.transpose(1, 0, 2).reshape(MEM_W, D_MODEL)
    mkv = _mem_kv(mem, mem_norm_g, w_mem_kv)
    o_m = _mem_attn(proj, mkv)

    (loss, dout, d_final_g, merged, dy_dn, dy_sb, dy_m, dgates, do_dn, do_sb, do_m) = _block_tail(
        proj, o_dn, o_sb, o_m, x, tgt, w_br_dn, w_br_sb, w_br_mem, w_out, final_g)
    dw_out = _matmul_tn(merged, dout, BF16, 256, 1024, 2048, "dw_out")
    dw_br_dn = _matmul_tn(o_dn, dy_dn, BF16, 256, 1024, 2048, "dw_br_dn")
    dw_br_sb = _matmul_tn(o_sb, dy_sb, BF16, 256, 1024, 2048, "dw_br_sb")
    dw_br_mem = _matmul_tn(o_m, dy_m, BF16, 256, 1024, 2048, "dw_br_mem")

    dmq, dmz, dmkv = _mem_attn_bwd(proj, mkv, do_m)
    d_mem_norm_g, dw_mem_kv = _mem_kv_bwd(mem, mem_norm_g, w_mem_kv, dmkv)
    rows_d = D_MODEL // N_DEV
    small_blocks = [
        dw_br_dn.reshape(N_DEV, rows_d, D_MODEL), dw_br_sb.reshape(N_DEV, rows_d, D_MODEL),
        dw_out.reshape(N_DEV, rows_d, D_MODEL), dw_mem_kv.reshape(N_DEV, rows_d // 2, D_MODEL),
        dw_br_mem.reshape(MEM_W, N_DEV, rows_d).transpose(1, 0, 2).reshape(N_DEV, MEM_W // N_DEV, D_MODEL)]
    dq_sb, dk_sb, dv_sb, dz_sb, small_parts = _sb_bwd(proj, o_sb_raw, do_sb, small_blocks)
    d_small = _sum_slots(list(small_parts), "sum_small_grads")
    dc, dz_dn, dba, dscal, d_dn_norm_g = _dn_bwd(c, proj, alog_row, dtb_row, dn_norm_g, states, do_dn)
    dqkv_dn, d_conv_w = _dn_conv_bwd(proj, conv_w, dc)

    dproj = [dqkv_dn, dz_dn, dq_sb, dk_sb, dv_sb, dz_sb, dmq, dmz, dgates, dba.astype(BF16)]
    dw_alt = _dw_alt(dproj, h)
    grad_x, d_norm_g = _grad_x(dproj, w_alt, x, norm_g, dout)
    return dict(loss=loss, grad_x=grad_x, norm_g=d_norm_g, mem_norm_g=d_mem_norm_g, w_alt=dw_alt, conv_w=d_conv_w,
                scal=dscal, dn_norm_g=d_dn_norm_g, small=d_small, final_g=d_final_g)


MESH = pl.DeviceIdType.MESH
ANY = pl.BlockSpec(memory_space=pl.ANY)


def _position():
    return lax.axis_index("x"), lax.axis_index("y"), lax.axis_index("c")


def _other_devices(me):
    return [tuple(1 - p if (f >> s) & 1 else p for p, s in zip(me, (2, 1, 0))) for f in range(1, N_DEV)]


def _direct_gather_copies(x_refs, land_refs, send_sems, recv_sems, local_sems):
    me = _position()
    mine = 4 * me[0] + 2 * me[1] + me[2]
    cps = []
    for a, (x_ref, land) in enumerate(zip(x_refs, land_refs)):
        cps.append(pltpu.make_async_copy(x_ref, land.at[mine], local_sems.at[a]))
        for k, peer in enumerate(_other_devices(me)):
            cps.append(pltpu.make_async_remote_copy(
                src_ref=x_ref, dst_ref=land.at[mine], send_sem=send_sems.at[7 * a + k],
                recv_sem=recv_sems.at[7 * a + k], device_id=peer, device_id_type=MESH))
    return cps


def _gather_sems(n):
    return [pltpu.SemaphoreType.DMA((7 * n,)), pltpu.SemaphoreType.DMA((7 * n,)), pltpu.SemaphoreType.DMA((n,))]


def _all_gather(xs, name):
    n = len(xs)

    def body(*refs):
        x_refs, o_refs = refs[:n], refs[n:2 * n]
        send_sems, recv_sems, local_sems = refs[2 * n:]
        x, y, c = _position()
        me, sibling = (x, y, c), (x, y, 1 - c)
        x_nbr, y_nbr, diag = (1 - x, y, c), (x, 1 - y, c), (1 - x, 1 - y, c)
        south = c == 0
        relay_from = tuple(jnp.where(south, a, b) for a, b in zip(y_nbr, x_nbr))
        relay_to = tuple(jnp.where(south, a, b) for a, b in zip(x_nbr, y_nbr))

        def slot(p):
            return 4 * p[0] + 2 * p[1] + p[2]

        def copy(a, k, block, to, src=None):
            dst = o_refs[a].at[slot(block)]
            return pltpu.make_async_remote_copy(
                src_ref=dst if src is None else src, dst_ref=dst, send_sem=send_sems.at[7 * a + k],
                recv_sem=recv_sems.at[7 * a + k], device_id=to, device_id_type=MESH)

        mine = [pltpu.make_async_copy(x_refs[a], o_refs[a].at[slot(me)], local_sems.at[a]) for a in range(n)]
        for cp in mine:
            cp.start()
        sends = []
        for a in range(n):
            sends += [copy(a, 0, me, sibling, src=x_refs[a]), copy(a, 1, me, x_nbr, src=x_refs[a]),
                      copy(a, 2, me, y_nbr, src=x_refs[a])]
        for cp in sends:
            cp.start()
        later = []
        for a in range(n):
            copy(a, 1, x_nbr, me).wait_recv()
            copy(a, 2, y_nbr, me).wait_recv()
            later += [copy(a, 3, relay_from, relay_to), copy(a, 4, x_nbr, sibling), copy(a, 5, y_nbr, sibling)]
            for cp in later[-3:]:
                cp.start()
        for a in range(n):
            copy(a, 3, diag, me).wait_recv()
            later.append(copy(a, 6, diag, sibling))
            later[-1].start()
        for a in range(n):
            copy(a, 0, sibling, me).wait_recv()
            for k, chip in ((4, x_nbr), (5, y_nbr), (6, diag)):
                copy(a, k, (chip[0], chip[1], 1 - c), me).wait_recv()
        for cp in sends + later:
            cp.wait_send()
        for cp in mine:
            cp.wait()

    return pl.pallas_call(
        body,
        name=name,
        in_specs=[ANY] * n,
        out_specs=[ANY] * n,
        out_shape=[jax.ShapeDtypeStruct((N_DEV, *v.shape), v.dtype) for v in xs],
        scratch_shapes=[pltpu.SemaphoreType.DMA((7 * n,)), pltpu.SemaphoreType.DMA((7 * n,)),
                        pltpu.SemaphoreType.DMA((n,))],
    )(*xs)


def _window_view(ref, dest):
    return ref.at[pl.ds(WIN_ROW0[dest], WIN_W), :]


def _chunk_rows(rows, cols):
    return max(ch for ch in range(ROW_TILE, rows + 1, ROW_TILE) if rows % ch == 0 and ch * cols <= (1 << 20))


def _halving_stage(xs, axis, name, out_dtype, windowed=(), gather=()):
    n_arr = len(xs)
    metas = []
    for k, v in enumerate(xs):
        if k in windowed:
            metas.append((N_DEV // 2, WIN_W, v.shape[1]))
        else:
            assert v.shape[1] == 2
            metas.append((v.shape[0], v.shape[2], v.shape[3]))
    chunk = [_chunk_rows(r, c) for (_, r, c) in metas]
    offs = [sum(m[0] for m in metas[:k]) for k in range(n_arr)]
    n_sem = sum(m[0] for m in metas)

    n_g = len(gather)

    def body(*refs):
        x_refs, g_refs = refs[:n_arr], refs[n_arr:n_arr + n_g]
        outs = refs[n_arr + n_g:]
        o_refs, land_refs, gl_refs = outs[:n_arr], outs[n_arr:2 * n_arr], outs[2 * n_arr:2 * n_arr + n_g]
        rest = outs[2 * n_arr + n_g:]
        bufs = rest[:3 * n_arr]
        send_sems, recv_sems, in_sems, out_sems = rest[3 * n_arr:3 * n_arr + 4]
        gathers = _direct_gather_copies(g_refs, gl_refs, *rest[3 * n_arr + 4:]) if n_g else []
        for cp in gathers:
            cp.start()
        pos = dict(zip("xyc", _position()))
        bit = pos[axis]
        peer = tuple(1 - pos[a] if a == axis else pos[a] for a in "xyc")

        def view(k, i, b):
            if k in windowed:
                return _window_view(x_refs[k], 2 * i + b)
            return x_refs[k].at[i, b]

        def add_blocks(k, a_view, b_view, o_view):
            _hbm_add(a_view, b_view, o_view, bufs[3 * k:3 * k + 3], in_sems, out_sems, chunk[k])

        for b in (0, 1):
            @pl.when(bit == b)
            def _(b=b):
                sends = []
                for k in range(n_arr):
                    for i in range(metas[k][0]):
                        cp = pltpu.make_async_remote_copy(
                            src_ref=view(k, i, 1 - b), dst_ref=land_refs[k].at[i], send_sem=send_sems.at[offs[k] + i],
                            recv_sem=recv_sems.at[offs[k] + i], device_id=peer, device_id_type=MESH)
                        cp.start()
                        sends.append(cp)
                idx = 0
                for k in range(n_arr):
                    for i in range(metas[k][0]):
                        sends[idx].wait_recv()
                        add_blocks(k, view(k, i, b), land_refs[k].at[i], o_refs[k].at[i])
                        idx += 1
                for cp in sends:
                    cp.wait_send()

        for cp in gathers:
            cp.wait()

    out_shape = [jax.ShapeDtypeStruct(m, out_dtype) for m in metas]
    land_shape = [jax.ShapeDtypeStruct(m, v.dtype) for m, v in zip(metas, xs)]
    g_shape = [jax.ShapeDtypeStruct((N_DEV, *v.shape), v.dtype) for v in gather]
    scratch = []
    for k in range(n_arr):
        blk = (2, chunk[k], metas[k][2])
        scratch += [pltpu.VMEM(blk, xs[k].dtype)] * 2 + [pltpu.VMEM(blk, out_dtype)]
    scratch += [pltpu.SemaphoreType.DMA((n_sem,)), pltpu.SemaphoreType.DMA((n_sem,)),
                pltpu.SemaphoreType.DMA((2, 2)), pltpu.SemaphoreType.DMA((2,))]
    if n_g:
        scratch += _gather_sems(n_g)
    outs = pl.pallas_call(
        body,
        name=name,
        in_specs=[ANY] * (n_arr + n_g),
        out_specs=[ANY] * (2 * n_arr + n_g),
        out_shape=out_shape + land_shape + g_shape,
        scratch_shapes=scratch,
    )(*xs, *gather)
    return outs[:n_arr], outs[2 * n_arr:]


def _hbm_add(a_view, b_view, o_view, bufs, in_sems, out_sems, ch):
    rows = a_view.shape[0]
    nch = rows // ch
    va, vb, vo = bufs

    def rows_of(j):
        return pl.ds(pl.multiple_of(j * ch, 16), ch)

    def loads(j, s):
        return (pltpu.make_async_copy(a_view.at[rows_of(j), :], va.at[s], in_sems.at[0, s]),
                pltpu.make_async_copy(b_view.at[rows_of(j), :], vb.at[s], in_sems.at[1, s]))

    def store(j, s):
        return pltpu.make_async_copy(vo.at[s], o_view.at[rows_of(j), :], out_sems.at[s])

    for cp in loads(0, 0):
        cp.start()

    def step(j, _):
        s = lax.rem(j, 2)

        @pl.when(j + 1 < nch)
        def _():
            for cp in loads(j + 1, 1 - s):
                cp.start()

        for cp in loads(j, s):
            cp.wait()

        @pl.when(j >= 2)
        def _():
            store(j - 2, s).wait()

        vo[s] = (va[s].astype(F32) + vb[s].astype(F32)).astype(vo.dtype)
        store(j, s).start()
        return 0

    lax.fori_loop(0, nch, step, 0)
    for j in range(max(0, nch - 2), nch):
        store(j, j % 2).wait()


def _xy_stage(xs, first, name):
    n_arr = len(xs)
    if first:
        shapes = [(v.shape[2] // 2, v.shape[3]) for v in xs]
        ins = list(xs)
    else:
        shapes = [(a.shape[1], a.shape[2]) for a, _ in xs]
        ins = [v for pair in xs for v in pair]
    n_blk = 2 if first else 1
    out_dtype = BF16 if first else F32
    chunk = [_chunk_rows(r, c) for (r, c) in shapes]
    n_sem = 2 * n_blk * n_arr

    def body(*refs):
        n_in = len(ins)
        in_refs = refs[:n_in]
        n_out = 2 * n_arr if first else n_arr
        o_refs = refs[n_in:n_in + n_out]
        land = refs[n_in + n_out:n_in + n_out + 2 * n_arr]
        rest = refs[n_in + n_out + 2 * n_arr:]
        bufs = rest[:3 * n_arr]
        send_sems, recv_sems, in_sems, out_sems = rest[3 * n_arr:]
        x, y, c = _position()
        peers = {"x": (1 - x, y, c), "y": (x, 1 - y, c)}
        jobs = []
        for k in range(n_arr):
            r, _ = shapes[k]
            half_a, half_b = pl.ds(0, r), pl.ds(r, r)
            if first:
                src = in_refs[k]
                for i in range(2):
                    jobs.append((k, src.at[i, 1 - y, half_a, :], src.at[i, y, half_a, :], land[2 * k].at[i],
                                 o_refs[2 * k].at[i], "y"))
                    jobs.append((k, src.at[1 - x, i, half_b, :], src.at[x, i, half_b, :], land[2 * k + 1].at[i],
                                 o_refs[2 * k + 1].at[i], "x"))
            else:
                a1, b1 = in_refs[2 * k], in_refs[2 * k + 1]
                jobs.append((k, a1.at[1 - x], a1.at[x], land[2 * k], o_refs[k].at[half_a, :], "x"))
                jobs.append((k, b1.at[1 - y], b1.at[y], land[2 * k + 1], o_refs[k].at[half_b, :], "y"))
        sends = []
        for n, (k, send, _, landing, _, axis) in enumerate(jobs):
            cp = pltpu.make_async_remote_copy(src_ref=send, dst_ref=landing, send_sem=send_sems.at[n],
                                              recv_sem=recv_sems.at[n], device_id=peers[axis], device_id_type=MESH)
            cp.start()
            sends.append(cp)
        for cp, (k, _, kept, landing, out, _) in zip(sends, jobs):
            cp.wait_recv()
            _hbm_add(kept, landing, out, bufs[3 * k:3 * k + 3], in_sems, out_sems, chunk[k])
        for cp in sends:
            cp.wait_send()

    if first:
        out_shape = [jax.ShapeDtypeStruct((2, r, c), BF16) for (r, c) in shapes for _ in range(2)]
        land_shape = out_shape
    else:
        out_shape = [jax.ShapeDtypeStruct((2 * r, c), F32) for (r, c) in shapes]
        land_shape = [jax.ShapeDtypeStruct((r, c), BF16) for (r, c) in shapes for _ in range(2)]
    scratch = []
    for k in range(n_arr):
        scratch += [pltpu.VMEM((2, chunk[k], shapes[k][1]), BF16)] * 2 + [pltpu.VMEM((2, chunk[k], shapes[k][1]), out_dtype)]
    scratch += [pltpu.SemaphoreType.DMA((n_sem,)), pltpu.SemaphoreType.DMA((n_sem,)),
                pltpu.SemaphoreType.DMA((2, 2)), pltpu.SemaphoreType.DMA((2,))]
    outs = pl.pallas_call(
        body,
        name=name,
        in_specs=[ANY] * len(ins),
        out_specs=[ANY] * (len(out_shape) + len(land_shape)),
        out_shape=out_shape + land_shape,
        scratch_shapes=scratch,
    )(*ins)
    outs = outs[:len(out_shape)]
    return [(outs[2 * k], outs[2 * k + 1]) for k in range(n_arr)] if first else list(outs)


def _reduce_scatter(dw_al, blocks, gather=()):
    xs = [dw_al] + [b.reshape(N_DEV // 2, 2, *b.shape[1:]) for b in blocks]
    ys, gathered = _halving_stage(xs, "c", "rs_c", BF16, windowed=(0,), gather=gather)
    pairs = _xy_stage([v.reshape(2, 2, *v.shape[1:]) for v in ys], True, "rs_xy1")
    return _xy_stage(pairs, False, "rs_xy2"), gathered


def _sum_slots(gs, name):
    n = len(gs)

    def body(*refs):
        for g_ref, o_ref in zip(refs[:n], refs[n:]):
            acc = g_ref[0].astype(F32)
            for d in range(1, N_DEV):
                acc = acc + g_ref[d].astype(F32)
            o_ref[...] = acc

    return pl.pallas_call(body, name=name, out_shape=[jax.ShapeDtypeStruct(g.shape[1:], F32) for g in gs])(*gs)


def _assemble_w_al(wins, bas):
    cols = wins.shape[2]
    n_buf = 3
    ends = [WIN_ROW0[d + 1] if d + 1 < N_DEV else WIN_ROW0[d] + WIN_W for d in range(N_DEV)]
    tail = W_AL - ends[-1]

    def body(w_ref, ba_ref, o_ref, buf, zeros, ld_sems, st_sems, ba_sem):
        def load(d):
            return pltpu.make_async_copy(w_ref.at[d], buf.at[d % n_buf], ld_sems.at[d % n_buf])

        def store(d):
            n = ends[d] - WIN_ROW0[d]
            return pltpu.make_async_copy(buf.at[d % n_buf, pl.ds(0, n), :],
                                         o_ref.at[pl.ds(WIN_ROW0[d], n), :], st_sems.at[d % n_buf])

        zeros[...] = jnp.zeros_like(zeros)
        fill = pltpu.make_async_copy(zeros, o_ref.at[pl.ds(ends[-1], tail), :], ba_sem)
        fill.start()
        fill.wait()
        load(0).start()
        for d in range(N_DEV):
            if d + 1 < N_DEV:
                if d + 1 >= n_buf:
                    store(d + 1 - n_buf).wait()
                load(d + 1).start()
            load(d).wait()
            if d > 0:
                ov = WIN_ROW0[d - 1] + WIN_W - WIN_ROW0[d]
                buf[d % n_buf, :ov, :] = buf[d % n_buf, :ov, :] + buf[(d - 1) % n_buf, WIN_W - ov:, :]
            if d == N_DEV - 1:
                ba_copy = pltpu.make_async_copy(
                    ba_ref.at[BA_DEV], buf.at[d % n_buf, pl.ds(WIN_W - N_BA, N_BA), :], ba_sem)
                ba_copy.start()
                ba_copy.wait()
            store(d).start()
        for d in range(N_DEV - n_buf, N_DEV):
            store(d).wait()

    return pl.pallas_call(
        body,
        name="assemble_w_al",
        in_specs=[ANY, ANY],
        out_specs=ANY,
        out_shape=jax.ShapeDtypeStruct((W_AL, cols), wins.dtype),
        scratch_shapes=[pltpu.VMEM((n_buf, WIN_W, cols), wins.dtype), pltpu.VMEM((tail, cols), wins.dtype),
                        pltpu.SemaphoreType.DMA((n_buf,)), pltpu.SemaphoreType.DMA((n_buf,)), pltpu.SemaphoreType.DMA],
    )(wins, bas)


def _adamw_math(w, g, m, v):
    m_new = ADAM_B1 * m + (1.0 - ADAM_B1) * g
    v_new = ADAM_B2 * v + (1.0 - ADAM_B2) * (g * g)
    m_hat = m_new / (1.0 - ADAM_B1 ** ADAM_STEP)
    v_hat = v_new / (1.0 - ADAM_B2 ** ADAM_STEP)
    return -ADAM_LR * (m_hat / (jnp.sqrt(v_hat) + ADAM_EPS) + ADAM_WD * w), m_new, v_new


def _adamw(w, g, m, v, name, tb=134):
    r, _, c = w.shape
    assert r % tb == 0

    def body(w_ref, g_ref, m_ref, v_ref, d_ref, nm_ref, nv_ref):
        d_ref[...], nm_ref[...], nv_ref[...] = _adamw_math(w_ref[...], g_ref[...], m_ref[...], v_ref[...])

    blk = pl.BlockSpec((tb, 1, c), lambda i: (i, 0, 0))
    o = jax.ShapeDtypeStruct(w.shape, F32)
    return pl.pallas_call(body, name=name, grid=(r // tb,), in_specs=[blk] * 4, out_specs=[blk] * 3,
                          out_shape=[o, o, o])(w, g, m, v)


def _adamw_many(ws, gs, ms, vs, name):
    n = len(ws)

    def body(*refs):
        for k in range(n):
            w_ref, g_ref, m_ref, v_ref = (refs[j * n + k] for j in range(4))
            d_ref, nm_ref, nv_ref = (refs[(4 + j) * n + k] for j in range(3))
            d_ref[...], nm_ref[...], nv_ref[...] = _adamw_math(w_ref[...], g_ref[...], m_ref[...], v_ref[...])

    shapes = [jax.ShapeDtypeStruct(w.shape, F32) for w in ws]
    outs = pl.pallas_call(body, name=name, out_shape=shapes * 3)(*ws, *gs, *ms, *vs)
    return outs[:n], outs[n:2 * n], outs[2 * n:]


def _select(me, table):
    return sum(jnp.where(me == d, jnp.int32(v), jnp.int32(0)) for d, v in enumerate(table))


WIN_SHIFT = tuple(SHARD_W * d - WIN_ROW0[d] for d in range(N_DEV))
PAD_L = 64
PAD_R = 64
assert max(WIN_SHIFT) <= PAD_L and WIN_W + N_BA - SHARD_W <= PAD_R


def _shard_to_window(shard_t, me):
    shift = _select(me, WIN_SHIFT)
    padded = jnp.pad(shard_t, ((PAD_L, PAD_R), (0, 0)))
    cols = shard_t.shape[1]
    lo = lax.dynamic_slice(padded, (PAD_L - shift, 0), (WIN_W, cols))
    hi = lax.dynamic_slice(padded, (PAD_L - shift + N_BA, 0), (WIN_W, cols))
    aligned = _select(me, WIN_ROW0) + lax.broadcasted_iota(jnp.int32, (WIN_W, 1), 0)
    return jnp.where(aligned >= ORIG_BA, hi, lo)


def _window_to_shard(win, ba_grad, me):
    shift = _select(me, WIN_SHIFT)
    cols = win.shape[1]
    padded = jnp.pad(win, ((N_BA, PAD_R), (0, 0)))
    lo = lax.dynamic_slice(padded, (N_BA + shift, 0), (SHARD_W, cols))
    hi = lax.dynamic_slice(padded, (shift, 0), (SHARD_W, cols))
    orig = SHARD_W * me + lax.broadcasted_iota(jnp.int32, (SHARD_W, 1), 0)
    ba_full = lax.dynamic_update_slice(jnp.zeros((SHARD_W, cols), win.dtype), ba_grad, (BA_LOCAL, 0))
    return jnp.where(orig < ORIG_BA, lo, jnp.where(orig >= ORIG_BA + N_BA, hi, ba_full))


def _pad_row(v, width=D_MODEL):
    v = v.reshape(1, -1)
    return jnp.pad(v, ((0, 0), (0, width - v.shape[1])))


def kernel(x, mem, norm_g, mem_norm_g, w_in, conv_w, a_log, dt_bias, dn_norm_g, w_mem_kv, w_br_dn, w_br_sb, w_br_mem, w_out, final_g, loss_target, m_norm_g, m_mem_norm_g, m_w_in, m_conv_w, m_a_log, m_dt_bias, m_dn_norm_g, m_w_mem_kv, m_w_br_dn, m_w_br_sb, m_w_br_mem, m_w_out, m_final_g, v_norm_g, v_mem_norm_g, v_w_in, v_conv_w, v_a_log, v_dt_bias, v_dn_norm_g, v_w_mem_kv, v_w_br_dn, v_w_br_sb, v_w_br_mem, v_w_out, v_final_g):
    xi, yi, ci = _position()
    me = 4 * xi + 2 * yi + ci

    shard_t = w_in[0].T
    win = _shard_to_window(shard_t, me).astype(BF16)
    ba = shard_t[BA_LOCAL:BA_LOCAL + N_BA, :].astype(BF16)
    g_win, g_ba = _all_gather([win, ba], "gather_weights")
    w_alt = _assemble_w_al(g_win, g_ba)

    shards = [w_mem_kv[0].astype(BF16), w_br_dn[0].astype(BF16), w_br_sb[0].astype(BF16), w_out[0].astype(BF16),
              w_br_mem[0].astype(BF16), conv_w[0]]
    r = _local_step(x[0], mem[0], loss_target[0], norm_g, mem_norm_g, w_alt, _pad_row(a_log, LANE),
                    _pad_row(dt_bias, LANE), dn_norm_g, final_g.reshape(1, D_MODEL), shards)

    dw_alt = r["w_alt"]
    parts = [r["norm_g"], r["mem_norm_g"], r["final_g"], r["dn_norm_g"], r["scal"], r["loss"], r["conv_w"],
             dw_alt[O_BA:O_BA + N_BA, :].astype(F32)]
    (g_win,), gathered = _reduce_scatter(dw_alt, [], gather=parts)
    rows_d = D_MODEL // N_DEV
    g_dn, g_sb, g_out, g_kv, g_mem = r["small"]
    g_kv = g_kv.reshape(rows_d, 2 * MEM_W)
    g_mem = g_mem.reshape(MEM_W, rows_d)
    s_norm_g, s_mem_norm_g, s_final_g, s_dn_norm_g, s_scal, s_loss, s_conv, s_ba = _sum_slots(gathered, "sum_small")
    loss = s_loss[0, 0]
    cw = conv_w.shape[2]
    g_conv = lax.dynamic_slice(s_conv, (0, cw * me), (CONV_K, cw))
    g_w_in_t = _window_to_shard(g_win, s_ba, me)
    grads = dict(norm_g=s_norm_g, mem_norm_g=s_mem_norm_g, w_in=g_w_in_t.T[None], conv_w=g_conv[None],
                 a_log=s_scal[0:1, :N_HEADS], dt_bias=s_scal[1:2, :N_HEADS], dn_norm_g=s_dn_norm_g, w_mem_kv=g_kv[None],
                 w_br_dn=g_dn[None], w_br_sb=g_sb[None], w_br_mem=g_mem[None], w_out=g_out[None],
                 final_g=s_final_g.reshape(D_MODEL))

    params = dict(norm_g=(norm_g, m_norm_g, v_norm_g), mem_norm_g=(mem_norm_g, m_mem_norm_g, v_mem_norm_g),
                  w_in=(w_in, m_w_in, v_w_in), conv_w=(conv_w, m_conv_w, v_conv_w), a_log=(a_log, m_a_log, v_a_log),
                  dt_bias=(dt_bias, m_dt_bias, v_dt_bias), dn_norm_g=(dn_norm_g, m_dn_norm_g, v_dn_norm_g),
                  w_mem_kv=(w_mem_kv, m_w_mem_kv, v_w_mem_kv), w_br_dn=(w_br_dn, m_w_br_dn, v_w_br_dn),
                  w_br_sb=(w_br_sb, m_w_br_sb, v_w_br_sb), w_br_mem=(w_br_mem, m_w_br_mem, v_w_br_mem),
                  w_out=(w_out, m_w_out, v_w_out), final_g=(final_g, m_final_g, v_final_g))
    order = list(params)
    deltas, new_m, new_v = {}, {}, {}
    deltas["w_in"], new_m["w_in"], new_v["w_in"] = (jnp.transpose(o, (1, 2, 0)) for o in _adamw(
        jnp.transpose(w_in, (2, 0, 1)), g_w_in_t[:, None, :], jnp.transpose(m_w_in, (2, 0, 1)),
        jnp.transpose(v_w_in, (2, 0, 1)), "adamw_w_in"))
    rest = [nm for nm in order if nm != "w_in"]

    def two_d(a):
        return a.reshape(1, -1) if a.ndim == 1 else a

    d_l, m_l, v_l = _adamw_many([two_d(params[nm][0]) for nm in rest], [two_d(grads[nm]) for nm in rest],
                                [two_d(params[nm][1]) for nm in rest], [two_d(params[nm][2]) for nm in rest], "adamw_rest")
    for k, nm in enumerate(rest):
        shp = params[nm][0].shape
        deltas[nm], new_m[nm], new_v[nm] = d_l[k].reshape(shp), m_l[k].reshape(shp), v_l[k].reshape(shp)
    return (loss, r["grad_x"][None], *[grads[nm] for nm in order], *[deltas[nm] for nm in order],
            *[new_m[nm] for nm in order], *[new_v[nm] for nm in order])
```

```python
import functools
import math

import jax
import jax.numpy as jnp
from jax import lax
from jax.experimental import pallas as pl
from jax.experimental.pallas import tpu as pltpu

F32 = jnp.float32
BF16 = jnp.bfloat16

D_MODEL = 1024
N_DEV = 8
N_HEADS = 8
D_HEAD = 128
DN_CHUNK = 64
CONV_K = 4
MEM_LEN = 256
MEM_HEADS = 4
MEM_DH = 64
MEM_W = MEM_HEADS * MEM_DH
NORM_EPS = 1e-6
IN_WIDTH = 11792
SHARD_W = IN_WIDTH // N_DEV

LANE = 128
SUPER = 2 * DN_CHUNK

O_QKV_DN = 0
O_Z_DN = 3072
O_QKV_SB = 4096
O_Z_SB = 7168
O_MQ = 8192
O_MZ = 8448
O_GATES = 8704
O_BA = 11776
W_AL = 11904
ORIG_BA = 4096
N_BA = 16

BA_DEV = ORIG_BA // SHARD_W
BA_LOCAL = ORIG_BA - BA_DEV * SHARD_W


def _aligned_col(o):
    return o if o < ORIG_BA else o - N_BA


ROW_TILE = 16
WIN_W = 1504
WIN_ROW0 = tuple(_aligned_col(SHARD_W * d) // ROW_TILE * ROW_TILE for d in range(N_DEV))
assert not any(ORIG_BA <= SHARD_W * d < ORIG_BA + N_BA for d in range(N_DEV))
assert all(WIN_ROW0[d] + WIN_W >= _aligned_col(SHARD_W * (d + 1) - 1) + 1 for d in range(N_DEV))
assert all(WIN_ROW0[d + 1] <= WIN_ROW0[d] + WIN_W for d in range(N_DEV - 1))
assert WIN_ROW0[-1] + WIN_W == O_BA + N_BA

ADAM_LR = 0.001
ADAM_B1 = 0.9
ADAM_B2 = 0.999
ADAM_EPS = 1e-08
ADAM_WD = 0.01
ADAM_STEP = 10

NN = (((1,), (0,)), ((), ()))
NT = (((1,), (1,)), ((), ()))
TN = (((0,), (0,)), ((), ()))


def _dot(a, b, dims):
    return lax.dot_general(a.astype(BF16), b.astype(BF16), dims, preferred_element_type=F32)


def _split2(a):
    hi = a.astype(BF16)
    lo = (a - hi.astype(F32)).astype(BF16)
    return hi, lo


def _dot3(a, b, dims):
    ah, al = _split2(a)
    bh, bl = _split2(b)
    d = functools.partial(lax.dot_general, dimension_numbers=dims, preferred_element_type=F32)
    return d(ah, bh) + (d(ah, bl) + d(al, bh))


def _sel_dot_impl(sel01, x, dims):
    sel = sel01.astype(BF16)
    h1 = x.astype(BF16)
    r1 = x - h1.astype(F32)
    h2 = r1.astype(BF16)
    h3 = (r1 - h2.astype(F32)).astype(BF16)
    d = functools.partial(lax.dot_general, dimension_numbers=dims, preferred_element_type=F32)
    return d(sel, h1) + (d(sel, h2) + d(sel, h3))


@jax.custom_vjp
def _sel_dot(sel01, x):
    return _sel_dot_impl(sel01, x, NN)


_sel_dot.defvjp(lambda s, x: (_sel_dot(s, x), s),
                lambda s, g: (jnp.zeros_like(s), _sel_dot_impl(s, g, TN)))


def _make_mm(dotfn):
    @jax.custom_vjp
    def nn(a, b):
        return dotfn(a, b, NN)

    @jax.custom_vjp
    def nt(a, b):
        return dotfn(a, b, NT)

    @jax.custom_vjp
    def tn(a, b):
        return dotfn(a, b, TN)

    nn.defvjp(lambda a, b: (nn(a, b), (a, b)), lambda r, g: (nt(g, r[1]), tn(r[0], g)))
    nt.defvjp(lambda a, b: (nt(a, b), (a, b)), lambda r, g: (nn(g, r[1]), tn(g, r[0])))
    tn.defvjp(lambda a, b: (tn(a, b), (a, b)), lambda r, g: (nt(r[1], g), nn(r[0], g)))
    return nn, nt, tn


mm_nn, mm_nt, mm_tn = _make_mm(_dot)
mm3_nn, mm3_nt, mm3_tn = _make_mm(_dot3)


def _sigmoid(x):
    return jax.nn.sigmoid(x)


def _silu(x):
    return x * _sigmoid(x)


def _softplus_parts(x):
    sp = jnp.log1p(jnp.exp(-jnp.abs(x)))
    return jnp.maximum(x, 0.0) + sp, jnp.maximum(-x, 0.0) + sp


def _rmsnorm(x, g):
    return x * lax.rsqrt(jnp.mean(x * x, axis=-1, keepdims=True) + NORM_EPS) * g


def _iota2(shape, dim):
    return lax.broadcasted_iota(jnp.int32, shape, dim)


def _div64(i):
    return lax.shift_right_logical(i, jnp.full(i.shape, 6, jnp.int32))


def _each(f, *lists):
    return [f(*a) for a in zip(*lists)]


@jax.custom_vjp
def _inv_unit_lower(ms):
    n = ms[0].shape[0]
    eye = (_iota2((n, n), 0) == _iota2((n, n), 1)).astype(F32)
    rs = [eye - m for m in ms]
    ps = ms
    for _ in range(5):
        ps = _each(mm3_nn, ps, ps)
        rs = _each(lambda r, p: r + mm_nn(r, p), rs, ps)
    return rs


def _inv_fwd(ms):
    rs = _inv_unit_lower(ms)
    return rs, rs


def _inv_bwd(rs, gs):
    ts = _each(mm_tn, rs, gs)
    return (_each(lambda t, r: -mm_nt(t, r), ts, rs),)


_inv_unit_lower.defvjp(_inv_fwd, _inv_bwd)


def _dn_block(cq, ck, cv, bcol, acol, zt, alog, dtb, gn, s0):
    n = SUPER
    h = DN_CHUNK
    row = _iota2((n, n), 0)
    col = _iota2((n, n), 1)
    same = _div64(row) == _div64(col)
    incl = jnp.logical_and(same, row >= col)
    strict = jnp.logical_and(same, row > col)
    incl_f = incl.astype(F32)

    qn = _each(lambda x: x * lax.rsqrt(jnp.sum(x * x, axis=-1, keepdims=True) + NORM_EPS) * (D_HEAD ** -0.5), cq)
    kn = _each(lambda x: x * lax.rsqrt(jnp.sum(x * x, axis=-1, keepdims=True) + NORM_EPS), ck)
    beta = _each(_sigmoid, bcol)
    g = _each(lambda al, ac, dt: -(jnp.exp(al) * _softplus_parts(ac + dt)[0]), alog, acol, dtb)
    gcum = _each(lambda x: _sel_dot(incl_f, jnp.broadcast_to(x, (n, n))), g)
    gam_incl = _each(lambda x: jnp.where(incl, jnp.exp(jnp.where(incl, x - x.T, 0.0)), 0.0), gcum)
    kk = _each(mm_nt, kn, kn)
    t_inv = _inv_unit_lower(_each(lambda b, x, gm: b * x * jnp.where(strict, gm, 0.0), beta, kk, gam_incl))
    eg = _each(jnp.exp, gcum)
    u = _each(lambda t, v, b: mm_nn(t, v * b), t_inv, cv, beta)
    w = _each(lambda t, k, b, e: mm_nn(t, k * (b * e)), t_inv, kn, beta, eg)
    a_intra = _each(lambda q, k, gm: mm_nt(q, k) * gm, qn, kn, gam_incl)
    q_dec = _each(lambda q, e: q * e, qn, eg)
    last0 = _each(lambda x: x[h - 1:h, :], gcum)
    last1 = _each(lambda x: x[n - 1:n, :], gcum)
    k_dec = _each(lambda k, x, l0, l1: k * jnp.exp(jnp.concatenate(
        [jnp.broadcast_to(l0, (h, n)), jnp.broadcast_to(l1, (h, n))], axis=0) - x), kn, gcum, last0, last1)
    v0 = _each(lambda uu, ww, s: uu[:h] - mm_nn(ww[:h], s), u, w, s0)
    o0 = _each(lambda q, s: mm_nn(q[:h], s), q_dec, s0)
    s1 = _each(lambda s, l0, k, v: s * jnp.exp(l0) + mm_tn(k[:h], v), s0, last0, k_dec, v0)
    v1 = _each(lambda uu, ww, s: uu[h:] - mm_nn(ww[h:], s), u, w, s1)
    o1 = _each(lambda q, s: mm_nn(q[h:], s), q_dec, s1)
    s2 = _each(lambda s, l1, k, v: s * jnp.exp(l1) + mm_tn(k[h:], v), s1, last1, k_dec, v1)
    o = _each(lambda a, b, am, x, y: jnp.concatenate([a, b], axis=0) + mm_nn(am, jnp.concatenate([x, y], axis=0)),
              o0, o1, a_intra, v0, v1)
    out = _each(lambda x, z: _rmsnorm(x, gn) * _silu(z), o, zt)
    return out, s2


def _mem_fn(mq, mz, mkv):
    mk = mkv[:, :MEM_W]
    mv = mkv[:, MEM_W:]
    lane = _iota2((1, MEM_W), 1)
    hm = [(_div64(lane) == hd).astype(F32) for hd in range(MEM_HEADS)]
    s = _each(lambda m: mm_nt(mq * m, mk) * (1.0 / math.sqrt(MEM_DH)), hm)
    s = _each(lambda x: x - jnp.max(x, axis=-1, keepdims=True), s)
    e = _each(jnp.exp, s)
    p = _each(lambda x: x / jnp.sum(x, axis=-1, keepdims=True), e)
    o = _each(lambda x, m: mm_nn(x, mv) * m, p, hm)
    out = o[0]
    for x in o[1:]:
        out = out + x
    return out * _silu(mz)


def _loss_fn(x, mo, fg, tgt):
    y = _rmsnorm(x + mo, fg)
    err = y - tgt
    return 0.5 * jnp.sum(jnp.mean(err * err, axis=-1, keepdims=True), axis=0, keepdims=True)


def _matmul_tn(a, b, out_dtype, tm, tn, tk, name):
    kdim, m = a.shape
    n = b.shape[1]
    tm, tn, tk = min(tm, m), min(tn, n), min(tk, kdim)
    assert m % tm == 0 and n % tn == 0 and kdim % tk == 0
    nk = kdim // tk

    def body(a_ref, b_ref, o_ref, acc_ref):
        k = pl.program_id(2)
        part = _dot(a_ref[...], b_ref[...], TN)

        @pl.when(k == 0)
        def _():
            acc_ref[...] = part

        @pl.when(k > 0)
        def _():
            acc_ref[...] += part

        @pl.when(k == nk - 1)
        def _():
            o_ref[...] = acc_ref[...].astype(o_ref.dtype)

    def body_one(a_ref, b_ref, o_ref, acc_ref):
        o_ref[...] = _dot(a_ref[...], b_ref[...], TN).astype(o_ref.dtype)

    return pl.pallas_call(
        body_one if nk == 1 else body,
        name=name,
        grid=(m // tm, n // tn, nk),
        in_specs=[pl.BlockSpec((tk, tm), lambda i, j, k: (k, i)), pl.BlockSpec((tk, tn), lambda i, j, k: (k, j))],
        out_specs=pl.BlockSpec((tm, tn), lambda i, j, k: (i, j)),
        out_shape=jax.ShapeDtypeStruct((m, n), out_dtype),
        scratch_shapes=[pltpu.VMEM((tm, tn), F32)],
        compiler_params=pltpu.CompilerParams(dimension_semantics=("parallel", "parallel", "arbitrary")),
    )(a, b)


def _norm_in(x, g, tm=512):
    t = x.shape[0]
    tm = min(tm, t)

    def body(x_ref, g_ref, h_ref):
        h_ref[...] = _rmsnorm(x_ref[...], g_ref[...]).astype(BF16)

    return pl.pallas_call(
        body,
        name="norm_in",
        grid=(t // tm,),
        in_specs=[pl.BlockSpec((tm, D_MODEL), lambda i: (i, 0)), pl.BlockSpec((1, D_MODEL), lambda i: (0, 0))],
        out_specs=pl.BlockSpec((tm, D_MODEL), lambda i: (i, 0)),
        out_shape=jax.ShapeDtypeStruct((t, D_MODEL), BF16),
    )(x, g)


def _dw_alt(parts, h, tm=512):
    t = h.shape[0]
    n_p = len(parts)
    widths = [p.shape[1] for p in parts]
    offs = [sum(widths[:s]) for s in range(n_p)]
    total = sum(widths)
    n_tiles = pl.cdiv(total, tm)

    n_buf = 3
    big, small, tiles = [], [], [[] for _ in range(n_tiles)]
    for s, (off, w) in enumerate(zip(offs, widths)):
        if w >= tm:
            assert w % tm == 0 and off % tm == 0
            for c in range(w // tm):
                tiles[off // tm + c].append(("big", len(big)))
                big.append((s, c * tm))
        else:
            assert off // tm == (off + w - 1) // tm
            tiles[off // tm].append(("small", len(small), off % tm, w))
            small.append(s)
    n_s = len(small)

    def body(*refs):
        a_refs, h_ref, o_hbm = refs[:n_p], refs[n_p], refs[n_p + 1]
        abuf, obuf = refs[n_p + 2], refs[n_p + 3]
        sbufs = refs[n_p + 4:n_p + 4 + n_s]
        in_sems, out_sems, small_sems = refs[n_p + 4 + n_s:]

        def in_copy(n):
            s, c0 = big[n]
            return pltpu.make_async_copy(a_refs[s].at[:, pl.ds(c0, tm)], abuf.at[n % n_buf], in_sems.at[n % n_buf])

        def small_copy(j):
            return pltpu.make_async_copy(a_refs[small[j]], sbufs[j], small_sems.at[j])

        def out_copy(o):
            return pltpu.make_async_copy(obuf.at[o % 2], o_hbm.at[pl.ds(o * tm, tm), :], out_sems.at[o % 2])

        for n in range(min(n_buf, len(big))):
            in_copy(n).start()
        for j in range(n_s):
            small_copy(j).start()
        for o in range(n_tiles):
            if o >= 2:
                out_copy(o - 2).wait()
            filled = 0
            for item in tiles[o]:
                if item[0] == "big":
                    n = item[1]
                    in_copy(n).wait()
                    obuf[o % 2] = _dot(abuf[n % n_buf], h_ref[...], TN).astype(BF16)
                    if n + n_buf < len(big):
                        in_copy(n + n_buf).start()
                    filled = tm
                else:
                    _, j, r0, w = item
                    small_copy(j).wait()
                    obuf[o % 2, r0:r0 + w, :] = _dot(sbufs[j][...], h_ref[...], TN).astype(BF16)
                    filled = r0 + w
            if filled < tm:
                obuf[o % 2, filled:, :] = jnp.zeros((tm - filled, D_MODEL), BF16)
            out_copy(o).start()
        for o in range(max(n_tiles - 2, 0), n_tiles):
            out_copy(o).wait()

    return pl.pallas_call(
        body,
        name="dw_alt",
        in_specs=[pl.BlockSpec(memory_space=pl.ANY)] * n_p + [pl.BlockSpec(memory_space=pltpu.VMEM)],
        out_specs=pl.BlockSpec(memory_space=pl.ANY),
        out_shape=jax.ShapeDtypeStruct((n_tiles * tm, D_MODEL), BF16),
        scratch_shapes=[pltpu.VMEM((n_buf, t, tm), BF16), pltpu.VMEM((2, tm, D_MODEL), BF16)]
        + [pltpu.VMEM((t, widths[s]), BF16) for s in small]
        + [pltpu.SemaphoreType.DMA((n_buf,)), pltpu.SemaphoreType.DMA((2,)), pltpu.SemaphoreType.DMA((n_s,))],
    )(*parts, h)


def _grad_x(parts, w_alt, x, g, dres, tm=256):
    t = x.shape[0]
    tm = min(tm, t)
    n_p = len(parts)
    assert sum(p.shape[1] for p in parts) == w_alt.shape[0] and t % tm == 0

    def body(*refs):
        a_refs = refs[:n_p]
        w_ref, x_ref, g_ref, dres_ref, dx_ref, dg_ref = refs[n_p:]
        dproj = jnp.concatenate([a_ref[...] for a_ref in a_refs], axis=1)
        dh = _dot(dproj, w_ref[...], NN)

        @pl.when(pl.program_id(0) == 0)
        def _():
            dg_ref[...] = jnp.zeros_like(dg_ref)

        _, vjp = jax.vjp(_rmsnorm, x_ref[...], g_ref[...])
        dx, dg = vjp(dh)
        dx_ref[...] = dx + dres_ref[...]
        dg_ref[...] += dg

    row = pl.BlockSpec((tm, D_MODEL), lambda i: (i, 0))
    vec = pl.BlockSpec((1, D_MODEL), lambda i: (0, 0))
    return pl.pallas_call(
        body,
        name="grad_x",
        grid=(t // tm,),
        in_specs=[pl.BlockSpec((tm, p.shape[1]), lambda i: (i, 0)) for p in parts]
        + [pl.BlockSpec(w_alt.shape, lambda i: (0, 0), pipeline_mode=pl.Buffered(1)), row, vec, row],
        out_specs=[row, vec],
        out_shape=[jax.ShapeDtypeStruct((t, D_MODEL), F32), jax.ShapeDtypeStruct((1, D_MODEL), F32)],
    )(*parts, w_alt, x, g, dres)


def _block_tail(proj, o_dn, o_sb, o_m, x, tgt, w_br_dn, w_br_sb, w_br_mem, w_out, fg, tm=256):
    t = x.shape[0]
    tm = min(tm, t)
    gw = 512
    n_g = 3 * D_MODEL // gw

    def body(*refs):
        g_refs = refs[:n_g]
        (odn_ref, osb_ref, om_ref, x_ref, t_ref, wdn_ref, wsb_ref, wm_ref, wo_ref, fg_ref, loss_ref, dout_ref, dfg_ref,
         mg_ref, dyd_ref, dys_ref, dym_ref, dg_ref, dod_ref, dos_ref, dom_ref) = refs[n_g:]
        y = [_dot(odn_ref[...], wdn_ref[...], NN), _dot(osb_ref[...], wsb_ref[...], NN),
             _dot(om_ref[...], wm_ref[...], NN)]
        s = [_sigmoid(jnp.concatenate([g_refs[2 * k][...], g_refs[2 * k + 1][...]], axis=1)) for k in range(3)]
        merged16 = (s[0] * y[0] + s[1] * y[1] + s[2] * y[2]).astype(BF16)
        mg_ref[...] = merged16
        mo = _dot(merged16, wo_ref[...], NN)
        loss, vjp = jax.vjp(_loss_fn, x_ref[...], mo, fg_ref[...], t_ref[...])
        _, dout, dfg, _ = vjp(jnp.ones((1, 1), F32))

        @pl.when(pl.program_id(0) == 0)
        def _():
            loss_ref[...] = jnp.zeros_like(loss_ref)
            dfg_ref[...] = jnp.zeros_like(dfg_ref)

        loss_ref[...] += jnp.broadcast_to(loss, loss_ref.shape)
        dfg_ref[...] += dfg
        dout_ref[...] = dout
        dmerged = _dot(dout, wo_ref[...], NT)
        dy = [(sk * dmerged).astype(BF16) for sk in s]
        dyd_ref[...], dys_ref[...], dym_ref[...] = dy
        dg_ref[...] = jnp.concatenate([dmerged * yk * (sk * (1.0 - sk)) for yk, sk in zip(y, s)], axis=1).astype(BF16)
        dod_ref[...] = _dot(dy[0], wdn_ref[...], NT).astype(BF16)
        dos_ref[...] = _dot(dy[1], wsb_ref[...], NT).astype(BF16)
        dom_ref[...] = _dot(dy[2], wm_ref[...], NT).astype(BF16)

    gates = [pl.BlockSpec((tm, gw), lambda i, j=j: (i, O_GATES // gw + j)) for j in range(n_g)]
    row = pl.BlockSpec((tm, D_MODEL), lambda i: (i, 0))
    rowm = pl.BlockSpec((tm, MEM_W), lambda i: (i, 0))
    vec = pl.BlockSpec((1, D_MODEL), lambda i: (0, 0))

    def whole(a):
        return pl.BlockSpec(a.shape, lambda i: (0, 0), pipeline_mode=pl.Buffered(1))

    def bf(c):
        return jax.ShapeDtypeStruct((t, c), BF16)

    return pl.pallas_call(
        body,
        name="block_tail",
        grid=(t // tm,),
        in_specs=gates + [row, row, rowm, row, row, whole(w_br_dn), whole(w_br_sb), whole(w_br_mem), whole(w_out), vec],
        out_specs=[pl.BlockSpec((1, LANE), lambda i: (0, 0)), row, vec, row, row, row, row,
                   pl.BlockSpec((tm, 3 * D_MODEL), lambda i: (i, 0)), row, row, rowm],
        out_shape=[jax.ShapeDtypeStruct((1, LANE), F32), jax.ShapeDtypeStruct((t, D_MODEL), F32),
                   jax.ShapeDtypeStruct((1, D_MODEL), F32), bf(D_MODEL), bf(D_MODEL), bf(D_MODEL), bf(D_MODEL),
                   bf(3 * D_MODEL), bf(D_MODEL), bf(D_MODEL), bf(MEM_W)],
    )(*([proj] * n_g), o_dn, o_sb, o_m, x, tgt, w_br_dn, w_br_sb, w_br_mem, w_out, fg)


def _shift_rows(x, s):
    t = x.shape[0]
    if s == 0:
        return x
    rolled = pltpu.roll(x, s % t, 0)
    row = _iota2(x.shape, 0)
    keep = row >= s if s > 0 else row < t + s
    return jnp.where(keep, rolled, 0.0)


def _conv_pre(x, w):
    return sum(_shift_rows(x, CONV_K - 1 - j) * w[j:j + 1, :] for j in range(CONV_K))


CONV_TC = 256


def _dn_conv(proj, conv_w):
    t = proj.shape[0]
    nb = 3 * D_MODEL // CONV_TC

    def body(x_ref, w_ref, c_ref):
        c_ref[...] = _silu(_conv_pre(x_ref[...], w_ref[...]))

    return pl.pallas_call(
        body,
        name="dn_conv",
        grid=(nb,),
        in_specs=[pl.BlockSpec((t, CONV_TC), lambda j: (0, j)), pl.BlockSpec((CONV_K, CONV_TC), lambda j: (0, j))],
        out_specs=pl.BlockSpec((t, CONV_TC), lambda j: (0, j)),
        out_shape=jax.ShapeDtypeStruct((t, 3 * D_MODEL), F32),
    )(proj, conv_w)


def _dn_conv_bwd(proj, conv_w, dc):
    t = proj.shape[0]
    nb = 3 * D_MODEL // CONV_TC

    def body(x_ref, w_ref, dc_ref, dx_ref, dw_ref):
        x = x_ref[...]
        w = w_ref[...]
        pre = _conv_pre(x, w)
        sg = _sigmoid(pre)
        dpre = dc_ref[...] * (sg * (1.0 + pre * (1.0 - sg)))
        ahead = [_shift_rows(dpre, -(CONV_K - 1 - j)) for j in range(CONV_K)]
        dx_ref[...] = sum(a * w[j:j + 1, :] for j, a in enumerate(ahead)).astype(BF16)
        dw_ref[...] = jnp.concatenate([jnp.sum(a * x, axis=0, keepdims=True) for a in ahead], axis=0)

    blk = pl.BlockSpec((t, CONV_TC), lambda j: (0, j))
    wblk = pl.BlockSpec((CONV_K, CONV_TC), lambda j: (0, j))
    return pl.pallas_call(
        body,
        name="dn_conv_bwd",
        grid=(nb,),
        in_specs=[blk, wblk, blk],
        out_specs=[blk, wblk],
        out_shape=[jax.ShapeDtypeStruct((t, 3 * D_MODEL), BF16), jax.ShapeDtypeStruct((CONV_K, 3 * D_MODEL), F32)],
    )(proj, conv_w, dc)


def _ba_columns(ba, hd):
    lane = _iota2(ba.shape, 1)
    bcol = jnp.sum(jnp.where(lane == hd, ba, 0.0), axis=1, keepdims=True)
    acol = jnp.sum(jnp.where(lane == N_HEADS + hd, ba, 0.0), axis=1, keepdims=True)
    return bcol, acol


def _head_scalar(row, hd):
    lane = _iota2(row.shape, 1)
    return jnp.sum(jnp.where(lane == hd, row, 0.0), axis=1, keepdims=True)


DN_HP = 8


def _dn_inputs(cq, ck, cv, ba_ref, z_ref, alog_ref, dtb_ref, heads, lanes):
    ba = ba_ref[...]
    cols = [_ba_columns(ba, hd) for hd in heads]
    return ([cq[:, ln] for ln in lanes], [ck[:, ln] for ln in lanes], [cv[:, ln] for ln in lanes],
            [c[0] for c in cols], [c[1] for c in cols], [z_ref[:, ln] for ln in lanes],
            [_head_scalar(alog_ref[...], hd) for hd in heads], [_head_scalar(dtb_ref[...], hd) for hd in heads])


def _dn_specs(nblk, reverse):
    w = DN_HP * LANE
    nq = D_MODEL // w

    def row(i):
        return nblk - 1 - i if reverse else i

    def colblk(b0):
        return pl.BlockSpec((SUPER, w), lambda i, h: (row(i), b0 + h))

    ba = pl.BlockSpec((SUPER, LANE), lambda i, h: (row(i), O_BA // LANE))
    vec = pl.BlockSpec((1, LANE), lambda i, h: (0, 0))
    st = pl.BlockSpec((1, DN_HP, D_HEAD, D_HEAD), lambda i, h: (row(i), h, 0, 0))
    return colblk, nq, ba, vec, st


def _dn_fwd(c, proj, alog_row, dtb_row, gn):
    t = c.shape[0]
    nblk = t // SUPER
    colblk, nq, ba, vec, st = _dn_specs(nblk, False)

    def body(cq, ck, cv, ba_ref, z_ref, alog_ref, dtb_ref, gn_ref, o_ref, s_ref, state):
        @pl.when(jnp.logical_and(pl.program_id(0) == 0, pl.program_id(1) == 0))
        def _():
            state[...] = jnp.zeros_like(state)

        heads = [pl.program_id(1) * DN_HP + j for j in range(DN_HP)]
        lanes = [slice(j * LANE, (j + 1) * LANE) for j in range(DN_HP)]
        s0 = [state[hd] for hd in heads]
        outs, s2 = _dn_block(*_dn_inputs(cq, ck, cv, ba_ref, z_ref, alog_ref, dtb_ref, heads, lanes), gn_ref[...], s0)
        for j, (hd, ln) in enumerate(zip(heads, lanes)):
            s_ref[0, j] = s0[j]
            o_ref[:, ln] = outs[j].astype(BF16)
            state[hd] = s2[j]

    return pl.pallas_call(
        body,
        name="dn_fwd",
        grid=(nblk, N_HEADS // DN_HP),
        in_specs=[colblk(0), colblk(nq), colblk(2 * nq), ba, colblk(O_Z_DN // (DN_HP * LANE)), vec, vec, vec],
        out_specs=[colblk(0), st],
        out_shape=[jax.ShapeDtypeStruct((t, D_MODEL), BF16),
                   jax.ShapeDtypeStruct((nblk, N_HEADS, D_HEAD, D_HEAD), F32)],
        scratch_shapes=[pltpu.VMEM((N_HEADS, D_HEAD, D_HEAD), F32)],
    )(c, c, c, proj, proj, alog_row, dtb_row, gn)


def _dn_bwd(c, proj, alog_row, dtb_row, gn, states, do):
    t = c.shape[0]
    nblk = t // SUPER
    colblk, nq, ba, vec, st = _dn_specs(nblk, True)
    assert nq == 1

    def body(cq, ck, cv, ba_ref, z_ref, alog_ref, dtb_ref, gn_ref, s_ref, do_ref,
             dc_ref, dz_ref, dba_ref, dsc_ref, dgn_ref, dstate):
        i = pl.program_id(0)
        hq = pl.program_id(1)

        @pl.when(jnp.logical_and(i == 0, hq == 0))
        def _():
            dstate[...] = jnp.zeros_like(dstate)
            dsc_ref[...] = jnp.zeros_like(dsc_ref)
            dgn_ref[...] = jnp.zeros_like(dgn_ref)

        @pl.when(hq == 0)
        def _():
            dba_ref[...] = jnp.zeros_like(dba_ref)

        lane = _iota2((SUPER, LANE), 1)
        lane1 = _iota2((1, LANE), 1)
        heads = [hq * DN_HP + j for j in range(DN_HP)]
        lanes = [slice(j * LANE, (j + 1) * LANE) for j in range(DN_HP)]
        ds_in = [dstate[hd] for hd in heads]
        s_in = [s_ref[0, j] for j in range(DN_HP)]
        _, vjp = jax.vjp(_dn_block, *_dn_inputs(cq, ck, cv, ba_ref, z_ref, alog_ref, dtb_ref, heads, lanes),
                         gn_ref[...], s_in)
        dq, dk, dv, dbc, dac, dz, dal, ddt, dgn, ds0 = vjp(([do_ref[:, ln].astype(F32) for ln in lanes], ds_in))
        dba = jnp.zeros((SUPER, LANE), F32)
        dal_row = jnp.zeros((1, LANE), F32)
        ddt_row = jnp.zeros((1, LANE), F32)
        for j, (hd, ln) in enumerate(zip(heads, lanes)):
            for part, d in enumerate((dq, dk, dv)):
                dc_ref[:, part * D_MODEL + j * LANE:part * D_MODEL + (j + 1) * LANE] = d[j]
            dz_ref[:, ln] = dz[j].astype(BF16)
            dstate[hd] = ds0[j]
            dba = dba + jnp.where(lane == hd, dbc[j], 0.0) + jnp.where(lane == N_HEADS + hd, dac[j], 0.0)
            dal_row = dal_row + jnp.where(lane1 == hd, dal[j], 0.0)
            ddt_row = ddt_row + jnp.where(lane1 == hd, ddt[j], 0.0)
        dba_ref[...] += dba
        dsc_ref[0:1, :] += dal_row
        dsc_ref[1:2, :] += ddt_row
        dgn_ref[...] += dgn

    outs = pl.pallas_call(
        body,
        name="dn_bwd",
        grid=(nblk, N_HEADS // DN_HP),
        in_specs=[colblk(0), colblk(nq), colblk(2 * nq), ba, colblk(O_Z_DN // (DN_HP * LANE)), vec, vec, vec, st,
                  colblk(0)],
        out_specs=[pl.BlockSpec((SUPER, 3 * D_MODEL), lambda i, h: (nblk - 1 - i, 0)), colblk(0),
                   pl.BlockSpec((SUPER, LANE), lambda i, h: (nblk - 1 - i, 0)),
                   pl.BlockSpec((2, LANE), lambda i, h: (0, 0)), vec],
        out_shape=[jax.ShapeDtypeStruct((t, 3 * D_MODEL), F32), jax.ShapeDtypeStruct((t, D_MODEL), BF16),
                   jax.ShapeDtypeStruct((t, LANE), F32), jax.ShapeDtypeStruct((2, LANE), F32),
                   jax.ShapeDtypeStruct((1, LANE), F32)],
        scratch_shapes=[pltpu.VMEM((N_HEADS, D_HEAD, D_HEAD), F32)],
    )(c, c, c, proj, proj, alog_row, dtb_row, gn, states, do)
    return outs


SB_TQ = 256
SB_TK = 256
SB_HP_FWD = 8
SB_HP_BWD = 4


def _sb_logits(z, mask):
    sp = jnp.log(1.0 + jnp.exp(-jnp.abs(z)))
    lf_raw = -(jnp.maximum(z, 0.0) + sp)
    lb = lf_raw + z
    lf = lf_raw if mask is None else jnp.where(mask, lf_raw, 0.0)
    return lb, lf_raw, lf


def _suffix_sums(x, sel):
    hi, lo = _split2(x)
    d = functools.partial(lax.dot_general, dimension_numbers=NN, preferred_element_type=F32)
    return d(hi, sel) + d(lo, sel)


def _sb_diag_mask(tq, r):
    return r * SB_TK + _iota2((tq, SB_TK), 1) < _iota2((tq, SB_TK), 0)


def _sb_specs(t, tq, hp):
    w = hp * LANE
    q0, k0, v0, z0 = (O_QKV_SB // w, (O_QKV_SB + D_MODEL) // w, (O_QKV_SB + 2 * D_MODEL) // w, O_Z_SB // w)

    def blk(b0):
        return pl.BlockSpec((tq, w), lambda h, i: (i, b0 + h))

    def full(b0, **kw):
        return pl.BlockSpec((t, w), lambda h, i: (0, b0 + h), **kw)

    once = dict(pipeline_mode=pl.Buffered(1))
    return blk(q0), full(k0, **once), full(v0, **once), blk(z0), blk(0), full(0)


def _sb_fwd(proj, shards):
    t = proj.shape[0]
    tq = min(SB_TQ, t)
    ndiag = tq // SB_TK
    scale = 1.0 / math.sqrt(D_HEAD)
    na = len(shards)

    def body(q_ref, k_ref, v_ref, z_ref, *rest):
        x_refs, (o_ref, oraw_ref), land = rest[:na], rest[na:na + 2], rest[na + 2:2 * na + 2]
        sems = rest[2 * na + 2:]
        qi = pl.program_id(1)
        first = jnp.logical_and(pl.program_id(0) == 0, qi == 0)
        last = jnp.logical_and(pl.program_id(0) == pl.num_programs(0) - 1, qi == pl.num_programs(1) - 1)

        @pl.when(first)
        def _():
            for cp in _direct_gather_copies(x_refs, land, *sems):
                cp.start()

        lanes = [slice(hd * LANE, (hd + 1) * LANE) for hd in range(SB_HP_FWD)]
        qs = [(q_ref[:, ln] * scale).astype(BF16) for ln in lanes]
        after = (_iota2((SB_TK, SB_TK), 0) > _iota2((SB_TK, SB_TK), 1)).astype(BF16)
        oraw_ref[...] = jnp.zeros_like(oraw_ref)

        def block(kb, mask, c_lf):
            rows = pl.ds(pl.multiple_of(kb * SB_TK, SB_TK), SB_TK)
            z = _each(lambda q, ln: _dot(q, k_ref[rows, ln], NT), qs, lanes)
            lg = _each(lambda x: _sb_logits(x, mask), z)
            surv = _each(lambda x: _suffix_sums(x[2], after), lg)
            att = _each(lambda x, s, c: jnp.exp(x[0] + s + c), lg, surv, c_lf)
            if mask is not None:
                att = _each(lambda a: jnp.where(mask, a, 0.0), att)
            pv = _each(lambda a, ln: _dot(a, v_ref[rows, ln], NN), att, lanes)
            for p, ln in zip(pv, lanes):
                oraw_ref[:, ln] += p
            return tuple(_each(lambda c, x: c + jnp.sum(x[2], axis=1, keepdims=True), c_lf, lg))

        carry = tuple(jnp.zeros((tq, 1), F32) for _ in range(SB_HP_FWD))
        for r in reversed(range(ndiag)):
            carry = block(qi * ndiag + r, _sb_diag_mask(tq, r), carry)
        lax.fori_loop(0, qi * ndiag, lambda i, c: block(qi * ndiag - 1 - i, None, c), carry)
        o_ref[...] = (oraw_ref[...] * _silu(z_ref[...])).astype(BF16)

        @pl.when(last)
        def _():
            for cp in _direct_gather_copies(x_refs, land, *sems):
                cp.wait()

    q_spec, k_spec, v_spec, z_spec, out, _ = _sb_specs(t, tq, SB_HP_FWD)
    outs = pl.pallas_call(
        body,
        name="sb_fwd",
        grid=(N_HEADS // SB_HP_FWD, t // tq),
        in_specs=[q_spec, k_spec, v_spec, z_spec] + [ANY] * na,
        out_specs=[out, out] + [ANY] * na,
        out_shape=[jax.ShapeDtypeStruct((t, D_MODEL), BF16), jax.ShapeDtypeStruct((t, D_MODEL), F32)]
        + [jax.ShapeDtypeStruct((N_DEV, *v.shape), v.dtype) for v in shards],
        scratch_shapes=_gather_sems(na),
    )(proj, proj, proj, proj, *shards)
    return outs[0], outs[1], outs[2:]


def _sb_bwd(proj, oraw, do, blocks):
    t = proj.shape[0]
    tq = min(SB_TQ, t)
    ndiag = tq // SB_TK
    scale = 1.0 / math.sqrt(D_HEAD)
    nb = len(blocks)

    def body(q_ref, k_ref, v_ref, z_ref, oraw_ref, do_ref, *rest):
        blk_refs, (dq_ref, dk_ref, dv_ref, dz_ref), land_refs = rest[:nb], rest[nb:nb + 4], rest[nb + 4:2 * nb + 4]
        dk_acc, dv_acc, p_scr, z_scr, send_sems, recv_sems, local_sems = rest[2 * nb + 4:]
        qi = pl.program_id(1)
        nq = pl.num_programs(1)
        hg = pl.program_id(0)
        me = _position()
        mine = 4 * me[0] + 2 * me[1] + me[2]

        def exchange():
            cps = []
            for a, (blk_ref, land_ref) in enumerate(zip(blk_refs, land_refs)):
                cps.append(pltpu.make_async_copy(blk_ref.at[mine], land_ref.at[mine], local_sems.at[a]))
                for k, peer in enumerate(_other_devices(me)):
                    cps.append(pltpu.make_async_remote_copy(
                        src_ref=blk_ref.at[4 * peer[0] + 2 * peer[1] + peer[2]], dst_ref=land_ref.at[mine],
                        send_sem=send_sems.at[7 * a + k], recv_sem=recv_sems.at[7 * a + k], device_id=peer,
                        device_id_type=MESH))
            return cps

        @pl.when(jnp.logical_and(hg == 0, qi == 0))
        def _():
            for cp in exchange():
                cp.start()

        @pl.when(qi == 0)
        def _():
            dk_acc[...] = jnp.zeros_like(dk_acc)
            dv_acc[...] = jnp.zeros_like(dv_acc)

        heads = range(SB_HP_BWD)
        lanes = [slice(hd * LANE, (hd + 1) * LANE) for hd in heads]
        zg = z_ref[...]
        sg = _sigmoid(zg)
        dog = do_ref[...].astype(F32)
        dz_ref[...] = (dog * oraw_ref[...] * (sg * (1.0 + zg * (1.0 - sg)))).astype(BF16)
        d_o = (dog * (zg * sg)).astype(BF16)
        d_o16 = [d_o[:, ln] for ln in lanes]
        qs = [(q_ref[:, ln] * scale).astype(BF16) for ln in lanes]
        ri = _iota2((SB_TK, SB_TK), 0)
        ci = _iota2((SB_TK, SB_TK), 1)
        after = (ri > ci).astype(BF16)
        earlier = (ri < ci).astype(BF16)

        def rows_of(kb):
            return pl.ds(pl.multiple_of(kb * SB_TK, SB_TK), SB_TK)

        def down(kb, mask, c_lf):
            rows = rows_of(kb)
            z = _each(lambda q, ln: _dot(q, k_ref[rows, ln], NT), qs, lanes)
            da = _each(lambda d, ln: _dot(d, v_ref[rows, ln], NT), d_o16, lanes)
            lg = _each(lambda x: _sb_logits(x, mask), z)
            surv = _each(lambda x: _suffix_sums(x[2], after), lg)
            att = _each(lambda x, s, c: jnp.exp(x[0] + s + c), lg, surv, c_lf)
            if mask is not None:
                att = _each(lambda a: jnp.where(mask, a, 0.0), att)
            dv = _each(lambda a, d: _dot(a, d, TN), att, d_o16)
            for hd in heads:
                p_scr[hd, kb] = att[hd] * da[hd]
                z_scr[hd, kb] = z[hd]
                dv_acc[rows, lanes[hd]] += dv[hd]
            return tuple(_each(lambda c, x: c + jnp.sum(x[2], axis=1, keepdims=True), c_lf, lg))

        c_lf = tuple(jnp.zeros((tq, 1), F32) for _ in heads)
        for r in reversed(range(ndiag)):
            c_lf = down(qi * ndiag + r, _sb_diag_mask(tq, r), c_lf)
        lax.fori_loop(0, qi * ndiag, lambda i, c: down(qi * ndiag - 1 - i, None, c), c_lf)

        def up(kb, mask, carry):
            dq, c_p = carry
            rows = rows_of(kb)
            p = [p_scr[hd, kb] for hd in heads]
            zs = [z_scr[hd, kb] for hd in heads]
            before = _each(lambda x, c: _suffix_sums(x, earlier) + c, p, c_p)
            e = _each(lambda x: jnp.exp(-jnp.abs(x)), zs)
            r = _each(lambda x: 1.0 / (1.0 + x), e)
            sig = _each(lambda x, a, b: jnp.where(x >= 0.0, b, a * b), zs, e, r)
            oms = _each(lambda x, a, b: jnp.where(x >= 0.0, a * b, b), zs, e, r)
            if mask is not None:
                sig = _each(lambda a: jnp.where(mask, a, 0.0), sig)
            dzz = _each(lambda x, o, g, b: x * o - g * b, p, oms, sig, before)
            dk = _each(lambda x, q: _dot(x, q, TN), dzz, qs)
            dq = _each(lambda a, x, ln: a + _dot(x, k_ref[rows, ln], NN), dq, dzz, lanes)
            for hd in heads:
                dk_acc[rows, lanes[hd]] += dk[hd]
            return tuple(dq), tuple(_each(lambda c, x: c + jnp.sum(x, axis=1, keepdims=True), c_p, p))

        carry = (tuple(jnp.zeros((tq, D_HEAD), F32) for _ in heads), tuple(jnp.zeros((tq, 1), F32) for _ in heads))
        carry = lax.fori_loop(0, qi * ndiag, lambda kb, c: up(kb, None, c), carry)
        for r in range(ndiag):
            carry = up(qi * ndiag + r, _sb_diag_mask(tq, r), carry)
        dq = carry[0]
        for hd in heads:
            dq_ref[:, lanes[hd]] = (dq[hd] * scale).astype(BF16)

        @pl.when(qi == nq - 1)
        def _():
            dk_ref[...] = dk_acc[...].astype(BF16)
            dv_ref[...] = dv_acc[...].astype(BF16)

        @pl.when(jnp.logical_and(hg == pl.num_programs(0) - 1, qi == nq - 1))
        def _():
            for cp in exchange():
                cp.wait()

    q_spec, k_spec, v_spec, z_spec, blk, full = _sb_specs(t, tq, SB_HP_BWD)
    o = jax.ShapeDtypeStruct((t, D_MODEL), BF16)
    w = SB_HP_BWD * LANE
    outs = pl.pallas_call(
        body,
        name="sb_bwd",
        grid=(N_HEADS // SB_HP_BWD, t // tq),
        in_specs=[q_spec, k_spec, v_spec, z_spec, blk, blk] + [ANY] * nb,
        out_specs=[blk, full, full, blk] + [ANY] * nb,
        out_shape=[o, o, o, o] + [jax.ShapeDtypeStruct(b.shape, b.dtype) for b in blocks],
        scratch_shapes=[pltpu.VMEM((t, w), F32), pltpu.VMEM((t, w), F32)]
        + [pltpu.VMEM((SB_HP_BWD, t // SB_TK, tq, SB_TK), F32)] * 2 + _gather_sems(nb),
    )(proj, proj, proj, proj, oraw, do, *blocks)
    return outs[0], outs[1], outs[2], outs[3], outs[4:]


def _mem_kv_fn(mem, mg, w):
    return mm_nn(_rmsnorm(mem, mg), w)


def _mem_kv(mem, mg, w):
    def body(m_ref, g_ref, w_ref, o_ref):
        o_ref[...] = _mem_kv_fn(m_ref[...], g_ref[...], w_ref[...])

    return pl.pallas_call(body, name="mem_kv", out_shape=jax.ShapeDtypeStruct((MEM_LEN, 2 * MEM_W), F32))(mem, mg, w)


def _mem_kv_bwd(mem, mg, w, dmkv):
    def body(m_ref, g_ref, w_ref, d_ref, dg_ref, dw_ref):
        _, vjp = jax.vjp(_mem_kv_fn, m_ref[...], g_ref[...], w_ref[...].astype(F32))
        _, dg, dw = vjp(d_ref[...])
        dg_ref[...] = dg
        dw_ref[...] = dw.astype(BF16)

    return pl.pallas_call(
        body, name="mem_kv_bwd",
        out_shape=[jax.ShapeDtypeStruct((1, D_MODEL), F32), jax.ShapeDtypeStruct((D_MODEL, 2 * MEM_W), BF16)],
    )(mem, mg, w, dmkv)


def _mem_attn(proj, mkv, tm=1024):
    t = proj.shape[0]
    tm = min(tm, t)

    def body(q_ref, z_ref, kv_ref, o_ref):
        o_ref[...] = _mem_fn(q_ref[...], z_ref[...], kv_ref[...]).astype(BF16)

    return pl.pallas_call(
        body,
        name="mem_attn",
        grid=(t // tm,),
        in_specs=[pl.BlockSpec((tm, MEM_W), lambda i: (i, O_MQ // MEM_W)),
                  pl.BlockSpec((tm, MEM_W), lambda i: (i, O_MZ // MEM_W)),
                  pl.BlockSpec((MEM_LEN, 2 * MEM_W), lambda i: (0, 0))],
        out_specs=pl.BlockSpec((tm, MEM_W), lambda i: (i, 0)),
        out_shape=jax.ShapeDtypeStruct((t, MEM_W), BF16),
    )(proj, proj, mkv)


def _mem_attn_bwd(proj, mkv, do, tm=1024):
    t = proj.shape[0]
    tm = min(tm, t)

    def body(q_ref, z_ref, kv_ref, do_ref, dq_ref, dz_ref, dkv_ref):
        _, vjp = jax.vjp(_mem_fn, q_ref[...], z_ref[...], kv_ref[...])
        dq, dz, dkv = vjp(do_ref[...].astype(F32))
        dq_ref[...] = dq.astype(BF16)
        dz_ref[...] = dz.astype(BF16)

        @pl.when(pl.program_id(0) == 0)
        def _():
            dkv_ref[...] = jnp.zeros_like(dkv_ref)

        dkv_ref[...] += dkv

    blk = pl.BlockSpec((tm, MEM_W), lambda i: (i, 0))
    kv = pl.BlockSpec((MEM_LEN, 2 * MEM_W), lambda i: (0, 0))
    return pl.pallas_call(
        body,
        name="mem_attn_bwd",
        grid=(t // tm,),
        in_specs=[pl.BlockSpec((tm, MEM_W), lambda i: (i, O_MQ // MEM_W)),
                  pl.BlockSpec((tm, MEM_W), lambda i: (i, O_MZ // MEM_W)), kv, blk],
        out_specs=[blk, blk, kv],
        out_shape=[jax.ShapeDtypeStruct((t, MEM_W), BF16), jax.ShapeDtypeStruct((t, MEM_W), BF16),
                   jax.ShapeDtypeStruct((MEM_LEN, 2 * MEM_W), F32)],
    )(proj, proj, mkv, do)


def _proj_gather(h, w_alt, shards, tm=512, tn=3968):
    t = h.shape[0]
    tm = min(tm, t)
    n, kdim = w_alt.shape
    assert n % tn == 0 and t % tm == 0
    nj, ni = n // tn, t // tm
    na = len(shards)

    def body(h_ref, w_ref, *rest):
        x_refs, o_ref, land = rest[:na], rest[na], rest[na + 1:2 * na + 1]
        send_sems, recv_sems, local_sems = rest[2 * na + 1:]
        j, i = pl.program_id(0), pl.program_id(1)

        def copies():
            return _direct_gather_copies(x_refs, land, send_sems, recv_sems, local_sems)

        @pl.when(jnp.logical_and(j == 0, i == 0))
        def _():
            for cp in copies():
                cp.start()

        o_ref[...] = _dot(h_ref[...], w_ref[...], NT)

        @pl.when(jnp.logical_and(j == nj - 1, i == ni - 1))
        def _():
            for cp in copies():
                cp.wait()

    outs = pl.pallas_call(
        body,
        name="proj",
        grid=(nj, ni),
        in_specs=[pl.BlockSpec((tm, kdim), lambda j, i: (i, 0)), pl.BlockSpec((tn, kdim), lambda j, i: (j, 0))]
        + [ANY] * na,
        out_specs=[pl.BlockSpec((tm, tn), lambda j, i: (i, j))] + [ANY] * na,
        out_shape=[jax.ShapeDtypeStruct((t, n), F32)]
        + [jax.ShapeDtypeStruct((N_DEV, *v.shape), v.dtype) for v in shards],
        scratch_shapes=_gather_sems(na),
    )(h, w_alt, *shards)
    return outs[0], outs[1:]


def _local_step(x, mem, tgt, norm_g, mem_norm_g, w_alt, alog_row, dtb_row, dn_norm_g, final_g, shards):
    h = _norm_in(x, norm_g)
    s_kv, s_dn, s_sb, s_out, s_mem, s_conv = shards
    proj, (g_kv, g_conv) = _proj_gather(h, w_alt, [s_kv, s_conv])
    w_mem_kv = g_kv.reshape(D_MODEL, 2 * MEM_W)
    conv_w = g_conv.transpose(1, 0, 2).reshape(CONV_K, 3 * D_MODEL)

    c = _dn_conv(proj, conv_w)
    o_dn, states = _dn_fwd(c, proj, alog_row, dtb_row, dn_norm_g)
    o_sb, o_sb_raw, (g_dn, g_sb, g_out, g_mem) = _sb_fwd(proj, [s_dn, s_sb, s_out, s_mem])
    w_br_dn = g_dn.reshape(D_MODEL, D_MODEL)
    w_br_sb = g_sb.reshape(D_MODEL, D_MODEL)
    w_out = g_out.reshape(D_MODEL, D_MODEL)
    w_br_mem = g_mem.transpose(1, 0, 2).reshape(MEM_W, D_MODEL)
    mkv = _mem_kv(mem, mem_norm_g, w_mem_kv)
    o_m = _mem_attn(proj, mkv)

    (loss, dout, d_final_g, merged, dy_dn, dy_sb, dy_m, dgates, do_dn, do_sb, do_m) = _block_tail(
        proj, o_dn, o_sb, o_m, x, tgt, w_br_dn, w_br_sb, w_br_mem, w_out, final_g)
    dw_out = _matmul_tn(merged, dout, BF16, 256, 1024, 2048, "dw_out")
    dw_br_dn = _matmul_tn(o_dn, dy_dn, BF16, 256, 1024, 2048, "dw_br_dn")
    dw_br_sb = _matmul_tn(o_sb, dy_sb, BF16, 256, 1024, 2048, "dw_br_sb")
    dw_br_mem = _matmul_tn(o_m, dy_m, BF16, 256, 1024, 2048, "dw_br_mem")

    dmq, dmz, dmkv = _mem_attn_bwd(proj, mkv, do_m)
    d_mem_norm_g, dw_mem_kv = _mem_kv_bwd(mem, mem_norm_g, w_mem_kv, dmkv)
    rows_d = D_MODEL // N_DEV
    small_blocks = [
        dw_br_dn.reshape(N_DEV, rows_d, D_MODEL), dw_br_sb.reshape(N_DEV, rows_d, D_MODEL),
        dw_out.reshape(N_DEV, rows_d, D_MODEL), dw_mem_kv.reshape(N_DEV, rows_d // 2, D_MODEL),
        dw_br_mem.reshape(MEM_W, N_DEV, rows_d).transpose(1, 0, 2).reshape(N_DEV, MEM_W // N_DEV, D_MODEL)]
    dq_sb, dk_sb, dv_sb, dz_sb, small_parts = _sb_bwd(proj, o_sb_raw, do_sb, small_blocks)
    d_small = _sum_slots(list(small_parts), "sum_small_grads")
    dc, dz_dn, dba, dscal, d_dn_norm_g = _dn_bwd(c, proj, alog_row, dtb_row, dn_norm_g, states, do_dn)
    dqkv_dn, d_conv_w = _dn_conv_bwd(proj, conv_w, dc)

    dproj = [dqkv_dn, dz_dn, dq_sb, dk_sb, dv_sb, dz_sb, dmq, dmz, dgates, dba.astype(BF16)]
    dw_alt = _dw_alt(dproj, h)
    grad_x, d_norm_g = _grad_x(dproj, w_alt, x, norm_g, dout)
    return dict(loss=loss, grad_x=grad_x, norm_g=d_norm_g, mem_norm_g=d_mem_norm_g, w_alt=dw_alt, conv_w=d_conv_w,
                scal=dscal, dn_norm_g=d_dn_norm_g, small=d_small, final_g=d_final_g)


MESH = pl.DeviceIdType.MESH
ANY = pl.BlockSpec(memory_space=pl.ANY)


def _position():
    return lax.axis_index("x"), lax.axis_index("y"), lax.axis_index("c")


def _other_devices(me):
    return [tuple(1 - p if (f >> s) & 1 else p for p, s in zip(me, (2, 1, 0))) for f in range(1, N_DEV)]


def _direct_gather_copies(x_refs, land_refs, send_sems, recv_sems, local_sems):
    me = _position()
    mine = 4 * me[0] + 2 * me[1] + me[2]
    cps = []
    for a, (x_ref, land) in enumerate(zip(x_refs, land_refs)):
        cps.append(pltpu.make_async_copy(x_ref, land.at[mine], local_sems.at[a]))
        for k, peer in enumerate(_other_devices(me)):
            cps.append(pltpu.make_async_remote_copy(
                src_ref=x_ref, dst_ref=land.at[mine], send_sem=send_sems.at[7 * a + k],
                recv_sem=recv_sems.at[7 * a + k], device_id=peer, device_id_type=MESH))
    return cps


def _gather_sems(n):
    return [pltpu.SemaphoreType.DMA((7 * n,)), pltpu.SemaphoreType.DMA((7 * n,)), pltpu.SemaphoreType.DMA((n,))]


def _all_gather(xs, name):
    n = len(xs)

    def body(*refs):
        x_refs, o_refs = refs[:n], refs[n:2 * n]
        send_sems, recv_sems, local_sems = refs[2 * n:]
        x, y, c = _position()
        me, sibling = (x, y, c), (x, y, 1 - c)
        x_nbr, y_nbr, diag = (1 - x, y, c), (x, 1 - y, c), (1 - x, 1 - y, c)
        south = c == 0
        relay_from = tuple(jnp.where(south, a, b) for a, b in zip(y_nbr, x_nbr))
        relay_to = tuple(jnp.where(south, a, b) for a, b in zip(x_nbr, y_nbr))

        def slot(p):
            return 4 * p[0] + 2 * p[1] + p[2]

        def copy(a, k, block, to, src=None):
            dst = o_refs[a].at[slot(block)]
            return pltpu.make_async_remote_copy(
                src_ref=dst if src is None else src, dst_ref=dst, send_sem=send_sems.at[7 * a + k],
                recv_sem=recv_sems.at[7 * a + k], device_id=to, device_id_type=MESH)

        mine = [pltpu.make_async_copy(x_refs[a], o_refs[a].at[slot(me)], local_sems.at[a]) for a in range(n)]
        for cp in mine:
            cp.start()
        sends = []
        for a in range(n):
            sends += [copy(a, 0, me, sibling, src=x_refs[a]), copy(a, 1, me, x_nbr, src=x_refs[a]),
                      copy(a, 2, me, y_nbr, src=x_refs[a])]
        for cp in sends:
            cp.start()
        later = []
        for a in range(n):
            copy(a, 1, x_nbr, me).wait_recv()
            copy(a, 2, y_nbr, me).wait_recv()
            later += [copy(a, 3, relay_from, relay_to), copy(a, 4, x_nbr, sibling), copy(a, 5, y_nbr, sibling)]
            for cp in later[-3:]:
                cp.start()
        for a in range(n):
            copy(a, 3, diag, me).wait_recv()
            later.append(copy(a, 6, diag, sibling))
            later[-1].start()
        for a in range(n):
            copy(a, 0, sibling, me).wait_recv()
            for k, chip in ((4, x_nbr), (5, y_nbr), (6, diag)):
                copy(a, k, (chip[0], chip[1], 1 - c), me).wait_recv()
        for cp in sends + later:
            cp.wait_send()
        for cp in mine:
            cp.wait()

    return pl.pallas_call(
        body,
        name=name,
        in_specs=[ANY] * n,
        out_specs=[ANY] * n,
        out_shape=[jax.ShapeDtypeStruct((N_DEV, *v.shape), v.dtype) for v in xs],
        scratch_shapes=[pltpu.SemaphoreType.DMA((7 * n,)), pltpu.SemaphoreType.DMA((7 * n,)),
                        pltpu.SemaphoreType.DMA((n,))],
    )(*xs)


def _window_view(ref, dest):
    return ref.at[pl.ds(WIN_ROW0[dest], WIN_W), :]


def _chunk_rows(rows, cols):
    return max(ch for ch in range(ROW_TILE, rows + 1, ROW_TILE) if rows % ch == 0 and ch * cols <= (1 << 20))


def _halving_stage(xs, axis, name, out_dtype, windowed=(), gather=()):
    n_arr = len(xs)
    metas = []
    for k, v in enumerate(xs):
        if k in windowed:
            metas.append((N_DEV // 2, WIN_W, v.shape[1]))
        else:
            assert v.shape[1] == 2
            metas.append((v.shape[0], v.shape[2], v.shape[3]))
    chunk = [_chunk_rows(r, c) for (_, r, c) in metas]
    offs = [sum(m[0] for m in metas[:k]) for k in range(n_arr)]
    n_sem = sum(m[0] for m in metas)

    n_g = len(gather)

    def body(*refs):
        x_refs, g_refs = refs[:n_arr], refs[n_arr:n_arr + n_g]
        outs = refs[n_arr + n_g:]
        o_refs, land_refs, gl_refs = outs[:n_arr], outs[n_arr:2 * n_arr], outs[2 * n_arr:2 * n_arr + n_g]
        rest = outs[2 * n_arr + n_g:]
        bufs = rest[:3 * n_arr]
        send_sems, recv_sems, in_sems, out_sems = rest[3 * n_arr:3 * n_arr + 4]
        gathers = _direct_gather_copies(g_refs, gl_refs, *rest[3 * n_arr + 4:]) if n_g else []
        for cp in gathers:
            cp.start()
        pos = dict(zip("xyc", _position()))
        bit = pos[axis]
        peer = tuple(1 - pos[a] if a == axis else pos[a] for a in "xyc")

        def view(k, i, b):
            if k in windowed:
                return _window_view(x_refs[k], 2 * i + b)
            return x_refs[k].at[i, b]

        def add_blocks(k, a_view, b_view, o_view):
            _hbm_add(a_view, b_view, o_view, bufs[3 * k:3 * k + 3], in_sems, out_sems, chunk[k])

        for b in (0, 1):
            @pl.when(bit == b)
            def _(b=b):
                sends = []
                for k in range(n_arr):
                    for i in range(metas[k][0]):
                        cp = pltpu.make_async_remote_copy(
                            src_ref=view(k, i, 1 - b), dst_ref=land_refs[k].at[i], send_sem=send_sems.at[offs[k] + i],
                            recv_sem=recv_sems.at[offs[k] + i], device_id=peer, device_id_type=MESH)
                        cp.start()
                        sends.append(cp)
                idx = 0
                for k in range(n_arr):
                    for i in range(metas[k][0]):
                        sends[idx].wait_recv()
                        add_blocks(k, view(k, i, b), land_refs[k].at[i], o_refs[k].at[i])
                        idx += 1
                for cp in sends:
                    cp.wait_send()

        for cp in gathers:
            cp.wait()

    out_shape = [jax.ShapeDtypeStruct(m, out_dtype) for m in metas]
    land_shape = [jax.ShapeDtypeStruct(m, v.dtype) for m, v in zip(metas, xs)]
    g_shape = [jax.ShapeDtypeStruct((N_DEV, *v.shape), v.dtype) for v in gather]
    scratch = []
    for k in range(n_arr):
        blk = (2, chunk[k], metas[k][2])
        scratch += [pltpu.VMEM(blk, xs[k].dtype)] * 2 + [pltpu.VMEM(blk, out_dtype)]
    scratch += [pltpu.SemaphoreType.DMA((n_sem,)), pltpu.SemaphoreType.DMA((n_sem,)),
                pltpu.SemaphoreType.DMA((2, 2)), pltpu.SemaphoreType.DMA((2,))]
    if n_g:
        scratch += _gather_sems(n_g)
    outs = pl.pallas_call(
        body,
        name=name,
        in_specs=[ANY] * (n_arr + n_g),
        out_specs=[ANY] * (2 * n_arr + n_g),
        out_shape=out_shape + land_shape + g_shape,
        scratch_shapes=scratch,
    )(*xs, *gather)
    return outs[:n_arr], outs[2 * n_arr:]


def _hbm_add(a_view, b_view, o_view, bufs, in_sems, out_sems, ch):
    rows = a_view.shape[0]
    nch = rows // ch
    va, vb, vo = bufs

    def rows_of(j):
        return pl.ds(pl.multiple_of(j * ch, 16), ch)

    def loads(j, s):
        return (pltpu.make_async_copy(a_view.at[rows_of(j), :], va.at[s], in_sems.at[0, s]),
                pltpu.make_async_copy(b_view.at[rows_of(j), :], vb.at[s], in_sems.at[1, s]))

    def store(j, s):
        return pltpu.make_async_copy(vo.at[s], o_view.at[rows_of(j), :], out_sems.at[s])

    for cp in loads(0, 0):
        cp.start()

    def step(j, _):
        s = lax.rem(j, 2)

        @pl.when(j + 1 < nch)
        def _():
            for cp in loads(j + 1, 1 - s):
                cp.start()

        for cp in loads(j, s):
            cp.wait()

        @pl.when(j >= 2)
        def _():
            store(j - 2, s).wait()

        vo[s] = (va[s].astype(F32) + vb[s].astype(F32)).astype(vo.dtype)
        store(j, s).start()
        return 0

    lax.fori_loop(0, nch, step, 0)
    for j in range(max(0, nch - 2), nch):
        store(j, j % 2).wait()


def _xy_stage(xs, first, name):
    n_arr = len(xs)
    if first:
        shapes = [(v.shape[2] // 2, v.shape[3]) for v in xs]
        ins = list(xs)
    else:
        shapes = [(a.shape[1], a.shape[2]) for a, _ in xs]
        ins = [v for pair in xs for v in pair]
    n_blk = 2 if first else 1
    out_dtype = BF16 if first else F32
    chunk = [_chunk_rows(r, c) for (r, c) in shapes]
    n_sem = 2 * n_blk * n_arr

    def body(*refs):
        n_in = len(ins)
        in_refs = refs[:n_in]
        n_out = 2 * n_arr if first else n_arr
        o_refs = refs[n_in:n_in + n_out]
        land = refs[n_in + n_out:n_in + n_out + 2 * n_arr]
        rest = refs[n_in + n_out + 2 * n_arr:]
        bufs = rest[:3 * n_arr]
        send_sems, recv_sems, in_sems, out_sems = rest[3 * n_arr:]
        x, y, c = _position()
        peers = {"x": (1 - x, y, c), "y": (x, 1 - y, c)}
        jobs = []
        for k in range(n_arr):
            r, _ = shapes[k]
            half_a, half_b = pl.ds(0, r), pl.ds(r, r)
            if first:
                src = in_refs[k]
                for i in range(2):
                    jobs.append((k, src.at[i, 1 - y, half_a, :], src.at[i, y, half_a, :], land[2 * k].at[i],
                                 o_refs[2 * k].at[i], "y"))
                    jobs.append((k, src.at[1 - x, i, half_b, :], src.at[x, i, half_b, :], land[2 * k + 1].at[i],
                                 o_refs[2 * k + 1].at[i], "x"))
            else:
                a1, b1 = in_refs[2 * k], in_refs[2 * k + 1]
                jobs.append((k, a1.at[1 - x], a1.at[x], land[2 * k], o_refs[k].at[half_a, :], "x"))
                jobs.append((k, b1.at[1 - y], b1.at[y], land[2 * k + 1], o_refs[k].at[half_b, :], "y"))
        sends = []
        for n, (k, send, _, landing, _, axis) in enumerate(jobs):
            cp = pltpu.make_async_remote_copy(src_ref=send, dst_ref=landing, send_sem=send_sems.at[n],
                                              recv_sem=recv_sems.at[n], device_id=peers[axis], device_id_type=MESH)
            cp.start()
            sends.append(cp)
        for cp, (k, _, kept, landing, out, _) in zip(sends, jobs):
            cp.wait_recv()
            _hbm_add(kept, landing, out, bufs[3 * k:3 * k + 3], in_sems, out_sems, chunk[k])
        for cp in sends:
            cp.wait_send()

    if first:
        out_shape = [jax.ShapeDtypeStruct((2, r, c), BF16) for (r, c) in shapes for _ in range(2)]
        land_shape = out_shape
    else:
        out_shape = [jax.ShapeDtypeStruct((2 * r, c), F32) for (r, c) in shapes]
        land_shape = [jax.ShapeDtypeStruct((r, c), BF16) for (r, c) in shapes for _ in range(2)]
    scratch = []
    for k in range(n_arr):
        scratch += [pltpu.VMEM((2, chunk[k], shapes[k][1]), BF16)] * 2 + [pltpu.VMEM((2, chunk[k], shapes[k][1]), out_dtype)]
    scratch += [pltpu.SemaphoreType.DMA((n_sem,)), pltpu.SemaphoreType.DMA((n_sem,)),
                pltpu.SemaphoreType.DMA((2, 2)), pltpu.SemaphoreType.DMA((2,))]
    outs = pl.pallas_call(
        body,
        name=name,
        in_specs=[ANY] * len(ins),
        out_specs=[ANY] * (len(out_shape) + len(land_shape)),
        out_shape=out_shape + land_shape,
        scratch_shapes=scratch,
    )(*ins)
    outs = outs[:len(out_shape)]
    return [(outs[2 * k], outs[2 * k + 1]) for k in range(n_arr)] if first else list(outs)


def _reduce_scatter(dw_al, blocks, gather=()):
    xs = [dw_al] + [b.reshape(N_DEV // 2, 2, *b.shape[1:]) for b in blocks]
    ys, gathered = _halving_stage(xs, "c", "rs_c", BF16, windowed=(0,), gather=gather)
    pairs = _xy_stage([v.reshape(2, 2, *v.shape[1:]) for v in ys], True, "rs_xy1")
    return _xy_stage(pairs, False, "rs_xy2"), gathered


def _sum_slots(gs, name):
    n = len(gs)

    def body(*refs):
        for g_ref, o_ref in zip(refs[:n], refs[n:]):
            acc = g_ref[0].astype(F32)
            for d in range(1, N_DEV):
                acc = acc + g_ref[d].astype(F32)
            o_ref[...] = acc

    return pl.pallas_call(body, name=name, out_shape=[jax.ShapeDtypeStruct(g.shape[1:], F32) for g in gs])(*gs)


def _assemble_w_al(wins, bas):
    cols = wins.shape[2]
    n_buf = 3
    ends = [WIN_ROW0[d + 1] if d + 1 < N_DEV else WIN_ROW0[d] + WIN_W for d in range(N_DEV)]
    tail = W_AL - ends[-1]

    def body(w_ref, ba_ref, o_ref, buf, zeros, ld_sems, st_sems, ba_sem):
        def load(d):
            return pltpu.make_async_copy(w_ref.at[d], buf.at[d % n_buf], ld_sems.at[d % n_buf])

        def store(d):
            n = ends[d] - WIN_ROW0[d]
            return pltpu.make_async_copy(buf.at[d % n_buf, pl.ds(0, n), :],
                                         o_ref.at[pl.ds(WIN_ROW0[d], n), :], st_sems.at[d % n_buf])

        zeros[...] = jnp.zeros_like(zeros)
        fill = pltpu.make_async_copy(zeros, o_ref.at[pl.ds(ends[-1], tail), :], ba_sem)
        fill.start()
        fill.wait()
        load(0).start()
        for d in range(N_DEV):
            if d + 1 < N_DEV:
                if d + 1 >= n_buf:
                    store(d + 1 - n_buf).wait()
                load(d + 1).start()
            load(d).wait()
            if d > 0:
                ov = WIN_ROW0[d - 1] + WIN_W - WIN_ROW0[d]
                buf[d % n_buf, :ov, :] = buf[d % n_buf, :ov, :] + buf[(d - 1) % n_buf, WIN_W - ov:, :]
            if d == N_DEV - 1:
                ba_copy = pltpu.make_async_copy(
                    ba_ref.at[BA_DEV], buf.at[d % n_buf, pl.ds(WIN_W - N_BA, N_BA), :], ba_sem)
                ba_copy.start()
                ba_copy.wait()
            store(d).start()
        for d in range(N_DEV - n_buf, N_DEV):
            store(d).wait()

    return pl.pallas_call(
        body,
        name="assemble_w_al",
        in_specs=[ANY, ANY],
        out_specs=ANY,
        out_shape=jax.ShapeDtypeStruct((W_AL, cols), wins.dtype),
        scratch_shapes=[pltpu.VMEM((n_buf, WIN_W, cols), wins.dtype), pltpu.VMEM((tail, cols), wins.dtype),
                        pltpu.SemaphoreType.DMA((n_buf,)), pltpu.SemaphoreType.DMA((n_buf,)), pltpu.SemaphoreType.DMA],
    )(wins, bas)


def _adamw_math(w, g, m, v):
    m_new = ADAM_B1 * m + (1.0 - ADAM_B1) * g
    v_new = ADAM_B2 * v + (1.0 - ADAM_B2) * (g * g)
    m_hat = m_new / (1.0 - ADAM_B1 ** ADAM_STEP)
    v_hat = v_new / (1.0 - ADAM_B2 ** ADAM_STEP)
    return -ADAM_LR * (m_hat / (jnp.sqrt(v_hat) + ADAM_EPS) + ADAM_WD * w), m_new, v_new


def _adamw(w, g, m, v, name, tb=134):
    r, _, c = w.shape
    assert r % tb == 0

    def body(w_ref, g_ref, m_ref, v_ref, d_ref, nm_ref, nv_ref):
        d_ref[...], nm_ref[...], nv_ref[...] = _adamw_math(w_ref[...], g_ref[...], m_ref[...], v_ref[...])

    blk = pl.BlockSpec((tb, 1, c), lambda i: (i, 0, 0))
    o = jax.ShapeDtypeStruct(w.shape, F32)
    return pl.pallas_call(body, name=name, grid=(r // tb,), in_specs=[blk] * 4, out_specs=[blk] * 3,
                          out_shape=[o, o, o])(w, g, m, v)


def _adamw_many(ws, gs, ms, vs, name):
    n = len(ws)

    def body(*refs):
        for k in range(n):
            w_ref, g_ref, m_ref, v_ref = (refs[j * n + k] for j in range(4))
            d_ref, nm_ref, nv_ref = (refs[(4 + j) * n + k] for j in range(3))
            d_ref[...], nm_ref[...], nv_ref[...] = _adamw_math(w_ref[...], g_ref[...], m_ref[...], v_ref[...])

    shapes = [jax.ShapeDtypeStruct(w.shape, F32) for w in ws]
    outs = pl.pallas_call(body, name=name, out_shape=shapes * 3)(*ws, *gs, *ms, *vs)
    return outs[:n], outs[n:2 * n], outs[2 * n:]


def _select(me, table):
    return sum(jnp.where(me == d, jnp.int32(v), jnp.int32(0)) for d, v in enumerate(table))


WIN_SHIFT = tuple(SHARD_W * d - WIN_ROW0[d] for d in range(N_DEV))
PAD_L = 64
PAD_R = 64
assert max(WIN_SHIFT) <= PAD_L and WIN_W + N_BA - SHARD_W <= PAD_R


def _shard_to_window(shard_t, me):
    shift = _select(me, WIN_SHIFT)
    padded = jnp.pad(shard_t, ((PAD_L, PAD_R), (0, 0)))
    cols = shard_t.shape[1]
    lo = lax.dynamic_slice(padded, (PAD_L - shift, 0), (WIN_W, cols))
    hi = lax.dynamic_slice(padded, (PAD_L - shift + N_BA, 0), (WIN_W, cols))
    aligned = _select(me, WIN_ROW0) + lax.broadcasted_iota(jnp.int32, (WIN_W, 1), 0)
    return jnp.where(aligned >= ORIG_BA, hi, lo)


def _window_to_shard(win, ba_grad, me):
    shift = _select(me, WIN_SHIFT)
    cols = win.shape[1]
    padded = jnp.pad(win, ((N_BA, PAD_R), (0, 0)))
    lo = lax.dynamic_slice(padded, (N_BA + shift, 0), (SHARD_W, cols))
    hi = lax.dynamic_slice(padded, (shift, 0), (SHARD_W, cols))
    orig = SHARD_W * me + lax.broadcasted_iota(jnp.int32, (SHARD_W, 1), 0)
    ba_full = lax.dynamic_update_slice(jnp.zeros((SHARD_W, cols), win.dtype), ba_grad, (BA_LOCAL, 0))
    return jnp.where(orig < ORIG_BA, lo, jnp.where(orig >= ORIG_BA + N_BA, hi, ba_full))


def _pad_row(v, width=D_MODEL):
    v = v.reshape(1, -1)
    return jnp.pad(v, ((0, 0), (0, width - v.shape[1])))


def kernel(x, mem, norm_g, mem_norm_g, w_in, conv_w, a_log, dt_bias, dn_norm_g, w_mem_kv, w_br_dn, w_br_sb, w_br_mem, w_out, final_g, loss_target, m_norm_g, m_mem_norm_g, m_w_in, m_conv_w, m_a_log, m_dt_bias, m_dn_norm_g, m_w_mem_kv, m_w_br_dn, m_w_br_sb, m_w_br_mem, m_w_out, m_final_g, v_norm_g, v_mem_norm_g, v_w_in, v_conv_w, v_a_log, v_dt_bias, v_dn_norm_g, v_w_mem_kv, v_w_br_dn, v_w_br_sb, v_w_br_mem, v_w_out, v_final_g):
    xi, yi, ci = _position()
    me = 4 * xi + 2 * yi + ci

    shard_t = w_in[0].T
    win = _shard_to_window(shard_t, me).astype(BF16)
    ba = shard_t[BA_LOCAL:BA_LOCAL + N_BA, :].astype(BF16)
    g_win, g_ba = _all_gather([win, ba], "gather_weights")
    w_alt = _assemble_w_al(g_win, g_ba)

    shards = [w_mem_kv[0].astype(BF16), w_br_dn[0].astype(BF16), w_br_sb[0].astype(BF16), w_out[0].astype(BF16),
              w_br_mem[0].astype(BF16), conv_w[0]]
    r = _local_step(x[0], mem[0], loss_target[0], norm_g, mem_norm_g, w_alt, _pad_row(a_log, LANE),
                    _pad_row(dt_bias, LANE), dn_norm_g, final_g.reshape(1, D_MODEL), shards)

    dw_alt = r["w_alt"]
    parts = [r["norm_g"], r["mem_norm_g"], r["final_g"], r["dn_norm_g"], r["scal"], r["loss"], r["conv_w"],
             dw_alt[O_BA:O_BA + N_BA, :].astype(F32)]
    (g_win,), gathered = _reduce_scatter(dw_alt, [], gather=parts)
    rows_d = D_MODEL // N_DEV
    g_dn, g_sb, g_out, g_kv, g_mem = r["small"]
    g_kv = g_kv.reshape(rows_d, 2 * MEM_W)
    g_mem = g_mem.reshape(MEM_W, rows_d)
    s_norm_g, s_mem_norm_g, s_final_g, s_dn_norm_g, s_scal, s_loss, s_conv, s_ba = _sum_slots(gathered, "sum_small")
    loss = s_loss[0, 0]
    cw = conv_w.shape[2]
    g_conv = lax.dynamic_slice(s_conv, (0, cw * me), (CONV_K, cw))
    g_w_in_t = _window_to_shard(g_win, s_ba, me)
    grads = dict(norm_g=s_norm_g, mem_norm_g=s_mem_norm_g, w_in=g_w_in_t.T[None], conv_w=g_conv[None],
                 a_log=s_scal[0:1, :N_HEADS], dt_bias=s_scal[1:2, :N_HEADS], dn_norm_g=s_dn_norm_g, w_mem_kv=g_kv[None],
                 w_br_dn=g_dn[None], w_br_sb=g_sb[None], w_br_mem=g_mem[None], w_out=g_out[None],
                 final_g=s_final_g.reshape(D_MODEL))

    params = dict(norm_g=(norm_g, m_norm_g, v_norm_g), mem_norm_g=(mem_norm_g, m_mem_norm_g, v_mem_norm_g),
                  w_in=(w_in, m_w_in, v_w_in), conv_w=(conv_w, m_conv_w, v_conv_w), a_log=(a_log, m_a_log, v_a_log),
                  dt_bias=(dt_bias, m_dt_bias, v_dt_bias), dn_norm_g=(dn_norm_g, m_dn_norm_g, v_dn_norm_g),
                  w_mem_kv=(w_mem_kv, m_w_mem_kv, v_w_mem_kv), w_br_dn=(w_br_dn, m_w_br_dn, v_w_br_dn),
                  w_br_sb=(w_br_sb, m_w_br_sb, v_w_br_sb), w_br_mem=(w_br_mem, m_w_br_mem, v_w_br_mem),
                  w_out=(w_out, m_w_out, v_w_out), final_g=(final_g, m_final_g, v_final_g))
    order = list(params)
    deltas, new_m, new_v = {}, {}, {}
    deltas["w_in"], new_m["w_in"], new_v["w_in"] = (jnp.transpose(o, (1, 2, 0)) for o in _adamw(
        jnp.transpose(w_in, (2, 0, 1)), g_w_in_t[:, None, :], jnp.transpose(m_w_in, (2, 0, 1)),
        jnp.transpose(v_w_in, (2, 0, 1)), "adamw_w_in"))
    rest = [nm for nm in order if nm != "w_in"]

    def two_d(a):
        return a.reshape(1, -1) if a.ndim == 1 else a

    d_l, m_l, v_l = _adamw_many([two_d(params[nm][0]) for nm in rest], [two_d(grads[nm]) for nm in rest],
                                [two_d(params[nm][1]) for nm in rest], [two_d(params[nm][2]) for nm in rest], "adamw_rest")
    for k, nm in enumerate(rest):
        shp = params[nm][0].shape
        deltas[nm], new_m[nm], new_v[nm] = d_l[k].reshape(shp), m_l[k].reshape(shp), v_l[k].reshape(shp)
    return (loss, r["grad_x"][None], *[grads[nm] for nm in order], *[deltas[nm] for nm in order],
            *[new_m[nm] for nm in order], *[new_v[nm] for nm in order])
```
